```python
import math
import jax, jax.numpy as jnp
from jax import lax
import numpy as np

D_MODEL = 2048
BATCH = 8
SEQ = 4096
DEPTH = 1

HEAD_DIM = 64
DILATION_PATTERNS = ((128, 1), (512, 4), (2048, 16))
N_ATT_GROUPS = len(DILATION_PATTERNS)
HEADS_PER_GROUP = 6
N_Q_HEADS = N_ATT_GROUPS * HEADS_PER_GROUP
KV_HEADS = HEADS_PER_GROUP
ATT_Q_WIDTH = N_Q_HEADS * HEAD_DIM
KV_WIDTH = KV_HEADS * HEAD_DIM
ROT_DIM = HEAD_DIM // 4
ROPE_THETA = 500000.0
BLK = 128
SSM_WIDTH = D_MODEL - ATT_Q_WIDTH
SSM_GROUP_CH = 16
SSM_GROUPS = SSM_WIDTH // SSM_GROUP_CH
SSM_STATE = 64
IN_WIDTH = ATT_Q_WIDTH + 2 * KV_WIDTH + SSM_WIDTH
OUT_IN_WIDTH = KV_WIDTH + SSM_WIDTH
D_FF = 4 * D_MODEL
N_MOD = 6
EPS = 1e-6

kernel_name = "hymba_s5_longnet_sandwich_adaln_block"


def rms_norm(x, g):
    xf = x.astype(jnp.float32)
    y = xf * lax.rsqrt(jnp.mean(xf * xf, axis=-1, keepdims=True) + EPS)
    return (y * g.astype(jnp.float32)).astype(x.dtype)


def rope_partial(t, positions):
    freqs = ROPE_THETA ** (-jnp.arange(0, ROT_DIM, 2, dtype=jnp.float32) / ROT_DIM)
    ang = positions.astype(jnp.float32)[..., None] * freqs
    cos = jnp.cos(ang)[:, :, None, :]
    sin = jnp.sin(ang)[:, :, None, :]
    tr = t[..., :ROT_DIM].astype(jnp.float32)
    x1, x2 = tr[..., :ROT_DIM // 2], tr[..., ROT_DIM // 2:]
    rot = jnp.concatenate([x1 * cos - x2 * sin, x2 * cos + x1 * sin], axis=-1).astype(t.dtype)
    return jnp.concatenate([rot, t[..., ROT_DIM:]], axis=-1)


def dilated_window_attention(q, k, v, dilation, span):
    assert span <= BLK
    B, L, H, Dh = q.shape
    n = L // dilation

    def to_sub(t):
        return t.reshape(B, n, dilation, H, Dh).transpose(0, 2, 3, 1, 4)

    qs, ks, vs = to_sub(q), to_sub(k), to_sub(v)
    n_pad = -(-n // BLK) * BLK
    pad = ((0, 0), (0, 0), (0, 0), (0, n_pad - n), (0, 0))
    qs, ks, vs = (jnp.pad(t, pad) for t in (qs, ks, vs))
    nb = n_pad // BLK
    qb = qs.reshape(B, dilation, H, nb, BLK, Dh)
    kb = ks.reshape(B, dilation, H, nb, BLK, Dh)
    vb = vs.reshape(B, dilation, H, nb, BLK, Dh)

    def with_prev(t):
        prev = jnp.concatenate([jnp.zeros_like(t[:, :, :, :1]), t[:, :, :, :-1]], axis=3)
        return jnp.concatenate([prev, t], axis=4)

    kc, vc = with_prev(kb), with_prev(vb)
    scores = jnp.einsum('brhnqe,brhnke->brhnqk', qb, kc).astype(jnp.float32) / math.sqrt(Dh)
    blk = jnp.arange(nb)[:, None, None]
    qi = jnp.arange(BLK)[None, :, None]
    kj = jnp.arange(2 * BLK)[None, None, :]
    dist = qi + BLK - kj
    valid = (dist >= 0) & (dist <= span) & (blk * BLK - BLK + kj >= 0)
    scores = jnp.where(valid, scores, -jnp.inf)
    lse = jax.nn.logsumexp(scores, axis=-1)
    p = jnp.exp(scores - lse[..., None]).astype(v.dtype)
    out = jnp.einsum('brhnqk,brhnke->brhnqe', p, vc)
    out = out.reshape(B, dilation, H, n_pad, Dh)[:, :, :, :n]
    out = out.transpose(0, 3, 1, 2, 4).reshape(B, L, H, Dh)
    lse = lse.reshape(B, dilation, H, n_pad)[:, :, :, :n].transpose(0, 3, 1, 2).reshape(B, L, H)
    return out, lse


def _scan_op(e1, e2):
    a1, b1 = e1
    a2, b2 = e2
    return a1 * a2, a2 * b1 + b2


def s5_mixer(u, a_re, a_im, log_dt, b_re, b_im, c_re, c_im, d_skip, w_glu, b_glu):
    B, L, G, P = u.shape
    u32 = u.astype(jnp.float32)
    lam = lax.complex(a_re.astype(jnp.float32), a_im.astype(jnp.float32))
    dt = jnp.exp(log_dt.astype(jnp.float32))[:, None]
    a_bar = jnp.exp(lam * dt)
    b_mat = lax.complex(b_re.astype(jnp.float32), b_im.astype(jnp.float32))
    b_bar = ((a_bar - 1.0) / lam)[..., None] * b_mat
    c_mat = lax.complex(c_re.astype(jnp.float32), c_im.astype(jnp.float32))
    bu = jnp.einsum('blgp,gnp->blgn', u32.astype(jnp.complex64), b_bar)
    a_all = jnp.broadcast_to(a_bar, bu.shape)
    _, state = lax.associative_scan(_scan_op, (a_all, bu), axis=1)
    y = jnp.einsum('blgn,gpn->blgp', state, c_mat).real + d_skip.astype(jnp.float32) * u32
    y = y.reshape(B, L, G * P)
    y = jax.nn.gelu(y)
    y = y * jax.nn.sigmoid(y @ w_glu.astype(jnp.float32) + b_glu.astype(jnp.float32))
    return y.astype(u.dtype)


def _fwd_setup_inputs(seed: int = 0) -> dict:
    key = jax.random.key(seed)
    ks = jax.random.split(key, 32)
    f32 = jnp.float32
    nrm = lambda k, shape, s: jax.random.normal(k, shape, f32) * s
    x = jax.random.normal(ks[0], (BATCH, SEQ, D_MODEL), f32)
    c = jax.random.normal(ks[1], (BATCH, D_MODEL), f32)
    offset = jax.random.randint(ks[2], (BATCH, 1), 0, 1024, dtype=jnp.int32)
    positions = (offset + jnp.arange(SEQ, dtype=jnp.int32)[None, :]).astype(jnp.int32)
    n_idx = jnp.arange(SSM_STATE, dtype=f32)
    return {
        "x": x,
        "c": c,
        "positions": positions,
        "w_ada": nrm(ks[3], (DEPTH, D_MODEL, N_MOD * D_MODEL), 0.5 * D_MODEL ** -0.5),
        "b_ada": nrm(ks[4], (DEPTH, N_MOD * D_MODEL), 0.01),
        "g_pre_mix": 1.0 + nrm(ks[5], (DEPTH, D_MODEL), 0.02),
        "g_post_mix": 1.0 + nrm(ks[6], (DEPTH, D_MODEL), 0.02),
        "w_in": nrm(ks[7], (DEPTH, D_MODEL, IN_WIDTH), D_MODEL ** -0.5),
        "ssm_a_re": -0.5 + nrm(ks[8], (DEPTH, SSM_GROUPS, SSM_STATE), 0.01),
        "ssm_a_im": math.pi * n_idx + nrm(ks[9], (DEPTH, SSM_GROUPS, SSM_STATE), 0.01),
        "ssm_log_dt": jax.random.uniform(ks[10], (DEPTH, SSM_GROUPS), f32, math.log(1e-3), math.log(1e-1)),
        "ssm_b_re": nrm(ks[11], (DEPTH, SSM_GROUPS, SSM_STATE, SSM_GROUP_CH), (2 * SSM_GROUP_CH) ** -0.5),
        "ssm_b_im": nrm(ks[12], (DEPTH, SSM_GROUPS, SSM_STATE, SSM_GROUP_CH), (2 * SSM_GROUP_CH) ** -0.5),
        "ssm_c_re": nrm(ks[13], (DEPTH, SSM_GROUPS, SSM_GROUP_CH, SSM_STATE), (2 * SSM_STATE) ** -0.5),
        "ssm_c_im": nrm(ks[14], (DEPTH, SSM_GROUPS, SSM_GROUP_CH, SSM_STATE), (2 * SSM_STATE) ** -0.5),
        "ssm_d": nrm(ks[15], (DEPTH, SSM_GROUPS, SSM_GROUP_CH), 1.0),
        "w_glu": nrm(ks[16], (DEPTH, SSM_WIDTH, SSM_WIDTH), SSM_WIDTH ** -0.5),
        "b_glu": nrm(ks[17], (DEPTH, SSM_WIDTH), 0.01),
        "g_attn_out": 1.0 + nrm(ks[18], (DEPTH, KV_WIDTH), 0.02),
        "g_ssm_out": 1.0 + nrm(ks[19], (DEPTH, SSM_WIDTH), 0.02),
        "w_out": nrm(ks[20], (DEPTH, OUT_IN_WIDTH, D_MODEL), OUT_IN_WIDTH ** -0.5),
        "g_pre_mlp": 1.0 + nrm(ks[21], (DEPTH, D_MODEL), 0.02),
        "g_post_mlp": 1.0 + nrm(ks[22], (DEPTH, D_MODEL), 0.02),
        "w_mlp_in": nrm(ks[23], (DEPTH, D_MODEL, D_FF), D_MODEL ** -0.5),
        "w_mlp_out": nrm(ks[24], (DEPTH, D_FF, D_MODEL), D_FF ** -0.5),
    }


def _fwd_reference(x, c, positions, w_ada, b_ada, g_pre_mix, g_post_mix, w_in,
              ssm_a_re, ssm_a_im, ssm_log_dt, ssm_b_re, ssm_b_im, ssm_c_re, ssm_c_im,
              ssm_d, w_glu, b_glu, g_attn_out, g_ssm_out, w_out,
              g_pre_mlp, g_post_mlp, w_mlp_in, w_mlp_out):
    B, L, _ = x.shape
    for l in range(DEPTH):
        mod = jax.nn.silu(c) @ w_ada[l] + b_ada[l]
        sh1, sc1, gt1, sh2, sc2, gt2 = (m[:, None, :] for m in jnp.split(mod, N_MOD, axis=-1))

        h = rms_norm(x, g_pre_mix[l]) * (1.0 + sc1) + sh1
        proj = h @ w_in[l]
        q = proj[..., :ATT_Q_WIDTH].reshape(B, L, N_Q_HEADS, HEAD_DIM)
        k = proj[..., ATT_Q_WIDTH:ATT_Q_WIDTH + KV_WIDTH].reshape(B, L, KV_HEADS, HEAD_DIM)
        v = proj[..., ATT_Q_WIDTH + KV_WIDTH:ATT_Q_WIDTH + 2 * KV_WIDTH].reshape(B, L, KV_HEADS, HEAD_DIM)
        u = proj[..., ATT_Q_WIDTH + 2 * KV_WIDTH:].reshape(B, L, SSM_GROUPS, SSM_GROUP_CH)

        q = rope_partial(q, positions).reshape(B, L, N_ATT_GROUPS, HEADS_PER_GROUP, HEAD_DIM)
        k = rope_partial(k, positions)
        outs, lses = [], []
        for gi, (window, dilation) in enumerate(DILATION_PATTERNS):
            o_g, lse_g = dilated_window_attention(q[:, :, gi], k, v, dilation, window // dilation)
            outs.append(o_g)
            lses.append(lse_g)
        wts = jax.nn.softmax(jnp.stack(lses, axis=0), axis=0)
        att = jnp.sum(wts[..., None].astype(x.dtype) * jnp.stack(outs, axis=0), axis=0)
        att = rms_norm(att.reshape(B, L, KV_WIDTH), g_attn_out[l])

        ssm = s5_mixer(u, ssm_a_re[l], ssm_a_im[l], ssm_log_dt[l], ssm_b_re[l], ssm_b_im[l],
                       ssm_c_re[l], ssm_c_im[l], ssm_d[l], w_glu[l], b_glu[l])
        ssm = rms_norm(ssm, g_ssm_out[l])

        mix = jnp.concatenate([att, ssm], axis=-1) @ w_out[l]
        x = x + gt1 * rms_norm(mix, g_post_mix[l])

        h = rms_norm(x, g_pre_mlp[l]) * (1.0 + sc2) + sh2
        y = jnp.square(jax.nn.relu(h @ w_mlp_in[l])) @ w_mlp_out[l]
        x = x + gt2 * rms_norm(y, g_post_mlp[l])
    return x


import jax as _jax
import jax.numpy as _jnp

TWIN_FORMAT = 'train_step'
FWD_PARAMS = ['x', 'c', 'positions', 'w_ada', 'b_ada', 'g_pre_mix', 'g_post_mix', 'w_in', 'ssm_a_re', 'ssm_a_im', 'ssm_log_dt', 'ssm_b_re', 'ssm_b_im', 'ssm_c_re', 'ssm_c_im', 'ssm_d', 'w_glu', 'b_glu', 'g_attn_out', 'g_ssm_out', 'w_out', 'g_pre_mlp', 'g_post_mlp', 'w_mlp_in', 'w_mlp_out']
TWIN_WEIGHTS = ['w_ada', 'b_ada', 'g_pre_mix', 'g_post_mix', 'w_in', 'ssm_a_re', 'ssm_a_im', 'ssm_log_dt', 'ssm_b_re', 'ssm_b_im', 'ssm_c_re', 'ssm_c_im', 'ssm_d', 'w_glu', 'b_glu', 'g_attn_out', 'g_ssm_out', 'w_out', 'g_pre_mlp', 'g_post_mlp', 'w_mlp_in', 'w_mlp_out']
TWIN_DIFF_INPUT = 'x'
TWIN_INPUTS = ['x', 'c', 'positions', 'w_ada', 'b_ada', 'g_pre_mix', 'g_post_mix', 'w_in', 'ssm_a_re', 'ssm_a_im', 'ssm_log_dt', 'ssm_b_re', 'ssm_b_im', 'ssm_c_re', 'ssm_c_im', 'ssm_d', 'w_glu', 'b_glu', 'g_attn_out', 'g_ssm_out', 'w_out', 'g_pre_mlp', 'g_post_mlp', 'w_mlp_in', 'w_mlp_out', 'loss_target', 'm_w_ada', 'm_b_ada', 'm_g_pre_mix', 'm_g_post_mix', 'm_w_in', 'm_ssm_a_re', 'm_ssm_a_im', 'm_ssm_log_dt', 'm_ssm_b_re', 'm_ssm_b_im', 'm_ssm_c_re', 'm_ssm_c_im', 'm_ssm_d', 'm_w_glu', 'm_b_glu', 'm_g_attn_out', 'm_g_ssm_out', 'm_w_out', 'm_g_pre_mlp', 'm_g_post_mlp', 'm_w_mlp_in', 'm_w_mlp_out', 'v_w_ada', 'v_b_ada', 'v_g_pre_mix', 'v_g_post_mix', 'v_w_in', 'v_ssm_a_re', 'v_ssm_a_im', 'v_ssm_log_dt', 'v_ssm_b_re', 'v_ssm_b_im', 'v_ssm_c_re', 'v_ssm_c_im', 'v_ssm_d', 'v_w_glu', 'v_b_glu', 'v_g_attn_out', 'v_g_ssm_out', 'v_w_out', 'v_g_pre_mlp', 'v_g_post_mlp', 'v_w_mlp_in', 'v_w_mlp_out']
TWIN_OUTPUTS = ['loss', 'grad_x', 'grad_w_ada', 'grad_b_ada', 'grad_g_pre_mix', 'grad_g_post_mix', 'grad_w_in', 'grad_ssm_a_re', 'grad_ssm_a_im', 'grad_ssm_log_dt', 'grad_ssm_b_re', 'grad_ssm_b_im', 'grad_ssm_c_re', 'grad_ssm_c_im', 'grad_ssm_d', 'grad_w_glu', 'grad_b_glu', 'grad_g_attn_out', 'grad_g_ssm_out', 'grad_w_out', 'grad_g_pre_mlp', 'grad_g_post_mlp', 'grad_w_mlp_in', 'grad_w_mlp_out', 'delta_w_ada', 'delta_b_ada', 'delta_g_pre_mix', 'delta_g_post_mix', 'delta_w_in', 'delta_ssm_a_re', 'delta_ssm_a_im', 'delta_ssm_log_dt', 'delta_ssm_b_re', 'delta_ssm_b_im', 'delta_ssm_c_re', 'delta_ssm_c_im', 'delta_ssm_d', 'delta_w_glu', 'delta_b_glu', 'delta_g_attn_out', 'delta_g_ssm_out', 'delta_w_out', 'delta_g_pre_mlp', 'delta_g_post_mlp', 'delta_w_mlp_in', 'delta_w_mlp_out', 'new_m_w_ada', 'new_m_b_ada', 'new_m_g_pre_mix', 'new_m_g_post_mix', 'new_m_w_in', 'new_m_ssm_a_re', 'new_m_ssm_a_im', 'new_m_ssm_log_dt', 'new_m_ssm_b_re', 'new_m_ssm_b_im', 'new_m_ssm_c_re', 'new_m_ssm_c_im', 'new_m_ssm_d', 'new_m_w_glu', 'new_m_b_glu', 'new_m_g_attn_out', 'new_m_g_ssm_out', 'new_m_w_out', 'new_m_g_pre_mlp', 'new_m_g_post_mlp', 'new_m_w_mlp_in', 'new_m_w_mlp_out', 'new_v_w_ada', 'new_v_b_ada', 'new_v_g_pre_mix', 'new_v_g_post_mix', 'new_v_w_in', 'new_v_ssm_a_re', 'new_v_ssm_a_im', 'new_v_ssm_log_dt', 'new_v_ssm_b_re', 'new_v_ssm_b_im', 'new_v_ssm_c_re', 'new_v_ssm_c_im', 'new_v_ssm_d', 'new_v_w_glu', 'new_v_b_glu', 'new_v_g_attn_out', 'new_v_g_ssm_out', 'new_v_w_out', 'new_v_g_pre_mlp', 'new_v_g_post_mlp', 'new_v_w_mlp_in', 'new_v_w_mlp_out']
TWIN_LEAF_KINDS = {'loss': 'loss', 'grad_x': 'grad_x', 'grad_w_ada': 'grad_w', 'grad_b_ada': 'grad_w', 'grad_g_pre_mix': 'grad_w', 'grad_g_post_mix': 'grad_w', 'grad_w_in': 'grad_w', 'grad_ssm_a_re': 'grad_w', 'grad_ssm_a_im': 'grad_w', 'grad_ssm_log_dt': 'grad_w', 'grad_ssm_b_re': 'grad_w', 'grad_ssm_b_im': 'grad_w', 'grad_ssm_c_re': 'grad_w', 'grad_ssm_c_im': 'grad_w', 'grad_ssm_d': 'grad_w', 'grad_w_glu': 'grad_w', 'grad_b_glu': 'grad_w', 'grad_g_attn_out': 'grad_w', 'grad_g_ssm_out': 'grad_w', 'grad_w_out': 'grad_w', 'grad_g_pre_mlp': 'grad_w', 'grad_g_post_mlp': 'grad_w', 'grad_w_mlp_in': 'grad_w', 'grad_w_mlp_out': 'grad_w', 'delta_w_ada': 'delta_w', 'delta_b_ada': 'delta_w', 'delta_g_pre_mix': 'delta_w', 'delta_g_post_mix': 'delta_w', 'delta_w_in': 'delta_w', 'delta_ssm_a_re': 'delta_w', 'delta_ssm_a_im': 'delta_w', 'delta_ssm_log_dt': 'delta_w', 'delta_ssm_b_re': 'delta_w', 'delta_ssm_b_im': 'delta_w', 'delta_ssm_c_re': 'delta_w', 'delta_ssm_c_im': 'delta_w', 'delta_ssm_d': 'delta_w', 'delta_w_glu': 'delta_w', 'delta_b_glu': 'delta_w', 'delta_g_attn_out': 'delta_w', 'delta_g_ssm_out': 'delta_w', 'delta_w_out': 'delta_w', 'delta_g_pre_mlp': 'delta_w', 'delta_g_post_mlp': 'delta_w', 'delta_w_mlp_in': 'delta_w', 'delta_w_mlp_out': 'delta_w', 'new_m_w_ada': 'new_m', 'new_m_b_ada': 'new_m', 'new_m_g_pre_mix': 'new_m', 'new_m_g_post_mix': 'new_m', 'new_m_w_in': 'new_m', 'new_m_ssm_a_re': 'new_m', 'new_m_ssm_a_im': 'new_m', 'new_m_ssm_log_dt': 'new_m', 'new_m_ssm_b_re': 'new_m', 'new_m_ssm_b_im': 'new_m', 'new_m_ssm_c_re': 'new_m', 'new_m_ssm_c_im': 'new_m', 'new_m_ssm_d': 'new_m', 'new_m_w_glu': 'new_m', 'new_m_b_glu': 'new_m', 'new_m_g_attn_out': 'new_m', 'new_m_g_ssm_out': 'new_m', 'new_m_w_out': 'new_m', 'new_m_g_pre_mlp': 'new_m', 'new_m_g_post_mlp': 'new_m', 'new_m_w_mlp_in': 'new_m', 'new_m_w_mlp_out': 'new_m', 'new_v_w_ada': 'new_v', 'new_v_b_ada': 'new_v', 'new_v_g_pre_mix': 'new_v', 'new_v_g_post_mix': 'new_v', 'new_v_w_in': 'new_v', 'new_v_ssm_a_re': 'new_v', 'new_v_ssm_a_im': 'new_v', 'new_v_ssm_log_dt': 'new_v', 'new_v_ssm_b_re': 'new_v', 'new_v_ssm_b_im': 'new_v', 'new_v_ssm_c_re': 'new_v', 'new_v_ssm_c_im': 'new_v', 'new_v_ssm_d': 'new_v', 'new_v_w_glu': 'new_v', 'new_v_b_glu': 'new_v', 'new_v_g_attn_out': 'new_v', 'new_v_g_ssm_out': 'new_v', 'new_v_w_out': 'new_v', 'new_v_g_pre_mlp': 'new_v', 'new_v_g_post_mlp': 'new_v', 'new_v_w_mlp_in': 'new_v', 'new_v_w_mlp_out': 'new_v'}


def _forward(args):
    return _fwd_reference(*[args[k] for k in FWD_PARAMS])


def _output_shape():
    def fwd():
        inp = _fwd_setup_inputs(0)
        return _fwd_reference(*[inp[k] for k in FWD_PARAMS])
    out = _jax.eval_shape(fwd)
    return out.shape, out.dtype

N_MICROBATCH = 1
ADAM_LR = 0.001
ADAM_B1 = 0.9
ADAM_B2 = 0.999
ADAM_EPS = 1e-08
ADAM_WD = 0.01
ADAM_STEP = 10
PER_EXAMPLE_BATCH_AXIS = {'x': 0, 'c': 0, 'positions': 0, 'loss_target': 0}
SHARED_INPUTS = []
_WEIGHT_DTYPES = {'w_ada': _jnp.float32, 'b_ada': _jnp.float32, 'g_pre_mix': _jnp.float32, 'g_post_mix': _jnp.float32, 'w_in': _jnp.float32, 'ssm_a_re': _jnp.float32, 'ssm_a_im': _jnp.float32, 'ssm_log_dt': _jnp.float32, 'ssm_b_re': _jnp.float32, 'ssm_b_im': _jnp.float32, 'ssm_c_re': _jnp.float32, 'ssm_c_im': _jnp.float32, 'ssm_d': _jnp.float32, 'w_glu': _jnp.float32, 'b_glu': _jnp.float32, 'g_attn_out': _jnp.float32, 'g_ssm_out': _jnp.float32, 'w_out': _jnp.float32, 'g_pre_mlp': _jnp.float32, 'g_post_mlp': _jnp.float32, 'w_mlp_in': _jnp.float32, 'w_mlp_out': _jnp.float32}
MOMENT_SCALE = {'w_ada': 7.565907e-01, 'b_ada': 1.546624e+00, 'g_pre_mix': 4.797436e-02, 'g_post_mix': 1.801460e+00, 'w_in': 2.452067e-01, 'ssm_a_re': 1.418244e-02, 'ssm_a_im': 1.388698e-02, 'ssm_log_dt': 1.247828e+00, 'ssm_b_re': 1.009245e-02, 'ssm_b_im': 1.016029e-02, 'ssm_c_re': 2.061238e-02, 'ssm_c_im': 1.996923e-02, 'ssm_d': 3.571803e-01, 'w_glu': 5.346621e-02, 'b_glu': 1.439610e-01, 'g_attn_out': 6.359846e-01, 'g_ssm_out': 3.278955e-01, 'w_out': 3.778185e-01, 'g_pre_mlp': 6.748297e-02, 'g_post_mlp': 1.699325e+00, 'w_mlp_in': 5.246183e-02, 'w_mlp_out': 2.170353e-01}


def _to_microbatches(a, axis):
    t = _jnp.moveaxis(a, axis, 0)
    t = t.reshape((N_MICROBATCH, t.shape[0] // N_MICROBATCH) + t.shape[1:])
    return _jnp.moveaxis(t, 1, axis + 1)


def setup_inputs(seed: int = 0) -> dict:
    inp = _fwd_setup_inputs(seed)
    key = _jax.random.fold_in(_jax.random.key(seed), 7919)
    shape, _ = _output_shape()
    out = dict(inp)
    out["loss_target"] = _jax.random.normal(_jax.random.fold_in(key, 0), shape, _jnp.float32)
    for i, name in enumerate(TWIN_WEIGHTS):
        w = inp[name].astype(_jnp.float32)
        if MOMENT_SCALE is None:
            s = _jnp.sqrt(_jnp.mean(_jnp.square(w)) + 1e-30)
        else:
            s = MOMENT_SCALE[name]
        km, kv = _jax.random.split(_jax.random.fold_in(key, i + 1))
        out[name] = w
        out["m_" + name] = s * _jax.random.normal(km, w.shape, _jnp.float32)
        out["v_" + name] = (s * s) * _jax.random.uniform(kv, w.shape, _jnp.float32, 0.5, 1.5)
    if N_MICROBATCH > 1:
        for name, axis in PER_EXAMPLE_BATCH_AXIS.items():
            out[name] = _to_microbatches(out[name], axis)
    return {'x': out['x'], 'c': out['c'], 'positions': out['positions'], 'w_ada': out['w_ada'], 'b_ada': out['b_ada'], 'g_pre_mix': out['g_pre_mix'], 'g_post_mix': out['g_post_mix'], 'w_in': out['w_in'], 'ssm_a_re': out['ssm_a_re'], 'ssm_a_im': out['ssm_a_im'], 'ssm_log_dt': out['ssm_log_dt'], 'ssm_b_re': out['ssm_b_re'], 'ssm_b_im': out['ssm_b_im'], 'ssm_c_re': out['ssm_c_re'], 'ssm_c_im': out['ssm_c_im'], 'ssm_d': out['ssm_d'], 'w_glu': out['w_glu'], 'b_glu': out['b_glu'], 'g_attn_out': out['g_attn_out'], 'g_ssm_out': out['g_ssm_out'], 'w_out': out['w_out'], 'g_pre_mlp': out['g_pre_mlp'], 'g_post_mlp': out['g_post_mlp'], 'w_mlp_in': out['w_mlp_in'], 'w_mlp_out': out['w_mlp_out'], 'loss_target': out['loss_target'], 'm_w_ada': out['m_w_ada'], 'm_b_ada': out['m_b_ada'], 'm_g_pre_mix': out['m_g_pre_mix'], 'm_g_post_mix': out['m_g_post_mix'], 'm_w_in': out['m_w_in'], 'm_ssm_a_re': out['m_ssm_a_re'], 'm_ssm_a_im': out['m_ssm_a_im'], 'm_ssm_log_dt': out['m_ssm_log_dt'], 'm_ssm_b_re': out['m_ssm_b_re'], 'm_ssm_b_im': out['m_ssm_b_im'], 'm_ssm_c_re': out['m_ssm_c_re'], 'm_ssm_c_im': out['m_ssm_c_im'], 'm_ssm_d': out['m_ssm_d'], 'm_w_glu': out['m_w_glu'], 'm_b_glu': out['m_b_glu'], 'm_g_attn_out': out['m_g_attn_out'], 'm_g_ssm_out': out['m_g_ssm_out'], 'm_w_out': out['m_w_out'], 'm_g_pre_mlp': out['m_g_pre_mlp'], 'm_g_post_mlp': out['m_g_post_mlp'], 'm_w_mlp_in': out['m_w_mlp_in'], 'm_w_mlp_out': out['m_w_mlp_out'], 'v_w_ada': out['v_w_ada'], 'v_b_ada': out['v_b_ada'], 'v_g_pre_mix': out['v_g_pre_mix'], 'v_g_post_mix': out['v_g_post_mix'], 'v_w_in': out['v_w_in'], 'v_ssm_a_re': out['v_ssm_a_re'], 'v_ssm_a_im': out['v_ssm_a_im'], 'v_ssm_log_dt': out['v_ssm_log_dt'], 'v_ssm_b_re': out['v_ssm_b_re'], 'v_ssm_b_im': out['v_ssm_b_im'], 'v_ssm_c_re': out['v_ssm_c_re'], 'v_ssm_c_im': out['v_ssm_c_im'], 'v_ssm_d': out['v_ssm_d'], 'v_w_glu': out['v_w_glu'], 'v_b_glu': out['v_b_glu'], 'v_g_attn_out': out['v_g_attn_out'], 'v_g_ssm_out': out['v_g_ssm_out'], 'v_w_out': out['v_w_out'], 'v_g_pre_mlp': out['v_g_pre_mlp'], 'v_g_post_mlp': out['v_g_post_mlp'], 'v_w_mlp_in': out['v_w_mlp_in'], 'v_w_mlp_out': out['v_w_mlp_out']}


def _loss(weights, diff, rest, loss_target):
    with _jax.named_scope("forward"):
        args = {**rest, TWIN_DIFF_INPUT: diff, **{k: w.astype(_WEIGHT_DTYPES[k]) for k, w in weights.items()}}
        y = _forward(args)
    with _jax.named_scope("loss_head"):
        err = _jnp.square(y.astype(_jnp.float32) - loss_target)
        return 0.5 * _jnp.sum(_jnp.mean(err, axis=-1)) if err.ndim else 0.5 * err


def _adamw(w, g, m, v):
    m = ADAM_B1 * m + (1.0 - ADAM_B1) * g
    v = ADAM_B2 * v + (1.0 - ADAM_B2) * _jnp.square(g)
    m_hat = m / (1.0 - ADAM_B1 ** ADAM_STEP)
    v_hat = v / (1.0 - ADAM_B2 ** ADAM_STEP)
    delta = -ADAM_LR * (m_hat / (_jnp.sqrt(v_hat) + ADAM_EPS) + ADAM_WD * w)
    return delta, m, v


def reference(x, c, positions, w_ada, b_ada, g_pre_mix, g_post_mix, w_in, ssm_a_re, ssm_a_im, ssm_log_dt, ssm_b_re, ssm_b_im, ssm_c_re, ssm_c_im, ssm_d, w_glu, b_glu, g_attn_out, g_ssm_out, w_out, g_pre_mlp, g_post_mlp, w_mlp_in, w_mlp_out, loss_target, m_w_ada, m_b_ada, m_g_pre_mix, m_g_post_mix, m_w_in, m_ssm_a_re, m_ssm_a_im, m_ssm_log_dt, m_ssm_b_re, m_ssm_b_im, m_ssm_c_re, m_ssm_c_im, m_ssm_d, m_w_glu, m_b_glu, m_g_attn_out, m_g_ssm_out, m_w_out, m_g_pre_mlp, m_g_post_mlp, m_w_mlp_in, m_w_mlp_out, v_w_ada, v_b_ada, v_g_pre_mix, v_g_post_mix, v_w_in, v_ssm_a_re, v_ssm_a_im, v_ssm_log_dt, v_ssm_b_re, v_ssm_b_im, v_ssm_c_re, v_ssm_c_im, v_ssm_d, v_w_glu, v_b_glu, v_g_attn_out, v_g_ssm_out, v_w_out, v_g_pre_mlp, v_g_post_mlp, v_w_mlp_in, v_w_mlp_out):
    given = dict(x=x, c=c, positions=positions, w_ada=w_ada, b_ada=b_ada, g_pre_mix=g_pre_mix, g_post_mix=g_post_mix, w_in=w_in, ssm_a_re=ssm_a_re, ssm_a_im=ssm_a_im, ssm_log_dt=ssm_log_dt, ssm_b_re=ssm_b_re, ssm_b_im=ssm_b_im, ssm_c_re=ssm_c_re, ssm_c_im=ssm_c_im, ssm_d=ssm_d, w_glu=w_glu, b_glu=b_glu, g_attn_out=g_attn_out, g_ssm_out=g_ssm_out, w_out=w_out, g_pre_mlp=g_pre_mlp, g_post_mlp=g_post_mlp, w_mlp_in=w_mlp_in, w_mlp_out=w_mlp_out, loss_target=loss_target, m_w_ada=m_w_ada, m_b_ada=m_b_ada, m_g_pre_mix=m_g_pre_mix, m_g_post_mix=m_g_post_mix, m_w_in=m_w_in, m_ssm_a_re=m_ssm_a_re, m_ssm_a_im=m_ssm_a_im, m_ssm_log_dt=m_ssm_log_dt, m_ssm_b_re=m_ssm_b_re, m_ssm_b_im=m_ssm_b_im, m_ssm_c_re=m_ssm_c_re, m_ssm_c_im=m_ssm_c_im, m_ssm_d=m_ssm_d, m_w_glu=m_w_glu, m_b_glu=m_b_glu, m_g_attn_out=m_g_attn_out, m_g_ssm_out=m_g_ssm_out, m_w_out=m_w_out, m_g_pre_mlp=m_g_pre_mlp, m_g_post_mlp=m_g_post_mlp, m_w_mlp_in=m_w_mlp_in, m_w_mlp_out=m_w_mlp_out, v_w_ada=v_w_ada, v_b_ada=v_b_ada, v_g_pre_mix=v_g_pre_mix, v_g_post_mix=v_g_post_mix, v_w_in=v_w_in, v_ssm_a_re=v_ssm_a_re, v_ssm_a_im=v_ssm_a_im, v_ssm_log_dt=v_ssm_log_dt, v_ssm_b_re=v_ssm_b_re, v_ssm_b_im=v_ssm_b_im, v_ssm_c_re=v_ssm_c_re, v_ssm_c_im=v_ssm_c_im, v_ssm_d=v_ssm_d, v_w_glu=v_w_glu, v_b_glu=v_b_glu, v_g_attn_out=v_g_attn_out, v_g_ssm_out=v_g_ssm_out, v_w_out=v_w_out, v_g_pre_mlp=v_g_pre_mlp, v_g_post_mlp=v_g_post_mlp, v_w_mlp_in=v_w_mlp_in, v_w_mlp_out=v_w_mlp_out)
    weights = {n: given[n] for n in TWIN_WEIGHTS}
    shared = {n: given[n] for n in SHARED_INPUTS}
    per_example = {n: given[n] for n in ['x', 'c', 'positions']}
    grad_fn = _jax.value_and_grad(_loss, argnums=(0, 1))

    def one_microbatch(ex, loss_target):
        ex = dict(ex)
        diff = ex.pop(TWIN_DIFF_INPUT)
        return grad_fn(weights, diff, {**shared, **ex}, loss_target)

    if N_MICROBATCH == 1:
        loss, (grad_w, grad_x) = one_microbatch(per_example, given["loss_target"])
    else:
        def body(carry, xs):
            loss_sum, grad_sum = carry
            l_k, (gw_k, gx_k) = one_microbatch(xs[0], xs[1])
            with _jax.named_scope("update"):
                return (loss_sum + l_k, _jax.tree.map(_jnp.add, grad_sum, gw_k)), gx_k

        init = (_jnp.zeros((), _jnp.float32), _jax.tree.map(_jnp.zeros_like, weights))
        (loss, grad_w), grad_x = _jax.lax.scan(body, init, (per_example, given["loss_target"]))
    with _jax.named_scope("update"):
        delta_w, new_m, new_v = {}, {}, {}
        for n in TWIN_WEIGHTS:
            delta_w[n], new_m[n], new_v[n] = _adamw(weights[n], grad_w[n], given["m_" + n], given["v_" + n])
    return (loss, grad_x, *[grad_w[n] for n in TWIN_WEIGHTS], *[delta_w[n] for n in TWIN_WEIGHTS],
            *[new_m[n] for n in TWIN_WEIGHTS], *[new_v[n] for n in TWIN_WEIGHTS])
```

```python
import functools
import math

import jax
import jax.numpy as jnp
from jax import lax
from jax.experimental import pallas as pl
from jax.experimental.pallas import tpu as pltpu

F32 = jnp.float32
BF16 = jnp.bfloat16
HI = lax.Precision.HIGHEST
MESH = pl.DeviceIdType.MESH

N_DEV = 8
L = 4096
D = 2048
HEAD_DIM = 64
N_GROUPS = 3
DILATIONS = (1, 4, 16)
HEADS = 6
QW = N_GROUPS * HEADS * HEAD_DIM
KVW = HEADS * HEAD_DIM
ROT_DIM = 16
ROPE_THETA = 500000.0
BLK = 128
NBLK = L // BLK
SSMW = D - QW
SSM_P = 16
SSM_G = SSMW // SSM_P
SSM_N = 64
SSM_GN = SSM_G * SSM_N
CL_G = 8
N_CL = SSM_G // CL_G
CL_U = CL_G * SSM_P
CL_S = CL_G * SSM_N
INW = QW + 2 * KVW + SSMW
OUTW = KVW + SSMW
DFF = 4 * D
NMOD = 6
EPS = 1e-6
LR, B1, B2, AEPS, WD, STEP = 0.001, 0.9, 0.999, 1e-08, 0.01, 10

T_SCAN = 512
MB = 2 ** 20

WEIGHTS = ['w_ada', 'b_ada', 'g_pre_mix', 'g_post_mix', 'w_in', 'ssm_a_re', 'ssm_a_im', 'ssm_log_dt',
           'ssm_b_re', 'ssm_b_im', 'ssm_c_re', 'ssm_c_im', 'ssm_d', 'w_glu', 'b_glu', 'g_attn_out',
           'g_ssm_out', 'w_out', 'g_pre_mlp', 'g_post_mlp', 'w_mlp_in', 'w_mlp_out']
BIG = ['w_in', 'w_glu', 'w_out', 'w_mlp_in', 'w_mlp_out']
SMALL = [n for n in WEIGHTS if n not in BIG and n != 'w_ada']
SMALL_SIZES = {'b_ada': NMOD * D, 'g_pre_mix': D, 'g_post_mix': D, 'ssm_a_re': SSM_GN, 'ssm_a_im': SSM_GN,
               'ssm_log_dt': SSM_G, 'ssm_b_re': SSM_GN * SSM_P, 'ssm_b_im': SSM_GN * SSM_P,
               'ssm_c_re': SSM_GN * SSM_P, 'ssm_c_im': SSM_GN * SSM_P, 'ssm_d': SSMW, 'b_glu': SSMW,
               'g_attn_out': KVW, 'g_ssm_out': SSMW, 'g_pre_mlp': D, 'g_post_mlp': D}
SEG = {n: -(-SMALL_SIZES[n] // 1024) * 1024 for n in SMALL}
NS = sum(SEG.values())


def _params(sem=None, vmem_mb=None):
    kw = {}
    if sem is not None:
        kw['dimension_semantics'] = sem
    if vmem_mb is not None:
        kw['vmem_limit_bytes'] = vmem_mb * MB
    return pltpu.CompilerParams(**kw)


def _vec(n):
    return pl.BlockSpec((1, n), lambda *_: (0, 0))


def _rms(x):
    return lax.rsqrt(jnp.mean(x * x, axis=-1, keepdims=True) + EPS)


def _rms_bwd(dn, n, r):
    return r * (dn - n * jnp.mean(dn * n, axis=-1, keepdims=True))


def _vec8(n):
    return pl.BlockSpec((8, n), lambda *_: (0, 0))


def _colsum(x):
    return jnp.sum(x.reshape(-1, 8, x.shape[-1]), axis=0)


def _fold8(last, *refs):
    @pl.when(last)
    def _():
        for r in refs:
            r[...] = jnp.broadcast_to(jnp.sum(r[...], axis=0, keepdims=True), r.shape)


def _mm(a, b, *, mode, name, out_dtype=F32, tm=1024, tn=1024, tk=512, epilogue=None, extra=None):
    if mode == 'nn':
        (M, K), (K2, N) = a.shape, b.shape
        dims = (((1,), (0,)), ((), ()))
        a_spec = pl.BlockSpec((tm, tk), lambda i, j, k: (i, k))
        b_spec = pl.BlockSpec((tk, tn), lambda i, j, k: (k, j))
    elif mode == 'nt':
        (M, K), (N, K2) = a.shape, b.shape
        dims = (((1,), (1,)), ((), ()))
        a_spec = pl.BlockSpec((tm, tk), lambda i, j, k: (i, k))
        b_spec = pl.BlockSpec((tn, tk), lambda i, j, k: (j, k))
    else:
        (K, M), (K2, N) = a.shape, b.shape
        dims = (((0,), (0,)), ((), ()))
        a_spec = pl.BlockSpec((tk, tm), lambda i, j, k: (k, i))
        b_spec = pl.BlockSpec((tk, tn), lambda i, j, k: (k, j))
    assert K == K2 and M % tm == 0 and N % tn == 0 and K % tk == 0, (name, a.shape, b.shape, tm, tn, tk)
    nk = K // tk
    o_spec = pl.BlockSpec((tm, tn), lambda i, j, k: (i, j))
    n_out = 2 if epilogue == 'relu2' else 1
    n_extra = 1 if extra is not None else 0

    def body(*refs):
        a_ref, b_ref = refs[0], refs[1]
        x_refs = refs[2:2 + n_extra]
        o_refs = refs[2 + n_extra:2 + n_extra + n_out]
        acc = refs[-1]
        k = pl.program_id(2)

        @pl.when(k == 0)
        def _():
            acc[...] = jnp.zeros_like(acc)

        acc[...] += lax.dot_general(a_ref[...], b_ref[...], dims, preferred_element_type=F32)

        @pl.when(k == nk - 1)
        def _():
            r = acc[...]
            if epilogue == 'relu2':
                o_refs[0][...] = r.astype(BF16)
                o_refs[1][...] = jnp.square(jnp.maximum(r, 0.0)).astype(BF16)
            elif epilogue == 'drelu2':
                pre = x_refs[0][...].astype(F32)
                o_refs[0][...] = (r * (2.0 * jnp.maximum(pre, 0.0))).astype(out_dtype)
            else:
                o_refs[0][...] = r.astype(out_dtype)

    if epilogue == 'relu2':
        out_shape = (jax.ShapeDtypeStruct((M, N), BF16), jax.ShapeDtypeStruct((M, N), BF16))
        out_specs = (o_spec, o_spec)
    else:
        out_shape = jax.ShapeDtypeStruct((M, N), out_dtype)
        out_specs = o_spec
    args = (a, b) + ((extra,) if extra is not None else ())
    in_specs = [a_spec, b_spec] + ([o_spec] if extra is not None else [])
    return pl.pallas_call(
        body, out_shape=out_shape, grid=(M // tm, N // tn, nk), in_specs=in_specs, out_specs=out_specs,
        scratch_shapes=[pltpu.VMEM((tm, tn), F32)], name=name,
        compiler_params=_params(('parallel', 'parallel', 'arbitrary'), 48))(*args)


TR = 256


def _rowspec(w=D):
    return pl.BlockSpec((TR, w), lambda i: (i, 0))


def _prenorm_fwd(x, g, sc, sh):
    def body(x_ref, g_ref, sc_ref, sh_ref, h_ref):
        xv = x_ref[...]
        n = xv * _rms(xv)
        h_ref[...] = ((n * g_ref[...]) * (1.0 + sc_ref[...]) + sh_ref[...]).astype(BF16)

    return pl.pallas_call(
        body, out_shape=jax.ShapeDtypeStruct((L, D), BF16), grid=(L // TR,),
        in_specs=[_rowspec(), _vec(D), _vec(D), _vec(D)], out_specs=_rowspec(), name='prenorm_fwd',
        compiler_params=_params(('parallel',), 40))(x, g, sc, sh)


def _postmix_fwd(x, mix, gpm, gt1, gpl, sc2, sh2):
    def body(x_ref, mix_ref, gpm_ref, gt1_ref, gpl_ref, sc2_ref, sh2_ref, x1_ref, h2_ref):
        mix_v = mix_ref[...]
        nm = mix_v * _rms(mix_v)
        x1 = x_ref[...] + gt1_ref[...] * (nm * gpm_ref[...])
        x1_ref[...] = x1
        n2 = x1 * _rms(x1)
        h2_ref[...] = ((n2 * gpl_ref[...]) * (1.0 + sc2_ref[...]) + sh2_ref[...]).astype(BF16)

    return pl.pallas_call(
        body, out_shape=(jax.ShapeDtypeStruct((L, D), F32), jax.ShapeDtypeStruct((L, D), BF16)), grid=(L // TR,),
        in_specs=[_rowspec(), _rowspec()] + [_vec(D)] * 5, out_specs=(_rowspec(), _rowspec()), name='postmix_fwd',
        compiler_params=_params(('parallel',), 40))(x, mix, gpm, gt1, gpl, sc2, sh2)


def _final_fwd_bwd(x1, y, tgt, g, gt2):
    def body(x1_ref, y_ref, t_ref, g_ref, gt2_ref, dx2_ref, dy_ref, loss_ref, dgt2_ref, dg_ref):
        @pl.when(pl.program_id(0) == 0)
        def _():
            loss_ref[...] = jnp.zeros_like(loss_ref)
            dgt2_ref[...] = jnp.zeros_like(dgt2_ref)
            dg_ref[...] = jnp.zeros_like(dg_ref)

        yv = y_ref[...]
        r = _rms(yv)
        n = yv * r
        ng = n * g_ref[...]
        x2 = x1_ref[...] + gt2_ref[...] * ng
        e = x2 - t_ref[...]
        loss_ref[...] += 0.5 * jnp.sum(jnp.mean(e * e, axis=-1, keepdims=True), axis=0, keepdims=True)
        dx2 = e * (1.0 / D)
        dx2_ref[...] = dx2
        dgt2_ref[...] += _colsum(dx2 * ng)
        dng = dx2 * gt2_ref[...]
        dg_ref[...] += _colsum(dng * n)
        dy_ref[...] = _rms_bwd(dng * g_ref[...], n, r).astype(BF16)
        _fold8(pl.program_id(0) == L // TR - 1, dgt2_ref, dg_ref)

    return pl.pallas_call(
        body,
        out_shape=(jax.ShapeDtypeStruct((L, D), F32), jax.ShapeDtypeStruct((L, D), BF16),
                   jax.ShapeDtypeStruct((8, 128), F32), jax.ShapeDtypeStruct((8, D), F32),
                   jax.ShapeDtypeStruct((8, D), F32)),
        grid=(L // TR,), in_specs=[_rowspec(), _rowspec(), _rowspec(), _vec(D), _vec(D)],
        out_specs=(_rowspec(), _rowspec(), _vec8(128), _vec8(D), _vec8(D)), name='final_fwd_bwd',
        compiler_params=_params(('arbitrary',), 40))(x1, y, tgt, g, gt2)


def _postmix_bwd(dx2, dh2, x1, mix, gpm, gt1, gpl, sc2):
    def body(dx2_ref, dh2_ref, x1_ref, mix_ref, gpm_ref, gt1_ref, gpl_ref, sc2_ref,
             dx1_ref, dmix_ref, dsc2_ref, dsh2_ref, dgpl_ref, dgt1_ref, dgpm_ref):
        @pl.when(pl.program_id(0) == 0)
        def _():
            for r_ in (dsc2_ref, dsh2_ref, dgpl_ref, dgt1_ref, dgpm_ref):
                r_[...] = jnp.zeros_like(r_)

        x1v = x1_ref[...]
        r2 = _rms(x1v)
        n2 = x1v * r2
        dh2v = dh2_ref[...]
        dsh2_ref[...] += _colsum(dh2v)
        dsc2_ref[...] += _colsum(dh2v * (n2 * gpl_ref[...]))
        t = dh2v * (1.0 + sc2_ref[...])
        dgpl_ref[...] += _colsum(t * n2)
        dx1 = dx2_ref[...] + _rms_bwd(t * gpl_ref[...], n2, r2)
        dx1_ref[...] = dx1
        mix_v = mix_ref[...]
        rm = _rms(mix_v)
        nm = mix_v * rm
        dgt1_ref[...] += _colsum(dx1 * (nm * gpm_ref[...]))
        u = dx1 * gt1_ref[...]
        dgpm_ref[...] += _colsum(u * nm)
        dmix_ref[...] = _rms_bwd(u * gpm_ref[...], nm, rm).astype(BF16)
        _fold8(pl.program_id(0) == L // TR - 1, dsc2_ref, dsh2_ref, dgpl_ref, dgt1_ref, dgpm_ref)

    vs = jax.ShapeDtypeStruct((8, D), F32)
    return pl.pallas_call(
        body, out_shape=(jax.ShapeDtypeStruct((L, D), F32), jax.ShapeDtypeStruct((L, D), BF16), vs, vs, vs, vs, vs),
        grid=(L // TR,), in_specs=[_rowspec()] * 4 + [_vec(D)] * 4,
        out_specs=(_rowspec(), _rowspec()) + (_vec8(D),) * 5, name='postmix_bwd',
        compiler_params=_params(('arbitrary',), 48))(dx2, dh2, x1, mix, gpm, gt1, gpl, sc2)


def _prenorm_bwd(dx1, dh1, x, g, sc1):
    def body(dx1_ref, dh1_ref, x_ref, g_ref, sc1_ref, dx_ref, dsc1_ref, dsh1_ref, dg_ref):
        @pl.when(pl.program_id(0) == 0)
        def _():
            for r_ in (dsc1_ref, dsh1_ref, dg_ref):
                r_[...] = jnp.zeros_like(r_)

        xv = x_ref[...]
        r = _rms(xv)
        n = xv * r
        dh = dh1_ref[...]
        dsh1_ref[...] += _colsum(dh)
        dsc1_ref[...] += _colsum(dh * (n * g_ref[...]))
        t = dh * (1.0 + sc1_ref[...])
        dg_ref[...] += _colsum(t * n)
        dx_ref[...] = dx1_ref[...] + _rms_bwd(t * g_ref[...], n, r)
        _fold8(pl.program_id(0) == L // TR - 1, dsc1_ref, dsh1_ref, dg_ref)

    vs = jax.ShapeDtypeStruct((8, D), F32)
    return pl.pallas_call(
        body, out_shape=(jax.ShapeDtypeStruct((L, D), F32), vs, vs, vs), grid=(L // TR,),
        in_specs=[_rowspec()] * 3 + [_vec(D)] * 2, out_specs=(_rowspec(),) + (_vec8(D),) * 3, name='prenorm_bwd',
        compiler_params=_params(('arbitrary',), 40))(dx1, dh1, x, g, sc1)


ROPE_W = QW + KVW


def _rope(xin, pos, fr, sign, name):
    def body(x_ref, pos_ref, fr_ref, o_ref):
        xv = x_ref[...]
        ang = pos_ref[...].astype(F32) * fr_ref[...]
        w = lax.broadcasted_iota(jnp.int32, (1, 128), 1) % HEAD_DIM
        cs = jnp.cos(ang)
        sn = jnp.sin(ang) * sign
        s1 = jnp.where(w < ROT_DIM // 2, -sn, 0.0)
        s2 = jnp.where((w >= ROT_DIM // 2) & (w < ROT_DIM), sn, 0.0)
        rep = ROPE_W // 128
        cs, s1, s2 = jnp.tile(cs, (1, rep)), jnp.tile(s1, (1, rep)), jnp.tile(s2, (1, rep))
        hi = pltpu.roll(xv, ROPE_W - ROT_DIM // 2, 1)
        lo = pltpu.roll(xv, ROT_DIM // 2, 1)
        o_ref[...] = (xv * cs + hi * s1 + lo * s2).astype(BF16)

    return pl.pallas_call(
        body, out_shape=jax.ShapeDtypeStruct((L, ROPE_W), BF16), grid=(L // TR,),
        in_specs=[_rowspec(ROPE_W), pl.BlockSpec((TR, 1), lambda i: (i, 0)), _vec(128)],
        out_specs=_rowspec(ROPE_W), name=name, compiler_params=_params(('parallel',), 40))(xin, pos, fr)


def _attn_mask(g, b):
    nbs = lax.shift_right_logical(jnp.int32(NBLK), 2 * g)
    first = (b & (nbs - 1)) == 0
    qi = lax.broadcasted_iota(jnp.int32, (BLK, 2 * BLK), 0)
    kj = lax.broadcasted_iota(jnp.int32, (BLK, 2 * BLK), 1)
    dist = qi + BLK - kj
    return (dist >= 0) & (dist <= BLK) & ((kj >= BLK) | jnp.logical_not(first))


def _blk(idx):
    return pl.BlockSpec((None, BLK, KVW), idx)


_CUR = lambda g, b: (g, b, 0)
_PREV = lambda g, b: (g, jnp.maximum(b - 1, 0), 0)
NEG = -1e30
NT_DIMS = (((1,), (1,)), ((), ()))
TN_DIMS = (((0,), (0,)), ((), ()))


def _attn_fwd(qp, kp, vp):
    def body(q_ref, kp_ref, kc_ref, vp_ref, vc_ref, o_ref, lse_ref):
        valid = _attn_mask(pl.program_id(0), pl.program_id(1))
        for h in range(HEADS):
            hs = slice(h * HEAD_DIM, (h + 1) * HEAD_DIM)
            q = q_ref[:, hs]
            kc = jnp.concatenate([kp_ref[:, hs], kc_ref[:, hs]], axis=0)
            vc = jnp.concatenate([vp_ref[:, hs], vc_ref[:, hs]], axis=0)
            s = lax.dot_general(q, kc, NT_DIMS, preferred_element_type=F32) * 0.125
            s = jnp.where(valid, s, NEG)
            m = jnp.max(s, axis=-1, keepdims=True)
            p = jnp.exp(s - m)
            l = jnp.sum(p, axis=-1, keepdims=True)
            o = jnp.dot(p.astype(BF16), vc, preferred_element_type=F32) / l
            o_ref[:, hs] = o
            lse_ref[:, hs] = jnp.broadcast_to(m + jnp.log(l), (BLK, HEAD_DIM))

    sh = jax.ShapeDtypeStruct((N_GROUPS, L, KVW), F32)
    return pl.pallas_call(
        body, out_shape=(sh, sh), grid=(N_GROUPS, NBLK),
        in_specs=[_blk(_CUR), _blk(_PREV), _blk(_CUR), _blk(_PREV), _blk(_CUR)],
        out_specs=(_blk(_CUR), _blk(_CUR)), name='attn_fwd',
        compiler_params=_params(('parallel', 'parallel'), 32))(qp, kp, kp, vp, vp)


def _attn_bwd(qp, kp, vp, o, lse, do, dlse):
    def body(q_ref, kp_ref, kc_ref, vp_ref, vc_ref, o_ref, lse_ref, do_ref, dlse_ref, dq_ref, dk_ref, dv_ref):
        b = pl.program_id(1)

        @pl.when(b == 0)
        def _():
            dk_ref[...] = jnp.zeros_like(dk_ref)
            dv_ref[...] = jnp.zeros_like(dv_ref)

        valid = _attn_mask(pl.program_id(0), b)
        prev0 = pl.multiple_of(jnp.maximum(b - 1, 0) * BLK, BLK)
        cur0 = pl.multiple_of(b * BLK, BLK)
        for h in range(HEADS):
            hs = slice(h * HEAD_DIM, (h + 1) * HEAD_DIM)
            q = q_ref[:, hs]
            kc = jnp.concatenate([kp_ref[:, hs], kc_ref[:, hs]], axis=0)
            vc = jnp.concatenate([vp_ref[:, hs], vc_ref[:, hs]], axis=0)
            s = lax.dot_general(q, kc, NT_DIMS, preferred_element_type=F32) * 0.125
            s = jnp.where(valid, s, NEG)
            p = jnp.exp(s - lse_ref[:, h * HEAD_DIM:h * HEAD_DIM + 1])
            do_h = do_ref[:, hs]
            delta = jnp.sum(do_h * o_ref[:, hs], axis=-1, keepdims=True)
            do_b = do_h.astype(BF16)
            dp = lax.dot_general(do_b, vc, NT_DIMS, preferred_element_type=F32)
            ds = p * (dp - delta + dlse_ref[:, h * HEAD_DIM:h * HEAD_DIM + 1])
            ds_b = (ds * 0.125).astype(BF16)
            dq_ref[:, hs] = jnp.dot(ds_b, kc, preferred_element_type=F32)
            dkc = lax.dot_general(ds_b, q, TN_DIMS, preferred_element_type=F32)
            dvc = lax.dot_general(p.astype(BF16), do_b, TN_DIMS, preferred_element_type=F32)
            dk_ref[pl.ds(prev0, BLK), hs] += dkc[:BLK]
            dv_ref[pl.ds(prev0, BLK), hs] += dvc[:BLK]
            dk_ref[pl.ds(cur0, BLK), hs] += dkc[BLK:]
            dv_ref[pl.ds(cur0, BLK), hs] += dvc[BLK:]

    sh = jax.ShapeDtypeStruct((N_GROUPS, L, KVW), F32)
    whole = pl.BlockSpec((None, L, KVW), lambda g, b: (g, 0, 0))
    return pl.pallas_call(
        body, out_shape=(sh, sh, sh), grid=(N_GROUPS, NBLK),
        in_specs=[_blk(_CUR), _blk(_PREV), _blk(_CUR), _blk(_PREV), _blk(_CUR)] + [_blk(_CUR)] * 4,
        out_specs=(_blk(_CUR), whole, whole), name='attn_bwd',
        compiler_params=_params(('arbitrary', 'arbitrary'), 48))(qp, kp, kp, vp, vp, o, lse, do, dlse)


TC = 512


def _comb_spec():
    return pl.BlockSpec((N_GROUPS, TC, KVW), lambda i: (0, i, 0))


def _combine_weights(lse_ref):
    l0, l1, l2 = lse_ref[0], lse_ref[1], lse_ref[2]
    m = jnp.maximum(jnp.maximum(l0, l1), l2)
    e0, e1, e2 = jnp.exp(l0 - m), jnp.exp(l1 - m), jnp.exp(l2 - m)
    z = e0 + e1 + e2
    return e0 / z, e1 / z, e2 / z


def _combine_fwd(o3, lse3, g):
    def body(o_ref, lse_ref, g_ref, att_ref):
        w0, w1, w2 = _combine_weights(lse_ref)
        a = w0 * o_ref[0] + w1 * o_ref[1] + w2 * o_ref[2]
        att_ref[...] = ((a * _rms(a)) * g_ref[...]).astype(BF16)

    return pl.pallas_call(
        body, out_shape=jax.ShapeDtypeStruct((L, KVW), BF16), grid=(L // TC,),
        in_specs=[_comb_spec(), _comb_spec(), _vec(KVW)], out_specs=pl.BlockSpec((TC, KVW), lambda i: (i, 0)),
        name='combine_fwd', compiler_params=_params(('parallel',), 40))(o3, lse3, g)


def _combine_bwd(datt, o3, lse3, g, head_ones):
    def body(datt_ref, o_ref, lse_ref, g_ref, e_ref, do_ref, dlse_ref, dg_ref):
        @pl.when(pl.program_id(0) == 0)
        def _():
            dg_ref[...] = jnp.zeros_like(dg_ref)

        ws = _combine_weights(lse_ref)
        a = ws[0] * o_ref[0] + ws[1] * o_ref[1] + ws[2] * o_ref[2]
        r = _rms(a)
        n = a * r
        dv = datt_ref[...]
        dg_ref[...] += _colsum(dv * n)
        da = _rms_bwd(dv * g_ref[...], n, r)
        dws = [jnp.dot(da * o_ref[i], e_ref[...], preferred_element_type=F32, precision=HI) for i in range(3)]
        dbar = ws[0] * dws[0] + ws[1] * dws[1] + ws[2] * dws[2]
        for i in range(3):
            do_ref[i] = ws[i] * da
            dlse_ref[i] = ws[i] * (dws[i] - dbar)
        _fold8(pl.program_id(0) == L // TC - 1, dg_ref)

    sh = jax.ShapeDtypeStruct((N_GROUPS, L, KVW), F32)
    return pl.pallas_call(
        body, out_shape=(sh, sh, jax.ShapeDtypeStruct((8, KVW), F32)), grid=(L // TC,),
        in_specs=[pl.BlockSpec((TC, KVW), lambda i: (i, 0)), _comb_spec(), _comb_spec(), _vec(KVW),
                  pl.BlockSpec((KVW, KVW), lambda i: (0, 0))],
        out_specs=(_comb_spec(), _comb_spec(), _vec8(KVW)), name='combine_bwd',
        compiler_params=_params(('arbitrary',), 48))(datt, o3, lse3, g, head_ones)


def _ssm_disc(ar, ai, ldt):
    dt = jnp.exp(ldt)
    zr, zi = ar * dt, ai * dt
    ez = jnp.exp(zr)
    A_r, A_i = ez * jnp.cos(zi), ez * jnp.sin(zi)
    den = ar * ar + ai * ai
    xr, xi = A_r - 1.0, A_i
    cr = (xr * ar + xi * ai) / den
    ci = (xi * ar - xr * ai) / den
    return dt, zr, zi, A_r, A_i, den, cr, ci


def _ssm_pre(ar, ai, ldt, br, bi):
    def body(ar_ref, ai_ref, ldt_ref, br_ref, bi_ref, bbr_ref, bbi_ref, pwr_ref, pwi_ref):
        _, zr, zi, _, _, _, cr, ci = _ssm_disc(ar_ref[...], ai_ref[...], ldt_ref[...])
        bbr_ref[...] = cr * br_ref[...] - ci * bi_ref[...]
        bbi_ref[...] = cr * bi_ref[...] + ci * br_ref[...]
        k = (lax.broadcasted_iota(jnp.int32, (1, 8), 1) + 1).astype(F32)
        ek = jnp.exp(zr * k)
        pwr_ref[...] = ek * jnp.cos(zi * k)
        pwi_ref[...] = ek * jnp.sin(zi * k)

    s16 = jax.ShapeDtypeStruct((SSM_GN, SSM_P), F32)
    s8 = jax.ShapeDtypeStruct((SSM_GN, 8), F32)
    return pl.pallas_call(body, out_shape=(s16, s16, s8, s8), name='ssm_pre',
                          compiler_params=_params(None, 40))(ar, ai, ldt, br, bi)


def _ssm_post(ar, ai, ldt, br, bi, gar, gai, gbr, gbi, sel):
    def body(ar_ref, ai_ref, ldt_ref, br_ref, bi_ref, gar_ref, gai_ref, gbr_ref, gbi_ref, sel_ref,
             dar_ref, dai_ref, dbr_ref, dbi_ref, dldt_ref):
        a_r, a_i = ar_ref[...], ai_ref[...]
        dt, _, _, A_r, A_i, den, cr, ci = _ssm_disc(a_r, a_i, ldt_ref[...])
        b_r, b_i, g_br, g_bi = br_ref[...], bi_ref[...], gbr_ref[...], gbi_ref[...]
        gcr = jnp.sum(g_br * b_r + g_bi * b_i, axis=-1, keepdims=True)
        gci = jnp.sum(g_bi * b_r - g_br * b_i, axis=-1, keepdims=True)
        dbr_ref[...] = g_br * cr + g_bi * ci
        dbi_ref[...] = g_bi * cr - g_br * ci
        g_ar = gar_ref[...] + (gcr * a_r - gci * a_i) / den
        g_ai = gai_ref[...] + (gcr * a_i + gci * a_r) / den
        qr = (cr * a_r + ci * a_i) / den
        qi = (ci * a_r - cr * a_i) / den
        glr = -(gcr * qr + gci * qi)
        gli = -(gci * qr - gcr * qi)
        gzr = g_ar * A_r + g_ai * A_i
        gzi = g_ai * A_r - g_ar * A_i
        dar_ref[...] = glr + gzr * dt
        dai_ref[...] = gli + gzi * dt
        gdt = (gzr * a_r + gzi * a_i) * dt
        dldt_ref[...] = jnp.dot(sel_ref[...], jnp.broadcast_to(gdt, (SSM_GN, 128)),
                                preferred_element_type=F32, precision=HI)

    s1 = jax.ShapeDtypeStruct((SSM_GN, 1), F32)
    s16 = jax.ShapeDtypeStruct((SSM_GN, SSM_P), F32)
    return pl.pallas_call(body, out_shape=(s1, s1, s16, s16, jax.ShapeDtypeStruct((SSM_G, 128), F32)),
                          name='ssm_post', compiler_params=_params(None, 48))(
                              ar, ai, ldt, br, bi, gar, gai, gbr, gbi, sel)


def _scan_fwd_tiles(s_ref, pw, carry):
    pwr, pwi = pw[:, :CL_S], pw[:, CL_S:]
    row = lax.broadcasted_iota(jnp.int32, (8, CL_S), 0)
    steps = [(k, pwr[k - 1:k], pwi[k - 1:k]) for k in (1, 2, 4)]

    def tile(i, c):
        cr, ci = c
        r0 = pl.multiple_of(i * 8, 8)
        xr = s_ref[pl.ds(r0, 8), 0:CL_S]
        xi = s_ref[pl.ds(r0, 8), CL_S:2 * CL_S]
        for k, pr, pi in steps:
            sr = jnp.where(row >= k, pltpu.roll(xr, k, 0), 0.0)
            si = jnp.where(row >= k, pltpu.roll(xi, k, 0), 0.0)
            xr, xi = xr + pr * sr - pi * si, xi + pr * si + pi * sr
        xr, xi = xr + pwr * cr - pwi * ci, xi + pwr * ci + pwi * cr
        s_ref[pl.ds(r0, 8), 0:CL_S] = xr
        s_ref[pl.ds(r0, 8), CL_S:2 * CL_S] = xi
        return xr[7:8], xi[7:8]

    return lax.fori_loop(0, T_SCAN // 8, tile, (carry[:, :CL_S], carry[:, CL_S:]))


def _scan_bwd_tiles(l_ref, pw, carry):
    pwr, pwi = pw[:, :CL_S], pw[:, CL_S:]
    rpr = jnp.concatenate([pwr[7 - r:8 - r] for r in range(8)], axis=0)
    rpi = jnp.concatenate([pwi[7 - r:8 - r] for r in range(8)], axis=0)
    row = lax.broadcasted_iota(jnp.int32, (8, CL_S), 0)
    steps = [(k, pwr[k - 1:k], pwi[k - 1:k]) for k in (1, 2, 4)]
    nt = T_SCAN // 8

    def tile(j, c):
        cr, ci = c
        r0 = pl.multiple_of((nt - 1 - j) * 8, 8)
        xr = l_ref[pl.ds(r0, 8), 0:CL_S]
        xi = l_ref[pl.ds(r0, 8), CL_S:2 * CL_S]
        for k, pr, pi in steps:
            sr = jnp.where(row < 8 - k, pltpu.roll(xr, 8 - k, 0), 0.0)
            si = jnp.where(row < 8 - k, pltpu.roll(xi, 8 - k, 0), 0.0)
            xr, xi = xr + pr * sr + pi * si, xi + pr * si - pi * sr
        xr, xi = xr + rpr * cr + rpi * ci, xi + rpr * ci - rpi * cr
        l_ref[pl.ds(r0, 8), 0:CL_S] = xr
        l_ref[pl.ds(r0, 8), CL_S:2 * CL_S] = xi
        return xr[0:1], xi[0:1]

    return lax.fori_loop(0, nt, tile, (carry[:, :CL_S], carry[:, CL_S:]))


NT_SCAN = L // T_SCAN


def _ssm_fwd(u, bm, cm, pw, dvec):
    def body(u_ref, bm_ref, cm_ref, pw_ref, d_ref, y_ref, bnd_ref, s_ref, carry_ref):
        @pl.when(pl.program_id(1) == 0)
        def _():
            carry_ref[...] = jnp.zeros_like(carry_ref)

        bnd_ref[...] = carry_ref[...]
        uv = u_ref[...]
        s_ref[...] = jnp.dot(uv, bm_ref[...], preferred_element_type=F32, precision=HI)
        cr, ci = _scan_fwd_tiles(s_ref, pw_ref[...], carry_ref[...])
        carry_ref[...] = jnp.concatenate([cr, ci], axis=1)
        y_ref[...] = jnp.dot(s_ref[...], cm_ref[...], preferred_element_type=F32, precision=HI) + d_ref[...] * uv

    return pl.pallas_call(
        body,
        out_shape=(jax.ShapeDtypeStruct((L, SSMW), F32), jax.ShapeDtypeStruct((N_CL, NT_SCAN, 1, 2 * CL_S), F32)),
        grid=(N_CL, NT_SCAN),
        in_specs=[pl.BlockSpec((T_SCAN, CL_U), lambda c, t: (t, c)),
                  pl.BlockSpec((None, CL_U, 2 * CL_S), lambda c, t: (c, 0, 0)),
                  pl.BlockSpec((None, 2 * CL_S, CL_U), lambda c, t: (c, 0, 0)),
                  pl.BlockSpec((None, 8, 2 * CL_S), lambda c, t: (c, 0, 0)),
                  pl.BlockSpec((1, CL_U), lambda c, t: (0, c))],
        out_specs=(pl.BlockSpec((T_SCAN, CL_U), lambda c, t: (t, c)),
                   pl.BlockSpec((None, None, 1, 2 * CL_S), lambda c, t: (c, t, 0, 0))),
        scratch_shapes=[pltpu.VMEM((T_SCAN, 2 * CL_S), F32), pltpu.VMEM((1, 2 * CL_S), F32)],
        name='ssm_fwd', compiler_params=_params(('arbitrary', 'arbitrary'), 40))(u, bm, cm, pw, dvec)


def _ssm_bwd(u, dy, bm, bmt, cmt, pw, dvec, bnd):
    rev = lambda t: NT_SCAN - 1 - t

    def body(u_ref, dy_ref, bm_ref, bmt_ref, cmt_ref, pw_ref, d_ref, bnd_ref,
             du_ref, dbm_ref, dcm_ref, da_ref, dd_ref, s_ref, l_ref, carry_ref):
        @pl.when(pl.program_id(1) == 0)
        def _():
            carry_ref[...] = jnp.zeros_like(carry_ref)
            dbm_ref[...] = jnp.zeros_like(dbm_ref)
            dcm_ref[...] = jnp.zeros_like(dcm_ref)
            da_ref[...] = jnp.zeros_like(da_ref)
            dd_ref[...] = jnp.zeros_like(dd_ref)

        uv, dyv, pw = u_ref[...], dy_ref[...], pw_ref[...]
        entry = bnd_ref[...]
        s_ref[...] = jnp.dot(uv, bm_ref[...], preferred_element_type=F32, precision=HI)
        _scan_fwd_tiles(s_ref, pw, entry)
        l_ref[...] = jnp.dot(dyv, cmt_ref[...], preferred_element_type=F32, precision=HI)
        cr, ci = _scan_bwd_tiles(l_ref, pw, carry_ref[...])
        carry_ref[...] = jnp.concatenate([cr, ci], axis=1)
        sv, lv = s_ref[...], l_ref[...]
        du_ref[...] = dyv * d_ref[...] + jnp.dot(lv, bmt_ref[...], preferred_element_type=F32, precision=HI)
        dbm_ref[...] += lax.dot_general(uv, lv, TN_DIMS, preferred_element_type=F32, precision=HI)
        dcm_ref[...] += lax.dot_general(sv, dyv, TN_DIMS, preferred_element_type=F32, precision=HI)
        dd_ref[...] += _colsum(dyv * uv)
        row = lax.broadcasted_iota(jnp.int32, (T_SCAN, 2 * CL_S), 0)
        sp = jnp.where(row == 0, entry, pltpu.roll(sv, 1, 0))
        spr, spi = sp[:, :CL_S], sp[:, CL_S:]
        lr, li = lv[:, :CL_S], lv[:, CL_S:]
        da_ref[:, 0:CL_S] += _colsum(lr * spr + li * spi)
        da_ref[:, CL_S:2 * CL_S] += _colsum(li * spr - lr * spi)
        _fold8(pl.program_id(1) == NT_SCAN - 1, da_ref, dd_ref)

    return pl.pallas_call(
        body,
        out_shape=(jax.ShapeDtypeStruct((L, SSMW), F32), jax.ShapeDtypeStruct((N_CL, CL_U, 2 * CL_S), F32),
                   jax.ShapeDtypeStruct((N_CL, 2 * CL_S, CL_U), F32), jax.ShapeDtypeStruct((N_CL, 8, 2 * CL_S), F32),
                   jax.ShapeDtypeStruct((8, SSMW), F32)),
        grid=(N_CL, NT_SCAN),
        in_specs=[pl.BlockSpec((T_SCAN, CL_U), lambda c, t: (rev(t), c)),
                  pl.BlockSpec((T_SCAN, CL_U), lambda c, t: (rev(t), c)),
                  pl.BlockSpec((None, CL_U, 2 * CL_S), lambda c, t: (c, 0, 0)),
                  pl.BlockSpec((None, 2 * CL_S, CL_U), lambda c, t: (c, 0, 0)),
                  pl.BlockSpec((None, CL_U, 2 * CL_S), lambda c, t: (c, 0, 0)),
                  pl.BlockSpec((None, 8, 2 * CL_S), lambda c, t: (c, 0, 0)),
                  pl.BlockSpec((1, CL_U), lambda c, t: (0, c)),
                  pl.BlockSpec((None, None, 1, 2 * CL_S), lambda c, t: (c, rev(t), 0, 0))],
        out_specs=(pl.BlockSpec((T_SCAN, CL_U), lambda c, t: (rev(t), c)),
                   pl.BlockSpec((None, CL_U, 2 * CL_S), lambda c, t: (c, 0, 0)),
                   pl.BlockSpec((None, 2 * CL_S, CL_U), lambda c, t: (c, 0, 0)),
                   pl.BlockSpec((None, 8, 2 * CL_S), lambda c, t: (c, 0, 0)),
                   pl.BlockSpec((8, CL_U), lambda c, t: (0, c))),
        scratch_shapes=[pltpu.VMEM((T_SCAN, 2 * CL_S), F32), pltpu.VMEM((T_SCAN, 2 * CL_S), F32),
                        pltpu.VMEM((1, 2 * CL_S), F32)],
        name='ssm_bwd', compiler_params=_params(('arbitrary', 'arbitrary'), 48))(u, dy, bm, bmt, cmt, pw, dvec, bnd)


GELU_C = math.sqrt(2.0 / math.pi)
GELU_K = 0.044715


def _gelu_parts(x):
    t = jnp.tanh(GELU_C * (x + GELU_K * (x * x * x)))
    return x * (0.5 * (1.0 + t)), t


def _glu_fwd(ypre, wglu, bglu, gs):
    def body(y_ref, w_ref, b_ref, g_ref, o_ref):
        yg, _ = _gelu_parts(y_ref[...])
        z = jnp.dot(yg.astype(BF16), w_ref[...], preferred_element_type=F32) + b_ref[...]
        s = yg * jax.nn.sigmoid(z)
        o_ref[...] = ((s * _rms(s)) * g_ref[...]).astype(BF16)

    return pl.pallas_call(
        body, out_shape=jax.ShapeDtypeStruct((L, SSMW), BF16), grid=(L // TR,),
        in_specs=[_rowspec(SSMW), pl.BlockSpec((SSMW, SSMW), lambda i: (0, 0)), _vec(SSMW), _vec(SSMW)],
        out_specs=_rowspec(SSMW), name='glu_fwd', compiler_params=_params(('parallel',), 32))(ypre, wglu, bglu, gs)


def _glu_bwd(ypre, dsn, wglu, bglu, gs):
    def body(y_ref, d_ref, w_ref, b_ref, g_ref, dy_ref, dw_ref, db_ref, dg_ref):
        @pl.when(pl.program_id(0) == 0)
        def _():
            dw_ref[...] = jnp.zeros_like(dw_ref)
            db_ref[...] = jnp.zeros_like(db_ref)
            dg_ref[...] = jnp.zeros_like(dg_ref)

        xv = y_ref[...]
        yg, t = _gelu_parts(xv)
        yg_b = yg.astype(BF16)
        z = jnp.dot(yg_b, w_ref[...], preferred_element_type=F32) + b_ref[...]
        sg = jax.nn.sigmoid(z)
        s = yg * sg
        r = _rms(s)
        n = s * r
        dv = d_ref[...]
        dg_ref[...] += _colsum(dv * n)
        ds = _rms_bwd(dv * g_ref[...], n, r)
        dz = (ds * yg) * (sg * (1.0 - sg))
        dz_b = dz.astype(BF16)
        db_ref[...] += _colsum(dz)
        dw_ref[...] += lax.dot_general(yg_b, dz_b, TN_DIMS, preferred_element_type=F32)
        dyg = ds * sg + lax.dot_general(dz_b, w_ref[...], NT_DIMS, preferred_element_type=F32)
        dgelu = 0.5 * (1.0 + t) + (0.5 * xv) * (1.0 - t * t) * (GELU_C * (1.0 + 3.0 * GELU_K * (xv * xv)))
        dy_ref[...] = dyg * dgelu
        _fold8(pl.program_id(0) == L // TR - 1, db_ref, dg_ref)

    vs = jax.ShapeDtypeStruct((8, SSMW), F32)
    return pl.pallas_call(
        body, out_shape=(jax.ShapeDtypeStruct((L, SSMW), F32), jax.ShapeDtypeStruct((SSMW, SSMW), F32), vs, vs),
        grid=(L // TR,),
        in_specs=[_rowspec(SSMW), _rowspec(SSMW), pl.BlockSpec((SSMW, SSMW), lambda i: (0, 0)), _vec(SSMW), _vec(SSMW)],
        out_specs=(_rowspec(SSMW), pl.BlockSpec((SSMW, SSMW), lambda i: (0, 0)), _vec8(SSMW), _vec8(SSMW)),
        name='glu_bwd', compiler_params=_params(('arbitrary',), 40))(ypre, dsn, wglu, bglu, gs)


def _me():
    return lax.axis_index('x'), lax.axis_index('y'), lax.axis_index('c')


def _peer(k):
    x, y, c = _me()
    px = 1 - x if k & 4 else x
    py = 1 - y if k & 2 else y
    pc = 1 - c if k & 1 else c
    return (px, py, pc), 4 * px + 2 * py + pc


def _mod_exchange(c_row, w_ada, b_ada8):
    cw = NMOD * D // N_DEV

    def body(c_ref, w_ref, b_ref, call_ref, mod_ref, part_ref, send_sems, recv_sems):
        x, y, c = _me()
        me = 4 * x + 2 * y + c
        call_ref[me] = c_ref[0]
        sends = []
        for k in range(1, N_DEV):
            peer, _ = _peer(k)
            cp = pltpu.make_async_remote_copy(src_ref=c_ref.at[0], dst_ref=call_ref.at[me], send_sem=send_sems.at[0, k - 1],
                                              recv_sem=recv_sems.at[0, k - 1], device_id=peer, device_id_type=MESH)
            cp.start()
            sends.append(cp)
        for k in range(1, N_DEV):
            peer, pidx = _peer(k)
            pltpu.make_async_remote_copy(src_ref=c_ref.at[0], dst_ref=call_ref.at[pidx], send_sem=send_sems.at[0, k - 1],
                                         recv_sem=recv_sems.at[0, k - 1], device_id=peer, device_id_type=MESH).wait_recv()
        for cp in sends:
            cp.wait_send()
        cv = call_ref[...].reshape(N_DEV, D)
        part = jnp.dot(cv * jax.nn.sigmoid(cv), w_ref[...], preferred_element_type=F32, precision=HI)
        part_ref[...] = part.reshape(N_DEV, 1, cw)
        mod_ref[me] = part_ref[me]
        sends = []
        for k in range(1, N_DEV):
            peer, pidx = _peer(k)
            cp = pltpu.make_async_remote_copy(src_ref=part_ref.at[pidx], dst_ref=mod_ref.at[me], send_sem=send_sems.at[1, k - 1],
                                              recv_sem=recv_sems.at[1, k - 1], device_id=peer, device_id_type=MESH)
            cp.start()
            sends.append(cp)
        for k in range(1, N_DEV):
            peer, pidx = _peer(k)
            pltpu.make_async_remote_copy(src_ref=part_ref.at[pidx], dst_ref=mod_ref.at[pidx], send_sem=send_sems.at[1, k - 1],
                                         recv_sem=recv_sems.at[1, k - 1], device_id=peer, device_id_type=MESH).wait_recv()
        for cp in sends:
            cp.wait_send()
        mod_ref[...] = mod_ref[...] + b_ref[...]

    vm = pl.BlockSpec(memory_space=pltpu.VMEM)
    return pl.pallas_call(
        body, out_shape=(jax.ShapeDtypeStruct((N_DEV, 1, D), F32), jax.ShapeDtypeStruct((N_DEV, 1, cw), F32)),
        in_specs=[vm, vm, vm], out_specs=(vm, vm),
        scratch_shapes=[pltpu.VMEM((N_DEV, 1, cw), F32), pltpu.SemaphoreType.DMA((2, N_DEV - 1)),
                        pltpu.SemaphoreType.DMA((2, N_DEV - 1))],
        name='mod_exchange', compiler_params=_params(None, 48))(c_row, w_ada, b_ada8)


def _exchange(arrs, scatter, name):
    n = len(arrs)
    hbm = pl.BlockSpec(memory_space=pltpu.HBM)

    def body(*refs):
        ins, outs = refs[:n], refs[n:2 * n]
        send_sems, recv_sems, local_sems = refs[2 * n:]
        x, y, c = _me()
        me = 4 * x + 2 * y + c
        locals_, sends = [], []
        for i in range(n):
            src_own = ins[i].at[me] if scatter else ins[i]
            lc = pltpu.make_async_copy(src_own, outs[i].at[me], local_sems.at[i])
            lc.start()
            locals_.append(lc)
            for k in range(1, N_DEV):
                peer, pidx = _peer(k)
                src = ins[i].at[pidx] if scatter else ins[i]
                cp = pltpu.make_async_remote_copy(src_ref=src, dst_ref=outs[i].at[me], send_sem=send_sems.at[i, k - 1],
                                                  recv_sem=recv_sems.at[i, k - 1], device_id=peer, device_id_type=MESH)
                cp.start()
                sends.append(cp)
        for i in range(n):
            for k in range(1, N_DEV):
                peer, pidx = _peer(k)
                src = ins[i].at[pidx] if scatter else ins[i]
                pltpu.make_async_remote_copy(src_ref=src, dst_ref=outs[i].at[pidx], send_sem=send_sems.at[i, k - 1],
                                             recv_sem=recv_sems.at[i, k - 1], device_id=peer, device_id_type=MESH).wait_recv()
        for cp in sends:
            cp.wait_send()
        for lc in locals_:
            lc.wait()

    out_shape = tuple(jax.ShapeDtypeStruct(a.shape if scatter else (N_DEV,) + a.shape, a.dtype) for a in arrs)
    return pl.pallas_call(
        body, out_shape=out_shape, in_specs=[hbm] * n, out_specs=(hbm,) * n,
        scratch_shapes=[pltpu.SemaphoreType.DMA((n, N_DEV - 1)), pltpu.SemaphoreType.DMA((n, N_DEV - 1)),
                        pltpu.SemaphoreType.DMA((n,))],
        name=name)(*arrs)


def _adam(w, g, m, v):
    m2 = B1 * m + (1.0 - B1) * g
    v2 = B2 * v + (1.0 - B2) * jnp.square(g)
    m_hat = m2 / (1.0 - B1 ** STEP)
    v_hat = v2 / (1.0 - B2 ** STEP)
    delta = -LR * (m_hat / (jnp.sqrt(v_hat) + AEPS) + WD * w)
    return delta, m2, v2


def _small_update(gp, wp, mp, vp):
    def body(g_ref, w_ref, m_ref, v_ref, all_ref, go_ref, d_ref, mo_ref, vo_ref, send_sems, recv_sems):
        x, y, c = _me()
        me = 4 * x + 2 * y + c
        all_ref[me] = g_ref[0]
        sends = []
        for k in range(1, N_DEV):
            peer, _ = _peer(k)
            cp = pltpu.make_async_remote_copy(src_ref=g_ref.at[0], dst_ref=all_ref.at[me], send_sem=send_sems.at[k - 1],
                                              recv_sem=recv_sems.at[k - 1], device_id=peer, device_id_type=MESH)
            cp.start()
            sends.append(cp)
        for k in range(1, N_DEV):
            peer, pidx = _peer(k)
            pltpu.make_async_remote_copy(src_ref=g_ref.at[0], dst_ref=all_ref.at[pidx], send_sem=send_sems.at[k - 1],
                                         recv_sem=recv_sems.at[k - 1], device_id=peer, device_id_type=MESH).wait_recv()
        for cp in sends:
            cp.wait_send()
        g = all_ref[0]
        for d in range(1, N_DEV):
            g = g + all_ref[d]
        delta, m2, v2 = _adam(w_ref[...], g, m_ref[...], v_ref[...])
        go_ref[...] = g
        d_ref[...] = delta
        mo_ref[...] = m2
        vo_ref[...] = v2

    vm = pl.BlockSpec(memory_space=pltpu.VMEM)
    vs = jax.ShapeDtypeStruct((1, NS), F32)
    return pl.pallas_call(
        body, out_shape=(jax.ShapeDtypeStruct((N_DEV, 1, NS), F32), vs, vs, vs, vs), in_specs=[vm] * 4,
        out_specs=(vm,) * 5,
        scratch_shapes=[pltpu.SemaphoreType.DMA((N_DEV - 1,)), pltpu.SemaphoreType.DMA((N_DEV - 1,))],
        name='small_update', compiler_params=_params(None, 48))(gp, wp, mp, vp)


def _big_update(parts, w, m, v, name):
    _, R, C = parts.shape
    tr = R if R % 256 else (128 if C >= 2048 else 256)

    def body(p_ref, w_ref, m_ref, v_ref, g_ref, d_ref, mo_ref, vo_ref):
        g = p_ref[0].astype(F32)
        for d in range(1, N_DEV):
            g = g + p_ref[d].astype(F32)
        delta, m2, v2 = _adam(w_ref[...], g, m_ref[...], v_ref[...])
        g_ref[...] = g
        d_ref[...] = delta
        mo_ref[...] = m2
        vo_ref[...] = v2

    blk = pl.BlockSpec((tr, C), lambda i: (i, 0))
    sh = jax.ShapeDtypeStruct((R, C), F32)
    return pl.pallas_call(
        body, out_shape=(sh, sh, sh, sh), grid=(R // tr,),
        in_specs=[pl.BlockSpec((N_DEV, tr, C), lambda i: (0, i, 0)), blk, blk, blk], out_specs=(blk,) * 4,
        name=name, compiler_params=_params(('parallel',), 48))(parts, w, m, v)


def _ada_update(c_all, dmod_cols, w, m, v):
    C = w.shape[1]
    tr = 256

    def body(c_ref, dm_ref, w_ref, m_ref, v_ref, g_ref, d_ref, mo_ref, vo_ref):
        cv = c_ref[...]
        s = cv * jax.nn.sigmoid(cv)
        g = lax.dot_general(s, dm_ref[...], TN_DIMS, preferred_element_type=F32, precision=HI)
        delta, m2, v2 = _adam(w_ref[...], g, m_ref[...], v_ref[...])
        g_ref[...] = g
        d_ref[...] = delta
        mo_ref[...] = m2
        vo_ref[...] = v2

    blk = pl.BlockSpec((tr, C), lambda i: (i, 0))
    sh = jax.ShapeDtypeStruct((D, C), F32)
    return pl.pallas_call(
        body, out_shape=(sh, sh, sh, sh), grid=(D // tr,),
        in_specs=[pl.BlockSpec((N_DEV, tr), lambda i: (0, i)), pl.BlockSpec((N_DEV, C), lambda i: (0, 0)), blk, blk, blk],
        out_specs=(blk,) * 4, name='ada_update', compiler_params=_params(('parallel',), 48))(c_all, dmod_cols, w, m, v)


def _to_sub(t, d):
    if d == 1:
        return t
    return t.reshape(L // d, d, t.shape[-1]).transpose(1, 0, 2).reshape(L, t.shape[-1])


def _from_sub(t, d):
    if d == 1:
        return t
    return t.reshape(d, L // d, t.shape[-1]).transpose(1, 0, 2).reshape(L, t.shape[-1])


def _rows_to_cluster_lanes(t):
    k = t.shape[1]
    return t.reshape(N_CL, CL_S, k).transpose(0, 2, 1)


def _blockdiag_in(t):
    t = t.reshape(N_CL, CL_G, SSM_N, SSM_P).transpose(0, 1, 3, 2)
    eye = jnp.eye(CL_G, dtype=t.dtype)
    t = t[:, :, :, None, :] * eye[None, :, None, :, None]
    return t.reshape(N_CL, CL_U, CL_S)


def _blockdiag_extract(t):
    t = t.reshape(N_CL, CL_G, SSM_P, CL_G, SSM_N)
    t = jnp.stack([t[:, i, :, i, :] for i in range(CL_G)], axis=1)
    return t.transpose(0, 1, 3, 2).reshape(SSM_GN, SSM_P)


def _c_to_rows(t):
    return t.transpose(0, 2, 1).reshape(SSM_GN, SSM_P)


def _rows_to_c(t):
    return t.reshape(SSM_G, SSM_N, SSM_P).transpose(0, 2, 1)


def _local_step(x, pos, mod, tgt, w_in, w_glu, w_out, w_mi, w_mo, sp):
    sh1, sc1, gt1, sh2, sc2, gt2 = (mod[i:i + 1] for i in range(NMOD))
    vec = lambda n: sp[n].reshape(1, -1)

    h1 = _prenorm_fwd(x, vec('g_pre_mix'), sc1, sh1)
    proj = _mm(h1, w_in, mode='nn', name='mm_in', tn=1408)
    fr1 = ROPE_THETA ** (-jnp.arange(0, ROT_DIM, 2, dtype=F32) / ROT_DIM)
    lane = jnp.arange(128) % HEAD_DIM
    fr = jnp.where(lane < ROT_DIM, fr1[lane % (ROT_DIM // 2)], 0.0).reshape(1, 128).astype(F32)
    qk = _rope(proj[:, :ROPE_W], pos, fr, 1.0, 'rope_fwd')
    v_b = proj[:, ROPE_W:ROPE_W + KVW].astype(BF16)
    u = proj[:, ROPE_W + KVW:]
    qp = jnp.stack([_to_sub(qk[:, gi * KVW:(gi + 1) * KVW], d) for gi, d in enumerate(DILATIONS)])
    kp = jnp.stack([_to_sub(qk[:, QW:], d) for d in DILATIONS])
    vp = jnp.stack([_to_sub(v_b, d) for d in DILATIONS])
    o_p, lse_p = _attn_fwd(qp, kp, vp)
    o3 = jnp.stack([_from_sub(o_p[gi], d) for gi, d in enumerate(DILATIONS)])
    lse3 = jnp.stack([_from_sub(lse_p[gi], d) for gi, d in enumerate(DILATIONS)])
    att = _combine_fwd(o3, lse3, vec('g_attn_out'))

    rows = lambda n: sp[n].reshape(SSM_GN, 1)
    a_re, a_im = rows('ssm_a_re'), rows('ssm_a_im')
    ldt = jnp.repeat(sp['ssm_log_dt'].reshape(SSM_G, 1), SSM_N, axis=0)
    b_re, b_im = sp['ssm_b_re'].reshape(SSM_GN, SSM_P), sp['ssm_b_im'].reshape(SSM_GN, SSM_P)
    c_re, c_im = _c_to_rows(sp['ssm_c_re'].reshape(SSM_G, SSM_P, SSM_N)), _c_to_rows(sp['ssm_c_im'].reshape(SSM_G, SSM_P, SSM_N))
    bbr, bbi, pwr, pwi = _ssm_pre(a_re, a_im, ldt, b_re, b_im)
    bm = jnp.concatenate([_blockdiag_in(bbr), _blockdiag_in(bbi)], axis=2)
    cmt = jnp.concatenate([_blockdiag_in(c_re), -_blockdiag_in(c_im)], axis=2)
    bmt, cm = bm.transpose(0, 2, 1), cmt.transpose(0, 2, 1)
    pw = jnp.concatenate([_rows_to_cluster_lanes(pwr), _rows_to_cluster_lanes(pwi)], axis=2)
    dvec = vec('ssm_d')
    ypre, bnd = _ssm_fwd(u, bm, cm, pw, dvec)
    ssm_n = _glu_fwd(ypre, w_glu, vec('b_glu'), vec('g_ssm_out'))

    cat = jnp.concatenate([att, ssm_n], axis=1)
    mix = _mm(cat, w_out, mode='nn', name='mm_out', tk=640)
    x1, h2 = _postmix_fwd(x, mix, vec('g_post_mix'), gt1, vec('g_pre_mlp'), sc2, sh2)
    a_pre, r_act = _mm(h2, w_mi, mode='nn', name='mm_mlp_in', epilogue='relu2')
    y = _mm(r_act, w_mo, mode='nn', name='mm_mlp_out')
    dx2, dy, loss, dgt2, dg_post_mlp = _final_fwd_bwd(x1, y, tgt, vec('g_post_mlp'), gt2)
    dgt2, dg_post_mlp = dgt2[:1], dg_post_mlp[:1]

    da = _mm(dy, w_mo, mode='nt', name='mm_d_act', out_dtype=BF16, epilogue='drelu2', extra=a_pre)
    g_w_mo = _mm(r_act, dy, mode='tn', name='mm_dw_mlp_out', out_dtype=BF16)
    dh2 = _mm(da, w_mi, mode='nt', name='mm_dh2')
    g_w_mi = _mm(h2, da, mode='tn', name='mm_dw_mlp_in', out_dtype=BF16)
    dx1, dmix, dsc2, dsh2, dg_pre_mlp, dgt1, dg_post_mix = _postmix_bwd(
        dx2, dh2, x1, mix, vec('g_post_mix'), gt1, vec('g_pre_mlp'), sc2)
    dsc2, dsh2, dg_pre_mlp, dgt1, dg_post_mix = (t[:1] for t in (dsc2, dsh2, dg_pre_mlp, dgt1, dg_post_mix))
    dcat = _mm(dmix, w_out, mode='nt', name='mm_dcat', tn=1280)
    g_w_out = _mm(cat, dmix, mode='tn', name='mm_dw_out', out_dtype=BF16, tm=640)
    datt, dsn = dcat[:, :KVW], dcat[:, KVW:]

    dypre, g_w_glu, g_b_glu, g_g_ssm = _glu_bwd(ypre, dsn, w_glu, vec('b_glu'), vec('g_ssm_out'))
    g_b_glu, g_g_ssm = g_b_glu[:1], g_g_ssm[:1]
    du, dbm, dcm, dA, dD = _ssm_bwd(u, dypre, bm, bmt, cmt, pw, dvec, bnd)
    dD = dD[:1]
    gbr, gbi = _blockdiag_extract(dbm[:, :, :CL_S]), _blockdiag_extract(dbm[:, :, CL_S:])
    dcmt = dcm.transpose(0, 2, 1)
    g_c_re = _rows_to_c(_blockdiag_extract(dcmt[:, :, :CL_S]))
    g_c_im = _rows_to_c(-_blockdiag_extract(dcmt[:, :, CL_S:]))
    gar = dA[:, 0, :CL_S].reshape(SSM_GN, 1)
    gai = dA[:, 0, CL_S:].reshape(SSM_GN, 1)
    sel = (jnp.arange(SSM_GN)[None, :] // SSM_N == jnp.arange(SSM_G)[:, None]).astype(F32)
    g_a_re, g_a_im, g_b_re, g_b_im, g_ldt = _ssm_post(a_re, a_im, ldt, b_re, b_im, gar, gai, gbr, gbi, sel)

    head_ones = (jnp.arange(KVW)[:, None] // HEAD_DIM == jnp.arange(KVW)[None, :] // HEAD_DIM).astype(F32)
    do3, dlse3, g_g_attn = _combine_bwd(datt, o3, lse3, vec('g_attn_out'), head_ones)
    g_g_attn = g_g_attn[:1]
    do_p = jnp.stack([_to_sub(do3[gi], d) for gi, d in enumerate(DILATIONS)])
    dlse_p = jnp.stack([_to_sub(dlse3[gi], d) for gi, d in enumerate(DILATIONS)])
    dq_p, dk_p, dv_p = _attn_bwd(qp, kp, vp, o_p, lse_p, do_p, dlse_p)
    dq = jnp.concatenate([_from_sub(dq_p[gi], d) for gi, d in enumerate(DILATIONS)], axis=1)
    dk = sum(_from_sub(dk_p[gi], d) for gi, d in enumerate(DILATIONS))
    dv = sum(_from_sub(dv_p[gi], d) for gi, d in enumerate(DILATIONS))
    dqk = _rope(jnp.concatenate([dq, dk], axis=1), pos, fr, -1.0, 'rope_bwd')
    dproj = jnp.concatenate([dqk, dv.astype(BF16), du.astype(BF16)], axis=1)
    dh1 = _mm(dproj, w_in, mode='nt', name='mm_dh1', tk=1408)
    g_w_in = _mm(h1, dproj, mode='tn', name='mm_dw_in', out_dtype=BF16, tn=1408)
    grad_x, dsc1, dsh1, dg_pre_mix = _prenorm_bwd(dx1, dh1, x, vec('g_pre_mix'), sc1)
    dsc1, dsh1, dg_pre_mix = dsc1[:1], dsh1[:1], dg_pre_mix[:1]

    dmod = jnp.concatenate([dsh1, dsc1, dgt1, dsh2, dsc2, dgt2], axis=0)
    big = {'w_in': g_w_in, 'w_glu': g_w_glu.astype(BF16), 'w_out': g_w_out, 'w_mlp_in': g_w_mi, 'w_mlp_out': g_w_mo}
    small = {'b_ada': dmod, 'g_pre_mix': dg_pre_mix, 'g_post_mix': dg_post_mix, 'ssm_a_re': g_a_re, 'ssm_a_im': g_a_im,
             'ssm_log_dt': g_ldt[:, 0], 'ssm_b_re': g_b_re, 'ssm_b_im': g_b_im, 'ssm_c_re': g_c_re, 'ssm_c_im': g_c_im,
             'ssm_d': dD, 'b_glu': g_b_glu, 'g_attn_out': g_g_attn, 'g_ssm_out': g_g_ssm, 'g_pre_mlp': dg_pre_mlp,
             'g_post_mlp': dg_post_mlp}
    return loss[0, 0], grad_x, big, small, dmod


def _pack(d):
    return jnp.concatenate([jnp.pad(d[n].reshape(-1).astype(F32), (0, SEG[n] - SMALL_SIZES[n])) for n in SMALL])


def _shard_major(t, name):
    if name in ('w_in', 'w_out', 'w_mlp_in'):
        k, n = t.shape
        return t.reshape(k, N_DEV, n // N_DEV).transpose(1, 0, 2)
    k, n = t.shape
    return t.reshape(N_DEV, k // N_DEV, n)


def _from_shard_major(t, name):
    if name in ('w_in', 'w_out', 'w_mlp_in'):
        _, k, n = t.shape
        return t.transpose(1, 0, 2).reshape(k, N_DEV * n)
    _, k, n = t.shape
    return t.reshape(N_DEV * k, n)


def kernel(x, c, positions, w_ada, b_ada, g_pre_mix, g_post_mix, w_in, ssm_a_re, ssm_a_im, ssm_log_dt, ssm_b_re, ssm_b_im, ssm_c_re, ssm_c_im, ssm_d, w_glu, b_glu, g_attn_out, g_ssm_out, w_out, g_pre_mlp, g_post_mlp, w_mlp_in, w_mlp_out, loss_target, m_w_ada, m_b_ada, m_g_pre_mix, m_g_post_mix, m_w_in, m_ssm_a_re, m_ssm_a_im, m_ssm_log_dt, m_ssm_b_re, m_ssm_b_im, m_ssm_c_re, m_ssm_c_im, m_ssm_d, m_w_glu, m_b_glu, m_g_attn_out, m_g_ssm_out, m_w_out, m_g_pre_mlp, m_g_post_mlp, m_w_mlp_in, m_w_mlp_out, v_w_ada, v_b_ada, v_g_pre_mix, v_g_post_mix, v_w_in, v_ssm_a_re, v_ssm_a_im, v_ssm_log_dt, v_ssm_b_re, v_ssm_b_im, v_ssm_c_re, v_ssm_c_im, v_ssm_d, v_w_glu, v_b_glu, v_g_attn_out, v_g_ssm_out, v_w_out, v_g_pre_mlp, v_g_post_mlp, v_w_mlp_in, v_w_mlp_out):
    loc = dict(locals())
    W = {n: loc[n] for n in WEIGHTS}
    M = {n: loc['m_' + n] for n in WEIGHTS}
    V = {n: loc['v_' + n] for n in WEIGHTS}
    assert x.shape == (1, L, D) and w_in.shape == (1, D, INW // N_DEV), (x.shape, w_in.shape)

    cw = NMOD * D // N_DEV
    c_all, mod8 = _mod_exchange(c.reshape(1, 1, D), w_ada[0], b_ada.reshape(N_DEV, 1, cw))
    mod = mod8.reshape(NMOD, D)

    gathered = _exchange([W[n][0].astype(BF16) for n in BIG], False, 'weight_gather')
    full = {n: _from_shard_major(g, n) for n, g in zip(BIG, gathered)}

    sp = {n: W[n][0] for n in SMALL}
    loss, grad_x, big, small, _ = _local_step(
        x[0], positions.reshape(L, 1), mod, loss_target[0], full['w_in'], full['w_glu'], full['w_out'],
        full['w_mlp_in'], full['w_mlp_out'], sp)
    loss = lax.psum(loss, ('x', 'y', 'c'))

    parts = _exchange([_shard_major(big[n], n) for n in BIG], True, 'grad_scatter')
    out_g, out_d, out_m, out_v = {}, {}, {}, {}
    for n, p in zip(BIG, parts):
        out_g[n], out_d[n], out_m[n], out_v[n] = _big_update(p, W[n][0], M[n][0], V[n][0], 'update_' + n)

    rows_all, sg, sd, sm, sv = _small_update(
        _pack(small).reshape(1, 1, NS), _pack({n: W[n] for n in SMALL}).reshape(1, NS),
        _pack({n: M[n] for n in SMALL}).reshape(1, NS), _pack({n: V[n] for n in SMALL}).reshape(1, NS))
    off = 0
    for n in SMALL:
        sz = SMALL_SIZES[n]
        for dst, src in ((out_g, sg), (out_d, sd), (out_m, sm), (out_v, sv)):
            dst[n] = src[0, off:off + sz].reshape(W[n].shape[1:])
        off += SEG[n]

    me = 4 * lax.axis_index('x') + 2 * lax.axis_index('y') + lax.axis_index('c')
    dmod_all = rows_all[:, 0, :NMOD * D]
    dmod_cols = lax.dynamic_slice_in_dim(dmod_all, me * cw, cw, axis=1)
    out_g['w_ada'], out_d['w_ada'], out_m['w_ada'], out_v['w_ada'] = _ada_update(
        c_all.reshape(N_DEV, D), dmod_cols, w_ada[0], m_w_ada[0], v_w_ada[0])

    lead = lambda t: t[None]
    return (loss, grad_x[None], *[lead(out_g[n]) for n in WEIGHTS], *[lead(out_d[n]) for n in WEIGHTS],
            *[lead(out_m[n]) for n in WEIGHTS], *[lead(out_v[n]) for n in WEIGHTS])
```

```python
import functools
import math

import jax
import jax.numpy as jnp
from jax import lax
from jax.experimental import pallas as pl
from jax.experimental.pallas import tpu as pltpu

F32 = jnp.float32
BF16 = jnp.bfloat16
HI = lax.Precision.HIGHEST
MESH = pl.DeviceIdType.MESH

N_DEV = 8
L = 4096
D = 2048
HEAD_DIM = 64
N_GROUPS = 3
DILATIONS = (1, 4, 16)
HEADS = 6
QW = N_GROUPS * HEADS * HEAD_DIM
KVW = HEADS * HEAD_DIM
ROT_DIM = 16
ROPE_THETA = 500000.0
BLK = 128
NBLK = L // BLK
SSMW = D - QW
SSM_P = 16
SSM_G = SSMW // SSM_P
SSM_N = 64
SSM_GN = SSM_G * SSM_N
CL_G = 8
N_CL = SSM_G // CL_G
CL_U = CL_G * SSM_P
CL_S = CL_G * SSM_N
INW = QW + 2 * KVW + SSMW
OUTW = KVW + SSMW
DFF = 4 * D
NMOD = 6
EPS = 1e-6
LR, B1, B2, AEPS, WD, STEP = 0.001, 0.9, 0.999, 1e-08, 0.01, 10

T_SCAN = 512
MB = 2 ** 20

WEIGHTS = ['w_ada', 'b_ada', 'g_pre_mix', 'g_post_mix', 'w_in', 'ssm_a_re', 'ssm_a_im', 'ssm_log_dt',
           'ssm_b_re', 'ssm_b_im', 'ssm_c_re', 'ssm_c_im', 'ssm_d', 'w_glu', 'b_glu', 'g_attn_out',
           'g_ssm_out', 'w_out', 'g_pre_mlp', 'g_post_mlp', 'w_mlp_in', 'w_mlp_out']
BIG = ['w_in', 'w_glu', 'w_out', 'w_mlp_in', 'w_mlp_out']
SMALL = [n for n in WEIGHTS if n not in BIG and n != 'w_ada']
SMALL_SIZES = {'b_ada': NMOD * D, 'g_pre_mix': D, 'g_post_mix': D, 'ssm_a_re': SSM_GN, 'ssm_a_im': SSM_GN,
               'ssm_log_dt': SSM_G, 'ssm_b_re': SSM_GN * SSM_P, 'ssm_b_im': SSM_GN * SSM_P,
               'ssm_c_re': SSM_GN * SSM_P, 'ssm_c_im': SSM_GN * SSM_P, 'ssm_d': SSMW, 'b_glu': SSMW,
               'g_attn_out': KVW, 'g_ssm_out': SSMW, 'g_pre_mlp': D, 'g_post_mlp': D}
SEG = {n: -(-SMALL_SIZES[n] // 1024) * 1024 for n in SMALL}
NS = sum(SEG.values())


def _params(sem=None, vmem_mb=None):
    kw = {}
    if sem is not None:
        kw['dimension_semantics'] = sem
    if vmem_mb is not None:
        kw['vmem_limit_bytes'] = vmem_mb * MB
    return pltpu.CompilerParams(**kw)


def _vec(n):
    return pl.BlockSpec((1, n), lambda *_: (0, 0))


def _rms(x):
    return lax.rsqrt(jnp.mean(x * x, axis=-1, keepdims=True) + EPS)


def _rms_bwd(dn, n, r):
    return r * (dn - n * jnp.mean(dn * n, axis=-1, keepdims=True))


def _vec8(n):
    return pl.BlockSpec((8, n), lambda *_: (0, 0))


def _colsum(x):
    return jnp.sum(x.reshape(-1, 8, x.shape[-1]), axis=0)


def _fold8(last, *refs):
    @pl.when(last)
    def _():
        for r in refs:
            r[...] = jnp.broadcast_to(jnp.sum(r[...], axis=0, keepdims=True), r.shape)


def _mm(a, b, *, mode, name, out_dtype=F32, tm=1024, tn=1024, tk=512, epilogue=None, extra=None):
    if mode == 'nn':
        (M, K), (K2, N) = a.shape, b.shape
        dims = (((1,), (0,)), ((), ()))
        a_spec = pl.BlockSpec((tm, tk), lambda i, j, k: (i, k))
        b_spec = pl.BlockSpec((tk, tn), lambda i, j, k: (k, j))
    elif mode == 'nt':
        (M, K), (N, K2) = a.shape, b.shape
        dims = (((1,), (1,)), ((), ()))
        a_spec = pl.BlockSpec((tm, tk), lambda i, j, k: (i, k))
        b_spec = pl.BlockSpec((tn, tk), lambda i, j, k: (j, k))
    else:
        (K, M), (K2, N) = a.shape, b.shape
        dims = (((0,), (0,)), ((), ()))
        a_spec = pl.BlockSpec((tk, tm), lambda i, j, k: (k, i))
        b_spec = pl.BlockSpec((tk, tn), lambda i, j, k: (k, j))
    assert K == K2 and M % tm == 0 and N % tn == 0 and K % tk == 0, (name, a.shape, b.shape, tm, tn, tk)
    nk = K // tk
    o_spec = pl.BlockSpec((tm, tn), lambda i, j, k: (i, j))
    n_out = 2 if epilogue == 'relu2' else 1
    n_extra = 1 if extra is not None else 0

    def body(*refs):
        a_ref, b_ref = refs[0], refs[1]
        x_refs = refs[2:2 + n_extra]
        o_refs = refs[2 + n_extra:2 + n_extra + n_out]
        acc = refs[-1]
        k = pl.program_id(2)

        @pl.when(k == 0)
        def _():
            acc[...] = jnp.zeros_like(acc)

        acc[...] += lax.dot_general(a_ref[...], b_ref[...], dims, preferred_element_type=F32)

        @pl.when(k == nk - 1)
        def _():
            r = acc[...]
            if epilogue == 'relu2':
                o_refs[0][...] = r.astype(BF16)
                o_refs[1][...] = jnp.square(jnp.maximum(r, 0.0)).astype(BF16)
            elif epilogue == 'drelu2':
                pre = x_refs[0][...].astype(F32)
                o_refs[0][...] = (r * (2.0 * jnp.maximum(pre, 0.0))).astype(out_dtype)
            else:
                o_refs[0][...] = r.astype(out_dtype)

    if epilogue == 'relu2':
        out_shape = (jax.ShapeDtypeStruct((M, N), BF16), jax.ShapeDtypeStruct((M, N), BF16))
        out_specs = (o_spec, o_spec)
    else:
        out_shape = jax.ShapeDtypeStruct((M, N), out_dtype)
        out_specs = o_spec
    args = (a, b) + ((extra,) if extra is not None else ())
    in_specs = [a_spec, b_spec] + ([o_spec] if extra is not None else [])
    return pl.pallas_call(
        body, out_shape=out_shape, grid=(M // tm, N // tn, nk), in_specs=in_specs, out_specs=out_specs,
        scratch_shapes=[pltpu.VMEM((tm, tn), F32)], name=name,
        compiler_params=_params(('parallel', 'parallel', 'arbitrary'), 48))(*args)


TR = 256


def _rowspec(w=D):
    return pl.BlockSpec((TR, w), lambda i: (i, 0))


def _prenorm_fwd(x, g, sc, sh):
    def body(x_ref, g_ref, sc_ref, sh_ref, h_ref):
        xv = x_ref[...]
        n = xv * _rms(xv)
        h_ref[...] = ((n * g_ref[...]) * (1.0 + sc_ref[...]) + sh_ref[...]).astype(BF16)

    return pl.pallas_call(
        body, out_shape=jax.ShapeDtypeStruct((L, D), BF16), grid=(L // TR,),
        in_specs=[_rowspec(), _vec(D), _vec(D), _vec(D)], out_specs=_rowspec(), name='prenorm_fwd',
        compiler_params=_params(('parallel',), 40))(x, g, sc, sh)


def _postmix_fwd(x, mix, gpm, gt1, gpl, sc2, sh2):
    def body(x_ref, mix_ref, gpm_ref, gt1_ref, gpl_ref, sc2_ref, sh2_ref, x1_ref, h2_ref):
        mix_v = mix_ref[...]
        nm = mix_v * _rms(mix_v)
        x1 = x_ref[...] + gt1_ref[...] * (nm * gpm_ref[...])
        x1_ref[...] = x1
        n2 = x1 * _rms(x1)
        h2_ref[...] = ((n2 * gpl_ref[...]) * (1.0 + sc2_ref[...]) + sh2_ref[...]).astype(BF16)

    return pl.pallas_call(
        body, out_shape=(jax.ShapeDtypeStruct((L, D), F32), jax.ShapeDtypeStruct((L, D), BF16)), grid=(L // TR,),
        in_specs=[_rowspec(), _rowspec()] + [_vec(D)] * 5, out_specs=(_rowspec(), _rowspec()), name='postmix_fwd',
        compiler_params=_params(('parallel',), 40))(x, mix, gpm, gt1, gpl, sc2, sh2)


def _final_fwd_bwd(x1, y, tgt, g, gt2):
    def body(x1_ref, y_ref, t_ref, g_ref, gt2_ref, dx2_ref, dy_ref, loss_ref, dgt2_ref, dg_ref):
        @pl.when(pl.program_id(0) == 0)
        def _():
            loss_ref[...] = jnp.zeros_like(loss_ref)
            dgt2_ref[...] = jnp.zeros_like(dgt2_ref)
            dg_ref[...] = jnp.zeros_like(dg_ref)

        yv = y_ref[...]
        r = _rms(yv)
        n = yv * r
        ng = n * g_ref[...]
        x2 = x1_ref[...] + gt2_ref[...] * ng
        e = x2 - t_ref[...]
        loss_ref[...] += 0.5 * jnp.sum(jnp.mean(e * e, axis=-1, keepdims=True), axis=0, keepdims=True)
        dx2 = e * (1.0 / D)
        dx2_ref[...] = dx2
        dgt2_ref[...] += _colsum(dx2 * ng)
        dng = dx2 * gt2_ref[...]
        dg_ref[...] += _colsum(dng * n)
        dy_ref[...] = _rms_bwd(dng * g_ref[...], n, r).astype(BF16)
        _fold8(pl.program_id(0) == L // TR - 1, dgt2_ref, dg_ref)

    return pl.pallas_call(
        body,
        out_shape=(jax.ShapeDtypeStruct((L, D), F32), jax.ShapeDtypeStruct((L, D), BF16),
                   jax.ShapeDtypeStruct((8, 128), F32), jax.ShapeDtypeStruct((8, D), F32),
                   jax.ShapeDtypeStruct((8, D), F32)),
        grid=(L // TR,), in_specs=[_rowspec(), _rowspec(), _rowspec(), _vec(D), _vec(D)],
        out_specs=(_rowspec(), _rowspec(), _vec8(128), _vec8(D), _vec8(D)), name='final_fwd_bwd',
        compiler_params=_params(('arbitrary',), 40))(x1, y, tgt, g, gt2)


def _postmix_bwd(dx2, dh2, x1, mix, gpm, gt1, gpl, sc2):
    def body(dx2_ref, dh2_ref, x1_ref, mix_ref, gpm_ref, gt1_ref, gpl_ref, sc2_ref,
             dx1_ref, dmix_ref, dsc2_ref, dsh2_ref, dgpl_ref, dgt1_ref, dgpm_ref):
        @pl.when(pl.program_id(0) == 0)
        def _():
            for r_ in (dsc2_ref, dsh2_ref, dgpl_ref, dgt1_ref, dgpm_ref):
                r_[...] = jnp.zeros_like(r_)

        x1v = x1_ref[...]
        r2 = _rms(x1v)
        n2 = x1v * r2
        dh2v = dh2_ref[...]
        dsh2_ref[...] += _colsum(dh2v)
        dsc2_ref[...] += _colsum(dh2v * (n2 * gpl_ref[...]))
        t = dh2v * (1.0 + sc2_ref[...])
        dgpl_ref[...] += _colsum(t * n2)
        dx1 = dx2_ref[...] + _rms_bwd(t * gpl_ref[...], n2, r2)
        dx1_ref[...] = dx1
        mix_v = mix_ref[...]
        rm = _rms(mix_v)
        nm = mix_v * rm
        dgt1_ref[...] += _colsum(dx1 * (nm * gpm_ref[...]))
        u = dx1 * gt1_ref[...]
        dgpm_ref[...] += _colsum(u * nm)
        dmix_ref[...] = _rms_bwd(u * gpm_ref[...], nm, rm).astype(BF16)
        _fold8(pl.program_id(0) == L // TR - 1, dsc2_ref, dsh2_ref, dgpl_ref, dgt1_ref, dgpm_ref)

    vs = jax.ShapeDtypeStruct((8, D), F32)
    return pl.pallas_call(
        body, out_shape=(jax.ShapeDtypeStruct((L, D), F32), jax.ShapeDtypeStruct((L, D), BF16), vs, vs, vs, vs, vs),
        grid=(L // TR,), in_specs=[_rowspec()] * 4 + [_vec(D)] * 4,
        out_specs=(_rowspec(), _rowspec()) + (_vec8(D),) * 5, name='postmix_bwd',
        compiler_params=_params(('arbitrary',), 48))(dx2, dh2, x1, mix, gpm, gt1, gpl, sc2)


def _prenorm_bwd(dx1, dh1, x, g, sc1):
    def body(dx1_ref, dh1_ref, x_ref, g_ref, sc1_ref, dx_ref, dsc1_ref, dsh1_ref, dg_ref):
        @pl.when(pl.program_id(0) == 0)
        def _():
            for r_ in (dsc1_ref, dsh1_ref, dg_ref):
                r_[...] = jnp.zeros_like(r_)

        xv = x_ref[...]
        r = _rms(xv)
        n = xv * r
        dh = dh1_ref[...]
        dsh1_ref[...] += _colsum(dh)
        dsc1_ref[...] += _colsum(dh * (n * g_ref[...]))
        t = dh * (1.0 + sc1_ref[...])
        dg_ref[...] += _colsum(t * n)
        dx_ref[...] = dx1_ref[...] + _rms_bwd(t * g_ref[...], n, r)
        _fold8(pl.program_id(0) == L // TR - 1, dsc1_ref, dsh1_ref, dg_ref)

    vs = jax.ShapeDtypeStruct((8, D), F32)
    return pl.pallas_call(
        body, out_shape=(jax.ShapeDtypeStruct((L, D), F32), vs, vs, vs), grid=(L // TR,),
        in_specs=[_rowspec()] * 3 + [_vec(D)] * 2, out_specs=(_rowspec(),) + (_vec8(D),) * 3, name='prenorm_bwd',
        compiler_params=_params(('arbitrary',), 40))(dx1, dh1, x, g, sc1)


ROPE_W = QW + KVW


def _rope(xin, pos, fr, sign, name):
    def body(x_ref, pos_ref, fr_ref, o_ref):
        xv = x_ref[...]
        ang = pos_ref[...].astype(F32) * fr_ref[...]
        w = lax.broadcasted_iota(jnp.int32, (1, 128), 1) % HEAD_DIM
        cs = jnp.cos(ang)
        sn = jnp.sin(ang) * sign
        s1 = jnp.where(w < ROT_DIM // 2, -sn, 0.0)
        s2 = jnp.where((w >= ROT_DIM // 2) & (w < ROT_DIM), sn, 0.0)
        rep = ROPE_W // 128
        cs, s1, s2 = jnp.tile(cs, (1, rep)), jnp.tile(s1, (1, rep)), jnp.tile(s2, (1, rep))
        hi = pltpu.roll(xv, ROPE_W - ROT_DIM // 2, 1)
        lo = pltpu.roll(xv, ROT_DIM // 2, 1)
        o_ref[...] = (xv * cs + hi * s1 + lo * s2).astype(BF16)

    return pl.pallas_call(
        body, out_shape=jax.ShapeDtypeStruct((L, ROPE_W), BF16), grid=(L // TR,),
        in_specs=[_rowspec(ROPE_W), pl.BlockSpec((TR, 1), lambda i: (i, 0)), _vec(128)],
        out_specs=_rowspec(ROPE_W), name=name, compiler_params=_params(('parallel',), 40))(xin, pos, fr)


def _attn_mask(g, b):
    nbs = lax.shift_right_logical(jnp.int32(NBLK), 2 * g)
    first = (b & (nbs - 1)) == 0
    qi = lax.broadcasted_iota(jnp.int32, (BLK, 2 * BLK), 0)
    kj = lax.broadcasted_iota(jnp.int32, (BLK, 2 * BLK), 1)
    dist = qi + BLK - kj
    return (dist >= 0) & (dist <= BLK) & ((kj >= BLK) | jnp.logical_not(first))


def _blk(idx):
    return pl.BlockSpec((None, BLK, KVW), idx)


_CUR = lambda g, b: (g, b, 0)
_PREV = lambda g, b: (g, jnp.maximum(b - 1, 0), 0)
NEG = -1e30
NT_DIMS = (((1,), (1,)), ((), ()))
TN_DIMS = (((0,), (0,)), ((), ()))


def _attn_fwd(qp, kp, vp):
    def body(q_ref, kp_ref, kc_ref, vp_ref, vc_ref, o_ref, lse_ref):
        valid = _attn_mask(pl.program_id(0), pl.program_id(1))
        for h in range(HEADS):
            hs = slice(h * HEAD_DIM, (h + 1) * HEAD_DIM)
            q = q_ref[:, hs]
            kc = jnp.concatenate([kp_ref[:, hs], kc_ref[:, hs]], axis=0)
            vc = jnp.concatenate([vp_ref[:, hs], vc_ref[:, hs]], axis=0)
            s = lax.dot_general(q, kc, NT_DIMS, preferred_element_type=F32) * 0.125
            s = jnp.where(valid, s, NEG)
            m = jnp.max(s, axis=-1, keepdims=True)
            p = jnp.exp(s - m)
            l = jnp.sum(p, axis=-1, keepdims=True)
            o = jnp.dot(p.astype(BF16), vc, preferred_element_type=F32) / l
            o_ref[:, hs] = o
            lse_ref[:, hs] = jnp.broadcast_to(m + jnp.log(l), (BLK, HEAD_DIM))

    sh = jax.ShapeDtypeStruct((N_GROUPS, L, KVW), F32)
    return pl.pallas_call(
        body, out_shape=(sh, sh), grid=(N_GROUPS, NBLK),
        in_specs=[_blk(_CUR), _blk(_PREV), _blk(_CUR), _blk(_PREV), _blk(_CUR)],
        out_specs=(_blk(_CUR), _blk(_CUR)), name='attn_fwd',
        compiler_params=_params(('parallel', 'parallel'), 32))(qp, kp, kp, vp, vp)


def _attn_bwd(qp, kp, vp, o, lse, do, dlse):
    def body(q_ref, kp_ref, kc_ref, vp_ref, vc_ref, o_ref, lse_ref, do_ref, dlse_ref, dq_ref, dk_ref, dv_ref):
        b = pl.program_id(1)

        @pl.when(b == 0)
        def _():
            dk_ref[...] = jnp.zeros_like(dk_ref)
            dv_ref[...] = jnp.zeros_like(dv_ref)

        valid = _attn_mask(pl.program_id(0), b)
        prev0 = pl.multiple_of(jnp.maximum(b - 1, 0) * BLK, BLK)
        cur0 = pl.multiple_of(b * BLK, BLK)
        for h in range(HEADS):
            hs = slice(h * HEAD_DIM, (h + 1) * HEAD_DIM)
            q = q_ref[:, hs]
            kc = jnp.concatenate([kp_ref[:, hs], kc_ref[:, hs]], axis=0)
            vc = jnp.concatenate([vp_ref[:, hs], vc_ref[:, hs]], axis=0)
            s = lax.dot_general(q, kc, NT_DIMS, preferred_element_type=F32) * 0.125
            s = jnp.where(valid, s, NEG)
            p = jnp.exp(s - lse_ref[:, h * HEAD_DIM:h * HEAD_DIM + 1])
            do_h = do_ref[:, hs]
            delta = jnp.sum(do_h * o_ref[:, hs], axis=-1, keepdims=True)
            do_b = do_h.astype(BF16)
            dp = lax.dot_general(do_b, vc, NT_DIMS, preferred_element_type=F32)
            ds = p * (dp - delta + dlse_ref[:, h * HEAD_DIM:h * HEAD_DIM + 1])
            ds_b = (ds * 0.125).astype(BF16)
            dq_ref[:, hs] = jnp.dot(ds_b, kc, preferred_element_type=F32)
            dkc = lax.dot_general(ds_b, q, TN_DIMS, preferred_element_type=F32)
            dvc = lax.dot_general(p.astype(BF16), do_b, TN_DIMS, preferred_element_type=F32)
            dk_ref[pl.ds(prev0, BLK), hs] += dkc[:BLK]
            dv_ref[pl.ds(prev0, BLK), hs] += dvc[:BLK]
            dk_ref[pl.ds(cur0, BLK), hs] += dkc[BLK:]
            dv_ref[pl.ds(cur0, BLK), hs] += dvc[BLK:]

    sh = jax.ShapeDtypeStruct((N_GROUPS, L, KVW), F32)
    whole = pl.BlockSpec((None, L, KVW), lambda g, b: (g, 0, 0))
    return pl.pallas_call(
        body, out_shape=(sh, sh, sh), grid=(N_GROUPS, NBLK),
        in_specs=[_blk(_CUR), _blk(_PREV), _blk(_CUR), _blk(_PREV), _blk(_CUR)] + [_blk(_CUR)] * 4,
        out_specs=(_blk(_CUR), whole, whole), name='attn_bwd',
        compiler_params=_params(('arbitrary', 'arbitrary'), 48))(qp, kp, kp, vp, vp, o, lse, do, dlse)


TC = 512


def _comb_spec():
    return pl.BlockSpec((N_GROUPS, TC, KVW), lambda i: (0, i, 0))


def _combine_weights(lse_ref):
    l0, l1, l2 = lse_ref[0], lse_ref[1], lse_ref[2]
    m = jnp.maximum(jnp.maximum(l0, l1), l2)
    e0, e1, e2 = jnp.exp(l0 - m), jnp.exp(l1 - m), jnp.exp(l2 - m)
    z = e0 + e1 + e2
    return e0 / z, e1 / z, e2 / z


def _combine_fwd(o3, lse3, g):
    def body(o_ref, lse_ref, g_ref, att_ref):
        w0, w1, w2 = _combine_weights(lse_ref)
        a = w0 * o_ref[0] + w1 * o_ref[1] + w2 * o_ref[2]
        att_ref[...] = ((a * _rms(a)) * g_ref[...]).astype(BF16)

    return pl.pallas_call(
        body, out_shape=jax.ShapeDtypeStruct((L, KVW), BF16), grid=(L // TC,),
        in_specs=[_comb_spec(), _comb_spec(), _vec(KVW)], out_specs=pl.BlockSpec((TC, KVW), lambda i: (i, 0)),
        name='combine_fwd', compiler_params=_params(('parallel',), 40))(o3, lse3, g)


def _combine_bwd(datt, o3, lse3, g, head_ones):
    def body(datt_ref, o_ref, lse_ref, g_ref, e_ref, do_ref, dlse_ref, dg_ref):
        @pl.when(pl.program_id(0) == 0)
        def _():
            dg_ref[...] = jnp.zeros_like(dg_ref)

        ws = _combine_weights(lse_ref)
        a = ws[0] * o_ref[0] + ws[1] * o_ref[1] + ws[2] * o_ref[2]
        r = _rms(a)
        n = a * r
        dv = datt_ref[...]
        dg_ref[...] += _colsum(dv * n)
        da = _rms_bwd(dv * g_ref[...], n, r)
        dws = [jnp.dot(da * o_ref[i], e_ref[...], preferred_element_type=F32, precision=HI) for i in range(3)]
        dbar = ws[0] * dws[0] + ws[1] * dws[1] + ws[2] * dws[2]
        for i in range(3):
            do_ref[i] = ws[i] * da
            dlse_ref[i] = ws[i] * (dws[i] - dbar)
        _fold8(pl.program_id(0) == L // TC - 1, dg_ref)

    sh = jax.ShapeDtypeStruct((N_GROUPS, L, KVW), F32)
    return pl.pallas_call(
        body, out_shape=(sh, sh, jax.ShapeDtypeStruct((8, KVW), F32)), grid=(L // TC,),
        in_specs=[pl.BlockSpec((TC, KVW), lambda i: (i, 0)), _comb_spec(), _comb_spec(), _vec(KVW),
                  pl.BlockSpec((KVW, KVW), lambda i: (0, 0))],
        out_specs=(_comb_spec(), _comb_spec(), _vec8(KVW)), name='combine_bwd',
        compiler_params=_params(('arbitrary',), 48))(datt, o3, lse3, g, head_ones)


def _ssm_disc(ar, ai, ldt):
    dt = jnp.exp(ldt)
    zr, zi = ar * dt, ai * dt
    ez = jnp.exp(zr)
    A_r, A_i = ez * jnp.cos(zi), ez * jnp.sin(zi)
    den = ar * ar + ai * ai
    xr, xi = A_r - 1.0, A_i
    cr = (xr * ar + xi * ai) / den
    ci = (xi * ar - xr * ai) / den
    return dt, zr, zi, A_r, A_i, den, cr, ci


def _ssm_pre(ar, ai, ldt, br, bi):
    def body(ar_ref, ai_ref, ldt_ref, br_ref, bi_ref, bbr_ref, bbi_ref, pwr_ref, pwi_ref):
        _, zr, zi, _, _, _, cr, ci = _ssm_disc(ar_ref[...], ai_ref[...], ldt_ref[...])
        bbr_ref[...] = cr * br_ref[...] - ci * bi_ref[...]
        bbi_ref[...] = cr * bi_ref[...] + ci * br_ref[...]
        k = (lax.broadcasted_iota(jnp.int32, (1, 8), 1) + 1).astype(F32)
        ek = jnp.exp(zr * k)
        pwr_ref[...] = ek * jnp.cos(zi * k)
        pwi_ref[...] = ek * jnp.sin(zi * k)

    s16 = jax.ShapeDtypeStruct((SSM_GN, SSM_P), F32)
    s8 = jax.ShapeDtypeStruct((SSM_GN, 8), F32)
    return pl.pallas_call(body, out_shape=(s16, s16, s8, s8), name='ssm_pre',
                          compiler_params=_params(None, 40))(ar, ai, ldt, br, bi)


def _ssm_post(ar, ai, ldt, br, bi, gar, gai, gbr, gbi, sel):
    def body(ar_ref, ai_ref, ldt_ref, br_ref, bi_ref, gar_ref, gai_ref, gbr_ref, gbi_ref, sel_ref,
             dar_ref, dai_ref, dbr_ref, dbi_ref, dldt_ref):
        a_r, a_i = ar_ref[...], ai_ref[...]
        dt, _, _, A_r, A_i, den, cr, ci = _ssm_disc(a_r, a_i, ldt_ref[...])
        b_r, b_i, g_br, g_bi = br_ref[...], bi_ref[...], gbr_ref[...], gbi_ref[...]
        gcr = jnp.sum(g_br * b_r + g_bi * b_i, axis=-1, keepdims=True)
        gci = jnp.sum(g_bi * b_r - g_br * b_i, axis=-1, keepdims=True)
        dbr_ref[...] = g_br * cr + g_bi * ci
        dbi_ref[...] = g_bi * cr - g_br * ci
        g_ar = gar_ref[...] + (gcr * a_r - gci * a_i) / den
        g_ai = gai_ref[...] + (gcr * a_i + gci * a_r) / den
        qr = (cr * a_r + ci * a_i) / den
        qi = (ci * a_r - cr * a_i) / den
        glr = -(gcr * qr + gci * qi)
        gli = -(gci * qr - gcr * qi)
        gzr = g_ar * A_r + g_ai * A_i
        gzi = g_ai * A_r - g_ar * A_i
        dar_ref[...] = glr + gzr * dt
        dai_ref[...] = gli + gzi * dt
        gdt = (gzr * a_r + gzi * a_i) * dt
        dldt_ref[...] = jnp.dot(sel_ref[...], jnp.broadcast_to(gdt, (SSM_GN, 128)),
                                preferred_element_type=F32, precision=HI)

    s1 = jax.ShapeDtypeStruct((SSM_GN, 1), F32)
    s16 = jax.ShapeDtypeStruct((SSM_GN, SSM_P), F32)
    return pl.pallas_call(body, out_shape=(s1, s1, s16, s16, jax.ShapeDtypeStruct((SSM_G, 128), F32)),
                          name='ssm_post', compiler_params=_params(None, 48))(
                              ar, ai, ldt, br, bi, gar, gai, gbr, gbi, sel)


SCAN_CH = 8


def _scan_fwd_tiles(s_ref, pw, carry):
    pwr, pwi = pw[:, :CL_S], pw[:, CL_S:]
    row = lax.broadcasted_iota(jnp.int32, (SCAN_CH, 8, CL_S), 1)
    steps = [(k, pwr[k - 1:k], pwi[k - 1:k]) for k in (1, 2, 4)]
    rows = 8 * SCAN_CH

    def chunk(i, c):
        cr, ci = c
        r0 = pl.multiple_of(i * rows, rows)
        xr = s_ref[pl.ds(r0, rows), 0:CL_S].reshape(SCAN_CH, 8, CL_S)
        xi = s_ref[pl.ds(r0, rows), CL_S:2 * CL_S].reshape(SCAN_CH, 8, CL_S)
        for k, pr, pi in steps:
            sr = jnp.where(row >= k, pltpu.roll(xr, k, 1), 0.0)
            si = jnp.where(row >= k, pltpu.roll(xi, k, 1), 0.0)
            xr, xi = xr + pr * sr - pi * si, xi + pr * si + pi * sr
        for j in range(SCAN_CH):
            tr = xr[j] + pwr * cr - pwi * ci
            ti = xi[j] + pwr * ci + pwi * cr
            s_ref[pl.ds(r0 + 8 * j, 8), 0:CL_S] = tr
            s_ref[pl.ds(r0 + 8 * j, 8), CL_S:2 * CL_S] = ti
            cr, ci = tr[7:8], ti[7:8]
        return cr, ci

    return lax.fori_loop(0, T_SCAN // rows, chunk, (carry[:, :CL_S], carry[:, CL_S:]))


def _scan_bwd_tiles(l_ref, pw, carry):
    pwr, pwi = pw[:, :CL_S], pw[:, CL_S:]
    rpr = jnp.concatenate([pwr[7 - r:8 - r] for r in range(8)], axis=0)
    rpi = jnp.concatenate([pwi[7 - r:8 - r] for r in range(8)], axis=0)
    row = lax.broadcasted_iota(jnp.int32, (SCAN_CH, 8, CL_S), 1)
    steps = [(k, pwr[k - 1:k], pwi[k - 1:k]) for k in (1, 2, 4)]
    rows = 8 * SCAN_CH
    nc = T_SCAN // rows

    def chunk(i, c):
        cr, ci = c
        r0 = pl.multiple_of((nc - 1 - i) * rows, rows)
        xr = l_ref[pl.ds(r0, rows), 0:CL_S].reshape(SCAN_CH, 8, CL_S)
        xi = l_ref[pl.ds(r0, rows), CL_S:2 * CL_S].reshape(SCAN_CH, 8, CL_S)
        for k, pr, pi in steps:
            sr = jnp.where(row < 8 - k, pltpu.roll(xr, 8 - k, 1), 0.0)
            si = jnp.where(row < 8 - k, pltpu.roll(xi, 8 - k, 1), 0.0)
            xr, xi = xr + pr * sr + pi * si, xi + pr * si - pi * sr
        for j in reversed(range(SCAN_CH)):
            tr = xr[j] + rpr * cr + rpi * ci
            ti = xi[j] + rpr * ci - rpi * cr
            l_ref[pl.ds(r0 + 8 * j, 8), 0:CL_S] = tr
            l_ref[pl.ds(r0 + 8 * j, 8), CL_S:2 * CL_S] = ti
            cr, ci = tr[0:1], ti[0:1]
        return cr, ci

    return lax.fori_loop(0, nc, chunk, (carry[:, :CL_S], carry[:, CL_S:]))


NT_SCAN = L // T_SCAN


def _hilo(t):
    hi = t.astype(BF16)
    return jnp.stack([hi, (t - hi.astype(F32)).astype(BF16)], axis=1)


def _dot3(a, b_ref):
    ah = a.astype(BF16)
    al = (a - ah.astype(F32)).astype(BF16)
    bh, bl = b_ref[0], b_ref[1]
    return (jnp.dot(ah, bh, preferred_element_type=F32) + jnp.dot(al, bh, preferred_element_type=F32)
            + jnp.dot(ah, bl, preferred_element_type=F32))


def _hl_spec(r, c):
    return pl.BlockSpec((None, 2, r, c), lambda c_, t: (c_, 0, 0, 0))


def _ssm_fwd(u, bm2, cm2, pw, dvec):
    def body(u_ref, bm_ref, cm_ref, pw_ref, d_ref, y_ref, bnd_ref, s_ref, carry_ref):
        @pl.when(pl.program_id(1) == 0)
        def _():
            carry_ref[...] = jnp.zeros_like(carry_ref)

        bnd_ref[...] = carry_ref[...]
        uv = u_ref[...]
        s_ref[...] = _dot3(uv, bm_ref)
        cr, ci = _scan_fwd_tiles(s_ref, pw_ref[...], carry_ref[...])
        carry_ref[...] = jnp.concatenate([cr, ci], axis=1)
        y_ref[...] = _dot3(s_ref[...], cm_ref) + d_ref[...] * uv

    return pl.pallas_call(
        body,
        out_shape=(jax.ShapeDtypeStruct((L, SSMW), F32), jax.ShapeDtypeStruct((N_CL, NT_SCAN, 1, 2 * CL_S), F32)),
        grid=(N_CL, NT_SCAN),
        in_specs=[pl.BlockSpec((T_SCAN, CL_U), lambda c, t: (t, c)),
                  _hl_spec(CL_U, 2 * CL_S), _hl_spec(2 * CL_S, CL_U),
                  pl.BlockSpec((None, 8, 2 * CL_S), lambda c, t: (c, 0, 0)),
                  pl.BlockSpec((1, CL_U), lambda c, t: (0, c))],
        out_specs=(pl.BlockSpec((T_SCAN, CL_U), lambda c, t: (t, c)),
                   pl.BlockSpec((None, None, 1, 2 * CL_S), lambda c, t: (c, t, 0, 0))),
        scratch_shapes=[pltpu.VMEM((T_SCAN, 2 * CL_S), F32), pltpu.VMEM((1, 2 * CL_S), F32)],
        name='ssm_fwd', compiler_params=_params(('arbitrary', 'arbitrary'), 40))(u, bm2, cm2, pw, dvec)


def _ssm_bwd(u, dy, bm2, bmt, cmt, pw, dvec, bnd):
    rev = lambda t: NT_SCAN - 1 - t

    def body(u_ref, dy_ref, bm_ref, bmt_ref, cmt_ref, pw_ref, d_ref, bnd_ref,
             du_ref, dbm_ref, dcm_ref, da_ref, dd_ref, s_ref, l_ref, carry_ref):
        @pl.when(pl.program_id(1) == 0)
        def _():
            carry_ref[...] = jnp.zeros_like(carry_ref)
            dbm_ref[...] = jnp.zeros_like(dbm_ref)
            dcm_ref[...] = jnp.zeros_like(dcm_ref)
            da_ref[...] = jnp.zeros_like(da_ref)
            dd_ref[...] = jnp.zeros_like(dd_ref)

        uv, dyv, pw = u_ref[...], dy_ref[...], pw_ref[...]
        dy_b = dyv.astype(BF16)
        entry = bnd_ref[...]
        s_ref[...] = _dot3(uv, bm_ref)
        _scan_fwd_tiles(s_ref, pw, entry)
        l_ref[...] = jnp.dot(dy_b, cmt_ref[...], preferred_element_type=F32)
        cr, ci = _scan_bwd_tiles(l_ref, pw, carry_ref[...])
        carry_ref[...] = jnp.concatenate([cr, ci], axis=1)
        sv, lv = s_ref[...], l_ref[...]
        lv_b = lv.astype(BF16)
        du_ref[...] = dyv * d_ref[...] + jnp.dot(lv_b, bmt_ref[...], preferred_element_type=F32)
        dbm_ref[...] += lax.dot_general(uv.astype(BF16), lv_b, TN_DIMS, preferred_element_type=F32)
        dcm_ref[...] += lax.dot_general(sv.astype(BF16), dy_b, TN_DIMS, preferred_element_type=F32)
        dd_ref[...] += _colsum(dyv * uv)
        row = lax.broadcasted_iota(jnp.int32, (T_SCAN, 2 * CL_S), 0)
        sp = jnp.where(row == 0, entry, pltpu.roll(sv, 1, 0))
        spr, spi = sp[:, :CL_S], sp[:, CL_S:]
        lr, li = lv[:, :CL_S], lv[:, CL_S:]
        da_ref[:, 0:CL_S] += _colsum(lr * spr + li * spi)
        da_ref[:, CL_S:2 * CL_S] += _colsum(li * spr - lr * spi)
        _fold8(pl.program_id(1) == NT_SCAN - 1, da_ref, dd_ref)

    return pl.pallas_call(
        body,
        out_shape=(jax.ShapeDtypeStruct((L, SSMW), F32), jax.ShapeDtypeStruct((N_CL, CL_U, 2 * CL_S), F32),
                   jax.ShapeDtypeStruct((N_CL, 2 * CL_S, CL_U), F32), jax.ShapeDtypeStruct((N_CL, 8, 2 * CL_S), F32),
                   jax.ShapeDtypeStruct((8, SSMW), F32)),
        grid=(N_CL, NT_SCAN),
        in_specs=[pl.BlockSpec((T_SCAN, CL_U), lambda c, t: (rev(t), c)),
                  pl.BlockSpec((T_SCAN, CL_U), lambda c, t: (rev(t), c)),
                  _hl_spec(CL_U, 2 * CL_S),
                  pl.BlockSpec((None, 2 * CL_S, CL_U), lambda c, t: (c, 0, 0)),
                  pl.BlockSpec((None, CL_U, 2 * CL_S), lambda c, t: (c, 0, 0)),
                  pl.BlockSpec((None, 8, 2 * CL_S), lambda c, t: (c, 0, 0)),
                  pl.BlockSpec((1, CL_U), lambda c, t: (0, c)),
                  pl.BlockSpec((None, None, 1, 2 * CL_S), lambda c, t: (c, rev(t), 0, 0))],
        out_specs=(pl.BlockSpec((T_SCAN, CL_U), lambda c, t: (rev(t), c)),
                   pl.BlockSpec((None, CL_U, 2 * CL_S), lambda c, t: (c, 0, 0)),
                   pl.BlockSpec((None, 2 * CL_S, CL_U), lambda c, t: (c, 0, 0)),
                   pl.BlockSpec((None, 8, 2 * CL_S), lambda c, t: (c, 0, 0)),
                   pl.BlockSpec((8, CL_U), lambda c, t: (0, c))),
        scratch_shapes=[pltpu.VMEM((T_SCAN, 2 * CL_S), F32), pltpu.VMEM((T_SCAN, 2 * CL_S), F32),
                        pltpu.VMEM((1, 2 * CL_S), F32)],
        name='ssm_bwd', compiler_params=_params(('arbitrary', 'arbitrary'), 48))(u, dy, bm2, bmt, cmt, pw, dvec, bnd)


GELU_C = math.sqrt(2.0 / math.pi)
GELU_K = 0.044715


def _gelu_parts(x):
    t = jnp.tanh(GELU_C * (x + GELU_K * (x * x * x)))
    return x * (0.5 * (1.0 + t)), t


def _glu_fwd(ypre, wglu, bglu, gs):
    def body(y_ref, w_ref, b_ref, g_ref, o_ref):
        yg, _ = _gelu_parts(y_ref[...])
        z = jnp.dot(yg.astype(BF16), w_ref[...], preferred_element_type=F32) + b_ref[...]
        s = yg * jax.nn.sigmoid(z)
        o_ref[...] = ((s * _rms(s)) * g_ref[...]).astype(BF16)

    return pl.pallas_call(
        body, out_shape=jax.ShapeDtypeStruct((L, SSMW), BF16), grid=(L // TR,),
        in_specs=[_rowspec(SSMW), pl.BlockSpec((SSMW, SSMW), lambda i: (0, 0)), _vec(SSMW), _vec(SSMW)],
        out_specs=_rowspec(SSMW), name='glu_fwd', compiler_params=_params(('parallel',), 32))(ypre, wglu, bglu, gs)


def _glu_bwd(ypre, dsn, wglu, bglu, gs):
    def body(y_ref, d_ref, w_ref, b_ref, g_ref, dy_ref, dw_ref, db_ref, dg_ref):
        @pl.when(pl.program_id(0) == 0)
        def _():
            dw_ref[...] = jnp.zeros_like(dw_ref)
            db_ref[...] = jnp.zeros_like(db_ref)
            dg_ref[...] = jnp.zeros_like(dg_ref)

        xv = y_ref[...]
        yg, t = _gelu_parts(xv)
        yg_b = yg.astype(BF16)
        z = jnp.dot(yg_b, w_ref[...], preferred_element_type=F32) + b_ref[...]
        sg = jax.nn.sigmoid(z)
        s = yg * sg
        r = _rms(s)
        n = s * r
        dv = d_ref[...]
        dg_ref[...] += _colsum(dv * n)
        ds = _rms_bwd(dv * g_ref[...], n, r)
        dz = (ds * yg) * (sg * (1.0 - sg))
        dz_b = dz.astype(BF16)
        db_ref[...] += _colsum(dz)
        dw_ref[...] += lax.dot_general(yg_b, dz_b, TN_DIMS, preferred_element_type=F32)
        dyg = ds * sg + lax.dot_general(dz_b, w_ref[...], NT_DIMS, preferred_element_type=F32)
        dgelu = 0.5 * (1.0 + t) + (0.5 * xv) * (1.0 - t * t) * (GELU_C * (1.0 + 3.0 * GELU_K * (xv * xv)))
        dy_ref[...] = dyg * dgelu
        _fold8(pl.program_id(0) == L // TR - 1, db_ref, dg_ref)

    vs = jax.ShapeDtypeStruct((8, SSMW), F32)
    return pl.pallas_call(
        body, out_shape=(jax.ShapeDtypeStruct((L, SSMW), F32), jax.ShapeDtypeStruct((SSMW, SSMW), F32), vs, vs),
        grid=(L // TR,),
        in_specs=[_rowspec(SSMW), _rowspec(SSMW), pl.BlockSpec((SSMW, SSMW), lambda i: (0, 0)), _vec(SSMW), _vec(SSMW)],
        out_specs=(_rowspec(SSMW), pl.BlockSpec((SSMW, SSMW), lambda i: (0, 0)), _vec8(SSMW), _vec8(SSMW)),
        name='glu_bwd', compiler_params=_params(('arbitrary',), 40))(ypre, dsn, wglu, bglu, gs)


def _me():
    return lax.axis_index('x'), lax.axis_index('y'), lax.axis_index('c')


def _peer(k):
    x, y, c = _me()
    px = 1 - x if k & 4 else x
    py = 1 - y if k & 2 else y
    pc = 1 - c if k & 1 else c
    return (px, py, pc), 4 * px + 2 * py + pc


def _mod_exchange(c_row, w_ada, b_ada8):
    cw = NMOD * D // N_DEV

    def body(c_ref, w_ref, b_ref, call_ref, mod_ref, part_ref, send_sems, recv_sems):
        x, y, c = _me()
        me = 4 * x + 2 * y + c
        call_ref[me] = c_ref[0]
        sends = []
        for k in range(1, N_DEV):
            peer, _ = _peer(k)
            cp = pltpu.make_async_remote_copy(src_ref=c_ref.at[0], dst_ref=call_ref.at[me], send_sem=send_sems.at[0, k - 1],
                                              recv_sem=recv_sems.at[0, k - 1], device_id=peer, device_id_type=MESH)
            cp.start()
            sends.append(cp)
        for k in range(1, N_DEV):
            peer, pidx = _peer(k)
            pltpu.make_async_remote_copy(src_ref=c_ref.at[0], dst_ref=call_ref.at[pidx], send_sem=send_sems.at[0, k - 1],
                                         recv_sem=recv_sems.at[0, k - 1], device_id=peer, device_id_type=MESH).wait_recv()
        for cp in sends:
            cp.wait_send()
        cv = call_ref[...].reshape(N_DEV, D)
        part = jnp.dot(cv * jax.nn.sigmoid(cv), w_ref[...], preferred_element_type=F32, precision=HI)
        part_ref[...] = part.reshape(N_DEV, 1, cw)
        mod_ref[me] = part_ref[me]
        sends = []
        for k in range(1, N_DEV):
            peer, pidx = _peer(k)
            cp = pltpu.make_async_remote_copy(src_ref=part_ref.at[pidx], dst_ref=mod_ref.at[me], send_sem=send_sems.at[1, k - 1],
                                              recv_sem=recv_sems.at[1, k - 1], device_id=peer, device_id_type=MESH)
            cp.start()
            sends.append(cp)
        for k in range(1, N_DEV):
            peer, pidx = _peer(k)
            pltpu.make_async_remote_copy(src_ref=part_ref.at[pidx], dst_ref=mod_ref.at[pidx], send_sem=send_sems.at[1, k - 1],
                                         recv_sem=recv_sems.at[1, k - 1], device_id=peer, device_id_type=MESH).wait_recv()
        for cp in sends:
            cp.wait_send()
        mod_ref[...] = mod_ref[...] + b_ref[...]

    vm = pl.BlockSpec(memory_space=pltpu.VMEM)
    return pl.pallas_call(
        body, out_shape=(jax.ShapeDtypeStruct((N_DEV, 1, D), F32), jax.ShapeDtypeStruct((N_DEV, 1, cw), F32)),
        in_specs=[vm, vm, vm], out_specs=(vm, vm),
        scratch_shapes=[pltpu.VMEM((N_DEV, 1, cw), F32), pltpu.SemaphoreType.DMA((2, N_DEV - 1)),
                        pltpu.SemaphoreType.DMA((2, N_DEV - 1))],
        name='mod_exchange', compiler_params=_params(None, 48))(c_row, w_ada, b_ada8)


def _exchange(arrs, scatter, name):
    n = len(arrs)
    hbm = pl.BlockSpec(memory_space=pltpu.HBM)

    def body(*refs):
        ins, outs = refs[:n], refs[n:2 * n]
        send_sems, recv_sems, local_sems = refs[2 * n:]
        x, y, c = _me()
        me = 4 * x + 2 * y + c
        locals_, sends = [], []
        for i in range(n):
            src_own = ins[i].at[me] if scatter else ins[i]
            lc = pltpu.make_async_copy(src_own, outs[i].at[me], local_sems.at[i])
            lc.start()
            locals_.append(lc)
            for k in range(1, N_DEV):
                peer, pidx = _peer(k)
                src = ins[i].at[pidx] if scatter else ins[i]
                cp = pltpu.make_async_remote_copy(src_ref=src, dst_ref=outs[i].at[me], send_sem=send_sems.at[i, k - 1],
                                                  recv_sem=recv_sems.at[i, k - 1], device_id=peer, device_id_type=MESH)
                cp.start()
                sends.append(cp)
        for i in range(n):
            for k in range(1, N_DEV):
                peer, pidx = _peer(k)
                src = ins[i].at[pidx] if scatter else ins[i]
                pltpu.make_async_remote_copy(src_ref=src, dst_ref=outs[i].at[pidx], send_sem=send_sems.at[i, k - 1],
                                             recv_sem=recv_sems.at[i, k - 1], device_id=peer, device_id_type=MESH).wait_recv()
        for cp in sends:
            cp.wait_send()
        for lc in locals_:
            lc.wait()

    out_shape = tuple(jax.ShapeDtypeStruct(a.shape if scatter else (N_DEV,) + a.shape, a.dtype) for a in arrs)
    return pl.pallas_call(
        body, out_shape=out_shape, in_specs=[hbm] * n, out_specs=(hbm,) * n,
        scratch_shapes=[pltpu.SemaphoreType.DMA((n, N_DEV - 1)), pltpu.SemaphoreType.DMA((n, N_DEV - 1)),
                        pltpu.SemaphoreType.DMA((n,))],
        name=name)(*arrs)


def _adam(w, g, m, v):
    m2 = B1 * m + (1.0 - B1) * g
    v2 = B2 * v + (1.0 - B2) * jnp.square(g)
    m_hat = m2 / (1.0 - B1 ** STEP)
    v_hat = v2 / (1.0 - B2 ** STEP)
    delta = -LR * (m_hat / (jnp.sqrt(v_hat) + AEPS) + WD * w)
    return delta, m2, v2


def _small_update(gp, wp, mp, vp):
    def body(g_ref, w_ref, m_ref, v_ref, all_ref, go_ref, d_ref, mo_ref, vo_ref, send_sems, recv_sems):
        x, y, c = _me()
        me = 4 * x + 2 * y + c
        all_ref[me] = g_ref[...]
        sends = []
        for k in range(1, N_DEV):
            peer, _ = _peer(k)
            cp = pltpu.make_async_remote_copy(src_ref=g_ref, dst_ref=all_ref.at[me], send_sem=send_sems.at[k - 1],
                                              recv_sem=recv_sems.at[k - 1], device_id=peer, device_id_type=MESH)
            cp.start()
            sends.append(cp)
        for k in range(1, N_DEV):
            peer, pidx = _peer(k)
            pltpu.make_async_remote_copy(src_ref=g_ref, dst_ref=all_ref.at[pidx], send_sem=send_sems.at[k - 1],
                                         recv_sem=recv_sems.at[k - 1], device_id=peer, device_id_type=MESH).wait_recv()
        for cp in sends:
            cp.wait_send()
        g = all_ref[0]
        for d in range(1, N_DEV):
            g = g + all_ref[d]
        delta, m2, v2 = _adam(w_ref[...], g, m_ref[...], v_ref[...])
        go_ref[...] = g
        d_ref[...] = delta
        mo_ref[...] = m2
        vo_ref[...] = v2

    vm = pl.BlockSpec(memory_space=pltpu.VMEM)
    vs = jax.ShapeDtypeStruct((NS // 128, 128), F32)
    return pl.pallas_call(
        body, out_shape=(jax.ShapeDtypeStruct((N_DEV, NS // 128, 128), F32), vs, vs, vs, vs), in_specs=[vm] * 4,
        out_specs=(vm,) * 5,
        scratch_shapes=[pltpu.SemaphoreType.DMA((N_DEV - 1,)), pltpu.SemaphoreType.DMA((N_DEV - 1,))],
        name='small_update', compiler_params=_params(None, 48))(gp, wp, mp, vp)


def _big_update(parts, w, m, v, name):
    _, R, C = parts.shape
    tr = R if R % 256 else (128 if C >= 2048 else 256)

    def body(p_ref, w_ref, m_ref, v_ref, g_ref, d_ref, mo_ref, vo_ref):
        g = p_ref[0].astype(F32)
        for d in range(1, N_DEV):
            g = g + p_ref[d].astype(F32)
        delta, m2, v2 = _adam(w_ref[...], g, m_ref[...], v_ref[...])
        g_ref[...] = g
        d_ref[...] = delta
        mo_ref[...] = m2
        vo_ref[...] = v2

    blk = pl.BlockSpec((tr, C), lambda i: (i, 0))
    sh = jax.ShapeDtypeStruct((R, C), F32)
    return pl.pallas_call(
        body, out_shape=(sh, sh, sh, sh), grid=(R // tr,),
        in_specs=[pl.BlockSpec((N_DEV, tr, C), lambda i: (0, i, 0)), blk, blk, blk], out_specs=(blk,) * 4,
        name=name, compiler_params=_params(('parallel',), 48))(parts, w, m, v)


def _ada_update(c_all, dmod_cols, w, m, v):
    C = w.shape[1]
    tr = 256

    def body(c_ref, dm_ref, w_ref, m_ref, v_ref, g_ref, d_ref, mo_ref, vo_ref):
        cv = c_ref[...]
        s = cv * jax.nn.sigmoid(cv)
        g = lax.dot_general(s, dm_ref[...], TN_DIMS, preferred_element_type=F32, precision=HI)
        delta, m2, v2 = _adam(w_ref[...], g, m_ref[...], v_ref[...])
        g_ref[...] = g
        d_ref[...] = delta
        mo_ref[...] = m2
        vo_ref[...] = v2

    blk = pl.BlockSpec((tr, C), lambda i: (i, 0))
    sh = jax.ShapeDtypeStruct((D, C), F32)
    return pl.pallas_call(
        body, out_shape=(sh, sh, sh, sh), grid=(D // tr,),
        in_specs=[pl.BlockSpec((N_DEV, tr), lambda i: (0, i)), pl.BlockSpec((N_DEV, C), lambda i: (0, 0)), blk, blk, blk],
        out_specs=(blk,) * 4, name='ada_update', compiler_params=_params(('parallel',), 48))(c_all, dmod_cols, w, m, v)


def _to_sub(t, d):
    if d == 1:
        return t
    return t.reshape(L // d, d, t.shape[-1]).transpose(1, 0, 2).reshape(L, t.shape[-1])


def _from_sub(t, d):
    if d == 1:
        return t
    return t.reshape(d, L // d, t.shape[-1]).transpose(1, 0, 2).reshape(L, t.shape[-1])


def _rows_to_cluster_lanes(t):
    k = t.shape[1]
    return t.reshape(N_CL, CL_S, k).transpose(0, 2, 1)


def _blockdiag_in(t):
    t = t.reshape(N_CL, CL_G, SSM_N, SSM_P).transpose(0, 1, 3, 2)
    eye = jnp.eye(CL_G, dtype=t.dtype)
    t = t[:, :, :, None, :] * eye[None, :, None, :, None]
    return t.reshape(N_CL, CL_U, CL_S)


def _blockdiag_extract(t):
    t = t.reshape(N_CL, CL_G, SSM_P, CL_G, SSM_N)
    t = jnp.stack([t[:, i, :, i, :] for i in range(CL_G)], axis=1)
    return t.transpose(0, 1, 3, 2).reshape(SSM_GN, SSM_P)


def _c_to_rows(t):
    return t.transpose(0, 2, 1).reshape(SSM_GN, SSM_P)


def _rows_to_c(t):
    return t.reshape(SSM_G, SSM_N, SSM_P).transpose(0, 2, 1)


def _local_step(x, pos, mod, tgt, w_in, w_glu, w_out, w_mi, w_mo, sp):
    sh1, sc1, gt1, sh2, sc2, gt2 = (mod[i:i + 1] for i in range(NMOD))
    vec = lambda n: sp[n].reshape(1, -1)

    h1 = _prenorm_fwd(x, vec('g_pre_mix'), sc1, sh1)
    proj = _mm(h1, w_in, mode='nn', name='mm_in', tn=1408)
    fr1 = ROPE_THETA ** (-jnp.arange(0, ROT_DIM, 2, dtype=F32) / ROT_DIM)
    lane = jnp.arange(128) % HEAD_DIM
    fr = jnp.where(lane < ROT_DIM, fr1[lane % (ROT_DIM // 2)], 0.0).reshape(1, 128).astype(F32)
    qk = _rope(proj[:, :ROPE_W], pos, fr, 1.0, 'rope_fwd')
    v_b = proj[:, ROPE_W:ROPE_W + KVW].astype(BF16)
    u = proj[:, ROPE_W + KVW:]
    qp = jnp.stack([_to_sub(qk[:, gi * KVW:(gi + 1) * KVW], d) for gi, d in enumerate(DILATIONS)])
    kp = jnp.stack([_to_sub(qk[:, QW:], d) for d in DILATIONS])
    vp = jnp.stack([_to_sub(v_b, d) for d in DILATIONS])
    o_p, lse_p = _attn_fwd(qp, kp, vp)
    o3 = jnp.stack([_from_sub(o_p[gi], d) for gi, d in enumerate(DILATIONS)])
    lse3 = jnp.stack([_from_sub(lse_p[gi], d) for gi, d in enumerate(DILATIONS)])
    att = _combine_fwd(o3, lse3, vec('g_attn_out'))

    rows = lambda n: sp[n].reshape(SSM_GN, 1)
    a_re, a_im = rows('ssm_a_re'), rows('ssm_a_im')
    ldt = jnp.repeat(sp['ssm_log_dt'].reshape(SSM_G, 1), SSM_N, axis=0)
    b_re, b_im = sp['ssm_b_re'].reshape(SSM_GN, SSM_P), sp['ssm_b_im'].reshape(SSM_GN, SSM_P)
    c_re, c_im = _c_to_rows(sp['ssm_c_re'].reshape(SSM_G, SSM_P, SSM_N)), _c_to_rows(sp['ssm_c_im'].reshape(SSM_G, SSM_P, SSM_N))
    bbr, bbi, pwr, pwi = _ssm_pre(a_re, a_im, ldt, b_re, b_im)
    bm = jnp.concatenate([_blockdiag_in(bbr), _blockdiag_in(bbi)], axis=2)
    cmt = jnp.concatenate([_blockdiag_in(c_re), -_blockdiag_in(c_im)], axis=2)
    bmt, cm = bm.transpose(0, 2, 1), cmt.transpose(0, 2, 1)
    pw = jnp.concatenate([_rows_to_cluster_lanes(pwr), _rows_to_cluster_lanes(pwi)], axis=2)
    dvec = vec('ssm_d')
    bm2, cm2 = _hilo(bm), _hilo(cm)
    ypre, bnd = _ssm_fwd(u, bm2, cm2, pw, dvec)
    ssm_n = _glu_fwd(ypre, w_glu, vec('b_glu'), vec('g_ssm_out'))

    cat = jnp.concatenate([att, ssm_n], axis=1)
    mix = _mm(cat, w_out, mode='nn', name='mm_out', tk=640)
    x1, h2 = _postmix_fwd(x, mix, vec('g_post_mix'), gt1, vec('g_pre_mlp'), sc2, sh2)
    a_pre, r_act = _mm(h2, w_mi, mode='nn', name='mm_mlp_in', epilogue='relu2')
    y = _mm(r_act, w_mo, mode='nn', name='mm_mlp_out')
    dx2, dy, loss, dgt2, dg_post_mlp = _final_fwd_bwd(x1, y, tgt, vec('g_post_mlp'), gt2)
    dgt2, dg_post_mlp = dgt2[:1], dg_post_mlp[:1]

    da = _mm(dy, w_mo, mode='nt', name='mm_d_act', out_dtype=BF16, epilogue='drelu2', extra=a_pre)
    g_w_mo = _mm(r_act, dy, mode='tn', name='mm_dw_mlp_out', out_dtype=BF16)
    dh2 = _mm(da, w_mi, mode='nt', name='mm_dh2')
    g_w_mi = _mm(h2, da, mode='tn', name='mm_dw_mlp_in', out_dtype=BF16)
    dx1, dmix, dsc2, dsh2, dg_pre_mlp, dgt1, dg_post_mix = _postmix_bwd(
        dx2, dh2, x1, mix, vec('g_post_mix'), gt1, vec('g_pre_mlp'), sc2)
    dsc2, dsh2, dg_pre_mlp, dgt1, dg_post_mix = (t[:1] for t in (dsc2, dsh2, dg_pre_mlp, dgt1, dg_post_mix))
    dcat = _mm(dmix, w_out, mode='nt', name='mm_dcat', tn=1280)
    g_w_out = _mm(cat, dmix, mode='tn', name='mm_dw_out', out_dtype=BF16, tm=640)
    datt, dsn = dcat[:, :KVW], dcat[:, KVW:]

    dypre, g_w_glu, g_b_glu, g_g_ssm = _glu_bwd(ypre, dsn, w_glu, vec('b_glu'), vec('g_ssm_out'))
    g_b_glu, g_g_ssm = g_b_glu[:1], g_g_ssm[:1]
    du, dbm, dcm, dA, dD = _ssm_bwd(u, dypre, bm2, bmt.astype(BF16), cmt.astype(BF16), pw, dvec, bnd)
    dD = dD[:1]
    gbr, gbi = _blockdiag_extract(dbm[:, :, :CL_S]), _blockdiag_extract(dbm[:, :, CL_S:])
    dcmt = dcm.transpose(0, 2, 1)
    g_c_re = _rows_to_c(_blockdiag_extract(dcmt[:, :, :CL_S]))
    g_c_im = _rows_to_c(-_blockdiag_extract(dcmt[:, :, CL_S:]))
    gar = dA[:, 0, :CL_S].reshape(SSM_GN, 1)
    gai = dA[:, 0, CL_S:].reshape(SSM_GN, 1)
    sel = (jnp.arange(SSM_GN)[None, :] // SSM_N == jnp.arange(SSM_G)[:, None]).astype(F32)
    g_a_re, g_a_im, g_b_re, g_b_im, g_ldt = _ssm_post(a_re, a_im, ldt, b_re, b_im, gar, gai, gbr, gbi, sel)

    head_ones = (jnp.arange(KVW)[:, None] // HEAD_DIM == jnp.arange(KVW)[None, :] // HEAD_DIM).astype(F32)
    do3, dlse3, g_g_attn = _combine_bwd(datt, o3, lse3, vec('g_attn_out'), head_ones)
    g_g_attn = g_g_attn[:1]
    do_p = jnp.stack([_to_sub(do3[gi], d) for gi, d in enumerate(DILATIONS)])
    dlse_p = jnp.stack([_to_sub(dlse3[gi], d) for gi, d in enumerate(DILATIONS)])
    dq_p, dk_p, dv_p = _attn_bwd(qp, kp, vp, o_p, lse_p, do_p, dlse_p)
    dq = jnp.concatenate([_from_sub(dq_p[gi], d) for gi, d in enumerate(DILATIONS)], axis=1)
    dk = sum(_from_sub(dk_p[gi], d) for gi, d in enumerate(DILATIONS))
    dv = sum(_from_sub(dv_p[gi], d) for gi, d in enumerate(DILATIONS))
    dqk = _rope(jnp.concatenate([dq, dk], axis=1), pos, fr, -1.0, 'rope_bwd')
    dproj = jnp.concatenate([dqk, dv.astype(BF16), du.astype(BF16)], axis=1)
    dh1 = _mm(dproj, w_in, mode='nt', name='mm_dh1', tk=1408)
    g_w_in = _mm(h1, dproj, mode='tn', name='mm_dw_in', out_dtype=BF16, tn=1408)
    grad_x, dsc1, dsh1, dg_pre_mix = _prenorm_bwd(dx1, dh1, x, vec('g_pre_mix'), sc1)
    dsc1, dsh1, dg_pre_mix = dsc1[:1], dsh1[:1], dg_pre_mix[:1]

    dmod = jnp.concatenate([dsh1, dsc1, dgt1, dsh2, dsc2, dgt2], axis=0)
    big = {'w_in': g_w_in, 'w_glu': g_w_glu.astype(BF16), 'w_out': g_w_out, 'w_mlp_in': g_w_mi, 'w_mlp_out': g_w_mo}
    small = {'b_ada': dmod, 'g_pre_mix': dg_pre_mix, 'g_post_mix': dg_post_mix, 'ssm_a_re': g_a_re, 'ssm_a_im': g_a_im,
             'ssm_log_dt': g_ldt[:, 0], 'ssm_b_re': g_b_re, 'ssm_b_im': g_b_im, 'ssm_c_re': g_c_re, 'ssm_c_im': g_c_im,
             'ssm_d': dD, 'b_glu': g_b_glu, 'g_attn_out': g_g_attn, 'g_ssm_out': g_g_ssm, 'g_pre_mlp': dg_pre_mlp,
             'g_post_mlp': dg_post_mlp}
    return loss[0, 0], grad_x, big, small, dmod


def _pack(d):
    return jnp.concatenate([jnp.pad(d[n].reshape(-1).astype(F32), (0, SEG[n] - SMALL_SIZES[n])) for n in SMALL])


def _shard_major(t, name):
    if name in ('w_in', 'w_out', 'w_mlp_in'):
        k, n = t.shape
        return t.reshape(k, N_DEV, n // N_DEV).transpose(1, 0, 2)
    k, n = t.shape
    return t.reshape(N_DEV, k // N_DEV, n)


def _from_shard_major(t, name):
    if name in ('w_in', 'w_out', 'w_mlp_in'):
        _, k, n = t.shape
        return t.transpose(1, 0, 2).reshape(k, N_DEV * n)
    _, k, n = t.shape
    return t.reshape(N_DEV * k, n)


def kernel(x, c, positions, w_ada, b_ada, g_pre_mix, g_post_mix, w_in, ssm_a_re, ssm_a_im, ssm_log_dt, ssm_b_re, ssm_b_im, ssm_c_re, ssm_c_im, ssm_d, w_glu, b_glu, g_attn_out, g_ssm_out, w_out, g_pre_mlp, g_post_mlp, w_mlp_in, w_mlp_out, loss_target, m_w_ada, m_b_ada, m_g_pre_mix, m_g_post_mix, m_w_in, m_ssm_a_re, m_ssm_a_im, m_ssm_log_dt, m_ssm_b_re, m_ssm_b_im, m_ssm_c_re, m_ssm_c_im, m_ssm_d, m_w_glu, m_b_glu, m_g_attn_out, m_g_ssm_out, m_w_out, m_g_pre_mlp, m_g_post_mlp, m_w_mlp_in, m_w_mlp_out, v_w_ada, v_b_ada, v_g_pre_mix, v_g_post_mix, v_w_in, v_ssm_a_re, v_ssm_a_im, v_ssm_log_dt, v_ssm_b_re, v_ssm_b_im, v_ssm_c_re, v_ssm_c_im, v_ssm_d, v_w_glu, v_b_glu, v_g_attn_out, v_g_ssm_out, v_w_out, v_g_pre_mlp, v_g_post_mlp, v_w_mlp_in, v_w_mlp_out):
    loc = dict(locals())
    W = {n: loc[n] for n in WEIGHTS}
    M = {n: loc['m_' + n] for n in WEIGHTS}
    V = {n: loc['v_' + n] for n in WEIGHTS}
    assert x.shape == (1, L, D) and w_in.shape == (1, D, INW // N_DEV), (x.shape, w_in.shape)

    cw = NMOD * D // N_DEV
    c_all, mod8 = _mod_exchange(c.reshape(1, 1, D), w_ada[0], b_ada.reshape(N_DEV, 1, cw))
    mod = mod8.reshape(NMOD, D)

    gathered = _exchange([W[n][0].astype(BF16) for n in BIG], False, 'weight_gather')
    full = {n: _from_shard_major(g, n) for n, g in zip(BIG, gathered)}

    sp = {n: W[n][0] for n in SMALL}
    loss, grad_x, big, small, _ = _local_step(
        x[0], positions.reshape(L, 1), mod, loss_target[0], full['w_in'], full['w_glu'], full['w_out'],
        full['w_mlp_in'], full['w_mlp_out'], sp)
    loss = lax.psum(loss, ('x', 'y', 'c'))

    parts = _exchange([_shard_major(big[n], n) for n in BIG], True, 'grad_scatter')
    out_g, out_d, out_m, out_v = {}, {}, {}, {}
    for n, p in zip(BIG, parts):
        out_g[n], out_d[n], out_m[n], out_v[n] = _big_update(p, W[n][0], M[n][0], V[n][0], 'update_' + n)

    rows_all, sg, sd, sm, sv = _small_update(
        *[_pack(d).reshape(NS // 128, 128) for d in (small, {n: W[n] for n in SMALL}, {n: M[n] for n in SMALL},
                                                    {n: V[n] for n in SMALL})])
    off = 0
    for n in SMALL:
        sz = SMALL_SIZES[n]
        for dst, src in ((out_g, sg), (out_d, sd), (out_m, sm), (out_v, sv)):
            dst[n] = src[off // 128:(off + SEG[n]) // 128].reshape(-1)[:sz].reshape(W[n].shape[1:])
        off += SEG[n]

    me = 4 * lax.axis_index('x') + 2 * lax.axis_index('y') + lax.axis_index('c')
    dmod_all = rows_all[:, :NMOD * D // 128].reshape(N_DEV, NMOD * D)
    dmod_cols = lax.dynamic_slice_in_dim(dmod_all, me * cw, cw, axis=1)
    out_g['w_ada'], out_d['w_ada'], out_m['w_ada'], out_v['w_ada'] = _ada_update(
        c_all.reshape(N_DEV, D), dmod_cols, w_ada[0], m_w_ada[0], v_w_ada[0])

    lead = lambda t: t[None]
    return (loss, grad_x[None], *[lead(out_g[n]) for n in WEIGHTS], *[lead(out_d[n]) for n in WEIGHTS],
            *[lead(out_m[n]) for n in WEIGHTS], *[lead(out_v[n]) for n in WEIGHTS])
```

```python
import functools
import math

import jax
import jax.numpy as jnp
from jax import lax
from jax.experimental import pallas as pl
from jax.experimental.pallas import tpu as pltpu

F32 = jnp.float32
BF16 = jnp.bfloat16
HI = lax.Precision.HIGHEST
MESH = pl.DeviceIdType.MESH

N_DEV = 8
L = 4096
D = 2048
HEAD_DIM = 64
N_GROUPS = 3
DILATIONS = (1, 4, 16)
HEADS = 6
QW = N_GROUPS * HEADS * HEAD_DIM
KVW = HEADS * HEAD_DIM
ROT_DIM = 16
ROPE_THETA = 500000.0
BLK = 128
NBLK = L // BLK
SSMW = D - QW
SSM_P = 16
SSM_G = SSMW // SSM_P
SSM_N = 64
SSM_GN = SSM_G * SSM_N
CL_G = 8
N_CL = SSM_G // CL_G
CL_U = CL_G * SSM_P
CL_S = CL_G * SSM_N
INW = QW + 2 * KVW + SSMW
OUTW = KVW + SSMW
DFF = 4 * D
NMOD = 6
EPS = 1e-6
LR, B1, B2, AEPS, WD, STEP = 0.001, 0.9, 0.999, 1e-08, 0.01, 10

T_SCAN = 512
MB = 2 ** 20

WEIGHTS = ['w_ada', 'b_ada', 'g_pre_mix', 'g_post_mix', 'w_in', 'ssm_a_re', 'ssm_a_im', 'ssm_log_dt',
           'ssm_b_re', 'ssm_b_im', 'ssm_c_re', 'ssm_c_im', 'ssm_d', 'w_glu', 'b_glu', 'g_attn_out',
           'g_ssm_out', 'w_out', 'g_pre_mlp', 'g_post_mlp', 'w_mlp_in', 'w_mlp_out']
BIG = ['w_in', 'w_glu', 'w_out', 'w_mlp_in', 'w_mlp_out']
SMALL = [n for n in WEIGHTS if n not in BIG and n != 'w_ada']
SMALL_SIZES = {'b_ada': NMOD * D, 'g_pre_mix': D, 'g_post_mix': D, 'ssm_a_re': SSM_GN, 'ssm_a_im': SSM_GN,
               'ssm_log_dt': SSM_G, 'ssm_b_re': SSM_GN * SSM_P, 'ssm_b_im': SSM_GN * SSM_P,
               'ssm_c_re': SSM_GN * SSM_P, 'ssm_c_im': SSM_GN * SSM_P, 'ssm_d': SSMW, 'b_glu': SSMW,
               'g_attn_out': KVW, 'g_ssm_out': SSMW, 'g_pre_mlp': D, 'g_post_mlp': D}
SEG = {n: -(-SMALL_SIZES[n] // 1024) * 1024 for n in SMALL}
NS = sum(SEG.values())


def _params(sem=None, vmem_mb=None):
    kw = {}
    if sem is not None:
        kw['dimension_semantics'] = sem
    if vmem_mb is not None:
        kw['vmem_limit_bytes'] = vmem_mb * MB
    return pltpu.CompilerParams(**kw)


def _vec(n):
    return pl.BlockSpec((1, n), lambda *_: (0, 0))


def _rms(x):
    return lax.rsqrt(jnp.mean(x * x, axis=-1, keepdims=True) + EPS)


def _rms_bwd(dn, n, r):
    return r * (dn - n * jnp.mean(dn * n, axis=-1, keepdims=True))


def _vec8(n):
    return pl.BlockSpec((8, n), lambda *_: (0, 0))


def _colsum(x):
    return jnp.sum(x.reshape(-1, 8, x.shape[-1]), axis=0)


def _fold8(last, *refs):
    @pl.when(last)
    def _():
        for r in refs:
            r[...] = jnp.broadcast_to(jnp.sum(r[...], axis=0, keepdims=True), r.shape)


def _mm(a, b, *, mode, name, out_dtype=F32, tm=1024, tn=1024, tk=512, epilogue=None, extra=None,
        b_sharded=False, out_sharded=False, deps=()):
    if mode == 'nn':
        M, K = a.shape
        dims = (((1,), (0,)), ((), ()))
        a_spec = pl.BlockSpec((tm, tk), lambda i, j, k: (i, k))
        if b_sharded:
            _, K2, per = b.shape
            N, q = N_DEV * per, per // tn
            b_spec = pl.BlockSpec((None, tk, tn), lambda i, j, k: (j // q, k, j % q))
        else:
            K2, N = b.shape
            b_spec = pl.BlockSpec((tk, tn), lambda i, j, k: (k, j))
    elif mode == 'nt':
        M, K = a.shape
        dims = (((1,), (1,)), ((), ()))
        a_spec = pl.BlockSpec((tm, tk), lambda i, j, k: (i, k))
        if b_sharded:
            _, N, per = b.shape
            K2, q = N_DEV * per, per // tk
            b_spec = pl.BlockSpec((None, tn, tk), lambda i, j, k: (k // q, j, k % q))
        else:
            N, K2 = b.shape
            b_spec = pl.BlockSpec((tn, tk), lambda i, j, k: (j, k))
    else:
        (K, M), (K2, N) = a.shape, b.shape
        dims = (((0,), (0,)), ((), ()))
        a_spec = pl.BlockSpec((tk, tm), lambda i, j, k: (k, i))
        b_spec = pl.BlockSpec((tk, tn), lambda i, j, k: (k, j))
    assert K == K2 and M % tm == 0 and N % tn == 0 and K % tk == 0, (name, a.shape, b.shape, tm, tn, tk)
    nk = K // tk
    o_spec = pl.BlockSpec((tm, tn), lambda i, j, k: (i, j))
    o_dims = (M, N)
    if out_sharded:
        qo = N // N_DEV // tn
        o_spec = pl.BlockSpec((None, tm, tn), lambda i, j, k: (j // qo, i, j % qo))
        o_dims = (N_DEV, M, N // N_DEV)
    n_out = 2 if epilogue == 'relu2' else 1
    n_extra = 1 if extra is not None else 0
    n_in = 2 + n_extra + len(deps)

    def body(*refs):
        a_ref, b_ref = refs[0], refs[1]
        x_refs = refs[2:2 + n_extra]
        o_refs = refs[n_in:n_in + n_out]
        acc = refs[-1]
        k = pl.program_id(2)

        @pl.when(k == 0)
        def _():
            acc[...] = jnp.zeros_like(acc)

        acc[...] += lax.dot_general(a_ref[...], b_ref[...], dims, preferred_element_type=F32)

        @pl.when(k == nk - 1)
        def _():
            r = acc[...]
            if epilogue == 'relu2':
                o_refs[0][...] = r.astype(BF16)
                o_refs[1][...] = jnp.square(jnp.maximum(r, 0.0)).astype(BF16)
            elif epilogue == 'drelu2':
                pre = x_refs[0][...].astype(F32)
                o_refs[0][...] = (r * (2.0 * jnp.maximum(pre, 0.0))).astype(out_dtype)
            else:
                o_refs[0][...] = r.astype(out_dtype)

    if epilogue == 'relu2':
        out_shape = (jax.ShapeDtypeStruct((M, N), BF16), jax.ShapeDtypeStruct((M, N), BF16))
        out_specs = (o_spec, o_spec)
    else:
        out_shape = jax.ShapeDtypeStruct(o_dims, out_dtype)
        out_specs = o_spec
    args = (a, b) + ((extra,) if extra is not None else ()) + tuple(deps)
    in_specs = ([a_spec, b_spec] + ([o_spec] if extra is not None else [])
                + [pl.BlockSpec(memory_space=pl.ANY)] * len(deps))
    return pl.pallas_call(
        body, out_shape=out_shape, grid=(M // tm, N // tn, nk), in_specs=in_specs, out_specs=out_specs,
        scratch_shapes=[pltpu.VMEM((tm, tn), F32)], name=name,
        compiler_params=_params(('parallel', 'parallel', 'arbitrary'), 48))(*args)


TR = 256


def _rowspec(w=D):
    return pl.BlockSpec((TR, w), lambda i: (i, 0))


def _prenorm_fwd(x, g, sc, sh):
    def body(x_ref, g_ref, sc_ref, sh_ref, h_ref):
        xv = x_ref[...]
        n = xv * _rms(xv)
        h_ref[...] = ((n * g_ref[...]) * (1.0 + sc_ref[...]) + sh_ref[...]).astype(BF16)

    return pl.pallas_call(
        body, out_shape=jax.ShapeDtypeStruct((L, D), BF16), grid=(L // TR,),
        in_specs=[_rowspec(), _vec(D), _vec(D), _vec(D)], out_specs=_rowspec(), name='prenorm_fwd',
        compiler_params=_params(('parallel',), 40))(x, g, sc, sh)


def _postmix_fwd(x, mix, gpm, gt1, gpl, sc2, sh2):
    def body(x_ref, mix_ref, gpm_ref, gt1_ref, gpl_ref, sc2_ref, sh2_ref, x1_ref, h2_ref):
        mix_v = mix_ref[...]
        nm = mix_v * _rms(mix_v)
        x1 = x_ref[...] + gt1_ref[...] * (nm * gpm_ref[...])
        x1_ref[...] = x1
        n2 = x1 * _rms(x1)
        h2_ref[...] = ((n2 * gpl_ref[...]) * (1.0 + sc2_ref[...]) + sh2_ref[...]).astype(BF16)

    return pl.pallas_call(
        body, out_shape=(jax.ShapeDtypeStruct((L, D), F32), jax.ShapeDtypeStruct((L, D), BF16)), grid=(L // TR,),
        in_specs=[_rowspec(), _rowspec()] + [_vec(D)] * 5, out_specs=(_rowspec(), _rowspec()), name='postmix_fwd',
        compiler_params=_params(('parallel',), 40))(x, mix, gpm, gt1, gpl, sc2, sh2)


def _final_fwd_bwd(x1, y, tgt, g, gt2):
    def body(x1_ref, y_ref, t_ref, g_ref, gt2_ref, dx2_ref, dy_ref, loss_ref, dgt2_ref, dg_ref):
        @pl.when(pl.program_id(0) == 0)
        def _():
            loss_ref[...] = jnp.zeros_like(loss_ref)
            dgt2_ref[...] = jnp.zeros_like(dgt2_ref)
            dg_ref[...] = jnp.zeros_like(dg_ref)

        yv = y_ref[...]
        r = _rms(yv)
        n = yv * r
        ng = n * g_ref[...]
        x2 = x1_ref[...] + gt2_ref[...] * ng
        e = x2 - t_ref[...]
        loss_ref[...] += 0.5 * jnp.sum(jnp.mean(e * e, axis=-1, keepdims=True), axis=0, keepdims=True)
        dx2 = e * (1.0 / D)
        dx2_ref[...] = dx2
        dgt2_ref[...] += _colsum(dx2 * ng)
        dng = dx2 * gt2_ref[...]
        dg_ref[...] += _colsum(dng * n)
        dy_ref[...] = _rms_bwd(dng * g_ref[...], n, r).astype(BF16)
        _fold8(pl.program_id(0) == L // TR - 1, dgt2_ref, dg_ref)

    return pl.pallas_call(
        body,
        out_shape=(jax.ShapeDtypeStruct((L, D), F32), jax.ShapeDtypeStruct((L, D), BF16),
                   jax.ShapeDtypeStruct((8, 128), F32), jax.ShapeDtypeStruct((8, D), F32),
                   jax.ShapeDtypeStruct((8, D), F32)),
        grid=(L // TR,), in_specs=[_rowspec(), _rowspec(), _rowspec(), _vec(D), _vec(D)],
        out_specs=(_rowspec(), _rowspec(), _vec8(128), _vec8(D), _vec8(D)), name='final_fwd_bwd',
        compiler_params=_params(('arbitrary',), 40))(x1, y, tgt, g, gt2)


def _postmix_bwd(dx2, dh2, x1, mix, gpm, gt1, gpl, sc2):
    def body(dx2_ref, dh2_ref, x1_ref, mix_ref, gpm_ref, gt1_ref, gpl_ref, sc2_ref,
             dx1_ref, dmix_ref, dsc2_ref, dsh2_ref, dgpl_ref, dgt1_ref, dgpm_ref):
        @pl.when(pl.program_id(0) == 0)
        def _():
            for r_ in (dsc2_ref, dsh2_ref, dgpl_ref, dgt1_ref, dgpm_ref):
                r_[...] = jnp.zeros_like(r_)

        x1v = x1_ref[...]
        r2 = _rms(x1v)
        n2 = x1v * r2
        dh2v = dh2_ref[...]
        dsh2_ref[...] += _colsum(dh2v)
        dsc2_ref[...] += _colsum(dh2v * (n2 * gpl_ref[...]))
        t = dh2v * (1.0 + sc2_ref[...])
        dgpl_ref[...] += _colsum(t * n2)
        dx1 = dx2_ref[...] + _rms_bwd(t * gpl_ref[...], n2, r2)
        dx1_ref[...] = dx1
        mix_v = mix_ref[...]
        rm = _rms(mix_v)
        nm = mix_v * rm
        dgt1_ref[...] += _colsum(dx1 * (nm * gpm_ref[...]))
        u = dx1 * gt1_ref[...]
        dgpm_ref[...] += _colsum(u * nm)
        dmix_ref[...] = _rms_bwd(u * gpm_ref[...], nm, rm).astype(BF16)
        _fold8(pl.program_id(0) == L // TR - 1, dsc2_ref, dsh2_ref, dgpl_ref, dgt1_ref, dgpm_ref)

    vs = jax.ShapeDtypeStruct((8, D), F32)
    return pl.pallas_call(
        body, out_shape=(jax.ShapeDtypeStruct((L, D), F32), jax.ShapeDtypeStruct((L, D), BF16), vs, vs, vs, vs, vs),
        grid=(L // TR,), in_specs=[_rowspec()] * 4 + [_vec(D)] * 4,
        out_specs=(_rowspec(), _rowspec()) + (_vec8(D),) * 5, name='postmix_bwd',
        compiler_params=_params(('arbitrary',), 48))(dx2, dh2, x1, mix, gpm, gt1, gpl, sc2)


def _prenorm_bwd(dx1, dh1, x, g, sc1):
    def body(dx1_ref, dh1_ref, x_ref, g_ref, sc1_ref, dx_ref, dsc1_ref, dsh1_ref, dg_ref):
        @pl.when(pl.program_id(0) == 0)
        def _():
            for r_ in (dsc1_ref, dsh1_ref, dg_ref):
                r_[...] = jnp.zeros_like(r_)

        xv = x_ref[...]
        r = _rms(xv)
        n = xv * r
        dh = dh1_ref[...]
        dsh1_ref[...] += _colsum(dh)
        dsc1_ref[...] += _colsum(dh * (n * g_ref[...]))
        t = dh * (1.0 + sc1_ref[...])
        dg_ref[...] += _colsum(t * n)
        dx_ref[...] = dx1_ref[...] + _rms_bwd(t * g_ref[...], n, r)
        _fold8(pl.program_id(0) == L // TR - 1, dsc1_ref, dsh1_ref, dg_ref)

    vs = jax.ShapeDtypeStruct((8, D), F32)
    return pl.pallas_call(
        body, out_shape=(jax.ShapeDtypeStruct((L, D), F32), vs, vs, vs), grid=(L // TR,),
        in_specs=[_rowspec()] * 3 + [_vec(D)] * 2, out_specs=(_rowspec(),) + (_vec8(D),) * 3, name='prenorm_bwd',
        compiler_params=_params(('arbitrary',), 40))(dx1, dh1, x, g, sc1)


ROPE_W = QW + KVW


def _rope(xin, pos, fr, sign, name):
    def body(x_ref, pos_ref, fr_ref, o_ref):
        xv = x_ref[...]
        ang = pos_ref[...].astype(F32) * fr_ref[...]
        w = lax.broadcasted_iota(jnp.int32, (1, 128), 1) % HEAD_DIM
        cs = jnp.cos(ang)
        sn = jnp.sin(ang) * sign
        s1 = jnp.where(w < ROT_DIM // 2, -sn, 0.0)
        s2 = jnp.where((w >= ROT_DIM // 2) & (w < ROT_DIM), sn, 0.0)
        rep = ROPE_W // 128
        cs, s1, s2 = jnp.tile(cs, (1, rep)), jnp.tile(s1, (1, rep)), jnp.tile(s2, (1, rep))
        hi = pltpu.roll(xv, ROPE_W - ROT_DIM // 2, 1)
        lo = pltpu.roll(xv, ROT_DIM // 2, 1)
        o_ref[...] = (xv * cs + hi * s1 + lo * s2).astype(BF16)

    return pl.pallas_call(
        body, out_shape=jax.ShapeDtypeStruct((L, ROPE_W), BF16), grid=(L // TR,),
        in_specs=[_rowspec(ROPE_W), pl.BlockSpec((TR, 1), lambda i: (i, 0)), _vec(128)],
        out_specs=_rowspec(ROPE_W), name=name, compiler_params=_params(('parallel',), 40))(xin, pos, fr)


def _attn_mask(g, b):
    nbs = lax.shift_right_logical(jnp.int32(NBLK), 2 * g)
    first = (b & (nbs - 1)) == 0
    qi = lax.broadcasted_iota(jnp.int32, (BLK, 2 * BLK), 0)
    kj = lax.broadcasted_iota(jnp.int32, (BLK, 2 * BLK), 1)
    dist = qi + BLK - kj
    return (dist >= 0) & (dist <= BLK) & ((kj >= BLK) | jnp.logical_not(first))


def _blk(idx):
    return pl.BlockSpec((None, BLK, KVW), idx)


_CUR = lambda g, b: (g, b, 0)
_PREV = lambda g, b: (g, jnp.maximum(b - 1, 0), 0)
NEG = -1e30
NT_DIMS = (((1,), (1,)), ((), ()))
TN_DIMS = (((0,), (0,)), ((), ()))


def _attn_fwd(qp, kp, vp):
    def body(q_ref, kp_ref, kc_ref, vp_ref, vc_ref, o_ref, lse_ref):
        valid = _attn_mask(pl.program_id(0), pl.program_id(1))
        for h in range(HEADS):
            hs = slice(h * HEAD_DIM, (h + 1) * HEAD_DIM)
            q = q_ref[:, hs]
            kc = jnp.concatenate([kp_ref[:, hs], kc_ref[:, hs]], axis=0)
            vc = jnp.concatenate([vp_ref[:, hs], vc_ref[:, hs]], axis=0)
            s = lax.dot_general(q, kc, NT_DIMS, preferred_element_type=F32) * 0.125
            s = jnp.where(valid, s, NEG)
            m = jnp.max(s, axis=-1, keepdims=True)
            p = jnp.exp(s - m)
            l = jnp.sum(p, axis=-1, keepdims=True)
            o = jnp.dot(p.astype(BF16), vc, preferred_element_type=F32) / l
            o_ref[:, hs] = o
            lse_ref[:, hs] = jnp.broadcast_to(m + jnp.log(l), (BLK, HEAD_DIM))

    sh = jax.ShapeDtypeStruct((N_GROUPS, L, KVW), F32)
    return pl.pallas_call(
        body, out_shape=(sh, sh), grid=(N_GROUPS, NBLK),
        in_specs=[_blk(_CUR), _blk(_PREV), _blk(_CUR), _blk(_PREV), _blk(_CUR)],
        out_specs=(_blk(_CUR), _blk(_CUR)), name='attn_fwd',
        compiler_params=_params(('parallel', 'parallel'), 32))(qp, kp, kp, vp, vp)


def _attn_bwd(qp, kp, vp, o, lse, do, dlse):
    def body(q_ref, kp_ref, kc_ref, vp_ref, vc_ref, o_ref, lse_ref, do_ref, dlse_ref, dq_ref, dk_ref, dv_ref):
        b = pl.program_id(1)

        @pl.when(b == 0)
        def _():
            dk_ref[...] = jnp.zeros_like(dk_ref)
            dv_ref[...] = jnp.zeros_like(dv_ref)

        valid = _attn_mask(pl.program_id(0), b)
        prev0 = pl.multiple_of(jnp.maximum(b - 1, 0) * BLK, BLK)
        cur0 = pl.multiple_of(b * BLK, BLK)
        for h in range(HEADS):
            hs = slice(h * HEAD_DIM, (h + 1) * HEAD_DIM)
            q = q_ref[:, hs]
            kc = jnp.concatenate([kp_ref[:, hs], kc_ref[:, hs]], axis=0)
            vc = jnp.concatenate([vp_ref[:, hs], vc_ref[:, hs]], axis=0)
            s = lax.dot_general(q, kc, NT_DIMS, preferred_element_type=F32) * 0.125
            s = jnp.where(valid, s, NEG)
            p = jnp.exp(s - lse_ref[:, h * HEAD_DIM:h * HEAD_DIM + 1])
            do_h = do_ref[:, hs]
            delta = jnp.sum(do_h * o_ref[:, hs], axis=-1, keepdims=True)
            do_b = do_h.astype(BF16)
            dp = lax.dot_general(do_b, vc, NT_DIMS, preferred_element_type=F32)
            ds = p * (dp - delta + dlse_ref[:, h * HEAD_DIM:h * HEAD_DIM + 1])
            ds_b = (ds * 0.125).astype(BF16)
            dq_ref[:, hs] = jnp.dot(ds_b, kc, preferred_element_type=F32)
            dkc = lax.dot_general(ds_b, q, TN_DIMS, preferred_element_type=F32)
            dvc = lax.dot_general(p.astype(BF16), do_b, TN_DIMS, preferred_element_type=F32)
            dk_ref[pl.ds(prev0, BLK), hs] += dkc[:BLK]
            dv_ref[pl.ds(prev0, BLK), hs] += dvc[:BLK]
            dk_ref[pl.ds(cur0, BLK), hs] += dkc[BLK:]
            dv_ref[pl.ds(cur0, BLK), hs] += dvc[BLK:]

    sh = jax.ShapeDtypeStruct((N_GROUPS, L, KVW), F32)
    whole = pl.BlockSpec((None, L, KVW), lambda g, b: (g, 0, 0))
    return pl.pallas_call(
        body, out_shape=(sh, sh, sh), grid=(N_GROUPS, NBLK),
        in_specs=[_blk(_CUR), _blk(_PREV), _blk(_CUR), _blk(_PREV), _blk(_CUR)] + [_blk(_CUR)] * 4,
        out_specs=(_blk(_CUR), whole, whole), name='attn_bwd',
        compiler_params=_params(('arbitrary', 'arbitrary'), 48))(qp, kp, kp, vp, vp, o, lse, do, dlse)


TC = 512


def _comb_spec():
    return pl.BlockSpec((N_GROUPS, TC, KVW), lambda i: (0, i, 0))


def _combine_weights(lse_ref):
    l0, l1, l2 = lse_ref[0], lse_ref[1], lse_ref[2]
    m = jnp.maximum(jnp.maximum(l0, l1), l2)
    e0, e1, e2 = jnp.exp(l0 - m), jnp.exp(l1 - m), jnp.exp(l2 - m)
    z = e0 + e1 + e2
    return e0 / z, e1 / z, e2 / z


def _combine_fwd(o3, lse3, g):
    def body(o_ref, lse_ref, g_ref, att_ref):
        w0, w1, w2 = _combine_weights(lse_ref)
        a = w0 * o_ref[0] + w1 * o_ref[1] + w2 * o_ref[2]
        att_ref[...] = ((a * _rms(a)) * g_ref[...]).astype(BF16)

    return pl.pallas_call(
        body, out_shape=jax.ShapeDtypeStruct((L, KVW), BF16), grid=(L // TC,),
        in_specs=[_comb_spec(), _comb_spec(), _vec(KVW)], out_specs=pl.BlockSpec((TC, KVW), lambda i: (i, 0)),
        name='combine_fwd', compiler_params=_params(('parallel',), 40))(o3, lse3, g)


def _combine_bwd(datt, o3, lse3, g, head_ones):
    def body(datt_ref, o_ref, lse_ref, g_ref, e_ref, do_ref, dlse_ref, dg_ref):
        @pl.when(pl.program_id(0) == 0)
        def _():
            dg_ref[...] = jnp.zeros_like(dg_ref)

        ws = _combine_weights(lse_ref)
        a = ws[0] * o_ref[0] + ws[1] * o_ref[1] + ws[2] * o_ref[2]
        r = _rms(a)
        n = a * r
        dv = datt_ref[...]
        dg_ref[...] += _colsum(dv * n)
        da = _rms_bwd(dv * g_ref[...], n, r)
        dws = [jnp.dot(da * o_ref[i], e_ref[...], preferred_element_type=F32, precision=HI) for i in range(3)]
        dbar = ws[0] * dws[0] + ws[1] * dws[1] + ws[2] * dws[2]
        for i in range(3):
            do_ref[i] = ws[i] * da
            dlse_ref[i] = ws[i] * (dws[i] - dbar)
        _fold8(pl.program_id(0) == L // TC - 1, dg_ref)

    sh = jax.ShapeDtypeStruct((N_GROUPS, L, KVW), F32)
    return pl.pallas_call(
        body, out_shape=(sh, sh, jax.ShapeDtypeStruct((8, KVW), F32)), grid=(L // TC,),
        in_specs=[pl.BlockSpec((TC, KVW), lambda i: (i, 0)), _comb_spec(), _comb_spec(), _vec(KVW),
                  pl.BlockSpec((KVW, KVW), lambda i: (0, 0))],
        out_specs=(_comb_spec(), _comb_spec(), _vec8(KVW)), name='combine_bwd',
        compiler_params=_params(('arbitrary',), 48))(datt, o3, lse3, g, head_ones)


def _ssm_disc(ar, ai, ldt):
    dt = jnp.exp(ldt)
    zr, zi = ar * dt, ai * dt
    ez = jnp.exp(zr)
    A_r, A_i = ez * jnp.cos(zi), ez * jnp.sin(zi)
    den = ar * ar + ai * ai
    xr, xi = A_r - 1.0, A_i
    cr = (xr * ar + xi * ai) / den
    ci = (xi * ar - xr * ai) / den
    return dt, zr, zi, A_r, A_i, den, cr, ci


def _ssm_pre(ar, ai, ldt, br, bi):
    def body(ar_ref, ai_ref, ldt_ref, br_ref, bi_ref, bbr_ref, bbi_ref, pwr_ref, pwi_ref):
        _, zr, zi, _, _, _, cr, ci = _ssm_disc(ar_ref[...], ai_ref[...], ldt_ref[...])
        bbr_ref[...] = cr * br_ref[...] - ci * bi_ref[...]
        bbi_ref[...] = cr * bi_ref[...] + ci * br_ref[...]
        k = (lax.broadcasted_iota(jnp.int32, (1, 8), 1) + 1).astype(F32)
        ek = jnp.exp(zr * k)
        pwr_ref[...] = ek * jnp.cos(zi * k)
        pwi_ref[...] = ek * jnp.sin(zi * k)

    s16 = jax.ShapeDtypeStruct((SSM_GN, SSM_P), F32)
    s8 = jax.ShapeDtypeStruct((SSM_GN, 8), F32)
    return pl.pallas_call(body, out_shape=(s16, s16, s8, s8), name='ssm_pre',
                          compiler_params=_params(None, 40))(ar, ai, ldt, br, bi)


def _ssm_post(ar, ai, ldt, br, bi, gar, gai, gbr, gbi, sel):
    def body(ar_ref, ai_ref, ldt_ref, br_ref, bi_ref, gar_ref, gai_ref, gbr_ref, gbi_ref, sel_ref,
             dar_ref, dai_ref, dbr_ref, dbi_ref, dldt_ref):
        a_r, a_i = ar_ref[...], ai_ref[...]
        dt, _, _, A_r, A_i, den, cr, ci = _ssm_disc(a_r, a_i, ldt_ref[...])
        b_r, b_i, g_br, g_bi = br_ref[...], bi_ref[...], gbr_ref[...], gbi_ref[...]
        gcr = jnp.sum(g_br * b_r + g_bi * b_i, axis=-1, keepdims=True)
        gci = jnp.sum(g_bi * b_r - g_br * b_i, axis=-1, keepdims=True)
        dbr_ref[...] = g_br * cr + g_bi * ci
        dbi_ref[...] = g_bi * cr - g_br * ci
        g_ar = gar_ref[...] + (gcr * a_r - gci * a_i) / den
        g_ai = gai_ref[...] + (gcr * a_i + gci * a_r) / den
        qr = (cr * a_r + ci * a_i) / den
        qi = (ci * a_r - cr * a_i) / den
        glr = -(gcr * qr + gci * qi)
        gli = -(gci * qr - gcr * qi)
        gzr = g_ar * A_r + g_ai * A_i
        gzi = g_ai * A_r - g_ar * A_i
        dar_ref[...] = glr + gzr * dt
        dai_ref[...] = gli + gzi * dt
        gdt = (gzr * a_r + gzi * a_i) * dt
        dldt_ref[...] = jnp.dot(sel_ref[...], jnp.broadcast_to(gdt, (SSM_GN, 128)),
                                preferred_element_type=F32, precision=HI)

    s1 = jax.ShapeDtypeStruct((SSM_GN, 1), F32)
    s16 = jax.ShapeDtypeStruct((SSM_GN, SSM_P), F32)
    return pl.pallas_call(body, out_shape=(s1, s1, s16, s16, jax.ShapeDtypeStruct((SSM_G, 128), F32)),
                          name='ssm_post', compiler_params=_params(None, 48))(
                              ar, ai, ldt, br, bi, gar, gai, gbr, gbi, sel)


SCAN_CH = 8


def _scan_fwd_tiles(s_ref, pw, carry):
    pwr, pwi = pw[:, :CL_S], pw[:, CL_S:]
    row = lax.broadcasted_iota(jnp.int32, (SCAN_CH, 8, CL_S), 1)
    steps = [(k, pwr[k - 1:k], pwi[k - 1:k]) for k in (1, 2, 4)]
    rows = 8 * SCAN_CH

    def chunk(i, c):
        cr, ci = c
        r0 = pl.multiple_of(i * rows, rows)
        xr = s_ref[pl.ds(r0, rows), 0:CL_S].reshape(SCAN_CH, 8, CL_S)
        xi = s_ref[pl.ds(r0, rows), CL_S:2 * CL_S].reshape(SCAN_CH, 8, CL_S)
        for k, pr, pi in steps:
            sr = jnp.where(row >= k, pltpu.roll(xr, k, 1), 0.0)
            si = jnp.where(row >= k, pltpu.roll(xi, k, 1), 0.0)
            xr, xi = xr + pr * sr - pi * si, xi + pr * si + pi * sr
        for j in range(SCAN_CH):
            tr = xr[j] + pwr * cr - pwi * ci
            ti = xi[j] + pwr * ci + pwi * cr
            s_ref[pl.ds(r0 + 8 * j, 8), 0:CL_S] = tr
            s_ref[pl.ds(r0 + 8 * j, 8), CL_S:2 * CL_S] = ti
            cr, ci = tr[7:8], ti[7:8]
        return cr, ci

    return lax.fori_loop(0, T_SCAN // rows, chunk, (carry[:, :CL_S], carry[:, CL_S:]))


def _scan_bwd_tiles(l_ref, pw, carry):
    pwr, pwi = pw[:, :CL_S], pw[:, CL_S:]
    rpr = jnp.concatenate([pwr[7 - r:8 - r] for r in range(8)], axis=0)
    rpi = jnp.concatenate([pwi[7 - r:8 - r] for r in range(8)], axis=0)
    row = lax.broadcasted_iota(jnp.int32, (SCAN_CH, 8, CL_S), 1)
    steps = [(k, pwr[k - 1:k], pwi[k - 1:k]) for k in (1, 2, 4)]
    rows = 8 * SCAN_CH
    nc = T_SCAN // rows

    def chunk(i, c):
        cr, ci = c
        r0 = pl.multiple_of((nc - 1 - i) * rows, rows)
        xr = l_ref[pl.ds(r0, rows), 0:CL_S].reshape(SCAN_CH, 8, CL_S)
        xi = l_ref[pl.ds(r0, rows), CL_S:2 * CL_S].reshape(SCAN_CH, 8, CL_S)
        for k, pr, pi in steps:
            sr = jnp.where(row < 8 - k, pltpu.roll(xr, 8 - k, 1), 0.0)
            si = jnp.where(row < 8 - k, pltpu.roll(xi, 8 - k, 1), 0.0)
            xr, xi = xr + pr * sr + pi * si, xi + pr * si - pi * sr
        for j in reversed(range(SCAN_CH)):
            tr = xr[j] + rpr * cr + rpi * ci
            ti = xi[j] + rpr * ci - rpi * cr
            l_ref[pl.ds(r0 + 8 * j, 8), 0:CL_S] = tr
            l_ref[pl.ds(r0 + 8 * j, 8), CL_S:2 * CL_S] = ti
            cr, ci = tr[0:1], ti[0:1]
        return cr, ci

    return lax.fori_loop(0, nc, chunk, (carry[:, :CL_S], carry[:, CL_S:]))


NT_SCAN = L // T_SCAN


def _hilo(t):
    hi = t.astype(BF16)
    return jnp.stack([hi, (t - hi.astype(F32)).astype(BF16)], axis=1)


def _dot3(a, b_ref):
    ah = a.astype(BF16)
    al = (a - ah.astype(F32)).astype(BF16)
    bh, bl = b_ref[0], b_ref[1]
    return (jnp.dot(ah, bh, preferred_element_type=F32) + jnp.dot(al, bh, preferred_element_type=F32)
            + jnp.dot(ah, bl, preferred_element_type=F32))


def _hl_spec(r, c):
    return pl.BlockSpec((None, 2, r, c), lambda c_, t: (c_, 0, 0, 0))


def _ssm_fwd(u, bm2, cm2, pw, dvec):
    def body(u_ref, bm_ref, cm_ref, pw_ref, d_ref, y_ref, bnd_ref, s_ref, carry_ref):
        @pl.when(pl.program_id(1) == 0)
        def _():
            carry_ref[...] = jnp.zeros_like(carry_ref)

        bnd_ref[...] = carry_ref[...]
        uv = u_ref[...]
        s_ref[...] = _dot3(uv, bm_ref)
        cr, ci = _scan_fwd_tiles(s_ref, pw_ref[...], carry_ref[...])
        carry_ref[...] = jnp.concatenate([cr, ci], axis=1)
        y_ref[...] = _dot3(s_ref[...], cm_ref) + d_ref[...] * uv

    return pl.pallas_call(
        body,
        out_shape=(jax.ShapeDtypeStruct((L, SSMW), F32), jax.ShapeDtypeStruct((N_CL, NT_SCAN, 1, 2 * CL_S), F32)),
        grid=(N_CL, NT_SCAN),
        in_specs=[pl.BlockSpec((T_SCAN, CL_U), lambda c, t: (t, c)),
                  _hl_spec(CL_U, 2 * CL_S), _hl_spec(2 * CL_S, CL_U),
                  pl.BlockSpec((None, 8, 2 * CL_S), lambda c, t: (c, 0, 0)),
                  pl.BlockSpec((1, CL_U), lambda c, t: (0, c))],
        out_specs=(pl.BlockSpec((T_SCAN, CL_U), lambda c, t: (t, c)),
                   pl.BlockSpec((None, None, 1, 2 * CL_S), lambda c, t: (c, t, 0, 0))),
        scratch_shapes=[pltpu.VMEM((T_SCAN, 2 * CL_S), F32), pltpu.VMEM((1, 2 * CL_S), F32)],
        name='ssm_fwd', compiler_params=_params(('arbitrary', 'arbitrary'), 40))(u, bm2, cm2, pw, dvec)


def _ssm_bwd(u, dy, bm2, bmt, cmt, pw, dvec, bnd):
    rev = lambda t: NT_SCAN - 1 - t

    def body(u_ref, dy_ref, bm_ref, bmt_ref, cmt_ref, pw_ref, d_ref, bnd_ref,
             du_ref, dbm_ref, dcm_ref, da_ref, dd_ref, s_ref, l_ref, carry_ref):
        @pl.when(pl.program_id(1) == 0)
        def _():
            carry_ref[...] = jnp.zeros_like(carry_ref)
            dbm_ref[...] = jnp.zeros_like(dbm_ref)
            dcm_ref[...] = jnp.zeros_like(dcm_ref)
            da_ref[...] = jnp.zeros_like(da_ref)
            dd_ref[...] = jnp.zeros_like(dd_ref)

        uv, dyv, pw = u_ref[...], dy_ref[...], pw_ref[...]
        dy_b = dyv.astype(BF16)
        entry = bnd_ref[...]
        s_ref[...] = _dot3(uv, bm_ref)
        _scan_fwd_tiles(s_ref, pw, entry)
        l_ref[...] = jnp.dot(dy_b, cmt_ref[...], preferred_element_type=F32)
        cr, ci = _scan_bwd_tiles(l_ref, pw, carry_ref[...])
        carry_ref[...] = jnp.concatenate([cr, ci], axis=1)
        sv, lv = s_ref[...], l_ref[...]
        lv_b = lv.astype(BF16)
        du_ref[...] = dyv * d_ref[...] + jnp.dot(lv_b, bmt_ref[...], preferred_element_type=F32)
        dbm_ref[...] += lax.dot_general(uv.astype(BF16), lv_b, TN_DIMS, preferred_element_type=F32)
        dcm_ref[...] += lax.dot_general(sv.astype(BF16), dy_b, TN_DIMS, preferred_element_type=F32)
        dd_ref[...] += _colsum(dyv * uv)
        row = lax.broadcasted_iota(jnp.int32, (T_SCAN, 2 * CL_S), 0)
        sp = jnp.where(row == 0, entry, pltpu.roll(sv, 1, 0))
        spr, spi = sp[:, :CL_S], sp[:, CL_S:]
        lr, li = lv[:, :CL_S], lv[:, CL_S:]
        da_ref[:, 0:CL_S] += _colsum(lr * spr + li * spi)
        da_ref[:, CL_S:2 * CL_S] += _colsum(li * spr - lr * spi)
        _fold8(pl.program_id(1) == NT_SCAN - 1, da_ref, dd_ref)

    return pl.pallas_call(
        body,
        out_shape=(jax.ShapeDtypeStruct((L, SSMW), F32), jax.ShapeDtypeStruct((N_CL, CL_U, 2 * CL_S), F32),
                   jax.ShapeDtypeStruct((N_CL, 2 * CL_S, CL_U), F32), jax.ShapeDtypeStruct((N_CL, 8, 2 * CL_S), F32),
                   jax.ShapeDtypeStruct((8, SSMW), F32)),
        grid=(N_CL, NT_SCAN),
        in_specs=[pl.BlockSpec((T_SCAN, CL_U), lambda c, t: (rev(t), c)),
                  pl.BlockSpec((T_SCAN, CL_U), lambda c, t: (rev(t), c)),
                  _hl_spec(CL_U, 2 * CL_S),
                  pl.BlockSpec((None, 2 * CL_S, CL_U), lambda c, t: (c, 0, 0)),
                  pl.BlockSpec((None, CL_U, 2 * CL_S), lambda c, t: (c, 0, 0)),
                  pl.BlockSpec((None, 8, 2 * CL_S), lambda c, t: (c, 0, 0)),
                  pl.BlockSpec((1, CL_U), lambda c, t: (0, c)),
                  pl.BlockSpec((None, None, 1, 2 * CL_S), lambda c, t: (c, rev(t), 0, 0))],
        out_specs=(pl.BlockSpec((T_SCAN, CL_U), lambda c, t: (rev(t), c)),
                   pl.BlockSpec((None, CL_U, 2 * CL_S), lambda c, t: (c, 0, 0)),
                   pl.BlockSpec((None, 2 * CL_S, CL_U), lambda c, t: (c, 0, 0)),
                   pl.BlockSpec((None, 8, 2 * CL_S), lambda c, t: (c, 0, 0)),
                   pl.BlockSpec((8, CL_U), lambda c, t: (0, c))),
        scratch_shapes=[pltpu.VMEM((T_SCAN, 2 * CL_S), F32), pltpu.VMEM((T_SCAN, 2 * CL_S), F32),
                        pltpu.VMEM((1, 2 * CL_S), F32)],
        name='ssm_bwd', compiler_params=_params(('arbitrary', 'arbitrary'), 48))(u, dy, bm2, bmt, cmt, pw, dvec, bnd)


GELU_C = math.sqrt(2.0 / math.pi)
GELU_K = 0.044715


def _gelu_parts(x):
    t = jnp.tanh(GELU_C * (x + GELU_K * (x * x * x)))
    return x * (0.5 * (1.0 + t)), t


def _glu_fwd(ypre, wglu, bglu, gs):
    def body(y_ref, w_ref, b_ref, g_ref, o_ref):
        yg, _ = _gelu_parts(y_ref[...])
        z = jnp.dot(yg.astype(BF16), w_ref[...], preferred_element_type=F32) + b_ref[...]
        s = yg * jax.nn.sigmoid(z)
        o_ref[...] = ((s * _rms(s)) * g_ref[...]).astype(BF16)

    return pl.pallas_call(
        body, out_shape=jax.ShapeDtypeStruct((L, SSMW), BF16), grid=(L // TR,),
        in_specs=[_rowspec(SSMW), pl.BlockSpec((SSMW, SSMW), lambda i: (0, 0)), _vec(SSMW), _vec(SSMW)],
        out_specs=_rowspec(SSMW), name='glu_fwd', compiler_params=_params(('parallel',), 32))(ypre, wglu, bglu, gs)


def _glu_bwd(ypre, dsn, wglu, bglu, gs):
    def body(y_ref, d_ref, w_ref, b_ref, g_ref, dy_ref, dw_ref, db_ref, dg_ref):
        @pl.when(pl.program_id(0) == 0)
        def _():
            dw_ref[...] = jnp.zeros_like(dw_ref)
            db_ref[...] = jnp.zeros_like(db_ref)
            dg_ref[...] = jnp.zeros_like(dg_ref)

        xv = y_ref[...]
        yg, t = _gelu_parts(xv)
        yg_b = yg.astype(BF16)
        z = jnp.dot(yg_b, w_ref[...], preferred_element_type=F32) + b_ref[...]
        sg = jax.nn.sigmoid(z)
        s = yg * sg
        r = _rms(s)
        n = s * r
        dv = d_ref[...]
        dg_ref[...] += _colsum(dv * n)
        ds = _rms_bwd(dv * g_ref[...], n, r)
        dz = (ds * yg) * (sg * (1.0 - sg))
        dz_b = dz.astype(BF16)
        db_ref[...] += _colsum(dz)
        dw_ref[...] += lax.dot_general(yg_b, dz_b, TN_DIMS, preferred_element_type=F32)
        dyg = ds * sg + lax.dot_general(dz_b, w_ref[...], NT_DIMS, preferred_element_type=F32)
        dgelu = 0.5 * (1.0 + t) + (0.5 * xv) * (1.0 - t * t) * (GELU_C * (1.0 + 3.0 * GELU_K * (xv * xv)))
        dy_ref[...] = dyg * dgelu
        _fold8(pl.program_id(0) == L // TR - 1, db_ref, dg_ref)

    vs = jax.ShapeDtypeStruct((8, SSMW), F32)
    return pl.pallas_call(
        body, out_shape=(jax.ShapeDtypeStruct((L, SSMW), F32), jax.ShapeDtypeStruct((SSMW, SSMW), F32), vs, vs),
        grid=(L // TR,),
        in_specs=[_rowspec(SSMW), _rowspec(SSMW), pl.BlockSpec((SSMW, SSMW), lambda i: (0, 0)), _vec(SSMW), _vec(SSMW)],
        out_specs=(_rowspec(SSMW), pl.BlockSpec((SSMW, SSMW), lambda i: (0, 0)), _vec8(SSMW), _vec8(SSMW)),
        name='glu_bwd', compiler_params=_params(('arbitrary',), 40))(ypre, dsn, wglu, bglu, gs)


def _me():
    return lax.axis_index('x'), lax.axis_index('y'), lax.axis_index('c')


def _peer(k):
    x, y, c = _me()
    px = 1 - x if k & 4 else x
    py = 1 - y if k & 2 else y
    pc = 1 - c if k & 1 else c
    return (px, py, pc), 4 * px + 2 * py + pc


def _mod_exchange(c_row, w_ada, b_ada8):
    cw = NMOD * D // N_DEV

    def body(c_ref, w_ref, b_ref, call_ref, mod_ref, part_ref, send_sems, recv_sems):
        x, y, c = _me()
        me = 4 * x + 2 * y + c
        call_ref[me] = c_ref[0]
        sends = []
        for k in range(1, N_DEV):
            peer, _ = _peer(k)
            cp = pltpu.make_async_remote_copy(src_ref=c_ref.at[0], dst_ref=call_ref.at[me], send_sem=send_sems.at[0, k - 1],
                                              recv_sem=recv_sems.at[0, k - 1], device_id=peer, device_id_type=MESH)
            cp.start()
            sends.append(cp)
        for k in range(1, N_DEV):
            peer, pidx = _peer(k)
            pltpu.make_async_remote_copy(src_ref=c_ref.at[0], dst_ref=call_ref.at[pidx], send_sem=send_sems.at[0, k - 1],
                                         recv_sem=recv_sems.at[0, k - 1], device_id=peer, device_id_type=MESH).wait_recv()
        for cp in sends:
            cp.wait_send()
        cv = call_ref[...].reshape(N_DEV, D)
        part = jnp.dot(cv * jax.nn.sigmoid(cv), w_ref[...], preferred_element_type=F32, precision=HI)
        part_ref[...] = part.reshape(N_DEV, 1, cw)
        mod_ref[me] = part_ref[me]
        sends = []
        for k in range(1, N_DEV):
            peer, pidx = _peer(k)
            cp = pltpu.make_async_remote_copy(src_ref=part_ref.at[pidx], dst_ref=mod_ref.at[me], send_sem=send_sems.at[1, k - 1],
                                              recv_sem=recv_sems.at[1, k - 1], device_id=peer, device_id_type=MESH)
            cp.start()
            sends.append(cp)
        for k in range(1, N_DEV):
            peer, pidx = _peer(k)
            pltpu.make_async_remote_copy(src_ref=part_ref.at[pidx], dst_ref=mod_ref.at[pidx], send_sem=send_sems.at[1, k - 1],
                                         recv_sem=recv_sems.at[1, k - 1], device_id=peer, device_id_type=MESH).wait_recv()
        for cp in sends:
            cp.wait_send()
        mod_ref[...] = mod_ref[...] + b_ref[...]

    vm = pl.BlockSpec(memory_space=pltpu.VMEM)
    return pl.pallas_call(
        body, out_shape=(jax.ShapeDtypeStruct((N_DEV, 1, D), F32), jax.ShapeDtypeStruct((N_DEV, 1, cw), F32)),
        in_specs=[vm, vm, vm], out_specs=(vm, vm),
        scratch_shapes=[pltpu.VMEM((N_DEV, 1, cw), F32), pltpu.SemaphoreType.DMA((2, N_DEV - 1)),
                        pltpu.SemaphoreType.DMA((2, N_DEV - 1))],
        name='mod_exchange', compiler_params=_params(None, 48))(c_row, w_ada, b_ada8)


HBM_SPEC = pl.BlockSpec(memory_space=pltpu.HBM)
SEM_SPEC = pl.BlockSpec(memory_space=pltpu.SEMAPHORE)
DATAFLOW = pltpu.SideEffectType.DATAFLOW_SIDE_EFFECTING


def _own_slab(src, scatter, name):
    shape = src.shape if scatter else (N_DEV,) + src.shape

    def body(src_ref, land_ref, sem):
        x, y, c = _me()
        me = 4 * x + 2 * y + c
        cp = pltpu.make_async_copy(src_ref.at[me] if scatter else src_ref, land_ref.at[me], sem)
        cp.start()
        cp.wait()

    return pl.pallas_call(body, out_shape=jax.ShapeDtypeStruct(shape, src.dtype), in_specs=[HBM_SPEC],
                          out_specs=HBM_SPEC, scratch_shapes=[pltpu.SemaphoreType.DMA(())], name=name)(src)


def _push_start(src, land, scatter, name):
    def body(src_ref, land_ref, send_sem, recv_sem, src_thru, land_thru, token):
        x, y, c = _me()
        me = 4 * x + 2 * y + c
        for k in range(1, N_DEV):
            peer, pidx = _peer(k)
            pltpu.make_async_remote_copy(src_ref=src_ref.at[pidx] if scatter else src_ref, dst_ref=land_ref.at[me],
                                         send_sem=send_sem, recv_sem=recv_sem, device_id=peer,
                                         device_id_type=MESH).start()
        token[...] = jnp.zeros_like(token)

    return pl.pallas_call(
        body, name=name,
        out_shape=(pltpu.SemaphoreType.DMA(()), pltpu.SemaphoreType.DMA(()), pltpu.HBM(src.shape, src.dtype),
                   pltpu.HBM(land.shape, land.dtype), jax.ShapeDtypeStruct((8, 128), F32)),
        in_specs=(HBM_SPEC, HBM_SPEC),
        out_specs=(SEM_SPEC, SEM_SPEC, HBM_SPEC, HBM_SPEC, pl.BlockSpec(memory_space=pltpu.VMEM)),
        input_output_aliases={0: 2, 1: 3}, compiler_params=pltpu.CompilerParams(has_side_effects=DATAFLOW),
    )(pltpu.with_memory_space_constraint(src, pltpu.HBM), pltpu.with_memory_space_constraint(land, pltpu.HBM))


def _push_wait(handle, after, name):
    send_sem, recv_sem, src_thru, land_thru, _ = handle

    def body(src_ref, land_ref, send_sem, recv_sem, after_ref, src_dead, got_ref):
        seven = land_ref.at[pl.ds(0, N_DEV - 1)]
        cp = pltpu.make_async_remote_copy(src_ref=seven, dst_ref=seven, send_sem=send_sem, recv_sem=recv_sem,
                                          device_id=_me(), device_id_type=MESH)
        cp.wait_send()
        cp.wait_recv()

    return pl.pallas_call(
        body, name=name,
        out_shape=(pltpu.HBM(src_thru.shape, src_thru.dtype), pltpu.HBM(land_thru.shape, land_thru.dtype)),
        in_specs=(HBM_SPEC, HBM_SPEC, SEM_SPEC, SEM_SPEC, pl.BlockSpec(memory_space=pl.ANY)),
        out_specs=(HBM_SPEC, HBM_SPEC), input_output_aliases={0: 0, 1: 1},
        compiler_params=pltpu.CompilerParams(has_side_effects=DATAFLOW),
    )(src_thru, land_thru, send_sem, recv_sem, after)[1]


def _adam(w, g, m, v):
    m2 = B1 * m + (1.0 - B1) * g
    v2 = B2 * v + (1.0 - B2) * jnp.square(g)
    m_hat = m2 / (1.0 - B1 ** STEP)
    v_hat = v2 / (1.0 - B2 ** STEP)
    delta = -LR * (m_hat / (jnp.sqrt(v_hat) + AEPS) + WD * w)
    return delta, m2, v2


def _small_update(gp, wp, mp, vp):
    def body(g_ref, w_ref, m_ref, v_ref, all_ref, go_ref, d_ref, mo_ref, vo_ref, send_sems, recv_sems):
        x, y, c = _me()
        me = 4 * x + 2 * y + c
        all_ref[me] = g_ref[...]
        sends = []
        for k in range(1, N_DEV):
            peer, _ = _peer(k)
            cp = pltpu.make_async_remote_copy(src_ref=g_ref, dst_ref=all_ref.at[me], send_sem=send_sems.at[k - 1],
                                              recv_sem=recv_sems.at[k - 1], device_id=peer, device_id_type=MESH)
            cp.start()
            sends.append(cp)
        for k in range(1, N_DEV):
            peer, pidx = _peer(k)
            pltpu.make_async_remote_copy(src_ref=g_ref, dst_ref=all_ref.at[pidx], send_sem=send_sems.at[k - 1],
                                         recv_sem=recv_sems.at[k - 1], device_id=peer, device_id_type=MESH).wait_recv()
        for cp in sends:
            cp.wait_send()
        g = all_ref[0]
        for d in range(1, N_DEV):
            g = g + all_ref[d]
        delta, m2, v2 = _adam(w_ref[...], g, m_ref[...], v_ref[...])
        go_ref[...] = g
        d_ref[...] = delta
        mo_ref[...] = m2
        vo_ref[...] = v2

    vm = pl.BlockSpec(memory_space=pltpu.VMEM)
    vs = jax.ShapeDtypeStruct((NS // 128, 128), F32)
    return pl.pallas_call(
        body, out_shape=(jax.ShapeDtypeStruct((N_DEV, NS // 128, 128), F32), vs, vs, vs, vs), in_specs=[vm] * 4,
        out_specs=(vm,) * 5,
        scratch_shapes=[pltpu.SemaphoreType.DMA((N_DEV - 1,)), pltpu.SemaphoreType.DMA((N_DEV - 1,))],
        name='small_update', compiler_params=_params(None, 48))(gp, wp, mp, vp)


def _big_update(parts, w, m, v, name):
    _, R, C = parts.shape
    tr = R if R % 256 else (128 if C >= 2048 else 256)

    def body(p_ref, w_ref, m_ref, v_ref, g_ref, d_ref, mo_ref, vo_ref):
        g = p_ref[0].astype(F32)
        for d in range(1, N_DEV):
            g = g + p_ref[d].astype(F32)
        delta, m2, v2 = _adam(w_ref[...], g, m_ref[...], v_ref[...])
        g_ref[...] = g
        d_ref[...] = delta
        mo_ref[...] = m2
        vo_ref[...] = v2

    blk = pl.BlockSpec((tr, C), lambda i: (i, 0))
    sh = jax.ShapeDtypeStruct((R, C), F32)
    return pl.pallas_call(
        body, out_shape=(sh, sh, sh, sh), grid=(R // tr,),
        in_specs=[pl.BlockSpec((N_DEV, tr, C), lambda i: (0, i, 0)), blk, blk, blk], out_specs=(blk,) * 4,
        name=name, compiler_params=_params(('parallel',), 48))(parts, w, m, v)


def _ada_update(c_all, dmod_cols, w, m, v):
    C = w.shape[1]
    tr = 256

    def body(c_ref, dm_ref, w_ref, m_ref, v_ref, g_ref, d_ref, mo_ref, vo_ref):
        cv = c_ref[...]
        s = cv * jax.nn.sigmoid(cv)
        g = lax.dot_general(s, dm_ref[...], TN_DIMS, preferred_element_type=F32, precision=HI)
        delta, m2, v2 = _adam(w_ref[...], g, m_ref[...], v_ref[...])
        g_ref[...] = g
        d_ref[...] = delta
        mo_ref[...] = m2
        vo_ref[...] = v2

    blk = pl.BlockSpec((tr, C), lambda i: (i, 0))
    sh = jax.ShapeDtypeStruct((D, C), F32)
    return pl.pallas_call(
        body, out_shape=(sh, sh, sh, sh), grid=(D // tr,),
        in_specs=[pl.BlockSpec((N_DEV, tr), lambda i: (0, i)), pl.BlockSpec((N_DEV, C), lambda i: (0, 0)), blk, blk, blk],
        out_specs=(blk,) * 4, name='ada_update', compiler_params=_params(('parallel',), 48))(c_all, dmod_cols, w, m, v)


def _to_sub(t, d):
    if d == 1:
        return t
    return t.reshape(L // d, d, t.shape[-1]).transpose(1, 0, 2).reshape(L, t.shape[-1])


def _from_sub(t, d):
    if d == 1:
        return t
    return t.reshape(d, L // d, t.shape[-1]).transpose(1, 0, 2).reshape(L, t.shape[-1])


def _rows_to_cluster_lanes(t):
    k = t.shape[1]
    return t.reshape(N_CL, CL_S, k).transpose(0, 2, 1)


def _blockdiag_in(t):
    t = t.reshape(N_CL, CL_G, SSM_N, SSM_P).transpose(0, 1, 3, 2)
    eye = jnp.eye(CL_G, dtype=t.dtype)
    t = t[:, :, :, None, :] * eye[None, :, None, :, None]
    return t.reshape(N_CL, CL_U, CL_S)


def _blockdiag_extract(t):
    t = t.reshape(N_CL, CL_G, SSM_P, CL_G, SSM_N)
    t = jnp.stack([t[:, i, :, i, :] for i in range(CL_G)], axis=1)
    return t.transpose(0, 1, 3, 2).reshape(SSM_GN, SSM_P)


def _c_to_rows(t):
    return t.transpose(0, 2, 1).reshape(SSM_GN, SSM_P)


def _rows_to_c(t):
    return t.reshape(SSM_G, SSM_N, SSM_P).transpose(0, 2, 1)


def _local_step(x, pos, mod, tgt, sp, get_w, emit, first_deps=()):
    sh1, sc1, gt1, sh2, sc2, gt2 = (mod[i:i + 1] for i in range(NMOD))
    vec = lambda n: sp[n].reshape(1, -1)

    h1 = _prenorm_fwd(x, vec('g_pre_mix'), sc1, sh1)
    w_in = get_w('w_in', h1)
    proj = _mm(h1, w_in, mode='nn', name='mm_in', tn=1408, deps=first_deps)
    fr1 = ROPE_THETA ** (-jnp.arange(0, ROT_DIM, 2, dtype=F32) / ROT_DIM)
    lane = jnp.arange(128) % HEAD_DIM
    fr = jnp.where(lane < ROT_DIM, fr1[lane % (ROT_DIM // 2)], 0.0).reshape(1, 128).astype(F32)
    qk = _rope(proj[:, :ROPE_W], pos, fr, 1.0, 'rope_fwd')
    v_b = proj[:, ROPE_W:ROPE_W + KVW].astype(BF16)
    u = proj[:, ROPE_W + KVW:]
    qp = jnp.stack([_to_sub(qk[:, gi * KVW:(gi + 1) * KVW], d) for gi, d in enumerate(DILATIONS)])
    kp = jnp.stack([_to_sub(qk[:, QW:], d) for d in DILATIONS])
    vp = jnp.stack([_to_sub(v_b, d) for d in DILATIONS])
    o_p, lse_p = _attn_fwd(qp, kp, vp)
    o3 = jnp.stack([_from_sub(o_p[gi], d) for gi, d in enumerate(DILATIONS)])
    lse3 = jnp.stack([_from_sub(lse_p[gi], d) for gi, d in enumerate(DILATIONS)])
    att = _combine_fwd(o3, lse3, vec('g_attn_out'))

    rows = lambda n: sp[n].reshape(SSM_GN, 1)
    a_re, a_im = rows('ssm_a_re'), rows('ssm_a_im')
    ldt = jnp.repeat(sp['ssm_log_dt'].reshape(SSM_G, 1), SSM_N, axis=0)
    b_re, b_im = sp['ssm_b_re'].reshape(SSM_GN, SSM_P), sp['ssm_b_im'].reshape(SSM_GN, SSM_P)
    c_re, c_im = _c_to_rows(sp['ssm_c_re'].reshape(SSM_G, SSM_P, SSM_N)), _c_to_rows(sp['ssm_c_im'].reshape(SSM_G, SSM_P, SSM_N))
    bbr, bbi, pwr, pwi = _ssm_pre(a_re, a_im, ldt, b_re, b_im)
    bm = jnp.concatenate([_blockdiag_in(bbr), _blockdiag_in(bbi)], axis=2)
    cmt = jnp.concatenate([_blockdiag_in(c_re), -_blockdiag_in(c_im)], axis=2)
    bmt, cm = bm.transpose(0, 2, 1), cmt.transpose(0, 2, 1)
    pw = jnp.concatenate([_rows_to_cluster_lanes(pwr), _rows_to_cluster_lanes(pwi)], axis=2)
    dvec = vec('ssm_d')
    bm2, cm2 = _hilo(bm), _hilo(cm)
    ypre, bnd = _ssm_fwd(u, bm2, cm2, pw, dvec)
    w_glu = get_w('w_glu', ypre)
    ssm_n = _glu_fwd(ypre, w_glu, vec('b_glu'), vec('g_ssm_out'))

    cat = jnp.concatenate([att, ssm_n], axis=1)
    w_out = get_w('w_out', cat)
    mix = _mm(cat, w_out, mode='nn', name='mm_out', tk=640)
    x1, h2 = _postmix_fwd(x, mix, vec('g_post_mix'), gt1, vec('g_pre_mlp'), sc2, sh2)
    w_mi = get_w('w_mlp_in', h2)
    a_pre, r_act = _mm(h2, w_mi, mode='nn', name='mm_mlp_in', epilogue='relu2', b_sharded=True)
    w_mo = get_w('w_mlp_out', a_pre)
    y = _mm(r_act, w_mo, mode='nn', name='mm_mlp_out')
    dx2, dy, loss, dgt2, dg_post_mlp = _final_fwd_bwd(x1, y, tgt, vec('g_post_mlp'), gt2)
    dgt2, dg_post_mlp = dgt2[:1], dg_post_mlp[:1]

    da = _mm(dy, w_mo, mode='nt', name='mm_d_act', out_dtype=BF16, epilogue='drelu2', extra=a_pre)
    dep = emit('w_mlp_out', _mm(r_act, dy, mode='tn', name='mm_dw_mlp_out', out_dtype=BF16))
    dh2 = _mm(da, w_mi, mode='nt', name='mm_dh2', b_sharded=True, deps=dep)
    dep = emit('w_mlp_in', _mm(h2, da, mode='tn', name='mm_dw_mlp_in', out_dtype=BF16, out_sharded=True))
    dx1, dmix, dsc2, dsh2, dg_pre_mlp, dgt1, dg_post_mix = _postmix_bwd(
        dx2, dh2, x1, mix, vec('g_post_mix'), gt1, vec('g_pre_mlp'), sc2)
    dsc2, dsh2, dg_pre_mlp, dgt1, dg_post_mix = (t[:1] for t in (dsc2, dsh2, dg_pre_mlp, dgt1, dg_post_mix))
    dcat = _mm(dmix, w_out, mode='nt', name='mm_dcat', tn=1280, deps=dep)
    dep = emit('w_out', _mm(cat, dmix, mode='tn', name='mm_dw_out', out_dtype=BF16, tm=640))
    datt, dsn = dcat[:, :KVW], dcat[:, KVW:]

    dypre, g_w_glu, g_b_glu, g_g_ssm = _glu_bwd(ypre, dsn, w_glu, vec('b_glu'), vec('g_ssm_out'))
    g_b_glu, g_g_ssm = g_b_glu[:1], g_g_ssm[:1]
    dep = dep + emit('w_glu', g_w_glu.astype(BF16))
    du, dbm, dcm, dA, dD = _ssm_bwd(u, dypre, bm2, bmt.astype(BF16), cmt.astype(BF16), pw, dvec, bnd)
    dD = dD[:1]
    gbr, gbi = _blockdiag_extract(dbm[:, :, :CL_S]), _blockdiag_extract(dbm[:, :, CL_S:])
    dcmt = dcm.transpose(0, 2, 1)
    g_c_re = _rows_to_c(_blockdiag_extract(dcmt[:, :, :CL_S]))
    g_c_im = _rows_to_c(-_blockdiag_extract(dcmt[:, :, CL_S:]))
    gar = dA[:, 0, :CL_S].reshape(SSM_GN, 1)
    gai = dA[:, 0, CL_S:].reshape(SSM_GN, 1)
    sel = (jnp.arange(SSM_GN)[None, :] // SSM_N == jnp.arange(SSM_G)[:, None]).astype(F32)
    g_a_re, g_a_im, g_b_re, g_b_im, g_ldt = _ssm_post(a_re, a_im, ldt, b_re, b_im, gar, gai, gbr, gbi, sel)

    head_ones = (jnp.arange(KVW)[:, None] // HEAD_DIM == jnp.arange(KVW)[None, :] // HEAD_DIM).astype(F32)
    do3, dlse3, g_g_attn = _combine_bwd(datt, o3, lse3, vec('g_attn_out'), head_ones)
    g_g_attn = g_g_attn[:1]
    do_p = jnp.stack([_to_sub(do3[gi], d) for gi, d in enumerate(DILATIONS)])
    dlse_p = jnp.stack([_to_sub(dlse3[gi], d) for gi, d in enumerate(DILATIONS)])
    dq_p, dk_p, dv_p = _attn_bwd(qp, kp, vp, o_p, lse_p, do_p, dlse_p)
    dq = jnp.concatenate([_from_sub(dq_p[gi], d) for gi, d in enumerate(DILATIONS)], axis=1)
    dk = sum(_from_sub(dk_p[gi], d) for gi, d in enumerate(DILATIONS))
    dv = sum(_from_sub(dv_p[gi], d) for gi, d in enumerate(DILATIONS))
    dqk = _rope(jnp.concatenate([dq, dk], axis=1), pos, fr, -1.0, 'rope_bwd')
    dproj = jnp.concatenate([dqk, dv.astype(BF16), du.astype(BF16)], axis=1)
    dh1 = _mm(dproj, w_in, mode='nt', name='mm_dh1', tk=1408, deps=dep)
    emit('w_in', _mm(h1, dproj, mode='tn', name='mm_dw_in', out_dtype=BF16, tn=1408))
    grad_x, dsc1, dsh1, dg_pre_mix = _prenorm_bwd(dx1, dh1, x, vec('g_pre_mix'), sc1)
    dsc1, dsh1, dg_pre_mix = dsc1[:1], dsh1[:1], dg_pre_mix[:1]

    dmod = jnp.concatenate([dsh1, dsc1, dgt1, dsh2, dsc2, dgt2], axis=0)
    small = {'b_ada': dmod, 'g_pre_mix': dg_pre_mix, 'g_post_mix': dg_post_mix, 'ssm_a_re': g_a_re, 'ssm_a_im': g_a_im,
             'ssm_log_dt': g_ldt[:, 0], 'ssm_b_re': g_b_re, 'ssm_b_im': g_b_im, 'ssm_c_re': g_c_re, 'ssm_c_im': g_c_im,
             'ssm_d': dD, 'b_glu': g_b_glu, 'g_attn_out': g_g_attn, 'g_ssm_out': g_g_ssm, 'g_pre_mlp': dg_pre_mlp,
             'g_post_mlp': dg_post_mlp}
    return loss[0, 0], grad_x, small


def _pack(d):
    return jnp.concatenate([jnp.pad(d[n].reshape(-1).astype(F32), (0, SEG[n] - SMALL_SIZES[n])) for n in SMALL])


def _shard_major(t, name):
    if name in ('w_in', 'w_out', 'w_mlp_in'):
        k, n = t.shape
        return t.reshape(k, N_DEV, n // N_DEV).transpose(1, 0, 2)
    k, n = t.shape
    return t.reshape(N_DEV, k // N_DEV, n)


def _from_shard_major(t, name):
    if name in ('w_in', 'w_out', 'w_mlp_in'):
        _, k, n = t.shape
        return t.transpose(1, 0, 2).reshape(k, N_DEV * n)
    _, k, n = t.shape
    return t.reshape(N_DEV * k, n)


def kernel(x, c, positions, w_ada, b_ada, g_pre_mix, g_post_mix, w_in, ssm_a_re, ssm_a_im, ssm_log_dt, ssm_b_re, ssm_b_im, ssm_c_re, ssm_c_im, ssm_d, w_glu, b_glu, g_attn_out, g_ssm_out, w_out, g_pre_mlp, g_post_mlp, w_mlp_in, w_mlp_out, loss_target, m_w_ada, m_b_ada, m_g_pre_mix, m_g_post_mix, m_w_in, m_ssm_a_re, m_ssm_a_im, m_ssm_log_dt, m_ssm_b_re, m_ssm_b_im, m_ssm_c_re, m_ssm_c_im, m_ssm_d, m_w_glu, m_b_glu, m_g_attn_out, m_g_ssm_out, m_w_out, m_g_pre_mlp, m_g_post_mlp, m_w_mlp_in, m_w_mlp_out, v_w_ada, v_b_ada, v_g_pre_mix, v_g_post_mix, v_w_in, v_ssm_a_re, v_ssm_a_im, v_ssm_log_dt, v_ssm_b_re, v_ssm_b_im, v_ssm_c_re, v_ssm_c_im, v_ssm_d, v_w_glu, v_b_glu, v_g_attn_out, v_g_ssm_out, v_w_out, v_g_pre_mlp, v_g_post_mlp, v_w_mlp_in, v_w_mlp_out):
    loc = dict(locals())
    W = {n: loc[n] for n in WEIGHTS}
    M = {n: loc['m_' + n] for n in WEIGHTS}
    V = {n: loc['v_' + n] for n in WEIGHTS}
    assert x.shape == (1, L, D) and w_in.shape == (1, D, INW // N_DEV), (x.shape, w_in.shape)

    cw = NMOD * D // N_DEV
    c_all, mod8 = _mod_exchange(c.reshape(1, 1, D), w_ada[0], b_ada.reshape(N_DEV, 1, cw))
    mod = mod8.reshape(NMOD, D)

    gather = {}
    for n in BIG:
        shard = W[n][0].astype(BF16)
        gather[n] = _push_start(shard, _own_slab(shard, False, 'gather_own_' + n), False, 'gather_start_' + n)
    mod = mod + sum(gather[n][4][0, 0] for n in BIG)

    def get_w(n, after):
        g = _push_wait(gather[n], after, 'gather_wait_' + n)
        return g if n == 'w_mlp_in' else _from_shard_major(g, n)

    scatter = {}

    def emit(n, g):
        src = g if n == 'w_mlp_in' else _shard_major(g, n)
        scatter[n] = _push_start(src, _own_slab(src, True, 'scatter_own_' + n), True, 'scatter_start_' + n)
        return (scatter[n][4],)

    sp = {n: W[n][0] for n in SMALL}
    loss, grad_x, small = _local_step(x[0], positions.reshape(L, 1), mod, loss_target[0], sp, get_w, emit)
    loss = lax.psum(loss, ('x', 'y', 'c'))

    out_g, out_d, out_m, out_v = {}, {}, {}, {}
    after = grad_x
    for n in ('w_mlp_out', 'w_mlp_in', 'w_out', 'w_glu', 'w_in'):
        parts = _push_wait(scatter[n], after, 'scatter_wait_' + n)
        out_g[n], out_d[n], out_m[n], out_v[n] = _big_update(parts, W[n][0], M[n][0], V[n][0], 'update_' + n)
        after = out_v[n]

    rows_all, sg, sd, sm, sv = _small_update(
        *[_pack(d).reshape(NS // 128, 128) for d in (small, {n: W[n] for n in SMALL}, {n: M[n] for n in SMALL},
                                                    {n: V[n] for n in SMALL})])
    off = 0
    for n in SMALL:
        sz = SMALL_SIZES[n]
        for dst, src in ((out_g, sg), (out_d, sd), (out_m, sm), (out_v, sv)):
            dst[n] = src[off // 128:(off + SEG[n]) // 128].reshape(-1)[:sz].reshape(W[n].shape[1:])
        off += SEG[n]

    me = 4 * lax.axis_index('x') + 2 * lax.axis_index('y') + lax.axis_index('c')
    dmod_all = rows_all[:, :NMOD * D // 128].reshape(N_DEV, NMOD * D)
    dmod_cols = lax.dynamic_slice_in_dim(dmod_all, me * cw, cw, axis=1)
    out_g['w_ada'], out_d['w_ada'], out_m['w_ada'], out_v['w_ada'] = _ada_update(
        c_all.reshape(N_DEV, D), dmod_cols, w_ada[0], m_w_ada[0], v_w_ada[0])

    lead = lambda t: t[None]
    return (loss, grad_x[None], *[lead(out_g[n]) for n in WEIGHTS], *[lead(out_d[n]) for n in WEIGHTS],
            *[lead(out_m[n]) for n in WEIGHTS], *[lead(out_v[n]) for n in WEIGHTS])
```

```python
import functools
import math

import jax
import jax.numpy as jnp
from jax import lax
from jax.experimental import pallas as pl
from jax.experimental.pallas import tpu as pltpu

F32 = jnp.float32
BF16 = jnp.bfloat16
HI = lax.Precision.HIGHEST
MESH = pl.DeviceIdType.MESH

N_DEV = 8
L = 4096
D = 2048
HEAD_DIM = 64
N_GROUPS = 3
DILATIONS = (1, 4, 16)
HEADS = 6
QW = N_GROUPS * HEADS * HEAD_DIM
KVW = HEADS * HEAD_DIM
ROT_DIM = 16
ROPE_THETA = 500000.0
BLK = 128
NBLK = L // BLK
SSMW = D - QW
SSM_P = 16
SSM_G = SSMW // SSM_P
SSM_N = 64
SSM_GN = SSM_G * SSM_N
CL_G = 8
N_CL = SSM_G // CL_G
CL_U = CL_G * SSM_P
CL_S = CL_G * SSM_N
INW = QW + 2 * KVW + SSMW
OUTW = KVW + SSMW
DFF = 4 * D
NMOD = 6
EPS = 1e-6
LR, B1, B2, AEPS, WD, STEP = 0.001, 0.9, 0.999, 1e-08, 0.01, 10

T_SCAN = 512
MB = 2 ** 20

WEIGHTS = ['w_ada', 'b_ada', 'g_pre_mix', 'g_post_mix', 'w_in', 'ssm_a_re', 'ssm_a_im', 'ssm_log_dt',
           'ssm_b_re', 'ssm_b_im', 'ssm_c_re', 'ssm_c_im', 'ssm_d', 'w_glu', 'b_glu', 'g_attn_out',
           'g_ssm_out', 'w_out', 'g_pre_mlp', 'g_post_mlp', 'w_mlp_in', 'w_mlp_out']
BIG = ['w_in', 'w_glu', 'w_out', 'w_mlp_in', 'w_mlp_out']
SMALL = [n for n in WEIGHTS if n not in BIG and n != 'w_ada']
SMALL_SIZES = {'b_ada': NMOD * D, 'g_pre_mix': D, 'g_post_mix': D, 'ssm_a_re': SSM_GN, 'ssm_a_im': SSM_GN,
               'ssm_log_dt': SSM_G, 'ssm_b_re': SSM_GN * SSM_P, 'ssm_b_im': SSM_GN * SSM_P,
               'ssm_c_re': SSM_GN * SSM_P, 'ssm_c_im': SSM_GN * SSM_P, 'ssm_d': SSMW, 'b_glu': SSMW,
               'g_attn_out': KVW, 'g_ssm_out': SSMW, 'g_pre_mlp': D, 'g_post_mlp': D}
SEG = {n: -(-SMALL_SIZES[n] // 1024) * 1024 for n in SMALL}
NS = sum(SEG.values())


def _params(sem=None, vmem_mb=None):
    kw = {}
    if sem is not None:
        kw['dimension_semantics'] = sem
    if vmem_mb is not None:
        kw['vmem_limit_bytes'] = vmem_mb * MB
    return pltpu.CompilerParams(**kw)


def _vec(n):
    return pl.BlockSpec((1, n), lambda *_: (0, 0))


def _rms(x):
    return lax.rsqrt(jnp.mean(x * x, axis=-1, keepdims=True) + EPS)


def _rms_bwd(dn, n, r):
    return r * (dn - n * jnp.mean(dn * n, axis=-1, keepdims=True))


def _vec8(n):
    return pl.BlockSpec((8, n), lambda *_: (0, 0))


def _colsum(x):
    return jnp.sum(x.reshape(-1, 8, x.shape[-1]), axis=0)


def _fold8(last, *refs):
    @pl.when(last)
    def _():
        for r in refs:
            r[...] = jnp.broadcast_to(jnp.sum(r[...], axis=0, keepdims=True), r.shape)


def _mm(a, b, *, mode, name, out_dtype=F32, tm=1024, tn=1024, tk=512, epilogue=None, extra=None,
        b_sharded=False, out_sharded=False, deps=()):
    if mode == 'nn':
        M, K = a.shape
        dims = (((1,), (0,)), ((), ()))
        a_spec = pl.BlockSpec((tm, tk), lambda i, j, k: (i, k))
        if b_sharded:
            _, K2, per = b.shape
            N, q = N_DEV * per, per // tn
            b_spec = pl.BlockSpec((None, tk, tn), lambda i, j, k: (j // q, k, j % q))
        else:
            K2, N = b.shape
            b_spec = pl.BlockSpec((tk, tn), lambda i, j, k: (k, j))
    elif mode == 'nt':
        M, K = a.shape
        dims = (((1,), (1,)), ((), ()))
        a_spec = pl.BlockSpec((tm, tk), lambda i, j, k: (i, k))
        if b_sharded:
            _, N, per = b.shape
            K2, q = N_DEV * per, per // tk
            b_spec = pl.BlockSpec((None, tn, tk), lambda i, j, k: (k // q, j, k % q))
        else:
            N, K2 = b.shape
            b_spec = pl.BlockSpec((tn, tk), lambda i, j, k: (j, k))
    else:
        (K, M), (K2, N) = a.shape, b.shape
        dims = (((0,), (0,)), ((), ()))
        a_spec = pl.BlockSpec((tk, tm), lambda i, j, k: (k, i))
        b_spec = pl.BlockSpec((tk, tn), lambda i, j, k: (k, j))
    assert K == K2 and M % tm == 0 and N % tn == 0 and K % tk == 0, (name, a.shape, b.shape, tm, tn, tk)
    nk = K // tk
    o_spec = pl.BlockSpec((tm, tn), lambda i, j, k: (i, j))
    o_dims = (M, N)
    if out_sharded:
        qo = N // N_DEV // tn
        o_spec = pl.BlockSpec((None, tm, tn), lambda i, j, k: (j // qo, i, j % qo))
        o_dims = (N_DEV, M, N // N_DEV)
    n_out = 2 if epilogue == 'relu2' else 1
    n_extra = 1 if extra is not None else 0
    n_in = 2 + n_extra + len(deps)

    def body(*refs):
        a_ref, b_ref = refs[0], refs[1]
        x_refs = refs[2:2 + n_extra]
        o_refs = refs[n_in:n_in + n_out]
        acc = refs[-1]
        k = pl.program_id(2)

        @pl.when(k == 0)
        def _():
            acc[...] = jnp.zeros_like(acc)

        acc[...] += lax.dot_general(a_ref[...], b_ref[...], dims, preferred_element_type=F32)

        @pl.when(k == nk - 1)
        def _():
            r = acc[...]
            if epilogue == 'relu2':
                o_refs[0][...] = r.astype(BF16)
                o_refs[1][...] = jnp.square(jnp.maximum(r, 0.0)).astype(BF16)
            elif epilogue == 'drelu2':
                pre = x_refs[0][...].astype(F32)
                o_refs[0][...] = (r * (2.0 * jnp.maximum(pre, 0.0))).astype(out_dtype)
            else:
                o_refs[0][...] = r.astype(out_dtype)

    if epilogue == 'relu2':
        out_shape = (jax.ShapeDtypeStruct((M, N), BF16), jax.ShapeDtypeStruct((M, N), BF16))
        out_specs = (o_spec, o_spec)
    else:
        out_shape = jax.ShapeDtypeStruct(o_dims, out_dtype)
        out_specs = o_spec
    args = (a, b) + ((extra,) if extra is not None else ()) + tuple(deps)
    in_specs = ([a_spec, b_spec] + ([o_spec] if extra is not None else [])
                + [pl.BlockSpec(memory_space=pl.ANY)] * len(deps))
    return pl.pallas_call(
        body, out_shape=out_shape, grid=(M // tm, N // tn, nk), in_specs=in_specs, out_specs=out_specs,
        scratch_shapes=[pltpu.VMEM((tm, tn), F32)], name=name,
        compiler_params=_params(('parallel', 'parallel', 'arbitrary'), 48))(*args)


TR = 256


def _rowspec(w=D):
    return pl.BlockSpec((TR, w), lambda i: (i, 0))


def _prenorm_fwd(x, g, sc, sh):
    def body(x_ref, g_ref, sc_ref, sh_ref, h_ref):
        xv = x_ref[...]
        n = xv * _rms(xv)
        h_ref[...] = ((n * g_ref[...]) * (1.0 + sc_ref[...]) + sh_ref[...]).astype(BF16)

    return pl.pallas_call(
        body, out_shape=jax.ShapeDtypeStruct((L, D), BF16), grid=(L // TR,),
        in_specs=[_rowspec(), _vec(D), _vec(D), _vec(D)], out_specs=_rowspec(), name='prenorm_fwd',
        compiler_params=_params(('parallel',), 40))(x, g, sc, sh)


def _postmix_fwd(x, mix, gpm, gt1, gpl, sc2, sh2):
    def body(x_ref, mix_ref, gpm_ref, gt1_ref, gpl_ref, sc2_ref, sh2_ref, x1_ref, h2_ref):
        mix_v = mix_ref[...]
        nm = mix_v * _rms(mix_v)
        x1 = x_ref[...] + gt1_ref[...] * (nm * gpm_ref[...])
        x1_ref[...] = x1
        n2 = x1 * _rms(x1)
        h2_ref[...] = ((n2 * gpl_ref[...]) * (1.0 + sc2_ref[...]) + sh2_ref[...]).astype(BF16)

    return pl.pallas_call(
        body, out_shape=(jax.ShapeDtypeStruct((L, D), F32), jax.ShapeDtypeStruct((L, D), BF16)), grid=(L // TR,),
        in_specs=[_rowspec(), _rowspec()] + [_vec(D)] * 5, out_specs=(_rowspec(), _rowspec()), name='postmix_fwd',
        compiler_params=_params(('parallel',), 40))(x, mix, gpm, gt1, gpl, sc2, sh2)


def _final_fwd_bwd(x1, y, tgt, g, gt2):
    def body(x1_ref, y_ref, t_ref, g_ref, gt2_ref, dx2_ref, dy_ref, loss_ref, dgt2_ref, dg_ref):
        @pl.when(pl.program_id(0) == 0)
        def _():
            loss_ref[...] = jnp.zeros_like(loss_ref)
            dgt2_ref[...] = jnp.zeros_like(dgt2_ref)
            dg_ref[...] = jnp.zeros_like(dg_ref)

        yv = y_ref[...]
        r = _rms(yv)
        n = yv * r
        ng = n * g_ref[...]
        x2 = x1_ref[...] + gt2_ref[...] * ng
        e = x2 - t_ref[...]
        loss_ref[...] += 0.5 * jnp.sum(jnp.mean(e * e, axis=-1, keepdims=True), axis=0, keepdims=True)
        dx2 = e * (1.0 / D)
        dx2_ref[...] = dx2
        dgt2_ref[...] += _colsum(dx2 * ng)
        dng = dx2 * gt2_ref[...]
        dg_ref[...] += _colsum(dng * n)
        dy_ref[...] = _rms_bwd(dng * g_ref[...], n, r).astype(BF16)
        _fold8(pl.program_id(0) == L // TR - 1, dgt2_ref, dg_ref)

    return pl.pallas_call(
        body,
        out_shape=(jax.ShapeDtypeStruct((L, D), F32), jax.ShapeDtypeStruct((L, D), BF16),
                   jax.ShapeDtypeStruct((8, 128), F32), jax.ShapeDtypeStruct((8, D), F32),
                   jax.ShapeDtypeStruct((8, D), F32)),
        grid=(L // TR,), in_specs=[_rowspec(), _rowspec(), _rowspec(), _vec(D), _vec(D)],
        out_specs=(_rowspec(), _rowspec(), _vec8(128), _vec8(D), _vec8(D)), name='final_fwd_bwd',
        compiler_params=_params(('arbitrary',), 40))(x1, y, tgt, g, gt2)


def _postmix_bwd(dx2, dh2, x1, mix, gpm, gt1, gpl, sc2):
    def body(dx2_ref, dh2_ref, x1_ref, mix_ref, gpm_ref, gt1_ref, gpl_ref, sc2_ref,
             dx1_ref, dmix_ref, dsc2_ref, dsh2_ref, dgpl_ref, dgt1_ref, dgpm_ref):
        @pl.when(pl.program_id(0) == 0)
        def _():
            for r_ in (dsc2_ref, dsh2_ref, dgpl_ref, dgt1_ref, dgpm_ref):
                r_[...] = jnp.zeros_like(r_)

        x1v = x1_ref[...]
        r2 = _rms(x1v)
        n2 = x1v * r2
        dh2v = dh2_ref[...]
        dsh2_ref[...] += _colsum(dh2v)
        dsc2_ref[...] += _colsum(dh2v * (n2 * gpl_ref[...]))
        t = dh2v * (1.0 + sc2_ref[...])
        dgpl_ref[...] += _colsum(t * n2)
        dx1 = dx2_ref[...] + _rms_bwd(t * gpl_ref[...], n2, r2)
        dx1_ref[...] = dx1
        mix_v = mix_ref[...]
        rm = _rms(mix_v)
        nm = mix_v * rm
        dgt1_ref[...] += _colsum(dx1 * (nm * gpm_ref[...]))
        u = dx1 * gt1_ref[...]
        dgpm_ref[...] += _colsum(u * nm)
        dmix_ref[...] = _rms_bwd(u * gpm_ref[...], nm, rm).astype(BF16)
        _fold8(pl.program_id(0) == L // TR - 1, dsc2_ref, dsh2_ref, dgpl_ref, dgt1_ref, dgpm_ref)

    vs = jax.ShapeDtypeStruct((8, D), F32)
    return pl.pallas_call(
        body, out_shape=(jax.ShapeDtypeStruct((L, D), F32), jax.ShapeDtypeStruct((L, D), BF16), vs, vs, vs, vs, vs),
        grid=(L // TR,), in_specs=[_rowspec()] * 4 + [_vec(D)] * 4,
        out_specs=(_rowspec(), _rowspec()) + (_vec8(D),) * 5, name='postmix_bwd',
        compiler_params=_params(('arbitrary',), 48))(dx2, dh2, x1, mix, gpm, gt1, gpl, sc2)


def _prenorm_bwd(dx1, dh1, x, g, sc1):
    def body(dx1_ref, dh1_ref, x_ref, g_ref, sc1_ref, dx_ref, dsc1_ref, dsh1_ref, dg_ref):
        @pl.when(pl.program_id(0) == 0)
        def _():
            for r_ in (dsc1_ref, dsh1_ref, dg_ref):
                r_[...] = jnp.zeros_like(r_)

        xv = x_ref[...]
        r = _rms(xv)
        n = xv * r
        dh = dh1_ref[...]
        dsh1_ref[...] += _colsum(dh)
        dsc1_ref[...] += _colsum(dh * (n * g_ref[...]))
        t = dh * (1.0 + sc1_ref[...])
        dg_ref[...] += _colsum(t * n)
        dx_ref[...] = dx1_ref[...] + _rms_bwd(t * g_ref[...], n, r)
        _fold8(pl.program_id(0) == L // TR - 1, dsc1_ref, dsh1_ref, dg_ref)

    vs = jax.ShapeDtypeStruct((8, D), F32)
    return pl.pallas_call(
        body, out_shape=(jax.ShapeDtypeStruct((L, D), F32), vs, vs, vs), grid=(L // TR,),
        in_specs=[_rowspec()] * 3 + [_vec(D)] * 2, out_specs=(_rowspec(),) + (_vec8(D),) * 3, name='prenorm_bwd',
        compiler_params=_params(('arbitrary',), 40))(dx1, dh1, x, g, sc1)


ROPE_W = QW + KVW


def _rope(xin, pos, fr, sign, name):
    def body(x_ref, pos_ref, fr_ref, o_ref):
        xv = x_ref[...]
        ang = pos_ref[...].astype(F32) * fr_ref[...]
        w = lax.broadcasted_iota(jnp.int32, (1, 128), 1) % HEAD_DIM
        cs = jnp.cos(ang)
        sn = jnp.sin(ang) * sign
        s1 = jnp.where(w < ROT_DIM // 2, -sn, 0.0)
        s2 = jnp.where((w >= ROT_DIM // 2) & (w < ROT_DIM), sn, 0.0)
        rep = ROPE_W // 128
        cs, s1, s2 = jnp.tile(cs, (1, rep)), jnp.tile(s1, (1, rep)), jnp.tile(s2, (1, rep))
        hi = pltpu.roll(xv, ROPE_W - ROT_DIM // 2, 1)
        lo = pltpu.roll(xv, ROT_DIM // 2, 1)
        o_ref[...] = (xv * cs + hi * s1 + lo * s2).astype(BF16)

    return pl.pallas_call(
        body, out_shape=jax.ShapeDtypeStruct((L, ROPE_W), BF16), grid=(L // TR,),
        in_specs=[_rowspec(ROPE_W), pl.BlockSpec((TR, 1), lambda i: (i, 0)), _vec(128)],
        out_specs=_rowspec(ROPE_W), name=name, compiler_params=_params(('parallel',), 40))(xin, pos, fr)


def _attn_mask(g, b):
    nbs = lax.shift_right_logical(jnp.int32(NBLK), 2 * g)
    first = (b & (nbs - 1)) == 0
    qi = lax.broadcasted_iota(jnp.int32, (BLK, 2 * BLK), 0)
    kj = lax.broadcasted_iota(jnp.int32, (BLK, 2 * BLK), 1)
    dist = qi + BLK - kj
    return (dist >= 0) & (dist <= BLK) & ((kj >= BLK) | jnp.logical_not(first))


def _blk(idx):
    return pl.BlockSpec((None, BLK, KVW), idx)


_CUR = lambda g, b: (g, b, 0)
_PREV = lambda g, b: (g, jnp.maximum(b - 1, 0), 0)
NEG = -1e30
NT_DIMS = (((1,), (1,)), ((), ()))
TN_DIMS = (((0,), (0,)), ((), ()))


def _attn_fwd(qp, kp, vp):
    def body(q_ref, kp_ref, kc_ref, vp_ref, vc_ref, o_ref, lse_ref):
        valid = _attn_mask(pl.program_id(0), pl.program_id(1))
        for h in range(HEADS):
            hs = slice(h * HEAD_DIM, (h + 1) * HEAD_DIM)
            q = q_ref[:, hs]
            kc = jnp.concatenate([kp_ref[:, hs], kc_ref[:, hs]], axis=0)
            vc = jnp.concatenate([vp_ref[:, hs], vc_ref[:, hs]], axis=0)
            s = lax.dot_general(q, kc, NT_DIMS, preferred_element_type=F32) * 0.125
            s = jnp.where(valid, s, NEG)
            m = jnp.max(s, axis=-1, keepdims=True)
            p = jnp.exp(s - m)
            l = jnp.sum(p, axis=-1, keepdims=True)
            o = jnp.dot(p.astype(BF16), vc, preferred_element_type=F32) / l
            o_ref[:, hs] = o
            lse_ref[:, hs] = jnp.broadcast_to(m + jnp.log(l), (BLK, HEAD_DIM))

    sh = jax.ShapeDtypeStruct((N_GROUPS, L, KVW), F32)
    return pl.pallas_call(
        body, out_shape=(sh, sh), grid=(N_GROUPS, NBLK),
        in_specs=[_blk(_CUR), _blk(_PREV), _blk(_CUR), _blk(_PREV), _blk(_CUR)],
        out_specs=(_blk(_CUR), _blk(_CUR)), name='attn_fwd',
        compiler_params=_params(('parallel', 'parallel'), 32))(qp, kp, kp, vp, vp)


def _attn_bwd(qp, kp, vp, o, lse, do, dlse):
    def body(q_ref, kp_ref, kc_ref, vp_ref, vc_ref, o_ref, lse_ref, do_ref, dlse_ref, dq_ref, dk_ref, dv_ref):
        b = pl.program_id(1)

        @pl.when(b == 0)
        def _():
            dk_ref[...] = jnp.zeros_like(dk_ref)
            dv_ref[...] = jnp.zeros_like(dv_ref)

        valid = _attn_mask(pl.program_id(0), b)
        prev0 = pl.multiple_of(jnp.maximum(b - 1, 0) * BLK, BLK)
        cur0 = pl.multiple_of(b * BLK, BLK)
        for h in range(HEADS):
            hs = slice(h * HEAD_DIM, (h + 1) * HEAD_DIM)
            q = q_ref[:, hs]
            kc = jnp.concatenate([kp_ref[:, hs], kc_ref[:, hs]], axis=0)
            vc = jnp.concatenate([vp_ref[:, hs], vc_ref[:, hs]], axis=0)
            s = lax.dot_general(q, kc, NT_DIMS, preferred_element_type=F32) * 0.125
            s = jnp.where(valid, s, NEG)
            p = jnp.exp(s - lse_ref[:, h * HEAD_DIM:h * HEAD_DIM + 1])
            do_h = do_ref[:, hs]
            delta = jnp.sum(do_h * o_ref[:, hs], axis=-1, keepdims=True)
            do_b = do_h.astype(BF16)
            dp = lax.dot_general(do_b, vc, NT_DIMS, preferred_element_type=F32)
            ds = p * (dp - delta + dlse_ref[:, h * HEAD_DIM:h * HEAD_DIM + 1])
            ds_b = (ds * 0.125).astype(BF16)
            dq_ref[:, hs] = jnp.dot(ds_b, kc, preferred_element_type=F32)
            dkc = lax.dot_general(ds_b, q, TN_DIMS, preferred_element_type=F32)
            dvc = lax.dot_general(p.astype(BF16), do_b, TN_DIMS, preferred_element_type=F32)
            dk_ref[pl.ds(prev0, BLK), hs] += dkc[:BLK]
            dv_ref[pl.ds(prev0, BLK), hs] += dvc[:BLK]
            dk_ref[pl.ds(cur0, BLK), hs] += dkc[BLK:]
            dv_ref[pl.ds(cur0, BLK), hs] += dvc[BLK:]

    sh = jax.ShapeDtypeStruct((N_GROUPS, L, KVW), F32)
    whole = pl.BlockSpec((None, L, KVW), lambda g, b: (g, 0, 0))
    return pl.pallas_call(
        body, out_shape=(sh, sh, sh), grid=(N_GROUPS, NBLK),
        in_specs=[_blk(_CUR), _blk(_PREV), _blk(_CUR), _blk(_PREV), _blk(_CUR)] + [_blk(_CUR)] * 4,
        out_specs=(_blk(_CUR), whole, whole), name='attn_bwd',
        compiler_params=_params(('arbitrary', 'arbitrary'), 48))(qp, kp, kp, vp, vp, o, lse, do, dlse)


TC = 512


def _comb_spec():
    return pl.BlockSpec((N_GROUPS, TC, KVW), lambda i: (0, i, 0))


def _combine_weights(lse_ref):
    l0, l1, l2 = lse_ref[0], lse_ref[1], lse_ref[2]
    m = jnp.maximum(jnp.maximum(l0, l1), l2)
    e0, e1, e2 = jnp.exp(l0 - m), jnp.exp(l1 - m), jnp.exp(l2 - m)
    z = e0 + e1 + e2
    return e0 / z, e1 / z, e2 / z


def _combine_fwd(o3, lse3, g):
    def body(o_ref, lse_ref, g_ref, att_ref):
        w0, w1, w2 = _combine_weights(lse_ref)
        a = w0 * o_ref[0] + w1 * o_ref[1] + w2 * o_ref[2]
        att_ref[...] = ((a * _rms(a)) * g_ref[...]).astype(BF16)

    return pl.pallas_call(
        body, out_shape=jax.ShapeDtypeStruct((L, KVW), BF16), grid=(L // TC,),
        in_specs=[_comb_spec(), _comb_spec(), _vec(KVW)], out_specs=pl.BlockSpec((TC, KVW), lambda i: (i, 0)),
        name='combine_fwd', compiler_params=_params(('parallel',), 40))(o3, lse3, g)


def _combine_bwd(datt, o3, lse3, g, head_ones):
    def body(datt_ref, o_ref, lse_ref, g_ref, e_ref, do_ref, dlse_ref, dg_ref):
        @pl.when(pl.program_id(0) == 0)
        def _():
            dg_ref[...] = jnp.zeros_like(dg_ref)

        ws = _combine_weights(lse_ref)
        a = ws[0] * o_ref[0] + ws[1] * o_ref[1] + ws[2] * o_ref[2]
        r = _rms(a)
        n = a * r
        dv = datt_ref[...]
        dg_ref[...] += _colsum(dv * n)
        da = _rms_bwd(dv * g_ref[...], n, r)
        dws = [jnp.dot(da * o_ref[i], e_ref[...], preferred_element_type=F32, precision=HI) for i in range(3)]
        dbar = ws[0] * dws[0] + ws[1] * dws[1] + ws[2] * dws[2]
        for i in range(3):
            do_ref[i] = ws[i] * da
            dlse_ref[i] = ws[i] * (dws[i] - dbar)
        _fold8(pl.program_id(0) == L // TC - 1, dg_ref)

    sh = jax.ShapeDtypeStruct((N_GROUPS, L, KVW), F32)
    return pl.pallas_call(
        body, out_shape=(sh, sh, jax.ShapeDtypeStruct((8, KVW), F32)), grid=(L // TC,),
        in_specs=[pl.BlockSpec((TC, KVW), lambda i: (i, 0)), _comb_spec(), _comb_spec(), _vec(KVW),
                  pl.BlockSpec((KVW, KVW), lambda i: (0, 0))],
        out_specs=(_comb_spec(), _comb_spec(), _vec8(KVW)), name='combine_bwd',
        compiler_params=_params(('arbitrary',), 48))(datt, o3, lse3, g, head_ones)


def _ssm_disc(ar, ai, ldt):
    dt = jnp.exp(ldt)
    zr, zi = ar * dt, ai * dt
    ez = jnp.exp(zr)
    A_r, A_i = ez * jnp.cos(zi), ez * jnp.sin(zi)
    den = ar * ar + ai * ai
    xr, xi = A_r - 1.0, A_i
    cr = (xr * ar + xi * ai) / den
    ci = (xi * ar - xr * ai) / den
    return dt, zr, zi, A_r, A_i, den, cr, ci


def _ssm_pre(ar, ai, ldt, br, bi):
    def body(ar_ref, ai_ref, ldt_ref, br_ref, bi_ref, bbr_ref, bbi_ref, pwr_ref, pwi_ref):
        _, zr, zi, _, _, _, cr, ci = _ssm_disc(ar_ref[...], ai_ref[...], ldt_ref[...])
        bbr_ref[...] = cr * br_ref[...] - ci * bi_ref[...]
        bbi_ref[...] = cr * bi_ref[...] + ci * br_ref[...]
        k = (lax.broadcasted_iota(jnp.int32, (1, 8), 1) + 1).astype(F32)
        ek = jnp.exp(zr * k)
        pwr_ref[...] = ek * jnp.cos(zi * k)
        pwi_ref[...] = ek * jnp.sin(zi * k)

    s16 = jax.ShapeDtypeStruct((SSM_GN, SSM_P), F32)
    s8 = jax.ShapeDtypeStruct((SSM_GN, 8), F32)
    return pl.pallas_call(body, out_shape=(s16, s16, s8, s8), name='ssm_pre',
                          compiler_params=_params(None, 40))(ar, ai, ldt, br, bi)


def _ssm_post(ar, ai, ldt, br, bi, gar, gai, gbr, gbi, sel):
    def body(ar_ref, ai_ref, ldt_ref, br_ref, bi_ref, gar_ref, gai_ref, gbr_ref, gbi_ref, sel_ref,
             dar_ref, dai_ref, dbr_ref, dbi_ref, dldt_ref):
        a_r, a_i = ar_ref[...], ai_ref[...]
        dt, _, _, A_r, A_i, den, cr, ci = _ssm_disc(a_r, a_i, ldt_ref[...])
        b_r, b_i, g_br, g_bi = br_ref[...], bi_ref[...], gbr_ref[...], gbi_ref[...]
        gcr = jnp.sum(g_br * b_r + g_bi * b_i, axis=-1, keepdims=True)
        gci = jnp.sum(g_bi * b_r - g_br * b_i, axis=-1, keepdims=True)
        dbr_ref[...] = g_br * cr + g_bi * ci
        dbi_ref[...] = g_bi * cr - g_br * ci
        g_ar = gar_ref[...] + (gcr * a_r - gci * a_i) / den
        g_ai = gai_ref[...] + (gcr * a_i + gci * a_r) / den
        qr = (cr * a_r + ci * a_i) / den
        qi = (ci * a_r - cr * a_i) / den
        glr = -(gcr * qr + gci * qi)
        gli = -(gci * qr - gcr * qi)
        gzr = g_ar * A_r + g_ai * A_i
        gzi = g_ai * A_r - g_ar * A_i
        dar_ref[...] = glr + gzr * dt
        dai_ref[...] = gli + gzi * dt
        gdt = (gzr * a_r + gzi * a_i) * dt
        dldt_ref[...] = jnp.dot(sel_ref[...], jnp.broadcast_to(gdt, (SSM_GN, 128)),
                                preferred_element_type=F32, precision=HI)

    s1 = jax.ShapeDtypeStruct((SSM_GN, 1), F32)
    s16 = jax.ShapeDtypeStruct((SSM_GN, SSM_P), F32)
    return pl.pallas_call(body, out_shape=(s1, s1, s16, s16, jax.ShapeDtypeStruct((SSM_G, 128), F32)),
                          name='ssm_post', compiler_params=_params(None, 48))(
                              ar, ai, ldt, br, bi, gar, gai, gbr, gbi, sel)


SCAN_CH = 8


def _scan_fwd_tiles(s_ref, pw, carry):
    pwr, pwi = pw[:, :CL_S], pw[:, CL_S:]
    row = lax.broadcasted_iota(jnp.int32, (SCAN_CH, 8, CL_S), 1)
    steps = [(k, pwr[k - 1:k], pwi[k - 1:k]) for k in (1, 2, 4)]
    rows = 8 * SCAN_CH

    def chunk(i, c):
        cr, ci = c
        r0 = pl.multiple_of(i * rows, rows)
        xr = s_ref[pl.ds(r0, rows), 0:CL_S].reshape(SCAN_CH, 8, CL_S)
        xi = s_ref[pl.ds(r0, rows), CL_S:2 * CL_S].reshape(SCAN_CH, 8, CL_S)
        for k, pr, pi in steps:
            sr = jnp.where(row >= k, pltpu.roll(xr, k, 1), 0.0)
            si = jnp.where(row >= k, pltpu.roll(xi, k, 1), 0.0)
            xr, xi = xr + pr * sr - pi * si, xi + pr * si + pi * sr
        for j in range(SCAN_CH):
            tr = xr[j] + pwr * cr - pwi * ci
            ti = xi[j] + pwr * ci + pwi * cr
            s_ref[pl.ds(r0 + 8 * j, 8), 0:CL_S] = tr
            s_ref[pl.ds(r0 + 8 * j, 8), CL_S:2 * CL_S] = ti
            cr, ci = tr[7:8], ti[7:8]
        return cr, ci

    return lax.fori_loop(0, T_SCAN // rows, chunk, (carry[:, :CL_S], carry[:, CL_S:]))


def _scan_bwd_tiles(l_ref, pw, carry):
    pwr, pwi = pw[:, :CL_S], pw[:, CL_S:]
    rpr = jnp.concatenate([pwr[7 - r:8 - r] for r in range(8)], axis=0)
    rpi = jnp.concatenate([pwi[7 - r:8 - r] for r in range(8)], axis=0)
    row = lax.broadcasted_iota(jnp.int32, (SCAN_CH, 8, CL_S), 1)
    steps = [(k, pwr[k - 1:k], pwi[k - 1:k]) for k in (1, 2, 4)]
    rows = 8 * SCAN_CH
    nc = T_SCAN // rows

    def chunk(i, c):
        cr, ci = c
        r0 = pl.multiple_of((nc - 1 - i) * rows, rows)
        xr = l_ref[pl.ds(r0, rows), 0:CL_S].reshape(SCAN_CH, 8, CL_S)
        xi = l_ref[pl.ds(r0, rows), CL_S:2 * CL_S].reshape(SCAN_CH, 8, CL_S)
        for k, pr, pi in steps:
            sr = jnp.where(row < 8 - k, pltpu.roll(xr, 8 - k, 1), 0.0)
            si = jnp.where(row < 8 - k, pltpu.roll(xi, 8 - k, 1), 0.0)
            xr, xi = xr + pr * sr + pi * si, xi + pr * si - pi * sr
        for j in reversed(range(SCAN_CH)):
            tr = xr[j] + rpr * cr + rpi * ci
            ti = xi[j] + rpr * ci - rpi * cr
            l_ref[pl.ds(r0 + 8 * j, 8), 0:CL_S] = tr
            l_ref[pl.ds(r0 + 8 * j, 8), CL_S:2 * CL_S] = ti
            cr, ci = tr[0:1], ti[0:1]
        return cr, ci

    return lax.fori_loop(0, nc, chunk, (carry[:, :CL_S], carry[:, CL_S:]))


NT_SCAN = L // T_SCAN


def _hilo(t):
    hi = t.astype(BF16)
    return jnp.stack([hi, (t - hi.astype(F32)).astype(BF16)], axis=1)


def _dot3(a, b_ref):
    ah = a.astype(BF16)
    al = (a - ah.astype(F32)).astype(BF16)
    bh, bl = b_ref[0], b_ref[1]
    return (jnp.dot(ah, bh, preferred_element_type=F32) + jnp.dot(al, bh, preferred_element_type=F32)
            + jnp.dot(ah, bl, preferred_element_type=F32))


def _hl_spec(r, c):
    return pl.BlockSpec((None, 2, r, c), lambda c_, t: (c_, 0, 0, 0))


def _ssm_fwd(u, bm2, cm2, pw, dvec):
    def body(u_ref, bm_ref, cm_ref, pw_ref, d_ref, y_ref, bnd_ref, s_ref, carry_ref):
        @pl.when(pl.program_id(1) == 0)
        def _():
            carry_ref[...] = jnp.zeros_like(carry_ref)

        bnd_ref[...] = carry_ref[...]
        uv = u_ref[...]
        s_ref[...] = _dot3(uv, bm_ref)
        cr, ci = _scan_fwd_tiles(s_ref, pw_ref[...], carry_ref[...])
        carry_ref[...] = jnp.concatenate([cr, ci], axis=1)
        y_ref[...] = _dot3(s_ref[...], cm_ref) + d_ref[...] * uv

    return pl.pallas_call(
        body,
        out_shape=(jax.ShapeDtypeStruct((L, SSMW), F32), jax.ShapeDtypeStruct((N_CL, NT_SCAN, 1, 2 * CL_S), F32)),
        grid=(N_CL, NT_SCAN),
        in_specs=[pl.BlockSpec((T_SCAN, CL_U), lambda c, t: (t, c)),
                  _hl_spec(CL_U, 2 * CL_S), _hl_spec(2 * CL_S, CL_U),
                  pl.BlockSpec((None, 8, 2 * CL_S), lambda c, t: (c, 0, 0)),
                  pl.BlockSpec((1, CL_U), lambda c, t: (0, c))],
        out_specs=(pl.BlockSpec((T_SCAN, CL_U), lambda c, t: (t, c)),
                   pl.BlockSpec((None, None, 1, 2 * CL_S), lambda c, t: (c, t, 0, 0))),
        scratch_shapes=[pltpu.VMEM((T_SCAN, 2 * CL_S), F32), pltpu.VMEM((1, 2 * CL_S), F32)],
        name='ssm_fwd', compiler_params=_params(('arbitrary', 'arbitrary'), 40))(u, bm2, cm2, pw, dvec)


def _ssm_bwd(u, dy, bm2, bmt, cmt, pw, dvec, bnd):
    rev = lambda t: NT_SCAN - 1 - t

    def body(u_ref, dy_ref, bm_ref, bmt_ref, cmt_ref, pw_ref, d_ref, bnd_ref,
             du_ref, dbm_ref, dcm_ref, da_ref, dd_ref, s_ref, l_ref, carry_ref):
        @pl.when(pl.program_id(1) == 0)
        def _():
            carry_ref[...] = jnp.zeros_like(carry_ref)
            dbm_ref[...] = jnp.zeros_like(dbm_ref)
            dcm_ref[...] = jnp.zeros_like(dcm_ref)
            da_ref[...] = jnp.zeros_like(da_ref)
            dd_ref[...] = jnp.zeros_like(dd_ref)

        uv, dyv, pw = u_ref[...], dy_ref[...], pw_ref[...]
        dy_b = dyv.astype(BF16)
        entry = bnd_ref[...]
        s_ref[...] = _dot3(uv, bm_ref)
        _scan_fwd_tiles(s_ref, pw, entry)
        l_ref[...] = jnp.dot(dy_b, cmt_ref[...], preferred_element_type=F32)
        cr, ci = _scan_bwd_tiles(l_ref, pw, carry_ref[...])
        carry_ref[...] = jnp.concatenate([cr, ci], axis=1)
        sv, lv = s_ref[...], l_ref[...]
        lv_b = lv.astype(BF16)
        du_ref[...] = dyv * d_ref[...] + jnp.dot(lv_b, bmt_ref[...], preferred_element_type=F32)
        dbm_ref[...] += lax.dot_general(uv.astype(BF16), lv_b, TN_DIMS, preferred_element_type=F32)
        dcm_ref[...] += lax.dot_general(sv.astype(BF16), dy_b, TN_DIMS, preferred_element_type=F32)
        dd_ref[...] += _colsum(dyv * uv)
        row = lax.broadcasted_iota(jnp.int32, (T_SCAN, 2 * CL_S), 0)
        sp = jnp.where(row == 0, entry, pltpu.roll(sv, 1, 0))
        spr, spi = sp[:, :CL_S], sp[:, CL_S:]
        lr, li = lv[:, :CL_S], lv[:, CL_S:]
        da_ref[:, 0:CL_S] += _colsum(lr * spr + li * spi)
        da_ref[:, CL_S:2 * CL_S] += _colsum(li * spr - lr * spi)
        _fold8(pl.program_id(1) == NT_SCAN - 1, da_ref, dd_ref)

    return pl.pallas_call(
        body,
        out_shape=(jax.ShapeDtypeStruct((L, SSMW), F32), jax.ShapeDtypeStruct((N_CL, CL_U, 2 * CL_S), F32),
                   jax.ShapeDtypeStruct((N_CL, 2 * CL_S, CL_U), F32), jax.ShapeDtypeStruct((N_CL, 8, 2 * CL_S), F32),
                   jax.ShapeDtypeStruct((8, SSMW), F32)),
        grid=(N_CL, NT_SCAN),
        in_specs=[pl.BlockSpec((T_SCAN, CL_U), lambda c, t: (rev(t), c)),
                  pl.BlockSpec((T_SCAN, CL_U), lambda c, t: (rev(t), c)),
                  _hl_spec(CL_U, 2 * CL_S),
                  pl.BlockSpec((None, 2 * CL_S, CL_U), lambda c, t: (c, 0, 0)),
                  pl.BlockSpec((None, CL_U, 2 * CL_S), lambda c, t: (c, 0, 0)),
                  pl.BlockSpec((None, 8, 2 * CL_S), lambda c, t: (c, 0, 0)),
                  pl.BlockSpec((1, CL_U), lambda c, t: (0, c)),
                  pl.BlockSpec((None, None, 1, 2 * CL_S), lambda c, t: (c, rev(t), 0, 0))],
        out_specs=(pl.BlockSpec((T_SCAN, CL_U), lambda c, t: (rev(t), c)),
                   pl.BlockSpec((None, CL_U, 2 * CL_S), lambda c, t: (c, 0, 0)),
                   pl.BlockSpec((None, 2 * CL_S, CL_U), lambda c, t: (c, 0, 0)),
                   pl.BlockSpec((None, 8, 2 * CL_S), lambda c, t: (c, 0, 0)),
                   pl.BlockSpec((8, CL_U), lambda c, t: (0, c))),
        scratch_shapes=[pltpu.VMEM((T_SCAN, 2 * CL_S), F32), pltpu.VMEM((T_SCAN, 2 * CL_S), F32),
                        pltpu.VMEM((1, 2 * CL_S), F32)],
        name='ssm_bwd', compiler_params=_params(('arbitrary', 'arbitrary'), 48))(u, dy, bm2, bmt, cmt, pw, dvec, bnd)


GELU_C = math.sqrt(2.0 / math.pi)
GELU_K = 0.044715


def _gelu_parts(x):
    t = jnp.tanh(GELU_C * (x + GELU_K * (x * x * x)))
    return x * (0.5 * (1.0 + t)), t


def _glu_fwd(ypre, wglu, bglu, gs):
    def body(y_ref, w_ref, b_ref, g_ref, o_ref):
        yg, _ = _gelu_parts(y_ref[...])
        z = jnp.dot(yg.astype(BF16), w_ref[...], preferred_element_type=F32) + b_ref[...]
        s = yg * jax.nn.sigmoid(z)
        o_ref[...] = ((s * _rms(s)) * g_ref[...]).astype(BF16)

    return pl.pallas_call(
        body, out_shape=jax.ShapeDtypeStruct((L, SSMW), BF16), grid=(L // TR,),
        in_specs=[_rowspec(SSMW), pl.BlockSpec((SSMW, SSMW), lambda i: (0, 0)), _vec(SSMW), _vec(SSMW)],
        out_specs=_rowspec(SSMW), name='glu_fwd', compiler_params=_params(('parallel',), 32))(ypre, wglu, bglu, gs)


def _glu_bwd(ypre, dsn, wglu, bglu, gs):
    def body(y_ref, d_ref, w_ref, b_ref, g_ref, dy_ref, dw_ref, db_ref, dg_ref):
        @pl.when(pl.program_id(0) == 0)
        def _():
            dw_ref[...] = jnp.zeros_like(dw_ref)
            db_ref[...] = jnp.zeros_like(db_ref)
            dg_ref[...] = jnp.zeros_like(dg_ref)

        xv = y_ref[...]
        yg, t = _gelu_parts(xv)
        yg_b = yg.astype(BF16)
        z = jnp.dot(yg_b, w_ref[...], preferred_element_type=F32) + b_ref[...]
        sg = jax.nn.sigmoid(z)
        s = yg * sg
        r = _rms(s)
        n = s * r
        dv = d_ref[...]
        dg_ref[...] += _colsum(dv * n)
        ds = _rms_bwd(dv * g_ref[...], n, r)
        dz = (ds * yg) * (sg * (1.0 - sg))
        dz_b = dz.astype(BF16)
        db_ref[...] += _colsum(dz)
        dw_ref[...] += lax.dot_general(yg_b, dz_b, TN_DIMS, preferred_element_type=F32)
        dyg = ds * sg + lax.dot_general(dz_b, w_ref[...], NT_DIMS, preferred_element_type=F32)
        dgelu = 0.5 * (1.0 + t) + (0.5 * xv) * (1.0 - t * t) * (GELU_C * (1.0 + 3.0 * GELU_K * (xv * xv)))
        dy_ref[...] = dyg * dgelu
        _fold8(pl.program_id(0) == L // TR - 1, db_ref, dg_ref)

    vs = jax.ShapeDtypeStruct((8, SSMW), F32)
    return pl.pallas_call(
        body, out_shape=(jax.ShapeDtypeStruct((L, SSMW), F32), jax.ShapeDtypeStruct((SSMW, SSMW), F32), vs, vs),
        grid=(L // TR,),
        in_specs=[_rowspec(SSMW), _rowspec(SSMW), pl.BlockSpec((SSMW, SSMW), lambda i: (0, 0)), _vec(SSMW), _vec(SSMW)],
        out_specs=(_rowspec(SSMW), pl.BlockSpec((SSMW, SSMW), lambda i: (0, 0)), _vec8(SSMW), _vec8(SSMW)),
        name='glu_bwd', compiler_params=_params(('arbitrary',), 40))(ypre, dsn, wglu, bglu, gs)


def _me():
    return lax.axis_index('x'), lax.axis_index('y'), lax.axis_index('c')


def _my_index():
    return 4 * lax.axis_index('x') + 2 * lax.axis_index('y') + lax.axis_index('c')


def _peer(k):
    x, y, c = _me()
    px = 1 - x if k & 4 else x
    py = 1 - y if k & 2 else y
    pc = 1 - c if k & 1 else c
    return (px, py, pc), 4 * px + 2 * py + pc


def _mod_exchange(c_row, w_ada, b_ada8):
    cw = NMOD * D // N_DEV

    def body(c_ref, w_ref, b_ref, call_ref, mod_ref, part_ref, send_sems, recv_sems):
        x, y, c = _me()
        me = 4 * x + 2 * y + c
        call_ref[me] = c_ref[0]
        sends = []
        for k in range(1, N_DEV):
            peer, _ = _peer(k)
            cp = pltpu.make_async_remote_copy(src_ref=c_ref.at[0], dst_ref=call_ref.at[me], send_sem=send_sems.at[0, k - 1],
                                              recv_sem=recv_sems.at[0, k - 1], device_id=peer, device_id_type=MESH)
            cp.start()
            sends.append(cp)
        for k in range(1, N_DEV):
            peer, pidx = _peer(k)
            pltpu.make_async_remote_copy(src_ref=c_ref.at[0], dst_ref=call_ref.at[pidx], send_sem=send_sems.at[0, k - 1],
                                         recv_sem=recv_sems.at[0, k - 1], device_id=peer, device_id_type=MESH).wait_recv()
        for cp in sends:
            cp.wait_send()
        cv = call_ref[...].reshape(N_DEV, D)
        part = jnp.dot(cv * jax.nn.sigmoid(cv), w_ref[...], preferred_element_type=F32, precision=HI)
        part_ref[...] = part.reshape(N_DEV, 1, cw)
        mod_ref[me] = part_ref[me]
        sends = []
        for k in range(1, N_DEV):
            peer, pidx = _peer(k)
            cp = pltpu.make_async_remote_copy(src_ref=part_ref.at[pidx], dst_ref=mod_ref.at[me], send_sem=send_sems.at[1, k - 1],
                                              recv_sem=recv_sems.at[1, k - 1], device_id=peer, device_id_type=MESH)
            cp.start()
            sends.append(cp)
        for k in range(1, N_DEV):
            peer, pidx = _peer(k)
            pltpu.make_async_remote_copy(src_ref=part_ref.at[pidx], dst_ref=mod_ref.at[pidx], send_sem=send_sems.at[1, k - 1],
                                         recv_sem=recv_sems.at[1, k - 1], device_id=peer, device_id_type=MESH).wait_recv()
        for cp in sends:
            cp.wait_send()
        mod_ref[...] = mod_ref[...] + b_ref[...]

    vm = pl.BlockSpec(memory_space=pltpu.VMEM)
    return pl.pallas_call(
        body, out_shape=(jax.ShapeDtypeStruct((N_DEV, 1, D), F32), jax.ShapeDtypeStruct((N_DEV, 1, cw), F32)),
        in_specs=[vm, vm, vm], out_specs=(vm, vm),
        scratch_shapes=[pltpu.VMEM((N_DEV, 1, cw), F32), pltpu.SemaphoreType.DMA((2, N_DEV - 1)),
                        pltpu.SemaphoreType.DMA((2, N_DEV - 1))],
        name='mod_exchange', compiler_params=_params(None, 48))(c_row, w_ada, b_ada8)


HBM_SPEC = pl.BlockSpec(memory_space=pltpu.HBM)
SEM_SPEC = pl.BlockSpec(memory_space=pltpu.SEMAPHORE)
DATAFLOW = pltpu.SideEffectType.DATAFLOW_SIDE_EFFECTING


def _push_start(src, scatter, after, name):
    land = lax.empty(src.shape if scatter else (N_DEV,) + src.shape, src.dtype)

    def body(src_ref, land_ref, after_ref, send_sem, recv_sem, src_thru, land_thru, token):
        x, y, c = _me()
        me = 4 * x + 2 * y + c
        for k in range(1, N_DEV):
            peer, pidx = _peer(k)
            pltpu.make_async_remote_copy(src_ref=src_ref.at[pidx] if scatter else src_ref, dst_ref=land_ref.at[me],
                                         send_sem=send_sem, recv_sem=recv_sem, device_id=peer,
                                         device_id_type=MESH).start()
        token[...] = jnp.zeros_like(token)

    outs = pl.pallas_call(
        body, name=name,
        out_shape=(pltpu.SemaphoreType.DMA(()), pltpu.SemaphoreType.DMA(()), pltpu.HBM(src.shape, src.dtype),
                   pltpu.HBM(land.shape, land.dtype), jax.ShapeDtypeStruct((8, 128), F32)),
        in_specs=(HBM_SPEC, HBM_SPEC, pl.BlockSpec(memory_space=pl.ANY)),
        out_specs=(SEM_SPEC, SEM_SPEC, HBM_SPEC, HBM_SPEC, pl.BlockSpec(memory_space=pltpu.VMEM)),
        input_output_aliases={0: 2, 1: 3}, compiler_params=pltpu.CompilerParams(has_side_effects=DATAFLOW),
    )(pltpu.with_memory_space_constraint(src, pltpu.HBM), pltpu.with_memory_space_constraint(land, pltpu.HBM), after)
    own = lax.dynamic_index_in_dim(src, _my_index(), 0, keepdims=False) if scatter else src
    return (*outs, own)


def _push_wait(handle, after, name):
    send_sem, recv_sem, src_thru, land_thru, _, own = handle

    def body(src_ref, land_ref, send_sem, recv_sem, after_ref, src_dead, got_ref):
        seven = land_ref.at[pl.ds(0, N_DEV - 1)]
        cp = pltpu.make_async_remote_copy(src_ref=seven, dst_ref=seven, send_sem=send_sem, recv_sem=recv_sem,
                                          device_id=_me(), device_id_type=MESH)
        cp.wait_send()
        cp.wait_recv()

    landed = pl.pallas_call(
        body, name=name,
        out_shape=(pltpu.HBM(src_thru.shape, src_thru.dtype), pltpu.HBM(land_thru.shape, land_thru.dtype)),
        in_specs=(HBM_SPEC, HBM_SPEC, SEM_SPEC, SEM_SPEC, pl.BlockSpec(memory_space=pl.ANY)),
        out_specs=(HBM_SPEC, HBM_SPEC), input_output_aliases={0: 0, 1: 1},
        compiler_params=pltpu.CompilerParams(has_side_effects=DATAFLOW),
    )(src_thru, land_thru, send_sem, recv_sem, after)[1]
    return lax.dynamic_update_index_in_dim(landed, own, _my_index(), 0)


def _adam(w, g, m, v):
    m2 = B1 * m + (1.0 - B1) * g
    v2 = B2 * v + (1.0 - B2) * jnp.square(g)
    m_hat = m2 / (1.0 - B1 ** STEP)
    v_hat = v2 / (1.0 - B2 ** STEP)
    delta = -LR * (m_hat / (jnp.sqrt(v_hat) + AEPS) + WD * w)
    return delta, m2, v2


def _small_update(gp, wp, mp, vp):
    def body(g_ref, w_ref, m_ref, v_ref, all_ref, go_ref, d_ref, mo_ref, vo_ref, send_sems, recv_sems):
        x, y, c = _me()
        me = 4 * x + 2 * y + c
        all_ref[me] = g_ref[...]
        sends = []
        for k in range(1, N_DEV):
            peer, _ = _peer(k)
            cp = pltpu.make_async_remote_copy(src_ref=g_ref, dst_ref=all_ref.at[me], send_sem=send_sems.at[k - 1],
                                              recv_sem=recv_sems.at[k - 1], device_id=peer, device_id_type=MESH)
            cp.start()
            sends.append(cp)
        for k in range(1, N_DEV):
            peer, pidx = _peer(k)
            pltpu.make_async_remote_copy(src_ref=g_ref, dst_ref=all_ref.at[pidx], send_sem=send_sems.at[k - 1],
                                         recv_sem=recv_sems.at[k - 1], device_id=peer, device_id_type=MESH).wait_recv()
        for cp in sends:
            cp.wait_send()
        g = all_ref[0]
        for d in range(1, N_DEV):
            g = g + all_ref[d]
        delta, m2, v2 = _adam(w_ref[...], g, m_ref[...], v_ref[...])
        go_ref[...] = g
        d_ref[...] = delta
        mo_ref[...] = m2
        vo_ref[...] = v2

    vm = pl.BlockSpec(memory_space=pltpu.VMEM)
    vs = jax.ShapeDtypeStruct((NS // 128, 128), F32)
    return pl.pallas_call(
        body, out_shape=(jax.ShapeDtypeStruct((N_DEV, NS // 128, 128), F32), vs, vs, vs, vs), in_specs=[vm] * 4,
        out_specs=(vm,) * 5,
        scratch_shapes=[pltpu.SemaphoreType.DMA((N_DEV - 1,)), pltpu.SemaphoreType.DMA((N_DEV - 1,))],
        name='small_update', compiler_params=_params(None, 48))(gp, wp, mp, vp)


def _big_update(parts, w, m, v, name):
    _, R, C = parts.shape
    tr = R if R % 256 else (128 if C >= 2048 else 256)

    def body(p_ref, w_ref, m_ref, v_ref, g_ref, d_ref, mo_ref, vo_ref):
        g = p_ref[0].astype(F32)
        for d in range(1, N_DEV):
            g = g + p_ref[d].astype(F32)
        delta, m2, v2 = _adam(w_ref[...], g, m_ref[...], v_ref[...])
        g_ref[...] = g
        d_ref[...] = delta
        mo_ref[...] = m2
        vo_ref[...] = v2

    blk = pl.BlockSpec((tr, C), lambda i: (i, 0))
    sh = jax.ShapeDtypeStruct((R, C), F32)
    return pl.pallas_call(
        body, out_shape=(sh, sh, sh, sh), grid=(R // tr,),
        in_specs=[pl.BlockSpec((N_DEV, tr, C), lambda i: (0, i, 0)), blk, blk, blk], out_specs=(blk,) * 4,
        name=name, compiler_params=_params(('parallel',), 48))(parts, w, m, v)


def _ada_update(c_all, dmod_cols, w, m, v):
    C = w.shape[1]
    tr = 256

    def body(c_ref, dm_ref, w_ref, m_ref, v_ref, g_ref, d_ref, mo_ref, vo_ref):
        cv = c_ref[...]
        s = cv * jax.nn.sigmoid(cv)
        g = lax.dot_general(s, dm_ref[...], TN_DIMS, preferred_element_type=F32, precision=HI)
        delta, m2, v2 = _adam(w_ref[...], g, m_ref[...], v_ref[...])
        g_ref[...] = g
        d_ref[...] = delta
        mo_ref[...] = m2
        vo_ref[...] = v2

    blk = pl.BlockSpec((tr, C), lambda i: (i, 0))
    sh = jax.ShapeDtypeStruct((D, C), F32)
    return pl.pallas_call(
        body, out_shape=(sh, sh, sh, sh), grid=(D // tr,),
        in_specs=[pl.BlockSpec((N_DEV, tr), lambda i: (0, i)), pl.BlockSpec((N_DEV, C), lambda i: (0, 0)), blk, blk, blk],
        out_specs=(blk,) * 4, name='ada_update', compiler_params=_params(('parallel',), 48))(c_all, dmod_cols, w, m, v)


def _to_sub(t, d):
    if d == 1:
        return t
    return t.reshape(L // d, d, t.shape[-1]).transpose(1, 0, 2).reshape(L, t.shape[-1])


def _from_sub(t, d):
    if d == 1:
        return t
    return t.reshape(d, L // d, t.shape[-1]).transpose(1, 0, 2).reshape(L, t.shape[-1])


def _rows_to_cluster_lanes(t):
    k = t.shape[1]
    return t.reshape(N_CL, CL_S, k).transpose(0, 2, 1)


def _blockdiag_in(t):
    t = t.reshape(N_CL, CL_G, SSM_N, SSM_P).transpose(0, 1, 3, 2)
    eye = jnp.eye(CL_G, dtype=t.dtype)
    t = t[:, :, :, None, :] * eye[None, :, None, :, None]
    return t.reshape(N_CL, CL_U, CL_S)


def _blockdiag_extract(t):
    t = t.reshape(N_CL, CL_G, SSM_P, CL_G, SSM_N)
    t = jnp.stack([t[:, i, :, i, :] for i in range(CL_G)], axis=1)
    return t.transpose(0, 1, 3, 2).reshape(SSM_GN, SSM_P)


def _c_to_rows(t):
    return t.transpose(0, 2, 1).reshape(SSM_GN, SSM_P)


def _rows_to_c(t):
    return t.reshape(SSM_G, SSM_N, SSM_P).transpose(0, 2, 1)


def _local_step(x, pos, mod, tgt, sp, get_w, emit, first_deps=()):
    sh1, sc1, gt1, sh2, sc2, gt2 = (mod[i:i + 1] for i in range(NMOD))
    vec = lambda n: sp[n].reshape(1, -1)

    h1 = _prenorm_fwd(x, vec('g_pre_mix'), sc1, sh1)
    w_in = get_w('w_in', h1)
    proj = _mm(h1, w_in, mode='nn', name='mm_in', tn=1408, deps=first_deps)
    fr1 = ROPE_THETA ** (-jnp.arange(0, ROT_DIM, 2, dtype=F32) / ROT_DIM)
    lane = jnp.arange(128) % HEAD_DIM
    fr = jnp.where(lane < ROT_DIM, fr1[lane % (ROT_DIM // 2)], 0.0).reshape(1, 128).astype(F32)
    qk = _rope(proj[:, :ROPE_W], pos, fr, 1.0, 'rope_fwd')
    v_b = proj[:, ROPE_W:ROPE_W + KVW].astype(BF16)
    u = proj[:, ROPE_W + KVW:]
    qp = jnp.stack([_to_sub(qk[:, gi * KVW:(gi + 1) * KVW], d) for gi, d in enumerate(DILATIONS)])
    kp = jnp.stack([_to_sub(qk[:, QW:], d) for d in DILATIONS])
    vp = jnp.stack([_to_sub(v_b, d) for d in DILATIONS])
    o_p, lse_p = _attn_fwd(qp, kp, vp)
    o3 = jnp.stack([_from_sub(o_p[gi], d) for gi, d in enumerate(DILATIONS)])
    lse3 = jnp.stack([_from_sub(lse_p[gi], d) for gi, d in enumerate(DILATIONS)])
    att = _combine_fwd(o3, lse3, vec('g_attn_out'))

    rows = lambda n: sp[n].reshape(SSM_GN, 1)
    a_re, a_im = rows('ssm_a_re'), rows('ssm_a_im')
    ldt = jnp.repeat(sp['ssm_log_dt'].reshape(SSM_G, 1), SSM_N, axis=0)
    b_re, b_im = sp['ssm_b_re'].reshape(SSM_GN, SSM_P), sp['ssm_b_im'].reshape(SSM_GN, SSM_P)
    c_re, c_im = _c_to_rows(sp['ssm_c_re'].reshape(SSM_G, SSM_P, SSM_N)), _c_to_rows(sp['ssm_c_im'].reshape(SSM_G, SSM_P, SSM_N))
    bbr, bbi, pwr, pwi = _ssm_pre(a_re, a_im, ldt, b_re, b_im)
    bm = jnp.concatenate([_blockdiag_in(bbr), _blockdiag_in(bbi)], axis=2)
    cmt = jnp.concatenate([_blockdiag_in(c_re), -_blockdiag_in(c_im)], axis=2)
    bmt, cm = bm.transpose(0, 2, 1), cmt.transpose(0, 2, 1)
    pw = jnp.concatenate([_rows_to_cluster_lanes(pwr), _rows_to_cluster_lanes(pwi)], axis=2)
    dvec = vec('ssm_d')
    bm2, cm2 = _hilo(bm), _hilo(cm)
    ypre, bnd = _ssm_fwd(u, bm2, cm2, pw, dvec)
    w_glu = get_w('w_glu', ypre)
    ssm_n = _glu_fwd(ypre, w_glu, vec('b_glu'), vec('g_ssm_out'))

    cat = jnp.concatenate([att, ssm_n], axis=1)
    w_out = get_w('w_out', cat)
    mix = _mm(cat, w_out, mode='nn', name='mm_out', tk=640)
    x1, h2 = _postmix_fwd(x, mix, vec('g_post_mix'), gt1, vec('g_pre_mlp'), sc2, sh2)
    w_mi = get_w('w_mlp_in', h2)
    a_pre, r_act = _mm(h2, w_mi, mode='nn', name='mm_mlp_in', epilogue='relu2', b_sharded=True)
    w_mo = get_w('w_mlp_out', a_pre)
    y = _mm(r_act, w_mo, mode='nn', name='mm_mlp_out')
    dx2, dy, loss, dgt2, dg_post_mlp = _final_fwd_bwd(x1, y, tgt, vec('g_post_mlp'), gt2)
    dgt2, dg_post_mlp = dgt2[:1], dg_post_mlp[:1]

    da = _mm(dy, w_mo, mode='nt', name='mm_d_act', out_dtype=BF16, epilogue='drelu2', extra=a_pre)
    dep = emit('w_mlp_out', _mm(r_act, dy, mode='tn', name='mm_dw_mlp_out', out_dtype=BF16))
    dh2 = _mm(da, w_mi, mode='nt', name='mm_dh2', b_sharded=True, deps=dep)
    dep = emit('w_mlp_in', _mm(h2, da, mode='tn', name='mm_dw_mlp_in', out_dtype=BF16, out_sharded=True))
    dx1, dmix, dsc2, dsh2, dg_pre_mlp, dgt1, dg_post_mix = _postmix_bwd(
        dx2, dh2, x1, mix, vec('g_post_mix'), gt1, vec('g_pre_mlp'), sc2)
    dsc2, dsh2, dg_pre_mlp, dgt1, dg_post_mix = (t[:1] for t in (dsc2, dsh2, dg_pre_mlp, dgt1, dg_post_mix))
    dcat = _mm(dmix, w_out, mode='nt', name='mm_dcat', tn=1280, deps=dep)
    dep = emit('w_out', _mm(cat, dmix, mode='tn', name='mm_dw_out', out_dtype=BF16, tm=640))
    datt, dsn = dcat[:, :KVW], dcat[:, KVW:]

    dypre, g_w_glu, g_b_glu, g_g_ssm = _glu_bwd(ypre, dsn, w_glu, vec('b_glu'), vec('g_ssm_out'))
    g_b_glu, g_g_ssm = g_b_glu[:1], g_g_ssm[:1]
    dep = dep + emit('w_glu', g_w_glu.astype(BF16))
    du, dbm, dcm, dA, dD = _ssm_bwd(u, dypre, bm2, bmt.astype(BF16), cmt.astype(BF16), pw, dvec, bnd)
    dD = dD[:1]
    gbr, gbi = _blockdiag_extract(dbm[:, :, :CL_S]), _blockdiag_extract(dbm[:, :, CL_S:])
    dcmt = dcm.transpose(0, 2, 1)
    g_c_re = _rows_to_c(_blockdiag_extract(dcmt[:, :, :CL_S]))
    g_c_im = _rows_to_c(-_blockdiag_extract(dcmt[:, :, CL_S:]))
    gar = dA[:, 0, :CL_S].reshape(SSM_GN, 1)
    gai = dA[:, 0, CL_S:].reshape(SSM_GN, 1)
    sel = (jnp.arange(SSM_GN)[None, :] // SSM_N == jnp.arange(SSM_G)[:, None]).astype(F32)
    g_a_re, g_a_im, g_b_re, g_b_im, g_ldt = _ssm_post(a_re, a_im, ldt, b_re, b_im, gar, gai, gbr, gbi, sel)

    head_ones = (jnp.arange(KVW)[:, None] // HEAD_DIM == jnp.arange(KVW)[None, :] // HEAD_DIM).astype(F32)
    do3, dlse3, g_g_attn = _combine_bwd(datt, o3, lse3, vec('g_attn_out'), head_ones)
    g_g_attn = g_g_attn[:1]
    do_p = jnp.stack([_to_sub(do3[gi], d) for gi, d in enumerate(DILATIONS)])
    dlse_p = jnp.stack([_to_sub(dlse3[gi], d) for gi, d in enumerate(DILATIONS)])
    dq_p, dk_p, dv_p = _attn_bwd(qp, kp, vp, o_p, lse_p, do_p, dlse_p)
    dq = jnp.concatenate([_from_sub(dq_p[gi], d) for gi, d in enumerate(DILATIONS)], axis=1)
    dk = sum(_from_sub(dk_p[gi], d) for gi, d in enumerate(DILATIONS))
    dv = sum(_from_sub(dv_p[gi], d) for gi, d in enumerate(DILATIONS))
    dqk = _rope(jnp.concatenate([dq, dk], axis=1), pos, fr, -1.0, 'rope_bwd')
    dproj = jnp.concatenate([dqk, dv.astype(BF16), du.astype(BF16)], axis=1)
    dh1 = _mm(dproj, w_in, mode='nt', name='mm_dh1', tk=1408, deps=dep)
    emit('w_in', _mm(h1, dproj, mode='tn', name='mm_dw_in', out_dtype=BF16, tn=1408))
    grad_x, dsc1, dsh1, dg_pre_mix = _prenorm_bwd(dx1, dh1, x, vec('g_pre_mix'), sc1)
    dsc1, dsh1, dg_pre_mix = dsc1[:1], dsh1[:1], dg_pre_mix[:1]

    dmod = jnp.concatenate([dsh1, dsc1, dgt1, dsh2, dsc2, dgt2], axis=0)
    small = {'b_ada': dmod, 'g_pre_mix': dg_pre_mix, 'g_post_mix': dg_post_mix, 'ssm_a_re': g_a_re, 'ssm_a_im': g_a_im,
             'ssm_log_dt': g_ldt[:, 0], 'ssm_b_re': g_b_re, 'ssm_b_im': g_b_im, 'ssm_c_re': g_c_re, 'ssm_c_im': g_c_im,
             'ssm_d': dD, 'b_glu': g_b_glu, 'g_attn_out': g_g_attn, 'g_ssm_out': g_g_ssm, 'g_pre_mlp': dg_pre_mlp,
             'g_post_mlp': dg_post_mlp}
    return loss[0, 0], grad_x, small


def _pack(d):
    return jnp.concatenate([jnp.pad(d[n].reshape(-1).astype(F32), (0, SEG[n] - SMALL_SIZES[n])) for n in SMALL])


def _shard_major(t, name):
    if name in ('w_in', 'w_out', 'w_mlp_in'):
        k, n = t.shape
        return t.reshape(k, N_DEV, n // N_DEV).transpose(1, 0, 2)
    k, n = t.shape
    return t.reshape(N_DEV, k // N_DEV, n)


def _from_shard_major(t, name):
    if name in ('w_in', 'w_out', 'w_mlp_in'):
        _, k, n = t.shape
        return t.transpose(1, 0, 2).reshape(k, N_DEV * n)
    _, k, n = t.shape
    return t.reshape(N_DEV * k, n)


def kernel(x, c, positions, w_ada, b_ada, g_pre_mix, g_post_mix, w_in, ssm_a_re, ssm_a_im, ssm_log_dt, ssm_b_re, ssm_b_im, ssm_c_re, ssm_c_im, ssm_d, w_glu, b_glu, g_attn_out, g_ssm_out, w_out, g_pre_mlp, g_post_mlp, w_mlp_in, w_mlp_out, loss_target, m_w_ada, m_b_ada, m_g_pre_mix, m_g_post_mix, m_w_in, m_ssm_a_re, m_ssm_a_im, m_ssm_log_dt, m_ssm_b_re, m_ssm_b_im, m_ssm_c_re, m_ssm_c_im, m_ssm_d, m_w_glu, m_b_glu, m_g_attn_out, m_g_ssm_out, m_w_out, m_g_pre_mlp, m_g_post_mlp, m_w_mlp_in, m_w_mlp_out, v_w_ada, v_b_ada, v_g_pre_mix, v_g_post_mix, v_w_in, v_ssm_a_re, v_ssm_a_im, v_ssm_log_dt, v_ssm_b_re, v_ssm_b_im, v_ssm_c_re, v_ssm_c_im, v_ssm_d, v_w_glu, v_b_glu, v_g_attn_out, v_g_ssm_out, v_w_out, v_g_pre_mlp, v_g_post_mlp, v_w_mlp_in, v_w_mlp_out):
    loc = dict(locals())
    W = {n: loc[n] for n in WEIGHTS}
    M = {n: loc['m_' + n] for n in WEIGHTS}
    V = {n: loc['v_' + n] for n in WEIGHTS}
    assert x.shape == (1, L, D) and w_in.shape == (1, D, INW // N_DEV), (x.shape, w_in.shape)

    cw = NMOD * D // N_DEV
    c_all, mod8 = _mod_exchange(c.reshape(1, 1, D), w_ada[0], b_ada.reshape(N_DEV, 1, cw))
    mod = mod8.reshape(NMOD, D)

    gather = {n: _push_start(W[n][0].astype(BF16), False, mod8, 'gather_start_' + n) for n in BIG}
    mod = mod + sum(gather[n][4][0, 0] for n in BIG)

    def get_w(n, after):
        g = _push_wait(gather[n], after, 'gather_wait_' + n)
        return g if n == 'w_mlp_in' else _from_shard_major(g, n)

    scatter = {}

    def emit(n, g):
        src = g if n == 'w_mlp_in' else _shard_major(g, n)
        scatter[n] = _push_start(src, True, src, 'scatter_start_' + n)
        return (scatter[n][4],)

    sp = {n: W[n][0] for n in SMALL}
    loss, grad_x, small = _local_step(x[0], positions.reshape(L, 1), mod, loss_target[0], sp, get_w, emit)
    loss = lax.psum(loss, ('x', 'y', 'c'))
    out_g, out_d, out_m, out_v = {}, {}, {}, {}

    rows_all, sg, sd, sm, sv = _small_update(
        *[_pack(d).reshape(NS // 128, 128) for d in (small, {n: W[n] for n in SMALL}, {n: M[n] for n in SMALL},
                                                    {n: V[n] for n in SMALL})])
    off = 0
    for n in SMALL:
        sz = SMALL_SIZES[n]
        for dst, src in ((out_g, sg), (out_d, sd), (out_m, sm), (out_v, sv)):
            dst[n] = src[off // 128:(off + SEG[n]) // 128].reshape(-1)[:sz].reshape(W[n].shape[1:])
        off += SEG[n]

    me = 4 * lax.axis_index('x') + 2 * lax.axis_index('y') + lax.axis_index('c')
    dmod_all = rows_all[:, :NMOD * D // 128].reshape(N_DEV, NMOD * D)
    dmod_cols = lax.dynamic_slice_in_dim(dmod_all, me * cw, cw, axis=1)
    out_g['w_ada'], out_d['w_ada'], out_m['w_ada'], out_v['w_ada'] = _ada_update(
        c_all.reshape(N_DEV, D), dmod_cols, w_ada[0], m_w_ada[0], v_w_ada[0])

    after = out_v['w_ada']
    for n in ('w_mlp_out', 'w_mlp_in', 'w_out', 'w_glu', 'w_in'):
        parts = _push_wait(scatter[n], after, 'scatter_wait_' + n)
        out_g[n], out_d[n], out_m[n], out_v[n] = _big_update(parts, W[n][0], M[n][0], V[n][0], 'update_' + n)
        after = out_v[n]

    lead = lambda t: t[None]
    return (loss, grad_x[None], *[lead(out_g[n]) for n in WEIGHTS], *[lead(out_d[n]) for n in WEIGHTS],
            *[lead(out_m[n]) for n in WEIGHTS], *[lead(out_v[n]) for n in WEIGHTS])
```

```python
import functools
import math

import jax
import jax.numpy as jnp
from jax import lax
from jax.experimental import pallas as pl
from jax.experimental.pallas import tpu as pltpu

F32 = jnp.float32
BF16 = jnp.bfloat16
HI = lax.Precision.HIGHEST
MESH = pl.DeviceIdType.MESH

N_DEV = 8
L = 4096
D = 2048
HEAD_DIM = 64
N_GROUPS = 3
DILATIONS = (1, 4, 16)
HEADS = 6
QW = N_GROUPS * HEADS * HEAD_DIM
KVW = HEADS * HEAD_DIM
ROT_DIM = 16
ROPE_THETA = 500000.0
BLK = 128
NBLK = L // BLK
SSMW = D - QW
SSM_P = 16
SSM_G = SSMW // SSM_P
SSM_N = 64
SSM_GN = SSM_G * SSM_N
CL_G = 8
N_CL = SSM_G // CL_G
CL_U = CL_G * SSM_P
CL_S = CL_G * SSM_N
INW = QW + 2 * KVW + SSMW
OUTW = KVW + SSMW
DFF = 4 * D
NMOD = 6
EPS = 1e-6
LR, B1, B2, AEPS, WD, STEP = 0.001, 0.9, 0.999, 1e-08, 0.01, 10

T_SCAN = 512
MB = 2 ** 20

WEIGHTS = ['w_ada', 'b_ada', 'g_pre_mix', 'g_post_mix', 'w_in', 'ssm_a_re', 'ssm_a_im', 'ssm_log_dt',
           'ssm_b_re', 'ssm_b_im', 'ssm_c_re', 'ssm_c_im', 'ssm_d', 'w_glu', 'b_glu', 'g_attn_out',
           'g_ssm_out', 'w_out', 'g_pre_mlp', 'g_post_mlp', 'w_mlp_in', 'w_mlp_out']
BIG = ['w_in', 'w_glu', 'w_out', 'w_mlp_in', 'w_mlp_out']
SMALL = [n for n in WEIGHTS if n not in BIG and n != 'w_ada']
SMALL_SIZES = {'b_ada': NMOD * D, 'g_pre_mix': D, 'g_post_mix': D, 'ssm_a_re': SSM_GN, 'ssm_a_im': SSM_GN,
               'ssm_log_dt': SSM_G, 'ssm_b_re': SSM_GN * SSM_P, 'ssm_b_im': SSM_GN * SSM_P,
               'ssm_c_re': SSM_GN * SSM_P, 'ssm_c_im': SSM_GN * SSM_P, 'ssm_d': SSMW, 'b_glu': SSMW,
               'g_attn_out': KVW, 'g_ssm_out': SSMW, 'g_pre_mlp': D, 'g_post_mlp': D}
SEG = {n: -(-SMALL_SIZES[n] // 1024) * 1024 for n in SMALL}
NS = sum(SEG.values())


def _params(sem=None, vmem_mb=None):
    kw = {}
    if sem is not None:
        kw['dimension_semantics'] = sem
    if vmem_mb is not None:
        kw['vmem_limit_bytes'] = vmem_mb * MB
    return pltpu.CompilerParams(**kw)


def _vec(n):
    return pl.BlockSpec((1, n), lambda *_: (0, 0))


def _rms(x):
    return lax.rsqrt(jnp.mean(x * x, axis=-1, keepdims=True) + EPS)


def _rms_bwd(dn, n, r):
    return r * (dn - n * jnp.mean(dn * n, axis=-1, keepdims=True))


def _vec8(n):
    return pl.BlockSpec((8, n), lambda *_: (0, 0))


def _colsum(x):
    return jnp.sum(x.reshape(-1, 8, x.shape[-1]), axis=0)


def _fold8(last, *refs):
    @pl.when(last)
    def _():
        for r in refs:
            r[...] = jnp.broadcast_to(jnp.sum(r[...], axis=0, keepdims=True), r.shape)


def _mm(a, b, *, mode, name, out_dtype=F32, tm=1024, tn=1024, tk=2048, epilogue=None, extra=None,
        b_sharded=False, out_sharded=False, deps=()):
    if mode == 'nn':
        M, K = a.shape
        dims = (((1,), (0,)), ((), ()))
        a_spec = pl.BlockSpec((tm, tk), lambda i, j, k: (i, k))
        if b_sharded:
            _, K2, per = b.shape
            N, q = N_DEV * per, per // tn
            b_spec = pl.BlockSpec((None, tk, tn), lambda i, j, k: (j // q, k, j % q))
        else:
            K2, N = b.shape
            b_spec = pl.BlockSpec((tk, tn), lambda i, j, k: (k, j))
    elif mode == 'nt':
        M, K = a.shape
        dims = (((1,), (1,)), ((), ()))
        a_spec = pl.BlockSpec((tm, tk), lambda i, j, k: (i, k))
        if b_sharded:
            _, N, per = b.shape
            K2, q = N_DEV * per, per // tk
            b_spec = pl.BlockSpec((None, tn, tk), lambda i, j, k: (k // q, j, k % q))
        else:
            N, K2 = b.shape
            b_spec = pl.BlockSpec((tn, tk), lambda i, j, k: (j, k))
    else:
        (K, M), (K2, N) = a.shape, b.shape
        dims = (((0,), (0,)), ((), ()))
        a_spec = pl.BlockSpec((tk, tm), lambda i, j, k: (k, i))
        b_spec = pl.BlockSpec((tk, tn), lambda i, j, k: (k, j))
    assert K == K2 and M % tm == 0 and N % tn == 0 and K % tk == 0, (name, a.shape, b.shape, tm, tn, tk)
    nk = K // tk
    o_spec = pl.BlockSpec((tm, tn), lambda i, j, k: (i, j))
    o_dims = (M, N)
    if out_sharded:
        qo = N // N_DEV // tn
        o_spec = pl.BlockSpec((None, tm, tn), lambda i, j, k: (j // qo, i, j % qo))
        o_dims = (N_DEV, M, N // N_DEV)
    n_out = 2 if epilogue == 'relu2' else 1
    n_extra = 1 if extra is not None else 0
    n_in = 2 + n_extra + len(deps)

    def body(*refs):
        a_ref, b_ref = refs[0], refs[1]
        x_refs = refs[2:2 + n_extra]
        o_refs = refs[n_in:n_in + n_out]
        acc = refs[-1]
        k = pl.program_id(2)

        prod = lax.dot_general(a_ref[...], b_ref[...], dims, preferred_element_type=F32)

        def finish(r):
            if epilogue == 'relu2':
                o_refs[0][...] = r.astype(BF16)
                o_refs[1][...] = jnp.square(jnp.maximum(r, 0.0)).astype(BF16)
            elif epilogue == 'drelu2':
                pre = x_refs[0][...].astype(F32)
                o_refs[0][...] = (r * (2.0 * jnp.maximum(pre, 0.0))).astype(out_dtype)
            else:
                o_refs[0][...] = r.astype(out_dtype)

        if nk == 1:
            finish(prod)
        else:
            @pl.when(k == 0)
            def _():
                acc[...] = prod

            @pl.when((k > 0) & (k < nk - 1))
            def _():
                acc[...] += prod

            @pl.when(k == nk - 1)
            def _():
                finish(acc[...] + prod)

    if epilogue == 'relu2':
        out_shape = (jax.ShapeDtypeStruct((M, N), BF16), jax.ShapeDtypeStruct((M, N), BF16))
        out_specs = (o_spec, o_spec)
    else:
        out_shape = jax.ShapeDtypeStruct(o_dims, out_dtype)
        out_specs = o_spec
    args = (a, b) + ((extra,) if extra is not None else ()) + tuple(deps)
    in_specs = ([a_spec, b_spec] + ([o_spec] if extra is not None else [])
                + [pl.BlockSpec(memory_space=pl.ANY)] * len(deps))
    return pl.pallas_call(
        body, out_shape=out_shape, grid=(M // tm, N // tn, nk), in_specs=in_specs, out_specs=out_specs,
        scratch_shapes=[pltpu.VMEM((tm, tn) if nk > 1 else (8, 128), F32)], name=name,
        compiler_params=_params(('parallel', 'parallel', 'arbitrary'), 56))(*args)


TR = 256


def _rowspec(w=D):
    return pl.BlockSpec((TR, w), lambda i: (i, 0))


def _prenorm_fwd(x, g, sc, sh):
    def body(x_ref, g_ref, sc_ref, sh_ref, h_ref):
        xv = x_ref[...]
        n = xv * _rms(xv)
        h_ref[...] = ((n * g_ref[...]) * (1.0 + sc_ref[...]) + sh_ref[...]).astype(BF16)

    return pl.pallas_call(
        body, out_shape=jax.ShapeDtypeStruct((L, D), BF16), grid=(L // TR,),
        in_specs=[_rowspec(), _vec(D), _vec(D), _vec(D)], out_specs=_rowspec(), name='prenorm_fwd',
        compiler_params=_params(('parallel',), 40))(x, g, sc, sh)


def _postmix_fwd(x, mix, gpm, gt1, gpl, sc2, sh2):
    def body(x_ref, mix_ref, gpm_ref, gt1_ref, gpl_ref, sc2_ref, sh2_ref, x1_ref, h2_ref):
        mix_v = mix_ref[...]
        nm = mix_v * _rms(mix_v)
        x1 = x_ref[...] + gt1_ref[...] * (nm * gpm_ref[...])
        x1_ref[...] = x1
        n2 = x1 * _rms(x1)
        h2_ref[...] = ((n2 * gpl_ref[...]) * (1.0 + sc2_ref[...]) + sh2_ref[...]).astype(BF16)

    return pl.pallas_call(
        body, out_shape=(jax.ShapeDtypeStruct((L, D), F32), jax.ShapeDtypeStruct((L, D), BF16)), grid=(L // TR,),
        in_specs=[_rowspec(), _rowspec()] + [_vec(D)] * 5, out_specs=(_rowspec(), _rowspec()), name='postmix_fwd',
        compiler_params=_params(('parallel',), 40))(x, mix, gpm, gt1, gpl, sc2, sh2)


def _final_fwd_bwd(x1, y, tgt, g, gt2):
    def body(x1_ref, y_ref, t_ref, g_ref, gt2_ref, dx2_ref, dy_ref, loss_ref, dgt2_ref, dg_ref):
        @pl.when(pl.program_id(0) == 0)
        def _():
            loss_ref[...] = jnp.zeros_like(loss_ref)
            dgt2_ref[...] = jnp.zeros_like(dgt2_ref)
            dg_ref[...] = jnp.zeros_like(dg_ref)

        yv = y_ref[...]
        r = _rms(yv)
        n = yv * r
        ng = n * g_ref[...]
        x2 = x1_ref[...] + gt2_ref[...] * ng
        e = x2 - t_ref[...]
        loss_ref[...] += 0.5 * jnp.sum(jnp.mean(e * e, axis=-1, keepdims=True), axis=0, keepdims=True)
        dx2 = e * (1.0 / D)
        dx2_ref[...] = dx2
        dgt2_ref[...] += _colsum(dx2 * ng)
        dng = dx2 * gt2_ref[...]
        dg_ref[...] += _colsum(dng * n)
        dy_ref[...] = _rms_bwd(dng * g_ref[...], n, r).astype(BF16)
        _fold8(pl.program_id(0) == L // TR - 1, dgt2_ref, dg_ref)

    return pl.pallas_call(
        body,
        out_shape=(jax.ShapeDtypeStruct((L, D), F32), jax.ShapeDtypeStruct((L, D), BF16),
                   jax.ShapeDtypeStruct((8, 128), F32), jax.ShapeDtypeStruct((8, D), F32),
                   jax.ShapeDtypeStruct((8, D), F32)),
        grid=(L // TR,), in_specs=[_rowspec(), _rowspec(), _rowspec(), _vec(D), _vec(D)],
        out_specs=(_rowspec(), _rowspec(), _vec8(128), _vec8(D), _vec8(D)), name='final_fwd_bwd',
        compiler_params=_params(('arbitrary',), 40))(x1, y, tgt, g, gt2)


def _postmix_bwd(dx2, dh2, x1, mix, gpm, gt1, gpl, sc2):
    def body(dx2_ref, dh2_ref, x1_ref, mix_ref, gpm_ref, gt1_ref, gpl_ref, sc2_ref,
             dx1_ref, dmix_ref, dsc2_ref, dsh2_ref, dgpl_ref, dgt1_ref, dgpm_ref):
        @pl.when(pl.program_id(0) == 0)
        def _():
            for r_ in (dsc2_ref, dsh2_ref, dgpl_ref, dgt1_ref, dgpm_ref):
                r_[...] = jnp.zeros_like(r_)

        x1v = x1_ref[...]
        r2 = _rms(x1v)
        n2 = x1v * r2
        dh2v = dh2_ref[...]
        dsh2_ref[...] += _colsum(dh2v)
        dsc2_ref[...] += _colsum(dh2v * (n2 * gpl_ref[...]))
        t = dh2v * (1.0 + sc2_ref[...])
        dgpl_ref[...] += _colsum(t * n2)
        dx1 = dx2_ref[...] + _rms_bwd(t * gpl_ref[...], n2, r2)
        dx1_ref[...] = dx1
        mix_v = mix_ref[...]
        rm = _rms(mix_v)
        nm = mix_v * rm
        dgt1_ref[...] += _colsum(dx1 * (nm * gpm_ref[...]))
        u = dx1 * gt1_ref[...]
        dgpm_ref[...] += _colsum(u * nm)
        dmix_ref[...] = _rms_bwd(u * gpm_ref[...], nm, rm).astype(BF16)
        _fold8(pl.program_id(0) == L // TR - 1, dsc2_ref, dsh2_ref, dgpl_ref, dgt1_ref, dgpm_ref)

    vs = jax.ShapeDtypeStruct((8, D), F32)
    return pl.pallas_call(
        body, out_shape=(jax.ShapeDtypeStruct((L, D), F32), jax.ShapeDtypeStruct((L, D), BF16), vs, vs, vs, vs, vs),
        grid=(L // TR,), in_specs=[_rowspec()] * 4 + [_vec(D)] * 4,
        out_specs=(_rowspec(), _rowspec()) + (_vec8(D),) * 5, name='postmix_bwd',
        compiler_params=_params(('arbitrary',), 48))(dx2, dh2, x1, mix, gpm, gt1, gpl, sc2)


def _prenorm_bwd(dx1, dh1, x, g, sc1):
    def body(dx1_ref, dh1_ref, x_ref, g_ref, sc1_ref, dx_ref, dsc1_ref, dsh1_ref, dg_ref):
        @pl.when(pl.program_id(0) == 0)
        def _():
            for r_ in (dsc1_ref, dsh1_ref, dg_ref):
                r_[...] = jnp.zeros_like(r_)

        xv = x_ref[...]
        r = _rms(xv)
        n = xv * r
        dh = dh1_ref[...]
        dsh1_ref[...] += _colsum(dh)
        dsc1_ref[...] += _colsum(dh * (n * g_ref[...]))
        t = dh * (1.0 + sc1_ref[...])
        dg_ref[...] += _colsum(t * n)
        dx_ref[...] = dx1_ref[...] + _rms_bwd(t * g_ref[...], n, r)
        _fold8(pl.program_id(0) == L // TR - 1, dsc1_ref, dsh1_ref, dg_ref)

    vs = jax.ShapeDtypeStruct((8, D), F32)
    return pl.pallas_call(
        body, out_shape=(jax.ShapeDtypeStruct((L, D), F32), vs, vs, vs), grid=(L // TR,),
        in_specs=[_rowspec()] * 3 + [_vec(D)] * 2, out_specs=(_rowspec(),) + (_vec8(D),) * 3, name='prenorm_bwd',
        compiler_params=_params(('arbitrary',), 40))(dx1, dh1, x, g, sc1)


ROPE_W = QW + KVW


def _rope(xin, pos, fr, sign, name):
    def body(x_ref, pos_ref, fr_ref, o_ref):
        xv = x_ref[...]
        ang = pos_ref[...].astype(F32) * fr_ref[...]
        w = lax.broadcasted_iota(jnp.int32, (1, 128), 1) % HEAD_DIM
        cs = jnp.cos(ang)
        sn = jnp.sin(ang) * sign
        s1 = jnp.where(w < ROT_DIM // 2, -sn, 0.0)
        s2 = jnp.where((w >= ROT_DIM // 2) & (w < ROT_DIM), sn, 0.0)
        rep = ROPE_W // 128
        cs, s1, s2 = jnp.tile(cs, (1, rep)), jnp.tile(s1, (1, rep)), jnp.tile(s2, (1, rep))
        hi = pltpu.roll(xv, ROPE_W - ROT_DIM // 2, 1)
        lo = pltpu.roll(xv, ROT_DIM // 2, 1)
        o_ref[...] = (xv * cs + hi * s1 + lo * s2).astype(BF16)

    return pl.pallas_call(
        body, out_shape=jax.ShapeDtypeStruct((L, ROPE_W), BF16), grid=(L // TR,),
        in_specs=[_rowspec(ROPE_W), pl.BlockSpec((TR, 1), lambda i: (i, 0)), _vec(128)],
        out_specs=_rowspec(ROPE_W), name=name, compiler_params=_params(('parallel',), 40))(xin, pos, fr)


def _attn_mask(g, b):
    nbs = lax.shift_right_logical(jnp.int32(NBLK), 2 * g)
    first = (b & (nbs - 1)) == 0
    qi = lax.broadcasted_iota(jnp.int32, (BLK, 2 * BLK), 0)
    kj = lax.broadcasted_iota(jnp.int32, (BLK, 2 * BLK), 1)
    dist = qi + BLK - kj
    return (dist >= 0) & (dist <= BLK) & ((kj >= BLK) | jnp.logical_not(first))


def _blk(idx):
    return pl.BlockSpec((None, BLK, KVW), idx)


_CUR = lambda g, b: (g, b, 0)
_PREV = lambda g, b: (g, jnp.maximum(b - 1, 0), 0)
NEG = -1e30
NT_DIMS = (((1,), (1,)), ((), ()))
TN_DIMS = (((0,), (0,)), ((), ()))


def _attn_fwd(qp, kp, vp):
    def body(q_ref, kp_ref, kc_ref, vp_ref, vc_ref, o_ref, lse_ref):
        valid = _attn_mask(pl.program_id(0), pl.program_id(1))
        for h in range(HEADS):
            hs = slice(h * HEAD_DIM, (h + 1) * HEAD_DIM)
            q = q_ref[:, hs]
            kc = jnp.concatenate([kp_ref[:, hs], kc_ref[:, hs]], axis=0)
            vc = jnp.concatenate([vp_ref[:, hs], vc_ref[:, hs]], axis=0)
            s = lax.dot_general(q, kc, NT_DIMS, preferred_element_type=F32) * 0.125
            s = jnp.where(valid, s, NEG)
            m = jnp.max(s, axis=-1, keepdims=True)
            p = jnp.exp(s - m)
            l = jnp.sum(p, axis=-1, keepdims=True)
            o = jnp.dot(p.astype(BF16), vc, preferred_element_type=F32) / l
            o_ref[:, hs] = o
            lse_ref[:, hs] = jnp.broadcast_to(m + jnp.log(l), (BLK, HEAD_DIM))

    sh = jax.ShapeDtypeStruct((N_GROUPS, L, KVW), F32)
    return pl.pallas_call(
        body, out_shape=(sh, sh), grid=(N_GROUPS, NBLK),
        in_specs=[_blk(_CUR), _blk(_PREV), _blk(_CUR), _blk(_PREV), _blk(_CUR)],
        out_specs=(_blk(_CUR), _blk(_CUR)), name='attn_fwd',
        compiler_params=_params(('parallel', 'parallel'), 32))(qp, kp, kp, vp, vp)


def _attn_bwd(qp, kp, vp, o, lse, do, dlse):
    def body(q_ref, kp_ref, kc_ref, vp_ref, vc_ref, o_ref, lse_ref, do_ref, dlse_ref, dq_ref, dk_ref, dv_ref):
        b = pl.program_id(1)

        @pl.when(b == 0)
        def _():
            dk_ref[...] = jnp.zeros_like(dk_ref)
            dv_ref[...] = jnp.zeros_like(dv_ref)

        valid = _attn_mask(pl.program_id(0), b)
        prev0 = pl.multiple_of(jnp.maximum(b - 1, 0) * BLK, BLK)
        cur0 = pl.multiple_of(b * BLK, BLK)
        for h in range(HEADS):
            hs = slice(h * HEAD_DIM, (h + 1) * HEAD_DIM)
            q = q_ref[:, hs]
            kc = jnp.concatenate([kp_ref[:, hs], kc_ref[:, hs]], axis=0)
            vc = jnp.concatenate([vp_ref[:, hs], vc_ref[:, hs]], axis=0)
            s = lax.dot_general(q, kc, NT_DIMS, preferred_element_type=F32) * 0.125
            s = jnp.where(valid, s, NEG)
            p = jnp.exp(s - lse_ref[:, h * HEAD_DIM:h * HEAD_DIM + 1])
            do_h = do_ref[:, hs]
            delta = jnp.sum(do_h * o_ref[:, hs], axis=-1, keepdims=True)
            do_b = do_h.astype(BF16)
            dp = lax.dot_general(do_b, vc, NT_DIMS, preferred_element_type=F32)
            ds = p * (dp - delta + dlse_ref[:, h * HEAD_DIM:h * HEAD_DIM + 1])
            ds_b = (ds * 0.125).astype(BF16)
            dq_ref[:, hs] = jnp.dot(ds_b, kc, preferred_element_type=F32)
            dkc = lax.dot_general(ds_b, q, TN_DIMS, preferred_element_type=F32)
            dvc = lax.dot_general(p.astype(BF16), do_b, TN_DIMS, preferred_element_type=F32)
            dk_ref[pl.ds(prev0, BLK), hs] += dkc[:BLK]
            dv_ref[pl.ds(prev0, BLK), hs] += dvc[:BLK]
            dk_ref[pl.ds(cur0, BLK), hs] += dkc[BLK:]
            dv_ref[pl.ds(cur0, BLK), hs] += dvc[BLK:]

    sh = jax.ShapeDtypeStruct((N_GROUPS, L, KVW), F32)
    whole = pl.BlockSpec((None, L, KVW), lambda g, b: (g, 0, 0))
    return pl.pallas_call(
        body, out_shape=(sh, sh, sh), grid=(N_GROUPS, NBLK),
        in_specs=[_blk(_CUR), _blk(_PREV), _blk(_CUR), _blk(_PREV), _blk(_CUR)] + [_blk(_CUR)] * 4,
        out_specs=(_blk(_CUR), whole, whole), name='attn_bwd',
        compiler_params=_params(('arbitrary', 'arbitrary'), 48))(qp, kp, kp, vp, vp, o, lse, do, dlse)


TC = 512


def _comb_spec():
    return pl.BlockSpec((N_GROUPS, TC, KVW), lambda i: (0, i, 0))


def _combine_weights(lse_ref):
    l0, l1, l2 = lse_ref[0], lse_ref[1], lse_ref[2]
    m = jnp.maximum(jnp.maximum(l0, l1), l2)
    e0, e1, e2 = jnp.exp(l0 - m), jnp.exp(l1 - m), jnp.exp(l2 - m)
    z = e0 + e1 + e2
    return e0 / z, e1 / z, e2 / z


def _combine_fwd(o3, lse3, g):
    def body(o_ref, lse_ref, g_ref, att_ref):
        w0, w1, w2 = _combine_weights(lse_ref)
        a = w0 * o_ref[0] + w1 * o_ref[1] + w2 * o_ref[2]
        att_ref[...] = ((a * _rms(a)) * g_ref[...]).astype(BF16)

    return pl.pallas_call(
        body, out_shape=jax.ShapeDtypeStruct((L, KVW), BF16), grid=(L // TC,),
        in_specs=[_comb_spec(), _comb_spec(), _vec(KVW)], out_specs=pl.BlockSpec((TC, KVW), lambda i: (i, 0)),
        name='combine_fwd', compiler_params=_params(('parallel',), 40))(o3, lse3, g)


def _combine_bwd(datt, o3, lse3, g, head_ones):
    def body(datt_ref, o_ref, lse_ref, g_ref, e_ref, do_ref, dlse_ref, dg_ref):
        @pl.when(pl.program_id(0) == 0)
        def _():
            dg_ref[...] = jnp.zeros_like(dg_ref)

        ws = _combine_weights(lse_ref)
        a = ws[0] * o_ref[0] + ws[1] * o_ref[1] + ws[2] * o_ref[2]
        r = _rms(a)
        n = a * r
        dv = datt_ref[...]
        dg_ref[...] += _colsum(dv * n)
        da = _rms_bwd(dv * g_ref[...], n, r)
        dws = [jnp.dot(da * o_ref[i], e_ref[...], preferred_element_type=F32, precision=HI) for i in range(3)]
        dbar = ws[0] * dws[0] + ws[1] * dws[1] + ws[2] * dws[2]
        for i in range(3):
            do_ref[i] = ws[i] * da
            dlse_ref[i] = ws[i] * (dws[i] - dbar)
        _fold8(pl.program_id(0) == L // TC - 1, dg_ref)

    sh = jax.ShapeDtypeStruct((N_GROUPS, L, KVW), F32)
    return pl.pallas_call(
        body, out_shape=(sh, sh, jax.ShapeDtypeStruct((8, KVW), F32)), grid=(L // TC,),
        in_specs=[pl.BlockSpec((TC, KVW), lambda i: (i, 0)), _comb_spec(), _comb_spec(), _vec(KVW),
                  pl.BlockSpec((KVW, KVW), lambda i: (0, 0))],
        out_specs=(_comb_spec(), _comb_spec(), _vec8(KVW)), name='combine_bwd',
        compiler_params=_params(('arbitrary',), 48))(datt, o3, lse3, g, head_ones)


def _ssm_disc(ar, ai, ldt):
    dt = jnp.exp(ldt)
    zr, zi = ar * dt, ai * dt
    ez = jnp.exp(zr)
    A_r, A_i = ez * jnp.cos(zi), ez * jnp.sin(zi)
    den = ar * ar + ai * ai
    xr, xi = A_r - 1.0, A_i
    cr = (xr * ar + xi * ai) / den
    ci = (xi * ar - xr * ai) / den
    return dt, zr, zi, A_r, A_i, den, cr, ci


def _ssm_pre(ar, ai, ldt, br, bi):
    def body(ar_ref, ai_ref, ldt_ref, br_ref, bi_ref, bbr_ref, bbi_ref, pwr_ref, pwi_ref):
        _, zr, zi, _, _, _, cr, ci = _ssm_disc(ar_ref[...], ai_ref[...], ldt_ref[...])
        bbr_ref[...] = cr * br_ref[...] - ci * bi_ref[...]
        bbi_ref[...] = cr * bi_ref[...] + ci * br_ref[...]
        k = (lax.broadcasted_iota(jnp.int32, (1, 8), 1) + 1).astype(F32)
        ek = jnp.exp(zr * k)
        pwr_ref[...] = ek * jnp.cos(zi * k)
        pwi_ref[...] = ek * jnp.sin(zi * k)

    s16 = jax.ShapeDtypeStruct((SSM_GN, SSM_P), F32)
    s8 = jax.ShapeDtypeStruct((SSM_GN, 8), F32)
    return pl.pallas_call(body, out_shape=(s16, s16, s8, s8), name='ssm_pre',
                          compiler_params=_params(None, 40))(ar, ai, ldt, br, bi)


def _ssm_post(ar, ai, ldt, br, bi, gar, gai, gbr, gbi, sel):
    def body(ar_ref, ai_ref, ldt_ref, br_ref, bi_ref, gar_ref, gai_ref, gbr_ref, gbi_ref, sel_ref,
             dar_ref, dai_ref, dbr_ref, dbi_ref, dldt_ref):
        a_r, a_i = ar_ref[...], ai_ref[...]
        dt, _, _, A_r, A_i, den, cr, ci = _ssm_disc(a_r, a_i, ldt_ref[...])
        b_r, b_i, g_br, g_bi = br_ref[...], bi_ref[...], gbr_ref[...], gbi_ref[...]
        gcr = jnp.sum(g_br * b_r + g_bi * b_i, axis=-1, keepdims=True)
        gci = jnp.sum(g_bi * b_r - g_br * b_i, axis=-1, keepdims=True)
        dbr_ref[...] = g_br * cr + g_bi * ci
        dbi_ref[...] = g_bi * cr - g_br * ci
        g_ar = gar_ref[...] + (gcr * a_r - gci * a_i) / den
        g_ai = gai_ref[...] + (gcr * a_i + gci * a_r) / den
        qr = (cr * a_r + ci * a_i) / den
        qi = (ci * a_r - cr * a_i) / den
        glr = -(gcr * qr + gci * qi)
        gli = -(gci * qr - gcr * qi)
        gzr = g_ar * A_r + g_ai * A_i
        gzi = g_ai * A_r - g_ar * A_i
        dar_ref[...] = glr + gzr * dt
        dai_ref[...] = gli + gzi * dt
        gdt = (gzr * a_r + gzi * a_i) * dt
        dldt_ref[...] = jnp.dot(sel_ref[...], jnp.broadcast_to(gdt, (SSM_GN, 128)),
                                preferred_element_type=F32, precision=HI)

    s1 = jax.ShapeDtypeStruct((SSM_GN, 1), F32)
    s16 = jax.ShapeDtypeStruct((SSM_GN, SSM_P), F32)
    return pl.pallas_call(body, out_shape=(s1, s1, s16, s16, jax.ShapeDtypeStruct((SSM_G, 128), F32)),
                          name='ssm_post', compiler_params=_params(None, 48))(
                              ar, ai, ldt, br, bi, gar, gai, gbr, gbi, sel)


SCAN_CH = 8


def _scan_fwd_tiles(s_ref, pw, carry):
    pwr, pwi = pw[:, :CL_S], pw[:, CL_S:]
    row = lax.broadcasted_iota(jnp.int32, (SCAN_CH, 8, CL_S), 1)
    steps = [(k, pwr[k - 1:k], pwi[k - 1:k]) for k in (1, 2, 4)]
    rows = 8 * SCAN_CH

    def chunk(i, c):
        cr, ci = c
        r0 = pl.multiple_of(i * rows, rows)
        xr = s_ref[pl.ds(r0, rows), 0:CL_S].reshape(SCAN_CH, 8, CL_S)
        xi = s_ref[pl.ds(r0, rows), CL_S:2 * CL_S].reshape(SCAN_CH, 8, CL_S)
        for k, pr, pi in steps:
            sr = jnp.where(row >= k, pltpu.roll(xr, k, 1), 0.0)
            si = jnp.where(row >= k, pltpu.roll(xi, k, 1), 0.0)
            xr, xi = xr + pr * sr - pi * si, xi + pr * si + pi * sr
        for j in range(SCAN_CH):
            tr = xr[j] + pwr * cr - pwi * ci
            ti = xi[j] + pwr * ci + pwi * cr
            s_ref[pl.ds(r0 + 8 * j, 8), 0:CL_S] = tr
            s_ref[pl.ds(r0 + 8 * j, 8), CL_S:2 * CL_S] = ti
            cr, ci = tr[7:8], ti[7:8]
        return cr, ci

    return lax.fori_loop(0, T_SCAN // rows, chunk, (carry[:, :CL_S], carry[:, CL_S:]))


def _scan_bwd_tiles(l_ref, pw, carry):
    pwr, pwi = pw[:, :CL_S], pw[:, CL_S:]
    rpr = jnp.concatenate([pwr[7 - r:8 - r] for r in range(8)], axis=0)
    rpi = jnp.concatenate([pwi[7 - r:8 - r] for r in range(8)], axis=0)
    row = lax.broadcasted_iota(jnp.int32, (SCAN_CH, 8, CL_S), 1)
    steps = [(k, pwr[k - 1:k], pwi[k - 1:k]) for k in (1, 2, 4)]
    rows = 8 * SCAN_CH
    nc = T_SCAN // rows

    def chunk(i, c):
        cr, ci = c
        r0 = pl.multiple_of((nc - 1 - i) * rows, rows)
        xr = l_ref[pl.ds(r0, rows), 0:CL_S].reshape(SCAN_CH, 8, CL_S)
        xi = l_ref[pl.ds(r0, rows), CL_S:2 * CL_S].reshape(SCAN_CH, 8, CL_S)
        for k, pr, pi in steps:
            sr = jnp.where(row < 8 - k, pltpu.roll(xr, 8 - k, 1), 0.0)
            si = jnp.where(row < 8 - k, pltpu.roll(xi, 8 - k, 1), 0.0)
            xr, xi = xr + pr * sr + pi * si, xi + pr * si - pi * sr
        for j in reversed(range(SCAN_CH)):
            tr = xr[j] + rpr * cr + rpi * ci
            ti = xi[j] + rpr * ci - rpi * cr
            l_ref[pl.ds(r0 + 8 * j, 8), 0:CL_S] = tr
            l_ref[pl.ds(r0 + 8 * j, 8), CL_S:2 * CL_S] = ti
            cr, ci = tr[0:1], ti[0:1]
        return cr, ci

    return lax.fori_loop(0, nc, chunk, (carry[:, :CL_S], carry[:, CL_S:]))


NT_SCAN = L // T_SCAN


def _hilo(t):
    hi = t.astype(BF16)
    return jnp.stack([hi, (t - hi.astype(F32)).astype(BF16)], axis=1)


def _dot3(a, b_ref):
    ah = a.astype(BF16)
    al = (a - ah.astype(F32)).astype(BF16)
    bh, bl = b_ref[0], b_ref[1]
    return (jnp.dot(ah, bh, preferred_element_type=F32) + jnp.dot(al, bh, preferred_element_type=F32)
            + jnp.dot(ah, bl, preferred_element_type=F32))


def _hl_spec(r, c):
    return pl.BlockSpec((None, 2, r, c), lambda c_, t: (c_, 0, 0, 0))


def _ssm_fwd(u, bm2, cm2, pw, dvec):
    def body(u_ref, bm_ref, cm_ref, pw_ref, d_ref, y_ref, bnd_ref, s_ref, carry_ref):
        @pl.when(pl.program_id(1) == 0)
        def _():
            carry_ref[...] = jnp.zeros_like(carry_ref)

        bnd_ref[...] = carry_ref[...]
        uv = u_ref[...]
        s_ref[...] = _dot3(uv, bm_ref)
        cr, ci = _scan_fwd_tiles(s_ref, pw_ref[...], carry_ref[...])
        carry_ref[...] = jnp.concatenate([cr, ci], axis=1)
        y_ref[...] = _dot3(s_ref[...], cm_ref) + d_ref[...] * uv

    return pl.pallas_call(
        body,
        out_shape=(jax.ShapeDtypeStruct((L, SSMW), F32), jax.ShapeDtypeStruct((N_CL, NT_SCAN, 1, 2 * CL_S), F32)),
        grid=(N_CL, NT_SCAN),
        in_specs=[pl.BlockSpec((T_SCAN, CL_U), lambda c, t: (t, c)),
                  _hl_spec(CL_U, 2 * CL_S), _hl_spec(2 * CL_S, CL_U),
                  pl.BlockSpec((None, 8, 2 * CL_S), lambda c, t: (c, 0, 0)),
                  pl.BlockSpec((1, CL_U), lambda c, t: (0, c))],
        out_specs=(pl.BlockSpec((T_SCAN, CL_U), lambda c, t: (t, c)),
                   pl.BlockSpec((None, None, 1, 2 * CL_S), lambda c, t: (c, t, 0, 0))),
        scratch_shapes=[pltpu.VMEM((T_SCAN, 2 * CL_S), F32), pltpu.VMEM((1, 2 * CL_S), F32)],
        name='ssm_fwd', compiler_params=_params(('arbitrary', 'arbitrary'), 40))(u, bm2, cm2, pw, dvec)


def _ssm_bwd(u, dy, bm2, bmt, cmt, pw, dvec, bnd):
    rev = lambda t: NT_SCAN - 1 - t

    def body(u_ref, dy_ref, bm_ref, bmt_ref, cmt_ref, pw_ref, d_ref, bnd_ref,
             du_ref, dbm_ref, dcm_ref, da_ref, dd_ref, s_ref, l_ref, carry_ref):
        @pl.when(pl.program_id(1) == 0)
        def _():
            carry_ref[...] = jnp.zeros_like(carry_ref)
            dbm_ref[...] = jnp.zeros_like(dbm_ref)
            dcm_ref[...] = jnp.zeros_like(dcm_ref)
            da_ref[...] = jnp.zeros_like(da_ref)
            dd_ref[...] = jnp.zeros_like(dd_ref)

        uv, dyv, pw = u_ref[...], dy_ref[...], pw_ref[...]
        dy_b = dyv.astype(BF16)
        entry = bnd_ref[...]
        s_ref[...] = _dot3(uv, bm_ref)
        _scan_fwd_tiles(s_ref, pw, entry)
        l_ref[...] = jnp.dot(dy_b, cmt_ref[...], preferred_element_type=F32)
        cr, ci = _scan_bwd_tiles(l_ref, pw, carry_ref[...])
        carry_ref[...] = jnp.concatenate([cr, ci], axis=1)
        sv, lv = s_ref[...], l_ref[...]
        lv_b = lv.astype(BF16)
        du_ref[...] = dyv * d_ref[...] + jnp.dot(lv_b, bmt_ref[...], preferred_element_type=F32)
        dbm_ref[...] += lax.dot_general(uv.astype(BF16), lv_b, TN_DIMS, preferred_element_type=F32)
        dcm_ref[...] += lax.dot_general(sv.astype(BF16), dy_b, TN_DIMS, preferred_element_type=F32)
        dd_ref[...] += _colsum(dyv * uv)
        row = lax.broadcasted_iota(jnp.int32, (T_SCAN, 2 * CL_S), 0)
        sp = jnp.where(row == 0, entry, pltpu.roll(sv, 1, 0))
        spr, spi = sp[:, :CL_S], sp[:, CL_S:]
        lr, li = lv[:, :CL_S], lv[:, CL_S:]
        da_ref[:, 0:CL_S] += _colsum(lr * spr + li * spi)
        da_ref[:, CL_S:2 * CL_S] += _colsum(li * spr - lr * spi)
        _fold8(pl.program_id(1) == NT_SCAN - 1, da_ref, dd_ref)

    return pl.pallas_call(
        body,
        out_shape=(jax.ShapeDtypeStruct((L, SSMW), F32), jax.ShapeDtypeStruct((N_CL, CL_U, 2 * CL_S), F32),
                   jax.ShapeDtypeStruct((N_CL, 2 * CL_S, CL_U), F32), jax.ShapeDtypeStruct((N_CL, 8, 2 * CL_S), F32),
                   jax.ShapeDtypeStruct((8, SSMW), F32)),
        grid=(N_CL, NT_SCAN),
        in_specs=[pl.BlockSpec((T_SCAN, CL_U), lambda c, t: (rev(t), c)),
                  pl.BlockSpec((T_SCAN, CL_U), lambda c, t: (rev(t), c)),
                  _hl_spec(CL_U, 2 * CL_S),
                  pl.BlockSpec((None, 2 * CL_S, CL_U), lambda c, t: (c, 0, 0)),
                  pl.BlockSpec((None, CL_U, 2 * CL_S), lambda c, t: (c, 0, 0)),
                  pl.BlockSpec((None, 8, 2 * CL_S), lambda c, t: (c, 0, 0)),
                  pl.BlockSpec((1, CL_U), lambda c, t: (0, c)),
                  pl.BlockSpec((None, None, 1, 2 * CL_S), lambda c, t: (c, rev(t), 0, 0))],
        out_specs=(pl.BlockSpec((T_SCAN, CL_U), lambda c, t: (rev(t), c)),
                   pl.BlockSpec((None, CL_U, 2 * CL_S), lambda c, t: (c, 0, 0)),
                   pl.BlockSpec((None, 2 * CL_S, CL_U), lambda c, t: (c, 0, 0)),
                   pl.BlockSpec((None, 8, 2 * CL_S), lambda c, t: (c, 0, 0)),
                   pl.BlockSpec((8, CL_U), lambda c, t: (0, c))),
        scratch_shapes=[pltpu.VMEM((T_SCAN, 2 * CL_S), F32), pltpu.VMEM((T_SCAN, 2 * CL_S), F32),
                        pltpu.VMEM((1, 2 * CL_S), F32)],
        name='ssm_bwd', compiler_params=_params(('arbitrary', 'arbitrary'), 48))(u, dy, bm2, bmt, cmt, pw, dvec, bnd)


GELU_C = math.sqrt(2.0 / math.pi)
GELU_K = 0.044715


def _gelu_parts(x):
    t = jnp.tanh(GELU_C * (x + GELU_K * (x * x * x)))
    return x * (0.5 * (1.0 + t)), t


def _glu_fwd(ypre, wglu, bglu, gs):
    def body(y_ref, w_ref, b_ref, g_ref, o_ref):
        yg, _ = _gelu_parts(y_ref[...])
        z = jnp.dot(yg.astype(BF16), w_ref[...], preferred_element_type=F32) + b_ref[...]
        s = yg * jax.nn.sigmoid(z)
        o_ref[...] = ((s * _rms(s)) * g_ref[...]).astype(BF16)

    return pl.pallas_call(
        body, out_shape=jax.ShapeDtypeStruct((L, SSMW), BF16), grid=(L // TR,),
        in_specs=[_rowspec(SSMW), pl.BlockSpec((SSMW, SSMW), lambda i: (0, 0)), _vec(SSMW), _vec(SSMW)],
        out_specs=_rowspec(SSMW), name='glu_fwd', compiler_params=_params(('parallel',), 32))(ypre, wglu, bglu, gs)


def _glu_bwd(ypre, dsn, wglu, bglu, gs):
    def body(y_ref, d_ref, w_ref, b_ref, g_ref, dy_ref, dw_ref, db_ref, dg_ref):
        @pl.when(pl.program_id(0) == 0)
        def _():
            dw_ref[...] = jnp.zeros_like(dw_ref)
            db_ref[...] = jnp.zeros_like(db_ref)
            dg_ref[...] = jnp.zeros_like(dg_ref)

        xv = y_ref[...]
        yg, t = _gelu_parts(xv)
        yg_b = yg.astype(BF16)
        z = jnp.dot(yg_b, w_ref[...], preferred_element_type=F32) + b_ref[...]
        sg = jax.nn.sigmoid(z)
        s = yg * sg
        r = _rms(s)
        n = s * r
        dv = d_ref[...]
        dg_ref[...] += _colsum(dv * n)
        ds = _rms_bwd(dv * g_ref[...], n, r)
        dz = (ds * yg) * (sg * (1.0 - sg))
        dz_b = dz.astype(BF16)
        db_ref[...] += _colsum(dz)
        dw_ref[...] += lax.dot_general(yg_b, dz_b, TN_DIMS, preferred_element_type=F32)
        dyg = ds * sg + lax.dot_general(dz_b, w_ref[...], NT_DIMS, preferred_element_type=F32)
        dgelu = 0.5 * (1.0 + t) + (0.5 * xv) * (1.0 - t * t) * (GELU_C * (1.0 + 3.0 * GELU_K * (xv * xv)))
        dy_ref[...] = dyg * dgelu
        _fold8(pl.program_id(0) == L // TR - 1, db_ref, dg_ref)

    vs = jax.ShapeDtypeStruct((8, SSMW), F32)
    return pl.pallas_call(
        body, out_shape=(jax.ShapeDtypeStruct((L, SSMW), F32), jax.ShapeDtypeStruct((SSMW, SSMW), F32), vs, vs),
        grid=(L // TR,),
        in_specs=[_rowspec(SSMW), _rowspec(SSMW), pl.BlockSpec((SSMW, SSMW), lambda i: (0, 0)), _vec(SSMW), _vec(SSMW)],
        out_specs=(_rowspec(SSMW), pl.BlockSpec((SSMW, SSMW), lambda i: (0, 0)), _vec8(SSMW), _vec8(SSMW)),
        name='glu_bwd', compiler_params=_params(('arbitrary',), 40))(ypre, dsn, wglu, bglu, gs)


def _me():
    return lax.axis_index('x'), lax.axis_index('y'), lax.axis_index('c')


def _my_index():
    return 4 * lax.axis_index('x') + 2 * lax.axis_index('y') + lax.axis_index('c')


def _peer(k):
    x, y, c = _me()
    px = 1 - x if k & 4 else x
    py = 1 - y if k & 2 else y
    pc = 1 - c if k & 1 else c
    return (px, py, pc), 4 * px + 2 * py + pc


def _mod_exchange(c_row, w_ada, b_ada8):
    cw = NMOD * D // N_DEV

    def body(c_ref, w_ref, b_ref, call_ref, mod_ref, part_ref, send_sems, recv_sems):
        x, y, c = _me()
        me = 4 * x + 2 * y + c
        call_ref[me] = c_ref[0]
        sends = []
        for k in range(1, N_DEV):
            peer, _ = _peer(k)
            cp = pltpu.make_async_remote_copy(src_ref=c_ref.at[0], dst_ref=call_ref.at[me], send_sem=send_sems.at[0, k - 1],
                                              recv_sem=recv_sems.at[0, k - 1], device_id=peer, device_id_type=MESH)
            cp.start()
            sends.append(cp)
        for k in range(1, N_DEV):
            peer, pidx = _peer(k)
            pltpu.make_async_remote_copy(src_ref=c_ref.at[0], dst_ref=call_ref.at[pidx], send_sem=send_sems.at[0, k - 1],
                                         recv_sem=recv_sems.at[0, k - 1], device_id=peer, device_id_type=MESH).wait_recv()
        for cp in sends:
            cp.wait_send()
        cv = call_ref[...].reshape(N_DEV, D)
        part = jnp.dot(cv * jax.nn.sigmoid(cv), w_ref[...], preferred_element_type=F32, precision=HI)
        part_ref[...] = part.reshape(N_DEV, 1, cw)
        mod_ref[me] = part_ref[me]
        sends = []
        for k in range(1, N_DEV):
            peer, pidx = _peer(k)
            cp = pltpu.make_async_remote_copy(src_ref=part_ref.at[pidx], dst_ref=mod_ref.at[me], send_sem=send_sems.at[1, k - 1],
                                              recv_sem=recv_sems.at[1, k - 1], device_id=peer, device_id_type=MESH)
            cp.start()
            sends.append(cp)
        for k in range(1, N_DEV):
            peer, pidx = _peer(k)
            pltpu.make_async_remote_copy(src_ref=part_ref.at[pidx], dst_ref=mod_ref.at[pidx], send_sem=send_sems.at[1, k - 1],
                                         recv_sem=recv_sems.at[1, k - 1], device_id=peer, device_id_type=MESH).wait_recv()
        for cp in sends:
            cp.wait_send()
        mod_ref[...] = mod_ref[...] + b_ref[...]

    vm = pl.BlockSpec(memory_space=pltpu.VMEM)
    return pl.pallas_call(
        body, out_shape=(jax.ShapeDtypeStruct((N_DEV, 1, D), F32), jax.ShapeDtypeStruct((N_DEV, 1, cw), F32)),
        in_specs=[vm, vm, vm], out_specs=(vm, vm),
        scratch_shapes=[pltpu.VMEM((N_DEV, 1, cw), F32), pltpu.SemaphoreType.DMA((2, N_DEV - 1)),
                        pltpu.SemaphoreType.DMA((2, N_DEV - 1))],
        name='mod_exchange', compiler_params=_params(None, 48))(c_row, w_ada, b_ada8)


HBM_SPEC = pl.BlockSpec(memory_space=pltpu.HBM)
SEM_SPEC = pl.BlockSpec(memory_space=pltpu.SEMAPHORE)
DATAFLOW = pltpu.SideEffectType.DATAFLOW_SIDE_EFFECTING


def _push_start(src, scatter, after, name):
    land = lax.empty(src.shape if scatter else (N_DEV,) + src.shape, src.dtype)

    def body(src_ref, land_ref, after_ref, send_sem, recv_sem, src_thru, land_thru, token):
        x, y, c = _me()
        me = 4 * x + 2 * y + c
        for k in range(1, N_DEV):
            peer, pidx = _peer(k)
            pltpu.make_async_remote_copy(src_ref=src_ref.at[pidx] if scatter else src_ref, dst_ref=land_ref.at[me],
                                         send_sem=send_sem, recv_sem=recv_sem, device_id=peer,
                                         device_id_type=MESH).start()
        token[...] = jnp.zeros_like(token)

    outs = pl.pallas_call(
        body, name=name,
        out_shape=(pltpu.SemaphoreType.DMA(()), pltpu.SemaphoreType.DMA(()), pltpu.HBM(src.shape, src.dtype),
                   pltpu.HBM(land.shape, land.dtype), jax.ShapeDtypeStruct((8, 128), F32)),
        in_specs=(HBM_SPEC, HBM_SPEC, pl.BlockSpec(memory_space=pl.ANY)),
        out_specs=(SEM_SPEC, SEM_SPEC, HBM_SPEC, HBM_SPEC, pl.BlockSpec(memory_space=pltpu.VMEM)),
        input_output_aliases={0: 2, 1: 3}, compiler_params=pltpu.CompilerParams(has_side_effects=DATAFLOW),
    )(pltpu.with_memory_space_constraint(src, pltpu.HBM), pltpu.with_memory_space_constraint(land, pltpu.HBM), after)
    own = lax.dynamic_index_in_dim(src, _my_index(), 0, keepdims=False) if scatter else src
    return (*outs, own)


def _push_wait(handle, after, name):
    send_sem, recv_sem, src_thru, land_thru, _, own = handle
    after = tuple(after) if isinstance(after, (tuple, list)) else (after,)

    def body(src_ref, land_ref, send_sem, recv_sem, *rest):
        seven = land_ref.at[pl.ds(0, N_DEV - 1)]
        cp = pltpu.make_async_remote_copy(src_ref=seven, dst_ref=seven, send_sem=send_sem, recv_sem=recv_sem,
                                          device_id=_me(), device_id_type=MESH)
        cp.wait_send()
        cp.wait_recv()

    landed = pl.pallas_call(
        body, name=name,
        out_shape=(pltpu.HBM(src_thru.shape, src_thru.dtype), pltpu.HBM(land_thru.shape, land_thru.dtype)),
        in_specs=(HBM_SPEC, HBM_SPEC, SEM_SPEC, SEM_SPEC) + (pl.BlockSpec(memory_space=pl.ANY),) * len(after),
        out_specs=(HBM_SPEC, HBM_SPEC), input_output_aliases={0: 0, 1: 1},
        compiler_params=pltpu.CompilerParams(has_side_effects=DATAFLOW),
    )(src_thru, land_thru, send_sem, recv_sem, *after)[1]
    return lax.dynamic_update_index_in_dim(landed, own, _my_index(), 0)


def _adam(w, g, m, v):
    m2 = B1 * m + (1.0 - B1) * g
    v2 = B2 * v + (1.0 - B2) * jnp.square(g)
    m_hat = m2 / (1.0 - B1 ** STEP)
    v_hat = v2 / (1.0 - B2 ** STEP)
    delta = -LR * (m_hat / (jnp.sqrt(v_hat) + AEPS) + WD * w)
    return delta, m2, v2


def _small_update(gp, wp, mp, vp):
    def body(g_ref, w_ref, m_ref, v_ref, all_ref, go_ref, d_ref, mo_ref, vo_ref, send_sems, recv_sems):
        x, y, c = _me()
        me = 4 * x + 2 * y + c
        all_ref[me] = g_ref[...]
        sends = []
        for k in range(1, N_DEV):
            peer, _ = _peer(k)
            cp = pltpu.make_async_remote_copy(src_ref=g_ref, dst_ref=all_ref.at[me], send_sem=send_sems.at[k - 1],
                                              recv_sem=recv_sems.at[k - 1], device_id=peer, device_id_type=MESH)
            cp.start()
            sends.append(cp)
        for k in range(1, N_DEV):
            peer, pidx = _peer(k)
            pltpu.make_async_remote_copy(src_ref=g_ref, dst_ref=all_ref.at[pidx], send_sem=send_sems.at[k - 1],
                                         recv_sem=recv_sems.at[k - 1], device_id=peer, device_id_type=MESH).wait_recv()
        for cp in sends:
            cp.wait_send()
        g = all_ref[0]
        for d in range(1, N_DEV):
            g = g + all_ref[d]
        delta, m2, v2 = _adam(w_ref[...], g, m_ref[...], v_ref[...])
        go_ref[...] = g
        d_ref[...] = delta
        mo_ref[...] = m2
        vo_ref[...] = v2

    vm = pl.BlockSpec(memory_space=pltpu.VMEM)
    vs = jax.ShapeDtypeStruct((NS // 128, 128), F32)
    return pl.pallas_call(
        body, out_shape=(jax.ShapeDtypeStruct((N_DEV, NS // 128, 128), F32), vs, vs, vs, vs), in_specs=[vm] * 4,
        out_specs=(vm,) * 5,
        scratch_shapes=[pltpu.SemaphoreType.DMA((N_DEV - 1,)), pltpu.SemaphoreType.DMA((N_DEV - 1,))],
        name='small_update', compiler_params=_params(None, 48))(gp, wp, mp, vp)


def _big_update(parts, w, m, v, name):
    _, R, C = parts.shape
    tr = R if R % 256 else (128 if C >= 2048 else 256)

    def body(p_ref, w_ref, m_ref, v_ref, g_ref, d_ref, mo_ref, vo_ref):
        g = p_ref[0].astype(F32)
        for d in range(1, N_DEV):
            g = g + p_ref[d].astype(F32)
        delta, m2, v2 = _adam(w_ref[...], g, m_ref[...], v_ref[...])
        g_ref[...] = g
        d_ref[...] = delta
        mo_ref[...] = m2
        vo_ref[...] = v2

    blk = pl.BlockSpec((tr, C), lambda i: (i, 0))
    sh = jax.ShapeDtypeStruct((R, C), F32)
    return pl.pallas_call(
        body, out_shape=(sh, sh, sh, sh), grid=(R // tr,),
        in_specs=[pl.BlockSpec((N_DEV, tr, C), lambda i: (0, i, 0)), blk, blk, blk], out_specs=(blk,) * 4,
        name=name, compiler_params=_params(('parallel',), 48))(parts, w, m, v)


def _ada_update(c_all, dmod_cols, w, m, v):
    C = w.shape[1]
    tr = 256

    def body(c_ref, dm_ref, w_ref, m_ref, v_ref, g_ref, d_ref, mo_ref, vo_ref):
        cv = c_ref[...]
        s = cv * jax.nn.sigmoid(cv)
        g = lax.dot_general(s, dm_ref[...], TN_DIMS, preferred_element_type=F32, precision=HI)
        delta, m2, v2 = _adam(w_ref[...], g, m_ref[...], v_ref[...])
        g_ref[...] = g
        d_ref[...] = delta
        mo_ref[...] = m2
        vo_ref[...] = v2

    blk = pl.BlockSpec((tr, C), lambda i: (i, 0))
    sh = jax.ShapeDtypeStruct((D, C), F32)
    return pl.pallas_call(
        body, out_shape=(sh, sh, sh, sh), grid=(D // tr,),
        in_specs=[pl.BlockSpec((N_DEV, tr), lambda i: (0, i)), pl.BlockSpec((N_DEV, C), lambda i: (0, 0)), blk, blk, blk],
        out_specs=(blk,) * 4, name='ada_update', compiler_params=_params(('parallel',), 48))(c_all, dmod_cols, w, m, v)


def _to_sub(t, d):
    if d == 1:
        return t
    return t.reshape(L // d, d, t.shape[-1]).transpose(1, 0, 2).reshape(L, t.shape[-1])


def _from_sub(t, d):
    if d == 1:
        return t
    return t.reshape(d, L // d, t.shape[-1]).transpose(1, 0, 2).reshape(L, t.shape[-1])


def _rows_to_cluster_lanes(t):
    k = t.shape[1]
    return t.reshape(N_CL, CL_S, k).transpose(0, 2, 1)


def _blockdiag_in(t):
    t = t.reshape(N_CL, CL_G, SSM_N, SSM_P).transpose(0, 1, 3, 2)
    eye = jnp.eye(CL_G, dtype=t.dtype)
    t = t[:, :, :, None, :] * eye[None, :, None, :, None]
    return t.reshape(N_CL, CL_U, CL_S)


def _blockdiag_extract(t):
    t = t.reshape(N_CL, CL_G, SSM_P, CL_G, SSM_N)
    t = jnp.stack([t[:, i, :, i, :] for i in range(CL_G)], axis=1)
    return t.transpose(0, 1, 3, 2).reshape(SSM_GN, SSM_P)


def _c_to_rows(t):
    return t.transpose(0, 2, 1).reshape(SSM_GN, SSM_P)


def _rows_to_c(t):
    return t.reshape(SSM_G, SSM_N, SSM_P).transpose(0, 2, 1)


def _tied(v, deps):
    for t in deps:
        v = v + t[0, 0]
    return v


def _local_step(x, pos, mod, tgt, sp, get_w, emit, first_deps=()):
    sh1, sc1, gt1, sh2, sc2, gt2 = (mod[i:i + 1] for i in range(NMOD))
    vec = lambda n: sp[n].reshape(1, -1)

    rows = lambda n: sp[n].reshape(SSM_GN, 1)
    a_re, a_im = rows('ssm_a_re'), rows('ssm_a_im')
    ldt = jnp.repeat(sp['ssm_log_dt'].reshape(SSM_G, 1), SSM_N, axis=0)
    b_re, b_im = sp['ssm_b_re'].reshape(SSM_GN, SSM_P), sp['ssm_b_im'].reshape(SSM_GN, SSM_P)
    c_re, c_im = _c_to_rows(sp['ssm_c_re'].reshape(SSM_G, SSM_P, SSM_N)), _c_to_rows(sp['ssm_c_im'].reshape(SSM_G, SSM_P, SSM_N))
    bbr, bbi, pwr, pwi = _ssm_pre(a_re, a_im, ldt, b_re, b_im)
    bm = jnp.concatenate([_blockdiag_in(bbr), _blockdiag_in(bbi)], axis=2)
    cmt = jnp.concatenate([_blockdiag_in(c_re), -_blockdiag_in(c_im)], axis=2)
    bmt, cm = bm.transpose(0, 2, 1), cmt.transpose(0, 2, 1)
    pw = jnp.concatenate([_rows_to_cluster_lanes(pwr), _rows_to_cluster_lanes(pwi)], axis=2)
    dvec = vec('ssm_d')
    bm2, cm2 = _hilo(bm), _hilo(cm)
    bmt_b, cmt_b = bmt.astype(BF16), cmt.astype(BF16)

    h1 = _prenorm_fwd(x, vec('g_pre_mix'), sc1, sh1)
    w_in = get_w('w_in', (h1, bm2, cm2, pw, bmt_b, cmt_b))
    proj = _mm(h1, w_in, mode='nn', name='mm_in', tn=1408, deps=first_deps)
    fr1 =ROPE_THETA ** (-jnp.arange(0, ROT_DIM, 2, dtype=F32) / ROT_DIM)
    lane = jnp.arange(128) % HEAD_DIM
    fr = jnp.where(lane < ROT_DIM, fr1[lane % (ROT_DIM // 2)], 0.0).reshape(1, 128).astype(F32)
    qk = _rope(proj[:, :ROPE_W], pos, fr, 1.0, 'rope_fwd')
    v_b = proj[:, ROPE_W:ROPE_W + KVW].astype(BF16)
    u = proj[:, ROPE_W + KVW:]
    qp = jnp.stack([_to_sub(qk[:, gi * KVW:(gi + 1) * KVW], d) for gi, d in enumerate(DILATIONS)])
    kp = jnp.stack([_to_sub(qk[:, QW:], d) for d in DILATIONS])
    vp = jnp.stack([_to_sub(v_b, d) for d in DILATIONS])
    o_p, lse_p = _attn_fwd(qp, kp, vp)
    o3 = jnp.stack([_from_sub(o_p[gi], d) for gi, d in enumerate(DILATIONS)])
    lse3 = jnp.stack([_from_sub(lse_p[gi], d) for gi, d in enumerate(DILATIONS)])
    att = _combine_fwd(o3, lse3, vec('g_attn_out'))

    ypre, bnd = _ssm_fwd(u, bm2, cm2, pw, dvec)
    w_glu = get_w('w_glu', ypre)
    ssm_n = _glu_fwd(ypre, w_glu, vec('b_glu'), vec('g_ssm_out'))

    cat = jnp.concatenate([att, ssm_n], axis=1)
    w_out = get_w('w_out', cat)
    mix = _mm(cat, w_out, mode='nn', name='mm_out', tk=1280)
    x1, h2 = _postmix_fwd(x, mix, vec('g_post_mix'), gt1, vec('g_pre_mlp'), sc2, sh2)
    w_mi = get_w('w_mlp_in', h2)
    a_pre, r_act = _mm(h2, w_mi, mode='nn', name='mm_mlp_in', epilogue='relu2', b_sharded=True)
    w_mo = get_w('w_mlp_out', a_pre)
    y = _mm(r_act, w_mo, mode='nn', name='mm_mlp_out')
    dx2, dy, loss, dgt2, dg_post_mlp = _final_fwd_bwd(x1, y, tgt, vec('g_post_mlp'), gt2)
    dgt2, dg_post_mlp = dgt2[:1], dg_post_mlp[:1]

    da = _mm(dy, w_mo, mode='nt', name='mm_d_act', out_dtype=BF16, epilogue='drelu2', extra=a_pre)
    dep = emit('w_mlp_out', _mm(r_act, dy, mode='tn', name='mm_dw_mlp_out', out_dtype=BF16))
    dh2 = _mm(da, w_mi, mode='nt', name='mm_dh2', tk=1024, b_sharded=True, deps=dep)
    dep = emit('w_mlp_in', _mm(h2, da, mode='tn', name='mm_dw_mlp_in', out_dtype=BF16, out_sharded=True))
    dx1, dmix, dsc2, dsh2, dg_pre_mlp, dgt1, dg_post_mix = _postmix_bwd(
        dx2, dh2, x1, mix, vec('g_post_mix'), gt1, vec('g_pre_mlp'), sc2)
    dsc2, dsh2, dg_pre_mlp, dgt1, dg_post_mix = (t[:1] for t in (dsc2, dsh2, dg_pre_mlp, dgt1, dg_post_mix))
    dcat = _mm(dmix, w_out, mode='nt', name='mm_dcat', tn=1280, deps=dep)
    dep = emit('w_out', _mm(cat, dmix, mode='tn', name='mm_dw_out', out_dtype=BF16, tm=640))
    datt, dsn = dcat[:, :KVW], dcat[:, KVW:]

    dypre, g_w_glu, g_b_glu, g_g_ssm = _glu_bwd(ypre, dsn, w_glu, _tied(vec('b_glu'), dep), vec('g_ssm_out'))
    g_b_glu, g_g_ssm = g_b_glu[:1], g_g_ssm[:1]
    dep = dep + emit('w_glu', g_w_glu.astype(BF16))
    du, dbm, dcm, dA, dD = _ssm_bwd(u, dypre, bm2, bmt_b, cmt_b, pw, dvec, bnd)
    dD = dD[:1]
    gbr, gbi = _blockdiag_extract(dbm[:, :, :CL_S]), _blockdiag_extract(dbm[:, :, CL_S:])
    dcmt = dcm.transpose(0, 2, 1)
    g_c_re = _rows_to_c(_blockdiag_extract(dcmt[:, :, :CL_S]))
    g_c_im = _rows_to_c(-_blockdiag_extract(dcmt[:, :, CL_S:]))
    gar = dA[:, 0, :CL_S].reshape(SSM_GN, 1)
    gai = dA[:, 0, CL_S:].reshape(SSM_GN, 1)
    sel = (jnp.arange(SSM_GN)[None, :] // SSM_N == jnp.arange(SSM_G)[:, None]).astype(F32)
    g_a_re, g_a_im, g_b_re, g_b_im, g_ldt = _ssm_post(a_re, a_im, ldt, b_re, b_im, gar, gai, gbr, gbi, sel)

    head_ones = (jnp.arange(KVW)[:, None] // HEAD_DIM == jnp.arange(KVW)[None, :] // HEAD_DIM).astype(F32)
    do3, dlse3, g_g_attn = _combine_bwd(datt, o3, lse3, vec('g_attn_out'), head_ones)
    g_g_attn = g_g_attn[:1]
    do_p = jnp.stack([_to_sub(do3[gi], d) for gi, d in enumerate(DILATIONS)])
    dlse_p = jnp.stack([_to_sub(dlse3[gi], d) for gi, d in enumerate(DILATIONS)])
    dq_p, dk_p, dv_p = _attn_bwd(qp, kp, vp, o_p, lse_p, do_p, dlse_p)
    dq = jnp.concatenate([_from_sub(dq_p[gi], d) for gi, d in enumerate(DILATIONS)], axis=1)
    dk = sum(_from_sub(dk_p[gi], d) for gi, d in enumerate(DILATIONS))
    dv = sum(_from_sub(dv_p[gi], d) for gi, d in enumerate(DILATIONS))
    dqk = _rope(jnp.concatenate([dq, dk], axis=1), pos, fr, -1.0, 'rope_bwd')
    dproj = jnp.concatenate([dqk, dv.astype(BF16), du.astype(BF16)], axis=1)
    dh1 = _mm(dproj, w_in, mode='nt', name='mm_dh1', tk=1408, deps=dep)
    dep = emit('w_in', _mm(h1, dproj, mode='tn', name='mm_dw_in', out_dtype=BF16, tn=1408))
    grad_x, dsc1, dsh1, dg_pre_mix = _prenorm_bwd(dx1, dh1, x, vec('g_pre_mix'), _tied(sc1, dep))
    dsc1, dsh1, dg_pre_mix = dsc1[:1], dsh1[:1], dg_pre_mix[:1]

    dmod = jnp.concatenate([dsh1, dsc1, dgt1, dsh2, dsc2, dgt2], axis=0)
    small = {'b_ada': dmod, 'g_pre_mix': dg_pre_mix, 'g_post_mix': dg_post_mix, 'ssm_a_re': g_a_re, 'ssm_a_im': g_a_im,
             'ssm_log_dt': g_ldt[:, 0], 'ssm_b_re': g_b_re, 'ssm_b_im': g_b_im, 'ssm_c_re': g_c_re, 'ssm_c_im': g_c_im,
             'ssm_d': dD, 'b_glu': g_b_glu, 'g_attn_out': g_g_attn, 'g_ssm_out': g_g_ssm, 'g_pre_mlp': dg_pre_mlp,
             'g_post_mlp': dg_post_mlp}
    return loss[0, 0], grad_x, small


def _pack(d):
    return jnp.concatenate([jnp.pad(d[n].reshape(-1).astype(F32), (0, SEG[n] - SMALL_SIZES[n])) for n in SMALL])


def _shard_major(t, name):
    if name in ('w_in', 'w_out', 'w_mlp_in'):
        k, n = t.shape
        return t.reshape(k, N_DEV, n // N_DEV).transpose(1, 0, 2)
    k, n = t.shape
    return t.reshape(N_DEV, k // N_DEV, n)


def _from_shard_major(t, name):
    if name in ('w_in', 'w_out', 'w_mlp_in'):
        _, k, n = t.shape
        return t.transpose(1, 0, 2).reshape(k, N_DEV * n)
    _, k, n = t.shape
    return t.reshape(N_DEV * k, n)


def kernel(x, c, positions, w_ada, b_ada, g_pre_mix, g_post_mix, w_in, ssm_a_re, ssm_a_im, ssm_log_dt, ssm_b_re, ssm_b_im, ssm_c_re, ssm_c_im, ssm_d, w_glu, b_glu, g_attn_out, g_ssm_out, w_out, g_pre_mlp, g_post_mlp, w_mlp_in, w_mlp_out, loss_target, m_w_ada, m_b_ada, m_g_pre_mix, m_g_post_mix, m_w_in, m_ssm_a_re, m_ssm_a_im, m_ssm_log_dt, m_ssm_b_re, m_ssm_b_im, m_ssm_c_re, m_ssm_c_im, m_ssm_d, m_w_glu, m_b_glu, m_g_attn_out, m_g_ssm_out, m_w_out, m_g_pre_mlp, m_g_post_mlp, m_w_mlp_in, m_w_mlp_out, v_w_ada, v_b_ada, v_g_pre_mix, v_g_post_mix, v_w_in, v_ssm_a_re, v_ssm_a_im, v_ssm_log_dt, v_ssm_b_re, v_ssm_b_im, v_ssm_c_re, v_ssm_c_im, v_ssm_d, v_w_glu, v_b_glu, v_g_attn_out, v_g_ssm_out, v_w_out, v_g_pre_mlp, v_g_post_mlp, v_w_mlp_in, v_w_mlp_out):
    loc = dict(locals())
    W = {n: loc[n] for n in WEIGHTS}
    M = {n: loc['m_' + n] for n in WEIGHTS}
    V = {n: loc['v_' + n] for n in WEIGHTS}
    assert x.shape == (1, L, D) and w_in.shape == (1, D, INW // N_DEV), (x.shape, w_in.shape)

    cw = NMOD * D // N_DEV
    c_all, mod8 = _mod_exchange(c.reshape(1, 1, D), w_ada[0], b_ada.reshape(N_DEV, 1, cw))
    mod = mod8.reshape(NMOD, D)

    gather = {n: _push_start(W[n][0].astype(BF16), False, mod8, 'gather_start_' + n) for n in BIG}
    mod = mod + sum(gather[n][4][0, 0] for n in BIG)

    def get_w(n, after):
        g = _push_wait(gather[n], after, 'gather_wait_' + n)
        return g if n == 'w_mlp_in' else _from_shard_major(g, n)

    scatter = {}

    def emit(n, g):
        src = g if n == 'w_mlp_in' else _shard_major(g, n)
        scatter[n] = _push_start(src, True, src, 'scatter_start_' + n)
        return (scatter[n][4],)

    sp = {n: W[n][0] for n in SMALL}
    loss, grad_x, small = _local_step(x[0], positions.reshape(L, 1), mod, loss_target[0], sp, get_w, emit)
    loss = lax.psum(loss, ('x', 'y', 'c'))
    out_g, out_d, out_m, out_v = {}, {}, {}, {}

    rows_all, sg, sd, sm, sv = _small_update(
        *[_pack(d).reshape(NS // 128, 128) for d in (small, {n: W[n] for n in SMALL}, {n: M[n] for n in SMALL},
                                                    {n: V[n] for n in SMALL})])
    off = 0
    for n in SMALL:
        sz = SMALL_SIZES[n]
        for dst, src in ((out_g, sg), (out_d, sd), (out_m, sm), (out_v, sv)):
            dst[n] = src[off // 128:(off + SEG[n]) // 128].reshape(-1)[:sz].reshape(W[n].shape[1:])
        off += SEG[n]

    me = 4 * lax.axis_index('x') + 2 * lax.axis_index('y') + lax.axis_index('c')
    dmod_all = rows_all[:, :NMOD * D // 128].reshape(N_DEV, NMOD * D)
    dmod_cols = lax.dynamic_slice_in_dim(dmod_all, me * cw, cw, axis=1)
    out_g['w_ada'], out_d['w_ada'], out_m['w_ada'], out_v['w_ada'] = _ada_update(
        c_all.reshape(N_DEV, D), dmod_cols, w_ada[0], m_w_ada[0], v_w_ada[0])

    after = out_v['w_ada']
    for n in ('w_mlp_out', 'w_mlp_in', 'w_out', 'w_glu', 'w_in'):
        parts = _push_wait(scatter[n], after, 'scatter_wait_' + n)
        out_g[n], out_d[n], out_m[n], out_v[n] = _big_update(parts, W[n][0], M[n][0], V[n][0], 'update_' + n)
        after = out_v[n]

    lead = lambda t: t[None]
    return (loss, grad_x[None], *[lead(out_g[n]) for n in WEIGHTS], *[lead(out_d[n]) for n in WEIGHTS],
            *[lead(out_m[n]) for n in WEIGHTS], *[lead(out_v[n]) for n in WEIGHTS])
```

```python
import functools
import math

import jax
import jax.numpy as jnp
from jax import lax
from jax.experimental import pallas as pl
from jax.experimental.pallas import tpu as pltpu

F32 = jnp.float32
BF16 = jnp.bfloat16
HI = lax.Precision.HIGHEST
MESH = pl.DeviceIdType.MESH

N_DEV = 8
L = 4096
D = 2048
HEAD_DIM = 64
N_GROUPS = 3
DILATIONS = (1, 4, 16)
HEADS = 6
QW = N_GROUPS * HEADS * HEAD_DIM
KVW = HEADS * HEAD_DIM
ROT_DIM = 16
ROPE_THETA = 500000.0
BLK = 128
NBLK = L // BLK
SSMW = D - QW
SSM_P = 16
SSM_G = SSMW // SSM_P
SSM_N = 64
SSM_GN = SSM_G * SSM_N
CL_G = 8
N_CL = SSM_G // CL_G
CL_U = CL_G * SSM_P
CL_S = CL_G * SSM_N
INW = QW + 2 * KVW + SSMW
OUTW = KVW + SSMW
DFF = 4 * D
NMOD = 6
EPS = 1e-6
LR, B1, B2, AEPS, WD, STEP = 0.001, 0.9, 0.999, 1e-08, 0.01, 10

T_SCAN = 512
MB = 2 ** 20

WEIGHTS = ['w_ada', 'b_ada', 'g_pre_mix', 'g_post_mix', 'w_in', 'ssm_a_re', 'ssm_a_im', 'ssm_log_dt',
           'ssm_b_re', 'ssm_b_im', 'ssm_c_re', 'ssm_c_im', 'ssm_d', 'w_glu', 'b_glu', 'g_attn_out',
           'g_ssm_out', 'w_out', 'g_pre_mlp', 'g_post_mlp', 'w_mlp_in', 'w_mlp_out']
BIG = ['w_in', 'w_glu', 'w_out', 'w_mlp_in', 'w_mlp_out']
SMALL = [n for n in WEIGHTS if n not in BIG and n != 'w_ada']
SMALL_SIZES = {'b_ada': NMOD * D, 'g_pre_mix': D, 'g_post_mix': D, 'ssm_a_re': SSM_GN, 'ssm_a_im': SSM_GN,
               'ssm_log_dt': SSM_G, 'ssm_b_re': SSM_GN * SSM_P, 'ssm_b_im': SSM_GN * SSM_P,
               'ssm_c_re': SSM_GN * SSM_P, 'ssm_c_im': SSM_GN * SSM_P, 'ssm_d': SSMW, 'b_glu': SSMW,
               'g_attn_out': KVW, 'g_ssm_out': SSMW, 'g_pre_mlp': D, 'g_post_mlp': D}
SEG = {n: -(-SMALL_SIZES[n] // 1024) * 1024 for n in SMALL}
SMALL_LATE = ['b_ada', 'g_pre_mix']
SMALL_EARLY = [n for n in SMALL if n not in SMALL_LATE]


def _params(sem=None, vmem_mb=None):
    kw = {}
    if sem is not None:
        kw['dimension_semantics'] = sem
    if vmem_mb is not None:
        kw['vmem_limit_bytes'] = vmem_mb * MB
    return pltpu.CompilerParams(**kw)


def _vec(n):
    return pl.BlockSpec((1, n), lambda *_: (0, 0))


def _rms(x):
    return lax.rsqrt(jnp.mean(x * x, axis=-1, keepdims=True) + EPS)


def _rms_bwd(dn, n, r):
    return r * (dn - n * jnp.mean(dn * n, axis=-1, keepdims=True))


def _vec8(n):
    return pl.BlockSpec((8, n), lambda *_: (0, 0))


def _colsum(x):
    return jnp.sum(x.reshape(-1, 8, x.shape[-1]), axis=0)


def _fold8(last, *refs):
    @pl.when(last)
    def _():
        for r in refs:
            r[...] = jnp.broadcast_to(jnp.sum(r[...], axis=0, keepdims=True), r.shape)


def _mm(a, b, *, mode, name, out_dtype=F32, tm=1024, tn=1024, tk=2048, epilogue=None, extra=None,
        b_sharded=False, out_sharded=False, deps=()):
    if mode == 'nn':
        M, K = a.shape
        dims = (((1,), (0,)), ((), ()))
        a_spec = pl.BlockSpec((tm, tk), lambda i, j, k: (i, k))
        if b_sharded:
            _, K2, per = b.shape
            N, q = N_DEV * per, per // tn
            b_spec = pl.BlockSpec((None, tk, tn), lambda i, j, k: (j // q, k, j % q))
        else:
            K2, N = b.shape
            b_spec = pl.BlockSpec((tk, tn), lambda i, j, k: (k, j))
    elif mode == 'nt':
        M, K = a.shape
        dims = (((1,), (1,)), ((), ()))
        a_spec = pl.BlockSpec((tm, tk), lambda i, j, k: (i, k))
        if b_sharded:
            _, N, per = b.shape
            K2, q = N_DEV * per, per // tk
            b_spec = pl.BlockSpec((None, tn, tk), lambda i, j, k: (k // q, j, k % q))
        else:
            N, K2 = b.shape
            b_spec = pl.BlockSpec((tn, tk), lambda i, j, k: (j, k))
    else:
        (K, M), (K2, N) = a.shape, b.shape
        dims = (((0,), (0,)), ((), ()))
        a_spec = pl.BlockSpec((tk, tm), lambda i, j, k: (k, i))
        b_spec = pl.BlockSpec((tk, tn), lambda i, j, k: (k, j))
    assert K == K2 and M % tm == 0 and N % tn == 0 and K % tk == 0, (name, a.shape, b.shape, tm, tn, tk)
    nk = K // tk
    o_spec = pl.BlockSpec((tm, tn), lambda i, j, k: (i, j))
    o_dims = (M, N)
    if out_sharded:
        qo = N // N_DEV // tn
        o_spec = pl.BlockSpec((None, tm, tn), lambda i, j, k: (j // qo, i, j % qo))
        o_dims = (N_DEV, M, N // N_DEV)
    n_out = 2 if epilogue == 'relu2' else 1
    n_extra = 1 if extra is not None else 0
    n_in = 2 + n_extra + len(deps)

    def body(*refs):
        a_ref, b_ref = refs[0], refs[1]
        x_refs = refs[2:2 + n_extra]
        o_refs = refs[n_in:n_in + n_out]
        acc = refs[-1]
        k = pl.program_id(2)

        prod = lax.dot_general(a_ref[...], b_ref[...], dims, preferred_element_type=F32)

        def finish(r):
            if epilogue == 'relu2':
                o_refs[0][...] = r.astype(BF16)
                o_refs[1][...] = jnp.square(jnp.maximum(r, 0.0)).astype(BF16)
            elif epilogue == 'drelu2':
                pre = x_refs[0][...].astype(F32)
                o_refs[0][...] = (r * (2.0 * jnp.maximum(pre, 0.0))).astype(out_dtype)
            else:
                o_refs[0][...] = r.astype(out_dtype)

        if nk == 1:
            finish(prod)
        else:
            @pl.when(k == 0)
            def _():
                acc[...] = prod

            @pl.when((k > 0) & (k < nk - 1))
            def _():
                acc[...] += prod

            @pl.when(k == nk - 1)
            def _():
                finish(acc[...] + prod)

    if epilogue == 'relu2':
        out_shape = (jax.ShapeDtypeStruct((M, N), BF16), jax.ShapeDtypeStruct((M, N), BF16))
        out_specs = (o_spec, o_spec)
    else:
        out_shape = jax.ShapeDtypeStruct(o_dims, out_dtype)
        out_specs = o_spec
    args = (a, b) + ((extra,) if extra is not None else ()) + tuple(deps)
    in_specs = ([a_spec, b_spec] + ([o_spec] if extra is not None else [])
                + [pl.BlockSpec(memory_space=pl.ANY)] * len(deps))
    return pl.pallas_call(
        body, out_shape=out_shape, grid=(M // tm, N // tn, nk), in_specs=in_specs, out_specs=out_specs,
        scratch_shapes=[pltpu.VMEM((tm, tn) if nk > 1 else (8, 128), F32)], name=name,
        compiler_params=_params(('parallel', 'parallel', 'arbitrary'), 56))(*args)


TR = 256


def _rowspec(w=D):
    return pl.BlockSpec((TR, w), lambda i: (i, 0))


def _prenorm_fwd(x, g, sc, sh):
    def body(x_ref, g_ref, sc_ref, sh_ref, h_ref):
        xv = x_ref[...]
        n = xv * _rms(xv)
        h_ref[...] = ((n * g_ref[...]) * (1.0 + sc_ref[...]) + sh_ref[...]).astype(BF16)

    return pl.pallas_call(
        body, out_shape=jax.ShapeDtypeStruct((L, D), BF16), grid=(L // TR,),
        in_specs=[_rowspec(), _vec(D), _vec(D), _vec(D)], out_specs=_rowspec(), name='prenorm_fwd',
        compiler_params=_params(('parallel',), 40))(x, g, sc, sh)


def _postmix_fwd(x, mix, gpm, gt1, gpl, sc2, sh2):
    def body(x_ref, mix_ref, gpm_ref, gt1_ref, gpl_ref, sc2_ref, sh2_ref, x1_ref, h2_ref):
        mix_v = mix_ref[...]
        nm = mix_v * _rms(mix_v)
        x1 = x_ref[...] + gt1_ref[...] * (nm * gpm_ref[...])
        x1_ref[...] = x1
        n2 = x1 * _rms(x1)
        h2_ref[...] = ((n2 * gpl_ref[...]) * (1.0 + sc2_ref[...]) + sh2_ref[...]).astype(BF16)

    return pl.pallas_call(
        body, out_shape=(jax.ShapeDtypeStruct((L, D), F32), jax.ShapeDtypeStruct((L, D), BF16)), grid=(L // TR,),
        in_specs=[_rowspec(), _rowspec()] + [_vec(D)] * 5, out_specs=(_rowspec(), _rowspec()), name='postmix_fwd',
        compiler_params=_params(('parallel',), 40))(x, mix, gpm, gt1, gpl, sc2, sh2)


def _final_fwd_bwd(x1, y, tgt, g, gt2):
    def body(x1_ref, y_ref, t_ref, g_ref, gt2_ref, dx2_ref, dy_ref, loss_ref, dgt2_ref, dg_ref):
        @pl.when(pl.program_id(0) == 0)
        def _():
            loss_ref[...] = jnp.zeros_like(loss_ref)
            dgt2_ref[...] = jnp.zeros_like(dgt2_ref)
            dg_ref[...] = jnp.zeros_like(dg_ref)

        yv = y_ref[...]
        r = _rms(yv)
        n = yv * r
        ng = n * g_ref[...]
        x2 = x1_ref[...] + gt2_ref[...] * ng
        e = x2 - t_ref[...]
        loss_ref[...] += 0.5 * jnp.sum(jnp.mean(e * e, axis=-1, keepdims=True), axis=0, keepdims=True)
        dx2 = e * (1.0 / D)
        dx2_ref[...] = dx2
        dgt2_ref[...] += _colsum(dx2 * ng)
        dng = dx2 * gt2_ref[...]
        dg_ref[...] += _colsum(dng * n)
        dy_ref[...] = _rms_bwd(dng * g_ref[...], n, r).astype(BF16)
        _fold8(pl.program_id(0) == L // TR - 1, dgt2_ref, dg_ref)

    return pl.pallas_call(
        body,
        out_shape=(jax.ShapeDtypeStruct((L, D), F32), jax.ShapeDtypeStruct((L, D), BF16),
                   jax.ShapeDtypeStruct((8, 128), F32), jax.ShapeDtypeStruct((8, D), F32),
                   jax.ShapeDtypeStruct((8, D), F32)),
        grid=(L // TR,), in_specs=[_rowspec(), _rowspec(), _rowspec(), _vec(D), _vec(D)],
        out_specs=(_rowspec(), _rowspec(), _vec8(128), _vec8(D), _vec8(D)), name='final_fwd_bwd',
        compiler_params=_params(('arbitrary',), 40))(x1, y, tgt, g, gt2)


def _postmix_bwd(dx2, dh2, x1, mix, gpm, gt1, gpl, sc2):
    def body(dx2_ref, dh2_ref, x1_ref, mix_ref, gpm_ref, gt1_ref, gpl_ref, sc2_ref,
             dx1_ref, dmix_ref, dsc2_ref, dsh2_ref, dgpl_ref, dgt1_ref, dgpm_ref):
        @pl.when(pl.program_id(0) == 0)
        def _():
            for r_ in (dsc2_ref, dsh2_ref, dgpl_ref, dgt1_ref, dgpm_ref):
                r_[...] = jnp.zeros_like(r_)

        x1v = x1_ref[...]
        r2 = _rms(x1v)
        n2 = x1v * r2
        dh2v = dh2_ref[...]
        dsh2_ref[...] += _colsum(dh2v)
        dsc2_ref[...] += _colsum(dh2v * (n2 * gpl_ref[...]))
        t = dh2v * (1.0 + sc2_ref[...])
        dgpl_ref[...] += _colsum(t * n2)
        dx1 = dx2_ref[...] + _rms_bwd(t * gpl_ref[...], n2, r2)
        dx1_ref[...] = dx1
        mix_v = mix_ref[...]
        rm = _rms(mix_v)
        nm = mix_v * rm
        dgt1_ref[...] += _colsum(dx1 * (nm * gpm_ref[...]))
        u = dx1 * gt1_ref[...]
        dgpm_ref[...] += _colsum(u * nm)
        dmix_ref[...] = _rms_bwd(u * gpm_ref[...], nm, rm).astype(BF16)
        _fold8(pl.program_id(0) == L // TR - 1, dsc2_ref, dsh2_ref, dgpl_ref, dgt1_ref, dgpm_ref)

    vs = jax.ShapeDtypeStruct((8, D), F32)
    return pl.pallas_call(
        body, out_shape=(jax.ShapeDtypeStruct((L, D), F32), jax.ShapeDtypeStruct((L, D), BF16), vs, vs, vs, vs, vs),
        grid=(L // TR,), in_specs=[_rowspec()] * 4 + [_vec(D)] * 4,
        out_specs=(_rowspec(), _rowspec()) + (_vec8(D),) * 5, name='postmix_bwd',
        compiler_params=_params(('arbitrary',), 48))(dx2, dh2, x1, mix, gpm, gt1, gpl, sc2)


def _prenorm_bwd(dx1, dh1, x, g, sc1):
    def body(dx1_ref, dh1_ref, x_ref, g_ref, sc1_ref, dx_ref, dsc1_ref, dsh1_ref, dg_ref):
        @pl.when(pl.program_id(0) == 0)
        def _():
            for r_ in (dsc1_ref, dsh1_ref, dg_ref):
                r_[...] = jnp.zeros_like(r_)

        xv = x_ref[...]
        r = _rms(xv)
        n = xv * r
        dh = dh1_ref[...]
        dsh1_ref[...] += _colsum(dh)
        dsc1_ref[...] += _colsum(dh * (n * g_ref[...]))
        t = dh * (1.0 + sc1_ref[...])
        dg_ref[...] += _colsum(t * n)
        dx_ref[...] = dx1_ref[...] + _rms_bwd(t * g_ref[...], n, r)
        _fold8(pl.program_id(0) == L // TR - 1, dsc1_ref, dsh1_ref, dg_ref)

    vs = jax.ShapeDtypeStruct((8, D), F32)
    return pl.pallas_call(
        body, out_shape=(jax.ShapeDtypeStruct((L, D), F32), vs, vs, vs), grid=(L // TR,),
        in_specs=[_rowspec()] * 3 + [_vec(D)] * 2, out_specs=(_rowspec(),) + (_vec8(D),) * 3, name='prenorm_bwd',
        compiler_params=_params(('arbitrary',), 40))(dx1, dh1, x, g, sc1)


ROPE_W = QW + KVW


def _rope(xin, pos, fr, sign, name, carry=0):
    def body(x_ref, pos_ref, fr_ref, o_ref):
        xv = x_ref[:, 0:ROPE_W]
        ang = pos_ref[...].astype(F32) * fr_ref[...]
        w = lax.broadcasted_iota(jnp.int32, (1, 128), 1) % HEAD_DIM
        cs = jnp.cos(ang)
        sn = jnp.sin(ang) * sign
        s1 = jnp.where(w < ROT_DIM // 2, -sn, 0.0)
        s2 = jnp.where((w >= ROT_DIM // 2) & (w < ROT_DIM), sn, 0.0)
        rep = ROPE_W // 128
        cs, s1, s2 = jnp.tile(cs, (1, rep)), jnp.tile(s1, (1, rep)), jnp.tile(s2, (1, rep))
        hi = pltpu.roll(xv, ROPE_W - ROT_DIM // 2, 1)
        lo = pltpu.roll(xv, ROT_DIM // 2, 1)
        o_ref[:, 0:ROPE_W] = (xv * cs + hi * s1 + lo * s2).astype(BF16)
        if carry:
            o_ref[:, ROPE_W:ROPE_W + carry] = x_ref[:, ROPE_W:ROPE_W + carry].astype(BF16)

    return pl.pallas_call(
        body, out_shape=jax.ShapeDtypeStruct((L, ROPE_W + carry), BF16), grid=(L // TR,),
        in_specs=[_rowspec(ROPE_W + carry), pl.BlockSpec((TR, 1), lambda i: (i, 0)), _vec(128)],
        out_specs=_rowspec(ROPE_W + carry), name=name, compiler_params=_params(('parallel',), 40))(xin, pos, fr)


def _attn_mask(g, b):
    nbs = lax.shift_right_logical(jnp.int32(NBLK), 2 * g)
    first = (b & (nbs - 1)) == 0
    qi = lax.broadcasted_iota(jnp.int32, (BLK, 2 * BLK), 0)
    kj = lax.broadcasted_iota(jnp.int32, (BLK, 2 * BLK), 1)
    dist = qi + BLK - kj
    return (dist >= 0) & (dist <= BLK) & ((kj >= BLK) | jnp.logical_not(first))


def _blk(idx):
    return pl.BlockSpec((None, BLK, KVW), idx)


_CUR = lambda g, b: (g, b, 0)
_PREV = lambda g, b: (g, jnp.maximum(b - 1, 0), 0)
NEG = -1e30
NT_DIMS = (((1,), (1,)), ((), ()))
TN_DIMS = (((0,), (0,)), ((), ()))


def _attn_fwd(qp, kp, vp):
    def body(q_ref, kp_ref, kc_ref, vp_ref, vc_ref, o_ref, lse_ref):
        valid = _attn_mask(pl.program_id(0), pl.program_id(1))
        for h in range(HEADS):
            hs = slice(h * HEAD_DIM, (h + 1) * HEAD_DIM)
            q = q_ref[:, hs]
            kc = jnp.concatenate([kp_ref[:, hs], kc_ref[:, hs]], axis=0)
            vc = jnp.concatenate([vp_ref[:, hs], vc_ref[:, hs]], axis=0)
            s = lax.dot_general(q, kc, NT_DIMS, preferred_element_type=F32) * 0.125
            s = jnp.where(valid, s, NEG)
            m = jnp.max(s, axis=-1, keepdims=True)
            p = jnp.exp(s - m)
            l = jnp.sum(p, axis=-1, keepdims=True)
            o = jnp.dot(p.astype(BF16), vc, preferred_element_type=F32) / l
            o_ref[:, hs] = o
            lse_ref[:, hs] = jnp.broadcast_to(m + jnp.log(l), (BLK, HEAD_DIM))

    sh = jax.ShapeDtypeStruct((N_GROUPS, L, KVW), F32)
    return pl.pallas_call(
        body, out_shape=(sh, sh), grid=(N_GROUPS, NBLK),
        in_specs=[_blk(_CUR), _blk(_PREV), _blk(_CUR), _blk(_PREV), _blk(_CUR)],
        out_specs=(_blk(_CUR), _blk(_CUR)), name='attn_fwd',
        compiler_params=_params(('parallel', 'parallel'), 32))(qp, kp, kp, vp, vp)


def _attn_bwd(qp, kp, vp, o, lse, do, dlse):
    def body(q_ref, kp_ref, kc_ref, vp_ref, vc_ref, o_ref, lse_ref, do_ref, dlse_ref, dq_ref, dk_ref, dv_ref):
        b = pl.program_id(1)

        @pl.when(b == 0)
        def _():
            dk_ref[...] = jnp.zeros_like(dk_ref)
            dv_ref[...] = jnp.zeros_like(dv_ref)

        valid = _attn_mask(pl.program_id(0), b)
        prev0 = pl.multiple_of(jnp.maximum(b - 1, 0) * BLK, BLK)
        cur0 = pl.multiple_of(b * BLK, BLK)
        for h in range(HEADS):
            hs = slice(h * HEAD_DIM, (h + 1) * HEAD_DIM)
            q = q_ref[:, hs]
            kc = jnp.concatenate([kp_ref[:, hs], kc_ref[:, hs]], axis=0)
            vc = jnp.concatenate([vp_ref[:, hs], vc_ref[:, hs]], axis=0)
            s = lax.dot_general(q, kc, NT_DIMS, preferred_element_type=F32) * 0.125
            s = jnp.where(valid, s, NEG)
            p = jnp.exp(s - lse_ref[:, h * HEAD_DIM:h * HEAD_DIM + 1])
            do_h = do_ref[:, hs]
            delta = jnp.sum(do_h * o_ref[:, hs], axis=-1, keepdims=True)
            do_b = do_h.astype(BF16)
            dp = lax.dot_general(do_b, vc, NT_DIMS, preferred_element_type=F32)
            ds = p * (dp - delta + dlse_ref[:, h * HEAD_DIM:h * HEAD_DIM + 1])
            ds_b = (ds * 0.125).astype(BF16)
            dq_ref[:, hs] = jnp.dot(ds_b, kc, preferred_element_type=F32)
            dkc = lax.dot_general(ds_b, q, TN_DIMS, preferred_element_type=F32)
            dvc = lax.dot_general(p.astype(BF16), do_b, TN_DIMS, preferred_element_type=F32)
            dk_ref[pl.ds(prev0, BLK), hs] += dkc[:BLK]
            dv_ref[pl.ds(prev0, BLK), hs] += dvc[:BLK]
            dk_ref[pl.ds(cur0, BLK), hs] += dkc[BLK:]
            dv_ref[pl.ds(cur0, BLK), hs] += dvc[BLK:]

    sh = jax.ShapeDtypeStruct((N_GROUPS, L, KVW), F32)
    whole = pl.BlockSpec((None, L, KVW), lambda g, b: (g, 0, 0))
    return pl.pallas_call(
        body, out_shape=(sh, sh, sh), grid=(N_GROUPS, NBLK),
        in_specs=[_blk(_CUR), _blk(_PREV), _blk(_CUR), _blk(_PREV), _blk(_CUR)] + [_blk(_CUR)] * 4,
        out_specs=(_blk(_CUR), whole, whole), name='attn_bwd',
        compiler_params=_params(('arbitrary', 'arbitrary'), 48))(qp, kp, kp, vp, vp, o, lse, do, dlse)


TC = 512


def _comb_spec():
    return pl.BlockSpec((N_GROUPS, TC, KVW), lambda i: (0, i, 0))


def _combine_weights(lse_ref):
    l0, l1, l2 = lse_ref[0], lse_ref[1], lse_ref[2]
    m = jnp.maximum(jnp.maximum(l0, l1), l2)
    e0, e1, e2 = jnp.exp(l0 - m), jnp.exp(l1 - m), jnp.exp(l2 - m)
    z = e0 + e1 + e2
    return e0 / z, e1 / z, e2 / z


def _combine_fwd(o3, lse3, g):
    def body(o_ref, lse_ref, g_ref, att_ref):
        w0, w1, w2 = _combine_weights(lse_ref)
        a = w0 * o_ref[0] + w1 * o_ref[1] + w2 * o_ref[2]
        att_ref[...] = ((a * _rms(a)) * g_ref[...]).astype(BF16)

    return pl.pallas_call(
        body, out_shape=jax.ShapeDtypeStruct((L, KVW), BF16), grid=(L // TC,),
        in_specs=[_comb_spec(), _comb_spec(), _vec(KVW)], out_specs=pl.BlockSpec((TC, KVW), lambda i: (i, 0)),
        name='combine_fwd', compiler_params=_params(('parallel',), 40))(o3, lse3, g)


def _combine_bwd(datt, o3, lse3, g, head_ones):
    def body(datt_ref, o_ref, lse_ref, g_ref, e_ref, do_ref, dlse_ref, dg_ref):
        @pl.when(pl.program_id(0) == 0)
        def _():
            dg_ref[...] = jnp.zeros_like(dg_ref)

        ws = _combine_weights(lse_ref)
        a = ws[0] * o_ref[0] + ws[1] * o_ref[1] + ws[2] * o_ref[2]
        r = _rms(a)
        n = a * r
        dv = datt_ref[...]
        dg_ref[...] += _colsum(dv * n)
        da = _rms_bwd(dv * g_ref[...], n, r)
        dws = [jnp.dot(da * o_ref[i], e_ref[...], preferred_element_type=F32, precision=HI) for i in range(3)]
        dbar = ws[0] * dws[0] + ws[1] * dws[1] + ws[2] * dws[2]
        for i in range(3):
            do_ref[i] = ws[i] * da
            dlse_ref[i] = ws[i] * (dws[i] - dbar)
        _fold8(pl.program_id(0) == L // TC - 1, dg_ref)

    sh = jax.ShapeDtypeStruct((N_GROUPS, L, KVW), F32)
    return pl.pallas_call(
        body, out_shape=(sh, sh, jax.ShapeDtypeStruct((8, KVW), F32)), grid=(L // TC,),
        in_specs=[pl.BlockSpec((TC, KVW), lambda i: (i, 0)), _comb_spec(), _comb_spec(), _vec(KVW),
                  pl.BlockSpec((KVW, KVW), lambda i: (0, 0))],
        out_specs=(_comb_spec(), _comb_spec(), _vec8(KVW)), name='combine_bwd',
        compiler_params=_params(('arbitrary',), 48))(datt, o3, lse3, g, head_ones)


def _ssm_disc(ar, ai, ldt):
    dt = jnp.exp(ldt)
    zr, zi = ar * dt, ai * dt
    ez = jnp.exp(zr)
    A_r, A_i = ez * jnp.cos(zi), ez * jnp.sin(zi)
    den = ar * ar + ai * ai
    xr, xi = A_r - 1.0, A_i
    cr = (xr * ar + xi * ai) / den
    ci = (xi * ar - xr * ai) / den
    return dt, zr, zi, A_r, A_i, den, cr, ci


def _ssm_pre(ar, ai, ldt, br, bi):
    def body(ar_ref, ai_ref, ldt_ref, br_ref, bi_ref, bbr_ref, bbi_ref, pwr_ref, pwi_ref):
        _, zr, zi, _, _, _, cr, ci = _ssm_disc(ar_ref[...], ai_ref[...], ldt_ref[...])
        bbr_ref[...] = cr * br_ref[...] - ci * bi_ref[...]
        bbi_ref[...] = cr * bi_ref[...] + ci * br_ref[...]
        k = (lax.broadcasted_iota(jnp.int32, (1, 8), 1) + 1).astype(F32)
        ek = jnp.exp(zr * k)
        pwr_ref[...] = ek * jnp.cos(zi * k)
        pwi_ref[...] = ek * jnp.sin(zi * k)

    s16 = jax.ShapeDtypeStruct((SSM_GN, SSM_P), F32)
    s8 = jax.ShapeDtypeStruct((SSM_GN, 8), F32)
    return pl.pallas_call(body, out_shape=(s16, s16, s8, s8), name='ssm_pre',
                          compiler_params=_params(None, 40))(ar, ai, ldt, br, bi)


def _ssm_post(ar, ai, ldt, br, bi, gar, gai, gbr, gbi, sel):
    def body(ar_ref, ai_ref, ldt_ref, br_ref, bi_ref, gar_ref, gai_ref, gbr_ref, gbi_ref, sel_ref,
             dar_ref, dai_ref, dbr_ref, dbi_ref, dldt_ref):
        a_r, a_i = ar_ref[...], ai_ref[...]
        dt, _, _, A_r, A_i, den, cr, ci = _ssm_disc(a_r, a_i, ldt_ref[...])
        b_r, b_i, g_br, g_bi = br_ref[...], bi_ref[...], gbr_ref[...], gbi_ref[...]
        gcr = jnp.sum(g_br * b_r + g_bi * b_i, axis=-1, keepdims=True)
        gci = jnp.sum(g_bi * b_r - g_br * b_i, axis=-1, keepdims=True)
        dbr_ref[...] = g_br * cr + g_bi * ci
        dbi_ref[...] = g_bi * cr - g_br * ci
        g_ar = gar_ref[...] + (gcr * a_r - gci * a_i) / den
        g_ai = gai_ref[...] + (gcr * a_i + gci * a_r) / den
        qr = (cr * a_r + ci * a_i) / den
        qi = (ci * a_r - cr * a_i) / den
        glr = -(gcr * qr + gci * qi)
        gli = -(gci * qr - gcr * qi)
        gzr = g_ar * A_r + g_ai * A_i
        gzi = g_ai * A_r - g_ar * A_i
        dar_ref[...] = glr + gzr * dt
        dai_ref[...] = gli + gzi * dt
        gdt = (gzr * a_r + gzi * a_i) * dt
        dldt_ref[...] = jnp.dot(sel_ref[...], jnp.broadcast_to(gdt, (SSM_GN, 128)),
                                preferred_element_type=F32, precision=HI)

    s1 = jax.ShapeDtypeStruct((SSM_GN, 1), F32)
    s16 = jax.ShapeDtypeStruct((SSM_GN, SSM_P), F32)
    return pl.pallas_call(body, out_shape=(s1, s1, s16, s16, jax.ShapeDtypeStruct((SSM_G, 128), F32)),
                          name='ssm_post', compiler_params=_params(None, 48))(
                              ar, ai, ldt, br, bi, gar, gai, gbr, gbi, sel)


SCAN_CH = 8


def _scan_fwd_tiles(s_ref, pw, carry):
    pwr, pwi = pw[:, :CL_S], pw[:, CL_S:]
    row = lax.broadcasted_iota(jnp.int32, (8, CL_S), 0)
    steps = [(k, jnp.where(row >= k, pwr[k - 1:k], 0.0), jnp.where(row >= k, pwi[k - 1:k], 0.0)) for k in (1, 2, 4)]
    rows = 8 * SCAN_CH

    def chunk(i, c):
        cr, ci = c
        r0 = pl.multiple_of(i * rows, rows)
        xr = s_ref[pl.ds(r0, rows), 0:CL_S].reshape(SCAN_CH, 8, CL_S)
        xi = s_ref[pl.ds(r0, rows), CL_S:2 * CL_S].reshape(SCAN_CH, 8, CL_S)
        for k, pr, pi in steps:
            sr, si = pltpu.roll(xr, k, 1), pltpu.roll(xi, k, 1)
            xr, xi = xr + pr * sr - pi * si, xi + pr * si + pi * sr
        for j in range(SCAN_CH):
            tr = xr[j] + pwr * cr - pwi * ci
            ti = xi[j] + pwr * ci + pwi * cr
            s_ref[pl.ds(r0 + 8 * j, 8), 0:CL_S] = tr
            s_ref[pl.ds(r0 + 8 * j, 8), CL_S:2 * CL_S] = ti
            cr, ci = tr[7:8], ti[7:8]
        return cr, ci

    return lax.fori_loop(0, T_SCAN // rows, chunk, (carry[:, :CL_S], carry[:, CL_S:]))


def _scan_bwd_tiles(l_ref, pw, carry):
    pwr, pwi = pw[:, :CL_S], pw[:, CL_S:]
    rpr = jnp.concatenate([pwr[7 - r:8 - r] for r in range(8)], axis=0)
    rpi = jnp.concatenate([pwi[7 - r:8 - r] for r in range(8)], axis=0)
    row = lax.broadcasted_iota(jnp.int32, (8, CL_S), 0)
    steps = [(k, jnp.where(row < 8 - k, pwr[k - 1:k], 0.0), jnp.where(row < 8 - k, pwi[k - 1:k], 0.0))
             for k in (1, 2, 4)]
    rows = 8 * SCAN_CH
    nc = T_SCAN // rows

    def chunk(i, c):
        cr, ci = c
        r0 = pl.multiple_of((nc - 1 - i) * rows, rows)
        xr = l_ref[pl.ds(r0, rows), 0:CL_S].reshape(SCAN_CH, 8, CL_S)
        xi = l_ref[pl.ds(r0, rows), CL_S:2 * CL_S].reshape(SCAN_CH, 8, CL_S)
        for k, pr, pi in steps:
            sr, si = pltpu.roll(xr, 8 - k, 1), pltpu.roll(xi, 8 - k, 1)
            xr, xi = xr + pr * sr + pi * si, xi + pr * si - pi * sr
        for j in reversed(range(SCAN_CH)):
            tr = xr[j] + rpr * cr + rpi * ci
            ti = xi[j] + rpr * ci - rpi * cr
            l_ref[pl.ds(r0 + 8 * j, 8), 0:CL_S] = tr
            l_ref[pl.ds(r0 + 8 * j, 8), CL_S:2 * CL_S] = ti
            cr, ci = tr[0:1], ti[0:1]
        return cr, ci

    return lax.fori_loop(0, nc, chunk, (carry[:, :CL_S], carry[:, CL_S:]))


NT_SCAN = L // T_SCAN


def _hilo(t):
    hi = t.astype(BF16)
    return jnp.stack([hi, (t - hi.astype(F32)).astype(BF16)], axis=1)


def _dot3(a, b_ref):
    ah = a.astype(BF16)
    al = (a - ah.astype(F32)).astype(BF16)
    bh, bl = b_ref[0], b_ref[1]
    return (jnp.dot(ah, bh, preferred_element_type=F32) + jnp.dot(al, bh, preferred_element_type=F32)
            + jnp.dot(ah, bl, preferred_element_type=F32))


def _hl_spec(r, c):
    return pl.BlockSpec((None, 2, r, c), lambda c_, t: (c_, 0, 0, 0))


def _ssm_fwd(u, bm2, cm2, pw, dvec, u_off=0):
    def body(u_ref, bm_ref, cm_ref, pw_ref, d_ref, y_ref, bnd_ref, s_ref, carry_ref):
        @pl.when(pl.program_id(1) == 0)
        def _():
            carry_ref[...] = jnp.zeros_like(carry_ref)

        bnd_ref[...] = carry_ref[...]
        uv = u_ref[...]
        s_ref[...] = _dot3(uv, bm_ref)
        cr, ci = _scan_fwd_tiles(s_ref, pw_ref[...], carry_ref[...])
        carry_ref[...] = jnp.concatenate([cr, ci], axis=1)
        y_ref[...] = _dot3(s_ref[...], cm_ref) + d_ref[...] * uv

    return pl.pallas_call(
        body,
        out_shape=(jax.ShapeDtypeStruct((L, SSMW), F32), jax.ShapeDtypeStruct((N_CL, NT_SCAN, 1, 2 * CL_S), F32)),
        grid=(N_CL, NT_SCAN),
        in_specs=[pl.BlockSpec((T_SCAN, CL_U), lambda c, t: (t, c + u_off)),
                  _hl_spec(CL_U, 2 * CL_S), _hl_spec(2 * CL_S, CL_U),
                  pl.BlockSpec((None, 8, 2 * CL_S), lambda c, t: (c, 0, 0)),
                  pl.BlockSpec((1, CL_U), lambda c, t: (0, c))],
        out_specs=(pl.BlockSpec((T_SCAN, CL_U), lambda c, t: (t, c)),
                   pl.BlockSpec((None, None, 1, 2 * CL_S), lambda c, t: (c, t, 0, 0))),
        scratch_shapes=[pltpu.VMEM((T_SCAN, 2 * CL_S), F32), pltpu.VMEM((1, 2 * CL_S), F32)],
        name='ssm_fwd', compiler_params=_params(('arbitrary', 'arbitrary'), 40))(u, bm2, cm2, pw, dvec)


def _ssm_bwd(u, dy, bm2, bmt, cmt, pw, dvec, bnd, u_off=0):
    rev = lambda t: NT_SCAN - 1 - t

    def body(u_ref, dy_ref, bm_ref, bmt_ref, cmt_ref, pw_ref, d_ref, bnd_ref,
             du_ref, dbm_ref, dcm_ref, da_ref, dd_ref, s_ref, l_ref, carry_ref):
        @pl.when(pl.program_id(1) == 0)
        def _():
            carry_ref[...] = jnp.zeros_like(carry_ref)
            dbm_ref[...] = jnp.zeros_like(dbm_ref)
            dcm_ref[...] = jnp.zeros_like(dcm_ref)
            da_ref[...] = jnp.zeros_like(da_ref)
            dd_ref[...] = jnp.zeros_like(dd_ref)

        uv, dyv, pw = u_ref[...], dy_ref[...], pw_ref[...]
        dy_b = dyv.astype(BF16)
        entry = bnd_ref[...]
        s_ref[...] = _dot3(uv, bm_ref)
        _scan_fwd_tiles(s_ref, pw, entry)
        l_ref[...] = jnp.dot(dy_b, cmt_ref[...], preferred_element_type=F32)
        cr, ci = _scan_bwd_tiles(l_ref, pw, carry_ref[...])
        carry_ref[...] = jnp.concatenate([cr, ci], axis=1)
        sv, lv = s_ref[...], l_ref[...]
        lv_b = lv.astype(BF16)
        du_ref[...] = dyv * d_ref[...] + jnp.dot(lv_b, bmt_ref[...], preferred_element_type=F32)
        dbm_ref[...] += lax.dot_general(uv.astype(BF16), lv_b, TN_DIMS, preferred_element_type=F32)
        dcm_ref[...] += lax.dot_general(sv.astype(BF16), dy_b, TN_DIMS, preferred_element_type=F32)
        dd_ref[...] += _colsum(dyv * uv)
        row = lax.broadcasted_iota(jnp.int32, (T_SCAN, 2 * CL_S), 0)
        sp = jnp.where(row == 0, entry, pltpu.roll(sv, 1, 0))
        spr, spi = sp[:, :CL_S], sp[:, CL_S:]
        lr, li = lv[:, :CL_S], lv[:, CL_S:]
        da_ref[:, 0:CL_S] += _colsum(lr * spr + li * spi)
        da_ref[:, CL_S:2 * CL_S] += _colsum(li * spr - lr * spi)
        _fold8(pl.program_id(1) == NT_SCAN - 1, da_ref, dd_ref)

    return pl.pallas_call(
        body,
        out_shape=(jax.ShapeDtypeStruct((L, SSMW), F32), jax.ShapeDtypeStruct((N_CL, CL_U, 2 * CL_S), F32),
                   jax.ShapeDtypeStruct((N_CL, 2 * CL_S, CL_U), F32), jax.ShapeDtypeStruct((N_CL, 8, 2 * CL_S), F32),
                   jax.ShapeDtypeStruct((8, SSMW), F32)),
        grid=(N_CL, NT_SCAN),
        in_specs=[pl.BlockSpec((T_SCAN, CL_U), lambda c, t: (rev(t), c + u_off)),
                  pl.BlockSpec((T_SCAN, CL_U), lambda c, t: (rev(t), c)),
                  _hl_spec(CL_U, 2 * CL_S),
                  pl.BlockSpec((None, 2 * CL_S, CL_U), lambda c, t: (c, 0, 0)),
                  pl.BlockSpec((None, CL_U, 2 * CL_S), lambda c, t: (c, 0, 0)),
                  pl.BlockSpec((None, 8, 2 * CL_S), lambda c, t: (c, 0, 0)),
                  pl.BlockSpec((1, CL_U), lambda c, t: (0, c)),
                  pl.BlockSpec((None, None, 1, 2 * CL_S), lambda c, t: (c, rev(t), 0, 0))],
        out_specs=(pl.BlockSpec((T_SCAN, CL_U), lambda c, t: (rev(t), c)),
                   pl.BlockSpec((None, CL_U, 2 * CL_S), lambda c, t: (c, 0, 0)),
                   pl.BlockSpec((None, 2 * CL_S, CL_U), lambda c, t: (c, 0, 0)),
                   pl.BlockSpec((None, 8, 2 * CL_S), lambda c, t: (c, 0, 0)),
                   pl.BlockSpec((8, CL_U), lambda c, t: (0, c))),
        scratch_shapes=[pltpu.VMEM((T_SCAN, 2 * CL_S), F32), pltpu.VMEM((T_SCAN, 2 * CL_S), F32),
                        pltpu.VMEM((1, 2 * CL_S), F32)],
        name='ssm_bwd', compiler_params=_params(('arbitrary', 'arbitrary'), 48))(u, dy, bm2, bmt, cmt, pw, dvec, bnd)


GELU_C = math.sqrt(2.0 / math.pi)
GELU_K = 0.044715


def _gelu_parts(x):
    t = jnp.tanh(GELU_C * (x + GELU_K * (x * x * x)))
    return x * (0.5 * (1.0 + t)), t


def _glu_fwd(ypre, wglu, bglu, gs):
    def body(y_ref, w_ref, b_ref, g_ref, o_ref):
        yg, _ = _gelu_parts(y_ref[...])
        z = jnp.dot(yg.astype(BF16), w_ref[...], preferred_element_type=F32) + b_ref[...]
        s = yg * jax.nn.sigmoid(z)
        o_ref[...] = ((s * _rms(s)) * g_ref[...]).astype(BF16)

    return pl.pallas_call(
        body, out_shape=jax.ShapeDtypeStruct((L, SSMW), BF16), grid=(L // TR,),
        in_specs=[_rowspec(SSMW), pl.BlockSpec((SSMW, SSMW), lambda i: (0, 0)), _vec(SSMW), _vec(SSMW)],
        out_specs=_rowspec(SSMW), name='glu_fwd', compiler_params=_params(('parallel',), 32))(ypre, wglu, bglu, gs)


def _glu_bwd(ypre, dsn, wglu, bglu, gs):
    def body(y_ref, d_ref, w_ref, b_ref, g_ref, dy_ref, dw_ref, db_ref, dg_ref):
        @pl.when(pl.program_id(0) == 0)
        def _():
            dw_ref[...] = jnp.zeros_like(dw_ref)
            db_ref[...] = jnp.zeros_like(db_ref)
            dg_ref[...] = jnp.zeros_like(dg_ref)

        xv = y_ref[...]
        yg, t = _gelu_parts(xv)
        yg_b = yg.astype(BF16)
        z = jnp.dot(yg_b, w_ref[...], preferred_element_type=F32) + b_ref[...]
        sg = jax.nn.sigmoid(z)
        s = yg * sg
        r = _rms(s)
        n = s * r
        dv = d_ref[...]
        dg_ref[...] += _colsum(dv * n)
        ds = _rms_bwd(dv * g_ref[...], n, r)
        dz = (ds * yg) * (sg * (1.0 - sg))
        dz_b = dz.astype(BF16)
        db_ref[...] += _colsum(dz)
        dw_ref[...] += lax.dot_general(yg_b, dz_b, TN_DIMS, preferred_element_type=F32)
        dyg = ds * sg + lax.dot_general(dz_b, w_ref[...], NT_DIMS, preferred_element_type=F32)
        dgelu = 0.5 * (1.0 + t) + (0.5 * xv) * (1.0 - t * t) * (GELU_C * (1.0 + 3.0 * GELU_K * (xv * xv)))
        dy_ref[...] = dyg * dgelu
        _fold8(pl.program_id(0) == L // TR - 1, db_ref, dg_ref)

    vs = jax.ShapeDtypeStruct((8, SSMW), F32)
    return pl.pallas_call(
        body, out_shape=(jax.ShapeDtypeStruct((L, SSMW), F32), jax.ShapeDtypeStruct((SSMW, SSMW), F32), vs, vs),
        grid=(L // TR,),
        in_specs=[_rowspec(SSMW), _rowspec(SSMW), pl.BlockSpec((SSMW, SSMW), lambda i: (0, 0)), _vec(SSMW), _vec(SSMW)],
        out_specs=(_rowspec(SSMW), pl.BlockSpec((SSMW, SSMW), lambda i: (0, 0)), _vec8(SSMW), _vec8(SSMW)),
        name='glu_bwd', compiler_params=_params(('arbitrary',), 40))(ypre, dsn, wglu, bglu, gs)


def _me():
    return lax.axis_index('x'), lax.axis_index('y'), lax.axis_index('c')


def _my_index():
    return 4 * lax.axis_index('x') + 2 * lax.axis_index('y') + lax.axis_index('c')


def _peer(k):
    x, y, c = _me()
    px = 1 - x if k & 4 else x
    py = 1 - y if k & 2 else y
    pc = 1 - c if k & 1 else c
    return (px, py, pc), 4 * px + 2 * py + pc


def _mod_exchange(c_row, w_ada, b_ada8):
    cw = NMOD * D // N_DEV

    def body(c_ref, w_ref, b_ref, call_ref, mod_ref, part_ref, send_sems, recv_sems):
        x, y, c = _me()
        me = 4 * x + 2 * y + c
        call_ref[me] = c_ref[0]
        sends = []
        for k in range(1, N_DEV):
            peer, _ = _peer(k)
            cp = pltpu.make_async_remote_copy(src_ref=c_ref.at[0], dst_ref=call_ref.at[me], send_sem=send_sems.at[0, k - 1],
                                              recv_sem=recv_sems.at[0, k - 1], device_id=peer, device_id_type=MESH)
            cp.start()
            sends.append(cp)
        for k in range(1, N_DEV):
            peer, pidx = _peer(k)
            pltpu.make_async_remote_copy(src_ref=c_ref.at[0], dst_ref=call_ref.at[pidx], send_sem=send_sems.at[0, k - 1],
                                         recv_sem=recv_sems.at[0, k - 1], device_id=peer, device_id_type=MESH).wait_recv()
        for cp in sends:
            cp.wait_send()
        cv = call_ref[...].reshape(N_DEV, D)
        part = jnp.dot(cv * jax.nn.sigmoid(cv), w_ref[...], preferred_element_type=F32, precision=HI)
        part_ref[...] = part.reshape(N_DEV, 1, cw)
        mod_ref[me] = part_ref[me]
        sends = []
        for k in range(1, N_DEV):
            peer, pidx = _peer(k)
            cp = pltpu.make_async_remote_copy(src_ref=part_ref.at[pidx], dst_ref=mod_ref.at[me], send_sem=send_sems.at[1, k - 1],
                                              recv_sem=recv_sems.at[1, k - 1], device_id=peer, device_id_type=MESH)
            cp.start()
            sends.append(cp)
        for k in range(1, N_DEV):
            peer, pidx = _peer(k)
            pltpu.make_async_remote_copy(src_ref=part_ref.at[pidx], dst_ref=mod_ref.at[pidx], send_sem=send_sems.at[1, k - 1],
                                         recv_sem=recv_sems.at[1, k - 1], device_id=peer, device_id_type=MESH).wait_recv()
        for cp in sends:
            cp.wait_send()
        mod_ref[...] = mod_ref[...] + b_ref[...]

    vm = pl.BlockSpec(memory_space=pltpu.VMEM)
    return pl.pallas_call(
        body, out_shape=(jax.ShapeDtypeStruct((N_DEV, 1, D), F32), jax.ShapeDtypeStruct((N_DEV, 1, cw), F32)),
        in_specs=[vm, vm, vm], out_specs=(vm, vm),
        scratch_shapes=[pltpu.VMEM((N_DEV, 1, cw), F32), pltpu.SemaphoreType.DMA((2, N_DEV - 1)),
                        pltpu.SemaphoreType.DMA((2, N_DEV - 1))],
        name='mod_exchange', compiler_params=_params(None, 48))(c_row, w_ada, b_ada8)


HBM_SPEC = pl.BlockSpec(memory_space=pltpu.HBM)
SEM_SPEC = pl.BlockSpec(memory_space=pltpu.SEMAPHORE)
DATAFLOW = pltpu.SideEffectType.DATAFLOW_SIDE_EFFECTING


def _push_start(src, scatter, after, name):
    land = lax.empty(src.shape if scatter else (N_DEV,) + src.shape, src.dtype)

    def body(src_ref, land_ref, after_ref, send_sem, recv_sem, land_thru, token):
        x, y, c = _me()
        me = 4 * x + 2 * y + c
        for k in range(1, N_DEV):
            peer, pidx = _peer(k)
            pltpu.make_async_remote_copy(src_ref=src_ref.at[pidx] if scatter else src_ref, dst_ref=land_ref.at[me],
                                         send_sem=send_sem, recv_sem=recv_sem, device_id=peer,
                                         device_id_type=MESH).start()
        token[...] = jnp.zeros_like(token)

    own = lax.dynamic_index_in_dim(src, _my_index(), 0, keepdims=False) if scatter else src
    src = pltpu.with_memory_space_constraint(src, pltpu.HBM)
    send_sem, recv_sem, land_thru, token = pl.pallas_call(
        body, name=name,
        out_shape=(pltpu.SemaphoreType.DMA(()), pltpu.SemaphoreType.DMA(()),
                   pltpu.HBM(land.shape, land.dtype), jax.ShapeDtypeStruct((8, 128), F32)),
        in_specs=(HBM_SPEC, HBM_SPEC, pl.BlockSpec(memory_space=pl.ANY)),
        out_specs=(SEM_SPEC, SEM_SPEC, HBM_SPEC, pl.BlockSpec(memory_space=pltpu.VMEM)),
        input_output_aliases={1: 2}, compiler_params=pltpu.CompilerParams(has_side_effects=DATAFLOW),
    )(src, pltpu.with_memory_space_constraint(land, pltpu.HBM), after)
    return send_sem, recv_sem, src, land_thru, token, own


def _push_wait(handle, after, name):
    send_sem, recv_sem, src, land_thru, _, own = handle
    after = tuple(after) if isinstance(after, (tuple, list)) else (after,)

    def body(src_ref, land_ref, send_sem, recv_sem, *rest):
        seven = land_ref.at[pl.ds(0, N_DEV - 1)]
        cp = pltpu.make_async_remote_copy(src_ref=seven, dst_ref=seven, send_sem=send_sem, recv_sem=recv_sem,
                                          device_id=_me(), device_id_type=MESH)
        cp.wait_send()
        cp.wait_recv()

    landed = pl.pallas_call(
        body, name=name, out_shape=pltpu.HBM(land_thru.shape, land_thru.dtype),
        in_specs=(HBM_SPEC, HBM_SPEC, SEM_SPEC, SEM_SPEC) + (pl.BlockSpec(memory_space=pl.ANY),) * len(after),
        out_specs=HBM_SPEC, input_output_aliases={1: 0},
        compiler_params=pltpu.CompilerParams(has_side_effects=DATAFLOW),
    )(src, land_thru, send_sem, recv_sem, *after)
    return lax.dynamic_update_index_in_dim(landed, own, _my_index(), 0)


def _adam(w, g, m, v):
    m2 = B1 * m + (1.0 - B1) * g
    v2 = B2 * v + (1.0 - B2) * jnp.square(g)
    m_hat = m2 / (1.0 - B1 ** STEP)
    v_hat = v2 / (1.0 - B2 ** STEP)
    delta = -LR * (m_hat / (jnp.sqrt(v_hat) + AEPS) + WD * w)
    return delta, m2, v2


def _small_update(gp, wp, mp, vp):
    def body(g_ref, w_ref, m_ref, v_ref, all_ref, go_ref, d_ref, mo_ref, vo_ref, send_sems, recv_sems):
        x, y, c = _me()
        me = 4 * x + 2 * y + c
        all_ref[me] = g_ref[...]
        sends = []
        for k in range(1, N_DEV):
            peer, _ = _peer(k)
            cp = pltpu.make_async_remote_copy(src_ref=g_ref, dst_ref=all_ref.at[me], send_sem=send_sems.at[k - 1],
                                              recv_sem=recv_sems.at[k - 1], device_id=peer, device_id_type=MESH)
            cp.start()
            sends.append(cp)
        for k in range(1, N_DEV):
            peer, pidx = _peer(k)
            pltpu.make_async_remote_copy(src_ref=g_ref, dst_ref=all_ref.at[pidx], send_sem=send_sems.at[k - 1],
                                         recv_sem=recv_sems.at[k - 1], device_id=peer, device_id_type=MESH).wait_recv()
        for cp in sends:
            cp.wait_send()
        g = all_ref[0]
        for d in range(1, N_DEV):
            g = g + all_ref[d]
        delta, m2, v2 = _adam(w_ref[...], g, m_ref[...], v_ref[...])
        go_ref[...] = g
        d_ref[...] = delta
        mo_ref[...] = m2
        vo_ref[...] = v2

    vm = pl.BlockSpec(memory_space=pltpu.VMEM)
    vs = jax.ShapeDtypeStruct(gp.shape, F32)
    return pl.pallas_call(
        body, out_shape=(jax.ShapeDtypeStruct((N_DEV,) + gp.shape, F32), vs, vs, vs, vs), in_specs=[vm] * 4,
        out_specs=(vm,) * 5,
        scratch_shapes=[pltpu.SemaphoreType.DMA((N_DEV - 1,)), pltpu.SemaphoreType.DMA((N_DEV - 1,))],
        name='small_update', compiler_params=_params(None, 48))(gp, wp, mp, vp)


def _small_sum_update(parts, wp, mp, vp):
    def body(p_ref, w_ref, m_ref, v_ref, go_ref, d_ref, mo_ref, vo_ref):
        g = p_ref[0]
        for d in range(1, N_DEV):
            g = g + p_ref[d]
        delta, m2, v2 = _adam(w_ref[...], g, m_ref[...], v_ref[...])
        go_ref[...] = g
        d_ref[...] = delta
        mo_ref[...] = m2
        vo_ref[...] = v2

    vm = pl.BlockSpec(memory_space=pltpu.VMEM)
    vs = jax.ShapeDtypeStruct(wp.shape, F32)
    return pl.pallas_call(body, out_shape=(vs, vs, vs, vs), in_specs=[vm] * 4, out_specs=(vm,) * 4,
                          name='small_sum_update', compiler_params=_params(None, 48))(parts, wp, mp, vp)


def _big_update(parts, w, m, v, name):
    _, R, C = parts.shape
    tr = R if R % 256 else (128 if C >= 2048 else 256)

    def body(p_ref, w_ref, m_ref, v_ref, g_ref, d_ref, mo_ref, vo_ref):
        g = p_ref[0].astype(F32)
        for d in range(1, N_DEV):
            g = g + p_ref[d].astype(F32)
        delta, m2, v2 = _adam(w_ref[...], g, m_ref[...], v_ref[...])
        g_ref[...] = g
        d_ref[...] = delta
        mo_ref[...] = m2
        vo_ref[...] = v2

    blk = pl.BlockSpec((tr, C), lambda i: (i, 0))
    sh = jax.ShapeDtypeStruct((R, C), F32)
    return pl.pallas_call(
        body, out_shape=(sh, sh, sh, sh), grid=(R // tr,),
        in_specs=[pl.BlockSpec((N_DEV, tr, C), lambda i: (0, i, 0)), blk, blk, blk], out_specs=(blk,) * 4,
        name=name, compiler_params=_params(('parallel',), 48))(parts, w, m, v)


def _ada_update(c_all, dmod_cols, w, m, v):
    C = w.shape[1]
    tr = 256

    def body(c_ref, dm_ref, w_ref, m_ref, v_ref, g_ref, d_ref, mo_ref, vo_ref):
        cv = c_ref[...]
        s = cv * jax.nn.sigmoid(cv)
        g = lax.dot_general(s, dm_ref[...], TN_DIMS, preferred_element_type=F32, precision=HI)
        delta, m2, v2 = _adam(w_ref[...], g, m_ref[...], v_ref[...])
        g_ref[...] = g
        d_ref[...] = delta
        mo_ref[...] = m2
        vo_ref[...] = v2

    blk = pl.BlockSpec((tr, C), lambda i: (i, 0))
    sh = jax.ShapeDtypeStruct((D, C), F32)
    return pl.pallas_call(
        body, out_shape=(sh, sh, sh, sh), grid=(D // tr,),
        in_specs=[pl.BlockSpec((N_DEV, tr), lambda i: (0, i)), pl.BlockSpec((N_DEV, C), lambda i: (0, 0)), blk, blk, blk],
        out_specs=(blk,) * 4, name='ada_update', compiler_params=_params(('parallel',), 48))(c_all, dmod_cols, w, m, v)


def _to_sub(t, d):
    if d == 1:
        return t
    return t.reshape(L // d, d, t.shape[-1]).transpose(1, 0, 2).reshape(L, t.shape[-1])


def _from_sub(t, d):
    if d == 1:
        return t
    return t.reshape(d, L // d, t.shape[-1]).transpose(1, 0, 2).reshape(L, t.shape[-1])


def _rows_to_cluster_lanes(t):
    k = t.shape[1]
    return t.reshape(N_CL, CL_S, k).transpose(0, 2, 1)


def _blockdiag_in(t):
    t = t.reshape(N_CL, CL_G, SSM_N, SSM_P).transpose(0, 1, 3, 2)
    eye = jnp.eye(CL_G, dtype=t.dtype)
    t = t[:, :, :, None, :] * eye[None, :, None, :, None]
    return t.reshape(N_CL, CL_U, CL_S)


def _blockdiag_extract(t):
    t = t.reshape(N_CL, CL_G, SSM_P, CL_G, SSM_N)
    t = jnp.stack([t[:, i, :, i, :] for i in range(CL_G)], axis=1)
    return t.transpose(0, 1, 3, 2).reshape(SSM_GN, SSM_P)


def _c_to_rows(t):
    return t.transpose(0, 2, 1).reshape(SSM_GN, SSM_P)


def _rows_to_c(t):
    return t.reshape(SSM_G, SSM_N, SSM_P).transpose(0, 2, 1)


def _tied(v, deps):
    for t in deps:
        v = v + t[0, 0]
    return v


def _local_step(x, pos, mod, tgt, sp, get_w, emit, emit_small, first_deps=()):
    sh1, sc1, gt1, sh2, sc2, gt2 = (mod[i:i + 1] for i in range(NMOD))
    vec = lambda n: sp[n].reshape(1, -1)

    rows = lambda n: sp[n].reshape(SSM_GN, 1)
    a_re, a_im = rows('ssm_a_re'), rows('ssm_a_im')
    ldt = jnp.repeat(sp['ssm_log_dt'].reshape(SSM_G, 1), SSM_N, axis=0)
    b_re, b_im = sp['ssm_b_re'].reshape(SSM_GN, SSM_P), sp['ssm_b_im'].reshape(SSM_GN, SSM_P)
    c_re, c_im = _c_to_rows(sp['ssm_c_re'].reshape(SSM_G, SSM_P, SSM_N)), _c_to_rows(sp['ssm_c_im'].reshape(SSM_G, SSM_P, SSM_N))
    bbr, bbi, pwr, pwi = _ssm_pre(a_re, a_im, ldt, b_re, b_im)
    bm = jnp.concatenate([_blockdiag_in(bbr), _blockdiag_in(bbi)], axis=2)
    cmt = jnp.concatenate([_blockdiag_in(c_re), -_blockdiag_in(c_im)], axis=2)
    bmt, cm = bm.transpose(0, 2, 1), cmt.transpose(0, 2, 1)
    pw = jnp.concatenate([_rows_to_cluster_lanes(pwr), _rows_to_cluster_lanes(pwi)], axis=2)
    dvec = vec('ssm_d')
    bm2, cm2 = _hilo(bm), _hilo(cm)
    bmt_b, cmt_b = bmt.astype(BF16), cmt.astype(BF16)

    h1 = _prenorm_fwd(x, vec('g_pre_mix'), sc1, sh1)
    w_in = get_w('w_in', (h1, bm2, cm2, pw, bmt_b, cmt_b))
    proj = _mm(h1, w_in, mode='nn', name='mm_in', tn=1408, deps=first_deps)
    fr1 =ROPE_THETA ** (-jnp.arange(0, ROT_DIM, 2, dtype=F32) / ROT_DIM)
    lane = jnp.arange(128) % HEAD_DIM
    fr = jnp.where(lane < ROT_DIM, fr1[lane % (ROT_DIM // 2)], 0.0).reshape(1, 128).astype(F32)
    qk = _rope(proj, pos, fr, 1.0, 'rope_fwd', carry=KVW)
    v_b = qk[:, ROPE_W:]
    u_off = (ROPE_W + KVW) // CL_U
    qp = jnp.stack([_to_sub(qk[:, gi * KVW:(gi + 1) * KVW], d) for gi, d in enumerate(DILATIONS)])
    kp = jnp.stack([_to_sub(qk[:, QW:ROPE_W], d) for d in DILATIONS])
    vp = jnp.stack([_to_sub(v_b, d) for d in DILATIONS])
    o_p, lse_p = _attn_fwd(qp, kp, vp)
    o3 = jnp.stack([_from_sub(o_p[gi], d) for gi, d in enumerate(DILATIONS)])
    lse3 = jnp.stack([_from_sub(lse_p[gi], d) for gi, d in enumerate(DILATIONS)])
    att = _combine_fwd(o3, lse3, vec('g_attn_out'))

    ypre, bnd = _ssm_fwd(proj, bm2, cm2, pw, dvec, u_off)
    w_glu = get_w('w_glu', ypre)
    ssm_n = _glu_fwd(ypre, w_glu, vec('b_glu'), vec('g_ssm_out'))

    cat = jnp.concatenate([att, ssm_n], axis=1)
    w_out = get_w('w_out', cat)
    mix = _mm(cat, w_out, mode='nn', name='mm_out', tk=1280)
    x1, h2 = _postmix_fwd(x, mix, vec('g_post_mix'), gt1, vec('g_pre_mlp'), sc2, sh2)
    w_mi = get_w('w_mlp_in', h2)
    a_pre, r_act = _mm(h2, w_mi, mode='nn', name='mm_mlp_in', epilogue='relu2', b_sharded=True)
    w_mo = get_w('w_mlp_out', a_pre)
    y = _mm(r_act, w_mo, mode='nn', name='mm_mlp_out')
    dx2, dy, loss, dgt2, dg_post_mlp = _final_fwd_bwd(x1, y, tgt, vec('g_post_mlp'), gt2)
    dgt2, dg_post_mlp = dgt2[:1], dg_post_mlp[:1]

    da = _mm(dy, w_mo, mode='nt', name='mm_d_act', out_dtype=BF16, epilogue='drelu2', extra=a_pre)
    dep = emit('w_mlp_out', _mm(r_act, dy, mode='tn', name='mm_dw_mlp_out', out_dtype=BF16))
    dh2 = _mm(da, w_mi, mode='nt', name='mm_dh2', tk=1024, b_sharded=True, deps=dep)
    dep = emit('w_mlp_in', _mm(h2, da, mode='tn', name='mm_dw_mlp_in', out_dtype=BF16, out_sharded=True))
    dx1, dmix, dsc2, dsh2, dg_pre_mlp, dgt1, dg_post_mix = _postmix_bwd(
        dx2, dh2, x1, mix, vec('g_post_mix'), gt1, vec('g_pre_mlp'), sc2)
    dsc2, dsh2, dg_pre_mlp, dgt1, dg_post_mix = (t[:1] for t in (dsc2, dsh2, dg_pre_mlp, dgt1, dg_post_mix))
    dcat = _mm(dmix, w_out, mode='nt', name='mm_dcat', tn=1280, deps=dep)
    dep = emit('w_out', _mm(cat, dmix, mode='tn', name='mm_dw_out', out_dtype=BF16, tm=640))
    datt, dsn = dcat[:, :KVW], dcat[:, KVW:]

    dypre, g_w_glu, g_b_glu, g_g_ssm = _glu_bwd(ypre, dsn, w_glu, _tied(vec('b_glu'), dep), vec('g_ssm_out'))
    g_b_glu, g_g_ssm = g_b_glu[:1], g_g_ssm[:1]
    dep = dep + emit('w_glu', g_w_glu.astype(BF16))
    du, dbm, dcm, dA, dD = _ssm_bwd(proj, dypre, bm2, bmt_b, cmt_b, pw, dvec, bnd, u_off)
    dD = dD[:1]
    gbr, gbi = _blockdiag_extract(dbm[:, :, :CL_S]), _blockdiag_extract(dbm[:, :, CL_S:])
    dcmt = dcm.transpose(0, 2, 1)
    g_c_re = _rows_to_c(_blockdiag_extract(dcmt[:, :, :CL_S]))
    g_c_im = _rows_to_c(-_blockdiag_extract(dcmt[:, :, CL_S:]))
    gar = dA[:, 0, :CL_S].reshape(SSM_GN, 1)
    gai = dA[:, 0, CL_S:].reshape(SSM_GN, 1)
    sel = (jnp.arange(SSM_GN)[None, :] // SSM_N == jnp.arange(SSM_G)[:, None]).astype(F32)
    g_a_re, g_a_im, g_b_re, g_b_im, g_ldt = _ssm_post(a_re, a_im, ldt, b_re, b_im, gar, gai, gbr, gbi, sel)

    head_ones = (jnp.arange(KVW)[:, None] // HEAD_DIM == jnp.arange(KVW)[None, :] // HEAD_DIM).astype(F32)
    do3, dlse3, g_g_attn = _combine_bwd(datt, o3, lse3, vec('g_attn_out'), head_ones)
    g_g_attn = g_g_attn[:1]
    dep_small = emit_small({
        'g_post_mix': dg_post_mix, 'ssm_a_re': g_a_re, 'ssm_a_im': g_a_im, 'ssm_log_dt': g_ldt[:, 0],
        'ssm_b_re': g_b_re, 'ssm_b_im': g_b_im, 'ssm_c_re': g_c_re, 'ssm_c_im': g_c_im, 'ssm_d': dD, 'b_glu': g_b_glu,
        'g_attn_out': g_g_attn, 'g_ssm_out': g_g_ssm, 'g_pre_mlp': dg_pre_mlp, 'g_post_mlp': dg_post_mlp})
    do_p = jnp.stack([_to_sub(do3[gi], d) for gi, d in enumerate(DILATIONS)])
    dlse_p = jnp.stack([_to_sub(dlse3[gi], d) for gi, d in enumerate(DILATIONS)])
    dq_p, dk_p, dv_p = _attn_bwd(qp, kp, vp, o_p, lse_p, do_p, dlse_p)
    dq = jnp.concatenate([_from_sub(dq_p[gi], d) for gi, d in enumerate(DILATIONS)], axis=1)
    dk = sum(_from_sub(dk_p[gi], d) for gi, d in enumerate(DILATIONS))
    dv = sum(_from_sub(dv_p[gi], d) for gi, d in enumerate(DILATIONS))
    dqk = _rope(jnp.concatenate([dq, dk], axis=1), pos, _tied(fr, dep_small), -1.0, 'rope_bwd')
    dproj = jnp.concatenate([dqk, dv.astype(BF16), du.astype(BF16)], axis=1)
    dh1 = _mm(dproj, w_in, mode='nt', name='mm_dh1', tk=1408, deps=dep)
    dep = emit('w_in', _mm(h1, dproj, mode='tn', name='mm_dw_in', out_dtype=BF16, tn=1408))
    grad_x, dsc1, dsh1, dg_pre_mix = _prenorm_bwd(dx1, dh1, x, vec('g_pre_mix'), _tied(sc1, dep))
    dsc1, dsh1, dg_pre_mix = dsc1[:1], dsh1[:1], dg_pre_mix[:1]

    dmod = jnp.concatenate([dsh1, dsc1, dgt1, dsh2, dsc2, dgt2], axis=0)
    small = {'b_ada': dmod, 'g_pre_mix': dg_pre_mix}
    return loss[0, 0], grad_x, small


def _pack(d, names):
    flat = jnp.concatenate([jnp.pad(d[n].reshape(-1).astype(F32), (0, SEG[n] - SMALL_SIZES[n])) for n in names])
    return flat.reshape(-1, 128)


def _unpack(packed, names, shapes):
    out, off = {}, 0
    for n in names:
        out[n] = packed[off // 128:(off + SEG[n]) // 128].reshape(-1)[:SMALL_SIZES[n]].reshape(shapes[n])
        off += SEG[n]
    return out


def _shard_major(t, name):
    if name in ('w_in', 'w_out', 'w_mlp_in'):
        k, n = t.shape
        return t.reshape(k, N_DEV, n // N_DEV).transpose(1, 0, 2)
    k, n = t.shape
    return t.reshape(N_DEV, k // N_DEV, n)


def _from_shard_major(t, name):
    if name in ('w_in', 'w_out', 'w_mlp_in'):
        _, k, n = t.shape
        return t.transpose(1, 0, 2).reshape(k, N_DEV * n)
    _, k, n = t.shape
    return t.reshape(N_DEV * k, n)


def kernel(x, c, positions, w_ada, b_ada, g_pre_mix, g_post_mix, w_in, ssm_a_re, ssm_a_im, ssm_log_dt, ssm_b_re, ssm_b_im, ssm_c_re, ssm_c_im, ssm_d, w_glu, b_glu, g_attn_out, g_ssm_out, w_out, g_pre_mlp, g_post_mlp, w_mlp_in, w_mlp_out, loss_target, m_w_ada, m_b_ada, m_g_pre_mix, m_g_post_mix, m_w_in, m_ssm_a_re, m_ssm_a_im, m_ssm_log_dt, m_ssm_b_re, m_ssm_b_im, m_ssm_c_re, m_ssm_c_im, m_ssm_d, m_w_glu, m_b_glu, m_g_attn_out, m_g_ssm_out, m_w_out, m_g_pre_mlp, m_g_post_mlp, m_w_mlp_in, m_w_mlp_out, v_w_ada, v_b_ada, v_g_pre_mix, v_g_post_mix, v_w_in, v_ssm_a_re, v_ssm_a_im, v_ssm_log_dt, v_ssm_b_re, v_ssm_b_im, v_ssm_c_re, v_ssm_c_im, v_ssm_d, v_w_glu, v_b_glu, v_g_attn_out, v_g_ssm_out, v_w_out, v_g_pre_mlp, v_g_post_mlp, v_w_mlp_in, v_w_mlp_out):
    loc = dict(locals())
    W = {n: loc[n] for n in WEIGHTS}
    M = {n: loc['m_' + n] for n in WEIGHTS}
    V = {n: loc['v_' + n] for n in WEIGHTS}
    assert x.shape == (1, L, D) and w_in.shape == (1, D, INW // N_DEV), (x.shape, w_in.shape)

    cw = NMOD * D // N_DEV
    c_all, mod8 = _mod_exchange(c.reshape(1, 1, D), w_ada[0], b_ada.reshape(N_DEV, 1, cw))
    mod = mod8.reshape(NMOD, D)

    gather = {n: _push_start(W[n][0].astype(BF16), False, mod8, 'gather_start_' + n) for n in BIG}
    mod = mod + sum(gather[n][4][0, 0] for n in BIG)

    def get_w(n, after):
        g = _push_wait(gather[n], after, 'gather_wait_' + n)
        return g if n == 'w_mlp_in' else _from_shard_major(g, n)

    scatter = {}

    def emit(n, g):
        src = g if n == 'w_mlp_in' else _shard_major(g, n)
        scatter[n] = _push_start(src, True, src, 'scatter_start_' + n)
        return (scatter[n][4],)

    small_early = []

    def emit_small(d):
        pack = _pack(d, SMALL_EARLY)
        small_early.append(_push_start(pack, False, pack, 'small_start'))
        return (small_early[0][4],)

    sp = {n: W[n][0] for n in SMALL}
    loss, grad_x, small = _local_step(x[0], positions.reshape(L, 1), mod, loss_target[0], sp, get_w, emit, emit_small)
    loss = lax.psum(loss, ('x', 'y', 'c'))
    out_g, out_d, out_m, out_v = {}, {}, {}, {}
    shapes = {n: W[n].shape[1:] for n in SMALL}

    def put(names, packs):
        for dst, packed in zip((out_g, out_d, out_m, out_v), packs):
            dst.update(_unpack(packed, names, shapes))

    rows_all, *packs = _small_update(*[_pack(d, SMALL_LATE) for d in (small, W, M, V)])
    put(SMALL_LATE, packs)

    me = 4 * lax.axis_index('x') + 2 * lax.axis_index('y') + lax.axis_index('c')
    dmod_all = rows_all[:, :NMOD * D // 128].reshape(N_DEV, NMOD * D)
    dmod_cols = lax.dynamic_slice_in_dim(dmod_all, me * cw, cw, axis=1)
    out_g['w_ada'], out_d['w_ada'], out_m['w_ada'], out_v['w_ada'] = _ada_update(
        c_all.reshape(N_DEV, D), dmod_cols, w_ada[0], m_w_ada[0], v_w_ada[0])

    parts = _push_wait(small_early[0], out_v['w_ada'], 'small_wait')
    packs = _small_sum_update(parts, *[_pack(d, SMALL_EARLY) for d in (W, M, V)])
    put(SMALL_EARLY, packs)

    after = packs[3]
    for n in ('w_mlp_out', 'w_mlp_in', 'w_out', 'w_glu', 'w_in'):
        parts = _push_wait(scatter[n], after, 'scatter_wait_' + n)
        out_g[n], out_d[n], out_m[n], out_v[n] = _big_update(parts, W[n][0], M[n][0], V[n][0], 'update_' + n)
        after = out_v[n]

    lead = lambda t: t[None]
    return (loss, grad_x[None], *[lead(out_g[n]) for n in WEIGHTS], *[lead(out_d[n]) for n in WEIGHTS],
            *[lead(out_m[n]) for n in WEIGHTS], *[lead(out_v[n]) for n in WEIGHTS])
```

```python
import functools
import math

import jax
import jax.numpy as jnp
from jax import lax
from jax.experimental import pallas as pl
from jax.experimental.pallas import tpu as pltpu

F32 = jnp.float32
BF16 = jnp.bfloat16
HI = lax.Precision.HIGHEST
MESH = pl.DeviceIdType.MESH

N_DEV = 8
L = 4096
D = 2048
HEAD_DIM = 64
N_GROUPS = 3
DILATIONS = (1, 4, 16)
HEADS = 6
QW = N_GROUPS * HEADS * HEAD_DIM
KVW = HEADS * HEAD_DIM
ROT_DIM = 16
ROPE_THETA = 500000.0
BLK = 128
NBLK = L // BLK
SSMW = D - QW
SSM_P = 16
SSM_G = SSMW // SSM_P
SSM_N = 64
SSM_GN = SSM_G * SSM_N
CL_G = 8
N_CL = SSM_G // CL_G
CL_U = CL_G * SSM_P
CL_S = CL_G * SSM_N
INW = QW + 2 * KVW + SSMW
OUTW = KVW + SSMW
DFF = 4 * D
NMOD = 6
EPS = 1e-6
LR, B1, B2, AEPS, WD, STEP = 0.001, 0.9, 0.999, 1e-08, 0.01, 10

T_SCAN = 512
MB = 2 ** 20

WEIGHTS = ['w_ada', 'b_ada', 'g_pre_mix', 'g_post_mix', 'w_in', 'ssm_a_re', 'ssm_a_im', 'ssm_log_dt',
           'ssm_b_re', 'ssm_b_im', 'ssm_c_re', 'ssm_c_im', 'ssm_d', 'w_glu', 'b_glu', 'g_attn_out',
           'g_ssm_out', 'w_out', 'g_pre_mlp', 'g_post_mlp', 'w_mlp_in', 'w_mlp_out']
BIG = ['w_in', 'w_glu', 'w_out', 'w_mlp_in', 'w_mlp_out']
SMALL = [n for n in WEIGHTS if n not in BIG and n != 'w_ada']
SMALL_SIZES = {'b_ada': NMOD * D, 'g_pre_mix': D, 'g_post_mix': D, 'ssm_a_re': SSM_GN, 'ssm_a_im': SSM_GN,
               'ssm_log_dt': SSM_G, 'ssm_b_re': SSM_GN * SSM_P, 'ssm_b_im': SSM_GN * SSM_P,
               'ssm_c_re': SSM_GN * SSM_P, 'ssm_c_im': SSM_GN * SSM_P, 'ssm_d': SSMW, 'b_glu': SSMW,
               'g_attn_out': KVW, 'g_ssm_out': SSMW, 'g_pre_mlp': D, 'g_post_mlp': D}
SEG = {n: -(-SMALL_SIZES[n] // 1024) * 1024 for n in SMALL}
SMALL_LATE = ['b_ada', 'g_pre_mix']
SMALL_EARLY = [n for n in SMALL if n not in SMALL_LATE]


def _params(sem=None, vmem_mb=None):
    kw = {}
    if sem is not None:
        kw['dimension_semantics'] = sem
    if vmem_mb is not None:
        kw['vmem_limit_bytes'] = vmem_mb * MB
    return pltpu.CompilerParams(**kw)


def _vec(n):
    return pl.BlockSpec((1, n), lambda *_: (0, 0))


def _rms(x):
    return lax.rsqrt(jnp.mean(x * x, axis=-1, keepdims=True) + EPS)


def _rms_bwd(dn, n, r):
    return r * (dn - n * jnp.mean(dn * n, axis=-1, keepdims=True))


def _vec8(n):
    return pl.BlockSpec((8, n), lambda *_: (0, 0))


def _colsum(x):
    return jnp.sum(x.reshape(-1, 8, x.shape[-1]), axis=0)


def _fold8(last, *refs):
    @pl.when(last)
    def _():
        for r in refs:
            r[...] = jnp.broadcast_to(jnp.sum(r[...], axis=0, keepdims=True), r.shape)


def _mm(a, b, *, mode, name, out_dtype=F32, tm=1024, tn=1024, tk=2048, epilogue=None, extra=None,
        b_sharded=False, out_sharded=False, deps=()):
    if mode == 'nn':
        M, K = a.shape
        dims = (((1,), (0,)), ((), ()))
        a_spec = pl.BlockSpec((tm, tk), lambda i, j, k: (i, k))
        if b_sharded:
            _, K2, per = b.shape
            N, q = N_DEV * per, per // tn
            b_spec = pl.BlockSpec((None, tk, tn), lambda i, j, k: (j // q, k, j % q))
        else:
            K2, N = b.shape
            b_spec = pl.BlockSpec((tk, tn), lambda i, j, k: (k, j))
    elif mode == 'nt':
        M, K = a.shape
        dims = (((1,), (1,)), ((), ()))
        a_spec = pl.BlockSpec((tm, tk), lambda i, j, k: (i, k))
        if b_sharded:
            _, N, per = b.shape
            K2, q = N_DEV * per, per // tk
            b_spec = pl.BlockSpec((None, tn, tk), lambda i, j, k: (k // q, j, k % q))
        else:
            N, K2 = b.shape
            b_spec = pl.BlockSpec((tn, tk), lambda i, j, k: (j, k))
    else:
        (K, M), (K2, N) = a.shape, b.shape
        dims = (((0,), (0,)), ((), ()))
        a_spec = pl.BlockSpec((tk, tm), lambda i, j, k: (k, i))
        b_spec = pl.BlockSpec((tk, tn), lambda i, j, k: (k, j))
    assert K == K2 and M % tm == 0 and N % tn == 0 and K % tk == 0, (name, a.shape, b.shape, tm, tn, tk)
    nk = K // tk
    o_spec = pl.BlockSpec((tm, tn), lambda i, j, k: (i, j))
    o_dims = (M, N)
    if out_sharded:
        qo = N // N_DEV // tn
        o_spec = pl.BlockSpec((None, tm, tn), lambda i, j, k: (j // qo, i, j % qo))
        o_dims = (N_DEV, M, N // N_DEV)
    n_out = 2 if epilogue == 'relu2' else 1
    n_extra = 1 if extra is not None else 0
    n_in = 2 + n_extra + len(deps)

    def body(*refs):
        a_ref, b_ref = refs[0], refs[1]
        x_refs = refs[2:2 + n_extra]
        o_refs = refs[n_in:n_in + n_out]
        acc = refs[-1]
        k = pl.program_id(2)

        prod = lax.dot_general(a_ref[...], b_ref[...], dims, preferred_element_type=F32)

        def finish(r):
            if epilogue == 'relu2':
                o_refs[0][...] = r.astype(BF16)
                o_refs[1][...] = jnp.square(jnp.maximum(r, 0.0)).astype(BF16)
            elif epilogue == 'drelu2':
                pre = x_refs[0][...].astype(F32)
                o_refs[0][...] = (r * (2.0 * jnp.maximum(pre, 0.0))).astype(out_dtype)
            else:
                o_refs[0][...] = r.astype(out_dtype)

        if nk == 1:
            finish(prod)
        else:
            @pl.when(k == 0)
            def _():
                acc[...] = prod

            @pl.when((k > 0) & (k < nk - 1))
            def _():
                acc[...] += prod

            @pl.when(k == nk - 1)
            def _():
                finish(acc[...] + prod)

    if epilogue == 'relu2':
        out_shape = (jax.ShapeDtypeStruct((M, N), BF16), jax.ShapeDtypeStruct((M, N), BF16))
        out_specs = (o_spec, o_spec)
    else:
        out_shape = jax.ShapeDtypeStruct(o_dims, out_dtype)
        out_specs = o_spec
    args = (a, b) + ((extra,) if extra is not None else ()) + tuple(deps)
    in_specs = ([a_spec, b_spec] + ([o_spec] if extra is not None else [])
                + [pl.BlockSpec(memory_space=pl.ANY)] * len(deps))
    return pl.pallas_call(
        body, out_shape=out_shape, grid=(M // tm, N // tn, nk), in_specs=in_specs, out_specs=out_specs,
        scratch_shapes=[pltpu.VMEM((tm, tn) if nk > 1 else (8, 128), F32)], name=name,
        compiler_params=_params(('parallel', 'parallel', 'arbitrary'), 56))(*args)


TR = 256


def _rowspec(w=D):
    return pl.BlockSpec((TR, w), lambda i: (i, 0))


def _prenorm_fwd(x, g, sc, sh):
    def body(x_ref, g_ref, sc_ref, sh_ref, h_ref):
        xv = x_ref[...]
        n = xv * _rms(xv)
        h_ref[...] = ((n * g_ref[...]) * (1.0 + sc_ref[...]) + sh_ref[...]).astype(BF16)

    return pl.pallas_call(
        body, out_shape=jax.ShapeDtypeStruct((L, D), BF16), grid=(L // TR,),
        in_specs=[_rowspec(), _vec(D), _vec(D), _vec(D)], out_specs=_rowspec(), name='prenorm_fwd',
        compiler_params=_params(('parallel',), 40))(x, g, sc, sh)


def _postmix_fwd(x, mix, gpm, gt1, gpl, sc2, sh2):
    def body(x_ref, mix_ref, gpm_ref, gt1_ref, gpl_ref, sc2_ref, sh2_ref, x1_ref, h2_ref):
        mix_v = mix_ref[...]
        nm = mix_v * _rms(mix_v)
        x1 = x_ref[...] + gt1_ref[...] * (nm * gpm_ref[...])
        x1_ref[...] = x1
        n2 = x1 * _rms(x1)
        h2_ref[...] = ((n2 * gpl_ref[...]) * (1.0 + sc2_ref[...]) + sh2_ref[...]).astype(BF16)

    return pl.pallas_call(
        body, out_shape=(jax.ShapeDtypeStruct((L, D), F32), jax.ShapeDtypeStruct((L, D), BF16)), grid=(L // TR,),
        in_specs=[_rowspec(), _rowspec()] + [_vec(D)] * 5, out_specs=(_rowspec(), _rowspec()), name='postmix_fwd',
        compiler_params=_params(('parallel',), 40))(x, mix, gpm, gt1, gpl, sc2, sh2)


def _final_fwd_bwd(x1, y, tgt, g, gt2):
    def body(x1_ref, y_ref, t_ref, g_ref, gt2_ref, dx2_ref, dy_ref, loss_ref, dgt2_ref, dg_ref):
        @pl.when(pl.program_id(0) == 0)
        def _():
            loss_ref[...] = jnp.zeros_like(loss_ref)
            dgt2_ref[...] = jnp.zeros_like(dgt2_ref)
            dg_ref[...] = jnp.zeros_like(dg_ref)

        yv = y_ref[...]
        r = _rms(yv)
        n = yv * r
        ng = n * g_ref[...]
        x2 = x1_ref[...] + gt2_ref[...] * ng
        e = x2 - t_ref[...]
        loss_ref[...] += 0.5 * jnp.sum(jnp.mean(e * e, axis=-1, keepdims=True), axis=0, keepdims=True)
        dx2 = e * (1.0 / D)
        dx2_ref[...] = dx2
        dgt2_ref[...] += _colsum(dx2 * ng)
        dng = dx2 * gt2_ref[...]
        dg_ref[...] += _colsum(dng * n)
        dy_ref[...] = _rms_bwd(dng * g_ref[...], n, r).astype(BF16)
        _fold8(pl.program_id(0) == L // TR - 1, dgt2_ref, dg_ref)

    return pl.pallas_call(
        body,
        out_shape=(jax.ShapeDtypeStruct((L, D), F32), jax.ShapeDtypeStruct((L, D), BF16),
                   jax.ShapeDtypeStruct((8, 128), F32), jax.ShapeDtypeStruct((8, D), F32),
                   jax.ShapeDtypeStruct((8, D), F32)),
        grid=(L // TR,), in_specs=[_rowspec(), _rowspec(), _rowspec(), _vec(D), _vec(D)],
        out_specs=(_rowspec(), _rowspec(), _vec8(128), _vec8(D), _vec8(D)), name='final_fwd_bwd',
        compiler_params=_params(('arbitrary',), 40))(x1, y, tgt, g, gt2)


def _postmix_bwd(dx2, dh2, x1, mix, gpm, gt1, gpl, sc2):
    def body(dx2_ref, dh2_ref, x1_ref, mix_ref, gpm_ref, gt1_ref, gpl_ref, sc2_ref,
             dx1_ref, dmix_ref, dsc2_ref, dsh2_ref, dgpl_ref, dgt1_ref, dgpm_ref):
        @pl.when(pl.program_id(0) == 0)
        def _():
            for r_ in (dsc2_ref, dsh2_ref, dgpl_ref, dgt1_ref, dgpm_ref):
                r_[...] = jnp.zeros_like(r_)

        x1v = x1_ref[...]
        r2 = _rms(x1v)
        n2 = x1v * r2
        dh2v = dh2_ref[...]
        dsh2_ref[...] += _colsum(dh2v)
        dsc2_ref[...] += _colsum(dh2v * (n2 * gpl_ref[...]))
        t = dh2v * (1.0 + sc2_ref[...])
        dgpl_ref[...] += _colsum(t * n2)
        dx1 = dx2_ref[...] + _rms_bwd(t * gpl_ref[...], n2, r2)
        dx1_ref[...] = dx1
        mix_v = mix_ref[...]
        rm = _rms(mix_v)
        nm = mix_v * rm
        dgt1_ref[...] += _colsum(dx1 * (nm * gpm_ref[...]))
        u = dx1 * gt1_ref[...]
        dgpm_ref[...] += _colsum(u * nm)
        dmix_ref[...] = _rms_bwd(u * gpm_ref[...], nm, rm).astype(BF16)
        _fold8(pl.program_id(0) == L // TR - 1, dsc2_ref, dsh2_ref, dgpl_ref, dgt1_ref, dgpm_ref)

    vs = jax.ShapeDtypeStruct((8, D), F32)
    return pl.pallas_call(
        body, out_shape=(jax.ShapeDtypeStruct((L, D), F32), jax.ShapeDtypeStruct((L, D), BF16), vs, vs, vs, vs, vs),
        grid=(L // TR,), in_specs=[_rowspec()] * 4 + [_vec(D)] * 4,
        out_specs=(_rowspec(), _rowspec()) + (_vec8(D),) * 5, name='postmix_bwd',
        compiler_params=_params(('arbitrary',), 48))(dx2, dh2, x1, mix, gpm, gt1, gpl, sc2)


def _prenorm_bwd(dx1, dh1, x, g, sc1):
    def body(dx1_ref, dh1_ref, x_ref, g_ref, sc1_ref, dx_ref, dsc1_ref, dsh1_ref, dg_ref):
        @pl.when(pl.program_id(0) == 0)
        def _():
            for r_ in (dsc1_ref, dsh1_ref, dg_ref):
                r_[...] = jnp.zeros_like(r_)

        xv = x_ref[...]
        r = _rms(xv)
        n = xv * r
        dh = dh1_ref[...]
        dsh1_ref[...] += _colsum(dh)
        dsc1_ref[...] += _colsum(dh * (n * g_ref[...]))
        t = dh * (1.0 + sc1_ref[...])
        dg_ref[...] += _colsum(t * n)
        dx_ref[...] = dx1_ref[...] + _rms_bwd(t * g_ref[...], n, r)
        _fold8(pl.program_id(0) == L // TR - 1, dsc1_ref, dsh1_ref, dg_ref)

    vs = jax.ShapeDtypeStruct((8, D), F32)
    return pl.pallas_call(
        body, out_shape=(jax.ShapeDtypeStruct((L, D), F32), vs, vs, vs), grid=(L // TR,),
        in_specs=[_rowspec()] * 3 + [_vec(D)] * 2, out_specs=(_rowspec(),) + (_vec8(D),) * 3, name='prenorm_bwd',
        compiler_params=_params(('arbitrary',), 40))(dx1, dh1, x, g, sc1)


ROPE_W = QW + KVW


def _rope(xin, pos, fr, sign, name, carry=0):
    def body(x_ref, pos_ref, fr_ref, o_ref):
        xv = x_ref[:, 0:ROPE_W]
        ang = pos_ref[...].astype(F32) * fr_ref[...]
        w = lax.broadcasted_iota(jnp.int32, (1, 128), 1) % HEAD_DIM
        cs = jnp.cos(ang)
        sn = jnp.sin(ang) * sign
        s1 = jnp.where(w < ROT_DIM // 2, -sn, 0.0)
        s2 = jnp.where((w >= ROT_DIM // 2) & (w < ROT_DIM), sn, 0.0)
        rep = ROPE_W // 128
        cs, s1, s2 = jnp.tile(cs, (1, rep)), jnp.tile(s1, (1, rep)), jnp.tile(s2, (1, rep))
        hi = pltpu.roll(xv, ROPE_W - ROT_DIM // 2, 1)
        lo = pltpu.roll(xv, ROT_DIM // 2, 1)
        o_ref[:, 0:ROPE_W] = (xv * cs + hi * s1 + lo * s2).astype(BF16)
        if carry:
            o_ref[:, ROPE_W:ROPE_W + carry] = x_ref[:, ROPE_W:ROPE_W + carry].astype(BF16)

    return pl.pallas_call(
        body, out_shape=jax.ShapeDtypeStruct((L, ROPE_W + carry), BF16), grid=(L // TR,),
        in_specs=[_rowspec(ROPE_W + carry), pl.BlockSpec((TR, 1), lambda i: (i, 0)), _vec(128)],
        out_specs=_rowspec(ROPE_W + carry), name=name, compiler_params=_params(('parallel',), 40))(xin, pos, fr)


def _attn_mask(g, b):
    nbs = lax.shift_right_logical(jnp.int32(NBLK), 2 * g)
    first = (b & (nbs - 1)) == 0
    qi = lax.broadcasted_iota(jnp.int32, (BLK, 2 * BLK), 0)
    kj = lax.broadcasted_iota(jnp.int32, (BLK, 2 * BLK), 1)
    dist = qi + BLK - kj
    return (dist >= 0) & (dist <= BLK) & ((kj >= BLK) | jnp.logical_not(first))


def _blk(idx):
    return pl.BlockSpec((None, BLK, KVW), idx)


_CUR = lambda g, b: (g, b, 0)
_PREV = lambda g, b: (g, jnp.maximum(b - 1, 0), 0)
NEG = -1e30
NT_DIMS = (((1,), (1,)), ((), ()))
TN_DIMS = (((0,), (0,)), ((), ()))


def _attn_fwd(qp, kp, vp):
    def body(q_ref, kp_ref, kc_ref, vp_ref, vc_ref, o_ref, lse_ref):
        valid = _attn_mask(pl.program_id(0), pl.program_id(1))
        for h in range(HEADS):
            hs = slice(h * HEAD_DIM, (h + 1) * HEAD_DIM)
            q = q_ref[:, hs]
            kc = jnp.concatenate([kp_ref[:, hs], kc_ref[:, hs]], axis=0)
            vc = jnp.concatenate([vp_ref[:, hs], vc_ref[:, hs]], axis=0)
            s = lax.dot_general(q, kc, NT_DIMS, preferred_element_type=F32) * 0.125
            s = jnp.where(valid, s, NEG)
            m = jnp.max(s, axis=-1, keepdims=True)
            p = jnp.exp(s - m)
            l = jnp.sum(p, axis=-1, keepdims=True)
            o = jnp.dot(p.astype(BF16), vc, preferred_element_type=F32) / l
            o_ref[:, hs] = o
            lse_ref[:, hs] = jnp.broadcast_to(m + jnp.log(l), (BLK, HEAD_DIM))

    sh = jax.ShapeDtypeStruct((N_GROUPS, L, KVW), F32)
    return pl.pallas_call(
        body, out_shape=(sh, sh), grid=(N_GROUPS, NBLK),
        in_specs=[_blk(_CUR), _blk(_PREV), _blk(_CUR), _blk(_PREV), _blk(_CUR)],
        out_specs=(_blk(_CUR), _blk(_CUR)), name='attn_fwd',
        compiler_params=_params(('parallel', 'parallel'), 32))(qp, kp, kp, vp, vp)


def _attn_bwd(qp, kp, vp, o, lse, do, dlse):
    def body(q_ref, kp_ref, kc_ref, vp_ref, vc_ref, o_ref, lse_ref, do_ref, dlse_ref, dq_ref, dk_ref, dv_ref):
        b = pl.program_id(1)

        @pl.when(b == 0)
        def _():
            dk_ref[...] = jnp.zeros_like(dk_ref)
            dv_ref[...] = jnp.zeros_like(dv_ref)

        valid = _attn_mask(pl.program_id(0), b)
        prev0 = pl.multiple_of(jnp.maximum(b - 1, 0) * BLK, BLK)
        cur0 = pl.multiple_of(b * BLK, BLK)
        for h in range(HEADS):
            hs = slice(h * HEAD_DIM, (h + 1) * HEAD_DIM)
            q = q_ref[:, hs]
            kc = jnp.concatenate([kp_ref[:, hs], kc_ref[:, hs]], axis=0)
            vc = jnp.concatenate([vp_ref[:, hs], vc_ref[:, hs]], axis=0)
            s = lax.dot_general(q, kc, NT_DIMS, preferred_element_type=F32) * 0.125
            s = jnp.where(valid, s, NEG)
            p = jnp.exp(s - lse_ref[:, h * HEAD_DIM:h * HEAD_DIM + 1])
            do_h = do_ref[:, hs]
            delta = jnp.sum(do_h * o_ref[:, hs], axis=-1, keepdims=True)
            do_b = do_h.astype(BF16)
            dp = lax.dot_general(do_b, vc, NT_DIMS, preferred_element_type=F32)
            ds = p * (dp - delta + dlse_ref[:, h * HEAD_DIM:h * HEAD_DIM + 1])
            ds_b = (ds * 0.125).astype(BF16)
            dq_ref[:, hs] = jnp.dot(ds_b, kc, preferred_element_type=F32)
            dkc = lax.dot_general(ds_b, q, TN_DIMS, preferred_element_type=F32)
            dvc = lax.dot_general(p.astype(BF16), do_b, TN_DIMS, preferred_element_type=F32)
            dk_ref[pl.ds(prev0, BLK), hs] += dkc[:BLK]
            dv_ref[pl.ds(prev0, BLK), hs] += dvc[:BLK]
            dk_ref[pl.ds(cur0, BLK), hs] += dkc[BLK:]
            dv_ref[pl.ds(cur0, BLK), hs] += dvc[BLK:]

    sh = jax.ShapeDtypeStruct((N_GROUPS, L, KVW), F32)
    whole = pl.BlockSpec((None, L, KVW), lambda g, b: (g, 0, 0))
    return pl.pallas_call(
        body, out_shape=(sh, sh, sh), grid=(N_GROUPS, NBLK),
        in_specs=[_blk(_CUR), _blk(_PREV), _blk(_CUR), _blk(_PREV), _blk(_CUR)] + [_blk(_CUR)] * 4,
        out_specs=(_blk(_CUR), whole, whole), name='attn_bwd',
        compiler_params=_params(('arbitrary', 'arbitrary'), 48))(qp, kp, kp, vp, vp, o, lse, do, dlse)


TC = 512


def _comb_spec():
    return pl.BlockSpec((N_GROUPS, TC, KVW), lambda i: (0, i, 0))


def _combine_weights(lse_ref):
    l0, l1, l2 = lse_ref[0], lse_ref[1], lse_ref[2]
    m = jnp.maximum(jnp.maximum(l0, l1), l2)
    e0, e1, e2 = jnp.exp(l0 - m), jnp.exp(l1 - m), jnp.exp(l2 - m)
    z = e0 + e1 + e2
    return e0 / z, e1 / z, e2 / z


def _combine_fwd(o3, lse3, g):
    def body(o_ref, lse_ref, g_ref, att_ref):
        w0, w1, w2 = _combine_weights(lse_ref)
        a = w0 * o_ref[0] + w1 * o_ref[1] + w2 * o_ref[2]
        att_ref[...] = ((a * _rms(a)) * g_ref[...]).astype(BF16)

    return pl.pallas_call(
        body, out_shape=jax.ShapeDtypeStruct((L, KVW), BF16), grid=(L // TC,),
        in_specs=[_comb_spec(), _comb_spec(), _vec(KVW)], out_specs=pl.BlockSpec((TC, KVW), lambda i: (i, 0)),
        name='combine_fwd', compiler_params=_params(('parallel',), 40))(o3, lse3, g)


def _combine_bwd(datt, o3, lse3, g, head_ones):
    def body(datt_ref, o_ref, lse_ref, g_ref, e_ref, do_ref, dlse_ref, dg_ref):
        @pl.when(pl.program_id(0) == 0)
        def _():
            dg_ref[...] = jnp.zeros_like(dg_ref)

        ws = _combine_weights(lse_ref)
        a = ws[0] * o_ref[0] + ws[1] * o_ref[1] + ws[2] * o_ref[2]
        r = _rms(a)
        n = a * r
        dv = datt_ref[...]
        dg_ref[...] += _colsum(dv * n)
        da = _rms_bwd(dv * g_ref[...], n, r)
        dws = [jnp.dot(da * o_ref[i], e_ref[...], preferred_element_type=F32, precision=HI) for i in range(3)]
        dbar = ws[0] * dws[0] + ws[1] * dws[1] + ws[2] * dws[2]
        for i in range(3):
            do_ref[i] = ws[i] * da
            dlse_ref[i] = ws[i] * (dws[i] - dbar)
        _fold8(pl.program_id(0) == L // TC - 1, dg_ref)

    sh = jax.ShapeDtypeStruct((N_GROUPS, L, KVW), F32)
    return pl.pallas_call(
        body, out_shape=(sh, sh, jax.ShapeDtypeStruct((8, KVW), F32)), grid=(L // TC,),
        in_specs=[pl.BlockSpec((TC, KVW), lambda i: (i, 0)), _comb_spec(), _comb_spec(), _vec(KVW),
                  pl.BlockSpec((KVW, KVW), lambda i: (0, 0))],
        out_specs=(_comb_spec(), _comb_spec(), _vec8(KVW)), name='combine_bwd',
        compiler_params=_params(('arbitrary',), 48))(datt, o3, lse3, g, head_ones)


def _ssm_disc(ar, ai, ldt):
    dt = jnp.exp(ldt)
    zr, zi = ar * dt, ai * dt
    ez = jnp.exp(zr)
    A_r, A_i = ez * jnp.cos(zi), ez * jnp.sin(zi)
    den = ar * ar + ai * ai
    xr, xi = A_r - 1.0, A_i
    cr = (xr * ar + xi * ai) / den
    ci = (xi * ar - xr * ai) / den
    return dt, zr, zi, A_r, A_i, den, cr, ci


def _ssm_pre(ar, ai, ldt, br, bi):
    def body(ar_ref, ai_ref, ldt_ref, br_ref, bi_ref, bbr_ref, bbi_ref, pwr_ref, pwi_ref):
        _, zr, zi, _, _, _, cr, ci = _ssm_disc(ar_ref[...], ai_ref[...], ldt_ref[...])
        bbr_ref[...] = cr * br_ref[...] - ci * bi_ref[...]
        bbi_ref[...] = cr * bi_ref[...] + ci * br_ref[...]
        k = (lax.broadcasted_iota(jnp.int32, (1, 8), 1) + 1).astype(F32)
        ek = jnp.exp(zr * k)
        pwr_ref[...] = ek * jnp.cos(zi * k)
        pwi_ref[...] = ek * jnp.sin(zi * k)

    s16 = jax.ShapeDtypeStruct((SSM_GN, SSM_P), F32)
    s8 = jax.ShapeDtypeStruct((SSM_GN, 8), F32)
    return pl.pallas_call(body, out_shape=(s16, s16, s8, s8), name='ssm_pre',
                          compiler_params=_params(None, 40))(ar, ai, ldt, br, bi)


def _ssm_post(ar, ai, ldt, br, bi, gar, gai, gbr, gbi, sel):
    def body(ar_ref, ai_ref, ldt_ref, br_ref, bi_ref, gar_ref, gai_ref, gbr_ref, gbi_ref, sel_ref,
             dar_ref, dai_ref, dbr_ref, dbi_ref, dldt_ref):
        a_r, a_i = ar_ref[...], ai_ref[...]
        dt, _, _, A_r, A_i, den, cr, ci = _ssm_disc(a_r, a_i, ldt_ref[...])
        b_r, b_i, g_br, g_bi = br_ref[...], bi_ref[...], gbr_ref[...], gbi_ref[...]
        gcr = jnp.sum(g_br * b_r + g_bi * b_i, axis=-1, keepdims=True)
        gci = jnp.sum(g_bi * b_r - g_br * b_i, axis=-1, keepdims=True)
        dbr_ref[...] = g_br * cr + g_bi * ci
        dbi_ref[...] = g_bi * cr - g_br * ci
        g_ar = gar_ref[...] + (gcr * a_r - gci * a_i) / den
        g_ai = gai_ref[...] + (gcr * a_i + gci * a_r) / den
        qr = (cr * a_r + ci * a_i) / den
        qi = (ci * a_r - cr * a_i) / den
        glr = -(gcr * qr + gci * qi)
        gli = -(gci * qr - gcr * qi)
        gzr = g_ar * A_r + g_ai * A_i
        gzi = g_ai * A_r - g_ar * A_i
        dar_ref[...] = glr + gzr * dt
        dai_ref[...] = gli + gzi * dt
        gdt = (gzr * a_r + gzi * a_i) * dt
        dldt_ref[...] = jnp.dot(sel_ref[...], jnp.broadcast_to(gdt, (SSM_GN, 128)),
                                preferred_element_type=F32, precision=HI)

    s1 = jax.ShapeDtypeStruct((SSM_GN, 1), F32)
    s16 = jax.ShapeDtypeStruct((SSM_GN, SSM_P), F32)
    return pl.pallas_call(body, out_shape=(s1, s1, s16, s16, jax.ShapeDtypeStruct((SSM_G, 128), F32)),
                          name='ssm_post', compiler_params=_params(None, 48))(
                              ar, ai, ldt, br, bi, gar, gai, gbr, gbi, sel)


SCAN_CH = 8


def _scan_fwd_tiles(s_ref, pw, carry):
    pwr, pwi = pw[:, :CL_S], pw[:, CL_S:]
    row = lax.broadcasted_iota(jnp.int32, (8, CL_S), 0)
    steps = [(k, jnp.where(row >= k, pwr[k - 1:k], 0.0), jnp.where(row >= k, pwi[k - 1:k], 0.0)) for k in (1, 2, 4)]
    rows = 8 * SCAN_CH

    def chunk(i, c):
        cr, ci = c
        r0 = pl.multiple_of(i * rows, rows)
        xr = s_ref[pl.ds(r0, rows), 0:CL_S].reshape(SCAN_CH, 8, CL_S)
        xi = s_ref[pl.ds(r0, rows), CL_S:2 * CL_S].reshape(SCAN_CH, 8, CL_S)
        for k, pr, pi in steps:
            sr, si = pltpu.roll(xr, k, 1), pltpu.roll(xi, k, 1)
            xr, xi = xr + pr * sr - pi * si, xi + pr * si + pi * sr
        for j in range(SCAN_CH):
            tr = xr[j] + pwr * cr - pwi * ci
            ti = xi[j] + pwr * ci + pwi * cr
            s_ref[pl.ds(r0 + 8 * j, 8), 0:CL_S] = tr
            s_ref[pl.ds(r0 + 8 * j, 8), CL_S:2 * CL_S] = ti
            cr, ci = tr[7:8], ti[7:8]
        return cr, ci

    return lax.fori_loop(0, T_SCAN // rows, chunk, (carry[:, :CL_S], carry[:, CL_S:]))


def _scan_bwd_tiles(l_ref, pw, carry):
    pwr, pwi = pw[:, :CL_S], pw[:, CL_S:]
    rpr = jnp.concatenate([pwr[7 - r:8 - r] for r in range(8)], axis=0)
    rpi = jnp.concatenate([pwi[7 - r:8 - r] for r in range(8)], axis=0)
    row = lax.broadcasted_iota(jnp.int32, (8, CL_S), 0)
    steps = [(k, jnp.where(row < 8 - k, pwr[k - 1:k], 0.0), jnp.where(row < 8 - k, pwi[k - 1:k], 0.0))
             for k in (1, 2, 4)]
    rows = 8 * SCAN_CH
    nc = T_SCAN // rows

    def chunk(i, c):
        cr, ci = c
        r0 = pl.multiple_of((nc - 1 - i) * rows, rows)
        xr = l_ref[pl.ds(r0, rows), 0:CL_S].reshape(SCAN_CH, 8, CL_S)
        xi = l_ref[pl.ds(r0, rows), CL_S:2 * CL_S].reshape(SCAN_CH, 8, CL_S)
        for k, pr, pi in steps:
            sr, si = pltpu.roll(xr, 8 - k, 1), pltpu.roll(xi, 8 - k, 1)
            xr, xi = xr + pr * sr + pi * si, xi + pr * si - pi * sr
        for j in reversed(range(SCAN_CH)):
            tr = xr[j] + rpr * cr + rpi * ci
            ti = xi[j] + rpr * ci - rpi * cr
            l_ref[pl.ds(r0 + 8 * j, 8), 0:CL_S] = tr
            l_ref[pl.ds(r0 + 8 * j, 8), CL_S:2 * CL_S] = ti
            cr, ci = tr[0:1], ti[0:1]
        return cr, ci

    return lax.fori_loop(0, nc, chunk, (carry[:, :CL_S], carry[:, CL_S:]))


NT_SCAN = L // T_SCAN


def _hilo(t):
    hi = t.astype(BF16)
    return jnp.stack([hi, (t - hi.astype(F32)).astype(BF16)], axis=1)


def _dot3(a, b_ref):
    ah = a.astype(BF16)
    al = (a - ah.astype(F32)).astype(BF16)
    bh, bl = b_ref[0], b_ref[1]
    return (jnp.dot(ah, bh, preferred_element_type=F32) + jnp.dot(al, bh, preferred_element_type=F32)
            + jnp.dot(ah, bl, preferred_element_type=F32))


def _hl_spec(r, c):
    return pl.BlockSpec((None, 2, r, c), lambda c_, t: (c_, 0, 0, 0))


def _ssm_fwd(u, bm2, cm2, pw, dvec, u_off=0):
    def body(u_ref, bm_ref, cm_ref, pw_ref, d_ref, y_ref, bnd_ref, s_ref, carry_ref):
        @pl.when(pl.program_id(1) == 0)
        def _():
            carry_ref[...] = jnp.zeros_like(carry_ref)

        bnd_ref[...] = carry_ref[...]
        uv = u_ref[...]
        s_ref[...] = _dot3(uv, bm_ref)
        cr, ci = _scan_fwd_tiles(s_ref, pw_ref[...], carry_ref[...])
        carry_ref[...] = jnp.concatenate([cr, ci], axis=1)
        y_ref[...] = _dot3(s_ref[...], cm_ref) + d_ref[...] * uv

    return pl.pallas_call(
        body,
        out_shape=(jax.ShapeDtypeStruct((L, SSMW), F32), jax.ShapeDtypeStruct((N_CL, NT_SCAN, 1, 2 * CL_S), F32),
                   jax.ShapeDtypeStruct((L, N_CL * 2 * CL_S), F32)),
        grid=(N_CL, NT_SCAN),
        in_specs=[pl.BlockSpec((T_SCAN, CL_U), lambda c, t: (t, c + u_off)),
                  _hl_spec(CL_U, 2 * CL_S), _hl_spec(2 * CL_S, CL_U),
                  pl.BlockSpec((None, 8, 2 * CL_S), lambda c, t: (c, 0, 0)),
                  pl.BlockSpec((1, CL_U), lambda c, t: (0, c))],
        out_specs=(pl.BlockSpec((T_SCAN, CL_U), lambda c, t: (t, c)),
                   pl.BlockSpec((None, None, 1, 2 * CL_S), lambda c, t: (c, t, 0, 0)),
                   pl.BlockSpec((T_SCAN, 2 * CL_S), lambda c, t: (t, c))),
        scratch_shapes=[pltpu.VMEM((1, 2 * CL_S), F32)],
        name='ssm_fwd', compiler_params=_params(('arbitrary', 'arbitrary'), 40))(u, bm2, cm2, pw, dvec)


def _ssm_bwd(u, dy, states, bmt, cmt, pw, dvec, bnd, u_off=0):
    rev = lambda t: NT_SCAN - 1 - t

    def body(u_ref, dy_ref, s_ref, bmt_ref, cmt_ref, pw_ref, d_ref, bnd_ref,
             du_ref, dbm_ref, dcm_ref, da_ref, dd_ref, l_ref, carry_ref):
        @pl.when(pl.program_id(1) == 0)
        def _():
            carry_ref[...] = jnp.zeros_like(carry_ref)
            dbm_ref[...] = jnp.zeros_like(dbm_ref)
            dcm_ref[...] = jnp.zeros_like(dcm_ref)
            da_ref[...] = jnp.zeros_like(da_ref)
            dd_ref[...] = jnp.zeros_like(dd_ref)

        uv, dyv, pw = u_ref[...], dy_ref[...], pw_ref[...]
        dy_b = dyv.astype(BF16)
        entry = bnd_ref[...]
        l_ref[...] = jnp.dot(dy_b, cmt_ref[...], preferred_element_type=F32)
        cr, ci = _scan_bwd_tiles(l_ref, pw, carry_ref[...])
        carry_ref[...] = jnp.concatenate([cr, ci], axis=1)
        sv, lv = s_ref[...], l_ref[...]
        lv_b = lv.astype(BF16)
        du_ref[...] = dyv * d_ref[...] + jnp.dot(lv_b, bmt_ref[...], preferred_element_type=F32)
        dbm_ref[...] += lax.dot_general(uv.astype(BF16), lv_b, TN_DIMS, preferred_element_type=F32)
        dcm_ref[...] += lax.dot_general(sv.astype(BF16), dy_b, TN_DIMS, preferred_element_type=F32)
        dd_ref[...] += _colsum(dyv * uv)
        row = lax.broadcasted_iota(jnp.int32, (T_SCAN, 2 * CL_S), 0)
        sp = jnp.where(row == 0, entry, pltpu.roll(sv, 1, 0))
        spr, spi = sp[:, :CL_S], sp[:, CL_S:]
        lr, li = lv[:, :CL_S], lv[:, CL_S:]
        da_ref[:, 0:CL_S] += _colsum(lr * spr + li * spi)
        da_ref[:, CL_S:2 * CL_S] += _colsum(li * spr - lr * spi)
        _fold8(pl.program_id(1) == NT_SCAN - 1, da_ref, dd_ref)

    return pl.pallas_call(
        body,
        out_shape=(jax.ShapeDtypeStruct((L, SSMW), F32), jax.ShapeDtypeStruct((N_CL, CL_U, 2 * CL_S), F32),
                   jax.ShapeDtypeStruct((N_CL, 2 * CL_S, CL_U), F32), jax.ShapeDtypeStruct((N_CL, 8, 2 * CL_S), F32),
                   jax.ShapeDtypeStruct((8, SSMW), F32)),
        grid=(N_CL, NT_SCAN),
        in_specs=[pl.BlockSpec((T_SCAN, CL_U), lambda c, t: (rev(t), c + u_off)),
                  pl.BlockSpec((T_SCAN, CL_U), lambda c, t: (rev(t), c)),
                  pl.BlockSpec((T_SCAN, 2 * CL_S), lambda c, t: (rev(t), c)),
                  pl.BlockSpec((None, 2 * CL_S, CL_U), lambda c, t: (c, 0, 0)),
                  pl.BlockSpec((None, CL_U, 2 * CL_S), lambda c, t: (c, 0, 0)),
                  pl.BlockSpec((None, 8, 2 * CL_S), lambda c, t: (c, 0, 0)),
                  pl.BlockSpec((1, CL_U), lambda c, t: (0, c)),
                  pl.BlockSpec((None, None, 1, 2 * CL_S), lambda c, t: (c, rev(t), 0, 0))],
        out_specs=(pl.BlockSpec((T_SCAN, CL_U), lambda c, t: (rev(t), c)),
                   pl.BlockSpec((None, CL_U, 2 * CL_S), lambda c, t: (c, 0, 0)),
                   pl.BlockSpec((None, 2 * CL_S, CL_U), lambda c, t: (c, 0, 0)),
                   pl.BlockSpec((None, 8, 2 * CL_S), lambda c, t: (c, 0, 0)),
                   pl.BlockSpec((8, CL_U), lambda c, t: (0, c))),
        scratch_shapes=[pltpu.VMEM((T_SCAN, 2 * CL_S), F32), pltpu.VMEM((1, 2 * CL_S), F32)],
        name='ssm_bwd', compiler_params=_params(('arbitrary', 'arbitrary'), 48))(u, dy, states, bmt, cmt, pw, dvec, bnd)


GELU_C = math.sqrt(2.0 / math.pi)
GELU_K = 0.044715


def _gelu_parts(x):
    t = jnp.tanh(GELU_C * (x + GELU_K * (x * x * x)))
    return x * (0.5 * (1.0 + t)), t


def _glu_fwd(ypre, wglu, bglu, gs):
    def body(y_ref, w_ref, b_ref, g_ref, o_ref):
        yg, _ = _gelu_parts(y_ref[...])
        z = jnp.dot(yg.astype(BF16), w_ref[...], preferred_element_type=F32) + b_ref[...]
        s = yg * jax.nn.sigmoid(z)
        o_ref[...] = ((s * _rms(s)) * g_ref[...]).astype(BF16)

    return pl.pallas_call(
        body, out_shape=jax.ShapeDtypeStruct((L, SSMW), BF16), grid=(L // TR,),
        in_specs=[_rowspec(SSMW), pl.BlockSpec((SSMW, SSMW), lambda i: (0, 0)), _vec(SSMW), _vec(SSMW)],
        out_specs=_rowspec(SSMW), name='glu_fwd', compiler_params=_params(('parallel',), 32))(ypre, wglu, bglu, gs)


def _glu_bwd(ypre, dsn, wglu, bglu, gs):
    def body(y_ref, d_ref, w_ref, b_ref, g_ref, dy_ref, dw_ref, db_ref, dg_ref):
        @pl.when(pl.program_id(0) == 0)
        def _():
            dw_ref[...] = jnp.zeros_like(dw_ref)
            db_ref[...] = jnp.zeros_like(db_ref)
            dg_ref[...] = jnp.zeros_like(dg_ref)

        xv = y_ref[...]
        yg, t = _gelu_parts(xv)
        yg_b = yg.astype(BF16)
        z = jnp.dot(yg_b, w_ref[...], preferred_element_type=F32) + b_ref[...]
        sg = jax.nn.sigmoid(z)
        s = yg * sg
        r = _rms(s)
        n = s * r
        dv = d_ref[...]
        dg_ref[...] += _colsum(dv * n)
        ds = _rms_bwd(dv * g_ref[...], n, r)
        dz = (ds * yg) * (sg * (1.0 - sg))
        dz_b = dz.astype(BF16)
        db_ref[...] += _colsum(dz)
        dw_ref[...] += lax.dot_general(yg_b, dz_b, TN_DIMS, preferred_element_type=F32)
        dyg = ds * sg + lax.dot_general(dz_b, w_ref[...], NT_DIMS, preferred_element_type=F32)
        dgelu = 0.5 * (1.0 + t) + (0.5 * xv) * (1.0 - t * t) * (GELU_C * (1.0 + 3.0 * GELU_K * (xv * xv)))
        dy_ref[...] = dyg * dgelu
        _fold8(pl.program_id(0) == L // TR - 1, db_ref, dg_ref)

    vs = jax.ShapeDtypeStruct((8, SSMW), F32)
    return pl.pallas_call(
        body, out_shape=(jax.ShapeDtypeStruct((L, SSMW), F32), jax.ShapeDtypeStruct((SSMW, SSMW), F32), vs, vs),
        grid=(L // TR,),
        in_specs=[_rowspec(SSMW), _rowspec(SSMW), pl.BlockSpec((SSMW, SSMW), lambda i: (0, 0)), _vec(SSMW), _vec(SSMW)],
        out_specs=(_rowspec(SSMW), pl.BlockSpec((SSMW, SSMW), lambda i: (0, 0)), _vec8(SSMW), _vec8(SSMW)),
        name='glu_bwd', compiler_params=_params(('arbitrary',), 40))(ypre, dsn, wglu, bglu, gs)


def _me():
    return lax.axis_index('x'), lax.axis_index('y'), lax.axis_index('c')


def _my_index():
    return 4 * lax.axis_index('x') + 2 * lax.axis_index('y') + lax.axis_index('c')


def _peer(k):
    x, y, c = _me()
    px = 1 - x if k & 4 else x
    py = 1 - y if k & 2 else y
    pc = 1 - c if k & 1 else c
    return (px, py, pc), 4 * px + 2 * py + pc


def _mod_exchange(c_row, w_ada, b_ada8, deps=()):
    cw = NMOD * D // N_DEV

    def body(c_ref, w_ref, b_ref, *rest):
        call_ref, mod_ref, part_ref, send_sems, recv_sems = rest[len(deps):]
        x, y, c = _me()
        me = 4 * x + 2 * y + c
        call_ref[me] = c_ref[0]
        sends = []
        for k in range(1, N_DEV):
            peer, _ = _peer(k)
            cp = pltpu.make_async_remote_copy(src_ref=c_ref.at[0], dst_ref=call_ref.at[me], send_sem=send_sems.at[0, k - 1],
                                              recv_sem=recv_sems.at[0, k - 1], device_id=peer, device_id_type=MESH)
            cp.start()
            sends.append(cp)
        for k in range(1, N_DEV):
            peer, pidx = _peer(k)
            pltpu.make_async_remote_copy(src_ref=c_ref.at[0], dst_ref=call_ref.at[pidx], send_sem=send_sems.at[0, k - 1],
                                         recv_sem=recv_sems.at[0, k - 1], device_id=peer, device_id_type=MESH).wait_recv()
        for cp in sends:
            cp.wait_send()
        cv = call_ref[...].reshape(N_DEV, D)
        part = jnp.dot(cv * jax.nn.sigmoid(cv), w_ref[...], preferred_element_type=F32, precision=HI)
        part_ref[...] = part.reshape(N_DEV, 1, cw)
        mod_ref[me] = part_ref[me]
        sends = []
        for k in range(1, N_DEV):
            peer, pidx = _peer(k)
            cp = pltpu.make_async_remote_copy(src_ref=part_ref.at[pidx], dst_ref=mod_ref.at[me], send_sem=send_sems.at[1, k - 1],
                                              recv_sem=recv_sems.at[1, k - 1], device_id=peer, device_id_type=MESH)
            cp.start()
            sends.append(cp)
        for k in range(1, N_DEV):
            peer, pidx = _peer(k)
            pltpu.make_async_remote_copy(src_ref=part_ref.at[pidx], dst_ref=mod_ref.at[pidx], send_sem=send_sems.at[1, k - 1],
                                         recv_sem=recv_sems.at[1, k - 1], device_id=peer, device_id_type=MESH).wait_recv()
        for cp in sends:
            cp.wait_send()
        mod_ref[...] = mod_ref[...] + b_ref[...]

    vm = pl.BlockSpec(memory_space=pltpu.VMEM)
    return pl.pallas_call(
        body, out_shape=(jax.ShapeDtypeStruct((N_DEV, 1, D), F32), jax.ShapeDtypeStruct((N_DEV, 1, cw), F32)),
        in_specs=[vm, vm, vm] + [pl.BlockSpec(memory_space=pl.ANY)] * len(deps), out_specs=(vm, vm),
        scratch_shapes=[pltpu.VMEM((N_DEV, 1, cw), F32), pltpu.SemaphoreType.DMA((2, N_DEV - 1)),
                        pltpu.SemaphoreType.DMA((2, N_DEV - 1))],
        name='mod_exchange', compiler_params=_params(None, 48))(c_row, w_ada, b_ada8, *deps)


HBM_SPEC = pl.BlockSpec(memory_space=pltpu.HBM)
SEM_SPEC = pl.BlockSpec(memory_space=pltpu.SEMAPHORE)
DATAFLOW = pltpu.SideEffectType.DATAFLOW_SIDE_EFFECTING


def _push_start(src, scatter, after, name):
    land = lax.empty(src.shape if scatter else (N_DEV,) + src.shape, src.dtype)

    def body(src_ref, land_ref, after_ref, send_sem, recv_sem, land_thru, token):
        x, y, c = _me()
        me = 4 * x + 2 * y + c
        for k in range(1, N_DEV):
            peer, pidx = _peer(k)
            pltpu.make_async_remote_copy(src_ref=src_ref.at[pidx] if scatter else src_ref, dst_ref=land_ref.at[me],
                                         send_sem=send_sem, recv_sem=recv_sem, device_id=peer,
                                         device_id_type=MESH).start()
        token[...] = jnp.zeros_like(token)

    own = lax.dynamic_index_in_dim(src, _my_index(), 0, keepdims=False) if scatter else src
    src = pltpu.with_memory_space_constraint(src, pltpu.HBM)
    send_sem, recv_sem, land_thru, token = pl.pallas_call(
        body, name=name,
        out_shape=(pltpu.SemaphoreType.DMA(()), pltpu.SemaphoreType.DMA(()),
                   pltpu.HBM(land.shape, land.dtype), jax.ShapeDtypeStruct((8, 128), F32)),
        in_specs=(HBM_SPEC, HBM_SPEC, pl.BlockSpec(memory_space=pl.ANY)),
        out_specs=(SEM_SPEC, SEM_SPEC, HBM_SPEC, pl.BlockSpec(memory_space=pltpu.VMEM)),
        input_output_aliases={1: 2}, compiler_params=pltpu.CompilerParams(has_side_effects=DATAFLOW),
    )(src, pltpu.with_memory_space_constraint(land, pltpu.HBM), after)
    return send_sem, recv_sem, src, land_thru, token, own


def _push_wait(handle, after, name):
    send_sem, recv_sem, src, land_thru, _, own = handle
    after = tuple(after) if isinstance(after, (tuple, list)) else (after,)

    def body(src_ref, land_ref, send_sem, recv_sem, *rest):
        seven = land_ref.at[pl.ds(0, N_DEV - 1)]
        cp = pltpu.make_async_remote_copy(src_ref=seven, dst_ref=seven, send_sem=send_sem, recv_sem=recv_sem,
                                          device_id=_me(), device_id_type=MESH)
        cp.wait_send()
        cp.wait_recv()

    landed = pl.pallas_call(
        body, name=name, out_shape=pltpu.HBM(land_thru.shape, land_thru.dtype),
        in_specs=(HBM_SPEC, HBM_SPEC, SEM_SPEC, SEM_SPEC) + (pl.BlockSpec(memory_space=pl.ANY),) * len(after),
        out_specs=HBM_SPEC, input_output_aliases={1: 0},
        compiler_params=pltpu.CompilerParams(has_side_effects=DATAFLOW),
    )(src, land_thru, send_sem, recv_sem, *after)
    return lax.dynamic_update_index_in_dim(landed, own, _my_index(), 0)


def _adam(w, g, m, v):
    m2 = B1 * m + (1.0 - B1) * g
    v2 = B2 * v + (1.0 - B2) * jnp.square(g)
    m_hat = m2 / (1.0 - B1 ** STEP)
    v_hat = v2 / (1.0 - B2 ** STEP)
    delta = -LR * (m_hat / (jnp.sqrt(v_hat) + AEPS) + WD * w)
    return delta, m2, v2


def _small_update(gp, wp, mp, vp):
    def body(g_ref, w_ref, m_ref, v_ref, all_ref, go_ref, d_ref, mo_ref, vo_ref, send_sems, recv_sems):
        x, y, c = _me()
        me = 4 * x + 2 * y + c
        all_ref[me] = g_ref[...]
        sends = []
        for k in range(1, N_DEV):
            peer, _ = _peer(k)
            cp = pltpu.make_async_remote_copy(src_ref=g_ref, dst_ref=all_ref.at[me], send_sem=send_sems.at[k - 1],
                                              recv_sem=recv_sems.at[k - 1], device_id=peer, device_id_type=MESH)
            cp.start()
            sends.append(cp)
        for k in range(1, N_DEV):
            peer, pidx = _peer(k)
            pltpu.make_async_remote_copy(src_ref=g_ref, dst_ref=all_ref.at[pidx], send_sem=send_sems.at[k - 1],
                                         recv_sem=recv_sems.at[k - 1], device_id=peer, device_id_type=MESH).wait_recv()
        for cp in sends:
            cp.wait_send()
        g = all_ref[0]
        for d in range(1, N_DEV):
            g = g + all_ref[d]
        delta, m2, v2 = _adam(w_ref[...], g, m_ref[...], v_ref[...])
        go_ref[...] = g
        d_ref[...] = delta
        mo_ref[...] = m2
        vo_ref[...] = v2

    vm = pl.BlockSpec(memory_space=pltpu.VMEM)
    vs = jax.ShapeDtypeStruct(gp.shape, F32)
    return pl.pallas_call(
        body, out_shape=(jax.ShapeDtypeStruct((N_DEV,) + gp.shape, F32), vs, vs, vs, vs), in_specs=[vm] * 4,
        out_specs=(vm,) * 5,
        scratch_shapes=[pltpu.SemaphoreType.DMA((N_DEV - 1,)), pltpu.SemaphoreType.DMA((N_DEV - 1,))],
        name='small_update', compiler_params=_params(None, 48))(gp, wp, mp, vp)


def _small_sum_update(parts, wp, mp, vp):
    def body(p_ref, w_ref, m_ref, v_ref, go_ref, d_ref, mo_ref, vo_ref):
        g = p_ref[0]
        for d in range(1, N_DEV):
            g = g + p_ref[d]
        delta, m2, v2 = _adam(w_ref[...], g, m_ref[...], v_ref[...])
        go_ref[...] = g
        d_ref[...] = delta
        mo_ref[...] = m2
        vo_ref[...] = v2

    vm = pl.BlockSpec(memory_space=pltpu.VMEM)
    vs = jax.ShapeDtypeStruct(wp.shape, F32)
    return pl.pallas_call(body, out_shape=(vs, vs, vs, vs), in_specs=[vm] * 4, out_specs=(vm,) * 4,
                          name='small_sum_update', compiler_params=_params(None, 48))(parts, wp, mp, vp)


def _big_update(parts, w, m, v, name):
    _, R, C = parts.shape
    tr = R if R % 256 else (128 if C >= 2048 else 256)

    def body(p_ref, w_ref, m_ref, v_ref, g_ref, d_ref, mo_ref, vo_ref):
        g = p_ref[0].astype(F32)
        for d in range(1, N_DEV):
            g = g + p_ref[d].astype(F32)
        delta, m2, v2 = _adam(w_ref[...], g, m_ref[...], v_ref[...])
        g_ref[...] = g
        d_ref[...] = delta
        mo_ref[...] = m2
        vo_ref[...] = v2

    blk = pl.BlockSpec((tr, C), lambda i: (i, 0))
    sh = jax.ShapeDtypeStruct((R, C), F32)
    return pl.pallas_call(
        body, out_shape=(sh, sh, sh, sh), grid=(R // tr,),
        in_specs=[pl.BlockSpec((N_DEV, tr, C), lambda i: (0, i, 0)), blk, blk, blk], out_specs=(blk,) * 4,
        name=name, compiler_params=_params(('parallel',), 48))(parts, w, m, v)


def _ada_update(c_all, dmod_cols, w, m, v):
    C = w.shape[1]
    tr = 256

    def body(c_ref, dm_ref, w_ref, m_ref, v_ref, g_ref, d_ref, mo_ref, vo_ref):
        cv = c_ref[...]
        s = cv * jax.nn.sigmoid(cv)
        g = lax.dot_general(s, dm_ref[...], TN_DIMS, preferred_element_type=F32, precision=HI)
        delta, m2, v2 = _adam(w_ref[...], g, m_ref[...], v_ref[...])
        g_ref[...] = g
        d_ref[...] = delta
        mo_ref[...] = m2
        vo_ref[...] = v2

    blk = pl.BlockSpec((tr, C), lambda i: (i, 0))
    sh = jax.ShapeDtypeStruct((D, C), F32)
    return pl.pallas_call(
        body, out_shape=(sh, sh, sh, sh), grid=(D // tr,),
        in_specs=[pl.BlockSpec((N_DEV, tr), lambda i: (0, i)), pl.BlockSpec((N_DEV, C), lambda i: (0, 0)), blk, blk, blk],
        out_specs=(blk,) * 4, name='ada_update', compiler_params=_params(('parallel',), 48))(c_all, dmod_cols, w, m, v)


def _to_sub(t, d):
    if d == 1:
        return t
    return t.reshape(L // d, d, t.shape[-1]).transpose(1, 0, 2).reshape(L, t.shape[-1])


def _from_sub(t, d):
    if d == 1:
        return t
    return t.reshape(d, L // d, t.shape[-1]).transpose(1, 0, 2).reshape(L, t.shape[-1])


def _rows_to_cluster_lanes(t):
    k = t.shape[1]
    return t.reshape(N_CL, CL_S, k).transpose(0, 2, 1)


def _blockdiag_in(t):
    t = t.reshape(N_CL, CL_G, SSM_N, SSM_P).transpose(0, 1, 3, 2)
    eye = jnp.eye(CL_G, dtype=t.dtype)
    t = t[:, :, :, None, :] * eye[None, :, None, :, None]
    return t.reshape(N_CL, CL_U, CL_S)


def _blockdiag_extract(t):
    t = t.reshape(N_CL, CL_G, SSM_P, CL_G, SSM_N)
    t = jnp.stack([t[:, i, :, i, :] for i in range(CL_G)], axis=1)
    return t.transpose(0, 1, 3, 2).reshape(SSM_GN, SSM_P)


def _c_to_rows(t):
    return t.transpose(0, 2, 1).reshape(SSM_GN, SSM_P)


def _rows_to_c(t):
    return t.reshape(SSM_G, SSM_N, SSM_P).transpose(0, 2, 1)


def _ssm_prep(sp):
    rows = lambda n: sp[n].reshape(SSM_GN, 1)
    a_re, a_im = rows('ssm_a_re'), rows('ssm_a_im')
    ldt = jnp.repeat(sp['ssm_log_dt'].reshape(SSM_G, 1), SSM_N, axis=0)
    b_re, b_im = sp['ssm_b_re'].reshape(SSM_GN, SSM_P), sp['ssm_b_im'].reshape(SSM_GN, SSM_P)
    c_re, c_im = _c_to_rows(sp['ssm_c_re'].reshape(SSM_G, SSM_P, SSM_N)), _c_to_rows(sp['ssm_c_im'].reshape(SSM_G, SSM_P, SSM_N))
    bbr, bbi, pwr, pwi = _ssm_pre(a_re, a_im, ldt, b_re, b_im)
    bm = jnp.concatenate([_blockdiag_in(bbr), _blockdiag_in(bbi)], axis=2)
    cmt = jnp.concatenate([_blockdiag_in(c_re), -_blockdiag_in(c_im)], axis=2)
    bmt, cm = bm.transpose(0, 2, 1), cmt.transpose(0, 2, 1)
    pw = jnp.concatenate([_rows_to_cluster_lanes(pwr), _rows_to_cluster_lanes(pwi)], axis=2)
    bm2, cm2 = _hilo(bm), _hilo(cm)
    bmt_b, cmt_b = bmt.astype(BF16), cmt.astype(BF16)
    return a_re, a_im, ldt, b_re, b_im, bm2, cm2, bmt_b, cmt_b, pw


def _tied(v, deps):
    for t in deps:
        v = v + t[0, 0]
    return v


def _local_step(x, pos, mod, tgt, sp, prep, get_w, emit, emit_small, emit_late, first_deps=()):
    sh1, sc1, gt1, sh2, sc2, gt2 = (mod[i:i + 1] for i in range(NMOD))
    vec = lambda n: sp[n].reshape(1, -1)
    a_re, a_im, ldt, b_re, b_im, bm2, cm2, bmt_b, cmt_b, pw = prep
    dvec = vec('ssm_d')

    h1 = _prenorm_fwd(x, vec('g_pre_mix'), sc1, sh1)
    w_in = get_w('w_in', h1)
    proj = _mm(h1, w_in, mode='nn', name='mm_in', tn=1408, deps=first_deps)
    fr1 =ROPE_THETA ** (-jnp.arange(0, ROT_DIM, 2, dtype=F32) / ROT_DIM)
    lane = jnp.arange(128) % HEAD_DIM
    fr = jnp.where(lane < ROT_DIM, fr1[lane % (ROT_DIM // 2)], 0.0).reshape(1, 128).astype(F32)
    qk = _rope(proj, pos, fr, 1.0, 'rope_fwd', carry=KVW)
    v_b = qk[:, ROPE_W:]
    u_off = (ROPE_W + KVW) // CL_U
    qp = jnp.stack([_to_sub(qk[:, gi * KVW:(gi + 1) * KVW], d) for gi, d in enumerate(DILATIONS)])
    kp = jnp.stack([_to_sub(qk[:, QW:ROPE_W], d) for d in DILATIONS])
    vp = jnp.stack([_to_sub(v_b, d) for d in DILATIONS])
    o_p, lse_p = _attn_fwd(qp, kp, vp)
    o3 = jnp.stack([_from_sub(o_p[gi], d) for gi, d in enumerate(DILATIONS)])
    lse3 = jnp.stack([_from_sub(lse_p[gi], d) for gi, d in enumerate(DILATIONS)])
    att = _combine_fwd(o3, lse3, vec('g_attn_out'))

    ypre, bnd, states = _ssm_fwd(proj, bm2, cm2, pw, dvec, u_off)
    w_glu = get_w('w_glu', ypre)
    ssm_n = _glu_fwd(ypre, w_glu, vec('b_glu'), vec('g_ssm_out'))

    cat = jnp.concatenate([att, ssm_n], axis=1)
    w_out = get_w('w_out', cat)
    mix = _mm(cat, w_out, mode='nn', name='mm_out', tk=1280)
    x1, h2 = _postmix_fwd(x, mix, vec('g_post_mix'), gt1, vec('g_pre_mlp'), sc2, sh2)
    w_mi = get_w('w_mlp_in', h2)
    a_pre, r_act = _mm(h2, w_mi, mode='nn', name='mm_mlp_in', epilogue='relu2', b_sharded=True)
    w_mo = get_w('w_mlp_out', a_pre)
    y = _mm(r_act, w_mo, mode='nn', name='mm_mlp_out')
    dx2, dy, loss, dgt2, dg_post_mlp = _final_fwd_bwd(x1, y, tgt, vec('g_post_mlp'), gt2)
    dgt2, dg_post_mlp = dgt2[:1], dg_post_mlp[:1]

    da = _mm(dy, w_mo, mode='nt', name='mm_d_act', out_dtype=BF16, epilogue='drelu2', extra=a_pre)
    dep = emit('w_mlp_out', _mm(r_act, dy, mode='tn', name='mm_dw_mlp_out', out_dtype=BF16))
    dh2 = _mm(da, w_mi, mode='nt', name='mm_dh2', tk=1024, b_sharded=True, deps=dep)
    dep = emit('w_mlp_in', _mm(h2, da, mode='tn', name='mm_dw_mlp_in', out_dtype=BF16, out_sharded=True))
    dx1, dmix, dsc2, dsh2, dg_pre_mlp, dgt1, dg_post_mix = _postmix_bwd(
        dx2, dh2, x1, mix, vec('g_post_mix'), gt1, vec('g_pre_mlp'), sc2)
    dsc2, dsh2, dg_pre_mlp, dgt1, dg_post_mix = (t[:1] for t in (dsc2, dsh2, dg_pre_mlp, dgt1, dg_post_mix))
    dcat = _mm(dmix, w_out, mode='nt', name='mm_dcat', tn=1280, deps=dep)
    dep = emit('w_out', _mm(cat, dmix, mode='tn', name='mm_dw_out', out_dtype=BF16, tm=640))
    datt, dsn = dcat[:, :KVW], dcat[:, KVW:]

    dypre, g_w_glu, g_b_glu, g_g_ssm = _glu_bwd(ypre, dsn, w_glu, _tied(vec('b_glu'), dep), vec('g_ssm_out'))
    g_b_glu, g_g_ssm = g_b_glu[:1], g_g_ssm[:1]
    dep = dep + emit('w_glu', g_w_glu.astype(BF16))
    du, dbm, dcm, dA, dD = _ssm_bwd(proj, dypre, states, bmt_b, cmt_b, pw, dvec, bnd, u_off)
    dD = dD[:1]
    gbr, gbi = _blockdiag_extract(dbm[:, :, :CL_S]), _blockdiag_extract(dbm[:, :, CL_S:])
    dcmt = dcm.transpose(0, 2, 1)
    g_c_re = _rows_to_c(_blockdiag_extract(dcmt[:, :, :CL_S]))
    g_c_im = _rows_to_c(-_blockdiag_extract(dcmt[:, :, CL_S:]))
    gar = dA[:, 0, :CL_S].reshape(SSM_GN, 1)
    gai = dA[:, 0, CL_S:].reshape(SSM_GN, 1)
    sel = (jnp.arange(SSM_GN)[None, :] // SSM_N == jnp.arange(SSM_G)[:, None]).astype(F32)
    g_a_re, g_a_im, g_b_re, g_b_im, g_ldt = _ssm_post(a_re, a_im, ldt, b_re, b_im, gar, gai, gbr, gbi, sel)

    head_ones = (jnp.arange(KVW)[:, None] // HEAD_DIM == jnp.arange(KVW)[None, :] // HEAD_DIM).astype(F32)
    do3, dlse3, g_g_attn = _combine_bwd(datt, o3, lse3, vec('g_attn_out'), head_ones)
    g_g_attn = g_g_attn[:1]
    dep_small = emit_small({
        'g_post_mix': dg_post_mix, 'ssm_a_re': g_a_re, 'ssm_a_im': g_a_im, 'ssm_log_dt': g_ldt[:, 0],
        'ssm_b_re': g_b_re, 'ssm_b_im': g_b_im, 'ssm_c_re': g_c_re, 'ssm_c_im': g_c_im, 'ssm_d': dD, 'b_glu': g_b_glu,
        'g_attn_out': g_g_attn, 'g_ssm_out': g_g_ssm, 'g_pre_mlp': dg_pre_mlp, 'g_post_mlp': dg_post_mlp})
    do_p = jnp.stack([_to_sub(do3[gi], d) for gi, d in enumerate(DILATIONS)])
    dlse_p = jnp.stack([_to_sub(dlse3[gi], d) for gi, d in enumerate(DILATIONS)])
    dq_p, dk_p, dv_p = _attn_bwd(qp, kp, vp, o_p, lse_p, do_p, dlse_p)
    dq = jnp.concatenate([_from_sub(dq_p[gi], d) for gi, d in enumerate(DILATIONS)], axis=1)
    dk = sum(_from_sub(dk_p[gi], d) for gi, d in enumerate(DILATIONS))
    dv = sum(_from_sub(dv_p[gi], d) for gi, d in enumerate(DILATIONS))
    dqk = _rope(jnp.concatenate([dq, dk], axis=1), pos, _tied(fr, dep_small), -1.0, 'rope_bwd')
    dproj = jnp.concatenate([dqk, dv.astype(BF16), du.astype(BF16)], axis=1)
    dh1 = _mm(dproj, w_in, mode='nt', name='mm_dh1', tk=1408, deps=dep)
    grad_x, dsc1, dsh1, dg_pre_mix = _prenorm_bwd(dx1, dh1, x, vec('g_pre_mix'), sc1)
    dsc1, dsh1, dg_pre_mix = dsc1[:1], dsh1[:1], dg_pre_mix[:1]
    dmod = jnp.concatenate([dsh1, dsc1, dgt1, dsh2, dsc2, dgt2], axis=0)
    dep = emit_late({'b_ada': dmod, 'g_pre_mix': dg_pre_mix})
    emit('w_in', _mm(h1, dproj, mode='tn', name='mm_dw_in', out_dtype=BF16, tn=1408, deps=dep))
    return loss[0, 0], grad_x


def _pack(d, names):
    flat = jnp.concatenate([jnp.pad(d[n].reshape(-1).astype(F32), (0, SEG[n] - SMALL_SIZES[n])) for n in names])
    return flat.reshape(-1, 128)


def _unpack(packed, names, shapes):
    out, off = {}, 0
    for n in names:
        out[n] = packed[off // 128:(off + SEG[n]) // 128].reshape(-1)[:SMALL_SIZES[n]].reshape(shapes[n])
        off += SEG[n]
    return out


def _shard_major(t, name):
    if name in ('w_in', 'w_out', 'w_mlp_in'):
        k, n = t.shape
        return t.reshape(k, N_DEV, n // N_DEV).transpose(1, 0, 2)
    k, n = t.shape
    return t.reshape(N_DEV, k // N_DEV, n)


def _from_shard_major(t, name):
    if name in ('w_in', 'w_out', 'w_mlp_in'):
        _, k, n = t.shape
        return t.transpose(1, 0, 2).reshape(k, N_DEV * n)
    _, k, n = t.shape
    return t.reshape(N_DEV * k, n)


def kernel(x, c, positions, w_ada, b_ada, g_pre_mix, g_post_mix, w_in, ssm_a_re, ssm_a_im, ssm_log_dt, ssm_b_re, ssm_b_im, ssm_c_re, ssm_c_im, ssm_d, w_glu, b_glu, g_attn_out, g_ssm_out, w_out, g_pre_mlp, g_post_mlp, w_mlp_in, w_mlp_out, loss_target, m_w_ada, m_b_ada, m_g_pre_mix, m_g_post_mix, m_w_in, m_ssm_a_re, m_ssm_a_im, m_ssm_log_dt, m_ssm_b_re, m_ssm_b_im, m_ssm_c_re, m_ssm_c_im, m_ssm_d, m_w_glu, m_b_glu, m_g_attn_out, m_g_ssm_out, m_w_out, m_g_pre_mlp, m_g_post_mlp, m_w_mlp_in, m_w_mlp_out, v_w_ada, v_b_ada, v_g_pre_mix, v_g_post_mix, v_w_in, v_ssm_a_re, v_ssm_a_im, v_ssm_log_dt, v_ssm_b_re, v_ssm_b_im, v_ssm_c_re, v_ssm_c_im, v_ssm_d, v_w_glu, v_b_glu, v_g_attn_out, v_g_ssm_out, v_w_out, v_g_pre_mlp, v_g_post_mlp, v_w_mlp_in, v_w_mlp_out):
    loc = dict(locals())
    W = {n: loc[n] for n in WEIGHTS}
    M = {n: loc['m_' + n] for n in WEIGHTS}
    V = {n: loc['v_' + n] for n in WEIGHTS}
    assert x.shape == (1, L, D) and w_in.shape == (1, D, INW // N_DEV), (x.shape, w_in.shape)

    sp = {n: W[n][0] for n in SMALL}
    gather = {'w_in': _push_start(w_in[0].astype(BF16), False, c, 'gather_start_w_in')}
    prep = _ssm_prep(sp)

    cw = NMOD * D // N_DEV
    c_all, mod8 = _mod_exchange(c.reshape(1, 1, D), w_ada[0], b_ada.reshape(N_DEV, 1, cw),
                                deps=(gather['w_in'][4],) + prep[5:])
    mod = mod8.reshape(NMOD, D)
    for n in BIG[1:]:
        gather[n] = _push_start(W[n][0].astype(BF16), False, mod8, 'gather_start_' + n)
    mod = mod + sum(gather[n][4][0, 0] for n in BIG)

    def get_w(n, after):
        g = _push_wait(gather[n], after, 'gather_wait_' + n)
        return g if n == 'w_mlp_in' else _from_shard_major(g, n)

    scatter = {}

    def emit(n, g):
        src = g if n == 'w_mlp_in' else _shard_major(g, n)
        scatter[n] = _push_start(src, True, src, 'scatter_start_' + n)
        return (scatter[n][4],)

    small_early = []

    def emit_small(d):
        pack = _pack(d, SMALL_EARLY)
        small_early.append(_push_start(pack, False, pack, 'small_start'))
        return (small_early[0][4],)

    out_g, out_d, out_m, out_v = {}, {}, {}, {}
    shapes = {n: W[n].shape[1:] for n in SMALL}

    def put(names, packs):
        for dst, packed in zip((out_g, out_d, out_m, out_v), packs):
            dst.update(_unpack(packed, names, shapes))

    late = []

    def emit_late(d):
        rows_all, *packs = _small_update(*[_pack(t, SMALL_LATE) for t in (d, W, M, V)])
        put(SMALL_LATE, packs)
        late.append(rows_all)
        return (rows_all,)

    loss, grad_x = _local_step(x[0], positions.reshape(L, 1), mod, loss_target[0], sp, prep, get_w, emit, emit_small,
                               emit_late)
    loss = lax.psum(loss, ('x', 'y', 'c'))

    me = 4 * lax.axis_index('x') + 2 * lax.axis_index('y') + lax.axis_index('c')
    dmod_all = late[0][:, :NMOD * D // 128].reshape(N_DEV, NMOD * D)
    dmod_cols = lax.dynamic_slice_in_dim(dmod_all, me * cw, cw, axis=1)
    out_g['w_ada'], out_d['w_ada'], out_m['w_ada'], out_v['w_ada'] = _ada_update(
        c_all.reshape(N_DEV, D), dmod_cols, w_ada[0], m_w_ada[0], v_w_ada[0])

    parts = _push_wait(small_early[0], out_v['w_ada'], 'small_wait')
    packs = _small_sum_update(parts, *[_pack(d, SMALL_EARLY) for d in (W, M, V)])
    put(SMALL_EARLY, packs)

    after = packs[3]
    for n in ('w_mlp_out', 'w_mlp_in', 'w_out', 'w_glu', 'w_in'):
        parts = _push_wait(scatter[n], after, 'scatter_wait_' + n)
        out_g[n], out_d[n], out_m[n], out_v[n] = _big_update(parts, W[n][0], M[n][0], V[n][0], 'update_' + n)
        after = out_v[n]

    lead = lambda t: t[None]
    return (loss, grad_x[None], *[lead(out_g[n]) for n in WEIGHTS], *[lead(out_d[n]) for n in WEIGHTS],
            *[lead(out_m[n]) for n in WEIGHTS], *[lead(out_v[n]) for n in WEIGHTS])
```

```python
import functools
import math

import jax
import jax.numpy as jnp
from jax import lax
from jax.experimental import pallas as pl
from jax.experimental.pallas import tpu as pltpu

F32 = jnp.float32
BF16 = jnp.bfloat16
HI = lax.Precision.HIGHEST
MESH = pl.DeviceIdType.MESH

N_DEV = 8
L = 4096
D = 2048
HEAD_DIM = 64
N_GROUPS = 3
DILATIONS = (1, 4, 16)
HEADS = 6
QW = N_GROUPS * HEADS * HEAD_DIM
KVW = HEADS * HEAD_DIM
ROT_DIM = 16
ROPE_THETA = 500000.0
BLK = 128
NBLK = L // BLK
SSMW = D - QW
SSM_P = 16
SSM_G = SSMW // SSM_P
SSM_N = 64
SSM_GN = SSM_G * SSM_N
CL_G = 8
N_CL = SSM_G // CL_G
CL_U = CL_G * SSM_P
CL_S = CL_G * SSM_N
INW = QW + 2 * KVW + SSMW
OUTW = KVW + SSMW
DFF = 4 * D
NMOD = 6
EPS = 1e-6
LR, B1, B2, AEPS, WD, STEP = 0.001, 0.9, 0.999, 1e-08, 0.01, 10

T_SCAN = 512
MB = 2 ** 20

WEIGHTS = ['w_ada', 'b_ada', 'g_pre_mix', 'g_post_mix', 'w_in', 'ssm_a_re', 'ssm_a_im', 'ssm_log_dt',
           'ssm_b_re', 'ssm_b_im', 'ssm_c_re', 'ssm_c_im', 'ssm_d', 'w_glu', 'b_glu', 'g_attn_out',
           'g_ssm_out', 'w_out', 'g_pre_mlp', 'g_post_mlp', 'w_mlp_in', 'w_mlp_out']
BIG = ['w_in', 'w_glu', 'w_out', 'w_mlp_in', 'w_mlp_out']
SMALL = [n for n in WEIGHTS if n not in BIG and n != 'w_ada']
SMALL_SIZES = {'b_ada': NMOD * D, 'g_pre_mix': D, 'g_post_mix': D, 'ssm_a_re': SSM_GN, 'ssm_a_im': SSM_GN,
               'ssm_log_dt': SSM_G, 'ssm_b_re': SSM_GN * SSM_P, 'ssm_b_im': SSM_GN * SSM_P,
               'ssm_c_re': SSM_GN * SSM_P, 'ssm_c_im': SSM_GN * SSM_P, 'ssm_d': SSMW, 'b_glu': SSMW,
               'g_attn_out': KVW, 'g_ssm_out': SSMW, 'g_pre_mlp': D, 'g_post_mlp': D}
SEG = {n: -(-SMALL_SIZES[n] // 1024) * 1024 for n in SMALL}
SMALL_LATE = ['b_ada', 'g_pre_mix']
SMALL_EARLY = [n for n in SMALL if n not in SMALL_LATE]


def _params(sem=None, vmem_mb=None):
    kw = {}
    if sem is not None:
        kw['dimension_semantics'] = sem
    if vmem_mb is not None:
        kw['vmem_limit_bytes'] = vmem_mb * MB
    return pltpu.CompilerParams(**kw)


def _vec(n):
    return pl.BlockSpec((1, n), lambda *_: (0, 0))


def _rms(x):
    return lax.rsqrt(jnp.mean(x * x, axis=-1, keepdims=True) + EPS)


def _rms_bwd(dn, n, r):
    return r * (dn - n * jnp.mean(dn * n, axis=-1, keepdims=True))


def _vec8(n):
    return pl.BlockSpec((8, n), lambda *_: (0, 0))


def _colsum(x):
    return jnp.sum(x.reshape(-1, 8, x.shape[-1]), axis=0)


def _fold8(last, *refs):
    @pl.when(last)
    def _():
        for r in refs:
            r[...] = jnp.broadcast_to(jnp.sum(r[...], axis=0, keepdims=True), r.shape)


def _mm(a, b, *, mode, name, out_dtype=F32, tm=1024, tn=1024, tk=2048, epilogue=None, extra=None,
        b_sharded=False, out_sharded=False, deps=()):
    if mode == 'nn':
        M, K = a.shape
        dims = (((1,), (0,)), ((), ()))
        a_spec = pl.BlockSpec((tm, tk), lambda i, j, k: (i, k))
        if b_sharded:
            _, K2, per = b.shape
            N, q = N_DEV * per, per // tn
            b_spec = pl.BlockSpec((None, tk, tn), lambda i, j, k: (j // q, k, j % q))
        else:
            K2, N = b.shape
            b_spec = pl.BlockSpec((tk, tn), lambda i, j, k: (k, j))
    elif mode == 'nt':
        M, K = a.shape
        dims = (((1,), (1,)), ((), ()))
        a_spec = pl.BlockSpec((tm, tk), lambda i, j, k: (i, k))
        if b_sharded:
            _, N, per = b.shape
            K2, q = N_DEV * per, per // tk
            b_spec = pl.BlockSpec((None, tn, tk), lambda i, j, k: (k // q, j, k % q))
        else:
            N, K2 = b.shape
            b_spec = pl.BlockSpec((tn, tk), lambda i, j, k: (j, k))
    else:
        (K, M), (K2, N) = a.shape, b.shape
        dims = (((0,), (0,)), ((), ()))
        a_spec = pl.BlockSpec((tk, tm), lambda i, j, k: (k, i))
        b_spec = pl.BlockSpec((tk, tn), lambda i, j, k: (k, j))
    assert K == K2 and M % tm == 0 and N % tn == 0 and K % tk == 0, (name, a.shape, b.shape, tm, tn, tk)
    nk = K // tk
    o_spec = pl.BlockSpec((tm, tn), lambda i, j, k: (i, j))
    o_dims = (M, N)
    if out_sharded:
        qo = N // N_DEV // tn
        o_spec = pl.BlockSpec((None, tm, tn), lambda i, j, k: (j // qo, i, j % qo))
        o_dims = (N_DEV, M, N // N_DEV)
    n_out = 2 if epilogue == 'relu2' else 1
    n_extra = 1 if extra is not None else 0
    n_in = 2 + n_extra + len(deps)

    def body(*refs):
        a_ref, b_ref = refs[0], refs[1]
        x_refs = refs[2:2 + n_extra]
        o_refs = refs[n_in:n_in + n_out]
        acc = refs[-1]
        k = pl.program_id(2)

        prod = lax.dot_general(a_ref[...], b_ref[...], dims, preferred_element_type=F32)

        def finish(r):
            if epilogue == 'relu2':
                o_refs[0][...] = r.astype(BF16)
                o_refs[1][...] = jnp.square(jnp.maximum(r, 0.0)).astype(BF16)
            elif epilogue == 'drelu2':
                pre = x_refs[0][...].astype(F32)
                o_refs[0][...] = (r * (2.0 * jnp.maximum(pre, 0.0))).astype(out_dtype)
            else:
                o_refs[0][...] = r.astype(out_dtype)

        if nk == 1:
            finish(prod)
        else:
            @pl.when(k == 0)
            def _():
                acc[...] = prod

            @pl.when((k > 0) & (k < nk - 1))
            def _():
                acc[...] += prod

            @pl.when(k == nk - 1)
            def _():
                finish(acc[...] + prod)

    if epilogue == 'relu2':
        out_shape = (jax.ShapeDtypeStruct((M, N), BF16), jax.ShapeDtypeStruct((M, N), BF16))
        out_specs = (o_spec, o_spec)
    else:
        out_shape = jax.ShapeDtypeStruct(o_dims, out_dtype)
        out_specs = o_spec
    args = (a, b) + ((extra,) if extra is not None else ()) + tuple(deps)
    in_specs = ([a_spec, b_spec] + ([o_spec] if extra is not None else [])
                + [pl.BlockSpec(memory_space=pl.ANY)] * len(deps))
    return pl.pallas_call(
        body, out_shape=out_shape, grid=(M // tm, N // tn, nk), in_specs=in_specs, out_specs=out_specs,
        scratch_shapes=[pltpu.VMEM((tm, tn) if nk > 1 else (8, 128), F32)], name=name,
        compiler_params=_params(('parallel', 'parallel', 'arbitrary'), 56))(*args)


TR = 256


def _rowspec(w=D):
    return pl.BlockSpec((TR, w), lambda i: (i, 0))


def _prenorm_fwd(x, g, sc, sh):
    def body(x_ref, g_ref, sc_ref, sh_ref, h_ref):
        xv = x_ref[...]
        n = xv * _rms(xv)
        h_ref[...] = ((n * g_ref[...]) * (1.0 + sc_ref[...]) + sh_ref[...]).astype(BF16)

    return pl.pallas_call(
        body, out_shape=jax.ShapeDtypeStruct((L, D), BF16), grid=(L // TR,),
        in_specs=[_rowspec(), _vec(D), _vec(D), _vec(D)], out_specs=_rowspec(), name='prenorm_fwd',
        compiler_params=_params(('parallel',), 40))(x, g, sc, sh)


def _postmix_fwd(x, mix, gpm, gt1, gpl, sc2, sh2):
    def body(x_ref, mix_ref, gpm_ref, gt1_ref, gpl_ref, sc2_ref, sh2_ref, x1_ref, h2_ref):
        mix_v = mix_ref[...]
        nm = mix_v * _rms(mix_v)
        x1 = x_ref[...] + gt1_ref[...] * (nm * gpm_ref[...])
        x1_ref[...] = x1
        n2 = x1 * _rms(x1)
        h2_ref[...] = ((n2 * gpl_ref[...]) * (1.0 + sc2_ref[...]) + sh2_ref[...]).astype(BF16)

    return pl.pallas_call(
        body, out_shape=(jax.ShapeDtypeStruct((L, D), F32), jax.ShapeDtypeStruct((L, D), BF16)), grid=(L // TR,),
        in_specs=[_rowspec(), _rowspec()] + [_vec(D)] * 5, out_specs=(_rowspec(), _rowspec()), name='postmix_fwd',
        compiler_params=_params(('parallel',), 40))(x, mix, gpm, gt1, gpl, sc2, sh2)


def _final_fwd_bwd(x1, y, tgt, g, gt2):
    def body(x1_ref, y_ref, t_ref, g_ref, gt2_ref, dx2_ref, dy_ref, loss_ref, dgt2_ref, dg_ref):
        @pl.when(pl.program_id(0) == 0)
        def _():
            loss_ref[...] = jnp.zeros_like(loss_ref)
            dgt2_ref[...] = jnp.zeros_like(dgt2_ref)
            dg_ref[...] = jnp.zeros_like(dg_ref)

        yv = y_ref[...]
        r = _rms(yv)
        n = yv * r
        ng = n * g_ref[...]
        x2 = x1_ref[...] + gt2_ref[...] * ng
        e = x2 - t_ref[...]
        loss_ref[...] += 0.5 * jnp.sum(jnp.mean(e * e, axis=-1, keepdims=True), axis=0, keepdims=True)
        dx2 = e * (1.0 / D)
        dx2_ref[...] = dx2
        dgt2_ref[...] += _colsum(dx2 * ng)
        dng = dx2 * gt2_ref[...]
        dg_ref[...] += _colsum(dng * n)
        dy_ref[...] = _rms_bwd(dng * g_ref[...], n, r).astype(BF16)
        _fold8(pl.program_id(0) == L // TR - 1, dgt2_ref, dg_ref)

    return pl.pallas_call(
        body,
        out_shape=(jax.ShapeDtypeStruct((L, D), F32), jax.ShapeDtypeStruct((L, D), BF16),
                   jax.ShapeDtypeStruct((8, 128), F32), jax.ShapeDtypeStruct((8, D), F32),
                   jax.ShapeDtypeStruct((8, D), F32)),
        grid=(L // TR,), in_specs=[_rowspec(), _rowspec(), _rowspec(), _vec(D), _vec(D)],
        out_specs=(_rowspec(), _rowspec(), _vec8(128), _vec8(D), _vec8(D)), name='final_fwd_bwd',
        compiler_params=_params(('arbitrary',), 40))(x1, y, tgt, g, gt2)


def _postmix_bwd(dx2, dh2, x1, mix, gpm, gt1, gpl, sc2):
    def body(dx2_ref, dh2_ref, x1_ref, mix_ref, gpm_ref, gt1_ref, gpl_ref, sc2_ref,
             dx1_ref, dmix_ref, dsc2_ref, dsh2_ref, dgpl_ref, dgt1_ref, dgpm_ref):
        @pl.when(pl.program_id(0) == 0)
        def _():
            for r_ in (dsc2_ref, dsh2_ref, dgpl_ref, dgt1_ref, dgpm_ref):
                r_[...] = jnp.zeros_like(r_)

        x1v = x1_ref[...]
        r2 = _rms(x1v)
        n2 = x1v * r2
        dh2v = dh2_ref[...]
        dsh2_ref[...] += _colsum(dh2v)
        dsc2_ref[...] += _colsum(dh2v * (n2 * gpl_ref[...]))
        t = dh2v * (1.0 + sc2_ref[...])
        dgpl_ref[...] += _colsum(t * n2)
        dx1 = dx2_ref[...] + _rms_bwd(t * gpl_ref[...], n2, r2)
        dx1_ref[...] = dx1
        mix_v = mix_ref[...]
        rm = _rms(mix_v)
        nm = mix_v * rm
        dgt1_ref[...] += _colsum(dx1 * (nm * gpm_ref[...]))
        u = dx1 * gt1_ref[...]
        dgpm_ref[...] += _colsum(u * nm)
        dmix_ref[...] = _rms_bwd(u * gpm_ref[...], nm, rm).astype(BF16)
        _fold8(pl.program_id(0) == L // TR - 1, dsc2_ref, dsh2_ref, dgpl_ref, dgt1_ref, dgpm_ref)

    vs = jax.ShapeDtypeStruct((8, D), F32)
    return pl.pallas_call(
        body, out_shape=(jax.ShapeDtypeStruct((L, D), F32), jax.ShapeDtypeStruct((L, D), BF16), vs, vs, vs, vs, vs),
        grid=(L // TR,), in_specs=[_rowspec()] * 4 + [_vec(D)] * 4,
        out_specs=(_rowspec(), _rowspec()) + (_vec8(D),) * 5, name='postmix_bwd',
        compiler_params=_params(('arbitrary',), 48))(dx2, dh2, x1, mix, gpm, gt1, gpl, sc2)


def _prenorm_bwd(dx1, dh1, x, g, sc1):
    def body(dx1_ref, dh1_ref, x_ref, g_ref, sc1_ref, dx_ref, dsc1_ref, dsh1_ref, dg_ref):
        @pl.when(pl.program_id(0) == 0)
        def _():
            for r_ in (dsc1_ref, dsh1_ref, dg_ref):
                r_[...] = jnp.zeros_like(r_)

        xv = x_ref[...]
        r = _rms(xv)
        n = xv * r
        dh = dh1_ref[...]
        dsh1_ref[...] += _colsum(dh)
        dsc1_ref[...] += _colsum(dh * (n * g_ref[...]))
        t = dh * (1.0 + sc1_ref[...])
        dg_ref[...] += _colsum(t * n)
        dx_ref[...] = dx1_ref[...] + _rms_bwd(t * g_ref[...], n, r)
        _fold8(pl.program_id(0) == L // TR - 1, dsc1_ref, dsh1_ref, dg_ref)

    vs = jax.ShapeDtypeStruct((8, D), F32)
    return pl.pallas_call(
        body, out_shape=(jax.ShapeDtypeStruct((L, D), F32), vs, vs, vs), grid=(L // TR,),
        in_specs=[_rowspec()] * 3 + [_vec(D)] * 2, out_specs=(_rowspec(),) + (_vec8(D),) * 3, name='prenorm_bwd',
        compiler_params=_params(('arbitrary',), 40))(dx1, dh1, x, g, sc1)


ROPE_W = QW + KVW


def _rope(xin, pos, fr, sign, name, carry=0):
    def body(x_ref, pos_ref, fr_ref, o_ref):
        xv = x_ref[:, 0:ROPE_W]
        ang = pos_ref[...].astype(F32) * fr_ref[...]
        w = lax.broadcasted_iota(jnp.int32, (1, 128), 1) % HEAD_DIM
        cs = jnp.cos(ang)
        sn = jnp.sin(ang) * sign
        s1 = jnp.where(w < ROT_DIM // 2, -sn, 0.0)
        s2 = jnp.where((w >= ROT_DIM // 2) & (w < ROT_DIM), sn, 0.0)
        rep = ROPE_W // 128
        cs, s1, s2 = jnp.tile(cs, (1, rep)), jnp.tile(s1, (1, rep)), jnp.tile(s2, (1, rep))
        hi = pltpu.roll(xv, ROPE_W - ROT_DIM // 2, 1)
        lo = pltpu.roll(xv, ROT_DIM // 2, 1)
        o_ref[:, 0:ROPE_W] = (xv * cs + hi * s1 + lo * s2).astype(BF16)
        if carry:
            o_ref[:, ROPE_W:ROPE_W + carry] = x_ref[:, ROPE_W:ROPE_W + carry].astype(BF16)

    return pl.pallas_call(
        body, out_shape=jax.ShapeDtypeStruct((L, ROPE_W + carry), BF16), grid=(L // TR,),
        in_specs=[_rowspec(ROPE_W + carry), pl.BlockSpec((TR, 1), lambda i: (i, 0)), _vec(128)],
        out_specs=_rowspec(ROPE_W + carry), name=name, compiler_params=_params(('parallel',), 40))(xin, pos, fr)


def _attn_mask(g, b):
    nbs = lax.shift_right_logical(jnp.int32(NBLK), 2 * g)
    first = (b & (nbs - 1)) == 0
    qi = lax.broadcasted_iota(jnp.int32, (BLK, 2 * BLK), 0)
    kj = lax.broadcasted_iota(jnp.int32, (BLK, 2 * BLK), 1)
    dist = qi + BLK - kj
    return (dist >= 0) & (dist <= BLK) & ((kj >= BLK) | jnp.logical_not(first))


def _blk(idx):
    return pl.BlockSpec((None, BLK, KVW), idx)


_CUR = lambda g, b: (g, b, 0)
_PREV = lambda g, b: (g, jnp.maximum(b - 1, 0), 0)
NEG = -1e30
NT_DIMS = (((1,), (1,)), ((), ()))
TN_DIMS = (((0,), (0,)), ((), ()))


def _attn_fwd(qp, kp, vp):
    def body(q_ref, kp_ref, kc_ref, vp_ref, vc_ref, o_ref, lse_ref):
        valid = _attn_mask(pl.program_id(0), pl.program_id(1))
        for h in range(HEADS):
            hs = slice(h * HEAD_DIM, (h + 1) * HEAD_DIM)
            q = q_ref[:, hs]
            kc = jnp.concatenate([kp_ref[:, hs], kc_ref[:, hs]], axis=0)
            vc = jnp.concatenate([vp_ref[:, hs], vc_ref[:, hs]], axis=0)
            s = lax.dot_general(q, kc, NT_DIMS, preferred_element_type=F32) * 0.125
            s = jnp.where(valid, s, NEG)
            m = jnp.max(s, axis=-1, keepdims=True)
            p = jnp.exp(s - m)
            l = jnp.sum(p, axis=-1, keepdims=True)
            o = jnp.dot(p.astype(BF16), vc, preferred_element_type=F32) / l
            o_ref[:, hs] = o
            lse_ref[:, hs] = jnp.broadcast_to(m + jnp.log(l), (BLK, HEAD_DIM))

    sh = jax.ShapeDtypeStruct((N_GROUPS, L, KVW), F32)
    return pl.pallas_call(
        body, out_shape=(sh, sh), grid=(N_GROUPS, NBLK),
        in_specs=[_blk(_CUR), _blk(_PREV), _blk(_CUR), _blk(_PREV), _blk(_CUR)],
        out_specs=(_blk(_CUR), _blk(_CUR)), name='attn_fwd',
        compiler_params=_params(('parallel', 'parallel'), 32))(qp, kp, kp, vp, vp)


def _attn_bwd(qp, kp, vp, o, lse, do, dlse):
    def body(q_ref, kp_ref, kc_ref, vp_ref, vc_ref, o_ref, lse_ref, do_ref, dlse_ref, dq_ref, dk_ref, dv_ref):
        b = pl.program_id(1)

        @pl.when(b == 0)
        def _():
            dk_ref[...] = jnp.zeros_like(dk_ref)
            dv_ref[...] = jnp.zeros_like(dv_ref)

        valid = _attn_mask(pl.program_id(0), b)
        prev0 = pl.multiple_of(jnp.maximum(b - 1, 0) * BLK, BLK)
        cur0 = pl.multiple_of(b * BLK, BLK)
        for h in range(HEADS):
            hs = slice(h * HEAD_DIM, (h + 1) * HEAD_DIM)
            q = q_ref[:, hs]
            kc = jnp.concatenate([kp_ref[:, hs], kc_ref[:, hs]], axis=0)
            vc = jnp.concatenate([vp_ref[:, hs], vc_ref[:, hs]], axis=0)
            s = lax.dot_general(q, kc, NT_DIMS, preferred_element_type=F32) * 0.125
            s = jnp.where(valid, s, NEG)
            p = jnp.exp(s - lse_ref[:, h * HEAD_DIM:h * HEAD_DIM + 1])
            do_h = do_ref[:, hs]
            delta = jnp.sum(do_h * o_ref[:, hs], axis=-1, keepdims=True)
            do_b = do_h.astype(BF16)
            dp = lax.dot_general(do_b, vc, NT_DIMS, preferred_element_type=F32)
            ds = p * (dp - delta + dlse_ref[:, h * HEAD_DIM:h * HEAD_DIM + 1])
            ds_b = (ds * 0.125).astype(BF16)
            dq_ref[:, hs] = jnp.dot(ds_b, kc, preferred_element_type=F32)
            dkc = lax.dot_general(ds_b, q, TN_DIMS, preferred_element_type=F32)
            dvc = lax.dot_general(p.astype(BF16), do_b, TN_DIMS, preferred_element_type=F32)
            dk_ref[pl.ds(prev0, BLK), hs] += dkc[:BLK]
            dv_ref[pl.ds(prev0, BLK), hs] += dvc[:BLK]
            dk_ref[pl.ds(cur0, BLK), hs] += dkc[BLK:]
            dv_ref[pl.ds(cur0, BLK), hs] += dvc[BLK:]

    sh = jax.ShapeDtypeStruct((N_GROUPS, L, KVW), F32)
    whole = pl.BlockSpec((None, L, KVW), lambda g, b: (g, 0, 0))
    return pl.pallas_call(
        body, out_shape=(sh, sh, sh), grid=(N_GROUPS, NBLK),
        in_specs=[_blk(_CUR), _blk(_PREV), _blk(_CUR), _blk(_PREV), _blk(_CUR)] + [_blk(_CUR)] * 4,
        out_specs=(_blk(_CUR), whole, whole), name='attn_bwd',
        compiler_params=_params(('arbitrary', 'arbitrary'), 48))(qp, kp, kp, vp, vp, o, lse, do, dlse)


TC = 512


def _comb_spec():
    return pl.BlockSpec((N_GROUPS, TC, KVW), lambda i: (0, i, 0))


def _combine_weights(lse_ref):
    l0, l1, l2 = lse_ref[0], lse_ref[1], lse_ref[2]
    m = jnp.maximum(jnp.maximum(l0, l1), l2)
    e0, e1, e2 = jnp.exp(l0 - m), jnp.exp(l1 - m), jnp.exp(l2 - m)
    z = e0 + e1 + e2
    return e0 / z, e1 / z, e2 / z


def _combine_fwd(o3, lse3, g):
    def body(o_ref, lse_ref, g_ref, att_ref):
        w0, w1, w2 = _combine_weights(lse_ref)
        a = w0 * o_ref[0] + w1 * o_ref[1] + w2 * o_ref[2]
        att_ref[...] = ((a * _rms(a)) * g_ref[...]).astype(BF16)

    return pl.pallas_call(
        body, out_shape=jax.ShapeDtypeStruct((L, KVW), BF16), grid=(L // TC,),
        in_specs=[_comb_spec(), _comb_spec(), _vec(KVW)], out_specs=pl.BlockSpec((TC, KVW), lambda i: (i, 0)),
        name='combine_fwd', compiler_params=_params(('parallel',), 40))(o3, lse3, g)


def _combine_bwd(datt, o3, lse3, g, head_ones):
    def body(datt_ref, o_ref, lse_ref, g_ref, e_ref, do_ref, dlse_ref, dg_ref):
        @pl.when(pl.program_id(0) == 0)
        def _():
            dg_ref[...] = jnp.zeros_like(dg_ref)

        ws = _combine_weights(lse_ref)
        a = ws[0] * o_ref[0] + ws[1] * o_ref[1] + ws[2] * o_ref[2]
        r = _rms(a)
        n = a * r
        dv = datt_ref[...]
        dg_ref[...] += _colsum(dv * n)
        da = _rms_bwd(dv * g_ref[...], n, r)
        dws = [jnp.dot(da * o_ref[i], e_ref[...], preferred_element_type=F32, precision=HI) for i in range(3)]
        dbar = ws[0] * dws[0] + ws[1] * dws[1] + ws[2] * dws[2]
        for i in range(3):
            do_ref[i] = ws[i] * da
            dlse_ref[i] = ws[i] * (dws[i] - dbar)
        _fold8(pl.program_id(0) == L // TC - 1, dg_ref)

    sh = jax.ShapeDtypeStruct((N_GROUPS, L, KVW), F32)
    return pl.pallas_call(
        body, out_shape=(sh, sh, jax.ShapeDtypeStruct((8, KVW), F32)), grid=(L // TC,),
        in_specs=[pl.BlockSpec((TC, KVW), lambda i: (i, 0)), _comb_spec(), _comb_spec(), _vec(KVW),
                  pl.BlockSpec((KVW, KVW), lambda i: (0, 0))],
        out_specs=(_comb_spec(), _comb_spec(), _vec8(KVW)), name='combine_bwd',
        compiler_params=_params(('arbitrary',), 48))(datt, o3, lse3, g, head_ones)


def _ssm_disc(ar, ai, ldt):
    dt = jnp.exp(ldt)
    zr, zi = ar * dt, ai * dt
    ez = jnp.exp(zr)
    A_r, A_i = ez * jnp.cos(zi), ez * jnp.sin(zi)
    den = ar * ar + ai * ai
    xr, xi = A_r - 1.0, A_i
    cr = (xr * ar + xi * ai) / den
    ci = (xi * ar - xr * ai) / den
    return dt, zr, zi, A_r, A_i, den, cr, ci


def _ssm_pre(ar, ai, ldt, br, bi):
    def body(ar_ref, ai_ref, ldt_ref, br_ref, bi_ref, bbr_ref, bbi_ref, pwr_ref, pwi_ref):
        _, zr, zi, _, _, _, cr, ci = _ssm_disc(ar_ref[...], ai_ref[...], ldt_ref[...])
        bbr_ref[...] = cr * br_ref[...] - ci * bi_ref[...]
        bbi_ref[...] = cr * bi_ref[...] + ci * br_ref[...]
        k = (lax.broadcasted_iota(jnp.int32, (1, 8), 1) + 1).astype(F32)
        ek = jnp.exp(zr * k)
        pwr_ref[...] = ek * jnp.cos(zi * k)
        pwi_ref[...] = ek * jnp.sin(zi * k)

    s16 = jax.ShapeDtypeStruct((SSM_GN, SSM_P), F32)
    s8 = jax.ShapeDtypeStruct((SSM_GN, 8), F32)
    return pl.pallas_call(body, out_shape=(s16, s16, s8, s8), name='ssm_pre',
                          compiler_params=_params(None, 40))(ar, ai, ldt, br, bi)


def _ssm_post(ar, ai, ldt, br, bi, gar, gai, gbr, gbi, sel):
    def body(ar_ref, ai_ref, ldt_ref, br_ref, bi_ref, gar_ref, gai_ref, gbr_ref, gbi_ref, sel_ref,
             dar_ref, dai_ref, dbr_ref, dbi_ref, dldt_ref):
        a_r, a_i = ar_ref[...], ai_ref[...]
        dt, _, _, A_r, A_i, den, cr, ci = _ssm_disc(a_r, a_i, ldt_ref[...])
        b_r, b_i, g_br, g_bi = br_ref[...], bi_ref[...], gbr_ref[...], gbi_ref[...]
        gcr = jnp.sum(g_br * b_r + g_bi * b_i, axis=-1, keepdims=True)
        gci = jnp.sum(g_bi * b_r - g_br * b_i, axis=-1, keepdims=True)
        dbr_ref[...] = g_br * cr + g_bi * ci
        dbi_ref[...] = g_bi * cr - g_br * ci
        g_ar = gar_ref[...] + (gcr * a_r - gci * a_i) / den
        g_ai = gai_ref[...] + (gcr * a_i + gci * a_r) / den
        qr = (cr * a_r + ci * a_i) / den
        qi = (ci * a_r - cr * a_i) / den
        glr = -(gcr * qr + gci * qi)
        gli = -(gci * qr - gcr * qi)
        gzr = g_ar * A_r + g_ai * A_i
        gzi = g_ai * A_r - g_ar * A_i
        dar_ref[...] = glr + gzr * dt
        dai_ref[...] = gli + gzi * dt
        gdt = (gzr * a_r + gzi * a_i) * dt
        dldt_ref[...] = jnp.dot(sel_ref[...], jnp.broadcast_to(gdt, (SSM_GN, 128)),
                                preferred_element_type=F32, precision=HI)

    s1 = jax.ShapeDtypeStruct((SSM_GN, 1), F32)
    s16 = jax.ShapeDtypeStruct((SSM_GN, SSM_P), F32)
    return pl.pallas_call(body, out_shape=(s1, s1, s16, s16, jax.ShapeDtypeStruct((SSM_G, 128), F32)),
                          name='ssm_post', compiler_params=_params(None, 48))(
                              ar, ai, ldt, br, bi, gar, gai, gbr, gbi, sel)


SCAN_CH = 8


def _scan_fwd_tiles(s_ref, pw, carry):
    pwr, pwi = pw[:, :CL_S], pw[:, CL_S:]
    row = lax.broadcasted_iota(jnp.int32, (8, CL_S), 0)
    steps = [(k, jnp.where(row >= k, pwr[k - 1:k], 0.0), jnp.where(row >= k, pwi[k - 1:k], 0.0)) for k in (1, 2, 4)]
    rows = 8 * SCAN_CH

    def chunk(i, c):
        cr, ci = c
        r0 = pl.multiple_of(i * rows, rows)
        xr = s_ref[pl.ds(r0, rows), 0:CL_S].reshape(SCAN_CH, 8, CL_S)
        xi = s_ref[pl.ds(r0, rows), CL_S:2 * CL_S].reshape(SCAN_CH, 8, CL_S)
        for k, pr, pi in steps:
            sr, si = pltpu.roll(xr, k, 1), pltpu.roll(xi, k, 1)
            xr, xi = xr + pr * sr - pi * si, xi + pr * si + pi * sr
        for j in range(SCAN_CH):
            tr = xr[j] + pwr * cr - pwi * ci
            ti = xi[j] + pwr * ci + pwi * cr
            s_ref[pl.ds(r0 + 8 * j, 8), 0:CL_S] = tr
            s_ref[pl.ds(r0 + 8 * j, 8), CL_S:2 * CL_S] = ti
            cr, ci = tr[7:8], ti[7:8]
        return cr, ci

    return lax.fori_loop(0, T_SCAN // rows, chunk, (carry[:, :CL_S], carry[:, CL_S:]))


def _scan_bwd_tiles(l_ref, pw, carry):
    pwr, pwi = pw[:, :CL_S], pw[:, CL_S:]
    rpr = jnp.concatenate([pwr[7 - r:8 - r] for r in range(8)], axis=0)
    rpi = jnp.concatenate([pwi[7 - r:8 - r] for r in range(8)], axis=0)
    row = lax.broadcasted_iota(jnp.int32, (8, CL_S), 0)
    steps = [(k, jnp.where(row < 8 - k, pwr[k - 1:k], 0.0), jnp.where(row < 8 - k, pwi[k - 1:k], 0.0))
             for k in (1, 2, 4)]
    rows = 8 * SCAN_CH
    nc = T_SCAN // rows

    def chunk(i, c):
        cr, ci = c
        r0 = pl.multiple_of((nc - 1 - i) * rows, rows)
        xr = l_ref[pl.ds(r0, rows), 0:CL_S].reshape(SCAN_CH, 8, CL_S)
        xi = l_ref[pl.ds(r0, rows), CL_S:2 * CL_S].reshape(SCAN_CH, 8, CL_S)
        for k, pr, pi in steps:
            sr, si = pltpu.roll(xr, 8 - k, 1), pltpu.roll(xi, 8 - k, 1)
            xr, xi = xr + pr * sr + pi * si, xi + pr * si - pi * sr
        for j in reversed(range(SCAN_CH)):
            tr = xr[j] + rpr * cr + rpi * ci
            ti = xi[j] + rpr * ci - rpi * cr
            l_ref[pl.ds(r0 + 8 * j, 8), 0:CL_S] = tr
            l_ref[pl.ds(r0 + 8 * j, 8), CL_S:2 * CL_S] = ti
            cr, ci = tr[0:1], ti[0:1]
        return cr, ci

    return lax.fori_loop(0, nc, chunk, (carry[:, :CL_S], carry[:, CL_S:]))


NT_SCAN = L // T_SCAN


def _hilo(t):
    hi = t.astype(BF16)
    return jnp.stack([hi, (t - hi.astype(F32)).astype(BF16)], axis=1)


def _dot3(a, b_ref):
    ah = a.astype(BF16)
    al = (a - ah.astype(F32)).astype(BF16)
    bh, bl = b_ref[0], b_ref[1]
    return (jnp.dot(ah, bh, preferred_element_type=F32) + jnp.dot(al, bh, preferred_element_type=F32)
            + jnp.dot(ah, bl, preferred_element_type=F32))


def _hl_spec(r, c):
    return pl.BlockSpec((None, 2, r, c), lambda c_, t: (c_, 0, 0, 0))


def _ssm_fwd(u, bm2, cm2, pw, dvec, u_off=0):
    def body(u_ref, bm_ref, cm_ref, pw_ref, d_ref, y_ref, bnd_ref, s_ref, carry_ref):
        @pl.when(pl.program_id(1) == 0)
        def _():
            carry_ref[...] = jnp.zeros_like(carry_ref)

        bnd_ref[...] = carry_ref[...]
        uv = u_ref[...]
        s_ref[...] = _dot3(uv, bm_ref)
        cr, ci = _scan_fwd_tiles(s_ref, pw_ref[...], carry_ref[...])
        carry_ref[...] = jnp.concatenate([cr, ci], axis=1)
        y_ref[...] = _dot3(s_ref[...], cm_ref) + d_ref[...] * uv

    return pl.pallas_call(
        body,
        out_shape=(jax.ShapeDtypeStruct((L, SSMW), F32), jax.ShapeDtypeStruct((N_CL, NT_SCAN, 1, 2 * CL_S), F32),
                   jax.ShapeDtypeStruct((L, N_CL * 2 * CL_S), F32)),
        grid=(N_CL, NT_SCAN),
        in_specs=[pl.BlockSpec((T_SCAN, CL_U), lambda c, t: (t, c + u_off)),
                  _hl_spec(CL_U, 2 * CL_S), _hl_spec(2 * CL_S, CL_U),
                  pl.BlockSpec((None, 8, 2 * CL_S), lambda c, t: (c, 0, 0)),
                  pl.BlockSpec((1, CL_U), lambda c, t: (0, c))],
        out_specs=(pl.BlockSpec((T_SCAN, CL_U), lambda c, t: (t, c)),
                   pl.BlockSpec((None, None, 1, 2 * CL_S), lambda c, t: (c, t, 0, 0)),
                   pl.BlockSpec((T_SCAN, 2 * CL_S), lambda c, t: (t, c))),
        scratch_shapes=[pltpu.VMEM((1, 2 * CL_S), F32)],
        name='ssm_fwd', compiler_params=_params(('arbitrary', 'arbitrary'), 40))(u, bm2, cm2, pw, dvec)


def _ssm_bwd(u, dy, states, bmt, cmt, pw, dvec, bnd, u_off=0):
    rev = lambda t: NT_SCAN - 1 - t

    def body(u_ref, dy_ref, s_ref, bmt_ref, cmt_ref, pw_ref, d_ref, bnd_ref,
             du_ref, dbm_ref, dcm_ref, da_ref, dd_ref, l_ref, carry_ref):
        @pl.when(pl.program_id(1) == 0)
        def _():
            carry_ref[...] = jnp.zeros_like(carry_ref)
            dbm_ref[...] = jnp.zeros_like(dbm_ref)
            dcm_ref[...] = jnp.zeros_like(dcm_ref)
            da_ref[...] = jnp.zeros_like(da_ref)
            dd_ref[...] = jnp.zeros_like(dd_ref)

        uv, dyv, pw = u_ref[...], dy_ref[...], pw_ref[...]
        dy_b = dyv.astype(BF16)
        entry = bnd_ref[...]
        l_ref[...] = jnp.dot(dy_b, cmt_ref[...], preferred_element_type=F32)
        cr, ci = _scan_bwd_tiles(l_ref, pw, carry_ref[...])
        carry_ref[...] = jnp.concatenate([cr, ci], axis=1)
        sv, lv = s_ref[...], l_ref[...]
        lv_b = lv.astype(BF16)
        du_ref[...] = dyv * d_ref[...] + jnp.dot(lv_b, bmt_ref[...], preferred_element_type=F32)
        dbm_ref[...] += lax.dot_general(uv.astype(BF16), lv_b, TN_DIMS, preferred_element_type=F32)
        dcm_ref[...] += lax.dot_general(sv.astype(BF16), dy_b, TN_DIMS, preferred_element_type=F32)
        dd_ref[...] += _colsum(dyv * uv)
        row = lax.broadcasted_iota(jnp.int32, (T_SCAN, 2 * CL_S), 0)
        sp = jnp.where(row == 0, entry, pltpu.roll(sv, 1, 0))
        spr, spi = sp[:, :CL_S], sp[:, CL_S:]
        lr, li = lv[:, :CL_S], lv[:, CL_S:]
        da_ref[:, 0:CL_S] += _colsum(lr * spr + li * spi)
        da_ref[:, CL_S:2 * CL_S] += _colsum(li * spr - lr * spi)
        _fold8(pl.program_id(1) == NT_SCAN - 1, da_ref, dd_ref)

    return pl.pallas_call(
        body,
        out_shape=(jax.ShapeDtypeStruct((L, SSMW), F32), jax.ShapeDtypeStruct((N_CL, CL_U, 2 * CL_S), F32),
                   jax.ShapeDtypeStruct((N_CL, 2 * CL_S, CL_U), F32), jax.ShapeDtypeStruct((N_CL, 8, 2 * CL_S), F32),
                   jax.ShapeDtypeStruct((8, SSMW), F32)),
        grid=(N_CL, NT_SCAN),
        in_specs=[pl.BlockSpec((T_SCAN, CL_U), lambda c, t: (rev(t), c + u_off)),
                  pl.BlockSpec((T_SCAN, CL_U), lambda c, t: (rev(t), c)),
                  pl.BlockSpec((T_SCAN, 2 * CL_S), lambda c, t: (rev(t), c)),
                  pl.BlockSpec((None, 2 * CL_S, CL_U), lambda c, t: (c, 0, 0)),
                  pl.BlockSpec((None, CL_U, 2 * CL_S), lambda c, t: (c, 0, 0)),
                  pl.BlockSpec((None, 8, 2 * CL_S), lambda c, t: (c, 0, 0)),
                  pl.BlockSpec((1, CL_U), lambda c, t: (0, c)),
                  pl.BlockSpec((None, None, 1, 2 * CL_S), lambda c, t: (c, rev(t), 0, 0))],
        out_specs=(pl.BlockSpec((T_SCAN, CL_U), lambda c, t: (rev(t), c)),
                   pl.BlockSpec((None, CL_U, 2 * CL_S), lambda c, t: (c, 0, 0)),
                   pl.BlockSpec((None, 2 * CL_S, CL_U), lambda c, t: (c, 0, 0)),
                   pl.BlockSpec((None, 8, 2 * CL_S), lambda c, t: (c, 0, 0)),
                   pl.BlockSpec((8, CL_U), lambda c, t: (0, c))),
        scratch_shapes=[pltpu.VMEM((T_SCAN, 2 * CL_S), F32), pltpu.VMEM((1, 2 * CL_S), F32)],
        name='ssm_bwd', compiler_params=_params(('arbitrary', 'arbitrary'), 48))(u, dy, states, bmt, cmt, pw, dvec, bnd)


GELU_C = math.sqrt(2.0 / math.pi)
GELU_K = 0.044715


def _gelu_parts(x):
    t = jnp.tanh(GELU_C * (x + GELU_K * (x * x * x)))
    return x * (0.5 * (1.0 + t)), t


def _glu_fwd(ypre, wglu, bglu, gs):
    def body(y_ref, w_ref, b_ref, g_ref, o_ref):
        yg, _ = _gelu_parts(y_ref[...])
        z = jnp.dot(yg.astype(BF16), w_ref[...], preferred_element_type=F32) + b_ref[...]
        s = yg * jax.nn.sigmoid(z)
        o_ref[...] = ((s * _rms(s)) * g_ref[...]).astype(BF16)

    return pl.pallas_call(
        body, out_shape=jax.ShapeDtypeStruct((L, SSMW), BF16), grid=(L // TR,),
        in_specs=[_rowspec(SSMW), pl.BlockSpec((SSMW, SSMW), lambda i: (0, 0)), _vec(SSMW), _vec(SSMW)],
        out_specs=_rowspec(SSMW), name='glu_fwd', compiler_params=_params(('parallel',), 32))(ypre, wglu, bglu, gs)


def _glu_bwd(ypre, dsn, wglu, bglu, gs):
    def body(y_ref, d_ref, w_ref, b_ref, g_ref, dy_ref, dw_ref, db_ref, dg_ref):
        @pl.when(pl.program_id(0) == 0)
        def _():
            dw_ref[...] = jnp.zeros_like(dw_ref)
            db_ref[...] = jnp.zeros_like(db_ref)
            dg_ref[...] = jnp.zeros_like(dg_ref)

        xv = y_ref[...]
        yg, t = _gelu_parts(xv)
        yg_b = yg.astype(BF16)
        z = jnp.dot(yg_b, w_ref[...], preferred_element_type=F32) + b_ref[...]
        sg = jax.nn.sigmoid(z)
        s = yg * sg
        r = _rms(s)
        n = s * r
        dv = d_ref[...]
        dg_ref[...] += _colsum(dv * n)
        ds = _rms_bwd(dv * g_ref[...], n, r)
        dz = (ds * yg) * (sg * (1.0 - sg))
        dz_b = dz.astype(BF16)
        db_ref[...] += _colsum(dz)
        dw_ref[...] += lax.dot_general(yg_b, dz_b, TN_DIMS, preferred_element_type=F32)
        dyg = ds * sg + lax.dot_general(dz_b, w_ref[...], NT_DIMS, preferred_element_type=F32)
        dgelu = 0.5 * (1.0 + t) + (0.5 * xv) * (1.0 - t * t) * (GELU_C * (1.0 + 3.0 * GELU_K * (xv * xv)))
        dy_ref[...] = dyg * dgelu
        _fold8(pl.program_id(0) == L // TR - 1, db_ref, dg_ref)

    vs = jax.ShapeDtypeStruct((8, SSMW), F32)
    return pl.pallas_call(
        body, out_shape=(jax.ShapeDtypeStruct((L, SSMW), F32), jax.ShapeDtypeStruct((SSMW, SSMW), F32), vs, vs),
        grid=(L // TR,),
        in_specs=[_rowspec(SSMW), _rowspec(SSMW), pl.BlockSpec((SSMW, SSMW), lambda i: (0, 0)), _vec(SSMW), _vec(SSMW)],
        out_specs=(_rowspec(SSMW), pl.BlockSpec((SSMW, SSMW), lambda i: (0, 0)), _vec8(SSMW), _vec8(SSMW)),
        name='glu_bwd', compiler_params=_params(('arbitrary',), 40))(ypre, dsn, wglu, bglu, gs)


def _me():
    return lax.axis_index('x'), lax.axis_index('y'), lax.axis_index('c')


def _my_index():
    return 4 * lax.axis_index('x') + 2 * lax.axis_index('y') + lax.axis_index('c')


def _peer(k):
    x, y, c = _me()
    px = 1 - x if k & 4 else x
    py = 1 - y if k & 2 else y
    pc = 1 - c if k & 1 else c
    return (px, py, pc), 4 * px + 2 * py + pc


def _mod_exchange(c_row, w_ada, b_ada8, deps=()):
    cw = NMOD * D // N_DEV

    def body(c_ref, w_ref, b_ref, *rest):
        call_ref, mod_ref, part_ref, send_sems, recv_sems = rest[len(deps):]
        x, y, c = _me()
        me = 4 * x + 2 * y + c
        call_ref[me] = c_ref[0]
        sends = []
        for k in range(1, N_DEV):
            peer, _ = _peer(k)
            cp = pltpu.make_async_remote_copy(src_ref=c_ref.at[0], dst_ref=call_ref.at[me], send_sem=send_sems.at[0, k - 1],
                                              recv_sem=recv_sems.at[0, k - 1], device_id=peer, device_id_type=MESH)
            cp.start()
            sends.append(cp)
        for k in range(1, N_DEV):
            peer, pidx = _peer(k)
            pltpu.make_async_remote_copy(src_ref=c_ref.at[0], dst_ref=call_ref.at[pidx], send_sem=send_sems.at[0, k - 1],
                                         recv_sem=recv_sems.at[0, k - 1], device_id=peer, device_id_type=MESH).wait_recv()
        for cp in sends:
            cp.wait_send()
        cv = call_ref[...].reshape(N_DEV, D)
        part = jnp.dot(cv * jax.nn.sigmoid(cv), w_ref[...], preferred_element_type=F32, precision=HI)
        part_ref[...] = part.reshape(N_DEV, 1, cw)
        mod_ref[me] = part_ref[me]
        sends = []
        for k in range(1, N_DEV):
            peer, pidx = _peer(k)
            cp = pltpu.make_async_remote_copy(src_ref=part_ref.at[pidx], dst_ref=mod_ref.at[me], send_sem=send_sems.at[1, k - 1],
                                              recv_sem=recv_sems.at[1, k - 1], device_id=peer, device_id_type=MESH)
            cp.start()
            sends.append(cp)
        for k in range(1, N_DEV):
            peer, pidx = _peer(k)
            pltpu.make_async_remote_copy(src_ref=part_ref.at[pidx], dst_ref=mod_ref.at[pidx], send_sem=send_sems.at[1, k - 1],
                                         recv_sem=recv_sems.at[1, k - 1], device_id=peer, device_id_type=MESH).wait_recv()
        for cp in sends:
            cp.wait_send()
        mod_ref[...] = mod_ref[...] + b_ref[...]

    vm = pl.BlockSpec(memory_space=pltpu.VMEM)
    return pl.pallas_call(
        body, out_shape=(jax.ShapeDtypeStruct((N_DEV, 1, D), F32), jax.ShapeDtypeStruct((N_DEV, 1, cw), F32)),
        in_specs=[vm, vm, vm] + [pl.BlockSpec(memory_space=pl.ANY)] * len(deps), out_specs=(vm, vm),
        scratch_shapes=[pltpu.VMEM((N_DEV, 1, cw), F32), pltpu.SemaphoreType.DMA((2, N_DEV - 1)),
                        pltpu.SemaphoreType.DMA((2, N_DEV - 1))],
        name='mod_exchange', compiler_params=_params(None, 48))(c_row, w_ada, b_ada8, *deps)


HBM_SPEC = pl.BlockSpec(memory_space=pltpu.HBM)
SEM_SPEC = pl.BlockSpec(memory_space=pltpu.SEMAPHORE)
DATAFLOW = pltpu.SideEffectType.DATAFLOW_SIDE_EFFECTING


def _push_start(src, scatter, after, name):
    land = lax.empty(src.shape if scatter else (N_DEV,) + src.shape, src.dtype)

    def body(src_ref, land_ref, after_ref, send_sem, recv_sem, land_thru, token):
        x, y, c = _me()
        me = 4 * x + 2 * y + c
        for k in range(1, N_DEV):
            peer, pidx = _peer(k)
            pltpu.make_async_remote_copy(src_ref=src_ref.at[pidx] if scatter else src_ref, dst_ref=land_ref.at[me],
                                         send_sem=send_sem, recv_sem=recv_sem, device_id=peer,
                                         device_id_type=MESH).start()
        token[...] = jnp.zeros_like(token)

    own = lax.dynamic_index_in_dim(src, _my_index(), 0, keepdims=False) if scatter else src
    src = pltpu.with_memory_space_constraint(src, pltpu.HBM)
    send_sem, recv_sem, land_thru, token = pl.pallas_call(
        body, name=name,
        out_shape=(pltpu.SemaphoreType.DMA(()), pltpu.SemaphoreType.DMA(()),
                   pltpu.HBM(land.shape, land.dtype), jax.ShapeDtypeStruct((8, 128), F32)),
        in_specs=(HBM_SPEC, HBM_SPEC, pl.BlockSpec(memory_space=pl.ANY)),
        out_specs=(SEM_SPEC, SEM_SPEC, HBM_SPEC, pl.BlockSpec(memory_space=pltpu.VMEM)),
        input_output_aliases={1: 2}, compiler_params=pltpu.CompilerParams(has_side_effects=DATAFLOW),
    )(src, pltpu.with_memory_space_constraint(land, pltpu.HBM), after)
    return send_sem, recv_sem, src, land_thru, token, own


def _push_wait(handle, after, name):
    send_sem, recv_sem, src, land_thru, _, own = handle
    after = tuple(after) if isinstance(after, (tuple, list)) else (after,)

    def body(src_ref, land_ref, send_sem, recv_sem, *rest):
        seven = land_ref.at[pl.ds(0, N_DEV - 1)]
        cp = pltpu.make_async_remote_copy(src_ref=seven, dst_ref=seven, send_sem=send_sem, recv_sem=recv_sem,
                                          device_id=_me(), device_id_type=MESH)
        cp.wait_send()
        cp.wait_recv()

    landed = pl.pallas_call(
        body, name=name, out_shape=pltpu.HBM(land_thru.shape, land_thru.dtype),
        in_specs=(HBM_SPEC, HBM_SPEC, SEM_SPEC, SEM_SPEC) + (pl.BlockSpec(memory_space=pl.ANY),) * len(after),
        out_specs=HBM_SPEC, input_output_aliases={1: 0},
        compiler_params=pltpu.CompilerParams(has_side_effects=DATAFLOW),
    )(src, land_thru, send_sem, recv_sem, *after)
    return lax.dynamic_update_index_in_dim(landed, own, _my_index(), 0)


def _adam(w, g, m, v):
    m2 = B1 * m + (1.0 - B1) * g
    v2 = B2 * v + (1.0 - B2) * jnp.square(g)
    m_hat = m2 / (1.0 - B1 ** STEP)
    v_hat = v2 / (1.0 - B2 ** STEP)
    delta = -LR * (m_hat / (jnp.sqrt(v_hat) + AEPS) + WD * w)
    return delta, m2, v2


def _small_update(gp, wp, mp, vp):
    def body(g_ref, w_ref, m_ref, v_ref, all_ref, go_ref, d_ref, mo_ref, vo_ref, send_sems, recv_sems):
        x, y, c = _me()
        me = 4 * x + 2 * y + c
        all_ref[me] = g_ref[...]
        sends = []
        for k in range(1, N_DEV):
            peer, _ = _peer(k)
            cp = pltpu.make_async_remote_copy(src_ref=g_ref, dst_ref=all_ref.at[me], send_sem=send_sems.at[k - 1],
                                              recv_sem=recv_sems.at[k - 1], device_id=peer, device_id_type=MESH)
            cp.start()
            sends.append(cp)
        for k in range(1, N_DEV):
            peer, pidx = _peer(k)
            pltpu.make_async_remote_copy(src_ref=g_ref, dst_ref=all_ref.at[pidx], send_sem=send_sems.at[k - 1],
                                         recv_sem=recv_sems.at[k - 1], device_id=peer, device_id_type=MESH).wait_recv()
        for cp in sends:
            cp.wait_send()
        g = all_ref[0]
        for d in range(1, N_DEV):
            g = g + all_ref[d]
        delta, m2, v2 = _adam(w_ref[...], g, m_ref[...], v_ref[...])
        go_ref[...] = g
        d_ref[...] = delta
        mo_ref[...] = m2
        vo_ref[...] = v2

    vm = pl.BlockSpec(memory_space=pltpu.VMEM)
    vs = jax.ShapeDtypeStruct(gp.shape, F32)
    return pl.pallas_call(
        body, out_shape=(jax.ShapeDtypeStruct((N_DEV,) + gp.shape, F32), vs, vs, vs, vs), in_specs=[vm] * 4,
        out_specs=(vm,) * 5,
        scratch_shapes=[pltpu.SemaphoreType.DMA((N_DEV - 1,)), pltpu.SemaphoreType.DMA((N_DEV - 1,))],
        name='small_update', compiler_params=_params(None, 48))(gp, wp, mp, vp)


def _small_sum_update(parts, wp, mp, vp):
    def body(p_ref, w_ref, m_ref, v_ref, go_ref, d_ref, mo_ref, vo_ref):
        g = p_ref[0]
        for d in range(1, N_DEV):
            g = g + p_ref[d]
        delta, m2, v2 = _adam(w_ref[...], g, m_ref[...], v_ref[...])
        go_ref[...] = g
        d_ref[...] = delta
        mo_ref[...] = m2
        vo_ref[...] = v2

    vm = pl.BlockSpec(memory_space=pltpu.VMEM)
    vs = jax.ShapeDtypeStruct(wp.shape, F32)
    return pl.pallas_call(body, out_shape=(vs, vs, vs, vs), in_specs=[vm] * 4, out_specs=(vm,) * 4,
                          name='small_sum_update', compiler_params=_params(None, 48))(parts, wp, mp, vp)


def _big_update(parts, w, m, v, name):
    _, R, C = parts.shape
    tr = R if R % 256 else (128 if C >= 2048 else 256)

    def body(p_ref, w_ref, m_ref, v_ref, g_ref, d_ref, mo_ref, vo_ref):
        g = p_ref[0].astype(F32)
        for d in range(1, N_DEV):
            g = g + p_ref[d].astype(F32)
        delta, m2, v2 = _adam(w_ref[...], g, m_ref[...], v_ref[...])
        g_ref[...] = g
        d_ref[...] = delta
        mo_ref[...] = m2
        vo_ref[...] = v2

    blk = pl.BlockSpec((tr, C), lambda i: (i, 0))
    sh = jax.ShapeDtypeStruct((R, C), F32)
    return pl.pallas_call(
        body, out_shape=(sh, sh, sh, sh), grid=(R // tr,),
        in_specs=[pl.BlockSpec((N_DEV, tr, C), lambda i: (0, i, 0)), blk, blk, blk], out_specs=(blk,) * 4,
        name=name, compiler_params=_params(('parallel',), 48))(parts, w, m, v)


def _ada_update(c_all, dmod_cols, w, m, v):
    C = w.shape[1]
    tr = 256

    def body(c_ref, dm_ref, w_ref, m_ref, v_ref, g_ref, d_ref, mo_ref, vo_ref):
        cv = c_ref[...]
        s = cv * jax.nn.sigmoid(cv)
        g = lax.dot_general(s, dm_ref[...], TN_DIMS, preferred_element_type=F32, precision=HI)
        delta, m2, v2 = _adam(w_ref[...], g, m_ref[...], v_ref[...])
        g_ref[...] = g
        d_ref[...] = delta
        mo_ref[...] = m2
        vo_ref[...] = v2

    blk = pl.BlockSpec((tr, C), lambda i: (i, 0))
    sh = jax.ShapeDtypeStruct((D, C), F32)
    return pl.pallas_call(
        body, out_shape=(sh, sh, sh, sh), grid=(D // tr,),
        in_specs=[pl.BlockSpec((N_DEV, tr), lambda i: (0, i)), pl.BlockSpec((N_DEV, C), lambda i: (0, 0)), blk, blk, blk],
        out_specs=(blk,) * 4, name='ada_update', compiler_params=_params(('parallel',), 48))(c_all, dmod_cols, w, m, v)


def _to_sub(t, d):
    if d == 1:
        return t
    return t.reshape(L // d, d, t.shape[-1]).transpose(1, 0, 2).reshape(L, t.shape[-1])


def _from_sub(t, d):
    if d == 1:
        return t
    return t.reshape(d, L // d, t.shape[-1]).transpose(1, 0, 2).reshape(L, t.shape[-1])


def _rows_to_cluster_lanes(t):
    k = t.shape[1]
    return t.reshape(N_CL, CL_S, k).transpose(0, 2, 1)


def _blockdiag_in(t):
    t = t.reshape(N_CL, CL_G, SSM_N, SSM_P).transpose(0, 1, 3, 2)
    eye = jnp.eye(CL_G, dtype=t.dtype)
    t = t[:, :, :, None, :] * eye[None, :, None, :, None]
    return t.reshape(N_CL, CL_U, CL_S)


def _blockdiag_extract(t):
    t = t.reshape(N_CL, CL_G, SSM_P, CL_G, SSM_N)
    t = jnp.stack([t[:, i, :, i, :] for i in range(CL_G)], axis=1)
    return t.transpose(0, 1, 3, 2).reshape(SSM_GN, SSM_P)


def _c_to_rows(t):
    return t.transpose(0, 2, 1).reshape(SSM_GN, SSM_P)


def _rows_to_c(t):
    return t.reshape(SSM_G, SSM_N, SSM_P).transpose(0, 2, 1)


def _ssm_prep(sp):
    rows = lambda n: sp[n].reshape(SSM_GN, 1)
    a_re, a_im = rows('ssm_a_re'), rows('ssm_a_im')
    ldt = jnp.repeat(sp['ssm_log_dt'].reshape(SSM_G, 1), SSM_N, axis=0)
    b_re, b_im = sp['ssm_b_re'].reshape(SSM_GN, SSM_P), sp['ssm_b_im'].reshape(SSM_GN, SSM_P)
    c_re, c_im = _c_to_rows(sp['ssm_c_re'].reshape(SSM_G, SSM_P, SSM_N)), _c_to_rows(sp['ssm_c_im'].reshape(SSM_G, SSM_P, SSM_N))
    bbr, bbi, pwr, pwi = _ssm_pre(a_re, a_im, ldt, b_re, b_im)
    bm = jnp.concatenate([_blockdiag_in(bbr), _blockdiag_in(bbi)], axis=2)
    cmt = jnp.concatenate([_blockdiag_in(c_re), -_blockdiag_in(c_im)], axis=2)
    bmt, cm = bm.transpose(0, 2, 1), cmt.transpose(0, 2, 1)
    pw = jnp.concatenate([_rows_to_cluster_lanes(pwr), _rows_to_cluster_lanes(pwi)], axis=2)
    bm2, cm2 = _hilo(bm), _hilo(cm)
    bmt_b, cmt_b = bmt.astype(BF16), cmt.astype(BF16)
    return a_re, a_im, ldt, b_re, b_im, bm2, cm2, bmt_b, cmt_b, pw


def _tied(v, deps):
    for t in deps:
        v = v + t[0, 0]
    return v


def _local_step(x, pos, mod, tgt, sp, prep, get_w, emit, emit_small, emit_late, first_deps=()):
    sh1, sc1, gt1, sh2, sc2, gt2 = (mod[i:i + 1] for i in range(NMOD))
    vec = lambda n: sp[n].reshape(1, -1)
    a_re, a_im, ldt, b_re, b_im, bm2, cm2, bmt_b, cmt_b, pw = prep
    dvec = vec('ssm_d')

    h1 = _prenorm_fwd(x, vec('g_pre_mix'), sc1, sh1)
    w_in = get_w('w_in', (h1, bm2, cm2, pw, bmt_b, cmt_b))
    proj = _mm(h1, w_in, mode='nn', name='mm_in', tn=1408, deps=first_deps)
    fr1 =ROPE_THETA ** (-jnp.arange(0, ROT_DIM, 2, dtype=F32) / ROT_DIM)
    lane = jnp.arange(128) % HEAD_DIM
    fr = jnp.where(lane < ROT_DIM, fr1[lane % (ROT_DIM // 2)], 0.0).reshape(1, 128).astype(F32)
    qk = _rope(proj, pos, fr, 1.0, 'rope_fwd', carry=KVW)
    v_b = qk[:, ROPE_W:]
    u_off = (ROPE_W + KVW) // CL_U
    qp = jnp.stack([_to_sub(qk[:, gi * KVW:(gi + 1) * KVW], d) for gi, d in enumerate(DILATIONS)])
    kp = jnp.stack([_to_sub(qk[:, QW:ROPE_W], d) for d in DILATIONS])
    vp = jnp.stack([_to_sub(v_b, d) for d in DILATIONS])
    o_p, lse_p = _attn_fwd(qp, kp, vp)
    o3 = jnp.stack([_from_sub(o_p[gi], d) for gi, d in enumerate(DILATIONS)])
    lse3 = jnp.stack([_from_sub(lse_p[gi], d) for gi, d in enumerate(DILATIONS)])
    att = _combine_fwd(o3, lse3, vec('g_attn_out'))

    ypre, bnd, states = _ssm_fwd(proj, bm2, cm2, pw, dvec, u_off)
    w_glu = get_w('w_glu', ypre)
    ssm_n = _glu_fwd(ypre, w_glu, vec('b_glu'), vec('g_ssm_out'))

    cat = jnp.concatenate([att, ssm_n], axis=1)
    w_out = get_w('w_out', cat)
    mix = _mm(cat, w_out, mode='nn', name='mm_out', tk=1280)
    x1, h2 = _postmix_fwd(x, mix, vec('g_post_mix'), gt1, vec('g_pre_mlp'), sc2, sh2)
    w_mi = get_w('w_mlp_in', h2)
    a_pre, r_act = _mm(h2, w_mi, mode='nn', name='mm_mlp_in', epilogue='relu2', b_sharded=True)
    w_mo = get_w('w_mlp_out', a_pre)
    y = _mm(r_act, w_mo, mode='nn', name='mm_mlp_out')
    dx2, dy, loss, dgt2, dg_post_mlp = _final_fwd_bwd(x1, y, tgt, vec('g_post_mlp'), gt2)
    dgt2, dg_post_mlp = dgt2[:1], dg_post_mlp[:1]

    da = _mm(dy, w_mo, mode='nt', name='mm_d_act', out_dtype=BF16, epilogue='drelu2', extra=a_pre)
    dep = emit('w_mlp_out', _mm(r_act, dy, mode='tn', name='mm_dw_mlp_out', out_dtype=BF16))
    dh2 = _mm(da, w_mi, mode='nt', name='mm_dh2', tk=1024, b_sharded=True, deps=dep)
    dep = emit('w_mlp_in', _mm(h2, da, mode='tn', name='mm_dw_mlp_in', out_dtype=BF16, out_sharded=True))
    dx1, dmix, dsc2, dsh2, dg_pre_mlp, dgt1, dg_post_mix = _postmix_bwd(
        dx2, dh2, x1, mix, vec('g_post_mix'), gt1, vec('g_pre_mlp'), sc2)
    dsc2, dsh2, dg_pre_mlp, dgt1, dg_post_mix = (t[:1] for t in (dsc2, dsh2, dg_pre_mlp, dgt1, dg_post_mix))
    dcat = _mm(dmix, w_out, mode='nt', name='mm_dcat', tn=1280, deps=dep)
    dep = emit('w_out', _mm(cat, dmix, mode='tn', name='mm_dw_out', out_dtype=BF16, tm=640))
    datt, dsn = dcat[:, :KVW], dcat[:, KVW:]

    dypre, g_w_glu, g_b_glu, g_g_ssm = _glu_bwd(ypre, dsn, w_glu, _tied(vec('b_glu'), dep), vec('g_ssm_out'))
    g_b_glu, g_g_ssm = g_b_glu[:1], g_g_ssm[:1]
    dep = dep + emit('w_glu', g_w_glu.astype(BF16))
    du, dbm, dcm, dA, dD = _ssm_bwd(proj, dypre, states, bmt_b, cmt_b, pw, dvec, bnd, u_off)
    dD = dD[:1]
    gbr, gbi = _blockdiag_extract(dbm[:, :, :CL_S]), _blockdiag_extract(dbm[:, :, CL_S:])
    dcmt = dcm.transpose(0, 2, 1)
    g_c_re = _rows_to_c(_blockdiag_extract(dcmt[:, :, :CL_S]))
    g_c_im = _rows_to_c(-_blockdiag_extract(dcmt[:, :, CL_S:]))
    gar = dA[:, 0, :CL_S].reshape(SSM_GN, 1)
    gai = dA[:, 0, CL_S:].reshape(SSM_GN, 1)
    sel = (jnp.arange(SSM_GN)[None, :] // SSM_N == jnp.arange(SSM_G)[:, None]).astype(F32)
    g_a_re, g_a_im, g_b_re, g_b_im, g_ldt = _ssm_post(a_re, a_im, ldt, b_re, b_im, gar, gai, gbr, gbi, sel)

    head_ones = (jnp.arange(KVW)[:, None] // HEAD_DIM == jnp.arange(KVW)[None, :] // HEAD_DIM).astype(F32)
    do3, dlse3, g_g_attn = _combine_bwd(datt, o3, lse3, vec('g_attn_out'), head_ones)
    g_g_attn = g_g_attn[:1]
    dep_small = emit_small({
        'g_post_mix': dg_post_mix, 'ssm_a_re': g_a_re, 'ssm_a_im': g_a_im, 'ssm_log_dt': g_ldt[:, 0],
        'ssm_b_re': g_b_re, 'ssm_b_im': g_b_im, 'ssm_c_re': g_c_re, 'ssm_c_im': g_c_im, 'ssm_d': dD, 'b_glu': g_b_glu,
        'g_attn_out': g_g_attn, 'g_ssm_out': g_g_ssm, 'g_pre_mlp': dg_pre_mlp, 'g_post_mlp': dg_post_mlp})
    do_p = jnp.stack([_to_sub(do3[gi], d) for gi, d in enumerate(DILATIONS)])
    dlse_p = jnp.stack([_to_sub(dlse3[gi], d) for gi, d in enumerate(DILATIONS)])
    dq_p, dk_p, dv_p = _attn_bwd(qp, kp, vp, o_p, lse_p, do_p, dlse_p)
    dq = jnp.concatenate([_from_sub(dq_p[gi], d) for gi, d in enumerate(DILATIONS)], axis=1)
    dk = sum(_from_sub(dk_p[gi], d) for gi, d in enumerate(DILATIONS))
    dv = sum(_from_sub(dv_p[gi], d) for gi, d in enumerate(DILATIONS))
    dqk = _rope(jnp.concatenate([dq, dk], axis=1), pos, _tied(fr, dep_small), -1.0, 'rope_bwd')
    dproj = jnp.concatenate([dqk, dv.astype(BF16), du.astype(BF16)], axis=1)
    dh1 = _mm(dproj, w_in, mode='nt', name='mm_dh1', tk=1408, deps=dep)
    grad_x, dsc1, dsh1, dg_pre_mix = _prenorm_bwd(dx1, dh1, x, vec('g_pre_mix'), sc1)
    dsc1, dsh1, dg_pre_mix = dsc1[:1], dsh1[:1], dg_pre_mix[:1]
    dmod = jnp.concatenate([dsh1, dsc1, dgt1, dsh2, dsc2, dgt2], axis=0)
    dep = emit_late({'b_ada': dmod, 'g_pre_mix': dg_pre_mix})
    emit('w_in', _mm(h1, dproj, mode='tn', name='mm_dw_in', out_dtype=BF16, tn=1408, deps=dep))
    return loss[0, 0], grad_x


def _pack(d, names):
    flat = jnp.concatenate([jnp.pad(d[n].reshape(-1).astype(F32), (0, SEG[n] - SMALL_SIZES[n])) for n in names])
    return flat.reshape(-1, 128)


def _unpack(packed, names, shapes):
    out, off = {}, 0
    for n in names:
        out[n] = packed[off // 128:(off + SEG[n]) // 128].reshape(-1)[:SMALL_SIZES[n]].reshape(shapes[n])
        off += SEG[n]
    return out


def _shard_major(t, name):
    if name in ('w_in', 'w_out', 'w_mlp_in'):
        k, n = t.shape
        return t.reshape(k, N_DEV, n // N_DEV).transpose(1, 0, 2)
    k, n = t.shape
    return t.reshape(N_DEV, k // N_DEV, n)


def _from_shard_major(t, name):
    if name in ('w_in', 'w_out', 'w_mlp_in'):
        _, k, n = t.shape
        return t.transpose(1, 0, 2).reshape(k, N_DEV * n)
    _, k, n = t.shape
    return t.reshape(N_DEV * k, n)


def kernel(x, c, positions, w_ada, b_ada, g_pre_mix, g_post_mix, w_in, ssm_a_re, ssm_a_im, ssm_log_dt, ssm_b_re, ssm_b_im, ssm_c_re, ssm_c_im, ssm_d, w_glu, b_glu, g_attn_out, g_ssm_out, w_out, g_pre_mlp, g_post_mlp, w_mlp_in, w_mlp_out, loss_target, m_w_ada, m_b_ada, m_g_pre_mix, m_g_post_mix, m_w_in, m_ssm_a_re, m_ssm_a_im, m_ssm_log_dt, m_ssm_b_re, m_ssm_b_im, m_ssm_c_re, m_ssm_c_im, m_ssm_d, m_w_glu, m_b_glu, m_g_attn_out, m_g_ssm_out, m_w_out, m_g_pre_mlp, m_g_post_mlp, m_w_mlp_in, m_w_mlp_out, v_w_ada, v_b_ada, v_g_pre_mix, v_g_post_mix, v_w_in, v_ssm_a_re, v_ssm_a_im, v_ssm_log_dt, v_ssm_b_re, v_ssm_b_im, v_ssm_c_re, v_ssm_c_im, v_ssm_d, v_w_glu, v_b_glu, v_g_attn_out, v_g_ssm_out, v_w_out, v_g_pre_mlp, v_g_post_mlp, v_w_mlp_in, v_w_mlp_out):
    loc = dict(locals())
    W = {n: loc[n] for n in WEIGHTS}
    M = {n: loc['m_' + n] for n in WEIGHTS}
    V = {n: loc['v_' + n] for n in WEIGHTS}
    assert x.shape == (1, L, D) and w_in.shape == (1, D, INW // N_DEV), (x.shape, w_in.shape)

    cw = NMOD * D // N_DEV
    c_all, mod8 = _mod_exchange(c.reshape(1, 1, D), w_ada[0], b_ada.reshape(N_DEV, 1, cw))
    mod = mod8.reshape(NMOD, D)

    gather = {n: _push_start(W[n][0].astype(BF16), False, mod8, 'gather_start_' + n) for n in BIG}
    tokens = tuple(gather[n][4] for n in BIG)
    mod = _tied(mod, tokens)
    sp = {n: W[n][0] for n in SMALL}
    prep = _ssm_prep({**sp, 'ssm_a_re': _tied(sp['ssm_a_re'], tokens)})

    def get_w(n, after):
        g = _push_wait(gather[n], after, 'gather_wait_' + n)
        return g if n == 'w_mlp_in' else _from_shard_major(g, n)

    scatter = {}

    def emit(n, g):
        src = g if n == 'w_mlp_in' else _shard_major(g, n)
        scatter[n] = _push_start(src, True, src, 'scatter_start_' + n)
        return (scatter[n][4],)

    small_early = []

    def emit_small(d):
        pack = _pack(d, SMALL_EARLY)
        small_early.append(_push_start(pack, False, pack, 'small_start'))
        return (small_early[0][4],)

    out_g, out_d, out_m, out_v = {}, {}, {}, {}
    shapes = {n: W[n].shape[1:] for n in SMALL}

    def put(names, packs):
        for dst, packed in zip((out_g, out_d, out_m, out_v), packs):
            dst.update(_unpack(packed, names, shapes))

    late = []

    def emit_late(d):
        rows_all, *packs = _small_update(*[_pack(t, SMALL_LATE) for t in (d, W, M, V)])
        put(SMALL_LATE, packs)
        late.append(rows_all)
        return (rows_all,)

    loss, grad_x = _local_step(x[0], positions.reshape(L, 1), mod, loss_target[0], sp, prep, get_w, emit, emit_small,
                               emit_late)
    loss = lax.psum(loss, ('x', 'y', 'c'))

    me = 4 * lax.axis_index('x') + 2 * lax.axis_index('y') + lax.axis_index('c')
    dmod_all = late[0][:, :NMOD * D // 128].reshape(N_DEV, NMOD * D)
    dmod_cols = _tied(lax.dynamic_slice_in_dim(dmod_all, me * cw, cw, axis=1), (scatter['w_in'][4],))
    out_g['w_ada'], out_d['w_ada'], out_m['w_ada'], out_v['w_ada'] = _ada_update(
        c_all.reshape(N_DEV, D), dmod_cols, w_ada[0], m_w_ada[0], v_w_ada[0])

    parts = _push_wait(small_early[0], out_v['w_ada'], 'small_wait')
    packs = _small_sum_update(parts, *[_pack(d, SMALL_EARLY) for d in (W, M, V)])
    put(SMALL_EARLY, packs)

    after = packs[3]
    for n in ('w_mlp_out', 'w_mlp_in', 'w_out', 'w_glu', 'w_in'):
        parts = _push_wait(scatter[n], after, 'scatter_wait_' + n)
        out_g[n], out_d[n], out_m[n], out_v[n] = _big_update(parts, W[n][0], M[n][0], V[n][0], 'update_' + n)
        after = out_v[n]

    lead = lambda t: t[None]
    return (loss, grad_x[None], *[lead(out_g[n]) for n in WEIGHTS], *[lead(out_d[n]) for n in WEIGHTS],
            *[lead(out_m[n]) for n in WEIGHTS], *[lead(out_v[n]) for n in WEIGHTS])
```

```python
import functools
import math

import jax
import jax.numpy as jnp
from jax import lax
from jax.experimental import pallas as pl
from jax.experimental.pallas import tpu as pltpu

F32 = jnp.float32
BF16 = jnp.bfloat16
HI = lax.Precision.HIGHEST
MESH = pl.DeviceIdType.MESH

N_DEV = 8
L = 4096
D = 2048
HEAD_DIM = 64
N_GROUPS = 3
DILATIONS = (1, 4, 16)
HEADS = 6
QW = N_GROUPS * HEADS * HEAD_DIM
KVW = HEADS * HEAD_DIM
ROT_DIM = 16
ROPE_THETA = 500000.0
BLK = 128
NBLK = L // BLK
SSMW = D - QW
SSM_P = 16
SSM_G = SSMW // SSM_P
SSM_N = 64
SSM_GN = SSM_G * SSM_N
CL_G = 8
N_CL = SSM_G // CL_G
CL_U = CL_G * SSM_P
CL_S = CL_G * SSM_N
INW = QW + 2 * KVW + SSMW
OUTW = KVW + SSMW
DFF = 4 * D
NMOD = 6
EPS = 1e-6
LR, B1, B2, AEPS, WD, STEP = 0.001, 0.9, 0.999, 1e-08, 0.01, 10

T_SCAN = 512
MB = 2 ** 20

WEIGHTS = ['w_ada', 'b_ada', 'g_pre_mix', 'g_post_mix', 'w_in', 'ssm_a_re', 'ssm_a_im', 'ssm_log_dt',
           'ssm_b_re', 'ssm_b_im', 'ssm_c_re', 'ssm_c_im', 'ssm_d', 'w_glu', 'b_glu', 'g_attn_out',
           'g_ssm_out', 'w_out', 'g_pre_mlp', 'g_post_mlp', 'w_mlp_in', 'w_mlp_out']
BIG = ['w_in', 'w_glu', 'w_out', 'w_mlp_in', 'w_mlp_out']
SMALL = [n for n in WEIGHTS if n not in BIG and n != 'w_ada']
SMALL_SIZES = {'b_ada': NMOD * D, 'g_pre_mix': D, 'g_post_mix': D, 'ssm_a_re': SSM_GN, 'ssm_a_im': SSM_GN,
               'ssm_log_dt': SSM_G, 'ssm_b_re': SSM_GN * SSM_P, 'ssm_b_im': SSM_GN * SSM_P,
               'ssm_c_re': SSM_GN * SSM_P, 'ssm_c_im': SSM_GN * SSM_P, 'ssm_d': SSMW, 'b_glu': SSMW,
               'g_attn_out': KVW, 'g_ssm_out': SSMW, 'g_pre_mlp': D, 'g_post_mlp': D}
SEG = {n: -(-SMALL_SIZES[n] // 1024) * 1024 for n in SMALL}
SMALL_LATE = ['b_ada', 'g_pre_mix']
SMALL_EARLY = [n for n in SMALL if n not in SMALL_LATE]


def _params(sem=None, vmem_mb=None):
    kw = {}
    if sem is not None:
        kw['dimension_semantics'] = sem
    if vmem_mb is not None:
        kw['vmem_limit_bytes'] = vmem_mb * MB
    return pltpu.CompilerParams(**kw)


def _vec(n):
    return pl.BlockSpec((1, n), lambda *_: (0, 0))


def _rms(x):
    return lax.rsqrt(jnp.mean(x * x, axis=-1, keepdims=True) + EPS)


def _rms_bwd(dn, n, r):
    return r * (dn - n * jnp.mean(dn * n, axis=-1, keepdims=True))


def _vec8(n):
    return pl.BlockSpec((8, n), lambda *_: (0, 0))


def _colsum(x):
    return jnp.sum(x.reshape(-1, 8, x.shape[-1]), axis=0)


def _fold8(last, *refs):
    @pl.when(last)
    def _():
        for r in refs:
            r[...] = jnp.broadcast_to(jnp.sum(r[...], axis=0, keepdims=True), r.shape)


def _mm(a, b, *, mode, name, out_dtype=F32, tm=1024, tn=1024, tk=2048, epilogue=None, extra=None,
        b_sharded=False, out_sharded=False, deps=()):
    if mode == 'nn':
        M, K = a.shape
        dims = (((1,), (0,)), ((), ()))
        a_spec = pl.BlockSpec((tm, tk), lambda i, j, k: (i, k))
        if b_sharded:
            _, K2, per = b.shape
            N, q = N_DEV * per, per // tn
            b_spec = pl.BlockSpec((None, tk, tn), lambda i, j, k: (j // q, k, j % q))
        else:
            K2, N = b.shape
            b_spec = pl.BlockSpec((tk, tn), lambda i, j, k: (k, j))
    elif mode == 'nt':
        M, K = a.shape
        dims = (((1,), (1,)), ((), ()))
        a_spec = pl.BlockSpec((tm, tk), lambda i, j, k: (i, k))
        if b_sharded:
            _, N, per = b.shape
            K2, q = N_DEV * per, per // tk
            b_spec = pl.BlockSpec((None, tn, tk), lambda i, j, k: (k // q, j, k % q))
        else:
            N, K2 = b.shape
            b_spec = pl.BlockSpec((tn, tk), lambda i, j, k: (j, k))
    else:
        (K, M), (K2, N) = a.shape, b.shape
        dims = (((0,), (0,)), ((), ()))
        a_spec = pl.BlockSpec((tk, tm), lambda i, j, k: (k, i))
        b_spec = pl.BlockSpec((tk, tn), lambda i, j, k: (k, j))
    assert K == K2 and M % tm == 0 and N % tn == 0 and K % tk == 0, (name, a.shape, b.shape, tm, tn, tk)
    nk = K // tk
    o_spec = pl.BlockSpec((tm, tn), lambda i, j, k: (i, j))
    o_dims = (M, N)
    if out_sharded:
        qo = N // N_DEV // tn
        o_spec = pl.BlockSpec((None, tm, tn), lambda i, j, k: (j // qo, i, j % qo))
        o_dims = (N_DEV, M, N // N_DEV)
    n_out = 2 if epilogue == 'relu2' else 1
    n_extra = 1 if extra is not None else 0
    n_in = 2 + n_extra + len(deps)

    def body(*refs):
        a_ref, b_ref = refs[0], refs[1]
        x_refs = refs[2:2 + n_extra]
        o_refs = refs[n_in:n_in + n_out]
        acc = refs[-1]
        k = pl.program_id(2)

        prod = lax.dot_general(a_ref[...], b_ref[...], dims, preferred_element_type=F32)

        def finish(r):
            if epilogue == 'relu2':
                o_refs[0][...] = r.astype(BF16)
                o_refs[1][...] = jnp.square(jnp.maximum(r, 0.0)).astype(BF16)
            elif epilogue == 'drelu2':
                pre = x_refs[0][...].astype(F32)
                o_refs[0][...] = (r * (2.0 * jnp.maximum(pre, 0.0))).astype(out_dtype)
            else:
                o_refs[0][...] = r.astype(out_dtype)

        if nk == 1:
            finish(prod)
        else:
            @pl.when(k == 0)
            def _():
                acc[...] = prod

            @pl.when((k > 0) & (k < nk - 1))
            def _():
                acc[...] += prod

            @pl.when(k == nk - 1)
            def _():
                finish(acc[...] + prod)

    if epilogue == 'relu2':
        out_shape = (jax.ShapeDtypeStruct((M, N), BF16), jax.ShapeDtypeStruct((M, N), BF16))
        out_specs = (o_spec, o_spec)
    else:
        out_shape = jax.ShapeDtypeStruct(o_dims, out_dtype)
        out_specs = o_spec
    args = (a, b) + ((extra,) if extra is not None else ()) + tuple(deps)
    in_specs = ([a_spec, b_spec] + ([o_spec] if extra is not None else [])
                + [pl.BlockSpec(memory_space=pl.ANY)] * len(deps))
    return pl.pallas_call(
        body, out_shape=out_shape, grid=(M // tm, N // tn, nk), in_specs=in_specs, out_specs=out_specs,
        scratch_shapes=[pltpu.VMEM((tm, tn) if nk > 1 else (8, 128), F32)], name=name,
        compiler_params=_params(('parallel', 'parallel', 'arbitrary'), 56))(*args)


TR = 256


def _rowspec(w=D):
    return pl.BlockSpec((TR, w), lambda i: (i, 0))


def _prenorm_fwd(x, g, sc, sh):
    def body(x_ref, g_ref, sc_ref, sh_ref, h_ref):
        xv = x_ref[...]
        n = xv * _rms(xv)
        h_ref[...] = ((n * g_ref[...]) * (1.0 + sc_ref[...]) + sh_ref[...]).astype(BF16)

    return pl.pallas_call(
        body, out_shape=jax.ShapeDtypeStruct((L, D), BF16), grid=(L // TR,),
        in_specs=[_rowspec(), _vec(D), _vec(D), _vec(D)], out_specs=_rowspec(), name='prenorm_fwd',
        compiler_params=_params(('parallel',), 40))(x, g, sc, sh)


def _postmix_fwd(x, mix, gpm, gt1, gpl, sc2, sh2):
    def body(x_ref, mix_ref, gpm_ref, gt1_ref, gpl_ref, sc2_ref, sh2_ref, x1_ref, h2_ref):
        mix_v = mix_ref[...]
        nm = mix_v * _rms(mix_v)
        x1 = x_ref[...] + gt1_ref[...] * (nm * gpm_ref[...])
        x1_ref[...] = x1
        n2 = x1 * _rms(x1)
        h2_ref[...] = ((n2 * gpl_ref[...]) * (1.0 + sc2_ref[...]) + sh2_ref[...]).astype(BF16)

    return pl.pallas_call(
        body, out_shape=(jax.ShapeDtypeStruct((L, D), F32), jax.ShapeDtypeStruct((L, D), BF16)), grid=(L // TR,),
        in_specs=[_rowspec(), _rowspec()] + [_vec(D)] * 5, out_specs=(_rowspec(), _rowspec()), name='postmix_fwd',
        compiler_params=_params(('parallel',), 40))(x, mix, gpm, gt1, gpl, sc2, sh2)


def _final_fwd_bwd(x1, y, tgt, g, gt2):
    def body(x1_ref, y_ref, t_ref, g_ref, gt2_ref, dx2_ref, dy_ref, loss_ref, dgt2_ref, dg_ref):
        @pl.when(pl.program_id(0) == 0)
        def _():
            loss_ref[...] = jnp.zeros_like(loss_ref)
            dgt2_ref[...] = jnp.zeros_like(dgt2_ref)
            dg_ref[...] = jnp.zeros_like(dg_ref)

        yv = y_ref[...]
        r = _rms(yv)
        n = yv * r
        ng = n * g_ref[...]
        x2 = x1_ref[...] + gt2_ref[...] * ng
        e = x2 - t_ref[...]
        loss_ref[...] += 0.5 * jnp.sum(jnp.mean(e * e, axis=-1, keepdims=True), axis=0, keepdims=True)
        dx2 = e * (1.0 / D)
        dx2_ref[...] = dx2
        dgt2_ref[...] += _colsum(dx2 * ng)
        dng = dx2 * gt2_ref[...]
        dg_ref[...] += _colsum(dng * n)
        dy_ref[...] = _rms_bwd(dng * g_ref[...], n, r).astype(BF16)
        _fold8(pl.program_id(0) == L // TR - 1, dgt2_ref, dg_ref)

    return pl.pallas_call(
        body,
        out_shape=(jax.ShapeDtypeStruct((L, D), F32), jax.ShapeDtypeStruct((L, D), BF16),
                   jax.ShapeDtypeStruct((8, 128), F32), jax.ShapeDtypeStruct((8, D), F32),
                   jax.ShapeDtypeStruct((8, D), F32)),
        grid=(L // TR,), in_specs=[_rowspec(), _rowspec(), _rowspec(), _vec(D), _vec(D)],
        out_specs=(_rowspec(), _rowspec(), _vec8(128), _vec8(D), _vec8(D)), name='final_fwd_bwd',
        compiler_params=_params(('arbitrary',), 40))(x1, y, tgt, g, gt2)


def _postmix_bwd(dx2, dh2, x1, mix, gpm, gt1, gpl, sc2):
    def body(dx2_ref, dh2_ref, x1_ref, mix_ref, gpm_ref, gt1_ref, gpl_ref, sc2_ref,
             dx1_ref, dmix_ref, dsc2_ref, dsh2_ref, dgpl_ref, dgt1_ref, dgpm_ref):
        @pl.when(pl.program_id(0) == 0)
        def _():
            for r_ in (dsc2_ref, dsh2_ref, dgpl_ref, dgt1_ref, dgpm_ref):
                r_[...] = jnp.zeros_like(r_)

        x1v = x1_ref[...]
        r2 = _rms(x1v)
        n2 = x1v * r2
        dh2v = dh2_ref[...]
        dsh2_ref[...] += _colsum(dh2v)
        dsc2_ref[...] += _colsum(dh2v * (n2 * gpl_ref[...]))
        t = dh2v * (1.0 + sc2_ref[...])
        dgpl_ref[...] += _colsum(t * n2)
        dx1 = dx2_ref[...] + _rms_bwd(t * gpl_ref[...], n2, r2)
        dx1_ref[...] = dx1
        mix_v = mix_ref[...]
        rm = _rms(mix_v)
        nm = mix_v * rm
        dgt1_ref[...] += _colsum(dx1 * (nm * gpm_ref[...]))
        u = dx1 * gt1_ref[...]
        dgpm_ref[...] += _colsum(u * nm)
        dmix_ref[...] = _rms_bwd(u * gpm_ref[...], nm, rm).astype(BF16)
        _fold8(pl.program_id(0) == L // TR - 1, dsc2_ref, dsh2_ref, dgpl_ref, dgt1_ref, dgpm_ref)

    vs = jax.ShapeDtypeStruct((8, D), F32)
    return pl.pallas_call(
        body, out_shape=(jax.ShapeDtypeStruct((L, D), F32), jax.ShapeDtypeStruct((L, D), BF16), vs, vs, vs, vs, vs),
        grid=(L // TR,), in_specs=[_rowspec()] * 4 + [_vec(D)] * 4,
        out_specs=(_rowspec(), _rowspec()) + (_vec8(D),) * 5, name='postmix_bwd',
        compiler_params=_params(('arbitrary',), 48))(dx2, dh2, x1, mix, gpm, gt1, gpl, sc2)


def _prenorm_bwd(dx1, dh1, x, g, sc1):
    def body(dx1_ref, dh1_ref, x_ref, g_ref, sc1_ref, dx_ref, dsc1_ref, dsh1_ref, dg_ref):
        @pl.when(pl.program_id(0) == 0)
        def _():
            for r_ in (dsc1_ref, dsh1_ref, dg_ref):
                r_[...] = jnp.zeros_like(r_)

        xv = x_ref[...]
        r = _rms(xv)
        n = xv * r
        dh = dh1_ref[...]
        dsh1_ref[...] += _colsum(dh)
        dsc1_ref[...] += _colsum(dh * (n * g_ref[...]))
        t = dh * (1.0 + sc1_ref[...])
        dg_ref[...] += _colsum(t * n)
        dx_ref[...] = dx1_ref[...] + _rms_bwd(t * g_ref[...], n, r)
        _fold8(pl.program_id(0) == L // TR - 1, dsc1_ref, dsh1_ref, dg_ref)

    vs = jax.ShapeDtypeStruct((8, D), F32)
    return pl.pallas_call(
        body, out_shape=(jax.ShapeDtypeStruct((L, D), F32), vs, vs, vs), grid=(L // TR,),
        in_specs=[_rowspec()] * 3 + [_vec(D)] * 2, out_specs=(_rowspec(),) + (_vec8(D),) * 3, name='prenorm_bwd',
        compiler_params=_params(('arbitrary',), 40))(dx1, dh1, x, g, sc1)


ROPE_W = QW + KVW


def _rope(xin, pos, fr, sign, name, carry=0):
    def body(x_ref, pos_ref, fr_ref, o_ref):
        xv = x_ref[:, 0:ROPE_W]
        ang = pos_ref[...].astype(F32) * fr_ref[...]
        w = lax.broadcasted_iota(jnp.int32, (1, 128), 1) % HEAD_DIM
        cs = jnp.cos(ang)
        sn = jnp.sin(ang) * sign
        s1 = jnp.where(w < ROT_DIM // 2, -sn, 0.0)
        s2 = jnp.where((w >= ROT_DIM // 2) & (w < ROT_DIM), sn, 0.0)
        rep = ROPE_W // 128
        cs, s1, s2 = jnp.tile(cs, (1, rep)), jnp.tile(s1, (1, rep)), jnp.tile(s2, (1, rep))
        hi = pltpu.roll(xv, ROPE_W - ROT_DIM // 2, 1)
        lo = pltpu.roll(xv, ROT_DIM // 2, 1)
        o_ref[:, 0:ROPE_W] = (xv * cs + hi * s1 + lo * s2).astype(BF16)
        if carry:
            o_ref[:, ROPE_W:ROPE_W + carry] = x_ref[:, ROPE_W:ROPE_W + carry].astype(BF16)

    return pl.pallas_call(
        body, out_shape=jax.ShapeDtypeStruct((L, ROPE_W + carry), BF16), grid=(L // TR,),
        in_specs=[_rowspec(ROPE_W + carry), pl.BlockSpec((TR, 1), lambda i: (i, 0)), _vec(128)],
        out_specs=_rowspec(ROPE_W + carry), name=name, compiler_params=_params(('parallel',), 40))(xin, pos, fr)


def _attn_mask(g, b):
    nbs = lax.shift_right_logical(jnp.int32(NBLK), 2 * g)
    first = (b & (nbs - 1)) == 0
    qi = lax.broadcasted_iota(jnp.int32, (BLK, 2 * BLK), 0)
    kj = lax.broadcasted_iota(jnp.int32, (BLK, 2 * BLK), 1)
    dist = qi + BLK - kj
    return (dist >= 0) & (dist <= BLK) & ((kj >= BLK) | jnp.logical_not(first))


def _blk(idx):
    return pl.BlockSpec((None, BLK, KVW), idx)


_CUR = lambda g, b: (g, b, 0)
_PREV = lambda g, b: (g, jnp.maximum(b - 1, 0), 0)
NEG = -1e30
NT_DIMS = (((1,), (1,)), ((), ()))
TN_DIMS = (((0,), (0,)), ((), ()))


def _attn_fwd(qp, kp, vp):
    def body(q_ref, kp_ref, kc_ref, vp_ref, vc_ref, o_ref, lse_ref):
        valid = _attn_mask(pl.program_id(0), pl.program_id(1))
        for h in range(HEADS):
            hs = slice(h * HEAD_DIM, (h + 1) * HEAD_DIM)
            q = q_ref[:, hs]
            kc = jnp.concatenate([kp_ref[:, hs], kc_ref[:, hs]], axis=0)
            vc = jnp.concatenate([vp_ref[:, hs], vc_ref[:, hs]], axis=0)
            s = lax.dot_general(q, kc, NT_DIMS, preferred_element_type=F32) * 0.125
            s = jnp.where(valid, s, NEG)
            m = jnp.max(s, axis=-1, keepdims=True)
            p = jnp.exp(s - m)
            l = jnp.sum(p, axis=-1, keepdims=True)
            o = jnp.dot(p.astype(BF16), vc, preferred_element_type=F32) / l
            o_ref[:, hs] = o
            lse_ref[:, hs] = jnp.broadcast_to(m + jnp.log(l), (BLK, HEAD_DIM))

    sh = jax.ShapeDtypeStruct((N_GROUPS, L, KVW), F32)
    return pl.pallas_call(
        body, out_shape=(sh, sh), grid=(N_GROUPS, NBLK),
        in_specs=[_blk(_CUR), _blk(_PREV), _blk(_CUR), _blk(_PREV), _blk(_CUR)],
        out_specs=(_blk(_CUR), _blk(_CUR)), name='attn_fwd',
        compiler_params=_params(('parallel', 'parallel'), 32))(qp, kp, kp, vp, vp)


def _attn_bwd(qp, kp, vp, o, lse, do, dlse):
    def body(q_ref, kp_ref, kc_ref, vp_ref, vc_ref, o_ref, lse_ref, do_ref, dlse_ref, dq_ref, dk_ref, dv_ref):
        b = pl.program_id(1)

        @pl.when(b == 0)
        def _():
            dk_ref[...] = jnp.zeros_like(dk_ref)
            dv_ref[...] = jnp.zeros_like(dv_ref)

        valid = _attn_mask(pl.program_id(0), b)
        prev0 = pl.multiple_of(jnp.maximum(b - 1, 0) * BLK, BLK)
        cur0 = pl.multiple_of(b * BLK, BLK)
        for h in range(HEADS):
            hs = slice(h * HEAD_DIM, (h + 1) * HEAD_DIM)
            q = q_ref[:, hs]
            kc = jnp.concatenate([kp_ref[:, hs], kc_ref[:, hs]], axis=0)
            vc = jnp.concatenate([vp_ref[:, hs], vc_ref[:, hs]], axis=0)
            s = lax.dot_general(q, kc, NT_DIMS, preferred_element_type=F32) * 0.125
            s = jnp.where(valid, s, NEG)
            p = jnp.exp(s - lse_ref[:, h * HEAD_DIM:h * HEAD_DIM + 1])
            do_h = do_ref[:, hs]
            delta = jnp.sum(do_h * o_ref[:, hs], axis=-1, keepdims=True)
            do_b = do_h.astype(BF16)
            dp = lax.dot_general(do_b, vc, NT_DIMS, preferred_element_type=F32)
            ds = p * (dp - delta + dlse_ref[:, h * HEAD_DIM:h * HEAD_DIM + 1])
            ds_b = (ds * 0.125).astype(BF16)
            dq_ref[:, hs] = jnp.dot(ds_b, kc, preferred_element_type=F32)
            dkc = lax.dot_general(ds_b, q, TN_DIMS, preferred_element_type=F32)
            dvc = lax.dot_general(p.astype(BF16), do_b, TN_DIMS, preferred_element_type=F32)
            dk_ref[pl.ds(prev0, BLK), hs] += dkc[:BLK]
            dv_ref[pl.ds(prev0, BLK), hs] += dvc[:BLK]
            dk_ref[pl.ds(cur0, BLK), hs] += dkc[BLK:]
            dv_ref[pl.ds(cur0, BLK), hs] += dvc[BLK:]

    sh = jax.ShapeDtypeStruct((N_GROUPS, L, KVW), F32)
    whole = pl.BlockSpec((None, L, KVW), lambda g, b: (g, 0, 0))
    return pl.pallas_call(
        body, out_shape=(sh, sh, sh), grid=(N_GROUPS, NBLK),
        in_specs=[_blk(_CUR), _blk(_PREV), _blk(_CUR), _blk(_PREV), _blk(_CUR)] + [_blk(_CUR)] * 4,
        out_specs=(_blk(_CUR), whole, whole), name='attn_bwd',
        compiler_params=_params(('arbitrary', 'arbitrary'), 48))(qp, kp, kp, vp, vp, o, lse, do, dlse)


TC = 512


def _comb_spec():
    return pl.BlockSpec((N_GROUPS, TC, KVW), lambda i: (0, i, 0))


def _combine_weights(lse_ref):
    l0, l1, l2 = lse_ref[0], lse_ref[1], lse_ref[2]
    m = jnp.maximum(jnp.maximum(l0, l1), l2)
    e0, e1, e2 = jnp.exp(l0 - m), jnp.exp(l1 - m), jnp.exp(l2 - m)
    z = e0 + e1 + e2
    return e0 / z, e1 / z, e2 / z


def _combine_fwd(o3, lse3, g):
    def body(o_ref, lse_ref, g_ref, att_ref):
        w0, w1, w2 = _combine_weights(lse_ref)
        a = w0 * o_ref[0] + w1 * o_ref[1] + w2 * o_ref[2]
        att_ref[...] = ((a * _rms(a)) * g_ref[...]).astype(BF16)

    return pl.pallas_call(
        body, out_shape=jax.ShapeDtypeStruct((L, KVW), BF16), grid=(L // TC,),
        in_specs=[_comb_spec(), _comb_spec(), _vec(KVW)], out_specs=pl.BlockSpec((TC, KVW), lambda i: (i, 0)),
        name='combine_fwd', compiler_params=_params(('parallel',), 40))(o3, lse3, g)


def _combine_bwd(datt, o3, lse3, g, head_ones):
    def body(datt_ref, o_ref, lse_ref, g_ref, e_ref, do_ref, dlse_ref, dg_ref):
        @pl.when(pl.program_id(0) == 0)
        def _():
            dg_ref[...] = jnp.zeros_like(dg_ref)

        ws = _combine_weights(lse_ref)
        a = ws[0] * o_ref[0] + ws[1] * o_ref[1] + ws[2] * o_ref[2]
        r = _rms(a)
        n = a * r
        dv = datt_ref[...]
        dg_ref[...] += _colsum(dv * n)
        da = _rms_bwd(dv * g_ref[...], n, r)
        dws = [jnp.dot(da * o_ref[i], e_ref[...], preferred_element_type=F32, precision=HI) for i in range(3)]
        dbar = ws[0] * dws[0] + ws[1] * dws[1] + ws[2] * dws[2]
        for i in range(3):
            do_ref[i] = ws[i] * da
            dlse_ref[i] = ws[i] * (dws[i] - dbar)
        _fold8(pl.program_id(0) == L // TC - 1, dg_ref)

    sh = jax.ShapeDtypeStruct((N_GROUPS, L, KVW), F32)
    return pl.pallas_call(
        body, out_shape=(sh, sh, jax.ShapeDtypeStruct((8, KVW), F32)), grid=(L // TC,),
        in_specs=[pl.BlockSpec((TC, KVW), lambda i: (i, 0)), _comb_spec(), _comb_spec(), _vec(KVW),
                  pl.BlockSpec((KVW, KVW), lambda i: (0, 0))],
        out_specs=(_comb_spec(), _comb_spec(), _vec8(KVW)), name='combine_bwd',
        compiler_params=_params(('arbitrary',), 48))(datt, o3, lse3, g, head_ones)


def _ssm_disc(ar, ai, ldt):
    dt = jnp.exp(ldt)
    zr, zi = ar * dt, ai * dt
    ez = jnp.exp(zr)
    A_r, A_i = ez * jnp.cos(zi), ez * jnp.sin(zi)
    den = ar * ar + ai * ai
    xr, xi = A_r - 1.0, A_i
    cr = (xr * ar + xi * ai) / den
    ci = (xi * ar - xr * ai) / den
    return dt, zr, zi, A_r, A_i, den, cr, ci


def _ssm_pre(ar, ai, ldt, br, bi):
    def body(ar_ref, ai_ref, ldt_ref, br_ref, bi_ref, bbr_ref, bbi_ref, pwr_ref, pwi_ref):
        _, zr, zi, _, _, _, cr, ci = _ssm_disc(ar_ref[...], ai_ref[...], ldt_ref[...])
        bbr_ref[...] = cr * br_ref[...] - ci * bi_ref[...]
        bbi_ref[...] = cr * bi_ref[...] + ci * br_ref[...]
        k = (lax.broadcasted_iota(jnp.int32, (1, 8), 1) + 1).astype(F32)
        ek = jnp.exp(zr * k)
        pwr_ref[...] = ek * jnp.cos(zi * k)
        pwi_ref[...] = ek * jnp.sin(zi * k)

    s16 = jax.ShapeDtypeStruct((SSM_GN, SSM_P), F32)
    s8 = jax.ShapeDtypeStruct((SSM_GN, 8), F32)
    return pl.pallas_call(body, out_shape=(s16, s16, s8, s8), name='ssm_pre',
                          compiler_params=_params(None, 40))(ar, ai, ldt, br, bi)


def _ssm_post(ar, ai, ldt, br, bi, gar, gai, gbr, gbi, sel):
    def body(ar_ref, ai_ref, ldt_ref, br_ref, bi_ref, gar_ref, gai_ref, gbr_ref, gbi_ref, sel_ref,
             dar_ref, dai_ref, dbr_ref, dbi_ref, dldt_ref):
        a_r, a_i = ar_ref[...], ai_ref[...]
        dt, _, _, A_r, A_i, den, cr, ci = _ssm_disc(a_r, a_i, ldt_ref[...])
        b_r, b_i, g_br, g_bi = br_ref[...], bi_ref[...], gbr_ref[...], gbi_ref[...]
        gcr = jnp.sum(g_br * b_r + g_bi * b_i, axis=-1, keepdims=True)
        gci = jnp.sum(g_bi * b_r - g_br * b_i, axis=-1, keepdims=True)
        dbr_ref[...] = g_br * cr + g_bi * ci
        dbi_ref[...] = g_bi * cr - g_br * ci
        g_ar = gar_ref[...] + (gcr * a_r - gci * a_i) / den
        g_ai = gai_ref[...] + (gcr * a_i + gci * a_r) / den
        qr = (cr * a_r + ci * a_i) / den
        qi = (ci * a_r - cr * a_i) / den
        glr = -(gcr * qr + gci * qi)
        gli = -(gci * qr - gcr * qi)
        gzr = g_ar * A_r + g_ai * A_i
        gzi = g_ai * A_r - g_ar * A_i
        dar_ref[...] = glr + gzr * dt
        dai_ref[...] = gli + gzi * dt
        gdt = (gzr * a_r + gzi * a_i) * dt
        dldt_ref[...] = jnp.dot(sel_ref[...], jnp.broadcast_to(gdt, (SSM_GN, 128)),
                                preferred_element_type=F32, precision=HI)

    s1 = jax.ShapeDtypeStruct((SSM_GN, 1), F32)
    s16 = jax.ShapeDtypeStruct((SSM_GN, SSM_P), F32)
    return pl.pallas_call(body, out_shape=(s1, s1, s16, s16, jax.ShapeDtypeStruct((SSM_G, 128), F32)),
                          name='ssm_post', compiler_params=_params(None, 48))(
                              ar, ai, ldt, br, bi, gar, gai, gbr, gbi, sel)


SCAN_CH = 8


def _scan_fwd_tiles(s_ref, pw, carry):
    pwr, pwi = pw[:, :CL_S], pw[:, CL_S:]
    row = lax.broadcasted_iota(jnp.int32, (8, CL_S), 0)
    steps = [(k, jnp.where(row >= k, pwr[k - 1:k], 0.0), jnp.where(row >= k, pwi[k - 1:k], 0.0)) for k in (1, 2, 4)]
    rows = 8 * SCAN_CH

    def chunk(i, c):
        cr, ci = c
        r0 = pl.multiple_of(i * rows, rows)
        xr = s_ref[pl.ds(r0, rows), 0:CL_S].reshape(SCAN_CH, 8, CL_S)
        xi = s_ref[pl.ds(r0, rows), CL_S:2 * CL_S].reshape(SCAN_CH, 8, CL_S)
        for k, pr, pi in steps:
            sr, si = pltpu.roll(xr, k, 1), pltpu.roll(xi, k, 1)
            xr, xi = xr + pr * sr - pi * si, xi + pr * si + pi * sr
        for j in range(SCAN_CH):
            tr = xr[j] + pwr * cr - pwi * ci
            ti = xi[j] + pwr * ci + pwi * cr
            s_ref[pl.ds(r0 + 8 * j, 8), 0:CL_S] = tr
            s_ref[pl.ds(r0 + 8 * j, 8), CL_S:2 * CL_S] = ti
            cr, ci = tr[7:8], ti[7:8]
        return cr, ci

    return lax.fori_loop(0, T_SCAN // rows, chunk, (carry[:, :CL_S], carry[:, CL_S:]))


def _scan_bwd_tiles(l_ref, pw, carry):
    pwr, pwi = pw[:, :CL_S], pw[:, CL_S:]
    rpr = jnp.concatenate([pwr[7 - r:8 - r] for r in range(8)], axis=0)
    rpi = jnp.concatenate([pwi[7 - r:8 - r] for r in range(8)], axis=0)
    row = lax.broadcasted_iota(jnp.int32, (8, CL_S), 0)
    steps = [(k, jnp.where(row < 8 - k, pwr[k - 1:k], 0.0), jnp.where(row < 8 - k, pwi[k - 1:k], 0.0))
             for k in (1, 2, 4)]
    rows = 8 * SCAN_CH
    nc = T_SCAN // rows

    def chunk(i, c):
        cr, ci = c
        r0 = pl.multiple_of((nc - 1 - i) * rows, rows)
        xr = l_ref[pl.ds(r0, rows), 0:CL_S].reshape(SCAN_CH, 8, CL_S)
        xi = l_ref[pl.ds(r0, rows), CL_S:2 * CL_S].reshape(SCAN_CH, 8, CL_S)
        for k, pr, pi in steps:
            sr, si = pltpu.roll(xr, 8 - k, 1), pltpu.roll(xi, 8 - k, 1)
            xr, xi = xr + pr * sr + pi * si, xi + pr * si - pi * sr
        for j in reversed(range(SCAN_CH)):
            tr = xr[j] + rpr * cr + rpi * ci
            ti = xi[j] + rpr * ci - rpi * cr
            l_ref[pl.ds(r0 + 8 * j, 8), 0:CL_S] = tr
            l_ref[pl.ds(r0 + 8 * j, 8), CL_S:2 * CL_S] = ti
            cr, ci = tr[0:1], ti[0:1]
        return cr, ci

    return lax.fori_loop(0, nc, chunk, (carry[:, :CL_S], carry[:, CL_S:]))


NT_SCAN = L // T_SCAN


def _hilo(t):
    hi = t.astype(BF16)
    return jnp.stack([hi, (t - hi.astype(F32)).astype(BF16)], axis=1)


def _dot3(a, b_ref):
    ah = a.astype(BF16)
    al = (a - ah.astype(F32)).astype(BF16)
    bh, bl = b_ref[0], b_ref[1]
    return (jnp.dot(ah, bh, preferred_element_type=F32) + jnp.dot(al, bh, preferred_element_type=F32)
            + jnp.dot(ah, bl, preferred_element_type=F32))


def _hl_spec(r, c):
    return pl.BlockSpec((None, 2, r, c), lambda c_, t: (c_, 0, 0, 0))


def _ssm_fwd(u, bm2, cm2, pw, dvec, u_off=0):
    def body(u_ref, bm_ref, cm_ref, pw_ref, d_ref, y_ref, bnd_ref, s_ref, carry_ref):
        @pl.when(pl.program_id(1) == 0)
        def _():
            carry_ref[...] = jnp.zeros_like(carry_ref)

        bnd_ref[...] = carry_ref[...]
        uv = u_ref[...]
        s_ref[...] = _dot3(uv, bm_ref)
        cr, ci = _scan_fwd_tiles(s_ref, pw_ref[...], carry_ref[...])
        carry_ref[...] = jnp.concatenate([cr, ci], axis=1)
        y_ref[...] = _dot3(s_ref[...], cm_ref) + d_ref[...] * uv

    return pl.pallas_call(
        body,
        out_shape=(jax.ShapeDtypeStruct((L, SSMW), F32), jax.ShapeDtypeStruct((N_CL, NT_SCAN, 1, 2 * CL_S), F32),
                   jax.ShapeDtypeStruct((L, N_CL * 2 * CL_S), F32)),
        grid=(N_CL, NT_SCAN),
        in_specs=[pl.BlockSpec((T_SCAN, CL_U), lambda c, t: (t, c + u_off)),
                  _hl_spec(CL_U, 2 * CL_S), _hl_spec(2 * CL_S, CL_U),
                  pl.BlockSpec((None, 8, 2 * CL_S), lambda c, t: (c, 0, 0)),
                  pl.BlockSpec((1, CL_U), lambda c, t: (0, c))],
        out_specs=(pl.BlockSpec((T_SCAN, CL_U), lambda c, t: (t, c)),
                   pl.BlockSpec((None, None, 1, 2 * CL_S), lambda c, t: (c, t, 0, 0)),
                   pl.BlockSpec((T_SCAN, 2 * CL_S), lambda c, t: (t, c))),
        scratch_shapes=[pltpu.VMEM((1, 2 * CL_S), F32)],
        name='ssm_fwd', compiler_params=_params(('arbitrary', 'arbitrary'), 40))(u, bm2, cm2, pw, dvec)


def _ssm_bwd(u, dy, states, bmt, cmt, pw, dvec, bnd, u_off=0):
    rev = lambda t: NT_SCAN - 1 - t

    def body(u_ref, dy_ref, s_ref, bmt_ref, cmt_ref, pw_ref, d_ref, bnd_ref,
             du_ref, dbm_ref, dcm_ref, da_ref, dd_ref, l_ref, carry_ref):
        @pl.when(pl.program_id(1) == 0)
        def _():
            carry_ref[...] = jnp.zeros_like(carry_ref)
            dbm_ref[...] = jnp.zeros_like(dbm_ref)
            dcm_ref[...] = jnp.zeros_like(dcm_ref)
            da_ref[...] = jnp.zeros_like(da_ref)
            dd_ref[...] = jnp.zeros_like(dd_ref)

        uv, dyv, pw = u_ref[...], dy_ref[...], pw_ref[...]
        dy_b = dyv.astype(BF16)
        entry = bnd_ref[...]
        l_ref[...] = jnp.dot(dy_b, cmt_ref[...], preferred_element_type=F32)
        cr, ci = _scan_bwd_tiles(l_ref, pw, carry_ref[...])
        carry_ref[...] = jnp.concatenate([cr, ci], axis=1)
        sv, lv = s_ref[...], l_ref[...]
        lv_b = lv.astype(BF16)
        du_ref[...] = dyv * d_ref[...] + jnp.dot(lv_b, bmt_ref[...], preferred_element_type=F32)
        dbm_ref[...] += lax.dot_general(uv.astype(BF16), lv_b, TN_DIMS, preferred_element_type=F32)
        dcm_ref[...] += lax.dot_general(sv.astype(BF16), dy_b, TN_DIMS, preferred_element_type=F32)
        dd_ref[...] += _colsum(dyv * uv)
        row = lax.broadcasted_iota(jnp.int32, (T_SCAN, 2 * CL_S), 0)
        sp = jnp.where(row == 0, entry, pltpu.roll(sv, 1, 0))
        spr, spi = sp[:, :CL_S], sp[:, CL_S:]
        lr, li = lv[:, :CL_S], lv[:, CL_S:]
        da_ref[:, 0:CL_S] += _colsum(lr * spr + li * spi)
        da_ref[:, CL_S:2 * CL_S] += _colsum(li * spr - lr * spi)
        _fold8(pl.program_id(1) == NT_SCAN - 1, da_ref, dd_ref)

    return pl.pallas_call(
        body,
        out_shape=(jax.ShapeDtypeStruct((L, SSMW), F32), jax.ShapeDtypeStruct((N_CL, CL_U, 2 * CL_S), F32),
                   jax.ShapeDtypeStruct((N_CL, 2 * CL_S, CL_U), F32), jax.ShapeDtypeStruct((N_CL, 8, 2 * CL_S), F32),
                   jax.ShapeDtypeStruct((8, SSMW), F32)),
        grid=(N_CL, NT_SCAN),
        in_specs=[pl.BlockSpec((T_SCAN, CL_U), lambda c, t: (rev(t), c + u_off)),
                  pl.BlockSpec((T_SCAN, CL_U), lambda c, t: (rev(t), c)),
                  pl.BlockSpec((T_SCAN, 2 * CL_S), lambda c, t: (rev(t), c)),
                  pl.BlockSpec((None, 2 * CL_S, CL_U), lambda c, t: (c, 0, 0)),
                  pl.BlockSpec((None, CL_U, 2 * CL_S), lambda c, t: (c, 0, 0)),
                  pl.BlockSpec((None, 8, 2 * CL_S), lambda c, t: (c, 0, 0)),
                  pl.BlockSpec((1, CL_U), lambda c, t: (0, c)),
                  pl.BlockSpec((None, None, 1, 2 * CL_S), lambda c, t: (c, rev(t), 0, 0))],
        out_specs=(pl.BlockSpec((T_SCAN, CL_U), lambda c, t: (rev(t), c)),
                   pl.BlockSpec((None, CL_U, 2 * CL_S), lambda c, t: (c, 0, 0)),
                   pl.BlockSpec((None, 2 * CL_S, CL_U), lambda c, t: (c, 0, 0)),
                   pl.BlockSpec((None, 8, 2 * CL_S), lambda c, t: (c, 0, 0)),
                   pl.BlockSpec((8, CL_U), lambda c, t: (0, c))),
        scratch_shapes=[pltpu.VMEM((T_SCAN, 2 * CL_S), F32), pltpu.VMEM((1, 2 * CL_S), F32)],
        name='ssm_bwd', compiler_params=_params(('arbitrary', 'arbitrary'), 48))(u, dy, states, bmt, cmt, pw, dvec, bnd)


GELU_C = math.sqrt(2.0 / math.pi)
GELU_K = 0.044715


def _gelu_parts(x):
    t = jnp.tanh(GELU_C * (x + GELU_K * (x * x * x)))
    return x * (0.5 * (1.0 + t)), t


def _glu_fwd(ypre, wglu, bglu, gs):
    def body(y_ref, w_ref, b_ref, g_ref, o_ref):
        yg, _ = _gelu_parts(y_ref[...])
        z = jnp.dot(yg.astype(BF16), w_ref[...], preferred_element_type=F32) + b_ref[...]
        s = yg * jax.nn.sigmoid(z)
        o_ref[...] = ((s * _rms(s)) * g_ref[...]).astype(BF16)

    return pl.pallas_call(
        body, out_shape=jax.ShapeDtypeStruct((L, SSMW), BF16), grid=(L // TR,),
        in_specs=[_rowspec(SSMW), pl.BlockSpec((SSMW, SSMW), lambda i: (0, 0)), _vec(SSMW), _vec(SSMW)],
        out_specs=_rowspec(SSMW), name='glu_fwd', compiler_params=_params(('parallel',), 32))(ypre, wglu, bglu, gs)


def _glu_bwd(ypre, dsn, wglu, bglu, gs):
    def body(y_ref, d_ref, w_ref, b_ref, g_ref, dy_ref, dw_ref, db_ref, dg_ref):
        @pl.when(pl.program_id(0) == 0)
        def _():
            dw_ref[...] = jnp.zeros_like(dw_ref)
            db_ref[...] = jnp.zeros_like(db_ref)
            dg_ref[...] = jnp.zeros_like(dg_ref)

        xv = y_ref[...]
        yg, t = _gelu_parts(xv)
        yg_b = yg.astype(BF16)
        z = jnp.dot(yg_b, w_ref[...], preferred_element_type=F32) + b_ref[...]
        sg = jax.nn.sigmoid(z)
        s = yg * sg
        r = _rms(s)
        n = s * r
        dv = d_ref[...]
        dg_ref[...] += _colsum(dv * n)
        ds = _rms_bwd(dv * g_ref[...], n, r)
        dz = (ds * yg) * (sg * (1.0 - sg))
        dz_b = dz.astype(BF16)
        db_ref[...] += _colsum(dz)
        dw_ref[...] += lax.dot_general(yg_b, dz_b, TN_DIMS, preferred_element_type=F32)
        dyg = ds * sg + lax.dot_general(dz_b, w_ref[...], NT_DIMS, preferred_element_type=F32)
        dgelu = 0.5 * (1.0 + t) + (0.5 * xv) * (1.0 - t * t) * (GELU_C * (1.0 + 3.0 * GELU_K * (xv * xv)))
        dy_ref[...] = dyg * dgelu
        _fold8(pl.program_id(0) == L // TR - 1, db_ref, dg_ref)

    vs = jax.ShapeDtypeStruct((8, SSMW), F32)
    return pl.pallas_call(
        body, out_shape=(jax.ShapeDtypeStruct((L, SSMW), F32), jax.ShapeDtypeStruct((SSMW, SSMW), F32), vs, vs),
        grid=(L // TR,),
        in_specs=[_rowspec(SSMW), _rowspec(SSMW), pl.BlockSpec((SSMW, SSMW), lambda i: (0, 0)), _vec(SSMW), _vec(SSMW)],
        out_specs=(_rowspec(SSMW), pl.BlockSpec((SSMW, SSMW), lambda i: (0, 0)), _vec8(SSMW), _vec8(SSMW)),
        name='glu_bwd', compiler_params=_params(('arbitrary',), 40))(ypre, dsn, wglu, bglu, gs)


def _me():
    return lax.axis_index('x'), lax.axis_index('y'), lax.axis_index('c')


def _my_index():
    return 4 * lax.axis_index('x') + 2 * lax.axis_index('y') + lax.axis_index('c')


def _peer(k):
    x, y, c = _me()
    px = 1 - x if k & 4 else x
    py = 1 - y if k & 2 else y
    pc = 1 - c if k & 1 else c
    return (px, py, pc), 4 * px + 2 * py + pc


def _mod_exchange(c_row, w_ada, b_ada8, deps=()):
    cw = NMOD * D // N_DEV

    def body(c_ref, w_ref, b_ref, *rest):
        call_ref, mod_ref, part_ref, send_sems, recv_sems = rest[len(deps):]
        x, y, c = _me()
        me = 4 * x + 2 * y + c
        call_ref[me] = c_ref[0]
        sends = []
        for k in range(1, N_DEV):
            peer, _ = _peer(k)
            cp = pltpu.make_async_remote_copy(src_ref=c_ref.at[0], dst_ref=call_ref.at[me], send_sem=send_sems.at[0, k - 1],
                                              recv_sem=recv_sems.at[0, k - 1], device_id=peer, device_id_type=MESH)
            cp.start()
            sends.append(cp)
        for k in range(1, N_DEV):
            peer, pidx = _peer(k)
            pltpu.make_async_remote_copy(src_ref=c_ref.at[0], dst_ref=call_ref.at[pidx], send_sem=send_sems.at[0, k - 1],
                                         recv_sem=recv_sems.at[0, k - 1], device_id=peer, device_id_type=MESH).wait_recv()
        for cp in sends:
            cp.wait_send()
        cv = call_ref[...].reshape(N_DEV, D)
        part = jnp.dot(cv * jax.nn.sigmoid(cv), w_ref[...], preferred_element_type=F32, precision=HI)
        part_ref[...] = part.reshape(N_DEV, 1, cw)
        mod_ref[me] = part_ref[me]
        sends = []
        for k in range(1, N_DEV):
            peer, pidx = _peer(k)
            cp = pltpu.make_async_remote_copy(src_ref=part_ref.at[pidx], dst_ref=mod_ref.at[me], send_sem=send_sems.at[1, k - 1],
                                              recv_sem=recv_sems.at[1, k - 1], device_id=peer, device_id_type=MESH)
            cp.start()
            sends.append(cp)
        for k in range(1, N_DEV):
            peer, pidx = _peer(k)
            pltpu.make_async_remote_copy(src_ref=part_ref.at[pidx], dst_ref=mod_ref.at[pidx], send_sem=send_sems.at[1, k - 1],
                                         recv_sem=recv_sems.at[1, k - 1], device_id=peer, device_id_type=MESH).wait_recv()
        for cp in sends:
            cp.wait_send()
        mod_ref[...] = mod_ref[...] + b_ref[...]

    vm = pl.BlockSpec(memory_space=pltpu.VMEM)
    return pl.pallas_call(
        body, out_shape=(jax.ShapeDtypeStruct((N_DEV, 1, D), F32), jax.ShapeDtypeStruct((N_DEV, 1, cw), F32)),
        in_specs=[vm, vm, vm] + [pl.BlockSpec(memory_space=pl.ANY)] * len(deps), out_specs=(vm, vm),
        scratch_shapes=[pltpu.VMEM((N_DEV, 1, cw), F32), pltpu.SemaphoreType.DMA((2, N_DEV - 1)),
                        pltpu.SemaphoreType.DMA((2, N_DEV - 1))],
        name='mod_exchange', compiler_params=_params(None, 48))(c_row, w_ada, b_ada8, *deps)


HBM_SPEC = pl.BlockSpec(memory_space=pltpu.HBM)
SEM_SPEC = pl.BlockSpec(memory_space=pltpu.SEMAPHORE)
DATAFLOW = pltpu.SideEffectType.DATAFLOW_SIDE_EFFECTING


def _push_start(src, scatter, after, name):
    land = lax.empty(src.shape if scatter else (N_DEV,) + src.shape, src.dtype)

    def body(src_ref, land_ref, after_ref, send_sem, recv_sem, land_thru, token):
        x, y, c = _me()
        me = 4 * x + 2 * y + c
        for k in range(1, N_DEV):
            peer, pidx = _peer(k)
            pltpu.make_async_remote_copy(src_ref=src_ref.at[pidx] if scatter else src_ref, dst_ref=land_ref.at[me],
                                         send_sem=send_sem, recv_sem=recv_sem, device_id=peer,
                                         device_id_type=MESH).start()
        token[...] = jnp.zeros_like(token)

    own = lax.dynamic_index_in_dim(src, _my_index(), 0, keepdims=False) if scatter else src
    src = pltpu.with_memory_space_constraint(src, pltpu.HBM)
    send_sem, recv_sem, land_thru, token = pl.pallas_call(
        body, name=name,
        out_shape=(pltpu.SemaphoreType.DMA(()), pltpu.SemaphoreType.DMA(()),
                   pltpu.HBM(land.shape, land.dtype), jax.ShapeDtypeStruct((8, 128), F32)),
        in_specs=(HBM_SPEC, HBM_SPEC, pl.BlockSpec(memory_space=pl.ANY)),
        out_specs=(SEM_SPEC, SEM_SPEC, HBM_SPEC, pl.BlockSpec(memory_space=pltpu.VMEM)),
        input_output_aliases={1: 2}, compiler_params=pltpu.CompilerParams(has_side_effects=DATAFLOW),
    )(src, pltpu.with_memory_space_constraint(land, pltpu.HBM), after)
    return send_sem, recv_sem, src, land_thru, token, own


def _push_wait(handle, after, name):
    send_sem, recv_sem, src, land_thru, _, own = handle
    after = tuple(after) if isinstance(after, (tuple, list)) else (after,)

    def body(src_ref, land_ref, send_sem, recv_sem, *rest):
        seven = land_ref.at[pl.ds(0, N_DEV - 1)]
        cp = pltpu.make_async_remote_copy(src_ref=seven, dst_ref=seven, send_sem=send_sem, recv_sem=recv_sem,
                                          device_id=_me(), device_id_type=MESH)
        cp.wait_send()
        cp.wait_recv()

    landed = pl.pallas_call(
        body, name=name, out_shape=pltpu.HBM(land_thru.shape, land_thru.dtype),
        in_specs=(HBM_SPEC, HBM_SPEC, SEM_SPEC, SEM_SPEC) + (pl.BlockSpec(memory_space=pl.ANY),) * len(after),
        out_specs=HBM_SPEC, input_output_aliases={1: 0},
        compiler_params=pltpu.CompilerParams(has_side_effects=DATAFLOW),
    )(src, land_thru, send_sem, recv_sem, *after)
    return lax.dynamic_update_index_in_dim(landed, own, _my_index(), 0)


def _adam(w, g, m, v):
    m2 = B1 * m + (1.0 - B1) * g
    v2 = B2 * v + (1.0 - B2) * jnp.square(g)
    m_hat = m2 / (1.0 - B1 ** STEP)
    v_hat = v2 / (1.0 - B2 ** STEP)
    delta = -LR * (m_hat / (jnp.sqrt(v_hat) + AEPS) + WD * w)
    return delta, m2, v2


def _small_update(gp, wp, mp, vp):
    def body(g_ref, w_ref, m_ref, v_ref, all_ref, go_ref, d_ref, mo_ref, vo_ref, send_sems, recv_sems):
        x, y, c = _me()
        me = 4 * x + 2 * y + c
        all_ref[me] = g_ref[...]
        sends = []
        for k in range(1, N_DEV):
            peer, _ = _peer(k)
            cp = pltpu.make_async_remote_copy(src_ref=g_ref, dst_ref=all_ref.at[me], send_sem=send_sems.at[k - 1],
                                              recv_sem=recv_sems.at[k - 1], device_id=peer, device_id_type=MESH)
            cp.start()
            sends.append(cp)
        for k in range(1, N_DEV):
            peer, pidx = _peer(k)
            pltpu.make_async_remote_copy(src_ref=g_ref, dst_ref=all_ref.at[pidx], send_sem=send_sems.at[k - 1],
                                         recv_sem=recv_sems.at[k - 1], device_id=peer, device_id_type=MESH).wait_recv()
        for cp in sends:
            cp.wait_send()
        g = all_ref[0]
        for d in range(1, N_DEV):
            g = g + all_ref[d]
        delta, m2, v2 = _adam(w_ref[...], g, m_ref[...], v_ref[...])
        go_ref[...] = g
        d_ref[...] = delta
        mo_ref[...] = m2
        vo_ref[...] = v2

    vm = pl.BlockSpec(memory_space=pltpu.VMEM)
    vs = jax.ShapeDtypeStruct(gp.shape, F32)
    return pl.pallas_call(
        body, out_shape=(jax.ShapeDtypeStruct((N_DEV,) + gp.shape, F32), vs, vs, vs, vs), in_specs=[vm] * 4,
        out_specs=(vm,) * 5,
        scratch_shapes=[pltpu.SemaphoreType.DMA((N_DEV - 1,)), pltpu.SemaphoreType.DMA((N_DEV - 1,))],
        name='small_update', compiler_params=_params(None, 48))(gp, wp, mp, vp)


def _small_sum_update(parts, wp, mp, vp):
    def body(p_ref, w_ref, m_ref, v_ref, go_ref, d_ref, mo_ref, vo_ref):
        g = p_ref[0]
        for d in range(1, N_DEV):
            g = g + p_ref[d]
        delta, m2, v2 = _adam(w_ref[...], g, m_ref[...], v_ref[...])
        go_ref[...] = g
        d_ref[...] = delta
        mo_ref[...] = m2
        vo_ref[...] = v2

    vm = pl.BlockSpec(memory_space=pltpu.VMEM)
    vs = jax.ShapeDtypeStruct(wp.shape, F32)
    return pl.pallas_call(body, out_shape=(vs, vs, vs, vs), in_specs=[vm] * 4, out_specs=(vm,) * 4,
                          name='small_sum_update', compiler_params=_params(None, 48))(parts, wp, mp, vp)


def _big_update(parts, w, m, v, name):
    _, R, C = parts.shape
    tr = R if R % 256 else (128 if C >= 2048 else 256)

    def body(p_ref, w_ref, m_ref, v_ref, g_ref, d_ref, mo_ref, vo_ref):
        g = p_ref[0].astype(F32)
        for d in range(1, N_DEV):
            g = g + p_ref[d].astype(F32)
        delta, m2, v2 = _adam(w_ref[...], g, m_ref[...], v_ref[...])
        g_ref[...] = g
        d_ref[...] = delta
        mo_ref[...] = m2
        vo_ref[...] = v2

    blk = pl.BlockSpec((tr, C), lambda i: (i, 0))
    sh = jax.ShapeDtypeStruct((R, C), F32)
    return pl.pallas_call(
        body, out_shape=(sh, sh, sh, sh), grid=(R // tr,),
        in_specs=[pl.BlockSpec((N_DEV, tr, C), lambda i: (0, i, 0)), blk, blk, blk], out_specs=(blk,) * 4,
        name=name, compiler_params=_params(('parallel',), 48))(parts, w, m, v)


def _ada_update(c_all, dmod_cols, w, m, v):
    C = w.shape[1]
    tr = 256

    def body(c_ref, dm_ref, w_ref, m_ref, v_ref, g_ref, d_ref, mo_ref, vo_ref):
        cv = c_ref[...]
        s = cv * jax.nn.sigmoid(cv)
        g = lax.dot_general(s, dm_ref[...], TN_DIMS, preferred_element_type=F32, precision=HI)
        delta, m2, v2 = _adam(w_ref[...], g, m_ref[...], v_ref[...])
        g_ref[...] = g
        d_ref[...] = delta
        mo_ref[...] = m2
        vo_ref[...] = v2

    blk = pl.BlockSpec((tr, C), lambda i: (i, 0))
    sh = jax.ShapeDtypeStruct((D, C), F32)
    return pl.pallas_call(
        body, out_shape=(sh, sh, sh, sh), grid=(D // tr,),
        in_specs=[pl.BlockSpec((N_DEV, tr), lambda i: (0, i)), pl.BlockSpec((N_DEV, C), lambda i: (0, 0)), blk, blk, blk],
        out_specs=(blk,) * 4, name='ada_update', compiler_params=_params(('parallel',), 48))(c_all, dmod_cols, w, m, v)


def _to_sub(t, d):
    if d == 1:
        return t
    return t.reshape(L // d, d, t.shape[-1]).transpose(1, 0, 2).reshape(L, t.shape[-1])


def _from_sub(t, d):
    if d == 1:
        return t
    return t.reshape(d, L // d, t.shape[-1]).transpose(1, 0, 2).reshape(L, t.shape[-1])


def _rows_to_cluster_lanes(t):
    k = t.shape[1]
    return t.reshape(N_CL, CL_S, k).transpose(0, 2, 1)


def _blockdiag_in(t):
    t = t.reshape(N_CL, CL_G, SSM_N, SSM_P).transpose(0, 1, 3, 2)
    eye = jnp.eye(CL_G, dtype=t.dtype)
    t = t[:, :, :, None, :] * eye[None, :, None, :, None]
    return t.reshape(N_CL, CL_U, CL_S)


def _blockdiag_extract(t):
    t = t.reshape(N_CL, CL_G, SSM_P, CL_G, SSM_N)
    t = jnp.stack([t[:, i, :, i, :] for i in range(CL_G)], axis=1)
    return t.transpose(0, 1, 3, 2).reshape(SSM_GN, SSM_P)


def _c_to_rows(t):
    return t.transpose(0, 2, 1).reshape(SSM_GN, SSM_P)


def _rows_to_c(t):
    return t.reshape(SSM_G, SSM_N, SSM_P).transpose(0, 2, 1)


def _ssm_prep(sp):
    rows = lambda n: sp[n].reshape(SSM_GN, 1)
    a_re, a_im = rows('ssm_a_re'), rows('ssm_a_im')
    ldt = jnp.repeat(sp['ssm_log_dt'].reshape(SSM_G, 1), SSM_N, axis=0)
    b_re, b_im = sp['ssm_b_re'].reshape(SSM_GN, SSM_P), sp['ssm_b_im'].reshape(SSM_GN, SSM_P)
    c_re, c_im = _c_to_rows(sp['ssm_c_re'].reshape(SSM_G, SSM_P, SSM_N)), _c_to_rows(sp['ssm_c_im'].reshape(SSM_G, SSM_P, SSM_N))
    bbr, bbi, pwr, pwi = _ssm_pre(a_re, a_im, ldt, b_re, b_im)
    bm = jnp.concatenate([_blockdiag_in(bbr), _blockdiag_in(bbi)], axis=2)
    cmt = jnp.concatenate([_blockdiag_in(c_re), -_blockdiag_in(c_im)], axis=2)
    bmt, cm = bm.transpose(0, 2, 1), cmt.transpose(0, 2, 1)
    pw = jnp.concatenate([_rows_to_cluster_lanes(pwr), _rows_to_cluster_lanes(pwi)], axis=2)
    bm2, cm2 = _hilo(bm), _hilo(cm)
    bmt_b, cmt_b = bmt.astype(BF16), cmt.astype(BF16)
    return a_re, a_im, ldt, b_re, b_im, bm2, cm2, bmt_b, cmt_b, pw


def _tied(v, deps):
    for t in deps:
        v = v + t[0, 0]
    return v


def _local_step(x, pos, mod, tgt, sp, prep, get_w, emit, emit_small, emit_late, first_deps=()):
    sh1, sc1, gt1, sh2, sc2, gt2 = (mod[i:i + 1] for i in range(NMOD))
    vec = lambda n: sp[n].reshape(1, -1)
    a_re, a_im, ldt, b_re, b_im, bm2, cm2, bmt_b, cmt_b, pw = prep
    dvec = vec('ssm_d')

    h1 = _prenorm_fwd(x, vec('g_pre_mix'), sc1, sh1)
    w_in = get_w('w_in', (h1, bm2, cm2, pw, bmt_b, cmt_b))
    proj = _mm(h1, w_in, mode='nn', name='mm_in', tn=1408, deps=first_deps)
    fr1 =ROPE_THETA ** (-jnp.arange(0, ROT_DIM, 2, dtype=F32) / ROT_DIM)
    lane = jnp.arange(128) % HEAD_DIM
    fr = jnp.where(lane < ROT_DIM, fr1[lane % (ROT_DIM // 2)], 0.0).reshape(1, 128).astype(F32)
    qk = _rope(proj, pos, fr, 1.0, 'rope_fwd', carry=KVW)
    v_b = qk[:, ROPE_W:]
    u_off = (ROPE_W + KVW) // CL_U
    qp = jnp.stack([_to_sub(qk[:, gi * KVW:(gi + 1) * KVW], d) for gi, d in enumerate(DILATIONS)])
    kp = jnp.stack([_to_sub(qk[:, QW:ROPE_W], d) for d in DILATIONS])
    vp = jnp.stack([_to_sub(v_b, d) for d in DILATIONS])
    o_p, lse_p = _attn_fwd(qp, kp, vp)
    o3 = jnp.stack([_from_sub(o_p[gi], d) for gi, d in enumerate(DILATIONS)])
    lse3 = jnp.stack([_from_sub(lse_p[gi], d) for gi, d in enumerate(DILATIONS)])
    att = _combine_fwd(o3, lse3, vec('g_attn_out'))

    ypre, bnd, states = _ssm_fwd(proj, bm2, cm2, pw, dvec, u_off)
    w_glu = get_w('w_glu', ypre)
    ssm_n = _glu_fwd(ypre, w_glu, vec('b_glu'), vec('g_ssm_out'))

    cat = jnp.concatenate([att, ssm_n], axis=1)
    w_out = get_w('w_out', cat)
    mix = _mm(cat, w_out, mode='nn', name='mm_out', tk=1280)
    x1, h2 = _postmix_fwd(x, mix, vec('g_post_mix'), gt1, vec('g_pre_mlp'), sc2, sh2)
    w_mi = get_w('w_mlp_in', h2)
    a_pre, r_act = _mm(h2, w_mi, mode='nn', name='mm_mlp_in', epilogue='relu2', b_sharded=True)
    w_mo = get_w('w_mlp_out', a_pre)
    y = _mm(r_act, w_mo, mode='nn', name='mm_mlp_out')
    dx2, dy, loss, dgt2, dg_post_mlp = _final_fwd_bwd(x1, y, tgt, vec('g_post_mlp'), gt2)
    dgt2, dg_post_mlp = dgt2[:1], dg_post_mlp[:1]

    da = _mm(dy, w_mo, mode='nt', name='mm_d_act', out_dtype=BF16, epilogue='drelu2', extra=a_pre)
    dep = emit('w_mlp_out', _mm(r_act, dy, mode='tn', name='mm_dw_mlp_out', out_dtype=BF16))
    dh2 = _mm(da, w_mi, mode='nt', name='mm_dh2', tk=1024, b_sharded=True, deps=dep)
    dep = emit('w_mlp_in', _mm(h2, da, mode='tn', name='mm_dw_mlp_in', out_dtype=BF16, out_sharded=True))
    dx1, dmix, dsc2, dsh2, dg_pre_mlp, dgt1, dg_post_mix = _postmix_bwd(
        dx2, dh2, x1, mix, vec('g_post_mix'), gt1, vec('g_pre_mlp'), sc2)
    dsc2, dsh2, dg_pre_mlp, dgt1, dg_post_mix = (t[:1] for t in (dsc2, dsh2, dg_pre_mlp, dgt1, dg_post_mix))
    dcat = _mm(dmix, w_out, mode='nt', name='mm_dcat', tn=1280, deps=dep)
    dep = emit('w_out', _mm(cat, dmix, mode='tn', name='mm_dw_out', out_dtype=BF16, tm=640))
    datt, dsn = dcat[:, :KVW], dcat[:, KVW:]

    dypre, g_w_glu, g_b_glu, g_g_ssm = _glu_bwd(ypre, dsn, w_glu, _tied(vec('b_glu'), dep), vec('g_ssm_out'))
    g_b_glu, g_g_ssm = g_b_glu[:1], g_g_ssm[:1]
    dep = dep + emit('w_glu', g_w_glu.astype(BF16))
    du, dbm, dcm, dA, dD = _ssm_bwd(proj, dypre, states, bmt_b, cmt_b, pw, dvec, bnd, u_off)
    dD = dD[:1]
    gbr, gbi = _blockdiag_extract(dbm[:, :, :CL_S]), _blockdiag_extract(dbm[:, :, CL_S:])
    dcmt = dcm.transpose(0, 2, 1)
    g_c_re = _rows_to_c(_blockdiag_extract(dcmt[:, :, :CL_S]))
    g_c_im = _rows_to_c(-_blockdiag_extract(dcmt[:, :, CL_S:]))
    gar = dA[:, 0, :CL_S].reshape(SSM_GN, 1)
    gai = dA[:, 0, CL_S:].reshape(SSM_GN, 1)
    sel = (jnp.arange(SSM_GN)[None, :] // SSM_N == jnp.arange(SSM_G)[:, None]).astype(F32)
    g_a_re, g_a_im, g_b_re, g_b_im, g_ldt = _ssm_post(a_re, a_im, ldt, b_re, b_im, gar, gai, gbr, gbi, sel)

    head_ones = (jnp.arange(KVW)[:, None] // HEAD_DIM == jnp.arange(KVW)[None, :] // HEAD_DIM).astype(F32)
    do3, dlse3, g_g_attn = _combine_bwd(datt, o3, lse3, vec('g_attn_out'), head_ones)
    g_g_attn = g_g_attn[:1]
    dep_small = emit_small({
        'g_post_mix': dg_post_mix, 'ssm_a_re': g_a_re, 'ssm_a_im': g_a_im, 'ssm_log_dt': g_ldt[:, 0],
        'ssm_b_re': g_b_re, 'ssm_b_im': g_b_im, 'ssm_c_re': g_c_re, 'ssm_c_im': g_c_im, 'ssm_d': dD, 'b_glu': g_b_glu,
        'g_attn_out': g_g_attn, 'g_ssm_out': g_g_ssm, 'g_pre_mlp': dg_pre_mlp, 'g_post_mlp': dg_post_mlp})
    do_p = jnp.stack([_to_sub(do3[gi], d) for gi, d in enumerate(DILATIONS)])
    dlse_p = jnp.stack([_to_sub(dlse3[gi], d) for gi, d in enumerate(DILATIONS)])
    dq_p, dk_p, dv_p = _attn_bwd(qp, kp, vp, o_p, lse_p, do_p, dlse_p)
    dq = jnp.concatenate([_from_sub(dq_p[gi], d) for gi, d in enumerate(DILATIONS)], axis=1)
    dk = sum(_from_sub(dk_p[gi], d) for gi, d in enumerate(DILATIONS))
    dv = sum(_from_sub(dv_p[gi], d) for gi, d in enumerate(DILATIONS))
    dqk = _rope(jnp.concatenate([dq, dk], axis=1), pos, _tied(fr, dep_small), -1.0, 'rope_bwd')
    dproj = jnp.concatenate([dqk, dv.astype(BF16), du.astype(BF16)], axis=1)
    dh1 = _mm(dproj, w_in, mode='nt', name='mm_dh1', tk=1408, deps=dep)
    grad_x, dsc1, dsh1, dg_pre_mix = _prenorm_bwd(dx1, dh1, x, vec('g_pre_mix'), sc1)
    dsc1, dsh1, dg_pre_mix = dsc1[:1], dsh1[:1], dg_pre_mix[:1]
    dmod = jnp.concatenate([dsh1, dsc1, dgt1, dsh2, dsc2, dgt2], axis=0)
    dep = emit_late({'b_ada': dmod, 'g_pre_mix': dg_pre_mix})
    emit('w_in', _mm(h1, dproj, mode='tn', name='mm_dw_in', out_dtype=BF16, tn=1408, deps=dep))
    return loss[0, 0], grad_x


def _pack(d, names):
    flat = jnp.concatenate([jnp.pad(d[n].reshape(-1).astype(F32), (0, SEG[n] - SMALL_SIZES[n])) for n in names])
    return flat.reshape(-1, 128)


def _unpack(packed, names, shapes):
    out, off = {}, 0
    for n in names:
        out[n] = packed[off // 128:(off + SEG[n]) // 128].reshape(-1)[:SMALL_SIZES[n]].reshape(shapes[n])
        off += SEG[n]
    return out


def _shard_major(t, name):
    if name in ('w_in', 'w_out', 'w_mlp_in'):
        k, n = t.shape
        return t.reshape(k, N_DEV, n // N_DEV).transpose(1, 0, 2)
    k, n = t.shape
    return t.reshape(N_DEV, k // N_DEV, n)


def _from_shard_major(t, name):
    if name in ('w_in', 'w_out', 'w_mlp_in'):
        _, k, n = t.shape
        return t.transpose(1, 0, 2).reshape(k, N_DEV * n)
    _, k, n = t.shape
    return t.reshape(N_DEV * k, n)


def kernel(x, c, positions, w_ada, b_ada, g_pre_mix, g_post_mix, w_in, ssm_a_re, ssm_a_im, ssm_log_dt, ssm_b_re, ssm_b_im, ssm_c_re, ssm_c_im, ssm_d, w_glu, b_glu, g_attn_out, g_ssm_out, w_out, g_pre_mlp, g_post_mlp, w_mlp_in, w_mlp_out, loss_target, m_w_ada, m_b_ada, m_g_pre_mix, m_g_post_mix, m_w_in, m_ssm_a_re, m_ssm_a_im, m_ssm_log_dt, m_ssm_b_re, m_ssm_b_im, m_ssm_c_re, m_ssm_c_im, m_ssm_d, m_w_glu, m_b_glu, m_g_attn_out, m_g_ssm_out, m_w_out, m_g_pre_mlp, m_g_post_mlp, m_w_mlp_in, m_w_mlp_out, v_w_ada, v_b_ada, v_g_pre_mix, v_g_post_mix, v_w_in, v_ssm_a_re, v_ssm_a_im, v_ssm_log_dt, v_ssm_b_re, v_ssm_b_im, v_ssm_c_re, v_ssm_c_im, v_ssm_d, v_w_glu, v_b_glu, v_g_attn_out, v_g_ssm_out, v_w_out, v_g_pre_mlp, v_g_post_mlp, v_w_mlp_in, v_w_mlp_out):
    loc = dict(locals())
    W = {n: loc[n] for n in WEIGHTS}
    M = {n: loc['m_' + n] for n in WEIGHTS}
    V = {n: loc['v_' + n] for n in WEIGHTS}
    assert x.shape == (1, L, D) and w_in.shape == (1, D, INW // N_DEV), (x.shape, w_in.shape)

    cw = NMOD * D // N_DEV
    c_all, mod8 = _mod_exchange(c.reshape(1, 1, D), w_ada[0], b_ada.reshape(N_DEV, 1, cw))
    mod = mod8.reshape(NMOD, D)

    gather, after = {}, mod8
    for n in BIG:
        gather[n] = _push_start(W[n][0].astype(BF16), False, after, 'gather_start_' + n)
        after = gather[n][4]
    tokens = tuple(gather[n][4] for n in BIG)
    mod = _tied(mod, tokens)
    sp = {n: W[n][0] for n in SMALL}
    prep = _ssm_prep({**sp, 'ssm_a_re': _tied(sp['ssm_a_re'], tokens)})

    def get_w(n, after):
        g = _push_wait(gather[n], after, 'gather_wait_' + n)
        return g if n == 'w_mlp_in' else _from_shard_major(g, n)

    scatter = {}

    def emit(n, g):
        src = g if n == 'w_mlp_in' else _shard_major(g, n)
        scatter[n] = _push_start(src, True, src, 'scatter_start_' + n)
        return (scatter[n][4],)

    small_early = []

    def emit_small(d):
        pack = _pack(d, SMALL_EARLY)
        small_early.append(_push_start(pack, False, pack, 'small_start'))
        return (small_early[0][4],)

    out_g, out_d, out_m, out_v = {}, {}, {}, {}
    shapes = {n: W[n].shape[1:] for n in SMALL}

    def put(names, packs):
        for dst, packed in zip((out_g, out_d, out_m, out_v), packs):
            dst.update(_unpack(packed, names, shapes))

    late = []

    def emit_late(d):
        rows_all, *packs = _small_update(*[_pack(t, SMALL_LATE) for t in (d, W, M, V)])
        put(SMALL_LATE, packs)
        late.append(rows_all)
        return (rows_all,)

    loss, grad_x = _local_step(x[0], positions.reshape(L, 1), mod, loss_target[0], sp, prep, get_w, emit, emit_small,
                               emit_late)
    loss = lax.psum(loss, ('x', 'y', 'c'))

    me = 4 * lax.axis_index('x') + 2 * lax.axis_index('y') + lax.axis_index('c')
    dmod_all = late[0][:, :NMOD * D // 128].reshape(N_DEV, NMOD * D)
    dmod_cols = _tied(lax.dynamic_slice_in_dim(dmod_all, me * cw, cw, axis=1), (scatter['w_in'][4],))
    out_g['w_ada'], out_d['w_ada'], out_m['w_ada'], out_v['w_ada'] = _ada_update(
        c_all.reshape(N_DEV, D), dmod_cols, w_ada[0], m_w_ada[0], v_w_ada[0])

    parts = _push_wait(small_early[0], out_v['w_ada'], 'small_wait')
    packs = _small_sum_update(parts, *[_pack(d, SMALL_EARLY) for d in (W, M, V)])
    put(SMALL_EARLY, packs)

    after = packs[3]
    for n in ('w_mlp_out', 'w_mlp_in', 'w_out', 'w_glu', 'w_in'):
        parts = _push_wait(scatter[n], after, 'scatter_wait_' + n)
        out_g[n], out_d[n], out_m[n], out_v[n] = _big_update(parts, W[n][0], M[n][0], V[n][0], 'update_' + n)
        after = out_v[n]

    lead = lambda t: t[None]
    return (loss, grad_x[None], *[lead(out_g[n]) for n in WEIGHTS], *[lead(out_d[n]) for n in WEIGHTS],
            *[lead(out_m[n]) for n in WEIGHTS], *[lead(out_v[n]) for n in WEIGHTS])
```

```python
import functools
import math

import jax
import jax.numpy as jnp
from jax import lax
from jax.experimental import pallas as pl
from jax.experimental.pallas import tpu as pltpu

F32 = jnp.float32
BF16 = jnp.bfloat16
HI = lax.Precision.HIGHEST
MESH = pl.DeviceIdType.MESH

N_DEV = 8
L = 4096
D = 2048
HEAD_DIM = 64
N_GROUPS = 3
DILATIONS = (1, 4, 16)
HEADS = 6
QW = N_GROUPS * HEADS * HEAD_DIM
KVW = HEADS * HEAD_DIM
ROT_DIM = 16
ROPE_THETA = 500000.0
BLK = 128
NBLK = L // BLK
SSMW = D - QW
SSM_P = 16
SSM_G = SSMW // SSM_P
SSM_N = 64
SSM_GN = SSM_G * SSM_N
CL_G = 8
N_CL = SSM_G // CL_G
CL_U = CL_G * SSM_P
CL_S = CL_G * SSM_N
INW = QW + 2 * KVW + SSMW
OUTW = KVW + SSMW
DFF = 4 * D
NMOD = 6
EPS = 1e-6
LR, B1, B2, AEPS, WD, STEP = 0.001, 0.9, 0.999, 1e-08, 0.01, 10

T_SCAN = 512
MB = 2 ** 20

WEIGHTS = ['w_ada', 'b_ada', 'g_pre_mix', 'g_post_mix', 'w_in', 'ssm_a_re', 'ssm_a_im', 'ssm_log_dt',
           'ssm_b_re', 'ssm_b_im', 'ssm_c_re', 'ssm_c_im', 'ssm_d', 'w_glu', 'b_glu', 'g_attn_out',
           'g_ssm_out', 'w_out', 'g_pre_mlp', 'g_post_mlp', 'w_mlp_in', 'w_mlp_out']
BIG = ['w_in', 'w_glu', 'w_out', 'w_mlp_in', 'w_mlp_out']
SMALL = [n for n in WEIGHTS if n not in BIG and n != 'w_ada']
SMALL_SIZES = {'b_ada': NMOD * D, 'g_pre_mix': D, 'g_post_mix': D, 'ssm_a_re': SSM_GN, 'ssm_a_im': SSM_GN,
               'ssm_log_dt': SSM_G, 'ssm_b_re': SSM_GN * SSM_P, 'ssm_b_im': SSM_GN * SSM_P,
               'ssm_c_re': SSM_GN * SSM_P, 'ssm_c_im': SSM_GN * SSM_P, 'ssm_d': SSMW, 'b_glu': SSMW,
               'g_attn_out': KVW, 'g_ssm_out': SSMW, 'g_pre_mlp': D, 'g_post_mlp': D}
SEG = {n: -(-SMALL_SIZES[n] // 1024) * 1024 for n in SMALL}
SMALL_LATE = ['b_ada', 'g_pre_mix']
SMALL_EARLY = [n for n in SMALL if n not in SMALL_LATE]


def _params(sem=None, vmem_mb=None):
    kw = {}
    if sem is not None:
        kw['dimension_semantics'] = sem
    if vmem_mb is not None:
        kw['vmem_limit_bytes'] = vmem_mb * MB
    return pltpu.CompilerParams(**kw)


def _vec(n):
    return pl.BlockSpec((1, n), lambda *_: (0, 0))


def _rms(x):
    return lax.rsqrt(jnp.mean(x * x, axis=-1, keepdims=True) + EPS)


def _rms_bwd(dn, n, r):
    return r * (dn - n * jnp.mean(dn * n, axis=-1, keepdims=True))


def _vec8(n):
    return pl.BlockSpec((8, n), lambda *_: (0, 0))


def _colsum(x):
    return jnp.sum(x.reshape(-1, 8, x.shape[-1]), axis=0)


def _fold8(last, *refs):
    @pl.when(last)
    def _():
        for r in refs:
            r[...] = jnp.broadcast_to(jnp.sum(r[...], axis=0, keepdims=True), r.shape)


def _mm(a, b, *, mode, name, out_dtype=F32, tm=1024, tn=1024, tk=2048, epilogue=None, extra=None,
        b_sharded=False, out_sharded=False, deps=()):
    if mode == 'nn':
        M, K = a.shape
        dims = (((1,), (0,)), ((), ()))
        a_spec = pl.BlockSpec((tm, tk), lambda i, j, k: (i, k))
        if b_sharded:
            _, K2, per = b.shape
            N, q = N_DEV * per, per // tn
            b_spec = pl.BlockSpec((None, tk, tn), lambda i, j, k: (j // q, k, j % q))
        else:
            K2, N = b.shape
            b_spec = pl.BlockSpec((tk, tn), lambda i, j, k: (k, j))
    elif mode == 'nt':
        M, K = a.shape
        dims = (((1,), (1,)), ((), ()))
        a_spec = pl.BlockSpec((tm, tk), lambda i, j, k: (i, k))
        if b_sharded:
            _, N, per = b.shape
            K2, q = N_DEV * per, per // tk
            b_spec = pl.BlockSpec((None, tn, tk), lambda i, j, k: (k // q, j, k % q))
        else:
            N, K2 = b.shape
            b_spec = pl.BlockSpec((tn, tk), lambda i, j, k: (j, k))
    else:
        (K, M), (K2, N) = a.shape, b.shape
        dims = (((0,), (0,)), ((), ()))
        a_spec = pl.BlockSpec((tk, tm), lambda i, j, k: (k, i))
        b_spec = pl.BlockSpec((tk, tn), lambda i, j, k: (k, j))
    assert K == K2 and M % tm == 0 and N % tn == 0 and K % tk == 0, (name, a.shape, b.shape, tm, tn, tk)
    nk = K // tk
    o_spec = pl.BlockSpec((tm, tn), lambda i, j, k: (i, j))
    o_dims = (M, N)
    if out_sharded:
        qo = N // N_DEV // tn
        o_spec = pl.BlockSpec((None, tm, tn), lambda i, j, k: (j // qo, i, j % qo))
        o_dims = (N_DEV, M, N // N_DEV)
    n_out = 2 if epilogue == 'relu2' else 1
    n_extra = 1 if extra is not None else 0
    n_in = 2 + n_extra + len(deps)

    def body(*refs):
        a_ref, b_ref = refs[0], refs[1]
        x_refs = refs[2:2 + n_extra]
        o_refs = refs[n_in:n_in + n_out]
        acc = refs[-1]
        k = pl.program_id(2)

        prod = lax.dot_general(a_ref[...], b_ref[...], dims, preferred_element_type=F32)

        def finish(r):
            if epilogue == 'relu2':
                o_refs[0][...] = r.astype(BF16)
                o_refs[1][...] = jnp.square(jnp.maximum(r, 0.0)).astype(BF16)
            elif epilogue == 'drelu2':
                pre = x_refs[0][...].astype(F32)
                o_refs[0][...] = (r * (2.0 * jnp.maximum(pre, 0.0))).astype(out_dtype)
            else:
                o_refs[0][...] = r.astype(out_dtype)

        if nk == 1:
            finish(prod)
        else:
            @pl.when(k == 0)
            def _():
                acc[...] = prod

            @pl.when((k > 0) & (k < nk - 1))
            def _():
                acc[...] += prod

            @pl.when(k == nk - 1)
            def _():
                finish(acc[...] + prod)

    if epilogue == 'relu2':
        out_shape = (jax.ShapeDtypeStruct((M, N), BF16), jax.ShapeDtypeStruct((M, N), BF16))
        out_specs = (o_spec, o_spec)
    else:
        out_shape = jax.ShapeDtypeStruct(o_dims, out_dtype)
        out_specs = o_spec
    args = (a, b) + ((extra,) if extra is not None else ()) + tuple(deps)
    in_specs = ([a_spec, b_spec] + ([o_spec] if extra is not None else [])
                + [pl.BlockSpec(memory_space=pl.ANY)] * len(deps))
    return pl.pallas_call(
        body, out_shape=out_shape, grid=(M // tm, N // tn, nk), in_specs=in_specs, out_specs=out_specs,
        scratch_shapes=[pltpu.VMEM((tm, tn) if nk > 1 else (8, 128), F32)], name=name,
        compiler_params=_params(('parallel', 'parallel', 'arbitrary'), 56))(*args)


TR = 256


def _rowspec(w=D):
    return pl.BlockSpec((TR, w), lambda i: (i, 0))


def _prenorm_fwd(x, g, sc, sh):
    def body(x_ref, g_ref, sc_ref, sh_ref, h_ref):
        xv = x_ref[...]
        n = xv * _rms(xv)
        h_ref[...] = ((n * g_ref[...]) * (1.0 + sc_ref[...]) + sh_ref[...]).astype(BF16)

    return pl.pallas_call(
        body, out_shape=jax.ShapeDtypeStruct((L, D), BF16), grid=(L // TR,),
        in_specs=[_rowspec(), _vec(D), _vec(D), _vec(D)], out_specs=_rowspec(), name='prenorm_fwd',
        compiler_params=_params(('parallel',), 40))(x, g, sc, sh)


def _postmix_fwd(x, mix, gpm, gt1, gpl, sc2, sh2):
    def body(x_ref, mix_ref, gpm_ref, gt1_ref, gpl_ref, sc2_ref, sh2_ref, x1_ref, h2_ref):
        mix_v = mix_ref[...]
        nm = mix_v * _rms(mix_v)
        x1 = x_ref[...] + gt1_ref[...] * (nm * gpm_ref[...])
        x1_ref[...] = x1
        n2 = x1 * _rms(x1)
        h2_ref[...] = ((n2 * gpl_ref[...]) * (1.0 + sc2_ref[...]) + sh2_ref[...]).astype(BF16)

    return pl.pallas_call(
        body, out_shape=(jax.ShapeDtypeStruct((L, D), F32), jax.ShapeDtypeStruct((L, D), BF16)), grid=(L // TR,),
        in_specs=[_rowspec(), _rowspec()] + [_vec(D)] * 5, out_specs=(_rowspec(), _rowspec()), name='postmix_fwd',
        compiler_params=_params(('parallel',), 40))(x, mix, gpm, gt1, gpl, sc2, sh2)


def _final_fwd_bwd(x1, y, tgt, g, gt2):
    def body(x1_ref, y_ref, t_ref, g_ref, gt2_ref, dx2_ref, dy_ref, loss_ref, dgt2_ref, dg_ref):
        @pl.when(pl.program_id(0) == 0)
        def _():
            loss_ref[...] = jnp.zeros_like(loss_ref)
            dgt2_ref[...] = jnp.zeros_like(dgt2_ref)
            dg_ref[...] = jnp.zeros_like(dg_ref)

        yv = y_ref[...]
        r = _rms(yv)
        n = yv * r
        ng = n * g_ref[...]
        x2 = x1_ref[...] + gt2_ref[...] * ng
        e = x2 - t_ref[...]
        loss_ref[...] += 0.5 * jnp.sum(jnp.mean(e * e, axis=-1, keepdims=True), axis=0, keepdims=True)
        dx2 = e * (1.0 / D)
        dx2_ref[...] = dx2
        dgt2_ref[...] += _colsum(dx2 * ng)
        dng = dx2 * gt2_ref[...]
        dg_ref[...] += _colsum(dng * n)
        dy_ref[...] = _rms_bwd(dng * g_ref[...], n, r).astype(BF16)
        _fold8(pl.program_id(0) == L // TR - 1, dgt2_ref, dg_ref)

    return pl.pallas_call(
        body,
        out_shape=(jax.ShapeDtypeStruct((L, D), F32), jax.ShapeDtypeStruct((L, D), BF16),
                   jax.ShapeDtypeStruct((8, 128), F32), jax.ShapeDtypeStruct((8, D), F32),
                   jax.ShapeDtypeStruct((8, D), F32)),
        grid=(L // TR,), in_specs=[_rowspec(), _rowspec(), _rowspec(), _vec(D), _vec(D)],
        out_specs=(_rowspec(), _rowspec(), _vec8(128), _vec8(D), _vec8(D)), name='final_fwd_bwd',
        compiler_params=_params(('arbitrary',), 40))(x1, y, tgt, g, gt2)


def _postmix_bwd(dx2, dh2, x1, mix, gpm, gt1, gpl, sc2):
    def body(dx2_ref, dh2_ref, x1_ref, mix_ref, gpm_ref, gt1_ref, gpl_ref, sc2_ref,
             dx1_ref, dmix_ref, dsc2_ref, dsh2_ref, dgpl_ref, dgt1_ref, dgpm_ref):
        @pl.when(pl.program_id(0) == 0)
        def _():
            for r_ in (dsc2_ref, dsh2_ref, dgpl_ref, dgt1_ref, dgpm_ref):
                r_[...] = jnp.zeros_like(r_)

        x1v = x1_ref[...]
        r2 = _rms(x1v)
        n2 = x1v * r2
        dh2v = dh2_ref[...]
        dsh2_ref[...] += _colsum(dh2v)
        dsc2_ref[...] += _colsum(dh2v * (n2 * gpl_ref[...]))
        t = dh2v * (1.0 + sc2_ref[...])
        dgpl_ref[...] += _colsum(t * n2)
        dx1 = dx2_ref[...] + _rms_bwd(t * gpl_ref[...], n2, r2)
        dx1_ref[...] = dx1
        mix_v = mix_ref[...]
        rm = _rms(mix_v)
        nm = mix_v * rm
        dgt1_ref[...] += _colsum(dx1 * (nm * gpm_ref[...]))
        u = dx1 * gt1_ref[...]
        dgpm_ref[...] += _colsum(u * nm)
        dmix_ref[...] = _rms_bwd(u * gpm_ref[...], nm, rm).astype(BF16)
        _fold8(pl.program_id(0) == L // TR - 1, dsc2_ref, dsh2_ref, dgpl_ref, dgt1_ref, dgpm_ref)

    vs = jax.ShapeDtypeStruct((8, D), F32)
    return pl.pallas_call(
        body, out_shape=(jax.ShapeDtypeStruct((L, D), F32), jax.ShapeDtypeStruct((L, D), BF16), vs, vs, vs, vs, vs),
        grid=(L // TR,), in_specs=[_rowspec()] * 4 + [_vec(D)] * 4,
        out_specs=(_rowspec(), _rowspec()) + (_vec8(D),) * 5, name='postmix_bwd',
        compiler_params=_params(('arbitrary',), 48))(dx2, dh2, x1, mix, gpm, gt1, gpl, sc2)


def _prenorm_bwd(dx1, dh1, x, g, sc1):
    def body(dx1_ref, dh1_ref, x_ref, g_ref, sc1_ref, dx_ref, dsc1_ref, dsh1_ref, dg_ref):
        @pl.when(pl.program_id(0) == 0)
        def _():
            for r_ in (dsc1_ref, dsh1_ref, dg_ref):
                r_[...] = jnp.zeros_like(r_)

        xv = x_ref[...]
        r = _rms(xv)
        n = xv * r
        dh = dh1_ref[...]
        dsh1_ref[...] += _colsum(dh)
        dsc1_ref[...] += _colsum(dh * (n * g_ref[...]))
        t = dh * (1.0 + sc1_ref[...])
        dg_ref[...] += _colsum(t * n)
        dx_ref[...] = dx1_ref[...] + _rms_bwd(t * g_ref[...], n, r)
        _fold8(pl.program_id(0) == L // TR - 1, dsc1_ref, dsh1_ref, dg_ref)

    vs = jax.ShapeDtypeStruct((8, D), F32)
    return pl.pallas_call(
        body, out_shape=(jax.ShapeDtypeStruct((L, D), F32), vs, vs, vs), grid=(L // TR,),
        in_specs=[_rowspec()] * 3 + [_vec(D)] * 2, out_specs=(_rowspec(),) + (_vec8(D),) * 3, name='prenorm_bwd',
        compiler_params=_params(('arbitrary',), 40))(dx1, dh1, x, g, sc1)


ROPE_W = QW + KVW
QKV_W = 3 * KVW
NB_KV = KVW // 128


def _rope_rotate(xv, pos, fr, sign):
    ang = pos.astype(F32) * fr
    w = lax.broadcasted_iota(jnp.int32, (1, 128), 1) % HEAD_DIM
    cs = jnp.cos(ang)
    sn = jnp.sin(ang) * sign
    s1 = jnp.where(w < ROT_DIM // 2, -sn, 0.0)
    s2 = jnp.where((w >= ROT_DIM // 2) & (w < ROT_DIM), sn, 0.0)
    width = xv.shape[1]
    rep = width // 128
    cs, s1, s2 = jnp.tile(cs, (1, rep)), jnp.tile(s1, (1, rep)), jnp.tile(s2, (1, rep))
    hi = pltpu.roll(xv, width - ROT_DIM // 2, 1)
    lo = pltpu.roll(xv, ROT_DIM // 2, 1)
    return xv * cs + hi * s1 + lo * s2


def _sub_spec(d, rows, width):
    return pl.BlockSpec((d, rows // d, width), lambda i: (0, i, 0))


def _gather_rows(scr, blocks, r, d, rows):
    return jnp.concatenate([scr.at[j][pl.ds(r, rows // d, stride=d), :] for j in blocks], axis=1)


def _scatter_rows(scr, src_ref, d, rows):
    for r in range(d):
        for j in range(NB_KV):
            scr.at[j][pl.ds(r, rows // d, stride=d), :] = src_ref[r, :, j * 128:(j + 1) * 128]


def _token_order(scr):
    return jnp.concatenate([scr[j] for j in range(NB_KV)], axis=1)


def _rope_fwd(proj, pos, fr):
    nb = (ROPE_W + KVW) // 128

    def body(x_ref, pos_ref, fr_ref, o0_ref, o1_ref, o2_ref, scr):
        y = _rope_rotate(x_ref[:, 0:ROPE_W], pos_ref[...], fr_ref[...], 1.0)
        for j in range(ROPE_W // 128):
            scr[j] = y[:, j * 128:(j + 1) * 128]
        for j in range(ROPE_W // 128, nb):
            scr[j] = x_ref[:, j * 128:(j + 1) * 128]
        kv = list(range(QW // 128, nb))
        for g, (d, o_ref) in enumerate(zip(DILATIONS, (o0_ref, o1_ref, o2_ref))):
            blocks = list(range(g * NB_KV, (g + 1) * NB_KV)) + kv
            for r in range(d):
                o_ref[r] = _gather_rows(scr, blocks, r, d, TR).astype(BF16)

    return pl.pallas_call(
        body, out_shape=tuple(jax.ShapeDtypeStruct((d, L // d, QKV_W), BF16) for d in DILATIONS), grid=(L // TR,),
        in_specs=[_rowspec(ROPE_W + KVW), pl.BlockSpec((TR, 1), lambda i: (i, 0)), _vec(128)],
        out_specs=tuple(_sub_spec(d, TR, QKV_W) for d in DILATIONS),
        scratch_shapes=[pltpu.VMEM((nb, TR, 128), F32)], name='rope_fwd',
        compiler_params=_params(('parallel',), 40))(proj, pos, fr)


def _rope_bwd(dqkv, du, pos, fr):
    def body(*refs):
        grads = [refs[3 * g:3 * g + 3] for g in range(N_GROUPS)]
        du_ref, pos_ref, fr_ref, o_ref = refs[9:13]
        scrs = refs[13:]
        dq, dk, dv = [], None, None
        for g, d in enumerate(DILATIONS):
            parts = []
            for t in range(3):
                if d == 1:
                    parts.append(grads[g][t][0])
                else:
                    scr = scrs[3 * (g - 1) + t]
                    _scatter_rows(scr, grads[g][t], d, TR)
                    parts.append(_token_order(scr))
            dq.append(parts[0])
            dk = parts[1] if dk is None else dk + parts[1]
            dv = parts[2] if dv is None else dv + parts[2]
        x = jnp.concatenate(dq + [dk], axis=1)
        o_ref[:, 0:ROPE_W] = _rope_rotate(x, pos_ref[...], fr_ref[...], -1.0).astype(BF16)
        o_ref[:, ROPE_W:ROPE_W + KVW] = dv.astype(BF16)
        o_ref[:, ROPE_W + KVW:INW] = du_ref[...].astype(BF16)

    flat = [a for grp in dqkv for a in grp]
    in_specs = [_sub_spec(d, TR, KVW) for d in DILATIONS for _ in range(3)]
    in_specs += [_rowspec(SSMW), pl.BlockSpec((TR, 1), lambda i: (i, 0)), _vec(128)]
    return pl.pallas_call(
        body, out_shape=jax.ShapeDtypeStruct((L, INW), BF16), grid=(L // TR,), in_specs=in_specs,
        out_specs=_rowspec(INW), scratch_shapes=[pltpu.VMEM((NB_KV, TR, 128), F32)] * 6, name='rope_bwd',
        compiler_params=_params(('parallel',), 48))(*flat, du, pos, fr)


def _attn_mask(nbs, b):
    first = (b & (nbs - 1)) == 0
    qi = lax.broadcasted_iota(jnp.int32, (BLK, 2 * BLK), 0)
    kj = lax.broadcasted_iota(jnp.int32, (BLK, 2 * BLK), 1)
    dist = qi + BLK - kj
    return (dist >= 0) & (dist <= BLK) & ((kj >= BLK) | jnp.logical_not(first))


def _qkv_specs():
    cur = lambda col: pl.BlockSpec((BLK, KVW), lambda b: (b, col))
    prev = lambda col: pl.BlockSpec((BLK, KVW), lambda b: (jnp.maximum(b - 1, 0), col))
    return [cur(0), prev(1), cur(1), prev(2), cur(2)]


_ROWS = pl.BlockSpec((BLK, KVW), lambda b: (b, 0))
NEG = -1e30
NT_DIMS = (((1,), (1,)), ((), ()))
TN_DIMS = (((0,), (0,)), ((), ()))


def _attn_fwd(qkv, d):
    nbs = L // d // BLK

    def body(q_ref, kp_ref, kc_ref, vp_ref, vc_ref, o_ref, lse_ref):
        valid = _attn_mask(nbs, pl.program_id(0))
        for h in range(HEADS):
            hs = slice(h * HEAD_DIM, (h + 1) * HEAD_DIM)
            q = q_ref[:, hs]
            kc = jnp.concatenate([kp_ref[:, hs], kc_ref[:, hs]], axis=0)
            vc = jnp.concatenate([vp_ref[:, hs], vc_ref[:, hs]], axis=0)
            s = lax.dot_general(q, kc, NT_DIMS, preferred_element_type=F32) * 0.125
            s = jnp.where(valid, s, NEG)
            m = jnp.max(s, axis=-1, keepdims=True)
            p = jnp.exp(s - m)
            l = jnp.sum(p, axis=-1, keepdims=True)
            o = jnp.dot(p.astype(BF16), vc, preferred_element_type=F32) / l
            o_ref[:, hs] = o
            lse_ref[:, hs] = jnp.broadcast_to(m + jnp.log(l), (BLK, HEAD_DIM))

    sh = jax.ShapeDtypeStruct((L, KVW), F32)
    q2 = qkv.reshape(L, QKV_W)
    o, lse = pl.pallas_call(
        body, out_shape=(sh, sh), grid=(NBLK,), in_specs=_qkv_specs(), out_specs=(_ROWS, _ROWS),
        name='attn_fwd_d%d' % d, compiler_params=_params(('parallel',), 32))(q2, q2, q2, q2, q2)
    return o.reshape(d, L // d, KVW), lse.reshape(d, L // d, KVW)


def _attn_bwd(qkv, o, lse, do, dlse, d):
    nbs = L // d // BLK

    def body(q_ref, kp_ref, kc_ref, vp_ref, vc_ref, o_ref, lse_ref, do_ref, dlse_ref, dq_ref, dk_ref, dv_ref):
        b = pl.program_id(0)

        @pl.when(b == 0)
        def _():
            dk_ref[...] = jnp.zeros_like(dk_ref)
            dv_ref[...] = jnp.zeros_like(dv_ref)

        valid = _attn_mask(nbs, b)
        prev0 = pl.multiple_of(jnp.maximum(b - 1, 0) * BLK, BLK)
        cur0 = pl.multiple_of(b * BLK, BLK)
        for h in range(HEADS):
            hs = slice(h * HEAD_DIM, (h + 1) * HEAD_DIM)
            q = q_ref[:, hs]
            kc = jnp.concatenate([kp_ref[:, hs], kc_ref[:, hs]], axis=0)
            vc = jnp.concatenate([vp_ref[:, hs], vc_ref[:, hs]], axis=0)
            s = lax.dot_general(q, kc, NT_DIMS, preferred_element_type=F32) * 0.125
            s = jnp.where(valid, s, NEG)
            p = jnp.exp(s - lse_ref[:, h * HEAD_DIM:h * HEAD_DIM + 1])
            do_h = do_ref[:, hs]
            delta = jnp.sum(do_h * o_ref[:, hs], axis=-1, keepdims=True)
            do_b = do_h.astype(BF16)
            dp = lax.dot_general(do_b, vc, NT_DIMS, preferred_element_type=F32)
            ds = p * (dp - delta + dlse_ref[:, h * HEAD_DIM:h * HEAD_DIM + 1])
            ds_b = (ds * 0.125).astype(BF16)
            dq_ref[:, hs] = jnp.dot(ds_b, kc, preferred_element_type=F32)
            dkc = lax.dot_general(ds_b, q, TN_DIMS, preferred_element_type=F32)
            dvc = lax.dot_general(p.astype(BF16), do_b, TN_DIMS, preferred_element_type=F32)
            dk_ref[pl.ds(prev0, BLK), hs] += dkc[:BLK]
            dv_ref[pl.ds(prev0, BLK), hs] += dvc[:BLK]
            dk_ref[pl.ds(cur0, BLK), hs] += dkc[BLK:]
            dv_ref[pl.ds(cur0, BLK), hs] += dvc[BLK:]

    sh = jax.ShapeDtypeStruct((L, KVW), F32)
    whole = pl.BlockSpec((L, KVW), lambda b: (0, 0))
    q2 = qkv.reshape(L, QKV_W)
    flat = lambda t: t.reshape(L, KVW)
    outs = pl.pallas_call(
        body, out_shape=(sh, sh, sh), grid=(NBLK,), in_specs=_qkv_specs() + [_ROWS] * 4,
        out_specs=(_ROWS, whole, whole), name='attn_bwd_d%d' % d,
        compiler_params=_params(('arbitrary',), 48))(q2, q2, q2, q2, q2, flat(o), flat(lse), flat(do), flat(dlse))
    return tuple(t.reshape(d, L // d, KVW) for t in outs)


TC = 512


def _combine_weights(l0, l1, l2):
    m = jnp.maximum(jnp.maximum(l0, l1), l2)
    e0, e1, e2 = jnp.exp(l0 - m), jnp.exp(l1 - m), jnp.exp(l2 - m)
    z = e0 + e1 + e2
    return e0 / z, e1 / z, e2 / z


def _load_groups(refs, scrs):
    out = [refs[0][0]]
    for g in (1, 2):
        _scatter_rows(scrs[g - 1], refs[g], DILATIONS[g], TC)
        out.append(_token_order(scrs[g - 1]))
    return out


def _combine_fwd(os_, lses, g):
    def body(o0, o1, o2, l0, l1, l2, g_ref, att_ref, *scrs):
        ov = _load_groups((o0, o1, o2), scrs[0:2])
        lv = _load_groups((l0, l1, l2), scrs[2:4])
        w0, w1, w2 = _combine_weights(*lv)
        a = w0 * ov[0] + w1 * ov[1] + w2 * ov[2]
        att_ref[...] = ((a * _rms(a)) * g_ref[...]).astype(BF16)

    subs = [_sub_spec(d, TC, KVW) for d in DILATIONS]
    return pl.pallas_call(
        body, out_shape=jax.ShapeDtypeStruct((L, KVW), BF16), grid=(L // TC,), in_specs=subs + subs + [_vec(KVW)],
        out_specs=pl.BlockSpec((TC, KVW), lambda i: (i, 0)),
        scratch_shapes=[pltpu.VMEM((NB_KV, TC, 128), F32)] * 4, name='combine_fwd',
        compiler_params=_params(('parallel',), 40))(*os_, *lses, g)


def _combine_bwd(dcat, os_, lses, g, head_ones):
    def body(datt_ref, o0, o1, o2, l0, l1, l2, g_ref, e_ref, do0, do1, do2, dl0, dl1, dl2, dg_ref, *scrs):
        @pl.when(pl.program_id(0) == 0)
        def _():
            dg_ref[...] = jnp.zeros_like(dg_ref)

        ov = _load_groups((o0, o1, o2), scrs[0:2])
        lv = _load_groups((l0, l1, l2), scrs[2:4])
        ws = _combine_weights(*lv)
        a = ws[0] * ov[0] + ws[1] * ov[1] + ws[2] * ov[2]
        r = _rms(a)
        n = a * r
        dv = datt_ref[...]
        dg_ref[...] += _colsum(dv * n)
        da = _rms_bwd(dv * g_ref[...], n, r)
        dws = [jnp.dot(da * ov[i], e_ref[...], preferred_element_type=F32, precision=HI) for i in range(3)]
        dbar = ws[0] * dws[0] + ws[1] * dws[1] + ws[2] * dws[2]
        scr = scrs[4]
        for i, (d, do_ref, dl_ref) in enumerate(zip(DILATIONS, (do0, do1, do2), (dl0, dl1, dl2))):
            for val, out_ref in ((ws[i] * da, do_ref), (ws[i] * (dws[i] - dbar), dl_ref)):
                if d == 1:
                    out_ref[0] = val
                else:
                    for j in range(NB_KV):
                        scr[j] = val[:, j * 128:(j + 1) * 128]
                    for rr in range(d):
                        out_ref[rr] = _gather_rows(scr, range(NB_KV), rr, d, TC)
        _fold8(pl.program_id(0) == L // TC - 1, dg_ref)

    subs = [_sub_spec(d, TC, KVW) for d in DILATIONS]
    shs = tuple(jax.ShapeDtypeStruct((d, L // d, KVW), F32) for d in DILATIONS)
    outs = pl.pallas_call(
        body, out_shape=shs + shs + (jax.ShapeDtypeStruct((8, KVW), F32),), grid=(L // TC,),
        in_specs=[pl.BlockSpec((TC, KVW), lambda i: (i, 0))] + subs + subs + [_vec(KVW),
                                                                              pl.BlockSpec((KVW, KVW), lambda i: (0, 0))],
        out_specs=tuple(subs) + tuple(subs) + (_vec8(KVW),),
        scratch_shapes=[pltpu.VMEM((NB_KV, TC, 128), F32)] * 5, name='combine_bwd',
        compiler_params=_params(('arbitrary',), 48))(dcat, *os_, *lses, g, head_ones)
    return outs[0:3], outs[3:6], outs[6]


def _ssm_disc(ar, ai, ldt):
    dt = jnp.exp(ldt)
    zr, zi = ar * dt, ai * dt
    ez = jnp.exp(zr)
    A_r, A_i = ez * jnp.cos(zi), ez * jnp.sin(zi)
    den = ar * ar + ai * ai
    xr, xi = A_r - 1.0, A_i
    cr = (xr * ar + xi * ai) / den
    ci = (xi * ar - xr * ai) / den
    return dt, zr, zi, A_r, A_i, den, cr, ci


def _ssm_pre(ar, ai, ldt, br, bi):
    def body(ar_ref, ai_ref, ldt_ref, br_ref, bi_ref, bbr_ref, bbi_ref, pwr_ref, pwi_ref):
        _, zr, zi, _, _, _, cr, ci = _ssm_disc(ar_ref[...], ai_ref[...], ldt_ref[...])
        bbr_ref[...] = cr * br_ref[...] - ci * bi_ref[...]
        bbi_ref[...] = cr * bi_ref[...] + ci * br_ref[...]
        k = (lax.broadcasted_iota(jnp.int32, (1, 8), 1) + 1).astype(F32)
        ek = jnp.exp(zr * k)
        pwr_ref[...] = ek * jnp.cos(zi * k)
        pwi_ref[...] = ek * jnp.sin(zi * k)

    s16 = jax.ShapeDtypeStruct((SSM_GN, SSM_P), F32)
    s8 = jax.ShapeDtypeStruct((SSM_GN, 8), F32)
    return pl.pallas_call(body, out_shape=(s16, s16, s8, s8), name='ssm_pre',
                          compiler_params=_params(None, 40))(ar, ai, ldt, br, bi)


def _ssm_post(ar, ai, ldt, br, bi, gar, gai, gbr, gbi, sel):
    def body(ar_ref, ai_ref, ldt_ref, br_ref, bi_ref, gar_ref, gai_ref, gbr_ref, gbi_ref, sel_ref,
             dar_ref, dai_ref, dbr_ref, dbi_ref, dldt_ref):
        a_r, a_i = ar_ref[...], ai_ref[...]
        dt, _, _, A_r, A_i, den, cr, ci = _ssm_disc(a_r, a_i, ldt_ref[...])
        b_r, b_i, g_br, g_bi = br_ref[...], bi_ref[...], gbr_ref[...], gbi_ref[...]
        gcr = jnp.sum(g_br * b_r + g_bi * b_i, axis=-1, keepdims=True)
        gci = jnp.sum(g_bi * b_r - g_br * b_i, axis=-1, keepdims=True)
        dbr_ref[...] = g_br * cr + g_bi * ci
        dbi_ref[...] = g_bi * cr - g_br * ci
        g_ar = gar_ref[...] + (gcr * a_r - gci * a_i) / den
        g_ai = gai_ref[...] + (gcr * a_i + gci * a_r) / den
        qr = (cr * a_r + ci * a_i) / den
        qi = (ci * a_r - cr * a_i) / den
        glr = -(gcr * qr + gci * qi)
        gli = -(gci * qr - gcr * qi)
        gzr = g_ar * A_r + g_ai * A_i
        gzi = g_ai * A_r - g_ar * A_i
        dar_ref[...] = glr + gzr * dt
        dai_ref[...] = gli + gzi * dt
        gdt = (gzr * a_r + gzi * a_i) * dt
        dldt_ref[...] = jnp.dot(sel_ref[...], jnp.broadcast_to(gdt, (SSM_GN, 128)),
                                preferred_element_type=F32, precision=HI)

    s1 = jax.ShapeDtypeStruct((SSM_GN, 1), F32)
    s16 = jax.ShapeDtypeStruct((SSM_GN, SSM_P), F32)
    return pl.pallas_call(body, out_shape=(s1, s1, s16, s16, jax.ShapeDtypeStruct((SSM_G, 128), F32)),
                          name='ssm_post', compiler_params=_params(None, 48))(
                              ar, ai, ldt, br, bi, gar, gai, gbr, gbi, sel)


SCAN_CH = 8


def _scan_fwd_tiles(s_ref, pw, carry):
    pwr, pwi = pw[:, :CL_S], pw[:, CL_S:]
    row = lax.broadcasted_iota(jnp.int32, (8, CL_S), 0)
    steps = [(k, jnp.where(row >= k, pwr[k - 1:k], 0.0), jnp.where(row >= k, pwi[k - 1:k], 0.0)) for k in (1, 2, 4)]
    rows = 8 * SCAN_CH

    def chunk(i, c):
        cr, ci = c
        r0 = pl.multiple_of(i * rows, rows)
        xr = s_ref[pl.ds(r0, rows), 0:CL_S].reshape(SCAN_CH, 8, CL_S)
        xi = s_ref[pl.ds(r0, rows), CL_S:2 * CL_S].reshape(SCAN_CH, 8, CL_S)
        for k, pr, pi in steps:
            sr, si = pltpu.roll(xr, k, 1), pltpu.roll(xi, k, 1)
            xr, xi = xr + pr * sr - pi * si, xi + pr * si + pi * sr
        for j in range(SCAN_CH):
            tr = xr[j] + pwr * cr - pwi * ci
            ti = xi[j] + pwr * ci + pwi * cr
            s_ref[pl.ds(r0 + 8 * j, 8), 0:CL_S] = tr
            s_ref[pl.ds(r0 + 8 * j, 8), CL_S:2 * CL_S] = ti
            cr, ci = tr[7:8], ti[7:8]
        return cr, ci

    return lax.fori_loop(0, T_SCAN // rows, chunk, (carry[:, :CL_S], carry[:, CL_S:]))


def _scan_bwd_tiles(l_ref, pw, carry):
    pwr, pwi = pw[:, :CL_S], pw[:, CL_S:]
    rpr = jnp.concatenate([pwr[7 - r:8 - r] for r in range(8)], axis=0)
    rpi = jnp.concatenate([pwi[7 - r:8 - r] for r in range(8)], axis=0)
    row = lax.broadcasted_iota(jnp.int32, (8, CL_S), 0)
    steps = [(k, jnp.where(row < 8 - k, pwr[k - 1:k], 0.0), jnp.where(row < 8 - k, pwi[k - 1:k], 0.0))
             for k in (1, 2, 4)]
    rows = 8 * SCAN_CH
    nc = T_SCAN // rows

    def chunk(i, c):
        cr, ci = c
        r0 = pl.multiple_of((nc - 1 - i) * rows, rows)
        xr = l_ref[pl.ds(r0, rows), 0:CL_S].reshape(SCAN_CH, 8, CL_S)
        xi = l_ref[pl.ds(r0, rows), CL_S:2 * CL_S].reshape(SCAN_CH, 8, CL_S)
        for k, pr, pi in steps:
            sr, si = pltpu.roll(xr, 8 - k, 1), pltpu.roll(xi, 8 - k, 1)
            xr, xi = xr + pr * sr + pi * si, xi + pr * si - pi * sr
        for j in reversed(range(SCAN_CH)):
            tr = xr[j] + rpr * cr + rpi * ci
            ti = xi[j] + rpr * ci - rpi * cr
            l_ref[pl.ds(r0 + 8 * j, 8), 0:CL_S] = tr
            l_ref[pl.ds(r0 + 8 * j, 8), CL_S:2 * CL_S] = ti
            cr, ci = tr[0:1], ti[0:1]
        return cr, ci

    return lax.fori_loop(0, nc, chunk, (carry[:, :CL_S], carry[:, CL_S:]))


NT_SCAN = L // T_SCAN


def _hilo(t):
    hi = t.astype(BF16)
    return jnp.stack([hi, (t - hi.astype(F32)).astype(BF16)], axis=1)


def _dot3(a, b_ref):
    ah = a.astype(BF16)
    al = (a - ah.astype(F32)).astype(BF16)
    bh, bl = b_ref[0], b_ref[1]
    return (jnp.dot(ah, bh, preferred_element_type=F32) + jnp.dot(al, bh, preferred_element_type=F32)
            + jnp.dot(ah, bl, preferred_element_type=F32))


def _hl_spec(r, c):
    return pl.BlockSpec((None, 2, r, c), lambda c_, t: (c_, 0, 0, 0))


def _ssm_fwd(u, bm2, cm2, pw, dvec, u_off=0):
    def body(u_ref, bm_ref, cm_ref, pw_ref, d_ref, y_ref, bnd_ref, s_ref, carry_ref):
        @pl.when(pl.program_id(1) == 0)
        def _():
            carry_ref[...] = jnp.zeros_like(carry_ref)

        bnd_ref[...] = carry_ref[...]
        uv = u_ref[...]
        s_ref[...] = jnp.dot(uv.astype(BF16), bm_ref[0], preferred_element_type=F32)
        cr, ci = _scan_fwd_tiles(s_ref, pw_ref[...], carry_ref[...])
        carry_ref[...] = jnp.concatenate([cr, ci], axis=1)
        y_ref[...] = jnp.dot(s_ref[...].astype(BF16), cm_ref[0], preferred_element_type=F32) + d_ref[...] * uv

    return pl.pallas_call(
        body,
        out_shape=(jax.ShapeDtypeStruct((L, SSMW), F32), jax.ShapeDtypeStruct((N_CL, NT_SCAN, 1, 2 * CL_S), F32),
                   jax.ShapeDtypeStruct((L, N_CL * 2 * CL_S), F32)),
        grid=(N_CL, NT_SCAN),
        in_specs=[pl.BlockSpec((T_SCAN, CL_U), lambda c, t: (t, c + u_off)),
                  _hl_spec(CL_U, 2 * CL_S), _hl_spec(2 * CL_S, CL_U),
                  pl.BlockSpec((None, 8, 2 * CL_S), lambda c, t: (c, 0, 0)),
                  pl.BlockSpec((1, CL_U), lambda c, t: (0, c))],
        out_specs=(pl.BlockSpec((T_SCAN, CL_U), lambda c, t: (t, c)),
                   pl.BlockSpec((None, None, 1, 2 * CL_S), lambda c, t: (c, t, 0, 0)),
                   pl.BlockSpec((T_SCAN, 2 * CL_S), lambda c, t: (t, c))),
        scratch_shapes=[pltpu.VMEM((1, 2 * CL_S), F32)],
        name='ssm_fwd', compiler_params=_params(('arbitrary', 'arbitrary'), 40))(u, bm2, cm2, pw, dvec)


def _ssm_bwd(u, dy, states, bmt, cmt, pw, dvec, bnd, u_off=0):
    rev = lambda t: NT_SCAN - 1 - t

    def body(u_ref, dy_ref, s_ref, bmt_ref, cmt_ref, pw_ref, d_ref, bnd_ref,
             du_ref, dbm_ref, dcm_ref, da_ref, dd_ref, l_ref, carry_ref):
        @pl.when(pl.program_id(1) == 0)
        def _():
            carry_ref[...] = jnp.zeros_like(carry_ref)
            dbm_ref[...] = jnp.zeros_like(dbm_ref)
            dcm_ref[...] = jnp.zeros_like(dcm_ref)
            da_ref[...] = jnp.zeros_like(da_ref)
            dd_ref[...] = jnp.zeros_like(dd_ref)

        uv, dyv, pw = u_ref[...], dy_ref[...], pw_ref[...]
        dy_b = dyv.astype(BF16)
        entry = bnd_ref[...]
        l_ref[...] = jnp.dot(dy_b, cmt_ref[...], preferred_element_type=F32)
        cr, ci = _scan_bwd_tiles(l_ref, pw, carry_ref[...])
        carry_ref[...] = jnp.concatenate([cr, ci], axis=1)
        sv, lv = s_ref[...], l_ref[...]
        lv_b = lv.astype(BF16)
        du_ref[...] = dyv * d_ref[...] + jnp.dot(lv_b, bmt_ref[...], preferred_element_type=F32)
        dbm_ref[...] += lax.dot_general(uv.astype(BF16), lv_b, TN_DIMS, preferred_element_type=F32)
        dcm_ref[...] += lax.dot_general(sv.astype(BF16), dy_b, TN_DIMS, preferred_element_type=F32)
        dd_ref[...] += _colsum(dyv * uv)
        row = lax.broadcasted_iota(jnp.int32, (T_SCAN, 2 * CL_S), 0)
        sp = jnp.where(row == 0, entry, pltpu.roll(sv, 1, 0))
        spr, spi = sp[:, :CL_S], sp[:, CL_S:]
        lr, li = lv[:, :CL_S], lv[:, CL_S:]
        da_ref[:, 0:CL_S] += _colsum(lr * spr + li * spi)
        da_ref[:, CL_S:2 * CL_S] += _colsum(li * spr - lr * spi)
        _fold8(pl.program_id(1) == NT_SCAN - 1, da_ref, dd_ref)

    return pl.pallas_call(
        body,
        out_shape=(jax.ShapeDtypeStruct((L, SSMW), F32), jax.ShapeDtypeStruct((N_CL, CL_U, 2 * CL_S), F32),
                   jax.ShapeDtypeStruct((N_CL, 2 * CL_S, CL_U), F32), jax.ShapeDtypeStruct((N_CL, 8, 2 * CL_S), F32),
                   jax.ShapeDtypeStruct((8, SSMW), F32)),
        grid=(N_CL, NT_SCAN),
        in_specs=[pl.BlockSpec((T_SCAN, CL_U), lambda c, t: (rev(t), c + u_off)),
                  pl.BlockSpec((T_SCAN, CL_U), lambda c, t: (rev(t), c)),
                  pl.BlockSpec((T_SCAN, 2 * CL_S), lambda c, t: (rev(t), c)),
                  pl.BlockSpec((None, 2 * CL_S, CL_U), lambda c, t: (c, 0, 0)),
                  pl.BlockSpec((None, CL_U, 2 * CL_S), lambda c, t: (c, 0, 0)),
                  pl.BlockSpec((None, 8, 2 * CL_S), lambda c, t: (c, 0, 0)),
                  pl.BlockSpec((1, CL_U), lambda c, t: (0, c)),
                  pl.BlockSpec((None, None, 1, 2 * CL_S), lambda c, t: (c, rev(t), 0, 0))],
        out_specs=(pl.BlockSpec((T_SCAN, CL_U), lambda c, t: (rev(t), c)),
                   pl.BlockSpec((None, CL_U, 2 * CL_S), lambda c, t: (c, 0, 0)),
                   pl.BlockSpec((None, 2 * CL_S, CL_U), lambda c, t: (c, 0, 0)),
                   pl.BlockSpec((None, 8, 2 * CL_S), lambda c, t: (c, 0, 0)),
                   pl.BlockSpec((8, CL_U), lambda c, t: (0, c))),
        scratch_shapes=[pltpu.VMEM((T_SCAN, 2 * CL_S), F32), pltpu.VMEM((1, 2 * CL_S), F32)],
        name='ssm_bwd', compiler_params=_params(('arbitrary', 'arbitrary'), 48))(u, dy, states, bmt, cmt, pw, dvec, bnd)


GELU_C = math.sqrt(2.0 / math.pi)
GELU_K = 0.044715


def _gelu_parts(x):
    t = jnp.tanh(GELU_C * (x + GELU_K * (x * x * x)))
    return x * (0.5 * (1.0 + t)), t


def _glu_fwd(ypre, wglu, bglu, gs):
    def body(y_ref, w_ref, b_ref, g_ref, o_ref):
        yg, _ = _gelu_parts(y_ref[...])
        z = jnp.dot(yg.astype(BF16), w_ref[...], preferred_element_type=F32) + b_ref[...]
        s = yg * jax.nn.sigmoid(z)
        o_ref[...] = ((s * _rms(s)) * g_ref[...]).astype(BF16)

    return pl.pallas_call(
        body, out_shape=jax.ShapeDtypeStruct((L, SSMW), BF16), grid=(L // TR,),
        in_specs=[_rowspec(SSMW), pl.BlockSpec((SSMW, SSMW), lambda i: (0, 0)), _vec(SSMW), _vec(SSMW)],
        out_specs=_rowspec(SSMW), name='glu_fwd', compiler_params=_params(('parallel',), 32))(ypre, wglu, bglu, gs)


def _glu_bwd(ypre, dsn, wglu, bglu, gs):
    def body(y_ref, d_ref, w_ref, b_ref, g_ref, dy_ref, dw_ref, db_ref, dg_ref):
        @pl.when(pl.program_id(0) == 0)
        def _():
            dw_ref[...] = jnp.zeros_like(dw_ref)
            db_ref[...] = jnp.zeros_like(db_ref)
            dg_ref[...] = jnp.zeros_like(dg_ref)

        xv = y_ref[...]
        yg, t = _gelu_parts(xv)
        yg_b = yg.astype(BF16)
        z = jnp.dot(yg_b, w_ref[...], preferred_element_type=F32) + b_ref[...]
        sg = jax.nn.sigmoid(z)
        s = yg * sg
        r = _rms(s)
        n = s * r
        dv = d_ref[...]
        dg_ref[...] += _colsum(dv * n)
        ds = _rms_bwd(dv * g_ref[...], n, r)
        dz = (ds * yg) * (sg * (1.0 - sg))
        dz_b = dz.astype(BF16)
        db_ref[...] += _colsum(dz)
        dw_ref[...] += lax.dot_general(yg_b, dz_b, TN_DIMS, preferred_element_type=F32)
        dyg = ds * sg + lax.dot_general(dz_b, w_ref[...], NT_DIMS, preferred_element_type=F32)
        dgelu = 0.5 * (1.0 + t) + (0.5 * xv) * (1.0 - t * t) * (GELU_C * (1.0 + 3.0 * GELU_K * (xv * xv)))
        dy_ref[...] = dyg * dgelu
        _fold8(pl.program_id(0) == L // TR - 1, db_ref, dg_ref)

    vs = jax.ShapeDtypeStruct((8, SSMW), F32)
    return pl.pallas_call(
        body, out_shape=(jax.ShapeDtypeStruct((L, SSMW), F32), jax.ShapeDtypeStruct((SSMW, SSMW), F32), vs, vs),
        grid=(L // TR,),
        in_specs=[_rowspec(SSMW), _rowspec(SSMW), pl.BlockSpec((SSMW, SSMW), lambda i: (0, 0)), _vec(SSMW), _vec(SSMW)],
        out_specs=(_rowspec(SSMW), pl.BlockSpec((SSMW, SSMW), lambda i: (0, 0)), _vec8(SSMW), _vec8(SSMW)),
        name='glu_bwd', compiler_params=_params(('arbitrary',), 40))(ypre, dsn, wglu, bglu, gs)


def _me():
    return lax.axis_index('x'), lax.axis_index('y'), lax.axis_index('c')


def _my_index():
    return 4 * lax.axis_index('x') + 2 * lax.axis_index('y') + lax.axis_index('c')


def _peer(k):
    x, y, c = _me()
    px = 1 - x if k & 4 else x
    py = 1 - y if k & 2 else y
    pc = 1 - c if k & 1 else c
    return (px, py, pc), 4 * px + 2 * py + pc


def _mod_exchange(c_row, w_ada, b_ada8, deps=()):
    cw = NMOD * D // N_DEV

    def body(c_ref, w_ref, b_ref, *rest):
        call_ref, mod_ref, part_ref, send_sems, recv_sems = rest[len(deps):]
        x, y, c = _me()
        me = 4 * x + 2 * y + c
        call_ref[me] = c_ref[0]
        sends = []
        for k in range(1, N_DEV):
            peer, _ = _peer(k)
            cp = pltpu.make_async_remote_copy(src_ref=c_ref.at[0], dst_ref=call_ref.at[me], send_sem=send_sems.at[0, k - 1],
                                              recv_sem=recv_sems.at[0, k - 1], device_id=peer, device_id_type=MESH)
            cp.start()
            sends.append(cp)
        for k in range(1, N_DEV):
            peer, pidx = _peer(k)
            pltpu.make_async_remote_copy(src_ref=c_ref.at[0], dst_ref=call_ref.at[pidx], send_sem=send_sems.at[0, k - 1],
                                         recv_sem=recv_sems.at[0, k - 1], device_id=peer, device_id_type=MESH).wait_recv()
        for cp in sends:
            cp.wait_send()
        cv = call_ref[...].reshape(N_DEV, D)
        part = jnp.dot(cv * jax.nn.sigmoid(cv), w_ref[...], preferred_element_type=F32, precision=HI)
        part_ref[...] = part.reshape(N_DEV, 1, cw)
        mod_ref[me] = part_ref[me]
        sends = []
        for k in range(1, N_DEV):
            peer, pidx = _peer(k)
            cp = pltpu.make_async_remote_copy(src_ref=part_ref.at[pidx], dst_ref=mod_ref.at[me], send_sem=send_sems.at[1, k - 1],
                                              recv_sem=recv_sems.at[1, k - 1], device_id=peer, device_id_type=MESH)
            cp.start()
            sends.append(cp)
        for k in range(1, N_DEV):
            peer, pidx = _peer(k)
            pltpu.make_async_remote_copy(src_ref=part_ref.at[pidx], dst_ref=mod_ref.at[pidx], send_sem=send_sems.at[1, k - 1],
                                         recv_sem=recv_sems.at[1, k - 1], device_id=peer, device_id_type=MESH).wait_recv()
        for cp in sends:
            cp.wait_send()
        mod_ref[...] = mod_ref[...] + b_ref[...]

    vm = pl.BlockSpec(memory_space=pltpu.VMEM)
    return pl.pallas_call(
        body, out_shape=(jax.ShapeDtypeStruct((N_DEV, 1, D), F32), jax.ShapeDtypeStruct((N_DEV, 1, cw), F32)),
        in_specs=[vm, vm, vm] + [pl.BlockSpec(memory_space=pl.ANY)] * len(deps), out_specs=(vm, vm),
        scratch_shapes=[pltpu.VMEM((N_DEV, 1, cw), F32), pltpu.SemaphoreType.DMA((2, N_DEV - 1)),
                        pltpu.SemaphoreType.DMA((2, N_DEV - 1))],
        name='mod_exchange', compiler_params=_params(None, 48))(c_row, w_ada, b_ada8, *deps)


HBM_SPEC = pl.BlockSpec(memory_space=pltpu.HBM)
SEM_SPEC = pl.BlockSpec(memory_space=pltpu.SEMAPHORE)
DATAFLOW = pltpu.SideEffectType.DATAFLOW_SIDE_EFFECTING


def _push_start(src, scatter, after, name):
    land = lax.empty(src.shape if scatter else (N_DEV,) + src.shape, src.dtype)

    def body(src_ref, land_ref, after_ref, send_sem, recv_sem, land_thru, token):
        x, y, c = _me()
        me = 4 * x + 2 * y + c
        for k in range(1, N_DEV):
            peer, pidx = _peer(k)
            pltpu.make_async_remote_copy(src_ref=src_ref.at[pidx] if scatter else src_ref, dst_ref=land_ref.at[me],
                                         send_sem=send_sem, recv_sem=recv_sem, device_id=peer,
                                         device_id_type=MESH).start()
        token[...] = jnp.zeros_like(token)

    own = lax.dynamic_index_in_dim(src, _my_index(), 0, keepdims=False) if scatter else src
    src = pltpu.with_memory_space_constraint(src, pltpu.HBM)
    send_sem, recv_sem, land_thru, token = pl.pallas_call(
        body, name=name,
        out_shape=(pltpu.SemaphoreType.DMA(()), pltpu.SemaphoreType.DMA(()),
                   pltpu.HBM(land.shape, land.dtype), jax.ShapeDtypeStruct((8, 128), F32)),
        in_specs=(HBM_SPEC, HBM_SPEC, pl.BlockSpec(memory_space=pl.ANY)),
        out_specs=(SEM_SPEC, SEM_SPEC, HBM_SPEC, pl.BlockSpec(memory_space=pltpu.VMEM)),
        input_output_aliases={1: 2}, compiler_params=pltpu.CompilerParams(has_side_effects=DATAFLOW),
    )(src, pltpu.with_memory_space_constraint(land, pltpu.HBM), after)
    return send_sem, recv_sem, src, land_thru, token, own


def _push_wait(handle, after, name):
    send_sem, recv_sem, src, land_thru, _, own = handle
    after = tuple(after) if isinstance(after, (tuple, list)) else (after,)

    def body(src_ref, land_ref, send_sem, recv_sem, *rest):
        seven = land_ref.at[pl.ds(0, N_DEV - 1)]
        cp = pltpu.make_async_remote_copy(src_ref=seven, dst_ref=seven, send_sem=send_sem, recv_sem=recv_sem,
                                          device_id=_me(), device_id_type=MESH)
        cp.wait_send()
        cp.wait_recv()

    landed = pl.pallas_call(
        body, name=name, out_shape=pltpu.HBM(land_thru.shape, land_thru.dtype),
        in_specs=(HBM_SPEC, HBM_SPEC, SEM_SPEC, SEM_SPEC) + (pl.BlockSpec(memory_space=pl.ANY),) * len(after),
        out_specs=HBM_SPEC, input_output_aliases={1: 0},
        compiler_params=pltpu.CompilerParams(has_side_effects=DATAFLOW),
    )(src, land_thru, send_sem, recv_sem, *after)
    return lax.dynamic_update_index_in_dim(landed, own, _my_index(), 0)


def _adam(w, g, m, v):
    m2 = B1 * m + (1.0 - B1) * g
    v2 = B2 * v + (1.0 - B2) * jnp.square(g)
    m_hat = m2 / (1.0 - B1 ** STEP)
    v_hat = v2 / (1.0 - B2 ** STEP)
    delta = -LR * (m_hat / (jnp.sqrt(v_hat) + AEPS) + WD * w)
    return delta, m2, v2


def _small_update(gp, wp, mp, vp):
    def body(g_ref, w_ref, m_ref, v_ref, all_ref, go_ref, d_ref, mo_ref, vo_ref, send_sems, recv_sems):
        x, y, c = _me()
        me = 4 * x + 2 * y + c
        all_ref[me] = g_ref[...]
        sends = []
        for k in range(1, N_DEV):
            peer, _ = _peer(k)
            cp = pltpu.make_async_remote_copy(src_ref=g_ref, dst_ref=all_ref.at[me], send_sem=send_sems.at[k - 1],
                                              recv_sem=recv_sems.at[k - 1], device_id=peer, device_id_type=MESH)
            cp.start()
            sends.append(cp)
        for k in range(1, N_DEV):
            peer, pidx = _peer(k)
            pltpu.make_async_remote_copy(src_ref=g_ref, dst_ref=all_ref.at[pidx], send_sem=send_sems.at[k - 1],
                                         recv_sem=recv_sems.at[k - 1], device_id=peer, device_id_type=MESH).wait_recv()
        for cp in sends:
            cp.wait_send()
        g = all_ref[0]
        for d in range(1, N_DEV):
            g = g + all_ref[d]
        delta, m2, v2 = _adam(w_ref[...], g, m_ref[...], v_ref[...])
        go_ref[...] = g
        d_ref[...] = delta
        mo_ref[...] = m2
        vo_ref[...] = v2

    vm = pl.BlockSpec(memory_space=pltpu.VMEM)
    vs = jax.ShapeDtypeStruct(gp.shape, F32)
    return pl.pallas_call(
        body, out_shape=(jax.ShapeDtypeStruct((N_DEV,) + gp.shape, F32), vs, vs, vs, vs), in_specs=[vm] * 4,
        out_specs=(vm,) * 5,
        scratch_shapes=[pltpu.SemaphoreType.DMA((N_DEV - 1,)), pltpu.SemaphoreType.DMA((N_DEV - 1,))],
        name='small_update', compiler_params=_params(None, 48))(gp, wp, mp, vp)


def _small_sum_update(parts, wp, mp, vp):
    def body(p_ref, w_ref, m_ref, v_ref, go_ref, d_ref, mo_ref, vo_ref):
        g = p_ref[0]
        for d in range(1, N_DEV):
            g = g + p_ref[d]
        delta, m2, v2 = _adam(w_ref[...], g, m_ref[...], v_ref[...])
        go_ref[...] = g
        d_ref[...] = delta
        mo_ref[...] = m2
        vo_ref[...] = v2

    vm = pl.BlockSpec(memory_space=pltpu.VMEM)
    vs = jax.ShapeDtypeStruct(wp.shape, F32)
    return pl.pallas_call(body, out_shape=(vs, vs, vs, vs), in_specs=[vm] * 4, out_specs=(vm,) * 4,
                          name='small_sum_update', compiler_params=_params(None, 48))(parts, wp, mp, vp)


def _big_update(parts, w, m, v, name):
    _, R, C = parts.shape
    tr = R if R % 256 else (128 if C >= 2048 else 256)

    def body(p_ref, w_ref, m_ref, v_ref, g_ref, d_ref, mo_ref, vo_ref):
        g = p_ref[0].astype(F32)
        for d in range(1, N_DEV):
            g = g + p_ref[d].astype(F32)
        delta, m2, v2 = _adam(w_ref[...], g, m_ref[...], v_ref[...])
        g_ref[...] = g
        d_ref[...] = delta
        mo_ref[...] = m2
        vo_ref[...] = v2

    blk = pl.BlockSpec((tr, C), lambda i: (i, 0))
    sh = jax.ShapeDtypeStruct((R, C), F32)
    return pl.pallas_call(
        body, out_shape=(sh, sh, sh, sh), grid=(R // tr,),
        in_specs=[pl.BlockSpec((N_DEV, tr, C), lambda i: (0, i, 0)), blk, blk, blk], out_specs=(blk,) * 4,
        name=name, compiler_params=_params(('parallel',), 48))(parts, w, m, v)


def _ada_update(c_all, dmod_cols, w, m, v):
    C = w.shape[1]
    tr = 256

    def body(c_ref, dm_ref, w_ref, m_ref, v_ref, g_ref, d_ref, mo_ref, vo_ref):
        cv = c_ref[...]
        s = cv * jax.nn.sigmoid(cv)
        g = lax.dot_general(s, dm_ref[...], TN_DIMS, preferred_element_type=F32, precision=HI)
        delta, m2, v2 = _adam(w_ref[...], g, m_ref[...], v_ref[...])
        g_ref[...] = g
        d_ref[...] = delta
        mo_ref[...] = m2
        vo_ref[...] = v2

    blk = pl.BlockSpec((tr, C), lambda i: (i, 0))
    sh = jax.ShapeDtypeStruct((D, C), F32)
    return pl.pallas_call(
        body, out_shape=(sh, sh, sh, sh), grid=(D // tr,),
        in_specs=[pl.BlockSpec((N_DEV, tr), lambda i: (0, i)), pl.BlockSpec((N_DEV, C), lambda i: (0, 0)), blk, blk, blk],
        out_specs=(blk,) * 4, name='ada_update', compiler_params=_params(('parallel',), 48))(c_all, dmod_cols, w, m, v)


def _to_sub(t, d):
    if d == 1:
        return t
    return t.reshape(L // d, d, t.shape[-1]).transpose(1, 0, 2).reshape(L, t.shape[-1])


def _from_sub(t, d):
    if d == 1:
        return t
    return t.reshape(d, L // d, t.shape[-1]).transpose(1, 0, 2).reshape(L, t.shape[-1])


def _rows_to_cluster_lanes(t):
    k = t.shape[1]
    return t.reshape(N_CL, CL_S, k).transpose(0, 2, 1)


def _blockdiag_in(t):
    t = t.reshape(N_CL, CL_G, SSM_N, SSM_P).transpose(0, 1, 3, 2)
    eye = jnp.eye(CL_G, dtype=t.dtype)
    t = t[:, :, :, None, :] * eye[None, :, None, :, None]
    return t.reshape(N_CL, CL_U, CL_S)


def _blockdiag_extract(t):
    t = t.reshape(N_CL, CL_G, SSM_P, CL_G, SSM_N)
    t = jnp.stack([t[:, i, :, i, :] for i in range(CL_G)], axis=1)
    return t.transpose(0, 1, 3, 2).reshape(SSM_GN, SSM_P)


def _c_to_rows(t):
    return t.transpose(0, 2, 1).reshape(SSM_GN, SSM_P)


def _rows_to_c(t):
    return t.reshape(SSM_G, SSM_N, SSM_P).transpose(0, 2, 1)


def _ssm_prep(sp):
    rows = lambda n: sp[n].reshape(SSM_GN, 1)
    a_re, a_im = rows('ssm_a_re'), rows('ssm_a_im')
    ldt = jnp.repeat(sp['ssm_log_dt'].reshape(SSM_G, 1), SSM_N, axis=0)
    b_re, b_im = sp['ssm_b_re'].reshape(SSM_GN, SSM_P), sp['ssm_b_im'].reshape(SSM_GN, SSM_P)
    c_re, c_im = _c_to_rows(sp['ssm_c_re'].reshape(SSM_G, SSM_P, SSM_N)), _c_to_rows(sp['ssm_c_im'].reshape(SSM_G, SSM_P, SSM_N))
    bbr, bbi, pwr, pwi = _ssm_pre(a_re, a_im, ldt, b_re, b_im)
    bm = jnp.concatenate([_blockdiag_in(bbr), _blockdiag_in(bbi)], axis=2)
    cmt = jnp.concatenate([_blockdiag_in(c_re), -_blockdiag_in(c_im)], axis=2)
    bmt, cm = bm.transpose(0, 2, 1), cmt.transpose(0, 2, 1)
    pw = jnp.concatenate([_rows_to_cluster_lanes(pwr), _rows_to_cluster_lanes(pwi)], axis=2)
    bm2, cm2 = _hilo(bm), _hilo(cm)
    bmt_b, cmt_b = bmt.astype(BF16), cmt.astype(BF16)
    return a_re, a_im, ldt, b_re, b_im, bm2, cm2, bmt_b, cmt_b, pw


def _tied(v, deps):
    for t in deps:
        v = v + t[0, 0]
    return v


def _local_step(x, pos, mod, tgt, sp, prep, get_w, emit, emit_small, emit_late, first_deps=()):
    sh1, sc1, gt1, sh2, sc2, gt2 = (mod[i:i + 1] for i in range(NMOD))
    vec = lambda n: sp[n].reshape(1, -1)
    a_re, a_im, ldt, b_re, b_im, bm2, cm2, bmt_b, cmt_b, pw = prep
    dvec = vec('ssm_d')

    h1 = _prenorm_fwd(x, vec('g_pre_mix'), sc1, sh1)
    w_in = get_w('w_in', (h1, bm2, cm2, pw, bmt_b, cmt_b))
    proj = _mm(h1, w_in, mode='nn', name='mm_in', tn=1408, deps=first_deps)
    fr1 =ROPE_THETA ** (-jnp.arange(0, ROT_DIM, 2, dtype=F32) / ROT_DIM)
    lane = jnp.arange(128) % HEAD_DIM
    fr = jnp.where(lane < ROT_DIM, fr1[lane % (ROT_DIM // 2)], 0.0).reshape(1, 128).astype(F32)
    u_off = (ROPE_W + KVW) // CL_U
    qkvs = _rope_fwd(proj, pos, fr)
    fwd = [_attn_fwd(qkvs[g], d) for g, d in enumerate(DILATIONS)]
    os_, lses = [t[0] for t in fwd], [t[1] for t in fwd]
    att = _combine_fwd(os_, lses, vec('g_attn_out'))

    ypre, bnd, states = _ssm_fwd(proj, bm2, cm2, pw, dvec, u_off)
    w_glu = get_w('w_glu', ypre)
    ssm_n = _glu_fwd(ypre, w_glu, vec('b_glu'), vec('g_ssm_out'))

    cat = jnp.concatenate([att, ssm_n], axis=1)
    w_out = get_w('w_out', cat)
    mix = _mm(cat, w_out, mode='nn', name='mm_out', tk=1280)
    x1, h2 = _postmix_fwd(x, mix, vec('g_post_mix'), gt1, vec('g_pre_mlp'), sc2, sh2)
    w_mi = get_w('w_mlp_in', h2)
    a_pre, r_act = _mm(h2, w_mi, mode='nn', name='mm_mlp_in', epilogue='relu2', b_sharded=True)
    w_mo = get_w('w_mlp_out', a_pre)
    y = _mm(r_act, w_mo, mode='nn', name='mm_mlp_out')
    dx2, dy, loss, dgt2, dg_post_mlp = _final_fwd_bwd(x1, y, tgt, vec('g_post_mlp'), gt2)
    dgt2, dg_post_mlp = dgt2[:1], dg_post_mlp[:1]

    da = _mm(dy, w_mo, mode='nt', name='mm_d_act', out_dtype=BF16, epilogue='drelu2', extra=a_pre)
    dep = emit('w_mlp_out', _mm(r_act, dy, mode='tn', name='mm_dw_mlp_out', out_dtype=BF16))
    dh2 = _mm(da, w_mi, mode='nt', name='mm_dh2', tk=1024, b_sharded=True, deps=dep)
    dep = emit('w_mlp_in', _mm(h2, da, mode='tn', name='mm_dw_mlp_in', out_dtype=BF16, out_sharded=True))
    dx1, dmix, dsc2, dsh2, dg_pre_mlp, dgt1, dg_post_mix = _postmix_bwd(
        dx2, dh2, x1, mix, vec('g_post_mix'), gt1, vec('g_pre_mlp'), sc2)
    dsc2, dsh2, dg_pre_mlp, dgt1, dg_post_mix = (t[:1] for t in (dsc2, dsh2, dg_pre_mlp, dgt1, dg_post_mix))
    dcat = _mm(dmix, w_out, mode='nt', name='mm_dcat', tn=1280, deps=dep)
    dep = emit('w_out', _mm(cat, dmix, mode='tn', name='mm_dw_out', out_dtype=BF16, tm=640))
    dsn = dcat[:, KVW:]

    dypre, g_w_glu, g_b_glu, g_g_ssm = _glu_bwd(ypre, dsn, w_glu, _tied(vec('b_glu'), dep), vec('g_ssm_out'))
    g_b_glu, g_g_ssm = g_b_glu[:1], g_g_ssm[:1]
    dep = dep + emit('w_glu', g_w_glu.astype(BF16))
    du, dbm, dcm, dA, dD = _ssm_bwd(proj, dypre, states, bmt_b, cmt_b, pw, dvec, bnd, u_off)
    dD = dD[:1]
    gbr, gbi = _blockdiag_extract(dbm[:, :, :CL_S]), _blockdiag_extract(dbm[:, :, CL_S:])
    dcmt = dcm.transpose(0, 2, 1)
    g_c_re = _rows_to_c(_blockdiag_extract(dcmt[:, :, :CL_S]))
    g_c_im = _rows_to_c(-_blockdiag_extract(dcmt[:, :, CL_S:]))
    gar = dA[:, 0, :CL_S].reshape(SSM_GN, 1)
    gai = dA[:, 0, CL_S:].reshape(SSM_GN, 1)
    sel = (jnp.arange(SSM_GN)[None, :] // SSM_N == jnp.arange(SSM_G)[:, None]).astype(F32)
    g_a_re, g_a_im, g_b_re, g_b_im, g_ldt = _ssm_post(a_re, a_im, ldt, b_re, b_im, gar, gai, gbr, gbi, sel)

    head_ones = (jnp.arange(KVW)[:, None] // HEAD_DIM == jnp.arange(KVW)[None, :] // HEAD_DIM).astype(F32)
    dos, dlses, g_g_attn = _combine_bwd(dcat, os_, lses, vec('g_attn_out'), head_ones)
    g_g_attn = g_g_attn[:1]
    dep_small = emit_small({
        'g_post_mix': dg_post_mix, 'ssm_a_re': g_a_re, 'ssm_a_im': g_a_im, 'ssm_log_dt': g_ldt[:, 0],
        'ssm_b_re': g_b_re, 'ssm_b_im': g_b_im, 'ssm_c_re': g_c_re, 'ssm_c_im': g_c_im, 'ssm_d': dD, 'b_glu': g_b_glu,
        'g_attn_out': g_g_attn, 'g_ssm_out': g_g_ssm, 'g_pre_mlp': dg_pre_mlp, 'g_post_mlp': dg_post_mlp})
    dqkv = [_attn_bwd(qkvs[g], os_[g], lses[g], dos[g], dlses[g], d) for g, d in enumerate(DILATIONS)]
    dproj = _rope_bwd(dqkv, du, pos, _tied(fr, dep_small))
    dh1 = _mm(dproj, w_in, mode='nt', name='mm_dh1', tk=1408, deps=dep)
    grad_x, dsc1, dsh1, dg_pre_mix = _prenorm_bwd(dx1, dh1, x, vec('g_pre_mix'), sc1)
    dsc1, dsh1, dg_pre_mix = dsc1[:1], dsh1[:1], dg_pre_mix[:1]
    dmod = jnp.concatenate([dsh1, dsc1, dgt1, dsh2, dsc2, dgt2], axis=0)
    dep = emit_late({'b_ada': dmod, 'g_pre_mix': dg_pre_mix})
    emit('w_in', _mm(h1, dproj, mode='tn', name='mm_dw_in', out_dtype=BF16, tn=1408, deps=dep))
    return loss[0, 0], grad_x


def _pack(d, names):
    flat = jnp.concatenate([jnp.pad(d[n].reshape(-1).astype(F32), (0, SEG[n] - SMALL_SIZES[n])) for n in names])
    return flat.reshape(-1, 128)


def _unpack(packed, names, shapes):
    out, off = {}, 0
    for n in names:
        out[n] = packed[off // 128:(off + SEG[n]) // 128].reshape(-1)[:SMALL_SIZES[n]].reshape(shapes[n])
        off += SEG[n]
    return out


def _shard_major(t, name):
    if name in ('w_in', 'w_out', 'w_mlp_in'):
        k, n = t.shape
        return t.reshape(k, N_DEV, n // N_DEV).transpose(1, 0, 2)
    k, n = t.shape
    return t.reshape(N_DEV, k // N_DEV, n)


def _from_shard_major(t, name):
    if name in ('w_in', 'w_out', 'w_mlp_in'):
        _, k, n = t.shape
        return t.transpose(1, 0, 2).reshape(k, N_DEV * n)
    _, k, n = t.shape
    return t.reshape(N_DEV * k, n)


def kernel(x, c, positions, w_ada, b_ada, g_pre_mix, g_post_mix, w_in, ssm_a_re, ssm_a_im, ssm_log_dt, ssm_b_re, ssm_b_im, ssm_c_re, ssm_c_im, ssm_d, w_glu, b_glu, g_attn_out, g_ssm_out, w_out, g_pre_mlp, g_post_mlp, w_mlp_in, w_mlp_out, loss_target, m_w_ada, m_b_ada, m_g_pre_mix, m_g_post_mix, m_w_in, m_ssm_a_re, m_ssm_a_im, m_ssm_log_dt, m_ssm_b_re, m_ssm_b_im, m_ssm_c_re, m_ssm_c_im, m_ssm_d, m_w_glu, m_b_glu, m_g_attn_out, m_g_ssm_out, m_w_out, m_g_pre_mlp, m_g_post_mlp, m_w_mlp_in, m_w_mlp_out, v_w_ada, v_b_ada, v_g_pre_mix, v_g_post_mix, v_w_in, v_ssm_a_re, v_ssm_a_im, v_ssm_log_dt, v_ssm_b_re, v_ssm_b_im, v_ssm_c_re, v_ssm_c_im, v_ssm_d, v_w_glu, v_b_glu, v_g_attn_out, v_g_ssm_out, v_w_out, v_g_pre_mlp, v_g_post_mlp, v_w_mlp_in, v_w_mlp_out):
    loc = dict(locals())
    W = {n: loc[n] for n in WEIGHTS}
    M = {n: loc['m_' + n] for n in WEIGHTS}
    V = {n: loc['v_' + n] for n in WEIGHTS}
    assert x.shape == (1, L, D) and w_in.shape == (1, D, INW // N_DEV), (x.shape, w_in.shape)

    cw = NMOD * D // N_DEV
    c_all, mod8 = _mod_exchange(c.reshape(1, 1, D), w_ada[0], b_ada.reshape(N_DEV, 1, cw))
    mod = mod8.reshape(NMOD, D)

    gather, after = {}, mod8
    for n in BIG:
        gather[n] = _push_start(W[n][0].astype(BF16), False, after, 'gather_start_' + n)
        after = gather[n][4]
    tokens = tuple(gather[n][4] for n in BIG)
    mod = _tied(mod, tokens)
    sp = {n: W[n][0] for n in SMALL}
    prep = _ssm_prep({**sp, 'ssm_a_re': _tied(sp['ssm_a_re'], tokens)})

    def get_w(n, after):
        g = _push_wait(gather[n], after, 'gather_wait_' + n)
        return g if n == 'w_mlp_in' else _from_shard_major(g, n)

    scatter = {}

    def emit(n, g):
        src = g if n == 'w_mlp_in' else _shard_major(g, n)
        scatter[n] = _push_start(src, True, src, 'scatter_start_' + n)
        return (scatter[n][4],)

    small_early = []

    def emit_small(d):
        pack = _pack(d, SMALL_EARLY)
        small_early.append(_push_start(pack, False, pack, 'small_start'))
        return (small_early[0][4],)

    out_g, out_d, out_m, out_v = {}, {}, {}, {}
    shapes = {n: W[n].shape[1:] for n in SMALL}

    def put(names, packs):
        for dst, packed in zip((out_g, out_d, out_m, out_v), packs):
            dst.update(_unpack(packed, names, shapes))

    late = []

    def emit_late(d):
        rows_all, *packs = _small_update(*[_pack(t, SMALL_LATE) for t in (d, W, M, V)])
        put(SMALL_LATE, packs)
        late.append(rows_all)
        return (rows_all,)

    loss, grad_x = _local_step(x[0], positions.reshape(L, 1), mod, loss_target[0], sp, prep, get_w, emit, emit_small,
                               emit_late)
    loss = lax.psum(loss, ('x', 'y', 'c'))

    me = 4 * lax.axis_index('x') + 2 * lax.axis_index('y') + lax.axis_index('c')
    dmod_all = late[0][:, :NMOD * D // 128].reshape(N_DEV, NMOD * D)
    dmod_cols = _tied(lax.dynamic_slice_in_dim(dmod_all, me * cw, cw, axis=1), (scatter['w_in'][4],))
    out_g['w_ada'], out_d['w_ada'], out_m['w_ada'], out_v['w_ada'] = _ada_update(
        c_all.reshape(N_DEV, D), dmod_cols, w_ada[0], m_w_ada[0], v_w_ada[0])

    parts = _push_wait(small_early[0], out_v['w_ada'], 'small_wait')
    packs = _small_sum_update(parts, *[_pack(d, SMALL_EARLY) for d in (W, M, V)])
    put(SMALL_EARLY, packs)

    after = packs[3]
    for n in ('w_mlp_out', 'w_mlp_in', 'w_out', 'w_glu', 'w_in'):
        parts = _push_wait(scatter[n], after, 'scatter_wait_' + n)
        out_g[n], out_d[n], out_m[n], out_v[n] = _big_update(parts, W[n][0], M[n][0], V[n][0], 'update_' + n)
        after = out_v[n]

    lead = lambda t: t[None]
    return (loss, grad_x[None], *[lead(out_g[n]) for n in WEIGHTS], *[lead(out_d[n]) for n in WEIGHTS],
            *[lead(out_m[n]) for n in WEIGHTS], *[lead(out_v[n]) for n in WEIGHTS])
```

```python
import functools
import math

import jax
import jax.numpy as jnp
from jax import lax
from jax.experimental import pallas as pl
from jax.experimental.pallas import tpu as pltpu

F32 = jnp.float32
BF16 = jnp.bfloat16
HI = lax.Precision.HIGHEST
MESH = pl.DeviceIdType.MESH

N_DEV = 8
L = 4096
D = 2048
HEAD_DIM = 64
N_GROUPS = 3
DILATIONS = (1, 4, 16)
HEADS = 6
QW = N_GROUPS * HEADS * HEAD_DIM
KVW = HEADS * HEAD_DIM
ROT_DIM = 16
ROPE_THETA = 500000.0
BLK = 128
NBLK = L // BLK
SSMW = D - QW
SSM_P = 16
SSM_G = SSMW // SSM_P
SSM_N = 64
SSM_GN = SSM_G * SSM_N
CL_G = 8
N_CL = SSM_G // CL_G
CL_U = CL_G * SSM_P
CL_S = CL_G * SSM_N
INW = QW + 2 * KVW + SSMW
OUTW = KVW + SSMW
DFF = 4 * D
NMOD = 6
EPS = 1e-6
LR, B1, B2, AEPS, WD, STEP = 0.001, 0.9, 0.999, 1e-08, 0.01, 10

T_SCAN = 512
MB = 2 ** 20

WEIGHTS = ['w_ada', 'b_ada', 'g_pre_mix', 'g_post_mix', 'w_in', 'ssm_a_re', 'ssm_a_im', 'ssm_log_dt',
           'ssm_b_re', 'ssm_b_im', 'ssm_c_re', 'ssm_c_im', 'ssm_d', 'w_glu', 'b_glu', 'g_attn_out',
           'g_ssm_out', 'w_out', 'g_pre_mlp', 'g_post_mlp', 'w_mlp_in', 'w_mlp_out']
BIG = ['w_in', 'w_glu', 'w_out', 'w_mlp_in', 'w_mlp_out']
SMALL = [n for n in WEIGHTS if n not in BIG and n != 'w_ada']
SMALL_SIZES = {'b_ada': NMOD * D, 'g_pre_mix': D, 'g_post_mix': D, 'ssm_a_re': SSM_GN, 'ssm_a_im': SSM_GN,
               'ssm_log_dt': SSM_G, 'ssm_b_re': SSM_GN * SSM_P, 'ssm_b_im': SSM_GN * SSM_P,
               'ssm_c_re': SSM_GN * SSM_P, 'ssm_c_im': SSM_GN * SSM_P, 'ssm_d': SSMW, 'b_glu': SSMW,
               'g_attn_out': KVW, 'g_ssm_out': SSMW, 'g_pre_mlp': D, 'g_post_mlp': D}
SEG = {n: -(-SMALL_SIZES[n] // 1024) * 1024 for n in SMALL}
SMALL_LATE = ['b_ada', 'g_pre_mix']
SMALL_EARLY = [n for n in SMALL if n not in SMALL_LATE]


def _params(sem=None, vmem_mb=None):
    kw = {}
    if sem is not None:
        kw['dimension_semantics'] = sem
    if vmem_mb is not None:
        kw['vmem_limit_bytes'] = vmem_mb * MB
    return pltpu.CompilerParams(**kw)


def _vec(n):
    return pl.BlockSpec((1, n), lambda *_: (0, 0))


def _rms(x):
    return lax.rsqrt(jnp.mean(x * x, axis=-1, keepdims=True) + EPS)


def _rms_bwd(dn, n, r):
    return r * (dn - n * jnp.mean(dn * n, axis=-1, keepdims=True))


def _vec8(n):
    return pl.BlockSpec((8, n), lambda *_: (0, 0))


def _colsum(x):
    return jnp.sum(x.reshape(-1, 8, x.shape[-1]), axis=0)


def _fold8(last, *refs):
    @pl.when(last)
    def _():
        for r in refs:
            r[...] = jnp.broadcast_to(jnp.sum(r[...], axis=0, keepdims=True), r.shape)


def _mm(a, b, *, mode, name, out_dtype=F32, tm=1024, tn=1024, tk=2048, epilogue=None, extra=None,
        b_sharded=False, out_sharded=False, deps=()):
    if mode == 'nn':
        M, K = a.shape
        dims = (((1,), (0,)), ((), ()))
        a_spec = pl.BlockSpec((tm, tk), lambda i, j, k: (i, k))
        if b_sharded:
            _, K2, per = b.shape
            N, q = N_DEV * per, per // tn
            b_spec = pl.BlockSpec((None, tk, tn), lambda i, j, k: (j // q, k, j % q))
        else:
            K2, N = b.shape
            b_spec = pl.BlockSpec((tk, tn), lambda i, j, k: (k, j))
    elif mode == 'nt':
        M, K = a.shape
        dims = (((1,), (1,)), ((), ()))
        a_spec = pl.BlockSpec((tm, tk), lambda i, j, k: (i, k))
        if b_sharded:
            _, N, per = b.shape
            K2, q = N_DEV * per, per // tk
            b_spec = pl.BlockSpec((None, tn, tk), lambda i, j, k: (k // q, j, k % q))
        else:
            N, K2 = b.shape
            b_spec = pl.BlockSpec((tn, tk), lambda i, j, k: (j, k))
    else:
        (K, M), (K2, N) = a.shape, b.shape
        dims = (((0,), (0,)), ((), ()))
        a_spec = pl.BlockSpec((tk, tm), lambda i, j, k: (k, i))
        b_spec = pl.BlockSpec((tk, tn), lambda i, j, k: (k, j))
    assert K == K2 and M % tm == 0 and N % tn == 0 and K % tk == 0, (name, a.shape, b.shape, tm, tn, tk)
    nk = K // tk
    o_spec = pl.BlockSpec((tm, tn), lambda i, j, k: (i, j))
    o_dims = (M, N)
    if out_sharded:
        qo = N // N_DEV // tn
        o_spec = pl.BlockSpec((None, tm, tn), lambda i, j, k: (j // qo, i, j % qo))
        o_dims = (N_DEV, M, N // N_DEV)
    n_out = 2 if epilogue == 'relu2' else 1
    n_extra = 1 if extra is not None else 0
    n_in = 2 + n_extra + len(deps)

    def body(*refs):
        a_ref, b_ref = refs[0], refs[1]
        x_refs = refs[2:2 + n_extra]
        o_refs = refs[n_in:n_in + n_out]
        acc = refs[-1]
        k = pl.program_id(2)

        prod = lax.dot_general(a_ref[...], b_ref[...], dims, preferred_element_type=F32)

        def finish(r):
            if epilogue == 'relu2':
                o_refs[0][...] = r.astype(BF16)
                o_refs[1][...] = jnp.square(jnp.maximum(r, 0.0)).astype(BF16)
            elif epilogue == 'drelu2':
                pre = x_refs[0][...].astype(F32)
                o_refs[0][...] = (r * (2.0 * jnp.maximum(pre, 0.0))).astype(out_dtype)
            else:
                o_refs[0][...] = r.astype(out_dtype)

        if nk == 1:
            finish(prod)
        else:
            @pl.when(k == 0)
            def _():
                acc[...] = prod

            @pl.when((k > 0) & (k < nk - 1))
            def _():
                acc[...] += prod

            @pl.when(k == nk - 1)
            def _():
                finish(acc[...] + prod)

    if epilogue == 'relu2':
        out_shape = (jax.ShapeDtypeStruct((M, N), BF16), jax.ShapeDtypeStruct((M, N), BF16))
        out_specs = (o_spec, o_spec)
    else:
        out_shape = jax.ShapeDtypeStruct(o_dims, out_dtype)
        out_specs = o_spec
    args = (a, b) + ((extra,) if extra is not None else ()) + tuple(deps)
    in_specs = ([a_spec, b_spec] + ([o_spec] if extra is not None else [])
                + [pl.BlockSpec(memory_space=pl.ANY)] * len(deps))
    return pl.pallas_call(
        body, out_shape=out_shape, grid=(M // tm, N // tn, nk), in_specs=in_specs, out_specs=out_specs,
        scratch_shapes=[pltpu.VMEM((tm, tn) if nk > 1 else (8, 128), F32)], name=name,
        compiler_params=_params(('parallel', 'parallel', 'arbitrary'), 56))(*args)


TR = 256


def _rowspec(w=D):
    return pl.BlockSpec((TR, w), lambda i: (i, 0))


def _prenorm_fwd(x, g, sc, sh):
    def body(x_ref, g_ref, sc_ref, sh_ref, h_ref):
        xv = x_ref[...]
        n = xv * _rms(xv)
        h_ref[...] = ((n * g_ref[...]) * (1.0 + sc_ref[...]) + sh_ref[...]).astype(BF16)

    return pl.pallas_call(
        body, out_shape=jax.ShapeDtypeStruct((L, D), BF16), grid=(L // TR,),
        in_specs=[_rowspec(), _vec(D), _vec(D), _vec(D)], out_specs=_rowspec(), name='prenorm_fwd',
        compiler_params=_params(('parallel',), 40))(x, g, sc, sh)


def _postmix_fwd(x, mix, gpm, gt1, gpl, sc2, sh2):
    def body(x_ref, mix_ref, gpm_ref, gt1_ref, gpl_ref, sc2_ref, sh2_ref, x1_ref, h2_ref):
        mix_v = mix_ref[...]
        nm = mix_v * _rms(mix_v)
        x1 = x_ref[...] + gt1_ref[...] * (nm * gpm_ref[...])
        x1_ref[...] = x1
        n2 = x1 * _rms(x1)
        h2_ref[...] = ((n2 * gpl_ref[...]) * (1.0 + sc2_ref[...]) + sh2_ref[...]).astype(BF16)

    return pl.pallas_call(
        body, out_shape=(jax.ShapeDtypeStruct((L, D), F32), jax.ShapeDtypeStruct((L, D), BF16)), grid=(L // TR,),
        in_specs=[_rowspec(), _rowspec()] + [_vec(D)] * 5, out_specs=(_rowspec(), _rowspec()), name='postmix_fwd',
        compiler_params=_params(('parallel',), 40))(x, mix, gpm, gt1, gpl, sc2, sh2)


def _final_fwd_bwd(x1, y, tgt, g, gt2):
    def body(x1_ref, y_ref, t_ref, g_ref, gt2_ref, dx2_ref, dy_ref, loss_ref, dgt2_ref, dg_ref):
        @pl.when(pl.program_id(0) == 0)
        def _():
            loss_ref[...] = jnp.zeros_like(loss_ref)
            dgt2_ref[...] = jnp.zeros_like(dgt2_ref)
            dg_ref[...] = jnp.zeros_like(dg_ref)

        yv = y_ref[...]
        r = _rms(yv)
        n = yv * r
        ng = n * g_ref[...]
        x2 = x1_ref[...] + gt2_ref[...] * ng
        e = x2 - t_ref[...]
        loss_ref[...] += 0.5 * jnp.sum(jnp.mean(e * e, axis=-1, keepdims=True), axis=0, keepdims=True)
        dx2 = e * (1.0 / D)
        dx2_ref[...] = dx2
        dgt2_ref[...] += _colsum(dx2 * ng)
        dng = dx2 * gt2_ref[...]
        dg_ref[...] += _colsum(dng * n)
        dy_ref[...] = _rms_bwd(dng * g_ref[...], n, r).astype(BF16)
        _fold8(pl.program_id(0) == L // TR - 1, dgt2_ref, dg_ref)

    return pl.pallas_call(
        body,
        out_shape=(jax.ShapeDtypeStruct((L, D), F32), jax.ShapeDtypeStruct((L, D), BF16),
                   jax.ShapeDtypeStruct((8, 128), F32), jax.ShapeDtypeStruct((8, D), F32),
                   jax.ShapeDtypeStruct((8, D), F32)),
        grid=(L // TR,), in_specs=[_rowspec(), _rowspec(), _rowspec(), _vec(D), _vec(D)],
        out_specs=(_rowspec(), _rowspec(), _vec8(128), _vec8(D), _vec8(D)), name='final_fwd_bwd',
        compiler_params=_params(('arbitrary',), 40))(x1, y, tgt, g, gt2)


def _postmix_bwd(dx2, dh2, x1, mix, gpm, gt1, gpl, sc2):
    def body(dx2_ref, dh2_ref, x1_ref, mix_ref, gpm_ref, gt1_ref, gpl_ref, sc2_ref,
             dx1_ref, dmix_ref, dsc2_ref, dsh2_ref, dgpl_ref, dgt1_ref, dgpm_ref):
        @pl.when(pl.program_id(0) == 0)
        def _():
            for r_ in (dsc2_ref, dsh2_ref, dgpl_ref, dgt1_ref, dgpm_ref):
                r_[...] = jnp.zeros_like(r_)

        x1v = x1_ref[...]
        r2 = _rms(x1v)
        n2 = x1v * r2
        dh2v = dh2_ref[...]
        dsh2_ref[...] += _colsum(dh2v)
        dsc2_ref[...] += _colsum(dh2v * (n2 * gpl_ref[...]))
        t = dh2v * (1.0 + sc2_ref[...])
        dgpl_ref[...] += _colsum(t * n2)
        dx1 = dx2_ref[...] + _rms_bwd(t * gpl_ref[...], n2, r2)
        dx1_ref[...] = dx1
        mix_v = mix_ref[...]
        rm = _rms(mix_v)
        nm = mix_v * rm
        dgt1_ref[...] += _colsum(dx1 * (nm * gpm_ref[...]))
        u = dx1 * gt1_ref[...]
        dgpm_ref[...] += _colsum(u * nm)
        dmix_ref[...] = _rms_bwd(u * gpm_ref[...], nm, rm).astype(BF16)
        _fold8(pl.program_id(0) == L // TR - 1, dsc2_ref, dsh2_ref, dgpl_ref, dgt1_ref, dgpm_ref)

    vs = jax.ShapeDtypeStruct((8, D), F32)
    return pl.pallas_call(
        body, out_shape=(jax.ShapeDtypeStruct((L, D), F32), jax.ShapeDtypeStruct((L, D), BF16), vs, vs, vs, vs, vs),
        grid=(L // TR,), in_specs=[_rowspec()] * 4 + [_vec(D)] * 4,
        out_specs=(_rowspec(), _rowspec()) + (_vec8(D),) * 5, name='postmix_bwd',
        compiler_params=_params(('arbitrary',), 48))(dx2, dh2, x1, mix, gpm, gt1, gpl, sc2)


def _prenorm_bwd(dx1, dh1, x, g, sc1):
    def body(dx1_ref, dh1_ref, x_ref, g_ref, sc1_ref, dx_ref, dsc1_ref, dsh1_ref, dg_ref):
        @pl.when(pl.program_id(0) == 0)
        def _():
            for r_ in (dsc1_ref, dsh1_ref, dg_ref):
                r_[...] = jnp.zeros_like(r_)

        xv = x_ref[...]
        r = _rms(xv)
        n = xv * r
        dh = dh1_ref[...]
        dsh1_ref[...] += _colsum(dh)
        dsc1_ref[...] += _colsum(dh * (n * g_ref[...]))
        t = dh * (1.0 + sc1_ref[...])
        dg_ref[...] += _colsum(t * n)
        dx_ref[...] = dx1_ref[...] + _rms_bwd(t * g_ref[...], n, r)
        _fold8(pl.program_id(0) == L // TR - 1, dsc1_ref, dsh1_ref, dg_ref)

    vs = jax.ShapeDtypeStruct((8, D), F32)
    return pl.pallas_call(
        body, out_shape=(jax.ShapeDtypeStruct((L, D), F32), vs, vs, vs), grid=(L // TR,),
        in_specs=[_rowspec()] * 3 + [_vec(D)] * 2, out_specs=(_rowspec(),) + (_vec8(D),) * 3, name='prenorm_bwd',
        compiler_params=_params(('arbitrary',), 40))(dx1, dh1, x, g, sc1)


ROPE_W = QW + KVW
QKV_W = 3 * KVW
NB_KV = KVW // 128


def _rope_rotate(xv, pos, fr, sign):
    ang = pos.astype(F32) * fr
    w = lax.broadcasted_iota(jnp.int32, (1, 128), 1) % HEAD_DIM
    cs = jnp.cos(ang)
    sn = jnp.sin(ang) * sign
    s1 = jnp.where(w < ROT_DIM // 2, -sn, 0.0)
    s2 = jnp.where((w >= ROT_DIM // 2) & (w < ROT_DIM), sn, 0.0)
    width = xv.shape[1]
    rep = width // 128
    cs, s1, s2 = jnp.tile(cs, (1, rep)), jnp.tile(s1, (1, rep)), jnp.tile(s2, (1, rep))
    hi = pltpu.roll(xv, width - ROT_DIM // 2, 1)
    lo = pltpu.roll(xv, ROT_DIM // 2, 1)
    return xv * cs + hi * s1 + lo * s2


def _sub_spec(d, rows, width):
    return pl.BlockSpec((d, rows // d, width), lambda i: (0, i, 0))


def _gather_rows(scr, blocks, r, d, rows):
    return jnp.concatenate([scr.at[j][pl.ds(r, rows // d, stride=d), :] for j in blocks], axis=1)


def _scatter_rows(scr, src_ref, d, rows):
    for r in range(d):
        for j in range(NB_KV):
            scr.at[j][pl.ds(r, rows // d, stride=d), :] = src_ref[r, :, j * 128:(j + 1) * 128]


def _token_order(scr):
    return jnp.concatenate([scr[j] for j in range(NB_KV)], axis=1)


def _rope_fwd(proj, pos, fr):
    nb = (ROPE_W + KVW) // 128

    def body(x_ref, pos_ref, fr_ref, o0_ref, o1_ref, o2_ref, scr):
        y = _rope_rotate(x_ref[:, 0:ROPE_W], pos_ref[...], fr_ref[...], 1.0)
        for j in range(ROPE_W // 128):
            scr[j] = y[:, j * 128:(j + 1) * 128]
        for j in range(ROPE_W // 128, nb):
            scr[j] = x_ref[:, j * 128:(j + 1) * 128]
        kv = list(range(QW // 128, nb))
        for g, (d, o_ref) in enumerate(zip(DILATIONS, (o0_ref, o1_ref, o2_ref))):
            blocks = list(range(g * NB_KV, (g + 1) * NB_KV)) + kv
            for r in range(d):
                o_ref[r] = _gather_rows(scr, blocks, r, d, TR).astype(BF16)

    return pl.pallas_call(
        body, out_shape=tuple(jax.ShapeDtypeStruct((d, L // d, QKV_W), BF16) for d in DILATIONS), grid=(L // TR,),
        in_specs=[_rowspec(ROPE_W + KVW), pl.BlockSpec((TR, 1), lambda i: (i, 0)), _vec(128)],
        out_specs=tuple(_sub_spec(d, TR, QKV_W) for d in DILATIONS),
        scratch_shapes=[pltpu.VMEM((nb, TR, 128), F32)], name='rope_fwd',
        compiler_params=_params(('parallel',), 40))(proj, pos, fr)


def _rope_bwd(dqkv, du, pos, fr):
    def body(*refs):
        grads = [refs[3 * g:3 * g + 3] for g in range(N_GROUPS)]
        du_ref, pos_ref, fr_ref, o_ref = refs[9:13]
        scrs = refs[13:]
        dq, dk, dv = [], None, None
        for g, d in enumerate(DILATIONS):
            parts = []
            for t in range(3):
                if d == 1:
                    parts.append(grads[g][t][0])
                else:
                    scr = scrs[3 * (g - 1) + t]
                    _scatter_rows(scr, grads[g][t], d, TR)
                    parts.append(_token_order(scr))
            dq.append(parts[0])
            dk = parts[1] if dk is None else dk + parts[1]
            dv = parts[2] if dv is None else dv + parts[2]
        x = jnp.concatenate(dq + [dk], axis=1)
        o_ref[:, 0:ROPE_W] = _rope_rotate(x, pos_ref[...], fr_ref[...], -1.0).astype(BF16)
        o_ref[:, ROPE_W:ROPE_W + KVW] = dv.astype(BF16)
        o_ref[:, ROPE_W + KVW:INW] = du_ref[...].astype(BF16)

    flat = [a for grp in dqkv for a in grp]
    in_specs = [_sub_spec(d, TR, KVW) for d in DILATIONS for _ in range(3)]
    in_specs += [_rowspec(SSMW), pl.BlockSpec((TR, 1), lambda i: (i, 0)), _vec(128)]
    return pl.pallas_call(
        body, out_shape=jax.ShapeDtypeStruct((L, INW), BF16), grid=(L // TR,), in_specs=in_specs,
        out_specs=_rowspec(INW), scratch_shapes=[pltpu.VMEM((NB_KV, TR, 128), F32)] * 6, name='rope_bwd',
        compiler_params=_params(('parallel',), 48))(*flat, du, pos, fr)


def _attn_mask(nbs, b):
    first = (b & (nbs - 1)) == 0
    qi = lax.broadcasted_iota(jnp.int32, (BLK, 2 * BLK), 0)
    kj = lax.broadcasted_iota(jnp.int32, (BLK, 2 * BLK), 1)
    dist = qi + BLK - kj
    return (dist >= 0) & (dist <= BLK) & ((kj >= BLK) | jnp.logical_not(first))


def _qkv_specs():
    cur = lambda col: pl.BlockSpec((BLK, KVW), lambda b: (b, col))
    prev = lambda col: pl.BlockSpec((BLK, KVW), lambda b: (jnp.maximum(b - 1, 0), col))
    return [cur(0), prev(1), cur(1), prev(2), cur(2)]


_ROWS = pl.BlockSpec((BLK, KVW), lambda b: (b, 0))
NEG = -1e30
NT_DIMS = (((1,), (1,)), ((), ()))
TN_DIMS = (((0,), (0,)), ((), ()))


def _attn_fwd(qkv, d):
    nbs = L // d // BLK

    def body(q_ref, kp_ref, kc_ref, vp_ref, vc_ref, o_ref, lse_ref):
        valid = _attn_mask(nbs, pl.program_id(0))
        outs, lses = [], []
        for h in range(HEADS):
            hs = slice(h * HEAD_DIM, (h + 1) * HEAD_DIM)
            q = q_ref[:, hs]
            kc = jnp.concatenate([kp_ref[:, hs], kc_ref[:, hs]], axis=0)
            vc = jnp.concatenate([vp_ref[:, hs], vc_ref[:, hs]], axis=0)
            s = lax.dot_general(q, kc, NT_DIMS, preferred_element_type=F32) * 0.125
            s = jnp.where(valid, s, NEG)
            m = jnp.max(s, axis=-1, keepdims=True)
            p = jnp.exp(s - m)
            l = jnp.sum(p, axis=-1, keepdims=True)
            outs.append(jnp.dot(p.astype(BF16), vc, preferred_element_type=F32) / l)
            lses.append(jnp.broadcast_to(m + jnp.log(l), (BLK, HEAD_DIM)))
        o_ref[...] = jnp.concatenate(outs, axis=1)
        lse_ref[...] = jnp.concatenate(lses, axis=1)

    sh = jax.ShapeDtypeStruct((L, KVW), F32)
    q2 = qkv.reshape(L, QKV_W)
    o, lse = pl.pallas_call(
        body, out_shape=(sh, sh), grid=(NBLK,), in_specs=_qkv_specs(), out_specs=(_ROWS, _ROWS),
        name='attn_fwd_d%d' % d, compiler_params=_params(('parallel',), 32))(q2, q2, q2, q2, q2)
    return o.reshape(d, L // d, KVW), lse.reshape(d, L // d, KVW)


def _attn_bwd(qkv, o, lse, do, dlse, d):
    nbs = L // d // BLK

    def body(q_ref, kp_ref, kc_ref, vp_ref, vc_ref, o_ref, lse_ref, do_ref, dlse_ref, dq_ref, dk_ref, dv_ref):
        b = pl.program_id(0)

        @pl.when(b == 0)
        def _():
            dk_ref[...] = jnp.zeros_like(dk_ref)
            dv_ref[...] = jnp.zeros_like(dv_ref)

        valid = _attn_mask(nbs, b)
        prev0 = pl.multiple_of(jnp.maximum(b - 1, 0) * BLK, BLK)
        cur0 = pl.multiple_of(b * BLK, BLK)
        dqs, dks, dvs = [], [], []
        for h in range(HEADS):
            hs = slice(h * HEAD_DIM, (h + 1) * HEAD_DIM)
            q = q_ref[:, hs]
            kc = jnp.concatenate([kp_ref[:, hs], kc_ref[:, hs]], axis=0)
            vc = jnp.concatenate([vp_ref[:, hs], vc_ref[:, hs]], axis=0)
            s = lax.dot_general(q, kc, NT_DIMS, preferred_element_type=F32) * 0.125
            s = jnp.where(valid, s, NEG)
            p = jnp.exp(s - lse_ref[:, h * HEAD_DIM:h * HEAD_DIM + 1])
            do_h = do_ref[:, hs]
            delta = jnp.sum(do_h * o_ref[:, hs], axis=-1, keepdims=True)
            do_b = do_h.astype(BF16)
            dp = lax.dot_general(do_b, vc, NT_DIMS, preferred_element_type=F32)
            ds = p * (dp - delta + dlse_ref[:, h * HEAD_DIM:h * HEAD_DIM + 1])
            ds_b = (ds * 0.125).astype(BF16)
            dqs.append(jnp.dot(ds_b, kc, preferred_element_type=F32))
            dks.append(lax.dot_general(ds_b, q, TN_DIMS, preferred_element_type=F32))
            dvs.append(lax.dot_general(p.astype(BF16), do_b, TN_DIMS, preferred_element_type=F32))
        dq_ref[...] = jnp.concatenate(dqs, axis=1)
        dkc, dvc = jnp.concatenate(dks, axis=1), jnp.concatenate(dvs, axis=1)
        dk_ref[pl.ds(prev0, BLK), :] += dkc[:BLK]
        dv_ref[pl.ds(prev0, BLK), :] += dvc[:BLK]
        dk_ref[pl.ds(cur0, BLK), :] += dkc[BLK:]
        dv_ref[pl.ds(cur0, BLK), :] += dvc[BLK:]

    sh = jax.ShapeDtypeStruct((L, KVW), F32)
    whole = pl.BlockSpec((L, KVW), lambda b: (0, 0))
    q2 = qkv.reshape(L, QKV_W)
    flat = lambda t: t.reshape(L, KVW)
    outs = pl.pallas_call(
        body, out_shape=(sh, sh, sh), grid=(NBLK,), in_specs=_qkv_specs() + [_ROWS] * 4,
        out_specs=(_ROWS, whole, whole), name='attn_bwd_d%d' % d,
        compiler_params=_params(('arbitrary',), 48))(q2, q2, q2, q2, q2, flat(o), flat(lse), flat(do), flat(dlse))
    return tuple(t.reshape(d, L // d, KVW) for t in outs)


TC = 512


def _combine_weights(l0, l1, l2):
    m = jnp.maximum(jnp.maximum(l0, l1), l2)
    e0, e1, e2 = jnp.exp(l0 - m), jnp.exp(l1 - m), jnp.exp(l2 - m)
    z = e0 + e1 + e2
    return e0 / z, e1 / z, e2 / z


def _load_groups(refs, scrs):
    out = [refs[0][0]]
    for g in (1, 2):
        _scatter_rows(scrs[g - 1], refs[g], DILATIONS[g], TC)
        out.append(_token_order(scrs[g - 1]))
    return out


def _combine_fwd(os_, lses, g):
    def body(o0, o1, o2, l0, l1, l2, g_ref, att_ref, *scrs):
        ov = _load_groups((o0, o1, o2), scrs[0:2])
        lv = _load_groups((l0, l1, l2), scrs[2:4])
        w0, w1, w2 = _combine_weights(*lv)
        a = w0 * ov[0] + w1 * ov[1] + w2 * ov[2]
        att_ref[...] = ((a * _rms(a)) * g_ref[...]).astype(BF16)

    subs = [_sub_spec(d, TC, KVW) for d in DILATIONS]
    return pl.pallas_call(
        body, out_shape=jax.ShapeDtypeStruct((L, KVW), BF16), grid=(L // TC,), in_specs=subs + subs + [_vec(KVW)],
        out_specs=pl.BlockSpec((TC, KVW), lambda i: (i, 0)),
        scratch_shapes=[pltpu.VMEM((NB_KV, TC, 128), F32)] * 4, name='combine_fwd',
        compiler_params=_params(('parallel',), 40))(*os_, *lses, g)


def _combine_bwd(dcat, os_, lses, g, head_ones):
    def body(datt_ref, o0, o1, o2, l0, l1, l2, g_ref, e_ref, do0, do1, do2, dl0, dl1, dl2, dg_ref, *scrs):
        @pl.when(pl.program_id(0) == 0)
        def _():
            dg_ref[...] = jnp.zeros_like(dg_ref)

        ov = _load_groups((o0, o1, o2), scrs[0:2])
        lv = _load_groups((l0, l1, l2), scrs[2:4])
        ws = _combine_weights(*lv)
        a = ws[0] * ov[0] + ws[1] * ov[1] + ws[2] * ov[2]
        r = _rms(a)
        n = a * r
        dv = datt_ref[...]
        dg_ref[...] += _colsum(dv * n)
        da = _rms_bwd(dv * g_ref[...], n, r)
        dws = [jnp.dot(da * ov[i], e_ref[...], preferred_element_type=F32, precision=HI) for i in range(3)]
        dbar = ws[0] * dws[0] + ws[1] * dws[1] + ws[2] * dws[2]
        scr = scrs[4]
        for i, (d, do_ref, dl_ref) in enumerate(zip(DILATIONS, (do0, do1, do2), (dl0, dl1, dl2))):
            for val, out_ref in ((ws[i] * da, do_ref), (ws[i] * (dws[i] - dbar), dl_ref)):
                if d == 1:
                    out_ref[0] = val
                else:
                    for j in range(NB_KV):
                        scr[j] = val[:, j * 128:(j + 1) * 128]
                    for rr in range(d):
                        out_ref[rr] = _gather_rows(scr, range(NB_KV), rr, d, TC)
        _fold8(pl.program_id(0) == L // TC - 1, dg_ref)

    subs = [_sub_spec(d, TC, KVW) for d in DILATIONS]
    shs = tuple(jax.ShapeDtypeStruct((d, L // d, KVW), F32) for d in DILATIONS)
    outs = pl.pallas_call(
        body, out_shape=shs + shs + (jax.ShapeDtypeStruct((8, KVW), F32),), grid=(L // TC,),
        in_specs=[pl.BlockSpec((TC, KVW), lambda i: (i, 0))] + subs + subs + [_vec(KVW),
                                                                              pl.BlockSpec((KVW, KVW), lambda i: (0, 0))],
        out_specs=tuple(subs) + tuple(subs) + (_vec8(KVW),),
        scratch_shapes=[pltpu.VMEM((NB_KV, TC, 128), F32)] * 5, name='combine_bwd',
        compiler_params=_params(('arbitrary',), 48))(dcat, *os_, *lses, g, head_ones)
    return outs[0:3], outs[3:6], outs[6]


def _ssm_disc(ar, ai, ldt):
    dt = jnp.exp(ldt)
    zr, zi = ar * dt, ai * dt
    ez = jnp.exp(zr)
    A_r, A_i = ez * jnp.cos(zi), ez * jnp.sin(zi)
    den = ar * ar + ai * ai
    xr, xi = A_r - 1.0, A_i
    cr = (xr * ar + xi * ai) / den
    ci = (xi * ar - xr * ai) / den
    return dt, zr, zi, A_r, A_i, den, cr, ci


def _ssm_pre(ar, ai, ldt, br, bi):
    def body(ar_ref, ai_ref, ldt_ref, br_ref, bi_ref, bbr_ref, bbi_ref, pwr_ref, pwi_ref):
        _, zr, zi, _, _, _, cr, ci = _ssm_disc(ar_ref[...], ai_ref[...], ldt_ref[...])
        bbr_ref[...] = cr * br_ref[...] - ci * bi_ref[...]
        bbi_ref[...] = cr * bi_ref[...] + ci * br_ref[...]
        k = (lax.broadcasted_iota(jnp.int32, (1, 8), 1) + 1).astype(F32)
        ek = jnp.exp(zr * k)
        pwr_ref[...] = ek * jnp.cos(zi * k)
        pwi_ref[...] = ek * jnp.sin(zi * k)

    s16 = jax.ShapeDtypeStruct((SSM_GN, SSM_P), F32)
    s8 = jax.ShapeDtypeStruct((SSM_GN, 8), F32)
    return pl.pallas_call(body, out_shape=(s16, s16, s8, s8), name='ssm_pre',
                          compiler_params=_params(None, 40))(ar, ai, ldt, br, bi)


def _ssm_post(ar, ai, ldt, br, bi, gar, gai, gbr, gbi, sel):
    def body(ar_ref, ai_ref, ldt_ref, br_ref, bi_ref, gar_ref, gai_ref, gbr_ref, gbi_ref, sel_ref,
             dar_ref, dai_ref, dbr_ref, dbi_ref, dldt_ref):
        a_r, a_i = ar_ref[...], ai_ref[...]
        dt, _, _, A_r, A_i, den, cr, ci = _ssm_disc(a_r, a_i, ldt_ref[...])
        b_r, b_i, g_br, g_bi = br_ref[...], bi_ref[...], gbr_ref[...], gbi_ref[...]
        gcr = jnp.sum(g_br * b_r + g_bi * b_i, axis=-1, keepdims=True)
        gci = jnp.sum(g_bi * b_r - g_br * b_i, axis=-1, keepdims=True)
        dbr_ref[...] = g_br * cr + g_bi * ci
        dbi_ref[...] = g_bi * cr - g_br * ci
        g_ar = gar_ref[...] + (gcr * a_r - gci * a_i) / den
        g_ai = gai_ref[...] + (gcr * a_i + gci * a_r) / den
        qr = (cr * a_r + ci * a_i) / den
        qi = (ci * a_r - cr * a_i) / den
        glr = -(gcr * qr + gci * qi)
        gli = -(gci * qr - gcr * qi)
        gzr = g_ar * A_r + g_ai * A_i
        gzi = g_ai * A_r - g_ar * A_i
        dar_ref[...] = glr + gzr * dt
        dai_ref[...] = gli + gzi * dt
        gdt = (gzr * a_r + gzi * a_i) * dt
        dldt_ref[...] = jnp.dot(sel_ref[...], jnp.broadcast_to(gdt, (SSM_GN, 128)),
                                preferred_element_type=F32, precision=HI)

    s1 = jax.ShapeDtypeStruct((SSM_GN, 1), F32)
    s16 = jax.ShapeDtypeStruct((SSM_GN, SSM_P), F32)
    return pl.pallas_call(body, out_shape=(s1, s1, s16, s16, jax.ShapeDtypeStruct((SSM_G, 128), F32)),
                          name='ssm_post', compiler_params=_params(None, 48))(
                              ar, ai, ldt, br, bi, gar, gai, gbr, gbi, sel)


SCAN_CH = 8


def _scan_fwd_tiles(s_ref, pw, carry):
    pwr, pwi = pw[:, :CL_S], pw[:, CL_S:]
    row = lax.broadcasted_iota(jnp.int32, (8, CL_S), 0)
    steps = [(k, jnp.where(row >= k, pwr[k - 1:k], 0.0), jnp.where(row >= k, pwi[k - 1:k], 0.0)) for k in (1, 2, 4)]
    rows = 8 * SCAN_CH

    def chunk(i, c):
        cr, ci = c
        r0 = pl.multiple_of(i * rows, rows)
        xr = s_ref[pl.ds(r0, rows), 0:CL_S].reshape(SCAN_CH, 8, CL_S)
        xi = s_ref[pl.ds(r0, rows), CL_S:2 * CL_S].reshape(SCAN_CH, 8, CL_S)
        for k, pr, pi in steps:
            sr, si = pltpu.roll(xr, k, 1), pltpu.roll(xi, k, 1)
            xr, xi = xr + pr * sr - pi * si, xi + pr * si + pi * sr
        for j in range(SCAN_CH):
            tr = xr[j] + pwr * cr - pwi * ci
            ti = xi[j] + pwr * ci + pwi * cr
            s_ref[pl.ds(r0 + 8 * j, 8), 0:CL_S] = tr
            s_ref[pl.ds(r0 + 8 * j, 8), CL_S:2 * CL_S] = ti
            cr, ci = tr[7:8], ti[7:8]
        return cr, ci

    return lax.fori_loop(0, T_SCAN // rows, chunk, (carry[:, :CL_S], carry[:, CL_S:]))


def _scan_bwd_tiles(l_ref, pw, carry):
    pwr, pwi = pw[:, :CL_S], pw[:, CL_S:]
    rpr = jnp.concatenate([pwr[7 - r:8 - r] for r in range(8)], axis=0)
    rpi = jnp.concatenate([pwi[7 - r:8 - r] for r in range(8)], axis=0)
    row = lax.broadcasted_iota(jnp.int32, (8, CL_S), 0)
    steps = [(k, jnp.where(row < 8 - k, pwr[k - 1:k], 0.0), jnp.where(row < 8 - k, pwi[k - 1:k], 0.0))
             for k in (1, 2, 4)]
    rows = 8 * SCAN_CH
    nc = T_SCAN // rows

    def chunk(i, c):
        cr, ci = c
        r0 = pl.multiple_of((nc - 1 - i) * rows, rows)
        xr = l_ref[pl.ds(r0, rows), 0:CL_S].reshape(SCAN_CH, 8, CL_S)
        xi = l_ref[pl.ds(r0, rows), CL_S:2 * CL_S].reshape(SCAN_CH, 8, CL_S)
        for k, pr, pi in steps:
            sr, si = pltpu.roll(xr, 8 - k, 1), pltpu.roll(xi, 8 - k, 1)
            xr, xi = xr + pr * sr + pi * si, xi + pr * si - pi * sr
        for j in reversed(range(SCAN_CH)):
            tr = xr[j] + rpr * cr + rpi * ci
            ti = xi[j] + rpr * ci - rpi * cr
            l_ref[pl.ds(r0 + 8 * j, 8), 0:CL_S] = tr
            l_ref[pl.ds(r0 + 8 * j, 8), CL_S:2 * CL_S] = ti
            cr, ci = tr[0:1], ti[0:1]
        return cr, ci

    return lax.fori_loop(0, nc, chunk, (carry[:, :CL_S], carry[:, CL_S:]))


NT_SCAN = L // T_SCAN


def _hilo(t):
    hi = t.astype(BF16)
    return jnp.stack([hi, (t - hi.astype(F32)).astype(BF16)], axis=1)


def _dot3(a, b_ref):
    ah = a.astype(BF16)
    al = (a - ah.astype(F32)).astype(BF16)
    bh, bl = b_ref[0], b_ref[1]
    return (jnp.dot(ah, bh, preferred_element_type=F32) + jnp.dot(al, bh, preferred_element_type=F32)
            + jnp.dot(ah, bl, preferred_element_type=F32))


def _hl_spec(r, c):
    return pl.BlockSpec((None, 2, r, c), lambda c_, t: (c_, 0, 0, 0))


def _ssm_fwd(u, bm2, cm2, pw, dvec, u_off=0):
    def body(u_ref, bm_ref, cm_ref, pw_ref, d_ref, y_ref, bnd_ref, s_ref, carry_ref):
        @pl.when(pl.program_id(1) == 0)
        def _():
            carry_ref[...] = jnp.zeros_like(carry_ref)

        bnd_ref[...] = carry_ref[...]
        uv = u_ref[...]
        s_ref[...] = jnp.dot(uv.astype(BF16), bm_ref[0], preferred_element_type=F32)
        cr, ci = _scan_fwd_tiles(s_ref, pw_ref[...], carry_ref[...])
        carry_ref[...] = jnp.concatenate([cr, ci], axis=1)
        y_ref[...] = jnp.dot(s_ref[...].astype(BF16), cm_ref[0], preferred_element_type=F32) + d_ref[...] * uv

    return pl.pallas_call(
        body,
        out_shape=(jax.ShapeDtypeStruct((L, SSMW), F32), jax.ShapeDtypeStruct((N_CL, NT_SCAN, 1, 2 * CL_S), F32),
                   jax.ShapeDtypeStruct((L, N_CL * 2 * CL_S), F32)),
        grid=(N_CL, NT_SCAN),
        in_specs=[pl.BlockSpec((T_SCAN, CL_U), lambda c, t: (t, c + u_off)),
                  _hl_spec(CL_U, 2 * CL_S), _hl_spec(2 * CL_S, CL_U),
                  pl.BlockSpec((None, 8, 2 * CL_S), lambda c, t: (c, 0, 0)),
                  pl.BlockSpec((1, CL_U), lambda c, t: (0, c))],
        out_specs=(pl.BlockSpec((T_SCAN, CL_U), lambda c, t: (t, c)),
                   pl.BlockSpec((None, None, 1, 2 * CL_S), lambda c, t: (c, t, 0, 0)),
                   pl.BlockSpec((T_SCAN, 2 * CL_S), lambda c, t: (t, c))),
        scratch_shapes=[pltpu.VMEM((1, 2 * CL_S), F32)],
        name='ssm_fwd', compiler_params=_params(('arbitrary', 'arbitrary'), 40))(u, bm2, cm2, pw, dvec)


def _ssm_bwd(u, dy, states, bmt, cmt, pw, dvec, bnd, u_off=0):
    rev = lambda t: NT_SCAN - 1 - t

    def body(u_ref, dy_ref, s_ref, bmt_ref, cmt_ref, pw_ref, d_ref, bnd_ref,
             du_ref, dbm_ref, dcm_ref, da_ref, dd_ref, l_ref, carry_ref):
        @pl.when(pl.program_id(1) == 0)
        def _():
            carry_ref[...] = jnp.zeros_like(carry_ref)
            dbm_ref[...] = jnp.zeros_like(dbm_ref)
            dcm_ref[...] = jnp.zeros_like(dcm_ref)
            da_ref[...] = jnp.zeros_like(da_ref)
            dd_ref[...] = jnp.zeros_like(dd_ref)

        uv, dyv, pw = u_ref[...], dy_ref[...], pw_ref[...]
        dy_b = dyv.astype(BF16)
        entry = bnd_ref[...]
        l_ref[...] = jnp.dot(dy_b, cmt_ref[...], preferred_element_type=F32)
        cr, ci = _scan_bwd_tiles(l_ref, pw, carry_ref[...])
        carry_ref[...] = jnp.concatenate([cr, ci], axis=1)
        sv, lv = s_ref[...], l_ref[...]
        lv_b = lv.astype(BF16)
        du_ref[...] = dyv * d_ref[...] + jnp.dot(lv_b, bmt_ref[...], preferred_element_type=F32)
        dbm_ref[...] += lax.dot_general(uv.astype(BF16), lv_b, TN_DIMS, preferred_element_type=F32)
        dcm_ref[...] += lax.dot_general(sv.astype(BF16), dy_b, TN_DIMS, preferred_element_type=F32)
        dd_ref[...] += _colsum(dyv * uv)
        row = lax.broadcasted_iota(jnp.int32, (T_SCAN, 2 * CL_S), 0)
        sp = jnp.where(row == 0, entry, pltpu.roll(sv, 1, 0))
        spr, spi = sp[:, :CL_S], sp[:, CL_S:]
        lr, li = lv[:, :CL_S], lv[:, CL_S:]
        da_ref[:, 0:CL_S] += _colsum(lr * spr + li * spi)
        da_ref[:, CL_S:2 * CL_S] += _colsum(li * spr - lr * spi)
        _fold8(pl.program_id(1) == NT_SCAN - 1, da_ref, dd_ref)

    return pl.pallas_call(
        body,
        out_shape=(jax.ShapeDtypeStruct((L, SSMW), F32), jax.ShapeDtypeStruct((N_CL, CL_U, 2 * CL_S), F32),
                   jax.ShapeDtypeStruct((N_CL, 2 * CL_S, CL_U), F32), jax.ShapeDtypeStruct((N_CL, 8, 2 * CL_S), F32),
                   jax.ShapeDtypeStruct((8, SSMW), F32)),
        grid=(N_CL, NT_SCAN),
        in_specs=[pl.BlockSpec((T_SCAN, CL_U), lambda c, t: (rev(t), c + u_off)),
                  pl.BlockSpec((T_SCAN, CL_U), lambda c, t: (rev(t), c)),
                  pl.BlockSpec((T_SCAN, 2 * CL_S), lambda c, t: (rev(t), c)),
                  pl.BlockSpec((None, 2 * CL_S, CL_U), lambda c, t: (c, 0, 0)),
                  pl.BlockSpec((None, CL_U, 2 * CL_S), lambda c, t: (c, 0, 0)),
                  pl.BlockSpec((None, 8, 2 * CL_S), lambda c, t: (c, 0, 0)),
                  pl.BlockSpec((1, CL_U), lambda c, t: (0, c)),
                  pl.BlockSpec((None, None, 1, 2 * CL_S), lambda c, t: (c, rev(t), 0, 0))],
        out_specs=(pl.BlockSpec((T_SCAN, CL_U), lambda c, t: (rev(t), c)),
                   pl.BlockSpec((None, CL_U, 2 * CL_S), lambda c, t: (c, 0, 0)),
                   pl.BlockSpec((None, 2 * CL_S, CL_U), lambda c, t: (c, 0, 0)),
                   pl.BlockSpec((None, 8, 2 * CL_S), lambda c, t: (c, 0, 0)),
                   pl.BlockSpec((8, CL_U), lambda c, t: (0, c))),
        scratch_shapes=[pltpu.VMEM((T_SCAN, 2 * CL_S), F32), pltpu.VMEM((1, 2 * CL_S), F32)],
        name='ssm_bwd', compiler_params=_params(('arbitrary', 'arbitrary'), 48))(u, dy, states, bmt, cmt, pw, dvec, bnd)


GELU_C = math.sqrt(2.0 / math.pi)
GELU_K = 0.044715


def _gelu_parts(x):
    t = jnp.tanh(GELU_C * (x + GELU_K * (x * x * x)))
    return x * (0.5 * (1.0 + t)), t


def _glu_fwd(ypre, wglu, bglu, gs):
    def body(y_ref, w_ref, b_ref, g_ref, o_ref):
        yg, _ = _gelu_parts(y_ref[...])
        z = jnp.dot(yg.astype(BF16), w_ref[...], preferred_element_type=F32) + b_ref[...]
        s = yg * jax.nn.sigmoid(z)
        o_ref[...] = ((s * _rms(s)) * g_ref[...]).astype(BF16)

    return pl.pallas_call(
        body, out_shape=jax.ShapeDtypeStruct((L, SSMW), BF16), grid=(L // TR,),
        in_specs=[_rowspec(SSMW), pl.BlockSpec((SSMW, SSMW), lambda i: (0, 0)), _vec(SSMW), _vec(SSMW)],
        out_specs=_rowspec(SSMW), name='glu_fwd', compiler_params=_params(('parallel',), 32))(ypre, wglu, bglu, gs)


def _glu_bwd(ypre, dsn, wglu, bglu, gs):
    def body(y_ref, d_ref, w_ref, b_ref, g_ref, dy_ref, dw_ref, db_ref, dg_ref):
        @pl.when(pl.program_id(0) == 0)
        def _():
            dw_ref[...] = jnp.zeros_like(dw_ref)
            db_ref[...] = jnp.zeros_like(db_ref)
            dg_ref[...] = jnp.zeros_like(dg_ref)

        xv = y_ref[...]
        yg, t = _gelu_parts(xv)
        yg_b = yg.astype(BF16)
        z = jnp.dot(yg_b, w_ref[...], preferred_element_type=F32) + b_ref[...]
        sg = jax.nn.sigmoid(z)
        s = yg * sg
        r = _rms(s)
        n = s * r
        dv = d_ref[...]
        dg_ref[...] += _colsum(dv * n)
        ds = _rms_bwd(dv * g_ref[...], n, r)
        dz = (ds * yg) * (sg * (1.0 - sg))
        dz_b = dz.astype(BF16)
        db_ref[...] += _colsum(dz)
        dw_ref[...] += lax.dot_general(yg_b, dz_b, TN_DIMS, preferred_element_type=F32)
        dyg = ds * sg + lax.dot_general(dz_b, w_ref[...], NT_DIMS, preferred_element_type=F32)
        dgelu = 0.5 * (1.0 + t) + (0.5 * xv) * (1.0 - t * t) * (GELU_C * (1.0 + 3.0 * GELU_K * (xv * xv)))
        dy_ref[...] = dyg * dgelu
        _fold8(pl.program_id(0) == L // TR - 1, db_ref, dg_ref)

    vs = jax.ShapeDtypeStruct((8, SSMW), F32)
    return pl.pallas_call(
        body, out_shape=(jax.ShapeDtypeStruct((L, SSMW), F32), jax.ShapeDtypeStruct((SSMW, SSMW), F32), vs, vs),
        grid=(L // TR,),
        in_specs=[_rowspec(SSMW), _rowspec(SSMW), pl.BlockSpec((SSMW, SSMW), lambda i: (0, 0)), _vec(SSMW), _vec(SSMW)],
        out_specs=(_rowspec(SSMW), pl.BlockSpec((SSMW, SSMW), lambda i: (0, 0)), _vec8(SSMW), _vec8(SSMW)),
        name='glu_bwd', compiler_params=_params(('arbitrary',), 40))(ypre, dsn, wglu, bglu, gs)


def _me():
    return lax.axis_index('x'), lax.axis_index('y'), lax.axis_index('c')


def _my_index():
    return 4 * lax.axis_index('x') + 2 * lax.axis_index('y') + lax.axis_index('c')


def _peer(k):
    x, y, c = _me()
    px = 1 - x if k & 4 else x
    py = 1 - y if k & 2 else y
    pc = 1 - c if k & 1 else c
    return (px, py, pc), 4 * px + 2 * py + pc


def _mod_exchange(c_row, w_ada, b_ada8, deps=()):
    cw = NMOD * D // N_DEV

    def body(c_ref, w_ref, b_ref, *rest):
        call_ref, mod_ref, part_ref, send_sems, recv_sems = rest[len(deps):]
        x, y, c = _me()
        me = 4 * x + 2 * y + c
        call_ref[me] = c_ref[0]
        sends = []
        for k in range(1, N_DEV):
            peer, _ = _peer(k)
            cp = pltpu.make_async_remote_copy(src_ref=c_ref.at[0], dst_ref=call_ref.at[me], send_sem=send_sems.at[0, k - 1],
                                              recv_sem=recv_sems.at[0, k - 1], device_id=peer, device_id_type=MESH)
            cp.start()
            sends.append(cp)
        for k in range(1, N_DEV):
            peer, pidx = _peer(k)
            pltpu.make_async_remote_copy(src_ref=c_ref.at[0], dst_ref=call_ref.at[pidx], send_sem=send_sems.at[0, k - 1],
                                         recv_sem=recv_sems.at[0, k - 1], device_id=peer, device_id_type=MESH).wait_recv()
        for cp in sends:
            cp.wait_send()
        cv = call_ref[...].reshape(N_DEV, D)
        part = jnp.dot(cv * jax.nn.sigmoid(cv), w_ref[...], preferred_element_type=F32, precision=HI)
        part_ref[...] = part.reshape(N_DEV, 1, cw)
        mod_ref[me] = part_ref[me]
        sends = []
        for k in range(1, N_DEV):
            peer, pidx = _peer(k)
            cp = pltpu.make_async_remote_copy(src_ref=part_ref.at[pidx], dst_ref=mod_ref.at[me], send_sem=send_sems.at[1, k - 1],
                                              recv_sem=recv_sems.at[1, k - 1], device_id=peer, device_id_type=MESH)
            cp.start()
            sends.append(cp)
        for k in range(1, N_DEV):
            peer, pidx = _peer(k)
            pltpu.make_async_remote_copy(src_ref=part_ref.at[pidx], dst_ref=mod_ref.at[pidx], send_sem=send_sems.at[1, k - 1],
                                         recv_sem=recv_sems.at[1, k - 1], device_id=peer, device_id_type=MESH).wait_recv()
        for cp in sends:
            cp.wait_send()
        mod_ref[...] = mod_ref[...] + b_ref[...]

    vm = pl.BlockSpec(memory_space=pltpu.VMEM)
    return pl.pallas_call(
        body, out_shape=(jax.ShapeDtypeStruct((N_DEV, 1, D), F32), jax.ShapeDtypeStruct((N_DEV, 1, cw), F32)),
        in_specs=[vm, vm, vm] + [pl.BlockSpec(memory_space=pl.ANY)] * len(deps), out_specs=(vm, vm),
        scratch_shapes=[pltpu.VMEM((N_DEV, 1, cw), F32), pltpu.SemaphoreType.DMA((2, N_DEV - 1)),
                        pltpu.SemaphoreType.DMA((2, N_DEV - 1))],
        name='mod_exchange', compiler_params=_params(None, 48))(c_row, w_ada, b_ada8, *deps)


HBM_SPEC = pl.BlockSpec(memory_space=pltpu.HBM)
SEM_SPEC = pl.BlockSpec(memory_space=pltpu.SEMAPHORE)
DATAFLOW = pltpu.SideEffectType.DATAFLOW_SIDE_EFFECTING


def _push_start(src, scatter, after, name):
    land = lax.empty(src.shape if scatter else (N_DEV,) + src.shape, src.dtype)

    def body(src_ref, land_ref, after_ref, send_sem, recv_sem, land_thru, token):
        x, y, c = _me()
        me = 4 * x + 2 * y + c
        for k in range(1, N_DEV):
            peer, pidx = _peer(k)
            pltpu.make_async_remote_copy(src_ref=src_ref.at[pidx] if scatter else src_ref, dst_ref=land_ref.at[me],
                                         send_sem=send_sem, recv_sem=recv_sem, device_id=peer,
                                         device_id_type=MESH).start()
        token[...] = jnp.zeros_like(token)

    own = lax.dynamic_index_in_dim(src, _my_index(), 0, keepdims=False) if scatter else src
    src = pltpu.with_memory_space_constraint(src, pltpu.HBM)
    send_sem, recv_sem, land_thru, token = pl.pallas_call(
        body, name=name,
        out_shape=(pltpu.SemaphoreType.DMA(()), pltpu.SemaphoreType.DMA(()),
                   pltpu.HBM(land.shape, land.dtype), jax.ShapeDtypeStruct((8, 128), F32)),
        in_specs=(HBM_SPEC, HBM_SPEC, pl.BlockSpec(memory_space=pl.ANY)),
        out_specs=(SEM_SPEC, SEM_SPEC, HBM_SPEC, pl.BlockSpec(memory_space=pltpu.VMEM)),
        input_output_aliases={1: 2}, compiler_params=pltpu.CompilerParams(has_side_effects=DATAFLOW),
    )(src, pltpu.with_memory_space_constraint(land, pltpu.HBM), after)
    return send_sem, recv_sem, src, land_thru, token, own


def _push_wait(handle, after, name):
    send_sem, recv_sem, src, land_thru, _, own = handle
    after = tuple(after) if isinstance(after, (tuple, list)) else (after,)

    def body(src_ref, land_ref, send_sem, recv_sem, *rest):
        seven = land_ref.at[pl.ds(0, N_DEV - 1)]
        cp = pltpu.make_async_remote_copy(src_ref=seven, dst_ref=seven, send_sem=send_sem, recv_sem=recv_sem,
                                          device_id=_me(), device_id_type=MESH)
        cp.wait_send()
        cp.wait_recv()

    landed = pl.pallas_call(
        body, name=name, out_shape=pltpu.HBM(land_thru.shape, land_thru.dtype),
        in_specs=(HBM_SPEC, HBM_SPEC, SEM_SPEC, SEM_SPEC) + (pl.BlockSpec(memory_space=pl.ANY),) * len(after),
        out_specs=HBM_SPEC, input_output_aliases={1: 0},
        compiler_params=pltpu.CompilerParams(has_side_effects=DATAFLOW),
    )(src, land_thru, send_sem, recv_sem, *after)
    return lax.dynamic_update_index_in_dim(landed, own, _my_index(), 0)


def _adam(w, g, m, v):
    m2 = B1 * m + (1.0 - B1) * g
    v2 = B2 * v + (1.0 - B2) * jnp.square(g)
    m_hat = m2 / (1.0 - B1 ** STEP)
    v_hat = v2 / (1.0 - B2 ** STEP)
    delta = -LR * (m_hat / (jnp.sqrt(v_hat) + AEPS) + WD * w)
    return delta, m2, v2


def _small_update(gp, wp, mp, vp):
    def body(g_ref, w_ref, m_ref, v_ref, all_ref, go_ref, d_ref, mo_ref, vo_ref, send_sems, recv_sems):
        x, y, c = _me()
        me = 4 * x + 2 * y + c
        all_ref[me] = g_ref[...]
        sends = []
        for k in range(1, N_DEV):
            peer, _ = _peer(k)
            cp = pltpu.make_async_remote_copy(src_ref=g_ref, dst_ref=all_ref.at[me], send_sem=send_sems.at[k - 1],
                                              recv_sem=recv_sems.at[k - 1], device_id=peer, device_id_type=MESH)
            cp.start()
            sends.append(cp)
        for k in range(1, N_DEV):
            peer, pidx = _peer(k)
            pltpu.make_async_remote_copy(src_ref=g_ref, dst_ref=all_ref.at[pidx], send_sem=send_sems.at[k - 1],
                                         recv_sem=recv_sems.at[k - 1], device_id=peer, device_id_type=MESH).wait_recv()
        for cp in sends:
            cp.wait_send()
        g = all_ref[0]
        for d in range(1, N_DEV):
            g = g + all_ref[d]
        delta, m2, v2 = _adam(w_ref[...], g, m_ref[...], v_ref[...])
        go_ref[...] = g
        d_ref[...] = delta
        mo_ref[...] = m2
        vo_ref[...] = v2

    vm = pl.BlockSpec(memory_space=pltpu.VMEM)
    vs = jax.ShapeDtypeStruct(gp.shape, F32)
    return pl.pallas_call(
        body, out_shape=(jax.ShapeDtypeStruct((N_DEV,) + gp.shape, F32), vs, vs, vs, vs), in_specs=[vm] * 4,
        out_specs=(vm,) * 5,
        scratch_shapes=[pltpu.SemaphoreType.DMA((N_DEV - 1,)), pltpu.SemaphoreType.DMA((N_DEV - 1,))],
        name='small_update', compiler_params=_params(None, 48))(gp, wp, mp, vp)


def _small_sum_update(parts, wp, mp, vp):
    def body(p_ref, w_ref, m_ref, v_ref, go_ref, d_ref, mo_ref, vo_ref):
        g = p_ref[0]
        for d in range(1, N_DEV):
            g = g + p_ref[d]
        delta, m2, v2 = _adam(w_ref[...], g, m_ref[...], v_ref[...])
        go_ref[...] = g
        d_ref[...] = delta
        mo_ref[...] = m2
        vo_ref[...] = v2

    vm = pl.BlockSpec(memory_space=pltpu.VMEM)
    vs = jax.ShapeDtypeStruct(wp.shape, F32)
    return pl.pallas_call(body, out_shape=(vs, vs, vs, vs), in_specs=[vm] * 4, out_specs=(vm,) * 4,
                          name='small_sum_update', compiler_params=_params(None, 48))(parts, wp, mp, vp)


def _big_update(parts, w, m, v, name):
    _, R, C = parts.shape
    tr = R if R % 256 else (128 if C >= 2048 else 256)

    def body(p_ref, w_ref, m_ref, v_ref, g_ref, d_ref, mo_ref, vo_ref):
        g = p_ref[0].astype(F32)
        for d in range(1, N_DEV):
            g = g + p_ref[d].astype(F32)
        delta, m2, v2 = _adam(w_ref[...], g, m_ref[...], v_ref[...])
        g_ref[...] = g
        d_ref[...] = delta
        mo_ref[...] = m2
        vo_ref[...] = v2

    blk = pl.BlockSpec((tr, C), lambda i: (i, 0))
    sh = jax.ShapeDtypeStruct((R, C), F32)
    return pl.pallas_call(
        body, out_shape=(sh, sh, sh, sh), grid=(R // tr,),
        in_specs=[pl.BlockSpec((N_DEV, tr, C), lambda i: (0, i, 0)), blk, blk, blk], out_specs=(blk,) * 4,
        name=name, compiler_params=_params(('parallel',), 48))(parts, w, m, v)


def _ada_update(c_all, dmod_cols, w, m, v):
    C = w.shape[1]
    tr = 256

    def body(c_ref, dm_ref, w_ref, m_ref, v_ref, g_ref, d_ref, mo_ref, vo_ref):
        cv = c_ref[...]
        s = cv * jax.nn.sigmoid(cv)
        g = lax.dot_general(s, dm_ref[...], TN_DIMS, preferred_element_type=F32, precision=HI)
        delta, m2, v2 = _adam(w_ref[...], g, m_ref[...], v_ref[...])
        g_ref[...] = g
        d_ref[...] = delta
        mo_ref[...] = m2
        vo_ref[...] = v2

    blk = pl.BlockSpec((tr, C), lambda i: (i, 0))
    sh = jax.ShapeDtypeStruct((D, C), F32)
    return pl.pallas_call(
        body, out_shape=(sh, sh, sh, sh), grid=(D // tr,),
        in_specs=[pl.BlockSpec((N_DEV, tr), lambda i: (0, i)), pl.BlockSpec((N_DEV, C), lambda i: (0, 0)), blk, blk, blk],
        out_specs=(blk,) * 4, name='ada_update', compiler_params=_params(('parallel',), 48))(c_all, dmod_cols, w, m, v)


def _to_sub(t, d):
    if d == 1:
        return t
    return t.reshape(L // d, d, t.shape[-1]).transpose(1, 0, 2).reshape(L, t.shape[-1])


def _from_sub(t, d):
    if d == 1:
        return t
    return t.reshape(d, L // d, t.shape[-1]).transpose(1, 0, 2).reshape(L, t.shape[-1])


def _rows_to_cluster_lanes(t):
    k = t.shape[1]
    return t.reshape(N_CL, CL_S, k).transpose(0, 2, 1)


def _blockdiag_in(t):
    t = t.reshape(N_CL, CL_G, SSM_N, SSM_P).transpose(0, 1, 3, 2)
    eye = jnp.eye(CL_G, dtype=t.dtype)
    t = t[:, :, :, None, :] * eye[None, :, None, :, None]
    return t.reshape(N_CL, CL_U, CL_S)


def _blockdiag_extract(t):
    t = t.reshape(N_CL, CL_G, SSM_P, CL_G, SSM_N)
    t = jnp.stack([t[:, i, :, i, :] for i in range(CL_G)], axis=1)
    return t.transpose(0, 1, 3, 2).reshape(SSM_GN, SSM_P)


def _c_to_rows(t):
    return t.transpose(0, 2, 1).reshape(SSM_GN, SSM_P)


def _rows_to_c(t):
    return t.reshape(SSM_G, SSM_N, SSM_P).transpose(0, 2, 1)


def _ssm_prep(sp):
    rows = lambda n: sp[n].reshape(SSM_GN, 1)
    a_re, a_im = rows('ssm_a_re'), rows('ssm_a_im')
    ldt = jnp.repeat(sp['ssm_log_dt'].reshape(SSM_G, 1), SSM_N, axis=0)
    b_re, b_im = sp['ssm_b_re'].reshape(SSM_GN, SSM_P), sp['ssm_b_im'].reshape(SSM_GN, SSM_P)
    c_re, c_im = _c_to_rows(sp['ssm_c_re'].reshape(SSM_G, SSM_P, SSM_N)), _c_to_rows(sp['ssm_c_im'].reshape(SSM_G, SSM_P, SSM_N))
    bbr, bbi, pwr, pwi = _ssm_pre(a_re, a_im, ldt, b_re, b_im)
    bm = jnp.concatenate([_blockdiag_in(bbr), _blockdiag_in(bbi)], axis=2)
    cmt = jnp.concatenate([_blockdiag_in(c_re), -_blockdiag_in(c_im)], axis=2)
    bmt, cm = bm.transpose(0, 2, 1), cmt.transpose(0, 2, 1)
    pw = jnp.concatenate([_rows_to_cluster_lanes(pwr), _rows_to_cluster_lanes(pwi)], axis=2)
    bm2, cm2 = _hilo(bm), _hilo(cm)
    bmt_b, cmt_b = bmt.astype(BF16), cmt.astype(BF16)
    return a_re, a_im, ldt, b_re, b_im, bm2, cm2, bmt_b, cmt_b, pw


def _tied(v, deps):
    for t in deps:
        v = v + t[0, 0]
    return v


def _local_step(x, pos, mod, tgt, sp, prep, get_w, emit, emit_small, emit_late, first_deps=()):
    sh1, sc1, gt1, sh2, sc2, gt2 = (mod[i:i + 1] for i in range(NMOD))
    vec = lambda n: sp[n].reshape(1, -1)
    a_re, a_im, ldt, b_re, b_im, bm2, cm2, bmt_b, cmt_b, pw = prep
    dvec = vec('ssm_d')

    h1 = _prenorm_fwd(x, vec('g_pre_mix'), sc1, sh1)
    w_in = get_w('w_in', (h1, bm2, cm2, pw, bmt_b, cmt_b))
    proj = _mm(h1, w_in, mode='nn', name='mm_in', tn=1408, deps=first_deps)
    fr1 =ROPE_THETA ** (-jnp.arange(0, ROT_DIM, 2, dtype=F32) / ROT_DIM)
    lane = jnp.arange(128) % HEAD_DIM
    fr = jnp.where(lane < ROT_DIM, fr1[lane % (ROT_DIM // 2)], 0.0).reshape(1, 128).astype(F32)
    u_off = (ROPE_W + KVW) // CL_U
    qkvs = _rope_fwd(proj, pos, fr)
    fwd = [_attn_fwd(qkvs[g], d) for g, d in enumerate(DILATIONS)]
    os_, lses = [t[0] for t in fwd], [t[1] for t in fwd]
    att = _combine_fwd(os_, lses, vec('g_attn_out'))

    ypre, bnd, states = _ssm_fwd(proj, bm2, cm2, pw, dvec, u_off)
    w_glu = get_w('w_glu', ypre)
    ssm_n = _glu_fwd(ypre, w_glu, vec('b_glu'), vec('g_ssm_out'))

    cat = jnp.concatenate([att, ssm_n], axis=1)
    w_out = get_w('w_out', cat)
    mix = _mm(cat, w_out, mode='nn', name='mm_out', tk=1280)
    x1, h2 = _postmix_fwd(x, mix, vec('g_post_mix'), gt1, vec('g_pre_mlp'), sc2, sh2)
    w_mi = get_w('w_mlp_in', h2)
    a_pre, r_act = _mm(h2, w_mi, mode='nn', name='mm_mlp_in', epilogue='relu2', b_sharded=True)
    w_mo = get_w('w_mlp_out', a_pre)
    y = _mm(r_act, w_mo, mode='nn', name='mm_mlp_out')
    dx2, dy, loss, dgt2, dg_post_mlp = _final_fwd_bwd(x1, y, tgt, vec('g_post_mlp'), gt2)
    dgt2, dg_post_mlp = dgt2[:1], dg_post_mlp[:1]

    da = _mm(dy, w_mo, mode='nt', name='mm_d_act', out_dtype=BF16, epilogue='drelu2', extra=a_pre)
    dep = emit('w_mlp_out', _mm(r_act, dy, mode='tn', name='mm_dw_mlp_out', out_dtype=BF16))
    dh2 = _mm(da, w_mi, mode='nt', name='mm_dh2', tn=2048, tk=1024, b_sharded=True, deps=dep)
    dep = emit('w_mlp_in', _mm(h2, da, mode='tn', name='mm_dw_mlp_in', out_dtype=BF16, out_sharded=True))
    dx1, dmix, dsc2, dsh2, dg_pre_mlp, dgt1, dg_post_mix = _postmix_bwd(
        dx2, dh2, x1, mix, vec('g_post_mix'), gt1, vec('g_pre_mlp'), sc2)
    dsc2, dsh2, dg_pre_mlp, dgt1, dg_post_mix = (t[:1] for t in (dsc2, dsh2, dg_pre_mlp, dgt1, dg_post_mix))
    dcat = _mm(dmix, w_out, mode='nt', name='mm_dcat', tn=1280, deps=dep)
    dep = emit('w_out', _mm(cat, dmix, mode='tn', name='mm_dw_out', out_dtype=BF16, tm=640))
    dsn = dcat[:, KVW:]

    dypre, g_w_glu, g_b_glu, g_g_ssm = _glu_bwd(ypre, dsn, w_glu, _tied(vec('b_glu'), dep), vec('g_ssm_out'))
    g_b_glu, g_g_ssm = g_b_glu[:1], g_g_ssm[:1]
    dep = dep + emit('w_glu', g_w_glu.astype(BF16))
    du, dbm, dcm, dA, dD = _ssm_bwd(proj, dypre, states, bmt_b, cmt_b, pw, dvec, bnd, u_off)
    dD = dD[:1]
    gbr, gbi = _blockdiag_extract(dbm[:, :, :CL_S]), _blockdiag_extract(dbm[:, :, CL_S:])
    dcmt = dcm.transpose(0, 2, 1)
    g_c_re = _rows_to_c(_blockdiag_extract(dcmt[:, :, :CL_S]))
    g_c_im = _rows_to_c(-_blockdiag_extract(dcmt[:, :, CL_S:]))
    gar = dA[:, 0, :CL_S].reshape(SSM_GN, 1)
    gai = dA[:, 0, CL_S:].reshape(SSM_GN, 1)
    sel = (jnp.arange(SSM_GN)[None, :] // SSM_N == jnp.arange(SSM_G)[:, None]).astype(F32)
    g_a_re, g_a_im, g_b_re, g_b_im, g_ldt = _ssm_post(a_re, a_im, ldt, b_re, b_im, gar, gai, gbr, gbi, sel)

    head_ones = (jnp.arange(KVW)[:, None] // HEAD_DIM == jnp.arange(KVW)[None, :] // HEAD_DIM).astype(F32)
    dos, dlses, g_g_attn = _combine_bwd(dcat, os_, lses, vec('g_attn_out'), head_ones)
    g_g_attn = g_g_attn[:1]
    dep_small = emit_small({
        'g_post_mix': dg_post_mix, 'ssm_a_re': g_a_re, 'ssm_a_im': g_a_im, 'ssm_log_dt': g_ldt[:, 0],
        'ssm_b_re': g_b_re, 'ssm_b_im': g_b_im, 'ssm_c_re': g_c_re, 'ssm_c_im': g_c_im, 'ssm_d': dD, 'b_glu': g_b_glu,
        'g_attn_out': g_g_attn, 'g_ssm_out': g_g_ssm, 'g_pre_mlp': dg_pre_mlp, 'g_post_mlp': dg_post_mlp})
    dqkv = [_attn_bwd(qkvs[g], os_[g], lses[g], dos[g], dlses[g], d) for g, d in enumerate(DILATIONS)]
    dproj = _rope_bwd(dqkv, du, pos, _tied(fr, dep_small))
    dh1 = _mm(dproj, w_in, mode='nt', name='mm_dh1', tk=1408, deps=dep)
    grad_x, dsc1, dsh1, dg_pre_mix = _prenorm_bwd(dx1, dh1, x, vec('g_pre_mix'), sc1)
    dsc1, dsh1, dg_pre_mix = dsc1[:1], dsh1[:1], dg_pre_mix[:1]
    dmod = jnp.concatenate([dsh1, dsc1, dgt1, dsh2, dsc2, dgt2], axis=0)
    dep = emit_late({'b_ada': dmod, 'g_pre_mix': dg_pre_mix})
    emit('w_in', _mm(h1, dproj, mode='tn', name='mm_dw_in', out_dtype=BF16, tn=1408, deps=dep))
    return loss[0, 0], grad_x


def _pack(d, names):
    flat = jnp.concatenate([jnp.pad(d[n].reshape(-1).astype(F32), (0, SEG[n] - SMALL_SIZES[n])) for n in names])
    return flat.reshape(-1, 128)


def _unpack(packed, names, shapes):
    out, off = {}, 0
    for n in names:
        out[n] = packed[off // 128:(off + SEG[n]) // 128].reshape(-1)[:SMALL_SIZES[n]].reshape(shapes[n])
        off += SEG[n]
    return out


def _shard_major(t, name):
    if name in ('w_in', 'w_out', 'w_mlp_in'):
        k, n = t.shape
        return t.reshape(k, N_DEV, n // N_DEV).transpose(1, 0, 2)
    k, n = t.shape
    return t.reshape(N_DEV, k // N_DEV, n)


def _from_shard_major(t, name):
    if name in ('w_in', 'w_out', 'w_mlp_in'):
        _, k, n = t.shape
        return t.transpose(1, 0, 2).reshape(k, N_DEV * n)
    _, k, n = t.shape
    return t.reshape(N_DEV * k, n)


def kernel(x, c, positions, w_ada, b_ada, g_pre_mix, g_post_mix, w_in, ssm_a_re, ssm_a_im, ssm_log_dt, ssm_b_re, ssm_b_im, ssm_c_re, ssm_c_im, ssm_d, w_glu, b_glu, g_attn_out, g_ssm_out, w_out, g_pre_mlp, g_post_mlp, w_mlp_in, w_mlp_out, loss_target, m_w_ada, m_b_ada, m_g_pre_mix, m_g_post_mix, m_w_in, m_ssm_a_re, m_ssm_a_im, m_ssm_log_dt, m_ssm_b_re, m_ssm_b_im, m_ssm_c_re, m_ssm_c_im, m_ssm_d, m_w_glu, m_b_glu, m_g_attn_out, m_g_ssm_out, m_w_out, m_g_pre_mlp, m_g_post_mlp, m_w_mlp_in, m_w_mlp_out, v_w_ada, v_b_ada, v_g_pre_mix, v_g_post_mix, v_w_in, v_ssm_a_re, v_ssm_a_im, v_ssm_log_dt, v_ssm_b_re, v_ssm_b_im, v_ssm_c_re, v_ssm_c_im, v_ssm_d, v_w_glu, v_b_glu, v_g_attn_out, v_g_ssm_out, v_w_out, v_g_pre_mlp, v_g_post_mlp, v_w_mlp_in, v_w_mlp_out):
    loc = dict(locals())
    W = {n: loc[n] for n in WEIGHTS}
    M = {n: loc['m_' + n] for n in WEIGHTS}
    V = {n: loc['v_' + n] for n in WEIGHTS}
    assert x.shape == (1, L, D) and w_in.shape == (1, D, INW // N_DEV), (x.shape, w_in.shape)

    cw = NMOD * D // N_DEV
    c_all, mod8 = _mod_exchange(c.reshape(1, 1, D), w_ada[0], b_ada.reshape(N_DEV, 1, cw))
    mod = mod8.reshape(NMOD, D)

    gather, after = {}, mod8
    for n in BIG:
        gather[n] = _push_start(W[n][0].astype(BF16), False, after, 'gather_start_' + n)
        after = gather[n][4]
    tokens = tuple(gather[n][4] for n in BIG)
    mod = _tied(mod, tokens)
    sp = {n: W[n][0] for n in SMALL}
    prep = _ssm_prep({**sp, 'ssm_a_re': _tied(sp['ssm_a_re'], tokens)})

    def get_w(n, after):
        g = _push_wait(gather[n], after, 'gather_wait_' + n)
        return g if n == 'w_mlp_in' else _from_shard_major(g, n)

    scatter = {}

    def emit(n, g):
        src = g if n == 'w_mlp_in' else _shard_major(g, n)
        scatter[n] = _push_start(src, True, src, 'scatter_start_' + n)
        return (scatter[n][4],)

    small_early = []

    def emit_small(d):
        pack = _pack(d, SMALL_EARLY)
        small_early.append(_push_start(pack, False, pack, 'small_start'))
        return (small_early[0][4],)

    out_g, out_d, out_m, out_v = {}, {}, {}, {}
    shapes = {n: W[n].shape[1:] for n in SMALL}

    def put(names, packs):
        for dst, packed in zip((out_g, out_d, out_m, out_v), packs):
            dst.update(_unpack(packed, names, shapes))

    late = []

    def emit_late(d):
        rows_all, *packs = _small_update(*[_pack(t, SMALL_LATE) for t in (d, W, M, V)])
        put(SMALL_LATE, packs)
        late.append(rows_all)
        return (rows_all,)

    loss, grad_x = _local_step(x[0], positions.reshape(L, 1), mod, loss_target[0], sp, prep, get_w, emit, emit_small,
                               emit_late)
    loss = lax.psum(loss, ('x', 'y', 'c'))

    me = 4 * lax.axis_index('x') + 2 * lax.axis_index('y') + lax.axis_index('c')
    dmod_all = late[0][:, :NMOD * D // 128].reshape(N_DEV, NMOD * D)
    dmod_cols = _tied(lax.dynamic_slice_in_dim(dmod_all, me * cw, cw, axis=1), (scatter['w_in'][4],))
    out_g['w_ada'], out_d['w_ada'], out_m['w_ada'], out_v['w_ada'] = _ada_update(
        c_all.reshape(N_DEV, D), dmod_cols, w_ada[0], m_w_ada[0], v_w_ada[0])

    parts = _push_wait(small_early[0], out_v['w_ada'], 'small_wait')
    packs = _small_sum_update(parts, *[_pack(d, SMALL_EARLY) for d in (W, M, V)])
    put(SMALL_EARLY, packs)

    after = packs[3]
    for n in ('w_mlp_out', 'w_mlp_in', 'w_out', 'w_glu', 'w_in'):
        parts = _push_wait(scatter[n], after, 'scatter_wait_' + n)
        out_g[n], out_d[n], out_m[n], out_v[n] = _big_update(parts, W[n][0], M[n][0], V[n][0], 'update_' + n)
        after = out_v[n]

    lead = lambda t: t[None]
    return (loss, grad_x[None], *[lead(out_g[n]) for n in WEIGHTS], *[lead(out_d[n]) for n in WEIGHTS],
            *[lead(out_m[n]) for n in WEIGHTS], *[lead(out_v[n]) for n in WEIGHTS])
```

```python
import functools
import math

import jax
import jax.numpy as jnp
from jax import lax
from jax.experimental import pallas as pl
from jax.experimental.pallas import tpu as pltpu

F32 = jnp.float32
BF16 = jnp.bfloat16
HI = lax.Precision.HIGHEST
MESH = pl.DeviceIdType.MESH

N_DEV = 8
L = 4096
D = 2048
HEAD_DIM = 64
N_GROUPS = 3
DILATIONS = (1, 4, 16)
HEADS = 6
QW = N_GROUPS * HEADS * HEAD_DIM
KVW = HEADS * HEAD_DIM
ROT_DIM = 16
ROPE_THETA = 500000.0
BLK = 128
NBLK = L // BLK
SSMW = D - QW
SSM_P = 16
SSM_G = SSMW // SSM_P
SSM_N = 64
SSM_GN = SSM_G * SSM_N
CL_G = 8
N_CL = SSM_G // CL_G
CL_U = CL_G * SSM_P
CL_S = CL_G * SSM_N
INW = QW + 2 * KVW + SSMW
OUTW = KVW + SSMW
DFF = 4 * D
NMOD = 6
EPS = 1e-6
LR, B1, B2, AEPS, WD, STEP = 0.001, 0.9, 0.999, 1e-08, 0.01, 10

T_SCAN = 1024
MB = 2 ** 20

WEIGHTS = ['w_ada', 'b_ada', 'g_pre_mix', 'g_post_mix', 'w_in', 'ssm_a_re', 'ssm_a_im', 'ssm_log_dt',
           'ssm_b_re', 'ssm_b_im', 'ssm_c_re', 'ssm_c_im', 'ssm_d', 'w_glu', 'b_glu', 'g_attn_out',
           'g_ssm_out', 'w_out', 'g_pre_mlp', 'g_post_mlp', 'w_mlp_in', 'w_mlp_out']
BIG = ['w_in', 'w_glu', 'w_out', 'w_mlp_in', 'w_mlp_out']
SMALL = [n for n in WEIGHTS if n not in BIG and n != 'w_ada']
SMALL_SIZES = {'b_ada': NMOD * D, 'g_pre_mix': D, 'g_post_mix': D, 'ssm_a_re': SSM_GN, 'ssm_a_im': SSM_GN,
               'ssm_log_dt': SSM_G, 'ssm_b_re': SSM_GN * SSM_P, 'ssm_b_im': SSM_GN * SSM_P,
               'ssm_c_re': SSM_GN * SSM_P, 'ssm_c_im': SSM_GN * SSM_P, 'ssm_d': SSMW, 'b_glu': SSMW,
               'g_attn_out': KVW, 'g_ssm_out': SSMW, 'g_pre_mlp': D, 'g_post_mlp': D}
SEG = {n: -(-SMALL_SIZES[n] // 1024) * 1024 for n in SMALL}
SMALL_LATE = ['b_ada', 'g_pre_mix']
SMALL_EARLY = [n for n in SMALL if n not in SMALL_LATE]


def _params(sem=None, vmem_mb=None):
    kw = {}
    if sem is not None:
        kw['dimension_semantics'] = sem
    if vmem_mb is not None:
        kw['vmem_limit_bytes'] = vmem_mb * MB
    return pltpu.CompilerParams(**kw)


def _vec(n):
    return pl.BlockSpec((1, n), lambda *_: (0, 0))


def _rms(x):
    return lax.rsqrt(jnp.mean(x * x, axis=-1, keepdims=True) + EPS)


def _rms_bwd(dn, n, r):
    return r * (dn - n * jnp.mean(dn * n, axis=-1, keepdims=True))


def _vec8(n):
    return pl.BlockSpec((8, n), lambda *_: (0, 0))


def _colsum(x):
    return jnp.sum(x.reshape(-1, 8, x.shape[-1]), axis=0)


def _fold8(last, *refs):
    @pl.when(last)
    def _():
        for r in refs:
            r[...] = jnp.broadcast_to(jnp.sum(r[...], axis=0, keepdims=True), r.shape)


def _mm(a, b, *, mode, name, out_dtype=F32, tm=1024, tn=1024, tk=2048, epilogue=None, extra=None,
        b_sharded=False, out_sharded=False, deps=()):
    if mode == 'nn':
        M, K = a.shape
        dims = (((1,), (0,)), ((), ()))
        a_spec = pl.BlockSpec((tm, tk), lambda i, j, k: (i, k))
        if b_sharded:
            _, K2, per = b.shape
            N, q = N_DEV * per, per // tn
            b_spec = pl.BlockSpec((None, tk, tn), lambda i, j, k: (j // q, k, j % q))
        else:
            K2, N = b.shape
            b_spec = pl.BlockSpec((tk, tn), lambda i, j, k: (k, j))
    elif mode == 'nt':
        M, K = a.shape
        dims = (((1,), (1,)), ((), ()))
        a_spec = pl.BlockSpec((tm, tk), lambda i, j, k: (i, k))
        if b_sharded:
            _, N, per = b.shape
            K2, q = N_DEV * per, per // tk
            b_spec = pl.BlockSpec((None, tn, tk), lambda i, j, k: (k // q, j, k % q))
        else:
            N, K2 = b.shape
            b_spec = pl.BlockSpec((tn, tk), lambda i, j, k: (j, k))
    else:
        (K, M), (K2, N) = a.shape, b.shape
        dims = (((0,), (0,)), ((), ()))
        a_spec = pl.BlockSpec((tk, tm), lambda i, j, k: (k, i))
        b_spec = pl.BlockSpec((tk, tn), lambda i, j, k: (k, j))
    assert K == K2 and M % tm == 0 and N % tn == 0 and K % tk == 0, (name, a.shape, b.shape, tm, tn, tk)
    nk = K // tk
    o_spec = pl.BlockSpec((tm, tn), lambda i, j, k: (i, j))
    o_dims = (M, N)
    if out_sharded:
        qo = N // N_DEV // tn
        o_spec = pl.BlockSpec((None, tm, tn), lambda i, j, k: (j // qo, i, j % qo))
        o_dims = (N_DEV, M, N // N_DEV)
    n_out = 2 if epilogue == 'relu2' else 1
    n_extra = 1 if extra is not None else 0
    n_in = 2 + n_extra + len(deps)

    def body(*refs):
        a_ref, b_ref = refs[0], refs[1]
        x_refs = refs[2:2 + n_extra]
        o_refs = refs[n_in:n_in + n_out]
        acc = refs[-1]
        k = pl.program_id(2)

        prod = lax.dot_general(a_ref[...], b_ref[...], dims, preferred_element_type=F32)

        def finish(r):
            if epilogue == 'relu2':
                o_refs[0][...] = r.astype(BF16)
                o_refs[1][...] = jnp.square(jnp.maximum(r, 0.0)).astype(BF16)
            elif epilogue == 'drelu2':
                pre = x_refs[0][...].astype(F32)
                o_refs[0][...] = (r * (2.0 * jnp.maximum(pre, 0.0))).astype(out_dtype)
            else:
                o_refs[0][...] = r.astype(out_dtype)

        if nk == 1:
            finish(prod)
        else:
            @pl.when(k == 0)
            def _():
                acc[...] = prod

            @pl.when((k > 0) & (k < nk - 1))
            def _():
                acc[...] += prod

            @pl.when(k == nk - 1)
            def _():
                finish(acc[...] + prod)

    if epilogue == 'relu2':
        out_shape = (jax.ShapeDtypeStruct((M, N), BF16), jax.ShapeDtypeStruct((M, N), BF16))
        out_specs = (o_spec, o_spec)
    else:
        out_shape = jax.ShapeDtypeStruct(o_dims, out_dtype)
        out_specs = o_spec
    args = (a, b) + ((extra,) if extra is not None else ()) + tuple(deps)
    in_specs = ([a_spec, b_spec] + ([o_spec] if extra is not None else [])
                + [pl.BlockSpec(memory_space=pl.ANY)] * len(deps))
    return pl.pallas_call(
        body, out_shape=out_shape, grid=(M // tm, N // tn, nk), in_specs=in_specs, out_specs=out_specs,
        scratch_shapes=[pltpu.VMEM((tm, tn) if nk > 1 else (8, 128), F32)], name=name,
        compiler_params=_params(('parallel', 'parallel', 'arbitrary'), 56))(*args)


TR = 256


def _rowspec(w=D):
    return pl.BlockSpec((TR, w), lambda i: (i, 0))


def _prenorm_fwd(x, g, sc, sh):
    def body(x_ref, g_ref, sc_ref, sh_ref, h_ref):
        xv = x_ref[...]
        n = xv * _rms(xv)
        h_ref[...] = ((n * g_ref[...]) * (1.0 + sc_ref[...]) + sh_ref[...]).astype(BF16)

    return pl.pallas_call(
        body, out_shape=jax.ShapeDtypeStruct((L, D), BF16), grid=(L // TR,),
        in_specs=[_rowspec(), _vec(D), _vec(D), _vec(D)], out_specs=_rowspec(), name='prenorm_fwd',
        compiler_params=_params(('parallel',), 40))(x, g, sc, sh)


def _postmix_fwd(x, mix, gpm, gt1, gpl, sc2, sh2):
    def body(x_ref, mix_ref, gpm_ref, gt1_ref, gpl_ref, sc2_ref, sh2_ref, x1_ref, h2_ref):
        mix_v = mix_ref[...]
        nm = mix_v * _rms(mix_v)
        x1 = x_ref[...] + gt1_ref[...] * (nm * gpm_ref[...])
        x1_ref[...] = x1
        n2 = x1 * _rms(x1)
        h2_ref[...] = ((n2 * gpl_ref[...]) * (1.0 + sc2_ref[...]) + sh2_ref[...]).astype(BF16)

    return pl.pallas_call(
        body, out_shape=(jax.ShapeDtypeStruct((L, D), F32), jax.ShapeDtypeStruct((L, D), BF16)), grid=(L // TR,),
        in_specs=[_rowspec(), _rowspec()] + [_vec(D)] * 5, out_specs=(_rowspec(), _rowspec()), name='postmix_fwd',
        compiler_params=_params(('parallel',), 40))(x, mix, gpm, gt1, gpl, sc2, sh2)


def _final_fwd_bwd(x1, y, tgt, g, gt2):
    def body(x1_ref, y_ref, t_ref, g_ref, gt2_ref, dx2_ref, dy_ref, loss_ref, dgt2_ref, dg_ref):
        @pl.when(pl.program_id(0) == 0)
        def _():
            loss_ref[...] = jnp.zeros_like(loss_ref)
            dgt2_ref[...] = jnp.zeros_like(dgt2_ref)
            dg_ref[...] = jnp.zeros_like(dg_ref)

        yv = y_ref[...]
        r = _rms(yv)
        n = yv * r
        ng = n * g_ref[...]
        x2 = x1_ref[...] + gt2_ref[...] * ng
        e = x2 - t_ref[...]
        loss_ref[...] += 0.5 * jnp.sum(jnp.mean(e * e, axis=-1, keepdims=True), axis=0, keepdims=True)
        dx2 = e * (1.0 / D)
        dx2_ref[...] = dx2
        dgt2_ref[...] += _colsum(dx2 * ng)
        dng = dx2 * gt2_ref[...]
        dg_ref[...] += _colsum(dng * n)
        dy_ref[...] = _rms_bwd(dng * g_ref[...], n, r).astype(BF16)
        _fold8(pl.program_id(0) == L // TR - 1, dgt2_ref, dg_ref)

    return pl.pallas_call(
        body,
        out_shape=(jax.ShapeDtypeStruct((L, D), F32), jax.ShapeDtypeStruct((L, D), BF16),
                   jax.ShapeDtypeStruct((8, 128), F32), jax.ShapeDtypeStruct((8, D), F32),
                   jax.ShapeDtypeStruct((8, D), F32)),
        grid=(L // TR,), in_specs=[_rowspec(), _rowspec(), _rowspec(), _vec(D), _vec(D)],
        out_specs=(_rowspec(), _rowspec(), _vec8(128), _vec8(D), _vec8(D)), name='final_fwd_bwd',
        compiler_params=_params(('arbitrary',), 40))(x1, y, tgt, g, gt2)


def _postmix_bwd(dx2, dh2, x1, mix, gpm, gt1, gpl, sc2):
    def body(dx2_ref, dh2_ref, x1_ref, mix_ref, gpm_ref, gt1_ref, gpl_ref, sc2_ref,
             dx1_ref, dmix_ref, dsc2_ref, dsh2_ref, dgpl_ref, dgt1_ref, dgpm_ref):
        @pl.when(pl.program_id(0) == 0)
        def _():
            for r_ in (dsc2_ref, dsh2_ref, dgpl_ref, dgt1_ref, dgpm_ref):
                r_[...] = jnp.zeros_like(r_)

        x1v = x1_ref[...]
        r2 = _rms(x1v)
        n2 = x1v * r2
        dh2v = dh2_ref[...]
        dsh2_ref[...] += _colsum(dh2v)
        dsc2_ref[...] += _colsum(dh2v * (n2 * gpl_ref[...]))
        t = dh2v * (1.0 + sc2_ref[...])
        dgpl_ref[...] += _colsum(t * n2)
        dx1 = dx2_ref[...] + _rms_bwd(t * gpl_ref[...], n2, r2)
        dx1_ref[...] = dx1
        mix_v = mix_ref[...]
        rm = _rms(mix_v)
        nm = mix_v * rm
        dgt1_ref[...] += _colsum(dx1 * (nm * gpm_ref[...]))
        u = dx1 * gt1_ref[...]
        dgpm_ref[...] += _colsum(u * nm)
        dmix_ref[...] = _rms_bwd(u * gpm_ref[...], nm, rm).astype(BF16)
        _fold8(pl.program_id(0) == L // TR - 1, dsc2_ref, dsh2_ref, dgpl_ref, dgt1_ref, dgpm_ref)

    vs = jax.ShapeDtypeStruct((8, D), F32)
    return pl.pallas_call(
        body, out_shape=(jax.ShapeDtypeStruct((L, D), F32), jax.ShapeDtypeStruct((L, D), BF16), vs, vs, vs, vs, vs),
        grid=(L // TR,), in_specs=[_rowspec()] * 4 + [_vec(D)] * 4,
        out_specs=(_rowspec(), _rowspec()) + (_vec8(D),) * 5, name='postmix_bwd',
        compiler_params=_params(('arbitrary',), 48))(dx2, dh2, x1, mix, gpm, gt1, gpl, sc2)


def _prenorm_bwd(dx1, dh1, x, g, sc1):
    def body(dx1_ref, dh1_ref, x_ref, g_ref, sc1_ref, dx_ref, dsc1_ref, dsh1_ref, dg_ref):
        @pl.when(pl.program_id(0) == 0)
        def _():
            for r_ in (dsc1_ref, dsh1_ref, dg_ref):
                r_[...] = jnp.zeros_like(r_)

        xv = x_ref[...]
        r = _rms(xv)
        n = xv * r
        dh = dh1_ref[...]
        dsh1_ref[...] += _colsum(dh)
        dsc1_ref[...] += _colsum(dh * (n * g_ref[...]))
        t = dh * (1.0 + sc1_ref[...])
        dg_ref[...] += _colsum(t * n)
        dx_ref[...] = dx1_ref[...] + _rms_bwd(t * g_ref[...], n, r)
        _fold8(pl.program_id(0) == L // TR - 1, dsc1_ref, dsh1_ref, dg_ref)

    vs = jax.ShapeDtypeStruct((8, D), F32)
    return pl.pallas_call(
        body, out_shape=(jax.ShapeDtypeStruct((L, D), F32), vs, vs, vs), grid=(L // TR,),
        in_specs=[_rowspec()] * 3 + [_vec(D)] * 2, out_specs=(_rowspec(),) + (_vec8(D),) * 3, name='prenorm_bwd',
        compiler_params=_params(('arbitrary',), 40))(dx1, dh1, x, g, sc1)


ROPE_W = QW + KVW
QKV_W = 3 * KVW
NB_KV = KVW // 128


def _rope_rotate(xv, pos, fr, sign):
    ang = pos.astype(F32) * fr
    w = lax.broadcasted_iota(jnp.int32, (1, 128), 1) % HEAD_DIM
    cs = jnp.cos(ang)
    sn = jnp.sin(ang) * sign
    s1 = jnp.where(w < ROT_DIM // 2, -sn, 0.0)
    s2 = jnp.where((w >= ROT_DIM // 2) & (w < ROT_DIM), sn, 0.0)
    width = xv.shape[1]
    rep = width // 128
    cs, s1, s2 = jnp.tile(cs, (1, rep)), jnp.tile(s1, (1, rep)), jnp.tile(s2, (1, rep))
    hi = pltpu.roll(xv, width - ROT_DIM // 2, 1)
    lo = pltpu.roll(xv, ROT_DIM // 2, 1)
    return xv * cs + hi * s1 + lo * s2


def _sub_spec(d, rows, width):
    return pl.BlockSpec((d, rows // d, width), lambda i: (0, i, 0))


def _gather_rows(scr, blocks, r, d, rows):
    return jnp.concatenate([scr.at[j][pl.ds(r, rows // d, stride=d), :] for j in blocks], axis=1)


def _scatter_rows(scr, src_ref, d, rows):
    for r in range(d):
        for j in range(NB_KV):
            scr.at[j][pl.ds(r, rows // d, stride=d), :] = src_ref[r, :, j * 128:(j + 1) * 128]


def _token_order(scr):
    return jnp.concatenate([scr[j] for j in range(NB_KV)], axis=1)


def _rope_fwd(proj, pos, fr):
    nb = (ROPE_W + KVW) // 128

    def body(x_ref, pos_ref, fr_ref, o0_ref, o1_ref, o2_ref, scr):
        y = _rope_rotate(x_ref[:, 0:ROPE_W], pos_ref[...], fr_ref[...], 1.0)
        for j in range(ROPE_W // 128):
            scr[j] = y[:, j * 128:(j + 1) * 128]
        for j in range(ROPE_W // 128, nb):
            scr[j] = x_ref[:, j * 128:(j + 1) * 128]
        kv = list(range(QW // 128, nb))
        for g, (d, o_ref) in enumerate(zip(DILATIONS, (o0_ref, o1_ref, o2_ref))):
            blocks = list(range(g * NB_KV, (g + 1) * NB_KV)) + kv
            for r in range(d):
                o_ref[r] = _gather_rows(scr, blocks, r, d, TR).astype(BF16)

    return pl.pallas_call(
        body, out_shape=tuple(jax.ShapeDtypeStruct((d, L // d, QKV_W), BF16) for d in DILATIONS), grid=(L // TR,),
        in_specs=[_rowspec(ROPE_W + KVW), pl.BlockSpec((TR, 1), lambda i: (i, 0)), _vec(128)],
        out_specs=tuple(_sub_spec(d, TR, QKV_W) for d in DILATIONS),
        scratch_shapes=[pltpu.VMEM((nb, TR, 128), F32)], name='rope_fwd',
        compiler_params=_params(('parallel',), 40))(proj, pos, fr)


def _rope_bwd(dqkv, du, pos, fr):
    def body(*refs):
        grads = [refs[3 * g:3 * g + 3] for g in range(N_GROUPS)]
        du_ref, pos_ref, fr_ref, o_ref = refs[9:13]
        scrs = refs[13:]
        dq, dk, dv = [], None, None
        for g, d in enumerate(DILATIONS):
            parts = []
            for t in range(3):
                if d == 1:
                    parts.append(grads[g][t][0])
                else:
                    scr = scrs[3 * (g - 1) + t]
                    _scatter_rows(scr, grads[g][t], d, TR)
                    parts.append(_token_order(scr))
            dq.append(parts[0])
            dk = parts[1] if dk is None else dk + parts[1]
            dv = parts[2] if dv is None else dv + parts[2]
        x = jnp.concatenate(dq + [dk], axis=1)
        o_ref[:, 0:ROPE_W] = _rope_rotate(x, pos_ref[...], fr_ref[...], -1.0).astype(BF16)
        o_ref[:, ROPE_W:ROPE_W + KVW] = dv.astype(BF16)
        o_ref[:, ROPE_W + KVW:INW] = du_ref[...].astype(BF16)

    flat = [a for grp in dqkv for a in grp]
    in_specs = [_sub_spec(d, TR, KVW) for d in DILATIONS for _ in range(3)]
    in_specs += [_rowspec(SSMW), pl.BlockSpec((TR, 1), lambda i: (i, 0)), _vec(128)]
    return pl.pallas_call(
        body, out_shape=jax.ShapeDtypeStruct((L, INW), BF16), grid=(L // TR,), in_specs=in_specs,
        out_specs=_rowspec(INW), scratch_shapes=[pltpu.VMEM((NB_KV, TR, 128), F32)] * 6, name='rope_bwd',
        compiler_params=_params(('parallel',), 48))(*flat, du, pos, fr)


def _attn_mask(nbs, b):
    first = (b & (nbs - 1)) == 0
    qi = lax.broadcasted_iota(jnp.int32, (BLK, 2 * BLK), 0)
    kj = lax.broadcasted_iota(jnp.int32, (BLK, 2 * BLK), 1)
    dist = qi + BLK - kj
    return (dist >= 0) & (dist <= BLK) & ((kj >= BLK) | jnp.logical_not(first))


def _qkv_specs():
    cur = lambda col: pl.BlockSpec((BLK, KVW), lambda b: (b, col))
    prev = lambda col: pl.BlockSpec((BLK, KVW), lambda b: (jnp.maximum(b - 1, 0), col))
    return [cur(0), prev(1), cur(1), prev(2), cur(2)]


_ROWS = pl.BlockSpec((BLK, KVW), lambda b: (b, 0))
NEG = -1e30
NT_DIMS = (((1,), (1,)), ((), ()))
TN_DIMS = (((0,), (0,)), ((), ()))


def _attn_fwd(qkv, d):
    nbs = L // d // BLK

    def body(q_ref, kp_ref, kc_ref, vp_ref, vc_ref, o_ref, lse_ref):
        valid = _attn_mask(nbs, pl.program_id(0))
        outs, lses = [], []
        for h in range(HEADS):
            hs = slice(h * HEAD_DIM, (h + 1) * HEAD_DIM)
            q = q_ref[:, hs]
            kc = jnp.concatenate([kp_ref[:, hs], kc_ref[:, hs]], axis=0)
            vc = jnp.concatenate([vp_ref[:, hs], vc_ref[:, hs]], axis=0)
            s = lax.dot_general(q, kc, NT_DIMS, preferred_element_type=F32) * 0.125
            s = jnp.where(valid, s, NEG)
            m = jnp.max(s, axis=-1, keepdims=True)
            p = jnp.exp(s - m)
            l = jnp.sum(p, axis=-1, keepdims=True)
            outs.append(jnp.dot(p.astype(BF16), vc, preferred_element_type=F32) / l)
            lses.append(jnp.broadcast_to(m + jnp.log(l), (BLK, HEAD_DIM)))
        o_ref[...] = jnp.concatenate(outs, axis=1)
        lse_ref[...] = jnp.concatenate(lses, axis=1)

    sh = jax.ShapeDtypeStruct((L, KVW), F32)
    q2 = qkv.reshape(L, QKV_W)
    o, lse = pl.pallas_call(
        body, out_shape=(sh, sh), grid=(NBLK,), in_specs=_qkv_specs(), out_specs=(_ROWS, _ROWS),
        name='attn_fwd_d%d' % d, compiler_params=_params(('parallel',), 32))(q2, q2, q2, q2, q2)
    return o.reshape(d, L // d, KVW), lse.reshape(d, L // d, KVW)


def _attn_bwd(qkv, o, lse, do, dlse, d):
    nbs = L // d // BLK

    def body(q_ref, kp_ref, kc_ref, vp_ref, vc_ref, o_ref, lse_ref, do_ref, dlse_ref, dq_ref, dk_ref, dv_ref):
        b = pl.program_id(0)

        @pl.when(b == 0)
        def _():
            dk_ref[...] = jnp.zeros_like(dk_ref)
            dv_ref[...] = jnp.zeros_like(dv_ref)

        valid = _attn_mask(nbs, b)
        prev0 = pl.multiple_of(jnp.maximum(b - 1, 0) * BLK, BLK)
        cur0 = pl.multiple_of(b * BLK, BLK)
        dqs, dks, dvs = [], [], []
        for h in range(HEADS):
            hs = slice(h * HEAD_DIM, (h + 1) * HEAD_DIM)
            q = q_ref[:, hs]
            kc = jnp.concatenate([kp_ref[:, hs], kc_ref[:, hs]], axis=0)
            vc = jnp.concatenate([vp_ref[:, hs], vc_ref[:, hs]], axis=0)
            s = lax.dot_general(q, kc, NT_DIMS, preferred_element_type=F32) * 0.125
            s = jnp.where(valid, s, NEG)
            p = jnp.exp(s - lse_ref[:, h * HEAD_DIM:h * HEAD_DIM + 1])
            do_h = do_ref[:, hs]
            delta = jnp.sum(do_h * o_ref[:, hs], axis=-1, keepdims=True)
            do_b = do_h.astype(BF16)
            dp = lax.dot_general(do_b, vc, NT_DIMS, preferred_element_type=F32)
            ds = p * (dp - delta + dlse_ref[:, h * HEAD_DIM:h * HEAD_DIM + 1])
            ds_b = (ds * 0.125).astype(BF16)
            dqs.append(jnp.dot(ds_b, kc, preferred_element_type=F32))
            dks.append(lax.dot_general(ds_b, q, TN_DIMS, preferred_element_type=F32))
            dvs.append(lax.dot_general(p.astype(BF16), do_b, TN_DIMS, preferred_element_type=F32))
        dq_ref[...] = jnp.concatenate(dqs, axis=1)
        dkc, dvc = jnp.concatenate(dks, axis=1), jnp.concatenate(dvs, axis=1)
        dk_ref[pl.ds(prev0, BLK), :] += dkc[:BLK]
        dv_ref[pl.ds(prev0, BLK), :] += dvc[:BLK]
        dk_ref[pl.ds(cur0, BLK), :] += dkc[BLK:]
        dv_ref[pl.ds(cur0, BLK), :] += dvc[BLK:]

    sh = jax.ShapeDtypeStruct((L, KVW), F32)
    whole = pl.BlockSpec((L, KVW), lambda b: (0, 0))
    q2 = qkv.reshape(L, QKV_W)
    flat = lambda t: t.reshape(L, KVW)
    outs = pl.pallas_call(
        body, out_shape=(sh, sh, sh), grid=(NBLK,), in_specs=_qkv_specs() + [_ROWS] * 4,
        out_specs=(_ROWS, whole, whole), name='attn_bwd_d%d' % d,
        compiler_params=_params(('arbitrary',), 48))(q2, q2, q2, q2, q2, flat(o), flat(lse), flat(do), flat(dlse))
    return tuple(t.reshape(d, L // d, KVW) for t in outs)


TC = 512


def _combine_weights(l0, l1, l2):
    m = jnp.maximum(jnp.maximum(l0, l1), l2)
    e0, e1, e2 = jnp.exp(l0 - m), jnp.exp(l1 - m), jnp.exp(l2 - m)
    z = e0 + e1 + e2
    return e0 / z, e1 / z, e2 / z


def _load_groups(refs, scrs):
    out = [refs[0][0]]
    for g in (1, 2):
        _scatter_rows(scrs[g - 1], refs[g], DILATIONS[g], TC)
        out.append(_token_order(scrs[g - 1]))
    return out


def _combine_fwd(os_, lses, g):
    def body(o0, o1, o2, l0, l1, l2, g_ref, att_ref, *scrs):
        ov = _load_groups((o0, o1, o2), scrs[0:2])
        lv = _load_groups((l0, l1, l2), scrs[2:4])
        w0, w1, w2 = _combine_weights(*lv)
        a = w0 * ov[0] + w1 * ov[1] + w2 * ov[2]
        att_ref[...] = ((a * _rms(a)) * g_ref[...]).astype(BF16)

    subs = [_sub_spec(d, TC, KVW) for d in DILATIONS]
    return pl.pallas_call(
        body, out_shape=jax.ShapeDtypeStruct((L, KVW), BF16), grid=(L // TC,), in_specs=subs + subs + [_vec(KVW)],
        out_specs=pl.BlockSpec((TC, KVW), lambda i: (i, 0)),
        scratch_shapes=[pltpu.VMEM((NB_KV, TC, 128), F32)] * 4, name='combine_fwd',
        compiler_params=_params(('parallel',), 40))(*os_, *lses, g)


def _combine_bwd(dcat, os_, lses, g, head_ones):
    def body(datt_ref, o0, o1, o2, l0, l1, l2, g_ref, e_ref, do0, do1, do2, dl0, dl1, dl2, dg_ref, *scrs):
        @pl.when(pl.program_id(0) == 0)
        def _():
            dg_ref[...] = jnp.zeros_like(dg_ref)

        ov = _load_groups((o0, o1, o2), scrs[0:2])
        lv = _load_groups((l0, l1, l2), scrs[2:4])
        ws = _combine_weights(*lv)
        a = ws[0] * ov[0] + ws[1] * ov[1] + ws[2] * ov[2]
        r = _rms(a)
        n = a * r
        dv = datt_ref[...]
        dg_ref[...] += _colsum(dv * n)
        da = _rms_bwd(dv * g_ref[...], n, r)
        dws = [jnp.dot(da * ov[i], e_ref[...], preferred_element_type=F32, precision=HI) for i in range(3)]
        dbar = ws[0] * dws[0] + ws[1] * dws[1] + ws[2] * dws[2]
        scr = scrs[4]
        for i, (d, do_ref, dl_ref) in enumerate(zip(DILATIONS, (do0, do1, do2), (dl0, dl1, dl2))):
            for val, out_ref in ((ws[i] * da, do_ref), (ws[i] * (dws[i] - dbar), dl_ref)):
                if d == 1:
                    out_ref[0] = val
                else:
                    for j in range(NB_KV):
                        scr[j] = val[:, j * 128:(j + 1) * 128]
                    for rr in range(d):
                        out_ref[rr] = _gather_rows(scr, range(NB_KV), rr, d, TC)
        _fold8(pl.program_id(0) == L // TC - 1, dg_ref)

    subs = [_sub_spec(d, TC, KVW) for d in DILATIONS]
    shs = tuple(jax.ShapeDtypeStruct((d, L // d, KVW), F32) for d in DILATIONS)
    outs = pl.pallas_call(
        body, out_shape=shs + shs + (jax.ShapeDtypeStruct((8, KVW), F32),), grid=(L // TC,),
        in_specs=[pl.BlockSpec((TC, KVW), lambda i: (i, 0))] + subs + subs + [_vec(KVW),
                                                                              pl.BlockSpec((KVW, KVW), lambda i: (0, 0))],
        out_specs=tuple(subs) + tuple(subs) + (_vec8(KVW),),
        scratch_shapes=[pltpu.VMEM((NB_KV, TC, 128), F32)] * 5, name='combine_bwd',
        compiler_params=_params(('arbitrary',), 48))(dcat, *os_, *lses, g, head_ones)
    return outs[0:3], outs[3:6], outs[6]


def _ssm_disc(ar, ai, ldt):
    dt = jnp.exp(ldt)
    zr, zi = ar * dt, ai * dt
    ez = jnp.exp(zr)
    A_r, A_i = ez * jnp.cos(zi), ez * jnp.sin(zi)
    den = ar * ar + ai * ai
    xr, xi = A_r - 1.0, A_i
    cr = (xr * ar + xi * ai) / den
    ci = (xi * ar - xr * ai) / den
    return dt, zr, zi, A_r, A_i, den, cr, ci


def _ssm_pre(ar, ai, ldt, br, bi):
    def body(ar_ref, ai_ref, ldt_ref, br_ref, bi_ref, bbr_ref, bbi_ref, pwr_ref, pwi_ref):
        _, zr, zi, _, _, _, cr, ci = _ssm_disc(ar_ref[...], ai_ref[...], ldt_ref[...])
        bbr_ref[...] = cr * br_ref[...] - ci * bi_ref[...]
        bbi_ref[...] = cr * bi_ref[...] + ci * br_ref[...]
        k = (lax.broadcasted_iota(jnp.int32, (1, 8), 1) + 1).astype(F32)
        ek = jnp.exp(zr * k)
        pwr_ref[...] = ek * jnp.cos(zi * k)
        pwi_ref[...] = ek * jnp.sin(zi * k)

    s16 = jax.ShapeDtypeStruct((SSM_GN, SSM_P), F32)
    s8 = jax.ShapeDtypeStruct((SSM_GN, 8), F32)
    return pl.pallas_call(body, out_shape=(s16, s16, s8, s8), name='ssm_pre',
                          compiler_params=_params(None, 40))(ar, ai, ldt, br, bi)


def _ssm_post(ar, ai, ldt, br, bi, gar, gai, gbr, gbi, sel):
    def body(ar_ref, ai_ref, ldt_ref, br_ref, bi_ref, gar_ref, gai_ref, gbr_ref, gbi_ref, sel_ref,
             dar_ref, dai_ref, dbr_ref, dbi_ref, dldt_ref):
        a_r, a_i = ar_ref[...], ai_ref[...]
        dt, _, _, A_r, A_i, den, cr, ci = _ssm_disc(a_r, a_i, ldt_ref[...])
        b_r, b_i, g_br, g_bi = br_ref[...], bi_ref[...], gbr_ref[...], gbi_ref[...]
        gcr = jnp.sum(g_br * b_r + g_bi * b_i, axis=-1, keepdims=True)
        gci = jnp.sum(g_bi * b_r - g_br * b_i, axis=-1, keepdims=True)
        dbr_ref[...] = g_br * cr + g_bi * ci
        dbi_ref[...] = g_bi * cr - g_br * ci
        g_ar = gar_ref[...] + (gcr * a_r - gci * a_i) / den
        g_ai = gai_ref[...] + (gcr * a_i + gci * a_r) / den
        qr = (cr * a_r + ci * a_i) / den
        qi = (ci * a_r - cr * a_i) / den
        glr = -(gcr * qr + gci * qi)
        gli = -(gci * qr - gcr * qi)
        gzr = g_ar * A_r + g_ai * A_i
        gzi = g_ai * A_r - g_ar * A_i
        dar_ref[...] = glr + gzr * dt
        dai_ref[...] = gli + gzi * dt
        gdt = (gzr * a_r + gzi * a_i) * dt
        dldt_ref[...] = jnp.dot(sel_ref[...], jnp.broadcast_to(gdt, (SSM_GN, 128)),
                                preferred_element_type=F32, precision=HI)

    s1 = jax.ShapeDtypeStruct((SSM_GN, 1), F32)
    s16 = jax.ShapeDtypeStruct((SSM_GN, SSM_P), F32)
    return pl.pallas_call(body, out_shape=(s1, s1, s16, s16, jax.ShapeDtypeStruct((SSM_G, 128), F32)),
                          name='ssm_post', compiler_params=_params(None, 48))(
                              ar, ai, ldt, br, bi, gar, gai, gbr, gbi, sel)


SCAN_CH = 8


def _scan_fwd_tiles(s_ref, pw, carry):
    pwr, pwi = pw[:, :CL_S], pw[:, CL_S:]
    row = lax.broadcasted_iota(jnp.int32, (8, CL_S), 0)
    steps = [(k, jnp.where(row >= k, pwr[k - 1:k], 0.0), jnp.where(row >= k, pwi[k - 1:k], 0.0)) for k in (1, 2, 4)]
    rows = 8 * SCAN_CH

    def chunk(i, c):
        cr, ci = c
        r0 = pl.multiple_of(i * rows, rows)
        xr = s_ref[pl.ds(r0, rows), 0:CL_S].reshape(SCAN_CH, 8, CL_S)
        xi = s_ref[pl.ds(r0, rows), CL_S:2 * CL_S].reshape(SCAN_CH, 8, CL_S)
        for k, pr, pi in steps:
            sr, si = pltpu.roll(xr, k, 1), pltpu.roll(xi, k, 1)
            xr, xi = xr + pr * sr - pi * si, xi + pr * si + pi * sr
        for j in range(SCAN_CH):
            tr = xr[j] + pwr * cr - pwi * ci
            ti = xi[j] + pwr * ci + pwi * cr
            s_ref[pl.ds(r0 + 8 * j, 8), 0:CL_S] = tr
            s_ref[pl.ds(r0 + 8 * j, 8), CL_S:2 * CL_S] = ti
            cr, ci = tr[7:8], ti[7:8]
        return cr, ci

    return lax.fori_loop(0, T_SCAN // rows, chunk, (carry[:, :CL_S], carry[:, CL_S:]))


def _scan_bwd_tiles(l_ref, pw, carry):
    pwr, pwi = pw[:, :CL_S], pw[:, CL_S:]
    rpr = jnp.concatenate([pwr[7 - r:8 - r] for r in range(8)], axis=0)
    rpi = jnp.concatenate([pwi[7 - r:8 - r] for r in range(8)], axis=0)
    row = lax.broadcasted_iota(jnp.int32, (8, CL_S), 0)
    steps = [(k, jnp.where(row < 8 - k, pwr[k - 1:k], 0.0), jnp.where(row < 8 - k, pwi[k - 1:k], 0.0))
             for k in (1, 2, 4)]
    rows = 8 * SCAN_CH
    nc = T_SCAN // rows

    def chunk(i, c):
        cr, ci = c
        r0 = pl.multiple_of((nc - 1 - i) * rows, rows)
        xr = l_ref[pl.ds(r0, rows), 0:CL_S].reshape(SCAN_CH, 8, CL_S)
        xi = l_ref[pl.ds(r0, rows), CL_S:2 * CL_S].reshape(SCAN_CH, 8, CL_S)
        for k, pr, pi in steps:
            sr, si = pltpu.roll(xr, 8 - k, 1), pltpu.roll(xi, 8 - k, 1)
            xr, xi = xr + pr * sr + pi * si, xi + pr * si - pi * sr
        for j in reversed(range(SCAN_CH)):
            tr = xr[j] + rpr * cr + rpi * ci
            ti = xi[j] + rpr * ci - rpi * cr
            l_ref[pl.ds(r0 + 8 * j, 8), 0:CL_S] = tr
            l_ref[pl.ds(r0 + 8 * j, 8), CL_S:2 * CL_S] = ti
            cr, ci = tr[0:1], ti[0:1]
        return cr, ci

    return lax.fori_loop(0, nc, chunk, (carry[:, :CL_S], carry[:, CL_S:]))


NT_SCAN = L // T_SCAN


def _hilo(t):
    hi = t.astype(BF16)
    return jnp.stack([hi, (t - hi.astype(F32)).astype(BF16)], axis=1)


def _dot3(a, b_ref):
    ah = a.astype(BF16)
    al = (a - ah.astype(F32)).astype(BF16)
    bh, bl = b_ref[0], b_ref[1]
    return (jnp.dot(ah, bh, preferred_element_type=F32) + jnp.dot(al, bh, preferred_element_type=F32)
            + jnp.dot(ah, bl, preferred_element_type=F32))


def _hl_spec(r, c):
    return pl.BlockSpec((None, 2, r, c), lambda c_, t: (c_, 0, 0, 0))


def _ssm_fwd(u, bm2, cm2, pw, dvec, u_off=0):
    def body(u_ref, bm_ref, cm_ref, pw_ref, d_ref, y_ref, bnd_ref, s_ref, carry_ref):
        @pl.when(pl.program_id(1) == 0)
        def _():
            carry_ref[...] = jnp.zeros_like(carry_ref)

        bnd_ref[...] = carry_ref[...]
        uv = u_ref[...]
        s_ref[...] = jnp.dot(uv.astype(BF16), bm_ref[0], preferred_element_type=F32)
        cr, ci = _scan_fwd_tiles(s_ref, pw_ref[...], carry_ref[...])
        carry_ref[...] = jnp.concatenate([cr, ci], axis=1)
        y_ref[...] = jnp.dot(s_ref[...].astype(BF16), cm_ref[0], preferred_element_type=F32) + d_ref[...] * uv

    return pl.pallas_call(
        body,
        out_shape=(jax.ShapeDtypeStruct((L, SSMW), F32), jax.ShapeDtypeStruct((N_CL, NT_SCAN, 1, 2 * CL_S), F32),
                   jax.ShapeDtypeStruct((L, N_CL * 2 * CL_S), F32)),
        grid=(N_CL, NT_SCAN),
        in_specs=[pl.BlockSpec((T_SCAN, CL_U), lambda c, t: (t, c + u_off)),
                  _hl_spec(CL_U, 2 * CL_S), _hl_spec(2 * CL_S, CL_U),
                  pl.BlockSpec((None, 8, 2 * CL_S), lambda c, t: (c, 0, 0)),
                  pl.BlockSpec((1, CL_U), lambda c, t: (0, c))],
        out_specs=(pl.BlockSpec((T_SCAN, CL_U), lambda c, t: (t, c)),
                   pl.BlockSpec((None, None, 1, 2 * CL_S), lambda c, t: (c, t, 0, 0)),
                   pl.BlockSpec((T_SCAN, 2 * CL_S), lambda c, t: (t, c))),
        scratch_shapes=[pltpu.VMEM((1, 2 * CL_S), F32)],
        name='ssm_fwd', compiler_params=_params(('arbitrary', 'arbitrary'), 40))(u, bm2, cm2, pw, dvec)


def _ssm_bwd(u, dy, states, bmt, cmt, pw, dvec, bnd, u_off=0):
    rev = lambda t: NT_SCAN - 1 - t

    def body(u_ref, dy_ref, s_ref, bmt_ref, cmt_ref, pw_ref, d_ref, bnd_ref,
             du_ref, dbm_ref, dcm_ref, da_ref, dd_ref, l_ref, carry_ref):
        @pl.when(pl.program_id(1) == 0)
        def _():
            carry_ref[...] = jnp.zeros_like(carry_ref)
            dbm_ref[...] = jnp.zeros_like(dbm_ref)
            dcm_ref[...] = jnp.zeros_like(dcm_ref)
            da_ref[...] = jnp.zeros_like(da_ref)
            dd_ref[...] = jnp.zeros_like(dd_ref)

        uv, dyv, pw = u_ref[...], dy_ref[...], pw_ref[...]
        dy_b = dyv.astype(BF16)
        entry = bnd_ref[...]
        l_ref[...] = jnp.dot(dy_b, cmt_ref[...], preferred_element_type=F32)
        cr, ci = _scan_bwd_tiles(l_ref, pw, carry_ref[...])
        carry_ref[...] = jnp.concatenate([cr, ci], axis=1)
        sv, lv = s_ref[...], l_ref[...]
        lv_b = lv.astype(BF16)
        du_ref[...] = dyv * d_ref[...] + jnp.dot(lv_b, bmt_ref[...], preferred_element_type=F32)
        dbm_ref[...] += lax.dot_general(uv.astype(BF16), lv_b, TN_DIMS, preferred_element_type=F32)
        dcm_ref[...] += lax.dot_general(sv.astype(BF16), dy_b, TN_DIMS, preferred_element_type=F32)
        dd_ref[...] += _colsum(dyv * uv)
        row = lax.broadcasted_iota(jnp.int32, (T_SCAN, 2 * CL_S), 0)
        sp = jnp.where(row == 0, entry, pltpu.roll(sv, 1, 0))
        spr, spi = sp[:, :CL_S], sp[:, CL_S:]
        lr, li = lv[:, :CL_S], lv[:, CL_S:]
        da_ref[:, 0:CL_S] += _colsum(lr * spr + li * spi)
        da_ref[:, CL_S:2 * CL_S] += _colsum(li * spr - lr * spi)
        _fold8(pl.program_id(1) == NT_SCAN - 1, da_ref, dd_ref)

    return pl.pallas_call(
        body,
        out_shape=(jax.ShapeDtypeStruct((L, SSMW), F32), jax.ShapeDtypeStruct((N_CL, CL_U, 2 * CL_S), F32),
                   jax.ShapeDtypeStruct((N_CL, 2 * CL_S, CL_U), F32), jax.ShapeDtypeStruct((N_CL, 8, 2 * CL_S), F32),
                   jax.ShapeDtypeStruct((8, SSMW), F32)),
        grid=(N_CL, NT_SCAN),
        in_specs=[pl.BlockSpec((T_SCAN, CL_U), lambda c, t: (rev(t), c + u_off)),
                  pl.BlockSpec((T_SCAN, CL_U), lambda c, t: (rev(t), c)),
                  pl.BlockSpec((T_SCAN, 2 * CL_S), lambda c, t: (rev(t), c)),
                  pl.BlockSpec((None, 2 * CL_S, CL_U), lambda c, t: (c, 0, 0)),
                  pl.BlockSpec((None, CL_U, 2 * CL_S), lambda c, t: (c, 0, 0)),
                  pl.BlockSpec((None, 8, 2 * CL_S), lambda c, t: (c, 0, 0)),
                  pl.BlockSpec((1, CL_U), lambda c, t: (0, c)),
                  pl.BlockSpec((None, None, 1, 2 * CL_S), lambda c, t: (c, rev(t), 0, 0))],
        out_specs=(pl.BlockSpec((T_SCAN, CL_U), lambda c, t: (rev(t), c)),
                   pl.BlockSpec((None, CL_U, 2 * CL_S), lambda c, t: (c, 0, 0)),
                   pl.BlockSpec((None, 2 * CL_S, CL_U), lambda c, t: (c, 0, 0)),
                   pl.BlockSpec((None, 8, 2 * CL_S), lambda c, t: (c, 0, 0)),
                   pl.BlockSpec((8, CL_U), lambda c, t: (0, c))),
        scratch_shapes=[pltpu.VMEM((T_SCAN, 2 * CL_S), F32), pltpu.VMEM((1, 2 * CL_S), F32)],
        name='ssm_bwd', compiler_params=_params(('arbitrary', 'arbitrary'), 48))(u, dy, states, bmt, cmt, pw, dvec, bnd)


GELU_C = math.sqrt(2.0 / math.pi)
GELU_K = 0.044715


def _gelu_parts(x):
    t = jnp.tanh(GELU_C * (x + GELU_K * (x * x * x)))
    return x * (0.5 * (1.0 + t)), t


def _glu_fwd(ypre, wglu, bglu, gs):
    def body(y_ref, w_ref, b_ref, g_ref, o_ref):
        yg, _ = _gelu_parts(y_ref[...])
        z = jnp.dot(yg.astype(BF16), w_ref[...], preferred_element_type=F32) + b_ref[...]
        s = yg * jax.nn.sigmoid(z)
        o_ref[...] = ((s * _rms(s)) * g_ref[...]).astype(BF16)

    return pl.pallas_call(
        body, out_shape=jax.ShapeDtypeStruct((L, SSMW), BF16), grid=(L // TR,),
        in_specs=[_rowspec(SSMW), pl.BlockSpec((SSMW, SSMW), lambda i: (0, 0)), _vec(SSMW), _vec(SSMW)],
        out_specs=_rowspec(SSMW), name='glu_fwd', compiler_params=_params(('parallel',), 32))(ypre, wglu, bglu, gs)


def _glu_bwd(ypre, dsn, wglu, bglu, gs):
    def body(y_ref, d_ref, w_ref, b_ref, g_ref, dy_ref, dw_ref, db_ref, dg_ref):
        @pl.when(pl.program_id(0) == 0)
        def _():
            dw_ref[...] = jnp.zeros_like(dw_ref)
            db_ref[...] = jnp.zeros_like(db_ref)
            dg_ref[...] = jnp.zeros_like(dg_ref)

        xv = y_ref[...]
        yg, t = _gelu_parts(xv)
        yg_b = yg.astype(BF16)
        z = jnp.dot(yg_b, w_ref[...], preferred_element_type=F32) + b_ref[...]
        sg = jax.nn.sigmoid(z)
        s = yg * sg
        r = _rms(s)
        n = s * r
        dv = d_ref[:, d_ref.shape[1] - SSMW:]
        dg_ref[...] += _colsum(dv * n)
        ds = _rms_bwd(dv * g_ref[...], n, r)
        dz =(ds * yg) * (sg * (1.0 - sg))
        dz_b = dz.astype(BF16)
        db_ref[...] += _colsum(dz)
        dw_ref[...] += lax.dot_general(yg_b, dz_b, TN_DIMS, preferred_element_type=F32)
        dyg = ds * sg + lax.dot_general(dz_b, w_ref[...], NT_DIMS, preferred_element_type=F32)
        dgelu = 0.5 * (1.0 + t) + (0.5 * xv) * (1.0 - t * t) * (GELU_C * (1.0 + 3.0 * GELU_K * (xv * xv)))
        dy_ref[...] = dyg * dgelu
        _fold8(pl.program_id(0) == L // TR - 1, db_ref, dg_ref)

    vs = jax.ShapeDtypeStruct((8, SSMW), F32)
    return pl.pallas_call(
        body, out_shape=(jax.ShapeDtypeStruct((L, SSMW), F32), jax.ShapeDtypeStruct((SSMW, SSMW), F32), vs, vs),
        grid=(L // TR,),
        in_specs=[_rowspec(SSMW), _rowspec(dsn.shape[1]), pl.BlockSpec((SSMW, SSMW), lambda i: (0, 0)), _vec(SSMW),
                  _vec(SSMW)],
        out_specs=(_rowspec(SSMW), pl.BlockSpec((SSMW, SSMW), lambda i: (0, 0)), _vec8(SSMW), _vec8(SSMW)),
        name='glu_bwd', compiler_params=_params(('arbitrary',), 40))(ypre, dsn, wglu, bglu, gs)


def _me():
    return lax.axis_index('x'), lax.axis_index('y'), lax.axis_index('c')


def _my_index():
    return 4 * lax.axis_index('x') + 2 * lax.axis_index('y') + lax.axis_index('c')


def _peer(k):
    x, y, c = _me()
    px = 1 - x if k & 4 else x
    py = 1 - y if k & 2 else y
    pc = 1 - c if k & 1 else c
    return (px, py, pc), 4 * px + 2 * py + pc


def _mod_exchange(c_row, w_ada, b_ada8, deps=()):
    cw = NMOD * D // N_DEV

    def body(c_ref, w_ref, b_ref, *rest):
        call_ref, mod_ref, part_ref, send_sems, recv_sems = rest[len(deps):]
        x, y, c = _me()
        me = 4 * x + 2 * y + c
        call_ref[me] = c_ref[0]
        sends = []
        for k in range(1, N_DEV):
            peer, _ = _peer(k)
            cp = pltpu.make_async_remote_copy(src_ref=c_ref.at[0], dst_ref=call_ref.at[me], send_sem=send_sems.at[0, k - 1],
                                              recv_sem=recv_sems.at[0, k - 1], device_id=peer, device_id_type=MESH)
            cp.start()
            sends.append(cp)
        for k in range(1, N_DEV):
            peer, pidx = _peer(k)
            pltpu.make_async_remote_copy(src_ref=c_ref.at[0], dst_ref=call_ref.at[pidx], send_sem=send_sems.at[0, k - 1],
                                         recv_sem=recv_sems.at[0, k - 1], device_id=peer, device_id_type=MESH).wait_recv()
        for cp in sends:
            cp.wait_send()
        cv = call_ref[...].reshape(N_DEV, D)
        part = jnp.dot(cv * jax.nn.sigmoid(cv), w_ref[...], preferred_element_type=F32, precision=HI)
        part_ref[...] = part.reshape(N_DEV, 1, cw)
        mod_ref[me] = part_ref[me]
        sends = []
        for k in range(1, N_DEV):
            peer, pidx = _peer(k)
            cp = pltpu.make_async_remote_copy(src_ref=part_ref.at[pidx], dst_ref=mod_ref.at[me], send_sem=send_sems.at[1, k - 1],
                                              recv_sem=recv_sems.at[1, k - 1], device_id=peer, device_id_type=MESH)
            cp.start()
            sends.append(cp)
        for k in range(1, N_DEV):
            peer, pidx = _peer(k)
            pltpu.make_async_remote_copy(src_ref=part_ref.at[pidx], dst_ref=mod_ref.at[pidx], send_sem=send_sems.at[1, k - 1],
                                         recv_sem=recv_sems.at[1, k - 1], device_id=peer, device_id_type=MESH).wait_recv()
        for cp in sends:
            cp.wait_send()
        mod_ref[...] = mod_ref[...] + b_ref[...]

    vm = pl.BlockSpec(memory_space=pltpu.VMEM)
    return pl.pallas_call(
        body, out_shape=(jax.ShapeDtypeStruct((N_DEV, 1, D), F32), jax.ShapeDtypeStruct((N_DEV, 1, cw), F32)),
        in_specs=[vm, vm, vm] + [pl.BlockSpec(memory_space=pl.ANY)] * len(deps), out_specs=(vm, vm),
        scratch_shapes=[pltpu.VMEM((N_DEV, 1, cw), F32), pltpu.SemaphoreType.DMA((2, N_DEV - 1)),
                        pltpu.SemaphoreType.DMA((2, N_DEV - 1))],
        name='mod_exchange', compiler_params=_params(None, 48))(c_row, w_ada, b_ada8, *deps)


HBM_SPEC = pl.BlockSpec(memory_space=pltpu.HBM)
SEM_SPEC = pl.BlockSpec(memory_space=pltpu.SEMAPHORE)
DATAFLOW = pltpu.SideEffectType.DATAFLOW_SIDE_EFFECTING


def _push_start(src, scatter, after, name):
    land = lax.empty(src.shape if scatter else (N_DEV,) + src.shape, src.dtype)

    def body(src_ref, land_ref, after_ref, send_sem, recv_sem, land_thru, token):
        x, y, c = _me()
        me = 4 * x + 2 * y + c
        for k in range(1, N_DEV):
            peer, pidx = _peer(k)
            pltpu.make_async_remote_copy(src_ref=src_ref.at[pidx] if scatter else src_ref, dst_ref=land_ref.at[me],
                                         send_sem=send_sem, recv_sem=recv_sem, device_id=peer,
                                         device_id_type=MESH).start()
        token[...] = jnp.zeros_like(token)

    own = lax.dynamic_index_in_dim(src, _my_index(), 0, keepdims=False) if scatter else src
    src = pltpu.with_memory_space_constraint(src, pltpu.HBM)
    send_sem, recv_sem, land_thru, token = pl.pallas_call(
        body, name=name,
        out_shape=(pltpu.SemaphoreType.DMA(()), pltpu.SemaphoreType.DMA(()),
                   pltpu.HBM(land.shape, land.dtype), jax.ShapeDtypeStruct((8, 128), F32)),
        in_specs=(HBM_SPEC, HBM_SPEC, pl.BlockSpec(memory_space=pl.ANY)),
        out_specs=(SEM_SPEC, SEM_SPEC, HBM_SPEC, pl.BlockSpec(memory_space=pltpu.VMEM)),
        input_output_aliases={1: 2}, compiler_params=pltpu.CompilerParams(has_side_effects=DATAFLOW),
    )(src, pltpu.with_memory_space_constraint(land, pltpu.HBM), after)
    return send_sem, recv_sem, src, land_thru, token, own


def _push_wait(handle, after, name):
    send_sem, recv_sem, src, land_thru, _, own = handle
    after = tuple(after) if isinstance(after, (tuple, list)) else (after,)

    def body(src_ref, land_ref, send_sem, recv_sem, *rest):
        seven = land_ref.at[pl.ds(0, N_DEV - 1)]
        cp = pltpu.make_async_remote_copy(src_ref=seven, dst_ref=seven, send_sem=send_sem, recv_sem=recv_sem,
                                          device_id=_me(), device_id_type=MESH)
        cp.wait_send()
        cp.wait_recv()

    landed = pl.pallas_call(
        body, name=name, out_shape=pltpu.HBM(land_thru.shape, land_thru.dtype),
        in_specs=(HBM_SPEC, HBM_SPEC, SEM_SPEC, SEM_SPEC) + (pl.BlockSpec(memory_space=pl.ANY),) * len(after),
        out_specs=HBM_SPEC, input_output_aliases={1: 0},
        compiler_params=pltpu.CompilerParams(has_side_effects=DATAFLOW),
    )(src, land_thru, send_sem, recv_sem, *after)
    return lax.dynamic_update_index_in_dim(landed, own, _my_index(), 0)


def _adam(w, g, m, v):
    m2 = B1 * m + (1.0 - B1) * g
    v2 = B2 * v + (1.0 - B2) * jnp.square(g)
    m_hat = m2 / (1.0 - B1 ** STEP)
    v_hat = v2 / (1.0 - B2 ** STEP)
    delta = -LR * (m_hat / (jnp.sqrt(v_hat) + AEPS) + WD * w)
    return delta, m2, v2


def _small_update(gp, wp, mp, vp):
    def body(g_ref, w_ref, m_ref, v_ref, all_ref, go_ref, d_ref, mo_ref, vo_ref, send_sems, recv_sems):
        x, y, c = _me()
        me = 4 * x + 2 * y + c
        all_ref[me] = g_ref[...]
        sends = []
        for k in range(1, N_DEV):
            peer, _ = _peer(k)
            cp = pltpu.make_async_remote_copy(src_ref=g_ref, dst_ref=all_ref.at[me], send_sem=send_sems.at[k - 1],
                                              recv_sem=recv_sems.at[k - 1], device_id=peer, device_id_type=MESH)
            cp.start()
            sends.append(cp)
        for k in range(1, N_DEV):
            peer, pidx = _peer(k)
            pltpu.make_async_remote_copy(src_ref=g_ref, dst_ref=all_ref.at[pidx], send_sem=send_sems.at[k - 1],
                                         recv_sem=recv_sems.at[k - 1], device_id=peer, device_id_type=MESH).wait_recv()
        for cp in sends:
            cp.wait_send()
        g = all_ref[0]
        for d in range(1, N_DEV):
            g = g + all_ref[d]
        delta, m2, v2 = _adam(w_ref[...], g, m_ref[...], v_ref[...])
        go_ref[...] = g
        d_ref[...] = delta
        mo_ref[...] = m2
        vo_ref[...] = v2

    vm = pl.BlockSpec(memory_space=pltpu.VMEM)
    vs = jax.ShapeDtypeStruct(gp.shape, F32)
    return pl.pallas_call(
        body, out_shape=(jax.ShapeDtypeStruct((N_DEV,) + gp.shape, F32), vs, vs, vs, vs), in_specs=[vm] * 4,
        out_specs=(vm,) * 5,
        scratch_shapes=[pltpu.SemaphoreType.DMA((N_DEV - 1,)), pltpu.SemaphoreType.DMA((N_DEV - 1,))],
        name='small_update', compiler_params=_params(None, 48))(gp, wp, mp, vp)


def _small_sum_update(parts, wp, mp, vp):
    def body(p_ref, w_ref, m_ref, v_ref, go_ref, d_ref, mo_ref, vo_ref):
        g = p_ref[0]
        for d in range(1, N_DEV):
            g = g + p_ref[d]
        delta, m2, v2 = _adam(w_ref[...], g, m_ref[...], v_ref[...])
        go_ref[...] = g
        d_ref[...] = delta
        mo_ref[...] = m2
        vo_ref[...] = v2

    vm = pl.BlockSpec(memory_space=pltpu.VMEM)
    vs = jax.ShapeDtypeStruct(wp.shape, F32)
    return pl.pallas_call(body, out_shape=(vs, vs, vs, vs), in_specs=[vm] * 4, out_specs=(vm,) * 4,
                          name='small_sum_update', compiler_params=_params(None, 48))(parts, wp, mp, vp)


def _big_update(parts, w, m, v, name):
    _, R, C = parts.shape
    tr = R if R % 256 else (128 if C >= 2048 else 256)

    def body(p_ref, w_ref, m_ref, v_ref, g_ref, d_ref, mo_ref, vo_ref):
        g = p_ref[0].astype(F32)
        for d in range(1, N_DEV):
            g = g + p_ref[d].astype(F32)
        delta, m2, v2 = _adam(w_ref[...], g, m_ref[...], v_ref[...])
        g_ref[...] = g
        d_ref[...] = delta
        mo_ref[...] = m2
        vo_ref[...] = v2

    blk = pl.BlockSpec((tr, C), lambda i: (i, 0))
    sh = jax.ShapeDtypeStruct((R, C), F32)
    return pl.pallas_call(
        body, out_shape=(sh, sh, sh, sh), grid=(R // tr,),
        in_specs=[pl.BlockSpec((N_DEV, tr, C), lambda i: (0, i, 0)), blk, blk, blk], out_specs=(blk,) * 4,
        name=name, compiler_params=_params(('parallel',), 48))(parts, w, m, v)


def _ada_update(c_all, dmod_cols, w, m, v):
    C = w.shape[1]
    tr = 256

    def body(c_ref, dm_ref, w_ref, m_ref, v_ref, g_ref, d_ref, mo_ref, vo_ref):
        cv = c_ref[...]
        s = cv * jax.nn.sigmoid(cv)
        g = lax.dot_general(s, dm_ref[...], TN_DIMS, preferred_element_type=F32, precision=HI)
        delta, m2, v2 = _adam(w_ref[...], g, m_ref[...], v_ref[...])
        g_ref[...] = g
        d_ref[...] = delta
        mo_ref[...] = m2
        vo_ref[...] = v2

    blk = pl.BlockSpec((tr, C), lambda i: (i, 0))
    sh = jax.ShapeDtypeStruct((D, C), F32)
    return pl.pallas_call(
        body, out_shape=(sh, sh, sh, sh), grid=(D // tr,),
        in_specs=[pl.BlockSpec((N_DEV, tr), lambda i: (0, i)), pl.BlockSpec((N_DEV, C), lambda i: (0, 0)), blk, blk, blk],
        out_specs=(blk,) * 4, name='ada_update', compiler_params=_params(('parallel',), 48))(c_all, dmod_cols, w, m, v)


def _to_sub(t, d):
    if d == 1:
        return t
    return t.reshape(L // d, d, t.shape[-1]).transpose(1, 0, 2).reshape(L, t.shape[-1])


def _from_sub(t, d):
    if d == 1:
        return t
    return t.reshape(d, L // d, t.shape[-1]).transpose(1, 0, 2).reshape(L, t.shape[-1])


def _rows_to_cluster_lanes(t):
    k = t.shape[1]
    return t.reshape(N_CL, CL_S, k).transpose(0, 2, 1)


def _blockdiag_in(t):
    t = t.reshape(N_CL, CL_G, SSM_N, SSM_P).transpose(0, 1, 3, 2)
    eye = jnp.eye(CL_G, dtype=t.dtype)
    t = t[:, :, :, None, :] * eye[None, :, None, :, None]
    return t.reshape(N_CL, CL_U, CL_S)


def _blockdiag_extract(t):
    t = t.reshape(N_CL, CL_G, SSM_P, CL_G, SSM_N)
    t = jnp.stack([t[:, i, :, i, :] for i in range(CL_G)], axis=1)
    return t.transpose(0, 1, 3, 2).reshape(SSM_GN, SSM_P)


def _c_to_rows(t):
    return t.transpose(0, 2, 1).reshape(SSM_GN, SSM_P)


def _rows_to_c(t):
    return t.reshape(SSM_G, SSM_N, SSM_P).transpose(0, 2, 1)


def _ssm_prep(sp):
    rows = lambda n: sp[n].reshape(SSM_GN, 1)
    a_re, a_im = rows('ssm_a_re'), rows('ssm_a_im')
    ldt = jnp.repeat(sp['ssm_log_dt'].reshape(SSM_G, 1), SSM_N, axis=0)
    b_re, b_im = sp['ssm_b_re'].reshape(SSM_GN, SSM_P), sp['ssm_b_im'].reshape(SSM_GN, SSM_P)
    c_re, c_im = _c_to_rows(sp['ssm_c_re'].reshape(SSM_G, SSM_P, SSM_N)), _c_to_rows(sp['ssm_c_im'].reshape(SSM_G, SSM_P, SSM_N))
    bbr, bbi, pwr, pwi = _ssm_pre(a_re, a_im, ldt, b_re, b_im)
    bm = jnp.concatenate([_blockdiag_in(bbr), _blockdiag_in(bbi)], axis=2)
    cmt = jnp.concatenate([_blockdiag_in(c_re), -_blockdiag_in(c_im)], axis=2)
    bmt, cm = bm.transpose(0, 2, 1), cmt.transpose(0, 2, 1)
    pw = jnp.concatenate([_rows_to_cluster_lanes(pwr), _rows_to_cluster_lanes(pwi)], axis=2)
    bm2, cm2 = _hilo(bm), _hilo(cm)
    bmt_b, cmt_b = bmt.astype(BF16), cmt.astype(BF16)
    return a_re, a_im, ldt, b_re, b_im, bm2, cm2, bmt_b, cmt_b, pw


def _tied(v, deps):
    for t in deps:
        v = v + t[0, 0]
    return v


def _local_step(x, pos, mod, tgt, sp, prep, get_w, emit, emit_small, emit_late, first_deps=()):
    sh1, sc1, gt1, sh2, sc2, gt2 = (mod[i:i + 1] for i in range(NMOD))
    vec = lambda n: sp[n].reshape(1, -1)
    a_re, a_im, ldt, b_re, b_im, bm2, cm2, bmt_b, cmt_b, pw = prep
    dvec = vec('ssm_d')

    h1 = _prenorm_fwd(x, vec('g_pre_mix'), sc1, sh1)
    w_in = get_w('w_in', (h1, bm2, cm2, pw, bmt_b, cmt_b))
    proj = _mm(h1, w_in, mode='nn', name='mm_in', tn=1408, deps=first_deps)
    fr1 =ROPE_THETA ** (-jnp.arange(0, ROT_DIM, 2, dtype=F32) / ROT_DIM)
    lane = jnp.arange(128) % HEAD_DIM
    fr = jnp.where(lane < ROT_DIM, fr1[lane % (ROT_DIM // 2)], 0.0).reshape(1, 128).astype(F32)
    u_off = (ROPE_W + KVW) // CL_U
    qkvs = _rope_fwd(proj, pos, fr)
    fwd = [_attn_fwd(qkvs[g], d) for g, d in enumerate(DILATIONS)]
    os_, lses = [t[0] for t in fwd], [t[1] for t in fwd]
    att = _combine_fwd(os_, lses, vec('g_attn_out'))

    ypre, bnd, states = _ssm_fwd(proj, bm2, cm2, pw, dvec, u_off)
    w_glu = get_w('w_glu', ypre)
    ssm_n = _glu_fwd(ypre, w_glu, vec('b_glu'), vec('g_ssm_out'))

    cat = jnp.concatenate([att, ssm_n], axis=1)
    w_out = get_w('w_out', cat)
    mix = _mm(cat, w_out, mode='nn', name='mm_out', tk=1280)
    x1, h2 = _postmix_fwd(x, mix, vec('g_post_mix'), gt1, vec('g_pre_mlp'), sc2, sh2)
    w_mi = get_w('w_mlp_in', h2)
    a_pre, r_act = _mm(h2, w_mi, mode='nn', name='mm_mlp_in', epilogue='relu2', b_sharded=True)
    w_mo = get_w('w_mlp_out', a_pre)
    y = _mm(r_act, w_mo, mode='nn', name='mm_mlp_out')
    dx2, dy, loss, dgt2, dg_post_mlp = _final_fwd_bwd(x1, y, tgt, vec('g_post_mlp'), gt2)
    dgt2, dg_post_mlp = dgt2[:1], dg_post_mlp[:1]

    da = _mm(dy, w_mo, mode='nt', name='mm_d_act', out_dtype=BF16, epilogue='drelu2', extra=a_pre)
    dep = emit('w_mlp_out', _mm(r_act, dy, mode='tn', name='mm_dw_mlp_out', out_dtype=BF16))
    dh2 = _mm(da, w_mi, mode='nt', name='mm_dh2', tn=2048, tk=1024, b_sharded=True, deps=dep)
    dep = emit('w_mlp_in', _mm(h2, da, mode='tn', name='mm_dw_mlp_in', out_dtype=BF16, out_sharded=True))
    dx1, dmix, dsc2, dsh2, dg_pre_mlp, dgt1, dg_post_mix = _postmix_bwd(
        dx2, dh2, x1, mix, vec('g_post_mix'), gt1, vec('g_pre_mlp'), sc2)
    dsc2, dsh2, dg_pre_mlp, dgt1, dg_post_mix = (t[:1] for t in (dsc2, dsh2, dg_pre_mlp, dgt1, dg_post_mix))
    dcat = _mm(dmix, w_out, mode='nt', name='mm_dcat', tn=1280, deps=dep)
    dep = emit('w_out', _mm(cat, dmix, mode='tn', name='mm_dw_out', out_dtype=BF16, tm=640))
    dypre, g_w_glu, g_b_glu, g_g_ssm = _glu_bwd(ypre, dcat, w_glu, _tied(vec('b_glu'), dep), vec('g_ssm_out'))
    g_b_glu, g_g_ssm = g_b_glu[:1], g_g_ssm[:1]
    dep = dep + emit('w_glu', g_w_glu.astype(BF16))
    du, dbm, dcm, dA, dD = _ssm_bwd(proj, dypre, states, bmt_b, cmt_b, pw, dvec, bnd, u_off)
    dD = dD[:1]
    gbr, gbi = _blockdiag_extract(dbm[:, :, :CL_S]), _blockdiag_extract(dbm[:, :, CL_S:])
    dcmt = dcm.transpose(0, 2, 1)
    g_c_re = _rows_to_c(_blockdiag_extract(dcmt[:, :, :CL_S]))
    g_c_im = _rows_to_c(-_blockdiag_extract(dcmt[:, :, CL_S:]))
    gar = dA[:, 0, :CL_S].reshape(SSM_GN, 1)
    gai = dA[:, 0, CL_S:].reshape(SSM_GN, 1)
    sel = (jnp.arange(SSM_GN)[None, :] // SSM_N == jnp.arange(SSM_G)[:, None]).astype(F32)
    g_a_re, g_a_im, g_b_re, g_b_im, g_ldt = _ssm_post(a_re, a_im, ldt, b_re, b_im, gar, gai, gbr, gbi, sel)

    head_ones = (jnp.arange(KVW)[:, None] // HEAD_DIM == jnp.arange(KVW)[None, :] // HEAD_DIM).astype(F32)
    dos, dlses, g_g_attn = _combine_bwd(dcat, os_, lses, vec('g_attn_out'), head_ones)
    g_g_attn = g_g_attn[:1]
    dep_small = emit_small({
        'g_post_mix': dg_post_mix, 'ssm_a_re': g_a_re, 'ssm_a_im': g_a_im, 'ssm_log_dt': g_ldt[:, 0],
        'ssm_b_re': g_b_re, 'ssm_b_im': g_b_im, 'ssm_c_re': g_c_re, 'ssm_c_im': g_c_im, 'ssm_d': dD, 'b_glu': g_b_glu,
        'g_attn_out': g_g_attn, 'g_ssm_out': g_g_ssm, 'g_pre_mlp': dg_pre_mlp, 'g_post_mlp': dg_post_mlp})
    dqkv = [_attn_bwd(qkvs[g], os_[g], lses[g], dos[g], dlses[g], d) for g, d in enumerate(DILATIONS)]
    dproj = _rope_bwd(dqkv, du, pos, _tied(fr, dep_small))
    dh1 = _mm(dproj, w_in, mode='nt', name='mm_dh1', tk=1408, deps=dep)
    grad_x, dsc1, dsh1, dg_pre_mix = _prenorm_bwd(dx1, dh1, x, vec('g_pre_mix'), sc1)
    dsc1, dsh1, dg_pre_mix = dsc1[:1], dsh1[:1], dg_pre_mix[:1]
    dmod = jnp.concatenate([dsh1, dsc1, dgt1, dsh2, dsc2, dgt2], axis=0)
    dep = emit_late({'b_ada': dmod, 'g_pre_mix': dg_pre_mix})
    emit('w_in', _mm(h1, dproj, mode='tn', name='mm_dw_in', out_dtype=BF16, tn=1408, deps=dep))
    return loss[0, 0], grad_x


def _pack(d, names):
    flat = jnp.concatenate([jnp.pad(d[n].reshape(-1).astype(F32), (0, SEG[n] - SMALL_SIZES[n])) for n in names])
    return flat.reshape(-1, 128)


def _unpack(packed, names, shapes):
    out, off = {}, 0
    for n in names:
        out[n] = packed[off // 128:(off + SEG[n]) // 128].reshape(-1)[:SMALL_SIZES[n]].reshape(shapes[n])
        off += SEG[n]
    return out


def _shard_major(t, name):
    if name in ('w_in', 'w_out', 'w_mlp_in'):
        k, n = t.shape
        return t.reshape(k, N_DEV, n // N_DEV).transpose(1, 0, 2)
    k, n = t.shape
    return t.reshape(N_DEV, k // N_DEV, n)


def _from_shard_major(t, name):
    if name in ('w_in', 'w_out', 'w_mlp_in'):
        _, k, n = t.shape
        return t.transpose(1, 0, 2).reshape(k, N_DEV * n)
    _, k, n = t.shape
    return t.reshape(N_DEV * k, n)


def kernel(x, c, positions, w_ada, b_ada, g_pre_mix, g_post_mix, w_in, ssm_a_re, ssm_a_im, ssm_log_dt, ssm_b_re, ssm_b_im, ssm_c_re, ssm_c_im, ssm_d, w_glu, b_glu, g_attn_out, g_ssm_out, w_out, g_pre_mlp, g_post_mlp, w_mlp_in, w_mlp_out, loss_target, m_w_ada, m_b_ada, m_g_pre_mix, m_g_post_mix, m_w_in, m_ssm_a_re, m_ssm_a_im, m_ssm_log_dt, m_ssm_b_re, m_ssm_b_im, m_ssm_c_re, m_ssm_c_im, m_ssm_d, m_w_glu, m_b_glu, m_g_attn_out, m_g_ssm_out, m_w_out, m_g_pre_mlp, m_g_post_mlp, m_w_mlp_in, m_w_mlp_out, v_w_ada, v_b_ada, v_g_pre_mix, v_g_post_mix, v_w_in, v_ssm_a_re, v_ssm_a_im, v_ssm_log_dt, v_ssm_b_re, v_ssm_b_im, v_ssm_c_re, v_ssm_c_im, v_ssm_d, v_w_glu, v_b_glu, v_g_attn_out, v_g_ssm_out, v_w_out, v_g_pre_mlp, v_g_post_mlp, v_w_mlp_in, v_w_mlp_out):
    loc = dict(locals())
    W = {n: loc[n] for n in WEIGHTS}
    M = {n: loc['m_' + n] for n in WEIGHTS}
    V = {n: loc['v_' + n] for n in WEIGHTS}
    assert x.shape == (1, L, D) and w_in.shape == (1, D, INW // N_DEV), (x.shape, w_in.shape)

    cw = NMOD * D // N_DEV
    c_all, mod8 = _mod_exchange(c.reshape(1, 1, D), w_ada[0], b_ada.reshape(N_DEV, 1, cw))
    mod = mod8.reshape(NMOD, D)

    gather, after = {}, mod8
    for n in BIG:
        gather[n] = _push_start(W[n][0].astype(BF16), False, after, 'gather_start_' + n)
        after = gather[n][4]
    tokens = tuple(gather[n][4] for n in BIG)
    mod = _tied(mod, tokens)
    sp = {n: W[n][0] for n in SMALL}
    prep = _ssm_prep({**sp, 'ssm_a_re': _tied(sp['ssm_a_re'], tokens)})

    def get_w(n, after):
        g = _push_wait(gather[n], after, 'gather_wait_' + n)
        return g if n == 'w_mlp_in' else _from_shard_major(g, n)

    scatter = {}

    def emit(n, g):
        src = g if n == 'w_mlp_in' else _shard_major(g, n)
        scatter[n] = _push_start(src, True, src, 'scatter_start_' + n)
        return (scatter[n][4],)

    small_early = []

    def emit_small(d):
        pack = _pack(d, SMALL_EARLY)
        small_early.append(_push_start(pack, False, pack, 'small_start'))
        return (small_early[0][4],)

    out_g, out_d, out_m, out_v = {}, {}, {}, {}
    shapes = {n: W[n].shape[1:] for n in SMALL}

    def put(names, packs):
        for dst, packed in zip((out_g, out_d, out_m, out_v), packs):
            dst.update(_unpack(packed, names, shapes))

    late = []

    def emit_late(d):
        rows_all, *packs = _small_update(*[_pack(t, SMALL_LATE) for t in (d, W, M, V)])
        put(SMALL_LATE, packs)
        late.append(rows_all)
        return (rows_all,)

    loss, grad_x = _local_step(x[0], positions.reshape(L, 1), mod, loss_target[0], sp, prep, get_w, emit, emit_small,
                               emit_late)
    loss = lax.psum(loss, ('x', 'y', 'c'))

    me = 4 * lax.axis_index('x') + 2 * lax.axis_index('y') + lax.axis_index('c')
    dmod_all = late[0][:, :NMOD * D // 128].reshape(N_DEV, NMOD * D)
    dmod_cols = _tied(lax.dynamic_slice_in_dim(dmod_all, me * cw, cw, axis=1), (scatter['w_in'][4],))
    out_g['w_ada'], out_d['w_ada'], out_m['w_ada'], out_v['w_ada'] = _ada_update(
        c_all.reshape(N_DEV, D), dmod_cols, w_ada[0], m_w_ada[0], v_w_ada[0])

    parts = _push_wait(small_early[0], out_v['w_ada'], 'small_wait')
    packs = _small_sum_update(parts, *[_pack(d, SMALL_EARLY) for d in (W, M, V)])
    put(SMALL_EARLY, packs)

    after = packs[3]
    for n in ('w_mlp_out', 'w_mlp_in', 'w_out', 'w_glu', 'w_in'):
        parts = _push_wait(scatter[n], after, 'scatter_wait_' + n)
        out_g[n], out_d[n], out_m[n], out_v[n] = _big_update(parts, W[n][0], M[n][0], V[n][0], 'update_' + n)
        after = out_v[n]

    lead = lambda t: t[None]
    return (loss, grad_x[None], *[lead(out_g[n]) for n in WEIGHTS], *[lead(out_d[n]) for n in WEIGHTS],
            *[lead(out_m[n]) for n in WEIGHTS], *[lead(out_v[n]) for n in WEIGHTS])
```

```python
import functools
import math

import jax
import jax.numpy as jnp
from jax import lax
from jax.experimental import pallas as pl
from jax.experimental.pallas import tpu as pltpu

F32 = jnp.float32
BF16 = jnp.bfloat16
HI = lax.Precision.HIGHEST
MESH = pl.DeviceIdType.MESH

N_DEV = 8
L = 4096
D = 2048
HEAD_DIM = 64
N_GROUPS = 3
DILATIONS = (1, 4, 16)
HEADS = 6
QW = N_GROUPS * HEADS * HEAD_DIM
KVW = HEADS * HEAD_DIM
ROT_DIM = 16
ROPE_THETA = 500000.0
BLK = 128
NBLK = L // BLK
SSMW = D - QW
SSM_P = 16
SSM_G = SSMW // SSM_P
SSM_N = 64
SSM_GN = SSM_G * SSM_N
CL_G = 8
N_CL = SSM_G // CL_G
CL_U = CL_G * SSM_P
CL_S = CL_G * SSM_N
INW = QW + 2 * KVW + SSMW
OUTW = KVW + SSMW
DFF = 4 * D
NMOD = 6
EPS = 1e-6
LR, B1, B2, AEPS, WD, STEP = 0.001, 0.9, 0.999, 1e-08, 0.01, 10

T_SCAN = 1024
MB = 2 ** 20

WEIGHTS = ['w_ada', 'b_ada', 'g_pre_mix', 'g_post_mix', 'w_in', 'ssm_a_re', 'ssm_a_im', 'ssm_log_dt',
           'ssm_b_re', 'ssm_b_im', 'ssm_c_re', 'ssm_c_im', 'ssm_d', 'w_glu', 'b_glu', 'g_attn_out',
           'g_ssm_out', 'w_out', 'g_pre_mlp', 'g_post_mlp', 'w_mlp_in', 'w_mlp_out']
BIG = ['w_in', 'w_glu', 'w_out', 'w_mlp_in', 'w_mlp_out']
SMALL = [n for n in WEIGHTS if n not in BIG and n != 'w_ada']
SMALL_SIZES = {'b_ada': NMOD * D, 'g_pre_mix': D, 'g_post_mix': D, 'ssm_a_re': SSM_GN, 'ssm_a_im': SSM_GN,
               'ssm_log_dt': SSM_G, 'ssm_b_re': SSM_GN * SSM_P, 'ssm_b_im': SSM_GN * SSM_P,
               'ssm_c_re': SSM_GN * SSM_P, 'ssm_c_im': SSM_GN * SSM_P, 'ssm_d': SSMW, 'b_glu': SSMW,
               'g_attn_out': KVW, 'g_ssm_out': SSMW, 'g_pre_mlp': D, 'g_post_mlp': D}
SEG = {n: -(-SMALL_SIZES[n] // 1024) * 1024 for n in SMALL}
SMALL_LATE = ['b_ada', 'g_pre_mix']
SMALL_EARLY = [n for n in SMALL if n not in SMALL_LATE]


def _params(sem=None, vmem_mb=None):
    kw = {}
    if sem is not None:
        kw['dimension_semantics'] = sem
    if vmem_mb is not None:
        kw['vmem_limit_bytes'] = vmem_mb * MB
    return pltpu.CompilerParams(**kw)


def _vec(n):
    return pl.BlockSpec((1, n), lambda *_: (0, 0))


def _rms(x):
    return lax.rsqrt(jnp.mean(x * x, axis=-1, keepdims=True) + EPS)


def _rms_bwd(dn, n, r):
    return r * (dn - n * jnp.mean(dn * n, axis=-1, keepdims=True))


def _vec8(n):
    return pl.BlockSpec((8, n), lambda *_: (0, 0))


def _colsum(x):
    return jnp.sum(x.reshape(-1, 8, x.shape[-1]), axis=0)


def _fold8(last, *refs):
    @pl.when(last)
    def _():
        for r in refs:
            r[...] = jnp.broadcast_to(jnp.sum(r[...], axis=0, keepdims=True), r.shape)


def _mm(a, b, *, mode, name, out_dtype=F32, tm=1024, tn=1024, tk=2048, epilogue=None, extra=None,
        b_sharded=False, out_sharded=False, deps=()):
    if mode == 'nn':
        M, K = a.shape
        dims = (((1,), (0,)), ((), ()))
        a_spec = pl.BlockSpec((tm, tk), lambda i, j, k: (i, k))
        if b_sharded:
            _, K2, per = b.shape
            N, q = N_DEV * per, per // tn
            b_spec = pl.BlockSpec((None, tk, tn), lambda i, j, k: (j // q, k, j % q))
        else:
            K2, N = b.shape
            b_spec = pl.BlockSpec((tk, tn), lambda i, j, k: (k, j))
    elif mode == 'nt':
        M, K = a.shape
        dims = (((1,), (1,)), ((), ()))
        a_spec = pl.BlockSpec((tm, tk), lambda i, j, k: (i, k))
        if b_sharded:
            _, N, per = b.shape
            K2, q = N_DEV * per, per // tk
            b_spec = pl.BlockSpec((None, tn, tk), lambda i, j, k: (k // q, j, k % q))
        else:
            N, K2 = b.shape
            b_spec = pl.BlockSpec((tn, tk), lambda i, j, k: (j, k))
    else:
        (K, M), (K2, N) = a.shape, b.shape
        dims = (((0,), (0,)), ((), ()))
        a_spec = pl.BlockSpec((tk, tm), lambda i, j, k: (k, i))
        b_spec = pl.BlockSpec((tk, tn), lambda i, j, k: (k, j))
    assert K == K2 and M % tm == 0 and N % tn == 0 and K % tk == 0, (name, a.shape, b.shape, tm, tn, tk)
    nk = K // tk
    o_spec = pl.BlockSpec((tm, tn), lambda i, j, k: (i, j))
    o_dims = (M, N)
    if out_sharded:
        qo = N // N_DEV // tn
        o_spec = pl.BlockSpec((None, tm, tn), lambda i, j, k: (j // qo, i, j % qo))
        o_dims = (N_DEV, M, N // N_DEV)
    n_out = 2 if epilogue == 'relu2' else 1
    n_extra = 1 if extra is not None else 0
    n_in = 2 + n_extra + len(deps)

    def body(*refs):
        a_ref, b_ref = refs[0], refs[1]
        x_refs = refs[2:2 + n_extra]
        o_refs = refs[n_in:n_in + n_out]
        acc = refs[-1]
        k = pl.program_id(2)

        prod = lax.dot_general(a_ref[...], b_ref[...], dims, preferred_element_type=F32)

        def finish(r):
            if epilogue == 'relu2':
                o_refs[0][...] = r.astype(BF16)
                o_refs[1][...] = jnp.square(jnp.maximum(r, 0.0)).astype(BF16)
            elif epilogue == 'drelu2':
                pre = x_refs[0][...].astype(F32)
                o_refs[0][...] = (r * (2.0 * jnp.maximum(pre, 0.0))).astype(out_dtype)
            else:
                o_refs[0][...] = r.astype(out_dtype)

        if nk == 1:
            finish(prod)
        else:
            @pl.when(k == 0)
            def _():
                acc[...] = prod

            @pl.when((k > 0) & (k < nk - 1))
            def _():
                acc[...] += prod

            @pl.when(k == nk - 1)
            def _():
                finish(acc[...] + prod)

    if epilogue == 'relu2':
        out_shape = (jax.ShapeDtypeStruct((M, N), BF16), jax.ShapeDtypeStruct((M, N), BF16))
        out_specs = (o_spec, o_spec)
    else:
        out_shape = jax.ShapeDtypeStruct(o_dims, out_dtype)
        out_specs = o_spec
    args = (a, b) + ((extra,) if extra is not None else ()) + tuple(deps)
    in_specs = ([a_spec, b_spec] + ([o_spec] if extra is not None else [])
                + [pl.BlockSpec(memory_space=pl.ANY)] * len(deps))
    return pl.pallas_call(
        body, out_shape=out_shape, grid=(M // tm, N // tn, nk), in_specs=in_specs, out_specs=out_specs,
        scratch_shapes=[pltpu.VMEM((tm, tn) if nk > 1 else (8, 128), F32)], name=name,
        compiler_params=_params(('parallel', 'parallel', 'arbitrary'), 56))(*args)


TR = 256


def _rowspec(w=D):
    return pl.BlockSpec((TR, w), lambda i: (i, 0))


def _prenorm_fwd(x, g, sc, sh):
    def body(x_ref, g_ref, sc_ref, sh_ref, h_ref):
        xv = x_ref[...]
        n = xv * _rms(xv)
        h_ref[...] = ((n * g_ref[...]) * (1.0 + sc_ref[...]) + sh_ref[...]).astype(BF16)

    return pl.pallas_call(
        body, out_shape=jax.ShapeDtypeStruct((L, D), BF16), grid=(L // TR,),
        in_specs=[_rowspec(), _vec(D), _vec(D), _vec(D)], out_specs=_rowspec(), name='prenorm_fwd',
        compiler_params=_params(('parallel',), 40))(x, g, sc, sh)


def _postmix_fwd(x, mix, gpm, gt1, gpl, sc2, sh2):
    def body(x_ref, mix_ref, gpm_ref, gt1_ref, gpl_ref, sc2_ref, sh2_ref, x1_ref, h2_ref):
        mix_v = mix_ref[...]
        nm = mix_v * _rms(mix_v)
        x1 = x_ref[...] + gt1_ref[...] * (nm * gpm_ref[...])
        x1_ref[...] = x1
        n2 = x1 * _rms(x1)
        h2_ref[...] = ((n2 * gpl_ref[...]) * (1.0 + sc2_ref[...]) + sh2_ref[...]).astype(BF16)

    return pl.pallas_call(
        body, out_shape=(jax.ShapeDtypeStruct((L, D), F32), jax.ShapeDtypeStruct((L, D), BF16)), grid=(L // TR,),
        in_specs=[_rowspec(), _rowspec()] + [_vec(D)] * 5, out_specs=(_rowspec(), _rowspec()), name='postmix_fwd',
        compiler_params=_params(('parallel',), 40))(x, mix, gpm, gt1, gpl, sc2, sh2)


def _final_fwd_bwd(x1, y, tgt, g, gt2):
    def body(x1_ref, y_ref, t_ref, g_ref, gt2_ref, dx2_ref, dy_ref, loss_ref, dgt2_ref, dg_ref):
        @pl.when(pl.program_id(0) == 0)
        def _():
            loss_ref[...] = jnp.zeros_like(loss_ref)
            dgt2_ref[...] = jnp.zeros_like(dgt2_ref)
            dg_ref[...] = jnp.zeros_like(dg_ref)

        yv = y_ref[...]
        r = _rms(yv)
        n = yv * r
        ng = n * g_ref[...]
        x2 = x1_ref[...] + gt2_ref[...] * ng
        e = x2 - t_ref[...]
        loss_ref[...] += 0.5 * jnp.sum(jnp.mean(e * e, axis=-1, keepdims=True), axis=0, keepdims=True)
        dx2 = e * (1.0 / D)
        dx2_ref[...] = dx2
        dgt2_ref[...] += _colsum(dx2 * ng)
        dng = dx2 * gt2_ref[...]
        dg_ref[...] += _colsum(dng * n)
        dy_ref[...] = _rms_bwd(dng * g_ref[...], n, r).astype(BF16)
        _fold8(pl.program_id(0) == L // TR - 1, dgt2_ref, dg_ref)

    return pl.pallas_call(
        body,
        out_shape=(jax.ShapeDtypeStruct((L, D), F32), jax.ShapeDtypeStruct((L, D), BF16),
                   jax.ShapeDtypeStruct((8, 128), F32), jax.ShapeDtypeStruct((8, D), F32),
                   jax.ShapeDtypeStruct((8, D), F32)),
        grid=(L // TR,), in_specs=[_rowspec(), _rowspec(), _rowspec(), _vec(D), _vec(D)],
        out_specs=(_rowspec(), _rowspec(), _vec8(128), _vec8(D), _vec8(D)), name='final_fwd_bwd',
        compiler_params=_params(('arbitrary',), 40))(x1, y, tgt, g, gt2)


def _postmix_bwd(dx2, dh2, x1, mix, gpm, gt1, gpl, sc2):
    def body(dx2_ref, dh2_ref, x1_ref, mix_ref, gpm_ref, gt1_ref, gpl_ref, sc2_ref,
             dx1_ref, dmix_ref, dsc2_ref, dsh2_ref, dgpl_ref, dgt1_ref, dgpm_ref):
        @pl.when(pl.program_id(0) == 0)
        def _():
            for r_ in (dsc2_ref, dsh2_ref, dgpl_ref, dgt1_ref, dgpm_ref):
                r_[...] = jnp.zeros_like(r_)

        x1v = x1_ref[...]
        r2 = _rms(x1v)
        n2 = x1v * r2
        dh2v = dh2_ref[...]
        dsh2_ref[...] += _colsum(dh2v)
        dsc2_ref[...] += _colsum(dh2v * (n2 * gpl_ref[...]))
        t = dh2v * (1.0 + sc2_ref[...])
        dgpl_ref[...] += _colsum(t * n2)
        dx1 = dx2_ref[...] + _rms_bwd(t * gpl_ref[...], n2, r2)
        dx1_ref[...] = dx1
        mix_v = mix_ref[...]
        rm = _rms(mix_v)
        nm = mix_v * rm
        dgt1_ref[...] += _colsum(dx1 * (nm * gpm_ref[...]))
        u = dx1 * gt1_ref[...]
        dgpm_ref[...] += _colsum(u * nm)
        dmix_ref[...] = _rms_bwd(u * gpm_ref[...], nm, rm).astype(BF16)
        _fold8(pl.program_id(0) == L // TR - 1, dsc2_ref, dsh2_ref, dgpl_ref, dgt1_ref, dgpm_ref)

    vs = jax.ShapeDtypeStruct((8, D), F32)
    return pl.pallas_call(
        body, out_shape=(jax.ShapeDtypeStruct((L, D), F32), jax.ShapeDtypeStruct((L, D), BF16), vs, vs, vs, vs, vs),
        grid=(L // TR,), in_specs=[_rowspec()] * 4 + [_vec(D)] * 4,
        out_specs=(_rowspec(), _rowspec()) + (_vec8(D),) * 5, name='postmix_bwd',
        compiler_params=_params(('arbitrary',), 48))(dx2, dh2, x1, mix, gpm, gt1, gpl, sc2)


def _prenorm_bwd(dx1, dh1, x, g, sc1):
    def body(dx1_ref, dh1_ref, x_ref, g_ref, sc1_ref, dx_ref, dsc1_ref, dsh1_ref, dg_ref):
        @pl.when(pl.program_id(0) == 0)
        def _():
            for r_ in (dsc1_ref, dsh1_ref, dg_ref):
                r_[...] = jnp.zeros_like(r_)

        xv = x_ref[...]
        r = _rms(xv)
        n = xv * r
        dh = dh1_ref[...]
        dsh1_ref[...] += _colsum(dh)
        dsc1_ref[...] += _colsum(dh * (n * g_ref[...]))
        t = dh * (1.0 + sc1_ref[...])
        dg_ref[...] += _colsum(t * n)
        dx_ref[...] = dx1_ref[...] + _rms_bwd(t * g_ref[...], n, r)
        _fold8(pl.program_id(0) == L // TR - 1, dsc1_ref, dsh1_ref, dg_ref)

    vs = jax.ShapeDtypeStruct((8, D), F32)
    return pl.pallas_call(
        body, out_shape=(jax.ShapeDtypeStruct((L, D), F32), vs, vs, vs), grid=(L // TR,),
        in_specs=[_rowspec()] * 3 + [_vec(D)] * 2, out_specs=(_rowspec(),) + (_vec8(D),) * 3, name='prenorm_bwd',
        compiler_params=_params(('arbitrary',), 40))(dx1, dh1, x, g, sc1)


ROPE_W = QW + KVW
QKV_W = 3 * KVW
NB_KV = KVW // 128


def _rope_rotate(xv, pos, fr, sign):
    ang = pos.astype(F32) * fr
    w = lax.broadcasted_iota(jnp.int32, (1, 128), 1) % HEAD_DIM
    cs = jnp.cos(ang)
    sn = jnp.sin(ang) * sign
    s1 = jnp.where(w < ROT_DIM // 2, -sn, 0.0)
    s2 = jnp.where((w >= ROT_DIM // 2) & (w < ROT_DIM), sn, 0.0)
    width = xv.shape[1]
    rep = width // 128
    cs, s1, s2 = jnp.tile(cs, (1, rep)), jnp.tile(s1, (1, rep)), jnp.tile(s2, (1, rep))
    hi = pltpu.roll(xv, width - ROT_DIM // 2, 1)
    lo = pltpu.roll(xv, ROT_DIM // 2, 1)
    return xv * cs + hi * s1 + lo * s2


def _sub_spec(d, rows, width):
    return pl.BlockSpec((d, rows // d, width), lambda i: (0, i, 0))


def _gather_rows(scr, blocks, r, d, rows):
    return jnp.concatenate([scr.at[j][pl.ds(r, rows // d, stride=d), :] for j in blocks], axis=1)


def _scatter_rows(scr, src_ref, d, rows):
    for r in range(d):
        for j in range(NB_KV):
            scr.at[j][pl.ds(r, rows // d, stride=d), :] = src_ref[r, :, j * 128:(j + 1) * 128]


def _token_order(scr):
    return jnp.concatenate([scr[j] for j in range(NB_KV)], axis=1)


def _rope_fwd(proj, pos, fr):
    nb = (ROPE_W + KVW) // 128

    def body(x_ref, pos_ref, fr_ref, o0_ref, o1_ref, o2_ref, scr):
        y = _rope_rotate(x_ref[:, 0:ROPE_W], pos_ref[...], fr_ref[...], 1.0)
        for j in range(ROPE_W // 128):
            scr[j] = y[:, j * 128:(j + 1) * 128]
        for j in range(ROPE_W // 128, nb):
            scr[j] = x_ref[:, j * 128:(j + 1) * 128]
        kv = list(range(QW // 128, nb))
        for g, (d, o_ref) in enumerate(zip(DILATIONS, (o0_ref, o1_ref, o2_ref))):
            blocks = list(range(g * NB_KV, (g + 1) * NB_KV)) + kv
            for r in range(d):
                o_ref[r] = _gather_rows(scr, blocks, r, d, TR).astype(BF16)

    return pl.pallas_call(
        body, out_shape=tuple(jax.ShapeDtypeStruct((d, L // d, QKV_W), BF16) for d in DILATIONS), grid=(L // TR,),
        in_specs=[_rowspec(ROPE_W + KVW), pl.BlockSpec((TR, 1), lambda i: (i, 0)), _vec(128)],
        out_specs=tuple(_sub_spec(d, TR, QKV_W) for d in DILATIONS),
        scratch_shapes=[pltpu.VMEM((nb, TR, 128), F32)], name='rope_fwd',
        compiler_params=_params(('parallel',), 40))(proj, pos, fr)


def _rope_bwd(dqkv, du, pos, fr):
    def body(*refs):
        grads = [refs[3 * g:3 * g + 3] for g in range(N_GROUPS)]
        du_ref, pos_ref, fr_ref, o_ref = refs[9:13]
        scrs = refs[13:]
        dq, dk, dv = [], None, None
        for g, d in enumerate(DILATIONS):
            parts = []
            for t in range(3):
                if d == 1:
                    parts.append(grads[g][t][0])
                else:
                    scr = scrs[3 * (g - 1) + t]
                    _scatter_rows(scr, grads[g][t], d, TR)
                    parts.append(_token_order(scr))
            dq.append(parts[0])
            dk = parts[1] if dk is None else dk + parts[1]
            dv = parts[2] if dv is None else dv + parts[2]
        x = jnp.concatenate(dq + [dk], axis=1)
        o_ref[:, 0:ROPE_W] = _rope_rotate(x, pos_ref[...], fr_ref[...], -1.0).astype(BF16)
        o_ref[:, ROPE_W:ROPE_W + KVW] = dv.astype(BF16)
        o_ref[:, ROPE_W + KVW:INW] = du_ref[...].astype(BF16)

    flat = [a for grp in dqkv for a in grp]
    in_specs = [_sub_spec(d, TR, KVW) for d in DILATIONS for _ in range(3)]
    in_specs += [_rowspec(SSMW), pl.BlockSpec((TR, 1), lambda i: (i, 0)), _vec(128)]
    return pl.pallas_call(
        body, out_shape=jax.ShapeDtypeStruct((L, INW), BF16), grid=(L // TR,), in_specs=in_specs,
        out_specs=_rowspec(INW), scratch_shapes=[pltpu.VMEM((NB_KV, TR, 128), F32)] * 6, name='rope_bwd',
        compiler_params=_params(('parallel',), 48))(*flat, du, pos, fr)


def _attn_mask(nbs, b):
    first = (b & (nbs - 1)) == 0
    qi = lax.broadcasted_iota(jnp.int32, (BLK, 2 * BLK), 0)
    kj = lax.broadcasted_iota(jnp.int32, (BLK, 2 * BLK), 1)
    dist = qi + BLK - kj
    return (dist >= 0) & (dist <= BLK) & ((kj >= BLK) | jnp.logical_not(first))


def _qkv_specs():
    cur = lambda col: pl.BlockSpec((BLK, KVW), lambda b: (b, col))
    prev = lambda col: pl.BlockSpec((BLK, KVW), lambda b: (jnp.maximum(b - 1, 0), col))
    return [cur(0), prev(1), cur(1), prev(2), cur(2)]


_ROWS = pl.BlockSpec((BLK, KVW), lambda b: (b, 0))
NEG = -1e30
NT_DIMS = (((1,), (1,)), ((), ()))
TN_DIMS = (((0,), (0,)), ((), ()))


def _attn_fwd(qkv, d):
    nbs = L // d // BLK

    def body(q_ref, kp_ref, kc_ref, vp_ref, vc_ref, o_ref, lse_ref):
        valid = _attn_mask(nbs, pl.program_id(0))
        outs, lses = [], []
        for h in range(HEADS):
            hs = slice(h * HEAD_DIM, (h + 1) * HEAD_DIM)
            q = q_ref[:, hs]
            kc = jnp.concatenate([kp_ref[:, hs], kc_ref[:, hs]], axis=0)
            vc = jnp.concatenate([vp_ref[:, hs], vc_ref[:, hs]], axis=0)
            s = lax.dot_general(q, kc, NT_DIMS, preferred_element_type=F32) * 0.125
            s = jnp.where(valid, s, NEG)
            m = jnp.max(s, axis=-1, keepdims=True)
            p = jnp.exp(s - m)
            l = jnp.sum(p, axis=-1, keepdims=True)
            outs.append(jnp.dot(p.astype(BF16), vc, preferred_element_type=F32) / l)
            lses.append(jnp.broadcast_to(m + jnp.log(l), (BLK, HEAD_DIM)))
        o_ref[...] = jnp.concatenate(outs, axis=1)
        lse_ref[...] = jnp.concatenate(lses, axis=1)

    sh = jax.ShapeDtypeStruct((L, KVW), F32)
    q2 = qkv.reshape(L, QKV_W)
    o, lse = pl.pallas_call(
        body, out_shape=(sh, sh), grid=(NBLK,), in_specs=_qkv_specs(), out_specs=(_ROWS, _ROWS),
        name='attn_fwd_d%d' % d, compiler_params=_params(('parallel',), 32))(q2, q2, q2, q2, q2)
    return o.reshape(d, L // d, KVW), lse.reshape(d, L // d, KVW)


def _attn_bwd(qkv, o, lse, do, dlse, d):
    nbs = L // d // BLK

    def body(q_ref, kp_ref, kc_ref, vp_ref, vc_ref, o_ref, lse_ref, do_ref, dlse_ref, dq_ref, dk_ref, dv_ref):
        b = pl.program_id(0)

        @pl.when(b == 0)
        def _():
            dk_ref[...] = jnp.zeros_like(dk_ref)
            dv_ref[...] = jnp.zeros_like(dv_ref)

        valid = _attn_mask(nbs, b)
        prev0 = pl.multiple_of(jnp.maximum(b - 1, 0) * BLK, BLK)
        cur0 = pl.multiple_of(b * BLK, BLK)
        dqs, dks, dvs = [], [], []
        for h in range(HEADS):
            hs = slice(h * HEAD_DIM, (h + 1) * HEAD_DIM)
            q = q_ref[:, hs]
            kc = jnp.concatenate([kp_ref[:, hs], kc_ref[:, hs]], axis=0)
            vc = jnp.concatenate([vp_ref[:, hs], vc_ref[:, hs]], axis=0)
            s = lax.dot_general(q, kc, NT_DIMS, preferred_element_type=F32) * 0.125
            s = jnp.where(valid, s, NEG)
            p = jnp.exp(s - lse_ref[:, h * HEAD_DIM:h * HEAD_DIM + 1])
            do_h = do_ref[:, hs]
            delta = jnp.sum(do_h * o_ref[:, hs], axis=-1, keepdims=True)
            do_b = do_h.astype(BF16)
            dp = lax.dot_general(do_b, vc, NT_DIMS, preferred_element_type=F32)
            ds = p * (dp - delta + dlse_ref[:, h * HEAD_DIM:h * HEAD_DIM + 1])
            ds_b = (ds * 0.125).astype(BF16)
            dqs.append(jnp.dot(ds_b, kc, preferred_element_type=F32))
            dks.append(lax.dot_general(ds_b, q, TN_DIMS, preferred_element_type=F32))
            dvs.append(lax.dot_general(p.astype(BF16), do_b, TN_DIMS, preferred_element_type=F32))
        dq_ref[...] = jnp.concatenate(dqs, axis=1)
        dkc, dvc = jnp.concatenate(dks, axis=1), jnp.concatenate(dvs, axis=1)
        dk_ref[pl.ds(prev0, BLK), :] += dkc[:BLK]
        dv_ref[pl.ds(prev0, BLK), :] += dvc[:BLK]
        dk_ref[pl.ds(cur0, BLK), :] += dkc[BLK:]
        dv_ref[pl.ds(cur0, BLK), :] += dvc[BLK:]

    sh = jax.ShapeDtypeStruct((L, KVW), F32)
    whole = pl.BlockSpec((L, KVW), lambda b: (0, 0))
    q2 = qkv.reshape(L, QKV_W)
    flat = lambda t: t.reshape(L, KVW)
    outs = pl.pallas_call(
        body, out_shape=(sh, sh, sh), grid=(NBLK,), in_specs=_qkv_specs() + [_ROWS] * 4,
        out_specs=(_ROWS, whole, whole), name='attn_bwd_d%d' % d,
        compiler_params=_params(('arbitrary',), 48))(q2, q2, q2, q2, q2, flat(o), flat(lse), flat(do), flat(dlse))
    return tuple(t.reshape(d, L // d, KVW) for t in outs)


TC = 512


def _combine_weights(l0, l1, l2):
    m = jnp.maximum(jnp.maximum(l0, l1), l2)
    e0, e1, e2 = jnp.exp(l0 - m), jnp.exp(l1 - m), jnp.exp(l2 - m)
    z = e0 + e1 + e2
    return e0 / z, e1 / z, e2 / z


def _load_groups(refs, scrs):
    out = [refs[0][0]]
    for g in (1, 2):
        _scatter_rows(scrs[g - 1], refs[g], DILATIONS[g], TC)
        out.append(_token_order(scrs[g - 1]))
    return out


def _combine_fwd(os_, lses, g):
    def body(o0, o1, o2, l0, l1, l2, g_ref, att_ref, *scrs):
        ov = _load_groups((o0, o1, o2), scrs[0:2])
        lv = _load_groups((l0, l1, l2), scrs[2:4])
        w0, w1, w2 = _combine_weights(*lv)
        a = w0 * ov[0] + w1 * ov[1] + w2 * ov[2]
        att_ref[...] = ((a * _rms(a)) * g_ref[...]).astype(BF16)

    subs = [_sub_spec(d, TC, KVW) for d in DILATIONS]
    return pl.pallas_call(
        body, out_shape=jax.ShapeDtypeStruct((L, KVW), BF16), grid=(L // TC,), in_specs=subs + subs + [_vec(KVW)],
        out_specs=pl.BlockSpec((TC, KVW), lambda i: (i, 0)),
        scratch_shapes=[pltpu.VMEM((NB_KV, TC, 128), F32)] * 4, name='combine_fwd',
        compiler_params=_params(('parallel',), 40))(*os_, *lses, g)


def _combine_bwd(dcat, os_, lses, g, head_ones):
    def body(datt_ref, o0, o1, o2, l0, l1, l2, g_ref, e_ref, do0, do1, do2, dl0, dl1, dl2, dg_ref, *scrs):
        @pl.when(pl.program_id(0) == 0)
        def _():
            dg_ref[...] = jnp.zeros_like(dg_ref)

        ov = _load_groups((o0, o1, o2), scrs[0:2])
        lv = _load_groups((l0, l1, l2), scrs[2:4])
        ws = _combine_weights(*lv)
        a = ws[0] * ov[0] + ws[1] * ov[1] + ws[2] * ov[2]
        r = _rms(a)
        n = a * r
        dv = datt_ref[...]
        dg_ref[...] += _colsum(dv * n)
        da = _rms_bwd(dv * g_ref[...], n, r)
        dws = [jnp.dot(da * ov[i], e_ref[...], preferred_element_type=F32, precision=HI) for i in range(3)]
        dbar = ws[0] * dws[0] + ws[1] * dws[1] + ws[2] * dws[2]
        scr = scrs[4]
        for i, (d, do_ref, dl_ref) in enumerate(zip(DILATIONS, (do0, do1, do2), (dl0, dl1, dl2))):
            for val, out_ref in ((ws[i] * da, do_ref), (ws[i] * (dws[i] - dbar), dl_ref)):
                if d == 1:
                    out_ref[0] = val
                else:
                    for j in range(NB_KV):
                        scr[j] = val[:, j * 128:(j + 1) * 128]
                    for rr in range(d):
                        out_ref[rr] = _gather_rows(scr, range(NB_KV), rr, d, TC)
        _fold8(pl.program_id(0) == L // TC - 1, dg_ref)

    subs = [_sub_spec(d, TC, KVW) for d in DILATIONS]
    shs = tuple(jax.ShapeDtypeStruct((d, L // d, KVW), F32) for d in DILATIONS)
    outs = pl.pallas_call(
        body, out_shape=shs + shs + (jax.ShapeDtypeStruct((8, KVW), F32),), grid=(L // TC,),
        in_specs=[pl.BlockSpec((TC, KVW), lambda i: (i, 0))] + subs + subs + [_vec(KVW),
                                                                              pl.BlockSpec((KVW, KVW), lambda i: (0, 0))],
        out_specs=tuple(subs) + tuple(subs) + (_vec8(KVW),),
        scratch_shapes=[pltpu.VMEM((NB_KV, TC, 128), F32)] * 5, name='combine_bwd',
        compiler_params=_params(('arbitrary',), 48))(dcat, *os_, *lses, g, head_ones)
    return outs[0:3], outs[3:6], outs[6]


def _ssm_disc(ar, ai, ldt):
    dt = jnp.exp(ldt)
    zr, zi = ar * dt, ai * dt
    ez = jnp.exp(zr)
    A_r, A_i = ez * jnp.cos(zi), ez * jnp.sin(zi)
    den = ar * ar + ai * ai
    xr, xi = A_r - 1.0, A_i
    cr = (xr * ar + xi * ai) / den
    ci = (xi * ar - xr * ai) / den
    return dt, zr, zi, A_r, A_i, den, cr, ci


def _ssm_pre(ar, ai, ldt, br, bi):
    def body(ar_ref, ai_ref, ldt_ref, br_ref, bi_ref, bbr_ref, bbi_ref, pwr_ref, pwi_ref):
        _, zr, zi, _, _, _, cr, ci = _ssm_disc(ar_ref[...], ai_ref[...], ldt_ref[...])
        bbr_ref[...] = cr * br_ref[...] - ci * bi_ref[...]
        bbi_ref[...] = cr * bi_ref[...] + ci * br_ref[...]
        k = (lax.broadcasted_iota(jnp.int32, (1, 8), 1) + 1).astype(F32)
        ek = jnp.exp(zr * k)
        pwr_ref[...] = ek * jnp.cos(zi * k)
        pwi_ref[...] = ek * jnp.sin(zi * k)

    s16 = jax.ShapeDtypeStruct((SSM_GN, SSM_P), F32)
    s8 = jax.ShapeDtypeStruct((SSM_GN, 8), F32)
    return pl.pallas_call(body, out_shape=(s16, s16, s8, s8), name='ssm_pre',
                          compiler_params=_params(None, 40))(ar, ai, ldt, br, bi)


def _ssm_post(ar, ai, ldt, br, bi, gar, gai, gbr, gbi, sel):
    def body(ar_ref, ai_ref, ldt_ref, br_ref, bi_ref, gar_ref, gai_ref, gbr_ref, gbi_ref, sel_ref,
             dar_ref, dai_ref, dbr_ref, dbi_ref, dldt_ref):
        a_r, a_i = ar_ref[...], ai_ref[...]
        dt, _, _, A_r, A_i, den, cr, ci = _ssm_disc(a_r, a_i, ldt_ref[...])
        b_r, b_i, g_br, g_bi = br_ref[...], bi_ref[...], gbr_ref[...], gbi_ref[...]
        gcr = jnp.sum(g_br * b_r + g_bi * b_i, axis=-1, keepdims=True)
        gci = jnp.sum(g_bi * b_r - g_br * b_i, axis=-1, keepdims=True)
        dbr_ref[...] = g_br * cr + g_bi * ci
        dbi_ref[...] = g_bi * cr - g_br * ci
        g_ar = gar_ref[...] + (gcr * a_r - gci * a_i) / den
        g_ai = gai_ref[...] + (gcr * a_i + gci * a_r) / den
        qr = (cr * a_r + ci * a_i) / den
        qi = (ci * a_r - cr * a_i) / den
        glr = -(gcr * qr + gci * qi)
        gli = -(gci * qr - gcr * qi)
        gzr = g_ar * A_r + g_ai * A_i
        gzi = g_ai * A_r - g_ar * A_i
        dar_ref[...] = glr + gzr * dt
        dai_ref[...] = gli + gzi * dt
        gdt = (gzr * a_r + gzi * a_i) * dt
        dldt_ref[...] = jnp.dot(sel_ref[...], jnp.broadcast_to(gdt, (SSM_GN, 128)),
                                preferred_element_type=F32, precision=HI)

    s1 = jax.ShapeDtypeStruct((SSM_GN, 1), F32)
    s16 = jax.ShapeDtypeStruct((SSM_GN, SSM_P), F32)
    return pl.pallas_call(body, out_shape=(s1, s1, s16, s16, jax.ShapeDtypeStruct((SSM_G, 128), F32)),
                          name='ssm_post', compiler_params=_params(None, 48))(
                              ar, ai, ldt, br, bi, gar, gai, gbr, gbi, sel)


SCAN_CH = 8


def _scan_fwd_tiles(s_ref, pw, carry):
    pwr, pwi = pw[:, :CL_S], pw[:, CL_S:]
    row = lax.broadcasted_iota(jnp.int32, (8, CL_S), 0)
    steps = [(k, jnp.where(row >= k, pwr[k - 1:k], 0.0), jnp.where(row >= k, pwi[k - 1:k], 0.0)) for k in (1, 2, 4)]
    rows = 8 * SCAN_CH

    def chunk(i, c):
        cr, ci = c
        r0 = pl.multiple_of(i * rows, rows)
        xr = s_ref[pl.ds(r0, rows), 0:CL_S].reshape(SCAN_CH, 8, CL_S)
        xi = s_ref[pl.ds(r0, rows), CL_S:2 * CL_S].reshape(SCAN_CH, 8, CL_S)
        for k, pr, pi in steps:
            sr, si = pltpu.roll(xr, k, 1), pltpu.roll(xi, k, 1)
            xr, xi = xr + pr * sr - pi * si, xi + pr * si + pi * sr
        for j in range(SCAN_CH):
            tr = xr[j] + pwr * cr - pwi * ci
            ti = xi[j] + pwr * ci + pwi * cr
            s_ref[pl.ds(r0 + 8 * j, 8), 0:CL_S] = tr
            s_ref[pl.ds(r0 + 8 * j, 8), CL_S:2 * CL_S] = ti
            cr, ci = tr[7:8], ti[7:8]
        return cr, ci

    return lax.fori_loop(0, T_SCAN // rows, chunk, (carry[:, :CL_S], carry[:, CL_S:]))


def _scan_bwd_tiles(l_ref, pw, carry):
    pwr, pwi = pw[:, :CL_S], pw[:, CL_S:]
    rpr = jnp.concatenate([pwr[7 - r:8 - r] for r in range(8)], axis=0)
    rpi = jnp.concatenate([pwi[7 - r:8 - r] for r in range(8)], axis=0)
    row = lax.broadcasted_iota(jnp.int32, (8, CL_S), 0)
    steps = [(k, jnp.where(row < 8 - k, pwr[k - 1:k], 0.0), jnp.where(row < 8 - k, pwi[k - 1:k], 0.0))
             for k in (1, 2, 4)]
    rows = 8 * SCAN_CH
    nc = T_SCAN // rows

    def chunk(i, c):
        cr, ci = c
        r0 = pl.multiple_of((nc - 1 - i) * rows, rows)
        xr = l_ref[pl.ds(r0, rows), 0:CL_S].reshape(SCAN_CH, 8, CL_S)
        xi = l_ref[pl.ds(r0, rows), CL_S:2 * CL_S].reshape(SCAN_CH, 8, CL_S)
        for k, pr, pi in steps:
            sr, si = pltpu.roll(xr, 8 - k, 1), pltpu.roll(xi, 8 - k, 1)
            xr, xi = xr + pr * sr + pi * si, xi + pr * si - pi * sr
        for j in reversed(range(SCAN_CH)):
            tr = xr[j] + rpr * cr + rpi * ci
            ti = xi[j] + rpr * ci - rpi * cr
            l_ref[pl.ds(r0 + 8 * j, 8), 0:CL_S] = tr
            l_ref[pl.ds(r0 + 8 * j, 8), CL_S:2 * CL_S] = ti
            cr, ci = tr[0:1], ti[0:1]
        return cr, ci

    return lax.fori_loop(0, nc, chunk, (carry[:, :CL_S], carry[:, CL_S:]))


NT_SCAN = L // T_SCAN


def _hilo(t):
    hi = t.astype(BF16)
    return jnp.stack([hi, (t - hi.astype(F32)).astype(BF16)], axis=1)


def _dot3(a, b_ref):
    ah = a.astype(BF16)
    al = (a - ah.astype(F32)).astype(BF16)
    bh, bl = b_ref[0], b_ref[1]
    return (jnp.dot(ah, bh, preferred_element_type=F32) + jnp.dot(al, bh, preferred_element_type=F32)
            + jnp.dot(ah, bl, preferred_element_type=F32))


def _hl_spec(r, c):
    return pl.BlockSpec((None, 2, r, c), lambda c_, t: (c_, 0, 0, 0))


def _ssm_fwd(u, bm2, cm2, pw, dvec, u_off=0):
    def body(u_ref, bm_ref, cm_ref, pw_ref, d_ref, y_ref, bnd_ref, s_ref, carry_ref):
        @pl.when(pl.program_id(1) == 0)
        def _():
            carry_ref[...] = jnp.zeros_like(carry_ref)

        bnd_ref[...] = carry_ref[...]
        uv = u_ref[...]
        s_ref[...] = jnp.dot(uv.astype(BF16), bm_ref[0], preferred_element_type=F32)
        cr, ci = _scan_fwd_tiles(s_ref, pw_ref[...], carry_ref[...])
        carry_ref[...] = jnp.concatenate([cr, ci], axis=1)
        y_ref[...] = jnp.dot(s_ref[...].astype(BF16), cm_ref[0], preferred_element_type=F32) + d_ref[...] * uv

    return pl.pallas_call(
        body,
        out_shape=(jax.ShapeDtypeStruct((L, SSMW), F32), jax.ShapeDtypeStruct((N_CL, NT_SCAN, 1, 2 * CL_S), F32),
                   jax.ShapeDtypeStruct((L, N_CL * 2 * CL_S), F32)),
        grid=(N_CL, NT_SCAN),
        in_specs=[pl.BlockSpec((T_SCAN, CL_U), lambda c, t: (t, c + u_off)),
                  _hl_spec(CL_U, 2 * CL_S), _hl_spec(2 * CL_S, CL_U),
                  pl.BlockSpec((None, 8, 2 * CL_S), lambda c, t: (c, 0, 0)),
                  pl.BlockSpec((1, CL_U), lambda c, t: (0, c))],
        out_specs=(pl.BlockSpec((T_SCAN, CL_U), lambda c, t: (t, c)),
                   pl.BlockSpec((None, None, 1, 2 * CL_S), lambda c, t: (c, t, 0, 0)),
                   pl.BlockSpec((T_SCAN, 2 * CL_S), lambda c, t: (t, c))),
        scratch_shapes=[pltpu.VMEM((1, 2 * CL_S), F32)],
        name='ssm_fwd', compiler_params=_params(('arbitrary', 'arbitrary'), 40))(u, bm2, cm2, pw, dvec)


def _ssm_bwd(u, dy, states, bmt, cmt, pw, dvec, bnd, u_off=0):
    rev = lambda t: NT_SCAN - 1 - t

    def body(u_ref, dy_ref, s_ref, bmt_ref, cmt_ref, pw_ref, d_ref, bnd_ref,
             du_ref, dbm_ref, dcm_ref, da_ref, dd_ref, l_ref, carry_ref):
        @pl.when(pl.program_id(1) == 0)
        def _():
            carry_ref[...] = jnp.zeros_like(carry_ref)
            dbm_ref[...] = jnp.zeros_like(dbm_ref)
            dcm_ref[...] = jnp.zeros_like(dcm_ref)
            da_ref[...] = jnp.zeros_like(da_ref)
            dd_ref[...] = jnp.zeros_like(dd_ref)

        uv, dyv, pw = u_ref[...], dy_ref[...], pw_ref[...]
        dy_b = dyv.astype(BF16)
        entry = bnd_ref[...]
        l_ref[...] = jnp.dot(dy_b, cmt_ref[...], preferred_element_type=F32)
        cr, ci = _scan_bwd_tiles(l_ref, pw, carry_ref[...])
        carry_ref[...] = jnp.concatenate([cr, ci], axis=1)
        sv, lv = s_ref[...], l_ref[...]
        lv_b = lv.astype(BF16)
        du_ref[...] = dyv * d_ref[...] + jnp.dot(lv_b, bmt_ref[...], preferred_element_type=F32)
        dbm_ref[...] += lax.dot_general(uv.astype(BF16), lv_b, TN_DIMS, preferred_element_type=F32)
        dcm_ref[...] += lax.dot_general(sv.astype(BF16), dy_b, TN_DIMS, preferred_element_type=F32)
        dd_ref[...] += _colsum(dyv * uv)
        row = lax.broadcasted_iota(jnp.int32, (T_SCAN, 2 * CL_S), 0)
        sp = jnp.where(row == 0, entry, pltpu.roll(sv, 1, 0))
        spr, spi = sp[:, :CL_S], sp[:, CL_S:]
        lr, li = lv[:, :CL_S], lv[:, CL_S:]
        da_ref[:, 0:CL_S] += _colsum(lr * spr + li * spi)
        da_ref[:, CL_S:2 * CL_S] += _colsum(li * spr - lr * spi)
        _fold8(pl.program_id(1) == NT_SCAN - 1, da_ref, dd_ref)

    return pl.pallas_call(
        body,
        out_shape=(jax.ShapeDtypeStruct((L, SSMW), F32), jax.ShapeDtypeStruct((N_CL, CL_U, 2 * CL_S), F32),
                   jax.ShapeDtypeStruct((N_CL, 2 * CL_S, CL_U), F32), jax.ShapeDtypeStruct((N_CL, 8, 2 * CL_S), F32),
                   jax.ShapeDtypeStruct((8, SSMW), F32)),
        grid=(N_CL, NT_SCAN),
        in_specs=[pl.BlockSpec((T_SCAN, CL_U), lambda c, t: (rev(t), c + u_off)),
                  pl.BlockSpec((T_SCAN, CL_U), lambda c, t: (rev(t), c)),
                  pl.BlockSpec((T_SCAN, 2 * CL_S), lambda c, t: (rev(t), c)),
                  pl.BlockSpec((None, 2 * CL_S, CL_U), lambda c, t: (c, 0, 0)),
                  pl.BlockSpec((None, CL_U, 2 * CL_S), lambda c, t: (c, 0, 0)),
                  pl.BlockSpec((None, 8, 2 * CL_S), lambda c, t: (c, 0, 0)),
                  pl.BlockSpec((1, CL_U), lambda c, t: (0, c)),
                  pl.BlockSpec((None, None, 1, 2 * CL_S), lambda c, t: (c, rev(t), 0, 0))],
        out_specs=(pl.BlockSpec((T_SCAN, CL_U), lambda c, t: (rev(t), c)),
                   pl.BlockSpec((None, CL_U, 2 * CL_S), lambda c, t: (c, 0, 0)),
                   pl.BlockSpec((None, 2 * CL_S, CL_U), lambda c, t: (c, 0, 0)),
                   pl.BlockSpec((None, 8, 2 * CL_S), lambda c, t: (c, 0, 0)),
                   pl.BlockSpec((8, CL_U), lambda c, t: (0, c))),
        scratch_shapes=[pltpu.VMEM((T_SCAN, 2 * CL_S), F32), pltpu.VMEM((1, 2 * CL_S), F32)],
        name='ssm_bwd', compiler_params=_params(('arbitrary', 'arbitrary'), 48))(u, dy, states, bmt, cmt, pw, dvec, bnd)


GELU_C = math.sqrt(2.0 / math.pi)
GELU_K = 0.044715


def _gelu_parts(x):
    t = jnp.tanh(GELU_C * (x + GELU_K * (x * x * x)))
    return x * (0.5 * (1.0 + t)), t


def _glu_fwd(ypre, wglu, bglu, gs):
    def body(y_ref, w_ref, b_ref, g_ref, o_ref):
        yg, _ = _gelu_parts(y_ref[...])
        z = jnp.dot(yg.astype(BF16), w_ref[...], preferred_element_type=F32) + b_ref[...]
        s = yg * jax.nn.sigmoid(z)
        o_ref[...] = ((s * _rms(s)) * g_ref[...]).astype(BF16)

    return pl.pallas_call(
        body, out_shape=jax.ShapeDtypeStruct((L, SSMW), BF16), grid=(L // TR,),
        in_specs=[_rowspec(SSMW), pl.BlockSpec((SSMW, SSMW), lambda i: (0, 0)), _vec(SSMW), _vec(SSMW)],
        out_specs=_rowspec(SSMW), name='glu_fwd', compiler_params=_params(('parallel',), 32))(ypre, wglu, bglu, gs)


def _glu_bwd(ypre, dsn, wglu, bglu, gs):
    def body(y_ref, d_ref, w_ref, b_ref, g_ref, dy_ref, dw_ref, db_ref, dg_ref):
        @pl.when(pl.program_id(0) == 0)
        def _():
            dw_ref[...] = jnp.zeros_like(dw_ref)
            db_ref[...] = jnp.zeros_like(db_ref)
            dg_ref[...] = jnp.zeros_like(dg_ref)

        xv = y_ref[...]
        yg, t = _gelu_parts(xv)
        yg_b = yg.astype(BF16)
        z = jnp.dot(yg_b, w_ref[...], preferred_element_type=F32) + b_ref[...]
        sg = jax.nn.sigmoid(z)
        s = yg * sg
        r = _rms(s)
        n = s * r
        dv = d_ref[:, d_ref.shape[1] - SSMW:]
        dg_ref[...] += _colsum(dv * n)
        ds = _rms_bwd(dv * g_ref[...], n, r)
        dz =(ds * yg) * (sg * (1.0 - sg))
        dz_b = dz.astype(BF16)
        db_ref[...] += _colsum(dz)
        dw_ref[...] += lax.dot_general(yg_b, dz_b, TN_DIMS, preferred_element_type=F32)
        dyg = ds * sg + lax.dot_general(dz_b, w_ref[...], NT_DIMS, preferred_element_type=F32)
        dgelu = 0.5 * (1.0 + t) + (0.5 * xv) * (1.0 - t * t) * (GELU_C * (1.0 + 3.0 * GELU_K * (xv * xv)))
        dy_ref[...] = dyg * dgelu
        _fold8(pl.program_id(0) == L // TR - 1, db_ref, dg_ref)

    vs = jax.ShapeDtypeStruct((8, SSMW), F32)
    return pl.pallas_call(
        body, out_shape=(jax.ShapeDtypeStruct((L, SSMW), F32), jax.ShapeDtypeStruct((SSMW, SSMW), F32), vs, vs),
        grid=(L // TR,),
        in_specs=[_rowspec(SSMW), _rowspec(dsn.shape[1]), pl.BlockSpec((SSMW, SSMW), lambda i: (0, 0)), _vec(SSMW),
                  _vec(SSMW)],
        out_specs=(_rowspec(SSMW), pl.BlockSpec((SSMW, SSMW), lambda i: (0, 0)), _vec8(SSMW), _vec8(SSMW)),
        name='glu_bwd', compiler_params=_params(('arbitrary',), 40))(ypre, dsn, wglu, bglu, gs)


def _me():
    return lax.axis_index('x'), lax.axis_index('y'), lax.axis_index('c')


def _my_index():
    return 4 * lax.axis_index('x') + 2 * lax.axis_index('y') + lax.axis_index('c')


def _peer(k):
    x, y, c = _me()
    px = 1 - x if k & 4 else x
    py = 1 - y if k & 2 else y
    pc = 1 - c if k & 1 else c
    return (px, py, pc), 4 * px + 2 * py + pc


def _mod_exchange(c_row, w_ada, b_ada8, deps=()):
    cw = NMOD * D // N_DEV

    def body(c_ref, w_ref, b_ref, *rest):
        call_ref, mod_ref, part_ref, send_sems, recv_sems = rest[len(deps):]
        x, y, c = _me()
        me = 4 * x + 2 * y + c
        call_ref[me] = c_ref[0]
        sends = []
        for k in range(1, N_DEV):
            peer, _ = _peer(k)
            cp = pltpu.make_async_remote_copy(src_ref=c_ref.at[0], dst_ref=call_ref.at[me], send_sem=send_sems.at[0, k - 1],
                                              recv_sem=recv_sems.at[0, k - 1], device_id=peer, device_id_type=MESH)
            cp.start()
            sends.append(cp)
        for k in range(1, N_DEV):
            peer, pidx = _peer(k)
            pltpu.make_async_remote_copy(src_ref=c_ref.at[0], dst_ref=call_ref.at[pidx], send_sem=send_sems.at[0, k - 1],
                                         recv_sem=recv_sems.at[0, k - 1], device_id=peer, device_id_type=MESH).wait_recv()
        for cp in sends:
            cp.wait_send()
        cv = call_ref[...].reshape(N_DEV, D)
        part = jnp.dot(cv * jax.nn.sigmoid(cv), w_ref[...], preferred_element_type=F32, precision=HI)
        part_ref[...] = part.reshape(N_DEV, 1, cw)
        mod_ref[me] = part_ref[me]
        sends = []
        for k in range(1, N_DEV):
            peer, pidx = _peer(k)
            cp = pltpu.make_async_remote_copy(src_ref=part_ref.at[pidx], dst_ref=mod_ref.at[me], send_sem=send_sems.at[1, k - 1],
                                              recv_sem=recv_sems.at[1, k - 1], device_id=peer, device_id_type=MESH)
            cp.start()
            sends.append(cp)
        for k in range(1, N_DEV):
            peer, pidx = _peer(k)
            pltpu.make_async_remote_copy(src_ref=part_ref.at[pidx], dst_ref=mod_ref.at[pidx], send_sem=send_sems.at[1, k - 1],
                                         recv_sem=recv_sems.at[1, k - 1], device_id=peer, device_id_type=MESH).wait_recv()
        for cp in sends:
            cp.wait_send()
        mod_ref[...] = mod_ref[...] + b_ref[...]

    vm = pl.BlockSpec(memory_space=pltpu.VMEM)
    return pl.pallas_call(
        body, out_shape=(jax.ShapeDtypeStruct((N_DEV, 1, D), F32), jax.ShapeDtypeStruct((N_DEV, 1, cw), F32)),
        in_specs=[vm, vm, vm] + [pl.BlockSpec(memory_space=pl.ANY)] * len(deps), out_specs=(vm, vm),
        scratch_shapes=[pltpu.VMEM((N_DEV, 1, cw), F32), pltpu.SemaphoreType.DMA((2, N_DEV - 1)),
                        pltpu.SemaphoreType.DMA((2, N_DEV - 1))],
        name='mod_exchange', compiler_params=_params(None, 48))(c_row, w_ada, b_ada8, *deps)


HBM_SPEC = pl.BlockSpec(memory_space=pltpu.HBM)
SEM_SPEC = pl.BlockSpec(memory_space=pltpu.SEMAPHORE)
DATAFLOW = pltpu.SideEffectType.DATAFLOW_SIDE_EFFECTING


def _push_start(src, scatter, after, name):
    land = lax.empty(src.shape if scatter else (N_DEV,) + src.shape, src.dtype)

    def body(src_ref, land_ref, after_ref, send_sem, recv_sem, land_thru, token):
        x, y, c = _me()
        me = 4 * x + 2 * y + c
        for k in range(1, N_DEV):
            peer, pidx = _peer(k)
            pltpu.make_async_remote_copy(src_ref=src_ref.at[pidx] if scatter else src_ref, dst_ref=land_ref.at[me],
                                         send_sem=send_sem, recv_sem=recv_sem, device_id=peer,
                                         device_id_type=MESH).start()
        token[...] = jnp.zeros_like(token)

    own = lax.dynamic_index_in_dim(src, _my_index(), 0, keepdims=False) if scatter else src
    src = pltpu.with_memory_space_constraint(src, pltpu.HBM)
    send_sem, recv_sem, land_thru, token = pl.pallas_call(
        body, name=name,
        out_shape=(pltpu.SemaphoreType.DMA(()), pltpu.SemaphoreType.DMA(()),
                   pltpu.HBM(land.shape, land.dtype), jax.ShapeDtypeStruct((8, 128), F32)),
        in_specs=(HBM_SPEC, HBM_SPEC, pl.BlockSpec(memory_space=pl.ANY)),
        out_specs=(SEM_SPEC, SEM_SPEC, HBM_SPEC, pl.BlockSpec(memory_space=pltpu.VMEM)),
        input_output_aliases={1: 2}, compiler_params=pltpu.CompilerParams(has_side_effects=DATAFLOW),
    )(src, pltpu.with_memory_space_constraint(land, pltpu.HBM), after)
    return send_sem, recv_sem, src, land_thru, token, own


def _push_wait(handle, after, name):
    send_sem, recv_sem, src, land_thru, _, own = handle
    after = tuple(after) if isinstance(after, (tuple, list)) else (after,)

    def body(src_ref, land_ref, send_sem, recv_sem, *rest):
        seven = land_ref.at[pl.ds(0, N_DEV - 1)]
        cp = pltpu.make_async_remote_copy(src_ref=seven, dst_ref=seven, send_sem=send_sem, recv_sem=recv_sem,
                                          device_id=_me(), device_id_type=MESH)
        cp.wait_send()
        cp.wait_recv()

    landed = pl.pallas_call(
        body, name=name, out_shape=pltpu.HBM(land_thru.shape, land_thru.dtype),
        in_specs=(HBM_SPEC, HBM_SPEC, SEM_SPEC, SEM_SPEC) + (pl.BlockSpec(memory_space=pl.ANY),) * len(after),
        out_specs=HBM_SPEC, input_output_aliases={1: 0},
        compiler_params=pltpu.CompilerParams(has_side_effects=DATAFLOW),
    )(src, land_thru, send_sem, recv_sem, *after)
    return lax.dynamic_update_index_in_dim(landed, own, _my_index(), 0)


def _adam(w, g, m, v):
    m2 = B1 * m + (1.0 - B1) * g
    v2 = B2 * v + (1.0 - B2) * jnp.square(g)
    m_hat = m2 / (1.0 - B1 ** STEP)
    v_hat = v2 / (1.0 - B2 ** STEP)
    delta = -LR * (m_hat / (jnp.sqrt(v_hat) + AEPS) + WD * w)
    return delta, m2, v2


def _small_update(gp, wp, mp, vp):
    def body(g_ref, w_ref, m_ref, v_ref, all_ref, go_ref, d_ref, mo_ref, vo_ref, send_sems, recv_sems):
        x, y, c = _me()
        me = 4 * x + 2 * y + c
        all_ref[me] = g_ref[...]
        sends = []
        for k in range(1, N_DEV):
            peer, _ = _peer(k)
            cp = pltpu.make_async_remote_copy(src_ref=g_ref, dst_ref=all_ref.at[me], send_sem=send_sems.at[k - 1],
                                              recv_sem=recv_sems.at[k - 1], device_id=peer, device_id_type=MESH)
            cp.start()
            sends.append(cp)
        for k in range(1, N_DEV):
            peer, pidx = _peer(k)
            pltpu.make_async_remote_copy(src_ref=g_ref, dst_ref=all_ref.at[pidx], send_sem=send_sems.at[k - 1],
                                         recv_sem=recv_sems.at[k - 1], device_id=peer, device_id_type=MESH).wait_recv()
        for cp in sends:
            cp.wait_send()
        g = all_ref[0]
        for d in range(1, N_DEV):
            g = g + all_ref[d]
        delta, m2, v2 = _adam(w_ref[...], g, m_ref[...], v_ref[...])
        go_ref[...] = g
        d_ref[...] = delta
        mo_ref[...] = m2
        vo_ref[...] = v2

    vm = pl.BlockSpec(memory_space=pltpu.VMEM)
    vs = jax.ShapeDtypeStruct(gp.shape, F32)
    return pl.pallas_call(
        body, out_shape=(jax.ShapeDtypeStruct((N_DEV,) + gp.shape, F32), vs, vs, vs, vs), in_specs=[vm] * 4,
        out_specs=(vm,) * 5,
        scratch_shapes=[pltpu.SemaphoreType.DMA((N_DEV - 1,)), pltpu.SemaphoreType.DMA((N_DEV - 1,))],
        name='small_update', compiler_params=_params(None, 48))(gp, wp, mp, vp)


def _small_sum_update(parts, wp, mp, vp):
    def body(p_ref, w_ref, m_ref, v_ref, go_ref, d_ref, mo_ref, vo_ref):
        g = p_ref[0]
        for d in range(1, N_DEV):
            g = g + p_ref[d]
        delta, m2, v2 = _adam(w_ref[...], g, m_ref[...], v_ref[...])
        go_ref[...] = g
        d_ref[...] = delta
        mo_ref[...] = m2
        vo_ref[...] = v2

    vm = pl.BlockSpec(memory_space=pltpu.VMEM)
    vs = jax.ShapeDtypeStruct(wp.shape, F32)
    return pl.pallas_call(body, out_shape=(vs, vs, vs, vs), in_specs=[vm] * 4, out_specs=(vm,) * 4,
                          name='small_sum_update', compiler_params=_params(None, 48))(parts, wp, mp, vp)


def _big_update(parts, w, m, v, name):
    _, R, C = parts.shape
    tr = R if R % 256 else (128 if C >= 2048 else 256)

    def body(p_ref, w_ref, m_ref, v_ref, g_ref, d_ref, mo_ref, vo_ref):
        g = p_ref[0].astype(F32)
        for d in range(1, N_DEV):
            g = g + p_ref[d].astype(F32)
        delta, m2, v2 = _adam(w_ref[...], g, m_ref[...], v_ref[...])
        g_ref[...] = g
        d_ref[...] = delta
        mo_ref[...] = m2
        vo_ref[...] = v2

    blk = pl.BlockSpec((tr, C), lambda i: (i, 0))
    sh = jax.ShapeDtypeStruct((R, C), F32)
    return pl.pallas_call(
        body, out_shape=(sh, sh, sh, sh), grid=(R // tr,),
        in_specs=[pl.BlockSpec((N_DEV, tr, C), lambda i: (0, i, 0)), blk, blk, blk], out_specs=(blk,) * 4,
        name=name, compiler_params=_params(('parallel',), 48))(parts, w, m, v)


def _ada_update(c_all, dmod_cols, w, m, v):
    C = w.shape[1]
    tr = 256

    def body(c_ref, dm_ref, w_ref, m_ref, v_ref, g_ref, d_ref, mo_ref, vo_ref):
        cv = c_ref[...]
        s = cv * jax.nn.sigmoid(cv)
        g = lax.dot_general(s, dm_ref[...], TN_DIMS, preferred_element_type=F32, precision=HI)
        delta, m2, v2 = _adam(w_ref[...], g, m_ref[...], v_ref[...])
        g_ref[...] = g
        d_ref[...] = delta
        mo_ref[...] = m2
        vo_ref[...] = v2

    blk = pl.BlockSpec((tr, C), lambda i: (i, 0))
    sh = jax.ShapeDtypeStruct((D, C), F32)
    return pl.pallas_call(
        body, out_shape=(sh, sh, sh, sh), grid=(D // tr,),
        in_specs=[pl.BlockSpec((N_DEV, tr), lambda i: (0, i)), pl.BlockSpec((N_DEV, C), lambda i: (0, 0)), blk, blk, blk],
        out_specs=(blk,) * 4, name='ada_update', compiler_params=_params(('parallel',), 48))(c_all, dmod_cols, w, m, v)


def _to_sub(t, d):
    if d == 1:
        return t
    return t.reshape(L // d, d, t.shape[-1]).transpose(1, 0, 2).reshape(L, t.shape[-1])


def _from_sub(t, d):
    if d == 1:
        return t
    return t.reshape(d, L // d, t.shape[-1]).transpose(1, 0, 2).reshape(L, t.shape[-1])


def _rows_to_cluster_lanes(t):
    k = t.shape[1]
    return t.reshape(N_CL, CL_S, k).transpose(0, 2, 1)


def _blockdiag_in(t):
    t = t.reshape(N_CL, CL_G, SSM_N, SSM_P).transpose(0, 1, 3, 2)
    eye = jnp.eye(CL_G, dtype=t.dtype)
    t = t[:, :, :, None, :] * eye[None, :, None, :, None]
    return t.reshape(N_CL, CL_U, CL_S)


def _blockdiag_extract(t):
    t = t.reshape(N_CL, CL_G, SSM_P, CL_G, SSM_N)
    t = jnp.stack([t[:, i, :, i, :] for i in range(CL_G)], axis=1)
    return t.transpose(0, 1, 3, 2).reshape(SSM_GN, SSM_P)


def _c_to_rows(t):
    return t.transpose(0, 2, 1).reshape(SSM_GN, SSM_P)


def _rows_to_c(t):
    return t.reshape(SSM_G, SSM_N, SSM_P).transpose(0, 2, 1)


def _ssm_prep(sp):
    rows = lambda n: sp[n].reshape(SSM_GN, 1)
    a_re, a_im = rows('ssm_a_re'), rows('ssm_a_im')
    ldt = jnp.repeat(sp['ssm_log_dt'].reshape(SSM_G, 1), SSM_N, axis=0)
    b_re, b_im = sp['ssm_b_re'].reshape(SSM_GN, SSM_P), sp['ssm_b_im'].reshape(SSM_GN, SSM_P)
    c_re, c_im = _c_to_rows(sp['ssm_c_re'].reshape(SSM_G, SSM_P, SSM_N)), _c_to_rows(sp['ssm_c_im'].reshape(SSM_G, SSM_P, SSM_N))
    bbr, bbi, pwr, pwi = _ssm_pre(a_re, a_im, ldt, b_re, b_im)
    bm = jnp.concatenate([_blockdiag_in(bbr), _blockdiag_in(bbi)], axis=2)
    cmt = jnp.concatenate([_blockdiag_in(c_re), -_blockdiag_in(c_im)], axis=2)
    bmt, cm = bm.transpose(0, 2, 1), cmt.transpose(0, 2, 1)
    pw = jnp.concatenate([_rows_to_cluster_lanes(pwr), _rows_to_cluster_lanes(pwi)], axis=2)
    bm2, cm2 = _hilo(bm), _hilo(cm)
    bmt_b, cmt_b = bmt.astype(BF16), cmt.astype(BF16)
    return a_re, a_im, ldt, b_re, b_im, bm2, cm2, bmt_b, cmt_b, pw


def _tied(v, deps):
    for t in deps:
        v = v + t[0, 0]
    return v


def _local_step(x, pos, mod, tgt, sp, prep, get_w, emit, emit_small, emit_late, first_deps=()):
    sh1, sc1, gt1, sh2, sc2, gt2 = (mod[i:i + 1] for i in range(NMOD))
    vec = lambda n: sp[n].reshape(1, -1)
    a_re, a_im, ldt, b_re, b_im, bm2, cm2, bmt_b, cmt_b, pw = prep
    dvec = vec('ssm_d')

    h1 = _prenorm_fwd(x, vec('g_pre_mix'), sc1, sh1)
    w_in = get_w('w_in', (h1, bm2, cm2, pw, bmt_b, cmt_b))
    proj = _mm(h1, w_in, mode='nn', name='mm_in', tn=1408, deps=first_deps)
    fr1 =ROPE_THETA ** (-jnp.arange(0, ROT_DIM, 2, dtype=F32) / ROT_DIM)
    lane = jnp.arange(128) % HEAD_DIM
    fr = jnp.where(lane < ROT_DIM, fr1[lane % (ROT_DIM // 2)], 0.0).reshape(1, 128).astype(F32)
    u_off = (ROPE_W + KVW) // CL_U
    qkvs = _rope_fwd(proj, pos, fr)
    fwd = [_attn_fwd(qkvs[g], d) for g, d in enumerate(DILATIONS)]
    os_, lses = [t[0] for t in fwd], [t[1] for t in fwd]
    att = _combine_fwd(os_, lses, vec('g_attn_out'))

    ypre, bnd, states = _ssm_fwd(proj, bm2, cm2, pw, dvec, u_off)
    w_glu = get_w('w_glu', ypre)
    ssm_n = _glu_fwd(ypre, w_glu, vec('b_glu'), vec('g_ssm_out'))

    cat = jnp.concatenate([att, ssm_n], axis=1)
    w_out = get_w('w_out', cat)
    mix = _mm(cat, w_out, mode='nn', name='mm_out', tk=1280)
    x1, h2 = _postmix_fwd(x, mix, vec('g_post_mix'), gt1, vec('g_pre_mlp'), sc2, sh2)
    w_mi = get_w('w_mlp_in', h2)
    a_pre, r_act = _mm(h2, w_mi, mode='nn', name='mm_mlp_in', epilogue='relu2', b_sharded=True)
    w_mo = get_w('w_mlp_out', a_pre)
    y = _mm(r_act, w_mo, mode='nn', name='mm_mlp_out')
    dx2, dy, loss, dgt2, dg_post_mlp = _final_fwd_bwd(x1, y, tgt, vec('g_post_mlp'), gt2)
    dgt2, dg_post_mlp = dgt2[:1], dg_post_mlp[:1]

    da = _mm(dy, w_mo, mode='nt', name='mm_d_act', out_dtype=BF16, epilogue='drelu2', extra=a_pre)
    dep = emit('w_mlp_out', _mm(r_act, dy, mode='tn', name='mm_dw_mlp_out', out_dtype=BF16))
    dh2 = _mm(da, w_mi, mode='nt', name='mm_dh2', tn=2048, tk=1024, b_sharded=True, deps=dep)
    dep = emit('w_mlp_in', _mm(h2, da, mode='tn', name='mm_dw_mlp_in', out_dtype=BF16, out_sharded=True))
    dx1, dmix, dsc2, dsh2, dg_pre_mlp, dgt1, dg_post_mix = _postmix_bwd(
        dx2, dh2, x1, mix, vec('g_post_mix'), gt1, vec('g_pre_mlp'), sc2)
    dsc2, dsh2, dg_pre_mlp, dgt1, dg_post_mix = (t[:1] for t in (dsc2, dsh2, dg_pre_mlp, dgt1, dg_post_mix))
    dcat = _mm(dmix, w_out, mode='nt', name='mm_dcat', tn=1280, deps=dep)
    dep = emit('w_out', _mm(cat, dmix, mode='tn', name='mm_dw_out', out_dtype=BF16, tm=640))
    dypre, g_w_glu, g_b_glu, g_g_ssm = _glu_bwd(ypre, dcat, w_glu, _tied(vec('b_glu'), dep), vec('g_ssm_out'))
    g_b_glu, g_g_ssm = g_b_glu[:1], g_g_ssm[:1]
    dep = dep + emit('w_glu', g_w_glu.astype(BF16))
    du, dbm, dcm, dA, dD = _ssm_bwd(proj, dypre, states, bmt_b, cmt_b, pw, dvec, bnd, u_off)
    dD = dD[:1]
    gbr, gbi = _blockdiag_extract(dbm[:, :, :CL_S]), _blockdiag_extract(dbm[:, :, CL_S:])
    dcmt = dcm.transpose(0, 2, 1)
    g_c_re = _rows_to_c(_blockdiag_extract(dcmt[:, :, :CL_S]))
    g_c_im = _rows_to_c(-_blockdiag_extract(dcmt[:, :, CL_S:]))
    gar = dA[:, 0, :CL_S].reshape(SSM_GN, 1)
    gai = dA[:, 0, CL_S:].reshape(SSM_GN, 1)
    sel = (jnp.arange(SSM_GN)[None, :] // SSM_N == jnp.arange(SSM_G)[:, None]).astype(F32)
    g_a_re, g_a_im, g_b_re, g_b_im, g_ldt = _ssm_post(a_re, a_im, ldt, b_re, b_im, gar, gai, gbr, gbi, sel)

    head_ones = (jnp.arange(KVW)[:, None] // HEAD_DIM == jnp.arange(KVW)[None, :] // HEAD_DIM).astype(F32)
    dos, dlses, g_g_attn = _combine_bwd(dcat, os_, lses, vec('g_attn_out'), head_ones)
    g_g_attn = g_g_attn[:1]
    dep_small = emit_small({
        'g_post_mix': dg_post_mix, 'ssm_a_re': g_a_re, 'ssm_a_im': g_a_im, 'ssm_log_dt': g_ldt[:, 0],
        'ssm_b_re': g_b_re, 'ssm_b_im': g_b_im, 'ssm_c_re': g_c_re, 'ssm_c_im': g_c_im, 'ssm_d': dD, 'b_glu': g_b_glu,
        'g_attn_out': g_g_attn, 'g_ssm_out': g_g_ssm, 'g_pre_mlp': dg_pre_mlp, 'g_post_mlp': dg_post_mlp})
    dqkv = [_attn_bwd(qkvs[g], os_[g], lses[g], dos[g], dlses[g], d) for g, d in enumerate(DILATIONS)]
    dproj = _rope_bwd(dqkv, du, pos, _tied(fr, dep_small))
    dh1 = _mm(dproj, w_in, mode='nt', name='mm_dh1', tk=1408, deps=dep)
    grad_x, dsc1, dsh1, dg_pre_mix = _prenorm_bwd(dx1, dh1, x, vec('g_pre_mix'), sc1)
    dsc1, dsh1, dg_pre_mix = dsc1[:1], dsh1[:1], dg_pre_mix[:1]
    dmod = jnp.concatenate([dsh1, dsc1, dgt1, dsh2, dsc2, dgt2], axis=0)
    dep = emit_late({'b_ada': dmod, 'g_pre_mix': dg_pre_mix})
    emit('w_in', _mm(h1, dproj, mode='tn', name='mm_dw_in', out_dtype=BF16, tn=1408, deps=dep))
    return loss[0, 0], grad_x


def _pack(d, names):
    flat = jnp.concatenate([jnp.pad(d[n].reshape(-1).astype(F32), (0, SEG[n] - SMALL_SIZES[n])) for n in names])
    return flat.reshape(-1, 128)


def _unpack(packed, names, shapes):
    out, off = {}, 0
    for n in names:
        out[n] = packed[off // 128:(off + SEG[n]) // 128].reshape(-1)[:SMALL_SIZES[n]].reshape(shapes[n])
        off += SEG[n]
    return out


def _shard_major(t, name):
    if name in ('w_in', 'w_out', 'w_mlp_in'):
        k, n = t.shape
        return t.reshape(k, N_DEV, n // N_DEV).transpose(1, 0, 2)
    k, n = t.shape
    return t.reshape(N_DEV, k // N_DEV, n)


def _from_shard_major(t, name):
    if name in ('w_in', 'w_out', 'w_mlp_in'):
        _, k, n = t.shape
        return t.transpose(1, 0, 2).reshape(k, N_DEV * n)
    _, k, n = t.shape
    return t.reshape(N_DEV * k, n)


def kernel(x, c, positions, w_ada, b_ada, g_pre_mix, g_post_mix, w_in, ssm_a_re, ssm_a_im, ssm_log_dt, ssm_b_re, ssm_b_im, ssm_c_re, ssm_c_im, ssm_d, w_glu, b_glu, g_attn_out, g_ssm_out, w_out, g_pre_mlp, g_post_mlp, w_mlp_in, w_mlp_out, loss_target, m_w_ada, m_b_ada, m_g_pre_mix, m_g_post_mix, m_w_in, m_ssm_a_re, m_ssm_a_im, m_ssm_log_dt, m_ssm_b_re, m_ssm_b_im, m_ssm_c_re, m_ssm_c_im, m_ssm_d, m_w_glu, m_b_glu, m_g_attn_out, m_g_ssm_out, m_w_out, m_g_pre_mlp, m_g_post_mlp, m_w_mlp_in, m_w_mlp_out, v_w_ada, v_b_ada, v_g_pre_mix, v_g_post_mix, v_w_in, v_ssm_a_re, v_ssm_a_im, v_ssm_log_dt, v_ssm_b_re, v_ssm_b_im, v_ssm_c_re, v_ssm_c_im, v_ssm_d, v_w_glu, v_b_glu, v_g_attn_out, v_g_ssm_out, v_w_out, v_g_pre_mlp, v_g_post_mlp, v_w_mlp_in, v_w_mlp_out):
    loc = dict(locals())
    W = {n: loc[n] for n in WEIGHTS}
    M = {n: loc['m_' + n] for n in WEIGHTS}
    V = {n: loc['v_' + n] for n in WEIGHTS}
    assert x.shape == (1, L, D) and w_in.shape == (1, D, INW // N_DEV), (x.shape, w_in.shape)

    cw = NMOD * D // N_DEV
    c_all, mod8 = _mod_exchange(c.reshape(1, 1, D), w_ada[0], b_ada.reshape(N_DEV, 1, cw))
    mod = mod8.reshape(NMOD, D)

    gather, after = {}, mod8
    for n in BIG:
        gather[n] = _push_start(W[n][0].astype(BF16), False, after, 'gather_start_' + n)
        after = gather[n][4]
    tokens = tuple(gather[n][4] for n in BIG)
    mod = _tied(mod, tokens)
    sp = {n: W[n][0] for n in SMALL}
    prep = _ssm_prep({**sp, 'ssm_a_re': _tied(sp['ssm_a_re'], tokens)})

    state_packs = {}

    def get_w(n, after):
        if n == 'w_mlp_out':
            tok = jnp.minimum(jnp.abs(after[0:8, 0:128].astype(F32)), 0.0)
            for key, src in (('w', W), ('m', M), ('v', V)):
                tied = {p: _tied(src[p], (tok,)) for p in SMALL}
                state_packs[key] = {'early': _pack(tied, SMALL_EARLY), 'late': _pack(tied, SMALL_LATE)}
            after = (after,) + tuple(state_packs[k][part] for k in 'wmv' for part in ('early', 'late'))
        g = _push_wait(gather[n], after, 'gather_wait_' + n)
        return g if n == 'w_mlp_in' else _from_shard_major(g, n)

    scatter = {}

    def emit(n, g):
        src = g if n == 'w_mlp_in' else _shard_major(g, n)
        scatter[n] = _push_start(src, True, src, 'scatter_start_' + n)
        return (scatter[n][4],)

    small_early = []

    def emit_small(d):
        pack = _pack(d, SMALL_EARLY)
        small_early.append(_push_start(pack, False, pack, 'small_start'))
        return (small_early[0][4],)

    out_g, out_d, out_m, out_v = {}, {}, {}, {}
    shapes = {n: W[n].shape[1:] for n in SMALL}

    def put(names, packs):
        for dst, packed in zip((out_g, out_d, out_m, out_v), packs):
            dst.update(_unpack(packed, names, shapes))

    late = []

    def emit_late(d):
        rows_all, *packs = _small_update(_pack(d, SMALL_LATE), *[state_packs[k]['late'] for k in 'wmv'])
        put(SMALL_LATE, packs)
        late.append(rows_all)
        return (rows_all,)

    loss, grad_x = _local_step(x[0], positions.reshape(L, 1), mod, loss_target[0], sp, prep, get_w, emit, emit_small,
                               emit_late)
    loss = lax.psum(loss, ('x', 'y', 'c'))

    me = 4 * lax.axis_index('x') + 2 * lax.axis_index('y') + lax.axis_index('c')
    dmod_all = late[0][:, :NMOD * D // 128].reshape(N_DEV, NMOD * D)
    dmod_cols = _tied(lax.dynamic_slice_in_dim(dmod_all, me * cw, cw, axis=1), (scatter['w_in'][4],))
    out_g['w_ada'], out_d['w_ada'], out_m['w_ada'], out_v['w_ada'] = _ada_update(
        c_all.reshape(N_DEV, D), dmod_cols, w_ada[0], m_w_ada[0], v_w_ada[0])

    parts = _push_wait(small_early[0], out_v['w_ada'], 'small_wait')
    packs = _small_sum_update(parts, *[state_packs[k]['early'] for k in 'wmv'])
    put(SMALL_EARLY, packs)

    after = packs[3]
    for n in ('w_mlp_out', 'w_mlp_in', 'w_out', 'w_glu', 'w_in'):
        parts = _push_wait(scatter[n], after, 'scatter_wait_' + n)
        out_g[n], out_d[n], out_m[n], out_v[n] = _big_update(parts, W[n][0], M[n][0], V[n][0], 'update_' + n)
        after = out_v[n]

    lead = lambda t: t[None]
    return (loss, grad_x[None], *[lead(out_g[n]) for n in WEIGHTS], *[lead(out_d[n]) for n in WEIGHTS],
            *[lead(out_m[n]) for n in WEIGHTS], *[lead(out_v[n]) for n in WEIGHTS])
```

```python
import functools
import math

import jax
import jax.numpy as jnp
from jax import lax
from jax.experimental import pallas as pl
from jax.experimental.pallas import tpu as pltpu

F32 = jnp.float32
BF16 = jnp.bfloat16
HI = lax.Precision.HIGHEST
MESH = pl.DeviceIdType.MESH

N_DEV = 8
L = 4096
D = 2048
HEAD_DIM = 64
N_GROUPS = 3
DILATIONS = (1, 4, 16)
HEADS = 6
QW = N_GROUPS * HEADS * HEAD_DIM
KVW = HEADS * HEAD_DIM
ROT_DIM = 16
ROPE_THETA = 500000.0
BLK = 128
NBLK = L // BLK
SSMW = D - QW
SSM_P = 16
SSM_G = SSMW // SSM_P
SSM_N = 64
SSM_GN = SSM_G * SSM_N
CL_G = 8
N_CL = SSM_G // CL_G
CL_U = CL_G * SSM_P
CL_S = CL_G * SSM_N
INW = QW + 2 * KVW + SSMW
OUTW = KVW + SSMW
DFF = 4 * D
NMOD = 6
EPS = 1e-6
LR, B1, B2, AEPS, WD, STEP = 0.001, 0.9, 0.999, 1e-08, 0.01, 10

T_SCAN = 1024
MB = 2 ** 20

WEIGHTS = ['w_ada', 'b_ada', 'g_pre_mix', 'g_post_mix', 'w_in', 'ssm_a_re', 'ssm_a_im', 'ssm_log_dt',
           'ssm_b_re', 'ssm_b_im', 'ssm_c_re', 'ssm_c_im', 'ssm_d', 'w_glu', 'b_glu', 'g_attn_out',
           'g_ssm_out', 'w_out', 'g_pre_mlp', 'g_post_mlp', 'w_mlp_in', 'w_mlp_out']
BIG = ['w_in', 'w_glu', 'w_out', 'w_mlp_in', 'w_mlp_out']
SMALL = [n for n in WEIGHTS if n not in BIG and n != 'w_ada']
SMALL_SIZES = {'b_ada': NMOD * D, 'g_pre_mix': D, 'g_post_mix': D, 'ssm_a_re': SSM_GN, 'ssm_a_im': SSM_GN,
               'ssm_log_dt': SSM_G, 'ssm_b_re': SSM_GN * SSM_P, 'ssm_b_im': SSM_GN * SSM_P,
               'ssm_c_re': SSM_GN * SSM_P, 'ssm_c_im': SSM_GN * SSM_P, 'ssm_d': SSMW, 'b_glu': SSMW,
               'g_attn_out': KVW, 'g_ssm_out': SSMW, 'g_pre_mlp': D, 'g_post_mlp': D}
SEG = {n: -(-SMALL_SIZES[n] // 1024) * 1024 for n in SMALL}
SMALL_LATE = ['b_ada', 'g_pre_mix']
SMALL_EARLY = [n for n in SMALL if n not in SMALL_LATE]


def _params(sem=None, vmem_mb=None):
    kw = {}
    if sem is not None:
        kw['dimension_semantics'] = sem
    if vmem_mb is not None:
        kw['vmem_limit_bytes'] = vmem_mb * MB
    return pltpu.CompilerParams(**kw)


def _vec(n):
    return pl.BlockSpec((1, n), lambda *_: (0, 0))


def _rms(x):
    return lax.rsqrt(jnp.mean(x * x, axis=-1, keepdims=True) + EPS)


def _rms_bwd(dn, n, r):
    return r * (dn - n * jnp.mean(dn * n, axis=-1, keepdims=True))


def _vec8(n):
    return pl.BlockSpec((8, n), lambda *_: (0, 0))


def _colsum(x):
    return jnp.sum(x.reshape(-1, 8, x.shape[-1]), axis=0)


def _fold8(last, *refs):
    @pl.when(last)
    def _():
        for r in refs:
            r[...] = jnp.broadcast_to(jnp.sum(r[...], axis=0, keepdims=True), r.shape)


def _mm(a, b, *, mode, name, out_dtype=F32, tm=1024, tn=1024, tk=2048, epilogue=None, extra=None,
        b_sharded=False, out_sharded=False, deps=()):
    if mode == 'nn':
        M, K = a.shape
        dims = (((1,), (0,)), ((), ()))
        a_spec = pl.BlockSpec((tm, tk), lambda i, j, k: (i, k))
        if b_sharded:
            _, K2, per = b.shape
            N, q = N_DEV * per, per // tn
            b_spec = pl.BlockSpec((None, tk, tn), lambda i, j, k: (j // q, k, j % q))
        else:
            K2, N = b.shape
            b_spec = pl.BlockSpec((tk, tn), lambda i, j, k: (k, j))
    elif mode == 'nt':
        M, K = a.shape
        dims = (((1,), (1,)), ((), ()))
        a_spec = pl.BlockSpec((tm, tk), lambda i, j, k: (i, k))
        if b_sharded:
            _, N, per = b.shape
            K2, q = N_DEV * per, per // tk
            b_spec = pl.BlockSpec((None, tn, tk), lambda i, j, k: (k // q, j, k % q))
        else:
            N, K2 = b.shape
            b_spec = pl.BlockSpec((tn, tk), lambda i, j, k: (j, k))
    else:
        (K, M), (K2, N) = a.shape, b.shape
        dims = (((0,), (0,)), ((), ()))
        a_spec = pl.BlockSpec((tk, tm), lambda i, j, k: (k, i))
        b_spec = pl.BlockSpec((tk, tn), lambda i, j, k: (k, j))
    assert K == K2 and M % tm == 0 and N % tn == 0 and K % tk == 0, (name, a.shape, b.shape, tm, tn, tk)
    nk = K // tk
    o_spec = pl.BlockSpec((tm, tn), lambda i, j, k: (i, j))
    o_dims = (M, N)
    if out_sharded:
        qo = N // N_DEV // tn
        o_spec = pl.BlockSpec((None, tm, tn), lambda i, j, k: (j // qo, i, j % qo))
        o_dims = (N_DEV, M, N // N_DEV)
    n_out = 2 if epilogue == 'relu2' else 1
    n_extra = 1 if extra is not None else 0
    n_in = 2 + n_extra + len(deps)

    def body(*refs):
        a_ref, b_ref = refs[0], refs[1]
        x_refs = refs[2:2 + n_extra]
        o_refs = refs[n_in:n_in + n_out]
        acc = refs[-1]
        k = pl.program_id(2)

        prod = lax.dot_general(a_ref[...], b_ref[...], dims, preferred_element_type=F32)

        def finish(r):
            if epilogue == 'relu2':
                o_refs[0][...] = r.astype(BF16)
                o_refs[1][...] = jnp.square(jnp.maximum(r, 0.0)).astype(BF16)
            elif epilogue == 'drelu2':
                pre = x_refs[0][...].astype(F32)
                o_refs[0][...] = (r * (2.0 * jnp.maximum(pre, 0.0))).astype(out_dtype)
            else:
                o_refs[0][...] = r.astype(out_dtype)

        if nk == 1:
            finish(prod)
        else:
            @pl.when(k == 0)
            def _():
                acc[...] = prod

            @pl.when((k > 0) & (k < nk - 1))
            def _():
                acc[...] += prod

            @pl.when(k == nk - 1)
            def _():
                finish(acc[...] + prod)

    if epilogue == 'relu2':
        out_shape = (jax.ShapeDtypeStruct((M, N), BF16), jax.ShapeDtypeStruct((M, N), BF16))
        out_specs = (o_spec, o_spec)
    else:
        out_shape = jax.ShapeDtypeStruct(o_dims, out_dtype)
        out_specs = o_spec
    args = (a, b) + ((extra,) if extra is not None else ()) + tuple(deps)
    in_specs = ([a_spec, b_spec] + ([o_spec] if extra is not None else [])
                + [pl.BlockSpec(memory_space=pl.ANY)] * len(deps))
    return pl.pallas_call(
        body, out_shape=out_shape, grid=(M // tm, N // tn, nk), in_specs=in_specs, out_specs=out_specs,
        scratch_shapes=[pltpu.VMEM((tm, tn) if nk > 1 else (8, 128), F32)], name=name,
        compiler_params=_params(('parallel', 'parallel', 'arbitrary'), 56))(*args)


TR = 256


def _rowspec(w=D):
    return pl.BlockSpec((TR, w), lambda i: (i, 0))


def _prenorm_fwd(x, g, sc, sh):
    def body(x_ref, g_ref, sc_ref, sh_ref, h_ref):
        xv = x_ref[...]
        n = xv * _rms(xv)
        h_ref[...] = ((n * g_ref[...]) * (1.0 + sc_ref[...]) + sh_ref[...]).astype(BF16)

    return pl.pallas_call(
        body, out_shape=jax.ShapeDtypeStruct((L, D), BF16), grid=(L // TR,),
        in_specs=[_rowspec(), _vec(D), _vec(D), _vec(D)], out_specs=_rowspec(), name='prenorm_fwd',
        compiler_params=_params(('parallel',), 40))(x, g, sc, sh)


def _postmix_fwd(x, mix, gpm, gt1, gpl, sc2, sh2):
    def body(x_ref, mix_ref, gpm_ref, gt1_ref, gpl_ref, sc2_ref, sh2_ref, x1_ref, h2_ref):
        mix_v = mix_ref[...]
        nm = mix_v * _rms(mix_v)
        x1 = x_ref[...] + gt1_ref[...] * (nm * gpm_ref[...])
        x1_ref[...] = x1
        n2 = x1 * _rms(x1)
        h2_ref[...] = ((n2 * gpl_ref[...]) * (1.0 + sc2_ref[...]) + sh2_ref[...]).astype(BF16)

    return pl.pallas_call(
        body, out_shape=(jax.ShapeDtypeStruct((L, D), F32), jax.ShapeDtypeStruct((L, D), BF16)), grid=(L // TR,),
        in_specs=[_rowspec(), _rowspec()] + [_vec(D)] * 5, out_specs=(_rowspec(), _rowspec()), name='postmix_fwd',
        compiler_params=_params(('parallel',), 40))(x, mix, gpm, gt1, gpl, sc2, sh2)


def _final_fwd_bwd(x1, y, tgt, g, gt2):
    def body(x1_ref, y_ref, t_ref, g_ref, gt2_ref, dx2_ref, dy_ref, loss_ref, dgt2_ref, dg_ref):
        @pl.when(pl.program_id(0) == 0)
        def _():
            loss_ref[...] = jnp.zeros_like(loss_ref)
            dgt2_ref[...] = jnp.zeros_like(dgt2_ref)
            dg_ref[...] = jnp.zeros_like(dg_ref)

        yv = y_ref[...]
        r = _rms(yv)
        n = yv * r
        ng = n * g_ref[...]
        x2 = x1_ref[...] + gt2_ref[...] * ng
        e = x2 - t_ref[...]
        loss_ref[...] += 0.5 * jnp.sum(jnp.mean(e * e, axis=-1, keepdims=True), axis=0, keepdims=True)
        dx2 = e * (1.0 / D)
        dx2_ref[...] = dx2
        dgt2_ref[...] += _colsum(dx2 * ng)
        dng = dx2 * gt2_ref[...]
        dg_ref[...] += _colsum(dng * n)
        dy_ref[...] = _rms_bwd(dng * g_ref[...], n, r).astype(BF16)
        _fold8(pl.program_id(0) == L // TR - 1, dgt2_ref, dg_ref)

    return pl.pallas_call(
        body,
        out_shape=(jax.ShapeDtypeStruct((L, D), F32), jax.ShapeDtypeStruct((L, D), BF16),
                   jax.ShapeDtypeStruct((8, 128), F32), jax.ShapeDtypeStruct((8, D), F32),
                   jax.ShapeDtypeStruct((8, D), F32)),
        grid=(L // TR,), in_specs=[_rowspec(), _rowspec(), _rowspec(), _vec(D), _vec(D)],
        out_specs=(_rowspec(), _rowspec(), _vec8(128), _vec8(D), _vec8(D)), name='final_fwd_bwd',
        compiler_params=_params(('arbitrary',), 40))(x1, y, tgt, g, gt2)


def _postmix_bwd(dx2, dh2, x1, mix, gpm, gt1, gpl, sc2):
    def body(dx2_ref, dh2_ref, x1_ref, mix_ref, gpm_ref, gt1_ref, gpl_ref, sc2_ref,
             dx1_ref, dmix_ref, dsc2_ref, dsh2_ref, dgpl_ref, dgt1_ref, dgpm_ref):
        @pl.when(pl.program_id(0) == 0)
        def _():
            for r_ in (dsc2_ref, dsh2_ref, dgpl_ref, dgt1_ref, dgpm_ref):
                r_[...] = jnp.zeros_like(r_)

        x1v = x1_ref[...]
        r2 = _rms(x1v)
        n2 = x1v * r2
        dh2v = dh2_ref[...]
        dsh2_ref[...] += _colsum(dh2v)
        dsc2_ref[...] += _colsum(dh2v * (n2 * gpl_ref[...]))
        t = dh2v * (1.0 + sc2_ref[...])
        dgpl_ref[...] += _colsum(t * n2)
        dx1 = dx2_ref[...] + _rms_bwd(t * gpl_ref[...], n2, r2)
        dx1_ref[...] = dx1
        mix_v = mix_ref[...]
        rm = _rms(mix_v)
        nm = mix_v * rm
        dgt1_ref[...] += _colsum(dx1 * (nm * gpm_ref[...]))
        u = dx1 * gt1_ref[...]
        dgpm_ref[...] += _colsum(u * nm)
        dmix_ref[...] = _rms_bwd(u * gpm_ref[...], nm, rm).astype(BF16)
        _fold8(pl.program_id(0) == L // TR - 1, dsc2_ref, dsh2_ref, dgpl_ref, dgt1_ref, dgpm_ref)

    vs = jax.ShapeDtypeStruct((8, D), F32)
    return pl.pallas_call(
        body, out_shape=(jax.ShapeDtypeStruct((L, D), F32), jax.ShapeDtypeStruct((L, D), BF16), vs, vs, vs, vs, vs),
        grid=(L // TR,), in_specs=[_rowspec()] * 4 + [_vec(D)] * 4,
        out_specs=(_rowspec(), _rowspec()) + (_vec8(D),) * 5, name='postmix_bwd',
        compiler_params=_params(('arbitrary',), 48))(dx2, dh2, x1, mix, gpm, gt1, gpl, sc2)


def _prenorm_bwd(dx1, dh1, x, g, sc1):
    def body(dx1_ref, dh1_ref, x_ref, g_ref, sc1_ref, dx_ref, dsc1_ref, dsh1_ref, dg_ref):
        @pl.when(pl.program_id(0) == 0)
        def _():
            for r_ in (dsc1_ref, dsh1_ref, dg_ref):
                r_[...] = jnp.zeros_like(r_)

        xv = x_ref[...]
        r = _rms(xv)
        n = xv * r
        dh = dh1_ref[...]
        dsh1_ref[...] += _colsum(dh)
        dsc1_ref[...] += _colsum(dh * (n * g_ref[...]))
        t = dh * (1.0 + sc1_ref[...])
        dg_ref[...] += _colsum(t * n)
        dx_ref[...] = dx1_ref[...] + _rms_bwd(t * g_ref[...], n, r)
        _fold8(pl.program_id(0) == L // TR - 1, dsc1_ref, dsh1_ref, dg_ref)

    vs = jax.ShapeDtypeStruct((8, D), F32)
    return pl.pallas_call(
        body, out_shape=(jax.ShapeDtypeStruct((L, D), F32), vs, vs, vs), grid=(L // TR,),
        in_specs=[_rowspec()] * 3 + [_vec(D)] * 2, out_specs=(_rowspec(),) + (_vec8(D),) * 3, name='prenorm_bwd',
        compiler_params=_params(('arbitrary',), 40))(dx1, dh1, x, g, sc1)


ROPE_W = QW + KVW
QKV_W = 3 * KVW
NB_KV = KVW // 128


def _rope_rotate(xv, pos, fr, sign):
    ang = pos.astype(F32) * fr
    w = lax.broadcasted_iota(jnp.int32, (1, 128), 1) % HEAD_DIM
    cs = jnp.cos(ang)
    sn = jnp.sin(ang) * sign
    s1 = jnp.where(w < ROT_DIM // 2, -sn, 0.0)
    s2 = jnp.where((w >= ROT_DIM // 2) & (w < ROT_DIM), sn, 0.0)
    width = xv.shape[1]
    rep = width // 128
    cs, s1, s2 = jnp.tile(cs, (1, rep)), jnp.tile(s1, (1, rep)), jnp.tile(s2, (1, rep))
    hi = pltpu.roll(xv, width - ROT_DIM // 2, 1)
    lo = pltpu.roll(xv, ROT_DIM // 2, 1)
    return xv * cs + hi * s1 + lo * s2


def _sub_spec(d, rows, width):
    return pl.BlockSpec((d, rows // d, width), lambda i: (0, i, 0))


def _gather_rows(scr, blocks, r, d, rows):
    return jnp.concatenate([scr.at[j][pl.ds(r, rows // d, stride=d), :] for j in blocks], axis=1)


def _scatter_rows(scr, src_ref, d, rows):
    for r in range(d):
        for j in range(NB_KV):
            scr.at[j][pl.ds(r, rows // d, stride=d), :] = src_ref[r, :, j * 128:(j + 1) * 128]


def _token_order(scr):
    return jnp.concatenate([scr[j] for j in range(NB_KV)], axis=1)


def _rope_fwd(proj, pos, fr):
    nb = (ROPE_W + KVW) // 128

    def body(x_ref, pos_ref, fr_ref, o0_ref, o1_ref, o2_ref, scr):
        y = _rope_rotate(x_ref[:, 0:ROPE_W], pos_ref[...], fr_ref[...], 1.0)
        for j in range(ROPE_W // 128):
            scr[j] = y[:, j * 128:(j + 1) * 128]
        for j in range(ROPE_W // 128, nb):
            scr[j] = x_ref[:, j * 128:(j + 1) * 128]
        kv = list(range(QW // 128, nb))
        for g, (d, o_ref) in enumerate(zip(DILATIONS, (o0_ref, o1_ref, o2_ref))):
            blocks = list(range(g * NB_KV, (g + 1) * NB_KV)) + kv
            for r in range(d):
                o_ref[r] = _gather_rows(scr, blocks, r, d, TR).astype(BF16)

    return pl.pallas_call(
        body, out_shape=tuple(jax.ShapeDtypeStruct((d, L // d, QKV_W), BF16) for d in DILATIONS), grid=(L // TR,),
        in_specs=[_rowspec(ROPE_W + KVW), pl.BlockSpec((TR, 1), lambda i: (i, 0)), _vec(128)],
        out_specs=tuple(_sub_spec(d, TR, QKV_W) for d in DILATIONS),
        scratch_shapes=[pltpu.VMEM((nb, TR, 128), F32)], name='rope_fwd',
        compiler_params=_params(('parallel',), 40))(proj, pos, fr)


def _rope_bwd(dqkv, du, pos, fr):
    def body(*refs):
        grads = [refs[3 * g:3 * g + 3] for g in range(N_GROUPS)]
        du_ref, pos_ref, fr_ref, o_ref = refs[9:13]
        scrs = refs[13:]
        dq, dk, dv = [], None, None
        for g, d in enumerate(DILATIONS):
            parts = []
            for t in range(3):
                if d == 1:
                    parts.append(grads[g][t][0])
                else:
                    scr = scrs[3 * (g - 1) + t]
                    _scatter_rows(scr, grads[g][t], d, TR)
                    parts.append(_token_order(scr))
            dq.append(parts[0])
            dk = parts[1] if dk is None else dk + parts[1]
            dv = parts[2] if dv is None else dv + parts[2]
        x = jnp.concatenate(dq + [dk], axis=1)
        o_ref[:, 0:ROPE_W] = _rope_rotate(x, pos_ref[...], fr_ref[...], -1.0).astype(BF16)
        o_ref[:, ROPE_W:ROPE_W + KVW] = dv.astype(BF16)
        o_ref[:, ROPE_W + KVW:INW] = du_ref[...].astype(BF16)

    flat = [a for grp in dqkv for a in grp]
    in_specs = [_sub_spec(d, TR, KVW) for d in DILATIONS for _ in range(3)]
    in_specs += [_rowspec(SSMW), pl.BlockSpec((TR, 1), lambda i: (i, 0)), _vec(128)]
    return pl.pallas_call(
        body, out_shape=jax.ShapeDtypeStruct((L, INW), BF16), grid=(L // TR,), in_specs=in_specs,
        out_specs=_rowspec(INW), scratch_shapes=[pltpu.VMEM((NB_KV, TR, 128), F32)] * 6, name='rope_bwd',
        compiler_params=_params(('parallel',), 48))(*flat, du, pos, fr)


def _attn_mask(nbs, b):
    first = (b & (nbs - 1)) == 0
    qi = lax.broadcasted_iota(jnp.int32, (BLK, 2 * BLK), 0)
    kj = lax.broadcasted_iota(jnp.int32, (BLK, 2 * BLK), 1)
    dist = qi + BLK - kj
    return (dist >= 0) & (dist <= BLK) & ((kj >= BLK) | jnp.logical_not(first))


def _qkv_specs():
    cur = lambda col: pl.BlockSpec((BLK, KVW), lambda b: (b, col))
    prev = lambda col: pl.BlockSpec((BLK, KVW), lambda b: (jnp.maximum(b - 1, 0), col))
    return [cur(0), prev(1), cur(1), prev(2), cur(2)]


_ROWS = pl.BlockSpec((BLK, KVW), lambda b: (b, 0))
NEG = -1e30
NT_DIMS = (((1,), (1,)), ((), ()))
TN_DIMS = (((0,), (0,)), ((), ()))


def _attn_fwd(qkv, d):
    nbs = L // d // BLK

    def body(q_ref, kp_ref, kc_ref, vp_ref, vc_ref, o_ref, lse_ref):
        valid = _attn_mask(nbs, pl.program_id(0))
        heads = [slice(h * HEAD_DIM, (h + 1) * HEAD_DIM) for h in range(HEADS)]
        kcs = [jnp.concatenate([kp_ref[:, hs], kc_ref[:, hs]], axis=0) for hs in heads]
        vcs = [jnp.concatenate([vp_ref[:, hs], vc_ref[:, hs]], axis=0) for hs in heads]
        ss = [lax.dot_general(q_ref[:, hs], kc, NT_DIMS, preferred_element_type=F32) for hs, kc in zip(heads, kcs)]
        ps, ls, lses = [], [], []
        for s in ss:
            s = jnp.where(valid, s * 0.125, NEG)
            m = jnp.max(s, axis=-1, keepdims=True)
            p = jnp.exp(s - m)
            l = jnp.sum(p, axis=-1, keepdims=True)
            ps.append(p.astype(BF16))
            ls.append(l)
            lses.append(jnp.broadcast_to(m + jnp.log(l), (BLK, HEAD_DIM)))
        outs = [jnp.dot(p, vc, preferred_element_type=F32) / l for p, vc, l in zip(ps, vcs, ls)]
        o_ref[...] = jnp.concatenate(outs, axis=1)
        lse_ref[...] = jnp.concatenate(lses, axis=1)

    sh = jax.ShapeDtypeStruct((L, KVW), F32)
    q2 = qkv.reshape(L, QKV_W)
    o, lse = pl.pallas_call(
        body, out_shape=(sh, sh), grid=(NBLK,), in_specs=_qkv_specs(), out_specs=(_ROWS, _ROWS),
        name='attn_fwd_d%d' % d, compiler_params=_params(('parallel',), 32))(q2, q2, q2, q2, q2)
    return o.reshape(d, L // d, KVW), lse.reshape(d, L // d, KVW)


def _attn_bwd(qkv, o, lse, do, dlse, d):
    nbs = L // d // BLK

    def body(q_ref, kp_ref, kc_ref, vp_ref, vc_ref, o_ref, lse_ref, do_ref, dlse_ref, dq_ref, dk_ref, dv_ref):
        b = pl.program_id(0)

        @pl.when(b == 0)
        def _():
            dk_ref[...] = jnp.zeros_like(dk_ref)
            dv_ref[...] = jnp.zeros_like(dv_ref)

        valid = _attn_mask(nbs, b)
        prev0 = pl.multiple_of(jnp.maximum(b - 1, 0) * BLK, BLK)
        cur0 = pl.multiple_of(b * BLK, BLK)
        heads = [slice(h * HEAD_DIM, (h + 1) * HEAD_DIM) for h in range(HEADS)]
        qs = [q_ref[:, hs] for hs in heads]
        kcs = [jnp.concatenate([kp_ref[:, hs], kc_ref[:, hs]], axis=0) for hs in heads]
        vcs = [jnp.concatenate([vp_ref[:, hs], vc_ref[:, hs]], axis=0) for hs in heads]
        dos = [do_ref[:, hs] for hs in heads]
        do_bs = [t.astype(BF16) for t in dos]
        ss = [lax.dot_general(q, kc, NT_DIMS, preferred_element_type=F32) for q, kc in zip(qs, kcs)]
        dps = [lax.dot_general(do_b, vc, NT_DIMS, preferred_element_type=F32) for do_b, vc in zip(do_bs, vcs)]
        p_bs, ds_bs = [], []
        for h, hs in enumerate(heads):
            s = jnp.where(valid, ss[h] * 0.125, NEG)
            p = jnp.exp(s - lse_ref[:, h * HEAD_DIM:h * HEAD_DIM + 1])
            delta = jnp.sum(dos[h] * o_ref[:, hs], axis=-1, keepdims=True)
            ds = p * (dps[h] - delta + dlse_ref[:, h * HEAD_DIM:h * HEAD_DIM + 1])
            p_bs.append(p.astype(BF16))
            ds_bs.append((ds * 0.125).astype(BF16))
        dqs = [jnp.dot(ds_b, kc, preferred_element_type=F32) for ds_b, kc in zip(ds_bs, kcs)]
        dks = [lax.dot_general(ds_b, q, TN_DIMS, preferred_element_type=F32) for ds_b, q in zip(ds_bs, qs)]
        dvs = [lax.dot_general(p_b, do_b, TN_DIMS, preferred_element_type=F32) for p_b, do_b in zip(p_bs, do_bs)]
        dq_ref[...] = jnp.concatenate(dqs, axis=1)
        dkc, dvc = jnp.concatenate(dks, axis=1), jnp.concatenate(dvs, axis=1)
        dk_ref[pl.ds(prev0, BLK), :] += dkc[:BLK]
        dv_ref[pl.ds(prev0, BLK), :] += dvc[:BLK]
        dk_ref[pl.ds(cur0, BLK), :] += dkc[BLK:]
        dv_ref[pl.ds(cur0, BLK), :] += dvc[BLK:]

    sh = jax.ShapeDtypeStruct((L, KVW), F32)
    whole = pl.BlockSpec((L, KVW), lambda b: (0, 0))
    q2 = qkv.reshape(L, QKV_W)
    flat = lambda t: t.reshape(L, KVW)
    outs = pl.pallas_call(
        body, out_shape=(sh, sh, sh), grid=(NBLK,), in_specs=_qkv_specs() + [_ROWS] * 4,
        out_specs=(_ROWS, whole, whole), name='attn_bwd_d%d' % d,
        compiler_params=_params(('arbitrary',), 48))(q2, q2, q2, q2, q2, flat(o), flat(lse), flat(do), flat(dlse))
    return tuple(t.reshape(d, L // d, KVW) for t in outs)


TC = 512


def _combine_weights(l0, l1, l2):
    m = jnp.maximum(jnp.maximum(l0, l1), l2)
    e0, e1, e2 = jnp.exp(l0 - m), jnp.exp(l1 - m), jnp.exp(l2 - m)
    z = e0 + e1 + e2
    return e0 / z, e1 / z, e2 / z


def _load_groups(refs, scrs):
    out = [refs[0][0]]
    for g in (1, 2):
        _scatter_rows(scrs[g - 1], refs[g], DILATIONS[g], TC)
        out.append(_token_order(scrs[g - 1]))
    return out


def _combine_fwd(os_, lses, g):
    def body(o0, o1, o2, l0, l1, l2, g_ref, att_ref, *scrs):
        ov = _load_groups((o0, o1, o2), scrs[0:2])
        lv = _load_groups((l0, l1, l2), scrs[2:4])
        w0, w1, w2 = _combine_weights(*lv)
        a = w0 * ov[0] + w1 * ov[1] + w2 * ov[2]
        att_ref[...] = ((a * _rms(a)) * g_ref[...]).astype(BF16)

    subs = [_sub_spec(d, TC, KVW) for d in DILATIONS]
    return pl.pallas_call(
        body, out_shape=jax.ShapeDtypeStruct((L, KVW), BF16), grid=(L // TC,), in_specs=subs + subs + [_vec(KVW)],
        out_specs=pl.BlockSpec((TC, KVW), lambda i: (i, 0)),
        scratch_shapes=[pltpu.VMEM((NB_KV, TC, 128), F32)] * 4, name='combine_fwd',
        compiler_params=_params(('parallel',), 40))(*os_, *lses, g)


def _combine_bwd(dcat, os_, lses, g, head_ones):
    def body(datt_ref, o0, o1, o2, l0, l1, l2, g_ref, e_ref, do0, do1, do2, dl0, dl1, dl2, dg_ref, *scrs):
        @pl.when(pl.program_id(0) == 0)
        def _():
            dg_ref[...] = jnp.zeros_like(dg_ref)

        ov = _load_groups((o0, o1, o2), scrs[0:2])
        lv = _load_groups((l0, l1, l2), scrs[2:4])
        ws = _combine_weights(*lv)
        a = ws[0] * ov[0] + ws[1] * ov[1] + ws[2] * ov[2]
        r = _rms(a)
        n = a * r
        dv = datt_ref[...]
        dg_ref[...] += _colsum(dv * n)
        da = _rms_bwd(dv * g_ref[...], n, r)
        dws = [jnp.dot(da * ov[i], e_ref[...], preferred_element_type=F32, precision=HI) for i in range(3)]
        dbar = ws[0] * dws[0] + ws[1] * dws[1] + ws[2] * dws[2]
        scr = scrs[4]
        for i, (d, do_ref, dl_ref) in enumerate(zip(DILATIONS, (do0, do1, do2), (dl0, dl1, dl2))):
            for val, out_ref in ((ws[i] * da, do_ref), (ws[i] * (dws[i] - dbar), dl_ref)):
                if d == 1:
                    out_ref[0] = val
                else:
                    for j in range(NB_KV):
                        scr[j] = val[:, j * 128:(j + 1) * 128]
                    for rr in range(d):
                        out_ref[rr] = _gather_rows(scr, range(NB_KV), rr, d, TC)
        _fold8(pl.program_id(0) == L // TC - 1, dg_ref)

    subs = [_sub_spec(d, TC, KVW) for d in DILATIONS]
    shs = tuple(jax.ShapeDtypeStruct((d, L // d, KVW), F32) for d in DILATIONS)
    outs = pl.pallas_call(
        body, out_shape=shs + shs + (jax.ShapeDtypeStruct((8, KVW), F32),), grid=(L // TC,),
        in_specs=[pl.BlockSpec((TC, KVW), lambda i: (i, 0))] + subs + subs + [_vec(KVW),
                                                                              pl.BlockSpec((KVW, KVW), lambda i: (0, 0))],
        out_specs=tuple(subs) + tuple(subs) + (_vec8(KVW),),
        scratch_shapes=[pltpu.VMEM((NB_KV, TC, 128), F32)] * 5, name='combine_bwd',
        compiler_params=_params(('arbitrary',), 48))(dcat, *os_, *lses, g, head_ones)
    return outs[0:3], outs[3:6], outs[6]


def _ssm_disc(ar, ai, ldt):
    dt = jnp.exp(ldt)
    zr, zi = ar * dt, ai * dt
    ez = jnp.exp(zr)
    A_r, A_i = ez * jnp.cos(zi), ez * jnp.sin(zi)
    den = ar * ar + ai * ai
    xr, xi = A_r - 1.0, A_i
    cr = (xr * ar + xi * ai) / den
    ci = (xi * ar - xr * ai) / den
    return dt, zr, zi, A_r, A_i, den, cr, ci


def _ssm_pre(ar, ai, ldt, br, bi):
    def body(ar_ref, ai_ref, ldt_ref, br_ref, bi_ref, bbr_ref, bbi_ref, pwr_ref, pwi_ref):
        _, zr, zi, _, _, _, cr, ci = _ssm_disc(ar_ref[...], ai_ref[...], ldt_ref[...])
        bbr_ref[...] = cr * br_ref[...] - ci * bi_ref[...]
        bbi_ref[...] = cr * bi_ref[...] + ci * br_ref[...]
        k = (lax.broadcasted_iota(jnp.int32, (1, 8), 1) + 1).astype(F32)
        ek = jnp.exp(zr * k)
        pwr_ref[...] = ek * jnp.cos(zi * k)
        pwi_ref[...] = ek * jnp.sin(zi * k)

    s16 = jax.ShapeDtypeStruct((SSM_GN, SSM_P), F32)
    s8 = jax.ShapeDtypeStruct((SSM_GN, 8), F32)
    return pl.pallas_call(body, out_shape=(s16, s16, s8, s8), name='ssm_pre',
                          compiler_params=_params(None, 40))(ar, ai, ldt, br, bi)


def _ssm_post(ar, ai, ldt, br, bi, gar, gai, gbr, gbi, sel):
    def body(ar_ref, ai_ref, ldt_ref, br_ref, bi_ref, gar_ref, gai_ref, gbr_ref, gbi_ref, sel_ref,
             dar_ref, dai_ref, dbr_ref, dbi_ref, dldt_ref):
        a_r, a_i = ar_ref[...], ai_ref[...]
        dt, _, _, A_r, A_i, den, cr, ci = _ssm_disc(a_r, a_i, ldt_ref[...])
        b_r, b_i, g_br, g_bi = br_ref[...], bi_ref[...], gbr_ref[...], gbi_ref[...]
        gcr = jnp.sum(g_br * b_r + g_bi * b_i, axis=-1, keepdims=True)
        gci = jnp.sum(g_bi * b_r - g_br * b_i, axis=-1, keepdims=True)
        dbr_ref[...] = g_br * cr + g_bi * ci
        dbi_ref[...] = g_bi * cr - g_br * ci
        g_ar = gar_ref[...] + (gcr * a_r - gci * a_i) / den
        g_ai = gai_ref[...] + (gcr * a_i + gci * a_r) / den
        qr = (cr * a_r + ci * a_i) / den
        qi = (ci * a_r - cr * a_i) / den
        glr = -(gcr * qr + gci * qi)
        gli = -(gci * qr - gcr * qi)
        gzr = g_ar * A_r + g_ai * A_i
        gzi = g_ai * A_r - g_ar * A_i
        dar_ref[...] = glr + gzr * dt
        dai_ref[...] = gli + gzi * dt
        gdt = (gzr * a_r + gzi * a_i) * dt
        dldt_ref[...] = jnp.dot(sel_ref[...], jnp.broadcast_to(gdt, (SSM_GN, 128)),
                                preferred_element_type=F32, precision=HI)

    s1 = jax.ShapeDtypeStruct((SSM_GN, 1), F32)
    s16 = jax.ShapeDtypeStruct((SSM_GN, SSM_P), F32)
    return pl.pallas_call(body, out_shape=(s1, s1, s16, s16, jax.ShapeDtypeStruct((SSM_G, 128), F32)),
                          name='ssm_post', compiler_params=_params(None, 48))(
                              ar, ai, ldt, br, bi, gar, gai, gbr, gbi, sel)


SCAN_CH = 8


def _scan_fwd_tiles(s_ref, pw, carry):
    pwr, pwi = pw[:, :CL_S], pw[:, CL_S:]
    row = lax.broadcasted_iota(jnp.int32, (8, CL_S), 0)
    steps = [(k, jnp.where(row >= k, pwr[k - 1:k], 0.0), jnp.where(row >= k, pwi[k - 1:k], 0.0)) for k in (1, 2, 4)]
    rows = 8 * SCAN_CH

    def chunk(i, c):
        cr, ci = c
        r0 = pl.multiple_of(i * rows, rows)
        xr = s_ref[pl.ds(r0, rows), 0:CL_S].reshape(SCAN_CH, 8, CL_S)
        xi = s_ref[pl.ds(r0, rows), CL_S:2 * CL_S].reshape(SCAN_CH, 8, CL_S)
        for k, pr, pi in steps:
            sr, si = pltpu.roll(xr, k, 1), pltpu.roll(xi, k, 1)
            xr, xi = xr + pr * sr - pi * si, xi + pr * si + pi * sr
        for j in range(SCAN_CH):
            tr = xr[j] + pwr * cr - pwi * ci
            ti = xi[j] + pwr * ci + pwi * cr
            s_ref[pl.ds(r0 + 8 * j, 8), 0:CL_S] = tr
            s_ref[pl.ds(r0 + 8 * j, 8), CL_S:2 * CL_S] = ti
            cr, ci = tr[7:8], ti[7:8]
        return cr, ci

    return lax.fori_loop(0, T_SCAN // rows, chunk, (carry[:, :CL_S], carry[:, CL_S:]))


def _scan_bwd_tiles(l_ref, pw, carry):
    pwr, pwi = pw[:, :CL_S], pw[:, CL_S:]
    rpr = jnp.concatenate([pwr[7 - r:8 - r] for r in range(8)], axis=0)
    rpi = jnp.concatenate([pwi[7 - r:8 - r] for r in range(8)], axis=0)
    row = lax.broadcasted_iota(jnp.int32, (8, CL_S), 0)
    steps = [(k, jnp.where(row < 8 - k, pwr[k - 1:k], 0.0), jnp.where(row < 8 - k, pwi[k - 1:k], 0.0))
             for k in (1, 2, 4)]
    rows = 8 * SCAN_CH
    nc = T_SCAN // rows

    def chunk(i, c):
        cr, ci = c
        r0 = pl.multiple_of((nc - 1 - i) * rows, rows)
        xr = l_ref[pl.ds(r0, rows), 0:CL_S].reshape(SCAN_CH, 8, CL_S)
        xi = l_ref[pl.ds(r0, rows), CL_S:2 * CL_S].reshape(SCAN_CH, 8, CL_S)
        for k, pr, pi in steps:
            sr, si = pltpu.roll(xr, 8 - k, 1), pltpu.roll(xi, 8 - k, 1)
            xr, xi = xr + pr * sr + pi * si, xi + pr * si - pi * sr
        for j in reversed(range(SCAN_CH)):
            tr = xr[j] + rpr * cr + rpi * ci
            ti = xi[j] + rpr * ci - rpi * cr
            l_ref[pl.ds(r0 + 8 * j, 8), 0:CL_S] = tr
            l_ref[pl.ds(r0 + 8 * j, 8), CL_S:2 * CL_S] = ti
            cr, ci = tr[0:1], ti[0:1]
        return cr, ci

    return lax.fori_loop(0, nc, chunk, (carry[:, :CL_S], carry[:, CL_S:]))


NT_SCAN = L // T_SCAN


def _hilo(t):
    hi = t.astype(BF16)
    return jnp.stack([hi, (t - hi.astype(F32)).astype(BF16)], axis=1)


def _dot3(a, b_ref):
    ah = a.astype(BF16)
    al = (a - ah.astype(F32)).astype(BF16)
    bh, bl = b_ref[0], b_ref[1]
    return (jnp.dot(ah, bh, preferred_element_type=F32) + jnp.dot(al, bh, preferred_element_type=F32)
            + jnp.dot(ah, bl, preferred_element_type=F32))


def _hl_spec(r, c):
    return pl.BlockSpec((None, 2, r, c), lambda c_, t: (c_, 0, 0, 0))


def _ssm_fwd(u, bm2, cm2, pw, dvec, u_off=0):
    def body(u_ref, bm_ref, cm_ref, pw_ref, d_ref, y_ref, bnd_ref, s_ref, carry_ref):
        @pl.when(pl.program_id(1) == 0)
        def _():
            carry_ref[...] = jnp.zeros_like(carry_ref)

        bnd_ref[...] = carry_ref[...]
        uv = u_ref[...]
        s_ref[...] = jnp.dot(uv.astype(BF16), bm_ref[0], preferred_element_type=F32)
        cr, ci = _scan_fwd_tiles(s_ref, pw_ref[...], carry_ref[...])
        carry_ref[...] = jnp.concatenate([cr, ci], axis=1)
        y_ref[...] = jnp.dot(s_ref[...].astype(BF16), cm_ref[0], preferred_element_type=F32) + d_ref[...] * uv

    return pl.pallas_call(
        body,
        out_shape=(jax.ShapeDtypeStruct((L, SSMW), F32), jax.ShapeDtypeStruct((N_CL, NT_SCAN, 1, 2 * CL_S), F32),
                   jax.ShapeDtypeStruct((L, N_CL * 2 * CL_S), F32)),
        grid=(N_CL, NT_SCAN),
        in_specs=[pl.BlockSpec((T_SCAN, CL_U), lambda c, t: (t, c + u_off)),
                  _hl_spec(CL_U, 2 * CL_S), _hl_spec(2 * CL_S, CL_U),
                  pl.BlockSpec((None, 8, 2 * CL_S), lambda c, t: (c, 0, 0)),
                  pl.BlockSpec((1, CL_U), lambda c, t: (0, c))],
        out_specs=(pl.BlockSpec((T_SCAN, CL_U), lambda c, t: (t, c)),
                   pl.BlockSpec((None, None, 1, 2 * CL_S), lambda c, t: (c, t, 0, 0)),
                   pl.BlockSpec((T_SCAN, 2 * CL_S), lambda c, t: (t, c))),
        scratch_shapes=[pltpu.VMEM((1, 2 * CL_S), F32)],
        name='ssm_fwd', compiler_params=_params(('arbitrary', 'arbitrary'), 40))(u, bm2, cm2, pw, dvec)


def _ssm_bwd(u, dy, states, bmt, cmt, pw, dvec, bnd, u_off=0):
    rev = lambda t: NT_SCAN - 1 - t

    def body(u_ref, dy_ref, s_ref, bmt_ref, cmt_ref, pw_ref, d_ref, bnd_ref,
             du_ref, dbm_ref, dcm_ref, da_ref, dd_ref, l_ref, carry_ref):
        @pl.when(pl.program_id(1) == 0)
        def _():
            carry_ref[...] = jnp.zeros_like(carry_ref)
            dbm_ref[...] = jnp.zeros_like(dbm_ref)
            dcm_ref[...] = jnp.zeros_like(dcm_ref)
            da_ref[...] = jnp.zeros_like(da_ref)
            dd_ref[...] = jnp.zeros_like(dd_ref)

        uv, dyv, pw = u_ref[...], dy_ref[...], pw_ref[...]
        dy_b = dyv.astype(BF16)
        entry = bnd_ref[...]
        l_ref[...] = jnp.dot(dy_b, cmt_ref[...], preferred_element_type=F32)
        cr, ci = _scan_bwd_tiles(l_ref, pw, carry_ref[...])
        carry_ref[...] = jnp.concatenate([cr, ci], axis=1)
        sv, lv = s_ref[...], l_ref[...]
        lv_b = lv.astype(BF16)
        du_ref[...] = dyv * d_ref[...] + jnp.dot(lv_b, bmt_ref[...], preferred_element_type=F32)
        dbm_ref[...] += lax.dot_general(uv.astype(BF16), lv_b, TN_DIMS, preferred_element_type=F32)
        dcm_ref[...] += lax.dot_general(sv.astype(BF16), dy_b, TN_DIMS, preferred_element_type=F32)
        dd_ref[...] += _colsum(dyv * uv)
        row = lax.broadcasted_iota(jnp.int32, (T_SCAN, 2 * CL_S), 0)
        sp = jnp.where(row == 0, entry, pltpu.roll(sv, 1, 0))
        spr, spi = sp[:, :CL_S], sp[:, CL_S:]
        lr, li = lv[:, :CL_S], lv[:, CL_S:]
        da_ref[:, 0:CL_S] += _colsum(lr * spr + li * spi)
        da_ref[:, CL_S:2 * CL_S] += _colsum(li * spr - lr * spi)
        _fold8(pl.program_id(1) == NT_SCAN - 1, da_ref, dd_ref)

    return pl.pallas_call(
        body,
        out_shape=(jax.ShapeDtypeStruct((L, SSMW), F32), jax.ShapeDtypeStruct((N_CL, CL_U, 2 * CL_S), F32),
                   jax.ShapeDtypeStruct((N_CL, 2 * CL_S, CL_U), F32), jax.ShapeDtypeStruct((N_CL, 8, 2 * CL_S), F32),
                   jax.ShapeDtypeStruct((8, SSMW), F32)),
        grid=(N_CL, NT_SCAN),
        in_specs=[pl.BlockSpec((T_SCAN, CL_U), lambda c, t: (rev(t), c + u_off)),
                  pl.BlockSpec((T_SCAN, CL_U), lambda c, t: (rev(t), c)),
                  pl.BlockSpec((T_SCAN, 2 * CL_S), lambda c, t: (rev(t), c)),
                  pl.BlockSpec((None, 2 * CL_S, CL_U), lambda c, t: (c, 0, 0)),
                  pl.BlockSpec((None, CL_U, 2 * CL_S), lambda c, t: (c, 0, 0)),
                  pl.BlockSpec((None, 8, 2 * CL_S), lambda c, t: (c, 0, 0)),
                  pl.BlockSpec((1, CL_U), lambda c, t: (0, c)),
                  pl.BlockSpec((None, None, 1, 2 * CL_S), lambda c, t: (c, rev(t), 0, 0))],
        out_specs=(pl.BlockSpec((T_SCAN, CL_U), lambda c, t: (rev(t), c)),
                   pl.BlockSpec((None, CL_U, 2 * CL_S), lambda c, t: (c, 0, 0)),
                   pl.BlockSpec((None, 2 * CL_S, CL_U), lambda c, t: (c, 0, 0)),
                   pl.BlockSpec((None, 8, 2 * CL_S), lambda c, t: (c, 0, 0)),
                   pl.BlockSpec((8, CL_U), lambda c, t: (0, c))),
        scratch_shapes=[pltpu.VMEM((T_SCAN, 2 * CL_S), F32), pltpu.VMEM((1, 2 * CL_S), F32)],
        name='ssm_bwd', compiler_params=_params(('arbitrary', 'arbitrary'), 48))(u, dy, states, bmt, cmt, pw, dvec, bnd)


GELU_C = math.sqrt(2.0 / math.pi)
GELU_K = 0.044715


def _gelu_parts(x):
    t = jnp.tanh(GELU_C * (x + GELU_K * (x * x * x)))
    return x * (0.5 * (1.0 + t)), t


def _glu_fwd(ypre, wglu, bglu, gs):
    def body(y_ref, w_ref, b_ref, g_ref, o_ref):
        yg, _ = _gelu_parts(y_ref[...])
        z = jnp.dot(yg.astype(BF16), w_ref[...], preferred_element_type=F32) + b_ref[...]
        s = yg * jax.nn.sigmoid(z)
        o_ref[...] = ((s * _rms(s)) * g_ref[...]).astype(BF16)

    return pl.pallas_call(
        body, out_shape=jax.ShapeDtypeStruct((L, SSMW), BF16), grid=(L // TR,),
        in_specs=[_rowspec(SSMW), pl.BlockSpec((SSMW, SSMW), lambda i: (0, 0)), _vec(SSMW), _vec(SSMW)],
        out_specs=_rowspec(SSMW), name='glu_fwd', compiler_params=_params(('parallel',), 32))(ypre, wglu, bglu, gs)


def _glu_bwd(ypre, dsn, wglu, bglu, gs):
    def body(y_ref, d_ref, w_ref, b_ref, g_ref, dy_ref, dw_ref, db_ref, dg_ref):
        @pl.when(pl.program_id(0) == 0)
        def _():
            dw_ref[...] = jnp.zeros_like(dw_ref)
            db_ref[...] = jnp.zeros_like(db_ref)
            dg_ref[...] = jnp.zeros_like(dg_ref)

        xv = y_ref[...]
        yg, t = _gelu_parts(xv)
        yg_b = yg.astype(BF16)
        z = jnp.dot(yg_b, w_ref[...], preferred_element_type=F32) + b_ref[...]
        sg = jax.nn.sigmoid(z)
        s = yg * sg
        r = _rms(s)
        n = s * r
        dv = d_ref[:, d_ref.shape[1] - SSMW:]
        dg_ref[...] += _colsum(dv * n)
        ds = _rms_bwd(dv * g_ref[...], n, r)
        dz =(ds * yg) * (sg * (1.0 - sg))
        dz_b = dz.astype(BF16)
        db_ref[...] += _colsum(dz)
        dw_ref[...] += lax.dot_general(yg_b, dz_b, TN_DIMS, preferred_element_type=F32)
        dyg = ds * sg + lax.dot_general(dz_b, w_ref[...], NT_DIMS, preferred_element_type=F32)
        dgelu = 0.5 * (1.0 + t) + (0.5 * xv) * (1.0 - t * t) * (GELU_C * (1.0 + 3.0 * GELU_K * (xv * xv)))
        dy_ref[...] = dyg * dgelu
        _fold8(pl.program_id(0) == L // TR - 1, db_ref, dg_ref)

    vs = jax.ShapeDtypeStruct((8, SSMW), F32)
    return pl.pallas_call(
        body, out_shape=(jax.ShapeDtypeStruct((L, SSMW), F32), jax.ShapeDtypeStruct((SSMW, SSMW), F32), vs, vs),
        grid=(L // TR,),
        in_specs=[_rowspec(SSMW), _rowspec(dsn.shape[1]), pl.BlockSpec((SSMW, SSMW), lambda i: (0, 0)), _vec(SSMW),
                  _vec(SSMW)],
        out_specs=(_rowspec(SSMW), pl.BlockSpec((SSMW, SSMW), lambda i: (0, 0)), _vec8(SSMW), _vec8(SSMW)),
        name='glu_bwd', compiler_params=_params(('arbitrary',), 40))(ypre, dsn, wglu, bglu, gs)


def _me():
    return lax.axis_index('x'), lax.axis_index('y'), lax.axis_index('c')


def _my_index():
    return 4 * lax.axis_index('x') + 2 * lax.axis_index('y') + lax.axis_index('c')


def _peer(k):
    x, y, c = _me()
    px = 1 - x if k & 4 else x
    py = 1 - y if k & 2 else y
    pc = 1 - c if k & 1 else c
    return (px, py, pc), 4 * px + 2 * py + pc


def _mod_exchange(c_row, w_ada, b_ada8, deps=()):
    cw = NMOD * D // N_DEV

    def body(c_ref, w_ref, b_ref, *rest):
        call_ref, mod_ref, part_ref, send_sems, recv_sems = rest[len(deps):]
        x, y, c = _me()
        me = 4 * x + 2 * y + c
        call_ref[me] = c_ref[0]
        sends = []
        for k in range(1, N_DEV):
            peer, _ = _peer(k)
            cp = pltpu.make_async_remote_copy(src_ref=c_ref.at[0], dst_ref=call_ref.at[me], send_sem=send_sems.at[0, k - 1],
                                              recv_sem=recv_sems.at[0, k - 1], device_id=peer, device_id_type=MESH)
            cp.start()
            sends.append(cp)
        for k in range(1, N_DEV):
            peer, pidx = _peer(k)
            pltpu.make_async_remote_copy(src_ref=c_ref.at[0], dst_ref=call_ref.at[pidx], send_sem=send_sems.at[0, k - 1],
                                         recv_sem=recv_sems.at[0, k - 1], device_id=peer, device_id_type=MESH).wait_recv()
        for cp in sends:
            cp.wait_send()
        cv = call_ref[...].reshape(N_DEV, D)
        part = jnp.dot(cv * jax.nn.sigmoid(cv), w_ref[...], preferred_element_type=F32, precision=HI)
        part_ref[...] = part.reshape(N_DEV, 1, cw)
        mod_ref[me] = part_ref[me]
        sends = []
        for k in range(1, N_DEV):
            peer, pidx = _peer(k)
            cp = pltpu.make_async_remote_copy(src_ref=part_ref.at[pidx], dst_ref=mod_ref.at[me], send_sem=send_sems.at[1, k - 1],
                                              recv_sem=recv_sems.at[1, k - 1], device_id=peer, device_id_type=MESH)
            cp.start()
            sends.append(cp)
        for k in range(1, N_DEV):
            peer, pidx = _peer(k)
            pltpu.make_async_remote_copy(src_ref=part_ref.at[pidx], dst_ref=mod_ref.at[pidx], send_sem=send_sems.at[1, k - 1],
                                         recv_sem=recv_sems.at[1, k - 1], device_id=peer, device_id_type=MESH).wait_recv()
        for cp in sends:
            cp.wait_send()
        mod_ref[...] = mod_ref[...] + b_ref[...]

    vm = pl.BlockSpec(memory_space=pltpu.VMEM)
    return pl.pallas_call(
        body, out_shape=(jax.ShapeDtypeStruct((N_DEV, 1, D), F32), jax.ShapeDtypeStruct((N_DEV, 1, cw), F32)),
        in_specs=[vm, vm, vm] + [pl.BlockSpec(memory_space=pl.ANY)] * len(deps), out_specs=(vm, vm),
        scratch_shapes=[pltpu.VMEM((N_DEV, 1, cw), F32), pltpu.SemaphoreType.DMA((2, N_DEV - 1)),
                        pltpu.SemaphoreType.DMA((2, N_DEV - 1))],
        name='mod_exchange', compiler_params=_params(None, 48))(c_row, w_ada, b_ada8, *deps)


HBM_SPEC = pl.BlockSpec(memory_space=pltpu.HBM)
SEM_SPEC = pl.BlockSpec(memory_space=pltpu.SEMAPHORE)
DATAFLOW = pltpu.SideEffectType.DATAFLOW_SIDE_EFFECTING


def _push_start(src, scatter, after, name):
    land = lax.empty(src.shape if scatter else (N_DEV,) + src.shape, src.dtype)

    def body(src_ref, land_ref, after_ref, send_sem, recv_sem, land_thru, token):
        x, y, c = _me()
        me = 4 * x + 2 * y + c
        for k in range(1, N_DEV):
            peer, pidx = _peer(k)
            pltpu.make_async_remote_copy(src_ref=src_ref.at[pidx] if scatter else src_ref, dst_ref=land_ref.at[me],
                                         send_sem=send_sem, recv_sem=recv_sem, device_id=peer,
                                         device_id_type=MESH).start()
        token[...] = jnp.zeros_like(token)

    own = lax.dynamic_index_in_dim(src, _my_index(), 0, keepdims=False) if scatter else src
    src = pltpu.with_memory_space_constraint(src, pltpu.HBM)
    send_sem, recv_sem, land_thru, token = pl.pallas_call(
        body, name=name,
        out_shape=(pltpu.SemaphoreType.DMA(()), pltpu.SemaphoreType.DMA(()),
                   pltpu.HBM(land.shape, land.dtype), jax.ShapeDtypeStruct((8, 128), F32)),
        in_specs=(HBM_SPEC, HBM_SPEC, pl.BlockSpec(memory_space=pl.ANY)),
        out_specs=(SEM_SPEC, SEM_SPEC, HBM_SPEC, pl.BlockSpec(memory_space=pltpu.VMEM)),
        input_output_aliases={1: 2}, compiler_params=pltpu.CompilerParams(has_side_effects=DATAFLOW),
    )(src, pltpu.with_memory_space_constraint(land, pltpu.HBM), after)
    return send_sem, recv_sem, src, land_thru, token, own


def _push_wait(handle, after, name):
    send_sem, recv_sem, src, land_thru, _, own = handle
    after = tuple(after) if isinstance(after, (tuple, list)) else (after,)

    def body(src_ref, land_ref, send_sem, recv_sem, *rest):
        seven = land_ref.at[pl.ds(0, N_DEV - 1)]
        cp = pltpu.make_async_remote_copy(src_ref=seven, dst_ref=seven, send_sem=send_sem, recv_sem=recv_sem,
                                          device_id=_me(), device_id_type=MESH)
        cp.wait_send()
        cp.wait_recv()

    landed = pl.pallas_call(
        body, name=name, out_shape=pltpu.HBM(land_thru.shape, land_thru.dtype),
        in_specs=(HBM_SPEC, HBM_SPEC, SEM_SPEC, SEM_SPEC) + (pl.BlockSpec(memory_space=pl.ANY),) * len(after),
        out_specs=HBM_SPEC, input_output_aliases={1: 0},
        compiler_params=pltpu.CompilerParams(has_side_effects=DATAFLOW),
    )(src, land_thru, send_sem, recv_sem, *after)
    return lax.dynamic_update_index_in_dim(landed, own, _my_index(), 0)


def _adam(w, g, m, v):
    m2 = B1 * m + (1.0 - B1) * g
    v2 = B2 * v + (1.0 - B2) * jnp.square(g)
    m_hat = m2 / (1.0 - B1 ** STEP)
    v_hat = v2 / (1.0 - B2 ** STEP)
    delta = -LR * (m_hat / (jnp.sqrt(v_hat) + AEPS) + WD * w)
    return delta, m2, v2


def _small_update(gp, wp, mp, vp):
    def body(g_ref, w_ref, m_ref, v_ref, all_ref, go_ref, d_ref, mo_ref, vo_ref, send_sems, recv_sems):
        x, y, c = _me()
        me = 4 * x + 2 * y + c
        all_ref[me] = g_ref[...]
        sends = []
        for k in range(1, N_DEV):
            peer, _ = _peer(k)
            cp = pltpu.make_async_remote_copy(src_ref=g_ref, dst_ref=all_ref.at[me], send_sem=send_sems.at[k - 1],
                                              recv_sem=recv_sems.at[k - 1], device_id=peer, device_id_type=MESH)
            cp.start()
            sends.append(cp)
        for k in range(1, N_DEV):
            peer, pidx = _peer(k)
            pltpu.make_async_remote_copy(src_ref=g_ref, dst_ref=all_ref.at[pidx], send_sem=send_sems.at[k - 1],
                                         recv_sem=recv_sems.at[k - 1], device_id=peer, device_id_type=MESH).wait_recv()
        for cp in sends:
            cp.wait_send()
        g = all_ref[0]
        for d in range(1, N_DEV):
            g = g + all_ref[d]
        delta, m2, v2 = _adam(w_ref[...], g, m_ref[...], v_ref[...])
        go_ref[...] = g
        d_ref[...] = delta
        mo_ref[...] = m2
        vo_ref[...] = v2

    vm = pl.BlockSpec(memory_space=pltpu.VMEM)
    vs = jax.ShapeDtypeStruct(gp.shape, F32)
    return pl.pallas_call(
        body, out_shape=(jax.ShapeDtypeStruct((N_DEV,) + gp.shape, F32), vs, vs, vs, vs), in_specs=[vm] * 4,
        out_specs=(vm,) * 5,
        scratch_shapes=[pltpu.SemaphoreType.DMA((N_DEV - 1,)), pltpu.SemaphoreType.DMA((N_DEV - 1,))],
        name='small_update', compiler_params=_params(None, 48))(gp, wp, mp, vp)


def _small_sum_update(parts, wp, mp, vp):
    def body(p_ref, w_ref, m_ref, v_ref, go_ref, d_ref, mo_ref, vo_ref):
        g = p_ref[0]
        for d in range(1, N_DEV):
            g = g + p_ref[d]
        delta, m2, v2 = _adam(w_ref[...], g, m_ref[...], v_ref[...])
        go_ref[...] = g
        d_ref[...] = delta
        mo_ref[...] = m2
        vo_ref[...] = v2

    vm = pl.BlockSpec(memory_space=pltpu.VMEM)
    vs = jax.ShapeDtypeStruct(wp.shape, F32)
    return pl.pallas_call(body, out_shape=(vs, vs, vs, vs), in_specs=[vm] * 4, out_specs=(vm,) * 4,
                          name='small_sum_update', compiler_params=_params(None, 48))(parts, wp, mp, vp)


def _big_update(parts, w, m, v, name):
    _, R, C = parts.shape
    tr = R if R % 256 else (128 if C >= 2048 else 256)

    def body(p_ref, w_ref, m_ref, v_ref, g_ref, d_ref, mo_ref, vo_ref):
        g = p_ref[0].astype(F32)
        for d in range(1, N_DEV):
            g = g + p_ref[d].astype(F32)
        delta, m2, v2 = _adam(w_ref[...], g, m_ref[...], v_ref[...])
        g_ref[...] = g
        d_ref[...] = delta
        mo_ref[...] = m2
        vo_ref[...] = v2

    blk = pl.BlockSpec((tr, C), lambda i: (i, 0))
    sh = jax.ShapeDtypeStruct((R, C), F32)
    return pl.pallas_call(
        body, out_shape=(sh, sh, sh, sh), grid=(R // tr,),
        in_specs=[pl.BlockSpec((N_DEV, tr, C), lambda i: (0, i, 0)), blk, blk, blk], out_specs=(blk,) * 4,
        name=name, compiler_params=_params(('parallel',), 48))(parts, w, m, v)


def _ada_update(c_all, dmod_cols, w, m, v):
    C = w.shape[1]
    tr = 256

    def body(c_ref, dm_ref, w_ref, m_ref, v_ref, g_ref, d_ref, mo_ref, vo_ref):
        cv = c_ref[...]
        s = cv * jax.nn.sigmoid(cv)
        g = lax.dot_general(s, dm_ref[...], TN_DIMS, preferred_element_type=F32, precision=HI)
        delta, m2, v2 = _adam(w_ref[...], g, m_ref[...], v_ref[...])
        g_ref[...] = g
        d_ref[...] = delta
        mo_ref[...] = m2
        vo_ref[...] = v2

    blk = pl.BlockSpec((tr, C), lambda i: (i, 0))
    sh = jax.ShapeDtypeStruct((D, C), F32)
    return pl.pallas_call(
        body, out_shape=(sh, sh, sh, sh), grid=(D // tr,),
        in_specs=[pl.BlockSpec((N_DEV, tr), lambda i: (0, i)), pl.BlockSpec((N_DEV, C), lambda i: (0, 0)), blk, blk, blk],
        out_specs=(blk,) * 4, name='ada_update', compiler_params=_params(('parallel',), 48))(c_all, dmod_cols, w, m, v)


def _to_sub(t, d):
    if d == 1:
        return t
    return t.reshape(L // d, d, t.shape[-1]).transpose(1, 0, 2).reshape(L, t.shape[-1])


def _from_sub(t, d):
    if d == 1:
        return t
    return t.reshape(d, L // d, t.shape[-1]).transpose(1, 0, 2).reshape(L, t.shape[-1])


def _rows_to_cluster_lanes(t):
    k = t.shape[1]
    return t.reshape(N_CL, CL_S, k).transpose(0, 2, 1)


def _blockdiag_in(t):
    t = t.reshape(N_CL, CL_G, SSM_N, SSM_P).transpose(0, 1, 3, 2)
    eye = jnp.eye(CL_G, dtype=t.dtype)
    t = t[:, :, :, None, :] * eye[None, :, None, :, None]
    return t.reshape(N_CL, CL_U, CL_S)


def _blockdiag_extract(t):
    t = t.reshape(N_CL, CL_G, SSM_P, CL_G, SSM_N)
    t = jnp.stack([t[:, i, :, i, :] for i in range(CL_G)], axis=1)
    return t.transpose(0, 1, 3, 2).reshape(SSM_GN, SSM_P)


def _c_to_rows(t):
    return t.transpose(0, 2, 1).reshape(SSM_GN, SSM_P)


def _rows_to_c(t):
    return t.reshape(SSM_G, SSM_N, SSM_P).transpose(0, 2, 1)


def _ssm_prep(sp):
    rows = lambda n: sp[n].reshape(SSM_GN, 1)
    a_re, a_im = rows('ssm_a_re'), rows('ssm_a_im')
    ldt = jnp.repeat(sp['ssm_log_dt'].reshape(SSM_G, 1), SSM_N, axis=0)
    b_re, b_im = sp['ssm_b_re'].reshape(SSM_GN, SSM_P), sp['ssm_b_im'].reshape(SSM_GN, SSM_P)
    c_re, c_im = _c_to_rows(sp['ssm_c_re'].reshape(SSM_G, SSM_P, SSM_N)), _c_to_rows(sp['ssm_c_im'].reshape(SSM_G, SSM_P, SSM_N))
    bbr, bbi, pwr, pwi = _ssm_pre(a_re, a_im, ldt, b_re, b_im)
    bm = jnp.concatenate([_blockdiag_in(bbr), _blockdiag_in(bbi)], axis=2)
    cmt = jnp.concatenate([_blockdiag_in(c_re), -_blockdiag_in(c_im)], axis=2)
    bmt, cm = bm.transpose(0, 2, 1), cmt.transpose(0, 2, 1)
    pw = jnp.concatenate([_rows_to_cluster_lanes(pwr), _rows_to_cluster_lanes(pwi)], axis=2)
    bm2, cm2 = _hilo(bm), _hilo(cm)
    bmt_b, cmt_b = bmt.astype(BF16), cmt.astype(BF16)
    return a_re, a_im, ldt, b_re, b_im, bm2, cm2, bmt_b, cmt_b, pw


def _tied(v, deps):
    for t in deps:
        v = v + t[0, 0]
    return v


def _local_step(x, pos, mod, tgt, sp, prep, get_w, emit, emit_small, emit_late, first_deps=()):
    sh1, sc1, gt1, sh2, sc2, gt2 = (mod[i:i + 1] for i in range(NMOD))
    vec = lambda n: sp[n].reshape(1, -1)
    a_re, a_im, ldt, b_re, b_im, bm2, cm2, bmt_b, cmt_b, pw = prep
    dvec = vec('ssm_d')

    h1 = _prenorm_fwd(x, vec('g_pre_mix'), sc1, sh1)
    w_in = get_w('w_in', (h1, bm2, cm2, pw, bmt_b, cmt_b))
    proj = _mm(h1, w_in, mode='nn', name='mm_in', tn=1408, deps=first_deps)
    fr1 =ROPE_THETA ** (-jnp.arange(0, ROT_DIM, 2, dtype=F32) / ROT_DIM)
    lane = jnp.arange(128) % HEAD_DIM
    fr = jnp.where(lane < ROT_DIM, fr1[lane % (ROT_DIM // 2)], 0.0).reshape(1, 128).astype(F32)
    u_off = (ROPE_W + KVW) // CL_U
    qkvs = _rope_fwd(proj, pos, fr)
    fwd = [_attn_fwd(qkvs[g], d) for g, d in enumerate(DILATIONS)]
    os_, lses = [t[0] for t in fwd], [t[1] for t in fwd]
    att = _combine_fwd(os_, lses, vec('g_attn_out'))

    ypre, bnd, states = _ssm_fwd(proj, bm2, cm2, pw, dvec, u_off)
    w_glu = get_w('w_glu', ypre)
    ssm_n = _glu_fwd(ypre, w_glu, vec('b_glu'), vec('g_ssm_out'))

    cat = jnp.concatenate([att, ssm_n], axis=1)
    w_out = get_w('w_out', cat)
    mix = _mm(cat, w_out, mode='nn', name='mm_out', tk=1280)
    x1, h2 = _postmix_fwd(x, mix, vec('g_post_mix'), gt1, vec('g_pre_mlp'), sc2, sh2)
    w_mi = get_w('w_mlp_in', h2)
    a_pre, r_act = _mm(h2, w_mi, mode='nn', name='mm_mlp_in', epilogue='relu2', b_sharded=True)
    w_mo = get_w('w_mlp_out', a_pre)
    y = _mm(r_act, w_mo, mode='nn', name='mm_mlp_out')
    dx2, dy, loss, dgt2, dg_post_mlp = _final_fwd_bwd(x1, y, tgt, vec('g_post_mlp'), gt2)
    dgt2, dg_post_mlp = dgt2[:1], dg_post_mlp[:1]

    da = _mm(dy, w_mo, mode='nt', name='mm_d_act', out_dtype=BF16, epilogue='drelu2', extra=a_pre)
    dep = emit('w_mlp_out', _mm(r_act, dy, mode='tn', name='mm_dw_mlp_out', out_dtype=BF16))
    dh2 = _mm(da, w_mi, mode='nt', name='mm_dh2', tn=2048, tk=1024, b_sharded=True, deps=dep)
    dep = emit('w_mlp_in', _mm(h2, da, mode='tn', name='mm_dw_mlp_in', out_dtype=BF16, out_sharded=True))
    dx1, dmix, dsc2, dsh2, dg_pre_mlp, dgt1, dg_post_mix = _postmix_bwd(
        dx2, dh2, x1, mix, vec('g_post_mix'), gt1, vec('g_pre_mlp'), sc2)
    dsc2, dsh2, dg_pre_mlp, dgt1, dg_post_mix = (t[:1] for t in (dsc2, dsh2, dg_pre_mlp, dgt1, dg_post_mix))
    dcat = _mm(dmix, w_out, mode='nt', name='mm_dcat', tn=1280, deps=dep)
    dep = emit('w_out', _mm(cat, dmix, mode='tn', name='mm_dw_out', out_dtype=BF16, tm=640))
    dypre, g_w_glu, g_b_glu, g_g_ssm = _glu_bwd(ypre, dcat, w_glu, _tied(vec('b_glu'), dep), vec('g_ssm_out'))
    g_b_glu, g_g_ssm = g_b_glu[:1], g_g_ssm[:1]
    dep = dep + emit('w_glu', g_w_glu.astype(BF16))
    du, dbm, dcm, dA, dD = _ssm_bwd(proj, dypre, states, bmt_b, cmt_b, pw, dvec, bnd, u_off)
    dD = dD[:1]
    gbr, gbi = _blockdiag_extract(dbm[:, :, :CL_S]), _blockdiag_extract(dbm[:, :, CL_S:])
    dcmt = dcm.transpose(0, 2, 1)
    g_c_re = _rows_to_c(_blockdiag_extract(dcmt[:, :, :CL_S]))
    g_c_im = _rows_to_c(-_blockdiag_extract(dcmt[:, :, CL_S:]))
    gar = dA[:, 0, :CL_S].reshape(SSM_GN, 1)
    gai = dA[:, 0, CL_S:].reshape(SSM_GN, 1)
    sel = (jnp.arange(SSM_GN)[None, :] // SSM_N == jnp.arange(SSM_G)[:, None]).astype(F32)
    g_a_re, g_a_im, g_b_re, g_b_im, g_ldt = _ssm_post(a_re, a_im, ldt, b_re, b_im, gar, gai, gbr, gbi, sel)

    head_ones = (jnp.arange(KVW)[:, None] // HEAD_DIM == jnp.arange(KVW)[None, :] // HEAD_DIM).astype(F32)
    dos, dlses, g_g_attn = _combine_bwd(dcat, os_, lses, vec('g_attn_out'), head_ones)
    g_g_attn = g_g_attn[:1]
    dep_small = emit_small({
        'g_post_mix': dg_post_mix, 'ssm_a_re': g_a_re, 'ssm_a_im': g_a_im, 'ssm_log_dt': g_ldt[:, 0],
        'ssm_b_re': g_b_re, 'ssm_b_im': g_b_im, 'ssm_c_re': g_c_re, 'ssm_c_im': g_c_im, 'ssm_d': dD, 'b_glu': g_b_glu,
        'g_attn_out': g_g_attn, 'g_ssm_out': g_g_ssm, 'g_pre_mlp': dg_pre_mlp, 'g_post_mlp': dg_post_mlp})
    dqkv = [_attn_bwd(qkvs[g], os_[g], lses[g], dos[g], dlses[g], d) for g, d in enumerate(DILATIONS)]
    dproj = _rope_bwd(dqkv, du, pos, _tied(fr, dep_small))
    dh1 = _mm(dproj, w_in, mode='nt', name='mm_dh1', tk=1408, deps=dep)
    grad_x, dsc1, dsh1, dg_pre_mix = _prenorm_bwd(dx1, dh1, x, vec('g_pre_mix'), sc1)
    dsc1, dsh1, dg_pre_mix = dsc1[:1], dsh1[:1], dg_pre_mix[:1]
    dmod = jnp.concatenate([dsh1, dsc1, dgt1, dsh2, dsc2, dgt2], axis=0)
    dep = emit_late({'b_ada': dmod, 'g_pre_mix': dg_pre_mix})
    emit('w_in', _mm(h1, dproj, mode='tn', name='mm_dw_in', out_dtype=BF16, tn=1408, deps=dep))
    return loss[0, 0], grad_x


def _pack(d, names):
    flat = jnp.concatenate([jnp.pad(d[n].reshape(-1).astype(F32), (0, SEG[n] - SMALL_SIZES[n])) for n in names])
    return flat.reshape(-1, 128)


def _unpack(packed, names, shapes):
    out, off = {}, 0
    for n in names:
        out[n] = packed[off // 128:(off + SEG[n]) // 128].reshape(-1)[:SMALL_SIZES[n]].reshape(shapes[n])
        off += SEG[n]
    return out


def _shard_major(t, name):
    if name in ('w_in', 'w_out', 'w_mlp_in'):
        k, n = t.shape
        return t.reshape(k, N_DEV, n // N_DEV).transpose(1, 0, 2)
    k, n = t.shape
    return t.reshape(N_DEV, k // N_DEV, n)


def _from_shard_major(t, name):
    if name in ('w_in', 'w_out', 'w_mlp_in'):
        _, k, n = t.shape
        return t.transpose(1, 0, 2).reshape(k, N_DEV * n)
    _, k, n = t.shape
    return t.reshape(N_DEV * k, n)


def kernel(x, c, positions, w_ada, b_ada, g_pre_mix, g_post_mix, w_in, ssm_a_re, ssm_a_im, ssm_log_dt, ssm_b_re, ssm_b_im, ssm_c_re, ssm_c_im, ssm_d, w_glu, b_glu, g_attn_out, g_ssm_out, w_out, g_pre_mlp, g_post_mlp, w_mlp_in, w_mlp_out, loss_target, m_w_ada, m_b_ada, m_g_pre_mix, m_g_post_mix, m_w_in, m_ssm_a_re, m_ssm_a_im, m_ssm_log_dt, m_ssm_b_re, m_ssm_b_im, m_ssm_c_re, m_ssm_c_im, m_ssm_d, m_w_glu, m_b_glu, m_g_attn_out, m_g_ssm_out, m_w_out, m_g_pre_mlp, m_g_post_mlp, m_w_mlp_in, m_w_mlp_out, v_w_ada, v_b_ada, v_g_pre_mix, v_g_post_mix, v_w_in, v_ssm_a_re, v_ssm_a_im, v_ssm_log_dt, v_ssm_b_re, v_ssm_b_im, v_ssm_c_re, v_ssm_c_im, v_ssm_d, v_w_glu, v_b_glu, v_g_attn_out, v_g_ssm_out, v_w_out, v_g_pre_mlp, v_g_post_mlp, v_w_mlp_in, v_w_mlp_out):
    loc = dict(locals())
    W = {n: loc[n] for n in WEIGHTS}
    M = {n: loc['m_' + n] for n in WEIGHTS}
    V = {n: loc['v_' + n] for n in WEIGHTS}
    assert x.shape == (1, L, D) and w_in.shape == (1, D, INW // N_DEV), (x.shape, w_in.shape)

    cw = NMOD * D // N_DEV
    c_all, mod8 = _mod_exchange(c.reshape(1, 1, D), w_ada[0], b_ada.reshape(N_DEV, 1, cw))
    mod = mod8.reshape(NMOD, D)

    gather, after = {}, mod8
    for n in BIG:
        gather[n] = _push_start(W[n][0].astype(BF16), False, after, 'gather_start_' + n)
        after = gather[n][4]
    tokens = tuple(gather[n][4] for n in BIG)
    mod = _tied(mod, tokens)
    sp = {n: W[n][0] for n in SMALL}
    prep = _ssm_prep({**sp, 'ssm_a_re': _tied(sp['ssm_a_re'], tokens)})

    state_packs = {}

    def get_w(n, after):
        if n == 'w_mlp_out':
            tok = jnp.minimum(jnp.abs(after[0:8, 0:128].astype(F32)), 0.0)
            for key, src in (('w', W), ('m', M), ('v', V)):
                tied = {p: _tied(src[p], (tok,)) for p in SMALL}
                state_packs[key] = {'early': _pack(tied, SMALL_EARLY), 'late': _pack(tied, SMALL_LATE)}
            after = (after,) + tuple(state_packs[k][part] for k in 'wmv' for part in ('early', 'late'))
        g = _push_wait(gather[n], after, 'gather_wait_' + n)
        return g if n == 'w_mlp_in' else _from_shard_major(g, n)

    scatter = {}

    def emit(n, g):
        src = g if n == 'w_mlp_in' else _shard_major(g, n)
        scatter[n] = _push_start(src, True, src, 'scatter_start_' + n)
        return (scatter[n][4],)

    small_early = []

    def emit_small(d):
        pack = _pack(d, SMALL_EARLY)
        small_early.append(_push_start(pack, False, pack, 'small_start'))
        return (small_early[0][4],)

    out_g, out_d, out_m, out_v = {}, {}, {}, {}
    shapes = {n: W[n].shape[1:] for n in SMALL}

    def put(names, packs):
        for dst, packed in zip((out_g, out_d, out_m, out_v), packs):
            dst.update(_unpack(packed, names, shapes))

    late = []

    def emit_late(d):
        rows_all, *packs = _small_update(_pack(d, SMALL_LATE), *[state_packs[k]['late'] for k in 'wmv'])
        put(SMALL_LATE, packs)
        late.append(rows_all)
        return (rows_all,)

    loss, grad_x = _local_step(x[0], positions.reshape(L, 1), mod, loss_target[0], sp, prep, get_w, emit, emit_small,
                               emit_late)
    loss = lax.psum(loss, ('x', 'y', 'c'))

    me = 4 * lax.axis_index('x') + 2 * lax.axis_index('y') + lax.axis_index('c')
    dmod_all = late[0][:, :NMOD * D // 128].reshape(N_DEV, NMOD * D)
    dmod_cols = _tied(lax.dynamic_slice_in_dim(dmod_all, me * cw, cw, axis=1), (scatter['w_in'][4],))
    out_g['w_ada'], out_d['w_ada'], out_m['w_ada'], out_v['w_ada'] = _ada_update(
        c_all.reshape(N_DEV, D), dmod_cols, w_ada[0], m_w_ada[0], v_w_ada[0])

    parts = _push_wait(small_early[0], out_v['w_ada'], 'small_wait')
    packs = _small_sum_update(parts, *[state_packs[k]['early'] for k in 'wmv'])
    put(SMALL_EARLY, packs)

    after = packs[3]
    for n in ('w_mlp_out', 'w_mlp_in', 'w_out', 'w_glu', 'w_in'):
        parts = _push_wait(scatter[n], after, 'scatter_wait_' + n)
        out_g[n], out_d[n], out_m[n], out_v[n] = _big_update(parts, W[n][0], M[n][0], V[n][0], 'update_' + n)
        after = out_v[n]

    lead = lambda t: t[None]
    return (loss, grad_x[None], *[lead(out_g[n]) for n in WEIGHTS], *[lead(out_d[n]) for n in WEIGHTS],
            *[lead(out_m[n]) for n in WEIGHTS], *[lead(out_v[n]) for n in WEIGHTS])
```

```python
import functools
import math

import jax
import jax.numpy as jnp
from jax import lax
from jax.experimental import pallas as pl
from jax.experimental.pallas import tpu as pltpu

F32 = jnp.float32
BF16 = jnp.bfloat16
HI = lax.Precision.HIGHEST
MESH = pl.DeviceIdType.MESH

N_DEV = 8
L = 4096
D = 2048
HEAD_DIM = 64
N_GROUPS = 3
DILATIONS = (1, 4, 16)
HEADS = 6
QW = N_GROUPS * HEADS * HEAD_DIM
KVW = HEADS * HEAD_DIM
ROT_DIM = 16
ROPE_THETA = 500000.0
BLK = 128
NBLK = L // BLK
SSMW = D - QW
SSM_P = 16
SSM_G = SSMW // SSM_P
SSM_N = 64
SSM_GN = SSM_G * SSM_N
CL_G = 8
N_CL = SSM_G // CL_G
CL_U = CL_G * SSM_P
CL_S = CL_G * SSM_N
INW = QW + 2 * KVW + SSMW
OUTW = KVW + SSMW
DFF = 4 * D
NMOD = 6
EPS = 1e-6
LR, B1, B2, AEPS, WD, STEP = 0.001, 0.9, 0.999, 1e-08, 0.01, 10

T_SCAN = 1024
MB = 2 ** 20

WEIGHTS = ['w_ada', 'b_ada', 'g_pre_mix', 'g_post_mix', 'w_in', 'ssm_a_re', 'ssm_a_im', 'ssm_log_dt',
           'ssm_b_re', 'ssm_b_im', 'ssm_c_re', 'ssm_c_im', 'ssm_d', 'w_glu', 'b_glu', 'g_attn_out',
           'g_ssm_out', 'w_out', 'g_pre_mlp', 'g_post_mlp', 'w_mlp_in', 'w_mlp_out']
BIG = ['w_in', 'w_glu', 'w_out', 'w_mlp_in', 'w_mlp_out']
SMALL = [n for n in WEIGHTS if n not in BIG and n != 'w_ada']
SMALL_SIZES = {'b_ada': NMOD * D, 'g_pre_mix': D, 'g_post_mix': D, 'ssm_a_re': SSM_GN, 'ssm_a_im': SSM_GN,
               'ssm_log_dt': SSM_G, 'ssm_b_re': SSM_GN * SSM_P, 'ssm_b_im': SSM_GN * SSM_P,
               'ssm_c_re': SSM_GN * SSM_P, 'ssm_c_im': SSM_GN * SSM_P, 'ssm_d': SSMW, 'b_glu': SSMW,
               'g_attn_out': KVW, 'g_ssm_out': SSMW, 'g_pre_mlp': D, 'g_post_mlp': D}
SEG = {n: -(-SMALL_SIZES[n] // 1024) * 1024 for n in SMALL}
SMALL_LATE = ['b_ada', 'g_pre_mix']
SMALL_EARLY = [n for n in SMALL if n not in SMALL_LATE]


def _params(sem=None, vmem_mb=None):
    kw = {}
    if sem is not None:
        kw['dimension_semantics'] = sem
    if vmem_mb is not None:
        kw['vmem_limit_bytes'] = vmem_mb * MB
    return pltpu.CompilerParams(**kw)


def _vec(n):
    return pl.BlockSpec((1, n), lambda *_: (0, 0))


def _rms(x):
    return lax.rsqrt(jnp.mean(x * x, axis=-1, keepdims=True) + EPS)


def _rms_bwd(dn, n, r):
    return r * (dn - n * jnp.mean(dn * n, axis=-1, keepdims=True))


def _vec8(n):
    return pl.BlockSpec((8, n), lambda *_: (0, 0))


def _colsum(x):
    return jnp.sum(x.reshape(-1, 8, x.shape[-1]), axis=0)


def _fold8(last, *refs):
    @pl.when(last)
    def _():
        for r in refs:
            r[...] = jnp.broadcast_to(jnp.sum(r[...], axis=0, keepdims=True), r.shape)


def _mm(a, b, *, mode, name, out_dtype=F32, tm=1024, tn=1024, tk=2048, epilogue=None, extra=None,
        b_sharded=False, out_sharded=False, deps=()):
    if mode == 'nn':
        M, K = a.shape
        dims = (((1,), (0,)), ((), ()))
        a_spec = pl.BlockSpec((tm, tk), lambda i, j, k: (i, k))
        if b_sharded:
            _, K2, per = b.shape
            N, q = N_DEV * per, per // tn
            b_spec = pl.BlockSpec((None, tk, tn), lambda i, j, k: (j // q, k, j % q))
        else:
            K2, N = b.shape
            b_spec = pl.BlockSpec((tk, tn), lambda i, j, k: (k, j))
    elif mode == 'nt':
        M, K = a.shape
        dims = (((1,), (1,)), ((), ()))
        a_spec = pl.BlockSpec((tm, tk), lambda i, j, k: (i, k))
        if b_sharded:
            _, N, per = b.shape
            K2, q = N_DEV * per, per // tk
            b_spec = pl.BlockSpec((None, tn, tk), lambda i, j, k: (k // q, j, k % q))
        else:
            N, K2 = b.shape
            b_spec = pl.BlockSpec((tn, tk), lambda i, j, k: (j, k))
    else:
        (K, M), (K2, N) = a.shape, b.shape
        dims = (((0,), (0,)), ((), ()))
        a_spec = pl.BlockSpec((tk, tm), lambda i, j, k: (k, i))
        b_spec = pl.BlockSpec((tk, tn), lambda i, j, k: (k, j))
    assert K == K2 and M % tm == 0 and N % tn == 0 and K % tk == 0, (name, a.shape, b.shape, tm, tn, tk)
    nk = K // tk
    o_spec = pl.BlockSpec((tm, tn), lambda i, j, k: (i, j))
    o_dims = (M, N)
    if out_sharded:
        qo = N // N_DEV // tn
        o_spec = pl.BlockSpec((None, tm, tn), lambda i, j, k: (j // qo, i, j % qo))
        o_dims = (N_DEV, M, N // N_DEV)
    n_out = 2 if epilogue == 'relu2' else 1
    n_extra = 1 if extra is not None else 0
    n_in = 2 + n_extra + len(deps)

    def body(*refs):
        a_ref, b_ref = refs[0], refs[1]
        x_refs = refs[2:2 + n_extra]
        o_refs = refs[n_in:n_in + n_out]
        acc = refs[-1]
        k = pl.program_id(2)

        prod = lax.dot_general(a_ref[...], b_ref[...], dims, preferred_element_type=F32)

        def finish(r):
            if epilogue == 'relu2':
                o_refs[0][...] = r.astype(BF16)
                o_refs[1][...] = jnp.square(jnp.maximum(r, 0.0)).astype(BF16)
            elif epilogue == 'drelu2':
                pre = x_refs[0][...].astype(F32)
                o_refs[0][...] = (r * (2.0 * jnp.maximum(pre, 0.0))).astype(out_dtype)
            else:
                o_refs[0][...] = r.astype(out_dtype)

        if nk == 1:
            finish(prod)
        else:
            @pl.when(k == 0)
            def _():
                acc[...] = prod

            @pl.when((k > 0) & (k < nk - 1))
            def _():
                acc[...] += prod

            @pl.when(k == nk - 1)
            def _():
                finish(acc[...] + prod)

    if epilogue == 'relu2':
        out_shape = (jax.ShapeDtypeStruct((M, N), BF16), jax.ShapeDtypeStruct((M, N), BF16))
        out_specs = (o_spec, o_spec)
    else:
        out_shape = jax.ShapeDtypeStruct(o_dims, out_dtype)
        out_specs = o_spec
    args = (a, b) + ((extra,) if extra is not None else ()) + tuple(deps)
    in_specs = ([a_spec, b_spec] + ([o_spec] if extra is not None else [])
                + [pl.BlockSpec(memory_space=pl.ANY)] * len(deps))
    return pl.pallas_call(
        body, out_shape=out_shape, grid=(M // tm, N // tn, nk), in_specs=in_specs, out_specs=out_specs,
        scratch_shapes=[pltpu.VMEM((tm, tn) if nk > 1 else (8, 128), F32)], name=name,
        compiler_params=_params(('parallel', 'parallel', 'arbitrary'), 56))(*args)


TR = 256


def _rowspec(w=D):
    return pl.BlockSpec((TR, w), lambda i: (i, 0))


def _prenorm_fwd(x, g, sc, sh):
    def body(x_ref, g_ref, sc_ref, sh_ref, h_ref):
        xv = x_ref[...]
        n = xv * _rms(xv)
        h_ref[...] = ((n * g_ref[...]) * (1.0 + sc_ref[...]) + sh_ref[...]).astype(BF16)

    return pl.pallas_call(
        body, out_shape=jax.ShapeDtypeStruct((L, D), BF16), grid=(L // TR,),
        in_specs=[_rowspec(), _vec(D), _vec(D), _vec(D)], out_specs=_rowspec(), name='prenorm_fwd',
        compiler_params=_params(('parallel',), 40))(x, g, sc, sh)


def _postmix_fwd(x, mix, gpm, gt1, gpl, sc2, sh2):
    def body(x_ref, mix_ref, gpm_ref, gt1_ref, gpl_ref, sc2_ref, sh2_ref, x1_ref, h2_ref):
        mix_v = mix_ref[...]
        nm = mix_v * _rms(mix_v)
        x1 = x_ref[...] + gt1_ref[...] * (nm * gpm_ref[...])
        x1_ref[...] = x1
        n2 = x1 * _rms(x1)
        h2_ref[...] = ((n2 * gpl_ref[...]) * (1.0 + sc2_ref[...]) + sh2_ref[...]).astype(BF16)

    return pl.pallas_call(
        body, out_shape=(jax.ShapeDtypeStruct((L, D), F32), jax.ShapeDtypeStruct((L, D), BF16)), grid=(L // TR,),
        in_specs=[_rowspec(), _rowspec()] + [_vec(D)] * 5, out_specs=(_rowspec(), _rowspec()), name='postmix_fwd',
        compiler_params=_params(('parallel',), 40))(x, mix, gpm, gt1, gpl, sc2, sh2)


def _final_fwd_bwd(x1, y, tgt, g, gt2):
    def body(x1_ref, y_ref, t_ref, g_ref, gt2_ref, dx2_ref, dy_ref, loss_ref, dgt2_ref, dg_ref):
        @pl.when(pl.program_id(0) == 0)
        def _():
            loss_ref[...] = jnp.zeros_like(loss_ref)
            dgt2_ref[...] = jnp.zeros_like(dgt2_ref)
            dg_ref[...] = jnp.zeros_like(dg_ref)

        yv = y_ref[...]
        r = _rms(yv)
        n = yv * r
        ng = n * g_ref[...]
        x2 = x1_ref[...] + gt2_ref[...] * ng
        e = x2 - t_ref[...]
        loss_ref[...] += 0.5 * jnp.sum(jnp.mean(e * e, axis=-1, keepdims=True), axis=0, keepdims=True)
        dx2 = e * (1.0 / D)
        dx2_ref[...] = dx2
        dgt2_ref[...] += _colsum(dx2 * ng)
        dng = dx2 * gt2_ref[...]
        dg_ref[...] += _colsum(dng * n)
        dy_ref[...] = _rms_bwd(dng * g_ref[...], n, r).astype(BF16)
        _fold8(pl.program_id(0) == L // TR - 1, dgt2_ref, dg_ref)

    return pl.pallas_call(
        body,
        out_shape=(jax.ShapeDtypeStruct((L, D), F32), jax.ShapeDtypeStruct((L, D), BF16),
                   jax.ShapeDtypeStruct((8, 128), F32), jax.ShapeDtypeStruct((8, D), F32),
                   jax.ShapeDtypeStruct((8, D), F32)),
        grid=(L // TR,), in_specs=[_rowspec(), _rowspec(), _rowspec(), _vec(D), _vec(D)],
        out_specs=(_rowspec(), _rowspec(), _vec8(128), _vec8(D), _vec8(D)), name='final_fwd_bwd',
        compiler_params=_params(('arbitrary',), 40))(x1, y, tgt, g, gt2)


def _postmix_bwd(dx2, dh2, x1, mix, gpm, gt1, gpl, sc2):
    def body(dx2_ref, dh2_ref, x1_ref, mix_ref, gpm_ref, gt1_ref, gpl_ref, sc2_ref,
             dx1_ref, dmix_ref, dsc2_ref, dsh2_ref, dgpl_ref, dgt1_ref, dgpm_ref):
        @pl.when(pl.program_id(0) == 0)
        def _():
            for r_ in (dsc2_ref, dsh2_ref, dgpl_ref, dgt1_ref, dgpm_ref):
                r_[...] = jnp.zeros_like(r_)

        x1v = x1_ref[...]
        r2 = _rms(x1v)
        n2 = x1v * r2
        dh2v = dh2_ref[...]
        dsh2_ref[...] += _colsum(dh2v)
        dsc2_ref[...] += _colsum(dh2v * (n2 * gpl_ref[...]))
        t = dh2v * (1.0 + sc2_ref[...])
        dgpl_ref[...] += _colsum(t * n2)
        dx1 = dx2_ref[...] + _rms_bwd(t * gpl_ref[...], n2, r2)
        dx1_ref[...] = dx1
        mix_v = mix_ref[...]
        rm = _rms(mix_v)
        nm = mix_v * rm
        dgt1_ref[...] += _colsum(dx1 * (nm * gpm_ref[...]))
        u = dx1 * gt1_ref[...]
        dgpm_ref[...] += _colsum(u * nm)
        dmix_ref[...] = _rms_bwd(u * gpm_ref[...], nm, rm).astype(BF16)
        _fold8(pl.program_id(0) == L // TR - 1, dsc2_ref, dsh2_ref, dgpl_ref, dgt1_ref, dgpm_ref)

    vs = jax.ShapeDtypeStruct((8, D), F32)
    return pl.pallas_call(
        body, out_shape=(jax.ShapeDtypeStruct((L, D), F32), jax.ShapeDtypeStruct((L, D), BF16), vs, vs, vs, vs, vs),
        grid=(L // TR,), in_specs=[_rowspec()] * 4 + [_vec(D)] * 4,
        out_specs=(_rowspec(), _rowspec()) + (_vec8(D),) * 5, name='postmix_bwd',
        compiler_params=_params(('arbitrary',), 48))(dx2, dh2, x1, mix, gpm, gt1, gpl, sc2)


def _prenorm_bwd(dx1, dh1, x, g, sc1):
    def body(dx1_ref, dh1_ref, x_ref, g_ref, sc1_ref, dx_ref, dsc1_ref, dsh1_ref, dg_ref):
        @pl.when(pl.program_id(0) == 0)
        def _():
            for r_ in (dsc1_ref, dsh1_ref, dg_ref):
                r_[...] = jnp.zeros_like(r_)

        xv = x_ref[...]
        r = _rms(xv)
        n = xv * r
        dh = dh1_ref[...]
        dsh1_ref[...] += _colsum(dh)
        dsc1_ref[...] += _colsum(dh * (n * g_ref[...]))
        t = dh * (1.0 + sc1_ref[...])
        dg_ref[...] += _colsum(t * n)
        dx_ref[...] = dx1_ref[...] + _rms_bwd(t * g_ref[...], n, r)
        _fold8(pl.program_id(0) == L // TR - 1, dsc1_ref, dsh1_ref, dg_ref)

    vs = jax.ShapeDtypeStruct((8, D), F32)
    return pl.pallas_call(
        body, out_shape=(jax.ShapeDtypeStruct((L, D), F32), vs, vs, vs), grid=(L // TR,),
        in_specs=[_rowspec()] * 3 + [_vec(D)] * 2, out_specs=(_rowspec(),) + (_vec8(D),) * 3, name='prenorm_bwd',
        compiler_params=_params(('arbitrary',), 40))(dx1, dh1, x, g, sc1)


ROPE_W = QW + KVW
QKV_W = 3 * KVW
NB_KV = KVW // 128


def _rope_rotate(xv, pos, fr, sign):
    ang = pos.astype(F32) * fr
    w = lax.broadcasted_iota(jnp.int32, (1, 128), 1) % HEAD_DIM
    cs = jnp.cos(ang)
    sn = jnp.sin(ang) * sign
    s1 = jnp.where(w < ROT_DIM // 2, -sn, 0.0)
    s2 = jnp.where((w >= ROT_DIM // 2) & (w < ROT_DIM), sn, 0.0)
    width = xv.shape[1]
    rep = width // 128
    cs, s1, s2 = jnp.tile(cs, (1, rep)), jnp.tile(s1, (1, rep)), jnp.tile(s2, (1, rep))
    hi = pltpu.roll(xv, width - ROT_DIM // 2, 1)
    lo = pltpu.roll(xv, ROT_DIM // 2, 1)
    return xv * cs + hi * s1 + lo * s2


def _sub_spec(d, rows, width):
    return pl.BlockSpec((d, rows // d, width), lambda i: (0, i, 0))


def _gather_rows(scr, blocks, r, d, rows):
    return jnp.concatenate([scr.at[j][pl.ds(r, rows // d, stride=d), :] for j in blocks], axis=1)


def _scatter_rows(scr, src_ref, d, rows):
    for r in range(d):
        for j in range(NB_KV):
            scr.at[j][pl.ds(r, rows // d, stride=d), :] = src_ref[r, :, j * 128:(j + 1) * 128]


def _token_order(scr):
    return jnp.concatenate([scr[j] for j in range(NB_KV)], axis=1)


def _rope_fwd(proj, pos, fr):
    nb = (ROPE_W + KVW) // 128

    def body(x_ref, pos_ref, fr_ref, o0_ref, o1_ref, o2_ref, scr):
        y = _rope_rotate(x_ref[:, 0:ROPE_W], pos_ref[...], fr_ref[...], 1.0)
        for j in range(ROPE_W // 128):
            scr[j] = y[:, j * 128:(j + 1) * 128]
        for j in range(ROPE_W // 128, nb):
            scr[j] = x_ref[:, j * 128:(j + 1) * 128]
        kv = list(range(QW // 128, nb))
        for g, (d, o_ref) in enumerate(zip(DILATIONS, (o0_ref, o1_ref, o2_ref))):
            blocks = list(range(g * NB_KV, (g + 1) * NB_KV)) + kv
            for r in range(d):
                o_ref[r] = _gather_rows(scr, blocks, r, d, TR).astype(BF16)

    return pl.pallas_call(
        body, out_shape=tuple(jax.ShapeDtypeStruct((d, L // d, QKV_W), BF16) for d in DILATIONS), grid=(L // TR,),
        in_specs=[_rowspec(ROPE_W + KVW), pl.BlockSpec((TR, 1), lambda i: (i, 0)), _vec(128)],
        out_specs=tuple(_sub_spec(d, TR, QKV_W) for d in DILATIONS),
        scratch_shapes=[pltpu.VMEM((nb, TR, 128), F32)], name='rope_fwd',
        compiler_params=_params(('parallel',), 40))(proj, pos, fr)


def _rope_bwd(dqkv, du, pos, fr):
    def body(*refs):
        grads = [refs[3 * g:3 * g + 3] for g in range(N_GROUPS)]
        du_ref, pos_ref, fr_ref, o_ref = refs[9:13]
        scrs = refs[13:]
        dq, dk, dv = [], None, None
        for g, d in enumerate(DILATIONS):
            parts = []
            for t in range(3):
                if d == 1:
                    parts.append(grads[g][t][0])
                else:
                    scr = scrs[3 * (g - 1) + t]
                    _scatter_rows(scr, grads[g][t], d, TR)
                    parts.append(_token_order(scr))
            dq.append(parts[0])
            dk = parts[1] if dk is None else dk + parts[1]
            dv = parts[2] if dv is None else dv + parts[2]
        x = jnp.concatenate(dq + [dk], axis=1)
        o_ref[:, 0:ROPE_W] = _rope_rotate(x, pos_ref[...], fr_ref[...], -1.0).astype(BF16)
        o_ref[:, ROPE_W:ROPE_W + KVW] = dv.astype(BF16)
        o_ref[:, ROPE_W + KVW:INW] = du_ref[...].astype(BF16)

    flat = [a for grp in dqkv for a in grp]
    in_specs = [_sub_spec(d, TR, KVW) for d in DILATIONS for _ in range(3)]
    in_specs += [_rowspec(SSMW), pl.BlockSpec((TR, 1), lambda i: (i, 0)), _vec(128)]
    return pl.pallas_call(
        body, out_shape=jax.ShapeDtypeStruct((L, INW), BF16), grid=(L // TR,), in_specs=in_specs,
        out_specs=_rowspec(INW), scratch_shapes=[pltpu.VMEM((NB_KV, TR, 128), F32)] * 6, name='rope_bwd',
        compiler_params=_params(('parallel',), 48))(*flat, du, pos, fr)


def _attn_mask(nbs, b):
    first = (b & (nbs - 1)) == 0
    qi = lax.broadcasted_iota(jnp.int32, (BLK, 2 * BLK), 0)
    kj = lax.broadcasted_iota(jnp.int32, (BLK, 2 * BLK), 1)
    dist = qi + BLK - kj
    return (dist >= 0) & (dist <= BLK) & ((kj >= BLK) | jnp.logical_not(first))


def _qkv_specs():
    cur = lambda col: pl.BlockSpec((BLK, KVW), lambda b: (b, col))
    prev = lambda col: pl.BlockSpec((BLK, KVW), lambda b: (jnp.maximum(b - 1, 0), col))
    return [cur(0), prev(1), cur(1), prev(2), cur(2)]


_ROWS = pl.BlockSpec((BLK, KVW), lambda b: (b, 0))
NEG = -1e30
NT_DIMS = (((1,), (1,)), ((), ()))
TN_DIMS = (((0,), (0,)), ((), ()))


def _attn_fwd(qkv, d):
    nbs = L // d // BLK

    def body(q_ref, kp_ref, kc_ref, vp_ref, vc_ref, o_ref, lse_ref):
        valid = _attn_mask(nbs, pl.program_id(0))
        heads = [slice(h * HEAD_DIM, (h + 1) * HEAD_DIM) for h in range(HEADS)]
        kcs = [jnp.concatenate([kp_ref[:, hs], kc_ref[:, hs]], axis=0) for hs in heads]
        vcs = [jnp.concatenate([vp_ref[:, hs], vc_ref[:, hs]], axis=0) for hs in heads]
        ss = [lax.dot_general(q_ref[:, hs], kc, NT_DIMS, preferred_element_type=F32) for hs, kc in zip(heads, kcs)]
        ps, ls, lses = [], [], []
        for s in ss:
            s = jnp.where(valid, s * 0.125, NEG)
            m = jnp.max(s, axis=-1, keepdims=True)
            p = jnp.exp(s - m)
            l = jnp.sum(p, axis=-1, keepdims=True)
            ps.append(p.astype(BF16))
            ls.append(l)
            lses.append(jnp.broadcast_to(m + jnp.log(l), (BLK, HEAD_DIM)))
        outs = [jnp.dot(p, vc, preferred_element_type=F32) / l for p, vc, l in zip(ps, vcs, ls)]
        o_ref[...] = jnp.concatenate(outs, axis=1)
        lse_ref[...] = jnp.concatenate(lses, axis=1)

    sh = jax.ShapeDtypeStruct((L, KVW), F32)
    q2 = qkv.reshape(L, QKV_W)
    o, lse = pl.pallas_call(
        body, out_shape=(sh, sh), grid=(NBLK,), in_specs=_qkv_specs(), out_specs=(_ROWS, _ROWS),
        name='attn_fwd_d%d' % d, compiler_params=_params(('parallel',), 32))(q2, q2, q2, q2, q2)
    return o.reshape(d, L // d, KVW), lse.reshape(d, L // d, KVW)


def _attn_bwd(qkv, o, lse, do, dlse, d):
    nbs = L // d // BLK

    def body(q_ref, kp_ref, kc_ref, vp_ref, vc_ref, o_ref, lse_ref, do_ref, dlse_ref, dq_ref, dk_ref, dv_ref):
        b = pl.program_id(0)

        @pl.when(b == 0)
        def _():
            dk_ref[...] = jnp.zeros_like(dk_ref)
            dv_ref[...] = jnp.zeros_like(dv_ref)

        valid = _attn_mask(nbs, b)
        prev0 = pl.multiple_of(jnp.maximum(b - 1, 0) * BLK, BLK)
        cur0 = pl.multiple_of(b * BLK, BLK)
        heads = [slice(h * HEAD_DIM, (h + 1) * HEAD_DIM) for h in range(HEADS)]
        qs = [q_ref[:, hs] for hs in heads]
        kcs = [jnp.concatenate([kp_ref[:, hs], kc_ref[:, hs]], axis=0) for hs in heads]
        vcs = [jnp.concatenate([vp_ref[:, hs], vc_ref[:, hs]], axis=0) for hs in heads]
        dos = [do_ref[:, hs] for hs in heads]
        do_bs = [t.astype(BF16) for t in dos]
        ss = [lax.dot_general(q, kc, NT_DIMS, preferred_element_type=F32) for q, kc in zip(qs, kcs)]
        dps = [lax.dot_general(do_b, vc, NT_DIMS, preferred_element_type=F32) for do_b, vc in zip(do_bs, vcs)]
        p_bs, ds_bs = [], []
        for h, hs in enumerate(heads):
            s = jnp.where(valid, ss[h] * 0.125, NEG)
            p = jnp.exp(s - lse_ref[:, h * HEAD_DIM:h * HEAD_DIM + 1])
            delta = jnp.sum(dos[h] * o_ref[:, hs], axis=-1, keepdims=True)
            ds = p * (dps[h] - delta + dlse_ref[:, h * HEAD_DIM:h * HEAD_DIM + 1])
            p_bs.append(p.astype(BF16))
            ds_bs.append((ds * 0.125).astype(BF16))
        dqs = [jnp.dot(ds_b, kc, preferred_element_type=F32) for ds_b, kc in zip(ds_bs, kcs)]
        dks = [lax.dot_general(ds_b, q, TN_DIMS, preferred_element_type=F32) for ds_b, q in zip(ds_bs, qs)]
        dvs = [lax.dot_general(p_b, do_b, TN_DIMS, preferred_element_type=F32) for p_b, do_b in zip(p_bs, do_bs)]
        dq_ref[...] = jnp.concatenate(dqs, axis=1)
        dkc, dvc = jnp.concatenate(dks, axis=1), jnp.concatenate(dvs, axis=1)
        dk_ref[pl.ds(prev0, BLK), :] += dkc[:BLK]
        dv_ref[pl.ds(prev0, BLK), :] += dvc[:BLK]
        dk_ref[pl.ds(cur0, BLK), :] += dkc[BLK:]
        dv_ref[pl.ds(cur0, BLK), :] += dvc[BLK:]

    sh = jax.ShapeDtypeStruct((L, KVW), F32)
    whole = pl.BlockSpec((L, KVW), lambda b: (0, 0))
    q2 = qkv.reshape(L, QKV_W)
    flat = lambda t: t.reshape(L, KVW)
    outs = pl.pallas_call(
        body, out_shape=(sh, sh, sh), grid=(NBLK,), in_specs=_qkv_specs() + [_ROWS] * 4,
        out_specs=(_ROWS, whole, whole), name='attn_bwd_d%d' % d,
        compiler_params=_params(('arbitrary',), 48))(q2, q2, q2, q2, q2, flat(o), flat(lse), flat(do), flat(dlse))
    return tuple(t.reshape(d, L // d, KVW) for t in outs)


TC = 512


def _combine_weights(l0, l1, l2):
    m = jnp.maximum(jnp.maximum(l0, l1), l2)
    e0, e1, e2 = jnp.exp(l0 - m), jnp.exp(l1 - m), jnp.exp(l2 - m)
    z = e0 + e1 + e2
    return e0 / z, e1 / z, e2 / z


def _load_groups(refs, scrs):
    out = [refs[0][0]]
    for g in (1, 2):
        _scatter_rows(scrs[g - 1], refs[g], DILATIONS[g], TC)
        out.append(_token_order(scrs[g - 1]))
    return out


def _combine_fwd(os_, lses, g):
    def body(o0, o1, o2, l0, l1, l2, g_ref, att_ref, *scrs):
        ov = _load_groups((o0, o1, o2), scrs[0:2])
        lv = _load_groups((l0, l1, l2), scrs[2:4])
        w0, w1, w2 = _combine_weights(*lv)
        a = w0 * ov[0] + w1 * ov[1] + w2 * ov[2]
        att_ref[...] = ((a * _rms(a)) * g_ref[...]).astype(BF16)

    subs = [_sub_spec(d, TC, KVW) for d in DILATIONS]
    return pl.pallas_call(
        body, out_shape=jax.ShapeDtypeStruct((L, KVW), BF16), grid=(L // TC,), in_specs=subs + subs + [_vec(KVW)],
        out_specs=pl.BlockSpec((TC, KVW), lambda i: (i, 0)),
        scratch_shapes=[pltpu.VMEM((NB_KV, TC, 128), F32)] * 4, name='combine_fwd',
        compiler_params=_params(('parallel',), 40))(*os_, *lses, g)


def _combine_bwd(dcat, os_, lses, g, head_ones):
    def body(datt_ref, o0, o1, o2, l0, l1, l2, g_ref, e_ref, do0, do1, do2, dl0, dl1, dl2, dg_ref, *scrs):
        @pl.when(pl.program_id(0) == 0)
        def _():
            dg_ref[...] = jnp.zeros_like(dg_ref)

        ov = _load_groups((o0, o1, o2), scrs[0:2])
        lv = _load_groups((l0, l1, l2), scrs[2:4])
        ws = _combine_weights(*lv)
        a = ws[0] * ov[0] + ws[1] * ov[1] + ws[2] * ov[2]
        r = _rms(a)
        n = a * r
        dv = datt_ref[...]
        dg_ref[...] += _colsum(dv * n)
        da = _rms_bwd(dv * g_ref[...], n, r)
        e_b = e_ref[...].astype(BF16)

        def head_sum(t):
            hi = t.astype(BF16)
            lo = (t - hi.astype(F32)).astype(BF16)
            return jnp.dot(hi, e_b, preferred_element_type=F32) + jnp.dot(lo, e_b, preferred_element_type=F32)

        dws = [head_sum(da * ov[i]) for i in range(3)]
        dbar = ws[0] * dws[0] + ws[1] * dws[1] + ws[2] * dws[2]
        scr = scrs[4]
        for i, (d, do_ref, dl_ref) in enumerate(zip(DILATIONS, (do0, do1, do2), (dl0, dl1, dl2))):
            for val, out_ref in ((ws[i] * da, do_ref), (ws[i] * (dws[i] - dbar), dl_ref)):
                if d == 1:
                    out_ref[0] = val
                else:
                    for j in range(NB_KV):
                        scr[j] = val[:, j * 128:(j + 1) * 128]
                    for rr in range(d):
                        out_ref[rr] = _gather_rows(scr, range(NB_KV), rr, d, TC)
        _fold8(pl.program_id(0) == L // TC - 1, dg_ref)

    subs = [_sub_spec(d, TC, KVW) for d in DILATIONS]
    shs = tuple(jax.ShapeDtypeStruct((d, L // d, KVW), F32) for d in DILATIONS)
    outs = pl.pallas_call(
        body, out_shape=shs + shs + (jax.ShapeDtypeStruct((8, KVW), F32),), grid=(L // TC,),
        in_specs=[pl.BlockSpec((TC, KVW), lambda i: (i, 0))] + subs + subs + [_vec(KVW),
                                                                              pl.BlockSpec((KVW, KVW), lambda i: (0, 0))],
        out_specs=tuple(subs) + tuple(subs) + (_vec8(KVW),),
        scratch_shapes=[pltpu.VMEM((NB_KV, TC, 128), F32)] * 5, name='combine_bwd',
        compiler_params=_params(('arbitrary',), 48))(dcat, *os_, *lses, g, head_ones)
    return outs[0:3], outs[3:6], outs[6]


def _ssm_disc(ar, ai, ldt):
    dt = jnp.exp(ldt)
    zr, zi = ar * dt, ai * dt
    ez = jnp.exp(zr)
    A_r, A_i = ez * jnp.cos(zi), ez * jnp.sin(zi)
    den = ar * ar + ai * ai
    xr, xi = A_r - 1.0, A_i
    cr = (xr * ar + xi * ai) / den
    ci = (xi * ar - xr * ai) / den
    return dt, zr, zi, A_r, A_i, den, cr, ci


def _ssm_pre(ar, ai, ldt, br, bi):
    def body(ar_ref, ai_ref, ldt_ref, br_ref, bi_ref, bbr_ref, bbi_ref, pwr_ref, pwi_ref):
        _, zr, zi, _, _, _, cr, ci = _ssm_disc(ar_ref[...], ai_ref[...], ldt_ref[...])
        bbr_ref[...] = cr * br_ref[...] - ci * bi_ref[...]
        bbi_ref[...] = cr * bi_ref[...] + ci * br_ref[...]
        k = (lax.broadcasted_iota(jnp.int32, (1, 8), 1) + 1).astype(F32)
        ek = jnp.exp(zr * k)
        pwr_ref[...] = ek * jnp.cos(zi * k)
        pwi_ref[...] = ek * jnp.sin(zi * k)

    s16 = jax.ShapeDtypeStruct((SSM_GN, SSM_P), F32)
    s8 = jax.ShapeDtypeStruct((SSM_GN, 8), F32)
    return pl.pallas_call(body, out_shape=(s16, s16, s8, s8), name='ssm_pre',
                          compiler_params=_params(None, 40))(ar, ai, ldt, br, bi)


def _ssm_post(ar, ai, ldt, br, bi, gar, gai, gbr, gbi, sel):
    def body(ar_ref, ai_ref, ldt_ref, br_ref, bi_ref, gar_ref, gai_ref, gbr_ref, gbi_ref, sel_ref,
             dar_ref, dai_ref, dbr_ref, dbi_ref, dldt_ref):
        a_r, a_i = ar_ref[...], ai_ref[...]
        dt, _, _, A_r, A_i, den, cr, ci = _ssm_disc(a_r, a_i, ldt_ref[...])
        b_r, b_i, g_br, g_bi = br_ref[...], bi_ref[...], gbr_ref[...], gbi_ref[...]
        gcr = jnp.sum(g_br * b_r + g_bi * b_i, axis=-1, keepdims=True)
        gci = jnp.sum(g_bi * b_r - g_br * b_i, axis=-1, keepdims=True)
        dbr_ref[...] = g_br * cr + g_bi * ci
        dbi_ref[...] = g_bi * cr - g_br * ci
        g_ar = gar_ref[...] + (gcr * a_r - gci * a_i) / den
        g_ai = gai_ref[...] + (gcr * a_i + gci * a_r) / den
        qr = (cr * a_r + ci * a_i) / den
        qi = (ci * a_r - cr * a_i) / den
        glr = -(gcr * qr + gci * qi)
        gli = -(gci * qr - gcr * qi)
        gzr = g_ar * A_r + g_ai * A_i
        gzi = g_ai * A_r - g_ar * A_i
        dar_ref[...] = glr + gzr * dt
        dai_ref[...] = gli + gzi * dt
        gdt = (gzr * a_r + gzi * a_i) * dt
        dldt_ref[...] = jnp.dot(sel_ref[...], jnp.broadcast_to(gdt, (SSM_GN, 128)),
                                preferred_element_type=F32, precision=HI)

    s1 = jax.ShapeDtypeStruct((SSM_GN, 1), F32)
    s16 = jax.ShapeDtypeStruct((SSM_GN, SSM_P), F32)
    return pl.pallas_call(body, out_shape=(s1, s1, s16, s16, jax.ShapeDtypeStruct((SSM_G, 128), F32)),
                          name='ssm_post', compiler_params=_params(None, 48))(
                              ar, ai, ldt, br, bi, gar, gai, gbr, gbi, sel)


SCAN_CH = 8


def _scan_fwd_tiles(s_ref, pw, carry):
    pwr, pwi = pw[:, :CL_S], pw[:, CL_S:]
    row = lax.broadcasted_iota(jnp.int32, (8, CL_S), 0)
    steps = [(k, jnp.where(row >= k, pwr[k - 1:k], 0.0), jnp.where(row >= k, pwi[k - 1:k], 0.0)) for k in (1, 2, 4)]
    rows = 8 * SCAN_CH

    def chunk(i, c):
        cr, ci = c
        r0 = pl.multiple_of(i * rows, rows)
        xr = s_ref[pl.ds(r0, rows), 0:CL_S].reshape(SCAN_CH, 8, CL_S)
        xi = s_ref[pl.ds(r0, rows), CL_S:2 * CL_S].reshape(SCAN_CH, 8, CL_S)
        for k, pr, pi in steps:
            sr, si = pltpu.roll(xr, k, 1), pltpu.roll(xi, k, 1)
            xr, xi = xr + pr * sr - pi * si, xi + pr * si + pi * sr
        for j in range(SCAN_CH):
            tr = xr[j] + pwr * cr - pwi * ci
            ti = xi[j] + pwr * ci + pwi * cr
            s_ref[pl.ds(r0 + 8 * j, 8), 0:CL_S] = tr
            s_ref[pl.ds(r0 + 8 * j, 8), CL_S:2 * CL_S] = ti
            cr, ci = tr[7:8], ti[7:8]
        return cr, ci

    return lax.fori_loop(0, T_SCAN // rows, chunk, (carry[:, :CL_S], carry[:, CL_S:]))


def _scan_bwd_tiles(l_ref, pw, carry):
    pwr, pwi = pw[:, :CL_S], pw[:, CL_S:]
    rpr = jnp.concatenate([pwr[7 - r:8 - r] for r in range(8)], axis=0)
    rpi = jnp.concatenate([pwi[7 - r:8 - r] for r in range(8)], axis=0)
    row = lax.broadcasted_iota(jnp.int32, (8, CL_S), 0)
    steps = [(k, jnp.where(row < 8 - k, pwr[k - 1:k], 0.0), jnp.where(row < 8 - k, pwi[k - 1:k], 0.0))
             for k in (1, 2, 4)]
    rows = 8 * SCAN_CH
    nc = T_SCAN // rows

    def chunk(i, c):
        cr, ci = c
        r0 = pl.multiple_of((nc - 1 - i) * rows, rows)
        xr = l_ref[pl.ds(r0, rows), 0:CL_S].reshape(SCAN_CH, 8, CL_S)
        xi = l_ref[pl.ds(r0, rows), CL_S:2 * CL_S].reshape(SCAN_CH, 8, CL_S)
        for k, pr, pi in steps:
            sr, si = pltpu.roll(xr, 8 - k, 1), pltpu.roll(xi, 8 - k, 1)
            xr, xi = xr + pr * sr + pi * si, xi + pr * si - pi * sr
        for j in reversed(range(SCAN_CH)):
            tr = xr[j] + rpr * cr + rpi * ci
            ti = xi[j] + rpr * ci - rpi * cr
            l_ref[pl.ds(r0 + 8 * j, 8), 0:CL_S] = tr
            l_ref[pl.ds(r0 + 8 * j, 8), CL_S:2 * CL_S] = ti
            cr, ci = tr[0:1], ti[0:1]
        return cr, ci

    return lax.fori_loop(0, nc, chunk, (carry[:, :CL_S], carry[:, CL_S:]))


NT_SCAN = L // T_SCAN


def _hilo(t):
    hi = t.astype(BF16)
    return jnp.stack([hi, (t - hi.astype(F32)).astype(BF16)], axis=1)


def _dot3(a, b_ref):
    ah = a.astype(BF16)
    al = (a - ah.astype(F32)).astype(BF16)
    bh, bl = b_ref[0], b_ref[1]
    return (jnp.dot(ah, bh, preferred_element_type=F32) + jnp.dot(al, bh, preferred_element_type=F32)
            + jnp.dot(ah, bl, preferred_element_type=F32))


def _hl_spec(r, c):
    return pl.BlockSpec((None, 2, r, c), lambda c_, t: (c_, 0, 0, 0))


def _ssm_fwd(u, bm2, cm2, pw, dvec, u_off=0):
    def body(u_ref, bm_ref, cm_ref, pw_ref, d_ref, y_ref, bnd_ref, s_ref, carry_ref):
        @pl.when(pl.program_id(1) == 0)
        def _():
            carry_ref[...] = jnp.zeros_like(carry_ref)

        bnd_ref[...] = carry_ref[...]
        uv = u_ref[...]
        s_ref[...] = jnp.dot(uv.astype(BF16), bm_ref[0], preferred_element_type=F32)
        cr, ci = _scan_fwd_tiles(s_ref, pw_ref[...], carry_ref[...])
        carry_ref[...] = jnp.concatenate([cr, ci], axis=1)
        y_ref[...] = jnp.dot(s_ref[...].astype(BF16), cm_ref[0], preferred_element_type=F32) + d_ref[...] * uv

    return pl.pallas_call(
        body,
        out_shape=(jax.ShapeDtypeStruct((L, SSMW), F32), jax.ShapeDtypeStruct((N_CL, NT_SCAN, 1, 2 * CL_S), F32),
                   jax.ShapeDtypeStruct((L, N_CL * 2 * CL_S), F32)),
        grid=(N_CL, NT_SCAN),
        in_specs=[pl.BlockSpec((T_SCAN, CL_U), lambda c, t: (t, c + u_off)),
                  _hl_spec(CL_U, 2 * CL_S), _hl_spec(2 * CL_S, CL_U),
                  pl.BlockSpec((None, 8, 2 * CL_S), lambda c, t: (c, 0, 0)),
                  pl.BlockSpec((1, CL_U), lambda c, t: (0, c))],
        out_specs=(pl.BlockSpec((T_SCAN, CL_U), lambda c, t: (t, c)),
                   pl.BlockSpec((None, None, 1, 2 * CL_S), lambda c, t: (c, t, 0, 0)),
                   pl.BlockSpec((T_SCAN, 2 * CL_S), lambda c, t: (t, c))),
        scratch_shapes=[pltpu.VMEM((1, 2 * CL_S), F32)],
        name='ssm_fwd', compiler_params=_params(('arbitrary', 'arbitrary'), 40))(u, bm2, cm2, pw, dvec)


def _ssm_bwd(u, dy, states, bmt, cmt, pw, dvec, bnd, u_off=0):
    rev = lambda t: NT_SCAN - 1 - t

    def body(u_ref, dy_ref, s_ref, bmt_ref, cmt_ref, pw_ref, d_ref, bnd_ref,
             du_ref, dbm_ref, dcm_ref, da_ref, dd_ref, l_ref, carry_ref):
        @pl.when(pl.program_id(1) == 0)
        def _():
            carry_ref[...] = jnp.zeros_like(carry_ref)
            dbm_ref[...] = jnp.zeros_like(dbm_ref)
            dcm_ref[...] = jnp.zeros_like(dcm_ref)
            da_ref[...] = jnp.zeros_like(da_ref)
            dd_ref[...] = jnp.zeros_like(dd_ref)

        uv, dyv, pw = u_ref[...], dy_ref[...], pw_ref[...]
        dy_b = dyv.astype(BF16)
        entry = bnd_ref[...]
        l_ref[...] = jnp.dot(dy_b, cmt_ref[...], preferred_element_type=F32)
        cr, ci = _scan_bwd_tiles(l_ref, pw, carry_ref[...])
        carry_ref[...] = jnp.concatenate([cr, ci], axis=1)
        sv, lv = s_ref[...], l_ref[...]
        lv_b = lv.astype(BF16)
        du_ref[...] = dyv * d_ref[...] + jnp.dot(lv_b, bmt_ref[...], preferred_element_type=F32)
        dbm_ref[...] += lax.dot_general(uv.astype(BF16), lv_b, TN_DIMS, preferred_element_type=F32)
        dcm_ref[...] += lax.dot_general(sv.astype(BF16), dy_b, TN_DIMS, preferred_element_type=F32)
        dd_ref[...] += _colsum(dyv * uv)
        row = lax.broadcasted_iota(jnp.int32, (T_SCAN, 2 * CL_S), 0)
        sp = jnp.where(row == 0, entry, pltpu.roll(sv, 1, 0))
        spr, spi = sp[:, :CL_S], sp[:, CL_S:]
        lr, li = lv[:, :CL_S], lv[:, CL_S:]
        da_ref[:, 0:CL_S] += _colsum(lr * spr + li * spi)
        da_ref[:, CL_S:2 * CL_S] += _colsum(li * spr - lr * spi)
        _fold8(pl.program_id(1) == NT_SCAN - 1, da_ref, dd_ref)

    return pl.pallas_call(
        body,
        out_shape=(jax.ShapeDtypeStruct((L, SSMW), F32), jax.ShapeDtypeStruct((N_CL, CL_U, 2 * CL_S), F32),
                   jax.ShapeDtypeStruct((N_CL, 2 * CL_S, CL_U), F32), jax.ShapeDtypeStruct((N_CL, 8, 2 * CL_S), F32),
                   jax.ShapeDtypeStruct((8, SSMW), F32)),
        grid=(N_CL, NT_SCAN),
        in_specs=[pl.BlockSpec((T_SCAN, CL_U), lambda c, t: (rev(t), c + u_off)),
                  pl.BlockSpec((T_SCAN, CL_U), lambda c, t: (rev(t), c)),
                  pl.BlockSpec((T_SCAN, 2 * CL_S), lambda c, t: (rev(t), c)),
                  pl.BlockSpec((None, 2 * CL_S, CL_U), lambda c, t: (c, 0, 0)),
                  pl.BlockSpec((None, CL_U, 2 * CL_S), lambda c, t: (c, 0, 0)),
                  pl.BlockSpec((None, 8, 2 * CL_S), lambda c, t: (c, 0, 0)),
                  pl.BlockSpec((1, CL_U), lambda c, t: (0, c)),
                  pl.BlockSpec((None, None, 1, 2 * CL_S), lambda c, t: (c, rev(t), 0, 0))],
        out_specs=(pl.BlockSpec((T_SCAN, CL_U), lambda c, t: (rev(t), c)),
                   pl.BlockSpec((None, CL_U, 2 * CL_S), lambda c, t: (c, 0, 0)),
                   pl.BlockSpec((None, 2 * CL_S, CL_U), lambda c, t: (c, 0, 0)),
                   pl.BlockSpec((None, 8, 2 * CL_S), lambda c, t: (c, 0, 0)),
                   pl.BlockSpec((8, CL_U), lambda c, t: (0, c))),
        scratch_shapes=[pltpu.VMEM((T_SCAN, 2 * CL_S), F32), pltpu.VMEM((1, 2 * CL_S), F32)],
        name='ssm_bwd', compiler_params=_params(('arbitrary', 'arbitrary'), 48))(u, dy, states, bmt, cmt, pw, dvec, bnd)


GELU_C = math.sqrt(2.0 / math.pi)
GELU_K = 0.044715


def _gelu_parts(x):
    t = jnp.tanh(GELU_C * (x + GELU_K * (x * x * x)))
    return x * (0.5 * (1.0 + t)), t


def _glu_fwd(ypre, wglu, bglu, gs):
    def body(y_ref, w_ref, b_ref, g_ref, o_ref):
        yg, _ = _gelu_parts(y_ref[...])
        z = jnp.dot(yg.astype(BF16), w_ref[...], preferred_element_type=F32) + b_ref[...]
        s = yg * jax.nn.sigmoid(z)
        o_ref[...] = ((s * _rms(s)) * g_ref[...]).astype(BF16)

    return pl.pallas_call(
        body, out_shape=jax.ShapeDtypeStruct((L, SSMW), BF16), grid=(L // TR,),
        in_specs=[_rowspec(SSMW), pl.BlockSpec((SSMW, SSMW), lambda i: (0, 0)), _vec(SSMW), _vec(SSMW)],
        out_specs=_rowspec(SSMW), name='glu_fwd', compiler_params=_params(('parallel',), 32))(ypre, wglu, bglu, gs)


def _glu_bwd(ypre, dsn, wglu, bglu, gs):
    def body(y_ref, d_ref, w_ref, b_ref, g_ref, dy_ref, dw_ref, db_ref, dg_ref):
        @pl.when(pl.program_id(0) == 0)
        def _():
            dw_ref[...] = jnp.zeros_like(dw_ref)
            db_ref[...] = jnp.zeros_like(db_ref)
            dg_ref[...] = jnp.zeros_like(dg_ref)

        xv = y_ref[...]
        yg, t = _gelu_parts(xv)
        yg_b = yg.astype(BF16)
        z = jnp.dot(yg_b, w_ref[...], preferred_element_type=F32) + b_ref[...]
        sg = jax.nn.sigmoid(z)
        s = yg * sg
        r = _rms(s)
        n = s * r
        dv = d_ref[:, d_ref.shape[1] - SSMW:]
        dg_ref[...] += _colsum(dv * n)
        ds = _rms_bwd(dv * g_ref[...], n, r)
        dz =(ds * yg) * (sg * (1.0 - sg))
        dz_b = dz.astype(BF16)
        db_ref[...] += _colsum(dz)
        dw_ref[...] += lax.dot_general(yg_b, dz_b, TN_DIMS, preferred_element_type=F32)
        dyg = ds * sg + lax.dot_general(dz_b, w_ref[...], NT_DIMS, preferred_element_type=F32)
        dgelu = 0.5 * (1.0 + t) + (0.5 * xv) * (1.0 - t * t) * (GELU_C * (1.0 + 3.0 * GELU_K * (xv * xv)))
        dy_ref[...] = dyg * dgelu
        _fold8(pl.program_id(0) == L // TR - 1, db_ref, dg_ref)

    vs = jax.ShapeDtypeStruct((8, SSMW), F32)
    return pl.pallas_call(
        body, out_shape=(jax.ShapeDtypeStruct((L, SSMW), F32), jax.ShapeDtypeStruct((SSMW, SSMW), F32), vs, vs),
        grid=(L // TR,),
        in_specs=[_rowspec(SSMW), _rowspec(dsn.shape[1]), pl.BlockSpec((SSMW, SSMW), lambda i: (0, 0)), _vec(SSMW),
                  _vec(SSMW)],
        out_specs=(_rowspec(SSMW), pl.BlockSpec((SSMW, SSMW), lambda i: (0, 0)), _vec8(SSMW), _vec8(SSMW)),
        name='glu_bwd', compiler_params=_params(('arbitrary',), 40))(ypre, dsn, wglu, bglu, gs)


def _me():
    return lax.axis_index('x'), lax.axis_index('y'), lax.axis_index('c')


def _my_index():
    return 4 * lax.axis_index('x') + 2 * lax.axis_index('y') + lax.axis_index('c')


def _peer(k):
    x, y, c = _me()
    px = 1 - x if k & 4 else x
    py = 1 - y if k & 2 else y
    pc = 1 - c if k & 1 else c
    return (px, py, pc), 4 * px + 2 * py + pc


def _mod_exchange(c_row, w_ada, b_ada8, deps=()):
    cw = NMOD * D // N_DEV

    def body(c_ref, w_ref, b_ref, *rest):
        call_ref, mod_ref, part_ref, send_sems, recv_sems = rest[len(deps):]
        x, y, c = _me()
        me = 4 * x + 2 * y + c
        call_ref[me] = c_ref[0]
        sends = []
        for k in range(1, N_DEV):
            peer, _ = _peer(k)
            cp = pltpu.make_async_remote_copy(src_ref=c_ref.at[0], dst_ref=call_ref.at[me], send_sem=send_sems.at[0, k - 1],
                                              recv_sem=recv_sems.at[0, k - 1], device_id=peer, device_id_type=MESH)
            cp.start()
            sends.append(cp)
        for k in range(1, N_DEV):
            peer, pidx = _peer(k)
            pltpu.make_async_remote_copy(src_ref=c_ref.at[0], dst_ref=call_ref.at[pidx], send_sem=send_sems.at[0, k - 1],
                                         recv_sem=recv_sems.at[0, k - 1], device_id=peer, device_id_type=MESH).wait_recv()
        for cp in sends:
            cp.wait_send()
        cv = call_ref[...].reshape(N_DEV, D)
        part = jnp.dot(cv * jax.nn.sigmoid(cv), w_ref[...], preferred_element_type=F32, precision=HI)
        part_ref[...] = part.reshape(N_DEV, 1, cw)
        mod_ref[me] = part_ref[me]
        sends = []
        for k in range(1, N_DEV):
            peer, pidx = _peer(k)
            cp = pltpu.make_async_remote_copy(src_ref=part_ref.at[pidx], dst_ref=mod_ref.at[me], send_sem=send_sems.at[1, k - 1],
                                              recv_sem=recv_sems.at[1, k - 1], device_id=peer, device_id_type=MESH)
            cp.start()
            sends.append(cp)
        for k in range(1, N_DEV):
            peer, pidx = _peer(k)
            pltpu.make_async_remote_copy(src_ref=part_ref.at[pidx], dst_ref=mod_ref.at[pidx], send_sem=send_sems.at[1, k - 1],
                                         recv_sem=recv_sems.at[1, k - 1], device_id=peer, device_id_type=MESH).wait_recv()
        for cp in sends:
            cp.wait_send()
        mod_ref[...] = mod_ref[...] + b_ref[...]

    vm = pl.BlockSpec(memory_space=pltpu.VMEM)
    return pl.pallas_call(
        body, out_shape=(jax.ShapeDtypeStruct((N_DEV, 1, D), F32), jax.ShapeDtypeStruct((N_DEV, 1, cw), F32)),
        in_specs=[vm, vm, vm] + [pl.BlockSpec(memory_space=pl.ANY)] * len(deps), out_specs=(vm, vm),
        scratch_shapes=[pltpu.VMEM((N_DEV, 1, cw), F32), pltpu.SemaphoreType.DMA((2, N_DEV - 1)),
                        pltpu.SemaphoreType.DMA((2, N_DEV - 1))],
        name='mod_exchange', compiler_params=_params(None, 48))(c_row, w_ada, b_ada8, *deps)


HBM_SPEC = pl.BlockSpec(memory_space=pltpu.HBM)
SEM_SPEC = pl.BlockSpec(memory_space=pltpu.SEMAPHORE)
DATAFLOW = pltpu.SideEffectType.DATAFLOW_SIDE_EFFECTING


def _push_start(src, scatter, after, name):
    land = lax.empty(src.shape if scatter else (N_DEV,) + src.shape, src.dtype)

    def body(src_ref, land_ref, after_ref, send_sem, recv_sem, land_thru, token):
        x, y, c = _me()
        me = 4 * x + 2 * y + c
        for k in range(1, N_DEV):
            peer, pidx = _peer(k)
            pltpu.make_async_remote_copy(src_ref=src_ref.at[pidx] if scatter else src_ref, dst_ref=land_ref.at[me],
                                         send_sem=send_sem, recv_sem=recv_sem, device_id=peer,
                                         device_id_type=MESH).start()
        token[...] = jnp.zeros_like(token)

    own = lax.dynamic_index_in_dim(src, _my_index(), 0, keepdims=False) if scatter else src
    src = pltpu.with_memory_space_constraint(src, pltpu.HBM)
    send_sem, recv_sem, land_thru, token = pl.pallas_call(
        body, name=name,
        out_shape=(pltpu.SemaphoreType.DMA(()), pltpu.SemaphoreType.DMA(()),
                   pltpu.HBM(land.shape, land.dtype), jax.ShapeDtypeStruct((8, 128), F32)),
        in_specs=(HBM_SPEC, HBM_SPEC, pl.BlockSpec(memory_space=pl.ANY)),
        out_specs=(SEM_SPEC, SEM_SPEC, HBM_SPEC, pl.BlockSpec(memory_space=pltpu.VMEM)),
        input_output_aliases={1: 2}, compiler_params=pltpu.CompilerParams(has_side_effects=DATAFLOW),
    )(src, pltpu.with_memory_space_constraint(land, pltpu.HBM), after)
    return send_sem, recv_sem, src, land_thru, token, own


def _push_wait(handle, after, name, merge=True):
    send_sem, recv_sem, src, land_thru, _, own = handle
    after = tuple(after) if isinstance(after, (tuple, list)) else (after,)

    def body(src_ref, land_ref, send_sem, recv_sem, *rest):
        seven = land_ref.at[pl.ds(0, N_DEV - 1)]
        cp = pltpu.make_async_remote_copy(src_ref=seven, dst_ref=seven, send_sem=send_sem, recv_sem=recv_sem,
                                          device_id=_me(), device_id_type=MESH)
        cp.wait_send()
        cp.wait_recv()

    landed = pl.pallas_call(
        body, name=name, out_shape=pltpu.HBM(land_thru.shape, land_thru.dtype),
        in_specs=(HBM_SPEC, HBM_SPEC, SEM_SPEC, SEM_SPEC) + (pl.BlockSpec(memory_space=pl.ANY),) * len(after),
        out_specs=HBM_SPEC, input_output_aliases={1: 0},
        compiler_params=pltpu.CompilerParams(has_side_effects=DATAFLOW),
    )(src, land_thru, send_sem, recv_sem, *after)
    if not merge:
        return landed, own
    return lax.dynamic_update_index_in_dim(landed, own, _my_index(), 0)


def _adam(w, g, m, v):
    m2 = B1 * m + (1.0 - B1) * g
    v2 = B2 * v + (1.0 - B2) * jnp.square(g)
    m_hat = m2 / (1.0 - B1 ** STEP)
    v_hat = v2 / (1.0 - B2 ** STEP)
    delta = -LR * (m_hat / (jnp.sqrt(v_hat) + AEPS) + WD * w)
    return delta, m2, v2


def _small_update(gp, wp, mp, vp):
    def body(g_ref, w_ref, m_ref, v_ref, all_ref, go_ref, d_ref, mo_ref, vo_ref, send_sems, recv_sems):
        x, y, c = _me()
        me = 4 * x + 2 * y + c
        all_ref[me] = g_ref[...]
        sends = []
        for k in range(1, N_DEV):
            peer, _ = _peer(k)
            cp = pltpu.make_async_remote_copy(src_ref=g_ref, dst_ref=all_ref.at[me], send_sem=send_sems.at[k - 1],
                                              recv_sem=recv_sems.at[k - 1], device_id=peer, device_id_type=MESH)
            cp.start()
            sends.append(cp)
        for k in range(1, N_DEV):
            peer, pidx = _peer(k)
            pltpu.make_async_remote_copy(src_ref=g_ref, dst_ref=all_ref.at[pidx], send_sem=send_sems.at[k - 1],
                                         recv_sem=recv_sems.at[k - 1], device_id=peer, device_id_type=MESH).wait_recv()
        for cp in sends:
            cp.wait_send()
        g = all_ref[0]
        for d in range(1, N_DEV):
            g = g + all_ref[d]
        delta, m2, v2 = _adam(w_ref[...], g, m_ref[...], v_ref[...])
        go_ref[...] = g
        d_ref[...] = delta
        mo_ref[...] = m2
        vo_ref[...] = v2

    vm = pl.BlockSpec(memory_space=pltpu.VMEM)
    vs = jax.ShapeDtypeStruct(gp.shape, F32)
    return pl.pallas_call(
        body, out_shape=(jax.ShapeDtypeStruct((N_DEV,) + gp.shape, F32), vs, vs, vs, vs), in_specs=[vm] * 4,
        out_specs=(vm,) * 5,
        scratch_shapes=[pltpu.SemaphoreType.DMA((N_DEV - 1,)), pltpu.SemaphoreType.DMA((N_DEV - 1,))],
        name='small_update', compiler_params=_params(None, 48))(gp, wp, mp, vp)


def _small_sum_update(parts, wp, mp, vp):
    def body(p_ref, w_ref, m_ref, v_ref, go_ref, d_ref, mo_ref, vo_ref):
        g = p_ref[0]
        for d in range(1, N_DEV):
            g = g + p_ref[d]
        delta, m2, v2 = _adam(w_ref[...], g, m_ref[...], v_ref[...])
        go_ref[...] = g
        d_ref[...] = delta
        mo_ref[...] = m2
        vo_ref[...] = v2

    vm = pl.BlockSpec(memory_space=pltpu.VMEM)
    vs = jax.ShapeDtypeStruct(wp.shape, F32)
    return pl.pallas_call(body, out_shape=(vs, vs, vs, vs), in_specs=[vm] * 4, out_specs=(vm,) * 4,
                          name='small_sum_update', compiler_params=_params(None, 48))(parts, wp, mp, vp)


def _big_update(parts, own, w, m, v, name):
    _, R, C = parts.shape
    tr = R if R % 256 else (128 if C >= 2048 else 256)

    def body(me_ref, p_ref, own_ref, w_ref, m_ref, v_ref, g_ref, d_ref, mo_ref, vo_ref):
        me = me_ref[0]
        mine = own_ref[...].astype(F32)
        g = jnp.where(me == 0, mine, p_ref[0].astype(F32))
        for d in range(1, N_DEV):
            g = g + jnp.where(me == d, mine, p_ref[d].astype(F32))
        delta, m2, v2 = _adam(w_ref[...], g, m_ref[...], v_ref[...])
        g_ref[...] = g
        d_ref[...] = delta
        mo_ref[...] = m2
        vo_ref[...] = v2

    blk = pl.BlockSpec((tr, C), lambda i: (i, 0))
    sh = jax.ShapeDtypeStruct((R, C), F32)
    return pl.pallas_call(
        body, out_shape=(sh, sh, sh, sh), grid=(R // tr,),
        in_specs=[pl.BlockSpec(memory_space=pltpu.SMEM), pl.BlockSpec((N_DEV, tr, C), lambda i: (0, i, 0)), blk, blk,
                  blk, blk], out_specs=(blk,) * 4,
        name=name, compiler_params=_params(('parallel',), 48))(_my_index().reshape(1), parts, own, w, m, v)


def _ada_update(c_all, dmod_cols, w, m, v):
    C = w.shape[1]
    tr = 256

    def body(c_ref, dm_ref, w_ref, m_ref, v_ref, g_ref, d_ref, mo_ref, vo_ref):
        cv = c_ref[...]
        s = cv * jax.nn.sigmoid(cv)
        g = lax.dot_general(s, dm_ref[...], TN_DIMS, preferred_element_type=F32, precision=HI)
        delta, m2, v2 = _adam(w_ref[...], g, m_ref[...], v_ref[...])
        g_ref[...] = g
        d_ref[...] = delta
        mo_ref[...] = m2
        vo_ref[...] = v2

    blk = pl.BlockSpec((tr, C), lambda i: (i, 0))
    sh = jax.ShapeDtypeStruct((D, C), F32)
    return pl.pallas_call(
        body, out_shape=(sh, sh, sh, sh), grid=(D // tr,),
        in_specs=[pl.BlockSpec((N_DEV, tr), lambda i: (0, i)), pl.BlockSpec((N_DEV, C), lambda i: (0, 0)), blk, blk, blk],
        out_specs=(blk,) * 4, name='ada_update', compiler_params=_params(('parallel',), 48))(c_all, dmod_cols, w, m, v)


def _to_sub(t, d):
    if d == 1:
        return t
    return t.reshape(L // d, d, t.shape[-1]).transpose(1, 0, 2).reshape(L, t.shape[-1])


def _from_sub(t, d):
    if d == 1:
        return t
    return t.reshape(d, L // d, t.shape[-1]).transpose(1, 0, 2).reshape(L, t.shape[-1])


def _rows_to_cluster_lanes(t):
    k = t.shape[1]
    return t.reshape(N_CL, CL_S, k).transpose(0, 2, 1)


def _blockdiag_in(t):
    t = t.reshape(N_CL, CL_G, SSM_N, SSM_P).transpose(0, 1, 3, 2)
    eye = jnp.eye(CL_G, dtype=t.dtype)
    t = t[:, :, :, None, :] * eye[None, :, None, :, None]
    return t.reshape(N_CL, CL_U, CL_S)


def _blockdiag_extract(t):
    t = t.reshape(N_CL, CL_G, SSM_P, CL_G, SSM_N)
    t = jnp.stack([t[:, i, :, i, :] for i in range(CL_G)], axis=1)
    return t.transpose(0, 1, 3, 2).reshape(SSM_GN, SSM_P)


def _c_to_rows(t):
    return t.transpose(0, 2, 1).reshape(SSM_GN, SSM_P)


def _rows_to_c(t):
    return t.reshape(SSM_G, SSM_N, SSM_P).transpose(0, 2, 1)


def _ssm_prep(sp):
    rows = lambda n: sp[n].reshape(SSM_GN, 1)
    a_re, a_im = rows('ssm_a_re'), rows('ssm_a_im')
    ldt = jnp.repeat(sp['ssm_log_dt'].reshape(SSM_G, 1), SSM_N, axis=0)
    b_re, b_im = sp['ssm_b_re'].reshape(SSM_GN, SSM_P), sp['ssm_b_im'].reshape(SSM_GN, SSM_P)
    c_re, c_im = _c_to_rows(sp['ssm_c_re'].reshape(SSM_G, SSM_P, SSM_N)), _c_to_rows(sp['ssm_c_im'].reshape(SSM_G, SSM_P, SSM_N))
    bbr, bbi, pwr, pwi = _ssm_pre(a_re, a_im, ldt, b_re, b_im)
    bm = jnp.concatenate([_blockdiag_in(bbr), _blockdiag_in(bbi)], axis=2)
    cmt = jnp.concatenate([_blockdiag_in(c_re), -_blockdiag_in(c_im)], axis=2)
    bmt, cm = bm.transpose(0, 2, 1), cmt.transpose(0, 2, 1)
    pw = jnp.concatenate([_rows_to_cluster_lanes(pwr), _rows_to_cluster_lanes(pwi)], axis=2)
    bm2, cm2 = _hilo(bm), _hilo(cm)
    bmt_b, cmt_b = bmt.astype(BF16), cmt.astype(BF16)
    return a_re, a_im, ldt, b_re, b_im, bm2, cm2, bmt_b, cmt_b, pw


def _tied(v, deps):
    for t in deps:
        v = v + t[0, 0]
    return v


def _local_step(x, pos, mod, tgt, sp, prep, get_w, emit, emit_small, emit_late, first_deps=()):
    sh1, sc1, gt1, sh2, sc2, gt2 = (mod[i:i + 1] for i in range(NMOD))
    vec = lambda n: sp[n].reshape(1, -1)
    a_re, a_im, ldt, b_re, b_im, bm2, cm2, bmt_b, cmt_b, pw = prep
    dvec = vec('ssm_d')

    h1 = _prenorm_fwd(x, vec('g_pre_mix'), sc1, sh1)
    w_in = get_w('w_in', (h1, bm2, cm2, pw, bmt_b, cmt_b))
    proj = _mm(h1, w_in, mode='nn', name='mm_in', tn=1408, deps=first_deps)
    fr1 =ROPE_THETA ** (-jnp.arange(0, ROT_DIM, 2, dtype=F32) / ROT_DIM)
    lane = jnp.arange(128) % HEAD_DIM
    fr = jnp.where(lane < ROT_DIM, fr1[lane % (ROT_DIM // 2)], 0.0).reshape(1, 128).astype(F32)
    u_off = (ROPE_W + KVW) // CL_U
    qkvs = _rope_fwd(proj, pos, fr)
    fwd = [_attn_fwd(qkvs[g], d) for g, d in enumerate(DILATIONS)]
    os_, lses = [t[0] for t in fwd], [t[1] for t in fwd]
    att = _combine_fwd(os_, lses, vec('g_attn_out'))

    ypre, bnd, states = _ssm_fwd(proj, bm2, cm2, pw, dvec, u_off)
    w_glu = get_w('w_glu', ypre)
    ssm_n = _glu_fwd(ypre, w_glu, vec('b_glu'), vec('g_ssm_out'))

    cat = jnp.concatenate([att, ssm_n], axis=1)
    w_out = get_w('w_out', cat)
    mix = _mm(cat, w_out, mode='nn', name='mm_out', tk=1280)
    x1, h2 = _postmix_fwd(x, mix, vec('g_post_mix'), gt1, vec('g_pre_mlp'), sc2, sh2)
    w_mi = get_w('w_mlp_in', h2)
    a_pre, r_act = _mm(h2, w_mi, mode='nn', name='mm_mlp_in', epilogue='relu2', b_sharded=True)
    w_mo = get_w('w_mlp_out', a_pre)
    y = _mm(r_act, w_mo, mode='nn', name='mm_mlp_out')
    dx2, dy, loss, dgt2, dg_post_mlp = _final_fwd_bwd(x1, y, tgt, vec('g_post_mlp'), gt2)
    dgt2, dg_post_mlp = dgt2[:1], dg_post_mlp[:1]

    da = _mm(dy, w_mo, mode='nt', name='mm_d_act', out_dtype=BF16, epilogue='drelu2', extra=a_pre)
    dep = emit('w_mlp_out', _mm(r_act, dy, mode='tn', name='mm_dw_mlp_out', out_dtype=BF16))
    dh2 = _mm(da, w_mi, mode='nt', name='mm_dh2', tn=2048, tk=1024, b_sharded=True, deps=dep)
    dep = emit('w_mlp_in', _mm(h2, da, mode='tn', name='mm_dw_mlp_in', out_dtype=BF16, out_sharded=True))
    dx1, dmix, dsc2, dsh2, dg_pre_mlp, dgt1, dg_post_mix = _postmix_bwd(
        dx2, dh2, x1, mix, vec('g_post_mix'), gt1, vec('g_pre_mlp'), sc2)
    dsc2, dsh2, dg_pre_mlp, dgt1, dg_post_mix = (t[:1] for t in (dsc2, dsh2, dg_pre_mlp, dgt1, dg_post_mix))
    dcat = _mm(dmix, w_out, mode='nt', name='mm_dcat', tn=1280, deps=dep)
    dep = emit('w_out', _mm(cat, dmix, mode='tn', name='mm_dw_out', out_dtype=BF16, tm=640))
    dypre, g_w_glu, g_b_glu, g_g_ssm = _glu_bwd(ypre, dcat, w_glu, _tied(vec('b_glu'), dep), vec('g_ssm_out'))
    g_b_glu, g_g_ssm = g_b_glu[:1], g_g_ssm[:1]
    dep = dep + emit('w_glu', g_w_glu.astype(BF16))
    du, dbm, dcm, dA, dD = _ssm_bwd(proj, dypre, states, bmt_b, cmt_b, pw, dvec, bnd, u_off)
    dD = dD[:1]
    gbr, gbi = _blockdiag_extract(dbm[:, :, :CL_S]), _blockdiag_extract(dbm[:, :, CL_S:])
    dcmt = dcm.transpose(0, 2, 1)
    g_c_re = _rows_to_c(_blockdiag_extract(dcmt[:, :, :CL_S]))
    g_c_im = _rows_to_c(-_blockdiag_extract(dcmt[:, :, CL_S:]))
    gar = dA[:, 0, :CL_S].reshape(SSM_GN, 1)
    gai = dA[:, 0, CL_S:].reshape(SSM_GN, 1)
    sel = (jnp.arange(SSM_GN)[None, :] // SSM_N == jnp.arange(SSM_G)[:, None]).astype(F32)
    g_a_re, g_a_im, g_b_re, g_b_im, g_ldt = _ssm_post(a_re, a_im, ldt, b_re, b_im, gar, gai, gbr, gbi, sel)

    head_ones = (jnp.arange(KVW)[:, None] // HEAD_DIM == jnp.arange(KVW)[None, :] // HEAD_DIM).astype(F32)
    dos, dlses, g_g_attn = _combine_bwd(dcat, os_, lses, vec('g_attn_out'), head_ones)
    g_g_attn = g_g_attn[:1]
    dep_small = emit_small({
        'g_post_mix': dg_post_mix, 'ssm_a_re': g_a_re, 'ssm_a_im': g_a_im, 'ssm_log_dt': g_ldt[:, 0],
        'ssm_b_re': g_b_re, 'ssm_b_im': g_b_im, 'ssm_c_re': g_c_re, 'ssm_c_im': g_c_im, 'ssm_d': dD, 'b_glu': g_b_glu,
        'g_attn_out': g_g_attn, 'g_ssm_out': g_g_ssm, 'g_pre_mlp': dg_pre_mlp, 'g_post_mlp': dg_post_mlp})
    dqkv = [_attn_bwd(qkvs[g], os_[g], lses[g], dos[g], dlses[g], d) for g, d in enumerate(DILATIONS)]
    dproj = _rope_bwd(dqkv, du, pos, _tied(fr, dep_small))
    dh1 = _mm(dproj, w_in, mode='nt', name='mm_dh1', tk=1408, deps=dep)
    grad_x, dsc1, dsh1, dg_pre_mix = _prenorm_bwd(dx1, dh1, x, vec('g_pre_mix'), sc1)
    dsc1, dsh1, dg_pre_mix = dsc1[:1], dsh1[:1], dg_pre_mix[:1]
    dmod = jnp.concatenate([dsh1, dsc1, dgt1, dsh2, dsc2, dgt2], axis=0)
    dep = emit_late({'b_ada': dmod, 'g_pre_mix': dg_pre_mix})
    emit('w_in', _mm(h1, dproj, mode='tn', name='mm_dw_in', out_dtype=BF16, tn=1408, deps=dep))
    return loss[0, 0], grad_x


def _pack(d, names):
    flat = jnp.concatenate([jnp.pad(d[n].reshape(-1).astype(F32), (0, SEG[n] - SMALL_SIZES[n])) for n in names])
    return flat.reshape(-1, 128)


def _unpack(packed, names, shapes):
    out, off = {}, 0
    for n in names:
        out[n] = packed[off // 128:(off + SEG[n]) // 128].reshape(-1)[:SMALL_SIZES[n]].reshape(shapes[n])
        off += SEG[n]
    return out


def _shard_major(t, name):
    if name in ('w_in', 'w_out', 'w_mlp_in'):
        k, n = t.shape
        return t.reshape(k, N_DEV, n // N_DEV).transpose(1, 0, 2)
    k, n = t.shape
    return t.reshape(N_DEV, k // N_DEV, n)


def _from_shard_major(t, name):
    if name in ('w_in', 'w_out', 'w_mlp_in'):
        _, k, n = t.shape
        return t.transpose(1, 0, 2).reshape(k, N_DEV * n)
    _, k, n = t.shape
    return t.reshape(N_DEV * k, n)


def kernel(x, c, positions, w_ada, b_ada, g_pre_mix, g_post_mix, w_in, ssm_a_re, ssm_a_im, ssm_log_dt, ssm_b_re, ssm_b_im, ssm_c_re, ssm_c_im, ssm_d, w_glu, b_glu, g_attn_out, g_ssm_out, w_out, g_pre_mlp, g_post_mlp, w_mlp_in, w_mlp_out, loss_target, m_w_ada, m_b_ada, m_g_pre_mix, m_g_post_mix, m_w_in, m_ssm_a_re, m_ssm_a_im, m_ssm_log_dt, m_ssm_b_re, m_ssm_b_im, m_ssm_c_re, m_ssm_c_im, m_ssm_d, m_w_glu, m_b_glu, m_g_attn_out, m_g_ssm_out, m_w_out, m_g_pre_mlp, m_g_post_mlp, m_w_mlp_in, m_w_mlp_out, v_w_ada, v_b_ada, v_g_pre_mix, v_g_post_mix, v_w_in, v_ssm_a_re, v_ssm_a_im, v_ssm_log_dt, v_ssm_b_re, v_ssm_b_im, v_ssm_c_re, v_ssm_c_im, v_ssm_d, v_w_glu, v_b_glu, v_g_attn_out, v_g_ssm_out, v_w_out, v_g_pre_mlp, v_g_post_mlp, v_w_mlp_in, v_w_mlp_out):
    loc = dict(locals())
    W = {n: loc[n] for n in WEIGHTS}
    M = {n: loc['m_' + n] for n in WEIGHTS}
    V = {n: loc['v_' + n] for n in WEIGHTS}
    assert x.shape == (1, L, D) and w_in.shape == (1, D, INW // N_DEV), (x.shape, w_in.shape)

    cw = NMOD * D // N_DEV
    c_all, mod8 = _mod_exchange(c.reshape(1, 1, D), w_ada[0], b_ada.reshape(N_DEV, 1, cw))
    mod = mod8.reshape(NMOD, D)

    gather, after = {}, mod8
    for n in BIG:
        gather[n] = _push_start(W[n][0].astype(BF16), False, after, 'gather_start_' + n)
        after = gather[n][4]
    tokens = tuple(gather[n][4] for n in BIG)
    mod = _tied(mod, tokens)
    sp = {n: W[n][0] for n in SMALL}
    prep = _ssm_prep({**sp, 'ssm_a_re': _tied(sp['ssm_a_re'], tokens)})

    state_packs = {}

    def get_w(n, after):
        if n == 'w_mlp_out':
            tok = jnp.minimum(jnp.abs(after[0:8, 0:128].astype(F32)), 0.0)
            for key, src in (('w', W), ('m', M), ('v', V)):
                tied = {p: _tied(src[p], (tok,)) for p in SMALL}
                state_packs[key] = {'early': _pack(tied, SMALL_EARLY), 'late': _pack(tied, SMALL_LATE)}
            after = (after,) + tuple(state_packs[k][part] for k in 'wmv' for part in ('early', 'late'))
        g = _push_wait(gather[n], after, 'gather_wait_' + n)
        return g if n == 'w_mlp_in' else _from_shard_major(g, n)

    scatter = {}

    def emit(n, g):
        src = g if n == 'w_mlp_in' else _shard_major(g, n)
        scatter[n] = _push_start(src, True, src, 'scatter_start_' + n)
        return (scatter[n][4],)

    small_early = []

    def emit_small(d):
        pack = _pack(d, SMALL_EARLY)
        small_early.append(_push_start(pack, False, pack, 'small_start'))
        return (small_early[0][4],)

    out_g, out_d, out_m, out_v = {}, {}, {}, {}
    shapes = {n: W[n].shape[1:] for n in SMALL}

    def put(names, packs):
        for dst, packed in zip((out_g, out_d, out_m, out_v), packs):
            dst.update(_unpack(packed, names, shapes))

    late = []

    def emit_late(d):
        rows_all, *packs = _small_update(_pack(d, SMALL_LATE), *[state_packs[k]['late'] for k in 'wmv'])
        put(SMALL_LATE, packs)
        late.append(rows_all)
        return (rows_all,)

    loss, grad_x = _local_step(x[0], positions.reshape(L, 1), mod, loss_target[0], sp, prep, get_w, emit, emit_small,
                               emit_late)
    loss = lax.psum(loss, ('x', 'y', 'c'))

    me = 4 * lax.axis_index('x') + 2 * lax.axis_index('y') + lax.axis_index('c')
    dmod_all = late[0][:, :NMOD * D // 128].reshape(N_DEV, NMOD * D)
    dmod_cols = _tied(lax.dynamic_slice_in_dim(dmod_all, me * cw, cw, axis=1), (scatter['w_in'][4],))
    out_g['w_ada'], out_d['w_ada'], out_m['w_ada'], out_v['w_ada'] = _ada_update(
        c_all.reshape(N_DEV, D), dmod_cols, w_ada[0], m_w_ada[0], v_w_ada[0])

    parts = _push_wait(small_early[0], out_v['w_ada'], 'small_wait')
    packs = _small_sum_update(parts, *[state_packs[k]['early'] for k in 'wmv'])
    put(SMALL_EARLY, packs)

    after = packs[3]
    for n in ('w_mlp_out', 'w_mlp_in', 'w_out', 'w_glu', 'w_in'):
        parts, own = _push_wait(scatter[n], after, 'scatter_wait_' + n, merge=False)
        out_g[n], out_d[n], out_m[n], out_v[n] = _big_update(parts, own, W[n][0], M[n][0], V[n][0], 'update_' + n)
        after = out_v[n]

    lead = lambda t: t[None]
    return (loss, grad_x[None], *[lead(out_g[n]) for n in WEIGHTS], *[lead(out_d[n]) for n in WEIGHTS],
            *[lead(out_m[n]) for n in WEIGHTS], *[lead(out_v[n]) for n in WEIGHTS])
```

```python
import functools
import math

import jax
import jax.numpy as jnp
from jax import lax
from jax.experimental import pallas as pl
from jax.experimental.pallas import tpu as pltpu

F32 = jnp.float32
BF16 = jnp.bfloat16
HI = lax.Precision.HIGHEST
MESH = pl.DeviceIdType.MESH

N_DEV = 8
L = 4096
D = 2048
HEAD_DIM = 64
N_GROUPS = 3
DILATIONS = (1, 4, 16)
HEADS = 6
QW = N_GROUPS * HEADS * HEAD_DIM
KVW = HEADS * HEAD_DIM
ROT_DIM = 16
ROPE_THETA = 500000.0
BLK = 128
NBLK = L // BLK
SSMW = D - QW
SSM_P = 16
SSM_G = SSMW // SSM_P
SSM_N = 64
SSM_GN = SSM_G * SSM_N
CL_G = 8
N_CL = SSM_G // CL_G
CL_U = CL_G * SSM_P
CL_S = CL_G * SSM_N
INW = QW + 2 * KVW + SSMW
OUTW = KVW + SSMW
DFF = 4 * D
NMOD = 6
EPS = 1e-6
LR, B1, B2, AEPS, WD, STEP = 0.001, 0.9, 0.999, 1e-08, 0.01, 10

T_SCAN = 1024
MB = 2 ** 20

WEIGHTS = ['w_ada', 'b_ada', 'g_pre_mix', 'g_post_mix', 'w_in', 'ssm_a_re', 'ssm_a_im', 'ssm_log_dt',
           'ssm_b_re', 'ssm_b_im', 'ssm_c_re', 'ssm_c_im', 'ssm_d', 'w_glu', 'b_glu', 'g_attn_out',
           'g_ssm_out', 'w_out', 'g_pre_mlp', 'g_post_mlp', 'w_mlp_in', 'w_mlp_out']
BIG = ['w_in', 'w_glu', 'w_out', 'w_mlp_in', 'w_mlp_out']
SMALL = [n for n in WEIGHTS if n not in BIG and n != 'w_ada']
SMALL_SIZES = {'b_ada': NMOD * D, 'g_pre_mix': D, 'g_post_mix': D, 'ssm_a_re': SSM_GN, 'ssm_a_im': SSM_GN,
               'ssm_log_dt': SSM_G, 'ssm_b_re': SSM_GN * SSM_P, 'ssm_b_im': SSM_GN * SSM_P,
               'ssm_c_re': SSM_GN * SSM_P, 'ssm_c_im': SSM_GN * SSM_P, 'ssm_d': SSMW, 'b_glu': SSMW,
               'g_attn_out': KVW, 'g_ssm_out': SSMW, 'g_pre_mlp': D, 'g_post_mlp': D}
SEG = {n: -(-SMALL_SIZES[n] // 1024) * 1024 for n in SMALL}
SMALL_LATE = ['b_ada', 'g_pre_mix']
SMALL_EARLY = [n for n in SMALL if n not in SMALL_LATE]


def _params(sem=None, vmem_mb=None):
    kw = {}
    if sem is not None:
        kw['dimension_semantics'] = sem
    if vmem_mb is not None:
        kw['vmem_limit_bytes'] = vmem_mb * MB
    return pltpu.CompilerParams(**kw)


def _vec(n):
    return pl.BlockSpec((1, n), lambda *_: (0, 0))


def _rms(x):
    return lax.rsqrt(jnp.mean(x * x, axis=-1, keepdims=True) + EPS)


def _rms_bwd(dn, n, r):
    return r * (dn - n * jnp.mean(dn * n, axis=-1, keepdims=True))


def _vec8(n):
    return pl.BlockSpec((8, n), lambda *_: (0, 0))


def _colsum(x):
    return jnp.sum(x.reshape(-1, 8, x.shape[-1]), axis=0)


def _fold8(last, *refs):
    @pl.when(last)
    def _():
        for r in refs:
            r[...] = jnp.broadcast_to(jnp.sum(r[...], axis=0, keepdims=True), r.shape)


def _mm(a, b, *, mode, name, out_dtype=F32, tm=1024, tn=1024, tk=2048, epilogue=None, extra=None,
        b_sharded=False, out_sharded=False, deps=()):
    if mode == 'nn':
        M, K = a.shape
        dims = (((1,), (0,)), ((), ()))
        a_spec = pl.BlockSpec((tm, tk), lambda i, j, k: (i, k))
        if b_sharded:
            _, K2, per = b.shape
            N, q = N_DEV * per, per // tn
            b_spec = pl.BlockSpec((None, tk, tn), lambda i, j, k: (j // q, k, j % q))
        else:
            K2, N = b.shape
            b_spec = pl.BlockSpec((tk, tn), lambda i, j, k: (k, j))
    elif mode == 'nt':
        M, K = a.shape
        dims = (((1,), (1,)), ((), ()))
        a_spec = pl.BlockSpec((tm, tk), lambda i, j, k: (i, k))
        if b_sharded:
            _, N, per = b.shape
            K2 = N_DEV * per
            if tk > per:
                b_spec = pl.BlockSpec((tk // per, tn, per), lambda i, j, k: (k, j, 0))
            else:
                q = per // tk
                b_spec = pl.BlockSpec((None, tn, tk), lambda i, j, k: (k // q, j, k % q))
        else:
            N, K2 = b.shape
            b_spec = pl.BlockSpec((tn, tk), lambda i, j, k: (j, k))
    else:
        (K, M), (K2, N) = a.shape, b.shape
        dims = (((0,), (0,)), ((), ()))
        a_spec = pl.BlockSpec((tk, tm), lambda i, j, k: (k, i))
        b_spec = pl.BlockSpec((tk, tn), lambda i, j, k: (k, j))
    assert K == K2 and M % tm == 0 and N % tn == 0 and K % tk == 0, (name, a.shape, b.shape, tm, tn, tk)
    nk = K // tk
    o_spec = pl.BlockSpec((tm, tn), lambda i, j, k: (i, j))
    o_dims = (M, N)
    if out_sharded:
        qo = N // N_DEV // tn
        o_spec = pl.BlockSpec((None, tm, tn), lambda i, j, k: (j // qo, i, j % qo))
        o_dims = (N_DEV, M, N // N_DEV)
    n_out = 2 if epilogue == 'relu2' else 1
    n_extra = 1 if extra is not None else 0
    n_in = 2 + n_extra + len(deps)

    def body(*refs):
        a_ref, b_ref = refs[0], refs[1]
        x_refs = refs[2:2 + n_extra]
        o_refs = refs[n_in:n_in + n_out]
        acc = refs[-1]
        k = pl.program_id(2)

        if len(b_ref.shape) == 3:
            per_ = b_ref.shape[2]
            prod = sum(lax.dot_general(a_ref[:, s * per_:(s + 1) * per_], b_ref[s], dims, preferred_element_type=F32)
                       for s in range(b_ref.shape[0]))
        else:
            prod = lax.dot_general(a_ref[...], b_ref[...], dims, preferred_element_type=F32)

        def finish(r):
            if epilogue == 'relu2':
                o_refs[0][...] = r.astype(BF16)
                o_refs[1][...] = jnp.square(jnp.maximum(r, 0.0)).astype(BF16)
            elif epilogue == 'drelu2':
                pre = x_refs[0][...].astype(F32)
                o_refs[0][...] = (r * (2.0 * jnp.maximum(pre, 0.0))).astype(out_dtype)
            else:
                o_refs[0][...] = r.astype(out_dtype)

        if nk == 1:
            finish(prod)
        else:
            @pl.when(k == 0)
            def _():
                acc[...] = prod

            @pl.when((k > 0) & (k < nk - 1))
            def _():
                acc[...] += prod

            @pl.when(k == nk - 1)
            def _():
                finish(acc[...] + prod)

    if epilogue == 'relu2':
        out_shape = (jax.ShapeDtypeStruct((M, N), BF16), jax.ShapeDtypeStruct((M, N), BF16))
        out_specs = (o_spec, o_spec)
    else:
        out_shape = jax.ShapeDtypeStruct(o_dims, out_dtype)
        out_specs = o_spec
    args = (a, b) + ((extra,) if extra is not None else ()) + tuple(deps)
    in_specs = ([a_spec, b_spec] + ([o_spec] if extra is not None else [])
                + [pl.BlockSpec(memory_space=pl.ANY)] * len(deps))
    return pl.pallas_call(
        body, out_shape=out_shape, grid=(M // tm, N // tn, nk), in_specs=in_specs, out_specs=out_specs,
        scratch_shapes=[pltpu.VMEM((tm, tn) if nk > 1 else (8, 128), F32)], name=name,
        compiler_params=_params(('parallel', 'parallel', 'arbitrary'), 56))(*args)


TR = 256


def _rowspec(w=D):
    return pl.BlockSpec((TR, w), lambda i: (i, 0))


def _prenorm_fwd(x, g, sc, sh):
    def body(x_ref, g_ref, sc_ref, sh_ref, h_ref):
        xv = x_ref[...]
        n = xv * _rms(xv)
        h_ref[...] = ((n * g_ref[...]) * (1.0 + sc_ref[...]) + sh_ref[...]).astype(BF16)

    return pl.pallas_call(
        body, out_shape=jax.ShapeDtypeStruct((L, D), BF16), grid=(L // TR,),
        in_specs=[_rowspec(), _vec(D), _vec(D), _vec(D)], out_specs=_rowspec(), name='prenorm_fwd',
        compiler_params=_params(('parallel',), 40))(x, g, sc, sh)


def _postmix_fwd(x, mix, gpm, gt1, gpl, sc2, sh2):
    def body(x_ref, mix_ref, gpm_ref, gt1_ref, gpl_ref, sc2_ref, sh2_ref, x1_ref, h2_ref):
        mix_v = mix_ref[...]
        nm = mix_v * _rms(mix_v)
        x1 = x_ref[...] + gt1_ref[...] * (nm * gpm_ref[...])
        x1_ref[...] = x1
        n2 = x1 * _rms(x1)
        h2_ref[...] = ((n2 * gpl_ref[...]) * (1.0 + sc2_ref[...]) + sh2_ref[...]).astype(BF16)

    return pl.pallas_call(
        body, out_shape=(jax.ShapeDtypeStruct((L, D), F32), jax.ShapeDtypeStruct((L, D), BF16)), grid=(L // TR,),
        in_specs=[_rowspec(), _rowspec()] + [_vec(D)] * 5, out_specs=(_rowspec(), _rowspec()), name='postmix_fwd',
        compiler_params=_params(('parallel',), 40))(x, mix, gpm, gt1, gpl, sc2, sh2)


def _final_fwd_bwd(x1, y, tgt, g, gt2):
    def body(x1_ref, y_ref, t_ref, g_ref, gt2_ref, dx2_ref, dy_ref, loss_ref, dgt2_ref, dg_ref):
        @pl.when(pl.program_id(0) == 0)
        def _():
            loss_ref[...] = jnp.zeros_like(loss_ref)
            dgt2_ref[...] = jnp.zeros_like(dgt2_ref)
            dg_ref[...] = jnp.zeros_like(dg_ref)

        yv = y_ref[...]
        r = _rms(yv)
        n = yv * r
        ng = n * g_ref[...]
        x2 = x1_ref[...] + gt2_ref[...] * ng
        e = x2 - t_ref[...]
        loss_ref[...] += 0.5 * jnp.sum(jnp.mean(e * e, axis=-1, keepdims=True), axis=0, keepdims=True)
        dx2 = e * (1.0 / D)
        dx2_ref[...] = dx2
        dgt2_ref[...] += _colsum(dx2 * ng)
        dng = dx2 * gt2_ref[...]
        dg_ref[...] += _colsum(dng * n)
        dy_ref[...] = _rms_bwd(dng * g_ref[...], n, r).astype(BF16)
        _fold8(pl.program_id(0) == L // TR - 1, dgt2_ref, dg_ref)

    return pl.pallas_call(
        body,
        out_shape=(jax.ShapeDtypeStruct((L, D), F32), jax.ShapeDtypeStruct((L, D), BF16),
                   jax.ShapeDtypeStruct((8, 128), F32), jax.ShapeDtypeStruct((8, D), F32),
                   jax.ShapeDtypeStruct((8, D), F32)),
        grid=(L // TR,), in_specs=[_rowspec(), _rowspec(), _rowspec(), _vec(D), _vec(D)],
        out_specs=(_rowspec(), _rowspec(), _vec8(128), _vec8(D), _vec8(D)), name='final_fwd_bwd',
        compiler_params=_params(('arbitrary',), 40))(x1, y, tgt, g, gt2)


def _postmix_bwd(dx2, dh2, x1, mix, gpm, gt1, gpl, sc2):
    def body(dx2_ref, dh2_ref, x1_ref, mix_ref, gpm_ref, gt1_ref, gpl_ref, sc2_ref,
             dx1_ref, dmix_ref, dsc2_ref, dsh2_ref, dgpl_ref, dgt1_ref, dgpm_ref):
        @pl.when(pl.program_id(0) == 0)
        def _():
            for r_ in (dsc2_ref, dsh2_ref, dgpl_ref, dgt1_ref, dgpm_ref):
                r_[...] = jnp.zeros_like(r_)

        x1v = x1_ref[...]
        r2 = _rms(x1v)
        n2 = x1v * r2
        dh2v = dh2_ref[...]
        dsh2_ref[...] += _colsum(dh2v)
        dsc2_ref[...] += _colsum(dh2v * (n2 * gpl_ref[...]))
        t = dh2v * (1.0 + sc2_ref[...])
        dgpl_ref[...] += _colsum(t * n2)
        dx1 = dx2_ref[...] + _rms_bwd(t * gpl_ref[...], n2, r2)
        dx1_ref[...] = dx1
        mix_v = mix_ref[...]
        rm = _rms(mix_v)
        nm = mix_v * rm
        dgt1_ref[...] += _colsum(dx1 * (nm * gpm_ref[...]))
        u = dx1 * gt1_ref[...]
        dgpm_ref[...] += _colsum(u * nm)
        dmix_ref[...] = _rms_bwd(u * gpm_ref[...], nm, rm).astype(BF16)
        _fold8(pl.program_id(0) == L // TR - 1, dsc2_ref, dsh2_ref, dgpl_ref, dgt1_ref, dgpm_ref)

    vs = jax.ShapeDtypeStruct((8, D), F32)
    return pl.pallas_call(
        body, out_shape=(jax.ShapeDtypeStruct((L, D), F32), jax.ShapeDtypeStruct((L, D), BF16), vs, vs, vs, vs, vs),
        grid=(L // TR,), in_specs=[_rowspec()] * 4 + [_vec(D)] * 4,
        out_specs=(_rowspec(), _rowspec()) + (_vec8(D),) * 5, name='postmix_bwd',
        compiler_params=_params(('arbitrary',), 48))(dx2, dh2, x1, mix, gpm, gt1, gpl, sc2)


def _prenorm_bwd(dx1, dh1, x, g, sc1):
    def body(dx1_ref, dh1_ref, x_ref, g_ref, sc1_ref, dx_ref, dsc1_ref, dsh1_ref, dg_ref):
        @pl.when(pl.program_id(0) == 0)
        def _():
            for r_ in (dsc1_ref, dsh1_ref, dg_ref):
                r_[...] = jnp.zeros_like(r_)

        xv = x_ref[...]
        r = _rms(xv)
        n = xv * r
        dh = dh1_ref[...]
        dsh1_ref[...] += _colsum(dh)
        dsc1_ref[...] += _colsum(dh * (n * g_ref[...]))
        t = dh * (1.0 + sc1_ref[...])
        dg_ref[...] += _colsum(t * n)
        dx_ref[...] = dx1_ref[...] + _rms_bwd(t * g_ref[...], n, r)
        _fold8(pl.program_id(0) == L // TR - 1, dsc1_ref, dsh1_ref, dg_ref)

    vs = jax.ShapeDtypeStruct((8, D), F32)
    return pl.pallas_call(
        body, out_shape=(jax.ShapeDtypeStruct((L, D), F32), vs, vs, vs), grid=(L // TR,),
        in_specs=[_rowspec()] * 3 + [_vec(D)] * 2, out_specs=(_rowspec(),) + (_vec8(D),) * 3, name='prenorm_bwd',
        compiler_params=_params(('arbitrary',), 40))(dx1, dh1, x, g, sc1)


ROPE_W = QW + KVW
QKV_W = 3 * KVW
NB_KV = KVW // 128


def _rope_rotate(xv, pos, fr, sign):
    ang = pos.astype(F32) * fr
    w = lax.broadcasted_iota(jnp.int32, (1, 128), 1) % HEAD_DIM
    cs = jnp.cos(ang)
    sn = jnp.sin(ang) * sign
    s1 = jnp.where(w < ROT_DIM // 2, -sn, 0.0)
    s2 = jnp.where((w >= ROT_DIM // 2) & (w < ROT_DIM), sn, 0.0)
    width = xv.shape[1]
    rep = width // 128
    cs, s1, s2 = jnp.tile(cs, (1, rep)), jnp.tile(s1, (1, rep)), jnp.tile(s2, (1, rep))
    hi = pltpu.roll(xv, width - ROT_DIM // 2, 1)
    lo = pltpu.roll(xv, ROT_DIM // 2, 1)
    return xv * cs + hi * s1 + lo * s2


def _sub_spec(d, rows, width):
    return pl.BlockSpec((d, rows // d, width), lambda i: (0, i, 0))


def _gather_rows(scr, blocks, r, d, rows):
    return jnp.concatenate([scr.at[j][pl.ds(r, rows // d, stride=d), :] for j in blocks], axis=1)


def _scatter_rows(scr, src_ref, d, rows):
    for r in range(d):
        for j in range(NB_KV):
            scr.at[j][pl.ds(r, rows // d, stride=d), :] = src_ref[r, :, j * 128:(j + 1) * 128]


def _token_order(scr):
    return jnp.concatenate([scr[j] for j in range(NB_KV)], axis=1)


def _rope_fwd(proj, pos, fr):
    nb = (ROPE_W + KVW) // 128

    def body(x_ref, pos_ref, fr_ref, o0_ref, o1_ref, o2_ref, scr):
        y = _rope_rotate(x_ref[:, 0:ROPE_W], pos_ref[...], fr_ref[...], 1.0)
        for j in range(ROPE_W // 128):
            scr[j] = y[:, j * 128:(j + 1) * 128]
        for j in range(ROPE_W // 128, nb):
            scr[j] = x_ref[:, j * 128:(j + 1) * 128]
        kv = list(range(QW // 128, nb))
        for g, (d, o_ref) in enumerate(zip(DILATIONS, (o0_ref, o1_ref, o2_ref))):
            blocks = list(range(g * NB_KV, (g + 1) * NB_KV)) + kv
            for r in range(d):
                o_ref[r] = _gather_rows(scr, blocks, r, d, TR).astype(BF16)

    return pl.pallas_call(
        body, out_shape=tuple(jax.ShapeDtypeStruct((d, L // d, QKV_W), BF16) for d in DILATIONS), grid=(L // TR,),
        in_specs=[_rowspec(ROPE_W + KVW), pl.BlockSpec((TR, 1), lambda i: (i, 0)), _vec(128)],
        out_specs=tuple(_sub_spec(d, TR, QKV_W) for d in DILATIONS),
        scratch_shapes=[pltpu.VMEM((nb, TR, 128), F32)], name='rope_fwd',
        compiler_params=_params(('parallel',), 40))(proj, pos, fr)


def _rope_bwd(dqkv, du, pos, fr):
    def body(*refs):
        grads = [refs[3 * g:3 * g + 3] for g in range(N_GROUPS)]
        du_ref, pos_ref, fr_ref, o_ref = refs[9:13]
        scrs = refs[13:]
        dq, dk, dv = [], None, None
        for g, d in enumerate(DILATIONS):
            parts = []
            for t in range(3):
                if d == 1:
                    parts.append(grads[g][t][0])
                else:
                    scr = scrs[3 * (g - 1) + t]
                    _scatter_rows(scr, grads[g][t], d, TR)
                    parts.append(_token_order(scr))
            dq.append(parts[0])
            dk = parts[1] if dk is None else dk + parts[1]
            dv = parts[2] if dv is None else dv + parts[2]
        x = jnp.concatenate(dq + [dk], axis=1)
        o_ref[:, 0:ROPE_W] = _rope_rotate(x, pos_ref[...], fr_ref[...], -1.0).astype(BF16)
        o_ref[:, ROPE_W:ROPE_W + KVW] = dv.astype(BF16)
        o_ref[:, ROPE_W + KVW:INW] = du_ref[...].astype(BF16)

    flat = [a for grp in dqkv for a in grp]
    in_specs = [_sub_spec(d, TR, KVW) for d in DILATIONS for _ in range(3)]
    in_specs += [_rowspec(SSMW), pl.BlockSpec((TR, 1), lambda i: (i, 0)), _vec(128)]
    return pl.pallas_call(
        body, out_shape=jax.ShapeDtypeStruct((L, INW), BF16), grid=(L // TR,), in_specs=in_specs,
        out_specs=_rowspec(INW), scratch_shapes=[pltpu.VMEM((NB_KV, TR, 128), F32)] * 6, name='rope_bwd',
        compiler_params=_params(('parallel',), 48))(*flat, du, pos, fr)


def _attn_mask(nbs, b):
    first = (b & (nbs - 1)) == 0
    qi = lax.broadcasted_iota(jnp.int32, (BLK, 2 * BLK), 0)
    kj = lax.broadcasted_iota(jnp.int32, (BLK, 2 * BLK), 1)
    dist = qi + BLK - kj
    return (dist >= 0) & (dist <= BLK) & ((kj >= BLK) | jnp.logical_not(first))


def _qkv_specs():
    cur = lambda col: pl.BlockSpec((BLK, KVW), lambda b: (b, col))
    prev = lambda col: pl.BlockSpec((BLK, KVW), lambda b: (jnp.maximum(b - 1, 0), col))
    return [cur(0), prev(1), cur(1), prev(2), cur(2)]


_ROWS = pl.BlockSpec((BLK, KVW), lambda b: (b, 0))
NEG = -1e30
NT_DIMS = (((1,), (1,)), ((), ()))
TN_DIMS = (((0,), (0,)), ((), ()))


def _attn_fwd(qkv, d):
    nbs = L // d // BLK

    def body(q_ref, kp_ref, kc_ref, vp_ref, vc_ref, o_ref, lse_ref):
        valid = _attn_mask(nbs, pl.program_id(0))
        heads = [slice(h * HEAD_DIM, (h + 1) * HEAD_DIM) for h in range(HEADS)]
        kcs = [jnp.concatenate([kp_ref[:, hs], kc_ref[:, hs]], axis=0) for hs in heads]
        vcs = [jnp.concatenate([vp_ref[:, hs], vc_ref[:, hs]], axis=0) for hs in heads]
        ss = [lax.dot_general(q_ref[:, hs], kc, NT_DIMS, preferred_element_type=F32) for hs, kc in zip(heads, kcs)]
        ps, ls, lses = [], [], []
        for s in ss:
            s = jnp.where(valid, s * 0.125, NEG)
            m = jnp.max(s, axis=-1, keepdims=True)
            p = jnp.exp(s - m)
            l = jnp.sum(p, axis=-1, keepdims=True)
            ps.append(p.astype(BF16))
            ls.append(l)
            lses.append(jnp.broadcast_to(m + jnp.log(l), (BLK, HEAD_DIM)))
        outs = [jnp.dot(p, vc, preferred_element_type=F32) / l for p, vc, l in zip(ps, vcs, ls)]
        o_ref[...] = jnp.concatenate(outs, axis=1)
        lse_ref[...] = jnp.concatenate(lses, axis=1)

    sh = jax.ShapeDtypeStruct((L, KVW), F32)
    q2 = qkv.reshape(L, QKV_W)
    o, lse = pl.pallas_call(
        body, out_shape=(sh, sh), grid=(NBLK,), in_specs=_qkv_specs(), out_specs=(_ROWS, _ROWS),
        name='attn_fwd_d%d' % d, compiler_params=_params(('parallel',), 32))(q2, q2, q2, q2, q2)
    return o.reshape(d, L // d, KVW), lse.reshape(d, L // d, KVW)


def _attn_bwd(qkv, o, lse, do, dlse, d):
    nbs = L // d // BLK

    def body(q_ref, kp_ref, kc_ref, vp_ref, vc_ref, o_ref, lse_ref, do_ref, dlse_ref, dq_ref, dk_ref, dv_ref):
        b = pl.program_id(0)

        @pl.when(b == 0)
        def _():
            dk_ref[...] = jnp.zeros_like(dk_ref)
            dv_ref[...] = jnp.zeros_like(dv_ref)

        valid = _attn_mask(nbs, b)
        prev0 = pl.multiple_of(jnp.maximum(b - 1, 0) * BLK, BLK)
        cur0 = pl.multiple_of(b * BLK, BLK)
        heads = [slice(h * HEAD_DIM, (h + 1) * HEAD_DIM) for h in range(HEADS)]
        qs = [q_ref[:, hs] for hs in heads]
        kcs = [jnp.concatenate([kp_ref[:, hs], kc_ref[:, hs]], axis=0) for hs in heads]
        vcs = [jnp.concatenate([vp_ref[:, hs], vc_ref[:, hs]], axis=0) for hs in heads]
        dos = [do_ref[:, hs] for hs in heads]
        do_bs = [t.astype(BF16) for t in dos]
        ss = [lax.dot_general(q, kc, NT_DIMS, preferred_element_type=F32) for q, kc in zip(qs, kcs)]
        dps = [lax.dot_general(do_b, vc, NT_DIMS, preferred_element_type=F32) for do_b, vc in zip(do_bs, vcs)]
        p_bs, ds_bs = [], []
        for h, hs in enumerate(heads):
            s = jnp.where(valid, ss[h] * 0.125, NEG)
            p = jnp.exp(s - lse_ref[:, h * HEAD_DIM:h * HEAD_DIM + 1])
            delta = jnp.sum(dos[h] * o_ref[:, hs], axis=-1, keepdims=True)
            ds = p * (dps[h] - delta + dlse_ref[:, h * HEAD_DIM:h * HEAD_DIM + 1])
            p_bs.append(p.astype(BF16))
            ds_bs.append((ds * 0.125).astype(BF16))
        dqs = [jnp.dot(ds_b, kc, preferred_element_type=F32) for ds_b, kc in zip(ds_bs, kcs)]
        dks = [lax.dot_general(ds_b, q, TN_DIMS, preferred_element_type=F32) for ds_b, q in zip(ds_bs, qs)]
        dvs = [lax.dot_general(p_b, do_b, TN_DIMS, preferred_element_type=F32) for p_b, do_b in zip(p_bs, do_bs)]
        dq_ref[...] = jnp.concatenate(dqs, axis=1)
        dkc, dvc = jnp.concatenate(dks, axis=1), jnp.concatenate(dvs, axis=1)
        dk_ref[pl.ds(prev0, BLK), :] += dkc[:BLK]
        dv_ref[pl.ds(prev0, BLK), :] += dvc[:BLK]
        dk_ref[pl.ds(cur0, BLK), :] += dkc[BLK:]
        dv_ref[pl.ds(cur0, BLK), :] += dvc[BLK:]

    sh = jax.ShapeDtypeStruct((L, KVW), F32)
    whole = pl.BlockSpec((L, KVW), lambda b: (0, 0))
    q2 = qkv.reshape(L, QKV_W)
    flat = lambda t: t.reshape(L, KVW)
    outs = pl.pallas_call(
        body, out_shape=(sh, sh, sh), grid=(NBLK,), in_specs=_qkv_specs() + [_ROWS] * 4,
        out_specs=(_ROWS, whole, whole), name='attn_bwd_d%d' % d,
        compiler_params=_params(('arbitrary',), 48))(q2, q2, q2, q2, q2, flat(o), flat(lse), flat(do), flat(dlse))
    return tuple(t.reshape(d, L // d, KVW) for t in outs)


TC = 512


def _combine_weights(l0, l1, l2):
    m = jnp.maximum(jnp.maximum(l0, l1), l2)
    e0, e1, e2 = jnp.exp(l0 - m), jnp.exp(l1 - m), jnp.exp(l2 - m)
    z = e0 + e1 + e2
    return e0 / z, e1 / z, e2 / z


def _load_groups(refs, scrs):
    out = [refs[0][0]]
    for g in (1, 2):
        _scatter_rows(scrs[g - 1], refs[g], DILATIONS[g], TC)
        out.append(_token_order(scrs[g - 1]))
    return out


def _combine_fwd(os_, lses, g):
    def body(o0, o1, o2, l0, l1, l2, g_ref, att_ref, *scrs):
        ov = _load_groups((o0, o1, o2), scrs[0:2])
        lv = _load_groups((l0, l1, l2), scrs[2:4])
        w0, w1, w2 = _combine_weights(*lv)
        a = w0 * ov[0] + w1 * ov[1] + w2 * ov[2]
        att_ref[...] = ((a * _rms(a)) * g_ref[...]).astype(BF16)

    subs = [_sub_spec(d, TC, KVW) for d in DILATIONS]
    return pl.pallas_call(
        body, out_shape=jax.ShapeDtypeStruct((L, KVW), BF16), grid=(L // TC,), in_specs=subs + subs + [_vec(KVW)],
        out_specs=pl.BlockSpec((TC, KVW), lambda i: (i, 0)),
        scratch_shapes=[pltpu.VMEM((NB_KV, TC, 128), F32)] * 4, name='combine_fwd',
        compiler_params=_params(('parallel',), 40))(*os_, *lses, g)


def _combine_bwd(dcat, os_, lses, g, head_ones):
    def body(datt_ref, o0, o1, o2, l0, l1, l2, g_ref, e_ref, do0, do1, do2, dl0, dl1, dl2, dg_ref, *scrs):
        @pl.when(pl.program_id(0) == 0)
        def _():
            dg_ref[...] = jnp.zeros_like(dg_ref)

        ov = _load_groups((o0, o1, o2), scrs[0:2])
        lv = _load_groups((l0, l1, l2), scrs[2:4])
        ws = _combine_weights(*lv)
        a = ws[0] * ov[0] + ws[1] * ov[1] + ws[2] * ov[2]
        r = _rms(a)
        n = a * r
        dv = datt_ref[...]
        dg_ref[...] += _colsum(dv * n)
        da = _rms_bwd(dv * g_ref[...], n, r)
        e_b = e_ref[...].astype(BF16)

        def head_sum(t):
            hi = t.astype(BF16)
            lo = (t - hi.astype(F32)).astype(BF16)
            return jnp.dot(hi, e_b, preferred_element_type=F32) + jnp.dot(lo, e_b, preferred_element_type=F32)

        dws = [head_sum(da * ov[i]) for i in range(3)]
        dbar = ws[0] * dws[0] + ws[1] * dws[1] + ws[2] * dws[2]
        scr = scrs[4]
        for i, (d, do_ref, dl_ref) in enumerate(zip(DILATIONS, (do0, do1, do2), (dl0, dl1, dl2))):
            for val, out_ref in ((ws[i] * da, do_ref), (ws[i] * (dws[i] - dbar), dl_ref)):
                if d == 1:
                    out_ref[0] = val
                else:
                    for j in range(NB_KV):
                        scr[j] = val[:, j * 128:(j + 1) * 128]
                    for rr in range(d):
                        out_ref[rr] = _gather_rows(scr, range(NB_KV), rr, d, TC)
        _fold8(pl.program_id(0) == L // TC - 1, dg_ref)

    subs = [_sub_spec(d, TC, KVW) for d in DILATIONS]
    shs = tuple(jax.ShapeDtypeStruct((d, L // d, KVW), F32) for d in DILATIONS)
    outs = pl.pallas_call(
        body, out_shape=shs + shs + (jax.ShapeDtypeStruct((8, KVW), F32),), grid=(L // TC,),
        in_specs=[pl.BlockSpec((TC, KVW), lambda i: (i, 0))] + subs + subs + [_vec(KVW),
                                                                              pl.BlockSpec((KVW, KVW), lambda i: (0, 0))],
        out_specs=tuple(subs) + tuple(subs) + (_vec8(KVW),),
        scratch_shapes=[pltpu.VMEM((NB_KV, TC, 128), F32)] * 5, name='combine_bwd',
        compiler_params=_params(('arbitrary',), 48))(dcat, *os_, *lses, g, head_ones)
    return outs[0:3], outs[3:6], outs[6]


def _ssm_disc(ar, ai, ldt):
    dt = jnp.exp(ldt)
    zr, zi = ar * dt, ai * dt
    ez = jnp.exp(zr)
    A_r, A_i = ez * jnp.cos(zi), ez * jnp.sin(zi)
    den = ar * ar + ai * ai
    xr, xi = A_r - 1.0, A_i
    cr = (xr * ar + xi * ai) / den
    ci = (xi * ar - xr * ai) / den
    return dt, zr, zi, A_r, A_i, den, cr, ci


def _ssm_pre(ar, ai, ldt, br, bi):
    def body(ar_ref, ai_ref, ldt_ref, br_ref, bi_ref, bbr_ref, bbi_ref, pwr_ref, pwi_ref):
        _, zr, zi, _, _, _, cr, ci = _ssm_disc(ar_ref[...], ai_ref[...], ldt_ref[...])
        bbr_ref[...] = cr * br_ref[...] - ci * bi_ref[...]
        bbi_ref[...] = cr * bi_ref[...] + ci * br_ref[...]
        k = (lax.broadcasted_iota(jnp.int32, (1, 8), 1) + 1).astype(F32)
        ek = jnp.exp(zr * k)
        pwr_ref[...] = ek * jnp.cos(zi * k)
        pwi_ref[...] = ek * jnp.sin(zi * k)

    s16 = jax.ShapeDtypeStruct((SSM_GN, SSM_P), F32)
    s8 = jax.ShapeDtypeStruct((SSM_GN, 8), F32)
    return pl.pallas_call(body, out_shape=(s16, s16, s8, s8), name='ssm_pre',
                          compiler_params=_params(None, 40))(ar, ai, ldt, br, bi)


def _ssm_post(ar, ai, ldt, br, bi, gar, gai, gbr, gbi, sel):
    def body(ar_ref, ai_ref, ldt_ref, br_ref, bi_ref, gar_ref, gai_ref, gbr_ref, gbi_ref, sel_ref,
             dar_ref, dai_ref, dbr_ref, dbi_ref, dldt_ref):
        a_r, a_i = ar_ref[...], ai_ref[...]
        dt, _, _, A_r, A_i, den, cr, ci = _ssm_disc(a_r, a_i, ldt_ref[...])
        b_r, b_i, g_br, g_bi = br_ref[...], bi_ref[...], gbr_ref[...], gbi_ref[...]
        gcr = jnp.sum(g_br * b_r + g_bi * b_i, axis=-1, keepdims=True)
        gci = jnp.sum(g_bi * b_r - g_br * b_i, axis=-1, keepdims=True)
        dbr_ref[...] = g_br * cr + g_bi * ci
        dbi_ref[...] = g_bi * cr - g_br * ci
        g_ar = gar_ref[...] + (gcr * a_r - gci * a_i) / den
        g_ai = gai_ref[...] + (gcr * a_i + gci * a_r) / den
        qr = (cr * a_r + ci * a_i) / den
        qi = (ci * a_r - cr * a_i) / den
        glr = -(gcr * qr + gci * qi)
        gli = -(gci * qr - gcr * qi)
        gzr = g_ar * A_r + g_ai * A_i
        gzi = g_ai * A_r - g_ar * A_i
        dar_ref[...] = glr + gzr * dt
        dai_ref[...] = gli + gzi * dt
        gdt = (gzr * a_r + gzi * a_i) * dt
        dldt_ref[...] = jnp.dot(sel_ref[...], jnp.broadcast_to(gdt, (SSM_GN, 128)),
                                preferred_element_type=F32, precision=HI)

    s1 = jax.ShapeDtypeStruct((SSM_GN, 1), F32)
    s16 = jax.ShapeDtypeStruct((SSM_GN, SSM_P), F32)
    return pl.pallas_call(body, out_shape=(s1, s1, s16, s16, jax.ShapeDtypeStruct((SSM_G, 128), F32)),
                          name='ssm_post', compiler_params=_params(None, 48))(
                              ar, ai, ldt, br, bi, gar, gai, gbr, gbi, sel)


SCAN_CH = 8


def _scan_fwd_tiles(s_ref, pw, carry):
    pwr, pwi = pw[:, :CL_S], pw[:, CL_S:]
    row = lax.broadcasted_iota(jnp.int32, (8, CL_S), 0)
    steps = [(k, jnp.where(row >= k, pwr[k - 1:k], 0.0), jnp.where(row >= k, pwi[k - 1:k], 0.0)) for k in (1, 2, 4)]
    rows = 8 * SCAN_CH

    def chunk(i, c):
        cr, ci = c
        r0 = pl.multiple_of(i * rows, rows)
        xr = s_ref[pl.ds(r0, rows), 0:CL_S].reshape(SCAN_CH, 8, CL_S)
        xi = s_ref[pl.ds(r0, rows), CL_S:2 * CL_S].reshape(SCAN_CH, 8, CL_S)
        for k, pr, pi in steps:
            sr, si = pltpu.roll(xr, k, 1), pltpu.roll(xi, k, 1)
            xr, xi = xr + pr * sr - pi * si, xi + pr * si + pi * sr
        for j in range(SCAN_CH):
            tr = xr[j] + pwr * cr - pwi * ci
            ti = xi[j] + pwr * ci + pwi * cr
            s_ref[pl.ds(r0 + 8 * j, 8), 0:CL_S] = tr
            s_ref[pl.ds(r0 + 8 * j, 8), CL_S:2 * CL_S] = ti
            cr, ci = tr[7:8], ti[7:8]
        return cr, ci

    return lax.fori_loop(0, T_SCAN // rows, chunk, (carry[:, :CL_S], carry[:, CL_S:]))


def _scan_bwd_tiles(l_ref, pw, carry):
    pwr, pwi = pw[:, :CL_S], pw[:, CL_S:]
    rpr = jnp.concatenate([pwr[7 - r:8 - r] for r in range(8)], axis=0)
    rpi = jnp.concatenate([pwi[7 - r:8 - r] for r in range(8)], axis=0)
    row = lax.broadcasted_iota(jnp.int32, (8, CL_S), 0)
    steps = [(k, jnp.where(row < 8 - k, pwr[k - 1:k], 0.0), jnp.where(row < 8 - k, pwi[k - 1:k], 0.0))
             for k in (1, 2, 4)]
    rows = 8 * SCAN_CH
    nc = T_SCAN // rows

    def chunk(i, c):
        cr, ci = c
        r0 = pl.multiple_of((nc - 1 - i) * rows, rows)
        xr = l_ref[pl.ds(r0, rows), 0:CL_S].reshape(SCAN_CH, 8, CL_S)
        xi = l_ref[pl.ds(r0, rows), CL_S:2 * CL_S].reshape(SCAN_CH, 8, CL_S)
        for k, pr, pi in steps:
            sr, si = pltpu.roll(xr, 8 - k, 1), pltpu.roll(xi, 8 - k, 1)
            xr, xi = xr + pr * sr + pi * si, xi + pr * si - pi * sr
        for j in reversed(range(SCAN_CH)):
            tr = xr[j] + rpr * cr + rpi * ci
            ti = xi[j] + rpr * ci - rpi * cr
            l_ref[pl.ds(r0 + 8 * j, 8), 0:CL_S] = tr
            l_ref[pl.ds(r0 + 8 * j, 8), CL_S:2 * CL_S] = ti
            cr, ci = tr[0:1], ti[0:1]
        return cr, ci

    return lax.fori_loop(0, nc, chunk, (carry[:, :CL_S], carry[:, CL_S:]))


NT_SCAN = L // T_SCAN


def _cl_spec(r, c):
    return pl.BlockSpec((None, r, c), lambda c_, t: (c_, 0, 0))


def _ssm_fwd(u, bm, cm, pw, dvec, u_off=0):
    def body(u_ref, bm_ref, cm_ref, pw_ref, d_ref, y_ref, bnd_ref, s_ref, carry_ref):
        @pl.when(pl.program_id(1) == 0)
        def _():
            carry_ref[...] = jnp.zeros_like(carry_ref)

        bnd_ref[...] = carry_ref[...]
        uv = u_ref[...]
        s_ref[...] = jnp.dot(uv.astype(BF16), bm_ref[...], preferred_element_type=F32)
        cr, ci = _scan_fwd_tiles(s_ref, pw_ref[...], carry_ref[...])
        carry_ref[...] = jnp.concatenate([cr, ci], axis=1)
        y_ref[...] = jnp.dot(s_ref[...].astype(BF16), cm_ref[...], preferred_element_type=F32) + d_ref[...] * uv

    return pl.pallas_call(
        body,
        out_shape=(jax.ShapeDtypeStruct((L, SSMW), F32), jax.ShapeDtypeStruct((N_CL, NT_SCAN, 1, 2 * CL_S), F32),
                   jax.ShapeDtypeStruct((L, N_CL * 2 * CL_S), F32)),
        grid=(N_CL, NT_SCAN),
        in_specs=[pl.BlockSpec((T_SCAN, CL_U), lambda c, t: (t, c + u_off)),
                  _cl_spec(CL_U, 2 * CL_S), _cl_spec(2 * CL_S, CL_U),
                  pl.BlockSpec((None, 8, 2 * CL_S), lambda c, t: (c, 0, 0)),
                  pl.BlockSpec((1, CL_U), lambda c, t: (0, c))],
        out_specs=(pl.BlockSpec((T_SCAN, CL_U), lambda c, t: (t, c)),
                   pl.BlockSpec((None, None, 1, 2 * CL_S), lambda c, t: (c, t, 0, 0)),
                   pl.BlockSpec((T_SCAN, 2 * CL_S), lambda c, t: (t, c))),
        scratch_shapes=[pltpu.VMEM((1, 2 * CL_S), F32)],
        name='ssm_fwd', compiler_params=_params(('arbitrary', 'arbitrary'), 40))(u, bm, cm, pw, dvec)


def _ssm_bwd(u, dy, states, bmt, cmt, pw, dvec, bnd, u_off=0):
    rev = lambda t: NT_SCAN - 1 - t

    def body(u_ref, dy_ref, s_ref, bmt_ref, cmt_ref, pw_ref, d_ref, bnd_ref,
             du_ref, dbm_ref, dcm_ref, da_ref, dd_ref, l_ref, carry_ref):
        @pl.when(pl.program_id(1) == 0)
        def _():
            carry_ref[...] = jnp.zeros_like(carry_ref)
            dbm_ref[...] = jnp.zeros_like(dbm_ref)
            dcm_ref[...] = jnp.zeros_like(dcm_ref)
            da_ref[...] = jnp.zeros_like(da_ref)
            dd_ref[...] = jnp.zeros_like(dd_ref)

        uv, dyv, pw = u_ref[...], dy_ref[...], pw_ref[...]
        dy_b = dyv.astype(BF16)
        entry = bnd_ref[...]
        l_ref[...] = jnp.dot(dy_b, cmt_ref[...], preferred_element_type=F32)
        cr, ci = _scan_bwd_tiles(l_ref, pw, carry_ref[...])
        carry_ref[...] = jnp.concatenate([cr, ci], axis=1)
        sv, lv = s_ref[...], l_ref[...]
        lv_b = lv.astype(BF16)
        du_ref[...] = dyv * d_ref[...] + jnp.dot(lv_b, bmt_ref[...], preferred_element_type=F32)
        dbm_ref[...] += lax.dot_general(uv.astype(BF16), lv_b, TN_DIMS, preferred_element_type=F32)
        dcm_ref[...] += lax.dot_general(sv.astype(BF16), dy_b, TN_DIMS, preferred_element_type=F32)
        dd_ref[...] += _colsum(dyv * uv)
        row = lax.broadcasted_iota(jnp.int32, (T_SCAN, 2 * CL_S), 0)
        sp = jnp.where(row == 0, entry, pltpu.roll(sv, 1, 0))
        spr, spi = sp[:, :CL_S], sp[:, CL_S:]
        lr, li = lv[:, :CL_S], lv[:, CL_S:]
        da_ref[:, 0:CL_S] += _colsum(lr * spr + li * spi)
        da_ref[:, CL_S:2 * CL_S] += _colsum(li * spr - lr * spi)
        _fold8(pl.program_id(1) == NT_SCAN - 1, da_ref, dd_ref)

    return pl.pallas_call(
        body,
        out_shape=(jax.ShapeDtypeStruct((L, SSMW), F32), jax.ShapeDtypeStruct((N_CL, CL_U, 2 * CL_S), F32),
                   jax.ShapeDtypeStruct((N_CL, 2 * CL_S, CL_U), F32), jax.ShapeDtypeStruct((N_CL, 8, 2 * CL_S), F32),
                   jax.ShapeDtypeStruct((8, SSMW), F32)),
        grid=(N_CL, NT_SCAN),
        in_specs=[pl.BlockSpec((T_SCAN, CL_U), lambda c, t: (rev(t), c + u_off)),
                  pl.BlockSpec((T_SCAN, CL_U), lambda c, t: (rev(t), c)),
                  pl.BlockSpec((T_SCAN, 2 * CL_S), lambda c, t: (rev(t), c)),
                  pl.BlockSpec((None, 2 * CL_S, CL_U), lambda c, t: (c, 0, 0)),
                  pl.BlockSpec((None, CL_U, 2 * CL_S), lambda c, t: (c, 0, 0)),
                  pl.BlockSpec((None, 8, 2 * CL_S), lambda c, t: (c, 0, 0)),
                  pl.BlockSpec((1, CL_U), lambda c, t: (0, c)),
                  pl.BlockSpec((None, None, 1, 2 * CL_S), lambda c, t: (c, rev(t), 0, 0))],
        out_specs=(pl.BlockSpec((T_SCAN, CL_U), lambda c, t: (rev(t), c)),
                   pl.BlockSpec((None, CL_U, 2 * CL_S), lambda c, t: (c, 0, 0)),
                   pl.BlockSpec((None, 2 * CL_S, CL_U), lambda c, t: (c, 0, 0)),
                   pl.BlockSpec((None, 8, 2 * CL_S), lambda c, t: (c, 0, 0)),
                   pl.BlockSpec((8, CL_U), lambda c, t: (0, c))),
        scratch_shapes=[pltpu.VMEM((T_SCAN, 2 * CL_S), F32), pltpu.VMEM((1, 2 * CL_S), F32)],
        name='ssm_bwd', compiler_params=_params(('arbitrary', 'arbitrary'), 48))(u, dy, states, bmt, cmt, pw, dvec, bnd)


GELU_C = math.sqrt(2.0 / math.pi)
GELU_K = 0.044715


def _gelu_parts(x):
    t = jnp.tanh(GELU_C * (x + GELU_K * (x * x * x)))
    return x * (0.5 * (1.0 + t)), t


def _glu_fwd(ypre, wglu, bglu, gs):
    def body(y_ref, w_ref, b_ref, g_ref, o_ref):
        yg, _ = _gelu_parts(y_ref[...])
        z = jnp.dot(yg.astype(BF16), w_ref[...], preferred_element_type=F32) + b_ref[...]
        s = yg * jax.nn.sigmoid(z)
        o_ref[...] = ((s * _rms(s)) * g_ref[...]).astype(BF16)

    return pl.pallas_call(
        body, out_shape=jax.ShapeDtypeStruct((L, SSMW), BF16), grid=(L // TR,),
        in_specs=[_rowspec(SSMW), pl.BlockSpec((SSMW, SSMW), lambda i: (0, 0)), _vec(SSMW), _vec(SSMW)],
        out_specs=_rowspec(SSMW), name='glu_fwd', compiler_params=_params(('parallel',), 32))(ypre, wglu, bglu, gs)


def _glu_bwd(ypre, dsn, wglu, bglu, gs):
    def body(y_ref, d_ref, w_ref, b_ref, g_ref, dy_ref, dw_ref, db_ref, dg_ref):
        @pl.when(pl.program_id(0) == 0)
        def _():
            dw_ref[...] = jnp.zeros_like(dw_ref)
            db_ref[...] = jnp.zeros_like(db_ref)
            dg_ref[...] = jnp.zeros_like(dg_ref)

        xv = y_ref[...]
        yg, t = _gelu_parts(xv)
        yg_b = yg.astype(BF16)
        z = jnp.dot(yg_b, w_ref[...], preferred_element_type=F32) + b_ref[...]
        sg = jax.nn.sigmoid(z)
        s = yg * sg
        r = _rms(s)
        n = s * r
        dv = d_ref[:, d_ref.shape[1] - SSMW:]
        dg_ref[...] += _colsum(dv * n)
        ds = _rms_bwd(dv * g_ref[...], n, r)
        dz =(ds * yg) * (sg * (1.0 - sg))
        dz_b = dz.astype(BF16)
        db_ref[...] += _colsum(dz)
        dw_ref[...] += lax.dot_general(yg_b, dz_b, TN_DIMS, preferred_element_type=F32)
        dyg = ds * sg + lax.dot_general(dz_b, w_ref[...], NT_DIMS, preferred_element_type=F32)
        dgelu = 0.5 * (1.0 + t) + (0.5 * xv) * (1.0 - t * t) * (GELU_C * (1.0 + 3.0 * GELU_K * (xv * xv)))
        dy_ref[...] = dyg * dgelu
        _fold8(pl.program_id(0) == L // TR - 1, db_ref, dg_ref)

    vs = jax.ShapeDtypeStruct((8, SSMW), F32)
    return pl.pallas_call(
        body, out_shape=(jax.ShapeDtypeStruct((L, SSMW), F32), jax.ShapeDtypeStruct((SSMW, SSMW), F32), vs, vs),
        grid=(L // TR,),
        in_specs=[_rowspec(SSMW), _rowspec(dsn.shape[1]), pl.BlockSpec((SSMW, SSMW), lambda i: (0, 0)), _vec(SSMW),
                  _vec(SSMW)],
        out_specs=(_rowspec(SSMW), pl.BlockSpec((SSMW, SSMW), lambda i: (0, 0)), _vec8(SSMW), _vec8(SSMW)),
        name='glu_bwd', compiler_params=_params(('arbitrary',), 40))(ypre, dsn, wglu, bglu, gs)


def _me():
    return lax.axis_index('x'), lax.axis_index('y'), lax.axis_index('c')


def _my_index():
    return 4 * lax.axis_index('x') + 2 * lax.axis_index('y') + lax.axis_index('c')


def _peer(k):
    x, y, c = _me()
    px = 1 - x if k & 4 else x
    py = 1 - y if k & 2 else y
    pc = 1 - c if k & 1 else c
    return (px, py, pc), 4 * px + 2 * py + pc


def _mod_exchange(c_row, w_ada, b_ada8, deps=()):
    cw = NMOD * D // N_DEV

    def body(c_ref, w_ref, b_ref, *rest):
        call_ref, mod_ref, part_ref, send_sems, recv_sems = rest[len(deps):]
        x, y, c = _me()
        me = 4 * x + 2 * y + c
        call_ref[me] = c_ref[0]
        sends = []
        for k in range(1, N_DEV):
            peer, _ = _peer(k)
            cp = pltpu.make_async_remote_copy(src_ref=c_ref.at[0], dst_ref=call_ref.at[me], send_sem=send_sems.at[0, k - 1],
                                              recv_sem=recv_sems.at[0, k - 1], device_id=peer, device_id_type=MESH)
            cp.start()
            sends.append(cp)
        for k in range(1, N_DEV):
            peer, pidx = _peer(k)
            pltpu.make_async_remote_copy(src_ref=c_ref.at[0], dst_ref=call_ref.at[pidx], send_sem=send_sems.at[0, k - 1],
                                         recv_sem=recv_sems.at[0, k - 1], device_id=peer, device_id_type=MESH).wait_recv()
        for cp in sends:
            cp.wait_send()
        cv = call_ref[...].reshape(N_DEV, D)
        part = jnp.dot(cv * jax.nn.sigmoid(cv), w_ref[...], preferred_element_type=F32, precision=HI)
        part_ref[...] = part.reshape(N_DEV, 1, cw)
        mod_ref[me] = part_ref[me]
        sends = []
        for k in range(1, N_DEV):
            peer, pidx = _peer(k)
            cp = pltpu.make_async_remote_copy(src_ref=part_ref.at[pidx], dst_ref=mod_ref.at[me], send_sem=send_sems.at[1, k - 1],
                                              recv_sem=recv_sems.at[1, k - 1], device_id=peer, device_id_type=MESH)
            cp.start()
            sends.append(cp)
        for k in range(1, N_DEV):
            peer, pidx = _peer(k)
            pltpu.make_async_remote_copy(src_ref=part_ref.at[pidx], dst_ref=mod_ref.at[pidx], send_sem=send_sems.at[1, k - 1],
                                         recv_sem=recv_sems.at[1, k - 1], device_id=peer, device_id_type=MESH).wait_recv()
        for cp in sends:
            cp.wait_send()
        mod_ref[...] = mod_ref[...] + b_ref[...]

    vm = pl.BlockSpec(memory_space=pltpu.VMEM)
    return pl.pallas_call(
        body, out_shape=(jax.ShapeDtypeStruct((N_DEV, 1, D), F32), jax.ShapeDtypeStruct((N_DEV, 1, cw), F32)),
        in_specs=[vm, vm, vm] + [pl.BlockSpec(memory_space=pl.ANY)] * len(deps), out_specs=(vm, vm),
        scratch_shapes=[pltpu.VMEM((N_DEV, 1, cw), F32), pltpu.SemaphoreType.DMA((2, N_DEV - 1)),
                        pltpu.SemaphoreType.DMA((2, N_DEV - 1))],
        name='mod_exchange', compiler_params=_params(None, 48))(c_row, w_ada, b_ada8, *deps)


HBM_SPEC = pl.BlockSpec(memory_space=pltpu.HBM)
SEM_SPEC = pl.BlockSpec(memory_space=pltpu.SEMAPHORE)
DATAFLOW = pltpu.SideEffectType.DATAFLOW_SIDE_EFFECTING


def _push_start(src, scatter, after, name):
    land = lax.empty(src.shape if scatter else (N_DEV,) + src.shape, src.dtype)

    def body(src_ref, land_ref, after_ref, send_sem, recv_sem, land_thru, token):
        x, y, c = _me()
        me = 4 * x + 2 * y + c
        for k in range(1, N_DEV):
            peer, pidx = _peer(k)
            pltpu.make_async_remote_copy(src_ref=src_ref.at[pidx] if scatter else src_ref, dst_ref=land_ref.at[me],
                                         send_sem=send_sem, recv_sem=recv_sem, device_id=peer,
                                         device_id_type=MESH).start()
        token[...] = jnp.zeros_like(token)

    own = lax.dynamic_index_in_dim(src, _my_index(), 0, keepdims=False) if scatter else src
    src = pltpu.with_memory_space_constraint(src, pltpu.HBM)
    send_sem, recv_sem, land_thru, token = pl.pallas_call(
        body, name=name,
        out_shape=(pltpu.SemaphoreType.DMA(()), pltpu.SemaphoreType.DMA(()),
                   pltpu.HBM(land.shape, land.dtype), jax.ShapeDtypeStruct((8, 128), F32)),
        in_specs=(HBM_SPEC, HBM_SPEC, pl.BlockSpec(memory_space=pl.ANY)),
        out_specs=(SEM_SPEC, SEM_SPEC, HBM_SPEC, pl.BlockSpec(memory_space=pltpu.VMEM)),
        input_output_aliases={1: 2}, compiler_params=pltpu.CompilerParams(has_side_effects=DATAFLOW),
    )(src, pltpu.with_memory_space_constraint(land, pltpu.HBM), after)
    return send_sem, recv_sem, src, land_thru, token, own


def _push_wait(handle, after, name, merge=True):
    send_sem, recv_sem, src, land_thru, _, own = handle
    after = tuple(after) if isinstance(after, (tuple, list)) else (after,)

    def body(src_ref, land_ref, send_sem, recv_sem, *rest):
        seven = land_ref.at[pl.ds(0, N_DEV - 1)]
        cp = pltpu.make_async_remote_copy(src_ref=seven, dst_ref=seven, send_sem=send_sem, recv_sem=recv_sem,
                                          device_id=_me(), device_id_type=MESH)
        cp.wait_send()
        cp.wait_recv()

    landed = pl.pallas_call(
        body, name=name, out_shape=pltpu.HBM(land_thru.shape, land_thru.dtype),
        in_specs=(HBM_SPEC, HBM_SPEC, SEM_SPEC, SEM_SPEC) + (pl.BlockSpec(memory_space=pl.ANY),) * len(after),
        out_specs=HBM_SPEC, input_output_aliases={1: 0},
        compiler_params=pltpu.CompilerParams(has_side_effects=DATAFLOW),
    )(src, land_thru, send_sem, recv_sem, *after)
    if not merge:
        return landed, own
    return lax.dynamic_update_index_in_dim(landed, own, _my_index(), 0)


def _adam(w, g, m, v):
    m2 = B1 * m + (1.0 - B1) * g
    v2 = B2 * v + (1.0 - B2) * jnp.square(g)
    m_hat = m2 / (1.0 - B1 ** STEP)
    v_hat = v2 / (1.0 - B2 ** STEP)
    delta = -LR * (m_hat / (jnp.sqrt(v_hat) + AEPS) + WD * w)
    return delta, m2, v2


def _small_update(gp, wp, mp, vp):
    def body(g_ref, w_ref, m_ref, v_ref, all_ref, go_ref, d_ref, mo_ref, vo_ref, send_sems, recv_sems):
        x, y, c = _me()
        me = 4 * x + 2 * y + c
        all_ref[me] = g_ref[...]
        sends = []
        for k in range(1, N_DEV):
            peer, _ = _peer(k)
            cp = pltpu.make_async_remote_copy(src_ref=g_ref, dst_ref=all_ref.at[me], send_sem=send_sems.at[k - 1],
                                              recv_sem=recv_sems.at[k - 1], device_id=peer, device_id_type=MESH)
            cp.start()
            sends.append(cp)
        for k in range(1, N_DEV):
            peer, pidx = _peer(k)
            pltpu.make_async_remote_copy(src_ref=g_ref, dst_ref=all_ref.at[pidx], send_sem=send_sems.at[k - 1],
                                         recv_sem=recv_sems.at[k - 1], device_id=peer, device_id_type=MESH).wait_recv()
        for cp in sends:
            cp.wait_send()
        g = all_ref[0]
        for d in range(1, N_DEV):
            g = g + all_ref[d]
        delta, m2, v2 = _adam(w_ref[...], g, m_ref[...], v_ref[...])
        go_ref[...] = g
        d_ref[...] = delta
        mo_ref[...] = m2
        vo_ref[...] = v2

    vm = pl.BlockSpec(memory_space=pltpu.VMEM)
    vs = jax.ShapeDtypeStruct(gp.shape, F32)
    return pl.pallas_call(
        body, out_shape=(jax.ShapeDtypeStruct((N_DEV,) + gp.shape, F32), vs, vs, vs, vs), in_specs=[vm] * 4,
        out_specs=(vm,) * 5,
        scratch_shapes=[pltpu.SemaphoreType.DMA((N_DEV - 1,)), pltpu.SemaphoreType.DMA((N_DEV - 1,))],
        name='small_update', compiler_params=_params(None, 48))(gp, wp, mp, vp)


def _small_sum_update(parts, wp, mp, vp):
    def body(p_ref, w_ref, m_ref, v_ref, go_ref, d_ref, mo_ref, vo_ref):
        g = p_ref[0]
        for d in range(1, N_DEV):
            g = g + p_ref[d]
        delta, m2, v2 = _adam(w_ref[...], g, m_ref[...], v_ref[...])
        go_ref[...] = g
        d_ref[...] = delta
        mo_ref[...] = m2
        vo_ref[...] = v2

    vm = pl.BlockSpec(memory_space=pltpu.VMEM)
    vs = jax.ShapeDtypeStruct(wp.shape, F32)
    return pl.pallas_call(body, out_shape=(vs, vs, vs, vs), in_specs=[vm] * 4, out_specs=(vm,) * 4,
                          name='small_sum_update', compiler_params=_params(None, 48))(parts, wp, mp, vp)


def _big_update(parts, own, w, m, v, name):
    _, R, C = parts.shape
    tr = R if R % 256 else (128 if C >= 2048 else 256)

    def body(me_ref, p_ref, own_ref, w_ref, m_ref, v_ref, g_ref, d_ref, mo_ref, vo_ref):
        me = me_ref[0]
        mine = own_ref[...].astype(F32)
        g = jnp.where(me == 0, mine, p_ref[0].astype(F32))
        for d in range(1, N_DEV):
            g = g + jnp.where(me == d, mine, p_ref[d].astype(F32))
        delta, m2, v2 = _adam(w_ref[...], g, m_ref[...], v_ref[...])
        g_ref[...] = g
        d_ref[...] = delta
        mo_ref[...] = m2
        vo_ref[...] = v2

    blk = pl.BlockSpec((tr, C), lambda i: (i, 0))
    sh = jax.ShapeDtypeStruct((R, C), F32)
    return pl.pallas_call(
        body, out_shape=(sh, sh, sh, sh), grid=(R // tr,),
        in_specs=[pl.BlockSpec(memory_space=pltpu.SMEM), pl.BlockSpec((N_DEV, tr, C), lambda i: (0, i, 0)), blk, blk,
                  blk, blk], out_specs=(blk,) * 4,
        name=name, compiler_params=_params(('parallel',), 48))(_my_index().reshape(1), parts, own, w, m, v)


def _ada_update(c_all, dmod_cols, w, m, v):
    C = w.shape[1]
    tr = 256

    def body(c_ref, dm_ref, w_ref, m_ref, v_ref, g_ref, d_ref, mo_ref, vo_ref):
        cv = c_ref[...]
        s = cv * jax.nn.sigmoid(cv)
        g = lax.dot_general(s, dm_ref[...], TN_DIMS, preferred_element_type=F32, precision=HI)
        delta, m2, v2 = _adam(w_ref[...], g, m_ref[...], v_ref[...])
        g_ref[...] = g
        d_ref[...] = delta
        mo_ref[...] = m2
        vo_ref[...] = v2

    blk = pl.BlockSpec((tr, C), lambda i: (i, 0))
    sh = jax.ShapeDtypeStruct((D, C), F32)
    return pl.pallas_call(
        body, out_shape=(sh, sh, sh, sh), grid=(D // tr,),
        in_specs=[pl.BlockSpec((N_DEV, tr), lambda i: (0, i)), pl.BlockSpec((N_DEV, C), lambda i: (0, 0)), blk, blk, blk],
        out_specs=(blk,) * 4, name='ada_update', compiler_params=_params(('parallel',), 48))(c_all, dmod_cols, w, m, v)


def _rows_to_cluster_lanes(t):
    k = t.shape[1]
    return t.reshape(N_CL, CL_S, k).transpose(0, 2, 1)


def _blockdiag_in(t):
    t = t.reshape(N_CL, CL_G, SSM_N, SSM_P).transpose(0, 1, 3, 2)
    eye = jnp.eye(CL_G, dtype=t.dtype)
    t = t[:, :, :, None, :] * eye[None, :, None, :, None]
    return t.reshape(N_CL, CL_U, CL_S)


def _blockdiag_extract(t):
    t = t.reshape(N_CL, CL_G, SSM_P, CL_G, SSM_N)
    eye = jnp.eye(CL_G, dtype=t.dtype)
    t = jnp.sum(t * eye[None, :, None, :, None], axis=3)
    return t.transpose(0, 1, 3, 2).reshape(SSM_GN, SSM_P)


def _c_to_rows(t):
    return t.transpose(0, 2, 1).reshape(SSM_GN, SSM_P)


def _rows_to_c(t):
    return t.reshape(SSM_G, SSM_N, SSM_P).transpose(0, 2, 1)


def _ssm_prep(sp):
    rows = lambda n: sp[n].reshape(SSM_GN, 1)
    a_re, a_im = rows('ssm_a_re'), rows('ssm_a_im')
    ldt = jnp.repeat(sp['ssm_log_dt'].reshape(SSM_G, 1), SSM_N, axis=0)
    b_re, b_im = sp['ssm_b_re'].reshape(SSM_GN, SSM_P), sp['ssm_b_im'].reshape(SSM_GN, SSM_P)
    c_re, c_im = _c_to_rows(sp['ssm_c_re'].reshape(SSM_G, SSM_P, SSM_N)), _c_to_rows(sp['ssm_c_im'].reshape(SSM_G, SSM_P, SSM_N))
    bbr, bbi, pwr, pwi = _ssm_pre(a_re, a_im, ldt, b_re, b_im)
    bm = jnp.concatenate([_blockdiag_in(bbr), _blockdiag_in(bbi)], axis=2).astype(BF16)
    cmt = jnp.concatenate([_blockdiag_in(c_re), -_blockdiag_in(c_im)], axis=2).astype(BF16)
    bmt, cm = bm.transpose(0, 2, 1), cmt.transpose(0, 2, 1)
    pw = jnp.concatenate([_rows_to_cluster_lanes(pwr), _rows_to_cluster_lanes(pwi)], axis=2)
    return a_re, a_im, ldt, b_re, b_im, bm, cm, bmt, cmt, pw


def _tied(v, deps):
    for t in deps:
        v = v + t[0, 0]
    return v


def _local_step(x, pos, mod, tgt, sp, prep, get_w, emit, emit_small, emit_late, first_deps=()):
    sh1, sc1, gt1, sh2, sc2, gt2 = (mod[i:i + 1] for i in range(NMOD))
    vec = lambda n: sp[n].reshape(1, -1)
    a_re, a_im, ldt, b_re, b_im, bm_b, cm_b, bmt_b, cmt_b, pw = prep
    dvec = vec('ssm_d')

    h1 = _prenorm_fwd(x, vec('g_pre_mix'), sc1, sh1)
    w_in = get_w('w_in', (h1, bm_b, cm_b, pw, bmt_b, cmt_b))
    proj = _mm(h1, w_in, mode='nn', name='mm_in', tn=1408, deps=first_deps)
    fr1 =ROPE_THETA ** (-jnp.arange(0, ROT_DIM, 2, dtype=F32) / ROT_DIM)
    lane = jnp.arange(128) % HEAD_DIM
    fr = jnp.where(lane < ROT_DIM, fr1[lane % (ROT_DIM // 2)], 0.0).reshape(1, 128).astype(F32)
    u_off = (ROPE_W + KVW) // CL_U
    qkvs = _rope_fwd(proj, pos, fr)
    fwd = [_attn_fwd(qkvs[g], d) for g, d in enumerate(DILATIONS)]
    os_, lses = [t[0] for t in fwd], [t[1] for t in fwd]
    att = _combine_fwd(os_, lses, vec('g_attn_out'))

    ypre, bnd, states = _ssm_fwd(proj, bm_b, cm_b, pw, dvec, u_off)
    w_glu = get_w('w_glu', ypre)
    ssm_n = _glu_fwd(ypre, w_glu, vec('b_glu'), vec('g_ssm_out'))

    cat = jnp.concatenate([att, ssm_n], axis=1)
    w_out = get_w('w_out', cat)
    mix = _mm(cat, w_out, mode='nn', name='mm_out', tk=1280)
    x1, h2 = _postmix_fwd(x, mix, vec('g_post_mix'), gt1, vec('g_pre_mlp'), sc2, sh2)
    w_mi = get_w('w_mlp_in', h2)
    a_pre, r_act = _mm(h2, w_mi, mode='nn', name='mm_mlp_in', epilogue='relu2', b_sharded=True)
    w_mo = get_w('w_mlp_out', a_pre)
    y = _mm(r_act, w_mo, mode='nn', name='mm_mlp_out')
    dx2, dy, loss, dgt2, dg_post_mlp = _final_fwd_bwd(x1, y, tgt, vec('g_post_mlp'), gt2)
    dgt2, dg_post_mlp = dgt2[:1], dg_post_mlp[:1]

    da = _mm(dy, w_mo, mode='nt', name='mm_d_act', out_dtype=BF16, epilogue='drelu2', extra=a_pre)
    dep = emit('w_mlp_out', _mm(r_act, dy, mode='tn', name='mm_dw_mlp_out', out_dtype=BF16))
    dh2 = _mm(da, w_mi, mode='nt', name='mm_dh2', tk=2048, b_sharded=True, deps=dep)
    dep = emit('w_mlp_in', _mm(h2, da, mode='tn', name='mm_dw_mlp_in', out_dtype=BF16, out_sharded=True))
    dx1, dmix, dsc2, dsh2, dg_pre_mlp, dgt1, dg_post_mix = _postmix_bwd(
        dx2, dh2, x1, mix, vec('g_post_mix'), gt1, vec('g_pre_mlp'), sc2)
    dsc2, dsh2, dg_pre_mlp, dgt1, dg_post_mix = (t[:1] for t in (dsc2, dsh2, dg_pre_mlp, dgt1, dg_post_mix))
    dcat = _mm(dmix, w_out, mode='nt', name='mm_dcat', tn=1280, deps=dep)
    dep = emit('w_out', _mm(cat, dmix, mode='tn', name='mm_dw_out', out_dtype=BF16, tm=640))
    dypre, g_w_glu, g_b_glu, g_g_ssm = _glu_bwd(ypre, dcat, w_glu, _tied(vec('b_glu'), dep), vec('g_ssm_out'))
    g_b_glu, g_g_ssm = g_b_glu[:1], g_g_ssm[:1]
    dep = dep + emit('w_glu', g_w_glu.astype(BF16))
    du, dbm, dcm, dA, dD = _ssm_bwd(proj, dypre, states, bmt_b, cmt_b, pw, dvec, bnd, u_off)
    dD = dD[:1]
    gbr, gbi = _blockdiag_extract(dbm[:, :, :CL_S]), _blockdiag_extract(dbm[:, :, CL_S:])
    dcmt = dcm.transpose(0, 2, 1)
    g_c_re = _rows_to_c(_blockdiag_extract(dcmt[:, :, :CL_S]))
    g_c_im = _rows_to_c(-_blockdiag_extract(dcmt[:, :, CL_S:]))
    gar = dA[:, 0, :CL_S].reshape(SSM_GN, 1)
    gai = dA[:, 0, CL_S:].reshape(SSM_GN, 1)
    sel = (jnp.arange(SSM_GN)[None, :] // SSM_N == jnp.arange(SSM_G)[:, None]).astype(F32)
    g_a_re, g_a_im, g_b_re, g_b_im, g_ldt = _ssm_post(a_re, a_im, ldt, b_re, b_im, gar, gai, gbr, gbi, sel)

    head_ones = (jnp.arange(KVW)[:, None] // HEAD_DIM == jnp.arange(KVW)[None, :] // HEAD_DIM).astype(F32)
    dos, dlses, g_g_attn = _combine_bwd(dcat, os_, lses, vec('g_attn_out'), head_ones)
    g_g_attn = g_g_attn[:1]
    dep_small = emit_small({
        'g_post_mix': dg_post_mix, 'ssm_a_re': g_a_re, 'ssm_a_im': g_a_im, 'ssm_log_dt': g_ldt[:, 0],
        'ssm_b_re': g_b_re, 'ssm_b_im': g_b_im, 'ssm_c_re': g_c_re, 'ssm_c_im': g_c_im, 'ssm_d': dD, 'b_glu': g_b_glu,
        'g_attn_out': g_g_attn, 'g_ssm_out': g_g_ssm, 'g_pre_mlp': dg_pre_mlp, 'g_post_mlp': dg_post_mlp})
    dqkv = [_attn_bwd(qkvs[g], os_[g], lses[g], dos[g], dlses[g], d) for g, d in enumerate(DILATIONS)]
    dproj = _rope_bwd(dqkv, du, pos, _tied(fr, dep_small))
    dh1 = _mm(dproj, w_in, mode='nt', name='mm_dh1', tk=1408, deps=dep)
    grad_x, dsc1, dsh1, dg_pre_mix = _prenorm_bwd(dx1, dh1, x, vec('g_pre_mix'), sc1)
    dsc1, dsh1, dg_pre_mix = dsc1[:1], dsh1[:1], dg_pre_mix[:1]
    dmod = jnp.concatenate([dsh1, dsc1, dgt1, dsh2, dsc2, dgt2], axis=0)
    dep = emit_late({'b_ada': dmod, 'g_pre_mix': dg_pre_mix})
    emit('w_in', _mm(h1, dproj, mode='tn', name='mm_dw_in', out_dtype=BF16, tn=1408, deps=dep))
    return loss[0, 0], grad_x


def _pack(d, names):
    flat = jnp.concatenate([jnp.pad(d[n].reshape(-1).astype(F32), (0, SEG[n] - SMALL_SIZES[n])) for n in names])
    return flat.reshape(-1, 128)


def _unpack(packed, names, shapes):
    out, off = {}, 0
    for n in names:
        out[n] = packed[off // 128:(off + SEG[n]) // 128].reshape(-1)[:SMALL_SIZES[n]].reshape(shapes[n])
        off += SEG[n]
    return out


def _shard_major(t, name):
    if name in ('w_in', 'w_out', 'w_mlp_in'):
        k, n = t.shape
        return t.reshape(k, N_DEV, n // N_DEV).transpose(1, 0, 2)
    k, n = t.shape
    return t.reshape(N_DEV, k // N_DEV, n)


def _from_shard_major(t, name):
    if name in ('w_in', 'w_out', 'w_mlp_in'):
        _, k, n = t.shape
        return t.transpose(1, 0, 2).reshape(k, N_DEV * n)
    _, k, n = t.shape
    return t.reshape(N_DEV * k, n)


def kernel(x, c, positions, w_ada, b_ada, g_pre_mix, g_post_mix, w_in, ssm_a_re, ssm_a_im, ssm_log_dt, ssm_b_re, ssm_b_im, ssm_c_re, ssm_c_im, ssm_d, w_glu, b_glu, g_attn_out, g_ssm_out, w_out, g_pre_mlp, g_post_mlp, w_mlp_in, w_mlp_out, loss_target, m_w_ada, m_b_ada, m_g_pre_mix, m_g_post_mix, m_w_in, m_ssm_a_re, m_ssm_a_im, m_ssm_log_dt, m_ssm_b_re, m_ssm_b_im, m_ssm_c_re, m_ssm_c_im, m_ssm_d, m_w_glu, m_b_glu, m_g_attn_out, m_g_ssm_out, m_w_out, m_g_pre_mlp, m_g_post_mlp, m_w_mlp_in, m_w_mlp_out, v_w_ada, v_b_ada, v_g_pre_mix, v_g_post_mix, v_w_in, v_ssm_a_re, v_ssm_a_im, v_ssm_log_dt, v_ssm_b_re, v_ssm_b_im, v_ssm_c_re, v_ssm_c_im, v_ssm_d, v_w_glu, v_b_glu, v_g_attn_out, v_g_ssm_out, v_w_out, v_g_pre_mlp, v_g_post_mlp, v_w_mlp_in, v_w_mlp_out):
    loc = dict(locals())
    W = {n: loc[n] for n in WEIGHTS}
    M = {n: loc['m_' + n] for n in WEIGHTS}
    V = {n: loc['v_' + n] for n in WEIGHTS}
    assert x.shape == (1, L, D) and w_in.shape == (1, D, INW // N_DEV), (x.shape, w_in.shape)

    cw = NMOD * D // N_DEV
    c_all, mod8 = _mod_exchange(c.reshape(1, 1, D), w_ada[0], b_ada.reshape(N_DEV, 1, cw))
    mod = mod8.reshape(NMOD, D)

    gather, after = {}, mod8
    for n in BIG:
        gather[n] = _push_start(W[n][0].astype(BF16), False, after, 'gather_start_' + n)
        after = gather[n][4]
    tokens = tuple(gather[n][4] for n in BIG)
    mod = _tied(mod, tokens)
    sp = {n: W[n][0] for n in SMALL}
    prep = _ssm_prep({**sp, 'ssm_a_re': _tied(sp['ssm_a_re'], tokens)})

    state_packs = {}

    def get_w(n, after):
        if n == 'w_mlp_out':
            tok = jnp.minimum(jnp.abs(after[0:8, 0:128].astype(F32)), 0.0)
            for key, src in (('w', W), ('m', M), ('v', V)):
                tied = {p: _tied(src[p], (tok,)) for p in SMALL}
                state_packs[key] = {'early': _pack(tied, SMALL_EARLY), 'late': _pack(tied, SMALL_LATE)}
            after = (after,) + tuple(state_packs[k][part] for k in 'wmv' for part in ('early', 'late'))
        g = _push_wait(gather[n], after, 'gather_wait_' + n)
        return g if n == 'w_mlp_in' else _from_shard_major(g, n)

    scatter = {}

    def emit(n, g):
        src = g if n == 'w_mlp_in' else _shard_major(g, n)
        scatter[n] = _push_start(src, True, src, 'scatter_start_' + n)
        return (scatter[n][4],)

    small_early = []

    def emit_small(d):
        pack = _pack(d, SMALL_EARLY)
        small_early.append(_push_start(pack, False, pack, 'small_start'))
        return (small_early[0][4],)

    out_g, out_d, out_m, out_v = {}, {}, {}, {}
    shapes = {n: W[n].shape[1:] for n in SMALL}

    def put(names, packs):
        for dst, packed in zip((out_g, out_d, out_m, out_v), packs):
            dst.update(_unpack(packed, names, shapes))

    late = []

    def emit_late(d):
        rows_all, *packs = _small_update(_pack(d, SMALL_LATE), *[state_packs[k]['late'] for k in 'wmv'])
        put(SMALL_LATE, packs)
        late.append(rows_all)
        return (rows_all,)

    loss, grad_x = _local_step(x[0], positions.reshape(L, 1), mod, loss_target[0], sp, prep, get_w, emit, emit_small,
                               emit_late)
    loss = lax.psum(loss, ('x', 'y', 'c'))

    me = 4 * lax.axis_index('x') + 2 * lax.axis_index('y') + lax.axis_index('c')
    dmod_all = late[0][:, :NMOD * D // 128].reshape(N_DEV, NMOD * D)
    dmod_cols = _tied(lax.dynamic_slice_in_dim(dmod_all, me * cw, cw, axis=1), (scatter['w_in'][4],))
    out_g['w_ada'], out_d['w_ada'], out_m['w_ada'], out_v['w_ada'] = _ada_update(
        c_all.reshape(N_DEV, D), dmod_cols, w_ada[0], m_w_ada[0], v_w_ada[0])

    parts = _push_wait(small_early[0], out_v['w_ada'], 'small_wait')
    packs = _small_sum_update(parts, *[state_packs[k]['early'] for k in 'wmv'])
    put(SMALL_EARLY, packs)

    after = packs[3]
    for n in ('w_mlp_out', 'w_mlp_in', 'w_out', 'w_glu', 'w_in'):
        parts, own = _push_wait(scatter[n], after, 'scatter_wait_' + n, merge=False)
        out_g[n], out_d[n], out_m[n], out_v[n] = _big_update(parts, own, W[n][0], M[n][0], V[n][0], 'update_' + n)
        after = out_v[n]

    lead = lambda t: t[None]
    return (loss, grad_x[None], *[lead(out_g[n]) for n in WEIGHTS], *[lead(out_d[n]) for n in WEIGHTS],
            *[lead(out_m[n]) for n in WEIGHTS], *[lead(out_v[n]) for n in WEIGHTS])
```

```python
import functools
import math

import jax
import jax.numpy as jnp
from jax import lax
from jax.experimental import pallas as pl
from jax.experimental.pallas import tpu as pltpu

F32 = jnp.float32
BF16 = jnp.bfloat16
HI = lax.Precision.HIGHEST
MESH = pl.DeviceIdType.MESH

N_DEV = 8
L = 4096
D = 2048
HEAD_DIM = 64
N_GROUPS = 3
DILATIONS = (1, 4, 16)
HEADS = 6
QW = N_GROUPS * HEADS * HEAD_DIM
KVW = HEADS * HEAD_DIM
ROT_DIM = 16
ROPE_THETA = 500000.0
BLK = 128
NBLK = L // BLK
SSMW = D - QW
SSM_P = 16
SSM_G = SSMW // SSM_P
SSM_N = 64
SSM_GN = SSM_G * SSM_N
CL_G = 8
N_CL = SSM_G // CL_G
CL_U = CL_G * SSM_P
CL_S = CL_G * SSM_N
INW = QW + 2 * KVW + SSMW
OUTW = KVW + SSMW
DFF = 4 * D
NMOD = 6
EPS = 1e-6
LR, B1, B2, AEPS, WD, STEP = 0.001, 0.9, 0.999, 1e-08, 0.01, 10

T_SCAN = 1024
MB = 2 ** 20

WEIGHTS = ['w_ada', 'b_ada', 'g_pre_mix', 'g_post_mix', 'w_in', 'ssm_a_re', 'ssm_a_im', 'ssm_log_dt',
           'ssm_b_re', 'ssm_b_im', 'ssm_c_re', 'ssm_c_im', 'ssm_d', 'w_glu', 'b_glu', 'g_attn_out',
           'g_ssm_out', 'w_out', 'g_pre_mlp', 'g_post_mlp', 'w_mlp_in', 'w_mlp_out']
BIG = ['w_in', 'w_glu', 'w_out', 'w_mlp_in', 'w_mlp_out']
SMALL = [n for n in WEIGHTS if n not in BIG and n != 'w_ada']
SMALL_SIZES = {'b_ada': NMOD * D, 'g_pre_mix': D, 'g_post_mix': D, 'ssm_a_re': SSM_GN, 'ssm_a_im': SSM_GN,
               'ssm_log_dt': SSM_G, 'ssm_b_re': SSM_GN * SSM_P, 'ssm_b_im': SSM_GN * SSM_P,
               'ssm_c_re': SSM_GN * SSM_P, 'ssm_c_im': SSM_GN * SSM_P, 'ssm_d': SSMW, 'b_glu': SSMW,
               'g_attn_out': KVW, 'g_ssm_out': SSMW, 'g_pre_mlp': D, 'g_post_mlp': D}
SEG = {n: -(-SMALL_SIZES[n] // 1024) * 1024 for n in SMALL}
SMALL_LATE = ['b_ada', 'g_pre_mix']
SMALL_EARLY = [n for n in SMALL if n not in SMALL_LATE]


def _params(sem=None, vmem_mb=None):
    kw = {}
    if sem is not None:
        kw['dimension_semantics'] = sem
    if vmem_mb is not None:
        kw['vmem_limit_bytes'] = vmem_mb * MB
    return pltpu.CompilerParams(**kw)


def _vec(n):
    return pl.BlockSpec((1, n), lambda *_: (0, 0))


def _rms(x):
    return lax.rsqrt(jnp.mean(x * x, axis=-1, keepdims=True) + EPS)


def _rms_bwd(dn, n, r):
    return r * (dn - n * jnp.mean(dn * n, axis=-1, keepdims=True))


def _vec8(n):
    return pl.BlockSpec((8, n), lambda *_: (0, 0))


def _colsum(x):
    return jnp.sum(x.reshape(-1, 8, x.shape[-1]), axis=0)


def _fold8(last, *refs):
    @pl.when(last)
    def _():
        for r in refs:
            r[...] = jnp.broadcast_to(jnp.sum(r[...], axis=0, keepdims=True), r.shape)


def _mm(a, b, *, mode, name, out_dtype=F32, tm=1024, tn=1024, tk=2048, epilogue=None, extra=None,
        b_sharded=False, out_sharded=False, deps=()):
    if mode == 'nn':
        M, K = a.shape
        dims = (((1,), (0,)), ((), ()))
        a_spec = pl.BlockSpec((tm, tk), lambda i, j, k: (i, k))
        if b_sharded:
            _, K2, per = b.shape
            N, q = N_DEV * per, per // tn
            b_spec = pl.BlockSpec((None, tk, tn), lambda i, j, k: (j // q, k, j % q))
        else:
            K2, N = b.shape
            b_spec = pl.BlockSpec((tk, tn), lambda i, j, k: (k, j))
    elif mode == 'nt':
        M, K = a.shape
        dims = (((1,), (1,)), ((), ()))
        a_spec = pl.BlockSpec((tm, tk), lambda i, j, k: (i, k))
        if b_sharded:
            _, N, per = b.shape
            K2 = N_DEV * per
            if tk > per:
                b_spec = pl.BlockSpec((tk // per, tn, per), lambda i, j, k: (k, j, 0))
            else:
                q = per // tk
                b_spec = pl.BlockSpec((None, tn, tk), lambda i, j, k: (k // q, j, k % q))
        else:
            N, K2 = b.shape
            b_spec = pl.BlockSpec((tn, tk), lambda i, j, k: (j, k))
    else:
        (K, M), (K2, N) = a.shape, b.shape
        dims = (((0,), (0,)), ((), ()))
        a_spec = pl.BlockSpec((tk, tm), lambda i, j, k: (k, i))
        b_spec = pl.BlockSpec((tk, tn), lambda i, j, k: (k, j))
    assert K == K2 and M % tm == 0 and N % tn == 0 and K % tk == 0, (name, a.shape, b.shape, tm, tn, tk)
    nk = K // tk
    o_spec = pl.BlockSpec((tm, tn), lambda i, j, k: (i, j))
    o_dims = (M, N)
    if out_sharded:
        qo = N // N_DEV // tn
        o_spec = pl.BlockSpec((None, tm, tn), lambda i, j, k: (j // qo, i, j % qo))
        o_dims = (N_DEV, M, N // N_DEV)
    n_out = 2 if epilogue == 'relu2' else 1
    n_extra = 1 if extra is not None else 0
    n_in = 2 + n_extra + len(deps)

    def body(*refs):
        a_ref, b_ref = refs[0], refs[1]
        x_refs = refs[2:2 + n_extra]
        o_refs = refs[n_in:n_in + n_out]
        acc = refs[-1]
        k = pl.program_id(2)

        if len(b_ref.shape) == 3:
            per_ = b_ref.shape[2]
            prod = sum(lax.dot_general(a_ref[:, s * per_:(s + 1) * per_], b_ref[s], dims, preferred_element_type=F32)
                       for s in range(b_ref.shape[0]))
        else:
            prod = lax.dot_general(a_ref[...], b_ref[...], dims, preferred_element_type=F32)

        def finish(r):
            if epilogue == 'relu2':
                o_refs[0][...] = r.astype(BF16)
                o_refs[1][...] = jnp.square(jnp.maximum(r, 0.0)).astype(BF16)
            elif epilogue == 'drelu2':
                pre = x_refs[0][...].astype(F32)
                o_refs[0][...] = (r * (2.0 * jnp.maximum(pre, 0.0))).astype(out_dtype)
            else:
                o_refs[0][...] = r.astype(out_dtype)

        if nk == 1:
            finish(prod)
        else:
            @pl.when(k == 0)
            def _():
                acc[...] = prod

            @pl.when((k > 0) & (k < nk - 1))
            def _():
                acc[...] += prod

            @pl.when(k == nk - 1)
            def _():
                finish(acc[...] + prod)

    if epilogue == 'relu2':
        out_shape = (jax.ShapeDtypeStruct((M, N), BF16), jax.ShapeDtypeStruct((M, N), BF16))
        out_specs = (o_spec, o_spec)
    else:
        out_shape = jax.ShapeDtypeStruct(o_dims, out_dtype)
        out_specs = o_spec
    args = (a, b) + ((extra,) if extra is not None else ()) + tuple(deps)
    in_specs = ([a_spec, b_spec] + ([o_spec] if extra is not None else [])
                + [pl.BlockSpec(memory_space=pl.ANY)] * len(deps))
    return pl.pallas_call(
        body, out_shape=out_shape, grid=(M // tm, N // tn, nk), in_specs=in_specs, out_specs=out_specs,
        scratch_shapes=[pltpu.VMEM((tm, tn) if nk > 1 else (8, 128), F32)], name=name,
        compiler_params=_params(('parallel', 'parallel', 'arbitrary'), 56))(*args)


TR = 256


def _rowspec(w=D):
    return pl.BlockSpec((TR, w), lambda i: (i, 0))


def _prenorm_fwd(x, g, sc, sh):
    def body(x_ref, g_ref, sc_ref, sh_ref, h_ref):
        xv = x_ref[...]
        n = xv * _rms(xv)
        h_ref[...] = ((n * g_ref[...]) * (1.0 + sc_ref[...]) + sh_ref[...]).astype(BF16)

    return pl.pallas_call(
        body, out_shape=jax.ShapeDtypeStruct((L, D), BF16), grid=(L // TR,),
        in_specs=[_rowspec(), _vec(D), _vec(D), _vec(D)], out_specs=_rowspec(), name='prenorm_fwd',
        compiler_params=_params(('parallel',), 40))(x, g, sc, sh)


def _postmix_fwd(x, mix, gpm, gt1, gpl, sc2, sh2):
    def body(x_ref, mix_ref, gpm_ref, gt1_ref, gpl_ref, sc2_ref, sh2_ref, x1_ref, h2_ref):
        mix_v = mix_ref[...]
        nm = mix_v * _rms(mix_v)
        x1 = x_ref[...] + gt1_ref[...] * (nm * gpm_ref[...])
        x1_ref[...] = x1
        n2 = x1 * _rms(x1)
        h2_ref[...] = ((n2 * gpl_ref[...]) * (1.0 + sc2_ref[...]) + sh2_ref[...]).astype(BF16)

    return pl.pallas_call(
        body, out_shape=(jax.ShapeDtypeStruct((L, D), F32), jax.ShapeDtypeStruct((L, D), BF16)), grid=(L // TR,),
        in_specs=[_rowspec(), _rowspec()] + [_vec(D)] * 5, out_specs=(_rowspec(), _rowspec()), name='postmix_fwd',
        compiler_params=_params(('parallel',), 40))(x, mix, gpm, gt1, gpl, sc2, sh2)


def _final_fwd_bwd(x1, y, tgt, g, gt2):
    def body(x1_ref, y_ref, t_ref, g_ref, gt2_ref, dx2_ref, dy_ref, loss_ref, dgt2_ref, dg_ref):
        @pl.when(pl.program_id(0) == 0)
        def _():
            loss_ref[...] = jnp.zeros_like(loss_ref)
            dgt2_ref[...] = jnp.zeros_like(dgt2_ref)
            dg_ref[...] = jnp.zeros_like(dg_ref)

        yv = y_ref[...]
        r = _rms(yv)
        n = yv * r
        ng = n * g_ref[...]
        x2 = x1_ref[...] + gt2_ref[...] * ng
        e = x2 - t_ref[...]
        loss_ref[...] += 0.5 * jnp.sum(jnp.mean(e * e, axis=-1, keepdims=True), axis=0, keepdims=True)
        dx2 = e * (1.0 / D)
        dx2_ref[...] = dx2
        dgt2_ref[...] += _colsum(dx2 * ng)
        dng = dx2 * gt2_ref[...]
        dg_ref[...] += _colsum(dng * n)
        dy_ref[...] = _rms_bwd(dng * g_ref[...], n, r).astype(BF16)
        _fold8(pl.program_id(0) == L // TR - 1, dgt2_ref, dg_ref)

    return pl.pallas_call(
        body,
        out_shape=(jax.ShapeDtypeStruct((L, D), F32), jax.ShapeDtypeStruct((L, D), BF16),
                   jax.ShapeDtypeStruct((8, 128), F32), jax.ShapeDtypeStruct((8, D), F32),
                   jax.ShapeDtypeStruct((8, D), F32)),
        grid=(L // TR,), in_specs=[_rowspec(), _rowspec(), _rowspec(), _vec(D), _vec(D)],
        out_specs=(_rowspec(), _rowspec(), _vec8(128), _vec8(D), _vec8(D)), name='final_fwd_bwd',
        compiler_params=_params(('arbitrary',), 40))(x1, y, tgt, g, gt2)


def _postmix_bwd(dx2, dh2, x1, mix, gpm, gt1, gpl, sc2):
    def body(dx2_ref, dh2_ref, x1_ref, mix_ref, gpm_ref, gt1_ref, gpl_ref, sc2_ref,
             dx1_ref, dmix_ref, dsc2_ref, dsh2_ref, dgpl_ref, dgt1_ref, dgpm_ref):
        @pl.when(pl.program_id(0) == 0)
        def _():
            for r_ in (dsc2_ref, dsh2_ref, dgpl_ref, dgt1_ref, dgpm_ref):
                r_[...] = jnp.zeros_like(r_)

        x1v = x1_ref[...]
        r2 = _rms(x1v)
        n2 = x1v * r2
        dh2v = dh2_ref[...].astype(F32)
        dsh2_ref[...] += _colsum(dh2v)
        dsc2_ref[...] += _colsum(dh2v * (n2 * gpl_ref[...]))
        t = dh2v * (1.0 + sc2_ref[...])
        dgpl_ref[...] += _colsum(t * n2)
        dx1 = dx2_ref[...] + _rms_bwd(t * gpl_ref[...], n2, r2)
        dx1_ref[...] = dx1
        mix_v = mix_ref[...]
        rm = _rms(mix_v)
        nm = mix_v * rm
        dgt1_ref[...] += _colsum(dx1 * (nm * gpm_ref[...]))
        u = dx1 * gt1_ref[...]
        dgpm_ref[...] += _colsum(u * nm)
        dmix_ref[...] = _rms_bwd(u * gpm_ref[...], nm, rm).astype(BF16)
        _fold8(pl.program_id(0) == L // TR - 1, dsc2_ref, dsh2_ref, dgpl_ref, dgt1_ref, dgpm_ref)

    vs = jax.ShapeDtypeStruct((8, D), F32)
    return pl.pallas_call(
        body, out_shape=(jax.ShapeDtypeStruct((L, D), F32), jax.ShapeDtypeStruct((L, D), BF16), vs, vs, vs, vs, vs),
        grid=(L // TR,), in_specs=[_rowspec()] * 4 + [_vec(D)] * 4,
        out_specs=(_rowspec(), _rowspec()) + (_vec8(D),) * 5, name='postmix_bwd',
        compiler_params=_params(('arbitrary',), 48))(dx2, dh2, x1, mix, gpm, gt1, gpl, sc2)


def _prenorm_bwd(dx1, dh1, x, g, sc1):
    def body(dx1_ref, dh1_ref, x_ref, g_ref, sc1_ref, dx_ref, dsc1_ref, dsh1_ref, dg_ref):
        @pl.when(pl.program_id(0) == 0)
        def _():
            for r_ in (dsc1_ref, dsh1_ref, dg_ref):
                r_[...] = jnp.zeros_like(r_)

        xv = x_ref[...]
        r = _rms(xv)
        n = xv * r
        dh = dh1_ref[...].astype(F32)
        dsh1_ref[...] += _colsum(dh)
        dsc1_ref[...] += _colsum(dh * (n * g_ref[...]))
        t = dh * (1.0 + sc1_ref[...])
        dg_ref[...] += _colsum(t * n)
        dx_ref[...] = dx1_ref[...] + _rms_bwd(t * g_ref[...], n, r)
        _fold8(pl.program_id(0) == L // TR - 1, dsc1_ref, dsh1_ref, dg_ref)

    vs = jax.ShapeDtypeStruct((8, D), F32)
    return pl.pallas_call(
        body, out_shape=(jax.ShapeDtypeStruct((L, D), F32), vs, vs, vs), grid=(L // TR,),
        in_specs=[_rowspec()] * 3 + [_vec(D)] * 2, out_specs=(_rowspec(),) + (_vec8(D),) * 3, name='prenorm_bwd',
        compiler_params=_params(('arbitrary',), 40))(dx1, dh1, x, g, sc1)


ROPE_W = QW + KVW
QKV_W = 3 * KVW
QK_SCALE = 0.125
NB_KV = KVW // 128


def _rope_rotate(xv, pos, fr, sign):
    ang = pos.astype(F32) * fr
    w = lax.broadcasted_iota(jnp.int32, (1, 128), 1) % HEAD_DIM
    cs = jnp.cos(ang)
    sn = jnp.sin(ang) * sign
    s1 = jnp.where(w < ROT_DIM // 2, -sn, 0.0)
    s2 = jnp.where((w >= ROT_DIM // 2) & (w < ROT_DIM), sn, 0.0)
    width = xv.shape[1]
    rep = width // 128
    cs, s1, s2 = jnp.tile(cs, (1, rep)), jnp.tile(s1, (1, rep)), jnp.tile(s2, (1, rep))
    hi = pltpu.roll(xv, width - ROT_DIM // 2, 1)
    lo = pltpu.roll(xv, ROT_DIM // 2, 1)
    return xv * cs + hi * s1 + lo * s2


def _sub_spec(d, rows, width):
    return pl.BlockSpec((d, rows // d, width), lambda i: (0, i, 0))


def _gather_rows(scr, blocks, r, d, rows):
    return jnp.concatenate([scr.at[j][pl.ds(r, rows // d, stride=d), :] for j in blocks], axis=1)


def _scatter_rows(scr, src_ref, d, rows):
    for r in range(d):
        for j in range(NB_KV):
            scr.at[j][pl.ds(r, rows // d, stride=d), :] = src_ref[r, :, j * 128:(j + 1) * 128]


def _token_order(scr):
    return jnp.concatenate([scr[j] for j in range(NB_KV)], axis=1)


def _rope_fwd(proj, pos, fr):
    nb = (ROPE_W + KVW) // 128

    def body(x_ref, pos_ref, fr_ref, o0_ref, o1_ref, o2_ref, scr):
        y = _rope_rotate(x_ref[:, 0:ROPE_W], pos_ref[...], fr_ref[...], 1.0)
        for j in range(ROPE_W // 128):
            scr[j] = y[:, j * 128:(j + 1) * 128] * (QK_SCALE if j < QW // 128 else 1.0)
        for j in range(ROPE_W // 128, nb):
            scr[j] = x_ref[:, j * 128:(j + 1) * 128]
        kv = list(range(QW // 128, nb))
        for g, (d, o_ref) in enumerate(zip(DILATIONS, (o0_ref, o1_ref, o2_ref))):
            blocks = list(range(g * NB_KV, (g + 1) * NB_KV)) + kv
            for r in range(d):
                o_ref[r] = _gather_rows(scr, blocks, r, d, TR).astype(BF16)

    return pl.pallas_call(
        body, out_shape=tuple(jax.ShapeDtypeStruct((d, L // d, QKV_W), BF16) for d in DILATIONS), grid=(L // TR,),
        in_specs=[_rowspec(ROPE_W + KVW), pl.BlockSpec((TR, 1), lambda i: (i, 0)), _vec(128)],
        out_specs=tuple(_sub_spec(d, TR, QKV_W) for d in DILATIONS),
        scratch_shapes=[pltpu.VMEM((nb, TR, 128), F32)], name='rope_fwd',
        compiler_params=_params(('parallel',), 40))(proj, pos, fr)


def _rope_bwd(dqkv, du, pos, fr):
    def body(*refs):
        grads = [refs[3 * g:3 * g + 3] for g in range(N_GROUPS)]
        du_ref, pos_ref, fr_ref, o_ref = refs[9:13]
        scrs = refs[13:]
        dq, dk, dv = [], None, None
        for g, d in enumerate(DILATIONS):
            parts = []
            for t in range(3):
                if d == 1:
                    parts.append(grads[g][t][0])
                else:
                    scr = scrs[3 * (g - 1) + t]
                    _scatter_rows(scr, grads[g][t], d, TR)
                    parts.append(_token_order(scr))
            dq.append(parts[0])
            dk = parts[1] if dk is None else dk + parts[1]
            dv = parts[2] if dv is None else dv + parts[2]
        x = jnp.concatenate(dq + [dk], axis=1)
        o_ref[:, 0:ROPE_W] = _rope_rotate(x, pos_ref[...], fr_ref[...], -1.0).astype(BF16)
        o_ref[:, ROPE_W:ROPE_W + KVW] = dv.astype(BF16)
        o_ref[:, ROPE_W + KVW:INW] = du_ref[...].astype(BF16)

    flat = [a for grp in dqkv for a in grp]
    in_specs = [_sub_spec(d, TR, KVW) for d in DILATIONS for _ in range(3)]
    in_specs += [_rowspec(SSMW), pl.BlockSpec((TR, 1), lambda i: (i, 0)), _vec(128)]
    return pl.pallas_call(
        body, out_shape=jax.ShapeDtypeStruct((L, INW), BF16), grid=(L // TR,), in_specs=in_specs,
        out_specs=_rowspec(INW), scratch_shapes=[pltpu.VMEM((NB_KV, TR, 128), F32)] * 6, name='rope_bwd',
        compiler_params=_params(('parallel',), 48))(*flat, du, pos, fr)


def _attn_mask(nbs, b):
    first = (b & (nbs - 1)) == 0
    qi = lax.broadcasted_iota(jnp.int32, (BLK, 2 * BLK), 0)
    kj = lax.broadcasted_iota(jnp.int32, (BLK, 2 * BLK), 1)
    dist = qi + BLK - kj
    return (dist >= 0) & (dist <= BLK) & ((kj >= BLK) | jnp.logical_not(first))


def _qkv_specs():
    cur = lambda col: pl.BlockSpec((BLK, KVW), lambda b: (b, col))
    prev = lambda col: pl.BlockSpec((BLK, KVW), lambda b: (jnp.maximum(b - 1, 0), col))
    return [cur(0), prev(1), cur(1), prev(2), cur(2)]


_ROWS = pl.BlockSpec((BLK, KVW), lambda b: (b, 0))
NEG = -1e30
NT_DIMS = (((1,), (1,)), ((), ()))
TN_DIMS = (((0,), (0,)), ((), ()))


def _attn_fwd(qkv, d):
    nbs = L // d // BLK

    def body(q_ref, kp_ref, kc_ref, vp_ref, vc_ref, o_ref, lse_ref):
        valid = _attn_mask(nbs, pl.program_id(0))
        heads = [slice(h * HEAD_DIM, (h + 1) * HEAD_DIM) for h in range(HEADS)]
        kcs = [jnp.concatenate([kp_ref[:, hs], kc_ref[:, hs]], axis=0) for hs in heads]
        vcs = [jnp.concatenate([vp_ref[:, hs], vc_ref[:, hs]], axis=0) for hs in heads]
        ss = [lax.dot_general(q_ref[:, hs], kc, NT_DIMS, preferred_element_type=F32) for hs, kc in zip(heads, kcs)]
        ps, ls, lses = [], [], []
        for s in ss:
            s = jnp.where(valid, s, NEG)
            m = jnp.max(s, axis=-1, keepdims=True)
            p = jnp.exp(s - m)
            l = jnp.sum(p, axis=-1, keepdims=True)
            ps.append(p.astype(BF16))
            ls.append(l)
            lses.append(jnp.broadcast_to(m + jnp.log(l), (BLK, HEAD_DIM)))
        outs = [jnp.dot(p, vc, preferred_element_type=F32) / l for p, vc, l in zip(ps, vcs, ls)]
        o_ref[...] = jnp.concatenate(outs, axis=1)
        lse_ref[...] = jnp.concatenate(lses, axis=1)

    sh = jax.ShapeDtypeStruct((L, KVW), F32)
    q2 = qkv.reshape(L, QKV_W)
    o, lse = pl.pallas_call(
        body, out_shape=(sh, sh), grid=(NBLK,), in_specs=_qkv_specs(), out_specs=(_ROWS, _ROWS),
        name='attn_fwd_d%d' % d, compiler_params=_params(('parallel',), 32))(q2, q2, q2, q2, q2)
    return o.reshape(d, L // d, KVW), lse.reshape(d, L // d, KVW)


def _attn_bwd(qkv, o, lse, do, dlse, d):
    nbs = L // d // BLK

    def body(q_ref, kp_ref, kc_ref, vp_ref, vc_ref, o_ref, lse_ref, do_ref, dlse_ref, dq_ref, dk_ref, dv_ref):
        b = pl.program_id(0)

        @pl.when(b == 0)
        def _():
            dk_ref[...] = jnp.zeros_like(dk_ref)
            dv_ref[...] = jnp.zeros_like(dv_ref)

        valid = _attn_mask(nbs, b)
        prev0 = pl.multiple_of(jnp.maximum(b - 1, 0) * BLK, BLK)
        cur0 = pl.multiple_of(b * BLK, BLK)
        heads = [slice(h * HEAD_DIM, (h + 1) * HEAD_DIM) for h in range(HEADS)]
        qs = [q_ref[:, hs] for hs in heads]
        kcs = [jnp.concatenate([kp_ref[:, hs], kc_ref[:, hs]], axis=0) for hs in heads]
        vcs = [jnp.concatenate([vp_ref[:, hs], vc_ref[:, hs]], axis=0) for hs in heads]
        dos = [do_ref[:, hs] for hs in heads]
        do_bs = [t.astype(BF16) for t in dos]
        ss = [lax.dot_general(q, kc, NT_DIMS, preferred_element_type=F32) for q, kc in zip(qs, kcs)]
        dps = [lax.dot_general(do_b, vc, NT_DIMS, preferred_element_type=F32) for do_b, vc in zip(do_bs, vcs)]
        p_bs, ds_bs = [], []
        for h, hs in enumerate(heads):
            s = jnp.where(valid, ss[h], NEG)
            p = jnp.exp(s - lse_ref[:, h * HEAD_DIM:h * HEAD_DIM + 1])
            delta = jnp.sum(dos[h] * o_ref[:, hs], axis=-1, keepdims=True)
            ds = p * (dps[h] - delta + dlse_ref[:, h * HEAD_DIM:h * HEAD_DIM + 1])
            p_bs.append(p.astype(BF16))
            ds_bs.append(ds.astype(BF16))
        dqs = [jnp.dot(ds_b, kc, preferred_element_type=F32) for ds_b, kc in zip(ds_bs, kcs)]
        dks = [lax.dot_general(ds_b, q, TN_DIMS, preferred_element_type=F32) for ds_b, q in zip(ds_bs, qs)]
        dvs = [lax.dot_general(p_b, do_b, TN_DIMS, preferred_element_type=F32) for p_b, do_b in zip(p_bs, do_bs)]
        dq_ref[...] = jnp.concatenate(dqs, axis=1) * QK_SCALE
        dkc, dvc = jnp.concatenate(dks, axis=1), jnp.concatenate(dvs, axis=1)
        dk_ref[pl.ds(prev0, BLK), :] += dkc[:BLK]
        dv_ref[pl.ds(prev0, BLK), :] += dvc[:BLK]
        dk_ref[pl.ds(cur0, BLK), :] += dkc[BLK:]
        dv_ref[pl.ds(cur0, BLK), :] += dvc[BLK:]

    sh = jax.ShapeDtypeStruct((L, KVW), F32)
    whole = pl.BlockSpec((L, KVW), lambda b: (0, 0))
    q2 = qkv.reshape(L, QKV_W)
    flat = lambda t: t.reshape(L, KVW)
    outs = pl.pallas_call(
        body, out_shape=(sh, sh, sh), grid=(NBLK,), in_specs=_qkv_specs() + [_ROWS] * 4,
        out_specs=(_ROWS, whole, whole), name='attn_bwd_d%d' % d,
        compiler_params=_params(('arbitrary',), 48))(q2, q2, q2, q2, q2, flat(o), flat(lse), flat(do), flat(dlse))
    return tuple(t.reshape(d, L // d, KVW) for t in outs)


TC = 512


def _combine_weights(l0, l1, l2):
    m = jnp.maximum(jnp.maximum(l0, l1), l2)
    e0, e1, e2 = jnp.exp(l0 - m), jnp.exp(l1 - m), jnp.exp(l2 - m)
    z = e0 + e1 + e2
    return e0 / z, e1 / z, e2 / z


def _load_groups(refs, scrs):
    out = [refs[0][0]]
    for g in (1, 2):
        _scatter_rows(scrs[g - 1], refs[g], DILATIONS[g], TC)
        out.append(_token_order(scrs[g - 1]))
    return out


def _combine_fwd(os_, lses, g):
    def body(o0, o1, o2, l0, l1, l2, g_ref, att_ref, *scrs):
        ov = _load_groups((o0, o1, o2), scrs[0:2])
        lv = _load_groups((l0, l1, l2), scrs[2:4])
        w0, w1, w2 = _combine_weights(*lv)
        a = w0 * ov[0] + w1 * ov[1] + w2 * ov[2]
        att_ref[...] = ((a * _rms(a)) * g_ref[...]).astype(BF16)

    subs = [_sub_spec(d, TC, KVW) for d in DILATIONS]
    return pl.pallas_call(
        body, out_shape=jax.ShapeDtypeStruct((L, KVW), BF16), grid=(L // TC,), in_specs=subs + subs + [_vec(KVW)],
        out_specs=pl.BlockSpec((TC, KVW), lambda i: (i, 0)),
        scratch_shapes=[pltpu.VMEM((NB_KV, TC, 128), F32)] * 4, name='combine_fwd',
        compiler_params=_params(('parallel',), 40))(*os_, *lses, g)


def _combine_bwd(dcat, os_, lses, g, head_ones):
    def body(datt_ref, o0, o1, o2, l0, l1, l2, g_ref, e_ref, do0, do1, do2, dl0, dl1, dl2, dg_ref, *scrs):
        @pl.when(pl.program_id(0) == 0)
        def _():
            dg_ref[...] = jnp.zeros_like(dg_ref)

        ov = _load_groups((o0, o1, o2), scrs[0:2])
        lv = _load_groups((l0, l1, l2), scrs[2:4])
        ws = _combine_weights(*lv)
        a = ws[0] * ov[0] + ws[1] * ov[1] + ws[2] * ov[2]
        r = _rms(a)
        n = a * r
        dv = datt_ref[...].astype(F32)
        dg_ref[...] += _colsum(dv * n)
        da = _rms_bwd(dv * g_ref[...], n, r)
        e_b = e_ref[...].astype(BF16)

        def head_sum(t):
            hi = t.astype(BF16)
            lo = (t - hi.astype(F32)).astype(BF16)
            return jnp.dot(hi, e_b, preferred_element_type=F32) + jnp.dot(lo, e_b, preferred_element_type=F32)

        dws = [head_sum(da * ov[i]) for i in range(3)]
        dbar = ws[0] * dws[0] + ws[1] * dws[1] + ws[2] * dws[2]
        scr = scrs[4]
        for i, (d, do_ref, dl_ref) in enumerate(zip(DILATIONS, (do0, do1, do2), (dl0, dl1, dl2))):
            for val, out_ref in ((ws[i] * da, do_ref), (ws[i] * (dws[i] - dbar), dl_ref)):
                if d == 1:
                    out_ref[0] = val
                else:
                    for j in range(NB_KV):
                        scr[j] = val[:, j * 128:(j + 1) * 128]
                    for rr in range(d):
                        out_ref[rr] = _gather_rows(scr, range(NB_KV), rr, d, TC)
        _fold8(pl.program_id(0) == L // TC - 1, dg_ref)

    subs = [_sub_spec(d, TC, KVW) for d in DILATIONS]
    shs = tuple(jax.ShapeDtypeStruct((d, L // d, KVW), F32) for d in DILATIONS)
    outs = pl.pallas_call(
        body, out_shape=shs + shs + (jax.ShapeDtypeStruct((8, KVW), F32),), grid=(L // TC,),
        in_specs=[pl.BlockSpec((TC, KVW), lambda i: (i, 0))] + subs + subs + [_vec(KVW),
                                                                              pl.BlockSpec((KVW, KVW), lambda i: (0, 0))],
        out_specs=tuple(subs) + tuple(subs) + (_vec8(KVW),),
        scratch_shapes=[pltpu.VMEM((NB_KV, TC, 128), F32)] * 5, name='combine_bwd',
        compiler_params=_params(('arbitrary',), 48))(dcat, *os_, *lses, g, head_ones)
    return outs[0:3], outs[3:6], outs[6]


def _ssm_disc(ar, ai, ldt):
    dt = jnp.exp(ldt)
    zr, zi = ar * dt, ai * dt
    ez = jnp.exp(zr)
    A_r, A_i = ez * jnp.cos(zi), ez * jnp.sin(zi)
    den = ar * ar + ai * ai
    xr, xi = A_r - 1.0, A_i
    cr = (xr * ar + xi * ai) / den
    ci = (xi * ar - xr * ai) / den
    return dt, zr, zi, A_r, A_i, den, cr, ci


def _ssm_pre(ar, ai, ldt, br, bi):
    def body(ar_ref, ai_ref, ldt_ref, br_ref, bi_ref, bbr_ref, bbi_ref, pwr_ref, pwi_ref):
        _, zr, zi, _, _, _, cr, ci = _ssm_disc(ar_ref[...], ai_ref[...], ldt_ref[...])
        bbr_ref[...] = cr * br_ref[...] - ci * bi_ref[...]
        bbi_ref[...] = cr * bi_ref[...] + ci * br_ref[...]
        k = (lax.broadcasted_iota(jnp.int32, (1, 8), 1) + 1).astype(F32)
        ek = jnp.exp(zr * k)
        pwr_ref[...] = ek * jnp.cos(zi * k)
        pwi_ref[...] = ek * jnp.sin(zi * k)

    s16 = jax.ShapeDtypeStruct((SSM_GN, SSM_P), F32)
    s8 = jax.ShapeDtypeStruct((SSM_GN, 8), F32)
    return pl.pallas_call(body, out_shape=(s16, s16, s8, s8), name='ssm_pre',
                          compiler_params=_params(None, 40))(ar, ai, ldt, br, bi)


def _ssm_post(ar, ai, ldt, br, bi, gar, gai, gbr, gbi, sel):
    def body(ar_ref, ai_ref, ldt_ref, br_ref, bi_ref, gar_ref, gai_ref, gbr_ref, gbi_ref, sel_ref,
             dar_ref, dai_ref, dbr_ref, dbi_ref, dldt_ref):
        a_r, a_i = ar_ref[...], ai_ref[...]
        dt, _, _, A_r, A_i, den, cr, ci = _ssm_disc(a_r, a_i, ldt_ref[...])
        b_r, b_i, g_br, g_bi = br_ref[...], bi_ref[...], gbr_ref[...], gbi_ref[...]
        gcr = jnp.sum(g_br * b_r + g_bi * b_i, axis=-1, keepdims=True)
        gci = jnp.sum(g_bi * b_r - g_br * b_i, axis=-1, keepdims=True)
        dbr_ref[...] = g_br * cr + g_bi * ci
        dbi_ref[...] = g_bi * cr - g_br * ci
        g_ar = gar_ref[...] + (gcr * a_r - gci * a_i) / den
        g_ai = gai_ref[...] + (gcr * a_i + gci * a_r) / den
        qr = (cr * a_r + ci * a_i) / den
        qi = (ci * a_r - cr * a_i) / den
        glr = -(gcr * qr + gci * qi)
        gli = -(gci * qr - gcr * qi)
        gzr = g_ar * A_r + g_ai * A_i
        gzi = g_ai * A_r - g_ar * A_i
        dar_ref[...] = glr + gzr * dt
        dai_ref[...] = gli + gzi * dt
        gdt = (gzr * a_r + gzi * a_i) * dt
        dldt_ref[...] = jnp.dot(sel_ref[...], jnp.broadcast_to(gdt, (SSM_GN, 128)),
                                preferred_element_type=F32, precision=HI)

    s1 = jax.ShapeDtypeStruct((SSM_GN, 1), F32)
    s16 = jax.ShapeDtypeStruct((SSM_GN, SSM_P), F32)
    return pl.pallas_call(body, out_shape=(s1, s1, s16, s16, jax.ShapeDtypeStruct((SSM_G, 128), F32)),
                          name='ssm_post', compiler_params=_params(None, 48))(
                              ar, ai, ldt, br, bi, gar, gai, gbr, gbi, sel)


SCAN_CH = 8


def _scan_fwd_tiles(s_ref, pw, carry):
    pwr, pwi = pw[:, :CL_S], pw[:, CL_S:]
    row = lax.broadcasted_iota(jnp.int32, (8, CL_S), 0)
    steps = [(k, jnp.where(row >= k, pwr[k - 1:k], 0.0), jnp.where(row >= k, pwi[k - 1:k], 0.0)) for k in (1, 2, 4)]
    rows = 8 * SCAN_CH

    def chunk(i, c):
        cr, ci = c
        r0 = pl.multiple_of(i * rows, rows)
        xr = s_ref[pl.ds(r0, rows), 0:CL_S].reshape(SCAN_CH, 8, CL_S)
        xi = s_ref[pl.ds(r0, rows), CL_S:2 * CL_S].reshape(SCAN_CH, 8, CL_S)
        for k, pr, pi in steps:
            sr, si = pltpu.roll(xr, k, 1), pltpu.roll(xi, k, 1)
            xr, xi = xr + pr * sr - pi * si, xi + pr * si + pi * sr
        for j in range(SCAN_CH):
            tr = xr[j] + pwr * cr - pwi * ci
            ti = xi[j] + pwr * ci + pwi * cr
            s_ref[pl.ds(r0 + 8 * j, 8), 0:CL_S] = tr
            s_ref[pl.ds(r0 + 8 * j, 8), CL_S:2 * CL_S] = ti
            cr, ci = tr[7:8], ti[7:8]
        return cr, ci

    return lax.fori_loop(0, T_SCAN // rows, chunk, (carry[:, :CL_S], carry[:, CL_S:]))


def _scan_bwd_tiles(l_ref, pw, carry):
    pwr, pwi = pw[:, :CL_S], pw[:, CL_S:]
    rpr = jnp.concatenate([pwr[7 - r:8 - r] for r in range(8)], axis=0)
    rpi = jnp.concatenate([pwi[7 - r:8 - r] for r in range(8)], axis=0)
    row = lax.broadcasted_iota(jnp.int32, (8, CL_S), 0)
    steps = [(k, jnp.where(row < 8 - k, pwr[k - 1:k], 0.0), jnp.where(row < 8 - k, pwi[k - 1:k], 0.0))
             for k in (1, 2, 4)]
    rows = 8 * SCAN_CH
    nc = T_SCAN // rows

    def chunk(i, c):
        cr, ci = c
        r0 = pl.multiple_of((nc - 1 - i) * rows, rows)
        xr = l_ref[pl.ds(r0, rows), 0:CL_S].reshape(SCAN_CH, 8, CL_S)
        xi = l_ref[pl.ds(r0, rows), CL_S:2 * CL_S].reshape(SCAN_CH, 8, CL_S)
        for k, pr, pi in steps:
            sr, si = pltpu.roll(xr, 8 - k, 1), pltpu.roll(xi, 8 - k, 1)
            xr, xi = xr + pr * sr + pi * si, xi + pr * si - pi * sr
        for j in reversed(range(SCAN_CH)):
            tr = xr[j] + rpr * cr + rpi * ci
            ti = xi[j] + rpr * ci - rpi * cr
            l_ref[pl.ds(r0 + 8 * j, 8), 0:CL_S] = tr
            l_ref[pl.ds(r0 + 8 * j, 8), CL_S:2 * CL_S] = ti
            cr, ci = tr[0:1], ti[0:1]
        return cr, ci

    return lax.fori_loop(0, nc, chunk, (carry[:, :CL_S], carry[:, CL_S:]))


NT_SCAN = L // T_SCAN


def _cl_spec(r, c):
    return pl.BlockSpec((None, r, c), lambda c_, t: (c_, 0, 0))


def _ssm_fwd(u, bm, cm, pw, dvec, u_off=0):
    def body(u_ref, bm_ref, cm_ref, pw_ref, d_ref, y_ref, bnd_ref, s_ref, carry_ref):
        @pl.when(pl.program_id(1) == 0)
        def _():
            carry_ref[...] = jnp.zeros_like(carry_ref)

        bnd_ref[...] = carry_ref[...]
        uv = u_ref[...]
        s_ref[...] = jnp.dot(uv.astype(BF16), bm_ref[...], preferred_element_type=F32)
        cr, ci = _scan_fwd_tiles(s_ref, pw_ref[...], carry_ref[...])
        carry_ref[...] = jnp.concatenate([cr, ci], axis=1)
        y_ref[...] = jnp.dot(s_ref[...].astype(BF16), cm_ref[...], preferred_element_type=F32) + d_ref[...] * uv

    return pl.pallas_call(
        body,
        out_shape=(jax.ShapeDtypeStruct((L, SSMW), F32), jax.ShapeDtypeStruct((N_CL, NT_SCAN, 1, 2 * CL_S), F32),
                   jax.ShapeDtypeStruct((L, N_CL * 2 * CL_S), F32)),
        grid=(N_CL, NT_SCAN),
        in_specs=[pl.BlockSpec((T_SCAN, CL_U), lambda c, t: (t, c + u_off)),
                  _cl_spec(CL_U, 2 * CL_S), _cl_spec(2 * CL_S, CL_U),
                  pl.BlockSpec((None, 8, 2 * CL_S), lambda c, t: (c, 0, 0)),
                  pl.BlockSpec((1, CL_U), lambda c, t: (0, c))],
        out_specs=(pl.BlockSpec((T_SCAN, CL_U), lambda c, t: (t, c)),
                   pl.BlockSpec((None, None, 1, 2 * CL_S), lambda c, t: (c, t, 0, 0)),
                   pl.BlockSpec((T_SCAN, 2 * CL_S), lambda c, t: (t, c))),
        scratch_shapes=[pltpu.VMEM((1, 2 * CL_S), F32)],
        name='ssm_fwd', compiler_params=_params(('arbitrary', 'arbitrary'), 40))(u, bm, cm, pw, dvec)


def _ssm_bwd(u, dy, states, bmt, cmt, pw, dvec, bnd, u_off=0):
    rev = lambda t: NT_SCAN - 1 - t

    def body(u_ref, dy_ref, s_ref, bmt_ref, cmt_ref, pw_ref, d_ref, bnd_ref,
             du_ref, dbm_ref, dcm_ref, da_ref, dd_ref, l_ref, carry_ref):
        @pl.when(pl.program_id(1) == 0)
        def _():
            carry_ref[...] = jnp.zeros_like(carry_ref)
            dbm_ref[...] = jnp.zeros_like(dbm_ref)
            dcm_ref[...] = jnp.zeros_like(dcm_ref)
            da_ref[...] = jnp.zeros_like(da_ref)
            dd_ref[...] = jnp.zeros_like(dd_ref)

        uv, dyv, pw = u_ref[...], dy_ref[...], pw_ref[...]
        dy_b = dyv.astype(BF16)
        entry = bnd_ref[...]
        l_ref[...] = jnp.dot(dy_b, cmt_ref[...], preferred_element_type=F32)
        cr, ci = _scan_bwd_tiles(l_ref, pw, carry_ref[...])
        carry_ref[...] = jnp.concatenate([cr, ci], axis=1)
        sv, lv = s_ref[...], l_ref[...]
        lv_b = lv.astype(BF16)
        du_ref[...] = dyv * d_ref[...] + jnp.dot(lv_b, bmt_ref[...], preferred_element_type=F32)
        dbm_ref[...] += lax.dot_general(uv.astype(BF16), lv_b, TN_DIMS, preferred_element_type=F32)
        dcm_ref[...] += lax.dot_general(sv.astype(BF16), dy_b, TN_DIMS, preferred_element_type=F32)
        dd_ref[...] += _colsum(dyv * uv)
        row = lax.broadcasted_iota(jnp.int32, (T_SCAN, 2 * CL_S), 0)
        sp = jnp.where(row == 0, entry, pltpu.roll(sv, 1, 0))
        spr, spi = sp[:, :CL_S], sp[:, CL_S:]
        lr, li = lv[:, :CL_S], lv[:, CL_S:]
        da_ref[:, 0:CL_S] += _colsum(lr * spr + li * spi)
        da_ref[:, CL_S:2 * CL_S] += _colsum(li * spr - lr * spi)
        _fold8(pl.program_id(1) == NT_SCAN - 1, da_ref, dd_ref)

    return pl.pallas_call(
        body,
        out_shape=(jax.ShapeDtypeStruct((L, SSMW), F32), jax.ShapeDtypeStruct((N_CL, CL_U, 2 * CL_S), F32),
                   jax.ShapeDtypeStruct((N_CL, 2 * CL_S, CL_U), F32), jax.ShapeDtypeStruct((N_CL, 8, 2 * CL_S), F32),
                   jax.ShapeDtypeStruct((8, SSMW), F32)),
        grid=(N_CL, NT_SCAN),
        in_specs=[pl.BlockSpec((T_SCAN, CL_U), lambda c, t: (rev(t), c + u_off)),
                  pl.BlockSpec((T_SCAN, CL_U), lambda c, t: (rev(t), c)),
                  pl.BlockSpec((T_SCAN, 2 * CL_S), lambda c, t: (rev(t), c)),
                  pl.BlockSpec((None, 2 * CL_S, CL_U), lambda c, t: (c, 0, 0)),
                  pl.BlockSpec((None, CL_U, 2 * CL_S), lambda c, t: (c, 0, 0)),
                  pl.BlockSpec((None, 8, 2 * CL_S), lambda c, t: (c, 0, 0)),
                  pl.BlockSpec((1, CL_U), lambda c, t: (0, c)),
                  pl.BlockSpec((None, None, 1, 2 * CL_S), lambda c, t: (c, rev(t), 0, 0))],
        out_specs=(pl.BlockSpec((T_SCAN, CL_U), lambda c, t: (rev(t), c)),
                   pl.BlockSpec((None, CL_U, 2 * CL_S), lambda c, t: (c, 0, 0)),
                   pl.BlockSpec((None, 2 * CL_S, CL_U), lambda c, t: (c, 0, 0)),
                   pl.BlockSpec((None, 8, 2 * CL_S), lambda c, t: (c, 0, 0)),
                   pl.BlockSpec((8, CL_U), lambda c, t: (0, c))),
        scratch_shapes=[pltpu.VMEM((T_SCAN, 2 * CL_S), F32), pltpu.VMEM((1, 2 * CL_S), F32)],
        name='ssm_bwd', compiler_params=_params(('arbitrary', 'arbitrary'), 48))(u, dy, states, bmt, cmt, pw, dvec, bnd)


GELU_C = math.sqrt(2.0 / math.pi)
GELU_K = 0.044715


def _gelu_parts(x):
    t = jnp.tanh(GELU_C * (x + GELU_K * (x * x * x)))
    return x * (0.5 * (1.0 + t)), t


def _glu_fwd(ypre, wglu, bglu, gs):
    def body(y_ref, w_ref, b_ref, g_ref, o_ref):
        yg, _ = _gelu_parts(y_ref[...])
        z = jnp.dot(yg.astype(BF16), w_ref[...], preferred_element_type=F32) + b_ref[...]
        s = yg * jax.nn.sigmoid(z)
        o_ref[...] = ((s * _rms(s)) * g_ref[...]).astype(BF16)

    return pl.pallas_call(
        body, out_shape=jax.ShapeDtypeStruct((L, SSMW), BF16), grid=(L // TR,),
        in_specs=[_rowspec(SSMW), pl.BlockSpec((SSMW, SSMW), lambda i: (0, 0)), _vec(SSMW), _vec(SSMW)],
        out_specs=_rowspec(SSMW), name='glu_fwd', compiler_params=_params(('parallel',), 32))(ypre, wglu, bglu, gs)


def _glu_bwd(ypre, dsn, wglu, bglu, gs):
    def body(y_ref, d_ref, w_ref, b_ref, g_ref, dy_ref, dw_ref, db_ref, dg_ref):
        @pl.when(pl.program_id(0) == 0)
        def _():
            dw_ref[...] = jnp.zeros_like(dw_ref)
            db_ref[...] = jnp.zeros_like(db_ref)
            dg_ref[...] = jnp.zeros_like(dg_ref)

        xv = y_ref[...]
        yg, t = _gelu_parts(xv)
        yg_b = yg.astype(BF16)
        z = jnp.dot(yg_b, w_ref[...], preferred_element_type=F32) + b_ref[...]
        sg = jax.nn.sigmoid(z)
        s = yg * sg
        r = _rms(s)
        n = s * r
        dv = d_ref[:, d_ref.shape[1] - SSMW:].astype(F32)
        dg_ref[...] += _colsum(dv * n)
        ds = _rms_bwd(dv * g_ref[...], n, r)
        dz =(ds * yg) * (sg * (1.0 - sg))
        dz_b = dz.astype(BF16)
        db_ref[...] += _colsum(dz)
        dw_ref[...] += lax.dot_general(yg_b, dz_b, TN_DIMS, preferred_element_type=F32)
        dyg = ds * sg + lax.dot_general(dz_b, w_ref[...], NT_DIMS, preferred_element_type=F32)
        dgelu = 0.5 * (1.0 + t) + (0.5 * xv) * (1.0 - t * t) * (GELU_C * (1.0 + 3.0 * GELU_K * (xv * xv)))
        dy_ref[...] = dyg * dgelu
        _fold8(pl.program_id(0) == L // TR - 1, db_ref, dg_ref)

    vs = jax.ShapeDtypeStruct((8, SSMW), F32)
    return pl.pallas_call(
        body, out_shape=(jax.ShapeDtypeStruct((L, SSMW), F32), jax.ShapeDtypeStruct((SSMW, SSMW), F32), vs, vs),
        grid=(L // TR,),
        in_specs=[_rowspec(SSMW), _rowspec(dsn.shape[1]), pl.BlockSpec((SSMW, SSMW), lambda i: (0, 0)), _vec(SSMW),
                  _vec(SSMW)],
        out_specs=(_rowspec(SSMW), pl.BlockSpec((SSMW, SSMW), lambda i: (0, 0)), _vec8(SSMW), _vec8(SSMW)),
        name='glu_bwd', compiler_params=_params(('arbitrary',), 40))(ypre, dsn, wglu, bglu, gs)


def _me():
    return lax.axis_index('x'), lax.axis_index('y'), lax.axis_index('c')


def _my_index():
    return 4 * lax.axis_index('x') + 2 * lax.axis_index('y') + lax.axis_index('c')


def _peer(k):
    x, y, c = _me()
    px = 1 - x if k & 4 else x
    py = 1 - y if k & 2 else y
    pc = 1 - c if k & 1 else c
    return (px, py, pc), 4 * px + 2 * py + pc


def _mod_exchange(c_row, w_ada, b_ada8, deps=()):
    cw = NMOD * D // N_DEV

    def body(c_ref, w_ref, b_ref, *rest):
        call_ref, mod_ref, part_ref, send_sems, recv_sems = rest[len(deps):]
        x, y, c = _me()
        me = 4 * x + 2 * y + c
        call_ref[me] = c_ref[0]
        sends = []
        for k in range(1, N_DEV):
            peer, _ = _peer(k)
            cp = pltpu.make_async_remote_copy(src_ref=c_ref.at[0], dst_ref=call_ref.at[me], send_sem=send_sems.at[0, k - 1],
                                              recv_sem=recv_sems.at[0, k - 1], device_id=peer, device_id_type=MESH)
            cp.start()
            sends.append(cp)
        for k in range(1, N_DEV):
            peer, pidx = _peer(k)
            pltpu.make_async_remote_copy(src_ref=c_ref.at[0], dst_ref=call_ref.at[pidx], send_sem=send_sems.at[0, k - 1],
                                         recv_sem=recv_sems.at[0, k - 1], device_id=peer, device_id_type=MESH).wait_recv()
        for cp in sends:
            cp.wait_send()
        cv = call_ref[...].reshape(N_DEV, D)
        part = jnp.dot(cv * jax.nn.sigmoid(cv), w_ref[...], preferred_element_type=F32, precision=HI)
        part_ref[...] = part.reshape(N_DEV, 1, cw)
        mod_ref[me] = part_ref[me]
        sends = []
        for k in range(1, N_DEV):
            peer, pidx = _peer(k)
            cp = pltpu.make_async_remote_copy(src_ref=part_ref.at[pidx], dst_ref=mod_ref.at[me], send_sem=send_sems.at[1, k - 1],
                                              recv_sem=recv_sems.at[1, k - 1], device_id=peer, device_id_type=MESH)
            cp.start()
            sends.append(cp)
        for k in range(1, N_DEV):
            peer, pidx = _peer(k)
            pltpu.make_async_remote_copy(src_ref=part_ref.at[pidx], dst_ref=mod_ref.at[pidx], send_sem=send_sems.at[1, k - 1],
                                         recv_sem=recv_sems.at[1, k - 1], device_id=peer, device_id_type=MESH).wait_recv()
        for cp in sends:
            cp.wait_send()
        mod_ref[...] = mod_ref[...] + b_ref[...]

    vm = pl.BlockSpec(memory_space=pltpu.VMEM)
    return pl.pallas_call(
        body, out_shape=(jax.ShapeDtypeStruct((N_DEV, 1, D), F32), jax.ShapeDtypeStruct((N_DEV, 1, cw), F32)),
        in_specs=[vm, vm, vm] + [pl.BlockSpec(memory_space=pl.ANY)] * len(deps), out_specs=(vm, vm),
        scratch_shapes=[pltpu.VMEM((N_DEV, 1, cw), F32), pltpu.SemaphoreType.DMA((2, N_DEV - 1)),
                        pltpu.SemaphoreType.DMA((2, N_DEV - 1))],
        name='mod_exchange', compiler_params=_params(None, 48))(c_row, w_ada, b_ada8, *deps)


HBM_SPEC = pl.BlockSpec(memory_space=pltpu.HBM)
SEM_SPEC = pl.BlockSpec(memory_space=pltpu.SEMAPHORE)
DATAFLOW = pltpu.SideEffectType.DATAFLOW_SIDE_EFFECTING


def _push_start(src, scatter, after, name):
    land = lax.empty(src.shape if scatter else (N_DEV,) + src.shape, src.dtype)

    def body(src_ref, land_ref, after_ref, send_sem, recv_sem, land_thru, token):
        x, y, c = _me()
        me = 4 * x + 2 * y + c
        for k in range(1, N_DEV):
            peer, pidx = _peer(k)
            pltpu.make_async_remote_copy(src_ref=src_ref.at[pidx] if scatter else src_ref, dst_ref=land_ref.at[me],
                                         send_sem=send_sem, recv_sem=recv_sem, device_id=peer,
                                         device_id_type=MESH).start()
        token[...] = jnp.zeros_like(token)

    own = lax.dynamic_index_in_dim(src, _my_index(), 0, keepdims=False) if scatter else src
    src = pltpu.with_memory_space_constraint(src, pltpu.HBM)
    send_sem, recv_sem, land_thru, token = pl.pallas_call(
        body, name=name,
        out_shape=(pltpu.SemaphoreType.DMA(()), pltpu.SemaphoreType.DMA(()),
                   pltpu.HBM(land.shape, land.dtype), jax.ShapeDtypeStruct((8, 128), F32)),
        in_specs=(HBM_SPEC, HBM_SPEC, pl.BlockSpec(memory_space=pl.ANY)),
        out_specs=(SEM_SPEC, SEM_SPEC, HBM_SPEC, pl.BlockSpec(memory_space=pltpu.VMEM)),
        input_output_aliases={1: 2}, compiler_params=pltpu.CompilerParams(has_side_effects=DATAFLOW),
    )(src, pltpu.with_memory_space_constraint(land, pltpu.HBM), after)
    return send_sem, recv_sem, src, land_thru, token, own


def _push_wait(handle, after, name, merge=True):
    send_sem, recv_sem, src, land_thru, _, own = handle
    after = tuple(after) if isinstance(after, (tuple, list)) else (after,)

    def body(src_ref, land_ref, send_sem, recv_sem, *rest):
        seven = land_ref.at[pl.ds(0, N_DEV - 1)]
        cp = pltpu.make_async_remote_copy(src_ref=seven, dst_ref=seven, send_sem=send_sem, recv_sem=recv_sem,
                                          device_id=_me(), device_id_type=MESH)
        cp.wait_send()
        cp.wait_recv()

    landed = pl.pallas_call(
        body, name=name, out_shape=pltpu.HBM(land_thru.shape, land_thru.dtype),
        in_specs=(HBM_SPEC, HBM_SPEC, SEM_SPEC, SEM_SPEC) + (pl.BlockSpec(memory_space=pl.ANY),) * len(after),
        out_specs=HBM_SPEC, input_output_aliases={1: 0},
        compiler_params=pltpu.CompilerParams(has_side_effects=DATAFLOW),
    )(src, land_thru, send_sem, recv_sem, *after)
    if not merge:
        return landed, own
    return lax.dynamic_update_index_in_dim(landed, own, _my_index(), 0)


def _adam(w, g, m, v):
    m2 = B1 * m + (1.0 - B1) * g
    v2 = B2 * v + (1.0 - B2) * jnp.square(g)
    m_hat = m2 / (1.0 - B1 ** STEP)
    v_hat = v2 / (1.0 - B2 ** STEP)
    delta = -LR * (m_hat / (jnp.sqrt(v_hat) + AEPS) + WD * w)
    return delta, m2, v2


def _small_update(gp, wp, mp, vp):
    def body(g_ref, w_ref, m_ref, v_ref, all_ref, go_ref, d_ref, mo_ref, vo_ref, send_sems, recv_sems):
        x, y, c = _me()
        me = 4 * x + 2 * y + c
        all_ref[me] = g_ref[...]
        sends = []
        for k in range(1, N_DEV):
            peer, _ = _peer(k)
            cp = pltpu.make_async_remote_copy(src_ref=g_ref, dst_ref=all_ref.at[me], send_sem=send_sems.at[k - 1],
                                              recv_sem=recv_sems.at[k - 1], device_id=peer, device_id_type=MESH)
            cp.start()
            sends.append(cp)
        for k in range(1, N_DEV):
            peer, pidx = _peer(k)
            pltpu.make_async_remote_copy(src_ref=g_ref, dst_ref=all_ref.at[pidx], send_sem=send_sems.at[k - 1],
                                         recv_sem=recv_sems.at[k - 1], device_id=peer, device_id_type=MESH).wait_recv()
        for cp in sends:
            cp.wait_send()
        g = all_ref[0]
        for d in range(1, N_DEV):
            g = g + all_ref[d]
        delta, m2, v2 = _adam(w_ref[...], g, m_ref[...], v_ref[...])
        go_ref[...] = g
        d_ref[...] = delta
        mo_ref[...] = m2
        vo_ref[...] = v2

    vm = pl.BlockSpec(memory_space=pltpu.VMEM)
    vs = jax.ShapeDtypeStruct(gp.shape, F32)
    return pl.pallas_call(
        body, out_shape=(jax.ShapeDtypeStruct((N_DEV,) + gp.shape, F32), vs, vs, vs, vs), in_specs=[vm] * 4,
        out_specs=(vm,) * 5,
        scratch_shapes=[pltpu.SemaphoreType.DMA((N_DEV - 1,)), pltpu.SemaphoreType.DMA((N_DEV - 1,))],
        name='small_update', compiler_params=_params(None, 48))(gp, wp, mp, vp)


def _small_sum_update(parts, wp, mp, vp):
    def body(p_ref, w_ref, m_ref, v_ref, go_ref, d_ref, mo_ref, vo_ref):
        g = p_ref[0]
        for d in range(1, N_DEV):
            g = g + p_ref[d]
        delta, m2, v2 = _adam(w_ref[...], g, m_ref[...], v_ref[...])
        go_ref[...] = g
        d_ref[...] = delta
        mo_ref[...] = m2
        vo_ref[...] = v2

    vm = pl.BlockSpec(memory_space=pltpu.VMEM)
    vs = jax.ShapeDtypeStruct(wp.shape, F32)
    return pl.pallas_call(body, out_shape=(vs, vs, vs, vs), in_specs=[vm] * 4, out_specs=(vm,) * 4,
                          name='small_sum_update', compiler_params=_params(None, 48))(parts, wp, mp, vp)


def _big_update(parts, own, w, m, v, name):
    _, R, C = parts.shape
    tr = R if R % 256 else (128 if C >= 2048 else 256)

    def body(me_ref, p_ref, own_ref, w_ref, m_ref, v_ref, g_ref, d_ref, mo_ref, vo_ref):
        me = me_ref[0]
        mine = own_ref[...].astype(F32)
        g = jnp.where(me == 0, mine, p_ref[0].astype(F32))
        for d in range(1, N_DEV):
            g = g + jnp.where(me == d, mine, p_ref[d].astype(F32))
        delta, m2, v2 = _adam(w_ref[...], g, m_ref[...], v_ref[...])
        g_ref[...] = g
        d_ref[...] = delta
        mo_ref[...] = m2
        vo_ref[...] = v2

    blk = pl.BlockSpec((tr, C), lambda i: (i, 0))
    sh = jax.ShapeDtypeStruct((R, C), F32)
    return pl.pallas_call(
        body, out_shape=(sh, sh, sh, sh), grid=(R // tr,),
        in_specs=[pl.BlockSpec(memory_space=pltpu.SMEM), pl.BlockSpec((N_DEV, tr, C), lambda i: (0, i, 0)), blk, blk,
                  blk, blk], out_specs=(blk,) * 4,
        name=name, compiler_params=_params(('parallel',), 48))(_my_index().reshape(1), parts, own, w, m, v)


def _ada_update(c_all, dmod_cols, w, m, v):
    C = w.shape[1]
    tr = 256

    def body(c_ref, dm_ref, w_ref, m_ref, v_ref, g_ref, d_ref, mo_ref, vo_ref):
        cv = c_ref[...]
        s = cv * jax.nn.sigmoid(cv)
        g = lax.dot_general(s, dm_ref[...], TN_DIMS, preferred_element_type=F32, precision=HI)
        delta, m2, v2 = _adam(w_ref[...], g, m_ref[...], v_ref[...])
        g_ref[...] = g
        d_ref[...] = delta
        mo_ref[...] = m2
        vo_ref[...] = v2

    blk = pl.BlockSpec((tr, C), lambda i: (i, 0))
    sh = jax.ShapeDtypeStruct((D, C), F32)
    return pl.pallas_call(
        body, out_shape=(sh, sh, sh, sh), grid=(D // tr,),
        in_specs=[pl.BlockSpec((N_DEV, tr), lambda i: (0, i)), pl.BlockSpec((N_DEV, C), lambda i: (0, 0)), blk, blk, blk],
        out_specs=(blk,) * 4, name='ada_update', compiler_params=_params(('parallel',), 48))(c_all, dmod_cols, w, m, v)


def _rows_to_cluster_lanes(t):
    k = t.shape[1]
    return t.reshape(N_CL, CL_S, k).transpose(0, 2, 1)


def _blockdiag_in(t):
    t = t.reshape(N_CL, CL_G, SSM_N, SSM_P).transpose(0, 1, 3, 2)
    eye = jnp.eye(CL_G, dtype=t.dtype)
    t = t[:, :, :, None, :] * eye[None, :, None, :, None]
    return t.reshape(N_CL, CL_U, CL_S)


def _blockdiag_extract(t):
    t = t.reshape(N_CL, CL_G, SSM_P, CL_G, SSM_N)
    eye = jnp.eye(CL_G, dtype=t.dtype)
    t = jnp.sum(t * eye[None, :, None, :, None], axis=3)
    return t.transpose(0, 1, 3, 2).reshape(SSM_GN, SSM_P)


def _c_to_rows(t):
    return t.transpose(0, 2, 1).reshape(SSM_GN, SSM_P)


def _rows_to_c(t):
    return t.reshape(SSM_G, SSM_N, SSM_P).transpose(0, 2, 1)


def _ssm_prep(sp):
    rows = lambda n: sp[n].reshape(SSM_GN, 1)
    a_re, a_im = rows('ssm_a_re'), rows('ssm_a_im')
    ldt = jnp.repeat(sp['ssm_log_dt'].reshape(SSM_G, 1), SSM_N, axis=0)
    b_re, b_im = sp['ssm_b_re'].reshape(SSM_GN, SSM_P), sp['ssm_b_im'].reshape(SSM_GN, SSM_P)
    c_re, c_im = _c_to_rows(sp['ssm_c_re'].reshape(SSM_G, SSM_P, SSM_N)), _c_to_rows(sp['ssm_c_im'].reshape(SSM_G, SSM_P, SSM_N))
    bbr, bbi, pwr, pwi = _ssm_pre(a_re, a_im, ldt, b_re, b_im)
    bm = jnp.concatenate([_blockdiag_in(bbr), _blockdiag_in(bbi)], axis=2).astype(BF16)
    cmt = jnp.concatenate([_blockdiag_in(c_re), -_blockdiag_in(c_im)], axis=2).astype(BF16)
    bmt, cm = bm.transpose(0, 2, 1), cmt.transpose(0, 2, 1)
    pw = jnp.concatenate([_rows_to_cluster_lanes(pwr), _rows_to_cluster_lanes(pwi)], axis=2)
    return a_re, a_im, ldt, b_re, b_im, bm, cm, bmt, cmt, pw


def _tied(v, deps):
    for t in deps:
        v = v + t[0, 0]
    return v


def _local_step(x, pos, mod, tgt, sp, prep, get_w, emit, emit_small, emit_late, first_deps=()):
    sh1, sc1, gt1, sh2, sc2, gt2 = (mod[i:i + 1] for i in range(NMOD))
    vec = lambda n: sp[n].reshape(1, -1)
    a_re, a_im, ldt, b_re, b_im, bm_b, cm_b, bmt_b, cmt_b, pw = prep
    dvec = vec('ssm_d')

    h1 = _prenorm_fwd(x, vec('g_pre_mix'), sc1, sh1)
    w_in = get_w('w_in', (h1, bm_b, cm_b, pw, bmt_b, cmt_b))
    proj = _mm(h1, w_in, mode='nn', name='mm_in', tn=1408, deps=first_deps)
    fr1 =ROPE_THETA ** (-jnp.arange(0, ROT_DIM, 2, dtype=F32) / ROT_DIM)
    lane = jnp.arange(128) % HEAD_DIM
    fr = jnp.where(lane < ROT_DIM, fr1[lane % (ROT_DIM // 2)], 0.0).reshape(1, 128).astype(F32)
    u_off = (ROPE_W + KVW) // CL_U
    qkvs = _rope_fwd(proj, pos, fr)
    fwd = [_attn_fwd(qkvs[g], d) for g, d in enumerate(DILATIONS)]
    os_, lses = [t[0] for t in fwd], [t[1] for t in fwd]
    att = _combine_fwd(os_, lses, vec('g_attn_out'))

    ypre, bnd, states = _ssm_fwd(proj, bm_b, cm_b, pw, dvec, u_off)
    w_glu = get_w('w_glu', ypre)
    ssm_n = _glu_fwd(ypre, w_glu, vec('b_glu'), vec('g_ssm_out'))

    cat = jnp.concatenate([att, ssm_n], axis=1)
    w_out = get_w('w_out', cat)
    mix = _mm(cat, w_out, mode='nn', name='mm_out', tk=1280)
    x1, h2 = _postmix_fwd(x, mix, vec('g_post_mix'), gt1, vec('g_pre_mlp'), sc2, sh2)
    w_mi = get_w('w_mlp_in', h2)
    a_pre, r_act = _mm(h2, w_mi, mode='nn', name='mm_mlp_in', epilogue='relu2', b_sharded=True)
    w_mo = get_w('w_mlp_out', a_pre)
    y = _mm(r_act, w_mo, mode='nn', name='mm_mlp_out')
    dx2, dy, loss, dgt2, dg_post_mlp = _final_fwd_bwd(x1, y, tgt, vec('g_post_mlp'), gt2)
    dgt2, dg_post_mlp = dgt2[:1], dg_post_mlp[:1]

    da = _mm(dy, w_mo, mode='nt', name='mm_d_act', out_dtype=BF16, epilogue='drelu2', extra=a_pre)
    dep = emit('w_mlp_out', _mm(r_act, dy, mode='tn', name='mm_dw_mlp_out', out_dtype=BF16))
    dh2 = _mm(da, w_mi, mode='nt', name='mm_dh2', tk=2048, out_dtype=BF16, b_sharded=True, deps=dep)
    dep = emit('w_mlp_in', _mm(h2, da, mode='tn', name='mm_dw_mlp_in', out_dtype=BF16, out_sharded=True))
    dx1, dmix, dsc2, dsh2, dg_pre_mlp, dgt1, dg_post_mix = _postmix_bwd(
        dx2, dh2, x1, mix, vec('g_post_mix'), gt1, vec('g_pre_mlp'), sc2)
    dsc2, dsh2, dg_pre_mlp, dgt1, dg_post_mix = (t[:1] for t in (dsc2, dsh2, dg_pre_mlp, dgt1, dg_post_mix))
    dcat = _mm(dmix, w_out, mode='nt', name='mm_dcat', tn=1280, out_dtype=BF16, deps=dep)
    dep = emit('w_out', _mm(cat, dmix, mode='tn', name='mm_dw_out', out_dtype=BF16, tm=640))
    dypre, g_w_glu, g_b_glu, g_g_ssm = _glu_bwd(ypre, dcat, w_glu, _tied(vec('b_glu'), dep), vec('g_ssm_out'))
    g_b_glu, g_g_ssm = g_b_glu[:1], g_g_ssm[:1]
    dep = dep + emit('w_glu', g_w_glu.astype(BF16))
    du, dbm, dcm, dA, dD = _ssm_bwd(proj, dypre, states, bmt_b, cmt_b, pw, dvec, bnd, u_off)
    dD = dD[:1]
    gbr, gbi = _blockdiag_extract(dbm[:, :, :CL_S]), _blockdiag_extract(dbm[:, :, CL_S:])
    dcmt = dcm.transpose(0, 2, 1)
    g_c_re = _rows_to_c(_blockdiag_extract(dcmt[:, :, :CL_S]))
    g_c_im = _rows_to_c(-_blockdiag_extract(dcmt[:, :, CL_S:]))
    gar = dA[:, 0, :CL_S].reshape(SSM_GN, 1)
    gai = dA[:, 0, CL_S:].reshape(SSM_GN, 1)
    sel = (jnp.arange(SSM_GN)[None, :] // SSM_N == jnp.arange(SSM_G)[:, None]).astype(F32)
    g_a_re, g_a_im, g_b_re, g_b_im, g_ldt = _ssm_post(a_re, a_im, ldt, b_re, b_im, gar, gai, gbr, gbi, sel)

    head_ones = (jnp.arange(KVW)[:, None] // HEAD_DIM == jnp.arange(KVW)[None, :] // HEAD_DIM).astype(F32)
    dos, dlses, g_g_attn = _combine_bwd(dcat, os_, lses, vec('g_attn_out'), head_ones)
    g_g_attn = g_g_attn[:1]
    dep_small = emit_small({
        'g_post_mix': dg_post_mix, 'ssm_a_re': g_a_re, 'ssm_a_im': g_a_im, 'ssm_log_dt': g_ldt[:, 0],
        'ssm_b_re': g_b_re, 'ssm_b_im': g_b_im, 'ssm_c_re': g_c_re, 'ssm_c_im': g_c_im, 'ssm_d': dD, 'b_glu': g_b_glu,
        'g_attn_out': g_g_attn, 'g_ssm_out': g_g_ssm, 'g_pre_mlp': dg_pre_mlp, 'g_post_mlp': dg_post_mlp})
    dqkv = [_attn_bwd(qkvs[g], os_[g], lses[g], dos[g], dlses[g], d) for g, d in enumerate(DILATIONS)]
    dproj = _rope_bwd(dqkv, du, pos, _tied(fr, dep_small))
    dh1 = _mm(dproj, w_in, mode='nt', name='mm_dh1', tk=1408, out_dtype=BF16, deps=dep)
    grad_x, dsc1, dsh1, dg_pre_mix = _prenorm_bwd(dx1, dh1, x, vec('g_pre_mix'), sc1)
    dsc1, dsh1, dg_pre_mix = dsc1[:1], dsh1[:1], dg_pre_mix[:1]
    dmod = jnp.concatenate([dsh1, dsc1, dgt1, dsh2, dsc2, dgt2], axis=0)
    dep = emit_late({'b_ada': dmod, 'g_pre_mix': dg_pre_mix})
    emit('w_in', _mm(h1, dproj, mode='tn', name='mm_dw_in', out_dtype=BF16, tn=1408, deps=dep))
    return loss[0, 0], grad_x


def _pack(d, names):
    flat = jnp.concatenate([jnp.pad(d[n].reshape(-1).astype(F32), (0, SEG[n] - SMALL_SIZES[n])) for n in names])
    return flat.reshape(-1, 128)


def _unpack(packed, names, shapes):
    out, off = {}, 0
    for n in names:
        out[n] = packed[off // 128:(off + SEG[n]) // 128].reshape(-1)[:SMALL_SIZES[n]].reshape(shapes[n])
        off += SEG[n]
    return out


def _shard_major(t, name):
    if name in ('w_in', 'w_out', 'w_mlp_in'):
        k, n = t.shape
        return t.reshape(k, N_DEV, n // N_DEV).transpose(1, 0, 2)
    k, n = t.shape
    return t.reshape(N_DEV, k // N_DEV, n)


def _from_shard_major(t, name):
    if name in ('w_in', 'w_out', 'w_mlp_in'):
        _, k, n = t.shape
        return t.transpose(1, 0, 2).reshape(k, N_DEV * n)
    _, k, n = t.shape
    return t.reshape(N_DEV * k, n)


def kernel(x, c, positions, w_ada, b_ada, g_pre_mix, g_post_mix, w_in, ssm_a_re, ssm_a_im, ssm_log_dt, ssm_b_re, ssm_b_im, ssm_c_re, ssm_c_im, ssm_d, w_glu, b_glu, g_attn_out, g_ssm_out, w_out, g_pre_mlp, g_post_mlp, w_mlp_in, w_mlp_out, loss_target, m_w_ada, m_b_ada, m_g_pre_mix, m_g_post_mix, m_w_in, m_ssm_a_re, m_ssm_a_im, m_ssm_log_dt, m_ssm_b_re, m_ssm_b_im, m_ssm_c_re, m_ssm_c_im, m_ssm_d, m_w_glu, m_b_glu, m_g_attn_out, m_g_ssm_out, m_w_out, m_g_pre_mlp, m_g_post_mlp, m_w_mlp_in, m_w_mlp_out, v_w_ada, v_b_ada, v_g_pre_mix, v_g_post_mix, v_w_in, v_ssm_a_re, v_ssm_a_im, v_ssm_log_dt, v_ssm_b_re, v_ssm_b_im, v_ssm_c_re, v_ssm_c_im, v_ssm_d, v_w_glu, v_b_glu, v_g_attn_out, v_g_ssm_out, v_w_out, v_g_pre_mlp, v_g_post_mlp, v_w_mlp_in, v_w_mlp_out):
    loc = dict(locals())
    W = {n: loc[n] for n in WEIGHTS}
    M = {n: loc['m_' + n] for n in WEIGHTS}
    V = {n: loc['v_' + n] for n in WEIGHTS}
    assert x.shape == (1, L, D) and w_in.shape == (1, D, INW // N_DEV), (x.shape, w_in.shape)

    cw = NMOD * D // N_DEV
    c_all, mod8 = _mod_exchange(c.reshape(1, 1, D), w_ada[0], b_ada.reshape(N_DEV, 1, cw))
    mod = mod8.reshape(NMOD, D)

    gather, after = {}, mod8
    for n in BIG:
        gather[n] = _push_start(W[n][0].astype(BF16), False, after, 'gather_start_' + n)
        after = gather[n][4]
    tokens = tuple(gather[n][4] for n in BIG)
    mod = _tied(mod, tokens)
    sp = {n: W[n][0] for n in SMALL}
    prep = _ssm_prep({**sp, 'ssm_a_re': _tied(sp['ssm_a_re'], tokens)})

    state_packs = {}

    def get_w(n, after):
        if n == 'w_mlp_out':
            tok = jnp.minimum(jnp.abs(after[0:8, 0:128].astype(F32)), 0.0)
            for key, src in (('w', W), ('m', M), ('v', V)):
                tied = {p: _tied(src[p], (tok,)) for p in SMALL}
                state_packs[key] = {'early': _pack(tied, SMALL_EARLY), 'late': _pack(tied, SMALL_LATE)}
            after = (after,) + tuple(state_packs[k][part] for k in 'wmv' for part in ('early', 'late'))
        g = _push_wait(gather[n], after, 'gather_wait_' + n)
        return g if n == 'w_mlp_in' else _from_shard_major(g, n)

    scatter = {}

    def emit(n, g):
        src = g if n == 'w_mlp_in' else _shard_major(g, n)
        scatter[n] = _push_start(src, True, src, 'scatter_start_' + n)
        return (scatter[n][4],)

    small_early = []

    def emit_small(d):
        pack = _pack(d, SMALL_EARLY)
        small_early.append(_push_start(pack, False, pack, 'small_start'))
        return (small_early[0][4],)

    out_g, out_d, out_m, out_v = {}, {}, {}, {}
    shapes = {n: W[n].shape[1:] for n in SMALL}

    def put(names, packs):
        for dst, packed in zip((out_g, out_d, out_m, out_v), packs):
            dst.update(_unpack(packed, names, shapes))

    late = []

    def emit_late(d):
        rows_all, *packs = _small_update(_pack(d, SMALL_LATE), *[state_packs[k]['late'] for k in 'wmv'])
        put(SMALL_LATE, packs)
        late.append(rows_all)
        return (rows_all,)

    loss, grad_x = _local_step(x[0], positions.reshape(L, 1), mod, loss_target[0], sp, prep, get_w, emit, emit_small,
                               emit_late)
    loss = lax.psum(loss, ('x', 'y', 'c'))

    me = 4 * lax.axis_index('x') + 2 * lax.axis_index('y') + lax.axis_index('c')
    dmod_all = late[0][:, :NMOD * D // 128].reshape(N_DEV, NMOD * D)
    dmod_cols = _tied(lax.dynamic_slice_in_dim(dmod_all, me * cw, cw, axis=1), (scatter['w_in'][4],))
    out_g['w_ada'], out_d['w_ada'], out_m['w_ada'], out_v['w_ada'] = _ada_update(
        c_all.reshape(N_DEV, D), dmod_cols, w_ada[0], m_w_ada[0], v_w_ada[0])

    parts = _push_wait(small_early[0], out_v['w_ada'], 'small_wait')
    packs = _small_sum_update(parts, *[state_packs[k]['early'] for k in 'wmv'])
    put(SMALL_EARLY, packs)

    after = packs[3]
    for n in ('w_mlp_out', 'w_mlp_in', 'w_out', 'w_glu', 'w_in'):
        parts, own = _push_wait(scatter[n], after, 'scatter_wait_' + n, merge=False)
        out_g[n], out_d[n], out_m[n], out_v[n] = _big_update(parts, own, W[n][0], M[n][0], V[n][0], 'update_' + n)
        after = out_v[n]

    lead = lambda t: t[None]
    return (loss, grad_x[None], *[lead(out_g[n]) for n in WEIGHTS], *[lead(out_d[n]) for n in WEIGHTS],
            *[lead(out_m[n]) for n in WEIGHTS], *[lead(out_v[n]) for n in WEIGHTS])
```

```python
import functools
import math

import jax
import jax.numpy as jnp
from jax import lax
from jax.experimental import pallas as pl
from jax.experimental.pallas import tpu as pltpu

F32 = jnp.float32
BF16 = jnp.bfloat16
HI = lax.Precision.HIGHEST
MESH = pl.DeviceIdType.MESH

N_DEV = 8
L = 4096
D = 2048
HEAD_DIM = 64
N_GROUPS = 3
DILATIONS = (1, 4, 16)
HEADS = 6
QW = N_GROUPS * HEADS * HEAD_DIM
KVW = HEADS * HEAD_DIM
ROT_DIM = 16
ROPE_THETA = 500000.0
BLK = 128
NBLK = L // BLK
SSMW = D - QW
SSM_P = 16
SSM_G = SSMW // SSM_P
SSM_N = 64
SSM_GN = SSM_G * SSM_N
CL_G = 8
N_CL = SSM_G // CL_G
CL_U = CL_G * SSM_P
CL_S = CL_G * SSM_N
INW = QW + 2 * KVW + SSMW
OUTW = KVW + SSMW
DFF = 4 * D
NMOD = 6
EPS = 1e-6
LR, B1, B2, AEPS, WD, STEP = 0.001, 0.9, 0.999, 1e-08, 0.01, 10

T_SCAN = 1024
MB = 2 ** 20

WEIGHTS = ['w_ada', 'b_ada', 'g_pre_mix', 'g_post_mix', 'w_in', 'ssm_a_re', 'ssm_a_im', 'ssm_log_dt',
           'ssm_b_re', 'ssm_b_im', 'ssm_c_re', 'ssm_c_im', 'ssm_d', 'w_glu', 'b_glu', 'g_attn_out',
           'g_ssm_out', 'w_out', 'g_pre_mlp', 'g_post_mlp', 'w_mlp_in', 'w_mlp_out']
BIG = ['w_in', 'w_glu', 'w_out', 'w_mlp_in', 'w_mlp_out']
SMALL = [n for n in WEIGHTS if n not in BIG and n != 'w_ada']
SMALL_SIZES = {'b_ada': NMOD * D, 'g_pre_mix': D, 'g_post_mix': D, 'ssm_a_re': SSM_GN, 'ssm_a_im': SSM_GN,
               'ssm_log_dt': SSM_G, 'ssm_b_re': SSM_GN * SSM_P, 'ssm_b_im': SSM_GN * SSM_P,
               'ssm_c_re': SSM_GN * SSM_P, 'ssm_c_im': SSM_GN * SSM_P, 'ssm_d': SSMW, 'b_glu': SSMW,
               'g_attn_out': KVW, 'g_ssm_out': SSMW, 'g_pre_mlp': D, 'g_post_mlp': D}
SEG = {n: -(-SMALL_SIZES[n] // 1024) * 1024 for n in SMALL}
SMALL_LATE = ['b_ada', 'g_pre_mix']
SMALL_EARLY = [n for n in SMALL if n not in SMALL_LATE]


def _params(sem=None, vmem_mb=None):
    kw = {}
    if sem is not None:
        kw['dimension_semantics'] = sem
    if vmem_mb is not None:
        kw['vmem_limit_bytes'] = vmem_mb * MB
    return pltpu.CompilerParams(**kw)


def _vec(n):
    return pl.BlockSpec((1, n), lambda *_: (0, 0))


def _rms(x):
    return lax.rsqrt(jnp.mean(x * x, axis=-1, keepdims=True) + EPS)


def _rms_bwd(dn, n, r):
    return r * (dn - n * jnp.mean(dn * n, axis=-1, keepdims=True))


def _vec8(n):
    return pl.BlockSpec((8, n), lambda *_: (0, 0))


def _colsum(x):
    return jnp.sum(x.reshape(-1, 8, x.shape[-1]), axis=0)


def _fold8(last, *refs):
    @pl.when(last)
    def _():
        for r in refs:
            r[...] = jnp.broadcast_to(jnp.sum(r[...], axis=0, keepdims=True), r.shape)


def _mm(a, b, *, mode, name, out_dtype=F32, tm=1024, tn=1024, tk=2048, epilogue=None, extra=None,
        b_sharded=False, out_sharded=False, deps=()):
    if mode == 'nn':
        M, K = a.shape
        dims = (((1,), (0,)), ((), ()))
        a_spec = pl.BlockSpec((tm, tk), lambda i, j, k: (i, k))
        if b_sharded:
            _, K2, per = b.shape
            N, q = N_DEV * per, per // tn
            b_spec = pl.BlockSpec((None, tk, tn), lambda i, j, k: (j // q, k, j % q))
        else:
            K2, N = b.shape
            b_spec = pl.BlockSpec((tk, tn), lambda i, j, k: (k, j))
    elif mode == 'nt':
        M, K = a.shape
        dims = (((1,), (1,)), ((), ()))
        a_spec = pl.BlockSpec((tm, tk), lambda i, j, k: (i, k))
        if b_sharded:
            _, N, per = b.shape
            K2 = N_DEV * per
            if tk > per:
                b_spec = pl.BlockSpec((tk // per, tn, per), lambda i, j, k: (k, j, 0))
            else:
                q = per // tk
                b_spec = pl.BlockSpec((None, tn, tk), lambda i, j, k: (k // q, j, k % q))
        else:
            N, K2 = b.shape
            b_spec = pl.BlockSpec((tn, tk), lambda i, j, k: (j, k))
    else:
        (K, M), (K2, N) = a.shape, b.shape
        dims = (((0,), (0,)), ((), ()))
        a_spec = pl.BlockSpec((tk, tm), lambda i, j, k: (k, i))
        b_spec = pl.BlockSpec((tk, tn), lambda i, j, k: (k, j))
    assert K == K2 and M % tm == 0 and N % tn == 0 and K % tk == 0, (name, a.shape, b.shape, tm, tn, tk)
    nk = K // tk
    o_spec = pl.BlockSpec((tm, tn), lambda i, j, k: (i, j))
    o_dims = (M, N)
    if out_sharded:
        qo = N // N_DEV // tn
        o_spec = pl.BlockSpec((None, tm, tn), lambda i, j, k: (j // qo, i, j % qo))
        o_dims = (N_DEV, M, N // N_DEV)
    n_out = 2 if epilogue == 'relu2' else 1
    n_extra = 1 if extra is not None else 0
    n_in = 2 + n_extra + len(deps)

    def body(*refs):
        a_ref, b_ref = refs[0], refs[1]
        x_refs = refs[2:2 + n_extra]
        o_refs = refs[n_in:n_in + n_out]
        acc = refs[-1]
        k = pl.program_id(2)

        if len(b_ref.shape) == 3:
            per_ = b_ref.shape[2]
            prod = sum(lax.dot_general(a_ref[:, s * per_:(s + 1) * per_], b_ref[s], dims, preferred_element_type=F32)
                       for s in range(b_ref.shape[0]))
        else:
            prod = lax.dot_general(a_ref[...], b_ref[...], dims, preferred_element_type=F32)

        def finish(r):
            if epilogue == 'relu2':
                o_refs[0][...] = r.astype(BF16)
                o_refs[1][...] = jnp.square(jnp.maximum(r, 0.0)).astype(BF16)
            elif epilogue == 'drelu2':
                pre = x_refs[0][...].astype(F32)
                o_refs[0][...] = (r * (2.0 * jnp.maximum(pre, 0.0))).astype(out_dtype)
            else:
                o_refs[0][...] = r.astype(out_dtype)

        if nk == 1:
            finish(prod)
        else:
            @pl.when(k == 0)
            def _():
                acc[...] = prod

            @pl.when((k > 0) & (k < nk - 1))
            def _():
                acc[...] += prod

            @pl.when(k == nk - 1)
            def _():
                finish(acc[...] + prod)

    if epilogue == 'relu2':
        out_shape = (jax.ShapeDtypeStruct((M, N), BF16), jax.ShapeDtypeStruct((M, N), BF16))
        out_specs = (o_spec, o_spec)
    else:
        out_shape = jax.ShapeDtypeStruct(o_dims, out_dtype)
        out_specs = o_spec
    args = (a, b) + ((extra,) if extra is not None else ()) + tuple(deps)
    in_specs = ([a_spec, b_spec] + ([o_spec] if extra is not None else [])
                + [pl.BlockSpec(memory_space=pl.ANY)] * len(deps))
    return pl.pallas_call(
        body, out_shape=out_shape, grid=(M // tm, N // tn, nk), in_specs=in_specs, out_specs=out_specs,
        scratch_shapes=[pltpu.VMEM((tm, tn) if nk > 1 else (8, 128), F32)], name=name,
        compiler_params=_params(('parallel', 'parallel', 'arbitrary'), 56))(*args)


TR = 256


def _rowspec(w=D):
    return pl.BlockSpec((TR, w), lambda i: (i, 0))


def _prenorm_fwd(x, g, sc, sh):
    def body(x_ref, g_ref, sc_ref, sh_ref, h_ref):
        xv = x_ref[...]
        n = xv * _rms(xv)
        h_ref[...] = ((n * g_ref[...]) * (1.0 + sc_ref[...]) + sh_ref[...]).astype(BF16)

    return pl.pallas_call(
        body, out_shape=jax.ShapeDtypeStruct((L, D), BF16), grid=(L // TR,),
        in_specs=[_rowspec(), _vec(D), _vec(D), _vec(D)], out_specs=_rowspec(), name='prenorm_fwd',
        compiler_params=_params(('parallel',), 40))(x, g, sc, sh)


def _postmix_fwd(x, mix, gpm, gt1, gpl, sc2, sh2):
    def body(x_ref, mix_ref, gpm_ref, gt1_ref, gpl_ref, sc2_ref, sh2_ref, x1_ref, h2_ref):
        mix_v = mix_ref[...]
        nm = mix_v * _rms(mix_v)
        x1 = x_ref[...] + gt1_ref[...] * (nm * gpm_ref[...])
        x1_ref[...] = x1
        n2 = x1 * _rms(x1)
        h2_ref[...] = ((n2 * gpl_ref[...]) * (1.0 + sc2_ref[...]) + sh2_ref[...]).astype(BF16)

    return pl.pallas_call(
        body, out_shape=(jax.ShapeDtypeStruct((L, D), F32), jax.ShapeDtypeStruct((L, D), BF16)), grid=(L // TR,),
        in_specs=[_rowspec(), _rowspec()] + [_vec(D)] * 5, out_specs=(_rowspec(), _rowspec()), name='postmix_fwd',
        compiler_params=_params(('parallel',), 40))(x, mix, gpm, gt1, gpl, sc2, sh2)


def _final_fwd_bwd(x1, y, tgt, g, gt2):
    def body(x1_ref, y_ref, t_ref, g_ref, gt2_ref, dx2_ref, dy_ref, loss_ref, dgt2_ref, dg_ref):
        @pl.when(pl.program_id(0) == 0)
        def _():
            loss_ref[...] = jnp.zeros_like(loss_ref)
            dgt2_ref[...] = jnp.zeros_like(dgt2_ref)
            dg_ref[...] = jnp.zeros_like(dg_ref)

        yv = y_ref[...]
        r = _rms(yv)
        n = yv * r
        ng = n * g_ref[...]
        x2 = x1_ref[...] + gt2_ref[...] * ng
        e = x2 - t_ref[...]
        loss_ref[...] += 0.5 * jnp.sum(jnp.mean(e * e, axis=-1, keepdims=True), axis=0, keepdims=True)
        dx2 = e * (1.0 / D)
        dx2_ref[...] = dx2
        dgt2_ref[...] += _colsum(dx2 * ng)
        dng = dx2 * gt2_ref[...]
        dg_ref[...] += _colsum(dng * n)
        dy_ref[...] = _rms_bwd(dng * g_ref[...], n, r).astype(BF16)
        _fold8(pl.program_id(0) == L // TR - 1, dgt2_ref, dg_ref)

    return pl.pallas_call(
        body,
        out_shape=(jax.ShapeDtypeStruct((L, D), F32), jax.ShapeDtypeStruct((L, D), BF16),
                   jax.ShapeDtypeStruct((8, 128), F32), jax.ShapeDtypeStruct((8, D), F32),
                   jax.ShapeDtypeStruct((8, D), F32)),
        grid=(L // TR,), in_specs=[_rowspec(), _rowspec(), _rowspec(), _vec(D), _vec(D)],
        out_specs=(_rowspec(), _rowspec(), _vec8(128), _vec8(D), _vec8(D)), name='final_fwd_bwd',
        compiler_params=_params(('arbitrary',), 40))(x1, y, tgt, g, gt2)


def _postmix_bwd(dx2, dh2, x1, mix, gpm, gt1, gpl, sc2):
    def body(dx2_ref, dh2_ref, x1_ref, mix_ref, gpm_ref, gt1_ref, gpl_ref, sc2_ref,
             dx1_ref, dmix_ref, dsc2_ref, dsh2_ref, dgpl_ref, dgt1_ref, dgpm_ref):
        @pl.when(pl.program_id(0) == 0)
        def _():
            for r_ in (dsc2_ref, dsh2_ref, dgpl_ref, dgt1_ref, dgpm_ref):
                r_[...] = jnp.zeros_like(r_)

        x1v = x1_ref[...]
        r2 = _rms(x1v)
        n2 = x1v * r2
        dh2v = dh2_ref[...].astype(F32)
        dsh2_ref[...] += _colsum(dh2v)
        dsc2_ref[...] += _colsum(dh2v * (n2 * gpl_ref[...]))
        t = dh2v * (1.0 + sc2_ref[...])
        dgpl_ref[...] += _colsum(t * n2)
        dx1 = dx2_ref[...] + _rms_bwd(t * gpl_ref[...], n2, r2)
        dx1_ref[...] = dx1
        mix_v = mix_ref[...]
        rm = _rms(mix_v)
        nm = mix_v * rm
        dgt1_ref[...] += _colsum(dx1 * (nm * gpm_ref[...]))
        u = dx1 * gt1_ref[...]
        dgpm_ref[...] += _colsum(u * nm)
        dmix_ref[...] = _rms_bwd(u * gpm_ref[...], nm, rm).astype(BF16)
        _fold8(pl.program_id(0) == L // TR - 1, dsc2_ref, dsh2_ref, dgpl_ref, dgt1_ref, dgpm_ref)

    vs = jax.ShapeDtypeStruct((8, D), F32)
    return pl.pallas_call(
        body, out_shape=(jax.ShapeDtypeStruct((L, D), F32), jax.ShapeDtypeStruct((L, D), BF16), vs, vs, vs, vs, vs),
        grid=(L // TR,), in_specs=[_rowspec()] * 4 + [_vec(D)] * 4,
        out_specs=(_rowspec(), _rowspec()) + (_vec8(D),) * 5, name='postmix_bwd',
        compiler_params=_params(('arbitrary',), 48))(dx2, dh2, x1, mix, gpm, gt1, gpl, sc2)


def _prenorm_bwd(dx1, dh1, x, g, sc1):
    def body(dx1_ref, dh1_ref, x_ref, g_ref, sc1_ref, dx_ref, dsc1_ref, dsh1_ref, dg_ref):
        @pl.when(pl.program_id(0) == 0)
        def _():
            for r_ in (dsc1_ref, dsh1_ref, dg_ref):
                r_[...] = jnp.zeros_like(r_)

        xv = x_ref[...]
        r = _rms(xv)
        n = xv * r
        dh = dh1_ref[...].astype(F32)
        dsh1_ref[...] += _colsum(dh)
        dsc1_ref[...] += _colsum(dh * (n * g_ref[...]))
        t = dh * (1.0 + sc1_ref[...])
        dg_ref[...] += _colsum(t * n)
        dx_ref[...] = dx1_ref[...] + _rms_bwd(t * g_ref[...], n, r)
        _fold8(pl.program_id(0) == L // TR - 1, dsc1_ref, dsh1_ref, dg_ref)

    vs = jax.ShapeDtypeStruct((8, D), F32)
    return pl.pallas_call(
        body, out_shape=(jax.ShapeDtypeStruct((L, D), F32), vs, vs, vs), grid=(L // TR,),
        in_specs=[_rowspec()] * 3 + [_vec(D)] * 2, out_specs=(_rowspec(),) + (_vec8(D),) * 3, name='prenorm_bwd',
        compiler_params=_params(('arbitrary',), 40))(dx1, dh1, x, g, sc1)


ROPE_W = QW + KVW
QKV_W = 3 * KVW
QK_SCALE = 0.125
NB_KV = KVW // 128


def _rope_rotate(xv, pos, fr, sign):
    ang = pos.astype(F32) * fr
    w = lax.broadcasted_iota(jnp.int32, (1, 128), 1) % HEAD_DIM
    cs = jnp.cos(ang)
    sn = jnp.sin(ang) * sign
    s1 = jnp.where(w < ROT_DIM // 2, -sn, 0.0)
    s2 = jnp.where((w >= ROT_DIM // 2) & (w < ROT_DIM), sn, 0.0)
    width = xv.shape[1]
    rep = width // 128
    cs, s1, s2 = jnp.tile(cs, (1, rep)), jnp.tile(s1, (1, rep)), jnp.tile(s2, (1, rep))
    hi = pltpu.roll(xv, width - ROT_DIM // 2, 1)
    lo = pltpu.roll(xv, ROT_DIM // 2, 1)
    return xv * cs + hi * s1 + lo * s2


def _sub_spec(d, rows, width):
    return pl.BlockSpec((d, rows // d, width), lambda i: (0, i, 0))


def _gather_rows(scr, blocks, r, d, rows):
    return jnp.concatenate([scr.at[j][pl.ds(r, rows // d, stride=d), :] for j in blocks], axis=1)


def _scatter_rows(scr, src_ref, d, rows):
    for r in range(d):
        for j in range(NB_KV):
            scr.at[j][pl.ds(r, rows // d, stride=d), :] = src_ref[r, :, j * 128:(j + 1) * 128]


def _token_order(scr):
    return jnp.concatenate([scr[j] for j in range(NB_KV)], axis=1)


def _rope_fwd(proj, pos, fr):
    nb = (ROPE_W + KVW) // 128

    def body(x_ref, pos_ref, fr_ref, o0_ref, o1_ref, o2_ref, scr):
        y = _rope_rotate(x_ref[:, 0:ROPE_W], pos_ref[...], fr_ref[...], 1.0)
        for j in range(ROPE_W // 128):
            scr[j] = y[:, j * 128:(j + 1) * 128] * (QK_SCALE if j < QW // 128 else 1.0)
        for j in range(ROPE_W // 128, nb):
            scr[j] = x_ref[:, j * 128:(j + 1) * 128]
        kv = list(range(QW // 128, nb))
        for g, (d, o_ref) in enumerate(zip(DILATIONS, (o0_ref, o1_ref, o2_ref))):
            blocks = list(range(g * NB_KV, (g + 1) * NB_KV)) + kv
            for r in range(d):
                o_ref[r] = _gather_rows(scr, blocks, r, d, TR).astype(BF16)

    return pl.pallas_call(
        body, out_shape=tuple(jax.ShapeDtypeStruct((d, L // d, QKV_W), BF16) for d in DILATIONS), grid=(L // TR,),
        in_specs=[_rowspec(ROPE_W + KVW), pl.BlockSpec((TR, 1), lambda i: (i, 0)), _vec(128)],
        out_specs=tuple(_sub_spec(d, TR, QKV_W) for d in DILATIONS),
        scratch_shapes=[pltpu.VMEM((nb, TR, 128), F32)], name='rope_fwd',
        compiler_params=_params(('parallel',), 40))(proj, pos, fr)


def _rope_bwd(dqkv, du, pos, fr):
    def body(*refs):
        grads = [refs[3 * g:3 * g + 3] for g in range(N_GROUPS)]
        du_ref, pos_ref, fr_ref, o_ref = refs[9:13]
        scrs = refs[13:]
        dq, dk, dv = [], None, None
        for g, d in enumerate(DILATIONS):
            parts = []
            for t in range(3):
                if d == 1:
                    parts.append(grads[g][t][0])
                else:
                    scr = scrs[3 * (g - 1) + t]
                    _scatter_rows(scr, grads[g][t], d, TR)
                    parts.append(_token_order(scr))
            dq.append(parts[0])
            dk = parts[1] if dk is None else dk + parts[1]
            dv = parts[2] if dv is None else dv + parts[2]
        x = jnp.concatenate(dq + [dk], axis=1)
        o_ref[:, 0:ROPE_W] = _rope_rotate(x, pos_ref[...], fr_ref[...], -1.0).astype(BF16)
        o_ref[:, ROPE_W:ROPE_W + KVW] = dv.astype(BF16)
        o_ref[:, ROPE_W + KVW:INW] = du_ref[...].astype(BF16)

    flat = [a for grp in dqkv for a in grp]
    in_specs = [_sub_spec(d, TR, KVW) for d in DILATIONS for _ in range(3)]
    in_specs += [_rowspec(SSMW), pl.BlockSpec((TR, 1), lambda i: (i, 0)), _vec(128)]
    return pl.pallas_call(
        body, out_shape=jax.ShapeDtypeStruct((L, INW), BF16), grid=(L // TR,), in_specs=in_specs,
        out_specs=_rowspec(INW), scratch_shapes=[pltpu.VMEM((NB_KV, TR, 128), F32)] * 6, name='rope_bwd',
        compiler_params=_params(('parallel',), 48))(*flat, du, pos, fr)


def _attn_mask(nbs, b):
    first = (b & (nbs - 1)) == 0
    qi = lax.broadcasted_iota(jnp.int32, (BLK, 2 * BLK), 0)
    kj = lax.broadcasted_iota(jnp.int32, (BLK, 2 * BLK), 1)
    dist = qi + BLK - kj
    return (dist >= 0) & (dist <= BLK) & ((kj >= BLK) | jnp.logical_not(first))


def _qkv_specs():
    cur = lambda col: pl.BlockSpec((BLK, KVW), lambda b: (b, col))
    prev = lambda col: pl.BlockSpec((BLK, KVW), lambda b: (jnp.maximum(b - 1, 0), col))
    return [cur(0), prev(1), cur(1), prev(2), cur(2)]


_ROWS = pl.BlockSpec((BLK, KVW), lambda b: (b, 0))
NEG = -1e30
NT_DIMS = (((1,), (1,)), ((), ()))
TN_DIMS = (((0,), (0,)), ((), ()))


def _attn_fwd(qkv, d):
    nbs = L // d // BLK

    def body(q_ref, kp_ref, kc_ref, vp_ref, vc_ref, o_ref, lse_ref):
        valid = _attn_mask(nbs, pl.program_id(0))
        heads = [slice(h * HEAD_DIM, (h + 1) * HEAD_DIM) for h in range(HEADS)]
        kcs = [jnp.concatenate([kp_ref[:, hs], kc_ref[:, hs]], axis=0) for hs in heads]
        vcs = [jnp.concatenate([vp_ref[:, hs], vc_ref[:, hs]], axis=0) for hs in heads]
        ss = [lax.dot_general(q_ref[:, hs], kc, NT_DIMS, preferred_element_type=F32) for hs, kc in zip(heads, kcs)]
        ps, ls, lses = [], [], []
        for s in ss:
            s = jnp.where(valid, s, NEG)
            m = jnp.max(s, axis=-1, keepdims=True)
            p = jnp.exp(s - m)
            l = jnp.sum(p, axis=-1, keepdims=True)
            ps.append(p.astype(BF16))
            ls.append(l)
            lses.append(jnp.broadcast_to(m + jnp.log(l), (BLK, HEAD_DIM)))
        outs = [jnp.dot(p, vc, preferred_element_type=F32) / l for p, vc, l in zip(ps, vcs, ls)]
        o_ref[...] = jnp.concatenate(outs, axis=1)
        lse_ref[...] = jnp.concatenate(lses, axis=1)

    sh = jax.ShapeDtypeStruct((L, KVW), F32)
    q2 = qkv.reshape(L, QKV_W)
    o, lse = pl.pallas_call(
        body, out_shape=(sh, sh), grid=(NBLK,), in_specs=_qkv_specs(), out_specs=(_ROWS, _ROWS),
        name='attn_fwd_d%d' % d, compiler_params=_params(('parallel',), 32))(q2, q2, q2, q2, q2)
    return o.reshape(d, L // d, KVW), lse.reshape(d, L // d, KVW)


def _attn_bwd(qkv, o, lse, do, dlse, d):
    nbs = L // d // BLK

    def body(q_ref, kp_ref, kc_ref, vp_ref, vc_ref, o_ref, lse_ref, do_ref, dlse_ref, dq_ref, dk_ref, dv_ref):
        b = pl.program_id(0)

        @pl.when(b == 0)
        def _():
            dk_ref[...] = jnp.zeros_like(dk_ref)
            dv_ref[...] = jnp.zeros_like(dv_ref)

        valid = _attn_mask(nbs, b)
        prev0 = pl.multiple_of(jnp.maximum(b - 1, 0) * BLK, BLK)
        cur0 = pl.multiple_of(b * BLK, BLK)
        heads = [slice(h * HEAD_DIM, (h + 1) * HEAD_DIM) for h in range(HEADS)]
        qs = [q_ref[:, hs] for hs in heads]
        kcs = [jnp.concatenate([kp_ref[:, hs], kc_ref[:, hs]], axis=0) for hs in heads]
        vcs = [jnp.concatenate([vp_ref[:, hs], vc_ref[:, hs]], axis=0) for hs in heads]
        dos = [do_ref[:, hs] for hs in heads]
        do_bs = [t.astype(BF16) for t in dos]
        ss = [lax.dot_general(q, kc, NT_DIMS, preferred_element_type=F32) for q, kc in zip(qs, kcs)]
        dps = [lax.dot_general(do_b, vc, NT_DIMS, preferred_element_type=F32) for do_b, vc in zip(do_bs, vcs)]
        p_bs, ds_bs = [], []
        for h, hs in enumerate(heads):
            s = jnp.where(valid, ss[h], NEG)
            p = jnp.exp(s - lse_ref[:, h * HEAD_DIM:h * HEAD_DIM + 1])
            delta = jnp.sum(dos[h] * o_ref[:, hs], axis=-1, keepdims=True)
            ds = p * (dps[h] - delta + dlse_ref[:, h * HEAD_DIM:h * HEAD_DIM + 1])
            p_bs.append(p.astype(BF16))
            ds_bs.append(ds.astype(BF16))
        dqs = [jnp.dot(ds_b, kc, preferred_element_type=F32) for ds_b, kc in zip(ds_bs, kcs)]
        dks = [lax.dot_general(ds_b, q, TN_DIMS, preferred_element_type=F32) for ds_b, q in zip(ds_bs, qs)]
        dvs = [lax.dot_general(p_b, do_b, TN_DIMS, preferred_element_type=F32) for p_b, do_b in zip(p_bs, do_bs)]
        dq_ref[...] = jnp.concatenate(dqs, axis=1) * QK_SCALE
        dkc, dvc = jnp.concatenate(dks, axis=1), jnp.concatenate(dvs, axis=1)
        dk_ref[pl.ds(prev0, BLK), :] += dkc[:BLK]
        dv_ref[pl.ds(prev0, BLK), :] += dvc[:BLK]
        dk_ref[pl.ds(cur0, BLK), :] += dkc[BLK:]
        dv_ref[pl.ds(cur0, BLK), :] += dvc[BLK:]

    sh = jax.ShapeDtypeStruct((L, KVW), F32)
    whole = pl.BlockSpec((L, KVW), lambda b: (0, 0))
    q2 = qkv.reshape(L, QKV_W)
    flat = lambda t: t.reshape(L, KVW)
    outs = pl.pallas_call(
        body, out_shape=(sh, sh, sh), grid=(NBLK,), in_specs=_qkv_specs() + [_ROWS] * 4,
        out_specs=(_ROWS, whole, whole), name='attn_bwd_d%d' % d,
        compiler_params=_params(('arbitrary',), 48))(q2, q2, q2, q2, q2, flat(o), flat(lse), flat(do), flat(dlse))
    return tuple(t.reshape(d, L // d, KVW) for t in outs)


TC = 512


def _combine_weights(l0, l1, l2):
    m = jnp.maximum(jnp.maximum(l0, l1), l2)
    e0, e1, e2 = jnp.exp(l0 - m), jnp.exp(l1 - m), jnp.exp(l2 - m)
    z = e0 + e1 + e2
    return e0 / z, e1 / z, e2 / z


def _load_groups(refs, scrs):
    out = [refs[0][0]]
    for g in (1, 2):
        _scatter_rows(scrs[g - 1], refs[g], DILATIONS[g], TC)
        out.append(_token_order(scrs[g - 1]))
    return out


def _combine_fwd(os_, lses, g):
    def body(o0, o1, o2, l0, l1, l2, g_ref, att_ref, *scrs):
        ov = _load_groups((o0, o1, o2), scrs[0:2])
        lv = _load_groups((l0, l1, l2), scrs[2:4])
        w0, w1, w2 = _combine_weights(*lv)
        a = w0 * ov[0] + w1 * ov[1] + w2 * ov[2]
        att_ref[...] = ((a * _rms(a)) * g_ref[...]).astype(BF16)

    subs = [_sub_spec(d, TC, KVW) for d in DILATIONS]
    return pl.pallas_call(
        body, out_shape=jax.ShapeDtypeStruct((L, KVW), BF16), grid=(L // TC,), in_specs=subs + subs + [_vec(KVW)],
        out_specs=pl.BlockSpec((TC, KVW), lambda i: (i, 0)),
        scratch_shapes=[pltpu.VMEM((NB_KV, TC, 128), F32)] * 4, name='combine_fwd',
        compiler_params=_params(('parallel',), 40))(*os_, *lses, g)


def _combine_bwd(dcat, os_, lses, g, head_ones):
    def body(datt_ref, o0, o1, o2, l0, l1, l2, g_ref, e_ref, do0, do1, do2, dl0, dl1, dl2, dg_ref, *scrs):
        @pl.when(pl.program_id(0) == 0)
        def _():
            dg_ref[...] = jnp.zeros_like(dg_ref)

        ov = _load_groups((o0, o1, o2), scrs[0:2])
        lv = _load_groups((l0, l1, l2), scrs[2:4])
        ws = _combine_weights(*lv)
        a = ws[0] * ov[0] + ws[1] * ov[1] + ws[2] * ov[2]
        r = _rms(a)
        n = a * r
        dv = datt_ref[...].astype(F32)
        dg_ref[...] += _colsum(dv * n)
        da = _rms_bwd(dv * g_ref[...], n, r)
        e_b = e_ref[...].astype(BF16)

        def head_sum(t):
            hi = t.astype(BF16)
            lo = (t - hi.astype(F32)).astype(BF16)
            return jnp.dot(hi, e_b, preferred_element_type=F32) + jnp.dot(lo, e_b, preferred_element_type=F32)

        dws = [head_sum(da * ov[i]) for i in range(3)]
        dbar = ws[0] * dws[0] + ws[1] * dws[1] + ws[2] * dws[2]
        scr = scrs[4]
        for i, (d, do_ref, dl_ref) in enumerate(zip(DILATIONS, (do0, do1, do2), (dl0, dl1, dl2))):
            for val, out_ref in ((ws[i] * da, do_ref), (ws[i] * (dws[i] - dbar), dl_ref)):
                if d == 1:
                    out_ref[0] = val
                else:
                    for j in range(NB_KV):
                        scr[j] = val[:, j * 128:(j + 1) * 128]
                    for rr in range(d):
                        out_ref[rr] = _gather_rows(scr, range(NB_KV), rr, d, TC)
        _fold8(pl.program_id(0) == L // TC - 1, dg_ref)

    subs = [_sub_spec(d, TC, KVW) for d in DILATIONS]
    shs = tuple(jax.ShapeDtypeStruct((d, L // d, KVW), F32) for d in DILATIONS)
    outs = pl.pallas_call(
        body, out_shape=shs + shs + (jax.ShapeDtypeStruct((8, KVW), F32),), grid=(L // TC,),
        in_specs=[pl.BlockSpec((TC, KVW), lambda i: (i, 0))] + subs + subs + [_vec(KVW),
                                                                              pl.BlockSpec((KVW, KVW), lambda i: (0, 0))],
        out_specs=tuple(subs) + tuple(subs) + (_vec8(KVW),),
        scratch_shapes=[pltpu.VMEM((NB_KV, TC, 128), F32)] * 5, name='combine_bwd',
        compiler_params=_params(('arbitrary',), 48))(dcat, *os_, *lses, g, head_ones)
    return outs[0:3], outs[3:6], outs[6]


def _ssm_disc(ar, ai, ldt):
    dt = jnp.exp(ldt)
    zr, zi = ar * dt, ai * dt
    ez = jnp.exp(zr)
    A_r, A_i = ez * jnp.cos(zi), ez * jnp.sin(zi)
    den = ar * ar + ai * ai
    xr, xi = A_r - 1.0, A_i
    cr = (xr * ar + xi * ai) / den
    ci = (xi * ar - xr * ai) / den
    return dt, zr, zi, A_r, A_i, den, cr, ci


def _ssm_pre(ar, ai, ldt, br, bi):
    def body(ar_ref, ai_ref, ldt_ref, br_ref, bi_ref, bbr_ref, bbi_ref, pwr_ref, pwi_ref):
        _, zr, zi, _, _, _, cr, ci = _ssm_disc(ar_ref[...], ai_ref[...], ldt_ref[...])
        bbr_ref[...] = cr * br_ref[...] - ci * bi_ref[...]
        bbi_ref[...] = cr * bi_ref[...] + ci * br_ref[...]
        k = (lax.broadcasted_iota(jnp.int32, (1, 8), 1) + 1).astype(F32)
        ek = jnp.exp(zr * k)
        pwr_ref[...] = ek * jnp.cos(zi * k)
        pwi_ref[...] = ek * jnp.sin(zi * k)

    s16 = jax.ShapeDtypeStruct((SSM_GN, SSM_P), F32)
    s8 = jax.ShapeDtypeStruct((SSM_GN, 8), F32)
    return pl.pallas_call(body, out_shape=(s16, s16, s8, s8), name='ssm_pre',
                          compiler_params=_params(None, 40))(ar, ai, ldt, br, bi)


def _ssm_post(ar, ai, ldt, br, bi, gar, gai, gbr, gbi, sel):
    def body(ar_ref, ai_ref, ldt_ref, br_ref, bi_ref, gar_ref, gai_ref, gbr_ref, gbi_ref, sel_ref,
             dar_ref, dai_ref, dbr_ref, dbi_ref, dldt_ref):
        a_r, a_i = ar_ref[...], ai_ref[...]
        dt, _, _, A_r, A_i, den, cr, ci = _ssm_disc(a_r, a_i, ldt_ref[...])
        b_r, b_i, g_br, g_bi = br_ref[...], bi_ref[...], gbr_ref[...], gbi_ref[...]
        gcr = jnp.sum(g_br * b_r + g_bi * b_i, axis=-1, keepdims=True)
        gci = jnp.sum(g_bi * b_r - g_br * b_i, axis=-1, keepdims=True)
        dbr_ref[...] = g_br * cr + g_bi * ci
        dbi_ref[...] = g_bi * cr - g_br * ci
        g_ar = gar_ref[...] + (gcr * a_r - gci * a_i) / den
        g_ai = gai_ref[...] + (gcr * a_i + gci * a_r) / den
        qr = (cr * a_r + ci * a_i) / den
        qi = (ci * a_r - cr * a_i) / den
        glr = -(gcr * qr + gci * qi)
        gli = -(gci * qr - gcr * qi)
        gzr = g_ar * A_r + g_ai * A_i
        gzi = g_ai * A_r - g_ar * A_i
        dar_ref[...] = glr + gzr * dt
        dai_ref[...] = gli + gzi * dt
        gdt = (gzr * a_r + gzi * a_i) * dt
        dldt_ref[...] = jnp.dot(sel_ref[...], jnp.broadcast_to(gdt, (SSM_GN, 128)),
                                preferred_element_type=F32, precision=HI)

    s1 = jax.ShapeDtypeStruct((SSM_GN, 1), F32)
    s16 = jax.ShapeDtypeStruct((SSM_GN, SSM_P), F32)
    return pl.pallas_call(body, out_shape=(s1, s1, s16, s16, jax.ShapeDtypeStruct((SSM_G, 128), F32)),
                          name='ssm_post', compiler_params=_params(None, 48))(
                              ar, ai, ldt, br, bi, gar, gai, gbr, gbi, sel)


SCAN_CH = 8


def _scan_fwd_tiles(s_ref, pw, carry):
    pwr, pwi = pw[:, :CL_S], pw[:, CL_S:]
    row = lax.broadcasted_iota(jnp.int32, (8, CL_S), 0)
    steps = [(k, jnp.where(row >= k, pwr[k - 1:k], 0.0), jnp.where(row >= k, pwi[k - 1:k], 0.0)) for k in (1, 2, 4)]
    rows = 8 * SCAN_CH

    def chunk(i, c):
        cr, ci = c
        r0 = pl.multiple_of(i * rows, rows)
        xr = s_ref[pl.ds(r0, rows), 0:CL_S].reshape(SCAN_CH, 8, CL_S)
        xi = s_ref[pl.ds(r0, rows), CL_S:2 * CL_S].reshape(SCAN_CH, 8, CL_S)
        for k, pr, pi in steps:
            sr, si = pltpu.roll(xr, k, 1), pltpu.roll(xi, k, 1)
            xr, xi = xr + pr * sr - pi * si, xi + pr * si + pi * sr
        for j in range(SCAN_CH):
            tr = xr[j] + pwr * cr - pwi * ci
            ti = xi[j] + pwr * ci + pwi * cr
            s_ref[pl.ds(r0 + 8 * j, 8), 0:CL_S] = tr
            s_ref[pl.ds(r0 + 8 * j, 8), CL_S:2 * CL_S] = ti
            cr, ci = tr[7:8], ti[7:8]
        return cr, ci

    return lax.fori_loop(0, T_SCAN // rows, chunk, (carry[:, :CL_S], carry[:, CL_S:]))


def _scan_bwd_tiles(l_ref, pw, carry):
    pwr, pwi = pw[:, :CL_S], pw[:, CL_S:]
    rpr = jnp.concatenate([pwr[7 - r:8 - r] for r in range(8)], axis=0)
    rpi = jnp.concatenate([pwi[7 - r:8 - r] for r in range(8)], axis=0)
    row = lax.broadcasted_iota(jnp.int32, (8, CL_S), 0)
    steps = [(k, jnp.where(row < 8 - k, pwr[k - 1:k], 0.0), jnp.where(row < 8 - k, pwi[k - 1:k], 0.0))
             for k in (1, 2, 4)]
    rows = 8 * SCAN_CH
    nc = T_SCAN // rows

    def chunk(i, c):
        cr, ci = c
        r0 = pl.multiple_of((nc - 1 - i) * rows, rows)
        xr = l_ref[pl.ds(r0, rows), 0:CL_S].reshape(SCAN_CH, 8, CL_S)
        xi = l_ref[pl.ds(r0, rows), CL_S:2 * CL_S].reshape(SCAN_CH, 8, CL_S)
        for k, pr, pi in steps:
            sr, si = pltpu.roll(xr, 8 - k, 1), pltpu.roll(xi, 8 - k, 1)
            xr, xi = xr + pr * sr + pi * si, xi + pr * si - pi * sr
        for j in reversed(range(SCAN_CH)):
            tr = xr[j] + rpr * cr + rpi * ci
            ti = xi[j] + rpr * ci - rpi * cr
            l_ref[pl.ds(r0 + 8 * j, 8), 0:CL_S] = tr
            l_ref[pl.ds(r0 + 8 * j, 8), CL_S:2 * CL_S] = ti
            cr, ci = tr[0:1], ti[0:1]
        return cr, ci

    return lax.fori_loop(0, nc, chunk, (carry[:, :CL_S], carry[:, CL_S:]))


NT_SCAN = L // T_SCAN


def _cl_spec(r, c):
    return pl.BlockSpec((None, r, c), lambda c_, t: (c_, 0, 0))


def _ssm_fwd(u, bm, cm, pw, dvec, u_off=0):
    def body(u_ref, bm_ref, cm_ref, pw_ref, d_ref, y_ref, bnd_ref, s_ref, carry_ref):
        @pl.when(pl.program_id(1) == 0)
        def _():
            carry_ref[...] = jnp.zeros_like(carry_ref)

        bnd_ref[...] = carry_ref[...]
        uv = u_ref[...]
        s_ref[...] = jnp.dot(uv.astype(BF16), bm_ref[...], preferred_element_type=F32)
        cr, ci = _scan_fwd_tiles(s_ref, pw_ref[...], carry_ref[...])
        carry_ref[...] = jnp.concatenate([cr, ci], axis=1)
        y_ref[...] = jnp.dot(s_ref[...].astype(BF16), cm_ref[...], preferred_element_type=F32) + d_ref[...] * uv

    return pl.pallas_call(
        body,
        out_shape=(jax.ShapeDtypeStruct((L, SSMW), F32), jax.ShapeDtypeStruct((N_CL, NT_SCAN, 1, 2 * CL_S), F32),
                   jax.ShapeDtypeStruct((L, N_CL * 2 * CL_S), F32)),
        grid=(N_CL, NT_SCAN),
        in_specs=[pl.BlockSpec((T_SCAN, CL_U), lambda c, t: (t, c + u_off)),
                  _cl_spec(CL_U, 2 * CL_S), _cl_spec(2 * CL_S, CL_U),
                  pl.BlockSpec((None, 8, 2 * CL_S), lambda c, t: (c, 0, 0)),
                  pl.BlockSpec((1, CL_U), lambda c, t: (0, c))],
        out_specs=(pl.BlockSpec((T_SCAN, CL_U), lambda c, t: (t, c)),
                   pl.BlockSpec((None, None, 1, 2 * CL_S), lambda c, t: (c, t, 0, 0)),
                   pl.BlockSpec((T_SCAN, 2 * CL_S), lambda c, t: (t, c))),
        scratch_shapes=[pltpu.VMEM((1, 2 * CL_S), F32)],
        name='ssm_fwd', compiler_params=_params(('arbitrary', 'arbitrary'), 40))(u, bm, cm, pw, dvec)


def _ssm_bwd(u, dy, states, bmt, cmt, pw, dvec, bnd, u_off=0):
    rev = lambda t: NT_SCAN - 1 - t

    def body(u_ref, dy_ref, s_ref, bmt_ref, cmt_ref, pw_ref, d_ref, bnd_ref,
             du_ref, dbm_ref, dcm_ref, da_ref, dd_ref, l_ref, carry_ref):
        @pl.when(pl.program_id(1) == 0)
        def _():
            carry_ref[...] = jnp.zeros_like(carry_ref)
            dbm_ref[...] = jnp.zeros_like(dbm_ref)
            dcm_ref[...] = jnp.zeros_like(dcm_ref)
            da_ref[...] = jnp.zeros_like(da_ref)
            dd_ref[...] = jnp.zeros_like(dd_ref)

        uv, dyv, pw = u_ref[...], dy_ref[...], pw_ref[...]
        dy_b = dyv.astype(BF16)
        entry = bnd_ref[...]
        l_ref[...] = jnp.dot(dy_b, cmt_ref[...], preferred_element_type=F32)
        cr, ci = _scan_bwd_tiles(l_ref, pw, carry_ref[...])
        carry_ref[...] = jnp.concatenate([cr, ci], axis=1)
        sv, lv = s_ref[...], l_ref[...]
        lv_b = lv.astype(BF16)
        du_ref[...] = dyv * d_ref[...] + jnp.dot(lv_b, bmt_ref[...], preferred_element_type=F32)
        dbm_ref[...] += lax.dot_general(uv.astype(BF16), lv_b, TN_DIMS, preferred_element_type=F32)
        dcm_ref[...] += lax.dot_general(sv.astype(BF16), dy_b, TN_DIMS, preferred_element_type=F32)
        dd_ref[...] += _colsum(dyv * uv)
        row = lax.broadcasted_iota(jnp.int32, (T_SCAN, 2 * CL_S), 0)
        sp = jnp.where(row == 0, entry, pltpu.roll(sv, 1, 0))
        spr, spi = sp[:, :CL_S], sp[:, CL_S:]
        lr, li = lv[:, :CL_S], lv[:, CL_S:]
        da_ref[:, 0:CL_S] += _colsum(lr * spr + li * spi)
        da_ref[:, CL_S:2 * CL_S] += _colsum(li * spr - lr * spi)
        _fold8(pl.program_id(1) == NT_SCAN - 1, da_ref, dd_ref)

    return pl.pallas_call(
        body,
        out_shape=(jax.ShapeDtypeStruct((L, SSMW), F32), jax.ShapeDtypeStruct((N_CL, CL_U, 2 * CL_S), F32),
                   jax.ShapeDtypeStruct((N_CL, 2 * CL_S, CL_U), F32), jax.ShapeDtypeStruct((N_CL, 8, 2 * CL_S), F32),
                   jax.ShapeDtypeStruct((8, SSMW), F32)),
        grid=(N_CL, NT_SCAN),
        in_specs=[pl.BlockSpec((T_SCAN, CL_U), lambda c, t: (rev(t), c + u_off)),
                  pl.BlockSpec((T_SCAN, CL_U), lambda c, t: (rev(t), c)),
                  pl.BlockSpec((T_SCAN, 2 * CL_S), lambda c, t: (rev(t), c)),
                  pl.BlockSpec((None, 2 * CL_S, CL_U), lambda c, t: (c, 0, 0)),
                  pl.BlockSpec((None, CL_U, 2 * CL_S), lambda c, t: (c, 0, 0)),
                  pl.BlockSpec((None, 8, 2 * CL_S), lambda c, t: (c, 0, 0)),
                  pl.BlockSpec((1, CL_U), lambda c, t: (0, c)),
                  pl.BlockSpec((None, None, 1, 2 * CL_S), lambda c, t: (c, rev(t), 0, 0))],
        out_specs=(pl.BlockSpec((T_SCAN, CL_U), lambda c, t: (rev(t), c)),
                   pl.BlockSpec((None, CL_U, 2 * CL_S), lambda c, t: (c, 0, 0)),
                   pl.BlockSpec((None, 2 * CL_S, CL_U), lambda c, t: (c, 0, 0)),
                   pl.BlockSpec((None, 8, 2 * CL_S), lambda c, t: (c, 0, 0)),
                   pl.BlockSpec((8, CL_U), lambda c, t: (0, c))),
        scratch_shapes=[pltpu.VMEM((T_SCAN, 2 * CL_S), F32), pltpu.VMEM((1, 2 * CL_S), F32)],
        name='ssm_bwd', compiler_params=_params(('arbitrary', 'arbitrary'), 48))(u, dy, states, bmt, cmt, pw, dvec, bnd)


GELU_C = math.sqrt(2.0 / math.pi)
GELU_K = 0.044715


def _gelu_parts(x):
    t = jnp.tanh(GELU_C * (x + GELU_K * (x * x * x)))
    return x * (0.5 * (1.0 + t)), t


def _glu_fwd(ypre, wglu, bglu, gs):
    def body(y_ref, w_ref, b_ref, g_ref, o_ref):
        yg, _ = _gelu_parts(y_ref[...])
        z = jnp.dot(yg.astype(BF16), w_ref[...], preferred_element_type=F32) + b_ref[...]
        s = yg * jax.nn.sigmoid(z)
        o_ref[...] = ((s * _rms(s)) * g_ref[...]).astype(BF16)

    return pl.pallas_call(
        body, out_shape=jax.ShapeDtypeStruct((L, SSMW), BF16), grid=(L // TR,),
        in_specs=[_rowspec(SSMW), pl.BlockSpec((SSMW, SSMW), lambda i: (0, 0)), _vec(SSMW), _vec(SSMW)],
        out_specs=_rowspec(SSMW), name='glu_fwd', compiler_params=_params(('parallel',), 32))(ypre, wglu, bglu, gs)


def _glu_bwd(ypre, dsn, wglu, bglu, gs):
    def body(y_ref, d_ref, w_ref, b_ref, g_ref, dy_ref, dw_ref, db_ref, dg_ref):
        @pl.when(pl.program_id(0) == 0)
        def _():
            dw_ref[...] = jnp.zeros_like(dw_ref)
            db_ref[...] = jnp.zeros_like(db_ref)
            dg_ref[...] = jnp.zeros_like(dg_ref)

        xv = y_ref[...]
        yg, t = _gelu_parts(xv)
        yg_b = yg.astype(BF16)
        z = jnp.dot(yg_b, w_ref[...], preferred_element_type=F32) + b_ref[...]
        sg = jax.nn.sigmoid(z)
        s = yg * sg
        r = _rms(s)
        n = s * r
        dv = d_ref[:, d_ref.shape[1] - SSMW:].astype(F32)
        dg_ref[...] += _colsum(dv * n)
        ds = _rms_bwd(dv * g_ref[...], n, r)
        dz =(ds * yg) * (sg * (1.0 - sg))
        dz_b = dz.astype(BF16)
        db_ref[...] += _colsum(dz)
        dw_ref[...] += lax.dot_general(yg_b, dz_b, TN_DIMS, preferred_element_type=F32)
        dyg = ds * sg + lax.dot_general(dz_b, w_ref[...], NT_DIMS, preferred_element_type=F32)
        dgelu = 0.5 * (1.0 + t) + (0.5 * xv) * (1.0 - t * t) * (GELU_C * (1.0 + 3.0 * GELU_K * (xv * xv)))
        dy_ref[...] = dyg * dgelu
        _fold8(pl.program_id(0) == L // TR - 1, db_ref, dg_ref)

    vs = jax.ShapeDtypeStruct((8, SSMW), F32)
    return pl.pallas_call(
        body, out_shape=(jax.ShapeDtypeStruct((L, SSMW), F32), jax.ShapeDtypeStruct((SSMW, SSMW), F32), vs, vs),
        grid=(L // TR,),
        in_specs=[_rowspec(SSMW), _rowspec(dsn.shape[1]), pl.BlockSpec((SSMW, SSMW), lambda i: (0, 0)), _vec(SSMW),
                  _vec(SSMW)],
        out_specs=(_rowspec(SSMW), pl.BlockSpec((SSMW, SSMW), lambda i: (0, 0)), _vec8(SSMW), _vec8(SSMW)),
        name='glu_bwd', compiler_params=_params(('arbitrary',), 40))(ypre, dsn, wglu, bglu, gs)


def _me():
    return lax.axis_index('x'), lax.axis_index('y'), lax.axis_index('c')


def _my_index():
    return 4 * lax.axis_index('x') + 2 * lax.axis_index('y') + lax.axis_index('c')


def _peer(k):
    x, y, c = _me()
    px = 1 - x if k & 4 else x
    py = 1 - y if k & 2 else y
    pc = 1 - c if k & 1 else c
    return (px, py, pc), 4 * px + 2 * py + pc


def _mod_exchange(c_row, w_ada, b_ada8, deps=()):
    cw = NMOD * D // N_DEV

    def body(c_ref, w_ref, b_ref, *rest):
        call_ref, mod_ref, part_ref, send_sems, recv_sems = rest[len(deps):]
        x, y, c = _me()
        me = 4 * x + 2 * y + c
        call_ref[me] = c_ref[0]
        sends = []
        for k in range(1, N_DEV):
            peer, _ = _peer(k)
            cp = pltpu.make_async_remote_copy(src_ref=c_ref.at[0], dst_ref=call_ref.at[me], send_sem=send_sems.at[0, k - 1],
                                              recv_sem=recv_sems.at[0, k - 1], device_id=peer, device_id_type=MESH)
            cp.start()
            sends.append(cp)
        for k in range(1, N_DEV):
            peer, pidx = _peer(k)
            pltpu.make_async_remote_copy(src_ref=c_ref.at[0], dst_ref=call_ref.at[pidx], send_sem=send_sems.at[0, k - 1],
                                         recv_sem=recv_sems.at[0, k - 1], device_id=peer, device_id_type=MESH).wait_recv()
        for cp in sends:
            cp.wait_send()
        cv = call_ref[...].reshape(N_DEV, D)
        part = jnp.dot(cv * jax.nn.sigmoid(cv), w_ref[...], preferred_element_type=F32, precision=HI)
        part_ref[...] = part.reshape(N_DEV, 1, cw)
        mod_ref[me] = part_ref[me]
        sends = []
        for k in range(1, N_DEV):
            peer, pidx = _peer(k)
            cp = pltpu.make_async_remote_copy(src_ref=part_ref.at[pidx], dst_ref=mod_ref.at[me], send_sem=send_sems.at[1, k - 1],
                                              recv_sem=recv_sems.at[1, k - 1], device_id=peer, device_id_type=MESH)
            cp.start()
            sends.append(cp)
        for k in range(1, N_DEV):
            peer, pidx = _peer(k)
            pltpu.make_async_remote_copy(src_ref=part_ref.at[pidx], dst_ref=mod_ref.at[pidx], send_sem=send_sems.at[1, k - 1],
                                         recv_sem=recv_sems.at[1, k - 1], device_id=peer, device_id_type=MESH).wait_recv()
        for cp in sends:
            cp.wait_send()
        mod_ref[...] = mod_ref[...] + b_ref[...]

    vm = pl.BlockSpec(memory_space=pltpu.VMEM)
    return pl.pallas_call(
        body, out_shape=(jax.ShapeDtypeStruct((N_DEV, 1, D), F32), jax.ShapeDtypeStruct((N_DEV, 1, cw), F32)),
        in_specs=[vm, vm, vm] + [pl.BlockSpec(memory_space=pl.ANY)] * len(deps), out_specs=(vm, vm),
        scratch_shapes=[pltpu.VMEM((N_DEV, 1, cw), F32), pltpu.SemaphoreType.DMA((2, N_DEV - 1)),
                        pltpu.SemaphoreType.DMA((2, N_DEV - 1))],
        name='mod_exchange', compiler_params=_params(None, 48))(c_row, w_ada, b_ada8, *deps)


HBM_SPEC = pl.BlockSpec(memory_space=pltpu.HBM)
SEM_SPEC = pl.BlockSpec(memory_space=pltpu.SEMAPHORE)
DATAFLOW = pltpu.SideEffectType.DATAFLOW_SIDE_EFFECTING


def _push_start(src, scatter, after, name):
    land = lax.empty(src.shape if scatter else (N_DEV,) + src.shape, src.dtype)

    def body(src_ref, land_ref, after_ref, send_sem, recv_sem, land_thru, token):
        x, y, c = _me()
        me = 4 * x + 2 * y + c
        for k in range(1, N_DEV):
            peer, pidx = _peer(k)
            pltpu.make_async_remote_copy(src_ref=src_ref.at[pidx] if scatter else src_ref, dst_ref=land_ref.at[me],
                                         send_sem=send_sem, recv_sem=recv_sem, device_id=peer,
                                         device_id_type=MESH).start()
        token[...] = jnp.zeros_like(token)

    own = lax.dynamic_index_in_dim(src, _my_index(), 0, keepdims=False) if scatter else src
    src = pltpu.with_memory_space_constraint(src, pltpu.HBM)
    send_sem, recv_sem, land_thru, token = pl.pallas_call(
        body, name=name,
        out_shape=(pltpu.SemaphoreType.DMA(()), pltpu.SemaphoreType.DMA(()),
                   pltpu.HBM(land.shape, land.dtype), jax.ShapeDtypeStruct((8, 128), F32)),
        in_specs=(HBM_SPEC, HBM_SPEC, pl.BlockSpec(memory_space=pl.ANY)),
        out_specs=(SEM_SPEC, SEM_SPEC, HBM_SPEC, pl.BlockSpec(memory_space=pltpu.VMEM)),
        input_output_aliases={1: 2}, compiler_params=pltpu.CompilerParams(has_side_effects=DATAFLOW),
    )(src, pltpu.with_memory_space_constraint(land, pltpu.HBM), after)
    return send_sem, recv_sem, src, land_thru, token, own


def _push_wait(handle, after, name, merge=True):
    send_sem, recv_sem, src, land_thru, _, own = handle
    after = tuple(after) if isinstance(after, (tuple, list)) else (after,)

    def body(src_ref, land_ref, send_sem, recv_sem, *rest):
        seven = land_ref.at[pl.ds(0, N_DEV - 1)]
        cp = pltpu.make_async_remote_copy(src_ref=seven, dst_ref=seven, send_sem=send_sem, recv_sem=recv_sem,
                                          device_id=_me(), device_id_type=MESH)
        cp.wait_send()
        cp.wait_recv()

    landed = pl.pallas_call(
        body, name=name, out_shape=pltpu.HBM(land_thru.shape, land_thru.dtype),
        in_specs=(HBM_SPEC, HBM_SPEC, SEM_SPEC, SEM_SPEC) + (pl.BlockSpec(memory_space=pl.ANY),) * len(after),
        out_specs=HBM_SPEC, input_output_aliases={1: 0},
        compiler_params=pltpu.CompilerParams(has_side_effects=DATAFLOW),
    )(src, land_thru, send_sem, recv_sem, *after)
    if not merge:
        return landed, own
    return lax.dynamic_update_index_in_dim(landed, own, _my_index(), 0)


def _adam(w, g, m, v):
    m2 = B1 * m + (1.0 - B1) * g
    v2 = B2 * v + (1.0 - B2) * jnp.square(g)
    m_hat = m2 / (1.0 - B1 ** STEP)
    v_hat = v2 / (1.0 - B2 ** STEP)
    delta = -LR * (m_hat / (jnp.sqrt(v_hat) + AEPS) + WD * w)
    return delta, m2, v2


def _small_update(gp, wp, mp, vp):
    def body(g_ref, w_ref, m_ref, v_ref, all_ref, go_ref, d_ref, mo_ref, vo_ref, send_sems, recv_sems):
        x, y, c = _me()
        me = 4 * x + 2 * y + c
        all_ref[me] = g_ref[...]
        sends = []
        for k in range(1, N_DEV):
            peer, _ = _peer(k)
            cp = pltpu.make_async_remote_copy(src_ref=g_ref, dst_ref=all_ref.at[me], send_sem=send_sems.at[k - 1],
                                              recv_sem=recv_sems.at[k - 1], device_id=peer, device_id_type=MESH)
            cp.start()
            sends.append(cp)
        for k in range(1, N_DEV):
            peer, pidx = _peer(k)
            pltpu.make_async_remote_copy(src_ref=g_ref, dst_ref=all_ref.at[pidx], send_sem=send_sems.at[k - 1],
                                         recv_sem=recv_sems.at[k - 1], device_id=peer, device_id_type=MESH).wait_recv()
        for cp in sends:
            cp.wait_send()
        g = all_ref[0]
        for d in range(1, N_DEV):
            g = g + all_ref[d]
        delta, m2, v2 = _adam(w_ref[...], g, m_ref[...], v_ref[...])
        go_ref[...] = g
        d_ref[...] = delta
        mo_ref[...] = m2
        vo_ref[...] = v2

    vm = pl.BlockSpec(memory_space=pltpu.VMEM)
    vs = jax.ShapeDtypeStruct(gp.shape, F32)
    return pl.pallas_call(
        body, out_shape=(jax.ShapeDtypeStruct((N_DEV,) + gp.shape, F32), vs, vs, vs, vs), in_specs=[vm] * 4,
        out_specs=(vm,) * 5,
        scratch_shapes=[pltpu.SemaphoreType.DMA((N_DEV - 1,)), pltpu.SemaphoreType.DMA((N_DEV - 1,))],
        name='small_update', compiler_params=_params(None, 48))(gp, wp, mp, vp)


def _small_sum_update(parts, wp, mp, vp):
    def body(p_ref, w_ref, m_ref, v_ref, go_ref, d_ref, mo_ref, vo_ref):
        g = p_ref[0]
        for d in range(1, N_DEV):
            g = g + p_ref[d]
        delta, m2, v2 = _adam(w_ref[...], g, m_ref[...], v_ref[...])
        go_ref[...] = g
        d_ref[...] = delta
        mo_ref[...] = m2
        vo_ref[...] = v2

    vm = pl.BlockSpec(memory_space=pltpu.VMEM)
    vs = jax.ShapeDtypeStruct(wp.shape, F32)
    return pl.pallas_call(body, out_shape=(vs, vs, vs, vs), in_specs=[vm] * 4, out_specs=(vm,) * 4,
                          name='small_sum_update', compiler_params=_params(None, 48))(parts, wp, mp, vp)


def _big_update(parts, own, w, m, v, name):
    _, R, C = parts.shape
    tr = R if R % 256 else (128 if C >= 2048 else 256)

    def body(me_ref, p_ref, own_ref, w_ref, m_ref, v_ref, g_ref, d_ref, mo_ref, vo_ref):
        me = me_ref[0]
        mine = own_ref[...].astype(F32)
        g = jnp.where(me == 0, mine, p_ref[0].astype(F32))
        for d in range(1, N_DEV):
            g = g + jnp.where(me == d, mine, p_ref[d].astype(F32))
        delta, m2, v2 = _adam(w_ref[...], g, m_ref[...], v_ref[...])
        g_ref[...] = g
        d_ref[...] = delta
        mo_ref[...] = m2
        vo_ref[...] = v2

    blk = pl.BlockSpec((tr, C), lambda i: (i, 0))
    sh = jax.ShapeDtypeStruct((R, C), F32)
    return pl.pallas_call(
        body, out_shape=(sh, sh, sh, sh), grid=(R // tr,),
        in_specs=[pl.BlockSpec(memory_space=pltpu.SMEM), pl.BlockSpec((N_DEV, tr, C), lambda i: (0, i, 0)), blk, blk,
                  blk, blk], out_specs=(blk,) * 4,
        name=name, compiler_params=_params(('parallel',), 48))(_my_index().reshape(1), parts, own, w, m, v)


def _ada_update(c_all, dmod_cols, w, m, v):
    C = w.shape[1]
    tr = 256

    def body(c_ref, dm_ref, w_ref, m_ref, v_ref, g_ref, d_ref, mo_ref, vo_ref):
        cv = c_ref[...]
        s = cv * jax.nn.sigmoid(cv)
        g = lax.dot_general(s, dm_ref[...], TN_DIMS, preferred_element_type=F32, precision=HI)
        delta, m2, v2 = _adam(w_ref[...], g, m_ref[...], v_ref[...])
        g_ref[...] = g
        d_ref[...] = delta
        mo_ref[...] = m2
        vo_ref[...] = v2

    blk = pl.BlockSpec((tr, C), lambda i: (i, 0))
    sh = jax.ShapeDtypeStruct((D, C), F32)
    return pl.pallas_call(
        body, out_shape=(sh, sh, sh, sh), grid=(D // tr,),
        in_specs=[pl.BlockSpec((N_DEV, tr), lambda i: (0, i)), pl.BlockSpec((N_DEV, C), lambda i: (0, 0)), blk, blk, blk],
        out_specs=(blk,) * 4, name='ada_update', compiler_params=_params(('parallel',), 48))(c_all, dmod_cols, w, m, v)


def _rows_to_cluster_lanes(t):
    k = t.shape[1]
    return t.reshape(N_CL, CL_S, k).transpose(0, 2, 1)


def _blockdiag_in(t):
    t = t.reshape(N_CL, CL_G, SSM_N, SSM_P).transpose(0, 1, 3, 2)
    eye = jnp.eye(CL_G, dtype=t.dtype)
    t = t[:, :, :, None, :] * eye[None, :, None, :, None]
    return t.reshape(N_CL, CL_U, CL_S)


def _blockdiag_extract(t):
    t = t.reshape(N_CL, CL_G, SSM_P, CL_G, SSM_N)
    eye = jnp.eye(CL_G, dtype=t.dtype)
    t = jnp.sum(t * eye[None, :, None, :, None], axis=3)
    return t.transpose(0, 1, 3, 2).reshape(SSM_GN, SSM_P)


def _c_to_rows(t):
    return t.transpose(0, 2, 1).reshape(SSM_GN, SSM_P)


def _rows_to_c(t):
    return t.reshape(SSM_G, SSM_N, SSM_P).transpose(0, 2, 1)


def _ssm_prep(sp):
    rows = lambda n: sp[n].reshape(SSM_GN, 1)
    a_re, a_im = rows('ssm_a_re'), rows('ssm_a_im')
    ldt = jnp.repeat(sp['ssm_log_dt'].reshape(SSM_G, 1), SSM_N, axis=0)
    b_re, b_im = sp['ssm_b_re'].reshape(SSM_GN, SSM_P), sp['ssm_b_im'].reshape(SSM_GN, SSM_P)
    c_re, c_im = _c_to_rows(sp['ssm_c_re'].reshape(SSM_G, SSM_P, SSM_N)), _c_to_rows(sp['ssm_c_im'].reshape(SSM_G, SSM_P, SSM_N))
    bbr, bbi, pwr, pwi = _ssm_pre(a_re, a_im, ldt, b_re, b_im)
    bm = jnp.concatenate([_blockdiag_in(bbr), _blockdiag_in(bbi)], axis=2).astype(BF16)
    cmt = jnp.concatenate([_blockdiag_in(c_re), -_blockdiag_in(c_im)], axis=2).astype(BF16)
    bmt, cm = bm.transpose(0, 2, 1), cmt.transpose(0, 2, 1)
    pw = jnp.concatenate([_rows_to_cluster_lanes(pwr), _rows_to_cluster_lanes(pwi)], axis=2)
    return a_re, a_im, ldt, b_re, b_im, bm, cm, bmt, cmt, pw


def _tied(v, deps):
    for t in deps:
        v = v + t[0, 0]
    return v


def _local_step(x, pos, mod, tgt, sp, prep, get_w, emit, emit_small, emit_late, first_deps=()):
    sh1, sc1, gt1, sh2, sc2, gt2 = (mod[i:i + 1] for i in range(NMOD))
    vec = lambda n: sp[n].reshape(1, -1)
    a_re, a_im, ldt, b_re, b_im, bm_b, cm_b, bmt_b, cmt_b, pw = prep
    dvec = vec('ssm_d')

    h1 = _prenorm_fwd(x, vec('g_pre_mix'), sc1, sh1)
    w_in = get_w('w_in', (h1, bm_b, cm_b, pw, bmt_b, cmt_b))
    proj = _mm(h1, w_in, mode='nn', name='mm_in', tm=512, tn=INW, deps=first_deps)
    fr1 =ROPE_THETA ** (-jnp.arange(0, ROT_DIM, 2, dtype=F32) / ROT_DIM)
    lane = jnp.arange(128) % HEAD_DIM
    fr = jnp.where(lane < ROT_DIM, fr1[lane % (ROT_DIM // 2)], 0.0).reshape(1, 128).astype(F32)
    u_off = (ROPE_W + KVW) // CL_U
    qkvs = _rope_fwd(proj, pos, fr)
    fwd = [_attn_fwd(qkvs[g], d) for g, d in enumerate(DILATIONS)]
    os_, lses = [t[0] for t in fwd], [t[1] for t in fwd]
    att = _combine_fwd(os_, lses, vec('g_attn_out'))

    ypre, bnd, states = _ssm_fwd(proj, bm_b, cm_b, pw, dvec, u_off)
    w_glu = get_w('w_glu', ypre)
    ssm_n = _glu_fwd(ypre, w_glu, vec('b_glu'), vec('g_ssm_out'))

    cat = jnp.concatenate([att, ssm_n], axis=1)
    w_out = get_w('w_out', cat)
    mix = _mm(cat, w_out, mode='nn', name='mm_out', tk=1280)
    x1, h2 = _postmix_fwd(x, mix, vec('g_post_mix'), gt1, vec('g_pre_mlp'), sc2, sh2)
    w_mi = get_w('w_mlp_in', h2)
    a_pre, r_act = _mm(h2, w_mi, mode='nn', name='mm_mlp_in', epilogue='relu2', b_sharded=True)
    w_mo = get_w('w_mlp_out', a_pre)
    y = _mm(r_act, w_mo, mode='nn', name='mm_mlp_out')
    dx2, dy, loss, dgt2, dg_post_mlp = _final_fwd_bwd(x1, y, tgt, vec('g_post_mlp'), gt2)
    dgt2, dg_post_mlp = dgt2[:1], dg_post_mlp[:1]

    da = _mm(dy, w_mo, mode='nt', name='mm_d_act', out_dtype=BF16, epilogue='drelu2', extra=a_pre)
    dep = emit('w_mlp_out', _mm(r_act, dy, mode='tn', name='mm_dw_mlp_out', out_dtype=BF16))
    dh2 = _mm(da, w_mi, mode='nt', name='mm_dh2', tk=2048, out_dtype=BF16, b_sharded=True, deps=dep)
    dep = emit('w_mlp_in', _mm(h2, da, mode='tn', name='mm_dw_mlp_in', out_dtype=BF16, out_sharded=True))
    dx1, dmix, dsc2, dsh2, dg_pre_mlp, dgt1, dg_post_mix = _postmix_bwd(
        dx2, dh2, x1, mix, vec('g_post_mix'), gt1, vec('g_pre_mlp'), sc2)
    dsc2, dsh2, dg_pre_mlp, dgt1, dg_post_mix = (t[:1] for t in (dsc2, dsh2, dg_pre_mlp, dgt1, dg_post_mix))
    dcat = _mm(dmix, w_out, mode='nt', name='mm_dcat', tn=1280, out_dtype=BF16, deps=dep)
    dep = emit('w_out', _mm(cat, dmix, mode='tn', name='mm_dw_out', out_dtype=BF16, tm=640))
    dypre, g_w_glu, g_b_glu, g_g_ssm = _glu_bwd(ypre, dcat, w_glu, _tied(vec('b_glu'), dep), vec('g_ssm_out'))
    g_b_glu, g_g_ssm = g_b_glu[:1], g_g_ssm[:1]
    dep = dep + emit('w_glu', g_w_glu.astype(BF16))
    du, dbm, dcm, dA, dD = _ssm_bwd(proj, dypre, states, bmt_b, cmt_b, pw, dvec, bnd, u_off)
    dD = dD[:1]
    gbr, gbi = _blockdiag_extract(dbm[:, :, :CL_S]), _blockdiag_extract(dbm[:, :, CL_S:])
    dcmt = dcm.transpose(0, 2, 1)
    g_c_re = _rows_to_c(_blockdiag_extract(dcmt[:, :, :CL_S]))
    g_c_im = _rows_to_c(-_blockdiag_extract(dcmt[:, :, CL_S:]))
    gar = dA[:, 0, :CL_S].reshape(SSM_GN, 1)
    gai = dA[:, 0, CL_S:].reshape(SSM_GN, 1)
    sel = (jnp.arange(SSM_GN)[None, :] // SSM_N == jnp.arange(SSM_G)[:, None]).astype(F32)
    g_a_re, g_a_im, g_b_re, g_b_im, g_ldt = _ssm_post(a_re, a_im, ldt, b_re, b_im, gar, gai, gbr, gbi, sel)

    head_ones = (jnp.arange(KVW)[:, None] // HEAD_DIM == jnp.arange(KVW)[None, :] // HEAD_DIM).astype(F32)
    dos, dlses, g_g_attn = _combine_bwd(dcat, os_, lses, vec('g_attn_out'), head_ones)
    g_g_attn = g_g_attn[:1]
    dep_small = emit_small({
        'g_post_mix': dg_post_mix, 'ssm_a_re': g_a_re, 'ssm_a_im': g_a_im, 'ssm_log_dt': g_ldt[:, 0],
        'ssm_b_re': g_b_re, 'ssm_b_im': g_b_im, 'ssm_c_re': g_c_re, 'ssm_c_im': g_c_im, 'ssm_d': dD, 'b_glu': g_b_glu,
        'g_attn_out': g_g_attn, 'g_ssm_out': g_g_ssm, 'g_pre_mlp': dg_pre_mlp, 'g_post_mlp': dg_post_mlp})
    dqkv = [_attn_bwd(qkvs[g], os_[g], lses[g], dos[g], dlses[g], d) for g, d in enumerate(DILATIONS)]
    dproj = _rope_bwd(dqkv, du, pos, _tied(fr, dep_small))
    dh1 = _mm(dproj, w_in, mode='nt', name='mm_dh1', tk=INW, out_dtype=BF16, deps=dep)
    grad_x, dsc1, dsh1, dg_pre_mix = _prenorm_bwd(dx1, dh1, x, vec('g_pre_mix'), sc1)
    dsc1, dsh1, dg_pre_mix = dsc1[:1], dsh1[:1], dg_pre_mix[:1]
    dmod = jnp.concatenate([dsh1, dsc1, dgt1, dsh2, dsc2, dgt2], axis=0)
    dep = emit_late({'b_ada': dmod, 'g_pre_mix': dg_pre_mix})
    emit('w_in', _mm(h1, dproj, mode='tn', name='mm_dw_in', out_dtype=BF16, tm=512, tn=INW, deps=dep))
    return loss[0, 0], grad_x


def _pack(d, names):
    flat = jnp.concatenate([jnp.pad(d[n].reshape(-1).astype(F32), (0, SEG[n] - SMALL_SIZES[n])) for n in names])
    return flat.reshape(-1, 128)


def _unpack(packed, names, shapes):
    out, off = {}, 0
    for n in names:
        out[n] = packed[off // 128:(off + SEG[n]) // 128].reshape(-1)[:SMALL_SIZES[n]].reshape(shapes[n])
        off += SEG[n]
    return out


def _shard_major(t, name):
    if name in ('w_in', 'w_out', 'w_mlp_in'):
        k, n = t.shape
        return t.reshape(k, N_DEV, n // N_DEV).transpose(1, 0, 2)
    k, n = t.shape
    return t.reshape(N_DEV, k // N_DEV, n)


def _from_shard_major(t, name):
    if name in ('w_in', 'w_out', 'w_mlp_in'):
        _, k, n = t.shape
        return t.transpose(1, 0, 2).reshape(k, N_DEV * n)
    _, k, n = t.shape
    return t.reshape(N_DEV * k, n)


def kernel(x, c, positions, w_ada, b_ada, g_pre_mix, g_post_mix, w_in, ssm_a_re, ssm_a_im, ssm_log_dt, ssm_b_re, ssm_b_im, ssm_c_re, ssm_c_im, ssm_d, w_glu, b_glu, g_attn_out, g_ssm_out, w_out, g_pre_mlp, g_post_mlp, w_mlp_in, w_mlp_out, loss_target, m_w_ada, m_b_ada, m_g_pre_mix, m_g_post_mix, m_w_in, m_ssm_a_re, m_ssm_a_im, m_ssm_log_dt, m_ssm_b_re, m_ssm_b_im, m_ssm_c_re, m_ssm_c_im, m_ssm_d, m_w_glu, m_b_glu, m_g_attn_out, m_g_ssm_out, m_w_out, m_g_pre_mlp, m_g_post_mlp, m_w_mlp_in, m_w_mlp_out, v_w_ada, v_b_ada, v_g_pre_mix, v_g_post_mix, v_w_in, v_ssm_a_re, v_ssm_a_im, v_ssm_log_dt, v_ssm_b_re, v_ssm_b_im, v_ssm_c_re, v_ssm_c_im, v_ssm_d, v_w_glu, v_b_glu, v_g_attn_out, v_g_ssm_out, v_w_out, v_g_pre_mlp, v_g_post_mlp, v_w_mlp_in, v_w_mlp_out):
    loc = dict(locals())
    W = {n: loc[n] for n in WEIGHTS}
    M = {n: loc['m_' + n] for n in WEIGHTS}
    V = {n: loc['v_' + n] for n in WEIGHTS}
    assert x.shape == (1, L, D) and w_in.shape == (1, D, INW // N_DEV), (x.shape, w_in.shape)

    cw = NMOD * D // N_DEV
    c_all, mod8 = _mod_exchange(c.reshape(1, 1, D), w_ada[0], b_ada.reshape(N_DEV, 1, cw))
    mod = mod8.reshape(NMOD, D)

    gather, after = {}, mod8
    for n in BIG:
        gather[n] = _push_start(W[n][0].astype(BF16), False, after, 'gather_start_' + n)
        after = gather[n][4]
    tokens = tuple(gather[n][4] for n in BIG)
    mod = _tied(mod, tokens)
    sp = {n: W[n][0] for n in SMALL}
    prep = _ssm_prep({**sp, 'ssm_a_re': _tied(sp['ssm_a_re'], tokens)})

    state_packs = {}

    def get_w(n, after):
        if n == 'w_mlp_out':
            tok = jnp.minimum(jnp.abs(after[0:8, 0:128].astype(F32)), 0.0)
            for key, src in (('w', W), ('m', M), ('v', V)):
                tied = {p: _tied(src[p], (tok,)) for p in SMALL}
                state_packs[key] = {'early': _pack(tied, SMALL_EARLY), 'late': _pack(tied, SMALL_LATE)}
            after = (after,) + tuple(state_packs[k][part] for k in 'wmv' for part in ('early', 'late'))
        g = _push_wait(gather[n], after, 'gather_wait_' + n)
        return g if n == 'w_mlp_in' else _from_shard_major(g, n)

    scatter = {}

    def emit(n, g):
        src = g if n == 'w_mlp_in' else _shard_major(g, n)
        scatter[n] = _push_start(src, True, src, 'scatter_start_' + n)
        return (scatter[n][4],)

    small_early = []

    def emit_small(d):
        pack = _pack(d, SMALL_EARLY)
        small_early.append(_push_start(pack, False, pack, 'small_start'))
        return (small_early[0][4],)

    out_g, out_d, out_m, out_v = {}, {}, {}, {}
    shapes = {n: W[n].shape[1:] for n in SMALL}

    def put(names, packs):
        for dst, packed in zip((out_g, out_d, out_m, out_v), packs):
            dst.update(_unpack(packed, names, shapes))

    late = []

    def emit_late(d):
        rows_all, *packs = _small_update(_pack(d, SMALL_LATE), *[state_packs[k]['late'] for k in 'wmv'])
        put(SMALL_LATE, packs)
        late.append(rows_all)
        return (rows_all,)

    loss, grad_x = _local_step(x[0], positions.reshape(L, 1), mod, loss_target[0], sp, prep, get_w, emit, emit_small,
                               emit_late)
    loss = lax.psum(loss, ('x', 'y', 'c'))

    me = 4 * lax.axis_index('x') + 2 * lax.axis_index('y') + lax.axis_index('c')
    dmod_all = late[0][:, :NMOD * D // 128].reshape(N_DEV, NMOD * D)
    dmod_cols = _tied(lax.dynamic_slice_in_dim(dmod_all, me * cw, cw, axis=1), (scatter['w_in'][4],))
    out_g['w_ada'], out_d['w_ada'], out_m['w_ada'], out_v['w_ada'] = _ada_update(
        c_all.reshape(N_DEV, D), dmod_cols, w_ada[0], m_w_ada[0], v_w_ada[0])

    parts = _push_wait(small_early[0], out_v['w_ada'], 'small_wait')
    packs = _small_sum_update(parts, *[state_packs[k]['early'] for k in 'wmv'])
    put(SMALL_EARLY, packs)

    after = packs[3]
    for n in ('w_mlp_out', 'w_mlp_in', 'w_out', 'w_glu', 'w_in'):
        parts, own = _push_wait(scatter[n], after, 'scatter_wait_' + n, merge=False)
        out_g[n], out_d[n], out_m[n], out_v[n] = _big_update(parts, own, W[n][0], M[n][0], V[n][0], 'update_' + n)
        after = out_v[n]

    lead = lambda t: t[None]
    return (loss, grad_x[None], *[lead(out_g[n]) for n in WEIGHTS], *[lead(out_d[n]) for n in WEIGHTS],
            *[lead(out_m[n]) for n in WEIGHTS], *[lead(out_v[n]) for n in WEIGHTS])
```

```python
import functools
import math

import jax
import jax.numpy as jnp
from jax import lax
from jax.experimental import pallas as pl
from jax.experimental.pallas import tpu as pltpu

F32 = jnp.float32
BF16 = jnp.bfloat16
HI = lax.Precision.HIGHEST
MESH = pl.DeviceIdType.MESH

N_DEV = 8
L = 4096
D = 2048
HEAD_DIM = 64
N_GROUPS = 3
DILATIONS = (1, 4, 16)
HEADS = 6
QW = N_GROUPS * HEADS * HEAD_DIM
KVW = HEADS * HEAD_DIM
ROT_DIM = 16
ROPE_THETA = 500000.0
BLK = 128
NBLK = L // BLK
SSMW = D - QW
SSM_P = 16
SSM_G = SSMW // SSM_P
SSM_N = 64
SSM_GN = SSM_G * SSM_N
CL_G = 8
N_CL = SSM_G // CL_G
CL_U = CL_G * SSM_P
CL_S = CL_G * SSM_N
INW = QW + 2 * KVW + SSMW
OUTW = KVW + SSMW
DFF = 4 * D
NMOD = 6
EPS = 1e-6
LR, B1, B2, AEPS, WD, STEP = 0.001, 0.9, 0.999, 1e-08, 0.01, 10

T_SCAN = 1024
MB = 2 ** 20

WEIGHTS = ['w_ada', 'b_ada', 'g_pre_mix', 'g_post_mix', 'w_in', 'ssm_a_re', 'ssm_a_im', 'ssm_log_dt',
           'ssm_b_re', 'ssm_b_im', 'ssm_c_re', 'ssm_c_im', 'ssm_d', 'w_glu', 'b_glu', 'g_attn_out',
           'g_ssm_out', 'w_out', 'g_pre_mlp', 'g_post_mlp', 'w_mlp_in', 'w_mlp_out']
BIG = ['w_in', 'w_glu', 'w_out', 'w_mlp_in', 'w_mlp_out']
SMALL = [n for n in WEIGHTS if n not in BIG and n != 'w_ada']
SMALL_SIZES = {'b_ada': NMOD * D, 'g_pre_mix': D, 'g_post_mix': D, 'ssm_a_re': SSM_GN, 'ssm_a_im': SSM_GN,
               'ssm_log_dt': SSM_G, 'ssm_b_re': SSM_GN * SSM_P, 'ssm_b_im': SSM_GN * SSM_P,
               'ssm_c_re': SSM_GN * SSM_P, 'ssm_c_im': SSM_GN * SSM_P, 'ssm_d': SSMW, 'b_glu': SSMW,
               'g_attn_out': KVW, 'g_ssm_out': SSMW, 'g_pre_mlp': D, 'g_post_mlp': D}
SEG = {n: -(-SMALL_SIZES[n] // 1024) * 1024 for n in SMALL}
SMALL_LATE = ['b_ada', 'g_pre_mix']
SMALL_EARLY = [n for n in SMALL if n not in SMALL_LATE]


def _params(sem=None, vmem_mb=None):
    kw = {}
    if sem is not None:
        kw['dimension_semantics'] = sem
    if vmem_mb is not None:
        kw['vmem_limit_bytes'] = vmem_mb * MB
    return pltpu.CompilerParams(**kw)


def _vec(n):
    return pl.BlockSpec((1, n), lambda *_: (0, 0))


def _rms(x):
    return lax.rsqrt(jnp.mean(x * x, axis=-1, keepdims=True) + EPS)


def _rms_bwd(dn, n, r):
    return r * (dn - n * jnp.mean(dn * n, axis=-1, keepdims=True))


def _vec8(n):
    return pl.BlockSpec((8, n), lambda *_: (0, 0))


def _colsum(x):
    return jnp.sum(x.reshape(-1, 8, x.shape[-1]), axis=0)


def _fold8(last, *refs):
    @pl.when(last)
    def _():
        for r in refs:
            r[...] = jnp.broadcast_to(jnp.sum(r[...], axis=0, keepdims=True), r.shape)


def _mm(a, b, *, mode, name, out_dtype=F32, tm=1024, tn=1024, tk=2048, epilogue=None, extra=None,
        b_sharded=False, out_sharded=False, deps=()):
    if mode == 'nn':
        M, K = a.shape
        dims = (((1,), (0,)), ((), ()))
        a_spec = pl.BlockSpec((tm, tk), lambda i, j, k: (i, k))
        if b_sharded:
            _, K2, per = b.shape
            N, q = N_DEV * per, per // tn
            b_spec = pl.BlockSpec((None, tk, tn), lambda i, j, k: (j // q, k, j % q))
        else:
            K2, N = b.shape
            b_spec = pl.BlockSpec((tk, tn), lambda i, j, k: (k, j))
    elif mode == 'nt':
        M, K = a.shape
        dims = (((1,), (1,)), ((), ()))
        a_spec = pl.BlockSpec((tm, tk), lambda i, j, k: (i, k))
        if b_sharded:
            _, N, per = b.shape
            K2 = N_DEV * per
            if tk > per:
                b_spec = pl.BlockSpec((tk // per, tn, per), lambda i, j, k: (k, j, 0))
            else:
                q = per // tk
                b_spec = pl.BlockSpec((None, tn, tk), lambda i, j, k: (k // q, j, k % q))
        else:
            N, K2 = b.shape
            b_spec = pl.BlockSpec((tn, tk), lambda i, j, k: (j, k))
    else:
        (K, M), (K2, N) = a.shape, b.shape
        dims = (((0,), (0,)), ((), ()))
        a_spec = pl.BlockSpec((tk, tm), lambda i, j, k: (k, i))
        b_spec = pl.BlockSpec((tk, tn), lambda i, j, k: (k, j))
    assert K == K2 and M % tm == 0 and N % tn == 0 and K % tk == 0, (name, a.shape, b.shape, tm, tn, tk)
    nk = K // tk
    o_spec = pl.BlockSpec((tm, tn), lambda i, j, k: (i, j))
    o_dims = (M, N)
    if out_sharded:
        qo = N // N_DEV // tn
        o_spec = pl.BlockSpec((None, tm, tn), lambda i, j, k: (j // qo, i, j % qo))
        o_dims = (N_DEV, M, N // N_DEV)
    n_out = 2 if epilogue == 'relu2' else 1
    n_extra = 1 if extra is not None else 0
    n_in = 2 + n_extra + len(deps)

    def body(*refs):
        a_ref, b_ref = refs[0], refs[1]
        x_refs = refs[2:2 + n_extra]
        o_refs = refs[n_in:n_in + n_out]
        acc = refs[-1]
        k = pl.program_id(2)

        if len(b_ref.shape) == 3:
            per_ = b_ref.shape[2]
            prod = sum(lax.dot_general(a_ref[:, s * per_:(s + 1) * per_], b_ref[s], dims, preferred_element_type=F32)
                       for s in range(b_ref.shape[0]))
        else:
            prod = lax.dot_general(a_ref[...], b_ref[...], dims, preferred_element_type=F32)

        def finish(r):
            if epilogue == 'relu2':
                o_refs[0][...] = r.astype(BF16)
                o_refs[1][...] = jnp.square(jnp.maximum(r, 0.0)).astype(BF16)
            elif epilogue == 'drelu2':
                pre = x_refs[0][...].astype(F32)
                o_refs[0][...] = (r * (2.0 * jnp.maximum(pre, 0.0))).astype(out_dtype)
            else:
                o_refs[0][...] = r.astype(out_dtype)

        if nk == 1:
            finish(prod)
        else:
            @pl.when(k == 0)
            def _():
                acc[...] = prod

            @pl.when((k > 0) & (k < nk - 1))
            def _():
                acc[...] += prod

            @pl.when(k == nk - 1)
            def _():
                finish(acc[...] + prod)

    if epilogue == 'relu2':
        out_shape = (jax.ShapeDtypeStruct((M, N), BF16), jax.ShapeDtypeStruct((M, N), BF16))
        out_specs = (o_spec, o_spec)
    else:
        out_shape = jax.ShapeDtypeStruct(o_dims, out_dtype)
        out_specs = o_spec
    args = (a, b) + ((extra,) if extra is not None else ()) + tuple(deps)
    in_specs = ([a_spec, b_spec] + ([o_spec] if extra is not None else [])
                + [pl.BlockSpec(memory_space=pl.ANY)] * len(deps))
    return pl.pallas_call(
        body, out_shape=out_shape, grid=(M // tm, N // tn, nk), in_specs=in_specs, out_specs=out_specs,
        scratch_shapes=[pltpu.VMEM((tm, tn) if nk > 1 else (8, 128), F32)], name=name,
        compiler_params=_params(('parallel', 'parallel', 'arbitrary'), 56))(*args)


TR = 256


def _rowspec(w=D):
    return pl.BlockSpec((TR, w), lambda i: (i, 0))


def _prenorm_fwd(x, g, sc, sh):
    def body(x_ref, g_ref, sc_ref, sh_ref, h_ref):
        xv = x_ref[...]
        n = xv * _rms(xv)
        h_ref[...] = ((n * g_ref[...]) * (1.0 + sc_ref[...]) + sh_ref[...]).astype(BF16)

    return pl.pallas_call(
        body, out_shape=jax.ShapeDtypeStruct((L, D), BF16), grid=(L // TR,),
        in_specs=[_rowspec(), _vec(D), _vec(D), _vec(D)], out_specs=_rowspec(), name='prenorm_fwd',
        compiler_params=_params(('parallel',), 40))(x, g, sc, sh)


def _postmix_fwd(x, mix, gpm, gt1, gpl, sc2, sh2):
    def body(x_ref, mix_ref, gpm_ref, gt1_ref, gpl_ref, sc2_ref, sh2_ref, x1_ref, h2_ref):
        mix_v = mix_ref[...]
        nm = mix_v * _rms(mix_v)
        x1 = x_ref[...] + gt1_ref[...] * (nm * gpm_ref[...])
        x1_ref[...] = x1
        n2 = x1 * _rms(x1)
        h2_ref[...] = ((n2 * gpl_ref[...]) * (1.0 + sc2_ref[...]) + sh2_ref[...]).astype(BF16)

    return pl.pallas_call(
        body, out_shape=(jax.ShapeDtypeStruct((L, D), F32), jax.ShapeDtypeStruct((L, D), BF16)), grid=(L // TR,),
        in_specs=[_rowspec(), _rowspec()] + [_vec(D)] * 5, out_specs=(_rowspec(), _rowspec()), name='postmix_fwd',
        compiler_params=_params(('parallel',), 40))(x, mix, gpm, gt1, gpl, sc2, sh2)


def _final_fwd_bwd(x1, y, tgt, g, gt2):
    def body(x1_ref, y_ref, t_ref, g_ref, gt2_ref, dx2_ref, dy_ref, loss_ref, dgt2_ref, dg_ref):
        @pl.when(pl.program_id(0) == 0)
        def _():
            loss_ref[...] = jnp.zeros_like(loss_ref)
            dgt2_ref[...] = jnp.zeros_like(dgt2_ref)
            dg_ref[...] = jnp.zeros_like(dg_ref)

        yv = y_ref[...]
        r = _rms(yv)
        n = yv * r
        ng = n * g_ref[...]
        x2 = x1_ref[...] + gt2_ref[...] * ng
        e = x2 - t_ref[...]
        loss_ref[...] += 0.5 * jnp.sum(jnp.mean(e * e, axis=-1, keepdims=True), axis=0, keepdims=True)
        dx2 = e * (1.0 / D)
        dx2_ref[...] = dx2
        dgt2_ref[...] += _colsum(dx2 * ng)
        dng = dx2 * gt2_ref[...]
        dg_ref[...] += _colsum(dng * n)
        dy_ref[...] = _rms_bwd(dng * g_ref[...], n, r).astype(BF16)
        _fold8(pl.program_id(0) == L // TR - 1, dgt2_ref, dg_ref)

    return pl.pallas_call(
        body,
        out_shape=(jax.ShapeDtypeStruct((L, D), F32), jax.ShapeDtypeStruct((L, D), BF16),
                   jax.ShapeDtypeStruct((8, 128), F32), jax.ShapeDtypeStruct((8, D), F32),
                   jax.ShapeDtypeStruct((8, D), F32)),
        grid=(L // TR,), in_specs=[_rowspec(), _rowspec(), _rowspec(), _vec(D), _vec(D)],
        out_specs=(_rowspec(), _rowspec(), _vec8(128), _vec8(D), _vec8(D)), name='final_fwd_bwd',
        compiler_params=_params(('arbitrary',), 40))(x1, y, tgt, g, gt2)


def _postmix_bwd(dx2, dh2, x1, mix, gpm, gt1, gpl, sc2):
    def body(dx2_ref, dh2_ref, x1_ref, mix_ref, gpm_ref, gt1_ref, gpl_ref, sc2_ref,
             dx1_ref, dmix_ref, dsc2_ref, dsh2_ref, dgpl_ref, dgt1_ref, dgpm_ref):
        @pl.when(pl.program_id(0) == 0)
        def _():
            for r_ in (dsc2_ref, dsh2_ref, dgpl_ref, dgt1_ref, dgpm_ref):
                r_[...] = jnp.zeros_like(r_)

        x1v = x1_ref[...]
        r2 = _rms(x1v)
        n2 = x1v * r2
        dh2v = dh2_ref[...].astype(F32)
        dsh2_ref[...] += _colsum(dh2v)
        dsc2_ref[...] += _colsum(dh2v * (n2 * gpl_ref[...]))
        t = dh2v * (1.0 + sc2_ref[...])
        dgpl_ref[...] += _colsum(t * n2)
        dx1 = dx2_ref[...] + _rms_bwd(t * gpl_ref[...], n2, r2)
        dx1_ref[...] = dx1
        mix_v = mix_ref[...]
        rm = _rms(mix_v)
        nm = mix_v * rm
        dgt1_ref[...] += _colsum(dx1 * (nm * gpm_ref[...]))
        u = dx1 * gt1_ref[...]
        dgpm_ref[...] += _colsum(u * nm)
        dmix_ref[...] = _rms_bwd(u * gpm_ref[...], nm, rm).astype(BF16)
        _fold8(pl.program_id(0) == L // TR - 1, dsc2_ref, dsh2_ref, dgpl_ref, dgt1_ref, dgpm_ref)

    vs = jax.ShapeDtypeStruct((8, D), F32)
    return pl.pallas_call(
        body, out_shape=(jax.ShapeDtypeStruct((L, D), F32), jax.ShapeDtypeStruct((L, D), BF16), vs, vs, vs, vs, vs),
        grid=(L // TR,), in_specs=[_rowspec()] * 4 + [_vec(D)] * 4,
        out_specs=(_rowspec(), _rowspec()) + (_vec8(D),) * 5, name='postmix_bwd',
        compiler_params=_params(('arbitrary',), 48))(dx2, dh2, x1, mix, gpm, gt1, gpl, sc2)


def _prenorm_bwd(dx1, dh1, x, g, sc1):
    def body(dx1_ref, dh1_ref, x_ref, g_ref, sc1_ref, dx_ref, dsc1_ref, dsh1_ref, dg_ref):
        @pl.when(pl.program_id(0) == 0)
        def _():
            for r_ in (dsc1_ref, dsh1_ref, dg_ref):
                r_[...] = jnp.zeros_like(r_)

        xv = x_ref[...]
        r = _rms(xv)
        n = xv * r
        dh = dh1_ref[...].astype(F32)
        dsh1_ref[...] += _colsum(dh)
        dsc1_ref[...] += _colsum(dh * (n * g_ref[...]))
        t = dh * (1.0 + sc1_ref[...])
        dg_ref[...] += _colsum(t * n)
        dx_ref[...] = dx1_ref[...] + _rms_bwd(t * g_ref[...], n, r)
        _fold8(pl.program_id(0) == L // TR - 1, dsc1_ref, dsh1_ref, dg_ref)

    vs = jax.ShapeDtypeStruct((8, D), F32)
    return pl.pallas_call(
        body, out_shape=(jax.ShapeDtypeStruct((L, D), F32), vs, vs, vs), grid=(L // TR,),
        in_specs=[_rowspec()] * 3 + [_vec(D)] * 2, out_specs=(_rowspec(),) + (_vec8(D),) * 3, name='prenorm_bwd',
        compiler_params=_params(('arbitrary',), 40))(dx1, dh1, x, g, sc1)


ROPE_W = QW + KVW
QKV_W = 3 * KVW
QK_SCALE = 0.125
NB_KV = KVW // 128


def _rope_rotate(xv, pos, fr, sign):
    ang = pos.astype(F32) * fr
    w = lax.broadcasted_iota(jnp.int32, (1, 128), 1) % HEAD_DIM
    cs = jnp.cos(ang)
    sn = jnp.sin(ang) * sign
    s1 = jnp.where(w < ROT_DIM // 2, -sn, 0.0)
    s2 = jnp.where((w >= ROT_DIM // 2) & (w < ROT_DIM), sn, 0.0)
    width = xv.shape[1]
    rep = width // 128
    cs, s1, s2 = jnp.tile(cs, (1, rep)), jnp.tile(s1, (1, rep)), jnp.tile(s2, (1, rep))
    hi = pltpu.roll(xv, width - ROT_DIM // 2, 1)
    lo = pltpu.roll(xv, ROT_DIM // 2, 1)
    return xv * cs + hi * s1 + lo * s2


def _sub_spec(d, rows, width):
    return pl.BlockSpec((d, rows // d, width), lambda i: (0, i, 0))


def _gather_rows(scr, blocks, r, d, rows):
    return jnp.concatenate([scr.at[j][pl.ds(r, rows // d, stride=d), :] for j in blocks], axis=1)


def _scatter_rows(scr, src_ref, d, rows):
    for r in range(d):
        for j in range(NB_KV):
            scr.at[j][pl.ds(r, rows // d, stride=d), :] = src_ref[r, :, j * 128:(j + 1) * 128]


def _token_order(scr):
    return jnp.concatenate([scr[j] for j in range(NB_KV)], axis=1)


def _rope_fwd(proj, pos, fr):
    nb = (ROPE_W + KVW) // 128

    def body(x_ref, pos_ref, fr_ref, o0_ref, o1_ref, o2_ref, scr):
        y = _rope_rotate(x_ref[:, 0:ROPE_W], pos_ref[...], fr_ref[...], 1.0)
        for j in range(ROPE_W // 128):
            scr[j] = y[:, j * 128:(j + 1) * 128] * (QK_SCALE if j < QW // 128 else 1.0)
        for j in range(ROPE_W // 128, nb):
            scr[j] = x_ref[:, j * 128:(j + 1) * 128]
        kv = list(range(QW // 128, nb))
        for g, (d, o_ref) in enumerate(zip(DILATIONS, (o0_ref, o1_ref, o2_ref))):
            blocks = list(range(g * NB_KV, (g + 1) * NB_KV)) + kv
            for r in range(d):
                o_ref[r] = _gather_rows(scr, blocks, r, d, TR).astype(BF16)

    return pl.pallas_call(
        body, out_shape=tuple(jax.ShapeDtypeStruct((d, L // d, QKV_W), BF16) for d in DILATIONS), grid=(L // TR,),
        in_specs=[_rowspec(ROPE_W + KVW), pl.BlockSpec((TR, 1), lambda i: (i, 0)), _vec(128)],
        out_specs=tuple(_sub_spec(d, TR, QKV_W) for d in DILATIONS),
        scratch_shapes=[pltpu.VMEM((nb, TR, 128), F32)], name='rope_fwd',
        compiler_params=_params(('parallel',), 40))(proj, pos, fr)


def _rope_bwd(dqkv, du, pos, fr):
    def body(*refs):
        grads = [refs[3 * g:3 * g + 3] for g in range(N_GROUPS)]
        du_ref, pos_ref, fr_ref, o_ref = refs[9:13]
        scrs = refs[13:]
        dq, dk, dv = [], None, None
        for g, d in enumerate(DILATIONS):
            parts = []
            for t in range(3):
                if d == 1:
                    parts.append(grads[g][t][0])
                else:
                    scr = scrs[3 * (g - 1) + t]
                    _scatter_rows(scr, grads[g][t], d, TR)
                    parts.append(_token_order(scr))
            dq.append(parts[0])
            dk = parts[1] if dk is None else dk + parts[1]
            dv = parts[2] if dv is None else dv + parts[2]
        x = jnp.concatenate(dq + [dk], axis=1)
        o_ref[:, 0:ROPE_W] = _rope_rotate(x, pos_ref[...], fr_ref[...], -1.0).astype(BF16)
        o_ref[:, ROPE_W:ROPE_W + KVW] = dv.astype(BF16)
        o_ref[:, ROPE_W + KVW:INW] = du_ref[...].astype(BF16)

    flat = [a for grp in dqkv for a in grp]
    in_specs = [_sub_spec(d, TR, KVW) for d in DILATIONS for _ in range(3)]
    in_specs += [_rowspec(SSMW), pl.BlockSpec((TR, 1), lambda i: (i, 0)), _vec(128)]
    return pl.pallas_call(
        body, out_shape=jax.ShapeDtypeStruct((L, INW), BF16), grid=(L // TR,), in_specs=in_specs,
        out_specs=_rowspec(INW), scratch_shapes=[pltpu.VMEM((NB_KV, TR, 128), F32)] * 6, name='rope_bwd',
        compiler_params=_params(('parallel',), 48))(*flat, du, pos, fr)


def _attn_mask(nbs, b):
    first = (b & (nbs - 1)) == 0
    qi = lax.broadcasted_iota(jnp.int32, (BLK, 2 * BLK), 0)
    kj = lax.broadcasted_iota(jnp.int32, (BLK, 2 * BLK), 1)
    dist = qi + BLK - kj
    return (dist >= 0) & (dist <= BLK) & ((kj >= BLK) | jnp.logical_not(first))


def _qkv_specs():
    cur = lambda col: pl.BlockSpec((BLK, KVW), lambda b: (b, col))
    prev = lambda col: pl.BlockSpec((BLK, KVW), lambda b: (jnp.maximum(b - 1, 0), col))
    return [cur(0), prev(1), cur(1), prev(2), cur(2)]


_ROWS = pl.BlockSpec((BLK, KVW), lambda b: (b, 0))
NEG = -1e30
NT_DIMS = (((1,), (1,)), ((), ()))
TN_DIMS = (((0,), (0,)), ((), ()))


def _attn_fwd(qkv, d):
    nbs = L // d // BLK

    def body(q_ref, kp_ref, kc_ref, vp_ref, vc_ref, o_ref, lse_ref):
        valid = _attn_mask(nbs, pl.program_id(0))
        heads = [slice(h * HEAD_DIM, (h + 1) * HEAD_DIM) for h in range(HEADS)]
        kcs = [jnp.concatenate([kp_ref[:, hs], kc_ref[:, hs]], axis=0) for hs in heads]
        vcs = [jnp.concatenate([vp_ref[:, hs], vc_ref[:, hs]], axis=0) for hs in heads]
        ss = [lax.dot_general(q_ref[:, hs], kc, NT_DIMS, preferred_element_type=F32) for hs, kc in zip(heads, kcs)]
        ps, ls, lses = [], [], []
        for s in ss:
            s = jnp.where(valid, s, NEG)
            m = jnp.max(s, axis=-1, keepdims=True)
            p = jnp.exp(s - m)
            l = jnp.sum(p, axis=-1, keepdims=True)
            ps.append(p.astype(BF16))
            ls.append(l)
            lses.append(jnp.broadcast_to(m + jnp.log(l), (BLK, HEAD_DIM)))
        outs = [jnp.dot(p, vc, preferred_element_type=F32) / l for p, vc, l in zip(ps, vcs, ls)]
        o_ref[...] = jnp.concatenate(outs, axis=1)
        lse_ref[...] = jnp.concatenate(lses, axis=1)

    sh = jax.ShapeDtypeStruct((L, KVW), F32)
    q2 = qkv.reshape(L, QKV_W)
    o, lse = pl.pallas_call(
        body, out_shape=(sh, sh), grid=(NBLK,), in_specs=_qkv_specs(), out_specs=(_ROWS, _ROWS),
        name='attn_fwd_d%d' % d, compiler_params=_params(('parallel',), 32))(q2, q2, q2, q2, q2)
    return o.reshape(d, L // d, KVW), lse.reshape(d, L // d, KVW)


def _attn_bwd(qkv, o, lse, do, dlse, d):
    nbs = L // d // BLK

    def body(q_ref, kp_ref, kc_ref, vp_ref, vc_ref, o_ref, lse_ref, do_ref, dlse_ref, dq_ref, dk_ref, dv_ref):
        b = pl.program_id(0)

        @pl.when(b == 0)
        def _():
            dk_ref[...] = jnp.zeros_like(dk_ref)
            dv_ref[...] = jnp.zeros_like(dv_ref)

        valid = _attn_mask(nbs, b)
        prev0 = pl.multiple_of(jnp.maximum(b - 1, 0) * BLK, BLK)
        cur0 = pl.multiple_of(b * BLK, BLK)
        heads = [slice(h * HEAD_DIM, (h + 1) * HEAD_DIM) for h in range(HEADS)]
        qs = [q_ref[:, hs] for hs in heads]
        kcs = [jnp.concatenate([kp_ref[:, hs], kc_ref[:, hs]], axis=0) for hs in heads]
        vcs = [jnp.concatenate([vp_ref[:, hs], vc_ref[:, hs]], axis=0) for hs in heads]
        dos = [do_ref[:, hs] for hs in heads]
        do_bs = [t.astype(BF16) for t in dos]
        ss = [lax.dot_general(q, kc, NT_DIMS, preferred_element_type=F32) for q, kc in zip(qs, kcs)]
        dps = [lax.dot_general(do_b, vc, NT_DIMS, preferred_element_type=F32) for do_b, vc in zip(do_bs, vcs)]
        p_bs, ds_bs = [], []
        for h, hs in enumerate(heads):
            s = jnp.where(valid, ss[h], NEG)
            p = jnp.exp(s - lse_ref[:, h * HEAD_DIM:h * HEAD_DIM + 1])
            delta = jnp.sum(dos[h] * o_ref[:, hs], axis=-1, keepdims=True)
            ds = p * (dps[h] - delta + dlse_ref[:, h * HEAD_DIM:h * HEAD_DIM + 1])
            p_bs.append(p.astype(BF16))
            ds_bs.append(ds.astype(BF16))
        dqs = [jnp.dot(ds_b, kc, preferred_element_type=F32) for ds_b, kc in zip(ds_bs, kcs)]
        dks = [lax.dot_general(ds_b, q, TN_DIMS, preferred_element_type=F32) for ds_b, q in zip(ds_bs, qs)]
        dvs = [lax.dot_general(p_b, do_b, TN_DIMS, preferred_element_type=F32) for p_b, do_b in zip(p_bs, do_bs)]
        dq_ref[...] = jnp.concatenate(dqs, axis=1) * QK_SCALE
        dkc, dvc = jnp.concatenate(dks, axis=1), jnp.concatenate(dvs, axis=1)
        dk_ref[pl.ds(prev0, BLK), :] += dkc[:BLK]
        dv_ref[pl.ds(prev0, BLK), :] += dvc[:BLK]
        dk_ref[pl.ds(cur0, BLK), :] += dkc[BLK:]
        dv_ref[pl.ds(cur0, BLK), :] += dvc[BLK:]

    sh = jax.ShapeDtypeStruct((L, KVW), F32)
    whole = pl.BlockSpec((L, KVW), lambda b: (0, 0))
    q2 = qkv.reshape(L, QKV_W)
    flat = lambda t: t.reshape(L, KVW)
    outs = pl.pallas_call(
        body, out_shape=(sh, sh, sh), grid=(NBLK,), in_specs=_qkv_specs() + [_ROWS] * 4,
        out_specs=(_ROWS, whole, whole), name='attn_bwd_d%d' % d,
        compiler_params=_params(('arbitrary',), 48))(q2, q2, q2, q2, q2, flat(o), flat(lse), flat(do), flat(dlse))
    return tuple(t.reshape(d, L // d, KVW) for t in outs)


TC = 512


def _combine_weights(l0, l1, l2):
    m = jnp.maximum(jnp.maximum(l0, l1), l2)
    e0, e1, e2 = jnp.exp(l0 - m), jnp.exp(l1 - m), jnp.exp(l2 - m)
    z = e0 + e1 + e2
    return e0 / z, e1 / z, e2 / z


def _load_groups(refs, scrs):
    out = [refs[0][0]]
    for g in (1, 2):
        _scatter_rows(scrs[g - 1], refs[g], DILATIONS[g], TC)
        out.append(_token_order(scrs[g - 1]))
    return out


def _combine_fwd(os_, lses, g):
    def body(o0, o1, o2, l0, l1, l2, g_ref, att_ref, *scrs):
        ov = _load_groups((o0, o1, o2), scrs[0:2])
        lv = _load_groups((l0, l1, l2), scrs[2:4])
        w0, w1, w2 = _combine_weights(*lv)
        a = w0 * ov[0] + w1 * ov[1] + w2 * ov[2]
        att_ref[...] = ((a * _rms(a)) * g_ref[...]).astype(BF16)

    subs = [_sub_spec(d, TC, KVW) for d in DILATIONS]
    return pl.pallas_call(
        body, out_shape=jax.ShapeDtypeStruct((L, KVW), BF16), grid=(L // TC,), in_specs=subs + subs + [_vec(KVW)],
        out_specs=pl.BlockSpec((TC, KVW), lambda i: (i, 0)),
        scratch_shapes=[pltpu.VMEM((NB_KV, TC, 128), F32)] * 4, name='combine_fwd',
        compiler_params=_params(('parallel',), 40))(*os_, *lses, g)


def _combine_bwd(dcat, os_, lses, g, head_ones):
    def body(datt_ref, o0, o1, o2, l0, l1, l2, g_ref, e_ref, do0, do1, do2, dl0, dl1, dl2, dg_ref, *scrs):
        @pl.when(pl.program_id(0) == 0)
        def _():
            dg_ref[...] = jnp.zeros_like(dg_ref)

        ov = _load_groups((o0, o1, o2), scrs[0:2])
        lv = _load_groups((l0, l1, l2), scrs[2:4])
        ws = _combine_weights(*lv)
        a = ws[0] * ov[0] + ws[1] * ov[1] + ws[2] * ov[2]
        r = _rms(a)
        n = a * r
        dv = datt_ref[...].astype(F32)
        dg_ref[...] += _colsum(dv * n)
        da = _rms_bwd(dv * g_ref[...], n, r)
        e_b = e_ref[...].astype(BF16)

        def head_sum(t):
            hi = t.astype(BF16)
            lo = (t - hi.astype(F32)).astype(BF16)
            return jnp.dot(hi, e_b, preferred_element_type=F32) + jnp.dot(lo, e_b, preferred_element_type=F32)

        dws = [head_sum(da * ov[i]) for i in range(3)]
        dbar = ws[0] * dws[0] + ws[1] * dws[1] + ws[2] * dws[2]
        scr = scrs[4]
        for i, (d, do_ref, dl_ref) in enumerate(zip(DILATIONS, (do0, do1, do2), (dl0, dl1, dl2))):
            for val, out_ref in ((ws[i] * da, do_ref), (ws[i] * (dws[i] - dbar), dl_ref)):
                if d == 1:
                    out_ref[0] = val
                else:
                    for j in range(NB_KV):
                        scr[j] = val[:, j * 128:(j + 1) * 128]
                    for rr in range(d):
                        out_ref[rr] = _gather_rows(scr, range(NB_KV), rr, d, TC)
        _fold8(pl.program_id(0) == L // TC - 1, dg_ref)

    subs = [_sub_spec(d, TC, KVW) for d in DILATIONS]
    shs = tuple(jax.ShapeDtypeStruct((d, L // d, KVW), F32) for d in DILATIONS)
    outs = pl.pallas_call(
        body, out_shape=shs + shs + (jax.ShapeDtypeStruct((8, KVW), F32),), grid=(L // TC,),
        in_specs=[pl.BlockSpec((TC, KVW), lambda i: (i, 0))] + subs + subs + [_vec(KVW),
                                                                              pl.BlockSpec((KVW, KVW), lambda i: (0, 0))],
        out_specs=tuple(subs) + tuple(subs) + (_vec8(KVW),),
        scratch_shapes=[pltpu.VMEM((NB_KV, TC, 128), F32)] * 5, name='combine_bwd',
        compiler_params=_params(('arbitrary',), 48))(dcat, *os_, *lses, g, head_ones)
    return outs[0:3], outs[3:6], outs[6]


def _ssm_disc(ar, ai, ldt):
    dt = jnp.exp(ldt)
    zr, zi = ar * dt, ai * dt
    ez = jnp.exp(zr)
    A_r, A_i = ez * jnp.cos(zi), ez * jnp.sin(zi)
    den = ar * ar + ai * ai
    xr, xi = A_r - 1.0, A_i
    cr = (xr * ar + xi * ai) / den
    ci = (xi * ar - xr * ai) / den
    return dt, zr, zi, A_r, A_i, den, cr, ci


def _ssm_pre(ar, ai, ldt, br, bi):
    def body(ar_ref, ai_ref, ldt_ref, br_ref, bi_ref, bbr_ref, bbi_ref, pwr_ref, pwi_ref):
        _, zr, zi, _, _, _, cr, ci = _ssm_disc(ar_ref[...], ai_ref[...], ldt_ref[...])
        bbr_ref[...] = cr * br_ref[...] - ci * bi_ref[...]
        bbi_ref[...] = cr * bi_ref[...] + ci * br_ref[...]
        k = (lax.broadcasted_iota(jnp.int32, (1, 8), 1) + 1).astype(F32)
        ek = jnp.exp(zr * k)
        pwr_ref[...] = ek * jnp.cos(zi * k)
        pwi_ref[...] = ek * jnp.sin(zi * k)

    s16 = jax.ShapeDtypeStruct((SSM_GN, SSM_P), F32)
    s8 = jax.ShapeDtypeStruct((SSM_GN, 8), F32)
    return pl.pallas_call(body, out_shape=(s16, s16, s8, s8), name='ssm_pre',
                          compiler_params=_params(None, 40))(ar, ai, ldt, br, bi)


def _ssm_post(ar, ai, ldt, br, bi, gar, gai, gbr, gbi, sel):
    def body(ar_ref, ai_ref, ldt_ref, br_ref, bi_ref, gar_ref, gai_ref, gbr_ref, gbi_ref, sel_ref,
             dar_ref, dai_ref, dbr_ref, dbi_ref, dldt_ref):
        a_r, a_i = ar_ref[...], ai_ref[...]
        dt, _, _, A_r, A_i, den, cr, ci = _ssm_disc(a_r, a_i, ldt_ref[...])
        b_r, b_i, g_br, g_bi = br_ref[...], bi_ref[...], gbr_ref[...], gbi_ref[...]
        gcr = jnp.sum(g_br * b_r + g_bi * b_i, axis=-1, keepdims=True)
        gci = jnp.sum(g_bi * b_r - g_br * b_i, axis=-1, keepdims=True)
        dbr_ref[...] = g_br * cr + g_bi * ci
        dbi_ref[...] = g_bi * cr - g_br * ci
        g_ar = gar_ref[...] + (gcr * a_r - gci * a_i) / den
        g_ai = gai_ref[...] + (gcr * a_i + gci * a_r) / den
        qr = (cr * a_r + ci * a_i) / den
        qi = (ci * a_r - cr * a_i) / den
        glr = -(gcr * qr + gci * qi)
        gli = -(gci * qr - gcr * qi)
        gzr = g_ar * A_r + g_ai * A_i
        gzi = g_ai * A_r - g_ar * A_i
        dar_ref[...] = glr + gzr * dt
        dai_ref[...] = gli + gzi * dt
        gdt = (gzr * a_r + gzi * a_i) * dt
        dldt_ref[...] = jnp.dot(sel_ref[...], jnp.broadcast_to(gdt, (SSM_GN, 128)),
                                preferred_element_type=F32, precision=HI)

    s1 = jax.ShapeDtypeStruct((SSM_GN, 1), F32)
    s16 = jax.ShapeDtypeStruct((SSM_GN, SSM_P), F32)
    return pl.pallas_call(body, out_shape=(s1, s1, s16, s16, jax.ShapeDtypeStruct((SSM_G, 128), F32)),
                          name='ssm_post', compiler_params=_params(None, 48))(
                              ar, ai, ldt, br, bi, gar, gai, gbr, gbi, sel)


SCAN_CH = 8


def _scan_fwd_tiles(s_ref, pw, carry):
    pwr, pwi = pw[:, :CL_S], pw[:, CL_S:]
    row = lax.broadcasted_iota(jnp.int32, (8, CL_S), 0)
    steps = [(k, jnp.where(row >= k, pwr[k - 1:k], 0.0), jnp.where(row >= k, pwi[k - 1:k], 0.0)) for k in (1, 2, 4)]
    rows = 8 * SCAN_CH

    def chunk(i, c):
        cr, ci = c
        r0 = pl.multiple_of(i * rows, rows)
        xr = s_ref[pl.ds(r0, rows), 0:CL_S].reshape(SCAN_CH, 8, CL_S)
        xi = s_ref[pl.ds(r0, rows), CL_S:2 * CL_S].reshape(SCAN_CH, 8, CL_S)
        for k, pr, pi in steps:
            sr, si = pltpu.roll(xr, k, 1), pltpu.roll(xi, k, 1)
            xr, xi = xr + pr * sr - pi * si, xi + pr * si + pi * sr
        for j in range(SCAN_CH):
            tr = xr[j] + pwr * cr - pwi * ci
            ti = xi[j] + pwr * ci + pwi * cr
            s_ref[pl.ds(r0 + 8 * j, 8), 0:CL_S] = tr
            s_ref[pl.ds(r0 + 8 * j, 8), CL_S:2 * CL_S] = ti
            cr, ci = tr[7:8], ti[7:8]
        return cr, ci

    return lax.fori_loop(0, T_SCAN // rows, chunk, (carry[:, :CL_S], carry[:, CL_S:]))


def _scan_bwd_tiles(l_ref, pw, carry):
    pwr, pwi = pw[:, :CL_S], pw[:, CL_S:]
    rpr = jnp.concatenate([pwr[7 - r:8 - r] for r in range(8)], axis=0)
    rpi = jnp.concatenate([pwi[7 - r:8 - r] for r in range(8)], axis=0)
    row = lax.broadcasted_iota(jnp.int32, (8, CL_S), 0)
    steps = [(k, jnp.where(row < 8 - k, pwr[k - 1:k], 0.0), jnp.where(row < 8 - k, pwi[k - 1:k], 0.0))
             for k in (1, 2, 4)]
    rows = 8 * SCAN_CH
    nc = T_SCAN // rows

    def chunk(i, c):
        cr, ci = c
        r0 = pl.multiple_of((nc - 1 - i) * rows, rows)
        xr = l_ref[pl.ds(r0, rows), 0:CL_S].reshape(SCAN_CH, 8, CL_S)
        xi = l_ref[pl.ds(r0, rows), CL_S:2 * CL_S].reshape(SCAN_CH, 8, CL_S)
        for k, pr, pi in steps:
            sr, si = pltpu.roll(xr, 8 - k, 1), pltpu.roll(xi, 8 - k, 1)
            xr, xi = xr + pr * sr + pi * si, xi + pr * si - pi * sr
        for j in reversed(range(SCAN_CH)):
            tr = xr[j] + rpr * cr + rpi * ci
            ti = xi[j] + rpr * ci - rpi * cr
            l_ref[pl.ds(r0 + 8 * j, 8), 0:CL_S] = tr
            l_ref[pl.ds(r0 + 8 * j, 8), CL_S:2 * CL_S] = ti
            cr, ci = tr[0:1], ti[0:1]
        return cr, ci

    return lax.fori_loop(0, nc, chunk, (carry[:, :CL_S], carry[:, CL_S:]))


NT_SCAN = L // T_SCAN


def _cl_spec(r, c):
    return pl.BlockSpec((None, r, c), lambda c_, t: (c_, 0, 0))


def _ssm_fwd(u, bm, cm, pw, dvec, u_off=0):
    def body(u_ref, bm_ref, cm_ref, pw_ref, d_ref, y_ref, bnd_ref, s_ref, carry_ref):
        @pl.when(pl.program_id(1) == 0)
        def _():
            carry_ref[...] = jnp.zeros_like(carry_ref)

        bnd_ref[...] = carry_ref[...]
        uv = u_ref[...]
        s_ref[...] = jnp.dot(uv.astype(BF16), bm_ref[...], preferred_element_type=F32)
        cr, ci = _scan_fwd_tiles(s_ref, pw_ref[...], carry_ref[...])
        carry_ref[...] = jnp.concatenate([cr, ci], axis=1)
        y_ref[...] = jnp.dot(s_ref[...].astype(BF16), cm_ref[...], preferred_element_type=F32) + d_ref[...] * uv

    return pl.pallas_call(
        body,
        out_shape=(jax.ShapeDtypeStruct((L, SSMW), F32), jax.ShapeDtypeStruct((N_CL, NT_SCAN, 1, 2 * CL_S), F32),
                   jax.ShapeDtypeStruct((L, N_CL * 2 * CL_S), F32)),
        grid=(N_CL, NT_SCAN),
        in_specs=[pl.BlockSpec((T_SCAN, CL_U), lambda c, t: (t, c + u_off)),
                  _cl_spec(CL_U, 2 * CL_S), _cl_spec(2 * CL_S, CL_U),
                  pl.BlockSpec((None, 8, 2 * CL_S), lambda c, t: (c, 0, 0)),
                  pl.BlockSpec((1, CL_U), lambda c, t: (0, c))],
        out_specs=(pl.BlockSpec((T_SCAN, CL_U), lambda c, t: (t, c)),
                   pl.BlockSpec((None, None, 1, 2 * CL_S), lambda c, t: (c, t, 0, 0)),
                   pl.BlockSpec((T_SCAN, 2 * CL_S), lambda c, t: (t, c))),
        scratch_shapes=[pltpu.VMEM((1, 2 * CL_S), F32)],
        name='ssm_fwd', compiler_params=_params(('arbitrary', 'arbitrary'), 40))(u, bm, cm, pw, dvec)


def _ssm_bwd(u, dy, states, bmt, cmt, pw, dvec, bnd, u_off=0):
    rev = lambda t: NT_SCAN - 1 - t

    def body(u_ref, dy_ref, s_ref, bmt_ref, cmt_ref, pw_ref, d_ref, bnd_ref,
             du_ref, dbm_ref, dcm_ref, da_ref, dd_ref, l_ref, carry_ref):
        @pl.when(pl.program_id(1) == 0)
        def _():
            carry_ref[...] = jnp.zeros_like(carry_ref)
            dbm_ref[...] = jnp.zeros_like(dbm_ref)
            dcm_ref[...] = jnp.zeros_like(dcm_ref)
            da_ref[...] = jnp.zeros_like(da_ref)
            dd_ref[...] = jnp.zeros_like(dd_ref)

        uv, dyv, pw = u_ref[...], dy_ref[...], pw_ref[...]
        dy_b = dyv.astype(BF16)
        entry = bnd_ref[...]
        l_ref[...] = jnp.dot(dy_b, cmt_ref[...], preferred_element_type=F32)
        cr, ci = _scan_bwd_tiles(l_ref, pw, carry_ref[...])
        carry_ref[...] = jnp.concatenate([cr, ci], axis=1)
        sv, lv = s_ref[...], l_ref[...]
        lv_b = lv.astype(BF16)
        du_ref[...] = dyv * d_ref[...] + jnp.dot(lv_b, bmt_ref[...], preferred_element_type=F32)
        dbm_ref[...] += lax.dot_general(uv.astype(BF16), lv_b, TN_DIMS, preferred_element_type=F32)
        dcm_ref[...] += lax.dot_general(sv.astype(BF16), dy_b, TN_DIMS, preferred_element_type=F32)
        dd_ref[...] += _colsum(dyv * uv)
        row = lax.broadcasted_iota(jnp.int32, (T_SCAN, 2 * CL_S), 0)
        sp = jnp.where(row == 0, entry, pltpu.roll(sv, 1, 0))
        spr, spi = sp[:, :CL_S], sp[:, CL_S:]
        lr, li = lv[:, :CL_S], lv[:, CL_S:]
        da_ref[:, 0:CL_S] += _colsum(lr * spr + li * spi)
        da_ref[:, CL_S:2 * CL_S] += _colsum(li * spr - lr * spi)
        _fold8(pl.program_id(1) == NT_SCAN - 1, da_ref, dd_ref)

    return pl.pallas_call(
        body,
        out_shape=(jax.ShapeDtypeStruct((L, SSMW), F32), jax.ShapeDtypeStruct((N_CL, CL_U, 2 * CL_S), F32),
                   jax.ShapeDtypeStruct((N_CL, 2 * CL_S, CL_U), F32), jax.ShapeDtypeStruct((N_CL, 8, 2 * CL_S), F32),
                   jax.ShapeDtypeStruct((8, SSMW), F32)),
        grid=(N_CL, NT_SCAN),
        in_specs=[pl.BlockSpec((T_SCAN, CL_U), lambda c, t: (rev(t), c + u_off)),
                  pl.BlockSpec((T_SCAN, CL_U), lambda c, t: (rev(t), c)),
                  pl.BlockSpec((T_SCAN, 2 * CL_S), lambda c, t: (rev(t), c)),
                  pl.BlockSpec((None, 2 * CL_S, CL_U), lambda c, t: (c, 0, 0)),
                  pl.BlockSpec((None, CL_U, 2 * CL_S), lambda c, t: (c, 0, 0)),
                  pl.BlockSpec((None, 8, 2 * CL_S), lambda c, t: (c, 0, 0)),
                  pl.BlockSpec((1, CL_U), lambda c, t: (0, c)),
                  pl.BlockSpec((None, None, 1, 2 * CL_S), lambda c, t: (c, rev(t), 0, 0))],
        out_specs=(pl.BlockSpec((T_SCAN, CL_U), lambda c, t: (rev(t), c)),
                   pl.BlockSpec((None, CL_U, 2 * CL_S), lambda c, t: (c, 0, 0)),
                   pl.BlockSpec((None, 2 * CL_S, CL_U), lambda c, t: (c, 0, 0)),
                   pl.BlockSpec((None, 8, 2 * CL_S), lambda c, t: (c, 0, 0)),
                   pl.BlockSpec((8, CL_U), lambda c, t: (0, c))),
        scratch_shapes=[pltpu.VMEM((T_SCAN, 2 * CL_S), F32), pltpu.VMEM((1, 2 * CL_S), F32)],
        name='ssm_bwd', compiler_params=_params(('arbitrary', 'arbitrary'), 48))(u, dy, states, bmt, cmt, pw, dvec, bnd)


GELU_C = math.sqrt(2.0 / math.pi)
GELU_K = 0.044715


def _gelu_parts(x):
    t = jnp.tanh(GELU_C * (x + GELU_K * (x * x * x)))
    return x * (0.5 * (1.0 + t)), t


def _glu_fwd(ypre, wglu, bglu, gs):
    def body(y_ref, w_ref, b_ref, g_ref, o_ref):
        yg, _ = _gelu_parts(y_ref[...])
        z = jnp.dot(yg.astype(BF16), w_ref[...], preferred_element_type=F32) + b_ref[...]
        s = yg * jax.nn.sigmoid(z)
        o_ref[...] = ((s * _rms(s)) * g_ref[...]).astype(BF16)

    return pl.pallas_call(
        body, out_shape=jax.ShapeDtypeStruct((L, SSMW), BF16), grid=(L // TR,),
        in_specs=[_rowspec(SSMW), pl.BlockSpec((SSMW, SSMW), lambda i: (0, 0)), _vec(SSMW), _vec(SSMW)],
        out_specs=_rowspec(SSMW), name='glu_fwd', compiler_params=_params(('parallel',), 32))(ypre, wglu, bglu, gs)


def _glu_bwd(ypre, dsn, wglu, bglu, gs):
    def body(y_ref, d_ref, w_ref, b_ref, g_ref, dy_ref, dw_ref, db_ref, dg_ref):
        @pl.when(pl.program_id(0) == 0)
        def _():
            dw_ref[...] = jnp.zeros_like(dw_ref)
            db_ref[...] = jnp.zeros_like(db_ref)
            dg_ref[...] = jnp.zeros_like(dg_ref)

        xv = y_ref[...]
        yg, t = _gelu_parts(xv)
        yg_b = yg.astype(BF16)
        z = jnp.dot(yg_b, w_ref[...], preferred_element_type=F32) + b_ref[...]
        sg = jax.nn.sigmoid(z)
        s = yg * sg
        r = _rms(s)
        n = s * r
        dv = d_ref[:, d_ref.shape[1] - SSMW:].astype(F32)
        dg_ref[...] += _colsum(dv * n)
        ds = _rms_bwd(dv * g_ref[...], n, r)
        dz =(ds * yg) * (sg * (1.0 - sg))
        dz_b = dz.astype(BF16)
        db_ref[...] += _colsum(dz)
        dw_ref[...] += lax.dot_general(yg_b, dz_b, TN_DIMS, preferred_element_type=F32)
        dyg = ds * sg + lax.dot_general(dz_b, w_ref[...], NT_DIMS, preferred_element_type=F32)
        dgelu = 0.5 * (1.0 + t) + (0.5 * xv) * (1.0 - t * t) * (GELU_C * (1.0 + 3.0 * GELU_K * (xv * xv)))
        dy_ref[...] = dyg * dgelu
        _fold8(pl.program_id(0) == L // TR - 1, db_ref, dg_ref)

    vs = jax.ShapeDtypeStruct((8, SSMW), F32)
    return pl.pallas_call(
        body, out_shape=(jax.ShapeDtypeStruct((L, SSMW), F32), jax.ShapeDtypeStruct((SSMW, SSMW), F32), vs, vs),
        grid=(L // TR,),
        in_specs=[_rowspec(SSMW), _rowspec(dsn.shape[1]), pl.BlockSpec((SSMW, SSMW), lambda i: (0, 0)), _vec(SSMW),
                  _vec(SSMW)],
        out_specs=(_rowspec(SSMW), pl.BlockSpec((SSMW, SSMW), lambda i: (0, 0)), _vec8(SSMW), _vec8(SSMW)),
        name='glu_bwd', compiler_params=_params(('arbitrary',), 40))(ypre, dsn, wglu, bglu, gs)


def _me():
    return lax.axis_index('x'), lax.axis_index('y'), lax.axis_index('c')


def _my_index():
    return 4 * lax.axis_index('x') + 2 * lax.axis_index('y') + lax.axis_index('c')


def _peer(k):
    x, y, c = _me()
    px = 1 - x if k & 4 else x
    py = 1 - y if k & 2 else y
    pc = 1 - c if k & 1 else c
    return (px, py, pc), 4 * px + 2 * py + pc


def _mod_exchange(c_row, w_ada, b_ada8, deps=()):
    cw = NMOD * D // N_DEV

    def body(c_ref, w_ref, b_ref, *rest):
        call_ref, mod_ref, part_ref, send_sems, recv_sems = rest[len(deps):]
        x, y, c = _me()
        me = 4 * x + 2 * y + c
        call_ref[me] = c_ref[0]
        sends = []
        for k in range(1, N_DEV):
            peer, _ = _peer(k)
            cp = pltpu.make_async_remote_copy(src_ref=c_ref.at[0], dst_ref=call_ref.at[me], send_sem=send_sems.at[0, k - 1],
                                              recv_sem=recv_sems.at[0, k - 1], device_id=peer, device_id_type=MESH)
            cp.start()
            sends.append(cp)
        for k in range(1, N_DEV):
            peer, pidx = _peer(k)
            pltpu.make_async_remote_copy(src_ref=c_ref.at[0], dst_ref=call_ref.at[pidx], send_sem=send_sems.at[0, k - 1],
                                         recv_sem=recv_sems.at[0, k - 1], device_id=peer, device_id_type=MESH).wait_recv()
        for cp in sends:
            cp.wait_send()
        cv = call_ref[...].reshape(N_DEV, D)
        part = jnp.dot(cv * jax.nn.sigmoid(cv), w_ref[...], preferred_element_type=F32, precision=HI)
        part_ref[...] = part.reshape(N_DEV, 1, cw)
        mod_ref[me] = part_ref[me]
        sends = []
        for k in range(1, N_DEV):
            peer, pidx = _peer(k)
            cp = pltpu.make_async_remote_copy(src_ref=part_ref.at[pidx], dst_ref=mod_ref.at[me], send_sem=send_sems.at[1, k - 1],
                                              recv_sem=recv_sems.at[1, k - 1], device_id=peer, device_id_type=MESH)
            cp.start()
            sends.append(cp)
        for k in range(1, N_DEV):
            peer, pidx = _peer(k)
            pltpu.make_async_remote_copy(src_ref=part_ref.at[pidx], dst_ref=mod_ref.at[pidx], send_sem=send_sems.at[1, k - 1],
                                         recv_sem=recv_sems.at[1, k - 1], device_id=peer, device_id_type=MESH).wait_recv()
        for cp in sends:
            cp.wait_send()
        mod_ref[...] = mod_ref[...] + b_ref[...]

    vm = pl.BlockSpec(memory_space=pltpu.VMEM)
    return pl.pallas_call(
        body, out_shape=(jax.ShapeDtypeStruct((N_DEV, 1, D), F32), jax.ShapeDtypeStruct((N_DEV, 1, cw), F32)),
        in_specs=[vm, vm, vm] + [pl.BlockSpec(memory_space=pl.ANY)] * len(deps), out_specs=(vm, vm),
        scratch_shapes=[pltpu.VMEM((N_DEV, 1, cw), F32), pltpu.SemaphoreType.DMA((2, N_DEV - 1)),
                        pltpu.SemaphoreType.DMA((2, N_DEV - 1))],
        name='mod_exchange', compiler_params=_params(None, 48))(c_row, w_ada, b_ada8, *deps)


HBM_SPEC = pl.BlockSpec(memory_space=pltpu.HBM)
SEM_SPEC = pl.BlockSpec(memory_space=pltpu.SEMAPHORE)
DATAFLOW = pltpu.SideEffectType.DATAFLOW_SIDE_EFFECTING


def _push_start(src, scatter, after, name):
    land = lax.empty(src.shape if scatter else (N_DEV,) + src.shape, src.dtype)

    def body(src_ref, land_ref, after_ref, send_sem, recv_sem, land_thru, token):
        x, y, c = _me()
        me = 4 * x + 2 * y + c
        for k in range(1, N_DEV):
            peer, pidx = _peer(k)
            pltpu.make_async_remote_copy(src_ref=src_ref.at[pidx] if scatter else src_ref, dst_ref=land_ref.at[me],
                                         send_sem=send_sem, recv_sem=recv_sem, device_id=peer,
                                         device_id_type=MESH).start()
        token[...] = jnp.zeros_like(token)

    own = src
    src = pltpu.with_memory_space_constraint(src, pltpu.HBM)
    send_sem, recv_sem, land_thru, token = pl.pallas_call(
        body, name=name,
        out_shape=(pltpu.SemaphoreType.DMA(()), pltpu.SemaphoreType.DMA(()),
                   pltpu.HBM(land.shape, land.dtype), jax.ShapeDtypeStruct((8, 128), F32)),
        in_specs=(HBM_SPEC, HBM_SPEC, pl.BlockSpec(memory_space=pl.ANY)),
        out_specs=(SEM_SPEC, SEM_SPEC, HBM_SPEC, pl.BlockSpec(memory_space=pltpu.VMEM)),
        input_output_aliases={1: 2}, compiler_params=pltpu.CompilerParams(has_side_effects=DATAFLOW),
    )(src, pltpu.with_memory_space_constraint(land, pltpu.HBM), after)
    return send_sem, recv_sem, src, land_thru, token, own


def _push_wait(handle, after, name, merge=True):
    send_sem, recv_sem, src, land_thru, _, own = handle
    after = tuple(after) if isinstance(after, (tuple, list)) else (after,)

    def body(src_ref, land_ref, send_sem, recv_sem, *rest):
        seven = land_ref.at[pl.ds(0, N_DEV - 1)]
        cp = pltpu.make_async_remote_copy(src_ref=seven, dst_ref=seven, send_sem=send_sem, recv_sem=recv_sem,
                                          device_id=_me(), device_id_type=MESH)
        cp.wait_send()
        cp.wait_recv()

    landed = pl.pallas_call(
        body, name=name, out_shape=pltpu.HBM(land_thru.shape, land_thru.dtype),
        in_specs=(HBM_SPEC, HBM_SPEC, SEM_SPEC, SEM_SPEC) + (pl.BlockSpec(memory_space=pl.ANY),) * len(after),
        out_specs=HBM_SPEC, input_output_aliases={1: 0},
        compiler_params=pltpu.CompilerParams(has_side_effects=DATAFLOW),
    )(src, land_thru, send_sem, recv_sem, *after)
    if not merge:
        return landed, own
    return lax.dynamic_update_index_in_dim(landed, own, _my_index(), 0)


def _adam(w, g, m, v):
    m2 = B1 * m + (1.0 - B1) * g
    v2 = B2 * v + (1.0 - B2) * jnp.square(g)
    m_hat = m2 / (1.0 - B1 ** STEP)
    v_hat = v2 / (1.0 - B2 ** STEP)
    delta = -LR * (m_hat / (jnp.sqrt(v_hat) + AEPS) + WD * w)
    return delta, m2, v2


def _small_update(gp, wp, mp, vp):
    def body(g_ref, w_ref, m_ref, v_ref, all_ref, go_ref, d_ref, mo_ref, vo_ref, send_sems, recv_sems):
        x, y, c = _me()
        me = 4 * x + 2 * y + c
        all_ref[me] = g_ref[...]
        sends = []
        for k in range(1, N_DEV):
            peer, _ = _peer(k)
            cp = pltpu.make_async_remote_copy(src_ref=g_ref, dst_ref=all_ref.at[me], send_sem=send_sems.at[k - 1],
                                              recv_sem=recv_sems.at[k - 1], device_id=peer, device_id_type=MESH)
            cp.start()
            sends.append(cp)
        for k in range(1, N_DEV):
            peer, pidx = _peer(k)
            pltpu.make_async_remote_copy(src_ref=g_ref, dst_ref=all_ref.at[pidx], send_sem=send_sems.at[k - 1],
                                         recv_sem=recv_sems.at[k - 1], device_id=peer, device_id_type=MESH).wait_recv()
        for cp in sends:
            cp.wait_send()
        g = all_ref[0]
        for d in range(1, N_DEV):
            g = g + all_ref[d]
        delta, m2, v2 = _adam(w_ref[...], g, m_ref[...], v_ref[...])
        go_ref[...] = g
        d_ref[...] = delta
        mo_ref[...] = m2
        vo_ref[...] = v2

    vm = pl.BlockSpec(memory_space=pltpu.VMEM)
    vs = jax.ShapeDtypeStruct(gp.shape, F32)
    return pl.pallas_call(
        body, out_shape=(jax.ShapeDtypeStruct((N_DEV,) + gp.shape, F32), vs, vs, vs, vs), in_specs=[vm] * 4,
        out_specs=(vm,) * 5,
        scratch_shapes=[pltpu.SemaphoreType.DMA((N_DEV - 1,)), pltpu.SemaphoreType.DMA((N_DEV - 1,))],
        name='small_update', compiler_params=_params(None, 48))(gp, wp, mp, vp)


def _small_sum_update(parts, wp, mp, vp):
    def body(p_ref, w_ref, m_ref, v_ref, go_ref, d_ref, mo_ref, vo_ref):
        g = p_ref[0]
        for d in range(1, N_DEV):
            g = g + p_ref[d]
        delta, m2, v2 = _adam(w_ref[...], g, m_ref[...], v_ref[...])
        go_ref[...] = g
        d_ref[...] = delta
        mo_ref[...] = m2
        vo_ref[...] = v2

    vm = pl.BlockSpec(memory_space=pltpu.VMEM)
    vs = jax.ShapeDtypeStruct(wp.shape, F32)
    return pl.pallas_call(body, out_shape=(vs, vs, vs, vs), in_specs=[vm] * 4, out_specs=(vm,) * 4,
                          name='small_sum_update', compiler_params=_params(None, 48))(parts, wp, mp, vp)


def _big_update(parts, src, w, m, v, name):
    _, R, C = parts.shape
    tr = R if R % 256 else (128 if C >= 2048 else 256)

    def body(me_ref, p_ref, own_ref, w_ref, m_ref, v_ref, g_ref, d_ref, mo_ref, vo_ref):
        me = me_ref[0]
        mine = own_ref[...].astype(F32)
        g = jnp.where(me == 0, mine, p_ref[0].astype(F32))
        for d in range(1, N_DEV):
            g = g + jnp.where(me == d, mine, p_ref[d].astype(F32))
        delta, m2, v2 = _adam(w_ref[...], g, m_ref[...], v_ref[...])
        g_ref[...] = g
        d_ref[...] = delta
        mo_ref[...] = m2
        vo_ref[...] = v2

    blk = pl.BlockSpec((tr, C), lambda i, me: (i, 0))
    sh = jax.ShapeDtypeStruct((R, C), F32)
    grid_spec = pltpu.PrefetchScalarGridSpec(
        num_scalar_prefetch=1, grid=(R // tr,),
        in_specs=[pl.BlockSpec((N_DEV, tr, C), lambda i, me: (0, i, 0)),
                  pl.BlockSpec((None, tr, C), lambda i, me: (me[0], i, 0)), blk, blk, blk],
        out_specs=(blk,) * 4)
    return pl.pallas_call(
        body, out_shape=(sh, sh, sh, sh), grid_spec=grid_spec, name=name,
        compiler_params=_params(('parallel',), 48))(_my_index().reshape(1), parts, src, w, m, v)


def _ada_update(c_all, dmod_cols, w, m, v):
    C = w.shape[1]
    tr = 256

    def body(c_ref, dm_ref, w_ref, m_ref, v_ref, g_ref, d_ref, mo_ref, vo_ref):
        cv = c_ref[...]
        s = cv * jax.nn.sigmoid(cv)
        g = lax.dot_general(s, dm_ref[...], TN_DIMS, preferred_element_type=F32, precision=HI)
        delta, m2, v2 = _adam(w_ref[...], g, m_ref[...], v_ref[...])
        g_ref[...] = g
        d_ref[...] = delta
        mo_ref[...] = m2
        vo_ref[...] = v2

    blk = pl.BlockSpec((tr, C), lambda i: (i, 0))
    sh = jax.ShapeDtypeStruct((D, C), F32)
    return pl.pallas_call(
        body, out_shape=(sh, sh, sh, sh), grid=(D // tr,),
        in_specs=[pl.BlockSpec((N_DEV, tr), lambda i: (0, i)), pl.BlockSpec((N_DEV, C), lambda i: (0, 0)), blk, blk, blk],
        out_specs=(blk,) * 4, name='ada_update', compiler_params=_params(('parallel',), 48))(c_all, dmod_cols, w, m, v)


def _rows_to_cluster_lanes(t):
    k = t.shape[1]
    return t.reshape(N_CL, CL_S, k).transpose(0, 2, 1)


def _blockdiag_in(t):
    t = t.reshape(N_CL, CL_G, SSM_N, SSM_P).transpose(0, 1, 3, 2)
    eye = jnp.eye(CL_G, dtype=t.dtype)
    t = t[:, :, :, None, :] * eye[None, :, None, :, None]
    return t.reshape(N_CL, CL_U, CL_S)


def _blockdiag_extract(t):
    t = t.reshape(N_CL, CL_G, SSM_P, CL_G, SSM_N)
    eye = jnp.eye(CL_G, dtype=t.dtype)
    t = jnp.sum(t * eye[None, :, None, :, None], axis=3)
    return t.transpose(0, 1, 3, 2).reshape(SSM_GN, SSM_P)


def _c_to_rows(t):
    return t.transpose(0, 2, 1).reshape(SSM_GN, SSM_P)


def _rows_to_c(t):
    return t.reshape(SSM_G, SSM_N, SSM_P).transpose(0, 2, 1)


def _ssm_prep(sp):
    rows = lambda n: sp[n].reshape(SSM_GN, 1)
    a_re, a_im = rows('ssm_a_re'), rows('ssm_a_im')
    ldt = jnp.repeat(sp['ssm_log_dt'].reshape(SSM_G, 1), SSM_N, axis=0)
    b_re, b_im = sp['ssm_b_re'].reshape(SSM_GN, SSM_P), sp['ssm_b_im'].reshape(SSM_GN, SSM_P)
    c_re, c_im = _c_to_rows(sp['ssm_c_re'].reshape(SSM_G, SSM_P, SSM_N)), _c_to_rows(sp['ssm_c_im'].reshape(SSM_G, SSM_P, SSM_N))
    bbr, bbi, pwr, pwi = _ssm_pre(a_re, a_im, ldt, b_re, b_im)
    bm = jnp.concatenate([_blockdiag_in(bbr), _blockdiag_in(bbi)], axis=2).astype(BF16)
    cmt = jnp.concatenate([_blockdiag_in(c_re), -_blockdiag_in(c_im)], axis=2).astype(BF16)
    bmt, cm = bm.transpose(0, 2, 1), cmt.transpose(0, 2, 1)
    pw = jnp.concatenate([_rows_to_cluster_lanes(pwr), _rows_to_cluster_lanes(pwi)], axis=2)
    return a_re, a_im, ldt, b_re, b_im, bm, cm, bmt, cmt, pw


def _tied(v, deps):
    for t in deps:
        v = v + t[0, 0]
    return v


def _local_step(x, pos, mod, tgt, sp, prep, get_w, emit, emit_small, emit_late, first_deps=()):
    sh1, sc1, gt1, sh2, sc2, gt2 = (mod[i:i + 1] for i in range(NMOD))
    vec = lambda n: sp[n].reshape(1, -1)
    a_re, a_im, ldt, b_re, b_im, bm_b, cm_b, bmt_b, cmt_b, pw = prep
    dvec = vec('ssm_d')

    h1 = _prenorm_fwd(x, vec('g_pre_mix'), sc1, sh1)
    w_in = get_w('w_in', (h1, bm_b, cm_b, pw, bmt_b, cmt_b))
    proj = _mm(h1, w_in, mode='nn', name='mm_in', tm=512, tn=INW, deps=first_deps)
    fr1 =ROPE_THETA ** (-jnp.arange(0, ROT_DIM, 2, dtype=F32) / ROT_DIM)
    lane = jnp.arange(128) % HEAD_DIM
    fr = jnp.where(lane < ROT_DIM, fr1[lane % (ROT_DIM // 2)], 0.0).reshape(1, 128).astype(F32)
    u_off = (ROPE_W + KVW) // CL_U
    qkvs = _rope_fwd(proj, pos, fr)
    fwd = [_attn_fwd(qkvs[g], d) for g, d in enumerate(DILATIONS)]
    os_, lses = [t[0] for t in fwd], [t[1] for t in fwd]
    att = _combine_fwd(os_, lses, vec('g_attn_out'))

    ypre, bnd, states = _ssm_fwd(proj, bm_b, cm_b, pw, dvec, u_off)
    w_glu = get_w('w_glu', ypre)
    ssm_n = _glu_fwd(ypre, w_glu, vec('b_glu'), vec('g_ssm_out'))

    cat = jnp.concatenate([att, ssm_n], axis=1)
    w_out = get_w('w_out', cat)
    mix = _mm(cat, w_out, mode='nn', name='mm_out', tk=1280)
    x1, h2 = _postmix_fwd(x, mix, vec('g_post_mix'), gt1, vec('g_pre_mlp'), sc2, sh2)
    w_mi = get_w('w_mlp_in', h2)
    a_pre, r_act = _mm(h2, w_mi, mode='nn', name='mm_mlp_in', epilogue='relu2', b_sharded=True)
    w_mo = get_w('w_mlp_out', a_pre)
    y = _mm(r_act, w_mo, mode='nn', name='mm_mlp_out')
    dx2, dy, loss, dgt2, dg_post_mlp = _final_fwd_bwd(x1, y, tgt, vec('g_post_mlp'), gt2)
    dgt2, dg_post_mlp = dgt2[:1], dg_post_mlp[:1]

    da = _mm(dy, w_mo, mode='nt', name='mm_d_act', out_dtype=BF16, epilogue='drelu2', extra=a_pre)
    dep = emit('w_mlp_out', _mm(r_act, dy, mode='tn', name='mm_dw_mlp_out', out_dtype=BF16))
    dh2 = _mm(da, w_mi, mode='nt', name='mm_dh2', tk=2048, out_dtype=BF16, b_sharded=True, deps=dep)
    dep = emit('w_mlp_in', _mm(h2, da, mode='tn', name='mm_dw_mlp_in', out_dtype=BF16, out_sharded=True))
    dx1, dmix, dsc2, dsh2, dg_pre_mlp, dgt1, dg_post_mix = _postmix_bwd(
        dx2, dh2, x1, mix, vec('g_post_mix'), gt1, vec('g_pre_mlp'), sc2)
    dsc2, dsh2, dg_pre_mlp, dgt1, dg_post_mix = (t[:1] for t in (dsc2, dsh2, dg_pre_mlp, dgt1, dg_post_mix))
    dcat = _mm(dmix, w_out, mode='nt', name='mm_dcat', tn=1280, out_dtype=BF16, deps=dep)
    dep = emit('w_out', _mm(cat, dmix, mode='tn', name='mm_dw_out', out_dtype=BF16, tm=640))
    dypre, g_w_glu, g_b_glu, g_g_ssm = _glu_bwd(ypre, dcat, w_glu, _tied(vec('b_glu'), dep), vec('g_ssm_out'))
    g_b_glu, g_g_ssm = g_b_glu[:1], g_g_ssm[:1]
    dep = dep + emit('w_glu', g_w_glu.astype(BF16))
    du, dbm, dcm, dA, dD = _ssm_bwd(proj, dypre, states, bmt_b, cmt_b, pw, dvec, bnd, u_off)
    dD = dD[:1]
    gbr, gbi = _blockdiag_extract(dbm[:, :, :CL_S]), _blockdiag_extract(dbm[:, :, CL_S:])
    dcmt = dcm.transpose(0, 2, 1)
    g_c_re = _rows_to_c(_blockdiag_extract(dcmt[:, :, :CL_S]))
    g_c_im = _rows_to_c(-_blockdiag_extract(dcmt[:, :, CL_S:]))
    gar = dA[:, 0, :CL_S].reshape(SSM_GN, 1)
    gai = dA[:, 0, CL_S:].reshape(SSM_GN, 1)
    sel = (jnp.arange(SSM_GN)[None, :] // SSM_N == jnp.arange(SSM_G)[:, None]).astype(F32)
    g_a_re, g_a_im, g_b_re, g_b_im, g_ldt = _ssm_post(a_re, a_im, ldt, b_re, b_im, gar, gai, gbr, gbi, sel)

    head_ones = (jnp.arange(KVW)[:, None] // HEAD_DIM == jnp.arange(KVW)[None, :] // HEAD_DIM).astype(F32)
    dos, dlses, g_g_attn = _combine_bwd(dcat, os_, lses, vec('g_attn_out'), head_ones)
    g_g_attn = g_g_attn[:1]
    dep_small = emit_small({
        'g_post_mix': dg_post_mix, 'ssm_a_re': g_a_re, 'ssm_a_im': g_a_im, 'ssm_log_dt': g_ldt[:, 0],
        'ssm_b_re': g_b_re, 'ssm_b_im': g_b_im, 'ssm_c_re': g_c_re, 'ssm_c_im': g_c_im, 'ssm_d': dD, 'b_glu': g_b_glu,
        'g_attn_out': g_g_attn, 'g_ssm_out': g_g_ssm, 'g_pre_mlp': dg_pre_mlp, 'g_post_mlp': dg_post_mlp})
    dqkv = [_attn_bwd(qkvs[g], os_[g], lses[g], dos[g], dlses[g], d) for g, d in enumerate(DILATIONS)]
    dproj = _rope_bwd(dqkv, du, pos, _tied(fr, dep_small))
    dh1 = _mm(dproj, w_in, mode='nt', name='mm_dh1', tk=INW, out_dtype=BF16, deps=dep)
    grad_x, dsc1, dsh1, dg_pre_mix = _prenorm_bwd(dx1, dh1, x, vec('g_pre_mix'), sc1)
    dsc1, dsh1, dg_pre_mix = dsc1[:1], dsh1[:1], dg_pre_mix[:1]
    dmod = jnp.concatenate([dsh1, dsc1, dgt1, dsh2, dsc2, dgt2], axis=0)
    dep = emit_late({'b_ada': dmod, 'g_pre_mix': dg_pre_mix})
    emit('w_in', _mm(h1, dproj, mode='tn', name='mm_dw_in', out_dtype=BF16, tm=512, tn=INW, deps=dep))
    return loss[0, 0], grad_x


def _pack(d, names):
    flat = jnp.concatenate([jnp.pad(d[n].reshape(-1).astype(F32), (0, SEG[n] - SMALL_SIZES[n])) for n in names])
    return flat.reshape(-1, 128)


def _unpack(packed, names, shapes):
    out, off = {}, 0
    for n in names:
        out[n] = packed[off // 128:(off + SEG[n]) // 128].reshape(-1)[:SMALL_SIZES[n]].reshape(shapes[n])
        off += SEG[n]
    return out


def _shard_major(t, name):
    if name in ('w_in', 'w_out', 'w_mlp_in'):
        k, n = t.shape
        return t.reshape(k, N_DEV, n // N_DEV).transpose(1, 0, 2)
    k, n = t.shape
    return t.reshape(N_DEV, k // N_DEV, n)


def _from_shard_major(t, name):
    if name in ('w_in', 'w_out', 'w_mlp_in'):
        _, k, n = t.shape
        return t.transpose(1, 0, 2).reshape(k, N_DEV * n)
    _, k, n = t.shape
    return t.reshape(N_DEV * k, n)


def kernel(x, c, positions, w_ada, b_ada, g_pre_mix, g_post_mix, w_in, ssm_a_re, ssm_a_im, ssm_log_dt, ssm_b_re, ssm_b_im, ssm_c_re, ssm_c_im, ssm_d, w_glu, b_glu, g_attn_out, g_ssm_out, w_out, g_pre_mlp, g_post_mlp, w_mlp_in, w_mlp_out, loss_target, m_w_ada, m_b_ada, m_g_pre_mix, m_g_post_mix, m_w_in, m_ssm_a_re, m_ssm_a_im, m_ssm_log_dt, m_ssm_b_re, m_ssm_b_im, m_ssm_c_re, m_ssm_c_im, m_ssm_d, m_w_glu, m_b_glu, m_g_attn_out, m_g_ssm_out, m_w_out, m_g_pre_mlp, m_g_post_mlp, m_w_mlp_in, m_w_mlp_out, v_w_ada, v_b_ada, v_g_pre_mix, v_g_post_mix, v_w_in, v_ssm_a_re, v_ssm_a_im, v_ssm_log_dt, v_ssm_b_re, v_ssm_b_im, v_ssm_c_re, v_ssm_c_im, v_ssm_d, v_w_glu, v_b_glu, v_g_attn_out, v_g_ssm_out, v_w_out, v_g_pre_mlp, v_g_post_mlp, v_w_mlp_in, v_w_mlp_out):
    loc = dict(locals())
    W = {n: loc[n] for n in WEIGHTS}
    M = {n: loc['m_' + n] for n in WEIGHTS}
    V = {n: loc['v_' + n] for n in WEIGHTS}
    assert x.shape == (1, L, D) and w_in.shape == (1, D, INW // N_DEV), (x.shape, w_in.shape)

    cw = NMOD * D // N_DEV
    c_all, mod8 = _mod_exchange(c.reshape(1, 1, D), w_ada[0], b_ada.reshape(N_DEV, 1, cw))
    mod = mod8.reshape(NMOD, D)

    gather, after = {}, mod8
    for n in BIG:
        gather[n] = _push_start(W[n][0].astype(BF16), False, after, 'gather_start_' + n)
        after = gather[n][4]
    tokens = tuple(gather[n][4] for n in BIG)
    mod = _tied(mod, tokens)
    sp = {n: W[n][0] for n in SMALL}
    prep = _ssm_prep({**sp, 'ssm_a_re': _tied(sp['ssm_a_re'], tokens)})

    state_packs = {}

    def get_w(n, after):
        if n == 'w_mlp_out':
            tok = jnp.minimum(jnp.abs(after[0:8, 0:128].astype(F32)), 0.0)
            for key, src in (('w', W), ('m', M), ('v', V)):
                tied = {p: _tied(src[p], (tok,)) for p in SMALL}
                state_packs[key] = {'early': _pack(tied, SMALL_EARLY), 'late': _pack(tied, SMALL_LATE)}
            after = (after,) + tuple(state_packs[k][part] for k in 'wmv' for part in ('early', 'late'))
        g = _push_wait(gather[n], after, 'gather_wait_' + n)
        return g if n == 'w_mlp_in' else _from_shard_major(g, n)

    scatter = {}

    def emit(n, g):
        src = g if n == 'w_mlp_in' else _shard_major(g, n)
        scatter[n] = _push_start(src, True, src, 'scatter_start_' + n)
        return (scatter[n][4],)

    small_early = []

    def emit_small(d):
        pack = _pack(d, SMALL_EARLY)
        small_early.append(_push_start(pack, False, pack, 'small_start'))
        return (small_early[0][4],)

    out_g, out_d, out_m, out_v = {}, {}, {}, {}
    shapes = {n: W[n].shape[1:] for n in SMALL}

    def put(names, packs):
        for dst, packed in zip((out_g, out_d, out_m, out_v), packs):
            dst.update(_unpack(packed, names, shapes))

    late = []

    def emit_late(d):
        rows_all, *packs = _small_update(_pack(d, SMALL_LATE), *[state_packs[k]['late'] for k in 'wmv'])
        put(SMALL_LATE, packs)
        late.append(rows_all)
        return (rows_all,)

    loss, grad_x = _local_step(x[0], positions.reshape(L, 1), mod, loss_target[0], sp, prep, get_w, emit, emit_small,
                               emit_late)
    loss = lax.psum(loss, ('x', 'y', 'c'))

    me = 4 * lax.axis_index('x') + 2 * lax.axis_index('y') + lax.axis_index('c')
    dmod_all = late[0][:, :NMOD * D // 128].reshape(N_DEV, NMOD * D)
    dmod_cols = _tied(lax.dynamic_slice_in_dim(dmod_all, me * cw, cw, axis=1), (scatter['w_in'][4],))
    out_g['w_ada'], out_d['w_ada'], out_m['w_ada'], out_v['w_ada'] = _ada_update(
        c_all.reshape(N_DEV, D), dmod_cols, w_ada[0], m_w_ada[0], v_w_ada[0])

    parts = _push_wait(small_early[0], out_v['w_ada'], 'small_wait')
    packs = _small_sum_update(parts, *[state_packs[k]['early'] for k in 'wmv'])
    put(SMALL_EARLY, packs)

    after = packs[3]
    for n in ('w_mlp_out', 'w_mlp_in', 'w_out', 'w_glu', 'w_in'):
        parts, src = _push_wait(scatter[n], after, 'scatter_wait_' + n, merge=False)
        out_g[n], out_d[n], out_m[n], out_v[n] = _big_update(parts, src, W[n][0], M[n][0], V[n][0], 'update_' + n)
        after = out_v[n]

    lead = lambda t: t[None]
    return (loss, grad_x[None], *[lead(out_g[n]) for n in WEIGHTS], *[lead(out_d[n]) for n in WEIGHTS],
            *[lead(out_m[n]) for n in WEIGHTS], *[lead(out_v[n]) for n in WEIGHTS])
```

```python
import functools
import math

import jax
import jax.numpy as jnp
from jax import lax
from jax.experimental import pallas as pl
from jax.experimental.pallas import tpu as pltpu

F32 = jnp.float32
BF16 = jnp.bfloat16
HI = lax.Precision.HIGHEST
MESH = pl.DeviceIdType.MESH

N_DEV = 8
L = 4096
D = 2048
HEAD_DIM = 64
N_GROUPS = 3
DILATIONS = (1, 4, 16)
HEADS = 6
QW = N_GROUPS * HEADS * HEAD_DIM
KVW = HEADS * HEAD_DIM
ROT_DIM = 16
ROPE_THETA = 500000.0
BLK = 128
NBLK = L // BLK
SSMW = D - QW
SSM_P = 16
SSM_G = SSMW // SSM_P
SSM_N = 64
SSM_GN = SSM_G * SSM_N
CL_G = 8
N_CL = SSM_G // CL_G
CL_U = CL_G * SSM_P
CL_S = CL_G * SSM_N
INW = QW + 2 * KVW + SSMW
OUTW = KVW + SSMW
DFF = 4 * D
NMOD = 6
EPS = 1e-6
LR, B1, B2, AEPS, WD, STEP = 0.001, 0.9, 0.999, 1e-08, 0.01, 10

T_SCAN = 1024
MB = 2 ** 20

WEIGHTS = ['w_ada', 'b_ada', 'g_pre_mix', 'g_post_mix', 'w_in', 'ssm_a_re', 'ssm_a_im', 'ssm_log_dt',
           'ssm_b_re', 'ssm_b_im', 'ssm_c_re', 'ssm_c_im', 'ssm_d', 'w_glu', 'b_glu', 'g_attn_out',
           'g_ssm_out', 'w_out', 'g_pre_mlp', 'g_post_mlp', 'w_mlp_in', 'w_mlp_out']
BIG = ['w_in', 'w_glu', 'w_out', 'w_mlp_in', 'w_mlp_out']
SMALL = [n for n in WEIGHTS if n not in BIG and n != 'w_ada']
SMALL_SIZES = {'b_ada': NMOD * D, 'g_pre_mix': D, 'g_post_mix': D, 'ssm_a_re': SSM_GN, 'ssm_a_im': SSM_GN,
               'ssm_log_dt': SSM_G, 'ssm_b_re': SSM_GN * SSM_P, 'ssm_b_im': SSM_GN * SSM_P,
               'ssm_c_re': SSM_GN * SSM_P, 'ssm_c_im': SSM_GN * SSM_P, 'ssm_d': SSMW, 'b_glu': SSMW,
               'g_attn_out': KVW, 'g_ssm_out': SSMW, 'g_pre_mlp': D, 'g_post_mlp': D}
SEG = {n: -(-SMALL_SIZES[n] // 1024) * 1024 for n in SMALL}
SMALL_LATE = ['b_ada', 'g_pre_mix']
SMALL_EARLY = [n for n in SMALL if n not in SMALL_LATE]


def _params(sem=None, vmem_mb=None):
    kw = {}
    if sem is not None:
        kw['dimension_semantics'] = sem
    if vmem_mb is not None:
        kw['vmem_limit_bytes'] = vmem_mb * MB
    return pltpu.CompilerParams(**kw)


def _vec(n):
    return pl.BlockSpec((1, n), lambda *_: (0, 0))


def _rms(x):
    return lax.rsqrt(jnp.mean(x * x, axis=-1, keepdims=True) + EPS)


def _rms_bwd(dn, n, r):
    return r * (dn - n * jnp.mean(dn * n, axis=-1, keepdims=True))


def _vec8(n):
    return pl.BlockSpec((8, n), lambda *_: (0, 0))


def _colsum(x):
    return jnp.sum(x.reshape(-1, 8, x.shape[-1]), axis=0)


def _fold8(last, *refs):
    @pl.when(last)
    def _():
        for r in refs:
            r[...] = jnp.broadcast_to(jnp.sum(r[...], axis=0, keepdims=True), r.shape)


def _mm(a, b, *, mode, name, out_dtype=F32, tm=1024, tn=1024, tk=2048, epilogue=None, extra=None,
        b_sharded=False, out_sharded=False, deps=()):
    if mode == 'nn':
        M, K = a.shape
        dims = (((1,), (0,)), ((), ()))
        a_spec = pl.BlockSpec((tm, tk), lambda i, j, k: (i, k))
        if b_sharded:
            _, K2, per = b.shape
            N, q = N_DEV * per, per // tn
            b_spec = pl.BlockSpec((None, tk, tn), lambda i, j, k: (j // q, k, j % q))
        else:
            K2, N = b.shape
            b_spec = pl.BlockSpec((tk, tn), lambda i, j, k: (k, j))
    elif mode == 'nt':
        M, K = a.shape
        dims = (((1,), (1,)), ((), ()))
        a_spec = pl.BlockSpec((tm, tk), lambda i, j, k: (i, k))
        if b_sharded:
            _, N, per = b.shape
            K2 = N_DEV * per
            if tk > per:
                b_spec = pl.BlockSpec((tk // per, tn, per), lambda i, j, k: (k, j, 0))
            else:
                q = per // tk
                b_spec = pl.BlockSpec((None, tn, tk), lambda i, j, k: (k // q, j, k % q))
        else:
            N, K2 = b.shape
            b_spec = pl.BlockSpec((tn, tk), lambda i, j, k: (j, k))
    else:
        (K, M), (K2, N) = a.shape, b.shape
        dims = (((0,), (0,)), ((), ()))
        a_spec = pl.BlockSpec((tk, tm), lambda i, j, k: (k, i))
        b_spec = pl.BlockSpec((tk, tn), lambda i, j, k: (k, j))
    assert K == K2 and M % tm == 0 and N % tn == 0 and K % tk == 0, (name, a.shape, b.shape, tm, tn, tk)
    nk = K // tk
    o_spec = pl.BlockSpec((tm, tn), lambda i, j, k: (i, j))
    o_dims = (M, N)
    if out_sharded:
        qo = N // N_DEV // tn
        o_spec = pl.BlockSpec((None, tm, tn), lambda i, j, k: (j // qo, i, j % qo))
        o_dims = (N_DEV, M, N // N_DEV)
    n_out = 2 if epilogue == 'relu2' else 1
    n_extra = 1 if extra is not None else 0
    n_in = 2 + n_extra + len(deps)

    def body(*refs):
        a_ref, b_ref = refs[0], refs[1]
        x_refs = refs[2:2 + n_extra]
        o_refs = refs[n_in:n_in + n_out]
        acc = refs[-1]
        k = pl.program_id(2)

        if len(b_ref.shape) == 3:
            per_ = b_ref.shape[2]
            prod = sum(lax.dot_general(a_ref[:, s * per_:(s + 1) * per_], b_ref[s], dims, preferred_element_type=F32)
                       for s in range(b_ref.shape[0]))
        else:
            prod = lax.dot_general(a_ref[...], b_ref[...], dims, preferred_element_type=F32)

        def finish(r):
            if epilogue == 'relu2':
                o_refs[0][...] = r.astype(BF16)
                o_refs[1][...] = jnp.square(jnp.maximum(r, 0.0)).astype(BF16)
            elif epilogue == 'drelu2':
                pre = x_refs[0][...].astype(F32)
                o_refs[0][...] = (r * (2.0 * jnp.maximum(pre, 0.0))).astype(out_dtype)
            else:
                o_refs[0][...] = r.astype(out_dtype)

        if nk == 1:
            finish(prod)
        else:
            @pl.when(k == 0)
            def _():
                acc[...] = prod

            @pl.when((k > 0) & (k < nk - 1))
            def _():
                acc[...] += prod

            @pl.when(k == nk - 1)
            def _():
                finish(acc[...] + prod)

    if epilogue == 'relu2':
        out_shape = (jax.ShapeDtypeStruct((M, N), BF16), jax.ShapeDtypeStruct((M, N), BF16))
        out_specs = (o_spec, o_spec)
    else:
        out_shape = jax.ShapeDtypeStruct(o_dims, out_dtype)
        out_specs = o_spec
    args = (a, b) + ((extra,) if extra is not None else ()) + tuple(deps)
    in_specs = ([a_spec, b_spec] + ([o_spec] if extra is not None else [])
                + [pl.BlockSpec(memory_space=pl.ANY)] * len(deps))
    return pl.pallas_call(
        body, out_shape=out_shape, grid=(M // tm, N // tn, nk), in_specs=in_specs, out_specs=out_specs,
        scratch_shapes=[pltpu.VMEM((tm, tn) if nk > 1 else (8, 128), F32)], name=name,
        compiler_params=_params(('parallel', 'parallel', 'arbitrary'), 56))(*args)


TR = 256


def _rowspec(w=D):
    return pl.BlockSpec((TR, w), lambda i: (i, 0))


def _prenorm_fwd(x, g, sc, sh):
    def body(x_ref, g_ref, sc_ref, sh_ref, h_ref):
        xv = x_ref[...]
        n = xv * _rms(xv)
        h_ref[...] = ((n * g_ref[...]) * (1.0 + sc_ref[...]) + sh_ref[...]).astype(BF16)

    return pl.pallas_call(
        body, out_shape=jax.ShapeDtypeStruct((L, D), BF16), grid=(L // TR,),
        in_specs=[_rowspec(), _vec(D), _vec(D), _vec(D)], out_specs=_rowspec(), name='prenorm_fwd',
        compiler_params=_params(('parallel',), 40))(x, g, sc, sh)


def _postmix_fwd(x, mix, gpm, gt1, gpl, sc2, sh2):
    def body(x_ref, mix_ref, gpm_ref, gt1_ref, gpl_ref, sc2_ref, sh2_ref, x1_ref, h2_ref):
        mix_v = mix_ref[...]
        nm = mix_v * _rms(mix_v)
        x1 = x_ref[...] + gt1_ref[...] * (nm * gpm_ref[...])
        x1_ref[...] = x1
        n2 = x1 * _rms(x1)
        h2_ref[...] = ((n2 * gpl_ref[...]) * (1.0 + sc2_ref[...]) + sh2_ref[...]).astype(BF16)

    return pl.pallas_call(
        body, out_shape=(jax.ShapeDtypeStruct((L, D), F32), jax.ShapeDtypeStruct((L, D), BF16)), grid=(L // TR,),
        in_specs=[_rowspec(), _rowspec()] + [_vec(D)] * 5, out_specs=(_rowspec(), _rowspec()), name='postmix_fwd',
        compiler_params=_params(('parallel',), 40))(x, mix, gpm, gt1, gpl, sc2, sh2)


def _final_fwd_bwd(x1, y, tgt, g, gt2):
    def body(x1_ref, y_ref, t_ref, g_ref, gt2_ref, dx2_ref, dy_ref, loss_ref, dgt2_ref, dg_ref):
        @pl.when(pl.program_id(0) == 0)
        def _():
            loss_ref[...] = jnp.zeros_like(loss_ref)
            dgt2_ref[...] = jnp.zeros_like(dgt2_ref)
            dg_ref[...] = jnp.zeros_like(dg_ref)

        yv = y_ref[...]
        r = _rms(yv)
        n = yv * r
        ng = n * g_ref[...]
        x2 = x1_ref[...] + gt2_ref[...] * ng
        e = x2 - t_ref[...]
        loss_ref[...] += 0.5 * jnp.sum(jnp.mean(e * e, axis=-1, keepdims=True), axis=0, keepdims=True)
        dx2 = e * (1.0 / D)
        dx2_ref[...] = dx2
        dgt2_ref[...] += _colsum(dx2 * ng)
        dng = dx2 * gt2_ref[...]
        dg_ref[...] += _colsum(dng * n)
        dy_ref[...] = _rms_bwd(dng * g_ref[...], n, r).astype(BF16)
        _fold8(pl.program_id(0) == L // TR - 1, dgt2_ref, dg_ref)

    return pl.pallas_call(
        body,
        out_shape=(jax.ShapeDtypeStruct((L, D), F32), jax.ShapeDtypeStruct((L, D), BF16),
                   jax.ShapeDtypeStruct((8, 128), F32), jax.ShapeDtypeStruct((8, D), F32),
                   jax.ShapeDtypeStruct((8, D), F32)),
        grid=(L // TR,), in_specs=[_rowspec(), _rowspec(), _rowspec(), _vec(D), _vec(D)],
        out_specs=(_rowspec(), _rowspec(), _vec8(128), _vec8(D), _vec8(D)), name='final_fwd_bwd',
        compiler_params=_params(('arbitrary',), 40))(x1, y, tgt, g, gt2)


def _postmix_bwd(dx2, dh2, x1, mix, gpm, gt1, gpl, sc2):
    def body(dx2_ref, dh2_ref, x1_ref, mix_ref, gpm_ref, gt1_ref, gpl_ref, sc2_ref,
             dx1_ref, dmix_ref, dsc2_ref, dsh2_ref, dgpl_ref, dgt1_ref, dgpm_ref):
        @pl.when(pl.program_id(0) == 0)
        def _():
            for r_ in (dsc2_ref, dsh2_ref, dgpl_ref, dgt1_ref, dgpm_ref):
                r_[...] = jnp.zeros_like(r_)

        x1v = x1_ref[...]
        r2 = _rms(x1v)
        n2 = x1v * r2
        dh2v = dh2_ref[...].astype(F32)
        dsh2_ref[...] += _colsum(dh2v)
        dsc2_ref[...] += _colsum(dh2v * (n2 * gpl_ref[...]))
        t = dh2v * (1.0 + sc2_ref[...])
        dgpl_ref[...] += _colsum(t * n2)
        dx1 = dx2_ref[...] + _rms_bwd(t * gpl_ref[...], n2, r2)
        dx1_ref[...] = dx1
        mix_v = mix_ref[...]
        rm = _rms(mix_v)
        nm = mix_v * rm
        dgt1_ref[...] += _colsum(dx1 * (nm * gpm_ref[...]))
        u = dx1 * gt1_ref[...]
        dgpm_ref[...] += _colsum(u * nm)
        dmix_ref[...] = _rms_bwd(u * gpm_ref[...], nm, rm).astype(BF16)
        _fold8(pl.program_id(0) == L // TR - 1, dsc2_ref, dsh2_ref, dgpl_ref, dgt1_ref, dgpm_ref)

    vs = jax.ShapeDtypeStruct((8, D), F32)
    return pl.pallas_call(
        body, out_shape=(jax.ShapeDtypeStruct((L, D), F32), jax.ShapeDtypeStruct((L, D), BF16), vs, vs, vs, vs, vs),
        grid=(L // TR,), in_specs=[_rowspec()] * 4 + [_vec(D)] * 4,
        out_specs=(_rowspec(), _rowspec()) + (_vec8(D),) * 5, name='postmix_bwd',
        compiler_params=_params(('arbitrary',), 48))(dx2, dh2, x1, mix, gpm, gt1, gpl, sc2)


def _prenorm_bwd(dx1, dh1, x, g, sc1):
    def body(dx1_ref, dh1_ref, x_ref, g_ref, sc1_ref, dx_ref, dsc1_ref, dsh1_ref, dg_ref):
        @pl.when(pl.program_id(0) == 0)
        def _():
            for r_ in (dsc1_ref, dsh1_ref, dg_ref):
                r_[...] = jnp.zeros_like(r_)

        xv = x_ref[...]
        r = _rms(xv)
        n = xv * r
        dh = dh1_ref[...].astype(F32)
        dsh1_ref[...] += _colsum(dh)
        dsc1_ref[...] += _colsum(dh * (n * g_ref[...]))
        t = dh * (1.0 + sc1_ref[...])
        dg_ref[...] += _colsum(t * n)
        dx_ref[...] = dx1_ref[...] + _rms_bwd(t * g_ref[...], n, r)
        _fold8(pl.program_id(0) == L // TR - 1, dsc1_ref, dsh1_ref, dg_ref)

    vs = jax.ShapeDtypeStruct((8, D), F32)
    return pl.pallas_call(
        body, out_shape=(jax.ShapeDtypeStruct((L, D), F32), vs, vs, vs), grid=(L // TR,),
        in_specs=[_rowspec()] * 3 + [_vec(D)] * 2, out_specs=(_rowspec(),) + (_vec8(D),) * 3, name='prenorm_bwd',
        compiler_params=_params(('arbitrary',), 40))(dx1, dh1, x, g, sc1)


ROPE_W = QW + KVW
QKV_W = 3 * KVW
QK_SCALE = 0.125
NB_KV = KVW // 128


def _rope_rotate(xv, pos, fr, sign):
    ang = pos.astype(F32) * fr
    w = lax.broadcasted_iota(jnp.int32, (1, 128), 1) % HEAD_DIM
    cs = jnp.cos(ang)
    sn = jnp.sin(ang) * sign
    s1 = jnp.where(w < ROT_DIM // 2, -sn, 0.0)
    s2 = jnp.where((w >= ROT_DIM // 2) & (w < ROT_DIM), sn, 0.0)
    width = xv.shape[1]
    rep = width // 128
    cs, s1, s2 = jnp.tile(cs, (1, rep)), jnp.tile(s1, (1, rep)), jnp.tile(s2, (1, rep))
    hi = pltpu.roll(xv, width - ROT_DIM // 2, 1)
    lo = pltpu.roll(xv, ROT_DIM // 2, 1)
    return xv * cs + hi * s1 + lo * s2


def _sub_spec(d, rows, width):
    return pl.BlockSpec((d, rows // d, width), lambda i: (0, i, 0))


def _gather_rows(scr, blocks, r, d, rows):
    return jnp.concatenate([scr.at[j][pl.ds(r, rows // d, stride=d), :] for j in blocks], axis=1)


def _scatter_rows(scr, src_ref, d, rows):
    for r in range(d):
        for j in range(NB_KV):
            scr.at[j][pl.ds(r, rows // d, stride=d), :] = src_ref[r, :, j * 128:(j + 1) * 128]


def _token_order(scr):
    return jnp.concatenate([scr[j] for j in range(NB_KV)], axis=1)


def _rope_fwd(proj, pos, fr):
    nb = (ROPE_W + KVW) // 128

    def body(x_ref, pos_ref, fr_ref, o0_ref, o1_ref, o2_ref, scr):
        y = _rope_rotate(x_ref[:, 0:ROPE_W], pos_ref[...], fr_ref[...], 1.0)
        for j in range(ROPE_W // 128):
            scr[j] = y[:, j * 128:(j + 1) * 128] * (QK_SCALE if j < QW // 128 else 1.0)
        for j in range(ROPE_W // 128, nb):
            scr[j] = x_ref[:, j * 128:(j + 1) * 128]
        kv = list(range(QW // 128, nb))
        for g, (d, o_ref) in enumerate(zip(DILATIONS, (o0_ref, o1_ref, o2_ref))):
            blocks = list(range(g * NB_KV, (g + 1) * NB_KV)) + kv
            for r in range(d):
                o_ref[r] = _gather_rows(scr, blocks, r, d, TR).astype(BF16)

    return pl.pallas_call(
        body, out_shape=tuple(jax.ShapeDtypeStruct((d, L // d, QKV_W), BF16) for d in DILATIONS), grid=(L // TR,),
        in_specs=[_rowspec(ROPE_W + KVW), pl.BlockSpec((TR, 1), lambda i: (i, 0)), _vec(128)],
        out_specs=tuple(_sub_spec(d, TR, QKV_W) for d in DILATIONS),
        scratch_shapes=[pltpu.VMEM((nb, TR, 128), F32)], name='rope_fwd',
        compiler_params=_params(('parallel',), 40))(proj, pos, fr)


def _rope_bwd(dqkv, du, pos, fr):
    def body(*refs):
        grads = [refs[3 * g:3 * g + 3] for g in range(N_GROUPS)]
        du_ref, pos_ref, fr_ref, o_ref = refs[9:13]
        scrs = refs[13:]
        dq, dk, dv = [], None, None
        for g, d in enumerate(DILATIONS):
            parts = []
            for t in range(3):
                if d == 1:
                    parts.append(grads[g][t][0])
                else:
                    scr = scrs[3 * (g - 1) + t]
                    _scatter_rows(scr, grads[g][t], d, TR)
                    parts.append(_token_order(scr))
            dq.append(parts[0])
            dk = parts[1] if dk is None else dk + parts[1]
            dv = parts[2] if dv is None else dv + parts[2]
        x = jnp.concatenate(dq + [dk], axis=1)
        o_ref[:, 0:ROPE_W] = _rope_rotate(x, pos_ref[...], fr_ref[...], -1.0).astype(BF16)
        o_ref[:, ROPE_W:ROPE_W + KVW] = dv.astype(BF16)
        o_ref[:, ROPE_W + KVW:INW] = du_ref[...].astype(BF16)

    flat = [a for grp in dqkv for a in grp]
    in_specs = [_sub_spec(d, TR, KVW) for d in DILATIONS for _ in range(3)]
    in_specs += [_rowspec(SSMW), pl.BlockSpec((TR, 1), lambda i: (i, 0)), _vec(128)]
    return pl.pallas_call(
        body, out_shape=jax.ShapeDtypeStruct((L, INW), BF16), grid=(L // TR,), in_specs=in_specs,
        out_specs=_rowspec(INW), scratch_shapes=[pltpu.VMEM((NB_KV, TR, 128), F32)] * 6, name='rope_bwd',
        compiler_params=_params(('parallel',), 48))(*flat, du, pos, fr)


def _attn_mask(nbs, b):
    first = (b & (nbs - 1)) == 0
    qi = lax.broadcasted_iota(jnp.int32, (BLK, 2 * BLK), 0)
    kj = lax.broadcasted_iota(jnp.int32, (BLK, 2 * BLK), 1)
    dist = qi + BLK - kj
    return (dist >= 0) & (dist <= BLK) & ((kj >= BLK) | jnp.logical_not(first))


def _qkv_specs():
    cur = lambda col: pl.BlockSpec((BLK, KVW), lambda b: (b, col))
    prev = lambda col: pl.BlockSpec((BLK, KVW), lambda b: (jnp.maximum(b - 1, 0), col))
    return [cur(0), prev(1), cur(1), prev(2), cur(2)]


_ROWS = pl.BlockSpec((BLK, KVW), lambda b: (b, 0))
NEG = -1e30
NT_DIMS = (((1,), (1,)), ((), ()))
TN_DIMS = (((0,), (0,)), ((), ()))


def _attn_fwd(qkv, d):
    nbs = L // d // BLK

    def body(q_ref, kp_ref, kc_ref, vp_ref, vc_ref, o_ref, lse_ref):
        valid = _attn_mask(nbs, pl.program_id(0))
        heads = [slice(h * HEAD_DIM, (h + 1) * HEAD_DIM) for h in range(HEADS)]
        kcs = [jnp.concatenate([kp_ref[:, hs], kc_ref[:, hs]], axis=0) for hs in heads]
        vcs = [jnp.concatenate([vp_ref[:, hs], vc_ref[:, hs]], axis=0) for hs in heads]
        ss = [lax.dot_general(q_ref[:, hs], kc, NT_DIMS, preferred_element_type=F32) for hs, kc in zip(heads, kcs)]
        ps, ls, lses = [], [], []
        for s in ss:
            s = jnp.where(valid, s, NEG)
            m = jnp.max(s, axis=-1, keepdims=True)
            p = jnp.exp(s - m)
            l = jnp.sum(p, axis=-1, keepdims=True)
            ps.append(p.astype(BF16))
            ls.append(l)
            lses.append(jnp.broadcast_to(m + jnp.log(l), (BLK, HEAD_DIM)))
        outs = [jnp.dot(p, vc, preferred_element_type=F32) / l for p, vc, l in zip(ps, vcs, ls)]
        o_ref[...] = jnp.concatenate(outs, axis=1)
        lse_ref[...] = jnp.concatenate(lses, axis=1)

    sh = jax.ShapeDtypeStruct((L, KVW), F32)
    q2 = qkv.reshape(L, QKV_W)
    o, lse = pl.pallas_call(
        body, out_shape=(sh, sh), grid=(NBLK,), in_specs=_qkv_specs(), out_specs=(_ROWS, _ROWS),
        name='attn_fwd_d%d' % d, compiler_params=_params(('parallel',), 32))(q2, q2, q2, q2, q2)
    return o.reshape(d, L // d, KVW), lse.reshape(d, L // d, KVW)


def _attn_bwd(qkv, o, lse, do, dlse, d, rider=None):
    nbs = L // d // BLK
    n_rider = 0 if rider is None else len(rider)

    def body(q_ref, kp_ref, kc_ref, vp_ref, vc_ref, o_ref, lse_ref, do_ref, dlse_ref, *rest):
        dq_ref, dk_ref, dv_ref = rest[n_rider:n_rider + 3]
        if rider is not None:
            _sum_update(rest[0][0], *rest[1:n_rider], *rest[n_rider + 3:])
        b = pl.program_id(0)

        @pl.when(b == 0)
        def _():
            dk_ref[...] = jnp.zeros_like(dk_ref)
            dv_ref[...] = jnp.zeros_like(dv_ref)

        valid = _attn_mask(nbs, b)
        prev0 = pl.multiple_of(jnp.maximum(b - 1, 0) * BLK, BLK)
        cur0 = pl.multiple_of(b * BLK, BLK)
        heads = [slice(h * HEAD_DIM, (h + 1) * HEAD_DIM) for h in range(HEADS)]
        qs = [q_ref[:, hs] for hs in heads]
        kcs = [jnp.concatenate([kp_ref[:, hs], kc_ref[:, hs]], axis=0) for hs in heads]
        vcs = [jnp.concatenate([vp_ref[:, hs], vc_ref[:, hs]], axis=0) for hs in heads]
        dos = [do_ref[:, hs] for hs in heads]
        do_bs = [t.astype(BF16) for t in dos]
        ss = [lax.dot_general(q, kc, NT_DIMS, preferred_element_type=F32) for q, kc in zip(qs, kcs)]
        dps = [lax.dot_general(do_b, vc, NT_DIMS, preferred_element_type=F32) for do_b, vc in zip(do_bs, vcs)]
        p_bs, ds_bs = [], []
        for h, hs in enumerate(heads):
            s = jnp.where(valid, ss[h], NEG)
            p = jnp.exp(s - lse_ref[:, h * HEAD_DIM:h * HEAD_DIM + 1])
            delta = jnp.sum(dos[h] * o_ref[:, hs], axis=-1, keepdims=True)
            ds = p * (dps[h] - delta + dlse_ref[:, h * HEAD_DIM:h * HEAD_DIM + 1])
            p_bs.append(p.astype(BF16))
            ds_bs.append(ds.astype(BF16))
        dqs = [jnp.dot(ds_b, kc, preferred_element_type=F32) for ds_b, kc in zip(ds_bs, kcs)]
        dks = [lax.dot_general(ds_b, q, TN_DIMS, preferred_element_type=F32) for ds_b, q in zip(ds_bs, qs)]
        dvs = [lax.dot_general(p_b, do_b, TN_DIMS, preferred_element_type=F32) for p_b, do_b in zip(p_bs, do_bs)]
        dq_ref[...] = jnp.concatenate(dqs, axis=1) * QK_SCALE
        dkc, dvc = jnp.concatenate(dks, axis=1), jnp.concatenate(dvs, axis=1)
        dk_ref[pl.ds(prev0, BLK), :] += dkc[:BLK]
        dv_ref[pl.ds(prev0, BLK), :] += dvc[:BLK]
        dk_ref[pl.ds(cur0, BLK), :] += dkc[BLK:]
        dv_ref[pl.ds(cur0, BLK), :] += dvc[BLK:]

    sh = jax.ShapeDtypeStruct((L, KVW), F32)
    whole = pl.BlockSpec((L, KVW), lambda b: (0, 0))
    q2 = qkv.reshape(L, QKV_W)
    flat = lambda t: t.reshape(L, KVW)
    in_specs, out_specs, out_shape, extra = _qkv_specs() + [_ROWS] * 4, (_ROWS, whole, whole), (sh, sh, sh), ()
    if rider is not None:
        R, C = rider[3].shape
        rb = pl.BlockSpec((R // NBLK, C), lambda b: (b, 0))
        in_specs += [pl.BlockSpec(memory_space=pltpu.SMEM), pl.BlockSpec((N_DEV, R // NBLK, C), lambda b: (0, b, 0)),
                     rb, rb, rb, rb]
        out_specs += (rb,) * 4
        out_shape += (jax.ShapeDtypeStruct((R, C), F32),) * 4
        extra = tuple(rider)
    outs = pl.pallas_call(
        body, out_shape=out_shape, grid=(NBLK,), in_specs=in_specs, out_specs=out_specs, name='attn_bwd_d%d' % d,
        compiler_params=_params(('arbitrary',), 48))(q2, q2, q2, q2, q2, flat(o), flat(lse), flat(do), flat(dlse),
                                                     *extra)
    return tuple(t.reshape(d, L // d, KVW) for t in outs[:3]) + tuple(outs[3:])


TC = 512


def _combine_weights(l0, l1, l2):
    m = jnp.maximum(jnp.maximum(l0, l1), l2)
    e0, e1, e2 = jnp.exp(l0 - m), jnp.exp(l1 - m), jnp.exp(l2 - m)
    z = e0 + e1 + e2
    return e0 / z, e1 / z, e2 / z


def _load_groups(refs, scrs):
    out = [refs[0][0]]
    for g in (1, 2):
        _scatter_rows(scrs[g - 1], refs[g], DILATIONS[g], TC)
        out.append(_token_order(scrs[g - 1]))
    return out


def _combine_fwd(os_, lses, g):
    def body(o0, o1, o2, l0, l1, l2, g_ref, att_ref, *scrs):
        ov = _load_groups((o0, o1, o2), scrs[0:2])
        lv = _load_groups((l0, l1, l2), scrs[2:4])
        w0, w1, w2 = _combine_weights(*lv)
        a = w0 * ov[0] + w1 * ov[1] + w2 * ov[2]
        att_ref[...] = ((a * _rms(a)) * g_ref[...]).astype(BF16)

    subs = [_sub_spec(d, TC, KVW) for d in DILATIONS]
    return pl.pallas_call(
        body, out_shape=jax.ShapeDtypeStruct((L, KVW), BF16), grid=(L // TC,), in_specs=subs + subs + [_vec(KVW)],
        out_specs=pl.BlockSpec((TC, KVW), lambda i: (i, 0)),
        scratch_shapes=[pltpu.VMEM((NB_KV, TC, 128), F32)] * 4, name='combine_fwd',
        compiler_params=_params(('parallel',), 40))(*os_, *lses, g)


def _combine_bwd(dcat, os_, lses, g, head_ones):
    def body(datt_ref, o0, o1, o2, l0, l1, l2, g_ref, e_ref, do0, do1, do2, dl0, dl1, dl2, dg_ref, *scrs):
        @pl.when(pl.program_id(0) == 0)
        def _():
            dg_ref[...] = jnp.zeros_like(dg_ref)

        ov = _load_groups((o0, o1, o2), scrs[0:2])
        lv = _load_groups((l0, l1, l2), scrs[2:4])
        ws = _combine_weights(*lv)
        a = ws[0] * ov[0] + ws[1] * ov[1] + ws[2] * ov[2]
        r = _rms(a)
        n = a * r
        dv = datt_ref[...].astype(F32)
        dg_ref[...] += _colsum(dv * n)
        da = _rms_bwd(dv * g_ref[...], n, r)
        e_b = e_ref[...].astype(BF16)

        def head_sum(t):
            hi = t.astype(BF16)
            lo = (t - hi.astype(F32)).astype(BF16)
            return jnp.dot(hi, e_b, preferred_element_type=F32) + jnp.dot(lo, e_b, preferred_element_type=F32)

        dws = [head_sum(da * ov[i]) for i in range(3)]
        dbar = ws[0] * dws[0] + ws[1] * dws[1] + ws[2] * dws[2]
        scr = scrs[4]
        for i, (d, do_ref, dl_ref) in enumerate(zip(DILATIONS, (do0, do1, do2), (dl0, dl1, dl2))):
            for val, out_ref in ((ws[i] * da, do_ref), (ws[i] * (dws[i] - dbar), dl_ref)):
                if d == 1:
                    out_ref[0] = val
                else:
                    for j in range(NB_KV):
                        scr[j] = val[:, j * 128:(j + 1) * 128]
                    for rr in range(d):
                        out_ref[rr] = _gather_rows(scr, range(NB_KV), rr, d, TC)
        _fold8(pl.program_id(0) == L // TC - 1, dg_ref)

    subs = [_sub_spec(d, TC, KVW) for d in DILATIONS]
    shs = tuple(jax.ShapeDtypeStruct((d, L // d, KVW), F32) for d in DILATIONS)
    outs = pl.pallas_call(
        body, out_shape=shs + shs + (jax.ShapeDtypeStruct((8, KVW), F32),), grid=(L // TC,),
        in_specs=[pl.BlockSpec((TC, KVW), lambda i: (i, 0))] + subs + subs + [_vec(KVW),
                                                                              pl.BlockSpec((KVW, KVW), lambda i: (0, 0))],
        out_specs=tuple(subs) + tuple(subs) + (_vec8(KVW),),
        scratch_shapes=[pltpu.VMEM((NB_KV, TC, 128), F32)] * 5, name='combine_bwd',
        compiler_params=_params(('arbitrary',), 48))(dcat, *os_, *lses, g, head_ones)
    return outs[0:3], outs[3:6], outs[6]


def _ssm_disc(ar, ai, ldt):
    dt = jnp.exp(ldt)
    zr, zi = ar * dt, ai * dt
    ez = jnp.exp(zr)
    A_r, A_i = ez * jnp.cos(zi), ez * jnp.sin(zi)
    den = ar * ar + ai * ai
    xr, xi = A_r - 1.0, A_i
    cr = (xr * ar + xi * ai) / den
    ci = (xi * ar - xr * ai) / den
    return dt, zr, zi, A_r, A_i, den, cr, ci


def _ssm_pre(ar, ai, ldt, br, bi):
    def body(ar_ref, ai_ref, ldt_ref, br_ref, bi_ref, bbr_ref, bbi_ref, pwr_ref, pwi_ref):
        _, zr, zi, _, _, _, cr, ci = _ssm_disc(ar_ref[...], ai_ref[...], ldt_ref[...])
        bbr_ref[...] = cr * br_ref[...] - ci * bi_ref[...]
        bbi_ref[...] = cr * bi_ref[...] + ci * br_ref[...]
        k = (lax.broadcasted_iota(jnp.int32, (1, 8), 1) + 1).astype(F32)
        ek = jnp.exp(zr * k)
        pwr_ref[...] = ek * jnp.cos(zi * k)
        pwi_ref[...] = ek * jnp.sin(zi * k)

    s16 = jax.ShapeDtypeStruct((SSM_GN, SSM_P), F32)
    s8 = jax.ShapeDtypeStruct((SSM_GN, 8), F32)
    return pl.pallas_call(body, out_shape=(s16, s16, s8, s8), name='ssm_pre',
                          compiler_params=_params(None, 40))(ar, ai, ldt, br, bi)


def _ssm_post(ar, ai, ldt, br, bi, gar, gai, gbr, gbi, sel):
    def body(ar_ref, ai_ref, ldt_ref, br_ref, bi_ref, gar_ref, gai_ref, gbr_ref, gbi_ref, sel_ref,
             dar_ref, dai_ref, dbr_ref, dbi_ref, dldt_ref):
        a_r, a_i = ar_ref[...], ai_ref[...]
        dt, _, _, A_r, A_i, den, cr, ci = _ssm_disc(a_r, a_i, ldt_ref[...])
        b_r, b_i, g_br, g_bi = br_ref[...], bi_ref[...], gbr_ref[...], gbi_ref[...]
        gcr = jnp.sum(g_br * b_r + g_bi * b_i, axis=-1, keepdims=True)
        gci = jnp.sum(g_bi * b_r - g_br * b_i, axis=-1, keepdims=True)
        dbr_ref[...] = g_br * cr + g_bi * ci
        dbi_ref[...] = g_bi * cr - g_br * ci
        g_ar = gar_ref[...] + (gcr * a_r - gci * a_i) / den
        g_ai = gai_ref[...] + (gcr * a_i + gci * a_r) / den
        qr = (cr * a_r + ci * a_i) / den
        qi = (ci * a_r - cr * a_i) / den
        glr = -(gcr * qr + gci * qi)
        gli = -(gci * qr - gcr * qi)
        gzr = g_ar * A_r + g_ai * A_i
        gzi = g_ai * A_r - g_ar * A_i
        dar_ref[...] = glr + gzr * dt
        dai_ref[...] = gli + gzi * dt
        gdt = (gzr * a_r + gzi * a_i) * dt
        dldt_ref[...] = jnp.dot(sel_ref[...], jnp.broadcast_to(gdt, (SSM_GN, 128)),
                                preferred_element_type=F32, precision=HI)

    s1 = jax.ShapeDtypeStruct((SSM_GN, 1), F32)
    s16 = jax.ShapeDtypeStruct((SSM_GN, SSM_P), F32)
    return pl.pallas_call(body, out_shape=(s1, s1, s16, s16, jax.ShapeDtypeStruct((SSM_G, 128), F32)),
                          name='ssm_post', compiler_params=_params(None, 48))(
                              ar, ai, ldt, br, bi, gar, gai, gbr, gbi, sel)


SCAN_CH = 8


def _scan_fwd_tiles(s_ref, pw, carry):
    pwr, pwi = pw[:, :CL_S], pw[:, CL_S:]
    row = lax.broadcasted_iota(jnp.int32, (8, CL_S), 0)
    steps = [(k, jnp.where(row >= k, pwr[k - 1:k], 0.0), jnp.where(row >= k, pwi[k - 1:k], 0.0)) for k in (1, 2, 4)]
    rows = 8 * SCAN_CH

    def chunk(i, c):
        cr, ci = c
        r0 = pl.multiple_of(i * rows, rows)
        xr = s_ref[pl.ds(r0, rows), 0:CL_S].reshape(SCAN_CH, 8, CL_S)
        xi = s_ref[pl.ds(r0, rows), CL_S:2 * CL_S].reshape(SCAN_CH, 8, CL_S)
        for k, pr, pi in steps:
            sr, si = pltpu.roll(xr, k, 1), pltpu.roll(xi, k, 1)
            xr, xi = xr + pr * sr - pi * si, xi + pr * si + pi * sr
        for j in range(SCAN_CH):
            tr = xr[j] + pwr * cr - pwi * ci
            ti = xi[j] + pwr * ci + pwi * cr
            s_ref[pl.ds(r0 + 8 * j, 8), 0:CL_S] = tr
            s_ref[pl.ds(r0 + 8 * j, 8), CL_S:2 * CL_S] = ti
            cr, ci = tr[7:8], ti[7:8]
        return cr, ci

    return lax.fori_loop(0, T_SCAN // rows, chunk, (carry[:, :CL_S], carry[:, CL_S:]))


def _scan_bwd_tiles(l_ref, pw, carry):
    pwr, pwi = pw[:, :CL_S], pw[:, CL_S:]
    rpr = jnp.concatenate([pwr[7 - r:8 - r] for r in range(8)], axis=0)
    rpi = jnp.concatenate([pwi[7 - r:8 - r] for r in range(8)], axis=0)
    row = lax.broadcasted_iota(jnp.int32, (8, CL_S), 0)
    steps = [(k, jnp.where(row < 8 - k, pwr[k - 1:k], 0.0), jnp.where(row < 8 - k, pwi[k - 1:k], 0.0))
             for k in (1, 2, 4)]
    rows = 8 * SCAN_CH
    nc = T_SCAN // rows

    def chunk(i, c):
        cr, ci = c
        r0 = pl.multiple_of((nc - 1 - i) * rows, rows)
        xr = l_ref[pl.ds(r0, rows), 0:CL_S].reshape(SCAN_CH, 8, CL_S)
        xi = l_ref[pl.ds(r0, rows), CL_S:2 * CL_S].reshape(SCAN_CH, 8, CL_S)
        for k, pr, pi in steps:
            sr, si = pltpu.roll(xr, 8 - k, 1), pltpu.roll(xi, 8 - k, 1)
            xr, xi = xr + pr * sr + pi * si, xi + pr * si - pi * sr
        for j in reversed(range(SCAN_CH)):
            tr = xr[j] + rpr * cr + rpi * ci
            ti = xi[j] + rpr * ci - rpi * cr
            l_ref[pl.ds(r0 + 8 * j, 8), 0:CL_S] = tr
            l_ref[pl.ds(r0 + 8 * j, 8), CL_S:2 * CL_S] = ti
            cr, ci = tr[0:1], ti[0:1]
        return cr, ci

    return lax.fori_loop(0, nc, chunk, (carry[:, :CL_S], carry[:, CL_S:]))


NT_SCAN = L // T_SCAN


def _cl_spec(r, c):
    return pl.BlockSpec((None, r, c), lambda c_, t: (c_, 0, 0))


def _ssm_fwd(u, bm, cm, pw, dvec, u_off=0):
    def body(u_ref, bm_ref, cm_ref, pw_ref, d_ref, y_ref, bnd_ref, s_ref, carry_ref):
        @pl.when(pl.program_id(1) == 0)
        def _():
            carry_ref[...] = jnp.zeros_like(carry_ref)

        bnd_ref[...] = carry_ref[...]
        uv = u_ref[...]
        s_ref[...] = jnp.dot(uv.astype(BF16), bm_ref[...], preferred_element_type=F32)
        cr, ci = _scan_fwd_tiles(s_ref, pw_ref[...], carry_ref[...])
        carry_ref[...] = jnp.concatenate([cr, ci], axis=1)
        y_ref[...] = jnp.dot(s_ref[...].astype(BF16), cm_ref[...], preferred_element_type=F32) + d_ref[...] * uv

    return pl.pallas_call(
        body,
        out_shape=(jax.ShapeDtypeStruct((L, SSMW), F32), jax.ShapeDtypeStruct((N_CL, NT_SCAN, 1, 2 * CL_S), F32),
                   jax.ShapeDtypeStruct((L, N_CL * 2 * CL_S), F32)),
        grid=(N_CL, NT_SCAN),
        in_specs=[pl.BlockSpec((T_SCAN, CL_U), lambda c, t: (t, c + u_off)),
                  _cl_spec(CL_U, 2 * CL_S), _cl_spec(2 * CL_S, CL_U),
                  pl.BlockSpec((None, 8, 2 * CL_S), lambda c, t: (c, 0, 0)),
                  pl.BlockSpec((1, CL_U), lambda c, t: (0, c))],
        out_specs=(pl.BlockSpec((T_SCAN, CL_U), lambda c, t: (t, c)),
                   pl.BlockSpec((None, None, 1, 2 * CL_S), lambda c, t: (c, t, 0, 0)),
                   pl.BlockSpec((T_SCAN, 2 * CL_S), lambda c, t: (t, c))),
        scratch_shapes=[pltpu.VMEM((1, 2 * CL_S), F32)],
        name='ssm_fwd', compiler_params=_params(('arbitrary', 'arbitrary'), 40))(u, bm, cm, pw, dvec)


def _ssm_bwd(u, dy, states, bmt, cmt, pw, dvec, bnd, u_off=0):
    rev = lambda t: NT_SCAN - 1 - t

    def body(u_ref, dy_ref, s_ref, bmt_ref, cmt_ref, pw_ref, d_ref, bnd_ref,
             du_ref, dbm_ref, dcm_ref, da_ref, dd_ref, l_ref, carry_ref):
        @pl.when(pl.program_id(1) == 0)
        def _():
            carry_ref[...] = jnp.zeros_like(carry_ref)
            dbm_ref[...] = jnp.zeros_like(dbm_ref)
            dcm_ref[...] = jnp.zeros_like(dcm_ref)
            da_ref[...] = jnp.zeros_like(da_ref)
            dd_ref[...] = jnp.zeros_like(dd_ref)

        uv, dyv, pw = u_ref[...], dy_ref[...], pw_ref[...]
        dy_b = dyv.astype(BF16)
        entry = bnd_ref[...]
        l_ref[...] = jnp.dot(dy_b, cmt_ref[...], preferred_element_type=F32)
        cr, ci = _scan_bwd_tiles(l_ref, pw, carry_ref[...])
        carry_ref[...] = jnp.concatenate([cr, ci], axis=1)
        sv, lv = s_ref[...], l_ref[...]
        lv_b = lv.astype(BF16)
        du_ref[...] = dyv * d_ref[...] + jnp.dot(lv_b, bmt_ref[...], preferred_element_type=F32)
        dbm_ref[...] += lax.dot_general(uv.astype(BF16), lv_b, TN_DIMS, preferred_element_type=F32)
        dcm_ref[...] += lax.dot_general(sv.astype(BF16), dy_b, TN_DIMS, preferred_element_type=F32)
        dd_ref[...] += _colsum(dyv * uv)
        row = lax.broadcasted_iota(jnp.int32, (T_SCAN, 2 * CL_S), 0)
        sp = jnp.where(row == 0, entry, pltpu.roll(sv, 1, 0))
        spr, spi = sp[:, :CL_S], sp[:, CL_S:]
        lr, li = lv[:, :CL_S], lv[:, CL_S:]
        da_ref[:, 0:CL_S] += _colsum(lr * spr + li * spi)
        da_ref[:, CL_S:2 * CL_S] += _colsum(li * spr - lr * spi)
        _fold8(pl.program_id(1) == NT_SCAN - 1, da_ref, dd_ref)

    return pl.pallas_call(
        body,
        out_shape=(jax.ShapeDtypeStruct((L, SSMW), F32), jax.ShapeDtypeStruct((N_CL, CL_U, 2 * CL_S), F32),
                   jax.ShapeDtypeStruct((N_CL, 2 * CL_S, CL_U), F32), jax.ShapeDtypeStruct((N_CL, 8, 2 * CL_S), F32),
                   jax.ShapeDtypeStruct((8, SSMW), F32)),
        grid=(N_CL, NT_SCAN),
        in_specs=[pl.BlockSpec((T_SCAN, CL_U), lambda c, t: (rev(t), c + u_off)),
                  pl.BlockSpec((T_SCAN, CL_U), lambda c, t: (rev(t), c)),
                  pl.BlockSpec((T_SCAN, 2 * CL_S), lambda c, t: (rev(t), c)),
                  pl.BlockSpec((None, 2 * CL_S, CL_U), lambda c, t: (c, 0, 0)),
                  pl.BlockSpec((None, CL_U, 2 * CL_S), lambda c, t: (c, 0, 0)),
                  pl.BlockSpec((None, 8, 2 * CL_S), lambda c, t: (c, 0, 0)),
                  pl.BlockSpec((1, CL_U), lambda c, t: (0, c)),
                  pl.BlockSpec((None, None, 1, 2 * CL_S), lambda c, t: (c, rev(t), 0, 0))],
        out_specs=(pl.BlockSpec((T_SCAN, CL_U), lambda c, t: (rev(t), c)),
                   pl.BlockSpec((None, CL_U, 2 * CL_S), lambda c, t: (c, 0, 0)),
                   pl.BlockSpec((None, 2 * CL_S, CL_U), lambda c, t: (c, 0, 0)),
                   pl.BlockSpec((None, 8, 2 * CL_S), lambda c, t: (c, 0, 0)),
                   pl.BlockSpec((8, CL_U), lambda c, t: (0, c))),
        scratch_shapes=[pltpu.VMEM((T_SCAN, 2 * CL_S), F32), pltpu.VMEM((1, 2 * CL_S), F32)],
        name='ssm_bwd', compiler_params=_params(('arbitrary', 'arbitrary'), 48))(u, dy, states, bmt, cmt, pw, dvec, bnd)


GELU_C = math.sqrt(2.0 / math.pi)
GELU_K = 0.044715


def _gelu_parts(x):
    t = jnp.tanh(GELU_C * (x + GELU_K * (x * x * x)))
    return x * (0.5 * (1.0 + t)), t


def _glu_fwd(ypre, wglu, bglu, gs):
    def body(y_ref, w_ref, b_ref, g_ref, o_ref):
        yg, _ = _gelu_parts(y_ref[...])
        z = jnp.dot(yg.astype(BF16), w_ref[...], preferred_element_type=F32) + b_ref[...]
        s = yg * jax.nn.sigmoid(z)
        o_ref[...] = ((s * _rms(s)) * g_ref[...]).astype(BF16)

    return pl.pallas_call(
        body, out_shape=jax.ShapeDtypeStruct((L, SSMW), BF16), grid=(L // TR,),
        in_specs=[_rowspec(SSMW), pl.BlockSpec((SSMW, SSMW), lambda i: (0, 0)), _vec(SSMW), _vec(SSMW)],
        out_specs=_rowspec(SSMW), name='glu_fwd', compiler_params=_params(('parallel',), 32))(ypre, wglu, bglu, gs)


def _glu_bwd(ypre, dsn, wglu, bglu, gs):
    def body(y_ref, d_ref, w_ref, b_ref, g_ref, dy_ref, dw_ref, db_ref, dg_ref):
        @pl.when(pl.program_id(0) == 0)
        def _():
            dw_ref[...] = jnp.zeros_like(dw_ref)
            db_ref[...] = jnp.zeros_like(db_ref)
            dg_ref[...] = jnp.zeros_like(dg_ref)

        xv = y_ref[...]
        yg, t = _gelu_parts(xv)
        yg_b = yg.astype(BF16)
        z = jnp.dot(yg_b, w_ref[...], preferred_element_type=F32) + b_ref[...]
        sg = jax.nn.sigmoid(z)
        s = yg * sg
        r = _rms(s)
        n = s * r
        dv = d_ref[:, d_ref.shape[1] - SSMW:].astype(F32)
        dg_ref[...] += _colsum(dv * n)
        ds = _rms_bwd(dv * g_ref[...], n, r)
        dz =(ds * yg) * (sg * (1.0 - sg))
        dz_b = dz.astype(BF16)
        db_ref[...] += _colsum(dz)
        dw_ref[...] += lax.dot_general(yg_b, dz_b, TN_DIMS, preferred_element_type=F32)
        dyg = ds * sg + lax.dot_general(dz_b, w_ref[...], NT_DIMS, preferred_element_type=F32)
        dgelu = 0.5 * (1.0 + t) + (0.5 * xv) * (1.0 - t * t) * (GELU_C * (1.0 + 3.0 * GELU_K * (xv * xv)))
        dy_ref[...] = dyg * dgelu
        _fold8(pl.program_id(0) == L // TR - 1, db_ref, dg_ref)

    vs = jax.ShapeDtypeStruct((8, SSMW), F32)
    return pl.pallas_call(
        body, out_shape=(jax.ShapeDtypeStruct((L, SSMW), F32), jax.ShapeDtypeStruct((SSMW, SSMW), F32), vs, vs),
        grid=(L // TR,),
        in_specs=[_rowspec(SSMW), _rowspec(dsn.shape[1]), pl.BlockSpec((SSMW, SSMW), lambda i: (0, 0)), _vec(SSMW),
                  _vec(SSMW)],
        out_specs=(_rowspec(SSMW), pl.BlockSpec((SSMW, SSMW), lambda i: (0, 0)), _vec8(SSMW), _vec8(SSMW)),
        name='glu_bwd', compiler_params=_params(('arbitrary',), 40))(ypre, dsn, wglu, bglu, gs)


def _me():
    return lax.axis_index('x'), lax.axis_index('y'), lax.axis_index('c')


def _my_index():
    return 4 * lax.axis_index('x') + 2 * lax.axis_index('y') + lax.axis_index('c')


def _peer(k):
    x, y, c = _me()
    px = 1 - x if k & 4 else x
    py = 1 - y if k & 2 else y
    pc = 1 - c if k & 1 else c
    return (px, py, pc), 4 * px + 2 * py + pc


def _mod_exchange(c_row, w_ada, b_ada8, deps=()):
    cw = NMOD * D // N_DEV

    def body(c_ref, w_ref, b_ref, *rest):
        call_ref, mod_ref, part_ref, send_sems, recv_sems = rest[len(deps):]
        x, y, c = _me()
        me = 4 * x + 2 * y + c
        call_ref[me] = c_ref[0]
        sends = []
        for k in range(1, N_DEV):
            peer, _ = _peer(k)
            cp = pltpu.make_async_remote_copy(src_ref=c_ref.at[0], dst_ref=call_ref.at[me], send_sem=send_sems.at[0, k - 1],
                                              recv_sem=recv_sems.at[0, k - 1], device_id=peer, device_id_type=MESH)
            cp.start()
            sends.append(cp)
        for k in range(1, N_DEV):
            peer, pidx = _peer(k)
            pltpu.make_async_remote_copy(src_ref=c_ref.at[0], dst_ref=call_ref.at[pidx], send_sem=send_sems.at[0, k - 1],
                                         recv_sem=recv_sems.at[0, k - 1], device_id=peer, device_id_type=MESH).wait_recv()
        for cp in sends:
            cp.wait_send()
        cv = call_ref[...].reshape(N_DEV, D)
        part = jnp.dot(cv * jax.nn.sigmoid(cv), w_ref[...], preferred_element_type=F32, precision=HI)
        part_ref[...] = part.reshape(N_DEV, 1, cw)
        mod_ref[me] = part_ref[me]
        sends = []
        for k in range(1, N_DEV):
            peer, pidx = _peer(k)
            cp = pltpu.make_async_remote_copy(src_ref=part_ref.at[pidx], dst_ref=mod_ref.at[me], send_sem=send_sems.at[1, k - 1],
                                              recv_sem=recv_sems.at[1, k - 1], device_id=peer, device_id_type=MESH)
            cp.start()
            sends.append(cp)
        for k in range(1, N_DEV):
            peer, pidx = _peer(k)
            pltpu.make_async_remote_copy(src_ref=part_ref.at[pidx], dst_ref=mod_ref.at[pidx], send_sem=send_sems.at[1, k - 1],
                                         recv_sem=recv_sems.at[1, k - 1], device_id=peer, device_id_type=MESH).wait_recv()
        for cp in sends:
            cp.wait_send()
        mod_ref[...] = mod_ref[...] + b_ref[...]

    vm = pl.BlockSpec(memory_space=pltpu.VMEM)
    return pl.pallas_call(
        body, out_shape=(jax.ShapeDtypeStruct((N_DEV, 1, D), F32), jax.ShapeDtypeStruct((N_DEV, 1, cw), F32)),
        in_specs=[vm, vm, vm] + [pl.BlockSpec(memory_space=pl.ANY)] * len(deps), out_specs=(vm, vm),
        scratch_shapes=[pltpu.VMEM((N_DEV, 1, cw), F32), pltpu.SemaphoreType.DMA((2, N_DEV - 1)),
                        pltpu.SemaphoreType.DMA((2, N_DEV - 1))],
        name='mod_exchange', compiler_params=_params(None, 48))(c_row, w_ada, b_ada8, *deps)


HBM_SPEC = pl.BlockSpec(memory_space=pltpu.HBM)
SEM_SPEC = pl.BlockSpec(memory_space=pltpu.SEMAPHORE)
DATAFLOW = pltpu.SideEffectType.DATAFLOW_SIDE_EFFECTING


def _push_start(src, scatter, after, name):
    land = lax.empty(src.shape if scatter else (N_DEV,) + src.shape, src.dtype)

    def body(src_ref, land_ref, after_ref, send_sem, recv_sem, land_thru, token):
        x, y, c = _me()
        me = 4 * x + 2 * y + c
        for k in range(1, N_DEV):
            peer, pidx = _peer(k)
            pltpu.make_async_remote_copy(src_ref=src_ref.at[pidx] if scatter else src_ref, dst_ref=land_ref.at[me],
                                         send_sem=send_sem, recv_sem=recv_sem, device_id=peer,
                                         device_id_type=MESH).start()
        token[...] = jnp.zeros_like(token)

    own = src
    src = pltpu.with_memory_space_constraint(src, pltpu.HBM)
    send_sem, recv_sem, land_thru, token = pl.pallas_call(
        body, name=name,
        out_shape=(pltpu.SemaphoreType.DMA(()), pltpu.SemaphoreType.DMA(()),
                   pltpu.HBM(land.shape, land.dtype), jax.ShapeDtypeStruct((8, 128), F32)),
        in_specs=(HBM_SPEC, HBM_SPEC, pl.BlockSpec(memory_space=pl.ANY)),
        out_specs=(SEM_SPEC, SEM_SPEC, HBM_SPEC, pl.BlockSpec(memory_space=pltpu.VMEM)),
        input_output_aliases={1: 2}, compiler_params=pltpu.CompilerParams(has_side_effects=DATAFLOW),
    )(src, pltpu.with_memory_space_constraint(land, pltpu.HBM), after)
    return send_sem, recv_sem, src, land_thru, token, own


def _push_wait(handle, after, name, merge=True):
    send_sem, recv_sem, src, land_thru, _, own = handle
    after = tuple(after) if isinstance(after, (tuple, list)) else (after,)

    def body(src_ref, land_ref, send_sem, recv_sem, *rest):
        seven = land_ref.at[pl.ds(0, N_DEV - 1)]
        cp = pltpu.make_async_remote_copy(src_ref=seven, dst_ref=seven, send_sem=send_sem, recv_sem=recv_sem,
                                          device_id=_me(), device_id_type=MESH)
        cp.wait_send()
        cp.wait_recv()

    landed = pl.pallas_call(
        body, name=name, out_shape=pltpu.HBM(land_thru.shape, land_thru.dtype),
        in_specs=(HBM_SPEC, HBM_SPEC, SEM_SPEC, SEM_SPEC) + (pl.BlockSpec(memory_space=pl.ANY),) * len(after),
        out_specs=HBM_SPEC, input_output_aliases={1: 0},
        compiler_params=pltpu.CompilerParams(has_side_effects=DATAFLOW),
    )(src, land_thru, send_sem, recv_sem, *after)
    if not merge:
        return landed, own
    return lax.dynamic_update_index_in_dim(landed, own, _my_index(), 0)


def _adam(w, g, m, v):
    m2 = B1 * m + (1.0 - B1) * g
    v2 = B2 * v + (1.0 - B2) * jnp.square(g)
    m_hat = m2 / (1.0 - B1 ** STEP)
    v_hat = v2 / (1.0 - B2 ** STEP)
    delta = -LR * (m_hat / (jnp.sqrt(v_hat) + AEPS) + WD * w)
    return delta, m2, v2


def _small_update(gp, wp, mp, vp):
    def body(g_ref, w_ref, m_ref, v_ref, all_ref, go_ref, d_ref, mo_ref, vo_ref, send_sems, recv_sems):
        x, y, c = _me()
        me = 4 * x + 2 * y + c
        all_ref[me] = g_ref[...]
        sends = []
        for k in range(1, N_DEV):
            peer, _ = _peer(k)
            cp = pltpu.make_async_remote_copy(src_ref=g_ref, dst_ref=all_ref.at[me], send_sem=send_sems.at[k - 1],
                                              recv_sem=recv_sems.at[k - 1], device_id=peer, device_id_type=MESH)
            cp.start()
            sends.append(cp)
        for k in range(1, N_DEV):
            peer, pidx = _peer(k)
            pltpu.make_async_remote_copy(src_ref=g_ref, dst_ref=all_ref.at[pidx], send_sem=send_sems.at[k - 1],
                                         recv_sem=recv_sems.at[k - 1], device_id=peer, device_id_type=MESH).wait_recv()
        for cp in sends:
            cp.wait_send()
        g = all_ref[0]
        for d in range(1, N_DEV):
            g = g + all_ref[d]
        delta, m2, v2 = _adam(w_ref[...], g, m_ref[...], v_ref[...])
        go_ref[...] = g
        d_ref[...] = delta
        mo_ref[...] = m2
        vo_ref[...] = v2

    vm = pl.BlockSpec(memory_space=pltpu.VMEM)
    vs = jax.ShapeDtypeStruct(gp.shape, F32)
    return pl.pallas_call(
        body, out_shape=(jax.ShapeDtypeStruct((N_DEV,) + gp.shape, F32), vs, vs, vs, vs), in_specs=[vm] * 4,
        out_specs=(vm,) * 5,
        scratch_shapes=[pltpu.SemaphoreType.DMA((N_DEV - 1,)), pltpu.SemaphoreType.DMA((N_DEV - 1,))],
        name='small_update', compiler_params=_params(None, 48))(gp, wp, mp, vp)


def _small_sum_update(parts, wp, mp, vp):
    def body(p_ref, w_ref, m_ref, v_ref, go_ref, d_ref, mo_ref, vo_ref):
        g = p_ref[0]
        for d in range(1, N_DEV):
            g = g + p_ref[d]
        delta, m2, v2 = _adam(w_ref[...], g, m_ref[...], v_ref[...])
        go_ref[...] = g
        d_ref[...] = delta
        mo_ref[...] = m2
        vo_ref[...] = v2

    vm = pl.BlockSpec(memory_space=pltpu.VMEM)
    vs = jax.ShapeDtypeStruct(wp.shape, F32)
    return pl.pallas_call(body, out_shape=(vs, vs, vs, vs), in_specs=[vm] * 4, out_specs=(vm,) * 4,
                          name='small_sum_update', compiler_params=_params(None, 48))(parts, wp, mp, vp)


def _sum_update(me, p_ref, own_ref, w_ref, m_ref, v_ref, g_ref, d_ref, mo_ref, vo_ref):
    mine = own_ref[...].astype(F32)
    g = jnp.where(me == 0, mine, p_ref[0].astype(F32))
    for d in range(1, N_DEV):
        g = g + jnp.where(me == d, mine, p_ref[d].astype(F32))
    delta, m2, v2 = _adam(w_ref[...], g, m_ref[...], v_ref[...])
    g_ref[...] = g
    d_ref[...] = delta
    mo_ref[...] = m2
    vo_ref[...] = v2


def _big_update(parts, src, w, m, v, name):
    _, R, C = parts.shape
    tr = R if R % 256 else (128 if C >= 2048 else 256)

    def body(me_ref, *refs):
        _sum_update(me_ref[0], *refs)

    blk = pl.BlockSpec((tr, C), lambda i, me: (i, 0))
    sh = jax.ShapeDtypeStruct((R, C), F32)
    grid_spec = pltpu.PrefetchScalarGridSpec(
        num_scalar_prefetch=1, grid=(R // tr,),
        in_specs=[pl.BlockSpec((N_DEV, tr, C), lambda i, me: (0, i, 0)),
                  pl.BlockSpec((None, tr, C), lambda i, me: (me[0], i, 0)), blk, blk, blk],
        out_specs=(blk,) * 4)
    return pl.pallas_call(
        body, out_shape=(sh, sh, sh, sh), grid_spec=grid_spec, name=name,
        compiler_params=_params(('parallel',), 48))(_my_index().reshape(1), parts, src, w, m, v)


def _ada_update(c_all, dmod_cols, w, m, v):
    C = w.shape[1]
    tr = 256

    def body(c_ref, dm_ref, w_ref, m_ref, v_ref, g_ref, d_ref, mo_ref, vo_ref):
        cv = c_ref[...]
        s = cv * jax.nn.sigmoid(cv)
        g = lax.dot_general(s, dm_ref[...], TN_DIMS, preferred_element_type=F32, precision=HI)
        delta, m2, v2 = _adam(w_ref[...], g, m_ref[...], v_ref[...])
        g_ref[...] = g
        d_ref[...] = delta
        mo_ref[...] = m2
        vo_ref[...] = v2

    blk = pl.BlockSpec((tr, C), lambda i: (i, 0))
    sh = jax.ShapeDtypeStruct((D, C), F32)
    return pl.pallas_call(
        body, out_shape=(sh, sh, sh, sh), grid=(D // tr,),
        in_specs=[pl.BlockSpec((N_DEV, tr), lambda i: (0, i)), pl.BlockSpec((N_DEV, C), lambda i: (0, 0)), blk, blk, blk],
        out_specs=(blk,) * 4, name='ada_update', compiler_params=_params(('parallel',), 48))(c_all, dmod_cols, w, m, v)


def _rows_to_cluster_lanes(t):
    k = t.shape[1]
    return t.reshape(N_CL, CL_S, k).transpose(0, 2, 1)


def _blockdiag_in(t):
    t = t.reshape(N_CL, CL_G, SSM_N, SSM_P).transpose(0, 1, 3, 2)
    eye = jnp.eye(CL_G, dtype=t.dtype)
    t = t[:, :, :, None, :] * eye[None, :, None, :, None]
    return t.reshape(N_CL, CL_U, CL_S)


def _blockdiag_extract(t):
    t = t.reshape(N_CL, CL_G, SSM_P, CL_G, SSM_N)
    eye = jnp.eye(CL_G, dtype=t.dtype)
    t = jnp.sum(t * eye[None, :, None, :, None], axis=3)
    return t.transpose(0, 1, 3, 2).reshape(SSM_GN, SSM_P)


def _c_to_rows(t):
    return t.transpose(0, 2, 1).reshape(SSM_GN, SSM_P)


def _rows_to_c(t):
    return t.reshape(SSM_G, SSM_N, SSM_P).transpose(0, 2, 1)


def _ssm_prep(sp):
    rows = lambda n: sp[n].reshape(SSM_GN, 1)
    a_re, a_im = rows('ssm_a_re'), rows('ssm_a_im')
    ldt = jnp.repeat(sp['ssm_log_dt'].reshape(SSM_G, 1), SSM_N, axis=0)
    b_re, b_im = sp['ssm_b_re'].reshape(SSM_GN, SSM_P), sp['ssm_b_im'].reshape(SSM_GN, SSM_P)
    c_re, c_im = _c_to_rows(sp['ssm_c_re'].reshape(SSM_G, SSM_P, SSM_N)), _c_to_rows(sp['ssm_c_im'].reshape(SSM_G, SSM_P, SSM_N))
    bbr, bbi, pwr, pwi = _ssm_pre(a_re, a_im, ldt, b_re, b_im)
    bm = jnp.concatenate([_blockdiag_in(bbr), _blockdiag_in(bbi)], axis=2).astype(BF16)
    cmt = jnp.concatenate([_blockdiag_in(c_re), -_blockdiag_in(c_im)], axis=2).astype(BF16)
    bmt, cm = bm.transpose(0, 2, 1), cmt.transpose(0, 2, 1)
    pw = jnp.concatenate([_rows_to_cluster_lanes(pwr), _rows_to_cluster_lanes(pwi)], axis=2)
    return a_re, a_im, ldt, b_re, b_im, bm, cm, bmt, cmt, pw


def _tied(v, deps):
    for t in deps:
        v = v + t[0, 0]
    return v


def _local_step(x, pos, mod, tgt, sp, prep, get_w, emit, emit_small, emit_late, rider_for, first_deps=()):
    sh1, sc1, gt1, sh2, sc2, gt2 = (mod[i:i + 1] for i in range(NMOD))
    vec = lambda n: sp[n].reshape(1, -1)
    a_re, a_im, ldt, b_re, b_im, bm_b, cm_b, bmt_b, cmt_b, pw = prep
    dvec = vec('ssm_d')

    h1 = _prenorm_fwd(x, vec('g_pre_mix'), sc1, sh1)
    w_in = get_w('w_in', (h1, bm_b, cm_b, pw, bmt_b, cmt_b))
    proj = _mm(h1, w_in, mode='nn', name='mm_in', tm=512, tn=INW, deps=first_deps)
    fr1 =ROPE_THETA ** (-jnp.arange(0, ROT_DIM, 2, dtype=F32) / ROT_DIM)
    lane = jnp.arange(128) % HEAD_DIM
    fr = jnp.where(lane < ROT_DIM, fr1[lane % (ROT_DIM // 2)], 0.0).reshape(1, 128).astype(F32)
    u_off = (ROPE_W + KVW) // CL_U
    qkvs = _rope_fwd(proj, pos, fr)
    fwd = [_attn_fwd(qkvs[g], d) for g, d in enumerate(DILATIONS)]
    os_, lses = [t[0] for t in fwd], [t[1] for t in fwd]
    att = _combine_fwd(os_, lses, vec('g_attn_out'))

    ypre, bnd, states = _ssm_fwd(proj, bm_b, cm_b, pw, dvec, u_off)
    w_glu = get_w('w_glu', ypre)
    ssm_n = _glu_fwd(ypre, w_glu, vec('b_glu'), vec('g_ssm_out'))

    cat = jnp.concatenate([att, ssm_n], axis=1)
    w_out = get_w('w_out', cat)
    mix = _mm(cat, w_out, mode='nn', name='mm_out', tk=1280)
    x1, h2 = _postmix_fwd(x, mix, vec('g_post_mix'), gt1, vec('g_pre_mlp'), sc2, sh2)
    w_mi = get_w('w_mlp_in', h2)
    a_pre, r_act = _mm(h2, w_mi, mode='nn', name='mm_mlp_in', epilogue='relu2', b_sharded=True)
    w_mo = get_w('w_mlp_out', a_pre)
    y = _mm(r_act, w_mo, mode='nn', name='mm_mlp_out')
    dx2, dy, loss, dgt2, dg_post_mlp = _final_fwd_bwd(x1, y, tgt, vec('g_post_mlp'), gt2)
    dgt2, dg_post_mlp = dgt2[:1], dg_post_mlp[:1]

    da = _mm(dy, w_mo, mode='nt', name='mm_d_act', out_dtype=BF16, epilogue='drelu2', extra=a_pre)
    dep = emit('w_mlp_out', _mm(r_act, dy, mode='tn', name='mm_dw_mlp_out', out_dtype=BF16))
    dh2 = _mm(da, w_mi, mode='nt', name='mm_dh2', tk=2048, out_dtype=BF16, b_sharded=True, deps=dep)
    dep = emit('w_mlp_in', _mm(h2, da, mode='tn', name='mm_dw_mlp_in', out_dtype=BF16, out_sharded=True))
    dx1, dmix, dsc2, dsh2, dg_pre_mlp, dgt1, dg_post_mix = _postmix_bwd(
        dx2, dh2, x1, mix, vec('g_post_mix'), gt1, vec('g_pre_mlp'), sc2)
    dsc2, dsh2, dg_pre_mlp, dgt1, dg_post_mix = (t[:1] for t in (dsc2, dsh2, dg_pre_mlp, dgt1, dg_post_mix))
    dcat = _mm(dmix, w_out, mode='nt', name='mm_dcat', tn=1280, out_dtype=BF16, deps=dep)
    dep = emit('w_out', _mm(cat, dmix, mode='tn', name='mm_dw_out', out_dtype=BF16, tm=640))
    dypre, g_w_glu, g_b_glu, g_g_ssm = _glu_bwd(ypre, dcat, w_glu, _tied(vec('b_glu'), dep), vec('g_ssm_out'))
    g_b_glu, g_g_ssm = g_b_glu[:1], g_g_ssm[:1]
    dep = dep + emit('w_glu', g_w_glu.astype(BF16))
    du, dbm, dcm, dA, dD = _ssm_bwd(proj, dypre, states, bmt_b, cmt_b, pw, dvec, bnd, u_off)
    dD = dD[:1]
    gbr, gbi = _blockdiag_extract(dbm[:, :, :CL_S]), _blockdiag_extract(dbm[:, :, CL_S:])
    dcmt = dcm.transpose(0, 2, 1)
    g_c_re = _rows_to_c(_blockdiag_extract(dcmt[:, :, :CL_S]))
    g_c_im = _rows_to_c(-_blockdiag_extract(dcmt[:, :, CL_S:]))
    gar = dA[:, 0, :CL_S].reshape(SSM_GN, 1)
    gai = dA[:, 0, CL_S:].reshape(SSM_GN, 1)
    sel = (jnp.arange(SSM_GN)[None, :] // SSM_N == jnp.arange(SSM_G)[:, None]).astype(F32)
    g_a_re, g_a_im, g_b_re, g_b_im, g_ldt = _ssm_post(a_re, a_im, ldt, b_re, b_im, gar, gai, gbr, gbi, sel)

    head_ones = (jnp.arange(KVW)[:, None] // HEAD_DIM == jnp.arange(KVW)[None, :] // HEAD_DIM).astype(F32)
    dos, dlses, g_g_attn = _combine_bwd(dcat, os_, lses, vec('g_attn_out'), head_ones)
    g_g_attn = g_g_attn[:1]
    dep_small = emit_small({
        'g_post_mix': dg_post_mix, 'ssm_a_re': g_a_re, 'ssm_a_im': g_a_im, 'ssm_log_dt': g_ldt[:, 0],
        'ssm_b_re': g_b_re, 'ssm_b_im': g_b_im, 'ssm_c_re': g_c_re, 'ssm_c_im': g_c_im, 'ssm_d': dD, 'b_glu': g_b_glu,
        'g_attn_out': g_g_attn, 'g_ssm_out': g_g_ssm, 'g_pre_mlp': dg_pre_mlp, 'g_post_mlp': dg_post_mlp})
    dqkv, after = [], (g_a_re,) + tuple(dep_small)
    for g, d in enumerate(DILATIONS):
        rider, take = rider_for(g, after)
        outs = _attn_bwd(qkvs[g], os_[g], lses[g], dos[g], dlses[g], d, rider)
        if rider is not None:
            take(outs[3:])
        dqkv.append(outs[:3])
        after = (outs[0],)
    dproj = _rope_bwd(dqkv, du, pos, _tied(fr, dep_small))
    dh1 = _mm(dproj, w_in, mode='nt', name='mm_dh1', tk=INW, out_dtype=BF16, deps=dep)
    grad_x, dsc1, dsh1, dg_pre_mix = _prenorm_bwd(dx1, dh1, x, vec('g_pre_mix'), sc1)
    dsc1, dsh1, dg_pre_mix = dsc1[:1], dsh1[:1], dg_pre_mix[:1]
    dmod = jnp.concatenate([dsh1, dsc1, dgt1, dsh2, dsc2, dgt2], axis=0)
    dep = emit_late({'b_ada': dmod, 'g_pre_mix': dg_pre_mix})
    emit('w_in', _mm(h1, dproj, mode='tn', name='mm_dw_in', out_dtype=BF16, tm=512, tn=INW, deps=dep))
    return loss[0, 0], grad_x


def _pack(d, names):
    flat = jnp.concatenate([jnp.pad(d[n].reshape(-1).astype(F32), (0, SEG[n] - SMALL_SIZES[n])) for n in names])
    return flat.reshape(-1, 128)


def _unpack(packed, names, shapes):
    out, off = {}, 0
    for n in names:
        out[n] = packed[off // 128:(off + SEG[n]) // 128].reshape(-1)[:SMALL_SIZES[n]].reshape(shapes[n])
        off += SEG[n]
    return out


def _shard_major(t, name):
    if name in ('w_in', 'w_out', 'w_mlp_in'):
        k, n = t.shape
        return t.reshape(k, N_DEV, n // N_DEV).transpose(1, 0, 2)
    k, n = t.shape
    return t.reshape(N_DEV, k // N_DEV, n)


def _from_shard_major(t, name):
    if name in ('w_in', 'w_out', 'w_mlp_in'):
        _, k, n = t.shape
        return t.transpose(1, 0, 2).reshape(k, N_DEV * n)
    _, k, n = t.shape
    return t.reshape(N_DEV * k, n)


def kernel(x, c, positions, w_ada, b_ada, g_pre_mix, g_post_mix, w_in, ssm_a_re, ssm_a_im, ssm_log_dt, ssm_b_re, ssm_b_im, ssm_c_re, ssm_c_im, ssm_d, w_glu, b_glu, g_attn_out, g_ssm_out, w_out, g_pre_mlp, g_post_mlp, w_mlp_in, w_mlp_out, loss_target, m_w_ada, m_b_ada, m_g_pre_mix, m_g_post_mix, m_w_in, m_ssm_a_re, m_ssm_a_im, m_ssm_log_dt, m_ssm_b_re, m_ssm_b_im, m_ssm_c_re, m_ssm_c_im, m_ssm_d, m_w_glu, m_b_glu, m_g_attn_out, m_g_ssm_out, m_w_out, m_g_pre_mlp, m_g_post_mlp, m_w_mlp_in, m_w_mlp_out, v_w_ada, v_b_ada, v_g_pre_mix, v_g_post_mix, v_w_in, v_ssm_a_re, v_ssm_a_im, v_ssm_log_dt, v_ssm_b_re, v_ssm_b_im, v_ssm_c_re, v_ssm_c_im, v_ssm_d, v_w_glu, v_b_glu, v_g_attn_out, v_g_ssm_out, v_w_out, v_g_pre_mlp, v_g_post_mlp, v_w_mlp_in, v_w_mlp_out):
    loc = dict(locals())
    W = {n: loc[n] for n in WEIGHTS}
    M = {n: loc['m_' + n] for n in WEIGHTS}
    V = {n: loc['v_' + n] for n in WEIGHTS}
    assert x.shape == (1, L, D) and w_in.shape == (1, D, INW // N_DEV), (x.shape, w_in.shape)

    cw = NMOD * D // N_DEV
    c_all, mod8 = _mod_exchange(c.reshape(1, 1, D), w_ada[0], b_ada.reshape(N_DEV, 1, cw))
    mod = mod8.reshape(NMOD, D)

    gather, after = {}, mod8
    for n in BIG:
        gather[n] = _push_start(W[n][0].astype(BF16), False, after, 'gather_start_' + n)
        after = gather[n][4]
    tokens = tuple(gather[n][4] for n in BIG)
    mod = _tied(mod, tokens)
    sp = {n: W[n][0] for n in SMALL}
    prep = _ssm_prep({**sp, 'ssm_a_re': _tied(sp['ssm_a_re'], tokens)})

    state_packs = {}

    def get_w(n, after):
        if n == 'w_mlp_out':
            tok = jnp.minimum(jnp.abs(after[0:8, 0:128].astype(F32)), 0.0)
            for key, src in (('w', W), ('m', M), ('v', V)):
                tied = {p: _tied(src[p], (tok,)) for p in SMALL}
                state_packs[key] = {'early': _pack(tied, SMALL_EARLY), 'late': _pack(tied, SMALL_LATE)}
            after = (after,) + tuple(state_packs[k][part] for k in 'wmv' for part in ('early', 'late'))
        g = _push_wait(gather[n], after, 'gather_wait_' + n)
        return g if n == 'w_mlp_in' else _from_shard_major(g, n)

    scatter = {}

    def emit(n, g):
        src = g if n == 'w_mlp_in' else _shard_major(g, n)
        scatter[n] = _push_start(src, True, src, 'scatter_start_' + n)
        return (scatter[n][4],)

    small_early = []

    def emit_small(d):
        pack = _pack(d, SMALL_EARLY)
        small_early.append(_push_start(pack, False, pack, 'small_start'))
        return (small_early[0][4],)

    out_g, out_d, out_m, out_v = {}, {}, {}, {}
    shapes = {n: W[n].shape[1:] for n in SMALL}

    def put(names, packs):
        for dst, packed in zip((out_g, out_d, out_m, out_v), packs):
            dst.update(_unpack(packed, names, shapes))

    late = []

    def emit_late(d):
        rows_all, *packs = _small_update(_pack(d, SMALL_LATE), *[state_packs[k]['late'] for k in 'wmv'])
        put(SMALL_LATE, packs)
        late.append(rows_all)
        return (rows_all,)

    def rider_for(i, after):
        if i >= 2:
            return None, None
        n = ('w_mlp_out', 'w_mlp_in')[i]
        parts, src = _push_wait(scatter[n], after, 'scatter_wait_' + n, merge=False)
        me1 = _my_index()
        own = lax.dynamic_index_in_dim(src, me1, 0, keepdims=False)

        def take(outs):
            out_g[n], out_d[n], out_m[n], out_v[n] = outs

        return (me1.reshape(1), parts, own, W[n][0], M[n][0], V[n][0]), take

    loss, grad_x = _local_step(x[0], positions.reshape(L, 1), mod, loss_target[0], sp, prep, get_w, emit, emit_small,
                               emit_late, rider_for)
    loss = lax.psum(loss, ('x', 'y', 'c'))

    me = 4 * lax.axis_index('x') + 2 * lax.axis_index('y') + lax.axis_index('c')
    dmod_all = late[0][:, :NMOD * D // 128].reshape(N_DEV, NMOD * D)
    dmod_cols = _tied(lax.dynamic_slice_in_dim(dmod_all, me * cw, cw, axis=1), (scatter['w_in'][4],))
    out_g['w_ada'], out_d['w_ada'], out_m['w_ada'], out_v['w_ada'] = _ada_update(
        c_all.reshape(N_DEV, D), dmod_cols, w_ada[0], m_w_ada[0], v_w_ada[0])

    parts = _push_wait(small_early[0], out_v['w_ada'], 'small_wait')
    packs = _small_sum_update(parts, *[state_packs[k]['early'] for k in 'wmv'])
    put(SMALL_EARLY, packs)

    after = packs[3]
    for n in ('w_out', 'w_glu', 'w_in'):
        parts, src = _push_wait(scatter[n], after, 'scatter_wait_' + n, merge=False)
        out_g[n], out_d[n], out_m[n], out_v[n] = _big_update(parts, src, W[n][0], M[n][0], V[n][0], 'update_' + n)
        after = out_v[n]

    lead = lambda t: t[None]
    return (loss, grad_x[None], *[lead(out_g[n]) for n in WEIGHTS], *[lead(out_d[n]) for n in WEIGHTS],
            *[lead(out_m[n]) for n in WEIGHTS], *[lead(out_v[n]) for n in WEIGHTS])
```

```python
import math

import jax
import jax.numpy as jnp
from jax import lax
from jax.experimental import pallas as pl
from jax.experimental.pallas import tpu as pltpu

F32 = jnp.float32
BF16 = jnp.bfloat16
HI = lax.Precision.HIGHEST
MESH = pl.DeviceIdType.MESH

N_DEV = 8
L = 4096
D = 2048
HEAD_DIM = 64
N_GROUPS = 3
DILATIONS = (1, 4, 16)
HEADS = 6
QW = N_GROUPS * HEADS * HEAD_DIM
KVW = HEADS * HEAD_DIM
ROT_DIM = 16
ROPE_THETA = 500000.0
BLK = 128
NBLK = L // BLK
SSMW = D - QW
SSM_P = 16
SSM_G = SSMW // SSM_P
SSM_N = 64
SSM_GN = SSM_G * SSM_N
CL_G = 8
N_CL = SSM_G // CL_G
CL_U = CL_G * SSM_P
CL_S = CL_G * SSM_N
INW = QW + 2 * KVW + SSMW
OUTW = KVW + SSMW
DFF = 4 * D
NMOD = 6
EPS = 1e-6
LR, B1, B2, AEPS, WD, STEP = 0.001, 0.9, 0.999, 1e-08, 0.01, 10

T_SCAN = 1024
MB = 2 ** 20

WEIGHTS = ['w_ada', 'b_ada', 'g_pre_mix', 'g_post_mix', 'w_in', 'ssm_a_re', 'ssm_a_im', 'ssm_log_dt',
           'ssm_b_re', 'ssm_b_im', 'ssm_c_re', 'ssm_c_im', 'ssm_d', 'w_glu', 'b_glu', 'g_attn_out',
           'g_ssm_out', 'w_out', 'g_pre_mlp', 'g_post_mlp', 'w_mlp_in', 'w_mlp_out']
BIG = ['w_in', 'w_glu', 'w_out', 'w_mlp_in', 'w_mlp_out']
SMALL = [n for n in WEIGHTS if n not in BIG and n != 'w_ada']
SMALL_SIZES = {'b_ada': NMOD * D, 'g_pre_mix': D, 'g_post_mix': D, 'ssm_a_re': SSM_GN, 'ssm_a_im': SSM_GN,
               'ssm_log_dt': SSM_G, 'ssm_b_re': SSM_GN * SSM_P, 'ssm_b_im': SSM_GN * SSM_P,
               'ssm_c_re': SSM_GN * SSM_P, 'ssm_c_im': SSM_GN * SSM_P, 'ssm_d': SSMW, 'b_glu': SSMW,
               'g_attn_out': KVW, 'g_ssm_out': SSMW, 'g_pre_mlp': D, 'g_post_mlp': D}
SEG = {n: -(-SMALL_SIZES[n] // 1024) * 1024 for n in SMALL}
SMALL_LATE = ['b_ada', 'g_pre_mix']
SMALL_EARLY = [n for n in SMALL if n not in SMALL_LATE]


def _params(sem=None, vmem_mb=None):
    kw = {}
    if sem is not None:
        kw['dimension_semantics'] = sem
    if vmem_mb is not None:
        kw['vmem_limit_bytes'] = vmem_mb * MB
    return pltpu.CompilerParams(**kw)


def _vec(n):
    return pl.BlockSpec((1, n), lambda *_: (0, 0))


def _rms(x):
    return lax.rsqrt(jnp.mean(x * x, axis=-1, keepdims=True) + EPS)


def _rms_bwd(dn, n, r):
    return r * (dn - n * jnp.mean(dn * n, axis=-1, keepdims=True))


def _vec8(n):
    return pl.BlockSpec((8, n), lambda *_: (0, 0))


def _colsum(x):
    return jnp.sum(x.reshape(-1, 8, x.shape[-1]), axis=0)


def _fold8(last, *refs):
    @pl.when(last)
    def _():
        for r in refs:
            r[...] = jnp.broadcast_to(jnp.sum(r[...], axis=0, keepdims=True), r.shape)


def _mm(a, b, *, mode, name, out_dtype=F32, tm=1024, tn=1024, tk=2048, epilogue=None, extra=None,
        b_sharded=False, out_sharded=False, deps=()):
    if mode == 'nn':
        M, K = a.shape
        dims = (((1,), (0,)), ((), ()))
        a_spec = pl.BlockSpec((tm, tk), lambda i, j, k: (i, k))
        if b_sharded:
            _, K2, per = b.shape
            N, q = N_DEV * per, per // tn
            b_spec = pl.BlockSpec((None, tk, tn), lambda i, j, k: (j // q, k, j % q))
        else:
            K2, N = b.shape
            b_spec = pl.BlockSpec((tk, tn), lambda i, j, k: (k, j))
    elif mode == 'nt':
        M, K = a.shape
        dims = (((1,), (1,)), ((), ()))
        a_spec = pl.BlockSpec((tm, tk), lambda i, j, k: (i, k))
        if b_sharded:
            _, N, per = b.shape
            K2 = N_DEV * per
            if tk > per:
                b_spec = pl.BlockSpec((tk // per, tn, per), lambda i, j, k: (k, j, 0))
            else:
                q = per // tk
                b_spec = pl.BlockSpec((None, tn, tk), lambda i, j, k: (k // q, j, k % q))
        else:
            N, K2 = b.shape
            b_spec = pl.BlockSpec((tn, tk), lambda i, j, k: (j, k))
    else:
        (K, M), (K2, N) = a.shape, b.shape
        dims = (((0,), (0,)), ((), ()))
        a_spec = pl.BlockSpec((tk, tm), lambda i, j, k: (k, i))
        b_spec = pl.BlockSpec((tk, tn), lambda i, j, k: (k, j))
    assert K == K2 and M % tm == 0 and N % tn == 0 and K % tk == 0, (name, a.shape, b.shape, tm, tn, tk)
    nk = K // tk
    o_spec = pl.BlockSpec((tm, tn), lambda i, j, k: (i, j))
    o_dims = (M, N)
    if out_sharded:
        qo = N // N_DEV // tn
        o_spec = pl.BlockSpec((None, tm, tn), lambda i, j, k: (j // qo, i, j % qo))
        o_dims = (N_DEV, M, N // N_DEV)
    n_out = 2 if epilogue == 'relu2' else 1
    n_extra = 1 if extra is not None else 0
    n_in = 2 + n_extra + len(deps)

    def body(*refs):
        a_ref, b_ref = refs[0], refs[1]
        x_refs = refs[2:2 + n_extra]
        o_refs = refs[n_in:n_in + n_out]
        acc = refs[-1]
        k = pl.program_id(2)

        if len(b_ref.shape) == 3:
            per_ = b_ref.shape[2]
            prod = sum(lax.dot_general(a_ref[:, s * per_:(s + 1) * per_], b_ref[s], dims, preferred_element_type=F32)
                       for s in range(b_ref.shape[0]))
        else:
            prod = lax.dot_general(a_ref[...], b_ref[...], dims, preferred_element_type=F32)

        def finish(r):
            if epilogue == 'relu2':
                o_refs[0][...] = r.astype(BF16)
                o_refs[1][...] = jnp.square(jnp.maximum(r, 0.0)).astype(BF16)
            elif epilogue == 'drelu2':
                pre = x_refs[0][...].astype(F32)
                o_refs[0][...] = (r * (2.0 * jnp.maximum(pre, 0.0))).astype(out_dtype)
            else:
                o_refs[0][...] = r.astype(out_dtype)

        if nk == 1:
            finish(prod)
        else:
            @pl.when(k == 0)
            def _():
                acc[...] = prod

            @pl.when((k > 0) & (k < nk - 1))
            def _():
                acc[...] += prod

            @pl.when(k == nk - 1)
            def _():
                finish(acc[...] + prod)

    if epilogue == 'relu2':
        out_shape = (jax.ShapeDtypeStruct((M, N), BF16), jax.ShapeDtypeStruct((M, N), BF16))
        out_specs = (o_spec, o_spec)
    else:
        out_shape = jax.ShapeDtypeStruct(o_dims, out_dtype)
        out_specs = o_spec
    args = (a, b) + ((extra,) if extra is not None else ()) + tuple(deps)
    in_specs = ([a_spec, b_spec] + ([o_spec] if extra is not None else [])
                + [pl.BlockSpec(memory_space=pl.ANY)] * len(deps))
    return pl.pallas_call(
        body, out_shape=out_shape, grid=(M // tm, N // tn, nk), in_specs=in_specs, out_specs=out_specs,
        scratch_shapes=[pltpu.VMEM((tm, tn) if nk > 1 else (8, 128), F32)], name=name,
        compiler_params=_params(('parallel', 'parallel', 'arbitrary'), 56))(*args)


TR = 256


def _rowspec(w=D):
    return pl.BlockSpec((TR, w), lambda i: (i, 0))


def _prenorm_fwd(x, g, sc, sh):
    def body(x_ref, g_ref, sc_ref, sh_ref, h_ref):
        xv = x_ref[...]
        n = xv * _rms(xv)
        h_ref[...] = ((n * g_ref[...]) * (1.0 + sc_ref[...]) + sh_ref[...]).astype(BF16)

    return pl.pallas_call(
        body, out_shape=jax.ShapeDtypeStruct((L, D), BF16), grid=(L // TR,),
        in_specs=[_rowspec(), _vec(D), _vec(D), _vec(D)], out_specs=_rowspec(), name='prenorm_fwd',
        compiler_params=_params(('parallel',), 40))(x, g, sc, sh)


def _postmix_fwd(x, mix, gpm, gt1, gpl, sc2, sh2):
    def body(x_ref, mix_ref, gpm_ref, gt1_ref, gpl_ref, sc2_ref, sh2_ref, x1_ref, h2_ref):
        mix_v = mix_ref[...]
        nm = mix_v * _rms(mix_v)
        x1 = x_ref[...] + gt1_ref[...] * (nm * gpm_ref[...])
        x1_ref[...] = x1
        n2 = x1 * _rms(x1)
        h2_ref[...] = ((n2 * gpl_ref[...]) * (1.0 + sc2_ref[...]) + sh2_ref[...]).astype(BF16)

    return pl.pallas_call(
        body, out_shape=(jax.ShapeDtypeStruct((L, D), F32), jax.ShapeDtypeStruct((L, D), BF16)), grid=(L // TR,),
        in_specs=[_rowspec(), _rowspec()] + [_vec(D)] * 5, out_specs=(_rowspec(), _rowspec()), name='postmix_fwd',
        compiler_params=_params(('parallel',), 40))(x, mix, gpm, gt1, gpl, sc2, sh2)


def _final_fwd_bwd(x1, y, tgt, g, gt2):
    def body(x1_ref, y_ref, t_ref, g_ref, gt2_ref, dx2_ref, dy_ref, loss_ref, dgt2_ref, dg_ref):
        @pl.when(pl.program_id(0) == 0)
        def _():
            loss_ref[...] = jnp.zeros_like(loss_ref)
            dgt2_ref[...] = jnp.zeros_like(dgt2_ref)
            dg_ref[...] = jnp.zeros_like(dg_ref)

        yv = y_ref[...]
        r = _rms(yv)
        n = yv * r
        ng = n * g_ref[...]
        x2 = x1_ref[...] + gt2_ref[...] * ng
        e = x2 - t_ref[...]
        loss_ref[...] += 0.5 * jnp.sum(jnp.mean(e * e, axis=-1, keepdims=True), axis=0, keepdims=True)
        dx2 = e * (1.0 / D)
        dx2_ref[...] = dx2
        dgt2_ref[...] += _colsum(dx2 * ng)
        dng = dx2 * gt2_ref[...]
        dg_ref[...] += _colsum(dng * n)
        dy_ref[...] = _rms_bwd(dng * g_ref[...], n, r).astype(BF16)
        _fold8(pl.program_id(0) == L // TR - 1, dgt2_ref, dg_ref)

    return pl.pallas_call(
        body,
        out_shape=(jax.ShapeDtypeStruct((L, D), F32), jax.ShapeDtypeStruct((L, D), BF16),
                   jax.ShapeDtypeStruct((8, 128), F32), jax.ShapeDtypeStruct((8, D), F32),
                   jax.ShapeDtypeStruct((8, D), F32)),
        grid=(L // TR,), in_specs=[_rowspec(), _rowspec(), _rowspec(), _vec(D), _vec(D)],
        out_specs=(_rowspec(), _rowspec(), _vec8(128), _vec8(D), _vec8(D)), name='final_fwd_bwd',
        compiler_params=_params(('arbitrary',), 40))(x1, y, tgt, g, gt2)


def _postmix_bwd(dx2, dh2, x1, mix, gpm, gt1, gpl, sc2):
    def body(dx2_ref, dh2_ref, x1_ref, mix_ref, gpm_ref, gt1_ref, gpl_ref, sc2_ref,
             dx1_ref, dmix_ref, dsc2_ref, dsh2_ref, dgpl_ref, dgt1_ref, dgpm_ref):
        @pl.when(pl.program_id(0) == 0)
        def _():
            for r_ in (dsc2_ref, dsh2_ref, dgpl_ref, dgt1_ref, dgpm_ref):
                r_[...] = jnp.zeros_like(r_)

        x1v = x1_ref[...]
        r2 = _rms(x1v)
        n2 = x1v * r2
        dh2v = dh2_ref[...].astype(F32)
        dsh2_ref[...] += _colsum(dh2v)
        dsc2_ref[...] += _colsum(dh2v * (n2 * gpl_ref[...]))
        t = dh2v * (1.0 + sc2_ref[...])
        dgpl_ref[...] += _colsum(t * n2)
        dx1 = dx2_ref[...] + _rms_bwd(t * gpl_ref[...], n2, r2)
        dx1_ref[...] = dx1
        mix_v = mix_ref[...]
        rm = _rms(mix_v)
        nm = mix_v * rm
        dgt1_ref[...] += _colsum(dx1 * (nm * gpm_ref[...]))
        u = dx1 * gt1_ref[...]
        dgpm_ref[...] += _colsum(u * nm)
        dmix_ref[...] = _rms_bwd(u * gpm_ref[...], nm, rm).astype(BF16)
        _fold8(pl.program_id(0) == L // TR - 1, dsc2_ref, dsh2_ref, dgpl_ref, dgt1_ref, dgpm_ref)

    vs = jax.ShapeDtypeStruct((8, D), F32)
    return pl.pallas_call(
        body, out_shape=(jax.ShapeDtypeStruct((L, D), F32), jax.ShapeDtypeStruct((L, D), BF16), vs, vs, vs, vs, vs),
        grid=(L // TR,), in_specs=[_rowspec()] * 4 + [_vec(D)] * 4,
        out_specs=(_rowspec(), _rowspec()) + (_vec8(D),) * 5, name='postmix_bwd',
        compiler_params=_params(('arbitrary',), 48))(dx2, dh2, x1, mix, gpm, gt1, gpl, sc2)


def _prenorm_bwd(dx1, dh1, x, g, sc1):
    def body(dx1_ref, dh1_ref, x_ref, g_ref, sc1_ref, dx_ref, dsc1_ref, dsh1_ref, dg_ref):
        @pl.when(pl.program_id(0) == 0)
        def _():
            for r_ in (dsc1_ref, dsh1_ref, dg_ref):
                r_[...] = jnp.zeros_like(r_)

        xv = x_ref[...]
        r = _rms(xv)
        n = xv * r
        dh = dh1_ref[...].astype(F32)
        dsh1_ref[...] += _colsum(dh)
        dsc1_ref[...] += _colsum(dh * (n * g_ref[...]))
        t = dh * (1.0 + sc1_ref[...])
        dg_ref[...] += _colsum(t * n)
        dx_ref[...] = dx1_ref[...] + _rms_bwd(t * g_ref[...], n, r)
        _fold8(pl.program_id(0) == L // TR - 1, dsc1_ref, dsh1_ref, dg_ref)

    vs = jax.ShapeDtypeStruct((8, D), F32)
    return pl.pallas_call(
        body, out_shape=(jax.ShapeDtypeStruct((L, D), F32), vs, vs, vs), grid=(L // TR,),
        in_specs=[_rowspec()] * 3 + [_vec(D)] * 2, out_specs=(_rowspec(),) + (_vec8(D),) * 3, name='prenorm_bwd',
        compiler_params=_params(('arbitrary',), 40))(dx1, dh1, x, g, sc1)


ROPE_W = QW + KVW
QKV_W = 3 * KVW
QK_SCALE = 0.125
NB_KV = KVW // 128


def _rope_rotate(xv, pos, fr, sign):
    ang = pos.astype(F32) * fr
    w = lax.broadcasted_iota(jnp.int32, (1, 128), 1) % HEAD_DIM
    cs = jnp.cos(ang)
    sn = jnp.sin(ang) * sign
    s1 = jnp.where(w < ROT_DIM // 2, -sn, 0.0)
    s2 = jnp.where((w >= ROT_DIM // 2) & (w < ROT_DIM), sn, 0.0)
    width = xv.shape[1]
    rep = width // 128
    cs, s1, s2 = jnp.tile(cs, (1, rep)), jnp.tile(s1, (1, rep)), jnp.tile(s2, (1, rep))
    hi = pltpu.roll(xv, width - ROT_DIM // 2, 1)
    lo = pltpu.roll(xv, ROT_DIM // 2, 1)
    return xv * cs + hi * s1 + lo * s2


def _sub_spec(d, rows, width):
    return pl.BlockSpec((d, rows // d, width), lambda i: (0, i, 0))


def _gather_rows(scr, blocks, r, d, rows):
    return jnp.concatenate([scr.at[j][pl.ds(r, rows // d, stride=d), :] for j in blocks], axis=1)


def _scatter_rows(scr, src_ref, d, rows):
    for r in range(d):
        for j in range(NB_KV):
            scr.at[j][pl.ds(r, rows // d, stride=d), :] = src_ref[r, :, j * 128:(j + 1) * 128]


def _token_order(scr):
    return jnp.concatenate([scr[j] for j in range(NB_KV)], axis=1)


def _rope_fwd(proj, pos, fr):
    nb = (ROPE_W + KVW) // 128

    def body(x_ref, pos_ref, fr_ref, o0_ref, o1_ref, o2_ref, scr):
        y = _rope_rotate(x_ref[:, 0:ROPE_W], pos_ref[...], fr_ref[...], 1.0)
        for j in range(ROPE_W // 128):
            scr[j] = y[:, j * 128:(j + 1) * 128] * (QK_SCALE if j < QW // 128 else 1.0)
        for j in range(ROPE_W // 128, nb):
            scr[j] = x_ref[:, j * 128:(j + 1) * 128]
        kv = list(range(QW // 128, nb))
        for g, (d, o_ref) in enumerate(zip(DILATIONS, (o0_ref, o1_ref, o2_ref))):
            blocks = list(range(g * NB_KV, (g + 1) * NB_KV)) + kv
            for r in range(d):
                o_ref[r] = _gather_rows(scr, blocks, r, d, TR).astype(BF16)

    return pl.pallas_call(
        body, out_shape=tuple(jax.ShapeDtypeStruct((d, L // d, QKV_W), BF16) for d in DILATIONS), grid=(L // TR,),
        in_specs=[_rowspec(ROPE_W + KVW), pl.BlockSpec((TR, 1), lambda i: (i, 0)), _vec(128)],
        out_specs=tuple(_sub_spec(d, TR, QKV_W) for d in DILATIONS),
        scratch_shapes=[pltpu.VMEM((nb, TR, 128), F32)], name='rope_fwd',
        compiler_params=_params(('parallel',), 40))(proj, pos, fr)


def _rope_bwd(dqkv, du, pos, fr):
    def body(*refs):
        grads = [refs[3 * g:3 * g + 3] for g in range(N_GROUPS)]
        du_ref, pos_ref, fr_ref, o_ref = refs[9:13]
        scrs = refs[13:]
        dq, dk, dv = [], None, None
        for g, d in enumerate(DILATIONS):
            parts = []
            for t in range(3):
                if d == 1:
                    parts.append(grads[g][t][0])
                else:
                    scr = scrs[3 * (g - 1) + t]
                    _scatter_rows(scr, grads[g][t], d, TR)
                    parts.append(_token_order(scr))
            dq.append(parts[0])
            dk = parts[1] if dk is None else dk + parts[1]
            dv = parts[2] if dv is None else dv + parts[2]
        x = jnp.concatenate(dq + [dk], axis=1)
        o_ref[:, 0:ROPE_W] = _rope_rotate(x, pos_ref[...], fr_ref[...], -1.0).astype(BF16)
        o_ref[:, ROPE_W:ROPE_W + KVW] = dv.astype(BF16)
        o_ref[:, ROPE_W + KVW:INW] = du_ref[...].astype(BF16)

    flat = [a for grp in dqkv for a in grp]
    in_specs = [_sub_spec(d, TR, KVW) for d in DILATIONS for _ in range(3)]
    in_specs += [_rowspec(SSMW), pl.BlockSpec((TR, 1), lambda i: (i, 0)), _vec(128)]
    return pl.pallas_call(
        body, out_shape=jax.ShapeDtypeStruct((L, INW), BF16), grid=(L // TR,), in_specs=in_specs,
        out_specs=_rowspec(INW), scratch_shapes=[pltpu.VMEM((NB_KV, TR, 128), F32)] * 6, name='rope_bwd',
        compiler_params=_params(('parallel',), 48))(*flat, du, pos, fr)


def _attn_mask(nbs, b):
    first = (b & (nbs - 1)) == 0
    qi = lax.broadcasted_iota(jnp.int32, (BLK, 2 * BLK), 0)
    kj = lax.broadcasted_iota(jnp.int32, (BLK, 2 * BLK), 1)
    dist = qi + BLK - kj
    return (dist >= 0) & (dist <= BLK) & ((kj >= BLK) | jnp.logical_not(first))


def _qkv_specs():
    cur = lambda col: pl.BlockSpec((BLK, KVW), lambda b: (b, col))
    prev = lambda col: pl.BlockSpec((BLK, KVW), lambda b: (jnp.maximum(b - 1, 0), col))
    return [cur(0), prev(1), cur(1), prev(2), cur(2)]


_ROWS = pl.BlockSpec((BLK, KVW), lambda b: (b, 0))
NEG = -1e30
NT_DIMS = (((1,), (1,)), ((), ()))
TN_DIMS = (((0,), (0,)), ((), ()))


def _attn_fwd(qkv, d):
    nbs = L // d // BLK

    def body(q_ref, kp_ref, kc_ref, vp_ref, vc_ref, o_ref, lse_ref):
        valid = _attn_mask(nbs, pl.program_id(0))
        heads = [slice(h * HEAD_DIM, (h + 1) * HEAD_DIM) for h in range(HEADS)]
        kcs = [jnp.concatenate([kp_ref[:, hs], kc_ref[:, hs]], axis=0) for hs in heads]
        vcs = [jnp.concatenate([vp_ref[:, hs], vc_ref[:, hs]], axis=0) for hs in heads]
        ss = [lax.dot_general(q_ref[:, hs], kc, NT_DIMS, preferred_element_type=F32) for hs, kc in zip(heads, kcs)]
        ps, ls, lses = [], [], []
        for s in ss:
            s = jnp.where(valid, s, NEG)
            m = jnp.max(s, axis=-1, keepdims=True)
            p = jnp.exp(s - m)
            l = jnp.sum(p, axis=-1, keepdims=True)
            ps.append(p.astype(BF16))
            ls.append(l)
            lses.append(jnp.broadcast_to(m + jnp.log(l), (BLK, HEAD_DIM)))
        outs = [jnp.dot(p, vc, preferred_element_type=F32) / l for p, vc, l in zip(ps, vcs, ls)]
        o_ref[...] = jnp.concatenate(outs, axis=1)
        lse_ref[...] = jnp.concatenate(lses, axis=1)

    sh = jax.ShapeDtypeStruct((L, KVW), F32)
    q2 = qkv.reshape(L, QKV_W)
    o, lse = pl.pallas_call(
        body, out_shape=(sh, sh), grid=(NBLK,), in_specs=_qkv_specs(), out_specs=(_ROWS, _ROWS),
        name='attn_fwd_d%d' % d, compiler_params=_params(('parallel',), 32))(q2, q2, q2, q2, q2)
    return o.reshape(d, L // d, KVW), lse.reshape(d, L // d, KVW)


def _attn_bwd(qkv, o, lse, do, dlse, d):
    nbs = L // d // BLK

    def body(q_ref, kp_ref, kc_ref, vp_ref, vc_ref, o_ref, lse_ref, do_ref, dlse_ref, dq_ref, dk_ref, dv_ref):
        b = pl.program_id(0)

        @pl.when(b == 0)
        def _():
            dk_ref[...] = jnp.zeros_like(dk_ref)
            dv_ref[...] = jnp.zeros_like(dv_ref)

        valid = _attn_mask(nbs, b)
        prev0 = pl.multiple_of(jnp.maximum(b - 1, 0) * BLK, BLK)
        cur0 = pl.multiple_of(b * BLK, BLK)
        heads = [slice(h * HEAD_DIM, (h + 1) * HEAD_DIM) for h in range(HEADS)]
        qs = [q_ref[:, hs] for hs in heads]
        kcs = [jnp.concatenate([kp_ref[:, hs], kc_ref[:, hs]], axis=0) for hs in heads]
        vcs = [jnp.concatenate([vp_ref[:, hs], vc_ref[:, hs]], axis=0) for hs in heads]
        dos = [do_ref[:, hs] for hs in heads]
        do_bs = [t.astype(BF16) for t in dos]
        ss = [lax.dot_general(q, kc, NT_DIMS, preferred_element_type=F32) for q, kc in zip(qs, kcs)]
        dps = [lax.dot_general(do_b, vc, NT_DIMS, preferred_element_type=F32) for do_b, vc in zip(do_bs, vcs)]
        p_bs, ds_bs = [], []
        for h, hs in enumerate(heads):
            s = jnp.where(valid, ss[h], NEG)
            p = jnp.exp(s - lse_ref[:, h * HEAD_DIM:h * HEAD_DIM + 1])
            delta = jnp.sum(dos[h] * o_ref[:, hs], axis=-1, keepdims=True)
            ds = p * (dps[h] - delta + dlse_ref[:, h * HEAD_DIM:h * HEAD_DIM + 1])
            p_bs.append(p.astype(BF16))
            ds_bs.append(ds.astype(BF16))
        dqs = [jnp.dot(ds_b, kc, preferred_element_type=F32) for ds_b, kc in zip(ds_bs, kcs)]
        dks = [lax.dot_general(ds_b, q, TN_DIMS, preferred_element_type=F32) for ds_b, q in zip(ds_bs, qs)]
        dvs = [lax.dot_general(p_b, do_b, TN_DIMS, preferred_element_type=F32) for p_b, do_b in zip(p_bs, do_bs)]
        dq_ref[...] = jnp.concatenate(dqs, axis=1) * QK_SCALE
        dkc, dvc = jnp.concatenate(dks, axis=1), jnp.concatenate(dvs, axis=1)
        dk_ref[pl.ds(prev0, BLK), :] += dkc[:BLK]
        dv_ref[pl.ds(prev0, BLK), :] += dvc[:BLK]
        dk_ref[pl.ds(cur0, BLK), :] += dkc[BLK:]
        dv_ref[pl.ds(cur0, BLK), :] += dvc[BLK:]

    sh = jax.ShapeDtypeStruct((L, KVW), F32)
    whole = pl.BlockSpec((L, KVW), lambda b: (0, 0))
    q2 = qkv.reshape(L, QKV_W)
    flat = lambda t: t.reshape(L, KVW)
    outs = pl.pallas_call(
        body, out_shape=(sh, sh, sh), grid=(NBLK,), in_specs=_qkv_specs() + [_ROWS] * 4,
        out_specs=(_ROWS, whole, whole), name='attn_bwd_d%d' % d,
        compiler_params=_params(('arbitrary',), 48))(q2, q2, q2, q2, q2, flat(o), flat(lse), flat(do), flat(dlse))
    return tuple(t.reshape(d, L // d, KVW) for t in outs)


TC = 512


def _combine_weights(l0, l1, l2):
    m = jnp.maximum(jnp.maximum(l0, l1), l2)
    e0, e1, e2 = jnp.exp(l0 - m), jnp.exp(l1 - m), jnp.exp(l2 - m)
    z = e0 + e1 + e2
    return e0 / z, e1 / z, e2 / z


def _load_groups(refs, scrs):
    out = [refs[0][0]]
    for g in (1, 2):
        _scatter_rows(scrs[g - 1], refs[g], DILATIONS[g], TC)
        out.append(_token_order(scrs[g - 1]))
    return out


def _combine_fwd(os_, lses, g):
    def body(o0, o1, o2, l0, l1, l2, g_ref, att_ref, *scrs):
        ov = _load_groups((o0, o1, o2), scrs[0:2])
        lv = _load_groups((l0, l1, l2), scrs[2:4])
        w0, w1, w2 = _combine_weights(*lv)
        a = w0 * ov[0] + w1 * ov[1] + w2 * ov[2]
        att_ref[...] = ((a * _rms(a)) * g_ref[...]).astype(BF16)

    subs = [_sub_spec(d, TC, KVW) for d in DILATIONS]
    return pl.pallas_call(
        body, out_shape=jax.ShapeDtypeStruct((L, KVW), BF16), grid=(L // TC,), in_specs=subs + subs + [_vec(KVW)],
        out_specs=pl.BlockSpec((TC, KVW), lambda i: (i, 0)),
        scratch_shapes=[pltpu.VMEM((NB_KV, TC, 128), F32)] * 4, name='combine_fwd',
        compiler_params=_params(('parallel',), 40))(*os_, *lses, g)


def _combine_bwd(dcat, os_, lses, g, head_ones):
    def body(datt_ref, o0, o1, o2, l0, l1, l2, g_ref, e_ref, do0, do1, do2, dl0, dl1, dl2, dg_ref, *scrs):
        @pl.when(pl.program_id(0) == 0)
        def _():
            dg_ref[...] = jnp.zeros_like(dg_ref)

        ov = _load_groups((o0, o1, o2), scrs[0:2])
        lv = _load_groups((l0, l1, l2), scrs[2:4])
        ws = _combine_weights(*lv)
        a = ws[0] * ov[0] + ws[1] * ov[1] + ws[2] * ov[2]
        r = _rms(a)
        n = a * r
        dv = datt_ref[...].astype(F32)
        dg_ref[...] += _colsum(dv * n)
        da = _rms_bwd(dv * g_ref[...], n, r)
        e_b = e_ref[...].astype(BF16)

        def head_sum(t):
            hi = t.astype(BF16)
            lo = (t - hi.astype(F32)).astype(BF16)
            return jnp.dot(hi, e_b, preferred_element_type=F32) + jnp.dot(lo, e_b, preferred_element_type=F32)

        dws = [head_sum(da * ov[i]) for i in range(3)]
        dbar = ws[0] * dws[0] + ws[1] * dws[1] + ws[2] * dws[2]
        scr = scrs[4]
        for i, (d, do_ref, dl_ref) in enumerate(zip(DILATIONS, (do0, do1, do2), (dl0, dl1, dl2))):
            for val, out_ref in ((ws[i] * da, do_ref), (ws[i] * (dws[i] - dbar), dl_ref)):
                if d == 1:
                    out_ref[0] = val
                else:
                    for j in range(NB_KV):
                        scr[j] = val[:, j * 128:(j + 1) * 128]
                    for rr in range(d):
                        out_ref[rr] = _gather_rows(scr, range(NB_KV), rr, d, TC)
        _fold8(pl.program_id(0) == L // TC - 1, dg_ref)

    subs = [_sub_spec(d, TC, KVW) for d in DILATIONS]
    shs = tuple(jax.ShapeDtypeStruct((d, L // d, KVW), F32) for d in DILATIONS)
    outs = pl.pallas_call(
        body, out_shape=shs + shs + (jax.ShapeDtypeStruct((8, KVW), F32),), grid=(L // TC,),
        in_specs=[pl.BlockSpec((TC, KVW), lambda i: (i, 0))] + subs + subs + [_vec(KVW),
                                                                              pl.BlockSpec((KVW, KVW), lambda i: (0, 0))],
        out_specs=tuple(subs) + tuple(subs) + (_vec8(KVW),),
        scratch_shapes=[pltpu.VMEM((NB_KV, TC, 128), F32)] * 5, name='combine_bwd',
        compiler_params=_params(('arbitrary',), 48))(dcat, *os_, *lses, g, head_ones)
    return outs[0:3], outs[3:6], outs[6]


def _ssm_disc(ar, ai, ldt):
    dt = jnp.exp(ldt)
    zr, zi = ar * dt, ai * dt
    ez = jnp.exp(zr)
    A_r, A_i = ez * jnp.cos(zi), ez * jnp.sin(zi)
    den = ar * ar + ai * ai
    xr, xi = A_r - 1.0, A_i
    cr = (xr * ar + xi * ai) / den
    ci = (xi * ar - xr * ai) / den
    return dt, zr, zi, A_r, A_i, den, cr, ci


def _ssm_pre(ar, ai, ldt, br, bi):
    def body(ar_ref, ai_ref, ldt_ref, br_ref, bi_ref, bbr_ref, bbi_ref, pwr_ref, pwi_ref):
        _, zr, zi, _, _, _, cr, ci = _ssm_disc(ar_ref[...], ai_ref[...], ldt_ref[...])
        bbr_ref[...] = cr * br_ref[...] - ci * bi_ref[...]
        bbi_ref[...] = cr * bi_ref[...] + ci * br_ref[...]
        k = (lax.broadcasted_iota(jnp.int32, (1, 8), 1) + 1).astype(F32)
        ek = jnp.exp(zr * k)
        pwr_ref[...] = ek * jnp.cos(zi * k)
        pwi_ref[...] = ek * jnp.sin(zi * k)

    s16 = jax.ShapeDtypeStruct((SSM_GN, SSM_P), F32)
    s8 = jax.ShapeDtypeStruct((SSM_GN, 8), F32)
    return pl.pallas_call(body, out_shape=(s16, s16, s8, s8), name='ssm_pre',
                          compiler_params=_params(None, 40))(ar, ai, ldt, br, bi)


def _ssm_post(ar, ai, ldt, br, bi, gar, gai, gbr, gbi, sel):
    def body(ar_ref, ai_ref, ldt_ref, br_ref, bi_ref, gar_ref, gai_ref, gbr_ref, gbi_ref, sel_ref,
             dar_ref, dai_ref, dbr_ref, dbi_ref, dldt_ref):
        a_r, a_i = ar_ref[...], ai_ref[...]
        dt, _, _, A_r, A_i, den, cr, ci = _ssm_disc(a_r, a_i, ldt_ref[...])
        b_r, b_i, g_br, g_bi = br_ref[...], bi_ref[...], gbr_ref[...], gbi_ref[...]
        gcr = jnp.sum(g_br * b_r + g_bi * b_i, axis=-1, keepdims=True)
        gci = jnp.sum(g_bi * b_r - g_br * b_i, axis=-1, keepdims=True)
        dbr_ref[...] = g_br * cr + g_bi * ci
        dbi_ref[...] = g_bi * cr - g_br * ci
        g_ar = gar_ref[...] + (gcr * a_r - gci * a_i) / den
        g_ai = gai_ref[...] + (gcr * a_i + gci * a_r) / den
        qr = (cr * a_r + ci * a_i) / den
        qi = (ci * a_r - cr * a_i) / den
        glr = -(gcr * qr + gci * qi)
        gli = -(gci * qr - gcr * qi)
        gzr = g_ar * A_r + g_ai * A_i
        gzi = g_ai * A_r - g_ar * A_i
        dar_ref[...] = glr + gzr * dt
        dai_ref[...] = gli + gzi * dt
        gdt = (gzr * a_r + gzi * a_i) * dt
        dldt_ref[...] = jnp.dot(sel_ref[...], jnp.broadcast_to(gdt, (SSM_GN, 128)),
                                preferred_element_type=F32, precision=HI)

    s1 = jax.ShapeDtypeStruct((SSM_GN, 1), F32)
    s16 = jax.ShapeDtypeStruct((SSM_GN, SSM_P), F32)
    return pl.pallas_call(body, out_shape=(s1, s1, s16, s16, jax.ShapeDtypeStruct((SSM_G, 128), F32)),
                          name='ssm_post', compiler_params=_params(None, 48))(
                              ar, ai, ldt, br, bi, gar, gai, gbr, gbi, sel)


SCAN_CH = 8


def _scan_fwd_tiles(s_ref, pw, carry):
    pwr, pwi = pw[:, :CL_S], pw[:, CL_S:]
    row = lax.broadcasted_iota(jnp.int32, (8, CL_S), 0)
    steps = [(k, jnp.where(row >= k, pwr[k - 1:k], 0.0), jnp.where(row >= k, pwi[k - 1:k], 0.0)) for k in (1, 2, 4)]
    rows = 8 * SCAN_CH

    def chunk(i, c):
        cr, ci = c
        r0 = pl.multiple_of(i * rows, rows)
        xr = s_ref[pl.ds(r0, rows), 0:CL_S].reshape(SCAN_CH, 8, CL_S)
        xi = s_ref[pl.ds(r0, rows), CL_S:2 * CL_S].reshape(SCAN_CH, 8, CL_S)
        for k, pr, pi in steps:
            sr, si = pltpu.roll(xr, k, 1), pltpu.roll(xi, k, 1)
            xr, xi = xr + pr * sr - pi * si, xi + pr * si + pi * sr
        for j in range(SCAN_CH):
            tr = xr[j] + pwr * cr - pwi * ci
            ti = xi[j] + pwr * ci + pwi * cr
            s_ref[pl.ds(r0 + 8 * j, 8), 0:CL_S] = tr
            s_ref[pl.ds(r0 + 8 * j, 8), CL_S:2 * CL_S] = ti
            cr, ci = tr[7:8], ti[7:8]
        return cr, ci

    return lax.fori_loop(0, T_SCAN // rows, chunk, (carry[:, :CL_S], carry[:, CL_S:]))


def _scan_bwd_tiles(l_ref, pw, carry):
    pwr, pwi = pw[:, :CL_S], pw[:, CL_S:]
    rpr = jnp.concatenate([pwr[7 - r:8 - r] for r in range(8)], axis=0)
    rpi = jnp.concatenate([pwi[7 - r:8 - r] for r in range(8)], axis=0)
    row = lax.broadcasted_iota(jnp.int32, (8, CL_S), 0)
    steps = [(k, jnp.where(row < 8 - k, pwr[k - 1:k], 0.0), jnp.where(row < 8 - k, pwi[k - 1:k], 0.0))
             for k in (1, 2, 4)]
    rows = 8 * SCAN_CH
    nc = T_SCAN // rows

    def chunk(i, c):
        cr, ci = c
        r0 = pl.multiple_of((nc - 1 - i) * rows, rows)
        xr = l_ref[pl.ds(r0, rows), 0:CL_S].reshape(SCAN_CH, 8, CL_S)
        xi = l_ref[pl.ds(r0, rows), CL_S:2 * CL_S].reshape(SCAN_CH, 8, CL_S)
        for k, pr, pi in steps:
            sr, si = pltpu.roll(xr, 8 - k, 1), pltpu.roll(xi, 8 - k, 1)
            xr, xi = xr + pr * sr + pi * si, xi + pr * si - pi * sr
        for j in reversed(range(SCAN_CH)):
            tr = xr[j] + rpr * cr + rpi * ci
            ti = xi[j] + rpr * ci - rpi * cr
            l_ref[pl.ds(r0 + 8 * j, 8), 0:CL_S] = tr
            l_ref[pl.ds(r0 + 8 * j, 8), CL_S:2 * CL_S] = ti
            cr, ci = tr[0:1], ti[0:1]
        return cr, ci

    return lax.fori_loop(0, nc, chunk, (carry[:, :CL_S], carry[:, CL_S:]))


NT_SCAN = L // T_SCAN


def _cl_spec(r, c):
    return pl.BlockSpec((None, r, c), lambda c_, t: (c_, 0, 0))


def _ssm_fwd(u, bm, cm, pw, dvec, u_off=0):
    def body(u_ref, bm_ref, cm_ref, pw_ref, d_ref, y_ref, bnd_ref, s_ref, carry_ref):
        @pl.when(pl.program_id(1) == 0)
        def _():
            carry_ref[...] = jnp.zeros_like(carry_ref)

        bnd_ref[...] = carry_ref[...]
        uv = u_ref[...]
        s_ref[...] = jnp.dot(uv.astype(BF16), bm_ref[...], preferred_element_type=F32)
        cr, ci = _scan_fwd_tiles(s_ref, pw_ref[...], carry_ref[...])
        carry_ref[...] = jnp.concatenate([cr, ci], axis=1)
        y_ref[...] = jnp.dot(s_ref[...].astype(BF16), cm_ref[...], preferred_element_type=F32) + d_ref[...] * uv

    return pl.pallas_call(
        body,
        out_shape=(jax.ShapeDtypeStruct((L, SSMW), F32), jax.ShapeDtypeStruct((N_CL, NT_SCAN, 1, 2 * CL_S), F32),
                   jax.ShapeDtypeStruct((L, N_CL * 2 * CL_S), F32)),
        grid=(N_CL, NT_SCAN),
        in_specs=[pl.BlockSpec((T_SCAN, CL_U), lambda c, t: (t, c + u_off)),
                  _cl_spec(CL_U, 2 * CL_S), _cl_spec(2 * CL_S, CL_U),
                  pl.BlockSpec((None, 8, 2 * CL_S), lambda c, t: (c, 0, 0)),
                  pl.BlockSpec((1, CL_U), lambda c, t: (0, c))],
        out_specs=(pl.BlockSpec((T_SCAN, CL_U), lambda c, t: (t, c)),
                   pl.BlockSpec((None, None, 1, 2 * CL_S), lambda c, t: (c, t, 0, 0)),
                   pl.BlockSpec((T_SCAN, 2 * CL_S), lambda c, t: (t, c))),
        scratch_shapes=[pltpu.VMEM((1, 2 * CL_S), F32)],
        name='ssm_fwd', compiler_params=_params(('arbitrary', 'arbitrary'), 40))(u, bm, cm, pw, dvec)


def _ssm_bwd(u, dy, states, bmt, cmt, pw, dvec, bnd, u_off=0):
    rev = lambda t: NT_SCAN - 1 - t

    def body(u_ref, dy_ref, s_ref, bmt_ref, cmt_ref, pw_ref, d_ref, bnd_ref,
             du_ref, dbm_ref, dcm_ref, da_ref, dd_ref, l_ref, carry_ref):
        @pl.when(pl.program_id(1) == 0)
        def _():
            carry_ref[...] = jnp.zeros_like(carry_ref)
            dbm_ref[...] = jnp.zeros_like(dbm_ref)
            dcm_ref[...] = jnp.zeros_like(dcm_ref)
            da_ref[...] = jnp.zeros_like(da_ref)
            dd_ref[...] = jnp.zeros_like(dd_ref)

        uv, dyv, pw = u_ref[...], dy_ref[...], pw_ref[...]
        dy_b = dyv.astype(BF16)
        entry = bnd_ref[...]
        l_ref[...] = jnp.dot(dy_b, cmt_ref[...], preferred_element_type=F32)
        cr, ci = _scan_bwd_tiles(l_ref, pw, carry_ref[...])
        carry_ref[...] = jnp.concatenate([cr, ci], axis=1)
        sv, lv = s_ref[...], l_ref[...]
        lv_b = lv.astype(BF16)
        du_ref[...] = dyv * d_ref[...] + jnp.dot(lv_b, bmt_ref[...], preferred_element_type=F32)
        dbm_ref[...] += lax.dot_general(uv.astype(BF16), lv_b, TN_DIMS, preferred_element_type=F32)
        dcm_ref[...] += lax.dot_general(sv.astype(BF16), dy_b, TN_DIMS, preferred_element_type=F32)
        dd_ref[...] += _colsum(dyv * uv)
        row = lax.broadcasted_iota(jnp.int32, (T_SCAN, 2 * CL_S), 0)
        sp = jnp.where(row == 0, entry, pltpu.roll(sv, 1, 0))
        spr, spi = sp[:, :CL_S], sp[:, CL_S:]
        lr, li = lv[:, :CL_S], lv[:, CL_S:]
        da_ref[:, 0:CL_S] += _colsum(lr * spr + li * spi)
        da_ref[:, CL_S:2 * CL_S] += _colsum(li * spr - lr * spi)
        _fold8(pl.program_id(1) == NT_SCAN - 1, da_ref, dd_ref)

    return pl.pallas_call(
        body,
        out_shape=(jax.ShapeDtypeStruct((L, SSMW), F32), jax.ShapeDtypeStruct((N_CL, CL_U, 2 * CL_S), F32),
                   jax.ShapeDtypeStruct((N_CL, 2 * CL_S, CL_U), F32), jax.ShapeDtypeStruct((N_CL, 8, 2 * CL_S), F32),
                   jax.ShapeDtypeStruct((8, SSMW), F32)),
        grid=(N_CL, NT_SCAN),
        in_specs=[pl.BlockSpec((T_SCAN, CL_U), lambda c, t: (rev(t), c + u_off)),
                  pl.BlockSpec((T_SCAN, CL_U), lambda c, t: (rev(t), c)),
                  pl.BlockSpec((T_SCAN, 2 * CL_S), lambda c, t: (rev(t), c)),
                  pl.BlockSpec((None, 2 * CL_S, CL_U), lambda c, t: (c, 0, 0)),
                  pl.BlockSpec((None, CL_U, 2 * CL_S), lambda c, t: (c, 0, 0)),
                  pl.BlockSpec((None, 8, 2 * CL_S), lambda c, t: (c, 0, 0)),
                  pl.BlockSpec((1, CL_U), lambda c, t: (0, c)),
                  pl.BlockSpec((None, None, 1, 2 * CL_S), lambda c, t: (c, rev(t), 0, 0))],
        out_specs=(pl.BlockSpec((T_SCAN, CL_U), lambda c, t: (rev(t), c)),
                   pl.BlockSpec((None, CL_U, 2 * CL_S), lambda c, t: (c, 0, 0)),
                   pl.BlockSpec((None, 2 * CL_S, CL_U), lambda c, t: (c, 0, 0)),
                   pl.BlockSpec((None, 8, 2 * CL_S), lambda c, t: (c, 0, 0)),
                   pl.BlockSpec((8, CL_U), lambda c, t: (0, c))),
        scratch_shapes=[pltpu.VMEM((T_SCAN, 2 * CL_S), F32), pltpu.VMEM((1, 2 * CL_S), F32)],
        name='ssm_bwd', compiler_params=_params(('arbitrary', 'arbitrary'), 48))(u, dy, states, bmt, cmt, pw, dvec, bnd)


GELU_C = math.sqrt(2.0 / math.pi)
GELU_K = 0.044715


def _gelu_parts(x):
    t = jnp.tanh(GELU_C * (x + GELU_K * (x * x * x)))
    return x * (0.5 * (1.0 + t)), t


def _glu_fwd(ypre, wglu, bglu, gs):
    def body(y_ref, w_ref, b_ref, g_ref, o_ref):
        yg, _ = _gelu_parts(y_ref[...])
        z = jnp.dot(yg.astype(BF16), w_ref[...], preferred_element_type=F32) + b_ref[...]
        s = yg * jax.nn.sigmoid(z)
        o_ref[...] = ((s * _rms(s)) * g_ref[...]).astype(BF16)

    return pl.pallas_call(
        body, out_shape=jax.ShapeDtypeStruct((L, SSMW), BF16), grid=(L // TR,),
        in_specs=[_rowspec(SSMW), pl.BlockSpec((SSMW, SSMW), lambda i: (0, 0)), _vec(SSMW), _vec(SSMW)],
        out_specs=_rowspec(SSMW), name='glu_fwd', compiler_params=_params(('parallel',), 32))(ypre, wglu, bglu, gs)


def _glu_bwd(ypre, dsn, wglu, bglu, gs):
    def body(y_ref, d_ref, w_ref, b_ref, g_ref, dy_ref, dw_ref, db_ref, dg_ref):
        @pl.when(pl.program_id(0) == 0)
        def _():
            dw_ref[...] = jnp.zeros_like(dw_ref)
            db_ref[...] = jnp.zeros_like(db_ref)
            dg_ref[...] = jnp.zeros_like(dg_ref)

        xv = y_ref[...]
        yg, t = _gelu_parts(xv)
        yg_b = yg.astype(BF16)
        z = jnp.dot(yg_b, w_ref[...], preferred_element_type=F32) + b_ref[...]
        sg = jax.nn.sigmoid(z)
        s = yg * sg
        r = _rms(s)
        n = s * r
        dv = d_ref[:, d_ref.shape[1] - SSMW:].astype(F32)
        dg_ref[...] += _colsum(dv * n)
        ds = _rms_bwd(dv * g_ref[...], n, r)
        dz =(ds * yg) * (sg * (1.0 - sg))
        dz_b = dz.astype(BF16)
        db_ref[...] += _colsum(dz)
        dw_ref[...] += lax.dot_general(yg_b, dz_b, TN_DIMS, preferred_element_type=F32)
        dyg = ds * sg + lax.dot_general(dz_b, w_ref[...], NT_DIMS, preferred_element_type=F32)
        dgelu = 0.5 * (1.0 + t) + (0.5 * xv) * (1.0 - t * t) * (GELU_C * (1.0 + 3.0 * GELU_K * (xv * xv)))
        dy_ref[...] = dyg * dgelu
        _fold8(pl.program_id(0) == L // TR - 1, db_ref, dg_ref)

    vs = jax.ShapeDtypeStruct((8, SSMW), F32)
    return pl.pallas_call(
        body, out_shape=(jax.ShapeDtypeStruct((L, SSMW), F32), jax.ShapeDtypeStruct((SSMW, SSMW), F32), vs, vs),
        grid=(L // TR,),
        in_specs=[_rowspec(SSMW), _rowspec(dsn.shape[1]), pl.BlockSpec((SSMW, SSMW), lambda i: (0, 0)), _vec(SSMW),
                  _vec(SSMW)],
        out_specs=(_rowspec(SSMW), pl.BlockSpec((SSMW, SSMW), lambda i: (0, 0)), _vec8(SSMW), _vec8(SSMW)),
        name='glu_bwd', compiler_params=_params(('arbitrary',), 40))(ypre, dsn, wglu, bglu, gs)


def _me():
    return lax.axis_index('x'), lax.axis_index('y'), lax.axis_index('c')


def _my_index():
    return 4 * lax.axis_index('x') + 2 * lax.axis_index('y') + lax.axis_index('c')


def _peer(k):
    x, y, c = _me()
    px = 1 - x if k & 4 else x
    py = 1 - y if k & 2 else y
    pc = 1 - c if k & 1 else c
    return (px, py, pc), 4 * px + 2 * py + pc


def _mod_exchange(c_row, w_ada, b_ada8, deps=()):
    cw = NMOD * D // N_DEV

    def body(c_ref, w_ref, b_ref, *rest):
        call_ref, mod_ref, part_ref, send_sems, recv_sems = rest[len(deps):]
        x, y, c = _me()
        me = 4 * x + 2 * y + c
        call_ref[me] = c_ref[0]
        sends = []
        for k in range(1, N_DEV):
            peer, _ = _peer(k)
            cp = pltpu.make_async_remote_copy(src_ref=c_ref.at[0], dst_ref=call_ref.at[me], send_sem=send_sems.at[0, k - 1],
                                              recv_sem=recv_sems.at[0, k - 1], device_id=peer, device_id_type=MESH)
            cp.start()
            sends.append(cp)
        for k in range(1, N_DEV):
            peer, pidx = _peer(k)
            pltpu.make_async_remote_copy(src_ref=c_ref.at[0], dst_ref=call_ref.at[pidx], send_sem=send_sems.at[0, k - 1],
                                         recv_sem=recv_sems.at[0, k - 1], device_id=peer, device_id_type=MESH).wait_recv()
        for cp in sends:
            cp.wait_send()
        cv = call_ref[...].reshape(N_DEV, D)
        part = jnp.dot(cv * jax.nn.sigmoid(cv), w_ref[...], preferred_element_type=F32, precision=HI)
        part_ref[...] = part.reshape(N_DEV, 1, cw)
        mod_ref[me] = part_ref[me]
        sends = []
        for k in range(1, N_DEV):
            peer, pidx = _peer(k)
            cp = pltpu.make_async_remote_copy(src_ref=part_ref.at[pidx], dst_ref=mod_ref.at[me], send_sem=send_sems.at[1, k - 1],
                                              recv_sem=recv_sems.at[1, k - 1], device_id=peer, device_id_type=MESH)
            cp.start()
            sends.append(cp)
        for k in range(1, N_DEV):
            peer, pidx = _peer(k)
            pltpu.make_async_remote_copy(src_ref=part_ref.at[pidx], dst_ref=mod_ref.at[pidx], send_sem=send_sems.at[1, k - 1],
                                         recv_sem=recv_sems.at[1, k - 1], device_id=peer, device_id_type=MESH).wait_recv()
        for cp in sends:
            cp.wait_send()
        mod_ref[...] = mod_ref[...] + b_ref[...]

    vm = pl.BlockSpec(memory_space=pltpu.VMEM)
    return pl.pallas_call(
        body, out_shape=(jax.ShapeDtypeStruct((N_DEV, 1, D), F32), jax.ShapeDtypeStruct((N_DEV, 1, cw), F32)),
        in_specs=[vm, vm, vm] + [pl.BlockSpec(memory_space=pl.ANY)] * len(deps), out_specs=(vm, vm),
        scratch_shapes=[pltpu.VMEM((N_DEV, 1, cw), F32), pltpu.SemaphoreType.DMA((2, N_DEV - 1)),
                        pltpu.SemaphoreType.DMA((2, N_DEV - 1))],
        name='mod_exchange', compiler_params=_params(None, 48))(c_row, w_ada, b_ada8, *deps)


HBM_SPEC = pl.BlockSpec(memory_space=pltpu.HBM)
SEM_SPEC = pl.BlockSpec(memory_space=pltpu.SEMAPHORE)
DATAFLOW = pltpu.SideEffectType.DATAFLOW_SIDE_EFFECTING


def _push_start(src, scatter, after, name):
    land = lax.empty(src.shape if scatter else (N_DEV,) + src.shape, src.dtype)

    def body(src_ref, land_ref, after_ref, send_sem, recv_sem, land_thru, token):
        x, y, c = _me()
        me = 4 * x + 2 * y + c
        for k in range(1, N_DEV):
            peer, pidx = _peer(k)
            pltpu.make_async_remote_copy(src_ref=src_ref.at[pidx] if scatter else src_ref, dst_ref=land_ref.at[me],
                                         send_sem=send_sem, recv_sem=recv_sem, device_id=peer,
                                         device_id_type=MESH).start()
        token[...] = jnp.zeros_like(token)

    own = src
    src = pltpu.with_memory_space_constraint(src, pltpu.HBM)
    send_sem, recv_sem, land_thru, token = pl.pallas_call(
        body, name=name,
        out_shape=(pltpu.SemaphoreType.DMA(()), pltpu.SemaphoreType.DMA(()),
                   pltpu.HBM(land.shape, land.dtype), jax.ShapeDtypeStruct((8, 128), F32)),
        in_specs=(HBM_SPEC, HBM_SPEC, pl.BlockSpec(memory_space=pl.ANY)),
        out_specs=(SEM_SPEC, SEM_SPEC, HBM_SPEC, pl.BlockSpec(memory_space=pltpu.VMEM)),
        input_output_aliases={1: 2}, compiler_params=pltpu.CompilerParams(has_side_effects=DATAFLOW),
    )(src, pltpu.with_memory_space_constraint(land, pltpu.HBM), after)
    return send_sem, recv_sem, src, land_thru, token, own


def _push_wait(handle, after, name, merge=True):
    send_sem, recv_sem, src, land_thru, _, own = handle
    after = tuple(after) if isinstance(after, (tuple, list)) else (after,)

    def body(src_ref, land_ref, send_sem, recv_sem, *rest):
        seven = land_ref.at[pl.ds(0, N_DEV - 1)]
        cp = pltpu.make_async_remote_copy(src_ref=seven, dst_ref=seven, send_sem=send_sem, recv_sem=recv_sem,
                                          device_id=_me(), device_id_type=MESH)
        cp.wait_send()
        cp.wait_recv()

    landed = pl.pallas_call(
        body, name=name, out_shape=pltpu.HBM(land_thru.shape, land_thru.dtype),
        in_specs=(HBM_SPEC, HBM_SPEC, SEM_SPEC, SEM_SPEC) + (pl.BlockSpec(memory_space=pl.ANY),) * len(after),
        out_specs=HBM_SPEC, input_output_aliases={1: 0},
        compiler_params=pltpu.CompilerParams(has_side_effects=DATAFLOW),
    )(src, land_thru, send_sem, recv_sem, *after)
    if not merge:
        return landed, own
    return lax.dynamic_update_index_in_dim(landed, own, _my_index(), 0)


def _adam(w, g, m, v):
    m2 = B1 * m + (1.0 - B1) * g
    v2 = B2 * v + (1.0 - B2) * jnp.square(g)
    m_hat = m2 / (1.0 - B1 ** STEP)
    v_hat = v2 / (1.0 - B2 ** STEP)
    delta = -LR * (m_hat / (jnp.sqrt(v_hat) + AEPS) + WD * w)
    return delta, m2, v2


def _small_update(gp, wp, mp, vp):
    def body(g_ref, w_ref, m_ref, v_ref, all_ref, go_ref, d_ref, mo_ref, vo_ref, send_sems, recv_sems):
        x, y, c = _me()
        me = 4 * x + 2 * y + c
        all_ref[me] = g_ref[...]
        sends = []
        for k in range(1, N_DEV):
            peer, _ = _peer(k)
            cp = pltpu.make_async_remote_copy(src_ref=g_ref, dst_ref=all_ref.at[me], send_sem=send_sems.at[k - 1],
                                              recv_sem=recv_sems.at[k - 1], device_id=peer, device_id_type=MESH)
            cp.start()
            sends.append(cp)
        for k in range(1, N_DEV):
            peer, pidx = _peer(k)
            pltpu.make_async_remote_copy(src_ref=g_ref, dst_ref=all_ref.at[pidx], send_sem=send_sems.at[k - 1],
                                         recv_sem=recv_sems.at[k - 1], device_id=peer, device_id_type=MESH).wait_recv()
        for cp in sends:
            cp.wait_send()
        g = all_ref[0]
        for d in range(1, N_DEV):
            g = g + all_ref[d]
        delta, m2, v2 = _adam(w_ref[...], g, m_ref[...], v_ref[...])
        go_ref[...] = g
        d_ref[...] = delta
        mo_ref[...] = m2
        vo_ref[...] = v2

    vm = pl.BlockSpec(memory_space=pltpu.VMEM)
    vs = jax.ShapeDtypeStruct(gp.shape, F32)
    return pl.pallas_call(
        body, out_shape=(jax.ShapeDtypeStruct((N_DEV,) + gp.shape, F32), vs, vs, vs, vs), in_specs=[vm] * 4,
        out_specs=(vm,) * 5,
        scratch_shapes=[pltpu.SemaphoreType.DMA((N_DEV - 1,)), pltpu.SemaphoreType.DMA((N_DEV - 1,))],
        name='small_update', compiler_params=_params(None, 48))(gp, wp, mp, vp)


def _small_sum_update(parts, wp, mp, vp):
    def body(p_ref, w_ref, m_ref, v_ref, go_ref, d_ref, mo_ref, vo_ref):
        g = p_ref[0]
        for d in range(1, N_DEV):
            g = g + p_ref[d]
        delta, m2, v2 = _adam(w_ref[...], g, m_ref[...], v_ref[...])
        go_ref[...] = g
        d_ref[...] = delta
        mo_ref[...] = m2
        vo_ref[...] = v2

    vm = pl.BlockSpec(memory_space=pltpu.VMEM)
    vs = jax.ShapeDtypeStruct(wp.shape, F32)
    return pl.pallas_call(body, out_shape=(vs, vs, vs, vs), in_specs=[vm] * 4, out_specs=(vm,) * 4,
                          name='small_sum_update', compiler_params=_params(None, 48))(parts, wp, mp, vp)


def _big_update(parts, src, w, m, v, name):
    _, R, C = parts.shape
    tr = (128 if C >= 2048 else 256) if R % 256 == 0 else (R // 2 if C >= 2048 and R % 32 == 0 else R)

    def body(me_ref, p_ref, own_ref, w_ref, m_ref, v_ref, g_ref, d_ref, mo_ref, vo_ref):
        me = me_ref[0]
        mine = own_ref[...].astype(F32)
        g = jnp.where(me == 0, mine, p_ref[0].astype(F32))
        for d in range(1, N_DEV):
            g = g + jnp.where(me == d, mine, p_ref[d].astype(F32))
        delta, m2, v2 = _adam(w_ref[...], g, m_ref[...], v_ref[...])
        g_ref[...] = g
        d_ref[...] = delta
        mo_ref[...] = m2
        vo_ref[...] = v2

    blk = pl.BlockSpec((tr, C), lambda i, me: (i, 0))
    sh = jax.ShapeDtypeStruct((R, C), F32)
    grid_spec = pltpu.PrefetchScalarGridSpec(
        num_scalar_prefetch=1, grid=(R // tr,),
        in_specs=[pl.BlockSpec((N_DEV, tr, C), lambda i, me: (0, i, 0)),
                  pl.BlockSpec((None, tr, C), lambda i, me: (me[0], i, 0)), blk, blk, blk],
        out_specs=(blk,) * 4)
    return pl.pallas_call(
        body, out_shape=(sh, sh, sh, sh), grid_spec=grid_spec, name=name,
        compiler_params=_params(('parallel',), 48))(_my_index().reshape(1), parts, src, w, m, v)


def _ada_update(c_all, dmod_cols, w, m, v):
    C = w.shape[1]
    tr = 256

    def body(c_ref, dm_ref, w_ref, m_ref, v_ref, g_ref, d_ref, mo_ref, vo_ref):
        cv = c_ref[...]
        s = cv * jax.nn.sigmoid(cv)
        g = lax.dot_general(s, dm_ref[...], TN_DIMS, preferred_element_type=F32, precision=HI)
        delta, m2, v2 = _adam(w_ref[...], g, m_ref[...], v_ref[...])
        g_ref[...] = g
        d_ref[...] = delta
        mo_ref[...] = m2
        vo_ref[...] = v2

    blk = pl.BlockSpec((tr, C), lambda i: (i, 0))
    sh = jax.ShapeDtypeStruct((D, C), F32)
    return pl.pallas_call(
        body, out_shape=(sh, sh, sh, sh), grid=(D // tr,),
        in_specs=[pl.BlockSpec((N_DEV, tr), lambda i: (0, i)), pl.BlockSpec((N_DEV, C), lambda i: (0, 0)), blk, blk, blk],
        out_specs=(blk,) * 4, name='ada_update', compiler_params=_params(('parallel',), 48))(c_all, dmod_cols, w, m, v)


def _rows_to_cluster_lanes(t):
    k = t.shape[1]
    return t.reshape(N_CL, CL_S, k).transpose(0, 2, 1)


def _blockdiag_in(t):
    t = t.reshape(N_CL, CL_G, SSM_N, SSM_P).transpose(0, 1, 3, 2)
    eye = jnp.eye(CL_G, dtype=t.dtype)
    t = t[:, :, :, None, :] * eye[None, :, None, :, None]
    return t.reshape(N_CL, CL_U, CL_S)


def _blockdiag_extract(t):
    t = t.reshape(N_CL, CL_G, SSM_P, CL_G, SSM_N)
    eye = jnp.eye(CL_G, dtype=t.dtype)
    t = jnp.sum(t * eye[None, :, None, :, None], axis=3)
    return t.transpose(0, 1, 3, 2).reshape(SSM_GN, SSM_P)


def _c_to_rows(t):
    return t.transpose(0, 2, 1).reshape(SSM_GN, SSM_P)


def _rows_to_c(t):
    return t.reshape(SSM_G, SSM_N, SSM_P).transpose(0, 2, 1)


def _ssm_prep(sp):
    rows = lambda n: sp[n].reshape(SSM_GN, 1)
    a_re, a_im = rows('ssm_a_re'), rows('ssm_a_im')
    ldt = jnp.repeat(sp['ssm_log_dt'].reshape(SSM_G, 1), SSM_N, axis=0)
    b_re, b_im = sp['ssm_b_re'].reshape(SSM_GN, SSM_P), sp['ssm_b_im'].reshape(SSM_GN, SSM_P)
    c_re, c_im = _c_to_rows(sp['ssm_c_re'].reshape(SSM_G, SSM_P, SSM_N)), _c_to_rows(sp['ssm_c_im'].reshape(SSM_G, SSM_P, SSM_N))
    bbr, bbi, pwr, pwi = _ssm_pre(a_re, a_im, ldt, b_re, b_im)
    bm = jnp.concatenate([_blockdiag_in(bbr), _blockdiag_in(bbi)], axis=2).astype(BF16)
    cmt = jnp.concatenate([_blockdiag_in(c_re), -_blockdiag_in(c_im)], axis=2).astype(BF16)
    bmt, cm = bm.transpose(0, 2, 1), cmt.transpose(0, 2, 1)
    pw = jnp.concatenate([_rows_to_cluster_lanes(pwr), _rows_to_cluster_lanes(pwi)], axis=2)
    return a_re, a_im, ldt, b_re, b_im, bm, cm, bmt, cmt, pw


def _tied(v, deps):
    for t in deps:
        v = v + t[0, 0]
    return v


def _local_step(x, pos, mod, tgt, sp, prep, get_w, emit, emit_small, emit_late):
    sh1, sc1, gt1, sh2, sc2, gt2 = (mod[i:i + 1] for i in range(NMOD))
    vec = lambda n: sp[n].reshape(1, -1)
    a_re, a_im, ldt, b_re, b_im, bm_b, cm_b, bmt_b, cmt_b, pw = prep
    dvec = vec('ssm_d')

    h1 = _prenorm_fwd(x, vec('g_pre_mix'), sc1, sh1)
    w_in = get_w('w_in', (h1, bm_b, cm_b, pw, bmt_b, cmt_b))
    proj = _mm(h1, w_in, mode='nt', name='mm_in', tm=512, tn=INW)
    fr1 =ROPE_THETA ** (-jnp.arange(0, ROT_DIM, 2, dtype=F32) / ROT_DIM)
    lane = jnp.arange(128) % HEAD_DIM
    fr = jnp.where(lane < ROT_DIM, fr1[lane % (ROT_DIM // 2)], 0.0).reshape(1, 128).astype(F32)
    u_off = (ROPE_W + KVW) // CL_U
    qkvs = _rope_fwd(proj, pos, fr)
    fwd = [_attn_fwd(qkvs[g], d) for g, d in enumerate(DILATIONS)]
    os_, lses = [t[0] for t in fwd], [t[1] for t in fwd]
    att = _combine_fwd(os_, lses, vec('g_attn_out'))

    ypre, bnd, states = _ssm_fwd(proj, bm_b, cm_b, pw, dvec, u_off)
    w_glu = get_w('w_glu', ypre)
    ssm_n = _glu_fwd(ypre, w_glu, vec('b_glu'), vec('g_ssm_out'))

    cat = jnp.concatenate([att, ssm_n], axis=1)
    w_out = get_w('w_out', cat)
    mix = _mm(cat, w_out, mode='nn', name='mm_out', tk=1280)
    x1, h2 = _postmix_fwd(x, mix, vec('g_post_mix'), gt1, vec('g_pre_mlp'), sc2, sh2)
    w_mi = get_w('w_mlp_in', h2)
    a_pre, r_act = _mm(h2, w_mi, mode='nn', name='mm_mlp_in', epilogue='relu2', b_sharded=True)
    w_mo = get_w('w_mlp_out', a_pre)
    y = _mm(r_act, w_mo, mode='nn', name='mm_mlp_out')
    dx2, dy, loss, dgt2, dg_post_mlp = _final_fwd_bwd(x1, y, tgt, vec('g_post_mlp'), gt2)
    dgt2, dg_post_mlp = dgt2[:1], dg_post_mlp[:1]

    da = _mm(dy, w_mo, mode='nt', name='mm_d_act', out_dtype=BF16, epilogue='drelu2', extra=a_pre)
    dep = emit('w_mlp_out', _mm(r_act, dy, mode='tn', name='mm_dw_mlp_out', out_dtype=BF16))
    dh2 = _mm(da, w_mi, mode='nt', name='mm_dh2', tk=2048, out_dtype=BF16, b_sharded=True, deps=dep)
    dep = emit('w_mlp_in', _mm(h2, da, mode='tn', name='mm_dw_mlp_in', out_dtype=BF16, out_sharded=True))
    dx1, dmix, dsc2, dsh2, dg_pre_mlp, dgt1, dg_post_mix = _postmix_bwd(
        dx2, dh2, x1, mix, vec('g_post_mix'), gt1, vec('g_pre_mlp'), sc2)
    dsc2, dsh2, dg_pre_mlp, dgt1, dg_post_mix = (t[:1] for t in (dsc2, dsh2, dg_pre_mlp, dgt1, dg_post_mix))
    dcat = _mm(dmix, w_out, mode='nt', name='mm_dcat', tn=1280, out_dtype=BF16, deps=dep)
    dep = emit('w_out', _mm(cat, dmix, mode='tn', name='mm_dw_out', out_dtype=BF16, tm=640))
    dypre, g_w_glu, g_b_glu, g_g_ssm = _glu_bwd(ypre, dcat, w_glu, _tied(vec('b_glu'), dep), vec('g_ssm_out'))
    g_b_glu, g_g_ssm = g_b_glu[:1], g_g_ssm[:1]
    dep = dep + emit('w_glu', g_w_glu.astype(BF16))
    du, dbm, dcm, dA, dD = _ssm_bwd(proj, dypre, states, bmt_b, cmt_b, pw, dvec, bnd, u_off)
    dD = dD[:1]
    gbr, gbi = _blockdiag_extract(dbm[:, :, :CL_S]), _blockdiag_extract(dbm[:, :, CL_S:])
    dcmt = dcm.transpose(0, 2, 1)
    g_c_re = _rows_to_c(_blockdiag_extract(dcmt[:, :, :CL_S]))
    g_c_im = _rows_to_c(-_blockdiag_extract(dcmt[:, :, CL_S:]))
    gar = dA[:, 0, :CL_S].reshape(SSM_GN, 1)
    gai = dA[:, 0, CL_S:].reshape(SSM_GN, 1)
    sel = (jnp.arange(SSM_GN)[None, :] // SSM_N == jnp.arange(SSM_G)[:, None]).astype(F32)
    g_a_re, g_a_im, g_b_re, g_b_im, g_ldt = _ssm_post(a_re, a_im, ldt, b_re, b_im, gar, gai, gbr, gbi, sel)

    head_ones = (jnp.arange(KVW)[:, None] // HEAD_DIM == jnp.arange(KVW)[None, :] // HEAD_DIM).astype(F32)
    dos, dlses, g_g_attn = _combine_bwd(dcat, os_, lses, vec('g_attn_out'), head_ones)
    g_g_attn = g_g_attn[:1]
    dep_small = emit_small({
        'g_post_mix': dg_post_mix, 'ssm_a_re': g_a_re, 'ssm_a_im': g_a_im, 'ssm_log_dt': g_ldt[:, 0],
        'ssm_b_re': g_b_re, 'ssm_b_im': g_b_im, 'ssm_c_re': g_c_re, 'ssm_c_im': g_c_im, 'ssm_d': dD, 'b_glu': g_b_glu,
        'g_attn_out': g_g_attn, 'g_ssm_out': g_g_ssm, 'g_pre_mlp': dg_pre_mlp, 'g_post_mlp': dg_post_mlp})
    dqkv = [_attn_bwd(qkvs[g], os_[g], lses[g], dos[g], dlses[g], d) for g, d in enumerate(DILATIONS)]
    dproj = _rope_bwd(dqkv, du, pos, _tied(fr, dep_small))
    dh1 = _mm(dproj, w_in, mode='nn', name='mm_dh1', tk=INW, out_dtype=BF16, deps=dep)
    grad_x, dsc1, dsh1, dg_pre_mix = _prenorm_bwd(dx1, dh1, x, vec('g_pre_mix'), sc1)
    dsc1, dsh1, dg_pre_mix = dsc1[:1], dsh1[:1], dg_pre_mix[:1]
    dmod = jnp.concatenate([dsh1, dsc1, dgt1, dsh2, dsc2, dgt2], axis=0)
    dep = emit_late({'b_ada': dmod, 'g_pre_mix': dg_pre_mix})
    emit('w_in', _mm(dproj, h1, mode='tn', name='mm_dw_in', out_dtype=BF16, tm=INW // 2, deps=dep))
    return loss[0, 0], grad_x


def _pack(d, names):
    flat = jnp.concatenate([jnp.pad(d[n].reshape(-1).astype(F32), (0, SEG[n] - SMALL_SIZES[n])) for n in names])
    return flat.reshape(-1, 128)


def _unpack(packed, names, shapes):
    out, off = {}, 0
    for n in names:
        out[n] = packed[off // 128:(off + SEG[n]) // 128].reshape(-1)[:SMALL_SIZES[n]].reshape(shapes[n])
        off += SEG[n]
    return out


def _shard_major(t, name):
    if name in ('w_in', 'w_out', 'w_mlp_in'):
        k, n = t.shape
        return t.reshape(k, N_DEV, n // N_DEV).transpose(1, 0, 2)
    k, n = t.shape
    return t.reshape(N_DEV, k // N_DEV, n)


def _from_shard_major(t, name):
    if name in ('w_in', 'w_out', 'w_mlp_in'):
        _, k, n = t.shape
        return t.transpose(1, 0, 2).reshape(k, N_DEV * n)
    _, k, n = t.shape
    return t.reshape(N_DEV * k, n)


def kernel(x, c, positions, w_ada, b_ada, g_pre_mix, g_post_mix, w_in, ssm_a_re, ssm_a_im, ssm_log_dt, ssm_b_re, ssm_b_im, ssm_c_re, ssm_c_im, ssm_d, w_glu, b_glu, g_attn_out, g_ssm_out, w_out, g_pre_mlp, g_post_mlp, w_mlp_in, w_mlp_out, loss_target, m_w_ada, m_b_ada, m_g_pre_mix, m_g_post_mix, m_w_in, m_ssm_a_re, m_ssm_a_im, m_ssm_log_dt, m_ssm_b_re, m_ssm_b_im, m_ssm_c_re, m_ssm_c_im, m_ssm_d, m_w_glu, m_b_glu, m_g_attn_out, m_g_ssm_out, m_w_out, m_g_pre_mlp, m_g_post_mlp, m_w_mlp_in, m_w_mlp_out, v_w_ada, v_b_ada, v_g_pre_mix, v_g_post_mix, v_w_in, v_ssm_a_re, v_ssm_a_im, v_ssm_log_dt, v_ssm_b_re, v_ssm_b_im, v_ssm_c_re, v_ssm_c_im, v_ssm_d, v_w_glu, v_b_glu, v_g_attn_out, v_g_ssm_out, v_w_out, v_g_pre_mlp, v_g_post_mlp, v_w_mlp_in, v_w_mlp_out):
    loc = dict(locals())
    W = {n: loc[n] for n in WEIGHTS}
    M = {n: loc['m_' + n] for n in WEIGHTS}
    V = {n: loc['v_' + n] for n in WEIGHTS}
    assert x.shape == (1, L, D) and w_in.shape == (1, D, INW // N_DEV), (x.shape, w_in.shape)

    cw = NMOD * D // N_DEV
    c_all, mod8 = _mod_exchange(c.reshape(1, 1, D), w_ada[0], b_ada.reshape(N_DEV, 1, cw))
    mod = mod8.reshape(NMOD, D)

    shard = lambda t, n: t[0].T if n == 'w_in' else t[0]
    gather, after = {}, mod8
    for n in BIG:
        gather[n] = _push_start(shard(W[n], n).astype(BF16), False, after, 'gather_start_' + n)
        after = gather[n][4]
    tokens = tuple(gather[n][4] for n in BIG)
    mod = _tied(mod, tokens)
    sp = {n: W[n][0] for n in SMALL}
    prep = _ssm_prep({**sp, 'ssm_a_re': _tied(sp['ssm_a_re'], tokens)})

    state_packs = {}

    def get_w(n, after):
        if n == 'w_mlp_out':
            tok = jnp.minimum(jnp.abs(after[0:8, 0:128].astype(F32)), 0.0)
            for key, src in (('w', W), ('m', M), ('v', V)):
                tied = {p: _tied(src[p], (tok,)) for p in SMALL}
                state_packs[key] = {'early': _pack(tied, SMALL_EARLY), 'late': _pack(tied, SMALL_LATE)}
            after = (after,) + tuple(state_packs[k][part] for k in 'wmv' for part in ('early', 'late'))
        g = _push_wait(gather[n], after, 'gather_wait_' + n)
        if n == 'w_in':
            return g.reshape(INW, D)
        return g if n == 'w_mlp_in' else _from_shard_major(g, n)

    scatter = {}

    def emit(n, g):
        src = g if n == 'w_mlp_in' else (g.reshape(N_DEV, INW // N_DEV, D) if n == 'w_in' else _shard_major(g, n))
        scatter[n] = _push_start(src, True, src, 'scatter_start_' + n)
        return (scatter[n][4],)

    small_early = []

    def emit_small(d):
        pack = _pack(d, SMALL_EARLY)
        small_early.append(_push_start(pack, False, pack, 'small_start'))
        return (small_early[0][4],)

    out_g, out_d, out_m, out_v = {}, {}, {}, {}
    shapes = {n: W[n].shape[1:] for n in SMALL}

    def put(names, packs):
        for dst, packed in zip((out_g, out_d, out_m, out_v), packs):
            dst.update(_unpack(packed, names, shapes))

    late = []

    def emit_late(d):
        rows_all, *packs = _small_update(_pack(d, SMALL_LATE), *[state_packs[k]['late'] for k in 'wmv'])
        put(SMALL_LATE, packs)
        late.append(rows_all)
        return (rows_all,)

    loss, grad_x = _local_step(x[0], positions.reshape(L, 1), mod, loss_target[0], sp, prep, get_w, emit, emit_small,
                               emit_late)
    loss = lax.psum(loss, ('x', 'y', 'c'))

    me = 4 * lax.axis_index('x') + 2 * lax.axis_index('y') + lax.axis_index('c')
    dmod_all = late[0][:, :NMOD * D // 128].reshape(N_DEV, NMOD * D)
    dmod_cols = _tied(lax.dynamic_slice_in_dim(dmod_all, me * cw, cw, axis=1), (scatter['w_in'][4],))
    out_g['w_ada'], out_d['w_ada'], out_m['w_ada'], out_v['w_ada'] = _ada_update(
        c_all.reshape(N_DEV, D), dmod_cols, w_ada[0], m_w_ada[0], v_w_ada[0])

    parts = _push_wait(small_early[0], out_v['w_ada'], 'small_wait')
    packs = _small_sum_update(parts, *[state_packs[k]['early'] for k in 'wmv'])
    put(SMALL_EARLY, packs)

    after = packs[3]
    for n in ('w_mlp_out', 'w_mlp_in', 'w_out', 'w_glu', 'w_in'):
        parts, src = _push_wait(scatter[n], after, 'scatter_wait_' + n, merge=False)
        outs = _big_update(parts, src, shard(W[n], n), shard(M[n], n), shard(V[n], n), 'update_' + n)
        after = outs[3]
        out_g[n], out_d[n], out_m[n], out_v[n] = (t.T for t in outs) if n == 'w_in' else outs

    lead = lambda t: t[None]
    return (loss, grad_x[None], *[lead(out_g[n]) for n in WEIGHTS], *[lead(out_d[n]) for n in WEIGHTS],
            *[lead(out_m[n]) for n in WEIGHTS], *[lead(out_v[n]) for n in WEIGHTS])
```

```python
import math

import jax
import jax.numpy as jnp
from jax import lax
from jax.experimental import pallas as pl
from jax.experimental.pallas import tpu as pltpu

F32 = jnp.float32
BF16 = jnp.bfloat16
HI = lax.Precision.HIGHEST
MESH = pl.DeviceIdType.MESH

N_DEV = 8
L = 4096
D = 2048
HEAD_DIM = 64
N_GROUPS = 3
DILATIONS = (1, 4, 16)
HEADS = 6
QW = N_GROUPS * HEADS * HEAD_DIM
KVW = HEADS * HEAD_DIM
ROT_DIM = 16
ROPE_THETA = 500000.0
BLK = 128
NBLK = L // BLK
SSMW = D - QW
SSM_P = 16
SSM_G = SSMW // SSM_P
SSM_N = 64
SSM_GN = SSM_G * SSM_N
CL_G = 8
N_CL = SSM_G // CL_G
CL_U = CL_G * SSM_P
CL_S = CL_G * SSM_N
INW = QW + 2 * KVW + SSMW
OUTW = KVW + SSMW
DFF = 4 * D
NMOD = 6
EPS = 1e-6
LR, B1, B2, AEPS, WD, STEP = 0.001, 0.9, 0.999, 1e-08, 0.01, 10

T_SCAN = 1024
MB = 2 ** 20

WEIGHTS = ['w_ada', 'b_ada', 'g_pre_mix', 'g_post_mix', 'w_in', 'ssm_a_re', 'ssm_a_im', 'ssm_log_dt',
           'ssm_b_re', 'ssm_b_im', 'ssm_c_re', 'ssm_c_im', 'ssm_d', 'w_glu', 'b_glu', 'g_attn_out',
           'g_ssm_out', 'w_out', 'g_pre_mlp', 'g_post_mlp', 'w_mlp_in', 'w_mlp_out']
BIG = ['w_in', 'w_glu', 'w_out', 'w_mlp_in', 'w_mlp_out']
SMALL = [n for n in WEIGHTS if n not in BIG and n != 'w_ada']
SMALL_SIZES = {'b_ada': NMOD * D, 'g_pre_mix': D, 'g_post_mix': D, 'ssm_a_re': SSM_GN, 'ssm_a_im': SSM_GN,
               'ssm_log_dt': SSM_G, 'ssm_b_re': SSM_GN * SSM_P, 'ssm_b_im': SSM_GN * SSM_P,
               'ssm_c_re': SSM_GN * SSM_P, 'ssm_c_im': SSM_GN * SSM_P, 'ssm_d': SSMW, 'b_glu': SSMW,
               'g_attn_out': KVW, 'g_ssm_out': SSMW, 'g_pre_mlp': D, 'g_post_mlp': D}
SEG = {n: -(-SMALL_SIZES[n] // 1024) * 1024 for n in SMALL}
SMALL_LATE = ['b_ada', 'g_pre_mix']
SMALL_EARLY = [n for n in SMALL if n not in SMALL_LATE]
SMALL_SIZES['loss'], SEG['loss'] = 1, 1024
LATE_PACK = SMALL_LATE + ['loss']


def _params(sem=None, vmem_mb=None):
    kw = {}
    if sem is not None:
        kw['dimension_semantics'] = sem
    if vmem_mb is not None:
        kw['vmem_limit_bytes'] = vmem_mb * MB
    return pltpu.CompilerParams(**kw)


def _vec(n):
    return pl.BlockSpec((1, n), lambda *_: (0, 0))


def _rms(x):
    return lax.rsqrt(jnp.mean(x * x, axis=-1, keepdims=True) + EPS)


def _rms_bwd(dn, n, r):
    return r * (dn - n * jnp.mean(dn * n, axis=-1, keepdims=True))


def _vec8(n):
    return pl.BlockSpec((8, n), lambda *_: (0, 0))


def _colsum(x):
    return jnp.sum(x.reshape(-1, 8, x.shape[-1]), axis=0)


def _fold8(last, *refs):
    @pl.when(last)
    def _():
        for r in refs:
            r[...] = jnp.broadcast_to(jnp.sum(r[...], axis=0, keepdims=True), r.shape)


def _mm(a, b, *, mode, name, out_dtype=F32, tm=1024, tn=1024, tk=2048, epilogue=None, extra=None,
        b_sharded=False, out_sharded=False, deps=()):
    if mode == 'nn':
        M, K = a.shape
        dims = (((1,), (0,)), ((), ()))
        a_spec = pl.BlockSpec((tm, tk), lambda i, j, k: (i, k))
        if b_sharded:
            _, K2, per = b.shape
            N, q = N_DEV * per, per // tn
            b_spec = pl.BlockSpec((None, tk, tn), lambda i, j, k: (j // q, k, j % q))
        else:
            K2, N = b.shape
            b_spec = pl.BlockSpec((tk, tn), lambda i, j, k: (k, j))
    elif mode == 'nt':
        M, K = a.shape
        dims = (((1,), (1,)), ((), ()))
        a_spec = pl.BlockSpec((tm, tk), lambda i, j, k: (i, k))
        if b_sharded:
            _, N, per = b.shape
            K2 = N_DEV * per
            if tk > per:
                b_spec = pl.BlockSpec((tk // per, tn, per), lambda i, j, k: (k, j, 0))
            else:
                q = per // tk
                b_spec = pl.BlockSpec((None, tn, tk), lambda i, j, k: (k // q, j, k % q))
        else:
            N, K2 = b.shape
            b_spec = pl.BlockSpec((tn, tk), lambda i, j, k: (j, k))
    else:
        (K, M), (K2, N) = a.shape, b.shape
        dims = (((0,), (0,)), ((), ()))
        a_spec = pl.BlockSpec((tk, tm), lambda i, j, k: (k, i))
        b_spec = pl.BlockSpec((tk, tn), lambda i, j, k: (k, j))
    assert K == K2 and M % tm == 0 and N % tn == 0 and K % tk == 0, (name, a.shape, b.shape, tm, tn, tk)
    nk = K // tk
    o_spec = pl.BlockSpec((tm, tn), lambda i, j, k: (i, j))
    o_dims = (M, N)
    if out_sharded:
        qo = N // N_DEV // tn
        o_spec = pl.BlockSpec((None, tm, tn), lambda i, j, k: (j // qo, i, j % qo))
        o_dims = (N_DEV, M, N // N_DEV)
    n_out = 2 if epilogue == 'relu2' else 1
    n_extra = 1 if extra is not None else 0
    n_in = 2 + n_extra + len(deps)

    def body(*refs):
        a_ref, b_ref = refs[0], refs[1]
        x_refs = refs[2:2 + n_extra]
        o_refs = refs[n_in:n_in + n_out]
        acc = refs[-1]
        k = pl.program_id(2)

        if len(b_ref.shape) == 3:
            per_ = b_ref.shape[2]
            prod = sum(lax.dot_general(a_ref[:, s * per_:(s + 1) * per_], b_ref[s], dims, preferred_element_type=F32)
                       for s in range(b_ref.shape[0]))
        else:
            prod = lax.dot_general(a_ref[...], b_ref[...], dims, preferred_element_type=F32)

        def finish(r):
            if epilogue == 'relu2':
                o_refs[0][...] = r.astype(BF16)
                o_refs[1][...] = jnp.square(jnp.maximum(r, 0.0)).astype(BF16)
            elif epilogue == 'drelu2':
                pre = x_refs[0][...].astype(F32)
                o_refs[0][...] = (r * (2.0 * jnp.maximum(pre, 0.0))).astype(out_dtype)
            else:
                o_refs[0][...] = r.astype(out_dtype)

        if nk == 1:
            finish(prod)
        else:
            @pl.when(k == 0)
            def _():
                acc[...] = prod

            @pl.when((k > 0) & (k < nk - 1))
            def _():
                acc[...] += prod

            @pl.when(k == nk - 1)
            def _():
                finish(acc[...] + prod)

    if epilogue == 'relu2':
        out_shape = (jax.ShapeDtypeStruct((M, N), BF16), jax.ShapeDtypeStruct((M, N), BF16))
        out_specs = (o_spec, o_spec)
    else:
        out_shape = jax.ShapeDtypeStruct(o_dims, out_dtype)
        out_specs = o_spec
    args = (a, b) + ((extra,) if extra is not None else ()) + tuple(deps)
    in_specs = ([a_spec, b_spec] + ([o_spec] if extra is not None else [])
                + [pl.BlockSpec(memory_space=pl.ANY)] * len(deps))
    return pl.pallas_call(
        body, out_shape=out_shape, grid=(M // tm, N // tn, nk), in_specs=in_specs, out_specs=out_specs,
        scratch_shapes=[pltpu.VMEM((tm, tn) if nk > 1 else (8, 128), F32)], name=name,
        compiler_params=_params(('parallel', 'parallel', 'arbitrary'), 56))(*args)


TR = 256


def _rowspec(w=D):
    return pl.BlockSpec((TR, w), lambda i: (i, 0))


def _prenorm_fwd(x, g, sc, sh):
    def body(x_ref, g_ref, sc_ref, sh_ref, h_ref):
        xv = x_ref[...]
        n = xv * _rms(xv)
        h_ref[...] = ((n * g_ref[...]) * (1.0 + sc_ref[...]) + sh_ref[...]).astype(BF16)

    return pl.pallas_call(
        body, out_shape=jax.ShapeDtypeStruct((L, D), BF16), grid=(L // TR,),
        in_specs=[_rowspec(), _vec(D), _vec(D), _vec(D)], out_specs=_rowspec(), name='prenorm_fwd',
        compiler_params=_params(('parallel',), 40))(x, g, sc, sh)


def _postmix_fwd(x, mix, gpm, gt1, gpl, sc2, sh2):
    def body(x_ref, mix_ref, gpm_ref, gt1_ref, gpl_ref, sc2_ref, sh2_ref, x1_ref, h2_ref):
        mix_v = mix_ref[...]
        nm = mix_v * _rms(mix_v)
        x1 = x_ref[...] + gt1_ref[...] * (nm * gpm_ref[...])
        x1_ref[...] = x1
        n2 = x1 * _rms(x1)
        h2_ref[...] = ((n2 * gpl_ref[...]) * (1.0 + sc2_ref[...]) + sh2_ref[...]).astype(BF16)

    return pl.pallas_call(
        body, out_shape=(jax.ShapeDtypeStruct((L, D), F32), jax.ShapeDtypeStruct((L, D), BF16)), grid=(L // TR,),
        in_specs=[_rowspec(), _rowspec()] + [_vec(D)] * 5, out_specs=(_rowspec(), _rowspec()), name='postmix_fwd',
        compiler_params=_params(('parallel',), 40))(x, mix, gpm, gt1, gpl, sc2, sh2)


def _final_fwd_bwd(x1, y, tgt, g, gt2):
    def body(x1_ref, y_ref, t_ref, g_ref, gt2_ref, dx2_ref, dy_ref, loss_ref, dgt2_ref, dg_ref):
        @pl.when(pl.program_id(0) == 0)
        def _():
            loss_ref[...] = jnp.zeros_like(loss_ref)
            dgt2_ref[...] = jnp.zeros_like(dgt2_ref)
            dg_ref[...] = jnp.zeros_like(dg_ref)

        yv = y_ref[...]
        r = _rms(yv)
        n = yv * r
        ng = n * g_ref[...]
        x2 = x1_ref[...] + gt2_ref[...] * ng
        e = x2 - t_ref[...]
        loss_ref[...] += 0.5 * jnp.sum(jnp.mean(e * e, axis=-1, keepdims=True), axis=0, keepdims=True)
        dx2 = e * (1.0 / D)
        dx2_ref[...] = dx2
        dgt2_ref[...] += _colsum(dx2 * ng)
        dng = dx2 * gt2_ref[...]
        dg_ref[...] += _colsum(dng * n)
        dy_ref[...] = _rms_bwd(dng * g_ref[...], n, r).astype(BF16)
        _fold8(pl.program_id(0) == L // TR - 1, dgt2_ref, dg_ref)

    return pl.pallas_call(
        body,
        out_shape=(jax.ShapeDtypeStruct((L, D), F32), jax.ShapeDtypeStruct((L, D), BF16),
                   jax.ShapeDtypeStruct((8, 128), F32), jax.ShapeDtypeStruct((8, D), F32),
                   jax.ShapeDtypeStruct((8, D), F32)),
        grid=(L // TR,), in_specs=[_rowspec(), _rowspec(), _rowspec(), _vec(D), _vec(D)],
        out_specs=(_rowspec(), _rowspec(), _vec8(128), _vec8(D), _vec8(D)), name='final_fwd_bwd',
        compiler_params=_params(('arbitrary',), 40))(x1, y, tgt, g, gt2)


def _postmix_bwd(dx2, dh2, x1, mix, gpm, gt1, gpl, sc2):
    def body(dx2_ref, dh2_ref, x1_ref, mix_ref, gpm_ref, gt1_ref, gpl_ref, sc2_ref,
             dx1_ref, dmix_ref, dsc2_ref, dsh2_ref, dgpl_ref, dgt1_ref, dgpm_ref):
        @pl.when(pl.program_id(0) == 0)
        def _():
            for r_ in (dsc2_ref, dsh2_ref, dgpl_ref, dgt1_ref, dgpm_ref):
                r_[...] = jnp.zeros_like(r_)

        x1v = x1_ref[...]
        r2 = _rms(x1v)
        n2 = x1v * r2
        dh2v = dh2_ref[...].astype(F32)
        dsh2_ref[...] += _colsum(dh2v)
        dsc2_ref[...] += _colsum(dh2v * (n2 * gpl_ref[...]))
        t = dh2v * (1.0 + sc2_ref[...])
        dgpl_ref[...] += _colsum(t * n2)
        dx1 = dx2_ref[...] + _rms_bwd(t * gpl_ref[...], n2, r2)
        dx1_ref[...] = dx1
        mix_v = mix_ref[...]
        rm = _rms(mix_v)
        nm = mix_v * rm
        dgt1_ref[...] += _colsum(dx1 * (nm * gpm_ref[...]))
        u = dx1 * gt1_ref[...]
        dgpm_ref[...] += _colsum(u * nm)
        dmix_ref[...] = _rms_bwd(u * gpm_ref[...], nm, rm).astype(BF16)
        _fold8(pl.program_id(0) == L // TR - 1, dsc2_ref, dsh2_ref, dgpl_ref, dgt1_ref, dgpm_ref)

    vs = jax.ShapeDtypeStruct((8, D), F32)
    return pl.pallas_call(
        body, out_shape=(jax.ShapeDtypeStruct((L, D), F32), jax.ShapeDtypeStruct((L, D), BF16), vs, vs, vs, vs, vs),
        grid=(L // TR,), in_specs=[_rowspec()] * 4 + [_vec(D)] * 4,
        out_specs=(_rowspec(), _rowspec()) + (_vec8(D),) * 5, name='postmix_bwd',
        compiler_params=_params(('arbitrary',), 48))(dx2, dh2, x1, mix, gpm, gt1, gpl, sc2)


def _prenorm_bwd(dx1, dh1, x, g, sc1):
    def body(dx1_ref, dh1_ref, x_ref, g_ref, sc1_ref, dx_ref, dsc1_ref, dsh1_ref, dg_ref):
        @pl.when(pl.program_id(0) == 0)
        def _():
            for r_ in (dsc1_ref, dsh1_ref, dg_ref):
                r_[...] = jnp.zeros_like(r_)

        xv = x_ref[...]
        r = _rms(xv)
        n = xv * r
        dh = dh1_ref[...].astype(F32)
        dsh1_ref[...] += _colsum(dh)
        dsc1_ref[...] += _colsum(dh * (n * g_ref[...]))
        t = dh * (1.0 + sc1_ref[...])
        dg_ref[...] += _colsum(t * n)
        dx_ref[...] = dx1_ref[...] + _rms_bwd(t * g_ref[...], n, r)
        _fold8(pl.program_id(0) == L // TR - 1, dsc1_ref, dsh1_ref, dg_ref)

    vs = jax.ShapeDtypeStruct((8, D), F32)
    return pl.pallas_call(
        body, out_shape=(jax.ShapeDtypeStruct((L, D), F32), vs, vs, vs), grid=(L // TR,),
        in_specs=[_rowspec()] * 3 + [_vec(D)] * 2, out_specs=(_rowspec(),) + (_vec8(D),) * 3, name='prenorm_bwd',
        compiler_params=_params(('arbitrary',), 40))(dx1, dh1, x, g, sc1)


ROPE_W = QW + KVW
QKV_W = 3 * KVW
QK_SCALE = 0.125
NB_KV = KVW // 128


def _rope_rotate(xv, pos, fr, sign):
    ang = pos.astype(F32) * fr
    w = lax.broadcasted_iota(jnp.int32, (1, 128), 1) % HEAD_DIM
    cs = jnp.cos(ang)
    sn = jnp.sin(ang) * sign
    s1 = jnp.where(w < ROT_DIM // 2, -sn, 0.0)
    s2 = jnp.where((w >= ROT_DIM // 2) & (w < ROT_DIM), sn, 0.0)
    width = xv.shape[1]
    rep = width // 128
    cs, s1, s2 = jnp.tile(cs, (1, rep)), jnp.tile(s1, (1, rep)), jnp.tile(s2, (1, rep))
    hi = pltpu.roll(xv, width - ROT_DIM // 2, 1)
    lo = pltpu.roll(xv, ROT_DIM // 2, 1)
    return xv * cs + hi * s1 + lo * s2


def _sub_spec(d, rows, width):
    return pl.BlockSpec((d, rows // d, width), lambda i: (0, i, 0))


def _gather_rows(scr, blocks, r, d, rows):
    return jnp.concatenate([scr.at[j][pl.ds(r, rows // d, stride=d), :] for j in blocks], axis=1)


def _scatter_rows(scr, src_ref, d, rows):
    for r in range(d):
        for j in range(NB_KV):
            scr.at[j][pl.ds(r, rows // d, stride=d), :] = src_ref[r, :, j * 128:(j + 1) * 128]


def _token_order(scr):
    return jnp.concatenate([scr[j] for j in range(NB_KV)], axis=1)


def _rope_fwd(proj, pos, fr):
    nb = (ROPE_W + KVW) // 128

    def body(x_ref, pos_ref, fr_ref, o0_ref, o1_ref, o2_ref, scr):
        y = _rope_rotate(x_ref[:, 0:ROPE_W], pos_ref[...], fr_ref[...], 1.0)
        for j in range(ROPE_W // 128):
            scr[j] = y[:, j * 128:(j + 1) * 128] * (QK_SCALE if j < QW // 128 else 1.0)
        for j in range(ROPE_W // 128, nb):
            scr[j] = x_ref[:, j * 128:(j + 1) * 128]
        kv = list(range(QW // 128, nb))
        for g, (d, o_ref) in enumerate(zip(DILATIONS, (o0_ref, o1_ref, o2_ref))):
            blocks = list(range(g * NB_KV, (g + 1) * NB_KV)) + kv
            for r in range(d):
                o_ref[r] = _gather_rows(scr, blocks, r, d, TR).astype(BF16)

    return pl.pallas_call(
        body, out_shape=tuple(jax.ShapeDtypeStruct((d, L // d, QKV_W), BF16) for d in DILATIONS), grid=(L // TR,),
        in_specs=[_rowspec(ROPE_W + KVW), pl.BlockSpec((TR, 1), lambda i: (i, 0)), _vec(128)],
        out_specs=tuple(_sub_spec(d, TR, QKV_W) for d in DILATIONS),
        scratch_shapes=[pltpu.VMEM((nb, TR, 128), F32)], name='rope_fwd',
        compiler_params=_params(('parallel',), 40))(proj, pos, fr)


def _rope_bwd(dqkv, du, pos, fr):
    def body(*refs):
        grads = [refs[3 * g:3 * g + 3] for g in range(N_GROUPS)]
        du_ref, pos_ref, fr_ref, o_ref = refs[9:13]
        scrs = refs[13:]
        dq, dk, dv = [], None, None
        for g, d in enumerate(DILATIONS):
            parts = []
            for t in range(3):
                if d == 1:
                    parts.append(grads[g][t][0])
                else:
                    scr = scrs[3 * (g - 1) + t]
                    _scatter_rows(scr, grads[g][t], d, TR)
                    parts.append(_token_order(scr))
            dq.append(parts[0])
            dk = parts[1] if dk is None else dk + parts[1]
            dv = parts[2] if dv is None else dv + parts[2]
        x = jnp.concatenate(dq + [dk], axis=1)
        o_ref[:, 0:ROPE_W] = _rope_rotate(x, pos_ref[...], fr_ref[...], -1.0).astype(BF16)
        o_ref[:, ROPE_W:ROPE_W + KVW] = dv.astype(BF16)
        o_ref[:, ROPE_W + KVW:INW] = du_ref[...].astype(BF16)

    flat = [a for grp in dqkv for a in grp]
    in_specs = [_sub_spec(d, TR, KVW) for d in DILATIONS for _ in range(3)]
    in_specs += [_rowspec(SSMW), pl.BlockSpec((TR, 1), lambda i: (i, 0)), _vec(128)]
    return pl.pallas_call(
        body, out_shape=jax.ShapeDtypeStruct((L, INW), BF16), grid=(L // TR,), in_specs=in_specs,
        out_specs=_rowspec(INW), scratch_shapes=[pltpu.VMEM((NB_KV, TR, 128), F32)] * 6, name='rope_bwd',
        compiler_params=_params(('parallel',), 48))(*flat, du, pos, fr)


def _attn_mask(nbs, b):
    first = (b & (nbs - 1)) == 0
    qi = lax.broadcasted_iota(jnp.int32, (BLK, 2 * BLK), 0)
    kj = lax.broadcasted_iota(jnp.int32, (BLK, 2 * BLK), 1)
    dist = qi + BLK - kj
    return (dist >= 0) & (dist <= BLK) & ((kj >= BLK) | jnp.logical_not(first))


def _qkv_specs():
    cur = lambda col: pl.BlockSpec((BLK, KVW), lambda b: (b, col))
    prev = lambda col: pl.BlockSpec((BLK, KVW), lambda b: (jnp.maximum(b - 1, 0), col))
    return [cur(0), prev(1), cur(1), prev(2), cur(2)]


_ROWS = pl.BlockSpec((BLK, KVW), lambda b: (b, 0))
NEG = -1e30
NT_DIMS = (((1,), (1,)), ((), ()))
TN_DIMS = (((0,), (0,)), ((), ()))


def _attn_fwd(qkv, d):
    nbs = L // d // BLK

    def body(q_ref, kp_ref, kc_ref, vp_ref, vc_ref, o_ref, lse_ref):
        valid = _attn_mask(nbs, pl.program_id(0))
        heads = [slice(h * HEAD_DIM, (h + 1) * HEAD_DIM) for h in range(HEADS)]
        kcs = [jnp.concatenate([kp_ref[:, hs], kc_ref[:, hs]], axis=0) for hs in heads]
        vcs = [jnp.concatenate([vp_ref[:, hs], vc_ref[:, hs]], axis=0) for hs in heads]
        ss = [lax.dot_general(q_ref[:, hs], kc, NT_DIMS, preferred_element_type=F32) for hs, kc in zip(heads, kcs)]
        ps, ls, lses = [], [], []
        for s in ss:
            s = jnp.where(valid, s, NEG)
            m = jnp.max(s, axis=-1, keepdims=True)
            p = jnp.exp(s - m)
            l = jnp.sum(p, axis=-1, keepdims=True)
            ps.append(p.astype(BF16))
            ls.append(l)
            lses.append(jnp.broadcast_to(m + jnp.log(l), (BLK, HEAD_DIM)))
        outs = [jnp.dot(p, vc, preferred_element_type=F32) / l for p, vc, l in zip(ps, vcs, ls)]
        o_ref[...] = jnp.concatenate(outs, axis=1)
        lse_ref[...] = jnp.concatenate(lses, axis=1)

    sh = jax.ShapeDtypeStruct((L, KVW), F32)
    q2 = qkv.reshape(L, QKV_W)
    o, lse = pl.pallas_call(
        body, out_shape=(sh, sh), grid=(NBLK,), in_specs=_qkv_specs(), out_specs=(_ROWS, _ROWS),
        name='attn_fwd_d%d' % d, compiler_params=_params(('parallel',), 32))(q2, q2, q2, q2, q2)
    return o.reshape(d, L // d, KVW), lse.reshape(d, L // d, KVW)


def _attn_bwd(qkv, o, lse, do, dlse, d):
    nbs = L // d // BLK

    def body(q_ref, kp_ref, kc_ref, vp_ref, vc_ref, o_ref, lse_ref, do_ref, dlse_ref, dq_ref, dk_ref, dv_ref):
        b = pl.program_id(0)

        @pl.when(b == 0)
        def _():
            dk_ref[...] = jnp.zeros_like(dk_ref)
            dv_ref[...] = jnp.zeros_like(dv_ref)

        valid = _attn_mask(nbs, b)
        prev0 = pl.multiple_of(jnp.maximum(b - 1, 0) * BLK, BLK)
        cur0 = pl.multiple_of(b * BLK, BLK)
        heads = [slice(h * HEAD_DIM, (h + 1) * HEAD_DIM) for h in range(HEADS)]
        qs = [q_ref[:, hs] for hs in heads]
        kcs = [jnp.concatenate([kp_ref[:, hs], kc_ref[:, hs]], axis=0) for hs in heads]
        vcs = [jnp.concatenate([vp_ref[:, hs], vc_ref[:, hs]], axis=0) for hs in heads]
        dos = [do_ref[:, hs] for hs in heads]
        do_bs = [t.astype(BF16) for t in dos]
        ss = [lax.dot_general(q, kc, NT_DIMS, preferred_element_type=F32) for q, kc in zip(qs, kcs)]
        dps = [lax.dot_general(do_b, vc, NT_DIMS, preferred_element_type=F32) for do_b, vc in zip(do_bs, vcs)]
        p_bs, ds_bs = [], []
        for h, hs in enumerate(heads):
            s = jnp.where(valid, ss[h], NEG)
            p = jnp.exp(s - lse_ref[:, h * HEAD_DIM:h * HEAD_DIM + 1])
            delta = jnp.sum(dos[h] * o_ref[:, hs], axis=-1, keepdims=True)
            ds = p * (dps[h] - delta + dlse_ref[:, h * HEAD_DIM:h * HEAD_DIM + 1])
            p_bs.append(p.astype(BF16))
            ds_bs.append(ds.astype(BF16))
        dqs = [jnp.dot(ds_b, kc, preferred_element_type=F32) for ds_b, kc in zip(ds_bs, kcs)]
        dks = [lax.dot_general(ds_b, q, TN_DIMS, preferred_element_type=F32) for ds_b, q in zip(ds_bs, qs)]
        dvs = [lax.dot_general(p_b, do_b, TN_DIMS, preferred_element_type=F32) for p_b, do_b in zip(p_bs, do_bs)]
        dq_ref[...] = jnp.concatenate(dqs, axis=1) * QK_SCALE
        dkc, dvc = jnp.concatenate(dks, axis=1), jnp.concatenate(dvs, axis=1)
        dk_ref[pl.ds(prev0, BLK), :] += dkc[:BLK]
        dv_ref[pl.ds(prev0, BLK), :] += dvc[:BLK]
        dk_ref[pl.ds(cur0, BLK), :] += dkc[BLK:]
        dv_ref[pl.ds(cur0, BLK), :] += dvc[BLK:]

    sh = jax.ShapeDtypeStruct((L, KVW), F32)
    whole = pl.BlockSpec((L, KVW), lambda b: (0, 0))
    q2 = qkv.reshape(L, QKV_W)
    flat = lambda t: t.reshape(L, KVW)
    outs = pl.pallas_call(
        body, out_shape=(sh, sh, sh), grid=(NBLK,), in_specs=_qkv_specs() + [_ROWS] * 4,
        out_specs=(_ROWS, whole, whole), name='attn_bwd_d%d' % d,
        compiler_params=_params(('arbitrary',), 48))(q2, q2, q2, q2, q2, flat(o), flat(lse), flat(do), flat(dlse))
    return tuple(t.reshape(d, L // d, KVW) for t in outs)


TC = 512


def _combine_weights(l0, l1, l2):
    m = jnp.maximum(jnp.maximum(l0, l1), l2)
    e0, e1, e2 = jnp.exp(l0 - m), jnp.exp(l1 - m), jnp.exp(l2 - m)
    z = e0 + e1 + e2
    return e0 / z, e1 / z, e2 / z


def _load_groups(refs, scrs):
    out = [refs[0][0]]
    for g in (1, 2):
        _scatter_rows(scrs[g - 1], refs[g], DILATIONS[g], TC)
        out.append(_token_order(scrs[g - 1]))
    return out


def _combine_fwd(os_, lses, g):
    def body(o0, o1, o2, l0, l1, l2, g_ref, att_ref, *scrs):
        ov = _load_groups((o0, o1, o2), scrs[0:2])
        lv = _load_groups((l0, l1, l2), scrs[2:4])
        w0, w1, w2 = _combine_weights(*lv)
        a = w0 * ov[0] + w1 * ov[1] + w2 * ov[2]
        att_ref[...] = ((a * _rms(a)) * g_ref[...]).astype(BF16)

    subs = [_sub_spec(d, TC, KVW) for d in DILATIONS]
    return pl.pallas_call(
        body, out_shape=jax.ShapeDtypeStruct((L, KVW), BF16), grid=(L // TC,), in_specs=subs + subs + [_vec(KVW)],
        out_specs=pl.BlockSpec((TC, KVW), lambda i: (i, 0)),
        scratch_shapes=[pltpu.VMEM((NB_KV, TC, 128), F32)] * 4, name='combine_fwd',
        compiler_params=_params(('parallel',), 40))(*os_, *lses, g)


def _combine_bwd(dcat, os_, lses, g, head_ones):
    def body(datt_ref, o0, o1, o2, l0, l1, l2, g_ref, e_ref, do0, do1, do2, dl0, dl1, dl2, dg_ref, *scrs):
        @pl.when(pl.program_id(0) == 0)
        def _():
            dg_ref[...] = jnp.zeros_like(dg_ref)

        ov = _load_groups((o0, o1, o2), scrs[0:2])
        lv = _load_groups((l0, l1, l2), scrs[2:4])
        ws = _combine_weights(*lv)
        a = ws[0] * ov[0] + ws[1] * ov[1] + ws[2] * ov[2]
        r = _rms(a)
        n = a * r
        dv = datt_ref[...].astype(F32)
        dg_ref[...] += _colsum(dv * n)
        da = _rms_bwd(dv * g_ref[...], n, r)
        e_b = e_ref[...].astype(BF16)

        def head_sum(t):
            hi = t.astype(BF16)
            lo = (t - hi.astype(F32)).astype(BF16)
            return jnp.dot(hi, e_b, preferred_element_type=F32) + jnp.dot(lo, e_b, preferred_element_type=F32)

        dws = [head_sum(da * ov[i]) for i in range(3)]
        dbar = ws[0] * dws[0] + ws[1] * dws[1] + ws[2] * dws[2]
        scr = scrs[4]
        for i, (d, do_ref, dl_ref) in enumerate(zip(DILATIONS, (do0, do1, do2), (dl0, dl1, dl2))):
            for val, out_ref in ((ws[i] * da, do_ref), (ws[i] * (dws[i] - dbar), dl_ref)):
                if d == 1:
                    out_ref[0] = val
                else:
                    for j in range(NB_KV):
                        scr[j] = val[:, j * 128:(j + 1) * 128]
                    for rr in range(d):
                        out_ref[rr] = _gather_rows(scr, range(NB_KV), rr, d, TC)
        _fold8(pl.program_id(0) == L // TC - 1, dg_ref)

    subs = [_sub_spec(d, TC, KVW) for d in DILATIONS]
    shs = tuple(jax.ShapeDtypeStruct((d, L // d, KVW), F32) for d in DILATIONS)
    outs = pl.pallas_call(
        body, out_shape=shs + shs + (jax.ShapeDtypeStruct((8, KVW), F32),), grid=(L // TC,),
        in_specs=[pl.BlockSpec((TC, KVW), lambda i: (i, 0))] + subs + subs + [_vec(KVW),
                                                                              pl.BlockSpec((KVW, KVW), lambda i: (0, 0))],
        out_specs=tuple(subs) + tuple(subs) + (_vec8(KVW),),
        scratch_shapes=[pltpu.VMEM((NB_KV, TC, 128), F32)] * 5, name='combine_bwd',
        compiler_params=_params(('arbitrary',), 48))(dcat, *os_, *lses, g, head_ones)
    return outs[0:3], outs[3:6], outs[6]


def _ssm_disc(ar, ai, ldt):
    dt = jnp.exp(ldt)
    zr, zi = ar * dt, ai * dt
    ez = jnp.exp(zr)
    A_r, A_i = ez * jnp.cos(zi), ez * jnp.sin(zi)
    den = ar * ar + ai * ai
    xr, xi = A_r - 1.0, A_i
    cr = (xr * ar + xi * ai) / den
    ci = (xi * ar - xr * ai) / den
    return dt, zr, zi, A_r, A_i, den, cr, ci


def _ssm_pre(ar, ai, ldt, br, bi):
    def body(ar_ref, ai_ref, ldt_ref, br_ref, bi_ref, bbr_ref, bbi_ref, pwr_ref, pwi_ref):
        _, zr, zi, _, _, _, cr, ci = _ssm_disc(ar_ref[...], ai_ref[...], ldt_ref[...])
        bbr_ref[...] = cr * br_ref[...] - ci * bi_ref[...]
        bbi_ref[...] = cr * bi_ref[...] + ci * br_ref[...]
        k = (lax.broadcasted_iota(jnp.int32, (1, 8), 1) + 1).astype(F32)
        ek = jnp.exp(zr * k)
        pwr_ref[...] = ek * jnp.cos(zi * k)
        pwi_ref[...] = ek * jnp.sin(zi * k)

    s16 = jax.ShapeDtypeStruct((SSM_GN, SSM_P), F32)
    s8 = jax.ShapeDtypeStruct((SSM_GN, 8), F32)
    return pl.pallas_call(body, out_shape=(s16, s16, s8, s8), name='ssm_pre',
                          compiler_params=_params(None, 40))(ar, ai, ldt, br, bi)


def _ssm_post(ar, ai, ldt, br, bi, gar, gai, gbr, gbi, sel):
    def body(ar_ref, ai_ref, ldt_ref, br_ref, bi_ref, gar_ref, gai_ref, gbr_ref, gbi_ref, sel_ref,
             dar_ref, dai_ref, dbr_ref, dbi_ref, dldt_ref):
        a_r, a_i = ar_ref[...], ai_ref[...]
        dt, _, _, A_r, A_i, den, cr, ci = _ssm_disc(a_r, a_i, ldt_ref[...])
        b_r, b_i, g_br, g_bi = br_ref[...], bi_ref[...], gbr_ref[...], gbi_ref[...]
        gcr = jnp.sum(g_br * b_r + g_bi * b_i, axis=-1, keepdims=True)
        gci = jnp.sum(g_bi * b_r - g_br * b_i, axis=-1, keepdims=True)
        dbr_ref[...] = g_br * cr + g_bi * ci
        dbi_ref[...] = g_bi * cr - g_br * ci
        g_ar = gar_ref[...] + (gcr * a_r - gci * a_i) / den
        g_ai = gai_ref[...] + (gcr * a_i + gci * a_r) / den
        qr = (cr * a_r + ci * a_i) / den
        qi = (ci * a_r - cr * a_i) / den
        glr = -(gcr * qr + gci * qi)
        gli = -(gci * qr - gcr * qi)
        gzr = g_ar * A_r + g_ai * A_i
        gzi = g_ai * A_r - g_ar * A_i
        dar_ref[...] = glr + gzr * dt
        dai_ref[...] = gli + gzi * dt
        gdt = (gzr * a_r + gzi * a_i) * dt
        dldt_ref[...] = jnp.dot(sel_ref[...], jnp.broadcast_to(gdt, (SSM_GN, 128)),
                                preferred_element_type=F32, precision=HI)

    s1 = jax.ShapeDtypeStruct((SSM_GN, 1), F32)
    s16 = jax.ShapeDtypeStruct((SSM_GN, SSM_P), F32)
    return pl.pallas_call(body, out_shape=(s1, s1, s16, s16, jax.ShapeDtypeStruct((SSM_G, 128), F32)),
                          name='ssm_post', compiler_params=_params(None, 48))(
                              ar, ai, ldt, br, bi, gar, gai, gbr, gbi, sel)


SCAN_CH = 8


def _scan_fwd_tiles(s_ref, pw, carry):
    pwr, pwi = pw[:, :CL_S], pw[:, CL_S:]
    row = lax.broadcasted_iota(jnp.int32, (8, CL_S), 0)
    steps = [(k, jnp.where(row >= k, pwr[k - 1:k], 0.0), jnp.where(row >= k, pwi[k - 1:k], 0.0)) for k in (1, 2, 4)]
    rows = 8 * SCAN_CH

    def chunk(i, c):
        cr, ci = c
        r0 = pl.multiple_of(i * rows, rows)
        xr = s_ref[pl.ds(r0, rows), 0:CL_S].reshape(SCAN_CH, 8, CL_S)
        xi = s_ref[pl.ds(r0, rows), CL_S:2 * CL_S].reshape(SCAN_CH, 8, CL_S)
        for k, pr, pi in steps:
            sr, si = pltpu.roll(xr, k, 1), pltpu.roll(xi, k, 1)
            xr, xi = xr + pr * sr - pi * si, xi + pr * si + pi * sr
        for j in range(SCAN_CH):
            tr = xr[j] + pwr * cr - pwi * ci
            ti = xi[j] + pwr * ci + pwi * cr
            s_ref[pl.ds(r0 + 8 * j, 8), 0:CL_S] = tr
            s_ref[pl.ds(r0 + 8 * j, 8), CL_S:2 * CL_S] = ti
            cr, ci = tr[7:8], ti[7:8]
        return cr, ci

    return lax.fori_loop(0, T_SCAN // rows, chunk, (carry[:, :CL_S], carry[:, CL_S:]))


def _scan_bwd_tiles(l_ref, pw, carry):
    pwr, pwi = pw[:, :CL_S], pw[:, CL_S:]
    rpr = jnp.concatenate([pwr[7 - r:8 - r] for r in range(8)], axis=0)
    rpi = jnp.concatenate([pwi[7 - r:8 - r] for r in range(8)], axis=0)
    row = lax.broadcasted_iota(jnp.int32, (8, CL_S), 0)
    steps = [(k, jnp.where(row < 8 - k, pwr[k - 1:k], 0.0), jnp.where(row < 8 - k, pwi[k - 1:k], 0.0))
             for k in (1, 2, 4)]
    rows = 8 * SCAN_CH
    nc = T_SCAN // rows

    def chunk(i, c):
        cr, ci = c
        r0 = pl.multiple_of((nc - 1 - i) * rows, rows)
        xr = l_ref[pl.ds(r0, rows), 0:CL_S].reshape(SCAN_CH, 8, CL_S)
        xi = l_ref[pl.ds(r0, rows), CL_S:2 * CL_S].reshape(SCAN_CH, 8, CL_S)
        for k, pr, pi in steps:
            sr, si = pltpu.roll(xr, 8 - k, 1), pltpu.roll(xi, 8 - k, 1)
            xr, xi = xr + pr * sr + pi * si, xi + pr * si - pi * sr
        for j in reversed(range(SCAN_CH)):
            tr = xr[j] + rpr * cr + rpi * ci
            ti = xi[j] + rpr * ci - rpi * cr
            l_ref[pl.ds(r0 + 8 * j, 8), 0:CL_S] = tr
            l_ref[pl.ds(r0 + 8 * j, 8), CL_S:2 * CL_S] = ti
            cr, ci = tr[0:1], ti[0:1]
        return cr, ci

    return lax.fori_loop(0, nc, chunk, (carry[:, :CL_S], carry[:, CL_S:]))


NT_SCAN = L // T_SCAN


def _cl_spec(r, c):
    return pl.BlockSpec((None, r, c), lambda c_, t: (c_, 0, 0))


def _ssm_fwd(u, bm, cm, pw, dvec, u_off=0):
    def body(u_ref, bm_ref, cm_ref, pw_ref, d_ref, y_ref, bnd_ref, s_ref, carry_ref):
        @pl.when(pl.program_id(1) == 0)
        def _():
            carry_ref[...] = jnp.zeros_like(carry_ref)

        bnd_ref[...] = carry_ref[...]
        uv = u_ref[...]
        s_ref[...] = jnp.dot(uv.astype(BF16), bm_ref[...], preferred_element_type=F32)
        cr, ci = _scan_fwd_tiles(s_ref, pw_ref[...], carry_ref[...])
        carry_ref[...] = jnp.concatenate([cr, ci], axis=1)
        y_ref[...] = jnp.dot(s_ref[...].astype(BF16), cm_ref[...], preferred_element_type=F32) + d_ref[...] * uv

    return pl.pallas_call(
        body,
        out_shape=(jax.ShapeDtypeStruct((L, SSMW), F32), jax.ShapeDtypeStruct((N_CL, NT_SCAN, 1, 2 * CL_S), F32),
                   jax.ShapeDtypeStruct((L, N_CL * 2 * CL_S), F32)),
        grid=(N_CL, NT_SCAN),
        in_specs=[pl.BlockSpec((T_SCAN, CL_U), lambda c, t: (t, c + u_off)),
                  _cl_spec(CL_U, 2 * CL_S), _cl_spec(2 * CL_S, CL_U),
                  pl.BlockSpec((None, 8, 2 * CL_S), lambda c, t: (c, 0, 0)),
                  pl.BlockSpec((1, CL_U), lambda c, t: (0, c))],
        out_specs=(pl.BlockSpec((T_SCAN, CL_U), lambda c, t: (t, c)),
                   pl.BlockSpec((None, None, 1, 2 * CL_S), lambda c, t: (c, t, 0, 0)),
                   pl.BlockSpec((T_SCAN, 2 * CL_S), lambda c, t: (t, c))),
        scratch_shapes=[pltpu.VMEM((1, 2 * CL_S), F32)],
        name='ssm_fwd', compiler_params=_params(('arbitrary', 'arbitrary'), 40))(u, bm, cm, pw, dvec)


def _ssm_bwd(u, dy, states, bmt, cmt, pw, dvec, bnd, u_off=0):
    rev = lambda t: NT_SCAN - 1 - t

    def body(u_ref, dy_ref, s_ref, bmt_ref, cmt_ref, pw_ref, d_ref, bnd_ref,
             du_ref, dbm_ref, dcm_ref, da_ref, dd_ref, l_ref, carry_ref):
        @pl.when(pl.program_id(1) == 0)
        def _():
            carry_ref[...] = jnp.zeros_like(carry_ref)
            dbm_ref[...] = jnp.zeros_like(dbm_ref)
            dcm_ref[...] = jnp.zeros_like(dcm_ref)
            da_ref[...] = jnp.zeros_like(da_ref)
            dd_ref[...] = jnp.zeros_like(dd_ref)

        uv, dyv, pw = u_ref[...], dy_ref[...], pw_ref[...]
        dy_b = dyv.astype(BF16)
        entry = bnd_ref[...]
        l_ref[...] = jnp.dot(dy_b, cmt_ref[...], preferred_element_type=F32)
        cr, ci = _scan_bwd_tiles(l_ref, pw, carry_ref[...])
        carry_ref[...] = jnp.concatenate([cr, ci], axis=1)
        sv, lv = s_ref[...], l_ref[...]
        lv_b = lv.astype(BF16)
        du_ref[...] = dyv * d_ref[...] + jnp.dot(lv_b, bmt_ref[...], preferred_element_type=F32)
        dbm_ref[...] += lax.dot_general(uv.astype(BF16), lv_b, TN_DIMS, preferred_element_type=F32)
        dcm_ref[...] += lax.dot_general(sv.astype(BF16), dy_b, TN_DIMS, preferred_element_type=F32)
        dd_ref[...] += _colsum(dyv * uv)
        row = lax.broadcasted_iota(jnp.int32, (T_SCAN, 2 * CL_S), 0)
        sp = jnp.where(row == 0, entry, pltpu.roll(sv, 1, 0))
        spr, spi = sp[:, :CL_S], sp[:, CL_S:]
        lr, li = lv[:, :CL_S], lv[:, CL_S:]
        da_ref[:, 0:CL_S] += _colsum(lr * spr + li * spi)
        da_ref[:, CL_S:2 * CL_S] += _colsum(li * spr - lr * spi)
        _fold8(pl.program_id(1) == NT_SCAN - 1, da_ref, dd_ref)

    return pl.pallas_call(
        body,
        out_shape=(jax.ShapeDtypeStruct((L, SSMW), F32), jax.ShapeDtypeStruct((N_CL, CL_U, 2 * CL_S), F32),
                   jax.ShapeDtypeStruct((N_CL, 2 * CL_S, CL_U), F32), jax.ShapeDtypeStruct((N_CL, 8, 2 * CL_S), F32),
                   jax.ShapeDtypeStruct((8, SSMW), F32)),
        grid=(N_CL, NT_SCAN),
        in_specs=[pl.BlockSpec((T_SCAN, CL_U), lambda c, t: (rev(t), c + u_off)),
                  pl.BlockSpec((T_SCAN, CL_U), lambda c, t: (rev(t), c)),
                  pl.BlockSpec((T_SCAN, 2 * CL_S), lambda c, t: (rev(t), c)),
                  pl.BlockSpec((None, 2 * CL_S, CL_U), lambda c, t: (c, 0, 0)),
                  pl.BlockSpec((None, CL_U, 2 * CL_S), lambda c, t: (c, 0, 0)),
                  pl.BlockSpec((None, 8, 2 * CL_S), lambda c, t: (c, 0, 0)),
                  pl.BlockSpec((1, CL_U), lambda c, t: (0, c)),
                  pl.BlockSpec((None, None, 1, 2 * CL_S), lambda c, t: (c, rev(t), 0, 0))],
        out_specs=(pl.BlockSpec((T_SCAN, CL_U), lambda c, t: (rev(t), c)),
                   pl.BlockSpec((None, CL_U, 2 * CL_S), lambda c, t: (c, 0, 0)),
                   pl.BlockSpec((None, 2 * CL_S, CL_U), lambda c, t: (c, 0, 0)),
                   pl.BlockSpec((None, 8, 2 * CL_S), lambda c, t: (c, 0, 0)),
                   pl.BlockSpec((8, CL_U), lambda c, t: (0, c))),
        scratch_shapes=[pltpu.VMEM((T_SCAN, 2 * CL_S), F32), pltpu.VMEM((1, 2 * CL_S), F32)],
        name='ssm_bwd', compiler_params=_params(('arbitrary', 'arbitrary'), 48))(u, dy, states, bmt, cmt, pw, dvec, bnd)


GELU_C = math.sqrt(2.0 / math.pi)
GELU_K = 0.044715


def _gelu_parts(x):
    t = jnp.tanh(GELU_C * (x + GELU_K * (x * x * x)))
    return x * (0.5 * (1.0 + t)), t


def _glu_fwd(ypre, wglu, bglu, gs):
    def body(y_ref, w_ref, b_ref, g_ref, o_ref):
        yg, _ = _gelu_parts(y_ref[...])
        z = jnp.dot(yg.astype(BF16), w_ref[...], preferred_element_type=F32) + b_ref[...]
        s = yg * jax.nn.sigmoid(z)
        o_ref[...] = ((s * _rms(s)) * g_ref[...]).astype(BF16)

    return pl.pallas_call(
        body, out_shape=jax.ShapeDtypeStruct((L, SSMW), BF16), grid=(L // TR,),
        in_specs=[_rowspec(SSMW), pl.BlockSpec((SSMW, SSMW), lambda i: (0, 0)), _vec(SSMW), _vec(SSMW)],
        out_specs=_rowspec(SSMW), name='glu_fwd', compiler_params=_params(('parallel',), 32))(ypre, wglu, bglu, gs)


def _glu_bwd(ypre, dsn, wglu, bglu, gs):
    def body(y_ref, d_ref, w_ref, b_ref, g_ref, dy_ref, dw_ref, db_ref, dg_ref):
        @pl.when(pl.program_id(0) == 0)
        def _():
            dw_ref[...] = jnp.zeros_like(dw_ref)
            db_ref[...] = jnp.zeros_like(db_ref)
            dg_ref[...] = jnp.zeros_like(dg_ref)

        xv = y_ref[...]
        yg, t = _gelu_parts(xv)
        yg_b = yg.astype(BF16)
        z = jnp.dot(yg_b, w_ref[...], preferred_element_type=F32) + b_ref[...]
        sg = jax.nn.sigmoid(z)
        s = yg * sg
        r = _rms(s)
        n = s * r
        dv = d_ref[:, d_ref.shape[1] - SSMW:].astype(F32)
        dg_ref[...] += _colsum(dv * n)
        ds = _rms_bwd(dv * g_ref[...], n, r)
        dz =(ds * yg) * (sg * (1.0 - sg))
        dz_b = dz.astype(BF16)
        db_ref[...] += _colsum(dz)
        dw_ref[...] += lax.dot_general(yg_b, dz_b, TN_DIMS, preferred_element_type=F32)
        dyg = ds * sg + lax.dot_general(dz_b, w_ref[...], NT_DIMS, preferred_element_type=F32)
        dgelu = 0.5 * (1.0 + t) + (0.5 * xv) * (1.0 - t * t) * (GELU_C * (1.0 + 3.0 * GELU_K * (xv * xv)))
        dy_ref[...] = dyg * dgelu
        _fold8(pl.program_id(0) == L // TR - 1, db_ref, dg_ref)

    vs = jax.ShapeDtypeStruct((8, SSMW), F32)
    return pl.pallas_call(
        body, out_shape=(jax.ShapeDtypeStruct((L, SSMW), F32), jax.ShapeDtypeStruct((SSMW, SSMW), F32), vs, vs),
        grid=(L // TR,),
        in_specs=[_rowspec(SSMW), _rowspec(dsn.shape[1]), pl.BlockSpec((SSMW, SSMW), lambda i: (0, 0)), _vec(SSMW),
                  _vec(SSMW)],
        out_specs=(_rowspec(SSMW), pl.BlockSpec((SSMW, SSMW), lambda i: (0, 0)), _vec8(SSMW), _vec8(SSMW)),
        name='glu_bwd', compiler_params=_params(('arbitrary',), 40))(ypre, dsn, wglu, bglu, gs)


def _me():
    return lax.axis_index('x'), lax.axis_index('y'), lax.axis_index('c')


def _my_index():
    return 4 * lax.axis_index('x') + 2 * lax.axis_index('y') + lax.axis_index('c')


def _peer(k):
    x, y, c = _me()
    px = 1 - x if k & 4 else x
    py = 1 - y if k & 2 else y
    pc = 1 - c if k & 1 else c
    return (px, py, pc), 4 * px + 2 * py + pc


def _mod_exchange(c_row, w_ada, b_ada8, deps=()):
    cw = NMOD * D // N_DEV

    def body(c_ref, w_ref, b_ref, *rest):
        call_ref, mod_ref, part_ref, send_sems, recv_sems = rest[len(deps):]
        x, y, c = _me()
        me = 4 * x + 2 * y + c
        call_ref[me] = c_ref[0]
        sends = []
        for k in range(1, N_DEV):
            peer, _ = _peer(k)
            cp = pltpu.make_async_remote_copy(src_ref=c_ref.at[0], dst_ref=call_ref.at[me], send_sem=send_sems.at[0, k - 1],
                                              recv_sem=recv_sems.at[0, k - 1], device_id=peer, device_id_type=MESH)
            cp.start()
            sends.append(cp)
        for k in range(1, N_DEV):
            peer, pidx = _peer(k)
            pltpu.make_async_remote_copy(src_ref=c_ref.at[0], dst_ref=call_ref.at[pidx], send_sem=send_sems.at[0, k - 1],
                                         recv_sem=recv_sems.at[0, k - 1], device_id=peer, device_id_type=MESH).wait_recv()
        for cp in sends:
            cp.wait_send()
        cv = call_ref[...].reshape(N_DEV, D)
        part = jnp.dot(cv * jax.nn.sigmoid(cv), w_ref[...], preferred_element_type=F32, precision=HI)
        part_ref[...] = part.reshape(N_DEV, 1, cw)
        mod_ref[me] = part_ref[me]
        sends = []
        for k in range(1, N_DEV):
            peer, pidx = _peer(k)
            cp = pltpu.make_async_remote_copy(src_ref=part_ref.at[pidx], dst_ref=mod_ref.at[me], send_sem=send_sems.at[1, k - 1],
                                              recv_sem=recv_sems.at[1, k - 1], device_id=peer, device_id_type=MESH)
            cp.start()
            sends.append(cp)
        for k in range(1, N_DEV):
            peer, pidx = _peer(k)
            pltpu.make_async_remote_copy(src_ref=part_ref.at[pidx], dst_ref=mod_ref.at[pidx], send_sem=send_sems.at[1, k - 1],
                                         recv_sem=recv_sems.at[1, k - 1], device_id=peer, device_id_type=MESH).wait_recv()
        for cp in sends:
            cp.wait_send()
        mod_ref[...] = mod_ref[...] + b_ref[...]

    vm = pl.BlockSpec(memory_space=pltpu.VMEM)
    return pl.pallas_call(
        body, out_shape=(jax.ShapeDtypeStruct((N_DEV, 1, D), F32), jax.ShapeDtypeStruct((N_DEV, 1, cw), F32)),
        in_specs=[vm, vm, vm] + [pl.BlockSpec(memory_space=pl.ANY)] * len(deps), out_specs=(vm, vm),
        scratch_shapes=[pltpu.VMEM((N_DEV, 1, cw), F32), pltpu.SemaphoreType.DMA((2, N_DEV - 1)),
                        pltpu.SemaphoreType.DMA((2, N_DEV - 1))],
        name='mod_exchange', compiler_params=_params(None, 48))(c_row, w_ada, b_ada8, *deps)


HBM_SPEC = pl.BlockSpec(memory_space=pltpu.HBM)
SEM_SPEC = pl.BlockSpec(memory_space=pltpu.SEMAPHORE)
DATAFLOW = pltpu.SideEffectType.DATAFLOW_SIDE_EFFECTING


def _push_start(src, scatter, after, name):
    land = lax.empty(src.shape if scatter else (N_DEV,) + src.shape, src.dtype)

    def body(src_ref, land_ref, after_ref, send_sem, recv_sem, land_thru, token):
        x, y, c = _me()
        me = 4 * x + 2 * y + c
        for k in range(1, N_DEV):
            peer, pidx = _peer(k)
            pltpu.make_async_remote_copy(src_ref=src_ref.at[pidx] if scatter else src_ref, dst_ref=land_ref.at[me],
                                         send_sem=send_sem, recv_sem=recv_sem, device_id=peer,
                                         device_id_type=MESH).start()
        token[...] = jnp.zeros_like(token)

    own = src
    src = pltpu.with_memory_space_constraint(src, pltpu.HBM)
    send_sem, recv_sem, land_thru, token = pl.pallas_call(
        body, name=name,
        out_shape=(pltpu.SemaphoreType.DMA(()), pltpu.SemaphoreType.DMA(()),
                   pltpu.HBM(land.shape, land.dtype), jax.ShapeDtypeStruct((8, 128), F32)),
        in_specs=(HBM_SPEC, HBM_SPEC, pl.BlockSpec(memory_space=pl.ANY)),
        out_specs=(SEM_SPEC, SEM_SPEC, HBM_SPEC, pl.BlockSpec(memory_space=pltpu.VMEM)),
        input_output_aliases={1: 2}, compiler_params=pltpu.CompilerParams(has_side_effects=DATAFLOW),
    )(src, pltpu.with_memory_space_constraint(land, pltpu.HBM), after)
    return send_sem, recv_sem, src, land_thru, token, own


def _push_wait(handle, after, name, merge=True):
    send_sem, recv_sem, src, land_thru, _, own = handle
    after = tuple(after) if isinstance(after, (tuple, list)) else (after,)

    def body(src_ref, land_ref, send_sem, recv_sem, *rest):
        seven = land_ref.at[pl.ds(0, N_DEV - 1)]
        cp = pltpu.make_async_remote_copy(src_ref=seven, dst_ref=seven, send_sem=send_sem, recv_sem=recv_sem,
                                          device_id=_me(), device_id_type=MESH)
        cp.wait_send()
        cp.wait_recv()

    landed = pl.pallas_call(
        body, name=name, out_shape=pltpu.HBM(land_thru.shape, land_thru.dtype),
        in_specs=(HBM_SPEC, HBM_SPEC, SEM_SPEC, SEM_SPEC) + (pl.BlockSpec(memory_space=pl.ANY),) * len(after),
        out_specs=HBM_SPEC, input_output_aliases={1: 0},
        compiler_params=pltpu.CompilerParams(has_side_effects=DATAFLOW),
    )(src, land_thru, send_sem, recv_sem, *after)
    if not merge:
        return landed, own
    return lax.dynamic_update_index_in_dim(landed, own, _my_index(), 0)


def _adam(w, g, m, v):
    m2 = B1 * m + (1.0 - B1) * g
    v2 = B2 * v + (1.0 - B2) * jnp.square(g)
    m_hat = m2 / (1.0 - B1 ** STEP)
    v_hat = v2 / (1.0 - B2 ** STEP)
    delta = -LR * (m_hat / (jnp.sqrt(v_hat) + AEPS) + WD * w)
    return delta, m2, v2


def _small_update(gp, wp, mp, vp):
    def body(g_ref, w_ref, m_ref, v_ref, all_ref, go_ref, d_ref, mo_ref, vo_ref, send_sems, recv_sems):
        x, y, c = _me()
        me = 4 * x + 2 * y + c
        all_ref[me] = g_ref[...]
        sends = []
        for k in range(1, N_DEV):
            peer, _ = _peer(k)
            cp = pltpu.make_async_remote_copy(src_ref=g_ref, dst_ref=all_ref.at[me], send_sem=send_sems.at[k - 1],
                                              recv_sem=recv_sems.at[k - 1], device_id=peer, device_id_type=MESH)
            cp.start()
            sends.append(cp)
        for k in range(1, N_DEV):
            peer, pidx = _peer(k)
            pltpu.make_async_remote_copy(src_ref=g_ref, dst_ref=all_ref.at[pidx], send_sem=send_sems.at[k - 1],
                                         recv_sem=recv_sems.at[k - 1], device_id=peer, device_id_type=MESH).wait_recv()
        for cp in sends:
            cp.wait_send()
        g = all_ref[0]
        for d in range(1, N_DEV):
            g = g + all_ref[d]
        delta, m2, v2 = _adam(w_ref[...], g, m_ref[...], v_ref[...])
        go_ref[...] = g
        d_ref[...] = delta
        mo_ref[...] = m2
        vo_ref[...] = v2

    vm = pl.BlockSpec(memory_space=pltpu.VMEM)
    vs = jax.ShapeDtypeStruct(gp.shape, F32)
    return pl.pallas_call(
        body, out_shape=(jax.ShapeDtypeStruct((N_DEV,) + gp.shape, F32), vs, vs, vs, vs), in_specs=[vm] * 4,
        out_specs=(vm,) * 5,
        scratch_shapes=[pltpu.SemaphoreType.DMA((N_DEV - 1,)), pltpu.SemaphoreType.DMA((N_DEV - 1,))],
        name='small_update', compiler_params=_params(None, 48))(gp, wp, mp, vp)


def _small_sum_update(parts, wp, mp, vp):
    def body(p_ref, w_ref, m_ref, v_ref, go_ref, d_ref, mo_ref, vo_ref):
        g = p_ref[0]
        for d in range(1, N_DEV):
            g = g + p_ref[d]
        delta, m2, v2 = _adam(w_ref[...], g, m_ref[...], v_ref[...])
        go_ref[...] = g
        d_ref[...] = delta
        mo_ref[...] = m2
        vo_ref[...] = v2

    vm = pl.BlockSpec(memory_space=pltpu.VMEM)
    vs = jax.ShapeDtypeStruct(wp.shape, F32)
    return pl.pallas_call(body, out_shape=(vs, vs, vs, vs), in_specs=[vm] * 4, out_specs=(vm,) * 4,
                          name='small_sum_update', compiler_params=_params(None, 48))(parts, wp, mp, vp)


def _big_update(parts, src, w, m, v, name):
    _, R, C = parts.shape
    tr = (128 if C >= 2048 else 256) if R % 256 == 0 else (R // 2 if C >= 2048 and R % 32 == 0 else R)

    def body(me_ref, p_ref, own_ref, w_ref, m_ref, v_ref, g_ref, d_ref, mo_ref, vo_ref):
        me = me_ref[0]
        mine = own_ref[...].astype(F32)
        g = jnp.where(me == 0, mine, p_ref[0].astype(F32))
        for d in range(1, N_DEV):
            g = g + jnp.where(me == d, mine, p_ref[d].astype(F32))
        delta, m2, v2 = _adam(w_ref[...], g, m_ref[...], v_ref[...])
        g_ref[...] = g
        d_ref[...] = delta
        mo_ref[...] = m2
        vo_ref[...] = v2

    blk = pl.BlockSpec((tr, C), lambda i, me: (i, 0))
    sh = jax.ShapeDtypeStruct((R, C), F32)
    grid_spec = pltpu.PrefetchScalarGridSpec(
        num_scalar_prefetch=1, grid=(R // tr,),
        in_specs=[pl.BlockSpec((N_DEV, tr, C), lambda i, me: (0, i, 0)),
                  pl.BlockSpec((None, tr, C), lambda i, me: (me[0], i, 0)), blk, blk, blk],
        out_specs=(blk,) * 4)
    return pl.pallas_call(
        body, out_shape=(sh, sh, sh, sh), grid_spec=grid_spec, name=name,
        compiler_params=_params(('parallel',), 48))(_my_index().reshape(1), parts, src, w, m, v)


def _ada_update(c_all, dmod_cols, w, m, v):
    C = w.shape[1]
    tr = 256

    def body(c_ref, dm_ref, w_ref, m_ref, v_ref, g_ref, d_ref, mo_ref, vo_ref):
        cv = c_ref[...]
        s = cv * jax.nn.sigmoid(cv)
        g = lax.dot_general(s, dm_ref[...], TN_DIMS, preferred_element_type=F32, precision=HI)
        delta, m2, v2 = _adam(w_ref[...], g, m_ref[...], v_ref[...])
        g_ref[...] = g
        d_ref[...] = delta
        mo_ref[...] = m2
        vo_ref[...] = v2

    blk = pl.BlockSpec((tr, C), lambda i: (i, 0))
    sh = jax.ShapeDtypeStruct((D, C), F32)
    return pl.pallas_call(
        body, out_shape=(sh, sh, sh, sh), grid=(D // tr,),
        in_specs=[pl.BlockSpec((N_DEV, tr), lambda i: (0, i)), pl.BlockSpec((N_DEV, C), lambda i: (0, 0)), blk, blk, blk],
        out_specs=(blk,) * 4, name='ada_update', compiler_params=_params(('parallel',), 48))(c_all, dmod_cols, w, m, v)


def _rows_to_cluster_lanes(t):
    k = t.shape[1]
    return t.reshape(N_CL, CL_S, k).transpose(0, 2, 1)


def _blockdiag_in(t):
    t = t.reshape(N_CL, CL_G, SSM_N, SSM_P).transpose(0, 1, 3, 2)
    eye = jnp.eye(CL_G, dtype=t.dtype)
    t = t[:, :, :, None, :] * eye[None, :, None, :, None]
    return t.reshape(N_CL, CL_U, CL_S)


def _blockdiag_extract(t):
    t = t.reshape(N_CL, CL_G, SSM_P, CL_G, SSM_N)
    eye = jnp.eye(CL_G, dtype=t.dtype)
    t = jnp.sum(t * eye[None, :, None, :, None], axis=3)
    return t.transpose(0, 1, 3, 2).reshape(SSM_GN, SSM_P)


def _c_to_rows(t):
    return t.transpose(0, 2, 1).reshape(SSM_GN, SSM_P)


def _rows_to_c(t):
    return t.reshape(SSM_G, SSM_N, SSM_P).transpose(0, 2, 1)


def _ssm_prep(sp):
    rows = lambda n: sp[n].reshape(SSM_GN, 1)
    a_re, a_im = rows('ssm_a_re'), rows('ssm_a_im')
    ldt = jnp.repeat(sp['ssm_log_dt'].reshape(SSM_G, 1), SSM_N, axis=0)
    b_re, b_im = sp['ssm_b_re'].reshape(SSM_GN, SSM_P), sp['ssm_b_im'].reshape(SSM_GN, SSM_P)
    c_re, c_im = _c_to_rows(sp['ssm_c_re'].reshape(SSM_G, SSM_P, SSM_N)), _c_to_rows(sp['ssm_c_im'].reshape(SSM_G, SSM_P, SSM_N))
    bbr, bbi, pwr, pwi = _ssm_pre(a_re, a_im, ldt, b_re, b_im)
    bm = jnp.concatenate([_blockdiag_in(bbr), _blockdiag_in(bbi)], axis=2).astype(BF16)
    cmt = jnp.concatenate([_blockdiag_in(c_re), -_blockdiag_in(c_im)], axis=2).astype(BF16)
    bmt, cm = bm.transpose(0, 2, 1), cmt.transpose(0, 2, 1)
    pw = jnp.concatenate([_rows_to_cluster_lanes(pwr), _rows_to_cluster_lanes(pwi)], axis=2)
    return a_re, a_im, ldt, b_re, b_im, bm, cm, bmt, cmt, pw


def _tied(v, deps):
    for t in deps:
        v = v + t[0, 0]
    return v


def _local_step(x, pos, mod, tgt, sp, prep, get_w, emit, emit_small, emit_late):
    sh1, sc1, gt1, sh2, sc2, gt2 = (mod[i:i + 1] for i in range(NMOD))
    vec = lambda n: sp[n].reshape(1, -1)
    a_re, a_im, ldt, b_re, b_im, bm_b, cm_b, bmt_b, cmt_b, pw = prep
    dvec = vec('ssm_d')

    h1 = _prenorm_fwd(x, vec('g_pre_mix'), sc1, sh1)
    w_in = get_w('w_in', (h1, bm_b, cm_b, pw, bmt_b, cmt_b))
    proj = _mm(h1, w_in, mode='nt', name='mm_in', tm=512, tn=INW)
    fr1 =ROPE_THETA ** (-jnp.arange(0, ROT_DIM, 2, dtype=F32) / ROT_DIM)
    lane = jnp.arange(128) % HEAD_DIM
    fr = jnp.where(lane < ROT_DIM, fr1[lane % (ROT_DIM // 2)], 0.0).reshape(1, 128).astype(F32)
    u_off = (ROPE_W + KVW) // CL_U
    qkvs = _rope_fwd(proj, pos, fr)
    fwd = [_attn_fwd(qkvs[g], d) for g, d in enumerate(DILATIONS)]
    os_, lses = [t[0] for t in fwd], [t[1] for t in fwd]
    att = _combine_fwd(os_, lses, vec('g_attn_out'))

    ypre, bnd, states = _ssm_fwd(proj, bm_b, cm_b, pw, dvec, u_off)
    w_glu = get_w('w_glu', ypre)
    ssm_n = _glu_fwd(ypre, w_glu, vec('b_glu'), vec('g_ssm_out'))

    cat = jnp.concatenate([att, ssm_n], axis=1)
    w_out = get_w('w_out', cat)
    mix = _mm(cat, w_out, mode='nn', name='mm_out', tk=1280)
    x1, h2 = _postmix_fwd(x, mix, vec('g_post_mix'), gt1, vec('g_pre_mlp'), sc2, sh2)
    w_mi = get_w('w_mlp_in', h2)
    a_pre, r_act = _mm(h2, w_mi, mode='nn', name='mm_mlp_in', epilogue='relu2', b_sharded=True)
    w_mo = get_w('w_mlp_out', a_pre)
    y = _mm(r_act, w_mo, mode='nn', name='mm_mlp_out')
    dx2, dy, loss, dgt2, dg_post_mlp = _final_fwd_bwd(x1, y, tgt, vec('g_post_mlp'), gt2)
    dgt2, dg_post_mlp = dgt2[:1], dg_post_mlp[:1]

    da = _mm(dy, w_mo, mode='nt', name='mm_d_act', out_dtype=BF16, epilogue='drelu2', extra=a_pre)
    dep = emit('w_mlp_out', _mm(r_act, dy, mode='tn', name='mm_dw_mlp_out', out_dtype=BF16))
    dh2 = _mm(da, w_mi, mode='nt', name='mm_dh2', tk=2048, out_dtype=BF16, b_sharded=True, deps=dep)
    dep = emit('w_mlp_in', _mm(h2, da, mode='tn', name='mm_dw_mlp_in', out_dtype=BF16, out_sharded=True))
    dx1, dmix, dsc2, dsh2, dg_pre_mlp, dgt1, dg_post_mix = _postmix_bwd(
        dx2, dh2, x1, mix, vec('g_post_mix'), gt1, vec('g_pre_mlp'), sc2)
    dsc2, dsh2, dg_pre_mlp, dgt1, dg_post_mix = (t[:1] for t in (dsc2, dsh2, dg_pre_mlp, dgt1, dg_post_mix))
    dcat = _mm(dmix, w_out, mode='nt', name='mm_dcat', tn=1280, out_dtype=BF16, deps=dep)
    dep = emit('w_out', _mm(cat, dmix, mode='tn', name='mm_dw_out', out_dtype=BF16, tm=640))
    dypre, g_w_glu, g_b_glu, g_g_ssm = _glu_bwd(ypre, dcat, w_glu, _tied(vec('b_glu'), dep), vec('g_ssm_out'))
    g_b_glu, g_g_ssm = g_b_glu[:1], g_g_ssm[:1]
    dep = dep + emit('w_glu', g_w_glu.astype(BF16))
    du, dbm, dcm, dA, dD = _ssm_bwd(proj, dypre, states, bmt_b, cmt_b, pw, dvec, bnd, u_off)
    dD = dD[:1]
    gbr, gbi = _blockdiag_extract(dbm[:, :, :CL_S]), _blockdiag_extract(dbm[:, :, CL_S:])
    dcmt = dcm.transpose(0, 2, 1)
    g_c_re = _rows_to_c(_blockdiag_extract(dcmt[:, :, :CL_S]))
    g_c_im = _rows_to_c(-_blockdiag_extract(dcmt[:, :, CL_S:]))
    gar = dA[:, 0, :CL_S].reshape(SSM_GN, 1)
    gai = dA[:, 0, CL_S:].reshape(SSM_GN, 1)
    sel = (jnp.arange(SSM_GN)[None, :] // SSM_N == jnp.arange(SSM_G)[:, None]).astype(F32)
    g_a_re, g_a_im, g_b_re, g_b_im, g_ldt = _ssm_post(a_re, a_im, ldt, b_re, b_im, gar, gai, gbr, gbi, sel)

    head_ones = (jnp.arange(KVW)[:, None] // HEAD_DIM == jnp.arange(KVW)[None, :] // HEAD_DIM).astype(F32)
    dos, dlses, g_g_attn = _combine_bwd(dcat, os_, lses, vec('g_attn_out'), head_ones)
    g_g_attn = g_g_attn[:1]
    dep_small = emit_small({
        'g_post_mix': dg_post_mix, 'ssm_a_re': g_a_re, 'ssm_a_im': g_a_im, 'ssm_log_dt': g_ldt[:, 0],
        'ssm_b_re': g_b_re, 'ssm_b_im': g_b_im, 'ssm_c_re': g_c_re, 'ssm_c_im': g_c_im, 'ssm_d': dD, 'b_glu': g_b_glu,
        'g_attn_out': g_g_attn, 'g_ssm_out': g_g_ssm, 'g_pre_mlp': dg_pre_mlp, 'g_post_mlp': dg_post_mlp})
    dqkv = [_attn_bwd(qkvs[g], os_[g], lses[g], dos[g], dlses[g], d) for g, d in enumerate(DILATIONS)]
    dproj = _rope_bwd(dqkv, du, pos, _tied(fr, dep_small))
    dh1 = _mm(dproj, w_in, mode='nn', name='mm_dh1', tk=INW, out_dtype=BF16, deps=dep)
    grad_x, dsc1, dsh1, dg_pre_mix = _prenorm_bwd(dx1, dh1, x, vec('g_pre_mix'), sc1)
    dsc1, dsh1, dg_pre_mix = dsc1[:1], dsh1[:1], dg_pre_mix[:1]
    dmod = jnp.concatenate([dsh1, dsc1, dgt1, dsh2, dsc2, dgt2], axis=0)
    dep = emit_late({'b_ada': dmod, 'g_pre_mix': dg_pre_mix, 'loss': loss[0:1, 0:1]})
    emit('w_in', _mm(dproj, h1, mode='tn', name='mm_dw_in', out_dtype=BF16, tm=INW // 2, deps=dep))
    return loss[0, 0], grad_x


def _pack(d, names):
    flat = jnp.concatenate([jnp.pad(d[n].reshape(-1).astype(F32), (0, SEG[n] - SMALL_SIZES[n])) for n in names])
    return flat.reshape(-1, 128)


def _unpack(packed, names, shapes):
    out, off = {}, 0
    for n in names:
        out[n] = packed[off // 128:(off + SEG[n]) // 128].reshape(-1)[:SMALL_SIZES[n]].reshape(shapes[n])
        off += SEG[n]
    return out


def _shard_major(t, name):
    if name in ('w_in', 'w_out', 'w_mlp_in'):
        k, n = t.shape
        return t.reshape(k, N_DEV, n // N_DEV).transpose(1, 0, 2)
    k, n = t.shape
    return t.reshape(N_DEV, k // N_DEV, n)


def _from_shard_major(t, name):
    if name in ('w_in', 'w_out', 'w_mlp_in'):
        _, k, n = t.shape
        return t.transpose(1, 0, 2).reshape(k, N_DEV * n)
    _, k, n = t.shape
    return t.reshape(N_DEV * k, n)


def kernel(x, c, positions, w_ada, b_ada, g_pre_mix, g_post_mix, w_in, ssm_a_re, ssm_a_im, ssm_log_dt, ssm_b_re, ssm_b_im, ssm_c_re, ssm_c_im, ssm_d, w_glu, b_glu, g_attn_out, g_ssm_out, w_out, g_pre_mlp, g_post_mlp, w_mlp_in, w_mlp_out, loss_target, m_w_ada, m_b_ada, m_g_pre_mix, m_g_post_mix, m_w_in, m_ssm_a_re, m_ssm_a_im, m_ssm_log_dt, m_ssm_b_re, m_ssm_b_im, m_ssm_c_re, m_ssm_c_im, m_ssm_d, m_w_glu, m_b_glu, m_g_attn_out, m_g_ssm_out, m_w_out, m_g_pre_mlp, m_g_post_mlp, m_w_mlp_in, m_w_mlp_out, v_w_ada, v_b_ada, v_g_pre_mix, v_g_post_mix, v_w_in, v_ssm_a_re, v_ssm_a_im, v_ssm_log_dt, v_ssm_b_re, v_ssm_b_im, v_ssm_c_re, v_ssm_c_im, v_ssm_d, v_w_glu, v_b_glu, v_g_attn_out, v_g_ssm_out, v_w_out, v_g_pre_mlp, v_g_post_mlp, v_w_mlp_in, v_w_mlp_out):
    loc = dict(locals())
    W = {n: loc[n] for n in WEIGHTS}
    M = {n: loc['m_' + n] for n in WEIGHTS}
    V = {n: loc['v_' + n] for n in WEIGHTS}
    assert x.shape == (1, L, D) and w_in.shape == (1, D, INW // N_DEV), (x.shape, w_in.shape)

    cw = NMOD * D // N_DEV
    c_all, mod8 = _mod_exchange(c.reshape(1, 1, D), w_ada[0], b_ada.reshape(N_DEV, 1, cw))
    mod = mod8.reshape(NMOD, D)

    shard = lambda t, n: t[0].T if n == 'w_in' else t[0]
    gather, after = {}, mod8
    for n in BIG:
        gather[n] = _push_start(shard(W[n], n).astype(BF16), False, after, 'gather_start_' + n)
        after = gather[n][4]
    tokens = tuple(gather[n][4] for n in BIG)
    mod = _tied(mod, tokens)
    sp = {n: W[n][0] for n in SMALL}
    prep = _ssm_prep({**sp, 'ssm_a_re': _tied(sp['ssm_a_re'], tokens)})

    state_packs = {}

    def get_w(n, after):
        if n == 'w_mlp_out':
            tok = jnp.minimum(jnp.abs(after[0:8, 0:128].astype(F32)), 0.0)
            for key, src in (('w', W), ('m', M), ('v', V)):
                tied = {p: _tied(src[p], (tok,)) for p in SMALL}
                tied['loss'] = tok[0:1, 0:1]
                state_packs[key] = {'early': _pack(tied, SMALL_EARLY), 'late': _pack(tied, LATE_PACK)}
            after = (after,) + tuple(state_packs[k][part] for k in 'wmv' for part in ('early', 'late'))
        g = _push_wait(gather[n], after, 'gather_wait_' + n)
        if n == 'w_in':
            return g.reshape(INW, D)
        return g if n == 'w_mlp_in' else _from_shard_major(g, n)

    scatter = {}

    def emit(n, g):
        src = g if n == 'w_mlp_in' else (g.reshape(N_DEV, INW // N_DEV, D) if n == 'w_in' else _shard_major(g, n))
        scatter[n] = _push_start(src, True, src, 'scatter_start_' + n)
        return (scatter[n][4],)

    small_early = []

    def emit_small(d):
        pack = _pack(d, SMALL_EARLY)
        small_early.append(_push_start(pack, False, pack, 'small_start'))
        return (small_early[0][4],)

    out_g, out_d, out_m, out_v = {}, {}, {}, {}
    shapes = {n: W[n].shape[1:] for n in SMALL}

    def put(names, packs):
        for dst, packed in zip((out_g, out_d, out_m, out_v), packs):
            dst.update(_unpack(packed, names, shapes))

    late = []

    def emit_late(d):
        rows_all, *packs = _small_update(_pack(d, LATE_PACK), *[state_packs[k]['late'] for k in 'wmv'])
        put(SMALL_LATE, packs)
        late.append(rows_all)
        late.append(packs[0][(SEG['b_ada'] + SEG['g_pre_mix']) // 128, 0])
        return (rows_all,)

    _, grad_x = _local_step(x[0], positions.reshape(L, 1), mod, loss_target[0], sp, prep, get_w, emit, emit_small,
                            emit_late)
    loss = late[1]

    me = 4 * lax.axis_index('x') + 2 * lax.axis_index('y') + lax.axis_index('c')
    dmod_all = late[0][:, :NMOD * D // 128].reshape(N_DEV, NMOD * D)
    dmod_cols = _tied(lax.dynamic_slice_in_dim(dmod_all, me * cw, cw, axis=1), (scatter['w_in'][4],))
    out_g['w_ada'], out_d['w_ada'], out_m['w_ada'], out_v['w_ada'] = _ada_update(
        c_all.reshape(N_DEV, D), dmod_cols, w_ada[0], m_w_ada[0], v_w_ada[0])

    parts = _push_wait(small_early[0], out_v['w_ada'], 'small_wait')
    packs = _small_sum_update(parts, *[state_packs[k]['early'] for k in 'wmv'])
    put(SMALL_EARLY, packs)

    after = packs[3]
    for n in ('w_mlp_out', 'w_mlp_in', 'w_out', 'w_glu', 'w_in'):
        parts, src = _push_wait(scatter[n], after, 'scatter_wait_' + n, merge=False)
        outs = _big_update(parts, src, shard(W[n], n), shard(M[n], n), shard(V[n], n), 'update_' + n)
        after = outs[3]
        out_g[n], out_d[n], out_m[n], out_v[n] = (t.T for t in outs) if n == 'w_in' else outs

    lead = lambda t: t[None]
    return (loss, grad_x[None], *[lead(out_g[n]) for n in WEIGHTS], *[lead(out_d[n]) for n in WEIGHTS],
            *[lead(out_m[n]) for n in WEIGHTS], *[lead(out_v[n]) for n in WEIGHTS])
```
